```python
import math
import jax, jax.numpy as jnp
from jax import lax
import numpy as np

D_MODEL = 1024
BATCH = 16
SEQ = 2048
DEPTH = 2

N_META = 16
BLOCK = 128
PAD_FRONT = BLOCK - N_META

ATT_HEADS = 8
ATT_KV_HEADS = 2
ATT_HEAD_DIM = 64
ATT_WIDTH = ATT_HEADS * ATT_HEAD_DIM
ATT_KV_WIDTH = ATT_KV_HEADS * ATT_HEAD_DIM
WINDOW = 128
N_BUCKETS = 32
MAX_EXACT = N_BUCKETS // 2
MAX_DISTANCE = 128

RET_HEADS = 4
RET_HEAD_DIM = 128
RET_WIDTH = RET_HEADS * RET_HEAD_DIM
ROT_BASE = 10000.0

CONV_WIDTH = 512
CONV_K = 3

N_BRANCH = 3
BRANCH_WIDTH = 512
SPLITS = (ATT_WIDTH, ATT_KV_WIDTH, ATT_KV_WIDTH, ATT_WIDTH,
          RET_WIDTH, RET_WIDTH, RET_WIDTH, RET_WIDTH,
          CONV_WIDTH, CONV_WIDTH, CONV_WIDTH, CONV_WIDTH,
          N_BRANCH * D_MODEL)
PROJ_WIDTH = 8448
RMS_EPS = 1e-6
GN_EPS = 1e-6
NEG_INF = -1e30

kernel_name = "hybrid_gated_swa_retention_shortconv"


def _split_points():
    return [int(v) for v in np.cumsum(SPLITS)[:-1]]


def rms_norm(x, g):
    xf = x.astype(jnp.float32)
    y = xf * lax.rsqrt(jnp.mean(xf * xf, axis=-1, keepdims=True) + RMS_EPS)
    return (y * g.astype(jnp.float32)).astype(x.dtype)


def t5_causal_bucket(dist):
    n = jnp.maximum(dist, 0)
    nf = jnp.maximum(n, 1).astype(jnp.float32)
    large = MAX_EXACT + (jnp.log(nf / MAX_EXACT) / math.log(MAX_DISTANCE / MAX_EXACT)
                         * (N_BUCKETS - MAX_EXACT)).astype(jnp.int32)
    large = jnp.minimum(large, N_BUCKETS - 1)
    return jnp.where(n < MAX_EXACT, n, large)


def sliding_window_attention(q, k, v, sinks, rel_bias, valid):
    B, Lp = q.shape[0], q.shape[1]
    nc = Lp // BLOCK
    G = ATT_HEADS // ATT_KV_HEADS
    qb = q.reshape(B, nc, BLOCK, ATT_KV_HEADS, G, ATT_HEAD_DIM)

    def band(t):
        tb = t.reshape((B, nc, BLOCK) + t.shape[2:])
        prev = jnp.concatenate([jnp.zeros_like(tb[:, :1]), tb[:, :-1]], axis=1)
        return jnp.concatenate([prev, tb], axis=2)

    kb, vb = band(k), band(v)
    vblk = valid.reshape(nc, BLOCK)
    vprev = jnp.concatenate([jnp.zeros_like(vblk[:1]), vblk[:-1]], axis=0)
    valid_band = jnp.concatenate([vprev, vblk], axis=1)

    r = jnp.arange(BLOCK)[:, None]
    c = jnp.arange(2 * BLOCK)[None, :]
    dist = BLOCK + r - c
    in_window = (dist >= 0) & (dist < WINDOW)
    bias = rel_bias[t5_causal_bucket(dist)]
    bias = bias.reshape(BLOCK, 2 * BLOCK, ATT_KV_HEADS, G).transpose(2, 3, 0, 1).astype(jnp.float32)
    mask = in_window[None] & valid_band[:, None, :]

    s = jnp.einsum('bnqhgd,bnkhd->bnhgqk', qb, kb).astype(jnp.float32) * (ATT_HEAD_DIM ** -0.5) + bias
    s = jnp.where(mask[None, :, None, None], s, NEG_INF)
    sink = sinks.astype(jnp.float32).reshape(ATT_KV_HEADS, G)[None, None, :, :, None, None]
    m = jnp.maximum(jnp.max(s, axis=-1, keepdims=True), sink)
    p = jnp.exp(s - m)
    denom = jnp.sum(p, axis=-1, keepdims=True) + jnp.exp(sink - m)
    p = (p / denom).astype(v.dtype)
    o = jnp.einsum('bnhgqk,bnkhd->bnqhgd', p, vb)
    return o.reshape(B, Lp, ATT_WIDTH)


def rotate(t, pos):
    half = t.shape[-1] // 2
    theta = 1.0 / (ROT_BASE ** jnp.linspace(0.0, 1.0, half, dtype=jnp.float32))
    ang = pos.astype(jnp.float32)[:, None] * theta[None, :]
    cos = jnp.cos(ang)[None, :, None, :]
    sin = jnp.sin(ang)[None, :, None, :]
    t1, t2 = t[..., :half].astype(jnp.float32), t[..., half:].astype(jnp.float32)
    return jnp.concatenate([t1 * cos - t2 * sin, t1 * sin + t2 * cos], axis=-1).astype(t.dtype)


def retention(q, k, v, valid, pos):
    B, Lp = q.shape[0], q.shape[1]
    nc = Lp // BLOCK
    q = rotate(q, pos)
    k = rotate(k, pos) * (RET_HEAD_DIM ** -0.5)
    k = k * valid[None, :, None, None].astype(k.dtype)
    log_gamma = jnp.log1p(-(2.0 ** (-5.0 - jnp.arange(RET_HEADS, dtype=jnp.float32))))
    i = jnp.arange(BLOCK, dtype=jnp.float32)
    diff = i[:, None] - i[None, :]
    decay = jnp.where(diff[None] >= 0, jnp.exp(diff[None] * log_gamma[:, None, None]), 0.0)
    zeta = jnp.exp((BLOCK - 1 - i)[None, :] * log_gamma[:, None])
    xi = jnp.exp((i + 1)[None, :] * log_gamma[:, None])
    gamma_chunk = jnp.exp(BLOCK * log_gamma)[None, :, None, None]

    shp = (B, nc, BLOCK, RET_HEADS, RET_HEAD_DIM)
    qc, kc, vc = q.reshape(shp), k.reshape(shp), v.reshape(shp)
    inner_s = jnp.einsum('bnihd,bnjhd->bnhij', qc, kc) * decay
    inner = jnp.einsum('bnhij,bnjhe->bnihe', inner_s, vc)
    chunk_kv = jnp.einsum('bnjhd,bnjhe,hj->nbhde', kc, vc, zeta)

    def step(state, kv):
        return gamma_chunk * state + kv, state

    _, prev_states = lax.scan(step, jnp.zeros_like(chunk_kv[0]), chunk_kv)
    cross = jnp.einsum('bnihd,nbhde,hi->bnihe', qc, prev_states, xi)
    o = (inner + cross).astype(jnp.float32)
    mu = jnp.mean(o, axis=-1, keepdims=True)
    var = jnp.mean(jnp.square(o - mu), axis=-1, keepdims=True)
    o = (o - mu) * lax.rsqrt(var + GN_EPS)
    return o.reshape(B, Lp, RET_WIDTH).astype(q.dtype)


def short_conv_mixer(b_gate, c_gate, x_in, conv_w, valid):
    u = c_gate * x_in * valid[None, :, None].astype(x_in.dtype)
    y = lax.conv_general_dilated(u, conv_w[:, None, :].astype(u.dtype), window_strides=(1,),
                                 padding=[(CONV_K - 1, 0)],
                                 dimension_numbers=('NWC', 'WIO', 'NWC'),
                                 feature_group_count=CONV_WIDTH)
    return b_gate * y


def hybrid_layer(x, valid, pos, rel_bias, g_pre, w_in, conv_w, sinks, w_branch, w_out, g_post):
    B, Lp, _ = x.shape
    h = rms_norm(x, g_pre)
    proj = h @ w_in
    (aq, ak, av, ag, rq, rk, rv, rg, cb, cc, cx, cg, merge) = jnp.split(proj, _split_points(), axis=-1)

    ya = sliding_window_attention(aq.reshape(B, Lp, ATT_HEADS, ATT_HEAD_DIM),
                                  ak.reshape(B, Lp, ATT_KV_HEADS, ATT_HEAD_DIM),
                                  av.reshape(B, Lp, ATT_KV_HEADS, ATT_HEAD_DIM),
                                  sinks, rel_bias, valid) * jax.nn.silu(ag)
    yr = retention(rq.reshape(B, Lp, RET_HEADS, RET_HEAD_DIM),
                   rk.reshape(B, Lp, RET_HEADS, RET_HEAD_DIM),
                   rv.reshape(B, Lp, RET_HEADS, RET_HEAD_DIM), valid, pos) * jax.nn.silu(rg)
    yc = short_conv_mixer(cb, cc, cx, conv_w, valid) * jax.nn.silu(cg)

    branches = jnp.stack([ya, yr, yc], axis=2)
    branch_out = jnp.einsum('blgc,gcd->blgd', branches, w_branch)
    gates = jax.nn.sigmoid(merge.reshape(B, Lp, N_BRANCH, D_MODEL))
    mixed = jnp.sum(gates * branch_out, axis=2) @ w_out
    return x + rms_norm(mixed, g_post).astype(x.dtype)


def _fwd_setup_inputs(seed: int = 0) -> dict:
    key = jax.random.key(seed)
    ks = jax.random.split(key, 10)
    f32 = jnp.float32
    x = jax.random.normal(ks[0], (BATCH, SEQ, D_MODEL), f32)
    meta_tokens = jax.random.normal(ks[1], (N_META, D_MODEL), f32)
    rel_bias = 0.1 * jax.random.normal(ks[2], (N_BUCKETS, ATT_HEADS), f32)
    norm_pre = 1.0 + 0.01 * jax.random.normal(ks[3], (DEPTH, D_MODEL), f32)
    w_in = jax.random.normal(ks[4], (DEPTH, D_MODEL, PROJ_WIDTH), f32) * (D_MODEL ** -0.5)
    conv_w = jax.random.normal(ks[5], (DEPTH, CONV_K, CONV_WIDTH), f32) * (CONV_K ** -0.5)
    attn_sinks = 0.5 * jax.random.normal(ks[6], (DEPTH, ATT_HEADS), f32)
    w_branch = jax.random.normal(ks[7], (DEPTH, N_BRANCH, BRANCH_WIDTH, D_MODEL), f32) * (BRANCH_WIDTH ** -0.5)
    w_out = jax.random.normal(ks[8], (DEPTH, D_MODEL, D_MODEL), f32) * (D_MODEL ** -0.5)
    norm_post = 1.0 + 0.01 * jax.random.normal(ks[9], (DEPTH, D_MODEL), f32)
    return {"x": x, "meta_tokens": meta_tokens, "rel_bias": rel_bias, "norm_pre": norm_pre,
            "w_in": w_in, "conv_w": conv_w, "attn_sinks": attn_sinks, "w_branch": w_branch,
            "w_out": w_out, "norm_post": norm_post}


def _fwd_reference(x, meta_tokens, rel_bias, norm_pre, w_in, conv_w, attn_sinks, w_branch, w_out, norm_post):
    B, S, _ = x.shape
    pad = jnp.zeros((B, PAD_FRONT, D_MODEL), x.dtype)
    meta = jnp.broadcast_to(meta_tokens[None].astype(x.dtype), (B, N_META, D_MODEL))
    h = jnp.concatenate([pad, meta, x], axis=1)
    idx = jnp.arange(PAD_FRONT + N_META + S)
    valid = idx >= PAD_FRONT
    pos = idx - PAD_FRONT
    for l in range(DEPTH):
        h = hybrid_layer(h, valid, pos, rel_bias, norm_pre[l], w_in[l], conv_w[l],
                         attn_sinks[l], w_branch[l], w_out[l], norm_post[l])
    return h[:, PAD_FRONT + N_META:]


import jax as _jax
import jax.numpy as _jnp

TWIN_FORMAT = 'train_step'
FWD_PARAMS = ['x', 'meta_tokens', 'rel_bias', 'norm_pre', 'w_in', 'conv_w', 'attn_sinks', 'w_branch', 'w_out', 'norm_post']
TWIN_WEIGHTS = ['meta_tokens', 'rel_bias', 'norm_pre', 'w_in', 'conv_w', 'attn_sinks', 'w_branch', 'w_out', 'norm_post']
TWIN_DIFF_INPUT = 'x'
TWIN_INPUTS = ['x', 'meta_tokens', 'rel_bias', 'norm_pre', 'w_in', 'conv_w', 'attn_sinks', 'w_branch', 'w_out', 'norm_post', 'loss_target', 'm_meta_tokens', 'm_rel_bias', 'm_norm_pre', 'm_w_in', 'm_conv_w', 'm_attn_sinks', 'm_w_branch', 'm_w_out', 'm_norm_post', 'v_meta_tokens', 'v_rel_bias', 'v_norm_pre', 'v_w_in', 'v_conv_w', 'v_attn_sinks', 'v_w_branch', 'v_w_out', 'v_norm_post']
TWIN_OUTPUTS = ['loss', 'grad_x', 'grad_meta_tokens', 'grad_rel_bias', 'grad_norm_pre', 'grad_w_in', 'grad_conv_w', 'grad_attn_sinks', 'grad_w_branch', 'grad_w_out', 'grad_norm_post', 'delta_meta_tokens', 'delta_rel_bias', 'delta_norm_pre', 'delta_w_in', 'delta_conv_w', 'delta_attn_sinks', 'delta_w_branch', 'delta_w_out', 'delta_norm_post', 'new_m_meta_tokens', 'new_m_rel_bias', 'new_m_norm_pre', 'new_m_w_in', 'new_m_conv_w', 'new_m_attn_sinks', 'new_m_w_branch', 'new_m_w_out', 'new_m_norm_post', 'new_v_meta_tokens', 'new_v_rel_bias', 'new_v_norm_pre', 'new_v_w_in', 'new_v_conv_w', 'new_v_attn_sinks', 'new_v_w_branch', 'new_v_w_out', 'new_v_norm_post']
TWIN_LEAF_KINDS = {'loss': 'loss', 'grad_x': 'grad_x', 'grad_meta_tokens': 'grad_w', 'grad_rel_bias': 'grad_w', 'grad_norm_pre': 'grad_w', 'grad_w_in': 'grad_w', 'grad_conv_w': 'grad_w', 'grad_attn_sinks': 'grad_w', 'grad_w_branch': 'grad_w', 'grad_w_out': 'grad_w', 'grad_norm_post': 'grad_w', 'delta_meta_tokens': 'delta_w', 'delta_rel_bias': 'delta_w', 'delta_norm_pre': 'delta_w', 'delta_w_in': 'delta_w', 'delta_conv_w': 'delta_w', 'delta_attn_sinks': 'delta_w', 'delta_w_branch': 'delta_w', 'delta_w_out': 'delta_w', 'delta_norm_post': 'delta_w', 'new_m_meta_tokens': 'new_m', 'new_m_rel_bias': 'new_m', 'new_m_norm_pre': 'new_m', 'new_m_w_in': 'new_m', 'new_m_conv_w': 'new_m', 'new_m_attn_sinks': 'new_m', 'new_m_w_branch': 'new_m', 'new_m_w_out': 'new_m', 'new_m_norm_post': 'new_m', 'new_v_meta_tokens': 'new_v', 'new_v_rel_bias': 'new_v', 'new_v_norm_pre': 'new_v', 'new_v_w_in': 'new_v', 'new_v_conv_w': 'new_v', 'new_v_attn_sinks': 'new_v', 'new_v_w_branch': 'new_v', 'new_v_w_out': 'new_v', 'new_v_norm_post': 'new_v'}


def _forward(args):
    return _fwd_reference(*[args[k] for k in FWD_PARAMS])


def _output_shape():
    out = _jax.eval_shape(lambda: _forward(_fwd_setup_inputs(0)))
    return out.shape, out.dtype

N_MICROBATCH = 1
ADAM_LR = 0.001
ADAM_B1 = 0.9
ADAM_B2 = 0.999
ADAM_EPS = 1e-08
ADAM_WD = 0.01
ADAM_STEP = 10
PER_EXAMPLE_BATCH_AXIS = {'x': 0, 'loss_target': 0}
SHARED_INPUTS = []
_WEIGHT_DTYPES = {'meta_tokens': _jnp.float32, 'rel_bias': _jnp.float32, 'norm_pre': _jnp.float32, 'w_in': _jnp.float32, 'conv_w': _jnp.float32, 'attn_sinks': _jnp.float32, 'w_branch': _jnp.float32, 'w_out': _jnp.float32, 'norm_post': _jnp.float32}
MOMENT_SCALE = {'meta_tokens': 6.061413e-02, 'rel_bias': 1.391292e-01, 'norm_pre': 1.007028e+00, 'w_in': 3.204595e-01, 'conv_w': 4.335864e-01, 'attn_sinks': 2.570789e-02, 'w_branch': 2.561473e-01, 'w_out': 4.472684e-01, 'norm_post': 3.189617e+01}


def _to_microbatches(a, axis):
    t = _jnp.moveaxis(a, axis, 0)
    t = t.reshape((N_MICROBATCH, t.shape[0] // N_MICROBATCH) + t.shape[1:])
    return _jnp.moveaxis(t, 1, axis + 1)


def setup_inputs(seed: int = 0) -> dict:
    inp = _fwd_setup_inputs(seed)
    key = _jax.random.fold_in(_jax.random.key(seed), 7919)
    shape, _ = _output_shape()
    out = dict(inp)
    out["loss_target"] = _jax.random.normal(_jax.random.fold_in(key, 0), shape, _jnp.float32)
    for i, name in enumerate(TWIN_WEIGHTS):
        w = inp[name].astype(_jnp.float32)
        if MOMENT_SCALE is None:
            s = _jnp.sqrt(_jnp.mean(_jnp.square(w)) + 1e-30)
        else:
            s = MOMENT_SCALE[name]
        km, kv = _jax.random.split(_jax.random.fold_in(key, i + 1))
        out[name] = w
        out["m_" + name] = s * _jax.random.normal(km, w.shape, _jnp.float32)
        out["v_" + name] = (s * s) * _jax.random.uniform(kv, w.shape, _jnp.float32, 0.5, 1.5)
    if N_MICROBATCH > 1:
        for name, axis in PER_EXAMPLE_BATCH_AXIS.items():
            out[name] = _to_microbatches(out[name], axis)
    return {'x': out['x'], 'meta_tokens': out['meta_tokens'], 'rel_bias': out['rel_bias'], 'norm_pre': out['norm_pre'], 'w_in': out['w_in'], 'conv_w': out['conv_w'], 'attn_sinks': out['attn_sinks'], 'w_branch': out['w_branch'], 'w_out': out['w_out'], 'norm_post': out['norm_post'], 'loss_target': out['loss_target'], 'm_meta_tokens': out['m_meta_tokens'], 'm_rel_bias': out['m_rel_bias'], 'm_norm_pre': out['m_norm_pre'], 'm_w_in': out['m_w_in'], 'm_conv_w': out['m_conv_w'], 'm_attn_sinks': out['m_attn_sinks'], 'm_w_branch': out['m_w_branch'], 'm_w_out': out['m_w_out'], 'm_norm_post': out['m_norm_post'], 'v_meta_tokens': out['v_meta_tokens'], 'v_rel_bias': out['v_rel_bias'], 'v_norm_pre': out['v_norm_pre'], 'v_w_in': out['v_w_in'], 'v_conv_w': out['v_conv_w'], 'v_attn_sinks': out['v_attn_sinks'], 'v_w_branch': out['v_w_branch'], 'v_w_out': out['v_w_out'], 'v_norm_post': out['v_norm_post']}


def _loss(weights, diff, rest, loss_target):
    with _jax.named_scope("forward"):
        args = {**rest, TWIN_DIFF_INPUT: diff, **{k: w.astype(_WEIGHT_DTYPES[k]) for k, w in weights.items()}}
        y = _forward(args)
    with _jax.named_scope("loss_head"):
        err = _jnp.square(y.astype(_jnp.float32) - loss_target)
        return 0.5 * _jnp.sum(_jnp.mean(err, axis=-1)) if err.ndim else 0.5 * err


def _adamw(w, g, m, v):
    m = ADAM_B1 * m + (1.0 - ADAM_B1) * g
    v = ADAM_B2 * v + (1.0 - ADAM_B2) * _jnp.square(g)
    m_hat = m / (1.0 - ADAM_B1 ** ADAM_STEP)
    v_hat = v / (1.0 - ADAM_B2 ** ADAM_STEP)
    delta = -ADAM_LR * (m_hat / (_jnp.sqrt(v_hat) + ADAM_EPS) + ADAM_WD * w)
    return delta, m, v


def reference(x, meta_tokens, rel_bias, norm_pre, w_in, conv_w, attn_sinks, w_branch, w_out, norm_post, loss_target, m_meta_tokens, m_rel_bias, m_norm_pre, m_w_in, m_conv_w, m_attn_sinks, m_w_branch, m_w_out, m_norm_post, v_meta_tokens, v_rel_bias, v_norm_pre, v_w_in, v_conv_w, v_attn_sinks, v_w_branch, v_w_out, v_norm_post):
    given = dict(x=x, meta_tokens=meta_tokens, rel_bias=rel_bias, norm_pre=norm_pre, w_in=w_in, conv_w=conv_w, attn_sinks=attn_sinks, w_branch=w_branch, w_out=w_out, norm_post=norm_post, loss_target=loss_target, m_meta_tokens=m_meta_tokens, m_rel_bias=m_rel_bias, m_norm_pre=m_norm_pre, m_w_in=m_w_in, m_conv_w=m_conv_w, m_attn_sinks=m_attn_sinks, m_w_branch=m_w_branch, m_w_out=m_w_out, m_norm_post=m_norm_post, v_meta_tokens=v_meta_tokens, v_rel_bias=v_rel_bias, v_norm_pre=v_norm_pre, v_w_in=v_w_in, v_conv_w=v_conv_w, v_attn_sinks=v_attn_sinks, v_w_branch=v_w_branch, v_w_out=v_w_out, v_norm_post=v_norm_post)
    weights = {n: given[n] for n in TWIN_WEIGHTS}
    shared = {n: given[n] for n in SHARED_INPUTS}
    per_example = {n: given[n] for n in ['x']}
    grad_fn = _jax.value_and_grad(_loss, argnums=(0, 1))

    def one_microbatch(ex, loss_target):
        ex = dict(ex)
        diff = ex.pop(TWIN_DIFF_INPUT)
        return grad_fn(weights, diff, {**shared, **ex}, loss_target)

    if N_MICROBATCH == 1:
        loss, (grad_w, grad_x) = one_microbatch(per_example, given["loss_target"])
    else:
        def body(carry, xs):
            loss_sum, grad_sum = carry
            l_k, (gw_k, gx_k) = one_microbatch(xs[0], xs[1])
            with _jax.named_scope("update"):
                return (loss_sum + l_k, _jax.tree.map(_jnp.add, grad_sum, gw_k)), gx_k

        init = (_jnp.zeros((), _jnp.float32), _jax.tree.map(_jnp.zeros_like, weights))
        (loss, grad_w), grad_x = _jax.lax.scan(body, init, (per_example, given["loss_target"]))
    with _jax.named_scope("update"):
        delta_w, new_m, new_v = {}, {}, {}
        for n in TWIN_WEIGHTS:
            delta_w[n], new_m[n], new_v[n] = _adamw(weights[n], grad_w[n], given["m_" + n], given["v_" + n])
    return (loss, grad_x, *[grad_w[n] for n in TWIN_WEIGHTS], *[delta_w[n] for n in TWIN_WEIGHTS],
            *[new_m[n] for n in TWIN_WEIGHTS], *[new_v[n] for n in TWIN_WEIGHTS])
```

```python
import functools
import math

import numpy as np
import jax
import jax.numpy as jnp
from jax import lax
from jax.experimental import pallas as pl
from jax.experimental.pallas import tpu as pltpu

F32 = jnp.float32
BF16 = jnp.bfloat16
MESH = pl.DeviceIdType.MESH

D_MODEL = 1024
DEPTH = 2
N_META = 16
BLOCK = 128
PAD_FRONT = BLOCK - N_META
ATT_HEADS = 8
ATT_HEAD_DIM = 64
N_BUCKETS = 32
MAX_EXACT = 16
MAX_DISTANCE = 128
RET_HEADS = 4
ROT_BASE = 10000.0
N_BRANCH = 3
BRANCH_WIDTH = 512
PROJ_WIDTH = 8448
ABC_WIDTH = 5376
MERGE_WIDTH = N_BRANCH * D_MODEL
RMS_EPS = 1e-6
GN_EPS = 1e-6
NEG_INF = -1e30
ATT_SCALE = ATT_HEAD_DIM ** -0.5
RET_SCALE = BLOCK ** -0.5
LOG_GAMMA = tuple(math.log1p(-(2.0 ** (-5.0 - h))) for h in range(RET_HEADS))

C_AQ, C_AK, C_AV, C_AG = 0, 512, 640, 768
C_RQ, C_RK, C_RV, C_RG = 1280, 1792, 2304, 2816
C_CB, C_CC, C_CX, C_CG = 3328, 3840, 4352, 4864

ADAM_LR = 0.001
ADAM_B1 = 0.9
ADAM_B2 = 0.999
ADAM_EPS = 1e-08
ADAM_WD = 0.01
ADAM_STEP = 10

N_CHIPS = 4
SHARD_IN = PROJ_WIDTH // N_CHIPS
SHARD_D = D_MODEL // N_CHIPS
LANES = 128
PACK_IN = D_MODEL * SHARD_IN
PACK_BR = N_BRANCH * BRANCH_WIDTH * SHARD_D
PACK_OUT = SHARD_D * D_MODEL
PACK_ROWS = (PACK_IN + PACK_BR + PACK_OUT) // LANES

VMEM_LIMIT = 56 * 1024 * 1024
COL_TILE = 768
ROW_TILE = 544


def _cparams(*sem):
    return pltpu.CompilerParams(dimension_semantics=sem, vmem_limit_bytes=VMEM_LIMIT)


def _nt(a, b):
    return lax.dot_general(a, b, (((1,), (1,)), ((), ())), preferred_element_type=F32)


def _tn(a, b):
    return lax.dot_general(a, b, (((0,), (0,)), ((), ())), preferred_element_type=F32)


def _nn(a, b):
    return jnp.dot(a, b, preferred_element_type=F32)


def _silu(x):
    return x * jax.nn.sigmoid(x)


def _dsilu(x):
    s = jax.nn.sigmoid(x)
    return s * (1.0 + x * (1.0 - s))


def _bucket_table():
    r = np.arange(BLOCK)[:, None]
    c = np.arange(2 * BLOCK)[None, :]
    n = np.maximum(BLOCK + r - c, 0)
    nf = np.maximum(n, 1).astype(np.float32)
    large = MAX_EXACT + (np.log(nf / MAX_EXACT) / math.log(MAX_DISTANCE / MAX_EXACT)
                         * (N_BUCKETS - MAX_EXACT)).astype(np.int32)
    large = np.minimum(large, N_BUCKETS - 1)
    return np.where(n < MAX_EXACT, n, large).astype(np.int32)


def _rot_tables(lp):
    half = BLOCK // 2
    pos = (jnp.arange(lp) - PAD_FRONT).astype(F32)
    theta = 1.0 / (ROT_BASE ** jnp.linspace(0.0, 1.0, half, dtype=F32))
    ang = pos[:, None] * theta[None, :]
    cos, sin = jnp.cos(ang), jnp.sin(ang)
    return jnp.concatenate([cos, cos], axis=1), jnp.concatenate([-sin, sin], axis=1)


def norm_matmul(x2d, g, w, col0_blocks, n_col_blocks):
    t = x2d.shape[0]
    tm = ROW_TILE if t % ROW_TILE == 0 else BLOCK

    def body(x_ref, g_ref, w_ref, hb_ref, o_ref):
        @pl.when(pl.program_id(1) == 0)
        def _():
            x = x_ref[...]
            r = lax.rsqrt(jnp.mean(x * x, axis=-1, keepdims=True) + RMS_EPS)
            hb_ref[...] = (x * r * g_ref[...]).astype(BF16)

        o_ref[...] = _nt(hb_ref[...], w_ref[...])

    return pl.pallas_call(
        body, name="norm_matmul",
        grid=(t // tm, n_col_blocks),
        in_specs=[pl.BlockSpec((tm, D_MODEL), lambda i, j: (i, 0)),
                  pl.BlockSpec((1, D_MODEL), lambda i, j: (0, 0)),
                  pl.BlockSpec((COL_TILE, D_MODEL), lambda i, j: (j + col0_blocks, 0))],
        out_specs=[pl.BlockSpec((tm, D_MODEL), lambda i, j: (i, 0)),
                   pl.BlockSpec((tm, COL_TILE), lambda i, j: (i, j))],
        out_shape=[jax.ShapeDtypeStruct((t, D_MODEL), BF16),
                   jax.ShapeDtypeStruct((t, n_col_blocks * COL_TILE), F32)],
        compiler_params=_cparams("parallel", "arbitrary"),
    )(x2d, g, w)


def matmul_cols(a, w, col0_blocks, n_col_blocks):
    t, k = a.shape
    tm = ROW_TILE if t % ROW_TILE == 0 else BLOCK

    def body(a_ref, w_ref, o_ref):
        o_ref[...] = _nt(a_ref[...], w_ref[...])

    return pl.pallas_call(
        body, name="matmul_cols",
        grid=(t // tm, n_col_blocks),
        in_specs=[pl.BlockSpec((tm, k), lambda i, j: (i, 0)),
                  pl.BlockSpec((COL_TILE, k), lambda i, j: (j + col0_blocks, 0))],
        out_specs=pl.BlockSpec((tm, COL_TILE), lambda i, j: (i, j)),
        out_shape=jax.ShapeDtypeStruct((t, n_col_blocks * COL_TILE), F32),
        compiler_params=_cparams("parallel", "arbitrary"),
    )(a, w)


def _build_bias(bkt_ref, rb_ref, bias_s):
    bkt = bkt_ref[...]
    for h in range(ATT_HEADS):
        acc = jnp.zeros((BLOCK, 2 * BLOCK), F32)
        for b in range(N_BUCKETS):
            acc = jnp.where(bkt == b, rb_ref[b, h], acc)
        bias_s[h] = acc


def _band_mask(n):
    r = lax.broadcasted_iota(jnp.int32, (BLOCK, 2 * BLOCK), 0)
    c = lax.broadcasted_iota(jnp.int32, (BLOCK, 2 * BLOCK), 1)
    key_pos = (n - 1) * BLOCK + c
    return (c > r) & (c <= r + BLOCK) & (key_pos >= PAD_FRONT)


def _split_heads(kv, kh):
    lane = lax.broadcasted_iota(jnp.int32, kv.shape, 1)
    if kh == 0:
        lo = jnp.where(lane < ATT_HEAD_DIM, kv, 0.0)
        hi = pltpu.roll(lo, ATT_HEAD_DIM, 1)
    else:
        hi = jnp.where(lane >= ATT_HEAD_DIM, kv, 0.0)
        lo = pltpu.roll(hi, ATT_HEAD_DIM, 1)
    return lo, hi


def _merge_heads(acc_lo, acc_hi, kh):
    lane = lax.broadcasted_iota(jnp.int32, acc_lo.shape, 1)
    if kh == 0:
        return jnp.where(lane < ATT_HEAD_DIM, acc_lo + pltpu.roll(acc_hi, ATT_HEAD_DIM, 1), 0.0)
    return jnp.where(lane >= ATT_HEAD_DIM, acc_hi + pltpu.roll(acc_lo, ATT_HEAD_DIM, 1), 0.0)


def _softmax_sink(q2b, kxb, bias_h, mask, sink_h):
    s = _nt(q2b, kxb) * ATT_SCALE + bias_h
    s = jnp.where(mask, s, NEG_INF)
    m = jnp.maximum(jnp.max(s, axis=-1, keepdims=True), sink_h)
    p = jnp.exp(s - m)
    es = jnp.exp(sink_h - m)
    inv = 1.0 / (jnp.sum(p, axis=-1, keepdims=True) + es)
    return p * inv, es * inv


def _rot(t, cosf, sinf):
    return t * cosf + pltpu.roll(t, BLOCK // 2, 1) * sinf


def _rot_t(d, cosf, sinf):
    return d * cosf + pltpu.roll(d * sinf, BLOCK // 2, 1)


def _decay_tables(h):
    lg = LOG_GAMMA[h]
    i = lax.broadcasted_iota(jnp.int32, (BLOCK, BLOCK), 0)
    j = lax.broadcasted_iota(jnp.int32, (BLOCK, BLOCK), 1)
    diff = (i - j).astype(F32)
    dm = jnp.where(diff >= 0, jnp.exp(diff * lg), 0.0)
    row = lax.broadcasted_iota(jnp.int32, (BLOCK, 1), 0).astype(F32)
    zeta = jnp.exp((BLOCK - 1 - row) * lg)
    xi = jnp.exp((row + 1.0) * lg)
    return dm, zeta, xi, math.exp(BLOCK * lg)


def _valid_col(n):
    row = lax.broadcasted_iota(jnp.int32, (BLOCK, 1), 0)
    return ((n * BLOCK + row) >= PAD_FRONT).astype(F32)


def _shift_down(cur, prev, k):
    row = lax.broadcasted_iota(jnp.int32, cur.shape, 0)
    return jnp.where(row >= k, pltpu.roll(cur, k, 0), pltpu.roll(prev, k, 0))


def _shift_up(cur, nxt, k):
    row = lax.broadcasted_iota(jnp.int32, cur.shape, 0)
    return jnp.where(row < BLOCK - k, pltpu.roll(cur, BLOCK - k, 0), pltpu.roll(nxt, BLOCK - k, 0))


def mixers_fwd(proj, cosf, sinf, bkt, rel_bias, sinks, conv_w, nb, nc):
    def body(p_ref, cos_ref, sin_ref, bkt_ref, rb_ref, sk_ref, cw_ref, br_ref, st_ref,
             bias_s, kv_s, state_s, u_s):
        b = pl.program_id(0)
        n = pl.program_id(1)

        @pl.when((b == 0) & (n == 0))
        def _():
            _build_bias(bkt_ref, rb_ref, bias_s)

        @pl.when(n == 0)
        def _():
            kv_s[0:BLOCK, :] = jnp.zeros((BLOCK, 2 * BLOCK), F32)
            state_s[...] = jnp.zeros_like(state_s)
            u_s[...] = jnp.zeros_like(u_s)

        valid = _valid_col(n)
        mask = _band_mask(n)

        kv_s[BLOCK:2 * BLOCK, :] = p_ref[:, C_AK:C_AK + 2 * BLOCK]
        k_all = kv_s[:, 0:BLOCK]
        v_all = kv_s[:, BLOCK:2 * BLOCK]
        for kh in range(2):
            k_lo, k_hi = [t.astype(BF16) for t in _split_heads(k_all, kh)]
            v_lo, v_hi = [t.astype(BF16) for t in _split_heads(v_all, kh)]
            for jj in range(2):
                j = 2 * kh + jj
                q2b = p_ref[:, C_AQ + BLOCK * j:C_AQ + BLOCK * (j + 1)].astype(BF16)
                p_lo, _ = _softmax_sink(q2b, k_lo, bias_s[2 * j], mask, sk_ref[0, 2 * j])
                p_hi, _ = _softmax_sink(q2b, k_hi, bias_s[2 * j + 1], mask, sk_ref[0, 2 * j + 1])
                o2 = _nn(p_lo.astype(BF16), v_lo) + _nn(p_hi.astype(BF16), v_hi)
                gate = p_ref[:, C_AG + BLOCK * j:C_AG + BLOCK * (j + 1)]
                br_ref[:, BLOCK * j:BLOCK * (j + 1)] = (o2 * _silu(gate)).astype(BF16)
        kv_s[0:BLOCK, :] = kv_s[BLOCK:2 * BLOCK, :]

        cosv = cos_ref[...]
        sinv = sin_ref[...]
        for h in range(RET_HEADS):
            dm, zeta, xi, gamma_chunk = _decay_tables(h)
            q = _rot(p_ref[:, C_RQ + BLOCK * h:C_RQ + BLOCK * (h + 1)], cosv, sinv).astype(BF16)
            k = (_rot(p_ref[:, C_RK + BLOCK * h:C_RK + BLOCK * (h + 1)], cosv, sinv)
                 * RET_SCALE * valid).astype(BF16)
            v = p_ref[:, C_RV + BLOCK * h:C_RV + BLOCK * (h + 1)]
            s_prev = state_s[h]
            st_ref[0, 0, h] = s_prev
            a = (_nt(q, k) * dm).astype(BF16)
            o = _nn(a, v.astype(BF16)) + xi * _nn(q, s_prev.astype(BF16))
            mu = jnp.mean(o, axis=-1, keepdims=True)
            var = jnp.mean(jnp.square(o - mu), axis=-1, keepdims=True)
            oh = (o - mu) * lax.rsqrt(var + GN_EPS)
            gate = p_ref[:, C_RG + BLOCK * h:C_RG + BLOCK * (h + 1)]
            br_ref[:, BRANCH_WIDTH + BLOCK * h:BRANCH_WIDTH + BLOCK * (h + 1)] = (oh * _silu(gate)).astype(BF16)
            state_s[h] = gamma_chunk * s_prev + _tn(k, (v * zeta).astype(BF16))

        u = p_ref[:, C_CC:C_CC + BRANCH_WIDTH] * p_ref[:, C_CX:C_CX + BRANCH_WIDTH] * valid
        u_prev = u_s[...]
        y = (cw_ref[0:1, :] * _shift_down(u, u_prev, 2) + cw_ref[1:2, :] * _shift_down(u, u_prev, 1)
             + cw_ref[2:3, :] * u)
        yc = p_ref[:, C_CB:C_CB + BRANCH_WIDTH] * y * _silu(p_ref[:, C_CG:C_CG + BRANCH_WIDTH])
        br_ref[:, 2 * BRANCH_WIDTH:3 * BRANCH_WIDTH] = yc.astype(BF16)
        u_s[...] = u

    rows = nb * nc * BLOCK
    smem = pl.BlockSpec(memory_space=pltpu.SMEM)
    return pl.pallas_call(
        body, name="mixers_fwd",
        grid=(nb, nc),
        in_specs=[pl.BlockSpec((BLOCK, ABC_WIDTH), lambda b, n: (b * nc + n, 0)),
                  pl.BlockSpec((BLOCK, BLOCK), lambda b, n: (n, 0)),
                  pl.BlockSpec((BLOCK, BLOCK), lambda b, n: (n, 0)),
                  pl.BlockSpec((BLOCK, 2 * BLOCK), lambda b, n: (0, 0)),
                  smem, smem,
                  pl.BlockSpec((3, BRANCH_WIDTH), lambda b, n: (0, 0))],
        out_specs=[pl.BlockSpec((BLOCK, N_BRANCH * BRANCH_WIDTH), lambda b, n: (b * nc + n, 0)),
                   pl.BlockSpec((1, 1, RET_HEADS, BLOCK, BLOCK), lambda b, n: (b, n, 0, 0, 0))],
        out_shape=[jax.ShapeDtypeStruct((rows, N_BRANCH * BRANCH_WIDTH), BF16),
                   jax.ShapeDtypeStruct((nb, nc, RET_HEADS, BLOCK, BLOCK), F32)],
        scratch_shapes=[pltpu.VMEM((ATT_HEADS, BLOCK, 2 * BLOCK), F32),
                        pltpu.VMEM((2 * BLOCK, 2 * BLOCK), F32),
                        pltpu.VMEM((RET_HEADS, BLOCK, BLOCK), F32),
                        pltpu.VMEM((BLOCK, BRANCH_WIDTH), F32)],
        compiler_params=_cparams("arbitrary", "arbitrary"),
    )(proj, cosf, sinf, bkt, rel_bias, sinks, conv_w)


def mixers_bwd(proj, d_br, states, cosf, sinf, bkt, rel_bias, sinks, conv_w, nb, nc):
    def body(p_ref, kvp_ref, cp_ref, dbr_ref, st_ref, cos_ref, sin_ref, bkt_ref, rb_ref, sk_ref, cw_ref,
             dp_ref, drb_ref, dsk_ref, dcw_ref,
             bias_s, dbias_s, dkv_s, g_s, dy_s):
        b = pl.program_id(0)
        step = pl.program_id(1)
        n = nc - 1 - step
        first = (b == 0) & (step == 0)
        last = (b == nb - 1) & (step == nc - 1)

        @pl.when(first)
        def _():
            _build_bias(bkt_ref, rb_ref, bias_s)
            dbias_s[...] = jnp.zeros_like(dbias_s)
            dsk_ref[...] = jnp.zeros_like(dsk_ref)
            dcw_ref[...] = jnp.zeros_like(dcw_ref)
            drb_ref[...] = jnp.zeros_like(drb_ref)

        @pl.when(step == 0)
        def _():
            dkv_s[...] = jnp.zeros_like(dkv_s)
            g_s[...] = jnp.zeros_like(g_s)
            dy_s[...] = jnp.zeros_like(dy_s)

        valid = _valid_col(n)
        mask = _band_mask(n)
        has_prev = (n > 0).astype(F32)

        kv_prev = kvp_ref[...] * has_prev
        kv_cur = p_ref[:, C_AK:C_AK + 2 * BLOCK]
        k_all = jnp.concatenate([kv_prev[:, 0:BLOCK], kv_cur[:, 0:BLOCK]], axis=0)
        v_all = jnp.concatenate([kv_prev[:, BLOCK:], kv_cur[:, BLOCK:]], axis=0)
        dk_tot = jnp.zeros((2 * BLOCK, BLOCK), F32)
        dv_tot = jnp.zeros((2 * BLOCK, BLOCK), F32)
        for kh in range(2):
            ks = [t.astype(BF16) for t in _split_heads(k_all, kh)]
            vs = [t.astype(BF16) for t in _split_heads(v_all, kh)]
            dk_acc = [jnp.zeros((2 * BLOCK, BLOCK), F32), jnp.zeros((2 * BLOCK, BLOCK), F32)]
            dv_acc = [jnp.zeros((2 * BLOCK, BLOCK), F32), jnp.zeros((2 * BLOCK, BLOCK), F32)]
            for jj in range(2):
                j = 2 * kh + jj
                q2b = p_ref[:, C_AQ + BLOCK * j:C_AQ + BLOCK * (j + 1)].astype(BF16)
                gate = p_ref[:, C_AG + BLOCK * j:C_AG + BLOCK * (j + 1)]
                d_ya = dbr_ref[:, BLOCK * j:BLOCK * (j + 1)]
                do2 = (d_ya * _silu(gate)).astype(BF16)
                o2 = jnp.zeros((BLOCK, BLOCK), F32)
                dq2 = jnp.zeros((BLOCK, BLOCK), F32)
                for x in range(2):
                    h = 2 * j + x
                    p, p_sink = _softmax_sink(q2b, ks[x], bias_s[h], mask, sk_ref[0, h])
                    pb = p.astype(BF16)
                    o2 = o2 + _nn(pb, vs[x])
                    dp = _nt(do2, vs[x])
                    delta = jnp.sum(p * dp, axis=-1, keepdims=True)
                    ds = p * (dp - delta)
                    dbias_s[h] += ds
                    dsk_ref[h:h + 1, :] += jnp.broadcast_to(
                        jnp.sum(-p_sink * delta, axis=0, keepdims=True), (1, BLOCK))
                    dsb = ds.astype(BF16)
                    dq2 = dq2 + _nn(dsb, ks[x]) * ATT_SCALE
                    dk_acc[x] = dk_acc[x] + _tn(dsb, q2b) * ATT_SCALE
                    dv_acc[x] = dv_acc[x] + _tn(pb, do2)
                dp_ref[:, C_AQ + BLOCK * j:C_AQ + BLOCK * (j + 1)] = dq2.astype(BF16)
                dp_ref[:, C_AG + BLOCK * j:C_AG + BLOCK * (j + 1)] = (d_ya * o2 * _dsilu(gate)).astype(BF16)
            dk_tot = dk_tot + _merge_heads(dk_acc[0], dk_acc[1], kh)
            dv_tot = dv_tot + _merge_heads(dv_acc[0], dv_acc[1], kh)
        dp_ref[:, C_AK:C_AK + BLOCK] = (dk_tot[BLOCK:, :] + dkv_s[:, 0:BLOCK]).astype(BF16)
        dp_ref[:, C_AV:C_AV + BLOCK] = (dv_tot[BLOCK:, :] + dkv_s[:, BLOCK:]).astype(BF16)
        dkv_s[:, 0:BLOCK] = dk_tot[0:BLOCK, :]
        dkv_s[:, BLOCK:] = dv_tot[0:BLOCK, :]

        cosv = cos_ref[...]
        sinv = sin_ref[...]
        for h in range(RET_HEADS):
            dm, zeta, xi, gamma_chunk = _decay_tables(h)
            sl = lambda c0: slice(c0 + BLOCK * h, c0 + BLOCK * (h + 1))
            q = _rot(p_ref[:, sl(C_RQ)], cosv, sinv).astype(BF16)
            k = (_rot(p_ref[:, sl(C_RK)], cosv, sinv) * RET_SCALE * valid).astype(BF16)
            v = p_ref[:, sl(C_RV)]
            vb = v.astype(BF16)
            gate = p_ref[:, sl(C_RG)]
            s_prev = st_ref[0, 0, h].astype(BF16)
            g_next = g_s[h]
            a = (_nt(q, k) * dm).astype(BF16)
            o = _nn(a, vb) + xi * _nn(q, s_prev)
            mu = jnp.mean(o, axis=-1, keepdims=True)
            var = jnp.mean(jnp.square(o - mu), axis=-1, keepdims=True)
            rstd = lax.rsqrt(var + GN_EPS)
            oh = (o - mu) * rstd
            d_yr = dbr_ref[:, BRANCH_WIDTH + BLOCK * h:BRANCH_WIDTH + BLOCK * (h + 1)]
            dp_ref[:, sl(C_RG)] = (d_yr * oh * _dsilu(gate)).astype(BF16)
            doh = d_yr * _silu(gate)
            do = rstd * (doh - jnp.mean(doh, axis=-1, keepdims=True)
                         - oh * jnp.mean(doh * oh, axis=-1, keepdims=True))
            dob = do.astype(BF16)
            dxo = (do * xi).astype(BF16)
            da = (_nt(dob, vb) * dm).astype(BF16)
            gb = g_next.astype(BF16)
            zv = (v * zeta).astype(BF16)
            dq = _nn(da, k) + _nt(dxo, s_prev)
            dk = _tn(da, q) + _nt(zv, gb)
            dv = _tn(a, dob) + zeta * _nn(k, gb)
            g_s[h] = gamma_chunk * g_next + _tn(q, dxo)
            dp_ref[:, sl(C_RQ)] = _rot_t(dq, cosv, sinv).astype(BF16)
            dp_ref[:, sl(C_RK)] = _rot_t(dk * (RET_SCALE * valid), cosv, sinv).astype(BF16)
            dp_ref[:, sl(C_RV)] = dv.astype(BF16)

        w0, w1, w2 = cw_ref[0:1, :], cw_ref[1:2, :], cw_ref[2:3, :]
        cb = p_ref[:, C_CB:C_CB + BRANCH_WIDTH]
        cc = p_ref[:, C_CC:C_CC + BRANCH_WIDTH]
        cx = p_ref[:, C_CX:C_CX + BRANCH_WIDTH]
        cg = p_ref[:, C_CG:C_CG + BRANCH_WIDTH]
        u = cc * cx * valid
        u_prev = cp_ref[:, 0:BRANCH_WIDTH] * cp_ref[:, BRANCH_WIDTH:2 * BRANCH_WIDTH] * (_valid_col(n - 1) * has_prev)
        u1 = _shift_down(u, u_prev, 1)
        u2 = _shift_down(u, u_prev, 2)
        y = w0 * u2 + w1 * u1 + w2 * u
        d_yc = dbr_ref[:, 2 * BRANCH_WIDTH:3 * BRANCH_WIDTH]
        sg = _silu(cg)
        dp_ref[:, C_CB:C_CB + BRANCH_WIDTH] = (d_yc * y * sg).astype(BF16)
        dp_ref[:, C_CG:C_CG + BRANCH_WIDTH] = (d_yc * cb * y * _dsilu(cg)).astype(BF16)
        dy = d_yc * cb * sg
        dy_next = dy_s[...]
        du = (w2 * dy + w1 * _shift_up(dy, dy_next, 1) + w0 * _shift_up(dy, dy_next, 2)) * valid
        dp_ref[:, C_CC:C_CC + BRANCH_WIDTH] = (du * cx).astype(BF16)
        dp_ref[:, C_CX:C_CX + BRANCH_WIDTH] = (du * cc).astype(BF16)
        dcw_ref[0:1, :] += jnp.sum(dy * u2, axis=0, keepdims=True)
        dcw_ref[1:2, :] += jnp.sum(dy * u1, axis=0, keepdims=True)
        dcw_ref[2:3, :] += jnp.sum(dy * u, axis=0, keepdims=True)
        dy_s[...] = dy

        @pl.when(last)
        def _():
            bkt = bkt_ref[...]
            row = lax.broadcasted_iota(jnp.int32, (N_BUCKETS, BLOCK), 0)
            lane = lax.broadcasted_iota(jnp.int32, (N_BUCKETS, BLOCK), 1)

            def one_bucket(bk, acc):
                sel = bkt == bk
                for h in range(ATT_HEADS):
                    t = jnp.where(sel, dbias_s[h], 0.0)
                    s = jnp.sum(jnp.sum(t, axis=1, keepdims=True), axis=0, keepdims=True)
                    acc = acc + jnp.where((row == bk) & (lane == h), jnp.broadcast_to(s, acc.shape), 0.0)
                return acc

            drb_ref[...] = lax.fori_loop(0, N_BUCKETS, one_bucket, jnp.zeros((N_BUCKETS, BLOCK), F32))

    rows = nb * nc * BLOCK
    smem = pl.BlockSpec(memory_space=pltpu.SMEM)
    blk = lambda b, s: b * nc + (nc - 1 - s)
    prev = lambda b, s: b * nc + jnp.maximum(nc - 2 - s, 0)
    return pl.pallas_call(
        body, name="mixers_bwd",
        grid=(nb, nc),
        in_specs=[pl.BlockSpec((BLOCK, ABC_WIDTH), lambda b, s: (blk(b, s), 0)),
                  pl.BlockSpec((BLOCK, 2 * BLOCK), lambda b, s: (prev(b, s), C_AK // (2 * BLOCK))),
                  pl.BlockSpec((BLOCK, 1280), lambda b, s: (prev(b, s), C_CC // 1280)),
                  pl.BlockSpec((BLOCK, N_BRANCH * BRANCH_WIDTH), lambda b, s: (blk(b, s), 0)),
                  pl.BlockSpec((1, 1, RET_HEADS, BLOCK, BLOCK), lambda b, s: (b, nc - 1 - s, 0, 0, 0)),
                  pl.BlockSpec((BLOCK, BLOCK), lambda b, s: (nc - 1 - s, 0)),
                  pl.BlockSpec((BLOCK, BLOCK), lambda b, s: (nc - 1 - s, 0)),
                  pl.BlockSpec((BLOCK, 2 * BLOCK), lambda b, s: (0, 0)),
                  smem, smem,
                  pl.BlockSpec((3, BRANCH_WIDTH), lambda b, s: (0, 0))],
        out_specs=[pl.BlockSpec((BLOCK, ABC_WIDTH), lambda b, s: (blk(b, s), 0)),
                   pl.BlockSpec((N_BUCKETS, BLOCK), lambda b, s: (0, 0)),
                   pl.BlockSpec((ATT_HEADS, BLOCK), lambda b, s: (0, 0)),
                   pl.BlockSpec((8, BRANCH_WIDTH), lambda b, s: (0, 0))],
        out_shape=[jax.ShapeDtypeStruct((rows, ABC_WIDTH), BF16),
                   jax.ShapeDtypeStruct((N_BUCKETS, BLOCK), F32),
                   jax.ShapeDtypeStruct((ATT_HEADS, BLOCK), F32),
                   jax.ShapeDtypeStruct((8, BRANCH_WIDTH), F32)],
        scratch_shapes=[pltpu.VMEM((ATT_HEADS, BLOCK, 2 * BLOCK), F32),
                        pltpu.VMEM((ATT_HEADS, BLOCK, 2 * BLOCK), F32),
                        pltpu.VMEM((BLOCK, 2 * BLOCK), F32),
                        pltpu.VMEM((RET_HEADS, BLOCK, BLOCK), F32),
                        pltpu.VMEM((BLOCK, BRANCH_WIDTH), F32)],
        compiler_params=_cparams("arbitrary", "arbitrary"),
    )(proj, proj, proj, d_br, states, cosf, sinf, bkt, rel_bias, sinks, conv_w)


MERGE_TILE = 256


def _merge_forward(br_ref, m_ref, wb_ref, wo_ref):
    bo, gates = [], []
    mixed_pre = None
    for g in range(N_BRANCH):
        br_g = br_ref[:, BRANCH_WIDTH * g:BRANCH_WIDTH * (g + 1)]
        bo_g = jnp.concatenate([_nn(br_g, wb_ref[p, g]) for p in range(N_CHIPS)], axis=1)
        gate_g = jax.nn.sigmoid(m_ref[:, D_MODEL * g:D_MODEL * (g + 1)])
        bo.append(bo_g)
        gates.append(gate_g)
        mixed_pre = gate_g * bo_g if mixed_pre is None else mixed_pre + gate_g * bo_g
    mixed = _nn(mixed_pre.astype(BF16), wo_ref[...])
    r = lax.rsqrt(jnp.mean(mixed * mixed, axis=-1, keepdims=True) + RMS_EPS)
    return bo, gates, mixed_pre, mixed, r


def merge_fwd(x2d, br, pm, wb, wo, g_post):
    t = x2d.shape[0]
    tm = MERGE_TILE if t % MERGE_TILE == 0 else BLOCK

    def body(x_ref, br_ref, m_ref, wb_ref, wo_ref, g_ref, o_ref):
        _, _, _, mixed, r = _merge_forward(br_ref, m_ref, wb_ref, wo_ref)
        o_ref[...] = x_ref[...] + mixed * r * g_ref[...]

    return pl.pallas_call(
        body, name="merge_fwd",
        grid=(t // tm,),
        in_specs=[pl.BlockSpec((tm, D_MODEL), lambda i: (i, 0)),
                  pl.BlockSpec((tm, N_BRANCH * BRANCH_WIDTH), lambda i: (i, 0)),
                  pl.BlockSpec((tm, MERGE_WIDTH), lambda i: (i, 0)),
                  pl.BlockSpec((N_CHIPS, N_BRANCH, BRANCH_WIDTH, SHARD_D), lambda i: (0, 0, 0, 0)),
                  pl.BlockSpec((D_MODEL, D_MODEL), lambda i: (0, 0)),
                  pl.BlockSpec((1, D_MODEL), lambda i: (0, 0))],
        out_specs=pl.BlockSpec((tm, D_MODEL), lambda i: (i, 0)),
        out_shape=jax.ShapeDtypeStruct((t, D_MODEL), F32),
        compiler_params=_cparams("parallel"),
    )(x2d, br, pm, wb, wo, g_post)


def merge_bwd(d_out, br, pm, wb, wo, g_post, layer, acc):
    t = d_out.shape[0]
    tm = MERGE_TILE if t % MERGE_TILE == 0 else BLOCK

    def body(*refs):
        do_ref, br_ref, m_ref, wb_ref, wo_ref, g_ref = refs[:6]
        dbr_ref, dm_ref, dg_ref, dwb_ref, dwo_ref = refs[-5:]

        @pl.when(pl.program_id(0) == 0)
        def _():
            dwb_ref[...] = jnp.zeros_like(dwb_ref)
            dwo_ref[...] = jnp.zeros_like(dwo_ref)
            dg_ref[...] = jnp.zeros_like(dg_ref)

        bo, gates, mixed_pre, mixed, r = _merge_forward(br_ref, m_ref, wb_ref, wo_ref)
        d_o = do_ref[...]
        nh = mixed * r
        dg_ref[0:1, :] += jnp.sum(d_o * nh, axis=0, keepdims=True)
        dn = d_o * g_ref[...]
        d_mixed = (r * (dn - nh * jnp.mean(dn * nh, axis=-1, keepdims=True))).astype(BF16)
        dwo_ref[...] += _tn(mixed_pre.astype(BF16), d_mixed)
        d_pre = _nt(d_mixed, wo_ref[...])
        for g in range(N_BRANCH):
            br_g = br_ref[:, BRANCH_WIDTH * g:BRANCH_WIDTH * (g + 1)]
            d_bo = (d_pre * gates[g]).astype(BF16)
            dm_ref[:, D_MODEL * g:D_MODEL * (g + 1)] = (
                d_pre * bo[g] * gates[g] * (1.0 - gates[g])).astype(BF16)
            d_br_g = None
            for p in range(N_CHIPS):
                d_bo_p = d_bo[:, SHARD_D * p:SHARD_D * (p + 1)]
                part = _nt(d_bo_p, wb_ref[p, g])
                d_br_g = part if d_br_g is None else d_br_g + part
                dwb_ref[p, g] += _tn(br_g, d_bo_p)
            dbr_ref[:, BRANCH_WIDTH * g:BRANCH_WIDTH * (g + 1)] = d_br_g

    ins = [d_out, br, pm, wb, wo, g_post]
    in_specs = [pl.BlockSpec((tm, D_MODEL), lambda i: (i, 0)),
                pl.BlockSpec((tm, N_BRANCH * BRANCH_WIDTH), lambda i: (i, 0)),
                pl.BlockSpec((tm, MERGE_WIDTH), lambda i: (i, 0)),
                pl.BlockSpec((N_CHIPS, N_BRANCH, BRANCH_WIDTH, SHARD_D), lambda i: (0, 0, 0, 0)),
                pl.BlockSpec((D_MODEL, D_MODEL), lambda i: (0, 0)),
                pl.BlockSpec((1, D_MODEL), lambda i: (0, 0))]
    aliases = {}
    if acc is not None:
        ins += list(acc)
        in_specs += [ANY, ANY]
        aliases = {6: 3, 7: 4}
    return pl.pallas_call(
        body, name="merge_bwd",
        grid=(t // tm,),
        in_specs=in_specs,
        out_specs=[pl.BlockSpec((tm, N_BRANCH * BRANCH_WIDTH), lambda i: (i, 0)),
                   pl.BlockSpec((tm, MERGE_WIDTH), lambda i: (i, 0)),
                   pl.BlockSpec((8, D_MODEL), lambda i: (0, 0)),
                   pl.BlockSpec((None, N_CHIPS, N_BRANCH, BRANCH_WIDTH, SHARD_D), lambda i: (layer, 0, 0, 0, 0)),
                   pl.BlockSpec((None, D_MODEL, D_MODEL), lambda i: (layer, 0, 0))],
        out_shape=[jax.ShapeDtypeStruct((t, N_BRANCH * BRANCH_WIDTH), F32),
                   jax.ShapeDtypeStruct((t, MERGE_WIDTH), BF16),
                   jax.ShapeDtypeStruct((8, D_MODEL), F32),
                   jax.ShapeDtypeStruct((DEPTH, N_CHIPS, N_BRANCH, BRANCH_WIDTH, SHARD_D), F32),
                   jax.ShapeDtypeStruct((DEPTH, D_MODEL, D_MODEL), F32)],
        input_output_aliases=aliases,
        compiler_params=_cparams("arbitrary"),
    )(*ins)


def loss_head(xf, target2d, nb, nc):
    def body(x_ref, t_ref, l_ref, dx_ref):
        b = pl.program_id(0)
        n = pl.program_id(1)

        @pl.when((b == 0) & (n == 0))
        def _():
            l_ref[...] = jnp.zeros_like(l_ref)

        @pl.when(n == 0)
        def _():
            dx_ref[...] = jnp.zeros_like(dx_ref)

        @pl.when(n > 0)
        def _():
            e = x_ref[...] - t_ref[...]
            dx_ref[...] = e * (1.0 / D_MODEL)
            s = jnp.sum(jnp.sum(e * e, axis=1, keepdims=True), axis=0, keepdims=True)
            l_ref[...] += jnp.broadcast_to(s * (0.5 / D_MODEL), l_ref.shape)

    return pl.pallas_call(
        body, name="loss_head",
        grid=(nb, nc),
        in_specs=[pl.BlockSpec((BLOCK, D_MODEL), lambda b, n: (b * nc + n, 0)),
                  pl.BlockSpec((BLOCK, D_MODEL), lambda b, n: (b * (nc - 1) + jnp.maximum(n - 1, 0), 0))],
        out_specs=[pl.BlockSpec((8, BLOCK), lambda b, n: (0, 0)),
                   pl.BlockSpec((BLOCK, D_MODEL), lambda b, n: (b * nc + n, 0))],
        out_shape=[jax.ShapeDtypeStruct((8, BLOCK), F32),
                   jax.ShapeDtypeStruct(xf.shape, F32)],
        compiler_params=_cparams("arbitrary", "arbitrary"),
    )(xf, target2d)


N_ABC_TILES = ABC_WIDTH // COL_TILE
N_M_TILES = MERGE_WIDTH // COL_TILE


def proj_dgrad(d_abc, d_m, w, x2d, g, d_out):
    t = x2d.shape[0]
    tm = ROW_TILE if t % ROW_TILE == 0 else BLOCK
    nk = N_ABC_TILES + N_M_TILES

    def body(da_ref, dm_ref, w_ref, x_ref, g_ref, do_ref, dx_ref, dg_ref, acc):
        i = pl.program_id(0)
        k = pl.program_id(1)

        @pl.when((i == 0) & (k == 0))
        def _():
            dg_ref[...] = jnp.zeros_like(dg_ref)

        @pl.when(k == 0)
        def _():
            acc[...] = jnp.zeros_like(acc)

        @pl.when(k < N_ABC_TILES)
        def _():
            acc[...] += _nn(da_ref[...], w_ref[...])

        @pl.when(k >= N_ABC_TILES)
        def _():
            acc[...] += _nn(dm_ref[...], w_ref[...])

        @pl.when(k == nk - 1)
        def _():
            x = x_ref[...]
            r = lax.rsqrt(jnp.mean(x * x, axis=-1, keepdims=True) + RMS_EPS)
            nh = x * r
            dh = acc[...]
            dg_ref[0:1, :] += jnp.sum(dh * nh, axis=0, keepdims=True)
            dn = dh * g_ref[...]
            dx_ref[...] = do_ref[...] + r * (dn - nh * jnp.mean(dn * nh, axis=-1, keepdims=True))

    return pl.pallas_call(
        body, name="proj_dgrad",
        grid=(t // tm, nk),
        in_specs=[pl.BlockSpec((tm, COL_TILE), lambda i, k: (i, jnp.minimum(k, N_ABC_TILES - 1))),
                  pl.BlockSpec((tm, COL_TILE), lambda i, k: (i, jnp.maximum(k - N_ABC_TILES, 0))),
                  pl.BlockSpec((COL_TILE, D_MODEL), lambda i, k: (k, 0)),
                  pl.BlockSpec((tm, D_MODEL), lambda i, k: (i, 0)),
                  pl.BlockSpec((1, D_MODEL), lambda i, k: (0, 0)),
                  pl.BlockSpec((tm, D_MODEL), lambda i, k: (i, 0))],
        out_specs=[pl.BlockSpec((tm, D_MODEL), lambda i, k: (i, 0)),
                   pl.BlockSpec((8, D_MODEL), lambda i, k: (0, 0))],
        out_shape=[jax.ShapeDtypeStruct((t, D_MODEL), F32),
                   jax.ShapeDtypeStruct((8, D_MODEL), F32)],
        scratch_shapes=[pltpu.VMEM((tm, D_MODEL), F32)],
        compiler_params=_cparams("arbitrary", "arbitrary"),
    )(d_abc, d_m, w, x2d, g, d_out)


def proj_wgrad(hb, d_abc, d_m, layer, acc):
    t = hb.shape[0]
    tk = ROW_TILE if t % ROW_TILE == 0 else BLOCK
    nj = N_ABC_TILES + N_M_TILES

    def body(*refs):
        h_ref, da_ref, dm_ref = refs[:3]
        o_ref = refs[-1]
        j = pl.program_id(0)

        @pl.when(pl.program_id(1) == 0)
        def _():
            o_ref[...] = jnp.zeros_like(o_ref)

        @pl.when(j < N_ABC_TILES)
        def _():
            o_ref[...] += _tn(da_ref[...], h_ref[...])

        @pl.when(j >= N_ABC_TILES)
        def _():
            o_ref[...] += _tn(dm_ref[...], h_ref[...])

    ins = [hb, d_abc, d_m]
    in_specs = [pl.BlockSpec((tk, D_MODEL), lambda j, k: (k, 0)),
                pl.BlockSpec((tk, COL_TILE), lambda j, k: (k, jnp.minimum(j, N_ABC_TILES - 1))),
                pl.BlockSpec((tk, COL_TILE), lambda j, k: (k, jnp.maximum(j - N_ABC_TILES, 0)))]
    aliases = {}
    if acc is not None:
        ins.append(acc)
        in_specs.append(ANY)
        aliases = {3: 0}
    return pl.pallas_call(
        body, name="proj_wgrad",
        grid=(nj, t // tk),
        in_specs=in_specs,
        out_specs=pl.BlockSpec((None, COL_TILE, D_MODEL), lambda j, k: (layer, j, 0)),
        out_shape=jax.ShapeDtypeStruct((DEPTH, PROJ_WIDTH, D_MODEL), F32),
        input_output_aliases=aliases,
        compiler_params=_cparams("parallel", "arbitrary"),
    )(*ins)


def _adamw_math(w, g, m, v):
    m = ADAM_B1 * m + (1.0 - ADAM_B1) * g
    v = ADAM_B2 * v + (1.0 - ADAM_B2) * jnp.square(g)
    m_hat = m / (1.0 - ADAM_B1 ** ADAM_STEP)
    v_hat = v / (1.0 - ADAM_B2 ** ADAM_STEP)
    delta = -ADAM_LR * (m_hat / (jnp.sqrt(v_hat) + ADAM_EPS) + ADAM_WD * w)
    return delta, m, v


def adamw_big(w, g, m, v):
    r, c = w.shape
    tr = 256 if r % 256 == 0 else r

    def body(w_ref, g_ref, m_ref, v_ref, d_ref, mo_ref, vo_ref):
        d, m_new, v_new = _adamw_math(w_ref[...], g_ref[...], m_ref[...], v_ref[...])
        d_ref[...] = d
        mo_ref[...] = m_new
        vo_ref[...] = v_new

    spec = pl.BlockSpec((tr, c), lambda i: (i, 0))
    sds = jax.ShapeDtypeStruct((r, c), F32)
    return pl.pallas_call(
        body, name="adamw_big",
        grid=(r // tr,),
        in_specs=[spec] * 4, out_specs=[spec] * 3, out_shape=[sds] * 3,
        compiler_params=_cparams("parallel"),
    )(w, g, m, v)


def adamw_small(params):
    k = len(params)

    def body(*refs):
        ins, outs = refs[:4 * k], refs[4 * k:]
        for i in range(k):
            d, m_new, v_new = _adamw_math(*[r[...] for r in ins[4 * i:4 * i + 4]])
            outs[3 * i][...] = d
            outs[3 * i + 1][...] = m_new
            outs[3 * i + 2][...] = v_new

    flat = [a for p in params for a in p]
    vm = pl.BlockSpec(memory_space=pltpu.VMEM)
    out_shape = [jax.ShapeDtypeStruct(p[0].shape, F32) for p in params for _ in range(3)]
    res = pl.pallas_call(
        body, name="adamw_small",
        in_specs=[vm] * len(flat), out_specs=[vm] * len(out_shape), out_shape=out_shape,
    )(*flat)
    return [tuple(res[3 * i:3 * i + 3]) for i in range(k)]


ANY = pl.BlockSpec(memory_space=pl.ANY)


def _place():
    return lax.axis_index("x"), lax.axis_index("y"), lax.axis_index("c")


def gather_weight_shards(shards):
    n = len(shards)

    def body(*refs):
        w_refs, g_refs = refs[:n], refs[n:2 * n]
        send_sems, recv_sems, local_sems = refs[2 * n:]
        x, y, c = _place()
        me_p = 2 * x + y
        sibling = (x, y, 1 - c)
        chips = [(1 - x, y), (x, 1 - y), (1 - x, 1 - y)]

        def copy(k, src, dst, to):
            return pltpu.make_async_remote_copy(src_ref=src, dst_ref=dst, send_sem=send_sems.at[k],
                                                recv_sem=recv_sems.at[k], device_id=to, device_id_type=MESH)

        local, first, passed = [], [], []
        for t in range(n):
            for l in range(DEPTH):
                cp = pltpu.make_async_copy(w_refs[t].at[l], g_refs[t].at[l, me_p], local_sems.at[DEPTH * t + l])
                cp.start()
                local.append(cp)
            for k, (qx, qy) in enumerate(chips):
                cp = copy(6 * t + k, w_refs[t].at[c], g_refs[t].at[c, me_p], (qx, qy, c))
                cp.start()
                first.append(cp)
        for t in range(n):
            for k, (qx, qy) in enumerate(chips):
                slab = g_refs[t].at[c, 2 * qx + qy]
                copy(6 * t + k, slab, slab, (qx, qy, c)).wait_recv()
                fwd = copy(6 * t + 3 + k, slab, slab, sibling)
                fwd.start()
                passed.append(fwd)
        for t in range(n):
            for k, (qx, qy) in enumerate(chips):
                slab = g_refs[t].at[1 - c, 2 * qx + qy]
                copy(6 * t + 3 + k, slab, slab, sibling).wait_recv()
        for cp in first + passed:
            cp.wait_send()
        for cp in local:
            cp.wait()

    return pl.pallas_call(
        body, name="gather_weight_shards",
        in_specs=[ANY] * n, out_specs=[ANY] * n,
        out_shape=[jax.ShapeDtypeStruct((DEPTH, N_CHIPS) + s.shape[1:], s.dtype) for s in shards],
        scratch_shapes=[pltpu.SemaphoreType.DMA((6 * n,)), pltpu.SemaphoreType.DMA((6 * n,)),
                        pltpu.SemaphoreType.DMA((DEPTH * n,))],
    )(*shards)


def exchange_small(pack):
    def body(p_ref, o_ref, send_sems, recv_sems, local_sem):
        x, y, c = _place()
        me = 4 * x + 2 * y + c
        mine = pltpu.make_async_copy(p_ref, o_ref.at[me], local_sem)
        mine.start()
        sends = []
        for k in range(1, 8):
            fx, fy, fc = (k >> 2) & 1, (k >> 1) & 1, k & 1
            peer = (x ^ fx, y ^ fy, c ^ fc)
            cp = pltpu.make_async_remote_copy(src_ref=p_ref, dst_ref=o_ref.at[me], send_sem=send_sems.at[k - 1],
                                              recv_sem=recv_sems.at[k - 1], device_id=peer, device_id_type=MESH)
            cp.start()
            sends.append(cp)
        for k in range(1, 8):
            fx, fy, fc = (k >> 2) & 1, (k >> 1) & 1, k & 1
            peer = (x ^ fx, y ^ fy, c ^ fc)
            slot = o_ref.at[4 * peer[0] + 2 * peer[1] + peer[2]]
            pltpu.make_async_remote_copy(src_ref=slot, dst_ref=slot, send_sem=send_sems.at[k - 1],
                                         recv_sem=recv_sems.at[k - 1], device_id=peer, device_id_type=MESH).wait_recv()
        for cp in sends:
            cp.wait_send()
        mine.wait()

    return pl.pallas_call(
        body, name="exchange_small",
        in_specs=[ANY], out_specs=ANY,
        out_shape=jax.ShapeDtypeStruct((8,) + pack.shape, pack.dtype),
        scratch_shapes=[pltpu.SemaphoreType.DMA((7,)), pltpu.SemaphoreType.DMA((7,)), pltpu.SemaphoreType.DMA],
    )(pack)


def sibling_swap_layers(grads):
    n = len(grads)

    def body(*refs):
        g_refs, o_refs = refs[:n], refs[n:2 * n]
        send_sems, recv_sems = refs[2 * n:]
        x, y, c = _place()
        cps = []
        for t in range(n):
            cp = pltpu.make_async_remote_copy(src_ref=g_refs[t].at[1 - c], dst_ref=o_refs[t],
                                              send_sem=send_sems.at[t], recv_sem=recv_sems.at[t],
                                              device_id=(x, y, 1 - c), device_id_type=MESH)
            cp.start()
            cps.append(cp)
        for cp in cps:
            cp.wait()

    return pl.pallas_call(
        body, name="sibling_swap_layers",
        in_specs=[ANY] * n, out_specs=[ANY] * n,
        out_shape=[jax.ShapeDtypeStruct(g.shape[1:], g.dtype) for g in grads],
        scratch_shapes=[pltpu.SemaphoreType.DMA((n,)), pltpu.SemaphoreType.DMA((n,))],
    )(*grads)


def _row_tile(r):
    return max(t for t in range(16, 513, 16) if r % t == 0)


def add_own_layer(g, other, c_arr):
    _, _, r, cols = g.shape
    tr = _row_tile(r)

    def body(c_ref, a_ref, b_ref, o_ref):
        o_ref[...] = (a_ref[...] + b_ref[...]).astype(BF16)

    return pl.pallas_call(
        body, name="add_own_layer",
        grid_spec=pltpu.PrefetchScalarGridSpec(
            num_scalar_prefetch=1, grid=(N_CHIPS, r // tr),
            in_specs=[pl.BlockSpec((None, None, tr, cols), lambda p, i, c_ref: (c_ref[0], p, i, 0)),
                      pl.BlockSpec((None, tr, cols), lambda p, i, c_ref: (p, i, 0))],
            out_specs=pl.BlockSpec((None, tr, cols), lambda p, i, c_ref: (p, i, 0))),
        out_shape=jax.ShapeDtypeStruct((N_CHIPS, r, cols), BF16),
        compiler_params=_cparams("parallel", "parallel"),
    )(c_arr, g, other)


def scatter_to_chips(partials):
    n = len(partials)

    def body(*refs):
        s_refs, o_refs = refs[:n], refs[n:2 * n]
        send_sems, recv_sems, local_sems = refs[2 * n:]
        x, y, c = _place()
        me_p = 2 * x + y
        chips = [(1 - x, y), (x, 1 - y), (1 - x, 1 - y)]
        local, cps = [], []
        for t in range(n):
            cp = pltpu.make_async_copy(s_refs[t].at[me_p], o_refs[t].at[me_p], local_sems.at[t])
            cp.start()
            local.append(cp)
            for k, (qx, qy) in enumerate(chips):
                cp = pltpu.make_async_remote_copy(src_ref=s_refs[t].at[2 * qx + qy], dst_ref=o_refs[t].at[me_p],
                                                  send_sem=send_sems.at[3 * t + k], recv_sem=recv_sems.at[3 * t + k],
                                                  device_id=(qx, qy, c), device_id_type=MESH)
                cp.start()
                cps.append(cp)
        for t in range(n):
            for k, (qx, qy) in enumerate(chips):
                slot = o_refs[t].at[2 * qx + qy]
                pltpu.make_async_remote_copy(src_ref=slot, dst_ref=slot, send_sem=send_sems.at[3 * t + k],
                                             recv_sem=recv_sems.at[3 * t + k], device_id=(qx, qy, c),
                                             device_id_type=MESH).wait_recv()
        for cp in cps:
            cp.wait_send()
        for cp in local:
            cp.wait()

    return pl.pallas_call(
        body, name="scatter_to_chips",
        in_specs=[ANY] * n, out_specs=[ANY] * n,
        out_shape=[jax.ShapeDtypeStruct(s.shape, s.dtype) for s in partials],
        scratch_shapes=[pltpu.SemaphoreType.DMA((3 * n,)), pltpu.SemaphoreType.DMA((3 * n,)),
                        pltpu.SemaphoreType.DMA((n,))],
    )(*partials)


def sum_chips(parts):
    _, r, cols = parts.shape
    tr = _row_tile(r)

    def body(p_ref, o_ref):
        acc = p_ref[0].astype(F32)
        for p in range(1, N_CHIPS):
            acc = acc + p_ref[p].astype(F32)
        o_ref[...] = acc

    return pl.pallas_call(
        body, name="sum_chips",
        grid=(r // tr,),
        in_specs=[pl.BlockSpec((N_CHIPS, tr, cols), lambda i: (0, i, 0))],
        out_specs=pl.BlockSpec((tr, cols), lambda i: (i, 0)),
        out_shape=jax.ShapeDtypeStruct((r, cols), F32),
        compiler_params=_cparams("parallel"),
    )(parts)


def sibling_share_layer(mine):
    n = len(mine)

    def body(*refs):
        m_refs, o_refs = refs[:n], refs[n:2 * n]
        send_sems, recv_sems, local_sems = refs[2 * n:]
        x, y, c = _place()
        sibling = (x, y, 1 - c)
        local, cps = [], []
        for t in range(n):
            cp = pltpu.make_async_copy(m_refs[t], o_refs[t].at[c], local_sems.at[t])
            cp.start()
            local.append(cp)
            cp = pltpu.make_async_remote_copy(src_ref=m_refs[t], dst_ref=o_refs[t].at[c], send_sem=send_sems.at[t],
                                              recv_sem=recv_sems.at[t], device_id=sibling, device_id_type=MESH)
            cp.start()
            cps.append(cp)
        for t in range(n):
            slot = o_refs[t].at[1 - c]
            pltpu.make_async_remote_copy(src_ref=slot, dst_ref=slot, send_sem=send_sems.at[t],
                                         recv_sem=recv_sems.at[t], device_id=sibling, device_id_type=MESH).wait_recv()
        for cp in cps:
            cp.wait_send()
        for cp in local:
            cp.wait()

    return pl.pallas_call(
        body, name="sibling_share_layer",
        in_specs=[ANY] * n, out_specs=[ANY] * n,
        out_shape=[jax.ShapeDtypeStruct((DEPTH,) + m.shape, m.dtype) for m in mine],
        scratch_shapes=[pltpu.SemaphoreType.DMA((n,)), pltpu.SemaphoreType.DMA((n,)), pltpu.SemaphoreType.DMA((n,))],
    )(*mine)


SP_META = 2 * (N_META * D_MODEL // LANES)
SP_NORM = DEPTH * D_MODEL // LANES
SP_RB = DEPTH * N_BUCKETS
SP_SINK = DEPTH * ATT_HEADS
SP_CONV = DEPTH * 3 * BRANCH_WIDTH // LANES
SP_LOSS = 8
SP_ROWS = SP_META + 2 * SP_NORM + SP_RB + SP_SINK + SP_CONV + SP_LOSS


def sum_small(slots):
    half = SP_META // 2
    rb0 = SP_META + 2 * SP_NORM
    rest_rows = SP_ROWS - SP_META

    def body(s_ref, meta_ref, rest_ref):
        acc = s_ref[0]
        for d in range(1, 8):
            acc = acc + s_ref[d]
        meta_ref[...] = acc[0:half] + acc[half:SP_META]
        rest_ref[...] = acc[SP_META:]
        rest_ref[rb0 - SP_META:rb0 - SP_META + N_BUCKETS, :] = (
            acc[rb0:rb0 + N_BUCKETS] + acc[rb0 + N_BUCKETS:rb0 + 2 * N_BUCKETS])

    vm = pl.BlockSpec(memory_space=pltpu.VMEM)
    return pl.pallas_call(
        body, name="sum_small",
        in_specs=[vm], out_specs=[vm, vm],
        out_shape=[jax.ShapeDtypeStruct((half, LANES), F32), jax.ShapeDtypeStruct((rest_rows, LANES), F32)],
    )(slots)


def local_step(x, loss_target, meta_full, rel_bias, norm_pre, conv_w_full, attn_sinks, norm_post, weights):
    nb, seq, _ = x.shape
    nc = seq // BLOCK + 1
    lp = nc * BLOCK
    rows = nb * lp
    pad = jnp.zeros((nb, PAD_FRONT, D_MODEL), F32)
    meta = jnp.broadcast_to(meta_full[None], (nb, N_META, D_MODEL))
    h0 = jnp.concatenate([pad, meta, x], axis=1).reshape(rows, D_MODEL)
    cosf, sinf = _rot_tables(lp)
    bkt = jnp.asarray(_bucket_table())

    acts = []
    h = h0
    for l in range(DEPTH):
        w_in, w_br, w_out = weights[l]
        g_pre = norm_pre[l][None]
        hb, p_abc = norm_matmul(h, g_pre, w_in, 0, N_ABC_TILES)
        p_m = matmul_cols(hb, w_in, N_ABC_TILES, N_M_TILES)
        br, states = mixers_fwd(p_abc, cosf, sinf, bkt, rel_bias, attn_sinks[l][None], conv_w_full[l], nb, nc)
        h_next = merge_fwd(h, br, p_m, w_br, w_out, norm_post[l][None])
        acts.append((h, hb, p_abc, p_m, br, states))
        h = h_next

    loss_part, d_h = loss_head(h, loss_target.reshape(nb * seq, D_MODEL), nb, nc)

    small = [None] * DEPTH
    g_win = None
    g_wbo = None
    for l in reversed(range(DEPTH)):
        w_in, w_br, w_out = weights[l]
        h_in, hb, p_abc, p_m, br, states = acts[l]
        d_br, d_m, d_gpost, g_wbr, g_wout = merge_bwd(d_h, br, p_m, w_br, w_out, norm_post[l][None], l, g_wbo)
        g_wbo = (g_wbr, g_wout)
        d_abc, d_rb, d_sk, d_cw = mixers_bwd(p_abc, d_br, states, cosf, sinf, bkt, rel_bias,
                                             attn_sinks[l][None], conv_w_full[l], nb, nc)
        d_h, d_gpre = proj_dgrad(d_abc, d_m, w_in, h_in, norm_pre[l][None], d_h)
        g_win = proj_wgrad(hb, d_abc, d_m, l, g_win)
        small[l] = (d_gpre[0], d_gpost[0], d_rb, d_sk, d_cw[0:3])

    d_h3 = d_h.reshape(nb, lp, D_MODEL)
    d_x = d_h3[:, BLOCK:]
    d_meta = d_h3[:, PAD_FRONT:BLOCK]
    sp = jnp.concatenate([
        d_meta.reshape(-1, LANES),
        jnp.stack([small[l][0] for l in range(DEPTH)]).reshape(-1, LANES),
        jnp.stack([small[l][1] for l in range(DEPTH)]).reshape(-1, LANES),
        jnp.concatenate([small[l][2] for l in range(DEPTH)], axis=0),
        jnp.concatenate([small[l][3] for l in range(DEPTH)], axis=0),
        jnp.stack([small[l][4] for l in range(DEPTH)]).reshape(-1, LANES),
        loss_part], axis=0)
    return d_x, g_win, g_wbo[0], g_wbo[1], sp


def kernel(x, meta_tokens, rel_bias, norm_pre, w_in, conv_w, attn_sinks, w_branch, w_out, norm_post, loss_target, m_meta_tokens, m_rel_bias, m_norm_pre, m_w_in, m_conv_w, m_attn_sinks, m_w_branch, m_w_out, m_norm_post, v_meta_tokens, v_rel_bias, v_norm_pre, v_w_in, v_conv_w, v_attn_sinks, v_w_branch, v_w_out, v_norm_post):
    assert x.shape[0] == 2 and SP_META == 2 * N_META * D_MODEL // LANES
    px, py, pc = _place()
    chip = 2 * px + py

    shards = [jnp.swapaxes(w_in, 1, 2).astype(BF16), w_branch.astype(BF16), w_out.astype(BF16)]
    a_in, a_br, a_out = gather_weight_shards(shards)
    a_in = a_in.reshape(DEPTH, PROJ_WIDTH, D_MODEL)
    a_out = a_out.reshape(DEPTH, D_MODEL, D_MODEL)
    weights = [(a_in[l], a_br[l], a_out[l]) for l in range(DEPTH)]
    side =jnp.concatenate([meta_tokens.reshape(-1), conv_w.reshape(-1)]).reshape(-1, LANES)
    side = jnp.concatenate([side, jnp.zeros((40 - side.shape[0], LANES), F32)], axis=0)
    side_all = exchange_small(side)
    side_chips = side_all[0::2]
    n_meta_rows = N_META * SHARD_D // LANES
    meta_full = jnp.moveaxis(side_chips[:, :n_meta_rows].reshape(N_CHIPS, N_META, SHARD_D), 0, 1).reshape(N_META, D_MODEL)
    conv_full = jnp.moveaxis(side_chips[:, n_meta_rows:n_meta_rows + 6].reshape(N_CHIPS, DEPTH, 3, LANES), 0, 2).reshape(DEPTH, 3, BRANCH_WIDTH)

    d_x, g_win, g_wbr, g_wout, sp = local_step(x, loss_target, meta_full, rel_bias, norm_pre, conv_full, attn_sinks,
                                               norm_post, weights)

    full = [g_win.reshape(DEPTH, N_CHIPS, SHARD_IN, D_MODEL),
            g_wbr.reshape(DEPTH, N_CHIPS, N_BRANCH * BRANCH_WIDTH, SHARD_D),
            g_wout.reshape(DEPTH, N_CHIPS, SHARD_D, D_MODEL)]
    others = sibling_swap_layers(full)
    c_arr = jnp.reshape(pc, (1,)).astype(jnp.int32)
    parts = scatter_to_chips([add_own_layer(g, o, c_arr) for g, o in zip(full, others)])
    r_in, r_br, r_out = sibling_share_layer([sum_chips(p) for p in parts])
    g_in = jnp.swapaxes(r_in, 1, 2)
    g_br = r_br.reshape(w_branch.shape)
    g_out = r_out

    meta_rows, rest = sum_small(exchange_small(sp))
    o = 0
    g_meta_full = meta_rows.reshape(N_META, D_MODEL)
    g_norm_pre = rest[o:o + SP_NORM].reshape(DEPTH, D_MODEL); o += SP_NORM
    g_norm_post = rest[o:o + SP_NORM].reshape(DEPTH, D_MODEL); o += SP_NORM
    g_rel_bias = rest[o:o + N_BUCKETS, :ATT_HEADS]; o += SP_RB
    g_sinks = rest[o:o + SP_SINK, 0].reshape(DEPTH, ATT_HEADS); o += SP_SINK
    g_conv_full = rest[o:o + SP_CONV].reshape(DEPTH, 3, BRANCH_WIDTH); o += SP_CONV
    loss = rest[o, 0]
    g_meta = lax.dynamic_slice_in_dim(g_meta_full, chip * SHARD_D, SHARD_D, axis=1)
    g_conv = lax.dynamic_slice_in_dim(g_conv_full, chip * LANES, LANES, axis=2)

    def big(w, g, m, v):
        shp = w.shape
        to2 = lambda a: a.reshape(-1, shp[-1])
        return [r.reshape(shp) for r in adamw_big(to2(w), to2(g), to2(m), to2(v))]

    u_in = big(w_in, g_in, m_w_in, v_w_in)
    u_br = big(w_branch, g_br, m_w_branch, v_w_branch)
    u_out = big(w_out, g_out, m_w_out, v_w_out)
    to2 = lambda a: a.reshape(-1, a.shape[-1])
    smalls = [(meta_tokens, g_meta, m_meta_tokens, v_meta_tokens),
              (rel_bias, g_rel_bias, m_rel_bias, v_rel_bias),
              (norm_pre, g_norm_pre, m_norm_pre, v_norm_pre),
              (to2(conv_w), to2(g_conv), to2(m_conv_w), to2(v_conv_w)),
              (attn_sinks, g_sinks, m_attn_sinks, v_attn_sinks),
              (norm_post, g_norm_post, m_norm_post, v_norm_post)]
    u_meta, u_rb, u_npre, u_conv, u_sink, u_npost = adamw_small(smalls)
    u_conv = tuple(a.reshape(conv_w.shape) for a in u_conv)

    grads = [g_meta, g_rel_bias, g_norm_pre, g_in, g_conv, g_sinks, g_br, g_out, g_norm_post]
    upd = [u_meta, u_rb, u_npre, u_in, u_conv, u_sink, u_br, u_out, u_npost]
    return (loss, d_x, *grads, *[u[0] for u in upd], *[u[1] for u in upd], *[u[2] for u in upd])
```

```python
import functools
import math

import numpy as np
import jax
import jax.numpy as jnp
from jax import lax
from jax.experimental import pallas as pl
from jax.experimental.pallas import tpu as pltpu

F32 = jnp.float32
BF16 = jnp.bfloat16
MESH = pl.DeviceIdType.MESH

D_MODEL = 1024
DEPTH = 2
N_META = 16
BLOCK = 128
PAD_FRONT = BLOCK - N_META
ATT_HEADS = 8
ATT_HEAD_DIM = 64
N_BUCKETS = 32
MAX_EXACT = 16
MAX_DISTANCE = 128
RET_HEADS = 4
ROT_BASE = 10000.0
N_BRANCH = 3
BRANCH_WIDTH = 512
PROJ_WIDTH = 8448
ABC_WIDTH = 5376
MERGE_WIDTH = N_BRANCH * D_MODEL
RMS_EPS = 1e-6
GN_EPS = 1e-6
NEG_INF = -1e30
ATT_SCALE = ATT_HEAD_DIM ** -0.5
RET_SCALE = BLOCK ** -0.5
LOG_GAMMA = tuple(math.log1p(-(2.0 ** (-5.0 - h))) for h in range(RET_HEADS))

C_AQ, C_AK, C_AV, C_AG = 0, 512, 640, 768
C_RQ, C_RK, C_RV, C_RG = 1280, 1792, 2304, 2816
C_CB, C_CC, C_CX, C_CG = 3328, 3840, 4352, 4864

ADAM_LR = 0.001
ADAM_B1 = 0.9
ADAM_B2 = 0.999
ADAM_EPS = 1e-08
ADAM_WD = 0.01
ADAM_STEP = 10

N_CHIPS = 4
SHARD_IN = PROJ_WIDTH // N_CHIPS
SHARD_D = D_MODEL // N_CHIPS
LANES = 128
PACK_IN = D_MODEL * SHARD_IN
PACK_BR = N_BRANCH * BRANCH_WIDTH * SHARD_D
PACK_OUT = SHARD_D * D_MODEL
PACK_ROWS = (PACK_IN + PACK_BR + PACK_OUT) // LANES

VMEM_LIMIT = 56 * 1024 * 1024
COL_TILE = 768
ROW_TILE = 544


def _cparams(*sem):
    return pltpu.CompilerParams(dimension_semantics=sem, vmem_limit_bytes=VMEM_LIMIT)


def _nt(a, b):
    return lax.dot_general(a, b, (((1,), (1,)), ((), ())), preferred_element_type=F32)


def _tn(a, b):
    return lax.dot_general(a, b, (((0,), (0,)), ((), ())), preferred_element_type=F32)


def _nn(a, b):
    return jnp.dot(a, b, preferred_element_type=F32)


def _silu(x):
    return x * jax.nn.sigmoid(x)


def _dsilu(x):
    s = jax.nn.sigmoid(x)
    return s * (1.0 + x * (1.0 - s))


def _bucket_table():
    r = np.arange(BLOCK)[:, None]
    c = np.arange(2 * BLOCK)[None, :]
    n = np.maximum(BLOCK + r - c, 0)
    nf = np.maximum(n, 1).astype(np.float32)
    large = MAX_EXACT + (np.log(nf / MAX_EXACT) / math.log(MAX_DISTANCE / MAX_EXACT)
                         * (N_BUCKETS - MAX_EXACT)).astype(np.int32)
    large = np.minimum(large, N_BUCKETS - 1)
    return np.where(n < MAX_EXACT, n, large).astype(np.int32)


def _rot_tables(lp):
    half = BLOCK // 2
    pos = (jnp.arange(lp) - PAD_FRONT).astype(F32)
    theta = 1.0 / (ROT_BASE ** jnp.linspace(0.0, 1.0, half, dtype=F32))
    ang = pos[:, None] * theta[None, :]
    cos, sin = jnp.cos(ang), jnp.sin(ang)
    return jnp.concatenate([cos, cos], axis=1), jnp.concatenate([-sin, sin], axis=1)


def norm_matmul(x2d, g, w, col0_blocks, n_col_blocks):
    t = x2d.shape[0]
    tm = ROW_TILE if t % ROW_TILE == 0 else BLOCK

    def body(x_ref, g_ref, w_ref, hb_ref, o_ref):
        @pl.when(pl.program_id(1) == 0)
        def _():
            x = x_ref[...]
            r = lax.rsqrt(jnp.mean(x * x, axis=-1, keepdims=True) + RMS_EPS)
            hb_ref[...] = (x * r * g_ref[...]).astype(BF16)

        o_ref[...] = _nt(hb_ref[...], w_ref[...])

    return pl.pallas_call(
        body, name="norm_matmul",
        grid=(t // tm, n_col_blocks),
        in_specs=[pl.BlockSpec((tm, D_MODEL), lambda i, j: (i, 0)),
                  pl.BlockSpec((1, D_MODEL), lambda i, j: (0, 0)),
                  pl.BlockSpec((COL_TILE, D_MODEL), lambda i, j: (j + col0_blocks, 0))],
        out_specs=[pl.BlockSpec((tm, D_MODEL), lambda i, j: (i, 0)),
                   pl.BlockSpec((tm, COL_TILE), lambda i, j: (i, j))],
        out_shape=[jax.ShapeDtypeStruct((t, D_MODEL), BF16),
                   jax.ShapeDtypeStruct((t, n_col_blocks * COL_TILE), F32)],
        compiler_params=_cparams("parallel", "arbitrary"),
    )(x2d, g, w)


def matmul_cols(a, w, col0_blocks, n_col_blocks):
    t, k = a.shape
    tm = ROW_TILE if t % ROW_TILE == 0 else BLOCK

    def body(a_ref, w_ref, o_ref):
        o_ref[...] = _nt(a_ref[...], w_ref[...])

    return pl.pallas_call(
        body, name="matmul_cols",
        grid=(t // tm, n_col_blocks),
        in_specs=[pl.BlockSpec((tm, k), lambda i, j: (i, 0)),
                  pl.BlockSpec((COL_TILE, k), lambda i, j: (j + col0_blocks, 0))],
        out_specs=pl.BlockSpec((tm, COL_TILE), lambda i, j: (i, j)),
        out_shape=jax.ShapeDtypeStruct((t, n_col_blocks * COL_TILE), F32),
        compiler_params=_cparams("parallel", "arbitrary"),
    )(a, w)


def _build_bias(bkt_ref, rb_ref, bias_s):
    bkt = bkt_ref[...]
    for h in range(ATT_HEADS):
        acc = jnp.zeros((BLOCK, 2 * BLOCK), F32)
        for b in range(N_BUCKETS):
            acc = jnp.where(bkt == b, rb_ref[b, h], acc)
        bias_s[h] = acc


def _band_mask(n):
    r = lax.broadcasted_iota(jnp.int32, (BLOCK, 2 * BLOCK), 0)
    c = lax.broadcasted_iota(jnp.int32, (BLOCK, 2 * BLOCK), 1)
    key_pos = (n - 1) * BLOCK + c
    return (c > r) & (c <= r + BLOCK) & (key_pos >= PAD_FRONT)


def _split_heads(kv, kh):
    lane = lax.broadcasted_iota(jnp.int32, kv.shape, 1)
    if kh == 0:
        lo = jnp.where(lane < ATT_HEAD_DIM, kv, 0.0)
        hi = pltpu.roll(lo, ATT_HEAD_DIM, 1)
    else:
        hi = jnp.where(lane >= ATT_HEAD_DIM, kv, 0.0)
        lo = pltpu.roll(hi, ATT_HEAD_DIM, 1)
    return lo, hi


def _merge_heads(acc_lo, acc_hi, kh):
    lane = lax.broadcasted_iota(jnp.int32, acc_lo.shape, 1)
    if kh == 0:
        return jnp.where(lane < ATT_HEAD_DIM, acc_lo + pltpu.roll(acc_hi, ATT_HEAD_DIM, 1), 0.0)
    return jnp.where(lane >= ATT_HEAD_DIM, acc_hi + pltpu.roll(acc_lo, ATT_HEAD_DIM, 1), 0.0)


def _softmax_sink(q2b, kxb, bias_h, mask, sink_h):
    s = _nt(q2b, kxb) * ATT_SCALE + bias_h
    s = jnp.where(mask, s, NEG_INF)
    m = jnp.maximum(jnp.max(s, axis=-1, keepdims=True), sink_h)
    p = jnp.exp(s - m)
    es = jnp.exp(sink_h - m)
    inv = 1.0 / (jnp.sum(p, axis=-1, keepdims=True) + es)
    return p * inv, es * inv


def _rot(t, cosf, sinf):
    return t * cosf + pltpu.roll(t, BLOCK // 2, 1) * sinf


def _rot_t(d, cosf, sinf):
    return d * cosf + pltpu.roll(d * sinf, BLOCK // 2, 1)


def _decay_tables(h):
    lg = LOG_GAMMA[h]
    i = lax.broadcasted_iota(jnp.int32, (BLOCK, BLOCK), 0)
    j = lax.broadcasted_iota(jnp.int32, (BLOCK, BLOCK), 1)
    diff = (i - j).astype(F32)
    dm = jnp.where(diff >= 0, jnp.exp(diff * lg), 0.0)
    row = lax.broadcasted_iota(jnp.int32, (BLOCK, 1), 0).astype(F32)
    zeta = jnp.exp((BLOCK - 1 - row) * lg)
    xi = jnp.exp((row + 1.0) * lg)
    return dm, zeta, xi, math.exp(BLOCK * lg)


def _valid_col(n):
    row = lax.broadcasted_iota(jnp.int32, (BLOCK, 1), 0)
    return ((n * BLOCK + row) >= PAD_FRONT).astype(F32)


def _shift_down(cur, prev, k):
    row = lax.broadcasted_iota(jnp.int32, cur.shape, 0)
    return jnp.where(row >= k, pltpu.roll(cur, k, 0), pltpu.roll(prev, k, 0))


def _shift_up(cur, nxt, k):
    row = lax.broadcasted_iota(jnp.int32, cur.shape, 0)
    return jnp.where(row < BLOCK - k, pltpu.roll(cur, BLOCK - k, 0), pltpu.roll(nxt, BLOCK - k, 0))


def mixers_fwd(proj, cosf, sinf, bkt, rel_bias, sinks, conv_w, nb, nc):
    def body(p_ref, cos_ref, sin_ref, bkt_ref, rb_ref, sk_ref, cw_ref, br_ref, st_ref,
             bias_s, kv_s, state_s, u_s):
        b = pl.program_id(0)
        n = pl.program_id(1)

        @pl.when((b == 0) & (n == 0))
        def _():
            _build_bias(bkt_ref, rb_ref, bias_s)

        @pl.when(n == 0)
        def _():
            kv_s[0:BLOCK, :] = jnp.zeros((BLOCK, 2 * BLOCK), F32)
            state_s[...] = jnp.zeros_like(state_s)
            u_s[...] = jnp.zeros_like(u_s)

        valid = _valid_col(n)
        mask = _band_mask(n)

        kv_s[BLOCK:2 * BLOCK, :] = p_ref[:, C_AK:C_AK + 2 * BLOCK]
        k_all = kv_s[:, 0:BLOCK]
        v_all = kv_s[:, BLOCK:2 * BLOCK]
        for kh in range(2):
            k_lo, k_hi = [t.astype(BF16) for t in _split_heads(k_all, kh)]
            v_lo, v_hi = [t.astype(BF16) for t in _split_heads(v_all, kh)]
            for jj in range(2):
                j = 2 * kh + jj
                q2b = p_ref[:, C_AQ + BLOCK * j:C_AQ + BLOCK * (j + 1)].astype(BF16)
                p_lo, _ = _softmax_sink(q2b, k_lo, bias_s[2 * j], mask, sk_ref[0, 2 * j])
                p_hi, _ = _softmax_sink(q2b, k_hi, bias_s[2 * j + 1], mask, sk_ref[0, 2 * j + 1])
                o2 = _nn(p_lo.astype(BF16), v_lo) + _nn(p_hi.astype(BF16), v_hi)
                gate = p_ref[:, C_AG + BLOCK * j:C_AG + BLOCK * (j + 1)]
                br_ref[:, BLOCK * j:BLOCK * (j + 1)] = (o2 * _silu(gate)).astype(BF16)
        kv_s[0:BLOCK, :] = kv_s[BLOCK:2 * BLOCK, :]

        cosv = cos_ref[...]
        sinv = sin_ref[...]
        for h in range(RET_HEADS):
            dm, zeta, xi, gamma_chunk = _decay_tables(h)
            q = _rot(p_ref[:, C_RQ + BLOCK * h:C_RQ + BLOCK * (h + 1)], cosv, sinv).astype(BF16)
            k = (_rot(p_ref[:, C_RK + BLOCK * h:C_RK + BLOCK * (h + 1)], cosv, sinv)
                 * RET_SCALE * valid).astype(BF16)
            v = p_ref[:, C_RV + BLOCK * h:C_RV + BLOCK * (h + 1)]
            s_prev = state_s[h]
            st_ref[0, 0, h] = s_prev
            a = (_nt(q, k) * dm).astype(BF16)
            o = _nn(a, v.astype(BF16)) + xi * _nn(q, s_prev.astype(BF16))
            mu = jnp.mean(o, axis=-1, keepdims=True)
            var = jnp.mean(jnp.square(o - mu), axis=-1, keepdims=True)
            oh = (o - mu) * lax.rsqrt(var + GN_EPS)
            gate = p_ref[:, C_RG + BLOCK * h:C_RG + BLOCK * (h + 1)]
            br_ref[:, BRANCH_WIDTH + BLOCK * h:BRANCH_WIDTH + BLOCK * (h + 1)] = (oh * _silu(gate)).astype(BF16)
            state_s[h] = gamma_chunk * s_prev + _tn(k, (v * zeta).astype(BF16))

        u = p_ref[:, C_CC:C_CC + BRANCH_WIDTH] * p_ref[:, C_CX:C_CX + BRANCH_WIDTH] * valid
        u_prev = u_s[...]
        y = (cw_ref[0:1, :] * _shift_down(u, u_prev, 2) + cw_ref[1:2, :] * _shift_down(u, u_prev, 1)
             + cw_ref[2:3, :] * u)
        yc = p_ref[:, C_CB:C_CB + BRANCH_WIDTH] * y * _silu(p_ref[:, C_CG:C_CG + BRANCH_WIDTH])
        br_ref[:, 2 * BRANCH_WIDTH:3 * BRANCH_WIDTH] = yc.astype(BF16)
        u_s[...] = u

    rows = nb * nc * BLOCK
    smem = pl.BlockSpec(memory_space=pltpu.SMEM)
    return pl.pallas_call(
        body, name="mixers_fwd",
        grid=(nb, nc),
        in_specs=[pl.BlockSpec((BLOCK, ABC_WIDTH), lambda b, n: (b * nc + n, 0)),
                  pl.BlockSpec((BLOCK, BLOCK), lambda b, n: (n, 0)),
                  pl.BlockSpec((BLOCK, BLOCK), lambda b, n: (n, 0)),
                  pl.BlockSpec((BLOCK, 2 * BLOCK), lambda b, n: (0, 0)),
                  smem, smem,
                  pl.BlockSpec((3, BRANCH_WIDTH), lambda b, n: (0, 0))],
        out_specs=[pl.BlockSpec((BLOCK, N_BRANCH * BRANCH_WIDTH), lambda b, n: (b * nc + n, 0)),
                   pl.BlockSpec((1, 1, RET_HEADS, BLOCK, BLOCK), lambda b, n: (b, n, 0, 0, 0))],
        out_shape=[jax.ShapeDtypeStruct((rows, N_BRANCH * BRANCH_WIDTH), BF16),
                   jax.ShapeDtypeStruct((nb, nc, RET_HEADS, BLOCK, BLOCK), F32)],
        scratch_shapes=[pltpu.VMEM((ATT_HEADS, BLOCK, 2 * BLOCK), F32),
                        pltpu.VMEM((2 * BLOCK, 2 * BLOCK), F32),
                        pltpu.VMEM((RET_HEADS, BLOCK, BLOCK), F32),
                        pltpu.VMEM((BLOCK, BRANCH_WIDTH), F32)],
        compiler_params=_cparams("arbitrary", "arbitrary"),
    )(proj, cosf, sinf, bkt, rel_bias, sinks, conv_w)


def mixers_bwd(proj, d_br, states, cosf, sinf, bkt, rel_bias, sinks, conv_w, nb, nc):
    def body(p_ref, kvp_ref, cp_ref, dbr_ref, st_ref, cos_ref, sin_ref, bkt_ref, rb_ref, sk_ref, cw_ref,
             dp_ref, drb_ref, dsk_ref, dcw_ref,
             bias_s, dbias_s, dkv_s, g_s, dy_s):
        b = pl.program_id(0)
        step = pl.program_id(1)
        n = nc - 1 - step
        first = (b == 0) & (step == 0)
        last = (b == nb - 1) & (step == nc - 1)

        @pl.when(first)
        def _():
            _build_bias(bkt_ref, rb_ref, bias_s)
            dbias_s[...] = jnp.zeros_like(dbias_s)
            dsk_ref[...] = jnp.zeros_like(dsk_ref)
            dcw_ref[...] = jnp.zeros_like(dcw_ref)
            drb_ref[...] = jnp.zeros_like(drb_ref)

        @pl.when(step == 0)
        def _():
            dkv_s[...] = jnp.zeros_like(dkv_s)
            g_s[...] = jnp.zeros_like(g_s)
            dy_s[...] = jnp.zeros_like(dy_s)

        valid = _valid_col(n)
        mask = _band_mask(n)
        has_prev = (n > 0).astype(F32)

        kv_prev = kvp_ref[...] * has_prev
        kv_cur = p_ref[:, C_AK:C_AK + 2 * BLOCK]
        k_all = jnp.concatenate([kv_prev[:, 0:BLOCK], kv_cur[:, 0:BLOCK]], axis=0)
        v_all = jnp.concatenate([kv_prev[:, BLOCK:], kv_cur[:, BLOCK:]], axis=0)
        dk_tot = jnp.zeros((2 * BLOCK, BLOCK), F32)
        dv_tot = jnp.zeros((2 * BLOCK, BLOCK), F32)
        for kh in range(2):
            ks = [t.astype(BF16) for t in _split_heads(k_all, kh)]
            vs = [t.astype(BF16) for t in _split_heads(v_all, kh)]
            dk_acc = [jnp.zeros((2 * BLOCK, BLOCK), F32), jnp.zeros((2 * BLOCK, BLOCK), F32)]
            dv_acc = [jnp.zeros((2 * BLOCK, BLOCK), F32), jnp.zeros((2 * BLOCK, BLOCK), F32)]
            for jj in range(2):
                j = 2 * kh + jj
                q2b = p_ref[:, C_AQ + BLOCK * j:C_AQ + BLOCK * (j + 1)].astype(BF16)
                gate = p_ref[:, C_AG + BLOCK * j:C_AG + BLOCK * (j + 1)]
                d_ya = dbr_ref[:, BLOCK * j:BLOCK * (j + 1)]
                do2 = (d_ya * _silu(gate)).astype(BF16)
                o2 = jnp.zeros((BLOCK, BLOCK), F32)
                dq2 = jnp.zeros((BLOCK, BLOCK), F32)
                for x in range(2):
                    h = 2 * j + x
                    p, p_sink = _softmax_sink(q2b, ks[x], bias_s[h], mask, sk_ref[0, h])
                    pb = p.astype(BF16)
                    o2 = o2 + _nn(pb, vs[x])
                    dp = _nt(do2, vs[x])
                    delta = jnp.sum(p * dp, axis=-1, keepdims=True)
                    ds = p * (dp - delta)
                    dbias_s[h] += ds
                    dsk_ref[h:h + 1, :] += jnp.broadcast_to(
                        jnp.sum(-p_sink * delta, axis=0, keepdims=True), (1, BLOCK))
                    dsb = ds.astype(BF16)
                    dq2 = dq2 + _nn(dsb, ks[x]) * ATT_SCALE
                    dk_acc[x] = dk_acc[x] + _tn(dsb, q2b) * ATT_SCALE
                    dv_acc[x] = dv_acc[x] + _tn(pb, do2)
                dp_ref[:, C_AQ + BLOCK * j:C_AQ + BLOCK * (j + 1)] = dq2.astype(BF16)
                dp_ref[:, C_AG + BLOCK * j:C_AG + BLOCK * (j + 1)] = (d_ya * o2 * _dsilu(gate)).astype(BF16)
            dk_tot = dk_tot + _merge_heads(dk_acc[0], dk_acc[1], kh)
            dv_tot = dv_tot + _merge_heads(dv_acc[0], dv_acc[1], kh)
        dp_ref[:, C_AK:C_AK + BLOCK] = (dk_tot[BLOCK:, :] + dkv_s[:, 0:BLOCK]).astype(BF16)
        dp_ref[:, C_AV:C_AV + BLOCK] = (dv_tot[BLOCK:, :] + dkv_s[:, BLOCK:]).astype(BF16)
        dkv_s[:, 0:BLOCK] = dk_tot[0:BLOCK, :]
        dkv_s[:, BLOCK:] = dv_tot[0:BLOCK, :]

        cosv = cos_ref[...]
        sinv = sin_ref[...]
        for h in range(RET_HEADS):
            dm, zeta, xi, gamma_chunk = _decay_tables(h)
            sl = lambda c0: slice(c0 + BLOCK * h, c0 + BLOCK * (h + 1))
            q = _rot(p_ref[:, sl(C_RQ)], cosv, sinv).astype(BF16)
            k = (_rot(p_ref[:, sl(C_RK)], cosv, sinv) * RET_SCALE * valid).astype(BF16)
            v = p_ref[:, sl(C_RV)]
            vb = v.astype(BF16)
            gate = p_ref[:, sl(C_RG)]
            s_prev = st_ref[0, 0, h].astype(BF16)
            g_next = g_s[h]
            a = (_nt(q, k) * dm).astype(BF16)
            o = _nn(a, vb) + xi * _nn(q, s_prev)
            mu = jnp.mean(o, axis=-1, keepdims=True)
            var = jnp.mean(jnp.square(o - mu), axis=-1, keepdims=True)
            rstd = lax.rsqrt(var + GN_EPS)
            oh = (o - mu) * rstd
            d_yr = dbr_ref[:, BRANCH_WIDTH + BLOCK * h:BRANCH_WIDTH + BLOCK * (h + 1)]
            dp_ref[:, sl(C_RG)] = (d_yr * oh * _dsilu(gate)).astype(BF16)
            doh = d_yr * _silu(gate)
            do = rstd * (doh - jnp.mean(doh, axis=-1, keepdims=True)
                         - oh * jnp.mean(doh * oh, axis=-1, keepdims=True))
            dob = do.astype(BF16)
            dxo = (do * xi).astype(BF16)
            da = (_nt(dob, vb) * dm).astype(BF16)
            gb = g_next.astype(BF16)
            zv = (v * zeta).astype(BF16)
            dq = _nn(da, k) + _nt(dxo, s_prev)
            dk = _tn(da, q) + _nt(zv, gb)
            dv = _tn(a, dob) + zeta * _nn(k, gb)
            g_s[h] = gamma_chunk * g_next + _tn(q, dxo)
            dp_ref[:, sl(C_RQ)] = _rot_t(dq, cosv, sinv).astype(BF16)
            dp_ref[:, sl(C_RK)] = _rot_t(dk * (RET_SCALE * valid), cosv, sinv).astype(BF16)
            dp_ref[:, sl(C_RV)] = dv.astype(BF16)

        w0, w1, w2 = cw_ref[0:1, :], cw_ref[1:2, :], cw_ref[2:3, :]
        cb = p_ref[:, C_CB:C_CB + BRANCH_WIDTH]
        cc = p_ref[:, C_CC:C_CC + BRANCH_WIDTH]
        cx = p_ref[:, C_CX:C_CX + BRANCH_WIDTH]
        cg = p_ref[:, C_CG:C_CG + BRANCH_WIDTH]
        u = cc * cx * valid
        u_prev = cp_ref[:, 0:BRANCH_WIDTH] * cp_ref[:, BRANCH_WIDTH:2 * BRANCH_WIDTH] * (_valid_col(n - 1) * has_prev)
        u1 = _shift_down(u, u_prev, 1)
        u2 = _shift_down(u, u_prev, 2)
        y = w0 * u2 + w1 * u1 + w2 * u
        d_yc = dbr_ref[:, 2 * BRANCH_WIDTH:3 * BRANCH_WIDTH]
        sg = _silu(cg)
        dp_ref[:, C_CB:C_CB + BRANCH_WIDTH] = (d_yc * y * sg).astype(BF16)
        dp_ref[:, C_CG:C_CG + BRANCH_WIDTH] = (d_yc * cb * y * _dsilu(cg)).astype(BF16)
        dy = d_yc * cb * sg
        dy_next = dy_s[...]
        du = (w2 * dy + w1 * _shift_up(dy, dy_next, 1) + w0 * _shift_up(dy, dy_next, 2)) * valid
        dp_ref[:, C_CC:C_CC + BRANCH_WIDTH] = (du * cx).astype(BF16)
        dp_ref[:, C_CX:C_CX + BRANCH_WIDTH] = (du * cc).astype(BF16)
        dcw_ref[0:1, :] += jnp.sum(dy * u2, axis=0, keepdims=True)
        dcw_ref[1:2, :] += jnp.sum(dy * u1, axis=0, keepdims=True)
        dcw_ref[2:3, :] += jnp.sum(dy * u, axis=0, keepdims=True)
        dy_s[...] = dy

        @pl.when(last)
        def _():
            bkt = bkt_ref[...]
            row = lax.broadcasted_iota(jnp.int32, (N_BUCKETS, BLOCK), 0)
            lane = lax.broadcasted_iota(jnp.int32, (N_BUCKETS, BLOCK), 1)

            def one_bucket(bk, acc):
                sel = bkt == bk
                for h in range(ATT_HEADS):
                    t = jnp.where(sel, dbias_s[h], 0.0)
                    s = jnp.sum(jnp.sum(t, axis=1, keepdims=True), axis=0, keepdims=True)
                    acc = acc + jnp.where((row == bk) & (lane == h), jnp.broadcast_to(s, acc.shape), 0.0)
                return acc

            drb_ref[...] = lax.fori_loop(0, N_BUCKETS, one_bucket, jnp.zeros((N_BUCKETS, BLOCK), F32))

    rows = nb * nc * BLOCK
    smem = pl.BlockSpec(memory_space=pltpu.SMEM)
    blk = lambda b, s: b * nc + (nc - 1 - s)
    prev = lambda b, s: b * nc + jnp.maximum(nc - 2 - s, 0)
    return pl.pallas_call(
        body, name="mixers_bwd",
        grid=(nb, nc),
        in_specs=[pl.BlockSpec((BLOCK, ABC_WIDTH), lambda b, s: (blk(b, s), 0)),
                  pl.BlockSpec((BLOCK, 2 * BLOCK), lambda b, s: (prev(b, s), C_AK // (2 * BLOCK))),
                  pl.BlockSpec((BLOCK, 1280), lambda b, s: (prev(b, s), C_CC // 1280)),
                  pl.BlockSpec((BLOCK, N_BRANCH * BRANCH_WIDTH), lambda b, s: (blk(b, s), 0)),
                  pl.BlockSpec((1, 1, RET_HEADS, BLOCK, BLOCK), lambda b, s: (b, nc - 1 - s, 0, 0, 0)),
                  pl.BlockSpec((BLOCK, BLOCK), lambda b, s: (nc - 1 - s, 0)),
                  pl.BlockSpec((BLOCK, BLOCK), lambda b, s: (nc - 1 - s, 0)),
                  pl.BlockSpec((BLOCK, 2 * BLOCK), lambda b, s: (0, 0)),
                  smem, smem,
                  pl.BlockSpec((3, BRANCH_WIDTH), lambda b, s: (0, 0))],
        out_specs=[pl.BlockSpec((BLOCK, ABC_WIDTH), lambda b, s: (blk(b, s), 0)),
                   pl.BlockSpec((N_BUCKETS, BLOCK), lambda b, s: (0, 0)),
                   pl.BlockSpec((ATT_HEADS, BLOCK), lambda b, s: (0, 0)),
                   pl.BlockSpec((8, BRANCH_WIDTH), lambda b, s: (0, 0))],
        out_shape=[jax.ShapeDtypeStruct((rows, ABC_WIDTH), BF16),
                   jax.ShapeDtypeStruct((N_BUCKETS, BLOCK), F32),
                   jax.ShapeDtypeStruct((ATT_HEADS, BLOCK), F32),
                   jax.ShapeDtypeStruct((8, BRANCH_WIDTH), F32)],
        scratch_shapes=[pltpu.VMEM((ATT_HEADS, BLOCK, 2 * BLOCK), F32),
                        pltpu.VMEM((ATT_HEADS, BLOCK, 2 * BLOCK), F32),
                        pltpu.VMEM((BLOCK, 2 * BLOCK), F32),
                        pltpu.VMEM((RET_HEADS, BLOCK, BLOCK), F32),
                        pltpu.VMEM((BLOCK, BRANCH_WIDTH), F32)],
        compiler_params=_cparams("arbitrary", "arbitrary"),
    )(proj, proj, proj, d_br, states, cosf, sinf, bkt, rel_bias, sinks, conv_w)


MERGE_TILE = 256


def _merge_forward(br_ref, m_ref, wb_ref, wo_ref):
    bo, gates = [], []
    mixed_pre = None
    for g in range(N_BRANCH):
        br_g = br_ref[:, BRANCH_WIDTH * g:BRANCH_WIDTH * (g + 1)]
        bo_g = jnp.concatenate([_nn(br_g, wb_ref[p, g]) for p in range(N_CHIPS)], axis=1)
        gate_g = jax.nn.sigmoid(m_ref[:, D_MODEL * g:D_MODEL * (g + 1)])
        bo.append(bo_g)
        gates.append(gate_g)
        mixed_pre = gate_g * bo_g if mixed_pre is None else mixed_pre + gate_g * bo_g
    mixed = _nn(mixed_pre.astype(BF16), wo_ref[...])
    r = lax.rsqrt(jnp.mean(mixed * mixed, axis=-1, keepdims=True) + RMS_EPS)
    return bo, gates, mixed_pre, mixed, r


def merge_fwd(x2d, br, pm, wb, wo, g_post):
    t = x2d.shape[0]
    tm = MERGE_TILE if t % MERGE_TILE == 0 else BLOCK

    def body(x_ref, br_ref, m_ref, wb_ref, wo_ref, g_ref, o_ref):
        _, _, _, mixed, r = _merge_forward(br_ref, m_ref, wb_ref, wo_ref)
        o_ref[...] = x_ref[...] + mixed * r * g_ref[...]

    return pl.pallas_call(
        body, name="merge_fwd",
        grid=(t // tm,),
        in_specs=[pl.BlockSpec((tm, D_MODEL), lambda i: (i, 0)),
                  pl.BlockSpec((tm, N_BRANCH * BRANCH_WIDTH), lambda i: (i, 0)),
                  pl.BlockSpec((tm, MERGE_WIDTH), lambda i: (i, 0)),
                  pl.BlockSpec((N_CHIPS, N_BRANCH, BRANCH_WIDTH, SHARD_D), lambda i: (0, 0, 0, 0)),
                  pl.BlockSpec((D_MODEL, D_MODEL), lambda i: (0, 0)),
                  pl.BlockSpec((1, D_MODEL), lambda i: (0, 0))],
        out_specs=pl.BlockSpec((tm, D_MODEL), lambda i: (i, 0)),
        out_shape=jax.ShapeDtypeStruct((t, D_MODEL), F32),
        compiler_params=_cparams("parallel"),
    )(x2d, br, pm, wb, wo, g_post)


def merge_bwd(d_out, br, pm, wb, wo, g_post, layer, acc):
    t = d_out.shape[0]
    tm = MERGE_TILE if t % MERGE_TILE == 0 else BLOCK

    def body(*refs):
        do_ref, br_ref, m_ref, wb_ref, wo_ref, g_ref = refs[:6]
        dbr_ref, dm_ref, dg_ref, dwb_ref, dwo_ref = refs[-5:]

        @pl.when(pl.program_id(0) == 0)
        def _():
            dwb_ref[...] = jnp.zeros_like(dwb_ref)
            dwo_ref[...] = jnp.zeros_like(dwo_ref)
            dg_ref[...] = jnp.zeros_like(dg_ref)

        bo, gates, mixed_pre, mixed, r = _merge_forward(br_ref, m_ref, wb_ref, wo_ref)
        d_o = do_ref[...]
        nh = mixed * r
        dg_ref[0:1, :] += jnp.sum(d_o * nh, axis=0, keepdims=True)
        dn = d_o * g_ref[...]
        d_mixed = (r * (dn - nh * jnp.mean(dn * nh, axis=-1, keepdims=True))).astype(BF16)
        dwo_ref[...] += _tn(mixed_pre.astype(BF16), d_mixed)
        d_pre = _nt(d_mixed, wo_ref[...])
        for g in range(N_BRANCH):
            br_g = br_ref[:, BRANCH_WIDTH * g:BRANCH_WIDTH * (g + 1)]
            d_bo = (d_pre * gates[g]).astype(BF16)
            dm_ref[:, D_MODEL * g:D_MODEL * (g + 1)] = (
                d_pre * bo[g] * gates[g] * (1.0 - gates[g])).astype(BF16)
            d_br_g = None
            for p in range(N_CHIPS):
                d_bo_p = d_bo[:, SHARD_D * p:SHARD_D * (p + 1)]
                part = _nt(d_bo_p, wb_ref[p, g])
                d_br_g = part if d_br_g is None else d_br_g + part
                dwb_ref[p, g] += _tn(br_g, d_bo_p)
            dbr_ref[:, BRANCH_WIDTH * g:BRANCH_WIDTH * (g + 1)] = d_br_g

    ins = [d_out, br, pm, wb, wo, g_post]
    in_specs = [pl.BlockSpec((tm, D_MODEL), lambda i: (i, 0)),
                pl.BlockSpec((tm, N_BRANCH * BRANCH_WIDTH), lambda i: (i, 0)),
                pl.BlockSpec((tm, MERGE_WIDTH), lambda i: (i, 0)),
                pl.BlockSpec((N_CHIPS, N_BRANCH, BRANCH_WIDTH, SHARD_D), lambda i: (0, 0, 0, 0)),
                pl.BlockSpec((D_MODEL, D_MODEL), lambda i: (0, 0)),
                pl.BlockSpec((1, D_MODEL), lambda i: (0, 0))]
    aliases = {}
    if acc is not None:
        ins += list(acc)
        in_specs += [ANY, ANY]
        aliases = {6: 3, 7: 4}
    return pl.pallas_call(
        body, name="merge_bwd",
        grid=(t // tm,),
        in_specs=in_specs,
        out_specs=[pl.BlockSpec((tm, N_BRANCH * BRANCH_WIDTH), lambda i: (i, 0)),
                   pl.BlockSpec((tm, MERGE_WIDTH), lambda i: (i, 0)),
                   pl.BlockSpec((8, D_MODEL), lambda i: (0, 0)),
                   pl.BlockSpec((None, N_CHIPS, N_BRANCH, BRANCH_WIDTH, SHARD_D), lambda i: (layer, 0, 0, 0, 0)),
                   pl.BlockSpec((None, D_MODEL, D_MODEL), lambda i: (layer, 0, 0))],
        out_shape=[jax.ShapeDtypeStruct((t, N_BRANCH * BRANCH_WIDTH), F32),
                   jax.ShapeDtypeStruct((t, MERGE_WIDTH), BF16),
                   jax.ShapeDtypeStruct((8, D_MODEL), F32),
                   jax.ShapeDtypeStruct((DEPTH, N_CHIPS, N_BRANCH, BRANCH_WIDTH, SHARD_D), F32),
                   jax.ShapeDtypeStruct((DEPTH, D_MODEL, D_MODEL), F32)],
        input_output_aliases=aliases,
        compiler_params=_cparams("arbitrary"),
    )(*ins)


def loss_head(xf, target2d, nb, nc):
    def body(x_ref, t_ref, l_ref, dx_ref):
        b = pl.program_id(0)
        n = pl.program_id(1)

        @pl.when((b == 0) & (n == 0))
        def _():
            l_ref[...] = jnp.zeros_like(l_ref)

        @pl.when(n == 0)
        def _():
            dx_ref[...] = jnp.zeros_like(dx_ref)

        @pl.when(n > 0)
        def _():
            e = x_ref[...] - t_ref[...]
            dx_ref[...] = e * (1.0 / D_MODEL)
            s = jnp.sum(jnp.sum(e * e, axis=1, keepdims=True), axis=0, keepdims=True)
            l_ref[...] += jnp.broadcast_to(s * (0.5 / D_MODEL), l_ref.shape)

    return pl.pallas_call(
        body, name="loss_head",
        grid=(nb, nc),
        in_specs=[pl.BlockSpec((BLOCK, D_MODEL), lambda b, n: (b * nc + n, 0)),
                  pl.BlockSpec((BLOCK, D_MODEL), lambda b, n: (b * (nc - 1) + jnp.maximum(n - 1, 0), 0))],
        out_specs=[pl.BlockSpec((8, BLOCK), lambda b, n: (0, 0)),
                   pl.BlockSpec((BLOCK, D_MODEL), lambda b, n: (b * nc + n, 0))],
        out_shape=[jax.ShapeDtypeStruct((8, BLOCK), F32),
                   jax.ShapeDtypeStruct(xf.shape, F32)],
        compiler_params=_cparams("arbitrary", "arbitrary"),
    )(xf, target2d)


N_ABC_TILES = ABC_WIDTH // COL_TILE
N_M_TILES = MERGE_WIDTH // COL_TILE


def proj_dgrad(d_abc, d_m, w, x2d, g, d_out):
    t = x2d.shape[0]
    tm = ROW_TILE if t % ROW_TILE == 0 else BLOCK
    nk = N_ABC_TILES + N_M_TILES

    def body(da_ref, dm_ref, w_ref, x_ref, g_ref, do_ref, dx_ref, dg_ref, acc):
        i = pl.program_id(0)
        k = pl.program_id(1)

        @pl.when((i == 0) & (k == 0))
        def _():
            dg_ref[...] = jnp.zeros_like(dg_ref)

        @pl.when(k == 0)
        def _():
            acc[...] = jnp.zeros_like(acc)

        @pl.when(k < N_ABC_TILES)
        def _():
            acc[...] += _nn(da_ref[...], w_ref[...])

        @pl.when(k >= N_ABC_TILES)
        def _():
            acc[...] += _nn(dm_ref[...], w_ref[...])

        @pl.when(k == nk - 1)
        def _():
            x = x_ref[...]
            r = lax.rsqrt(jnp.mean(x * x, axis=-1, keepdims=True) + RMS_EPS)
            nh = x * r
            dh = acc[...]
            dg_ref[0:1, :] += jnp.sum(dh * nh, axis=0, keepdims=True)
            dn = dh * g_ref[...]
            dx_ref[...] = do_ref[...] + r * (dn - nh * jnp.mean(dn * nh, axis=-1, keepdims=True))

    return pl.pallas_call(
        body, name="proj_dgrad",
        grid=(t // tm, nk),
        in_specs=[pl.BlockSpec((tm, COL_TILE), lambda i, k: (i, jnp.minimum(k, N_ABC_TILES - 1))),
                  pl.BlockSpec((tm, COL_TILE), lambda i, k: (i, jnp.maximum(k - N_ABC_TILES, 0))),
                  pl.BlockSpec((COL_TILE, D_MODEL), lambda i, k: (k, 0)),
                  pl.BlockSpec((tm, D_MODEL), lambda i, k: (i, 0)),
                  pl.BlockSpec((1, D_MODEL), lambda i, k: (0, 0)),
                  pl.BlockSpec((tm, D_MODEL), lambda i, k: (i, 0))],
        out_specs=[pl.BlockSpec((tm, D_MODEL), lambda i, k: (i, 0)),
                   pl.BlockSpec((8, D_MODEL), lambda i, k: (0, 0))],
        out_shape=[jax.ShapeDtypeStruct((t, D_MODEL), F32),
                   jax.ShapeDtypeStruct((8, D_MODEL), F32)],
        scratch_shapes=[pltpu.VMEM((tm, D_MODEL), F32)],
        compiler_params=_cparams("arbitrary", "arbitrary"),
    )(d_abc, d_m, w, x2d, g, d_out)


def proj_wgrad(hb, d_abc, d_m, layer, acc):
    t = hb.shape[0]
    tk = ROW_TILE if t % ROW_TILE == 0 else BLOCK
    nj = N_ABC_TILES + N_M_TILES

    def body(*refs):
        h_ref, da_ref, dm_ref = refs[:3]
        o_ref = refs[-1]
        j = pl.program_id(0)

        @pl.when(pl.program_id(1) == 0)
        def _():
            o_ref[...] = jnp.zeros_like(o_ref)

        @pl.when(j < N_ABC_TILES)
        def _():
            o_ref[...] += _tn(da_ref[...], h_ref[...])

        @pl.when(j >= N_ABC_TILES)
        def _():
            o_ref[...] += _tn(dm_ref[...], h_ref[...])

    ins = [hb, d_abc, d_m]
    in_specs = [pl.BlockSpec((tk, D_MODEL), lambda j, k: (k, 0)),
                pl.BlockSpec((tk, COL_TILE), lambda j, k: (k, jnp.minimum(j, N_ABC_TILES - 1))),
                pl.BlockSpec((tk, COL_TILE), lambda j, k: (k, jnp.maximum(j - N_ABC_TILES, 0)))]
    aliases = {}
    if acc is not None:
        ins.append(acc)
        in_specs.append(ANY)
        aliases = {3: 0}
    return pl.pallas_call(
        body, name="proj_wgrad",
        grid=(nj, t // tk),
        in_specs=in_specs,
        out_specs=pl.BlockSpec((None, COL_TILE, D_MODEL), lambda j, k: (layer, j, 0)),
        out_shape=jax.ShapeDtypeStruct((DEPTH, PROJ_WIDTH, D_MODEL), F32),
        input_output_aliases=aliases,
        compiler_params=_cparams("parallel", "arbitrary"),
    )(*ins)


def _adamw_math(w, g, m, v):
    m = ADAM_B1 * m + (1.0 - ADAM_B1) * g
    v = ADAM_B2 * v + (1.0 - ADAM_B2) * jnp.square(g)
    m_hat = m / (1.0 - ADAM_B1 ** ADAM_STEP)
    v_hat = v / (1.0 - ADAM_B2 ** ADAM_STEP)
    delta = -ADAM_LR * (m_hat / (jnp.sqrt(v_hat) + ADAM_EPS) + ADAM_WD * w)
    return delta, m, v


def adamw_big(w, g, m, v):
    r, c = w.shape
    tr = 256 if r % 256 == 0 else r

    def body(w_ref, g_ref, m_ref, v_ref, d_ref, mo_ref, vo_ref):
        d, m_new, v_new = _adamw_math(w_ref[...], g_ref[...], m_ref[...], v_ref[...])
        d_ref[...] = d
        mo_ref[...] = m_new
        vo_ref[...] = v_new

    spec = pl.BlockSpec((tr, c), lambda i: (i, 0))
    sds = jax.ShapeDtypeStruct((r, c), F32)
    return pl.pallas_call(
        body, name="adamw_big",
        grid=(r // tr,),
        in_specs=[spec] * 4, out_specs=[spec] * 3, out_shape=[sds] * 3,
        compiler_params=_cparams("parallel"),
    )(w, g, m, v)


def adamw_small(params):
    k = len(params)

    def body(*refs):
        ins, outs = refs[:4 * k], refs[4 * k:]
        for i in range(k):
            d, m_new, v_new = _adamw_math(*[r[...] for r in ins[4 * i:4 * i + 4]])
            outs[3 * i][...] = d
            outs[3 * i + 1][...] = m_new
            outs[3 * i + 2][...] = v_new

    flat = [a for p in params for a in p]
    vm = pl.BlockSpec(memory_space=pltpu.VMEM)
    out_shape = [jax.ShapeDtypeStruct(p[0].shape, F32) for p in params for _ in range(3)]
    res = pl.pallas_call(
        body, name="adamw_small",
        in_specs=[vm] * len(flat), out_specs=[vm] * len(out_shape), out_shape=out_shape,
    )(*flat)
    return [tuple(res[3 * i:3 * i + 3]) for i in range(k)]


ANY = pl.BlockSpec(memory_space=pl.ANY)


def _place():
    return lax.axis_index("x"), lax.axis_index("y"), lax.axis_index("c")


def _own_slot(shard, chip):
    buf = lax.empty((DEPTH, N_CHIPS) + shard.shape[1:], shard.dtype)
    return lax.dynamic_update_slice(buf, shard[:, None], (0, chip) + (0,) * (shard.ndim - 1))


def gather_weight_shards(bufs):
    n = len(bufs)

    def body(*refs):
        g_refs = refs[n:2 * n]
        send_sems, recv_sems = refs[2 * n:]
        x, y, c = _place()
        me_p = 2 * x + y
        sibling = (x, y, 1 - c)
        chips = [(1 - x, y), (x, 1 - y), (1 - x, 1 - y)]

        def copy(k, slab, to):
            return pltpu.make_async_remote_copy(src_ref=slab, dst_ref=slab, send_sem=send_sems.at[k],
                                                recv_sem=recv_sems.at[k], device_id=to, device_id_type=MESH)

        first, passed = [], []
        for t in range(n):
            for k, (qx, qy) in enumerate(chips):
                cp = copy(6 * t + k, g_refs[t].at[c, me_p], (qx, qy, c))
                cp.start()
                first.append(cp)
        for t in range(n):
            for k, (qx, qy) in enumerate(chips):
                slab = g_refs[t].at[c, 2 * qx + qy]
                copy(6 * t + k, slab, (qx, qy, c)).wait_recv()
                fwd = copy(6 * t + 3 + k, slab, sibling)
                fwd.start()
                passed.append(fwd)
        for t in range(n):
            for k, (qx, qy) in enumerate(chips):
                copy(6 * t + 3 + k, g_refs[t].at[1 - c, 2 * qx + qy], sibling).wait_recv()
        for cp in first + passed:
            cp.wait_send()

    return pl.pallas_call(
        body, name="gather_weight_shards",
        in_specs=[ANY] * n, out_specs=[ANY] * n,
        out_shape=[jax.ShapeDtypeStruct(b.shape, b.dtype) for b in bufs],
        input_output_aliases={t: t for t in range(n)},
        scratch_shapes=[pltpu.SemaphoreType.DMA((6 * n,)), pltpu.SemaphoreType.DMA((6 * n,))],
    )(*bufs)


def exchange_small(pack):
    def body(p_ref, o_ref, send_sems, recv_sems, local_sem):
        x, y, c = _place()
        me = 4 * x + 2 * y + c
        mine = pltpu.make_async_copy(p_ref, o_ref.at[me], local_sem)
        mine.start()
        sends = []
        for k in range(1, 8):
            fx, fy, fc = (k >> 2) & 1, (k >> 1) & 1, k & 1
            peer = (x ^ fx, y ^ fy, c ^ fc)
            cp = pltpu.make_async_remote_copy(src_ref=p_ref, dst_ref=o_ref.at[me], send_sem=send_sems.at[k - 1],
                                              recv_sem=recv_sems.at[k - 1], device_id=peer, device_id_type=MESH)
            cp.start()
            sends.append(cp)
        for k in range(1, 8):
            fx, fy, fc = (k >> 2) & 1, (k >> 1) & 1, k & 1
            peer = (x ^ fx, y ^ fy, c ^ fc)
            slot = o_ref.at[4 * peer[0] + 2 * peer[1] + peer[2]]
            pltpu.make_async_remote_copy(src_ref=slot, dst_ref=slot, send_sem=send_sems.at[k - 1],
                                         recv_sem=recv_sems.at[k - 1], device_id=peer, device_id_type=MESH).wait_recv()
        for cp in sends:
            cp.wait_send()
        mine.wait()

    return pl.pallas_call(
        body, name="exchange_small",
        in_specs=[ANY], out_specs=ANY,
        out_shape=jax.ShapeDtypeStruct((8,) + pack.shape, pack.dtype),
        scratch_shapes=[pltpu.SemaphoreType.DMA((7,)), pltpu.SemaphoreType.DMA((7,)), pltpu.SemaphoreType.DMA],
    )(pack)


def sibling_swap_layers(grads):
    n = len(grads)

    def body(*refs):
        g_refs, o_refs = refs[:n], refs[n:2 * n]
        send_sems, recv_sems = refs[2 * n:]
        x, y, c = _place()
        cps = []
        for t in range(n):
            cp = pltpu.make_async_remote_copy(src_ref=g_refs[t].at[1 - c], dst_ref=o_refs[t],
                                              send_sem=send_sems.at[t], recv_sem=recv_sems.at[t],
                                              device_id=(x, y, 1 - c), device_id_type=MESH)
            cp.start()
            cps.append(cp)
        for cp in cps:
            cp.wait()

    return pl.pallas_call(
        body, name="sibling_swap_layers",
        in_specs=[ANY] * n, out_specs=[ANY] * n,
        out_shape=[jax.ShapeDtypeStruct(g.shape[1:], g.dtype) for g in grads],
        scratch_shapes=[pltpu.SemaphoreType.DMA((n,)), pltpu.SemaphoreType.DMA((n,))],
    )(*grads)


def _row_tile(r):
    return max(t for t in range(16, 513, 16) if r % t == 0)


def add_own_layer(g, other, c_arr):
    _, _, r, cols = g.shape
    tr = _row_tile(r)

    def body(c_ref, a_ref, b_ref, o_ref):
        o_ref[...] = (a_ref[...] + b_ref[...]).astype(BF16)

    return pl.pallas_call(
        body, name="add_own_layer",
        grid_spec=pltpu.PrefetchScalarGridSpec(
            num_scalar_prefetch=1, grid=(N_CHIPS, r // tr),
            in_specs=[pl.BlockSpec((None, None, tr, cols), lambda p, i, c_ref: (c_ref[0], p, i, 0)),
                      pl.BlockSpec((None, tr, cols), lambda p, i, c_ref: (p, i, 0))],
            out_specs=pl.BlockSpec((None, tr, cols), lambda p, i, c_ref: (p, i, 0))),
        out_shape=jax.ShapeDtypeStruct((N_CHIPS, r, cols), BF16),
        compiler_params=_cparams("parallel", "parallel"),
    )(c_arr, g, other)


def scatter_to_chips(partials):
    n = len(partials)

    def body(*refs):
        s_refs, o_refs = refs[:n], refs[n:2 * n]
        send_sems, recv_sems = refs[2 * n:]
        x, y, c = _place()
        chips = [(1 - x, y), (x, 1 - y), (1 - x, 1 - y)]
        cps = []
        for t in range(n):
            for k, (qx, qy) in enumerate(chips):
                cp = pltpu.make_async_remote_copy(src_ref=s_refs[t].at[2 * qx + qy], dst_ref=o_refs[t].at[k],
                                                  send_sem=send_sems.at[3 * t + k], recv_sem=recv_sems.at[3 * t + k],
                                                  device_id=(qx, qy, c), device_id_type=MESH)
                cp.start()
                cps.append(cp)
        for cp in cps:
            cp.wait()

    return pl.pallas_call(
        body, name="scatter_to_chips",
        in_specs=[ANY] * n, out_specs=[ANY] * n,
        out_shape=[jax.ShapeDtypeStruct((3,) + s.shape[1:], s.dtype) for s in partials],
        scratch_shapes=[pltpu.SemaphoreType.DMA((3 * n,)), pltpu.SemaphoreType.DMA((3 * n,))],
    )(*partials)


def sum_chips(own, parts, where):
    _, r, cols = own.shape
    tr = _row_tile(r)

    def body(w_ref, a_ref, p_ref, o_ref):
        acc = a_ref[...].astype(F32)
        for k in range(3):
            acc = acc + p_ref[k].astype(F32)
        o_ref[...] = acc

    return pl.pallas_call(
        body, name="sum_chips",
        grid_spec=pltpu.PrefetchScalarGridSpec(
            num_scalar_prefetch=1, grid=(r // tr,),
            in_specs=[pl.BlockSpec((None, tr, cols), lambda i, w_ref: (w_ref[0], i, 0)),
                      pl.BlockSpec((3, tr, cols), lambda i, w_ref: (0, i, 0))],
            out_specs=pl.BlockSpec((None, tr, cols), lambda i, w_ref: (w_ref[1], i, 0))),
        out_shape=jax.ShapeDtypeStruct((DEPTH, r, cols), F32),
        compiler_params=_cparams("parallel"),
    )(where, own, parts)


def sibling_share_layer(bufs):
    n = len(bufs)

    def body(*refs):
        o_refs = refs[n:2 * n]
        send_sems, recv_sems = refs[2 * n:]
        x, y, c = _place()
        cps = []
        for t in range(n):
            cp = pltpu.make_async_remote_copy(src_ref=o_refs[t].at[c], dst_ref=o_refs[t].at[c], send_sem=send_sems.at[t],
                                              recv_sem=recv_sems.at[t], device_id=(x, y, 1 - c), device_id_type=MESH)
            cp.start()
            cps.append(cp)
        for t in range(n):
            slot = o_refs[t].at[1 - c]
            pltpu.make_async_remote_copy(src_ref=slot, dst_ref=slot, send_sem=send_sems.at[t], recv_sem=recv_sems.at[t],
                                         device_id=(x, y, 1 - c), device_id_type=MESH).wait_recv()
        for cp in cps:
            cp.wait_send()

    return pl.pallas_call(
        body, name="sibling_share_layer",
        in_specs=[ANY] * n, out_specs=[ANY] * n,
        out_shape=[jax.ShapeDtypeStruct(b.shape, b.dtype) for b in bufs],
        input_output_aliases={t: t for t in range(n)},
        scratch_shapes=[pltpu.SemaphoreType.DMA((n,)), pltpu.SemaphoreType.DMA((n,))],
    )(*bufs)


SP_META = 2 * (N_META * D_MODEL // LANES)
SP_NORM = DEPTH * D_MODEL // LANES
SP_RB = DEPTH * N_BUCKETS
SP_SINK = DEPTH * ATT_HEADS
SP_CONV = DEPTH * 3 * BRANCH_WIDTH // LANES
SP_LOSS = 8
SP_ROWS = SP_META + 2 * SP_NORM + SP_RB + SP_SINK + SP_CONV + SP_LOSS


def sum_small(slots):
    half = SP_META // 2
    rb0 = SP_META + 2 * SP_NORM
    rest_rows = SP_ROWS - SP_META

    def body(s_ref, meta_ref, rest_ref):
        acc = s_ref[0]
        for d in range(1, 8):
            acc = acc + s_ref[d]
        meta_ref[...] = acc[0:half] + acc[half:SP_META]
        rest_ref[...] = acc[SP_META:]
        rest_ref[rb0 - SP_META:rb0 - SP_META + N_BUCKETS, :] = (
            acc[rb0:rb0 + N_BUCKETS] + acc[rb0 + N_BUCKETS:rb0 + 2 * N_BUCKETS])

    vm = pl.BlockSpec(memory_space=pltpu.VMEM)
    return pl.pallas_call(
        body, name="sum_small",
        in_specs=[vm], out_specs=[vm, vm],
        out_shape=[jax.ShapeDtypeStruct((half, LANES), F32), jax.ShapeDtypeStruct((rest_rows, LANES), F32)],
    )(slots)


def local_step(x, loss_target, meta_full, rel_bias, norm_pre, conv_w_full, attn_sinks, norm_post, weights):
    nb, seq, _ = x.shape
    nc = seq // BLOCK + 1
    lp = nc * BLOCK
    rows = nb * lp
    pad = jnp.zeros((nb, PAD_FRONT, D_MODEL), F32)
    meta = jnp.broadcast_to(meta_full[None], (nb, N_META, D_MODEL))
    h0 = jnp.concatenate([pad, meta, x], axis=1).reshape(rows, D_MODEL)
    cosf, sinf = _rot_tables(lp)
    bkt = jnp.asarray(_bucket_table())

    acts = []
    h = h0
    for l in range(DEPTH):
        w_in, w_br, w_out = weights[l]
        g_pre = norm_pre[l][None]
        hb, p_abc = norm_matmul(h, g_pre, w_in, 0, N_ABC_TILES)
        p_m = matmul_cols(hb, w_in, N_ABC_TILES, N_M_TILES)
        br, states = mixers_fwd(p_abc, cosf, sinf, bkt, rel_bias, attn_sinks[l][None], conv_w_full[l], nb, nc)
        h_next = merge_fwd(h, br, p_m, w_br, w_out, norm_post[l][None])
        acts.append((h, hb, p_abc, p_m, br, states))
        h = h_next

    loss_part, d_h = loss_head(h, loss_target.reshape(nb * seq, D_MODEL), nb, nc)

    small = [None] * DEPTH
    g_win = None
    g_wbo = None
    for l in reversed(range(DEPTH)):
        w_in, w_br, w_out = weights[l]
        h_in, hb, p_abc, p_m, br, states = acts[l]
        d_br, d_m, d_gpost, g_wbr, g_wout = merge_bwd(d_h, br, p_m, w_br, w_out, norm_post[l][None], l, g_wbo)
        g_wbo = (g_wbr, g_wout)
        d_abc, d_rb, d_sk, d_cw = mixers_bwd(p_abc, d_br, states, cosf, sinf, bkt, rel_bias,
                                             attn_sinks[l][None], conv_w_full[l], nb, nc)
        d_h, d_gpre = proj_dgrad(d_abc, d_m, w_in, h_in, norm_pre[l][None], d_h)
        g_win = proj_wgrad(hb, d_abc, d_m, l, g_win)
        small[l] = (d_gpre[0], d_gpost[0], d_rb, d_sk, d_cw[0:3])

    d_h3 = d_h.reshape(nb, lp, D_MODEL)
    d_x = d_h3[:, BLOCK:]
    d_meta = d_h3[:, PAD_FRONT:BLOCK]
    sp = jnp.concatenate([
        d_meta.reshape(-1, LANES),
        jnp.stack([small[l][0] for l in range(DEPTH)]).reshape(-1, LANES),
        jnp.stack([small[l][1] for l in range(DEPTH)]).reshape(-1, LANES),
        jnp.concatenate([small[l][2] for l in range(DEPTH)], axis=0),
        jnp.concatenate([small[l][3] for l in range(DEPTH)], axis=0),
        jnp.stack([small[l][4] for l in range(DEPTH)]).reshape(-1, LANES),
        loss_part], axis=0)
    return d_x, g_win, g_wbo[0], g_wbo[1], sp


def kernel(x, meta_tokens, rel_bias, norm_pre, w_in, conv_w, attn_sinks, w_branch, w_out, norm_post, loss_target, m_meta_tokens, m_rel_bias, m_norm_pre, m_w_in, m_conv_w, m_attn_sinks, m_w_branch, m_w_out, m_norm_post, v_meta_tokens, v_rel_bias, v_norm_pre, v_w_in, v_conv_w, v_attn_sinks, v_w_branch, v_w_out, v_norm_post):
    assert x.shape[0] == 2 and SP_META == 2 * N_META * D_MODEL // LANES
    px, py, pc = _place()
    chip = 2 * px + py

    shards = [jnp.swapaxes(w_in, 1, 2).astype(BF16), w_branch.astype(BF16), w_out.astype(BF16)]
    a_in, a_br, a_out = gather_weight_shards([_own_slot(s, chip) for s in shards])
    a_in = a_in.reshape(DEPTH, PROJ_WIDTH, D_MODEL)
    a_out = a_out.reshape(DEPTH, D_MODEL, D_MODEL)
    weights = [(a_in[l], a_br[l], a_out[l]) for l in range(DEPTH)]
    side =jnp.concatenate([meta_tokens.reshape(-1), conv_w.reshape(-1)]).reshape(-1, LANES)
    side = jnp.concatenate([side, jnp.zeros((40 - side.shape[0], LANES), F32)], axis=0)
    side_all = exchange_small(side)
    side_chips = side_all[0::2]
    n_meta_rows = N_META * SHARD_D // LANES
    meta_full = jnp.moveaxis(side_chips[:, :n_meta_rows].reshape(N_CHIPS, N_META, SHARD_D), 0, 1).reshape(N_META, D_MODEL)
    conv_full = jnp.moveaxis(side_chips[:, n_meta_rows:n_meta_rows + 6].reshape(N_CHIPS, DEPTH, 3, LANES), 0, 2).reshape(DEPTH, 3, BRANCH_WIDTH)

    d_x, g_win, g_wbr, g_wout, sp = local_step(x, loss_target, meta_full, rel_bias, norm_pre, conv_full, attn_sinks,
                                               norm_post, weights)

    full = [g_win.reshape(DEPTH, N_CHIPS, SHARD_IN, D_MODEL),
            g_wbr.reshape(DEPTH, N_CHIPS, N_BRANCH * BRANCH_WIDTH, SHARD_D),
            g_wout.reshape(DEPTH, N_CHIPS, SHARD_D, D_MODEL)]
    others = sibling_swap_layers(full)
    c_arr = jnp.reshape(pc, (1,)).astype(jnp.int32)
    partials = [add_own_layer(g, o, c_arr) for g, o in zip(full, others)]
    parts = scatter_to_chips(partials)
    where = jnp.stack([chip, pc]).astype(jnp.int32)
    r_in, r_br, r_out = sibling_share_layer([sum_chips(a, p, where) for a, p in zip(partials, parts)])
    g_in = jnp.swapaxes(r_in, 1, 2)
    g_br = r_br.reshape(w_branch.shape)
    g_out = r_out

    meta_rows, rest = sum_small(exchange_small(sp))
    o = 0
    g_meta_full = meta_rows.reshape(N_META, D_MODEL)
    g_norm_pre = rest[o:o + SP_NORM].reshape(DEPTH, D_MODEL); o += SP_NORM
    g_norm_post = rest[o:o + SP_NORM].reshape(DEPTH, D_MODEL); o += SP_NORM
    g_rel_bias = rest[o:o + N_BUCKETS, :ATT_HEADS]; o += SP_RB
    g_sinks = rest[o:o + SP_SINK, 0].reshape(DEPTH, ATT_HEADS); o += SP_SINK
    g_conv_full = rest[o:o + SP_CONV].reshape(DEPTH, 3, BRANCH_WIDTH); o += SP_CONV
    loss = rest[o, 0]
    g_meta = lax.dynamic_slice_in_dim(g_meta_full, chip * SHARD_D, SHARD_D, axis=1)
    g_conv = lax.dynamic_slice_in_dim(g_conv_full, chip * LANES, LANES, axis=2)

    def big(w, g, m, v):
        shp = w.shape
        to2 = lambda a: a.reshape(-1, shp[-1])
        return [r.reshape(shp) for r in adamw_big(to2(w), to2(g), to2(m), to2(v))]

    u_in = big(w_in, g_in, m_w_in, v_w_in)
    u_br = big(w_branch, g_br, m_w_branch, v_w_branch)
    u_out = big(w_out, g_out, m_w_out, v_w_out)
    to2 = lambda a: a.reshape(-1, a.shape[-1])
    smalls = [(meta_tokens, g_meta, m_meta_tokens, v_meta_tokens),
              (rel_bias, g_rel_bias, m_rel_bias, v_rel_bias),
              (norm_pre, g_norm_pre, m_norm_pre, v_norm_pre),
              (to2(conv_w), to2(g_conv), to2(m_conv_w), to2(v_conv_w)),
              (attn_sinks, g_sinks, m_attn_sinks, v_attn_sinks),
              (norm_post, g_norm_post, m_norm_post, v_norm_post)]
    u_meta, u_rb, u_npre, u_conv, u_sink, u_npost = adamw_small(smalls)
    u_conv = tuple(a.reshape(conv_w.shape) for a in u_conv)

    grads = [g_meta, g_rel_bias, g_norm_pre, g_in, g_conv, g_sinks, g_br, g_out, g_norm_post]
    upd = [u_meta, u_rb, u_npre, u_in, u_conv, u_sink, u_br, u_out, u_npost]
    return (loss, d_x, *grads, *[u[0] for u in upd], *[u[1] for u in upd], *[u[2] for u in upd])
```

```python
import functools
import math

import numpy as np
import jax
import jax.numpy as jnp
from jax import lax
from jax.experimental import pallas as pl
from jax.experimental.pallas import tpu as pltpu

F32 = jnp.float32
BF16 = jnp.bfloat16
MESH = pl.DeviceIdType.MESH

D_MODEL = 1024
DEPTH = 2
N_META = 16
BLOCK = 128
PAD_FRONT = BLOCK - N_META
ATT_HEADS = 8
ATT_HEAD_DIM = 64
N_BUCKETS = 32
MAX_EXACT = 16
MAX_DISTANCE = 128
RET_HEADS = 4
ROT_BASE = 10000.0
N_BRANCH = 3
BRANCH_WIDTH = 512
PROJ_WIDTH = 8448
ABC_WIDTH = 5376
MERGE_WIDTH = N_BRANCH * D_MODEL
RMS_EPS = 1e-6
GN_EPS = 1e-6
NEG_INF = -1e30
ATT_SCALE = ATT_HEAD_DIM ** -0.5
RET_SCALE = BLOCK ** -0.5
LOG_GAMMA = tuple(math.log1p(-(2.0 ** (-5.0 - h))) for h in range(RET_HEADS))

C_AQ, C_AK, C_AV, C_AG = 0, 512, 640, 768
C_RQ, C_RK, C_RV, C_RG = 1280, 1792, 2304, 2816
C_CB, C_CC, C_CX, C_CG = 3328, 3840, 4352, 4864

ADAM_LR = 0.001
ADAM_B1 = 0.9
ADAM_B2 = 0.999
ADAM_EPS = 1e-08
ADAM_WD = 0.01
ADAM_STEP = 10

N_CHIPS = 4
SHARD_IN = PROJ_WIDTH // N_CHIPS
SHARD_D = D_MODEL // N_CHIPS
LANES = 128
PACK_IN = D_MODEL * SHARD_IN
PACK_BR = N_BRANCH * BRANCH_WIDTH * SHARD_D
PACK_OUT = SHARD_D * D_MODEL
PACK_ROWS = (PACK_IN + PACK_BR + PACK_OUT) // LANES

VMEM_LIMIT = 56 * 1024 * 1024
COL_TILE = 768
ROW_TILE = 1088


def _cparams(*sem):
    return pltpu.CompilerParams(dimension_semantics=sem, vmem_limit_bytes=VMEM_LIMIT)


def _nt(a, b):
    return lax.dot_general(a, b, (((1,), (1,)), ((), ())), preferred_element_type=F32)


def _tn(a, b):
    return lax.dot_general(a, b, (((0,), (0,)), ((), ())), preferred_element_type=F32)


def _nn(a, b):
    return jnp.dot(a, b, preferred_element_type=F32)


def _silu(x):
    return x * jax.nn.sigmoid(x)


def _dsilu(x):
    s = jax.nn.sigmoid(x)
    return s * (1.0 + x * (1.0 - s))


def _bucket_table():
    r = np.arange(BLOCK)[:, None]
    c = np.arange(2 * BLOCK)[None, :]
    n = np.maximum(BLOCK + r - c, 0)
    nf = np.maximum(n, 1).astype(np.float32)
    large = MAX_EXACT + (np.log(nf / MAX_EXACT) / math.log(MAX_DISTANCE / MAX_EXACT)
                         * (N_BUCKETS - MAX_EXACT)).astype(np.int32)
    large = np.minimum(large, N_BUCKETS - 1)
    return np.where(n < MAX_EXACT, n, large).astype(np.int32)


def _rot_tables(lp):
    half = BLOCK // 2
    pos = (jnp.arange(lp) - PAD_FRONT).astype(F32)
    theta = 1.0 / (ROT_BASE ** jnp.linspace(0.0, 1.0, half, dtype=F32))
    ang = pos[:, None] * theta[None, :]
    cos, sin = jnp.cos(ang), jnp.sin(ang)
    return jnp.concatenate([cos, cos], axis=1), jnp.concatenate([-sin, sin], axis=1)


def norm_matmul(x2d, g, w, layer, col0_blocks, n_col_blocks):
    t = x2d.shape[0]
    tm = ROW_TILE if t % ROW_TILE == 0 else BLOCK

    def body(x_ref, g_ref, w_ref, hb_ref, o_ref):
        @pl.when(pl.program_id(1) == 0)
        def _():
            x = x_ref[...]
            r = lax.rsqrt(jnp.mean(x * x, axis=-1, keepdims=True) + RMS_EPS)
            hb_ref[...] = (x * r * g_ref[...]).astype(BF16)

        o_ref[...] = _nt(hb_ref[...], w_ref[...])

    return pl.pallas_call(
        body, name="norm_matmul",
        grid=(t // tm, n_col_blocks),
        in_specs=[pl.BlockSpec((tm, D_MODEL), lambda i, j: (i, 0)),
                  pl.BlockSpec((1, D_MODEL), lambda i, j: (0, 0)),
                  pl.BlockSpec((None, COL_TILE, D_MODEL), lambda i, j: (layer, j + col0_blocks, 0))],
        out_specs=[pl.BlockSpec((tm, D_MODEL), lambda i, j: (i, 0)),
                   pl.BlockSpec((tm, COL_TILE), lambda i, j: (i, j))],
        out_shape=[jax.ShapeDtypeStruct((t, D_MODEL), BF16),
                   jax.ShapeDtypeStruct((t, n_col_blocks * COL_TILE), F32)],
        compiler_params=_cparams("parallel", "arbitrary"),
    )(x2d, g, w)


def matmul_cols(a, w, layer, col0_blocks, n_col_blocks):
    t, k = a.shape
    tm = ROW_TILE if t % ROW_TILE == 0 else BLOCK

    def body(a_ref, w_ref, o_ref):
        o_ref[...] = _nt(a_ref[...], w_ref[...])

    return pl.pallas_call(
        body, name="matmul_cols",
        grid=(t // tm, n_col_blocks),
        in_specs=[pl.BlockSpec((tm, k), lambda i, j: (i, 0)),
                  pl.BlockSpec((None, COL_TILE, k), lambda i, j: (layer, j + col0_blocks, 0))],
        out_specs=pl.BlockSpec((tm, COL_TILE), lambda i, j: (i, j)),
        out_shape=jax.ShapeDtypeStruct((t, n_col_blocks * COL_TILE), F32),
        compiler_params=_cparams("parallel", "arbitrary"),
    )(a, w)


def _build_bias(bkt_ref, rb_ref, bias_s):
    bkt = bkt_ref[...]
    for h in range(ATT_HEADS):
        acc = jnp.zeros((BLOCK, 2 * BLOCK), F32)
        for b in range(N_BUCKETS):
            acc = jnp.where(bkt == b, rb_ref[b, h], acc)
        bias_s[h] = acc


def _band_mask(n):
    r = lax.broadcasted_iota(jnp.int32, (BLOCK, 2 * BLOCK), 0)
    c = lax.broadcasted_iota(jnp.int32, (BLOCK, 2 * BLOCK), 1)
    key_pos = (n - 1) * BLOCK + c
    return (c > r) & (c <= r + BLOCK) & (key_pos >= PAD_FRONT)


def _split_heads(kv, kh):
    lane = lax.broadcasted_iota(jnp.int32, kv.shape, 1)
    if kh == 0:
        lo = jnp.where(lane < ATT_HEAD_DIM, kv, 0.0)
        hi = pltpu.roll(lo, ATT_HEAD_DIM, 1)
    else:
        hi = jnp.where(lane >= ATT_HEAD_DIM, kv, 0.0)
        lo = pltpu.roll(hi, ATT_HEAD_DIM, 1)
    return lo, hi


def _merge_heads(acc_lo, acc_hi, kh):
    lane = lax.broadcasted_iota(jnp.int32, acc_lo.shape, 1)
    if kh == 0:
        return jnp.where(lane < ATT_HEAD_DIM, acc_lo + pltpu.roll(acc_hi, ATT_HEAD_DIM, 1), 0.0)
    return jnp.where(lane >= ATT_HEAD_DIM, acc_hi + pltpu.roll(acc_lo, ATT_HEAD_DIM, 1), 0.0)


def _softmax_sink(q2b, kxb, bias_h, mask, sink_h):
    s = _nt(q2b, kxb) * ATT_SCALE + bias_h
    s = jnp.where(mask, s, NEG_INF)
    m = jnp.maximum(jnp.max(s, axis=-1, keepdims=True), sink_h)
    p = jnp.exp(s - m)
    es = jnp.exp(sink_h - m)
    inv = 1.0 / (jnp.sum(p, axis=-1, keepdims=True) + es)
    return p * inv, es * inv


def _rot(t, cosf, sinf):
    return t * cosf + pltpu.roll(t, BLOCK // 2, 1) * sinf


def _rot_t(d, cosf, sinf):
    return d * cosf + pltpu.roll(d * sinf, BLOCK // 2, 1)


def _decay_tables(h):
    lg = LOG_GAMMA[h]
    i = lax.broadcasted_iota(jnp.int32, (BLOCK, BLOCK), 0)
    j = lax.broadcasted_iota(jnp.int32, (BLOCK, BLOCK), 1)
    diff = (i - j).astype(F32)
    dm = jnp.where(diff >= 0, jnp.exp(diff * lg), 0.0)
    row = lax.broadcasted_iota(jnp.int32, (BLOCK, 1), 0).astype(F32)
    zeta = jnp.exp((BLOCK - 1 - row) * lg)
    xi = jnp.exp((row + 1.0) * lg)
    return dm, zeta, xi, math.exp(BLOCK * lg)


def _valid_col(n):
    row = lax.broadcasted_iota(jnp.int32, (BLOCK, 1), 0)
    return ((n * BLOCK + row) >= PAD_FRONT).astype(F32)


def _shift_down(cur, prev, k):
    row = lax.broadcasted_iota(jnp.int32, cur.shape, 0)
    return jnp.where(row >= k, pltpu.roll(cur, k, 0), pltpu.roll(prev, k, 0))


def _shift_up(cur, nxt, k):
    row = lax.broadcasted_iota(jnp.int32, cur.shape, 0)
    return jnp.where(row < BLOCK - k, pltpu.roll(cur, BLOCK - k, 0), pltpu.roll(nxt, BLOCK - k, 0))


def mixers_fwd(proj, cosf, sinf, bkt, rel_bias, sinks, conv_w, nb, nc):
    def body(p_ref, cos_ref, sin_ref, bkt_ref, rb_ref, sk_ref, cw_ref, br_ref, st_ref,
             bias_s, kv_s, state_s, u_s):
        b = pl.program_id(0)
        n = pl.program_id(1)

        @pl.when((b == 0) & (n == 0))
        def _():
            _build_bias(bkt_ref, rb_ref, bias_s)

        @pl.when(n == 0)
        def _():
            kv_s[0:BLOCK, :] = jnp.zeros((BLOCK, 2 * BLOCK), F32)
            state_s[...] = jnp.zeros_like(state_s)
            u_s[...] = jnp.zeros_like(u_s)

        valid = _valid_col(n)
        mask = _band_mask(n)

        kv_s[BLOCK:2 * BLOCK, :] = p_ref[:, C_AK:C_AK + 2 * BLOCK]
        k_all = kv_s[:, 0:BLOCK]
        v_all = kv_s[:, BLOCK:2 * BLOCK]
        for kh in range(2):
            k_lo, k_hi = [t.astype(BF16) for t in _split_heads(k_all, kh)]
            v_lo, v_hi = [t.astype(BF16) for t in _split_heads(v_all, kh)]
            for jj in range(2):
                j = 2 * kh + jj
                q2b = p_ref[:, C_AQ + BLOCK * j:C_AQ + BLOCK * (j + 1)].astype(BF16)
                p_lo, _ = _softmax_sink(q2b, k_lo, bias_s[2 * j], mask, sk_ref[0, 2 * j])
                p_hi, _ = _softmax_sink(q2b, k_hi, bias_s[2 * j + 1], mask, sk_ref[0, 2 * j + 1])
                o2 = _nn(p_lo.astype(BF16), v_lo) + _nn(p_hi.astype(BF16), v_hi)
                gate = p_ref[:, C_AG + BLOCK * j:C_AG + BLOCK * (j + 1)]
                br_ref[:, BLOCK * j:BLOCK * (j + 1)] = (o2 * _silu(gate)).astype(BF16)
        kv_s[0:BLOCK, :] = kv_s[BLOCK:2 * BLOCK, :]

        cosv = cos_ref[...]
        sinv = sin_ref[...]
        for h in range(RET_HEADS):
            dm, zeta, xi, gamma_chunk = _decay_tables(h)
            q = _rot(p_ref[:, C_RQ + BLOCK * h:C_RQ + BLOCK * (h + 1)], cosv, sinv).astype(BF16)
            k = (_rot(p_ref[:, C_RK + BLOCK * h:C_RK + BLOCK * (h + 1)], cosv, sinv)
                 * RET_SCALE * valid).astype(BF16)
            v = p_ref[:, C_RV + BLOCK * h:C_RV + BLOCK * (h + 1)]
            s_prev = state_s[h]
            st_ref[0, 0, h] = s_prev
            a = (_nt(q, k) * dm).astype(BF16)
            o = _nn(a, v.astype(BF16)) + xi * _nn(q, s_prev.astype(BF16))
            mu = jnp.mean(o, axis=-1, keepdims=True)
            var = jnp.mean(jnp.square(o - mu), axis=-1, keepdims=True)
            oh = (o - mu) * lax.rsqrt(var + GN_EPS)
            gate = p_ref[:, C_RG + BLOCK * h:C_RG + BLOCK * (h + 1)]
            br_ref[:, BRANCH_WIDTH + BLOCK * h:BRANCH_WIDTH + BLOCK * (h + 1)] = (oh * _silu(gate)).astype(BF16)
            state_s[h] = gamma_chunk * s_prev + _tn(k, (v * zeta).astype(BF16))

        u = p_ref[:, C_CC:C_CC + BRANCH_WIDTH] * p_ref[:, C_CX:C_CX + BRANCH_WIDTH] * valid
        u_prev = u_s[...]
        y = (cw_ref[0:1, :] * _shift_down(u, u_prev, 2) + cw_ref[1:2, :] * _shift_down(u, u_prev, 1)
             + cw_ref[2:3, :] * u)
        yc = p_ref[:, C_CB:C_CB + BRANCH_WIDTH] * y * _silu(p_ref[:, C_CG:C_CG + BRANCH_WIDTH])
        br_ref[:, 2 * BRANCH_WIDTH:3 * BRANCH_WIDTH] = yc.astype(BF16)
        u_s[...] = u

    rows = nb * nc * BLOCK
    smem = pl.BlockSpec(memory_space=pltpu.SMEM)
    return pl.pallas_call(
        body, name="mixers_fwd",
        grid=(nb, nc),
        in_specs=[pl.BlockSpec((BLOCK, ABC_WIDTH), lambda b, n: (b * nc + n, 0)),
                  pl.BlockSpec((BLOCK, BLOCK), lambda b, n: (n, 0)),
                  pl.BlockSpec((BLOCK, BLOCK), lambda b, n: (n, 0)),
                  pl.BlockSpec((BLOCK, 2 * BLOCK), lambda b, n: (0, 0)),
                  smem, smem,
                  pl.BlockSpec((3, BRANCH_WIDTH), lambda b, n: (0, 0))],
        out_specs=[pl.BlockSpec((BLOCK, N_BRANCH * BRANCH_WIDTH), lambda b, n: (b * nc + n, 0)),
                   pl.BlockSpec((1, 1, RET_HEADS, BLOCK, BLOCK), lambda b, n: (b, n, 0, 0, 0))],
        out_shape=[jax.ShapeDtypeStruct((rows, N_BRANCH * BRANCH_WIDTH), BF16),
                   jax.ShapeDtypeStruct((nb, nc, RET_HEADS, BLOCK, BLOCK), F32)],
        scratch_shapes=[pltpu.VMEM((ATT_HEADS, BLOCK, 2 * BLOCK), F32),
                        pltpu.VMEM((2 * BLOCK, 2 * BLOCK), F32),
                        pltpu.VMEM((RET_HEADS, BLOCK, BLOCK), F32),
                        pltpu.VMEM((BLOCK, BRANCH_WIDTH), F32)],
        compiler_params=_cparams("arbitrary", "arbitrary"),
    )(proj, cosf, sinf, bkt, rel_bias, sinks, conv_w)


def mixers_bwd(proj, d_br, states, cosf, sinf, bkt, rel_bias, sinks, conv_w, nb, nc):
    def body(p_ref, kvp_ref, cp_ref, dbr_ref, st_ref, cos_ref, sin_ref, bkt_ref, rb_ref, sk_ref, cw_ref,
             dp_ref, drb_ref, dsk_ref, dcw_ref,
             bias_s, dbias_s, dkv_s, g_s, dy_s):
        b = pl.program_id(0)
        step = pl.program_id(1)
        n = nc - 1 - step
        first = (b == 0) & (step == 0)
        last = (b == nb - 1) & (step == nc - 1)

        @pl.when(first)
        def _():
            _build_bias(bkt_ref, rb_ref, bias_s)
            dbias_s[...] = jnp.zeros_like(dbias_s)
            dsk_ref[...] = jnp.zeros_like(dsk_ref)
            dcw_ref[...] = jnp.zeros_like(dcw_ref)
            drb_ref[...] = jnp.zeros_like(drb_ref)

        @pl.when(step == 0)
        def _():
            dkv_s[...] = jnp.zeros_like(dkv_s)
            g_s[...] = jnp.zeros_like(g_s)
            dy_s[...] = jnp.zeros_like(dy_s)

        valid = _valid_col(n)
        mask = _band_mask(n)
        has_prev = (n > 0).astype(F32)

        kv_prev = kvp_ref[...] * has_prev
        kv_cur = p_ref[:, C_AK:C_AK + 2 * BLOCK]
        k_all = jnp.concatenate([kv_prev[:, 0:BLOCK], kv_cur[:, 0:BLOCK]], axis=0)
        v_all = jnp.concatenate([kv_prev[:, BLOCK:], kv_cur[:, BLOCK:]], axis=0)
        dk_tot = jnp.zeros((2 * BLOCK, BLOCK), F32)
        dv_tot = jnp.zeros((2 * BLOCK, BLOCK), F32)
        for kh in range(2):
            ks = [t.astype(BF16) for t in _split_heads(k_all, kh)]
            vs = [t.astype(BF16) for t in _split_heads(v_all, kh)]
            dk_acc = [jnp.zeros((2 * BLOCK, BLOCK), F32), jnp.zeros((2 * BLOCK, BLOCK), F32)]
            dv_acc = [jnp.zeros((2 * BLOCK, BLOCK), F32), jnp.zeros((2 * BLOCK, BLOCK), F32)]
            for jj in range(2):
                j = 2 * kh + jj
                q2b = p_ref[:, C_AQ + BLOCK * j:C_AQ + BLOCK * (j + 1)].astype(BF16)
                gate = p_ref[:, C_AG + BLOCK * j:C_AG + BLOCK * (j + 1)]
                d_ya = dbr_ref[:, BLOCK * j:BLOCK * (j + 1)]
                do2 = (d_ya * _silu(gate)).astype(BF16)
                o2 = jnp.zeros((BLOCK, BLOCK), F32)
                dq2 = jnp.zeros((BLOCK, BLOCK), F32)
                for x in range(2):
                    h = 2 * j + x
                    p, p_sink = _softmax_sink(q2b, ks[x], bias_s[h], mask, sk_ref[0, h])
                    pb = p.astype(BF16)
                    o2 = o2 + _nn(pb, vs[x])
                    dp = _nt(do2, vs[x])
                    delta = jnp.sum(p * dp, axis=-1, keepdims=True)
                    ds = p * (dp - delta)
                    dbias_s[h] += ds
                    dsk_ref[h:h + 1, :] += jnp.broadcast_to(
                        jnp.sum(-p_sink * delta, axis=0, keepdims=True), (1, BLOCK))
                    dsb = ds.astype(BF16)
                    dq2 = dq2 + _nn(dsb, ks[x]) * ATT_SCALE
                    dk_acc[x] = dk_acc[x] + _tn(dsb, q2b) * ATT_SCALE
                    dv_acc[x] = dv_acc[x] + _tn(pb, do2)
                dp_ref[:, C_AQ + BLOCK * j:C_AQ + BLOCK * (j + 1)] = dq2.astype(BF16)
                dp_ref[:, C_AG + BLOCK * j:C_AG + BLOCK * (j + 1)] = (d_ya * o2 * _dsilu(gate)).astype(BF16)
            dk_tot = dk_tot + _merge_heads(dk_acc[0], dk_acc[1], kh)
            dv_tot = dv_tot + _merge_heads(dv_acc[0], dv_acc[1], kh)
        dp_ref[:, C_AK:C_AK + BLOCK] = (dk_tot[BLOCK:, :] + dkv_s[:, 0:BLOCK]).astype(BF16)
        dp_ref[:, C_AV:C_AV + BLOCK] = (dv_tot[BLOCK:, :] + dkv_s[:, BLOCK:]).astype(BF16)
        dkv_s[:, 0:BLOCK] = dk_tot[0:BLOCK, :]
        dkv_s[:, BLOCK:] = dv_tot[0:BLOCK, :]

        cosv = cos_ref[...]
        sinv = sin_ref[...]
        for h in range(RET_HEADS):
            dm, zeta, xi, gamma_chunk = _decay_tables(h)
            sl = lambda c0: slice(c0 + BLOCK * h, c0 + BLOCK * (h + 1))
            q = _rot(p_ref[:, sl(C_RQ)], cosv, sinv).astype(BF16)
            k = (_rot(p_ref[:, sl(C_RK)], cosv, sinv) * RET_SCALE * valid).astype(BF16)
            v = p_ref[:, sl(C_RV)]
            vb = v.astype(BF16)
            gate = p_ref[:, sl(C_RG)]
            s_prev = st_ref[0, 0, h].astype(BF16)
            g_next = g_s[h]
            a = (_nt(q, k) * dm).astype(BF16)
            o = _nn(a, vb) + xi * _nn(q, s_prev)
            mu = jnp.mean(o, axis=-1, keepdims=True)
            var = jnp.mean(jnp.square(o - mu), axis=-1, keepdims=True)
            rstd = lax.rsqrt(var + GN_EPS)
            oh = (o - mu) * rstd
            d_yr = dbr_ref[:, BRANCH_WIDTH + BLOCK * h:BRANCH_WIDTH + BLOCK * (h + 1)]
            dp_ref[:, sl(C_RG)] = (d_yr * oh * _dsilu(gate)).astype(BF16)
            doh = d_yr * _silu(gate)
            do = rstd * (doh - jnp.mean(doh, axis=-1, keepdims=True)
                         - oh * jnp.mean(doh * oh, axis=-1, keepdims=True))
            dob = do.astype(BF16)
            dxo = (do * xi).astype(BF16)
            da = (_nt(dob, vb) * dm).astype(BF16)
            gb = g_next.astype(BF16)
            zv = (v * zeta).astype(BF16)
            dq = _nn(da, k) + _nt(dxo, s_prev)
            dk = _tn(da, q) + _nt(zv, gb)
            dv = _tn(a, dob) + zeta * _nn(k, gb)
            g_s[h] = gamma_chunk * g_next + _tn(q, dxo)
            dp_ref[:, sl(C_RQ)] = _rot_t(dq, cosv, sinv).astype(BF16)
            dp_ref[:, sl(C_RK)] = _rot_t(dk * (RET_SCALE * valid), cosv, sinv).astype(BF16)
            dp_ref[:, sl(C_RV)] = dv.astype(BF16)

        w0, w1, w2 = cw_ref[0:1, :], cw_ref[1:2, :], cw_ref[2:3, :]
        cb = p_ref[:, C_CB:C_CB + BRANCH_WIDTH]
        cc = p_ref[:, C_CC:C_CC + BRANCH_WIDTH]
        cx = p_ref[:, C_CX:C_CX + BRANCH_WIDTH]
        cg = p_ref[:, C_CG:C_CG + BRANCH_WIDTH]
        u = cc * cx * valid
        u_prev = cp_ref[:, 0:BRANCH_WIDTH] * cp_ref[:, BRANCH_WIDTH:2 * BRANCH_WIDTH] * (_valid_col(n - 1) * has_prev)
        u1 = _shift_down(u, u_prev, 1)
        u2 = _shift_down(u, u_prev, 2)
        y = w0 * u2 + w1 * u1 + w2 * u
        d_yc = dbr_ref[:, 2 * BRANCH_WIDTH:3 * BRANCH_WIDTH]
        sg = _silu(cg)
        dp_ref[:, C_CB:C_CB + BRANCH_WIDTH] = (d_yc * y * sg).astype(BF16)
        dp_ref[:, C_CG:C_CG + BRANCH_WIDTH] = (d_yc * cb * y * _dsilu(cg)).astype(BF16)
        dy = d_yc * cb * sg
        dy_next = dy_s[...]
        du = (w2 * dy + w1 * _shift_up(dy, dy_next, 1) + w0 * _shift_up(dy, dy_next, 2)) * valid
        dp_ref[:, C_CC:C_CC + BRANCH_WIDTH] = (du * cx).astype(BF16)
        dp_ref[:, C_CX:C_CX + BRANCH_WIDTH] = (du * cc).astype(BF16)
        dcw_ref[0:1, :] += jnp.sum(dy * u2, axis=0, keepdims=True)
        dcw_ref[1:2, :] += jnp.sum(dy * u1, axis=0, keepdims=True)
        dcw_ref[2:3, :] += jnp.sum(dy * u, axis=0, keepdims=True)
        dy_s[...] = dy

        @pl.when(last)
        def _():
            bkt = bkt_ref[...]
            row = lax.broadcasted_iota(jnp.int32, (N_BUCKETS, BLOCK), 0)
            lane = lax.broadcasted_iota(jnp.int32, (N_BUCKETS, BLOCK), 1)

            def one_bucket(bk, acc):
                sel = bkt == bk
                for h in range(ATT_HEADS):
                    t = jnp.where(sel, dbias_s[h], 0.0)
                    s = jnp.sum(jnp.sum(t, axis=1, keepdims=True), axis=0, keepdims=True)
                    acc = acc + jnp.where((row == bk) & (lane == h), jnp.broadcast_to(s, acc.shape), 0.0)
                return acc

            drb_ref[...] = lax.fori_loop(0, N_BUCKETS, one_bucket, jnp.zeros((N_BUCKETS, BLOCK), F32))

    rows = nb * nc * BLOCK
    smem = pl.BlockSpec(memory_space=pltpu.SMEM)
    blk = lambda b, s: b * nc + (nc - 1 - s)
    prev = lambda b, s: b * nc + jnp.maximum(nc - 2 - s, 0)
    return pl.pallas_call(
        body, name="mixers_bwd",
        grid=(nb, nc),
        in_specs=[pl.BlockSpec((BLOCK, ABC_WIDTH), lambda b, s: (blk(b, s), 0)),
                  pl.BlockSpec((BLOCK, 2 * BLOCK), lambda b, s: (prev(b, s), C_AK // (2 * BLOCK))),
                  pl.BlockSpec((BLOCK, 1280), lambda b, s: (prev(b, s), C_CC // 1280)),
                  pl.BlockSpec((BLOCK, N_BRANCH * BRANCH_WIDTH), lambda b, s: (blk(b, s), 0)),
                  pl.BlockSpec((1, 1, RET_HEADS, BLOCK, BLOCK), lambda b, s: (b, nc - 1 - s, 0, 0, 0)),
                  pl.BlockSpec((BLOCK, BLOCK), lambda b, s: (nc - 1 - s, 0)),
                  pl.BlockSpec((BLOCK, BLOCK), lambda b, s: (nc - 1 - s, 0)),
                  pl.BlockSpec((BLOCK, 2 * BLOCK), lambda b, s: (0, 0)),
                  smem, smem,
                  pl.BlockSpec((3, BRANCH_WIDTH), lambda b, s: (0, 0))],
        out_specs=[pl.BlockSpec((BLOCK, ABC_WIDTH), lambda b, s: (blk(b, s), 0)),
                   pl.BlockSpec((N_BUCKETS, BLOCK), lambda b, s: (0, 0)),
                   pl.BlockSpec((ATT_HEADS, BLOCK), lambda b, s: (0, 0)),
                   pl.BlockSpec((8, BRANCH_WIDTH), lambda b, s: (0, 0))],
        out_shape=[jax.ShapeDtypeStruct((rows, ABC_WIDTH), BF16),
                   jax.ShapeDtypeStruct((N_BUCKETS, BLOCK), F32),
                   jax.ShapeDtypeStruct((ATT_HEADS, BLOCK), F32),
                   jax.ShapeDtypeStruct((8, BRANCH_WIDTH), F32)],
        scratch_shapes=[pltpu.VMEM((ATT_HEADS, BLOCK, 2 * BLOCK), F32),
                        pltpu.VMEM((ATT_HEADS, BLOCK, 2 * BLOCK), F32),
                        pltpu.VMEM((BLOCK, 2 * BLOCK), F32),
                        pltpu.VMEM((RET_HEADS, BLOCK, BLOCK), F32),
                        pltpu.VMEM((BLOCK, BRANCH_WIDTH), F32)],
        compiler_params=_cparams("arbitrary", "arbitrary"),
    )(proj, proj, proj, d_br, states, cosf, sinf, bkt, rel_bias, sinks, conv_w)


MERGE_TILE = 256


def _merge_forward(br_ref, m_ref, wb_ref, wo_ref):
    bo, gates = [], []
    mixed_pre = None
    for g in range(N_BRANCH):
        br_g = br_ref[:, BRANCH_WIDTH * g:BRANCH_WIDTH * (g + 1)]
        bo_g = jnp.concatenate([_nn(br_g, wb_ref[p, g]) for p in range(N_CHIPS)], axis=1)
        gate_g = jax.nn.sigmoid(m_ref[:, D_MODEL * g:D_MODEL * (g + 1)])
        bo.append(bo_g)
        gates.append(gate_g)
        mixed_pre = gate_g * bo_g if mixed_pre is None else mixed_pre + gate_g * bo_g
    mixed = _nn(mixed_pre.astype(BF16), wo_ref[...])
    r = lax.rsqrt(jnp.mean(mixed * mixed, axis=-1, keepdims=True) + RMS_EPS)
    return bo, gates, mixed_pre, mixed, r


def merge_fwd(x2d, br, pm, wb, wo, g_post, layer):
    t = x2d.shape[0]
    tm = MERGE_TILE if t % MERGE_TILE == 0 else BLOCK

    def body(x_ref, br_ref, m_ref, wb_ref, wo_ref, g_ref, o_ref):
        _, _, _, mixed, r = _merge_forward(br_ref, m_ref, wb_ref, wo_ref)
        o_ref[...] = x_ref[...] + mixed * r * g_ref[...]

    return pl.pallas_call(
        body, name="merge_fwd",
        grid=(t // tm,),
        in_specs=[pl.BlockSpec((tm, D_MODEL), lambda i: (i, 0)),
                  pl.BlockSpec((tm, N_BRANCH * BRANCH_WIDTH), lambda i: (i, 0)),
                  pl.BlockSpec((tm, MERGE_WIDTH), lambda i: (i, 0)),
                  pl.BlockSpec((None, N_CHIPS, N_BRANCH, BRANCH_WIDTH, SHARD_D), lambda i: (layer, 0, 0, 0, 0)),
                  pl.BlockSpec((None, D_MODEL, D_MODEL), lambda i: (layer, 0, 0)),
                  pl.BlockSpec((1, D_MODEL), lambda i: (0, 0))],
        out_specs=pl.BlockSpec((tm, D_MODEL), lambda i: (i, 0)),
        out_shape=jax.ShapeDtypeStruct((t, D_MODEL), F32),
        compiler_params=_cparams("parallel"),
    )(x2d, br, pm, wb, wo, g_post)


def merge_bwd(d_out, br, pm, wb, wo, g_post, layer, acc):
    t = d_out.shape[0]
    tm = MERGE_TILE if t % MERGE_TILE == 0 else BLOCK

    def body(*refs):
        do_ref, br_ref, m_ref, wb_ref, wo_ref, g_ref = refs[:6]
        dbr_ref, dm_ref, dg_ref, dwb_ref, dwo_ref = refs[-5:]

        @pl.when(pl.program_id(0) == 0)
        def _():
            dwb_ref[...] = jnp.zeros_like(dwb_ref)
            dwo_ref[...] = jnp.zeros_like(dwo_ref)
            dg_ref[...] = jnp.zeros_like(dg_ref)

        bo, gates, mixed_pre, mixed, r = _merge_forward(br_ref, m_ref, wb_ref, wo_ref)
        d_o = do_ref[...]
        nh = mixed * r
        dg_ref[0:1, :] += jnp.sum(d_o * nh, axis=0, keepdims=True)
        dn = d_o * g_ref[...]
        d_mixed = (r * (dn - nh * jnp.mean(dn * nh, axis=-1, keepdims=True))).astype(BF16)
        dwo_ref[...] += _tn(mixed_pre.astype(BF16), d_mixed)
        d_pre = _nt(d_mixed, wo_ref[...])
        for g in range(N_BRANCH):
            br_g = br_ref[:, BRANCH_WIDTH * g:BRANCH_WIDTH * (g + 1)]
            d_bo = (d_pre * gates[g]).astype(BF16)
            dm_ref[:, D_MODEL * g:D_MODEL * (g + 1)] = (
                d_pre * bo[g] * gates[g] * (1.0 - gates[g])).astype(BF16)
            d_br_g = None
            for p in range(N_CHIPS):
                d_bo_p = d_bo[:, SHARD_D * p:SHARD_D * (p + 1)]
                part = _nt(d_bo_p, wb_ref[p, g])
                d_br_g = part if d_br_g is None else d_br_g + part
                dwb_ref[p, g] += _tn(br_g, d_bo_p)
            dbr_ref[:, BRANCH_WIDTH * g:BRANCH_WIDTH * (g + 1)] = d_br_g

    ins = [d_out, br, pm, wb, wo, g_post]
    in_specs = [pl.BlockSpec((tm, D_MODEL), lambda i: (i, 0)),
                pl.BlockSpec((tm, N_BRANCH * BRANCH_WIDTH), lambda i: (i, 0)),
                pl.BlockSpec((tm, MERGE_WIDTH), lambda i: (i, 0)),
                pl.BlockSpec((None, N_CHIPS, N_BRANCH, BRANCH_WIDTH, SHARD_D), lambda i: (layer, 0, 0, 0, 0)),
                pl.BlockSpec((None, D_MODEL, D_MODEL), lambda i: (layer, 0, 0)),
                pl.BlockSpec((1, D_MODEL), lambda i: (0, 0))]
    aliases = {}
    if acc is not None:
        ins += list(acc)
        in_specs += [ANY, ANY]
        aliases = {6: 3, 7: 4}
    return pl.pallas_call(
        body, name="merge_bwd",
        grid=(t // tm,),
        in_specs=in_specs,
        out_specs=[pl.BlockSpec((tm, N_BRANCH * BRANCH_WIDTH), lambda i: (i, 0)),
                   pl.BlockSpec((tm, MERGE_WIDTH), lambda i: (i, 0)),
                   pl.BlockSpec((8, D_MODEL), lambda i: (0, 0)),
                   pl.BlockSpec((None, N_CHIPS, N_BRANCH, BRANCH_WIDTH, SHARD_D), lambda i: (layer, 0, 0, 0, 0)),
                   pl.BlockSpec((None, D_MODEL, D_MODEL), lambda i: (layer, 0, 0))],
        out_shape=[jax.ShapeDtypeStruct((t, N_BRANCH * BRANCH_WIDTH), F32),
                   jax.ShapeDtypeStruct((t, MERGE_WIDTH), BF16),
                   jax.ShapeDtypeStruct((8, D_MODEL), F32),
                   jax.ShapeDtypeStruct((DEPTH, N_CHIPS, N_BRANCH, BRANCH_WIDTH, SHARD_D), F32),
                   jax.ShapeDtypeStruct((DEPTH, D_MODEL, D_MODEL), F32)],
        input_output_aliases=aliases,
        compiler_params=_cparams("arbitrary"),
    )(*ins)


def loss_head(xf, target2d, nb, nc):
    def body(x_ref, t_ref, l_ref, dx_ref):
        b = pl.program_id(0)
        n = pl.program_id(1)

        @pl.when((b == 0) & (n == 0))
        def _():
            l_ref[...] = jnp.zeros_like(l_ref)

        @pl.when(n == 0)
        def _():
            dx_ref[...] = jnp.zeros_like(dx_ref)

        @pl.when(n > 0)
        def _():
            e = x_ref[...] - t_ref[...]
            dx_ref[...] = e * (1.0 / D_MODEL)
            s = jnp.sum(jnp.sum(e * e, axis=1, keepdims=True), axis=0, keepdims=True)
            l_ref[...] += jnp.broadcast_to(s * (0.5 / D_MODEL), l_ref.shape)

    return pl.pallas_call(
        body, name="loss_head",
        grid=(nb, nc),
        in_specs=[pl.BlockSpec((BLOCK, D_MODEL), lambda b, n: (b * nc + n, 0)),
                  pl.BlockSpec((BLOCK, D_MODEL), lambda b, n: (b * (nc - 1) + jnp.maximum(n - 1, 0), 0))],
        out_specs=[pl.BlockSpec((8, BLOCK), lambda b, n: (0, 0)),
                   pl.BlockSpec((BLOCK, D_MODEL), lambda b, n: (b * nc + n, 0))],
        out_shape=[jax.ShapeDtypeStruct((8, BLOCK), F32),
                   jax.ShapeDtypeStruct(xf.shape, F32)],
        compiler_params=_cparams("arbitrary", "arbitrary"),
    )(xf, target2d)


N_ABC_TILES = ABC_WIDTH // COL_TILE
N_M_TILES = MERGE_WIDTH // COL_TILE


def proj_dgrad(d_abc, d_m, w, layer, x2d, g, d_out):
    t = x2d.shape[0]
    tm = ROW_TILE if t % ROW_TILE == 0 else BLOCK
    nk = N_ABC_TILES + N_M_TILES

    def body(da_ref, dm_ref, w_ref, x_ref, g_ref, do_ref, dx_ref, dg_ref, acc):
        i = pl.program_id(0)
        k = pl.program_id(1)

        @pl.when((i == 0) & (k == 0))
        def _():
            dg_ref[...] = jnp.zeros_like(dg_ref)

        @pl.when(k == 0)
        def _():
            acc[...] = jnp.zeros_like(acc)

        @pl.when(k < N_ABC_TILES)
        def _():
            acc[...] += _nn(da_ref[...], w_ref[...])

        @pl.when(k >= N_ABC_TILES)
        def _():
            acc[...] += _nn(dm_ref[...], w_ref[...])

        @pl.when(k == nk - 1)
        def _():
            x = x_ref[...]
            r = lax.rsqrt(jnp.mean(x * x, axis=-1, keepdims=True) + RMS_EPS)
            nh = x * r
            dh = acc[...]
            dg_ref[0:1, :] += jnp.sum(dh * nh, axis=0, keepdims=True)
            dn = dh * g_ref[...]
            dx_ref[...] = do_ref[...] + r * (dn - nh * jnp.mean(dn * nh, axis=-1, keepdims=True))

    return pl.pallas_call(
        body, name="proj_dgrad",
        grid=(t // tm, nk),
        in_specs=[pl.BlockSpec((tm, COL_TILE), lambda i, k: (i, jnp.minimum(k, N_ABC_TILES - 1))),
                  pl.BlockSpec((tm, COL_TILE), lambda i, k: (i, jnp.maximum(k - N_ABC_TILES, 0))),
                  pl.BlockSpec((None, COL_TILE, D_MODEL), lambda i, k: (layer, k, 0)),
                  pl.BlockSpec((tm, D_MODEL), lambda i, k: (i, 0)),
                  pl.BlockSpec((1, D_MODEL), lambda i, k: (0, 0)),
                  pl.BlockSpec((tm, D_MODEL), lambda i, k: (i, 0))],
        out_specs=[pl.BlockSpec((tm, D_MODEL), lambda i, k: (i, 0)),
                   pl.BlockSpec((8, D_MODEL), lambda i, k: (0, 0))],
        out_shape=[jax.ShapeDtypeStruct((t, D_MODEL), F32),
                   jax.ShapeDtypeStruct((8, D_MODEL), F32)],
        scratch_shapes=[pltpu.VMEM((tm, D_MODEL), F32)],
        compiler_params=_cparams("arbitrary", "arbitrary"),
    )(d_abc, d_m, w, x2d, g, d_out)


def proj_wgrad(hb, d_abc, d_m, layer, acc):
    t = hb.shape[0]
    nj = N_ABC_TILES + N_M_TILES

    def body(*refs):
        h_ref, da_ref, dm_ref = refs[:3]
        o_ref = refs[-1]
        j = pl.program_id(0)

        @pl.when(j < N_ABC_TILES)
        def _():
            o_ref[...] = _tn(da_ref[...], h_ref[...])

        @pl.when(j >= N_ABC_TILES)
        def _():
            o_ref[...] = _tn(dm_ref[...], h_ref[...])

    ins = [hb, d_abc, d_m]
    in_specs = [pl.BlockSpec((t, D_MODEL), lambda j: (0, 0)),
                pl.BlockSpec((t, COL_TILE), lambda j: (0, jnp.minimum(j, N_ABC_TILES - 1))),
                pl.BlockSpec((t, COL_TILE), lambda j: (0, jnp.maximum(j - N_ABC_TILES, 0)))]
    aliases = {}
    if acc is not None:
        ins.append(acc)
        in_specs.append(ANY)
        aliases = {3: 0}
    return pl.pallas_call(
        body, name="proj_wgrad",
        grid=(nj,),
        in_specs=in_specs,
        out_specs=pl.BlockSpec((None, COL_TILE, D_MODEL), lambda j: (layer, j, 0)),
        out_shape=jax.ShapeDtypeStruct((DEPTH, PROJ_WIDTH, D_MODEL), F32),
        input_output_aliases=aliases,
        compiler_params=_cparams("arbitrary"),
    )(*ins)


def _adamw_math(w, g, m, v):
    m = ADAM_B1 * m + (1.0 - ADAM_B1) * g
    v = ADAM_B2 * v + (1.0 - ADAM_B2) * jnp.square(g)
    m_hat = m / (1.0 - ADAM_B1 ** ADAM_STEP)
    v_hat = v / (1.0 - ADAM_B2 ** ADAM_STEP)
    delta = -ADAM_LR * (m_hat / (jnp.sqrt(v_hat) + ADAM_EPS) + ADAM_WD * w)
    return delta, m, v


def adamw_big(w, g, m, v):
    lead, (r, c) = w.shape[:-2], w.shape[-2:]
    tr = 256 if r % 256 == 0 else r

    def body(w_ref, g_ref, m_ref, v_ref, d_ref, mo_ref, vo_ref):
        d, m_new, v_new = _adamw_math(w_ref[...], g_ref[...], m_ref[...], v_ref[...])
        d_ref[...] = d
        mo_ref[...] = m_new
        vo_ref[...] = v_new

    spec = pl.BlockSpec((None,) * len(lead) + (tr, c), lambda *idx: idx + (0,))
    sds = jax.ShapeDtypeStruct(w.shape, F32)
    return pl.pallas_call(
        body, name="adamw_big",
        grid=lead + (r // tr,),
        in_specs=[spec] * 4, out_specs=[spec] * 3, out_shape=[sds] * 3,
        compiler_params=_cparams(*(["parallel"] * (len(lead) + 1))),
    )(w, g, m, v)


def adamw_small(params):
    k = len(params)

    def body(*refs):
        ins, outs = refs[:4 * k], refs[4 * k:]
        for i in range(k):
            d, m_new, v_new = _adamw_math(*[r[...] for r in ins[4 * i:4 * i + 4]])
            outs[3 * i][...] = d
            outs[3 * i + 1][...] = m_new
            outs[3 * i + 2][...] = v_new

    flat = [a for p in params for a in p]
    vm = pl.BlockSpec(memory_space=pltpu.VMEM)
    out_shape = [jax.ShapeDtypeStruct(p[0].shape, F32) for p in params for _ in range(3)]
    res = pl.pallas_call(
        body, name="adamw_small",
        in_specs=[vm] * len(flat), out_specs=[vm] * len(out_shape), out_shape=out_shape,
    )(*flat)
    return [tuple(res[3 * i:3 * i + 3]) for i in range(k)]


ANY = pl.BlockSpec(memory_space=pl.ANY)


def _place():
    return lax.axis_index("x"), lax.axis_index("y"), lax.axis_index("c")


def _own_slot(shard, chip):
    buf = lax.empty((DEPTH, N_CHIPS) + shard.shape[1:], shard.dtype)
    return lax.dynamic_update_slice(buf, shard[:, None], (0, chip) + (0,) * (shard.ndim - 1))


def gather_weight_shards(bufs):
    n = len(bufs)

    def body(*refs):
        g_refs = refs[n:2 * n]
        send_sems, recv_sems = refs[2 * n:]
        x, y, c = _place()
        me_p = 2 * x + y
        sibling = (x, y, 1 - c)
        chips = [(1 - x, y), (x, 1 - y), (1 - x, 1 - y)]

        def copy(k, slab, to):
            return pltpu.make_async_remote_copy(src_ref=slab, dst_ref=slab, send_sem=send_sems.at[k],
                                                recv_sem=recv_sems.at[k], device_id=to, device_id_type=MESH)

        first, passed = [], []
        for t in range(n):
            for k, (qx, qy) in enumerate(chips):
                cp = copy(6 * t + k, g_refs[t].at[c, me_p], (qx, qy, c))
                cp.start()
                first.append(cp)
        for t in range(n):
            for k, (qx, qy) in enumerate(chips):
                slab = g_refs[t].at[c, 2 * qx + qy]
                copy(6 * t + k, slab, (qx, qy, c)).wait_recv()
                fwd = copy(6 * t + 3 + k, slab, sibling)
                fwd.start()
                passed.append(fwd)
        for t in range(n):
            for k, (qx, qy) in enumerate(chips):
                copy(6 * t + 3 + k, g_refs[t].at[1 - c, 2 * qx + qy], sibling).wait_recv()
        for cp in first + passed:
            cp.wait_send()

    return pl.pallas_call(
        body, name="gather_weight_shards",
        in_specs=[ANY] * n, out_specs=[ANY] * n,
        out_shape=[jax.ShapeDtypeStruct(b.shape, b.dtype) for b in bufs],
        input_output_aliases={t: t for t in range(n)},
        scratch_shapes=[pltpu.SemaphoreType.DMA((6 * n,)), pltpu.SemaphoreType.DMA((6 * n,))],
    )(*bufs)


def exchange_small(pack):
    def body(p_ref, o_ref, send_sems, recv_sems, local_sem):
        x, y, c = _place()
        me = 4 * x + 2 * y + c
        mine = pltpu.make_async_copy(p_ref, o_ref.at[me], local_sem)
        mine.start()
        sends = []
        for k in range(1, 8):
            fx, fy, fc = (k >> 2) & 1, (k >> 1) & 1, k & 1
            peer = (x ^ fx, y ^ fy, c ^ fc)
            cp = pltpu.make_async_remote_copy(src_ref=p_ref, dst_ref=o_ref.at[me], send_sem=send_sems.at[k - 1],
                                              recv_sem=recv_sems.at[k - 1], device_id=peer, device_id_type=MESH)
            cp.start()
            sends.append(cp)
        for k in range(1, 8):
            fx, fy, fc = (k >> 2) & 1, (k >> 1) & 1, k & 1
            peer = (x ^ fx, y ^ fy, c ^ fc)
            slot = o_ref.at[4 * peer[0] + 2 * peer[1] + peer[2]]
            pltpu.make_async_remote_copy(src_ref=slot, dst_ref=slot, send_sem=send_sems.at[k - 1],
                                         recv_sem=recv_sems.at[k - 1], device_id=peer, device_id_type=MESH).wait_recv()
        for cp in sends:
            cp.wait_send()
        mine.wait()

    return pl.pallas_call(
        body, name="exchange_small",
        in_specs=[ANY], out_specs=ANY,
        out_shape=jax.ShapeDtypeStruct((8,) + pack.shape, pack.dtype),
        scratch_shapes=[pltpu.SemaphoreType.DMA((7,)), pltpu.SemaphoreType.DMA((7,)), pltpu.SemaphoreType.DMA],
    )(pack)


def sibling_swap_layers(grads):
    n = len(grads)

    def body(*refs):
        g_refs, o_refs = refs[:n], refs[n:2 * n]
        send_sems, recv_sems = refs[2 * n:]
        x, y, c = _place()
        cps = []
        for t in range(n):
            cp = pltpu.make_async_remote_copy(src_ref=g_refs[t].at[1 - c], dst_ref=o_refs[t],
                                              send_sem=send_sems.at[t], recv_sem=recv_sems.at[t],
                                              device_id=(x, y, 1 - c), device_id_type=MESH)
            cp.start()
            cps.append(cp)
        for cp in cps:
            cp.wait()

    return pl.pallas_call(
        body, name="sibling_swap_layers",
        in_specs=[ANY] * n, out_specs=[ANY] * n,
        out_shape=[jax.ShapeDtypeStruct(g.shape[1:], g.dtype) for g in grads],
        scratch_shapes=[pltpu.SemaphoreType.DMA((n,)), pltpu.SemaphoreType.DMA((n,))],
    )(*grads)


def _row_tile(r):
    return max(t for t in range(16, 513, 16) if r % t == 0)


def add_own_layer(g, other, c_arr):
    _, _, r, cols = g.shape
    tr = _row_tile(r)

    def body(c_ref, a_ref, b_ref, o_ref):
        o_ref[...] = (a_ref[...] + b_ref[...]).astype(BF16)

    return pl.pallas_call(
        body, name="add_own_layer",
        grid_spec=pltpu.PrefetchScalarGridSpec(
            num_scalar_prefetch=1, grid=(N_CHIPS, r // tr),
            in_specs=[pl.BlockSpec((None, None, tr, cols), lambda p, i, c_ref: (c_ref[0], p, i, 0)),
                      pl.BlockSpec((None, tr, cols), lambda p, i, c_ref: (p, i, 0))],
            out_specs=pl.BlockSpec((None, tr, cols), lambda p, i, c_ref: (p, i, 0))),
        out_shape=jax.ShapeDtypeStruct((N_CHIPS, r, cols), BF16),
        compiler_params=_cparams("parallel", "parallel"),
    )(c_arr, g, other)


def scatter_to_chips(partials):
    n = len(partials)

    def body(*refs):
        s_refs, o_refs = refs[:n], refs[n:2 * n]
        send_sems, recv_sems = refs[2 * n:]
        x, y, c = _place()
        chips = [(1 - x, y), (x, 1 - y), (1 - x, 1 - y)]
        cps = []
        for t in range(n):
            for k, (qx, qy) in enumerate(chips):
                cp = pltpu.make_async_remote_copy(src_ref=s_refs[t].at[2 * qx + qy], dst_ref=o_refs[t].at[k],
                                                  send_sem=send_sems.at[3 * t + k], recv_sem=recv_sems.at[3 * t + k],
                                                  device_id=(qx, qy, c), device_id_type=MESH)
                cp.start()
                cps.append(cp)
        for cp in cps:
            cp.wait()

    return pl.pallas_call(
        body, name="scatter_to_chips",
        in_specs=[ANY] * n, out_specs=[ANY] * n,
        out_shape=[jax.ShapeDtypeStruct((3,) + s.shape[1:], s.dtype) for s in partials],
        scratch_shapes=[pltpu.SemaphoreType.DMA((3 * n,)), pltpu.SemaphoreType.DMA((3 * n,))],
    )(*partials)


def sum_chips(own, parts, where):
    _, r, cols = own.shape
    tr = _row_tile(r)

    def body(w_ref, a_ref, p_ref, o_ref):
        acc = a_ref[...].astype(F32)
        for k in range(3):
            acc = acc + p_ref[k].astype(F32)
        o_ref[...] = acc

    return pl.pallas_call(
        body, name="sum_chips",
        grid_spec=pltpu.PrefetchScalarGridSpec(
            num_scalar_prefetch=1, grid=(r // tr,),
            in_specs=[pl.BlockSpec((None, tr, cols), lambda i, w_ref: (w_ref[0], i, 0)),
                      pl.BlockSpec((3, tr, cols), lambda i, w_ref: (0, i, 0))],
            out_specs=pl.BlockSpec((None, tr, cols), lambda i, w_ref: (w_ref[1], i, 0))),
        out_shape=jax.ShapeDtypeStruct((DEPTH, r, cols), F32),
        compiler_params=_cparams("parallel"),
    )(where, own, parts)


def sibling_share_layer(bufs):
    n = len(bufs)

    def body(*refs):
        o_refs = refs[n:2 * n]
        send_sems, recv_sems = refs[2 * n:]
        x, y, c = _place()
        cps = []
        for t in range(n):
            cp = pltpu.make_async_remote_copy(src_ref=o_refs[t].at[c], dst_ref=o_refs[t].at[c], send_sem=send_sems.at[t],
                                              recv_sem=recv_sems.at[t], device_id=(x, y, 1 - c), device_id_type=MESH)
            cp.start()
            cps.append(cp)
        for t in range(n):
            slot = o_refs[t].at[1 - c]
            pltpu.make_async_remote_copy(src_ref=slot, dst_ref=slot, send_sem=send_sems.at[t], recv_sem=recv_sems.at[t],
                                         device_id=(x, y, 1 - c), device_id_type=MESH).wait_recv()
        for cp in cps:
            cp.wait_send()

    return pl.pallas_call(
        body, name="sibling_share_layer",
        in_specs=[ANY] * n, out_specs=[ANY] * n,
        out_shape=[jax.ShapeDtypeStruct(b.shape, b.dtype) for b in bufs],
        input_output_aliases={t: t for t in range(n)},
        scratch_shapes=[pltpu.SemaphoreType.DMA((n,)), pltpu.SemaphoreType.DMA((n,))],
    )(*bufs)


SP_META = 2 * (N_META * D_MODEL // LANES)
SP_NORM = DEPTH * D_MODEL // LANES
SP_RB = DEPTH * N_BUCKETS
SP_SINK = DEPTH * ATT_HEADS
SP_CONV = DEPTH * 3 * BRANCH_WIDTH // LANES
SP_LOSS = 8
SP_ROWS = SP_META + 2 * SP_NORM + SP_RB + SP_SINK + SP_CONV + SP_LOSS


def sum_small(slots):
    half = SP_META // 2
    rb0 = SP_META + 2 * SP_NORM
    rest_rows = SP_ROWS - SP_META

    def body(s_ref, meta_ref, rest_ref):
        acc = s_ref[0]
        for d in range(1, 8):
            acc = acc + s_ref[d]
        meta_ref[...] = acc[0:half] + acc[half:SP_META]
        rest_ref[...] = acc[SP_META:]
        rest_ref[rb0 - SP_META:rb0 - SP_META + N_BUCKETS, :] = (
            acc[rb0:rb0 + N_BUCKETS] + acc[rb0 + N_BUCKETS:rb0 + 2 * N_BUCKETS])

    vm = pl.BlockSpec(memory_space=pltpu.VMEM)
    return pl.pallas_call(
        body, name="sum_small",
        in_specs=[vm], out_specs=[vm, vm],
        out_shape=[jax.ShapeDtypeStruct((half, LANES), F32), jax.ShapeDtypeStruct((rest_rows, LANES), F32)],
    )(slots)


def local_step(x, loss_target, meta_full, rel_bias, norm_pre, conv_w_full, attn_sinks, norm_post, weights):
    nb, seq, _ = x.shape
    nc = seq // BLOCK + 1
    lp = nc * BLOCK
    rows = nb * lp
    pad = jnp.zeros((nb, PAD_FRONT, D_MODEL), F32)
    meta = jnp.broadcast_to(meta_full[None], (nb, N_META, D_MODEL))
    h0 = jnp.concatenate([pad, meta, x], axis=1).reshape(rows, D_MODEL)
    cosf, sinf = _rot_tables(lp)
    bkt = jnp.asarray(_bucket_table())

    w_in, w_br, w_out = weights
    acts = []
    h = h0
    for l in range(DEPTH):
        g_pre = norm_pre[l][None]
        hb, p_abc = norm_matmul(h, g_pre, w_in, l, 0, N_ABC_TILES)
        p_m = matmul_cols(hb, w_in, l, N_ABC_TILES, N_M_TILES)
        br, states = mixers_fwd(p_abc, cosf, sinf, bkt, rel_bias, attn_sinks[l][None], conv_w_full[l], nb, nc)
        h_next = merge_fwd(h, br, p_m, w_br, w_out, norm_post[l][None], l)
        acts.append((h, hb, p_abc, p_m, br, states))
        h = h_next

    loss_part, d_h = loss_head(h, loss_target.reshape(nb * seq, D_MODEL), nb, nc)

    small = [None] * DEPTH
    g_win = None
    g_wbo = None
    for l in reversed(range(DEPTH)):
        h_in, hb, p_abc, p_m, br, states = acts[l]
        d_br, d_m, d_gpost, g_wbr, g_wout = merge_bwd(d_h, br, p_m, w_br, w_out, norm_post[l][None], l, g_wbo)
        g_wbo = (g_wbr, g_wout)
        d_abc, d_rb, d_sk, d_cw = mixers_bwd(p_abc, d_br, states, cosf, sinf, bkt, rel_bias,
                                             attn_sinks[l][None], conv_w_full[l], nb, nc)
        d_h, d_gpre = proj_dgrad(d_abc, d_m, w_in, l, h_in, norm_pre[l][None], d_h)
        g_win = proj_wgrad(hb, d_abc, d_m, l, g_win)
        small[l] = (d_gpre[0], d_gpost[0], d_rb, d_sk, d_cw[0:3])

    d_h3 = d_h.reshape(nb, lp, D_MODEL)
    d_x = d_h3[:, BLOCK:]
    d_meta = d_h3[:, PAD_FRONT:BLOCK]
    sp = jnp.concatenate([
        d_meta.reshape(-1, LANES),
        jnp.stack([small[l][0] for l in range(DEPTH)]).reshape(-1, LANES),
        jnp.stack([small[l][1] for l in range(DEPTH)]).reshape(-1, LANES),
        jnp.concatenate([small[l][2] for l in range(DEPTH)], axis=0),
        jnp.concatenate([small[l][3] for l in range(DEPTH)], axis=0),
        jnp.stack([small[l][4] for l in range(DEPTH)]).reshape(-1, LANES),
        loss_part], axis=0)
    return d_x, g_win, g_wbo[0], g_wbo[1], sp


def kernel(x, meta_tokens, rel_bias, norm_pre, w_in, conv_w, attn_sinks, w_branch, w_out, norm_post, loss_target, m_meta_tokens, m_rel_bias, m_norm_pre, m_w_in, m_conv_w, m_attn_sinks, m_w_branch, m_w_out, m_norm_post, v_meta_tokens, v_rel_bias, v_norm_pre, v_w_in, v_conv_w, v_attn_sinks, v_w_branch, v_w_out, v_norm_post):
    assert x.shape[0] == 2 and SP_META == 2 * N_META * D_MODEL // LANES
    px, py, pc = _place()
    chip = 2 * px + py

    shards = [jnp.swapaxes(w_in, 1, 2).astype(BF16), w_branch.astype(BF16), w_out.astype(BF16)]
    a_in, a_br, a_out = gather_weight_shards([_own_slot(s, chip) for s in shards])
    weights = (a_in.reshape(DEPTH, PROJ_WIDTH, D_MODEL), a_br, a_out.reshape(DEPTH, D_MODEL, D_MODEL))
    side = jnp.concatenate([meta_tokens.reshape(-1), conv_w.reshape(-1)]).reshape(-1, LANES)
    side = jnp.concatenate([side, jnp.zeros((40 - side.shape[0], LANES), F32)], axis=0)
    side_all = exchange_small(side)
    side_chips = side_all[0::2]
    n_meta_rows = N_META * SHARD_D // LANES
    meta_full = jnp.moveaxis(side_chips[:, :n_meta_rows].reshape(N_CHIPS, N_META, SHARD_D), 0, 1).reshape(N_META, D_MODEL)
    conv_full = jnp.moveaxis(side_chips[:, n_meta_rows:n_meta_rows + 6].reshape(N_CHIPS, DEPTH, 3, LANES), 0, 2).reshape(DEPTH, 3, BRANCH_WIDTH)

    d_x, g_win, g_wbr, g_wout, sp = local_step(x, loss_target, meta_full, rel_bias, norm_pre, conv_full, attn_sinks,
                                               norm_post, weights)

    full = [g_win.reshape(DEPTH, N_CHIPS, SHARD_IN, D_MODEL),
            g_wbr.reshape(DEPTH, N_CHIPS, N_BRANCH * BRANCH_WIDTH, SHARD_D),
            g_wout.reshape(DEPTH, N_CHIPS, SHARD_D, D_MODEL)]
    others = sibling_swap_layers(full)
    c_arr = jnp.reshape(pc, (1,)).astype(jnp.int32)
    partials = [add_own_layer(g, o, c_arr) for g, o in zip(full, others)]
    parts = scatter_to_chips(partials)
    where = jnp.stack([chip, pc]).astype(jnp.int32)
    r_in, r_br, r_out = sibling_share_layer([sum_chips(a, p, where) for a, p in zip(partials, parts)])
    g_in = jnp.swapaxes(r_in, 1, 2)
    g_br = r_br.reshape(w_branch.shape)
    g_out = r_out

    meta_rows, rest = sum_small(exchange_small(sp))
    o = 0
    g_meta_full = meta_rows.reshape(N_META, D_MODEL)
    g_norm_pre = rest[o:o + SP_NORM].reshape(DEPTH, D_MODEL); o += SP_NORM
    g_norm_post = rest[o:o + SP_NORM].reshape(DEPTH, D_MODEL); o += SP_NORM
    g_rel_bias = rest[o:o + N_BUCKETS, :ATT_HEADS]; o += SP_RB
    g_sinks = rest[o:o + SP_SINK, 0].reshape(DEPTH, ATT_HEADS); o += SP_SINK
    g_conv_full = rest[o:o + SP_CONV].reshape(DEPTH, 3, BRANCH_WIDTH); o += SP_CONV
    loss = rest[o, 0]
    g_meta = lax.dynamic_slice_in_dim(g_meta_full, chip * SHARD_D, SHARD_D, axis=1)
    g_conv = lax.dynamic_slice_in_dim(g_conv_full, chip * LANES, LANES, axis=2)

    u_in = adamw_big(w_in, g_in, m_w_in, v_w_in)
    u_br = adamw_big(w_branch, g_br, m_w_branch, v_w_branch)
    u_out = adamw_big(w_out, g_out, m_w_out, v_w_out)
    to2 = lambda a: a.reshape(-1, a.shape[-1])
    smalls = [(meta_tokens, g_meta, m_meta_tokens, v_meta_tokens),
              (rel_bias, g_rel_bias, m_rel_bias, v_rel_bias),
              (norm_pre, g_norm_pre, m_norm_pre, v_norm_pre),
              (to2(conv_w), to2(g_conv), to2(m_conv_w), to2(v_conv_w)),
              (attn_sinks, g_sinks, m_attn_sinks, v_attn_sinks),
              (norm_post, g_norm_post, m_norm_post, v_norm_post)]
    u_meta, u_rb, u_npre, u_conv, u_sink, u_npost = adamw_small(smalls)
    u_conv = tuple(a.reshape(conv_w.shape) for a in u_conv)

    grads = [g_meta, g_rel_bias, g_norm_pre, g_in, g_conv, g_sinks, g_br, g_out, g_norm_post]
    upd = [u_meta, u_rb, u_npre, u_in, u_conv, u_sink, u_br, u_out, u_npost]
    return (loss, d_x, *grads, *[u[0] for u in upd], *[u[1] for u in upd], *[u[2] for u in upd])
```

```python
import functools
import math

import numpy as np
import jax
import jax.numpy as jnp
from jax import lax
from jax.experimental import pallas as pl
from jax.experimental.pallas import tpu as pltpu

F32 = jnp.float32
BF16 = jnp.bfloat16
MESH = pl.DeviceIdType.MESH

D_MODEL = 1024
DEPTH = 2
N_META = 16
BLOCK = 128
PAD_FRONT = BLOCK - N_META
ATT_HEADS = 8
ATT_HEAD_DIM = 64
N_BUCKETS = 32
MAX_EXACT = 16
MAX_DISTANCE = 128
RET_HEADS = 4
ROT_BASE = 10000.0
N_BRANCH = 3
BRANCH_WIDTH = 512
PROJ_WIDTH = 8448
ABC_WIDTH = 5376
MERGE_WIDTH = N_BRANCH * D_MODEL
RMS_EPS = 1e-6
GN_EPS = 1e-6
NEG_INF = -1e30
ATT_SCALE = ATT_HEAD_DIM ** -0.5
RET_SCALE = BLOCK ** -0.5
LOG_GAMMA = tuple(math.log1p(-(2.0 ** (-5.0 - h))) for h in range(RET_HEADS))

C_AQ, C_AK, C_AV, C_AG = 0, 512, 640, 768
C_RQ, C_RK, C_RV, C_RG = 1280, 1792, 2304, 2816
C_CB, C_CC, C_CX, C_CG = 3328, 3840, 4352, 4864

ADAM_LR = 0.001
ADAM_B1 = 0.9
ADAM_B2 = 0.999
ADAM_EPS = 1e-08
ADAM_WD = 0.01
ADAM_STEP = 10

N_CHIPS = 4
SHARD_IN = PROJ_WIDTH // N_CHIPS
SHARD_D = D_MODEL // N_CHIPS
LANES = 128
PACK_IN = D_MODEL * SHARD_IN
PACK_BR = N_BRANCH * BRANCH_WIDTH * SHARD_D
PACK_OUT = SHARD_D * D_MODEL
PACK_ROWS = (PACK_IN + PACK_BR + PACK_OUT) // LANES

VMEM_LIMIT = 56 * 1024 * 1024
COL_TILE = 768
ROW_TILE = 1088


def _cparams(*sem):
    return pltpu.CompilerParams(dimension_semantics=sem, vmem_limit_bytes=VMEM_LIMIT)


def _nt(a, b):
    return lax.dot_general(a, b, (((1,), (1,)), ((), ())), preferred_element_type=F32)


def _tn(a, b):
    return lax.dot_general(a, b, (((0,), (0,)), ((), ())), preferred_element_type=F32)


def _nn(a, b):
    return jnp.dot(a, b, preferred_element_type=F32)


def _silu(x):
    return x * jax.nn.sigmoid(x)


def _dsilu(x):
    s = jax.nn.sigmoid(x)
    return s * (1.0 + x * (1.0 - s))


def _bucket_table():
    r = np.arange(BLOCK)[:, None]
    c = np.arange(2 * BLOCK)[None, :]
    n = np.maximum(BLOCK + r - c, 0)
    nf = np.maximum(n, 1).astype(np.float32)
    large = MAX_EXACT + (np.log(nf / MAX_EXACT) / math.log(MAX_DISTANCE / MAX_EXACT)
                         * (N_BUCKETS - MAX_EXACT)).astype(np.int32)
    large = np.minimum(large, N_BUCKETS - 1)
    return np.where(n < MAX_EXACT, n, large).astype(np.int32)


def _rot_tables(lp):
    half = BLOCK // 2
    pos = (jnp.arange(lp) - PAD_FRONT).astype(F32)
    theta = 1.0 / (ROT_BASE ** jnp.linspace(0.0, 1.0, half, dtype=F32))
    ang = pos[:, None] * theta[None, :]
    cos, sin = jnp.cos(ang), jnp.sin(ang)
    return jnp.concatenate([cos, cos], axis=1), jnp.concatenate([-sin, sin], axis=1)


def norm_matmul(x2d, g, w, layer, col0_blocks, n_col_blocks):
    t = x2d.shape[0]
    tm = ROW_TILE if t % ROW_TILE == 0 else BLOCK

    def body(x_ref, g_ref, w_ref, hb_ref, o_ref):
        @pl.when(pl.program_id(1) == 0)
        def _():
            x = x_ref[...]
            r = lax.rsqrt(jnp.mean(x * x, axis=-1, keepdims=True) + RMS_EPS)
            hb_ref[...] = (x * r * g_ref[...]).astype(BF16)

        o_ref[...] = _nt(hb_ref[...], w_ref[...])

    return pl.pallas_call(
        body, name="norm_matmul",
        grid=(t // tm, n_col_blocks),
        in_specs=[pl.BlockSpec((tm, D_MODEL), lambda i, j: (i, 0)),
                  pl.BlockSpec((1, D_MODEL), lambda i, j: (0, 0)),
                  pl.BlockSpec((None, COL_TILE, D_MODEL), lambda i, j: (layer, j + col0_blocks, 0))],
        out_specs=[pl.BlockSpec((tm, D_MODEL), lambda i, j: (i, 0)),
                   pl.BlockSpec((tm, COL_TILE), lambda i, j: (i, j))],
        out_shape=[jax.ShapeDtypeStruct((t, D_MODEL), BF16),
                   jax.ShapeDtypeStruct((t, n_col_blocks * COL_TILE), F32)],
        compiler_params=_cparams("parallel", "arbitrary"),
    )(x2d, g, w)


def matmul_cols(a, w, layer, col0_blocks, n_col_blocks):
    t, k = a.shape
    tm = ROW_TILE if t % ROW_TILE == 0 else BLOCK

    def body(a_ref, w_ref, o_ref):
        o_ref[...] = _nt(a_ref[...], w_ref[...])

    return pl.pallas_call(
        body, name="matmul_cols",
        grid=(t // tm, n_col_blocks),
        in_specs=[pl.BlockSpec((tm, k), lambda i, j: (i, 0)),
                  pl.BlockSpec((None, COL_TILE, k), lambda i, j: (layer, j + col0_blocks, 0))],
        out_specs=pl.BlockSpec((tm, COL_TILE), lambda i, j: (i, j)),
        out_shape=jax.ShapeDtypeStruct((t, n_col_blocks * COL_TILE), F32),
        compiler_params=_cparams("parallel", "arbitrary"),
    )(a, w)


def _build_bias(bkt_ref, rb_ref, bias_s):
    bkt = bkt_ref[...]
    for h in range(ATT_HEADS):
        acc = jnp.zeros((BLOCK, 2 * BLOCK), F32)
        for b in range(N_BUCKETS):
            acc = jnp.where(bkt == b, rb_ref[b, h], acc)
        bias_s[h] = acc


def _band_mask(n):
    r = lax.broadcasted_iota(jnp.int32, (BLOCK, 2 * BLOCK), 0)
    c = lax.broadcasted_iota(jnp.int32, (BLOCK, 2 * BLOCK), 1)
    key_pos = (n - 1) * BLOCK + c
    return (c > r) & (c <= r + BLOCK) & (key_pos >= PAD_FRONT)


def _split_heads(kv, kh):
    lane = lax.broadcasted_iota(jnp.int32, kv.shape, 1)
    if kh == 0:
        lo = jnp.where(lane < ATT_HEAD_DIM, kv, 0.0)
        hi = pltpu.roll(lo, ATT_HEAD_DIM, 1)
    else:
        hi = jnp.where(lane >= ATT_HEAD_DIM, kv, 0.0)
        lo = pltpu.roll(hi, ATT_HEAD_DIM, 1)
    return lo, hi


def _merge_heads(acc_lo, acc_hi, kh):
    lane = lax.broadcasted_iota(jnp.int32, acc_lo.shape, 1)
    if kh == 0:
        return jnp.where(lane < ATT_HEAD_DIM, acc_lo + pltpu.roll(acc_hi, ATT_HEAD_DIM, 1), 0.0)
    return jnp.where(lane >= ATT_HEAD_DIM, acc_hi + pltpu.roll(acc_lo, ATT_HEAD_DIM, 1), 0.0)


def _softmax_sink(q2b, kxb, bias_h, mask, sink_h):
    s = _nt(q2b, kxb) * ATT_SCALE + bias_h
    s = jnp.where(mask, s, NEG_INF)
    m = jnp.maximum(jnp.max(s, axis=-1, keepdims=True), sink_h)
    p = jnp.exp(s - m)
    es = jnp.exp(sink_h - m)
    inv = 1.0 / (jnp.sum(p, axis=-1, keepdims=True) + es)
    return p * inv, es * inv


def _rot(t, cosf, sinf):
    return t * cosf + pltpu.roll(t, BLOCK // 2, 1) * sinf


def _rot_t(d, cosf, sinf):
    return d * cosf + pltpu.roll(d * sinf, BLOCK // 2, 1)


def _decay_tables(h):
    lg = LOG_GAMMA[h]
    i = lax.broadcasted_iota(jnp.int32, (BLOCK, BLOCK), 0)
    j = lax.broadcasted_iota(jnp.int32, (BLOCK, BLOCK), 1)
    diff = (i - j).astype(F32)
    dm = jnp.where(diff >= 0, jnp.exp(diff * lg), 0.0)
    row = lax.broadcasted_iota(jnp.int32, (BLOCK, 1), 0).astype(F32)
    zeta = jnp.exp((BLOCK - 1 - row) * lg)
    xi = jnp.exp((row + 1.0) * lg)
    return dm, zeta, xi, math.exp(BLOCK * lg)


def _valid_col(n):
    row = lax.broadcasted_iota(jnp.int32, (BLOCK, 1), 0)
    return ((n * BLOCK + row) >= PAD_FRONT).astype(F32)


def _shift_down(cur, prev, k):
    row = lax.broadcasted_iota(jnp.int32, cur.shape, 0)
    return jnp.where(row >= k, pltpu.roll(cur, k, 0), pltpu.roll(prev, k, 0))


def _shift_up(cur, nxt, k):
    row = lax.broadcasted_iota(jnp.int32, cur.shape, 0)
    return jnp.where(row < BLOCK - k, pltpu.roll(cur, BLOCK - k, 0), pltpu.roll(nxt, BLOCK - k, 0))


def mixers_fwd(proj, cosf, sinf, bkt, rel_bias, sinks, conv_w, nb, nc):
    def body(p_ref, cos_ref, sin_ref, bkt_ref, rb_ref, sk_ref, cw_ref, br_ref, st_ref,
             bias_s, kv_s, state_s, u_s):
        b = pl.program_id(0)
        n = pl.program_id(1)

        @pl.when((b == 0) & (n == 0))
        def _():
            _build_bias(bkt_ref, rb_ref, bias_s)

        @pl.when(n == 0)
        def _():
            kv_s[0:BLOCK, :] = jnp.zeros((BLOCK, 2 * BLOCK), F32)
            state_s[...] = jnp.zeros_like(state_s)
            u_s[...] = jnp.zeros_like(u_s)

        valid = _valid_col(n)
        mask = _band_mask(n)

        kv_s[BLOCK:2 * BLOCK, :] = p_ref[:, C_AK:C_AK + 2 * BLOCK]
        k_all = kv_s[:, 0:BLOCK]
        v_all = kv_s[:, BLOCK:2 * BLOCK]
        for kh in range(2):
            k_lo, k_hi = [t.astype(BF16) for t in _split_heads(k_all, kh)]
            v_lo, v_hi = [t.astype(BF16) for t in _split_heads(v_all, kh)]
            for jj in range(2):
                j = 2 * kh + jj
                q2b = p_ref[:, C_AQ + BLOCK * j:C_AQ + BLOCK * (j + 1)].astype(BF16)
                p_lo, _ = _softmax_sink(q2b, k_lo, bias_s[2 * j], mask, sk_ref[0, 2 * j])
                p_hi, _ = _softmax_sink(q2b, k_hi, bias_s[2 * j + 1], mask, sk_ref[0, 2 * j + 1])
                o2 = _nn(p_lo.astype(BF16), v_lo) + _nn(p_hi.astype(BF16), v_hi)
                gate = p_ref[:, C_AG + BLOCK * j:C_AG + BLOCK * (j + 1)]
                br_ref[:, BLOCK * j:BLOCK * (j + 1)] = (o2 * _silu(gate)).astype(BF16)
        kv_s[0:BLOCK, :] = kv_s[BLOCK:2 * BLOCK, :]

        cosv = cos_ref[...]
        sinv = sin_ref[...]
        for h in range(RET_HEADS):
            dm, zeta, xi, gamma_chunk = _decay_tables(h)
            q = _rot(p_ref[:, C_RQ + BLOCK * h:C_RQ + BLOCK * (h + 1)], cosv, sinv).astype(BF16)
            k = (_rot(p_ref[:, C_RK + BLOCK * h:C_RK + BLOCK * (h + 1)], cosv, sinv)
                 * RET_SCALE * valid).astype(BF16)
            v = p_ref[:, C_RV + BLOCK * h:C_RV + BLOCK * (h + 1)]
            s_prev = state_s[h]
            st_ref[0, 0, h] = s_prev
            a = (_nt(q, k) * dm).astype(BF16)
            o = _nn(a, v.astype(BF16)) + xi * _nn(q, s_prev.astype(BF16))
            mu = jnp.mean(o, axis=-1, keepdims=True)
            var = jnp.mean(jnp.square(o - mu), axis=-1, keepdims=True)
            oh = (o - mu) * lax.rsqrt(var + GN_EPS)
            gate = p_ref[:, C_RG + BLOCK * h:C_RG + BLOCK * (h + 1)]
            br_ref[:, BRANCH_WIDTH + BLOCK * h:BRANCH_WIDTH + BLOCK * (h + 1)] = (oh * _silu(gate)).astype(BF16)
            state_s[h] = gamma_chunk * s_prev + _tn(k, (v * zeta).astype(BF16))

        u = p_ref[:, C_CC:C_CC + BRANCH_WIDTH] * p_ref[:, C_CX:C_CX + BRANCH_WIDTH] * valid
        u_prev = u_s[...]
        y = (cw_ref[0:1, :] * _shift_down(u, u_prev, 2) + cw_ref[1:2, :] * _shift_down(u, u_prev, 1)
             + cw_ref[2:3, :] * u)
        yc = p_ref[:, C_CB:C_CB + BRANCH_WIDTH] * y * _silu(p_ref[:, C_CG:C_CG + BRANCH_WIDTH])
        br_ref[:, 2 * BRANCH_WIDTH:3 * BRANCH_WIDTH] = yc.astype(BF16)
        u_s[...] = u

    rows = nb * nc * BLOCK
    smem = pl.BlockSpec(memory_space=pltpu.SMEM)
    return pl.pallas_call(
        body, name="mixers_fwd",
        grid=(nb, nc),
        in_specs=[pl.BlockSpec((BLOCK, ABC_WIDTH), lambda b, n: (b * nc + n, 0)),
                  pl.BlockSpec((BLOCK, BLOCK), lambda b, n: (n, 0)),
                  pl.BlockSpec((BLOCK, BLOCK), lambda b, n: (n, 0)),
                  pl.BlockSpec((BLOCK, 2 * BLOCK), lambda b, n: (0, 0)),
                  smem, smem,
                  pl.BlockSpec((3, BRANCH_WIDTH), lambda b, n: (0, 0))],
        out_specs=[pl.BlockSpec((BLOCK, N_BRANCH * BRANCH_WIDTH), lambda b, n: (b * nc + n, 0)),
                   pl.BlockSpec((1, 1, RET_HEADS, BLOCK, BLOCK), lambda b, n: (b, n, 0, 0, 0))],
        out_shape=[jax.ShapeDtypeStruct((rows, N_BRANCH * BRANCH_WIDTH), BF16),
                   jax.ShapeDtypeStruct((nb, nc, RET_HEADS, BLOCK, BLOCK), F32)],
        scratch_shapes=[pltpu.VMEM((ATT_HEADS, BLOCK, 2 * BLOCK), F32),
                        pltpu.VMEM((2 * BLOCK, 2 * BLOCK), F32),
                        pltpu.VMEM((RET_HEADS, BLOCK, BLOCK), F32),
                        pltpu.VMEM((BLOCK, BRANCH_WIDTH), F32)],
        compiler_params=_cparams("arbitrary", "arbitrary"),
    )(proj, cosf, sinf, bkt, rel_bias, sinks, conv_w)


def mixers_bwd(proj, d_br, states, cosf, sinf, bkt, rel_bias, sinks, conv_w, nb, nc):
    def body(p_ref, kvp_ref, cp_ref, dbr_ref, st_ref, cos_ref, sin_ref, bkt_ref, rb_ref, sk_ref, cw_ref,
             dp_ref, drb_ref, dsk_ref, dcw_ref,
             bias_s, dbias_s, dkv_s, g_s, dy_s):
        b = pl.program_id(0)
        step = pl.program_id(1)
        n = nc - 1 - step
        first = (b == 0) & (step == 0)
        last = (b == nb - 1) & (step == nc - 1)

        @pl.when(first)
        def _():
            _build_bias(bkt_ref, rb_ref, bias_s)
            dbias_s[...] = jnp.zeros_like(dbias_s)
            dsk_ref[...] = jnp.zeros_like(dsk_ref)
            dcw_ref[...] = jnp.zeros_like(dcw_ref)
            drb_ref[...] = jnp.zeros_like(drb_ref)

        @pl.when(step == 0)
        def _():
            dkv_s[...] = jnp.zeros_like(dkv_s)
            g_s[...] = jnp.zeros_like(g_s)
            dy_s[...] = jnp.zeros_like(dy_s)

        valid = _valid_col(n)
        mask = _band_mask(n)
        has_prev = (n > 0).astype(F32)

        kv_prev = kvp_ref[...] * has_prev
        kv_cur = p_ref[:, C_AK:C_AK + 2 * BLOCK]
        k_all = jnp.concatenate([kv_prev[:, 0:BLOCK], kv_cur[:, 0:BLOCK]], axis=0)
        v_all = jnp.concatenate([kv_prev[:, BLOCK:], kv_cur[:, BLOCK:]], axis=0)
        dk_tot = jnp.zeros((2 * BLOCK, BLOCK), F32)
        dv_tot = jnp.zeros((2 * BLOCK, BLOCK), F32)
        for kh in range(2):
            ks = [t.astype(BF16) for t in _split_heads(k_all, kh)]
            vs = [t.astype(BF16) for t in _split_heads(v_all, kh)]
            dk_acc = [jnp.zeros((2 * BLOCK, BLOCK), F32), jnp.zeros((2 * BLOCK, BLOCK), F32)]
            dv_acc = [jnp.zeros((2 * BLOCK, BLOCK), F32), jnp.zeros((2 * BLOCK, BLOCK), F32)]
            for jj in range(2):
                j = 2 * kh + jj
                q2b = p_ref[:, C_AQ + BLOCK * j:C_AQ + BLOCK * (j + 1)].astype(BF16)
                gate = p_ref[:, C_AG + BLOCK * j:C_AG + BLOCK * (j + 1)]
                d_ya = dbr_ref[:, BLOCK * j:BLOCK * (j + 1)]
                do2 = (d_ya * _silu(gate)).astype(BF16)
                o2 = jnp.zeros((BLOCK, BLOCK), F32)
                dq2 = jnp.zeros((BLOCK, BLOCK), F32)
                for x in range(2):
                    h = 2 * j + x
                    p, p_sink = _softmax_sink(q2b, ks[x], bias_s[h], mask, sk_ref[0, h])
                    pb = p.astype(BF16)
                    o2 = o2 + _nn(pb, vs[x])
                    dp = _nt(do2, vs[x])
                    delta = jnp.sum(p * dp, axis=-1, keepdims=True)
                    ds = p * (dp - delta)
                    dbias_s[h] += ds
                    dsk_ref[h:h + 1, :] += jnp.broadcast_to(
                        jnp.sum(-p_sink * delta, axis=0, keepdims=True), (1, BLOCK))
                    dsb = ds.astype(BF16)
                    dq2 = dq2 + _nn(dsb, ks[x]) * ATT_SCALE
                    dk_acc[x] = dk_acc[x] + _tn(dsb, q2b) * ATT_SCALE
                    dv_acc[x] = dv_acc[x] + _tn(pb, do2)
                dp_ref[:, C_AQ + BLOCK * j:C_AQ + BLOCK * (j + 1)] = dq2.astype(BF16)
                dp_ref[:, C_AG + BLOCK * j:C_AG + BLOCK * (j + 1)] = (d_ya * o2 * _dsilu(gate)).astype(BF16)
            dk_tot = dk_tot + _merge_heads(dk_acc[0], dk_acc[1], kh)
            dv_tot = dv_tot + _merge_heads(dv_acc[0], dv_acc[1], kh)
        dp_ref[:, C_AK:C_AK + BLOCK] = (dk_tot[BLOCK:, :] + dkv_s[:, 0:BLOCK]).astype(BF16)
        dp_ref[:, C_AV:C_AV + BLOCK] = (dv_tot[BLOCK:, :] + dkv_s[:, BLOCK:]).astype(BF16)
        dkv_s[:, 0:BLOCK] = dk_tot[0:BLOCK, :]
        dkv_s[:, BLOCK:] = dv_tot[0:BLOCK, :]

        cosv = cos_ref[...]
        sinv = sin_ref[...]
        for h in range(RET_HEADS):
            dm, zeta, xi, gamma_chunk = _decay_tables(h)
            sl = lambda c0: slice(c0 + BLOCK * h, c0 + BLOCK * (h + 1))
            q = _rot(p_ref[:, sl(C_RQ)], cosv, sinv).astype(BF16)
            k = (_rot(p_ref[:, sl(C_RK)], cosv, sinv) * RET_SCALE * valid).astype(BF16)
            v = p_ref[:, sl(C_RV)]
            vb = v.astype(BF16)
            gate = p_ref[:, sl(C_RG)]
            s_prev = st_ref[0, 0, h].astype(BF16)
            g_next = g_s[h]
            a = (_nt(q, k) * dm).astype(BF16)
            o = _nn(a, vb) + xi * _nn(q, s_prev)
            mu = jnp.mean(o, axis=-1, keepdims=True)
            var = jnp.mean(jnp.square(o - mu), axis=-1, keepdims=True)
            rstd = lax.rsqrt(var + GN_EPS)
            oh = (o - mu) * rstd
            d_yr = dbr_ref[:, BRANCH_WIDTH + BLOCK * h:BRANCH_WIDTH + BLOCK * (h + 1)]
            dp_ref[:, sl(C_RG)] = (d_yr * oh * _dsilu(gate)).astype(BF16)
            doh = d_yr * _silu(gate)
            do = rstd * (doh - jnp.mean(doh, axis=-1, keepdims=True)
                         - oh * jnp.mean(doh * oh, axis=-1, keepdims=True))
            dob = do.astype(BF16)
            dxo = (do * xi).astype(BF16)
            da = (_nt(dob, vb) * dm).astype(BF16)
            gb = g_next.astype(BF16)
            zv = (v * zeta).astype(BF16)
            dq = _nn(da, k) + _nt(dxo, s_prev)
            dk = _tn(da, q) + _nt(zv, gb)
            dv = _tn(a, dob) + zeta * _nn(k, gb)
            g_s[h] = gamma_chunk * g_next + _tn(q, dxo)
            dp_ref[:, sl(C_RQ)] = _rot_t(dq, cosv, sinv).astype(BF16)
            dp_ref[:, sl(C_RK)] = _rot_t(dk * (RET_SCALE * valid), cosv, sinv).astype(BF16)
            dp_ref[:, sl(C_RV)] = dv.astype(BF16)

        w0, w1, w2 = cw_ref[0:1, :], cw_ref[1:2, :], cw_ref[2:3, :]
        cb = p_ref[:, C_CB:C_CB + BRANCH_WIDTH]
        cc = p_ref[:, C_CC:C_CC + BRANCH_WIDTH]
        cx = p_ref[:, C_CX:C_CX + BRANCH_WIDTH]
        cg = p_ref[:, C_CG:C_CG + BRANCH_WIDTH]
        u = cc * cx * valid
        u_prev = cp_ref[:, 0:BRANCH_WIDTH] * cp_ref[:, BRANCH_WIDTH:2 * BRANCH_WIDTH] * (_valid_col(n - 1) * has_prev)
        u1 = _shift_down(u, u_prev, 1)
        u2 = _shift_down(u, u_prev, 2)
        y = w0 * u2 + w1 * u1 + w2 * u
        d_yc = dbr_ref[:, 2 * BRANCH_WIDTH:3 * BRANCH_WIDTH]
        sg = _silu(cg)
        dp_ref[:, C_CB:C_CB + BRANCH_WIDTH] = (d_yc * y * sg).astype(BF16)
        dp_ref[:, C_CG:C_CG + BRANCH_WIDTH] = (d_yc * cb * y * _dsilu(cg)).astype(BF16)
        dy = d_yc * cb * sg
        dy_next = dy_s[...]
        du = (w2 * dy + w1 * _shift_up(dy, dy_next, 1) + w0 * _shift_up(dy, dy_next, 2)) * valid
        dp_ref[:, C_CC:C_CC + BRANCH_WIDTH] = (du * cx).astype(BF16)
        dp_ref[:, C_CX:C_CX + BRANCH_WIDTH] = (du * cc).astype(BF16)
        dcw_ref[0:1, :] += jnp.sum(dy * u2, axis=0, keepdims=True)
        dcw_ref[1:2, :] += jnp.sum(dy * u1, axis=0, keepdims=True)
        dcw_ref[2:3, :] += jnp.sum(dy * u, axis=0, keepdims=True)
        dy_s[...] = dy

        @pl.when(last)
        def _():
            bkt = bkt_ref[...]
            row = lax.broadcasted_iota(jnp.int32, (N_BUCKETS, BLOCK), 0)
            lane = lax.broadcasted_iota(jnp.int32, (N_BUCKETS, BLOCK), 1)

            def one_bucket(bk, acc):
                sel = bkt == bk
                for h in range(ATT_HEADS):
                    t = jnp.where(sel, dbias_s[h], 0.0)
                    s = jnp.sum(jnp.sum(t, axis=1, keepdims=True), axis=0, keepdims=True)
                    acc = acc + jnp.where((row == bk) & (lane == h), jnp.broadcast_to(s, acc.shape), 0.0)
                return acc

            drb_ref[...] = lax.fori_loop(0, N_BUCKETS, one_bucket, jnp.zeros((N_BUCKETS, BLOCK), F32))

    rows = nb * nc * BLOCK
    smem = pl.BlockSpec(memory_space=pltpu.SMEM)
    blk = lambda b, s: b * nc + (nc - 1 - s)
    prev = lambda b, s: b * nc + jnp.maximum(nc - 2 - s, 0)
    return pl.pallas_call(
        body, name="mixers_bwd",
        grid=(nb, nc),
        in_specs=[pl.BlockSpec((BLOCK, ABC_WIDTH), lambda b, s: (blk(b, s), 0)),
                  pl.BlockSpec((BLOCK, 2 * BLOCK), lambda b, s: (prev(b, s), C_AK // (2 * BLOCK))),
                  pl.BlockSpec((BLOCK, 1280), lambda b, s: (prev(b, s), C_CC // 1280)),
                  pl.BlockSpec((BLOCK, N_BRANCH * BRANCH_WIDTH), lambda b, s: (blk(b, s), 0)),
                  pl.BlockSpec((1, 1, RET_HEADS, BLOCK, BLOCK), lambda b, s: (b, nc - 1 - s, 0, 0, 0)),
                  pl.BlockSpec((BLOCK, BLOCK), lambda b, s: (nc - 1 - s, 0)),
                  pl.BlockSpec((BLOCK, BLOCK), lambda b, s: (nc - 1 - s, 0)),
                  pl.BlockSpec((BLOCK, 2 * BLOCK), lambda b, s: (0, 0)),
                  smem, smem,
                  pl.BlockSpec((3, BRANCH_WIDTH), lambda b, s: (0, 0))],
        out_specs=[pl.BlockSpec((BLOCK, ABC_WIDTH), lambda b, s: (blk(b, s), 0)),
                   pl.BlockSpec((N_BUCKETS, BLOCK), lambda b, s: (0, 0)),
                   pl.BlockSpec((ATT_HEADS, BLOCK), lambda b, s: (0, 0)),
                   pl.BlockSpec((8, BRANCH_WIDTH), lambda b, s: (0, 0))],
        out_shape=[jax.ShapeDtypeStruct((rows, ABC_WIDTH), BF16),
                   jax.ShapeDtypeStruct((N_BUCKETS, BLOCK), F32),
                   jax.ShapeDtypeStruct((ATT_HEADS, BLOCK), F32),
                   jax.ShapeDtypeStruct((8, BRANCH_WIDTH), F32)],
        scratch_shapes=[pltpu.VMEM((ATT_HEADS, BLOCK, 2 * BLOCK), F32),
                        pltpu.VMEM((ATT_HEADS, BLOCK, 2 * BLOCK), F32),
                        pltpu.VMEM((BLOCK, 2 * BLOCK), F32),
                        pltpu.VMEM((RET_HEADS, BLOCK, BLOCK), F32),
                        pltpu.VMEM((BLOCK, BRANCH_WIDTH), F32)],
        compiler_params=_cparams("arbitrary", "arbitrary"),
    )(proj, proj, proj, d_br, states, cosf, sinf, bkt, rel_bias, sinks, conv_w)


MERGE_TILE = 256


def _merge_forward(br_ref, m_ref, wb_ref, wo_ref):
    bo, gates = [], []
    mixed_pre = None
    for g in range(N_BRANCH):
        br_g = br_ref[:, BRANCH_WIDTH * g:BRANCH_WIDTH * (g + 1)]
        bo_g = jnp.concatenate([_nn(br_g, wb_ref[p, g]) for p in range(N_CHIPS)], axis=1)
        gate_g = jax.nn.sigmoid(m_ref[:, D_MODEL * g:D_MODEL * (g + 1)])
        bo.append(bo_g)
        gates.append(gate_g)
        mixed_pre = gate_g * bo_g if mixed_pre is None else mixed_pre + gate_g * bo_g
    mixed = _nn(mixed_pre.astype(BF16), wo_ref[...])
    r = lax.rsqrt(jnp.mean(mixed * mixed, axis=-1, keepdims=True) + RMS_EPS)
    return bo, gates, mixed_pre, mixed, r


def merge_fwd(x2d, br, pm, wb, wo, g_post, layer):
    t = x2d.shape[0]
    tm = MERGE_TILE if t % MERGE_TILE == 0 else BLOCK

    def body(x_ref, br_ref, m_ref, wb_ref, wo_ref, g_ref, o_ref):
        _, _, _, mixed, r = _merge_forward(br_ref, m_ref, wb_ref, wo_ref)
        o_ref[...] = x_ref[...] + mixed * r * g_ref[...]

    return pl.pallas_call(
        body, name="merge_fwd",
        grid=(t // tm,),
        in_specs=[pl.BlockSpec((tm, D_MODEL), lambda i: (i, 0)),
                  pl.BlockSpec((tm, N_BRANCH * BRANCH_WIDTH), lambda i: (i, 0)),
                  pl.BlockSpec((tm, MERGE_WIDTH), lambda i: (i, 0)),
                  pl.BlockSpec((None, N_CHIPS, N_BRANCH, BRANCH_WIDTH, SHARD_D), lambda i: (layer, 0, 0, 0, 0)),
                  pl.BlockSpec((None, D_MODEL, D_MODEL), lambda i: (layer, 0, 0)),
                  pl.BlockSpec((1, D_MODEL), lambda i: (0, 0))],
        out_specs=pl.BlockSpec((tm, D_MODEL), lambda i: (i, 0)),
        out_shape=jax.ShapeDtypeStruct((t, D_MODEL), F32),
        compiler_params=_cparams("parallel"),
    )(x2d, br, pm, wb, wo, g_post)


def merge_bwd(d_out, br, pm, wb, wo, g_post, layer, acc):
    t = d_out.shape[0]
    tm = MERGE_TILE if t % MERGE_TILE == 0 else BLOCK

    def body(*refs):
        do_ref, br_ref, m_ref, wb_ref, wo_ref, g_ref = refs[:6]
        dbr_ref, dm_ref, dg_ref, dwb_ref, dwo_ref = refs[-5:]

        @pl.when(pl.program_id(0) == 0)
        def _():
            dwb_ref[...] = jnp.zeros_like(dwb_ref)
            dwo_ref[...] = jnp.zeros_like(dwo_ref)
            dg_ref[...] = jnp.zeros_like(dg_ref)

        bo, gates, mixed_pre, mixed, r = _merge_forward(br_ref, m_ref, wb_ref, wo_ref)
        d_o = do_ref[...]
        nh = mixed * r
        dg_ref[0:1, :] += jnp.sum(d_o * nh, axis=0, keepdims=True)
        dn = d_o * g_ref[...]
        d_mixed = (r * (dn - nh * jnp.mean(dn * nh, axis=-1, keepdims=True))).astype(BF16)
        dwo_ref[...] += _tn(mixed_pre.astype(BF16), d_mixed)
        d_pre = _nt(d_mixed, wo_ref[...])
        for g in range(N_BRANCH):
            br_g = br_ref[:, BRANCH_WIDTH * g:BRANCH_WIDTH * (g + 1)]
            d_bo = (d_pre * gates[g]).astype(BF16)
            dm_ref[:, D_MODEL * g:D_MODEL * (g + 1)] = (
                d_pre * bo[g] * gates[g] * (1.0 - gates[g])).astype(BF16)
            d_br_g = None
            for p in range(N_CHIPS):
                d_bo_p = d_bo[:, SHARD_D * p:SHARD_D * (p + 1)]
                part = _nt(d_bo_p, wb_ref[p, g])
                d_br_g = part if d_br_g is None else d_br_g + part
                dwb_ref[p, g] += _tn(br_g, d_bo_p)
            dbr_ref[:, BRANCH_WIDTH * g:BRANCH_WIDTH * (g + 1)] = d_br_g

    ins = [d_out, br, pm, wb, wo, g_post]
    in_specs = [pl.BlockSpec((tm, D_MODEL), lambda i: (i, 0)),
                pl.BlockSpec((tm, N_BRANCH * BRANCH_WIDTH), lambda i: (i, 0)),
                pl.BlockSpec((tm, MERGE_WIDTH), lambda i: (i, 0)),
                pl.BlockSpec((None, N_CHIPS, N_BRANCH, BRANCH_WIDTH, SHARD_D), lambda i: (layer, 0, 0, 0, 0)),
                pl.BlockSpec((None, D_MODEL, D_MODEL), lambda i: (layer, 0, 0)),
                pl.BlockSpec((1, D_MODEL), lambda i: (0, 0))]
    aliases = {}
    if acc is not None:
        ins += list(acc)
        in_specs += [ANY, ANY]
        aliases = {6: 3, 7: 4}
    return pl.pallas_call(
        body, name="merge_bwd",
        grid=(t // tm,),
        in_specs=in_specs,
        out_specs=[pl.BlockSpec((tm, N_BRANCH * BRANCH_WIDTH), lambda i: (i, 0)),
                   pl.BlockSpec((tm, MERGE_WIDTH), lambda i: (i, 0)),
                   pl.BlockSpec((8, D_MODEL), lambda i: (0, 0)),
                   pl.BlockSpec((None, N_CHIPS, N_BRANCH, BRANCH_WIDTH, SHARD_D), lambda i: (layer, 0, 0, 0, 0)),
                   pl.BlockSpec((None, D_MODEL, D_MODEL), lambda i: (layer, 0, 0))],
        out_shape=[jax.ShapeDtypeStruct((t, N_BRANCH * BRANCH_WIDTH), F32),
                   jax.ShapeDtypeStruct((t, MERGE_WIDTH), BF16),
                   jax.ShapeDtypeStruct((8, D_MODEL), F32),
                   jax.ShapeDtypeStruct((DEPTH, N_CHIPS, N_BRANCH, BRANCH_WIDTH, SHARD_D), F32),
                   jax.ShapeDtypeStruct((DEPTH, D_MODEL, D_MODEL), F32)],
        input_output_aliases=aliases,
        compiler_params=_cparams("arbitrary"),
    )(*ins)


def loss_head(xf, target2d, nb, nc):
    def body(x_ref, t_ref, l_ref, dx_ref):
        b = pl.program_id(0)
        n = pl.program_id(1)

        @pl.when((b == 0) & (n == 0))
        def _():
            l_ref[...] = jnp.zeros_like(l_ref)

        @pl.when(n == 0)
        def _():
            dx_ref[...] = jnp.zeros_like(dx_ref)

        @pl.when(n > 0)
        def _():
            e = x_ref[...] - t_ref[...]
            dx_ref[...] = e * (1.0 / D_MODEL)
            s = jnp.sum(jnp.sum(e * e, axis=1, keepdims=True), axis=0, keepdims=True)
            l_ref[...] += jnp.broadcast_to(s * (0.5 / D_MODEL), l_ref.shape)

    return pl.pallas_call(
        body, name="loss_head",
        grid=(nb, nc),
        in_specs=[pl.BlockSpec((BLOCK, D_MODEL), lambda b, n: (b * nc + n, 0)),
                  pl.BlockSpec((BLOCK, D_MODEL), lambda b, n: (b * (nc - 1) + jnp.maximum(n - 1, 0), 0))],
        out_specs=[pl.BlockSpec((8, BLOCK), lambda b, n: (0, 0)),
                   pl.BlockSpec((BLOCK, D_MODEL), lambda b, n: (b * nc + n, 0))],
        out_shape=[jax.ShapeDtypeStruct((8, BLOCK), F32),
                   jax.ShapeDtypeStruct(xf.shape, F32)],
        compiler_params=_cparams("arbitrary", "arbitrary"),
    )(xf, target2d)


N_ABC_TILES = ABC_WIDTH // COL_TILE
N_M_TILES = MERGE_WIDTH // COL_TILE


def proj_dgrad(d_abc, d_m, w, layer, x2d, g, d_out):
    t = x2d.shape[0]
    tm = ROW_TILE if t % ROW_TILE == 0 else BLOCK
    nk = N_ABC_TILES + N_M_TILES

    def body(da_ref, dm_ref, w_ref, x_ref, g_ref, do_ref, dx_ref, dg_ref, acc):
        i = pl.program_id(0)
        k = pl.program_id(1)

        @pl.when((i == 0) & (k == 0))
        def _():
            dg_ref[...] = jnp.zeros_like(dg_ref)

        @pl.when(k == 0)
        def _():
            acc[...] = jnp.zeros_like(acc)

        @pl.when(k < N_ABC_TILES)
        def _():
            acc[...] += _nn(da_ref[...], w_ref[...])

        @pl.when(k >= N_ABC_TILES)
        def _():
            acc[...] += _nn(dm_ref[...], w_ref[...])

        @pl.when(k == nk - 1)
        def _():
            x = x_ref[...]
            r = lax.rsqrt(jnp.mean(x * x, axis=-1, keepdims=True) + RMS_EPS)
            nh = x * r
            dh = acc[...]
            dg_ref[0:1, :] += jnp.sum(dh * nh, axis=0, keepdims=True)
            dn = dh * g_ref[...]
            dx_ref[...] = do_ref[...] + r * (dn - nh * jnp.mean(dn * nh, axis=-1, keepdims=True))

    return pl.pallas_call(
        body, name="proj_dgrad",
        grid=(t // tm, nk),
        in_specs=[pl.BlockSpec((tm, COL_TILE), lambda i, k: (i, jnp.minimum(k, N_ABC_TILES - 1))),
                  pl.BlockSpec((tm, COL_TILE), lambda i, k: (i, jnp.maximum(k - N_ABC_TILES, 0))),
                  pl.BlockSpec((None, COL_TILE, D_MODEL), lambda i, k: (layer, k, 0)),
                  pl.BlockSpec((tm, D_MODEL), lambda i, k: (i, 0)),
                  pl.BlockSpec((1, D_MODEL), lambda i, k: (0, 0)),
                  pl.BlockSpec((tm, D_MODEL), lambda i, k: (i, 0))],
        out_specs=[pl.BlockSpec((tm, D_MODEL), lambda i, k: (i, 0)),
                   pl.BlockSpec((8, D_MODEL), lambda i, k: (0, 0))],
        out_shape=[jax.ShapeDtypeStruct((t, D_MODEL), F32),
                   jax.ShapeDtypeStruct((8, D_MODEL), F32)],
        scratch_shapes=[pltpu.VMEM((tm, D_MODEL), F32)],
        compiler_params=_cparams("arbitrary", "arbitrary"),
    )(d_abc, d_m, w, x2d, g, d_out)


def proj_wgrad(hb, d_abc, d_m, layer, acc):
    t = hb.shape[0]
    nj = N_ABC_TILES + N_M_TILES

    def body(*refs):
        h_ref, da_ref, dm_ref = refs[:3]
        o_ref = refs[-1]
        j = pl.program_id(0)

        @pl.when(j < N_ABC_TILES)
        def _():
            o_ref[...] = _tn(da_ref[...], h_ref[...])

        @pl.when(j >= N_ABC_TILES)
        def _():
            o_ref[...] = _tn(dm_ref[...], h_ref[...])

    ins = [hb, d_abc, d_m]
    in_specs = [pl.BlockSpec((t, D_MODEL), lambda j: (0, 0)),
                pl.BlockSpec((t, COL_TILE), lambda j: (0, jnp.minimum(j, N_ABC_TILES - 1))),
                pl.BlockSpec((t, COL_TILE), lambda j: (0, jnp.maximum(j - N_ABC_TILES, 0)))]
    aliases = {}
    if acc is not None:
        ins.append(acc)
        in_specs.append(ANY)
        aliases = {3: 0}
    return pl.pallas_call(
        body, name="proj_wgrad",
        grid=(nj,),
        in_specs=in_specs,
        out_specs=pl.BlockSpec((None, COL_TILE, D_MODEL), lambda j: (layer, j, 0)),
        out_shape=jax.ShapeDtypeStruct((DEPTH, PROJ_WIDTH, D_MODEL), F32),
        input_output_aliases=aliases,
        compiler_params=_cparams("arbitrary"),
    )(*ins)


def _adamw_math(w, g, m, v):
    m = ADAM_B1 * m + (1.0 - ADAM_B1) * g
    v = ADAM_B2 * v + (1.0 - ADAM_B2) * jnp.square(g)
    m_hat = m / (1.0 - ADAM_B1 ** ADAM_STEP)
    v_hat = v / (1.0 - ADAM_B2 ** ADAM_STEP)
    delta = -ADAM_LR * (m_hat / (jnp.sqrt(v_hat) + ADAM_EPS) + ADAM_WD * w)
    return delta, m, v


def adamw_big(w, g, m, v):
    lead, (r, c) = w.shape[:-2], w.shape[-2:]
    tr = _row_tile(r)

    def body(w_ref, g_ref, m_ref, v_ref, d_ref, mo_ref, vo_ref):
        d, m_new, v_new = _adamw_math(w_ref[...], g_ref[...], m_ref[...], v_ref[...])
        d_ref[...] = d
        mo_ref[...] = m_new
        vo_ref[...] = v_new

    spec = pl.BlockSpec((None,) * len(lead) + (tr, c), lambda *idx: idx + (0,))
    sds = jax.ShapeDtypeStruct(w.shape, F32)
    return pl.pallas_call(
        body, name="adamw_big",
        grid=lead + (r // tr,),
        in_specs=[spec] * 4, out_specs=[spec] * 3, out_shape=[sds] * 3,
        compiler_params=_cparams(*(["parallel"] * (len(lead) + 1))),
    )(w, g, m, v)


def adamw_small(params):
    k = len(params)

    def body(*refs):
        ins, outs = refs[:4 * k], refs[4 * k:]
        for i in range(k):
            d, m_new, v_new = _adamw_math(*[r[...] for r in ins[4 * i:4 * i + 4]])
            outs[3 * i][...] = d
            outs[3 * i + 1][...] = m_new
            outs[3 * i + 2][...] = v_new

    flat = [a for p in params for a in p]
    vm = pl.BlockSpec(memory_space=pltpu.VMEM)
    out_shape = [jax.ShapeDtypeStruct(p[0].shape, F32) for p in params for _ in range(3)]
    res = pl.pallas_call(
        body, name="adamw_small",
        in_specs=[vm] * len(flat), out_specs=[vm] * len(out_shape), out_shape=out_shape,
    )(*flat)
    return [tuple(res[3 * i:3 * i + 3]) for i in range(k)]


ANY = pl.BlockSpec(memory_space=pl.ANY)


def _place():
    return lax.axis_index("x"), lax.axis_index("y"), lax.axis_index("c")


def _own_slot(shard, chip):
    buf = lax.empty((DEPTH, N_CHIPS) + shard.shape[1:], shard.dtype)
    return lax.dynamic_update_slice(buf, shard[:, None], (0, chip) + (0,) * (shard.ndim - 1))


def gather_weight_shards(bufs):
    n = len(bufs)

    def body(*refs):
        g_refs = refs[n:2 * n]
        send_sems, recv_sems = refs[2 * n:]
        x, y, c = _place()
        me_p = 2 * x + y
        sibling = (x, y, 1 - c)
        chips = [(1 - x, y), (x, 1 - y), (1 - x, 1 - y)]

        def copy(k, slab, to):
            return pltpu.make_async_remote_copy(src_ref=slab, dst_ref=slab, send_sem=send_sems.at[k],
                                                recv_sem=recv_sems.at[k], device_id=to, device_id_type=MESH)

        first, passed = [], []
        for t in range(n):
            for k, (qx, qy) in enumerate(chips):
                cp = copy(6 * t + k, g_refs[t].at[c, me_p], (qx, qy, c))
                cp.start()
                first.append(cp)
        for t in range(n):
            for k, (qx, qy) in enumerate(chips):
                slab = g_refs[t].at[c, 2 * qx + qy]
                copy(6 * t + k, slab, (qx, qy, c)).wait_recv()
                fwd = copy(6 * t + 3 + k, slab, sibling)
                fwd.start()
                passed.append(fwd)
        for t in range(n):
            for k, (qx, qy) in enumerate(chips):
                copy(6 * t + 3 + k, g_refs[t].at[1 - c, 2 * qx + qy], sibling).wait_recv()
        for cp in first + passed:
            cp.wait_send()

    return pl.pallas_call(
        body, name="gather_weight_shards",
        in_specs=[ANY] * n, out_specs=[ANY] * n,
        out_shape=[jax.ShapeDtypeStruct(b.shape, b.dtype) for b in bufs],
        input_output_aliases={t: t for t in range(n)},
        scratch_shapes=[pltpu.SemaphoreType.DMA((6 * n,)), pltpu.SemaphoreType.DMA((6 * n,))],
    )(*bufs)


def exchange_small(pack):
    def body(p_ref, o_ref, send_sems, recv_sems, local_sem):
        x, y, c = _place()
        me = 4 * x + 2 * y + c
        mine = pltpu.make_async_copy(p_ref, o_ref.at[me], local_sem)
        mine.start()
        sends = []
        for k in range(1, 8):
            fx, fy, fc = (k >> 2) & 1, (k >> 1) & 1, k & 1
            peer = (x ^ fx, y ^ fy, c ^ fc)
            cp = pltpu.make_async_remote_copy(src_ref=p_ref, dst_ref=o_ref.at[me], send_sem=send_sems.at[k - 1],
                                              recv_sem=recv_sems.at[k - 1], device_id=peer, device_id_type=MESH)
            cp.start()
            sends.append(cp)
        for k in range(1, 8):
            fx, fy, fc = (k >> 2) & 1, (k >> 1) & 1, k & 1
            peer = (x ^ fx, y ^ fy, c ^ fc)
            slot = o_ref.at[4 * peer[0] + 2 * peer[1] + peer[2]]
            pltpu.make_async_remote_copy(src_ref=slot, dst_ref=slot, send_sem=send_sems.at[k - 1],
                                         recv_sem=recv_sems.at[k - 1], device_id=peer, device_id_type=MESH).wait_recv()
        for cp in sends:
            cp.wait_send()
        mine.wait()

    return pl.pallas_call(
        body, name="exchange_small",
        in_specs=[ANY], out_specs=ANY,
        out_shape=jax.ShapeDtypeStruct((8,) + pack.shape, pack.dtype),
        scratch_shapes=[pltpu.SemaphoreType.DMA((7,)), pltpu.SemaphoreType.DMA((7,)), pltpu.SemaphoreType.DMA],
    )(pack)


def sibling_swap_layers(grads):
    n = len(grads)

    def body(*refs):
        g_refs, o_refs = refs[:n], refs[n:2 * n]
        send_sems, recv_sems = refs[2 * n:]
        x, y, c = _place()
        cps = []
        for t in range(n):
            cp = pltpu.make_async_remote_copy(src_ref=g_refs[t].at[1 - c], dst_ref=o_refs[t],
                                              send_sem=send_sems.at[t], recv_sem=recv_sems.at[t],
                                              device_id=(x, y, 1 - c), device_id_type=MESH)
            cp.start()
            cps.append(cp)
        for cp in cps:
            cp.wait()

    return pl.pallas_call(
        body, name="sibling_swap_layers",
        in_specs=[ANY] * n, out_specs=[ANY] * n,
        out_shape=[jax.ShapeDtypeStruct(g.shape[1:], g.dtype) for g in grads],
        scratch_shapes=[pltpu.SemaphoreType.DMA((n,)), pltpu.SemaphoreType.DMA((n,))],
    )(*grads)


def _row_tile(r):
    return max(t for t in range(16, 513, 16) if r % t == 0)


def add_own_layer(g, other, c_arr):
    _, _, r, cols = g.shape
    tr = _row_tile(r)

    def body(c_ref, a_ref, b_ref, o_ref):
        o_ref[...] = (a_ref[...] + b_ref[...]).astype(BF16)

    return pl.pallas_call(
        body, name="add_own_layer",
        grid_spec=pltpu.PrefetchScalarGridSpec(
            num_scalar_prefetch=1, grid=(N_CHIPS, r // tr),
            in_specs=[pl.BlockSpec((None, None, tr, cols), lambda p, i, c_ref: (c_ref[0], p, i, 0)),
                      pl.BlockSpec((None, tr, cols), lambda p, i, c_ref: (p, i, 0))],
            out_specs=pl.BlockSpec((None, tr, cols), lambda p, i, c_ref: (p, i, 0))),
        out_shape=jax.ShapeDtypeStruct((N_CHIPS, r, cols), BF16),
        compiler_params=_cparams("parallel", "parallel"),
    )(c_arr, g, other)


def scatter_to_chips(partials):
    n = len(partials)

    def body(*refs):
        s_refs, o_refs = refs[:n], refs[n:2 * n]
        send_sems, recv_sems = refs[2 * n:]
        x, y, c = _place()
        chips = [(1 - x, y), (x, 1 - y), (1 - x, 1 - y)]
        cps = []
        for t in range(n):
            for k, (qx, qy) in enumerate(chips):
                cp = pltpu.make_async_remote_copy(src_ref=s_refs[t].at[2 * qx + qy], dst_ref=o_refs[t].at[k],
                                                  send_sem=send_sems.at[3 * t + k], recv_sem=recv_sems.at[3 * t + k],
                                                  device_id=(qx, qy, c), device_id_type=MESH)
                cp.start()
                cps.append(cp)
        for cp in cps:
            cp.wait()

    return pl.pallas_call(
        body, name="scatter_to_chips",
        in_specs=[ANY] * n, out_specs=[ANY] * n,
        out_shape=[jax.ShapeDtypeStruct((3,) + s.shape[1:], s.dtype) for s in partials],
        scratch_shapes=[pltpu.SemaphoreType.DMA((3 * n,)), pltpu.SemaphoreType.DMA((3 * n,))],
    )(*partials)


def sum_chips(own, parts, where):
    _, r, cols = own.shape
    tr = _row_tile(r)

    def body(w_ref, a_ref, p_ref, o_ref):
        acc = a_ref[...].astype(F32)
        for k in range(3):
            acc = acc + p_ref[k].astype(F32)
        o_ref[...] = acc

    return pl.pallas_call(
        body, name="sum_chips",
        grid_spec=pltpu.PrefetchScalarGridSpec(
            num_scalar_prefetch=1, grid=(r // tr,),
            in_specs=[pl.BlockSpec((None, tr, cols), lambda i, w_ref: (w_ref[0], i, 0)),
                      pl.BlockSpec((3, tr, cols), lambda i, w_ref: (0, i, 0))],
            out_specs=pl.BlockSpec((None, tr, cols), lambda i, w_ref: (w_ref[1], i, 0))),
        out_shape=jax.ShapeDtypeStruct((DEPTH, r, cols), F32),
        compiler_params=_cparams("parallel"),
    )(where, own, parts)


def sibling_share_layer(bufs):
    n = len(bufs)

    def body(*refs):
        o_refs = refs[n:2 * n]
        send_sems, recv_sems = refs[2 * n:]
        x, y, c = _place()
        cps = []
        for t in range(n):
            cp = pltpu.make_async_remote_copy(src_ref=o_refs[t].at[c], dst_ref=o_refs[t].at[c], send_sem=send_sems.at[t],
                                              recv_sem=recv_sems.at[t], device_id=(x, y, 1 - c), device_id_type=MESH)
            cp.start()
            cps.append(cp)
        for t in range(n):
            slot = o_refs[t].at[1 - c]
            pltpu.make_async_remote_copy(src_ref=slot, dst_ref=slot, send_sem=send_sems.at[t], recv_sem=recv_sems.at[t],
                                         device_id=(x, y, 1 - c), device_id_type=MESH).wait_recv()
        for cp in cps:
            cp.wait_send()

    return pl.pallas_call(
        body, name="sibling_share_layer",
        in_specs=[ANY] * n, out_specs=[ANY] * n,
        out_shape=[jax.ShapeDtypeStruct(b.shape, b.dtype) for b in bufs],
        input_output_aliases={t: t for t in range(n)},
        scratch_shapes=[pltpu.SemaphoreType.DMA((n,)), pltpu.SemaphoreType.DMA((n,))],
    )(*bufs)


SP_META = 2 * (N_META * D_MODEL // LANES)
SP_NORM = DEPTH * D_MODEL // LANES
SP_RB = DEPTH * N_BUCKETS
SP_SINK = DEPTH * ATT_HEADS
SP_CONV = DEPTH * 3 * BRANCH_WIDTH // LANES
SP_LOSS = 8
SP_ROWS = SP_META + 2 * SP_NORM + SP_RB + SP_SINK + SP_CONV + SP_LOSS


def sum_small(slots):
    half = SP_META // 2
    rb0 = SP_META + 2 * SP_NORM
    rest_rows = SP_ROWS - SP_META

    def body(s_ref, meta_ref, rest_ref):
        acc = s_ref[0]
        for d in range(1, 8):
            acc = acc + s_ref[d]
        meta_ref[...] = acc[0:half] + acc[half:SP_META]
        rest_ref[...] = acc[SP_META:]
        rest_ref[rb0 - SP_META:rb0 - SP_META + N_BUCKETS, :] = (
            acc[rb0:rb0 + N_BUCKETS] + acc[rb0 + N_BUCKETS:rb0 + 2 * N_BUCKETS])

    vm = pl.BlockSpec(memory_space=pltpu.VMEM)
    return pl.pallas_call(
        body, name="sum_small",
        in_specs=[vm], out_specs=[vm, vm],
        out_shape=[jax.ShapeDtypeStruct((half, LANES), F32), jax.ShapeDtypeStruct((rest_rows, LANES), F32)],
    )(slots)


def local_step(x, loss_target, meta_full, rel_bias, norm_pre, conv_w_full, attn_sinks, norm_post, weights):
    nb, seq, _ = x.shape
    nc = seq // BLOCK + 1
    lp = nc * BLOCK
    rows = nb * lp
    pad = jnp.zeros((nb, PAD_FRONT, D_MODEL), F32)
    meta = jnp.broadcast_to(meta_full[None], (nb, N_META, D_MODEL))
    h0 = jnp.concatenate([pad, meta, x], axis=1).reshape(rows, D_MODEL)
    cosf, sinf = _rot_tables(lp)
    bkt = jnp.asarray(_bucket_table())

    w_in, w_br, w_out = weights
    acts = []
    h = h0
    for l in range(DEPTH):
        g_pre = norm_pre[l][None]
        hb, p_abc = norm_matmul(h, g_pre, w_in, l, 0, N_ABC_TILES)
        p_m = matmul_cols(hb, w_in, l, N_ABC_TILES, N_M_TILES)
        br, states = mixers_fwd(p_abc, cosf, sinf, bkt, rel_bias, attn_sinks[l][None], conv_w_full[l], nb, nc)
        h_next = merge_fwd(h, br, p_m, w_br, w_out, norm_post[l][None], l)
        acts.append((h, hb, p_abc, p_m, br, states))
        h = h_next

    loss_part, d_h = loss_head(h, loss_target.reshape(nb * seq, D_MODEL), nb, nc)

    small = [None] * DEPTH
    g_win = None
    g_wbo = None
    for l in reversed(range(DEPTH)):
        h_in, hb, p_abc, p_m, br, states = acts[l]
        d_br, d_m, d_gpost, g_wbr, g_wout = merge_bwd(d_h, br, p_m, w_br, w_out, norm_post[l][None], l, g_wbo)
        g_wbo = (g_wbr, g_wout)
        d_abc, d_rb, d_sk, d_cw = mixers_bwd(p_abc, d_br, states, cosf, sinf, bkt, rel_bias,
                                             attn_sinks[l][None], conv_w_full[l], nb, nc)
        d_h, d_gpre = proj_dgrad(d_abc, d_m, w_in, l, h_in, norm_pre[l][None], d_h)
        g_win = proj_wgrad(hb, d_abc, d_m, l, g_win)
        small[l] = (d_gpre[0], d_gpost[0], d_rb, d_sk, d_cw[0:3])

    d_h3 = d_h.reshape(nb, lp, D_MODEL)
    d_x = d_h3[:, BLOCK:]
    d_meta = d_h3[:, PAD_FRONT:BLOCK]
    sp = jnp.concatenate([
        d_meta.reshape(-1, LANES),
        jnp.stack([small[l][0] for l in range(DEPTH)]).reshape(-1, LANES),
        jnp.stack([small[l][1] for l in range(DEPTH)]).reshape(-1, LANES),
        jnp.concatenate([small[l][2] for l in range(DEPTH)], axis=0),
        jnp.concatenate([small[l][3] for l in range(DEPTH)], axis=0),
        jnp.stack([small[l][4] for l in range(DEPTH)]).reshape(-1, LANES),
        loss_part], axis=0)
    return d_x, g_win, g_wbo[0], g_wbo[1], sp


def kernel(x, meta_tokens, rel_bias, norm_pre, w_in, conv_w, attn_sinks, w_branch, w_out, norm_post, loss_target, m_meta_tokens, m_rel_bias, m_norm_pre, m_w_in, m_conv_w, m_attn_sinks, m_w_branch, m_w_out, m_norm_post, v_meta_tokens, v_rel_bias, v_norm_pre, v_w_in, v_conv_w, v_attn_sinks, v_w_branch, v_w_out, v_norm_post):
    assert x.shape[0] == 2 and SP_META == 2 * N_META * D_MODEL // LANES
    px, py, pc = _place()
    chip = 2 * px + py

    shards = [jnp.swapaxes(w_in, 1, 2).astype(BF16), w_branch.astype(BF16), w_out.astype(BF16)]
    a_in, a_br, a_out = gather_weight_shards([_own_slot(s, chip) for s in shards])
    weights = (a_in.reshape(DEPTH, PROJ_WIDTH, D_MODEL), a_br, a_out.reshape(DEPTH, D_MODEL, D_MODEL))
    side = jnp.concatenate([meta_tokens.reshape(-1), conv_w.reshape(-1)]).reshape(-1, LANES)
    side = jnp.concatenate([side, jnp.zeros((40 - side.shape[0], LANES), F32)], axis=0)
    side_all = exchange_small(side)
    side_chips = side_all[0::2]
    n_meta_rows = N_META * SHARD_D // LANES
    meta_full = jnp.moveaxis(side_chips[:, :n_meta_rows].reshape(N_CHIPS, N_META, SHARD_D), 0, 1).reshape(N_META, D_MODEL)
    conv_full = jnp.moveaxis(side_chips[:, n_meta_rows:n_meta_rows + 6].reshape(N_CHIPS, DEPTH, 3, LANES), 0, 2).reshape(DEPTH, 3, BRANCH_WIDTH)

    d_x, g_win, g_wbr, g_wout, sp = local_step(x, loss_target, meta_full, rel_bias, norm_pre, conv_full, attn_sinks,
                                               norm_post, weights)

    full = [g_win.reshape(DEPTH, N_CHIPS, SHARD_IN, D_MODEL),
            g_wbr.reshape(DEPTH, N_CHIPS, N_BRANCH * BRANCH_WIDTH, SHARD_D),
            g_wout.reshape(DEPTH, N_CHIPS, SHARD_D, D_MODEL)]
    others = sibling_swap_layers(full)
    c_arr = jnp.reshape(pc, (1,)).astype(jnp.int32)
    partials = [add_own_layer(g, o, c_arr) for g, o in zip(full, others)]
    parts = scatter_to_chips(partials)
    where = jnp.stack([chip, pc]).astype(jnp.int32)
    r_in, r_br, r_out = sibling_share_layer([sum_chips(a, p, where) for a, p in zip(partials, parts)])
    g_in = jnp.swapaxes(r_in, 1, 2)
    g_br = r_br.reshape(w_branch.shape)
    g_out = r_out

    meta_rows, rest = sum_small(exchange_small(sp))
    o = 0
    g_meta_full = meta_rows.reshape(N_META, D_MODEL)
    g_norm_pre = rest[o:o + SP_NORM].reshape(DEPTH, D_MODEL); o += SP_NORM
    g_norm_post = rest[o:o + SP_NORM].reshape(DEPTH, D_MODEL); o += SP_NORM
    g_rel_bias = rest[o:o + N_BUCKETS, :ATT_HEADS]; o += SP_RB
    g_sinks = rest[o:o + SP_SINK, 0].reshape(DEPTH, ATT_HEADS); o += SP_SINK
    g_conv_full = rest[o:o + SP_CONV].reshape(DEPTH, 3, BRANCH_WIDTH); o += SP_CONV
    loss = rest[o, 0]
    g_meta = lax.dynamic_slice_in_dim(g_meta_full, chip * SHARD_D, SHARD_D, axis=1)
    g_conv = lax.dynamic_slice_in_dim(g_conv_full, chip * LANES, LANES, axis=2)

    tr_ = lambda a: jnp.swapaxes(a, 1, 2)
    u_in = [tr_(a) for a in adamw_big(tr_(w_in), r_in, tr_(m_w_in), tr_(v_w_in))]
    u_br = adamw_big(w_branch, g_br, m_w_branch, v_w_branch)
    u_out = adamw_big(w_out, g_out, m_w_out, v_w_out)
    to2 = lambda a: a.reshape(-1, a.shape[-1])
    smalls = [(meta_tokens, g_meta, m_meta_tokens, v_meta_tokens),
              (rel_bias, g_rel_bias, m_rel_bias, v_rel_bias),
              (norm_pre, g_norm_pre, m_norm_pre, v_norm_pre),
              (to2(conv_w), to2(g_conv), to2(m_conv_w), to2(v_conv_w)),
              (attn_sinks, g_sinks, m_attn_sinks, v_attn_sinks),
              (norm_post, g_norm_post, m_norm_post, v_norm_post)]
    u_meta, u_rb, u_npre, u_conv, u_sink, u_npost = adamw_small(smalls)
    u_conv = tuple(a.reshape(conv_w.shape) for a in u_conv)

    grads = [g_meta, g_rel_bias, g_norm_pre, g_in, g_conv, g_sinks, g_br, g_out, g_norm_post]
    upd = [u_meta, u_rb, u_npre, u_in, u_conv, u_sink, u_br, u_out, u_npost]
    return (loss, d_x, *grads, *[u[0] for u in upd], *[u[1] for u in upd], *[u[2] for u in upd])
```

```python
import functools
import math

import numpy as np
import jax
import jax.numpy as jnp
from jax import lax
from jax.experimental import pallas as pl
from jax.experimental.pallas import tpu as pltpu

F32 = jnp.float32
BF16 = jnp.bfloat16
MESH = pl.DeviceIdType.MESH

D_MODEL = 1024
DEPTH = 2
N_META = 16
BLOCK = 128
PAD_FRONT = BLOCK - N_META
ATT_HEADS = 8
ATT_HEAD_DIM = 64
N_BUCKETS = 32
MAX_EXACT = 16
MAX_DISTANCE = 128
RET_HEADS = 4
ROT_BASE = 10000.0
N_BRANCH = 3
BRANCH_WIDTH = 512
PROJ_WIDTH = 8448
ABC_WIDTH = 5376
MERGE_WIDTH = N_BRANCH * D_MODEL
RMS_EPS = 1e-6
GN_EPS = 1e-6
NEG_INF = -1e30
ATT_SCALE = ATT_HEAD_DIM ** -0.5
RET_SCALE = BLOCK ** -0.5
LOG_GAMMA = tuple(math.log1p(-(2.0 ** (-5.0 - h))) for h in range(RET_HEADS))

C_AQ, C_AK, C_AV, C_AG = 0, 512, 640, 768
C_RQ, C_RK, C_RV, C_RG = 1280, 1792, 2304, 2816
C_CB, C_CC, C_CX, C_CG = 3328, 3840, 4352, 4864

ADAM_LR = 0.001
ADAM_B1 = 0.9
ADAM_B2 = 0.999
ADAM_EPS = 1e-08
ADAM_WD = 0.01
ADAM_STEP = 10

N_CHIPS = 4
SHARD_IN = PROJ_WIDTH // N_CHIPS
SHARD_D = D_MODEL // N_CHIPS
LANES = 128
PACK_IN = D_MODEL * SHARD_IN
PACK_BR = N_BRANCH * BRANCH_WIDTH * SHARD_D
PACK_OUT = SHARD_D * D_MODEL
PACK_ROWS = (PACK_IN + PACK_BR + PACK_OUT) // LANES

VMEM_LIMIT = 56 * 1024 * 1024
COL_TILE = 768
ROW_TILE = 1088


def _cparams(*sem):
    return pltpu.CompilerParams(dimension_semantics=sem, vmem_limit_bytes=VMEM_LIMIT)


def _nt(a, b):
    return lax.dot_general(a, b, (((1,), (1,)), ((), ())), preferred_element_type=F32)


def _tn(a, b):
    return lax.dot_general(a, b, (((0,), (0,)), ((), ())), preferred_element_type=F32)


def _nn(a, b):
    return jnp.dot(a, b, preferred_element_type=F32)


def _silu(x):
    return x * jax.nn.sigmoid(x)


def _dsilu(x):
    s = jax.nn.sigmoid(x)
    return s * (1.0 + x * (1.0 - s))


def _bucket_table():
    r = np.arange(BLOCK)[:, None]
    c = np.arange(2 * BLOCK)[None, :]
    n = np.maximum(BLOCK + r - c, 0)
    nf = np.maximum(n, 1).astype(np.float32)
    large = MAX_EXACT + (np.log(nf / MAX_EXACT) / math.log(MAX_DISTANCE / MAX_EXACT)
                         * (N_BUCKETS - MAX_EXACT)).astype(np.int32)
    large = np.minimum(large, N_BUCKETS - 1)
    return np.where(n < MAX_EXACT, n, large).astype(np.int32)


def _rot_tables(lp):
    half = BLOCK // 2
    pos = (jnp.arange(lp) - PAD_FRONT).astype(F32)
    theta = 1.0 / (ROT_BASE ** jnp.linspace(0.0, 1.0, half, dtype=F32))
    ang = pos[:, None] * theta[None, :]
    cos, sin = jnp.cos(ang), jnp.sin(ang)
    return jnp.concatenate([cos, cos], axis=1), jnp.concatenate([-sin, sin], axis=1)


def norm_matmul(x2d, g, w, col0_blocks, n_col_blocks):
    t = x2d.shape[0]
    tm = ROW_TILE if t % ROW_TILE == 0 else BLOCK

    def body(x_ref, g_ref, w_ref, hb_ref, o_ref):
        @pl.when(pl.program_id(1) == 0)
        def _():
            x = x_ref[...]
            r = lax.rsqrt(jnp.mean(x * x, axis=-1, keepdims=True) + RMS_EPS)
            hb_ref[...] = (x * r * g_ref[...]).astype(BF16)

        o_ref[...] = _nt(hb_ref[...], w_ref[...])

    return pl.pallas_call(
        body, name="norm_matmul",
        grid=(t // tm, n_col_blocks),
        in_specs=[pl.BlockSpec((tm, D_MODEL), lambda i, j: (i, 0)),
                  pl.BlockSpec((1, D_MODEL), lambda i, j: (0, 0)),
                  pl.BlockSpec((COL_TILE, D_MODEL), lambda i, j: (j + col0_blocks, 0))],
        out_specs=[pl.BlockSpec((tm, D_MODEL), lambda i, j: (i, 0)),
                   pl.BlockSpec((tm, COL_TILE), lambda i, j: (i, j))],
        out_shape=[jax.ShapeDtypeStruct((t, D_MODEL), BF16),
                   jax.ShapeDtypeStruct((t, n_col_blocks * COL_TILE), F32)],
        compiler_params=_cparams("parallel", "arbitrary"),
    )(x2d, g, w)


def matmul_cols(a, w, col0_blocks, n_col_blocks):
    t, k = a.shape
    tm = ROW_TILE if t % ROW_TILE == 0 else BLOCK

    def body(a_ref, w_ref, o_ref):
        o_ref[...] = _nt(a_ref[...], w_ref[...])

    return pl.pallas_call(
        body, name="matmul_cols",
        grid=(t // tm, n_col_blocks),
        in_specs=[pl.BlockSpec((tm, k), lambda i, j: (i, 0)),
                  pl.BlockSpec((COL_TILE, k), lambda i, j: (j + col0_blocks, 0))],
        out_specs=pl.BlockSpec((tm, COL_TILE), lambda i, j: (i, j)),
        out_shape=jax.ShapeDtypeStruct((t, n_col_blocks * COL_TILE), F32),
        compiler_params=_cparams("parallel", "arbitrary"),
    )(a, w)


def _build_bias(bkt_ref, rb_ref, bias_s):
    bkt = bkt_ref[...]
    for h in range(ATT_HEADS):
        acc = jnp.zeros((BLOCK, 2 * BLOCK), F32)
        for b in range(N_BUCKETS):
            acc = jnp.where(bkt == b, rb_ref[b, h], acc)
        bias_s[h] = acc


def _band_mask(n):
    r = lax.broadcasted_iota(jnp.int32, (BLOCK, 2 * BLOCK), 0)
    c = lax.broadcasted_iota(jnp.int32, (BLOCK, 2 * BLOCK), 1)
    key_pos = (n - 1) * BLOCK + c
    return (c > r) & (c <= r + BLOCK) & (key_pos >= PAD_FRONT)


def _split_heads(kv, kh):
    lane = lax.broadcasted_iota(jnp.int32, kv.shape, 1)
    if kh == 0:
        lo = jnp.where(lane < ATT_HEAD_DIM, kv, 0.0)
        hi = pltpu.roll(lo, ATT_HEAD_DIM, 1)
    else:
        hi = jnp.where(lane >= ATT_HEAD_DIM, kv, 0.0)
        lo = pltpu.roll(hi, ATT_HEAD_DIM, 1)
    return lo, hi


def _merge_heads(acc_lo, acc_hi, kh):
    lane = lax.broadcasted_iota(jnp.int32, acc_lo.shape, 1)
    if kh == 0:
        return jnp.where(lane < ATT_HEAD_DIM, acc_lo + pltpu.roll(acc_hi, ATT_HEAD_DIM, 1), 0.0)
    return jnp.where(lane >= ATT_HEAD_DIM, acc_hi + pltpu.roll(acc_lo, ATT_HEAD_DIM, 1), 0.0)


def _softmax_sink(q2b, kxb, bias_h, mask, sink_h):
    s = _nt(q2b, kxb) * ATT_SCALE + bias_h
    s = jnp.where(mask, s, NEG_INF)
    m = jnp.maximum(jnp.max(s, axis=-1, keepdims=True), sink_h)
    p = jnp.exp(s - m)
    es = jnp.exp(sink_h - m)
    inv = 1.0 / (jnp.sum(p, axis=-1, keepdims=True) + es)
    return p * inv, es * inv


def _rot(t, cosf, sinf):
    return t * cosf + pltpu.roll(t, BLOCK // 2, 1) * sinf


def _rot_t(d, cosf, sinf):
    return d * cosf + pltpu.roll(d * sinf, BLOCK // 2, 1)


def _decay_tables(h):
    lg = LOG_GAMMA[h]
    i = lax.broadcasted_iota(jnp.int32, (BLOCK, BLOCK), 0)
    j = lax.broadcasted_iota(jnp.int32, (BLOCK, BLOCK), 1)
    diff = (i - j).astype(F32)
    dm = jnp.where(diff >= 0, jnp.exp(diff * lg), 0.0)
    row = lax.broadcasted_iota(jnp.int32, (BLOCK, 1), 0).astype(F32)
    zeta = jnp.exp((BLOCK - 1 - row) * lg)
    xi = jnp.exp((row + 1.0) * lg)
    return dm, zeta, xi, math.exp(BLOCK * lg)


def _valid_col(n):
    row = lax.broadcasted_iota(jnp.int32, (BLOCK, 1), 0)
    return ((n * BLOCK + row) >= PAD_FRONT).astype(F32)


def _shift_down(cur, prev, k):
    row = lax.broadcasted_iota(jnp.int32, cur.shape, 0)
    return jnp.where(row >= k, pltpu.roll(cur, k, 0), pltpu.roll(prev, k, 0))


def _shift_up(cur, nxt, k):
    row = lax.broadcasted_iota(jnp.int32, cur.shape, 0)
    return jnp.where(row < BLOCK - k, pltpu.roll(cur, BLOCK - k, 0), pltpu.roll(nxt, BLOCK - k, 0))


def mixers_fwd(proj, cosf, sinf, bkt, rel_bias, sinks, conv_w, nb, nc):
    def body(p_ref, cos_ref, sin_ref, bkt_ref, rb_ref, sk_ref, cw_ref, br_ref, st_ref,
             bias_s, kv_s, state_s, u_s):
        b = pl.program_id(0)
        n = pl.program_id(1)

        @pl.when((b == 0) & (n == 0))
        def _():
            _build_bias(bkt_ref, rb_ref, bias_s)

        @pl.when(n == 0)
        def _():
            kv_s[0:BLOCK, :] = jnp.zeros((BLOCK, 2 * BLOCK), F32)
            state_s[...] = jnp.zeros_like(state_s)
            u_s[...] = jnp.zeros_like(u_s)

        valid = _valid_col(n)
        mask = _band_mask(n)

        kv_s[BLOCK:2 * BLOCK, :] = p_ref[:, C_AK:C_AK + 2 * BLOCK]
        k_all = kv_s[:, 0:BLOCK]
        v_all = kv_s[:, BLOCK:2 * BLOCK]
        for kh in range(2):
            k_lo, k_hi = [t.astype(BF16) for t in _split_heads(k_all, kh)]
            v_lo, v_hi = [t.astype(BF16) for t in _split_heads(v_all, kh)]
            for jj in range(2):
                j = 2 * kh + jj
                q2b = p_ref[:, C_AQ + BLOCK * j:C_AQ + BLOCK * (j + 1)].astype(BF16)
                p_lo, _ = _softmax_sink(q2b, k_lo, bias_s[2 * j], mask, sk_ref[0, 2 * j])
                p_hi, _ = _softmax_sink(q2b, k_hi, bias_s[2 * j + 1], mask, sk_ref[0, 2 * j + 1])
                o2 = _nn(p_lo.astype(BF16), v_lo) + _nn(p_hi.astype(BF16), v_hi)
                gate = p_ref[:, C_AG + BLOCK * j:C_AG + BLOCK * (j + 1)]
                br_ref[:, BLOCK * j:BLOCK * (j + 1)] = (o2 * _silu(gate)).astype(BF16)
        kv_s[0:BLOCK, :] = kv_s[BLOCK:2 * BLOCK, :]

        cosv = cos_ref[...]
        sinv = sin_ref[...]
        for h in range(RET_HEADS):
            dm, zeta, xi, gamma_chunk = _decay_tables(h)
            q = _rot(p_ref[:, C_RQ + BLOCK * h:C_RQ + BLOCK * (h + 1)], cosv, sinv).astype(BF16)
            k = (_rot(p_ref[:, C_RK + BLOCK * h:C_RK + BLOCK * (h + 1)], cosv, sinv)
                 * RET_SCALE * valid).astype(BF16)
            v = p_ref[:, C_RV + BLOCK * h:C_RV + BLOCK * (h + 1)]
            s_prev = state_s[h]
            st_ref[0, 0, h] = s_prev
            a = (_nt(q, k) * dm).astype(BF16)
            o = _nn(a, v.astype(BF16)) + xi * _nn(q, s_prev.astype(BF16))
            mu = jnp.mean(o, axis=-1, keepdims=True)
            var = jnp.mean(jnp.square(o - mu), axis=-1, keepdims=True)
            oh = (o - mu) * lax.rsqrt(var + GN_EPS)
            gate = p_ref[:, C_RG + BLOCK * h:C_RG + BLOCK * (h + 1)]
            br_ref[:, BRANCH_WIDTH + BLOCK * h:BRANCH_WIDTH + BLOCK * (h + 1)] = (oh * _silu(gate)).astype(BF16)
            state_s[h] = gamma_chunk * s_prev + _tn(k, (v * zeta).astype(BF16))

        u = p_ref[:, C_CC:C_CC + BRANCH_WIDTH] * p_ref[:, C_CX:C_CX + BRANCH_WIDTH] * valid
        u_prev = u_s[...]
        y = (cw_ref[0:1, :] * _shift_down(u, u_prev, 2) + cw_ref[1:2, :] * _shift_down(u, u_prev, 1)
             + cw_ref[2:3, :] * u)
        yc = p_ref[:, C_CB:C_CB + BRANCH_WIDTH] * y * _silu(p_ref[:, C_CG:C_CG + BRANCH_WIDTH])
        br_ref[:, 2 * BRANCH_WIDTH:3 * BRANCH_WIDTH] = yc.astype(BF16)
        u_s[...] = u

    rows = nb * nc * BLOCK
    smem = pl.BlockSpec(memory_space=pltpu.SMEM)
    return pl.pallas_call(
        body, name="mixers_fwd",
        grid=(nb, nc),
        in_specs=[pl.BlockSpec((BLOCK, ABC_WIDTH), lambda b, n: (b * nc + n, 0)),
                  pl.BlockSpec((BLOCK, BLOCK), lambda b, n: (n, 0)),
                  pl.BlockSpec((BLOCK, BLOCK), lambda b, n: (n, 0)),
                  pl.BlockSpec((BLOCK, 2 * BLOCK), lambda b, n: (0, 0)),
                  smem, smem,
                  pl.BlockSpec((3, BRANCH_WIDTH), lambda b, n: (0, 0))],
        out_specs=[pl.BlockSpec((BLOCK, N_BRANCH * BRANCH_WIDTH), lambda b, n: (b * nc + n, 0)),
                   pl.BlockSpec((1, 1, RET_HEADS, BLOCK, BLOCK), lambda b, n: (b, n, 0, 0, 0))],
        out_shape=[jax.ShapeDtypeStruct((rows, N_BRANCH * BRANCH_WIDTH), BF16),
                   jax.ShapeDtypeStruct((nb, nc, RET_HEADS, BLOCK, BLOCK), F32)],
        scratch_shapes=[pltpu.VMEM((ATT_HEADS, BLOCK, 2 * BLOCK), F32),
                        pltpu.VMEM((2 * BLOCK, 2 * BLOCK), F32),
                        pltpu.VMEM((RET_HEADS, BLOCK, BLOCK), F32),
                        pltpu.VMEM((BLOCK, BRANCH_WIDTH), F32)],
        compiler_params=_cparams("arbitrary", "arbitrary"),
    )(proj, cosf, sinf, bkt, rel_bias, sinks, conv_w)


def mixers_bwd(proj, d_br, states, cosf, sinf, bkt, rel_bias, sinks, conv_w, nb, nc):
    def body(p_ref, kvp_ref, cp_ref, dbr_ref, st_ref, cos_ref, sin_ref, bkt_ref, rb_ref, sk_ref, cw_ref,
             dp_ref, drb_ref, dsk_ref, dcw_ref,
             bias_s, dbias_s, dkv_s, g_s, dy_s):
        b = pl.program_id(0)
        step = pl.program_id(1)
        n = nc - 1 - step
        first = (b == 0) & (step == 0)
        last = (b == nb - 1) & (step == nc - 1)

        @pl.when(first)
        def _():
            _build_bias(bkt_ref, rb_ref, bias_s)
            dbias_s[...] = jnp.zeros_like(dbias_s)
            dsk_ref[...] = jnp.zeros_like(dsk_ref)
            dcw_ref[...] = jnp.zeros_like(dcw_ref)
            drb_ref[...] = jnp.zeros_like(drb_ref)

        @pl.when(step == 0)
        def _():
            dkv_s[...] = jnp.zeros_like(dkv_s)
            g_s[...] = jnp.zeros_like(g_s)
            dy_s[...] = jnp.zeros_like(dy_s)

        valid = _valid_col(n)
        mask = _band_mask(n)
        has_prev = (n > 0).astype(F32)

        kv_prev = kvp_ref[...] * has_prev
        kv_cur = p_ref[:, C_AK:C_AK + 2 * BLOCK]
        k_all = jnp.concatenate([kv_prev[:, 0:BLOCK], kv_cur[:, 0:BLOCK]], axis=0)
        v_all = jnp.concatenate([kv_prev[:, BLOCK:], kv_cur[:, BLOCK:]], axis=0)
        dk_tot = jnp.zeros((2 * BLOCK, BLOCK), F32)
        dv_tot = jnp.zeros((2 * BLOCK, BLOCK), F32)
        for kh in range(2):
            ks = [t.astype(BF16) for t in _split_heads(k_all, kh)]
            vs = [t.astype(BF16) for t in _split_heads(v_all, kh)]
            dk_acc = [jnp.zeros((2 * BLOCK, BLOCK), F32), jnp.zeros((2 * BLOCK, BLOCK), F32)]
            dv_acc = [jnp.zeros((2 * BLOCK, BLOCK), F32), jnp.zeros((2 * BLOCK, BLOCK), F32)]
            for jj in range(2):
                j = 2 * kh + jj
                q2b = p_ref[:, C_AQ + BLOCK * j:C_AQ + BLOCK * (j + 1)].astype(BF16)
                gate = p_ref[:, C_AG + BLOCK * j:C_AG + BLOCK * (j + 1)]
                d_ya = dbr_ref[:, BLOCK * j:BLOCK * (j + 1)]
                do2 = (d_ya * _silu(gate)).astype(BF16)
                o2 = jnp.zeros((BLOCK, BLOCK), F32)
                dq2 = jnp.zeros((BLOCK, BLOCK), F32)
                for x in range(2):
                    h = 2 * j + x
                    p, p_sink = _softmax_sink(q2b, ks[x], bias_s[h], mask, sk_ref[0, h])
                    pb = p.astype(BF16)
                    o2 = o2 + _nn(pb, vs[x])
                    dp = _nt(do2, vs[x])
                    delta = jnp.sum(p * dp, axis=-1, keepdims=True)
                    ds = p * (dp - delta)
                    dbias_s[h] += ds
                    dsk_ref[h:h + 1, :] += jnp.broadcast_to(
                        jnp.sum(-p_sink * delta, axis=0, keepdims=True), (1, BLOCK))
                    dsb = ds.astype(BF16)
                    dq2 = dq2 + _nn(dsb, ks[x]) * ATT_SCALE
                    dk_acc[x] = dk_acc[x] + _tn(dsb, q2b) * ATT_SCALE
                    dv_acc[x] = dv_acc[x] + _tn(pb, do2)
                dp_ref[:, C_AQ + BLOCK * j:C_AQ + BLOCK * (j + 1)] = dq2.astype(BF16)
                dp_ref[:, C_AG + BLOCK * j:C_AG + BLOCK * (j + 1)] = (d_ya * o2 * _dsilu(gate)).astype(BF16)
            dk_tot = dk_tot + _merge_heads(dk_acc[0], dk_acc[1], kh)
            dv_tot = dv_tot + _merge_heads(dv_acc[0], dv_acc[1], kh)
        dp_ref[:, C_AK:C_AK + BLOCK] = (dk_tot[BLOCK:, :] + dkv_s[:, 0:BLOCK]).astype(BF16)
        dp_ref[:, C_AV:C_AV + BLOCK] = (dv_tot[BLOCK:, :] + dkv_s[:, BLOCK:]).astype(BF16)
        dkv_s[:, 0:BLOCK] = dk_tot[0:BLOCK, :]
        dkv_s[:, BLOCK:] = dv_tot[0:BLOCK, :]

        cosv = cos_ref[...]
        sinv = sin_ref[...]
        for h in range(RET_HEADS):
            dm, zeta, xi, gamma_chunk = _decay_tables(h)
            sl = lambda c0: slice(c0 + BLOCK * h, c0 + BLOCK * (h + 1))
            q = _rot(p_ref[:, sl(C_RQ)], cosv, sinv).astype(BF16)
            k = (_rot(p_ref[:, sl(C_RK)], cosv, sinv) * RET_SCALE * valid).astype(BF16)
            v = p_ref[:, sl(C_RV)]
            vb = v.astype(BF16)
            gate = p_ref[:, sl(C_RG)]
            s_prev = st_ref[0, 0, h].astype(BF16)
            g_next = g_s[h]
            a = (_nt(q, k) * dm).astype(BF16)
            o = _nn(a, vb) + xi * _nn(q, s_prev)
            mu = jnp.mean(o, axis=-1, keepdims=True)
            var = jnp.mean(jnp.square(o - mu), axis=-1, keepdims=True)
            rstd = lax.rsqrt(var + GN_EPS)
            oh = (o - mu) * rstd
            d_yr = dbr_ref[:, BRANCH_WIDTH + BLOCK * h:BRANCH_WIDTH + BLOCK * (h + 1)]
            dp_ref[:, sl(C_RG)] = (d_yr * oh * _dsilu(gate)).astype(BF16)
            doh = d_yr * _silu(gate)
            do = rstd * (doh - jnp.mean(doh, axis=-1, keepdims=True)
                         - oh * jnp.mean(doh * oh, axis=-1, keepdims=True))
            dob = do.astype(BF16)
            dxo = (do * xi).astype(BF16)
            da = (_nt(dob, vb) * dm).astype(BF16)
            gb = g_next.astype(BF16)
            zv = (v * zeta).astype(BF16)
            dq = _nn(da, k) + _nt(dxo, s_prev)
            dk = _tn(da, q) + _nt(zv, gb)
            dv = _tn(a, dob) + zeta * _nn(k, gb)
            g_s[h] = gamma_chunk * g_next + _tn(q, dxo)
            dp_ref[:, sl(C_RQ)] = _rot_t(dq, cosv, sinv).astype(BF16)
            dp_ref[:, sl(C_RK)] = _rot_t(dk * (RET_SCALE * valid), cosv, sinv).astype(BF16)
            dp_ref[:, sl(C_RV)] = dv.astype(BF16)

        w0, w1, w2 = cw_ref[0:1, :], cw_ref[1:2, :], cw_ref[2:3, :]
        cb = p_ref[:, C_CB:C_CB + BRANCH_WIDTH]
        cc = p_ref[:, C_CC:C_CC + BRANCH_WIDTH]
        cx = p_ref[:, C_CX:C_CX + BRANCH_WIDTH]
        cg = p_ref[:, C_CG:C_CG + BRANCH_WIDTH]
        u = cc * cx * valid
        u_prev = cp_ref[:, 0:BRANCH_WIDTH] * cp_ref[:, BRANCH_WIDTH:2 * BRANCH_WIDTH] * (_valid_col(n - 1) * has_prev)
        u1 = _shift_down(u, u_prev, 1)
        u2 = _shift_down(u, u_prev, 2)
        y = w0 * u2 + w1 * u1 + w2 * u
        d_yc = dbr_ref[:, 2 * BRANCH_WIDTH:3 * BRANCH_WIDTH]
        sg = _silu(cg)
        dp_ref[:, C_CB:C_CB + BRANCH_WIDTH] = (d_yc * y * sg).astype(BF16)
        dp_ref[:, C_CG:C_CG + BRANCH_WIDTH] = (d_yc * cb * y * _dsilu(cg)).astype(BF16)
        dy = d_yc * cb * sg
        dy_next = dy_s[...]
        du = (w2 * dy + w1 * _shift_up(dy, dy_next, 1) + w0 * _shift_up(dy, dy_next, 2)) * valid
        dp_ref[:, C_CC:C_CC + BRANCH_WIDTH] = (du * cx).astype(BF16)
        dp_ref[:, C_CX:C_CX + BRANCH_WIDTH] = (du * cc).astype(BF16)
        dcw_ref[0:1, :] += jnp.sum(dy * u2, axis=0, keepdims=True)
        dcw_ref[1:2, :] += jnp.sum(dy * u1, axis=0, keepdims=True)
        dcw_ref[2:3, :] += jnp.sum(dy * u, axis=0, keepdims=True)
        dy_s[...] = dy

        @pl.when(last)
        def _():
            bkt = bkt_ref[...]
            row = lax.broadcasted_iota(jnp.int32, (N_BUCKETS, BLOCK), 0)
            lane = lax.broadcasted_iota(jnp.int32, (N_BUCKETS, BLOCK), 1)

            def one_bucket(bk, acc):
                sel = bkt == bk
                for h in range(ATT_HEADS):
                    t = jnp.where(sel, dbias_s[h], 0.0)
                    s = jnp.sum(jnp.sum(t, axis=1, keepdims=True), axis=0, keepdims=True)
                    acc = acc + jnp.where((row == bk) & (lane == h), jnp.broadcast_to(s, acc.shape), 0.0)
                return acc

            drb_ref[...] = lax.fori_loop(0, N_BUCKETS, one_bucket, jnp.zeros((N_BUCKETS, BLOCK), F32))

    rows = nb * nc * BLOCK
    smem = pl.BlockSpec(memory_space=pltpu.SMEM)
    blk = lambda b, s: b * nc + (nc - 1 - s)
    prev = lambda b, s: b * nc + jnp.maximum(nc - 2 - s, 0)
    return pl.pallas_call(
        body, name="mixers_bwd",
        grid=(nb, nc),
        in_specs=[pl.BlockSpec((BLOCK, ABC_WIDTH), lambda b, s: (blk(b, s), 0)),
                  pl.BlockSpec((BLOCK, 2 * BLOCK), lambda b, s: (prev(b, s), C_AK // (2 * BLOCK))),
                  pl.BlockSpec((BLOCK, 1280), lambda b, s: (prev(b, s), C_CC // 1280)),
                  pl.BlockSpec((BLOCK, N_BRANCH * BRANCH_WIDTH), lambda b, s: (blk(b, s), 0)),
                  pl.BlockSpec((1, 1, RET_HEADS, BLOCK, BLOCK), lambda b, s: (b, nc - 1 - s, 0, 0, 0)),
                  pl.BlockSpec((BLOCK, BLOCK), lambda b, s: (nc - 1 - s, 0)),
                  pl.BlockSpec((BLOCK, BLOCK), lambda b, s: (nc - 1 - s, 0)),
                  pl.BlockSpec((BLOCK, 2 * BLOCK), lambda b, s: (0, 0)),
                  smem, smem,
                  pl.BlockSpec((3, BRANCH_WIDTH), lambda b, s: (0, 0))],
        out_specs=[pl.BlockSpec((BLOCK, ABC_WIDTH), lambda b, s: (blk(b, s), 0)),
                   pl.BlockSpec((N_BUCKETS, BLOCK), lambda b, s: (0, 0)),
                   pl.BlockSpec((ATT_HEADS, BLOCK), lambda b, s: (0, 0)),
                   pl.BlockSpec((8, BRANCH_WIDTH), lambda b, s: (0, 0))],
        out_shape=[jax.ShapeDtypeStruct((rows, ABC_WIDTH), BF16),
                   jax.ShapeDtypeStruct((N_BUCKETS, BLOCK), F32),
                   jax.ShapeDtypeStruct((ATT_HEADS, BLOCK), F32),
                   jax.ShapeDtypeStruct((8, BRANCH_WIDTH), F32)],
        scratch_shapes=[pltpu.VMEM((ATT_HEADS, BLOCK, 2 * BLOCK), F32),
                        pltpu.VMEM((ATT_HEADS, BLOCK, 2 * BLOCK), F32),
                        pltpu.VMEM((BLOCK, 2 * BLOCK), F32),
                        pltpu.VMEM((RET_HEADS, BLOCK, BLOCK), F32),
                        pltpu.VMEM((BLOCK, BRANCH_WIDTH), F32)],
        compiler_params=_cparams("arbitrary", "arbitrary"),
    )(proj, proj, proj, d_br, states, cosf, sinf, bkt, rel_bias, sinks, conv_w)


MERGE_TILE = 256


def _merge_forward(br_ref, m_ref, wb_ref, wo_ref):
    bo, gates = [], []
    mixed_pre = None
    for g in range(N_BRANCH):
        br_g = br_ref[:, BRANCH_WIDTH * g:BRANCH_WIDTH * (g + 1)]
        bo_g = jnp.concatenate([_nn(br_g, wb_ref[p, g]) for p in range(N_CHIPS)], axis=1)
        gate_g = jax.nn.sigmoid(m_ref[:, D_MODEL * g:D_MODEL * (g + 1)])
        bo.append(bo_g)
        gates.append(gate_g)
        mixed_pre = gate_g * bo_g if mixed_pre is None else mixed_pre + gate_g * bo_g
    mixed = _nn(mixed_pre.astype(BF16), wo_ref[...])
    r = lax.rsqrt(jnp.mean(mixed * mixed, axis=-1, keepdims=True) + RMS_EPS)
    return bo, gates, mixed_pre, mixed, r


def merge_fwd(x2d, br, pm, wb, wo, g_post):
    t = x2d.shape[0]
    tm = MERGE_TILE if t % MERGE_TILE == 0 else BLOCK

    def body(x_ref, br_ref, m_ref, wb_ref, wo_ref, g_ref, o_ref):
        _, _, _, mixed, r = _merge_forward(br_ref, m_ref, wb_ref, wo_ref)
        o_ref[...] = x_ref[...] + mixed * r * g_ref[...]

    return pl.pallas_call(
        body, name="merge_fwd",
        grid=(t // tm,),
        in_specs=[pl.BlockSpec((tm, D_MODEL), lambda i: (i, 0)),
                  pl.BlockSpec((tm, N_BRANCH * BRANCH_WIDTH), lambda i: (i, 0)),
                  pl.BlockSpec((tm, MERGE_WIDTH), lambda i: (i, 0)),
                  pl.BlockSpec((N_CHIPS, N_BRANCH, BRANCH_WIDTH, SHARD_D), lambda i: (0, 0, 0, 0)),
                  pl.BlockSpec((D_MODEL, D_MODEL), lambda i: (0, 0)),
                  pl.BlockSpec((1, D_MODEL), lambda i: (0, 0))],
        out_specs=pl.BlockSpec((tm, D_MODEL), lambda i: (i, 0)),
        out_shape=jax.ShapeDtypeStruct((t, D_MODEL), F32),
        compiler_params=_cparams("parallel"),
    )(x2d, br, pm, wb, wo, g_post)


def merge_bwd(d_out, br, pm, wb, wo, g_post):
    t = d_out.shape[0]
    tm = MERGE_TILE if t % MERGE_TILE == 0 else BLOCK

    def body(do_ref, br_ref, m_ref, wb_ref, wo_ref, g_ref, dbr_ref, dm_ref, dg_ref, dwb_ref, dwo_ref):

        @pl.when(pl.program_id(0) == 0)
        def _():
            dwb_ref[...] = jnp.zeros_like(dwb_ref)
            dwo_ref[...] = jnp.zeros_like(dwo_ref)
            dg_ref[...] = jnp.zeros_like(dg_ref)

        bo, gates, mixed_pre, mixed, r = _merge_forward(br_ref, m_ref, wb_ref, wo_ref)
        d_o = do_ref[...]
        nh = mixed * r
        dg_ref[0:1, :] += jnp.sum(d_o * nh, axis=0, keepdims=True)
        dn = d_o * g_ref[...]
        d_mixed = (r * (dn - nh * jnp.mean(dn * nh, axis=-1, keepdims=True))).astype(BF16)
        dwo_ref[...] += _tn(mixed_pre.astype(BF16), d_mixed)
        d_pre = _nt(d_mixed, wo_ref[...])
        for g in range(N_BRANCH):
            br_g = br_ref[:, BRANCH_WIDTH * g:BRANCH_WIDTH * (g + 1)]
            d_bo = (d_pre * gates[g]).astype(BF16)
            dm_ref[:, D_MODEL * g:D_MODEL * (g + 1)] = (
                d_pre * bo[g] * gates[g] * (1.0 - gates[g])).astype(BF16)
            d_br_g = None
            for p in range(N_CHIPS):
                d_bo_p = d_bo[:, SHARD_D * p:SHARD_D * (p + 1)]
                part = _nt(d_bo_p, wb_ref[p, g])
                d_br_g = part if d_br_g is None else d_br_g + part
                dwb_ref[p, g] += _tn(br_g, d_bo_p)
            dbr_ref[:, BRANCH_WIDTH * g:BRANCH_WIDTH * (g + 1)] = d_br_g

    return pl.pallas_call(
        body, name="merge_bwd",
        grid=(t // tm,),
        in_specs=[pl.BlockSpec((tm, D_MODEL), lambda i: (i, 0)),
                  pl.BlockSpec((tm, N_BRANCH * BRANCH_WIDTH), lambda i: (i, 0)),
                  pl.BlockSpec((tm, MERGE_WIDTH), lambda i: (i, 0)),
                  pl.BlockSpec((N_CHIPS, N_BRANCH, BRANCH_WIDTH, SHARD_D), lambda i: (0, 0, 0, 0)),
                  pl.BlockSpec((D_MODEL, D_MODEL), lambda i: (0, 0)),
                  pl.BlockSpec((1, D_MODEL), lambda i: (0, 0))],
        out_specs=[pl.BlockSpec((tm, N_BRANCH * BRANCH_WIDTH), lambda i: (i, 0)),
                   pl.BlockSpec((tm, MERGE_WIDTH), lambda i: (i, 0)),
                   pl.BlockSpec((8, D_MODEL), lambda i: (0, 0)),
                   pl.BlockSpec((N_CHIPS, N_BRANCH, BRANCH_WIDTH, SHARD_D), lambda i: (0, 0, 0, 0)),
                   pl.BlockSpec((D_MODEL, D_MODEL), lambda i: (0, 0))],
        out_shape=[jax.ShapeDtypeStruct((t, N_BRANCH * BRANCH_WIDTH), F32),
                   jax.ShapeDtypeStruct((t, MERGE_WIDTH), BF16),
                   jax.ShapeDtypeStruct((8, D_MODEL), F32),
                   jax.ShapeDtypeStruct((N_CHIPS, N_BRANCH, BRANCH_WIDTH, SHARD_D), F32),
                   jax.ShapeDtypeStruct((D_MODEL, D_MODEL), F32)],
        compiler_params=_cparams("arbitrary"),
    )(d_out, br, pm, wb, wo, g_post)


def loss_head(xf, target2d, nb, nc):
    def body(x_ref, t_ref, l_ref, dx_ref):
        b = pl.program_id(0)
        n = pl.program_id(1)

        @pl.when((b == 0) & (n == 0))
        def _():
            l_ref[...] = jnp.zeros_like(l_ref)

        @pl.when(n == 0)
        def _():
            dx_ref[...] = jnp.zeros_like(dx_ref)

        @pl.when(n > 0)
        def _():
            e = x_ref[...] - t_ref[...]
            dx_ref[...] = e * (1.0 / D_MODEL)
            s = jnp.sum(jnp.sum(e * e, axis=1, keepdims=True), axis=0, keepdims=True)
            l_ref[...] += jnp.broadcast_to(s * (0.5 / D_MODEL), l_ref.shape)

    return pl.pallas_call(
        body, name="loss_head",
        grid=(nb, nc),
        in_specs=[pl.BlockSpec((BLOCK, D_MODEL), lambda b, n: (b * nc + n, 0)),
                  pl.BlockSpec((BLOCK, D_MODEL), lambda b, n: (b * (nc - 1) + jnp.maximum(n - 1, 0), 0))],
        out_specs=[pl.BlockSpec((8, BLOCK), lambda b, n: (0, 0)),
                   pl.BlockSpec((BLOCK, D_MODEL), lambda b, n: (b * nc + n, 0))],
        out_shape=[jax.ShapeDtypeStruct((8, BLOCK), F32),
                   jax.ShapeDtypeStruct(xf.shape, F32)],
        compiler_params=_cparams("arbitrary", "arbitrary"),
    )(xf, target2d)


N_ABC_TILES = ABC_WIDTH // COL_TILE
N_M_TILES = MERGE_WIDTH // COL_TILE


def proj_dgrad(d_abc, d_m, w, x2d, g, d_out):
    t = x2d.shape[0]
    tm = ROW_TILE if t % ROW_TILE == 0 else BLOCK
    nk = N_ABC_TILES + N_M_TILES

    def body(da_ref, dm_ref, w_ref, x_ref, g_ref, do_ref, dx_ref, dg_ref, acc):
        i = pl.program_id(0)
        k = pl.program_id(1)

        @pl.when((i == 0) & (k == 0))
        def _():
            dg_ref[...] = jnp.zeros_like(dg_ref)

        @pl.when(k == 0)
        def _():
            acc[...] = jnp.zeros_like(acc)

        @pl.when(k < N_ABC_TILES)
        def _():
            acc[...] += _nn(da_ref[...], w_ref[...])

        @pl.when(k >= N_ABC_TILES)
        def _():
            acc[...] += _nn(dm_ref[...], w_ref[...])

        @pl.when(k == nk - 1)
        def _():
            x = x_ref[...]
            r = lax.rsqrt(jnp.mean(x * x, axis=-1, keepdims=True) + RMS_EPS)
            nh = x * r
            dh = acc[...]
            dg_ref[0:1, :] += jnp.sum(dh * nh, axis=0, keepdims=True)
            dn = dh * g_ref[...]
            dx_ref[...] = do_ref[...] + r * (dn - nh * jnp.mean(dn * nh, axis=-1, keepdims=True))

    return pl.pallas_call(
        body, name="proj_dgrad",
        grid=(t // tm, nk),
        in_specs=[pl.BlockSpec((tm, COL_TILE), lambda i, k: (i, jnp.minimum(k, N_ABC_TILES - 1))),
                  pl.BlockSpec((tm, COL_TILE), lambda i, k: (i, jnp.maximum(k - N_ABC_TILES, 0))),
                  pl.BlockSpec((COL_TILE, D_MODEL), lambda i, k: (k, 0)),
                  pl.BlockSpec((tm, D_MODEL), lambda i, k: (i, 0)),
                  pl.BlockSpec((1, D_MODEL), lambda i, k: (0, 0)),
                  pl.BlockSpec((tm, D_MODEL), lambda i, k: (i, 0))],
        out_specs=[pl.BlockSpec((tm, D_MODEL), lambda i, k: (i, 0)),
                   pl.BlockSpec((8, D_MODEL), lambda i, k: (0, 0))],
        out_shape=[jax.ShapeDtypeStruct((t, D_MODEL), F32),
                   jax.ShapeDtypeStruct((8, D_MODEL), F32)],
        scratch_shapes=[pltpu.VMEM((tm, D_MODEL), F32)],
        compiler_params=_cparams("arbitrary", "arbitrary"),
    )(d_abc, d_m, w, x2d, g, d_out)


def proj_wgrad(hb, d_abc, d_m):
    t = hb.shape[0]
    nj = N_ABC_TILES + N_M_TILES

    def body(h_ref, da_ref, dm_ref, o_ref):
        j = pl.program_id(0)

        @pl.when(j < N_ABC_TILES)
        def _():
            o_ref[...] = _tn(da_ref[...], h_ref[...])

        @pl.when(j >= N_ABC_TILES)
        def _():
            o_ref[...] = _tn(dm_ref[...], h_ref[...])

    return pl.pallas_call(
        body, name="proj_wgrad",
        grid=(nj,),
        in_specs=[pl.BlockSpec((t, D_MODEL), lambda j: (0, 0)),
                  pl.BlockSpec((t, COL_TILE), lambda j: (0, jnp.minimum(j, N_ABC_TILES - 1))),
                  pl.BlockSpec((t, COL_TILE), lambda j: (0, jnp.maximum(j - N_ABC_TILES, 0)))],
        out_specs=pl.BlockSpec((COL_TILE, D_MODEL), lambda j: (j, 0)),
        out_shape=jax.ShapeDtypeStruct((PROJ_WIDTH, D_MODEL), F32),
        compiler_params=_cparams("arbitrary"),
    )(hb, d_abc, d_m)


def _adamw_math(w, g, m, v):
    m = ADAM_B1 * m + (1.0 - ADAM_B1) * g
    v = ADAM_B2 * v + (1.0 - ADAM_B2) * jnp.square(g)
    m_hat = m / (1.0 - ADAM_B1 ** ADAM_STEP)
    v_hat = v / (1.0 - ADAM_B2 ** ADAM_STEP)
    delta = -ADAM_LR * (m_hat / (jnp.sqrt(v_hat) + ADAM_EPS) + ADAM_WD * w)
    return delta, m, v


def adamw_layer(w, g, m, v, layer, acc):
    _, r, c = w.shape
    tr = _row_tile(r)

    def body(*refs):
        w_ref, g_ref, m_ref, v_ref = refs[:4]
        go_ref, d_ref, mo_ref, vo_ref = refs[-4:]
        g_val = g_ref[...]
        d, m_new, v_new = _adamw_math(w_ref[...], g_val, m_ref[...], v_ref[...])
        go_ref[...] = g_val
        d_ref[...] = d
        mo_ref[...] = m_new
        vo_ref[...] = v_new

    slab = pl.BlockSpec((None, tr, c), lambda i: (layer, i, 0))
    ins = [w, g, m, v]
    in_specs = [slab, pl.BlockSpec((tr, c), lambda i: (i, 0)), slab, slab]
    aliases = {}
    if acc is not None:
        ins += list(acc)
        in_specs += [ANY] * 4
        aliases = {4 + i: i for i in range(4)}
    return pl.pallas_call(
        body, name="adamw_layer",
        grid=(r // tr,),
        in_specs=in_specs, out_specs=[slab] * 4,
        out_shape=[jax.ShapeDtypeStruct(w.shape, F32)] * 4,
        input_output_aliases=aliases,
        compiler_params=_cparams("parallel"),
    )(*ins)


def adamw_small(params):
    k = len(params)

    def body(*refs):
        ins, outs = refs[:4 * k], refs[4 * k:]
        for i in range(k):
            d, m_new, v_new = _adamw_math(*[r[...] for r in ins[4 * i:4 * i + 4]])
            outs[3 * i][...] = d
            outs[3 * i + 1][...] = m_new
            outs[3 * i + 2][...] = v_new

    flat = [a for p in params for a in p]
    vm = pl.BlockSpec(memory_space=pltpu.VMEM)
    out_shape = [jax.ShapeDtypeStruct(p[0].shape, F32) for p in params for _ in range(3)]
    res = pl.pallas_call(
        body, name="adamw_small",
        in_specs=[vm] * len(flat), out_specs=[vm] * len(out_shape), out_shape=out_shape,
    )(*flat)
    return [tuple(res[3 * i:3 * i + 3]) for i in range(k)]


ANY = pl.BlockSpec(memory_space=pl.ANY)


def _place():
    return lax.axis_index("x"), lax.axis_index("y"), lax.axis_index("c")


HBM = pl.BlockSpec(memory_space=pltpu.HBM)
SEM = pl.BlockSpec(memory_space=pltpu.SEMAPHORE)
EFFECT = pltpu.SideEffectType.DATAFLOW_SIDE_EFFECTING


def _other_chips(x, y):
    return [(1 - x, y), (x, 1 - y), (1 - x, 1 - y)]


def _own_slot(shard, chip):
    buf = lax.empty((N_CHIPS,) + shard.shape, shard.dtype)
    return lax.dynamic_update_slice(buf, shard[None], (chip, 0, 0, 0))


def gather_weight_shards(bufs):
    n = len(bufs)

    def body(*refs):
        g_refs = refs[n:2 * n]
        send_sems, recv_sems = refs[2 * n:]
        x, y, c = _place()
        me_p = 2 * x + y
        sibling = (x, y, 1 - c)
        chips = _other_chips(x, y)

        def copy(k, slab, to):
            return pltpu.make_async_remote_copy(src_ref=slab, dst_ref=slab, send_sem=send_sems.at[k],
                                                recv_sem=recv_sems.at[k], device_id=to, device_id_type=MESH)

        first, passed = [], []
        for t in range(n):
            for k, (qx, qy) in enumerate(chips):
                cp = copy(6 * t + k, g_refs[t].at[me_p, c], (qx, qy, c))
                cp.start()
                first.append(cp)
        for t in range(n):
            for k, (qx, qy) in enumerate(chips):
                slab = g_refs[t].at[2 * qx + qy, c]
                copy(6 * t + k, slab, (qx, qy, c)).wait_recv()
                fwd = copy(6 * t + 3 + k, slab, sibling)
                fwd.start()
                passed.append(fwd)
        for t in range(n):
            for k, (qx, qy) in enumerate(chips):
                copy(6 * t + 3 + k, g_refs[t].at[2 * qx + qy, 1 - c], sibling).wait_recv()
        for cp in first + passed:
            cp.wait_send()

    return pl.pallas_call(
        body, name="gather_weight_shards",
        in_specs=[ANY] * n, out_specs=[ANY] * n,
        out_shape=[jax.ShapeDtypeStruct(b.shape, b.dtype) for b in bufs],
        input_output_aliases={t: t for t in range(n)},
        scratch_shapes=[pltpu.SemaphoreType.DMA((6 * n,)), pltpu.SemaphoreType.DMA((6 * n,))],
    )(*bufs)


def _hbm(a):
    return pltpu.with_memory_space_constraint(a, pltpu.HBM)


def gather_start(bufs, after):
    n = len(bufs)

    def body(*refs):
        g_refs = refs[:n]
        send_sems, recv_sems = refs[n + 1], refs[n + 2]
        token = refs[-1]
        x, y, c = _place()
        me_p = 2 * x + y
        for t in range(n):
            for k, (qx, qy) in enumerate(_other_chips(x, y)):
                slab = g_refs[t].at[me_p, c]
                pltpu.make_async_remote_copy(src_ref=slab, dst_ref=slab, send_sem=send_sems.at[3 * t + k],
                                             recv_sem=recv_sems.at[3 * t + k], device_id=(qx, qy, c),
                                             device_id_type=MESH).start()
        token[...] = jnp.zeros_like(token)

    res = pl.pallas_call(
        body, name="gather_start",
        in_specs=[HBM] * n + [ANY],
        out_specs=[SEM, SEM] + [HBM] * n + [pl.BlockSpec(memory_space=pltpu.VMEM)],
        out_shape=[pltpu.SemaphoreType.DMA((3 * n,)), pltpu.SemaphoreType.DMA((3 * n,))]
        + [pltpu.HBM(b.shape, b.dtype) for b in bufs] + [jax.ShapeDtypeStruct((8, LANES), F32)],
        input_output_aliases={t: 2 + t for t in range(n)},
        compiler_params=pltpu.CompilerParams(has_side_effects=EFFECT),
    )(*[_hbm(b) for b in bufs], after)
    return res[0], res[1], list(res[2:2 + n]), res[-1]


def gather_wait(bufs, send_sems, recv_sems, after):
    n = len(bufs)

    def body(*refs):
        g_refs = refs[:n]
        send_sems, recv_sems = refs[n], refs[n + 1]
        x, y, c = _place()
        me_p = 2 * x + y
        for t in range(n):
            for k, (qx, qy) in enumerate(_other_chips(x, y)):
                cp = pltpu.make_async_remote_copy(src_ref=g_refs[t].at[me_p, c], dst_ref=g_refs[t].at[2 * qx + qy, c],
                                                  send_sem=send_sems.at[3 * t + k], recv_sem=recv_sems.at[3 * t + k],
                                                  device_id=(qx, qy, c), device_id_type=MESH)
                cp.wait_send()
                cp.wait_recv()

    return pl.pallas_call(
        body, name="gather_wait",
        in_specs=[HBM] * n + [SEM, SEM, ANY],
        out_specs=[HBM] * n,
        out_shape=[pltpu.HBM(b.shape, b.dtype) for b in bufs],
        input_output_aliases={t: t for t in range(n)},
        compiler_params=pltpu.CompilerParams(has_side_effects=EFFECT),
    )(*bufs, send_sems, recv_sems, after)


def gather_forward(bufs):
    n = len(bufs)

    def body(*refs):
        g_refs = refs[n:2 * n]
        send_sems, recv_sems = refs[2 * n:]
        x, y, c = _place()
        sibling = (x, y, 1 - c)
        chips = _other_chips(x, y)
        passed = []
        for t in range(n):
            for k, (qx, qy) in enumerate(chips):
                slab = g_refs[t].at[2 * qx + qy, c]
                fwd = pltpu.make_async_remote_copy(src_ref=slab, dst_ref=slab, send_sem=send_sems.at[3 * t + k],
                                                   recv_sem=recv_sems.at[3 * t + k], device_id=sibling,
                                                   device_id_type=MESH)
                fwd.start()
                passed.append(fwd)
        for t in range(n):
            for k, (qx, qy) in enumerate(chips):
                slab = g_refs[t].at[2 * qx + qy, 1 - c]
                pltpu.make_async_remote_copy(src_ref=slab, dst_ref=slab, send_sem=send_sems.at[3 * t + k],
                                             recv_sem=recv_sems.at[3 * t + k], device_id=sibling,
                                             device_id_type=MESH).wait_recv()
        for cp in passed:
            cp.wait_send()

    return pl.pallas_call(
        body, name="gather_forward",
        in_specs=[ANY] * n, out_specs=[ANY] * n,
        out_shape=[jax.ShapeDtypeStruct(b.shape, b.dtype) for b in bufs],
        input_output_aliases={t: t for t in range(n)},
        scratch_shapes=[pltpu.SemaphoreType.DMA((3 * n,)), pltpu.SemaphoreType.DMA((3 * n,))],
    )(*bufs)


def exchange_small(pack):
    def body(p_ref, o_ref, send_sems, recv_sems, local_sem):
        x, y, c = _place()
        me = 4 * x + 2 * y + c
        mine = pltpu.make_async_copy(p_ref, o_ref.at[me], local_sem)
        mine.start()
        sends = []
        for k in range(1, 8):
            fx, fy, fc = (k >> 2) & 1, (k >> 1) & 1, k & 1
            peer = (x ^ fx, y ^ fy, c ^ fc)
            cp = pltpu.make_async_remote_copy(src_ref=p_ref, dst_ref=o_ref.at[me], send_sem=send_sems.at[k - 1],
                                              recv_sem=recv_sems.at[k - 1], device_id=peer, device_id_type=MESH)
            cp.start()
            sends.append(cp)
        for k in range(1, 8):
            fx, fy, fc = (k >> 2) & 1, (k >> 1) & 1, k & 1
            peer = (x ^ fx, y ^ fy, c ^ fc)
            slot = o_ref.at[4 * peer[0] + 2 * peer[1] + peer[2]]
            pltpu.make_async_remote_copy(src_ref=slot, dst_ref=slot, send_sem=send_sems.at[k - 1],
                                         recv_sem=recv_sems.at[k - 1], device_id=peer, device_id_type=MESH).wait_recv()
        for cp in sends:
            cp.wait_send()
        mine.wait()

    return pl.pallas_call(
        body, name="exchange_small",
        in_specs=[ANY], out_specs=ANY,
        out_shape=jax.ShapeDtypeStruct((8,) + pack.shape, pack.dtype),
        scratch_shapes=[pltpu.SemaphoreType.DMA((7,)), pltpu.SemaphoreType.DMA((7,)), pltpu.SemaphoreType.DMA],
    )(pack)


def sibling_swap_halves(grads):
    n = len(grads)

    def body(*refs):
        g_refs, o_refs = refs[:n], refs[n:2 * n]
        send_sems, recv_sems = refs[2 * n:]
        x, y, c = _place()
        cps = []
        for t in range(n):
            for p in range(N_CHIPS):
                cp = pltpu.make_async_remote_copy(src_ref=g_refs[t].at[p, 1 - c], dst_ref=o_refs[t].at[p],
                                                  send_sem=send_sems.at[N_CHIPS * t + p],
                                                  recv_sem=recv_sems.at[N_CHIPS * t + p],
                                                  device_id=(x, y, 1 - c), device_id_type=MESH)
                cp.start()
                cps.append(cp)
        for cp in cps:
            cp.wait()

    return pl.pallas_call(
        body, name="sibling_swap_halves",
        in_specs=[ANY] * n, out_specs=[ANY] * n,
        out_shape=[jax.ShapeDtypeStruct((N_CHIPS,) + g.shape[2:], g.dtype) for g in grads],
        scratch_shapes=[pltpu.SemaphoreType.DMA((N_CHIPS * n,)), pltpu.SemaphoreType.DMA((N_CHIPS * n,))],
    )(*grads)


def _row_tile(r):
    return max(t for t in range(16, 513, 16) if r % t == 0)


def add_own_half(g, other, c_arr):
    _, _, r, cols = g.shape
    tr = _row_tile(r)

    def body(c_ref, a_ref, b_ref, o_ref):
        o_ref[...] = (a_ref[...] + b_ref[...]).astype(BF16)

    return pl.pallas_call(
        body, name="add_own_half",
        grid_spec=pltpu.PrefetchScalarGridSpec(
            num_scalar_prefetch=1, grid=(N_CHIPS, r // tr),
            in_specs=[pl.BlockSpec((None, None, tr, cols), lambda p, i, c_ref: (p, c_ref[0], i, 0)),
                      pl.BlockSpec((None, tr, cols), lambda p, i, c_ref: (p, i, 0))],
            out_specs=pl.BlockSpec((None, tr, cols), lambda p, i, c_ref: (p, i, 0))),
        out_shape=jax.ShapeDtypeStruct((N_CHIPS, r, cols), BF16),
        compiler_params=_cparams("parallel", "parallel"),
    )(c_arr, g, other)


def scatter_to_chips(partials):
    n = len(partials)

    def body(*refs):
        s_refs, o_refs = refs[:n], refs[n:2 * n]
        send_sems, recv_sems = refs[2 * n:]
        x, y, c = _place()
        chips = [(1 - x, y), (x, 1 - y), (1 - x, 1 - y)]
        cps = []
        for t in range(n):
            for k, (qx, qy) in enumerate(chips):
                cp = pltpu.make_async_remote_copy(src_ref=s_refs[t].at[2 * qx + qy], dst_ref=o_refs[t].at[k],
                                                  send_sem=send_sems.at[3 * t + k], recv_sem=recv_sems.at[3 * t + k],
                                                  device_id=(qx, qy, c), device_id_type=MESH)
                cp.start()
                cps.append(cp)
        for cp in cps:
            cp.wait()

    return pl.pallas_call(
        body, name="scatter_to_chips",
        in_specs=[ANY] * n, out_specs=[ANY] * n,
        out_shape=[jax.ShapeDtypeStruct((3,) + s.shape[1:], s.dtype) for s in partials],
        scratch_shapes=[pltpu.SemaphoreType.DMA((3 * n,)), pltpu.SemaphoreType.DMA((3 * n,))],
    )(*partials)


def scatter_start(partials):
    n = len(partials)

    def body(*refs):
        s_refs, l_refs = refs[:n], refs[n:2 * n]
        send_sems, recv_sems = refs[2 * n], refs[2 * n + 1]
        token = refs[-1]
        x, y, c = _place()
        for t in range(n):
            for k, (qx, qy) in enumerate(_other_chips(x, y)):
                pltpu.make_async_remote_copy(src_ref=s_refs[t].at[2 * qx + qy], dst_ref=l_refs[t].at[k],
                                             send_sem=send_sems.at[3 * t + k], recv_sem=recv_sems.at[3 * t + k],
                                             device_id=(qx, qy, c), device_id_type=MESH).start()
        token[...] = jnp.zeros_like(token)

    lands = [lax.empty((3,) + s.shape[1:], s.dtype) for s in partials]
    res = pl.pallas_call(
        body, name="scatter_start",
        in_specs=[HBM] * (2 * n),
        out_specs=[SEM, SEM] + [HBM] * (2 * n) + [pl.BlockSpec(memory_space=pltpu.VMEM)],
        out_shape=[pltpu.SemaphoreType.DMA((3 * n,)), pltpu.SemaphoreType.DMA((3 * n,))]
        + [pltpu.HBM(a.shape, a.dtype) for a in partials + lands] + [jax.ShapeDtypeStruct((8, LANES), F32)],
        input_output_aliases={t: 2 + t for t in range(2 * n)},
        compiler_params=pltpu.CompilerParams(has_side_effects=EFFECT),
    )(*[_hbm(a) for a in partials + lands])
    return res[0], res[1], list(res[2:2 + n]), list(res[2 + n:2 + 2 * n]), res[-1]


def scatter_wait(partials, lands, send_sems, recv_sems, after):
    n = len(partials)

    def body(*refs):
        s_refs, l_refs = refs[:n], refs[n:2 * n]
        send_sems, recv_sems = refs[2 * n], refs[2 * n + 1]
        x, y, c = _place()
        for t in range(n):
            for k, (qx, qy) in enumerate(_other_chips(x, y)):
                cp = pltpu.make_async_remote_copy(src_ref=s_refs[t].at[2 * qx + qy], dst_ref=l_refs[t].at[k],
                                                  send_sem=send_sems.at[3 * t + k], recv_sem=recv_sems.at[3 * t + k],
                                                  device_id=(qx, qy, c), device_id_type=MESH)
                cp.wait_send()
                cp.wait_recv()

    res = pl.pallas_call(
        body, name="scatter_wait",
        in_specs=[HBM] * (2 * n) + [SEM, SEM, ANY],
        out_specs=[HBM] * (2 * n),
        out_shape=[pltpu.HBM(a.shape, a.dtype) for a in partials + lands],
        input_output_aliases={t: t for t in range(2 * n)},
        compiler_params=pltpu.CompilerParams(has_side_effects=EFFECT),
    )(*partials, *lands, send_sems, recv_sems, after)
    return list(res[:n]), list(res[n:])


def sum_chips(own, parts, where):
    _, r, cols = own.shape
    tr = _row_tile(r)

    def body(w_ref, a_ref, p_ref, o_ref):
        acc = a_ref[...].astype(F32)
        for k in range(3):
            acc = acc + p_ref[k].astype(F32)
        o_ref[...] = acc

    return pl.pallas_call(
        body, name="sum_chips",
        grid_spec=pltpu.PrefetchScalarGridSpec(
            num_scalar_prefetch=1, grid=(r // tr,),
            in_specs=[pl.BlockSpec((None, tr, cols), lambda i, w_ref: (w_ref[0], i, 0)),
                      pl.BlockSpec((3, tr, cols), lambda i, w_ref: (0, i, 0))],
            out_specs=pl.BlockSpec((None, tr, cols), lambda i, w_ref: (w_ref[1], i, 0))),
        out_shape=jax.ShapeDtypeStruct((DEPTH, r, cols), F32),
        compiler_params=_cparams("parallel"),
    )(where, own, parts)


def sibling_share_layer(bufs):
    n = len(bufs)

    def body(*refs):
        o_refs = refs[n:2 * n]
        send_sems, recv_sems = refs[2 * n:]
        x, y, c = _place()
        cps = []
        for t in range(n):
            cp = pltpu.make_async_remote_copy(src_ref=o_refs[t].at[c], dst_ref=o_refs[t].at[c], send_sem=send_sems.at[t],
                                              recv_sem=recv_sems.at[t], device_id=(x, y, 1 - c), device_id_type=MESH)
            cp.start()
            cps.append(cp)
        for t in range(n):
            slot = o_refs[t].at[1 - c]
            pltpu.make_async_remote_copy(src_ref=slot, dst_ref=slot, send_sem=send_sems.at[t], recv_sem=recv_sems.at[t],
                                         device_id=(x, y, 1 - c), device_id_type=MESH).wait_recv()
        for cp in cps:
            cp.wait_send()

    return pl.pallas_call(
        body, name="sibling_share_layer",
        in_specs=[ANY] * n, out_specs=[ANY] * n,
        out_shape=[jax.ShapeDtypeStruct(b.shape, b.dtype) for b in bufs],
        input_output_aliases={t: t for t in range(n)},
        scratch_shapes=[pltpu.SemaphoreType.DMA((n,)), pltpu.SemaphoreType.DMA((n,))],
    )(*bufs)


SP_META = 2 * (N_META * D_MODEL // LANES)
SP_NORM = DEPTH * D_MODEL // LANES
SP_RB = DEPTH * N_BUCKETS
SP_SINK = DEPTH * ATT_HEADS
SP_CONV = DEPTH * 3 * BRANCH_WIDTH // LANES
SP_LOSS = 8
SP_ROWS = SP_META + 2 * SP_NORM + SP_RB + SP_SINK + SP_CONV + SP_LOSS


def sum_small(slots):
    half = SP_META // 2
    rb0 = SP_META + 2 * SP_NORM
    rest_rows = SP_ROWS - SP_META

    def body(s_ref, meta_ref, rest_ref):
        acc = s_ref[0]
        for d in range(1, 8):
            acc = acc + s_ref[d]
        meta_ref[...] = acc[0:half] + acc[half:SP_META]
        rest_ref[...] = acc[SP_META:]
        rest_ref[rb0 - SP_META:rb0 - SP_META + N_BUCKETS, :] = (
            acc[rb0:rb0 + N_BUCKETS] + acc[rb0 + N_BUCKETS:rb0 + 2 * N_BUCKETS])

    vm = pl.BlockSpec(memory_space=pltpu.VMEM)
    return pl.pallas_call(
        body, name="sum_small",
        in_specs=[vm], out_specs=[vm, vm],
        out_shape=[jax.ShapeDtypeStruct((half, LANES), F32), jax.ShapeDtypeStruct((rest_rows, LANES), F32)],
    )(slots)


def local_step(x, loss_target, meta_full, rel_bias, norm_pre, conv_w_full, attn_sinks, norm_post, weights_of, grads_done):
    nb, seq, _ = x.shape
    nc = seq // BLOCK + 1
    lp = nc * BLOCK
    rows = nb * lp
    pad = jnp.zeros((nb, PAD_FRONT, D_MODEL), F32)
    meta = jnp.broadcast_to(meta_full[None], (nb, N_META, D_MODEL))
    h0 = jnp.concatenate([pad, meta, x], axis=1).reshape(rows, D_MODEL)
    cosf, sinf = _rot_tables(lp)
    bkt = jnp.asarray(_bucket_table())

    acts = []
    h = h0
    for l in range(DEPTH):
        (w_in, w_br, w_out), zero = weights_of(l, h)
        hb, p_abc = norm_matmul(h, norm_pre[l][None] + zero, w_in, 0, N_ABC_TILES)
        p_m = matmul_cols(hb, w_in, N_ABC_TILES, N_M_TILES)
        br, states = mixers_fwd(p_abc, cosf, sinf, bkt, rel_bias, attn_sinks[l][None], conv_w_full[l], nb, nc)
        h_next = merge_fwd(h, br, p_m, w_br, w_out, norm_post[l][None])
        acts.append((h, hb, p_abc, p_m, br, states, w_in, w_br, w_out))
        h = h_next

    loss_part, d_h = loss_head(h, loss_target.reshape(nb * seq, D_MODEL), nb, nc)

    small = [None] * DEPTH
    zero = jnp.zeros((1, 1), F32)
    for l in reversed(range(DEPTH)):
        h_in, hb, p_abc, p_m, br, states, w_in, w_br, w_out = acts[l]
        d_br, d_m, d_gpost, g_wbr, g_wout = merge_bwd(d_h, br, p_m, w_br, w_out, norm_post[l][None] + zero)
        d_abc, d_rb, d_sk, d_cw = mixers_bwd(p_abc, d_br, states, cosf, sinf, bkt, rel_bias,
                                             attn_sinks[l][None], conv_w_full[l], nb, nc)
        d_h, d_gpre = proj_dgrad(d_abc, d_m, w_in, h_in, norm_pre[l][None], d_h)
        g_win = proj_wgrad(hb, d_abc, d_m)
        zero = grads_done(l, [g_win, g_wbr, g_wout])
        small[l] = (d_gpre[0], d_gpost[0], d_rb, d_sk, d_cw[0:3])

    d_h3 = d_h.reshape(nb, lp, D_MODEL)
    d_x = d_h3[:, BLOCK:]
    d_meta = d_h3[:, PAD_FRONT:BLOCK]
    sp = jnp.concatenate([
        d_meta.reshape(-1, LANES),
        jnp.stack([small[l][0] for l in range(DEPTH)]).reshape(-1, LANES),
        jnp.stack([small[l][1] for l in range(DEPTH)]).reshape(-1, LANES),
        jnp.concatenate([small[l][2] for l in range(DEPTH)], axis=0),
        jnp.concatenate([small[l][3] for l in range(DEPTH)], axis=0),
        jnp.stack([small[l][4] for l in range(DEPTH)]).reshape(-1, LANES),
        loss_part], axis=0)
    return d_x, sp


def kernel(x, meta_tokens, rel_bias, norm_pre, w_in, conv_w, attn_sinks, w_branch, w_out, norm_post, loss_target, m_meta_tokens, m_rel_bias, m_norm_pre, m_w_in, m_conv_w, m_attn_sinks, m_w_branch, m_w_out, m_norm_post, v_meta_tokens, v_rel_bias, v_norm_pre, v_w_in, v_conv_w, v_attn_sinks, v_w_branch, v_w_out, v_norm_post):
    assert x.shape[0] == 2 and SP_META == 2 * N_META * D_MODEL // LANES
    px, py, pc = _place()
    chip = 2 * px + py

    c_arr = jnp.reshape(pc, (1,)).astype(jnp.int32)
    where = jnp.stack([chip, pc]).astype(jnp.int32)
    tr_ = lambda a: jnp.swapaxes(a, 1, 2)
    w3 = [tr_(w_in), w_branch.reshape(DEPTH, N_BRANCH * BRANCH_WIDTH, SHARD_D), w_out]
    halves = lambda a: a.reshape(2, a.shape[0] // 2, a.shape[1])

    def as_weights(bufs):
        a_in, a_br, a_out = bufs
        return (a_in.reshape(PROJ_WIDTH, D_MODEL), a_br.reshape(N_CHIPS, N_BRANCH, BRANCH_WIDTH, SHARD_D),
                a_out.reshape(D_MODEL, D_MODEL))

    slots = [[_own_slot(halves(w[l].astype(BF16)), chip) for w in w3] for l in range(DEPTH)]
    gathered0 = gather_weight_shards(slots[0])
    send1, recv1, flying1, started1 = gather_start(slots[1], gathered0[0])

    def weights_of(l, h):
        if l == 0:
            return as_weights(gathered0), started1[0:1, 0:1]
        landed = gather_forward(gather_wait(flying1, send1, recv1, h))
        return as_weights(landed), jnp.zeros((1, 1), F32)

    reduced = [None] * DEPTH
    flying = {}

    def finish(partials, parts):
        return sibling_share_layer([sum_chips(a, p, where) for a, p in zip(partials, parts)])

    def grads_done(l, grads):
        full = [g.reshape(N_CHIPS, 2, g.size // (2 * N_CHIPS * g.shape[-1]), g.shape[-1]) for g in grads]
        others = sibling_swap_halves(full)
        partials = [add_own_half(g, o, c_arr) for g, o in zip(full, others)]
        if l == 1:
            send, recv, thru, lands, started = scatter_start(partials)
            flying.update(send=send, recv=recv, thru=thru, lands=lands)
            return started[0:1, 0:1]
        thru, lands = scatter_wait(flying["thru"], flying["lands"], flying["send"], flying["recv"], grads[0])
        reduced[1] = finish(thru, lands)
        reduced[0] = finish(partials, scatter_to_chips(partials))
        return None

    side = jnp.concatenate([meta_tokens.reshape(-1), conv_w.reshape(-1)]).reshape(-1, LANES)
    side = jnp.concatenate([side, jnp.zeros((40 - side.shape[0], LANES), F32)], axis=0)
    side_all = exchange_small(side)
    side_chips = side_all[0::2]
    n_meta_rows = N_META * SHARD_D // LANES
    meta_full = jnp.moveaxis(side_chips[:, :n_meta_rows].reshape(N_CHIPS, N_META, SHARD_D), 0, 1).reshape(N_META, D_MODEL)
    conv_full = jnp.moveaxis(side_chips[:, n_meta_rows:n_meta_rows + 6].reshape(N_CHIPS, DEPTH, 3, LANES), 0, 2).reshape(DEPTH, 3, BRANCH_WIDTH)

    d_x, sp = local_step(x, loss_target, meta_full, rel_bias, norm_pre, conv_full, attn_sinks, norm_post,
                         weights_of, grads_done)

    meta_rows, rest = sum_small(exchange_small(sp))
    o = 0
    g_meta_full = meta_rows.reshape(N_META, D_MODEL)
    g_norm_pre = rest[o:o + SP_NORM].reshape(DEPTH, D_MODEL); o += SP_NORM
    g_norm_post = rest[o:o + SP_NORM].reshape(DEPTH, D_MODEL); o += SP_NORM
    g_rel_bias = rest[o:o + N_BUCKETS, :ATT_HEADS]; o += SP_RB
    g_sinks = rest[o:o + SP_SINK, 0].reshape(DEPTH, ATT_HEADS); o += SP_SINK
    g_conv_full = rest[o:o + SP_CONV].reshape(DEPTH, 3, BRANCH_WIDTH); o += SP_CONV
    loss = rest[o, 0]
    g_meta = lax.dynamic_slice_in_dim(g_meta_full, chip * SHARD_D, SHARD_D, axis=1)
    g_conv = lax.dynamic_slice_in_dim(g_conv_full, chip * LANES, LANES, axis=2)

    m3 = [tr_(m_w_in), m_w_branch.reshape(w3[1].shape), m_w_out]
    v3 = [tr_(v_w_in), v_w_branch.reshape(w3[1].shape), v_w_out]
    big = []
    for t in range(3):
        acc = None
        for l in reversed(range(DEPTH)):
            acc = adamw_layer(w3[t], reduced[l][t].reshape(w3[t].shape[1:]), m3[t], v3[t], l, acc)
        big.append(acc)
    g_in, *u_in = [tr_(a) for a in big[0]]
    g_br, *u_br = [a.reshape(w_branch.shape) for a in big[1]]
    g_out, *u_out = big[2]
    to2 = lambda a: a.reshape(-1, a.shape[-1])
    smalls = [(meta_tokens, g_meta, m_meta_tokens, v_meta_tokens),
              (rel_bias, g_rel_bias, m_rel_bias, v_rel_bias),
              (norm_pre, g_norm_pre, m_norm_pre, v_norm_pre),
              (to2(conv_w), to2(g_conv), to2(m_conv_w), to2(v_conv_w)),
              (attn_sinks, g_sinks, m_attn_sinks, v_attn_sinks),
              (norm_post, g_norm_post, m_norm_post, v_norm_post)]
    u_meta, u_rb, u_npre, u_conv, u_sink, u_npost = adamw_small(smalls)
    u_conv = tuple(a.reshape(conv_w.shape) for a in u_conv)

    grads = [g_meta, g_rel_bias, g_norm_pre, g_in, g_conv, g_sinks, g_br, g_out, g_norm_post]
    upd = [u_meta, u_rb, u_npre, u_in, u_conv, u_sink, u_br, u_out, u_npost]
    return (loss, d_x, *grads, *[u[0] for u in upd], *[u[1] for u in upd], *[u[2] for u in upd])
```

```python
import functools
import math

import numpy as np
import jax
import jax.numpy as jnp
from jax import lax
from jax.experimental import pallas as pl
from jax.experimental.pallas import tpu as pltpu

F32 = jnp.float32
BF16 = jnp.bfloat16
MESH = pl.DeviceIdType.MESH

D_MODEL = 1024
DEPTH = 2
N_META = 16
BLOCK = 128
PAD_FRONT = BLOCK - N_META
ATT_HEADS = 8
ATT_HEAD_DIM = 64
N_BUCKETS = 32
MAX_EXACT = 16
MAX_DISTANCE = 128
RET_HEADS = 4
ROT_BASE = 10000.0
N_BRANCH = 3
BRANCH_WIDTH = 512
PROJ_WIDTH = 8448
ABC_WIDTH = 5376
MERGE_WIDTH = N_BRANCH * D_MODEL
RMS_EPS = 1e-6
GN_EPS = 1e-6
NEG_INF = -1e30
ATT_SCALE = ATT_HEAD_DIM ** -0.5
RET_SCALE = BLOCK ** -0.5
LOG_GAMMA = tuple(math.log1p(-(2.0 ** (-5.0 - h))) for h in range(RET_HEADS))

C_AQ, C_AK, C_AV, C_AG = 0, 512, 640, 768
C_RQ, C_RK, C_RV, C_RG = 1280, 1792, 2304, 2816
C_CB, C_CC, C_CX, C_CG = 3328, 3840, 4352, 4864

ADAM_LR = 0.001
ADAM_B1 = 0.9
ADAM_B2 = 0.999
ADAM_EPS = 1e-08
ADAM_WD = 0.01
ADAM_STEP = 10

N_CHIPS = 4
SHARD_IN = PROJ_WIDTH // N_CHIPS
SHARD_D = D_MODEL // N_CHIPS
LANES = 128
PACK_IN = D_MODEL * SHARD_IN
PACK_BR = N_BRANCH * BRANCH_WIDTH * SHARD_D
PACK_OUT = SHARD_D * D_MODEL
PACK_ROWS = (PACK_IN + PACK_BR + PACK_OUT) // LANES

VMEM_LIMIT = 56 * 1024 * 1024
COL_TILE = 768
ROW_TILE = 1088


def _cparams(*sem):
    return pltpu.CompilerParams(dimension_semantics=sem, vmem_limit_bytes=VMEM_LIMIT)


def _nt(a, b):
    return lax.dot_general(a, b, (((1,), (1,)), ((), ())), preferred_element_type=F32)


def _tn(a, b):
    return lax.dot_general(a, b, (((0,), (0,)), ((), ())), preferred_element_type=F32)


def _nn(a, b):
    return jnp.dot(a, b, preferred_element_type=F32)


def _silu(x):
    return x * jax.nn.sigmoid(x)


def _dsilu(x):
    s = jax.nn.sigmoid(x)
    return s * (1.0 + x * (1.0 - s))


def _bucket_table():
    r = np.arange(BLOCK)[:, None]
    c = np.arange(2 * BLOCK)[None, :]
    n = np.maximum(BLOCK + r - c, 0)
    nf = np.maximum(n, 1).astype(np.float32)
    large = MAX_EXACT + (np.log(nf / MAX_EXACT) / math.log(MAX_DISTANCE / MAX_EXACT)
                         * (N_BUCKETS - MAX_EXACT)).astype(np.int32)
    large = np.minimum(large, N_BUCKETS - 1)
    return np.where(n < MAX_EXACT, n, large).astype(np.int32)


def _rot_tables(lp):
    half = BLOCK // 2
    pos = (jnp.arange(lp) - PAD_FRONT).astype(F32)
    theta = 1.0 / (ROT_BASE ** jnp.linspace(0.0, 1.0, half, dtype=F32))
    ang = pos[:, None] * theta[None, :]
    cos, sin = jnp.cos(ang), jnp.sin(ang)
    return jnp.concatenate([cos, cos], axis=1), jnp.concatenate([-sin, sin], axis=1)


def norm_matmul(x2d, g, w, col0_blocks, n_col_blocks):
    t = x2d.shape[0]
    tm = ROW_TILE if t % ROW_TILE == 0 else BLOCK

    def body(x_ref, g_ref, w_ref, hb_ref, o_ref):
        @pl.when(pl.program_id(1) == 0)
        def _():
            x = x_ref[...]
            r = lax.rsqrt(jnp.mean(x * x, axis=-1, keepdims=True) + RMS_EPS)
            hb_ref[...] = (x * r * g_ref[...]).astype(BF16)

        o_ref[...] = _nt(hb_ref[...], w_ref[...])

    return pl.pallas_call(
        body, name="norm_matmul",
        grid=(t // tm, n_col_blocks),
        in_specs=[pl.BlockSpec((tm, D_MODEL), lambda i, j: (i, 0)),
                  pl.BlockSpec((1, D_MODEL), lambda i, j: (0, 0)),
                  pl.BlockSpec((COL_TILE, D_MODEL), lambda i, j: (j + col0_blocks, 0))],
        out_specs=[pl.BlockSpec((tm, D_MODEL), lambda i, j: (i, 0)),
                   pl.BlockSpec((tm, COL_TILE), lambda i, j: (i, j))],
        out_shape=[jax.ShapeDtypeStruct((t, D_MODEL), BF16),
                   jax.ShapeDtypeStruct((t, n_col_blocks * COL_TILE), F32)],
        compiler_params=_cparams("parallel", "arbitrary"),
    )(x2d, g, w)


def matmul_cols(a, w, col0_blocks, n_col_blocks):
    t, k = a.shape
    tm = ROW_TILE if t % ROW_TILE == 0 else BLOCK

    def body(a_ref, w_ref, o_ref):
        o_ref[...] = _nt(a_ref[...], w_ref[...])

    return pl.pallas_call(
        body, name="matmul_cols",
        grid=(t // tm, n_col_blocks),
        in_specs=[pl.BlockSpec((tm, k), lambda i, j: (i, 0)),
                  pl.BlockSpec((COL_TILE, k), lambda i, j: (j + col0_blocks, 0))],
        out_specs=pl.BlockSpec((tm, COL_TILE), lambda i, j: (i, j)),
        out_shape=jax.ShapeDtypeStruct((t, n_col_blocks * COL_TILE), F32),
        compiler_params=_cparams("parallel", "arbitrary"),
    )(a, w)


def _build_bias(bkt_ref, rb_ref, bias_s):
    bkt = bkt_ref[...]
    for h in range(ATT_HEADS):
        acc = jnp.zeros((BLOCK, 2 * BLOCK), F32)
        for b in range(N_BUCKETS):
            acc = jnp.where(bkt == b, rb_ref[b, h], acc)
        bias_s[h] = acc


def _band_mask(n):
    r = lax.broadcasted_iota(jnp.int32, (BLOCK, 2 * BLOCK), 0)
    c = lax.broadcasted_iota(jnp.int32, (BLOCK, 2 * BLOCK), 1)
    key_pos = (n - 1) * BLOCK + c
    return (c > r) & (c <= r + BLOCK) & (key_pos >= PAD_FRONT)


def _split_heads(kv, kh):
    lane = lax.broadcasted_iota(jnp.int32, kv.shape, 1)
    if kh == 0:
        lo = jnp.where(lane < ATT_HEAD_DIM, kv, 0.0)
        hi = pltpu.roll(lo, ATT_HEAD_DIM, 1)
    else:
        hi = jnp.where(lane >= ATT_HEAD_DIM, kv, 0.0)
        lo = pltpu.roll(hi, ATT_HEAD_DIM, 1)
    return lo, hi


def _merge_heads(acc_lo, acc_hi, kh):
    lane = lax.broadcasted_iota(jnp.int32, acc_lo.shape, 1)
    if kh == 0:
        return jnp.where(lane < ATT_HEAD_DIM, acc_lo + pltpu.roll(acc_hi, ATT_HEAD_DIM, 1), 0.0)
    return jnp.where(lane >= ATT_HEAD_DIM, acc_hi + pltpu.roll(acc_lo, ATT_HEAD_DIM, 1), 0.0)


def _softmax_sink(q2b, kxb, bias_h, mask, sink_h):
    s = _nt(q2b, kxb) * ATT_SCALE + bias_h
    s = jnp.where(mask, s, NEG_INF)
    m = jnp.maximum(jnp.max(s, axis=-1, keepdims=True), sink_h)
    p = jnp.exp(s - m)
    es = jnp.exp(sink_h - m)
    inv = 1.0 / (jnp.sum(p, axis=-1, keepdims=True) + es)
    return p * inv, es * inv


def _rot(t, cosf, sinf):
    return t * cosf + pltpu.roll(t, BLOCK // 2, 1) * sinf


def _rot_t(d, cosf, sinf):
    return d * cosf + pltpu.roll(d * sinf, BLOCK // 2, 1)


def _decay_tables(h):
    lg = LOG_GAMMA[h]
    i = lax.broadcasted_iota(jnp.int32, (BLOCK, BLOCK), 0)
    j = lax.broadcasted_iota(jnp.int32, (BLOCK, BLOCK), 1)
    diff = (i - j).astype(F32)
    dm = jnp.where(diff >= 0, jnp.exp(diff * lg), 0.0)
    row = lax.broadcasted_iota(jnp.int32, (BLOCK, 1), 0).astype(F32)
    zeta = jnp.exp((BLOCK - 1 - row) * lg)
    xi = jnp.exp((row + 1.0) * lg)
    return dm, zeta, xi, math.exp(BLOCK * lg)


def _valid_col(n):
    row = lax.broadcasted_iota(jnp.int32, (BLOCK, 1), 0)
    return ((n * BLOCK + row) >= PAD_FRONT).astype(F32)


def _shift_down(cur, prev, k):
    row = lax.broadcasted_iota(jnp.int32, cur.shape, 0)
    return jnp.where(row >= k, pltpu.roll(cur, k, 0), pltpu.roll(prev, k, 0))


def _shift_up(cur, nxt, k):
    row = lax.broadcasted_iota(jnp.int32, cur.shape, 0)
    return jnp.where(row < BLOCK - k, pltpu.roll(cur, BLOCK - k, 0), pltpu.roll(nxt, BLOCK - k, 0))


def mixers_fwd(proj, cosf, sinf, bkt, rel_bias, sinks, conv_w, nb, nc):
    def body(p_ref, cos_ref, sin_ref, bkt_ref, rb_ref, sk_ref, cw_ref, br_ref, st_ref,
             bias_s, kv_s, state_s, u_s):
        b = pl.program_id(0)
        n = pl.program_id(1)

        @pl.when((b == 0) & (n == 0))
        def _():
            _build_bias(bkt_ref, rb_ref, bias_s)

        @pl.when(n == 0)
        def _():
            kv_s[0:BLOCK, :] = jnp.zeros((BLOCK, 2 * BLOCK), F32)
            state_s[...] = jnp.zeros_like(state_s)
            u_s[...] = jnp.zeros_like(u_s)

        valid = _valid_col(n)
        mask = _band_mask(n)

        kv_s[BLOCK:2 * BLOCK, :] = p_ref[:, C_AK:C_AK + 2 * BLOCK]
        k_all = kv_s[:, 0:BLOCK]
        v_all = kv_s[:, BLOCK:2 * BLOCK]
        for kh in range(2):
            k_lo, k_hi = [t.astype(BF16) for t in _split_heads(k_all, kh)]
            v_lo, v_hi = [t.astype(BF16) for t in _split_heads(v_all, kh)]
            for jj in range(2):
                j = 2 * kh + jj
                q2b = p_ref[:, C_AQ + BLOCK * j:C_AQ + BLOCK * (j + 1)].astype(BF16)
                p_lo, _ = _softmax_sink(q2b, k_lo, bias_s[2 * j], mask, sk_ref[0, 2 * j])
                p_hi, _ = _softmax_sink(q2b, k_hi, bias_s[2 * j + 1], mask, sk_ref[0, 2 * j + 1])
                o2 = _nn(p_lo.astype(BF16), v_lo) + _nn(p_hi.astype(BF16), v_hi)
                gate = p_ref[:, C_AG + BLOCK * j:C_AG + BLOCK * (j + 1)]
                br_ref[:, BLOCK * j:BLOCK * (j + 1)] = (o2 * _silu(gate)).astype(BF16)
        kv_s[0:BLOCK, :] = kv_s[BLOCK:2 * BLOCK, :]

        cosv = cos_ref[...]
        sinv = sin_ref[...]
        for h in range(RET_HEADS):
            dm, zeta, xi, gamma_chunk = _decay_tables(h)
            q = _rot(p_ref[:, C_RQ + BLOCK * h:C_RQ + BLOCK * (h + 1)], cosv, sinv).astype(BF16)
            k = (_rot(p_ref[:, C_RK + BLOCK * h:C_RK + BLOCK * (h + 1)], cosv, sinv)
                 * RET_SCALE * valid).astype(BF16)
            v = p_ref[:, C_RV + BLOCK * h:C_RV + BLOCK * (h + 1)]
            s_prev = state_s[h]
            st_ref[0, 0, h] = s_prev
            a = (_nt(q, k) * dm).astype(BF16)
            o = _nn(a, v.astype(BF16)) + xi * _nn(q, s_prev.astype(BF16))
            mu = jnp.mean(o, axis=-1, keepdims=True)
            var = jnp.mean(jnp.square(o - mu), axis=-1, keepdims=True)
            oh = (o - mu) * lax.rsqrt(var + GN_EPS)
            gate = p_ref[:, C_RG + BLOCK * h:C_RG + BLOCK * (h + 1)]
            br_ref[:, BRANCH_WIDTH + BLOCK * h:BRANCH_WIDTH + BLOCK * (h + 1)] = (oh * _silu(gate)).astype(BF16)
            state_s[h] = gamma_chunk * s_prev + _tn(k, (v * zeta).astype(BF16))

        u = p_ref[:, C_CC:C_CC + BRANCH_WIDTH] * p_ref[:, C_CX:C_CX + BRANCH_WIDTH] * valid
        u_prev = u_s[...]
        y = (cw_ref[0:1, :] * _shift_down(u, u_prev, 2) + cw_ref[1:2, :] * _shift_down(u, u_prev, 1)
             + cw_ref[2:3, :] * u)
        yc = p_ref[:, C_CB:C_CB + BRANCH_WIDTH] * y * _silu(p_ref[:, C_CG:C_CG + BRANCH_WIDTH])
        br_ref[:, 2 * BRANCH_WIDTH:3 * BRANCH_WIDTH] = yc.astype(BF16)
        u_s[...] = u

    rows = nb * nc * BLOCK
    smem = pl.BlockSpec(memory_space=pltpu.SMEM)
    return pl.pallas_call(
        body, name="mixers_fwd",
        grid=(nb, nc),
        in_specs=[pl.BlockSpec((BLOCK, ABC_WIDTH), lambda b, n: (b * nc + n, 0)),
                  pl.BlockSpec((BLOCK, BLOCK), lambda b, n: (n, 0)),
                  pl.BlockSpec((BLOCK, BLOCK), lambda b, n: (n, 0)),
                  pl.BlockSpec((BLOCK, 2 * BLOCK), lambda b, n: (0, 0)),
                  smem, smem,
                  pl.BlockSpec((3, BRANCH_WIDTH), lambda b, n: (0, 0))],
        out_specs=[pl.BlockSpec((BLOCK, N_BRANCH * BRANCH_WIDTH), lambda b, n: (b * nc + n, 0)),
                   pl.BlockSpec((1, 1, RET_HEADS, BLOCK, BLOCK), lambda b, n: (b, n, 0, 0, 0))],
        out_shape=[jax.ShapeDtypeStruct((rows, N_BRANCH * BRANCH_WIDTH), BF16),
                   jax.ShapeDtypeStruct((nb, nc, RET_HEADS, BLOCK, BLOCK), F32)],
        scratch_shapes=[pltpu.VMEM((ATT_HEADS, BLOCK, 2 * BLOCK), F32),
                        pltpu.VMEM((2 * BLOCK, 2 * BLOCK), F32),
                        pltpu.VMEM((RET_HEADS, BLOCK, BLOCK), F32),
                        pltpu.VMEM((BLOCK, BRANCH_WIDTH), F32)],
        compiler_params=_cparams("arbitrary", "arbitrary"),
    )(proj, cosf, sinf, bkt, rel_bias, sinks, conv_w)


def mixers_bwd(proj, d_br, states, cosf, sinf, bkt, rel_bias, sinks, conv_w, nb, nc):
    def body(p_ref, kvp_ref, cp_ref, dbr_ref, st_ref, cos_ref, sin_ref, bkt_ref, rb_ref, sk_ref, cw_ref,
             dp_ref, drb_ref, dsk_ref, dcw_ref,
             bias_s, dbias_s, dkv_s, g_s, dy_s):
        b = pl.program_id(0)
        step = pl.program_id(1)
        n = nc - 1 - step
        first = (b == 0) & (step == 0)
        last = (b == nb - 1) & (step == nc - 1)

        @pl.when(first)
        def _():
            _build_bias(bkt_ref, rb_ref, bias_s)
            dbias_s[...] = jnp.zeros_like(dbias_s)
            dsk_ref[...] = jnp.zeros_like(dsk_ref)
            dcw_ref[...] = jnp.zeros_like(dcw_ref)
            drb_ref[...] = jnp.zeros_like(drb_ref)

        @pl.when(step == 0)
        def _():
            dkv_s[...] = jnp.zeros_like(dkv_s)
            g_s[...] = jnp.zeros_like(g_s)
            dy_s[...] = jnp.zeros_like(dy_s)

        valid = _valid_col(n)
        mask = _band_mask(n)
        has_prev = (n > 0).astype(F32)

        kv_prev = kvp_ref[...] * has_prev
        kv_cur = p_ref[:, C_AK:C_AK + 2 * BLOCK]
        k_all = jnp.concatenate([kv_prev[:, 0:BLOCK], kv_cur[:, 0:BLOCK]], axis=0)
        v_all = jnp.concatenate([kv_prev[:, BLOCK:], kv_cur[:, BLOCK:]], axis=0)
        dk_tot = jnp.zeros((2 * BLOCK, BLOCK), F32)
        dv_tot = jnp.zeros((2 * BLOCK, BLOCK), F32)
        for kh in range(2):
            ks = [t.astype(BF16) for t in _split_heads(k_all, kh)]
            vs = [t.astype(BF16) for t in _split_heads(v_all, kh)]
            dk_acc = [jnp.zeros((2 * BLOCK, BLOCK), F32), jnp.zeros((2 * BLOCK, BLOCK), F32)]
            dv_acc = [jnp.zeros((2 * BLOCK, BLOCK), F32), jnp.zeros((2 * BLOCK, BLOCK), F32)]
            for jj in range(2):
                j = 2 * kh + jj
                q2b = p_ref[:, C_AQ + BLOCK * j:C_AQ + BLOCK * (j + 1)].astype(BF16)
                gate = p_ref[:, C_AG + BLOCK * j:C_AG + BLOCK * (j + 1)]
                d_ya = dbr_ref[:, BLOCK * j:BLOCK * (j + 1)]
                do2 = (d_ya * _silu(gate)).astype(BF16)
                o2 = jnp.zeros((BLOCK, BLOCK), F32)
                dq2 = jnp.zeros((BLOCK, BLOCK), F32)
                for x in range(2):
                    h = 2 * j + x
                    p, p_sink = _softmax_sink(q2b, ks[x], bias_s[h], mask, sk_ref[0, h])
                    pb = p.astype(BF16)
                    o2 = o2 + _nn(pb, vs[x])
                    dp = _nt(do2, vs[x])
                    delta = jnp.sum(p * dp, axis=-1, keepdims=True)
                    ds = p * (dp - delta)
                    dbias_s[h] += ds
                    dsk_ref[h:h + 1, :] += jnp.broadcast_to(
                        jnp.sum(-p_sink * delta, axis=0, keepdims=True), (1, BLOCK))
                    dsb = ds.astype(BF16)
                    dq2 = dq2 + _nn(dsb, ks[x]) * ATT_SCALE
                    dk_acc[x] = dk_acc[x] + _tn(dsb, q2b) * ATT_SCALE
                    dv_acc[x] = dv_acc[x] + _tn(pb, do2)
                dp_ref[:, C_AQ + BLOCK * j:C_AQ + BLOCK * (j + 1)] = dq2.astype(BF16)
                dp_ref[:, C_AG + BLOCK * j:C_AG + BLOCK * (j + 1)] = (d_ya * o2 * _dsilu(gate)).astype(BF16)
            dk_tot = dk_tot + _merge_heads(dk_acc[0], dk_acc[1], kh)
            dv_tot = dv_tot + _merge_heads(dv_acc[0], dv_acc[1], kh)
        dp_ref[:, C_AK:C_AK + BLOCK] = (dk_tot[BLOCK:, :] + dkv_s[:, 0:BLOCK]).astype(BF16)
        dp_ref[:, C_AV:C_AV + BLOCK] = (dv_tot[BLOCK:, :] + dkv_s[:, BLOCK:]).astype(BF16)
        dkv_s[:, 0:BLOCK] = dk_tot[0:BLOCK, :]
        dkv_s[:, BLOCK:] = dv_tot[0:BLOCK, :]

        cosv = cos_ref[...]
        sinv = sin_ref[...]
        for h in range(RET_HEADS):
            dm, zeta, xi, gamma_chunk = _decay_tables(h)
            sl = lambda c0: slice(c0 + BLOCK * h, c0 + BLOCK * (h + 1))
            q = _rot(p_ref[:, sl(C_RQ)], cosv, sinv).astype(BF16)
            k = (_rot(p_ref[:, sl(C_RK)], cosv, sinv) * RET_SCALE * valid).astype(BF16)
            v = p_ref[:, sl(C_RV)]
            vb = v.astype(BF16)
            gate = p_ref[:, sl(C_RG)]
            s_prev = st_ref[0, 0, h].astype(BF16)
            g_next = g_s[h]
            a = (_nt(q, k) * dm).astype(BF16)
            o = _nn(a, vb) + xi * _nn(q, s_prev)
            mu = jnp.mean(o, axis=-1, keepdims=True)
            var = jnp.mean(jnp.square(o - mu), axis=-1, keepdims=True)
            rstd = lax.rsqrt(var + GN_EPS)
            oh = (o - mu) * rstd
            d_yr = dbr_ref[:, BRANCH_WIDTH + BLOCK * h:BRANCH_WIDTH + BLOCK * (h + 1)]
            dp_ref[:, sl(C_RG)] = (d_yr * oh * _dsilu(gate)).astype(BF16)
            doh = d_yr * _silu(gate)
            do = rstd * (doh - jnp.mean(doh, axis=-1, keepdims=True)
                         - oh * jnp.mean(doh * oh, axis=-1, keepdims=True))
            dob = do.astype(BF16)
            dxo = (do * xi).astype(BF16)
            da = (_nt(dob, vb) * dm).astype(BF16)
            gb = g_next.astype(BF16)
            zv = (v * zeta).astype(BF16)
            dq = _nn(da, k) + _nt(dxo, s_prev)
            dk = _tn(da, q) + _nt(zv, gb)
            dv = _tn(a, dob) + zeta * _nn(k, gb)
            g_s[h] = gamma_chunk * g_next + _tn(q, dxo)
            dp_ref[:, sl(C_RQ)] = _rot_t(dq, cosv, sinv).astype(BF16)
            dp_ref[:, sl(C_RK)] = _rot_t(dk * (RET_SCALE * valid), cosv, sinv).astype(BF16)
            dp_ref[:, sl(C_RV)] = dv.astype(BF16)

        w0, w1, w2 = cw_ref[0:1, :], cw_ref[1:2, :], cw_ref[2:3, :]
        cb = p_ref[:, C_CB:C_CB + BRANCH_WIDTH]
        cc = p_ref[:, C_CC:C_CC + BRANCH_WIDTH]
        cx = p_ref[:, C_CX:C_CX + BRANCH_WIDTH]
        cg = p_ref[:, C_CG:C_CG + BRANCH_WIDTH]
        u = cc * cx * valid
        u_prev = cp_ref[:, 0:BRANCH_WIDTH] * cp_ref[:, BRANCH_WIDTH:2 * BRANCH_WIDTH] * (_valid_col(n - 1) * has_prev)
        u1 = _shift_down(u, u_prev, 1)
        u2 = _shift_down(u, u_prev, 2)
        y = w0 * u2 + w1 * u1 + w2 * u
        d_yc = dbr_ref[:, 2 * BRANCH_WIDTH:3 * BRANCH_WIDTH]
        sg = _silu(cg)
        dp_ref[:, C_CB:C_CB + BRANCH_WIDTH] = (d_yc * y * sg).astype(BF16)
        dp_ref[:, C_CG:C_CG + BRANCH_WIDTH] = (d_yc * cb * y * _dsilu(cg)).astype(BF16)
        dy = d_yc * cb * sg
        dy_next = dy_s[...]
        du = (w2 * dy + w1 * _shift_up(dy, dy_next, 1) + w0 * _shift_up(dy, dy_next, 2)) * valid
        dp_ref[:, C_CC:C_CC + BRANCH_WIDTH] = (du * cx).astype(BF16)
        dp_ref[:, C_CX:C_CX + BRANCH_WIDTH] = (du * cc).astype(BF16)
        dcw_ref[0:1, :] += jnp.sum(dy * u2, axis=0, keepdims=True)
        dcw_ref[1:2, :] += jnp.sum(dy * u1, axis=0, keepdims=True)
        dcw_ref[2:3, :] += jnp.sum(dy * u, axis=0, keepdims=True)
        dy_s[...] = dy

        @pl.when(last)
        def _():
            bkt = bkt_ref[...]
            row = lax.broadcasted_iota(jnp.int32, (N_BUCKETS, BLOCK), 0)
            lane = lax.broadcasted_iota(jnp.int32, (N_BUCKETS, BLOCK), 1)

            def one_bucket(bk, acc):
                sel = bkt == bk
                for h in range(ATT_HEADS):
                    t = jnp.where(sel, dbias_s[h], 0.0)
                    s = jnp.sum(jnp.sum(t, axis=1, keepdims=True), axis=0, keepdims=True)
                    acc = acc + jnp.where((row == bk) & (lane == h), jnp.broadcast_to(s, acc.shape), 0.0)
                return acc

            drb_ref[...] = lax.fori_loop(0, N_BUCKETS, one_bucket, jnp.zeros((N_BUCKETS, BLOCK), F32))

    rows = nb * nc * BLOCK
    smem = pl.BlockSpec(memory_space=pltpu.SMEM)
    blk = lambda b, s: b * nc + (nc - 1 - s)
    prev = lambda b, s: b * nc + jnp.maximum(nc - 2 - s, 0)
    return pl.pallas_call(
        body, name="mixers_bwd",
        grid=(nb, nc),
        in_specs=[pl.BlockSpec((BLOCK, ABC_WIDTH), lambda b, s: (blk(b, s), 0)),
                  pl.BlockSpec((BLOCK, 2 * BLOCK), lambda b, s: (prev(b, s), C_AK // (2 * BLOCK))),
                  pl.BlockSpec((BLOCK, 1280), lambda b, s: (prev(b, s), C_CC // 1280)),
                  pl.BlockSpec((BLOCK, N_BRANCH * BRANCH_WIDTH), lambda b, s: (blk(b, s), 0)),
                  pl.BlockSpec((1, 1, RET_HEADS, BLOCK, BLOCK), lambda b, s: (b, nc - 1 - s, 0, 0, 0)),
                  pl.BlockSpec((BLOCK, BLOCK), lambda b, s: (nc - 1 - s, 0)),
                  pl.BlockSpec((BLOCK, BLOCK), lambda b, s: (nc - 1 - s, 0)),
                  pl.BlockSpec((BLOCK, 2 * BLOCK), lambda b, s: (0, 0)),
                  smem, smem,
                  pl.BlockSpec((3, BRANCH_WIDTH), lambda b, s: (0, 0))],
        out_specs=[pl.BlockSpec((BLOCK, ABC_WIDTH), lambda b, s: (blk(b, s), 0)),
                   pl.BlockSpec((N_BUCKETS, BLOCK), lambda b, s: (0, 0)),
                   pl.BlockSpec((ATT_HEADS, BLOCK), lambda b, s: (0, 0)),
                   pl.BlockSpec((8, BRANCH_WIDTH), lambda b, s: (0, 0))],
        out_shape=[jax.ShapeDtypeStruct((rows, ABC_WIDTH), BF16),
                   jax.ShapeDtypeStruct((N_BUCKETS, BLOCK), F32),
                   jax.ShapeDtypeStruct((ATT_HEADS, BLOCK), F32),
                   jax.ShapeDtypeStruct((8, BRANCH_WIDTH), F32)],
        scratch_shapes=[pltpu.VMEM((ATT_HEADS, BLOCK, 2 * BLOCK), F32),
                        pltpu.VMEM((ATT_HEADS, BLOCK, 2 * BLOCK), F32),
                        pltpu.VMEM((BLOCK, 2 * BLOCK), F32),
                        pltpu.VMEM((RET_HEADS, BLOCK, BLOCK), F32),
                        pltpu.VMEM((BLOCK, BRANCH_WIDTH), F32)],
        compiler_params=_cparams("arbitrary", "arbitrary"),
    )(proj, proj, proj, d_br, states, cosf, sinf, bkt, rel_bias, sinks, conv_w)


MERGE_TILE = 256


def _merge_forward(br_ref, m_ref, wb_ref, wo_ref):
    bo, gates = [], []
    mixed_pre = None
    for g in range(N_BRANCH):
        br_g = br_ref[:, BRANCH_WIDTH * g:BRANCH_WIDTH * (g + 1)]
        bo_g = jnp.concatenate([_nn(br_g, wb_ref[p, g]) for p in range(N_CHIPS)], axis=1)
        gate_g = jax.nn.sigmoid(m_ref[:, D_MODEL * g:D_MODEL * (g + 1)])
        bo.append(bo_g)
        gates.append(gate_g)
        mixed_pre = gate_g * bo_g if mixed_pre is None else mixed_pre + gate_g * bo_g
    mixed = _nn(mixed_pre.astype(BF16), wo_ref[...])
    r = lax.rsqrt(jnp.mean(mixed * mixed, axis=-1, keepdims=True) + RMS_EPS)
    return bo, gates, mixed_pre, mixed, r


def merge_fwd(x2d, br, pm, wb, wo, g_post):
    t = x2d.shape[0]
    tm = MERGE_TILE if t % MERGE_TILE == 0 else BLOCK

    def body(x_ref, br_ref, m_ref, wb_ref, wo_ref, g_ref, o_ref):
        _, _, _, mixed, r = _merge_forward(br_ref, m_ref, wb_ref, wo_ref)
        o_ref[...] = x_ref[...] + mixed * r * g_ref[...]

    return pl.pallas_call(
        body, name="merge_fwd",
        grid=(t // tm,),
        in_specs=[pl.BlockSpec((tm, D_MODEL), lambda i: (i, 0)),
                  pl.BlockSpec((tm, N_BRANCH * BRANCH_WIDTH), lambda i: (i, 0)),
                  pl.BlockSpec((tm, MERGE_WIDTH), lambda i: (i, 0)),
                  pl.BlockSpec((N_CHIPS, N_BRANCH, BRANCH_WIDTH, SHARD_D), lambda i: (0, 0, 0, 0)),
                  pl.BlockSpec((D_MODEL, D_MODEL), lambda i: (0, 0)),
                  pl.BlockSpec((1, D_MODEL), lambda i: (0, 0))],
        out_specs=pl.BlockSpec((tm, D_MODEL), lambda i: (i, 0)),
        out_shape=jax.ShapeDtypeStruct((t, D_MODEL), F32),
        compiler_params=_cparams("parallel"),
    )(x2d, br, pm, wb, wo, g_post)


def merge_bwd(d_out, br, pm, wb, wo, g_post):
    t = d_out.shape[0]
    tm = MERGE_TILE if t % MERGE_TILE == 0 else BLOCK

    def body(do_ref, br_ref, m_ref, wb_ref, wo_ref, g_ref, dbr_ref, dm_ref, dg_ref, dwb_ref, dwo_ref):

        @pl.when(pl.program_id(0) == 0)
        def _():
            dwb_ref[...] = jnp.zeros_like(dwb_ref)
            dwo_ref[...] = jnp.zeros_like(dwo_ref)
            dg_ref[...] = jnp.zeros_like(dg_ref)

        bo, gates, mixed_pre, mixed, r = _merge_forward(br_ref, m_ref, wb_ref, wo_ref)
        d_o = do_ref[...]
        nh = mixed * r
        dg_ref[0:1, :] += jnp.sum(d_o * nh, axis=0, keepdims=True)
        dn = d_o * g_ref[...]
        d_mixed = (r * (dn - nh * jnp.mean(dn * nh, axis=-1, keepdims=True))).astype(BF16)
        dwo_ref[...] += _tn(mixed_pre.astype(BF16), d_mixed)
        d_pre = _nt(d_mixed, wo_ref[...])
        for g in range(N_BRANCH):
            br_g = br_ref[:, BRANCH_WIDTH * g:BRANCH_WIDTH * (g + 1)]
            d_bo = (d_pre * gates[g]).astype(BF16)
            dm_ref[:, D_MODEL * g:D_MODEL * (g + 1)] = (
                d_pre * bo[g] * gates[g] * (1.0 - gates[g])).astype(BF16)
            d_br_g = None
            for p in range(N_CHIPS):
                d_bo_p = d_bo[:, SHARD_D * p:SHARD_D * (p + 1)]
                part = _nt(d_bo_p, wb_ref[p, g])
                d_br_g = part if d_br_g is None else d_br_g + part
                dwb_ref[p, g] += _tn(br_g, d_bo_p)
            dbr_ref[:, BRANCH_WIDTH * g:BRANCH_WIDTH * (g + 1)] = d_br_g

    return pl.pallas_call(
        body, name="merge_bwd",
        grid=(t // tm,),
        in_specs=[pl.BlockSpec((tm, D_MODEL), lambda i: (i, 0)),
                  pl.BlockSpec((tm, N_BRANCH * BRANCH_WIDTH), lambda i: (i, 0)),
                  pl.BlockSpec((tm, MERGE_WIDTH), lambda i: (i, 0)),
                  pl.BlockSpec((N_CHIPS, N_BRANCH, BRANCH_WIDTH, SHARD_D), lambda i: (0, 0, 0, 0)),
                  pl.BlockSpec((D_MODEL, D_MODEL), lambda i: (0, 0)),
                  pl.BlockSpec((1, D_MODEL), lambda i: (0, 0))],
        out_specs=[pl.BlockSpec((tm, N_BRANCH * BRANCH_WIDTH), lambda i: (i, 0)),
                   pl.BlockSpec((tm, MERGE_WIDTH), lambda i: (i, 0)),
                   pl.BlockSpec((8, D_MODEL), lambda i: (0, 0)),
                   pl.BlockSpec((N_CHIPS, N_BRANCH, BRANCH_WIDTH, SHARD_D), lambda i: (0, 0, 0, 0)),
                   pl.BlockSpec((D_MODEL, D_MODEL), lambda i: (0, 0))],
        out_shape=[jax.ShapeDtypeStruct((t, N_BRANCH * BRANCH_WIDTH), F32),
                   jax.ShapeDtypeStruct((t, MERGE_WIDTH), BF16),
                   jax.ShapeDtypeStruct((8, D_MODEL), F32),
                   jax.ShapeDtypeStruct((N_CHIPS, N_BRANCH, BRANCH_WIDTH, SHARD_D), F32),
                   jax.ShapeDtypeStruct((D_MODEL, D_MODEL), F32)],
        compiler_params=_cparams("arbitrary"),
    )(d_out, br, pm, wb, wo, g_post)


def loss_head(xf, target2d, nb, nc):
    def body(x_ref, t_ref, l_ref, dx_ref):
        b = pl.program_id(0)
        n = pl.program_id(1)

        @pl.when((b == 0) & (n == 0))
        def _():
            l_ref[...] = jnp.zeros_like(l_ref)

        @pl.when(n == 0)
        def _():
            dx_ref[...] = jnp.zeros_like(dx_ref)

        @pl.when(n > 0)
        def _():
            e = x_ref[...] - t_ref[...]
            dx_ref[...] = e * (1.0 / D_MODEL)
            s = jnp.sum(jnp.sum(e * e, axis=1, keepdims=True), axis=0, keepdims=True)
            l_ref[...] += jnp.broadcast_to(s * (0.5 / D_MODEL), l_ref.shape)

    return pl.pallas_call(
        body, name="loss_head",
        grid=(nb, nc),
        in_specs=[pl.BlockSpec((BLOCK, D_MODEL), lambda b, n: (b * nc + n, 0)),
                  pl.BlockSpec((BLOCK, D_MODEL), lambda b, n: (b * (nc - 1) + jnp.maximum(n - 1, 0), 0))],
        out_specs=[pl.BlockSpec((8, BLOCK), lambda b, n: (0, 0)),
                   pl.BlockSpec((BLOCK, D_MODEL), lambda b, n: (b * nc + n, 0))],
        out_shape=[jax.ShapeDtypeStruct((8, BLOCK), F32),
                   jax.ShapeDtypeStruct(xf.shape, F32)],
        compiler_params=_cparams("arbitrary", "arbitrary"),
    )(xf, target2d)


N_ABC_TILES = ABC_WIDTH // COL_TILE
N_M_TILES = MERGE_WIDTH // COL_TILE


def proj_dgrad(d_abc, d_m, w, x2d, g, d_out):
    t = x2d.shape[0]
    tm = ROW_TILE if t % ROW_TILE == 0 else BLOCK
    nk = N_ABC_TILES + N_M_TILES

    def body(da_ref, dm_ref, w_ref, x_ref, g_ref, do_ref, dx_ref, dg_ref, acc):
        i = pl.program_id(0)
        k = pl.program_id(1)

        @pl.when((i == 0) & (k == 0))
        def _():
            dg_ref[...] = jnp.zeros_like(dg_ref)

        @pl.when(k == 0)
        def _():
            acc[...] = jnp.zeros_like(acc)

        @pl.when(k < N_ABC_TILES)
        def _():
            acc[...] += _nn(da_ref[...], w_ref[...])

        @pl.when(k >= N_ABC_TILES)
        def _():
            acc[...] += _nn(dm_ref[...], w_ref[...])

        @pl.when(k == nk - 1)
        def _():
            x = x_ref[...]
            r = lax.rsqrt(jnp.mean(x * x, axis=-1, keepdims=True) + RMS_EPS)
            nh = x * r
            dh = acc[...]
            dg_ref[0:1, :] += jnp.sum(dh * nh, axis=0, keepdims=True)
            dn = dh * g_ref[...]
            dx_ref[...] = do_ref[...] + r * (dn - nh * jnp.mean(dn * nh, axis=-1, keepdims=True))

    return pl.pallas_call(
        body, name="proj_dgrad",
        grid=(t // tm, nk),
        in_specs=[pl.BlockSpec((tm, COL_TILE), lambda i, k: (i, jnp.minimum(k, N_ABC_TILES - 1))),
                  pl.BlockSpec((tm, COL_TILE), lambda i, k: (i, jnp.maximum(k - N_ABC_TILES, 0))),
                  pl.BlockSpec((COL_TILE, D_MODEL), lambda i, k: (k, 0)),
                  pl.BlockSpec((tm, D_MODEL), lambda i, k: (i, 0)),
                  pl.BlockSpec((1, D_MODEL), lambda i, k: (0, 0)),
                  pl.BlockSpec((tm, D_MODEL), lambda i, k: (i, 0))],
        out_specs=[pl.BlockSpec((tm, D_MODEL), lambda i, k: (i, 0)),
                   pl.BlockSpec((8, D_MODEL), lambda i, k: (0, 0))],
        out_shape=[jax.ShapeDtypeStruct((t, D_MODEL), F32),
                   jax.ShapeDtypeStruct((8, D_MODEL), F32)],
        scratch_shapes=[pltpu.VMEM((tm, D_MODEL), F32)],
        compiler_params=_cparams("arbitrary", "arbitrary"),
    )(d_abc, d_m, w, x2d, g, d_out)


def proj_wgrad(hb, d_abc, d_m):
    t = hb.shape[0]
    nj = N_ABC_TILES + N_M_TILES

    def body(h_ref, da_ref, dm_ref, o_ref):
        j = pl.program_id(0)

        @pl.when(j < N_ABC_TILES)
        def _():
            o_ref[...] = _tn(da_ref[...], h_ref[...])

        @pl.when(j >= N_ABC_TILES)
        def _():
            o_ref[...] = _tn(dm_ref[...], h_ref[...])

    return pl.pallas_call(
        body, name="proj_wgrad",
        grid=(nj,),
        in_specs=[pl.BlockSpec((t, D_MODEL), lambda j: (0, 0)),
                  pl.BlockSpec((t, COL_TILE), lambda j: (0, jnp.minimum(j, N_ABC_TILES - 1))),
                  pl.BlockSpec((t, COL_TILE), lambda j: (0, jnp.maximum(j - N_ABC_TILES, 0)))],
        out_specs=pl.BlockSpec((COL_TILE, D_MODEL), lambda j: (j, 0)),
        out_shape=jax.ShapeDtypeStruct((PROJ_WIDTH, D_MODEL), F32),
        compiler_params=_cparams("arbitrary"),
    )(hb, d_abc, d_m)


def _adamw_math(w, g, m, v):
    m = ADAM_B1 * m + (1.0 - ADAM_B1) * g
    v = ADAM_B2 * v + (1.0 - ADAM_B2) * jnp.square(g)
    m_hat = m / (1.0 - ADAM_B1 ** ADAM_STEP)
    v_hat = v / (1.0 - ADAM_B2 ** ADAM_STEP)
    delta = -ADAM_LR * (m_hat / (jnp.sqrt(v_hat) + ADAM_EPS) + ADAM_WD * w)
    return delta, m, v


def adamw_layer(w, g, m, v, layer, acc):
    _, r, c = w.shape
    tr = _row_tile(r)

    def body(*refs):
        w_ref, g_ref, m_ref, v_ref = refs[:4]
        go_ref, d_ref, mo_ref, vo_ref = refs[-4:]
        g_val = g_ref[...]
        d, m_new, v_new = _adamw_math(w_ref[...], g_val, m_ref[...], v_ref[...])
        go_ref[...] = g_val
        d_ref[...] = d
        mo_ref[...] = m_new
        vo_ref[...] = v_new

    slab = pl.BlockSpec((None, tr, c), lambda i: (layer, i, 0))
    ins = [w, g, m, v]
    in_specs = [slab, pl.BlockSpec((tr, c), lambda i: (i, 0)), slab, slab]
    aliases = {}
    if acc is not None:
        ins += list(acc)
        in_specs += [ANY] * 4
        aliases = {4 + i: i for i in range(4)}
    return pl.pallas_call(
        body, name="adamw_layer",
        grid=(r // tr,),
        in_specs=in_specs, out_specs=[slab] * 4,
        out_shape=[jax.ShapeDtypeStruct(w.shape, F32)] * 4,
        input_output_aliases=aliases,
        compiler_params=_cparams("parallel"),
    )(*ins)


def adamw_small(params):
    k = len(params)

    def body(*refs):
        ins, outs = refs[:4 * k], refs[4 * k:]
        for i in range(k):
            d, m_new, v_new = _adamw_math(*[r[...] for r in ins[4 * i:4 * i + 4]])
            outs[3 * i][...] = d
            outs[3 * i + 1][...] = m_new
            outs[3 * i + 2][...] = v_new

    flat = [a for p in params for a in p]
    vm = pl.BlockSpec(memory_space=pltpu.VMEM)
    out_shape = [jax.ShapeDtypeStruct(p[0].shape, F32) for p in params for _ in range(3)]
    res = pl.pallas_call(
        body, name="adamw_small",
        in_specs=[vm] * len(flat), out_specs=[vm] * len(out_shape), out_shape=out_shape,
    )(*flat)
    return [tuple(res[3 * i:3 * i + 3]) for i in range(k)]


ANY = pl.BlockSpec(memory_space=pl.ANY)


def _place():
    return lax.axis_index("x"), lax.axis_index("y"), lax.axis_index("c")


HBM = pl.BlockSpec(memory_space=pltpu.HBM)
SEM = pl.BlockSpec(memory_space=pltpu.SEMAPHORE)
EFFECT = pltpu.SideEffectType.DATAFLOW_SIDE_EFFECTING


def _other_chips(x, y):
    return [(1 - x, y), (x, 1 - y), (1 - x, 1 - y)]


def _own_slot(shard, chip):
    buf = lax.empty((N_CHIPS,) + shard.shape, shard.dtype)
    return lax.dynamic_update_slice(buf, shard[None], (chip, 0, 0, 0))


def gather_weight_shards(bufs, after):
    n = len(bufs)

    def body(*refs):
        g_refs = refs[n + 1:2 * n + 1]
        send_sems, recv_sems = refs[2 * n + 1:]
        x, y, c = _place()
        me_p = 2 * x + y
        sibling = (x, y, 1 - c)
        chips = _other_chips(x, y)

        def copy(k, slab, to):
            return pltpu.make_async_remote_copy(src_ref=slab, dst_ref=slab, send_sem=send_sems.at[k],
                                                recv_sem=recv_sems.at[k], device_id=to, device_id_type=MESH)

        first, passed = [], []
        for t in range(n):
            for k, (qx, qy) in enumerate(chips):
                cp = copy(6 * t + k, g_refs[t].at[me_p, c], (qx, qy, c))
                cp.start()
                first.append(cp)
        for t in range(n):
            for k, (qx, qy) in enumerate(chips):
                slab = g_refs[t].at[2 * qx + qy, c]
                copy(6 * t + k, slab, (qx, qy, c)).wait_recv()
                fwd = copy(6 * t + 3 + k, slab, sibling)
                fwd.start()
                passed.append(fwd)
        for t in range(n):
            for k, (qx, qy) in enumerate(chips):
                copy(6 * t + 3 + k, g_refs[t].at[2 * qx + qy, 1 - c], sibling).wait_recv()
        for cp in first + passed:
            cp.wait_send()

    return pl.pallas_call(
        body, name="gather_weight_shards",
        in_specs=[ANY] * (n + 1), out_specs=[ANY] * n,
        out_shape=[jax.ShapeDtypeStruct(b.shape, b.dtype) for b in bufs],
        input_output_aliases={t: t for t in range(n)},
        scratch_shapes=[pltpu.SemaphoreType.DMA((6 * n,)), pltpu.SemaphoreType.DMA((6 * n,))],
    )(*bufs, after)


def _hbm(a):
    return pltpu.with_memory_space_constraint(a, pltpu.HBM)


def gather_start(bufs, after):
    n = len(bufs)

    def body(*refs):
        g_refs = refs[:n]
        send_sems, recv_sems = refs[n + 1], refs[n + 2]
        token = refs[-1]
        x, y, c = _place()
        me_p = 2 * x + y
        for t in range(n):
            for k, (qx, qy) in enumerate(_other_chips(x, y)):
                slab = g_refs[t].at[me_p, c]
                pltpu.make_async_remote_copy(src_ref=slab, dst_ref=slab, send_sem=send_sems.at[3 * t + k],
                                             recv_sem=recv_sems.at[3 * t + k], device_id=(qx, qy, c),
                                             device_id_type=MESH).start()
        token[...] = jnp.zeros_like(token)

    res = pl.pallas_call(
        body, name="gather_start",
        in_specs=[HBM] * n + [ANY],
        out_specs=[SEM, SEM] + [HBM] * n + [pl.BlockSpec(memory_space=pltpu.VMEM)],
        out_shape=[pltpu.SemaphoreType.DMA((3 * n,)), pltpu.SemaphoreType.DMA((3 * n,))]
        + [pltpu.HBM(b.shape, b.dtype) for b in bufs] + [jax.ShapeDtypeStruct((8, LANES), F32)],
        input_output_aliases={t: 2 + t for t in range(n)},
        compiler_params=pltpu.CompilerParams(has_side_effects=EFFECT),
    )(*[_hbm(b) for b in bufs], after)
    return res[0], res[1], list(res[2:2 + n]), res[-1]


def gather_wait(bufs, send_sems, recv_sems, after):
    n = len(bufs)

    def body(*refs):
        g_refs = refs[:n]
        send_sems, recv_sems = refs[n], refs[n + 1]
        x, y, c = _place()
        me_p = 2 * x + y
        for t in range(n):
            for k, (qx, qy) in enumerate(_other_chips(x, y)):
                cp = pltpu.make_async_remote_copy(src_ref=g_refs[t].at[me_p, c], dst_ref=g_refs[t].at[2 * qx + qy, c],
                                                  send_sem=send_sems.at[3 * t + k], recv_sem=recv_sems.at[3 * t + k],
                                                  device_id=(qx, qy, c), device_id_type=MESH)
                cp.wait_send()
                cp.wait_recv()

    return pl.pallas_call(
        body, name="gather_wait",
        in_specs=[HBM] * n + [SEM, SEM, ANY],
        out_specs=[HBM] * n,
        out_shape=[pltpu.HBM(b.shape, b.dtype) for b in bufs],
        input_output_aliases={t: t for t in range(n)},
        compiler_params=pltpu.CompilerParams(has_side_effects=EFFECT),
    )(*bufs, send_sems, recv_sems, after)


def gather_forward(bufs):
    n = len(bufs)

    def body(*refs):
        g_refs = refs[n:2 * n]
        send_sems, recv_sems = refs[2 * n:]
        x, y, c = _place()
        sibling = (x, y, 1 - c)
        chips = _other_chips(x, y)
        passed = []
        for t in range(n):
            for k, (qx, qy) in enumerate(chips):
                slab = g_refs[t].at[2 * qx + qy, c]
                fwd = pltpu.make_async_remote_copy(src_ref=slab, dst_ref=slab, send_sem=send_sems.at[3 * t + k],
                                                   recv_sem=recv_sems.at[3 * t + k], device_id=sibling,
                                                   device_id_type=MESH)
                fwd.start()
                passed.append(fwd)
        for t in range(n):
            for k, (qx, qy) in enumerate(chips):
                slab = g_refs[t].at[2 * qx + qy, 1 - c]
                pltpu.make_async_remote_copy(src_ref=slab, dst_ref=slab, send_sem=send_sems.at[3 * t + k],
                                             recv_sem=recv_sems.at[3 * t + k], device_id=sibling,
                                             device_id_type=MESH).wait_recv()
        for cp in passed:
            cp.wait_send()

    return pl.pallas_call(
        body, name="gather_forward",
        in_specs=[ANY] * n, out_specs=[ANY] * n,
        out_shape=[jax.ShapeDtypeStruct(b.shape, b.dtype) for b in bufs],
        input_output_aliases={t: t for t in range(n)},
        scratch_shapes=[pltpu.SemaphoreType.DMA((3 * n,)), pltpu.SemaphoreType.DMA((3 * n,))],
    )(*bufs)


def exchange_small(pack, after):
    def body(p_ref, after_ref, o_ref, send_sems, recv_sems, local_sem):
        x, y, c = _place()
        me = 4 * x + 2 * y + c
        mine = pltpu.make_async_copy(p_ref, o_ref.at[me], local_sem)
        mine.start()
        sends = []
        for k in range(1, 8):
            fx, fy, fc = (k >> 2) & 1, (k >> 1) & 1, k & 1
            peer = (x ^ fx, y ^ fy, c ^ fc)
            cp = pltpu.make_async_remote_copy(src_ref=p_ref, dst_ref=o_ref.at[me], send_sem=send_sems.at[k - 1],
                                              recv_sem=recv_sems.at[k - 1], device_id=peer, device_id_type=MESH)
            cp.start()
            sends.append(cp)
        for k in range(1, 8):
            fx, fy, fc = (k >> 2) & 1, (k >> 1) & 1, k & 1
            peer = (x ^ fx, y ^ fy, c ^ fc)
            slot = o_ref.at[4 * peer[0] + 2 * peer[1] + peer[2]]
            pltpu.make_async_remote_copy(src_ref=slot, dst_ref=slot, send_sem=send_sems.at[k - 1],
                                         recv_sem=recv_sems.at[k - 1], device_id=peer, device_id_type=MESH).wait_recv()
        for cp in sends:
            cp.wait_send()
        mine.wait()

    return pl.pallas_call(
        body, name="exchange_small",
        in_specs=[ANY, ANY], out_specs=ANY,
        out_shape=jax.ShapeDtypeStruct((8,) + pack.shape, pack.dtype),
        scratch_shapes=[pltpu.SemaphoreType.DMA((7,)), pltpu.SemaphoreType.DMA((7,)), pltpu.SemaphoreType.DMA],
    )(pack, after)


def sibling_swap_halves(grads):
    n = len(grads)

    def body(*refs):
        g_refs, o_refs = refs[:n], refs[n:2 * n]
        send_sems, recv_sems = refs[2 * n:]
        x, y, c = _place()
        cps = []
        for t in range(n):
            for p in range(N_CHIPS):
                cp = pltpu.make_async_remote_copy(src_ref=g_refs[t].at[p, 1 - c], dst_ref=o_refs[t].at[p],
                                                  send_sem=send_sems.at[N_CHIPS * t + p],
                                                  recv_sem=recv_sems.at[N_CHIPS * t + p],
                                                  device_id=(x, y, 1 - c), device_id_type=MESH)
                cp.start()
                cps.append(cp)
        for cp in cps:
            cp.wait()

    return pl.pallas_call(
        body, name="sibling_swap_halves",
        in_specs=[ANY] * n, out_specs=[ANY] * n,
        out_shape=[jax.ShapeDtypeStruct((N_CHIPS,) + g.shape[2:], g.dtype) for g in grads],
        scratch_shapes=[pltpu.SemaphoreType.DMA((N_CHIPS * n,)), pltpu.SemaphoreType.DMA((N_CHIPS * n,))],
    )(*grads)


def _row_tile(r):
    return max(t for t in range(16, 513, 16) if r % t == 0)


def add_own_half(g, other, c_arr):
    _, _, r, cols = g.shape
    tr = _row_tile(r)

    def body(c_ref, a_ref, b_ref, o_ref):
        o_ref[...] = (a_ref[...] + b_ref[...]).astype(BF16)

    return pl.pallas_call(
        body, name="add_own_half",
        grid_spec=pltpu.PrefetchScalarGridSpec(
            num_scalar_prefetch=1, grid=(N_CHIPS, r // tr),
            in_specs=[pl.BlockSpec((None, None, tr, cols), lambda p, i, c_ref: (p, c_ref[0], i, 0)),
                      pl.BlockSpec((None, tr, cols), lambda p, i, c_ref: (p, i, 0))],
            out_specs=pl.BlockSpec((None, tr, cols), lambda p, i, c_ref: (p, i, 0))),
        out_shape=jax.ShapeDtypeStruct((N_CHIPS, r, cols), BF16),
        compiler_params=_cparams("parallel", "parallel"),
    )(c_arr, g, other)


def scatter_start(partials):
    n = len(partials)

    def body(*refs):
        s_refs, l_refs = refs[:n], refs[n:2 * n]
        send_sems, recv_sems = refs[2 * n], refs[2 * n + 1]
        token = refs[-1]
        x, y, c = _place()
        for t in range(n):
            for k, (qx, qy) in enumerate(_other_chips(x, y)):
                pltpu.make_async_remote_copy(src_ref=s_refs[t].at[2 * qx + qy], dst_ref=l_refs[t].at[k],
                                             send_sem=send_sems.at[3 * t + k], recv_sem=recv_sems.at[3 * t + k],
                                             device_id=(qx, qy, c), device_id_type=MESH).start()
        token[...] = jnp.zeros_like(token)

    lands = [lax.empty((3,) + s.shape[1:], s.dtype) for s in partials]
    res = pl.pallas_call(
        body, name="scatter_start",
        in_specs=[HBM] * (2 * n),
        out_specs=[SEM, SEM] + [HBM] * (2 * n) + [pl.BlockSpec(memory_space=pltpu.VMEM)],
        out_shape=[pltpu.SemaphoreType.DMA((3 * n,)), pltpu.SemaphoreType.DMA((3 * n,))]
        + [pltpu.HBM(a.shape, a.dtype) for a in partials + lands] + [jax.ShapeDtypeStruct((8, LANES), F32)],
        input_output_aliases={t: 2 + t for t in range(2 * n)},
        compiler_params=pltpu.CompilerParams(has_side_effects=EFFECT),
    )(*[_hbm(a) for a in partials + lands])
    return res[0], res[1], list(res[2:2 + n]), list(res[2 + n:2 + 2 * n]), res[-1]


def scatter_wait(partials, lands, send_sems, recv_sems, after):
    n = len(partials)

    def body(*refs):
        s_refs, l_refs = refs[:n], refs[n:2 * n]
        send_sems, recv_sems = refs[2 * n], refs[2 * n + 1]
        x, y, c = _place()
        for t in range(n):
            for k, (qx, qy) in enumerate(_other_chips(x, y)):
                cp = pltpu.make_async_remote_copy(src_ref=s_refs[t].at[2 * qx + qy], dst_ref=l_refs[t].at[k],
                                                  send_sem=send_sems.at[3 * t + k], recv_sem=recv_sems.at[3 * t + k],
                                                  device_id=(qx, qy, c), device_id_type=MESH)
                cp.wait_send()
                cp.wait_recv()

    res = pl.pallas_call(
        body, name="scatter_wait",
        in_specs=[HBM] * (2 * n) + [SEM, SEM, ANY],
        out_specs=[HBM] * (2 * n),
        out_shape=[pltpu.HBM(a.shape, a.dtype) for a in partials + lands],
        input_output_aliases={t: t for t in range(2 * n)},
        compiler_params=pltpu.CompilerParams(has_side_effects=EFFECT),
    )(*partials, *lands, send_sems, recv_sems, after)
    return list(res[:n]), list(res[n:])


def sum_chips(own, parts, where):
    _, r, cols = own.shape
    tr = _row_tile(r)

    def body(w_ref, a_ref, p_ref, o_ref):
        acc = a_ref[...].astype(F32)
        for k in range(3):
            acc = acc + p_ref[k].astype(F32)
        o_ref[...] = acc

    return pl.pallas_call(
        body, name="sum_chips",
        grid_spec=pltpu.PrefetchScalarGridSpec(
            num_scalar_prefetch=1, grid=(r // tr,),
            in_specs=[pl.BlockSpec((None, tr, cols), lambda i, w_ref: (w_ref[0], i, 0)),
                      pl.BlockSpec((3, tr, cols), lambda i, w_ref: (0, i, 0))],
            out_specs=pl.BlockSpec((None, tr, cols), lambda i, w_ref: (w_ref[1], i, 0))),
        out_shape=jax.ShapeDtypeStruct((DEPTH, r, cols), F32),
        compiler_params=_cparams("parallel"),
    )(where, own, parts)


def sibling_share_layer(bufs):
    n = len(bufs)

    def body(*refs):
        o_refs = refs[n:2 * n]
        send_sems, recv_sems = refs[2 * n:]
        x, y, c = _place()
        cps = []
        for t in range(n):
            cp = pltpu.make_async_remote_copy(src_ref=o_refs[t].at[c], dst_ref=o_refs[t].at[c], send_sem=send_sems.at[t],
                                              recv_sem=recv_sems.at[t], device_id=(x, y, 1 - c), device_id_type=MESH)
            cp.start()
            cps.append(cp)
        for t in range(n):
            slot = o_refs[t].at[1 - c]
            pltpu.make_async_remote_copy(src_ref=slot, dst_ref=slot, send_sem=send_sems.at[t], recv_sem=recv_sems.at[t],
                                         device_id=(x, y, 1 - c), device_id_type=MESH).wait_recv()
        for cp in cps:
            cp.wait_send()

    return pl.pallas_call(
        body, name="sibling_share_layer",
        in_specs=[ANY] * n, out_specs=[ANY] * n,
        out_shape=[jax.ShapeDtypeStruct(b.shape, b.dtype) for b in bufs],
        input_output_aliases={t: t for t in range(n)},
        scratch_shapes=[pltpu.SemaphoreType.DMA((n,)), pltpu.SemaphoreType.DMA((n,))],
    )(*bufs)


SP_META = 2 * (N_META * D_MODEL // LANES)
SP_NORM = DEPTH * D_MODEL // LANES
SP_RB = DEPTH * N_BUCKETS
SP_SINK = DEPTH * ATT_HEADS
SP_CONV = DEPTH * 3 * BRANCH_WIDTH // LANES
SP_LOSS = 8
SP_ROWS = SP_META + 2 * SP_NORM + SP_RB + SP_SINK + SP_CONV + SP_LOSS


def sum_small(slots):
    half = SP_META // 2
    rb0 = SP_META + 2 * SP_NORM
    rest_rows = SP_ROWS - SP_META

    def body(s_ref, meta_ref, rest_ref):
        acc = s_ref[0]
        for d in range(1, 8):
            acc = acc + s_ref[d]
        meta_ref[...] = acc[0:half] + acc[half:SP_META]
        rest_ref[...] = acc[SP_META:]
        rest_ref[rb0 - SP_META:rb0 - SP_META + N_BUCKETS, :] = (
            acc[rb0:rb0 + N_BUCKETS] + acc[rb0 + N_BUCKETS:rb0 + 2 * N_BUCKETS])

    vm = pl.BlockSpec(memory_space=pltpu.VMEM)
    return pl.pallas_call(
        body, name="sum_small",
        in_specs=[vm], out_specs=[vm, vm],
        out_shape=[jax.ShapeDtypeStruct((half, LANES), F32), jax.ShapeDtypeStruct((rest_rows, LANES), F32)],
    )(slots)


def local_step(x, loss_target, meta_full, rel_bias, norm_pre, conv_w_full, attn_sinks, norm_post, weights_of, grads_done):
    nb, seq, _ = x.shape
    nc = seq // BLOCK + 1
    lp = nc * BLOCK
    rows = nb * lp
    pad = jnp.zeros((nb, PAD_FRONT, D_MODEL), F32)
    meta = jnp.broadcast_to(meta_full[None], (nb, N_META, D_MODEL))
    h0 = jnp.concatenate([pad, meta, x], axis=1).reshape(rows, D_MODEL)
    cosf, sinf = _rot_tables(lp)
    bkt = jnp.asarray(_bucket_table())

    acts = []
    h = h0
    for l in range(DEPTH):
        (w_in, w_br, w_out), zero = weights_of(l, h)
        hb, p_abc = norm_matmul(h, norm_pre[l][None] + zero, w_in, 0, N_ABC_TILES)
        p_m = matmul_cols(hb, w_in, N_ABC_TILES, N_M_TILES)
        br, states = mixers_fwd(p_abc, cosf, sinf, bkt, rel_bias, attn_sinks[l][None], conv_w_full[l], nb, nc)
        h_next = merge_fwd(h, br, p_m, w_br, w_out, norm_post[l][None])
        acts.append((h, hb, p_abc, p_m, br, states, w_in, w_br, w_out))
        h = h_next

    loss_part, d_h = loss_head(h, loss_target.reshape(nb * seq, D_MODEL), nb, nc)

    small = [None] * DEPTH
    for l in reversed(range(DEPTH)):
        h_in, hb, p_abc, p_m, br, states, w_in, w_br, w_out = acts[l]
        d_br, d_m, d_gpost, g_wbr, g_wout = merge_bwd(d_h, br, p_m, w_br, w_out, norm_post[l][None])
        d_abc, d_rb, d_sk, d_cw = mixers_bwd(p_abc, d_br, states, cosf, sinf, bkt, rel_bias,
                                             attn_sinks[l][None], conv_w_full[l], nb, nc)
        g_win = proj_wgrad(hb, d_abc, d_m)
        zero = grads_done(l, [g_win, g_wbr, g_wout])
        d_h, d_gpre = proj_dgrad(d_abc, d_m, w_in, h_in, norm_pre[l][None] + zero, d_h)
        small[l] = (d_gpre[0], d_gpost[0], d_rb, d_sk, d_cw[0:3])

    d_h3 = d_h.reshape(nb, lp, D_MODEL)
    d_x = d_h3[:, BLOCK:]
    d_meta = d_h3[:, PAD_FRONT:BLOCK]
    sp = jnp.concatenate([
        d_meta.reshape(-1, LANES),
        jnp.stack([small[l][0] for l in range(DEPTH)]).reshape(-1, LANES),
        jnp.stack([small[l][1] for l in range(DEPTH)]).reshape(-1, LANES),
        jnp.concatenate([small[l][2] for l in range(DEPTH)], axis=0),
        jnp.concatenate([small[l][3] for l in range(DEPTH)], axis=0),
        jnp.stack([small[l][4] for l in range(DEPTH)]).reshape(-1, LANES),
        loss_part], axis=0)
    return d_x, sp


def kernel(x, meta_tokens, rel_bias, norm_pre, w_in, conv_w, attn_sinks, w_branch, w_out, norm_post, loss_target, m_meta_tokens, m_rel_bias, m_norm_pre, m_w_in, m_conv_w, m_attn_sinks, m_w_branch, m_w_out, m_norm_post, v_meta_tokens, v_rel_bias, v_norm_pre, v_w_in, v_conv_w, v_attn_sinks, v_w_branch, v_w_out, v_norm_post):
    assert x.shape[0] == 2 and SP_META == 2 * N_META * D_MODEL // LANES
    px, py, pc = _place()
    chip = 2 * px + py

    c_arr = jnp.reshape(pc, (1,)).astype(jnp.int32)
    where = jnp.stack([chip, pc]).astype(jnp.int32)
    tr_ = lambda a: jnp.swapaxes(a, 1, 2)
    w3 = [tr_(w_in), w_branch.reshape(DEPTH, N_BRANCH * BRANCH_WIDTH, SHARD_D), w_out]
    halves = lambda a: a.reshape(2, a.shape[0] // 2, a.shape[1])

    def as_weights(bufs):
        a_in, a_br, a_out = bufs
        return (a_in.reshape(PROJ_WIDTH, D_MODEL), a_br.reshape(N_CHIPS, N_BRANCH, BRANCH_WIDTH, SHARD_D),
                a_out.reshape(D_MODEL, D_MODEL))

    side = jnp.concatenate([meta_tokens.reshape(-1), conv_w.reshape(-1)]).reshape(-1, LANES)
    side = jnp.concatenate([side, jnp.zeros((40 - side.shape[0], LANES), F32)], axis=0)
    side_all = exchange_small(side, side)
    side_chips = side_all[0::2]
    n_meta_rows = N_META * SHARD_D // LANES
    meta_full = jnp.moveaxis(side_chips[:, :n_meta_rows].reshape(N_CHIPS, N_META, SHARD_D), 0, 1).reshape(N_META, D_MODEL)
    conv_full = jnp.moveaxis(side_chips[:, n_meta_rows:n_meta_rows + 6].reshape(N_CHIPS, DEPTH, 3, LANES), 0, 2).reshape(DEPTH, 3, BRANCH_WIDTH)

    slots = [[_own_slot(halves(w[l].astype(BF16)), chip) for w in w3] for l in range(DEPTH)]
    gathered0 = gather_weight_shards(slots[0], side_all)
    send1, recv1, flying1, started1 = gather_start(slots[1], gathered0[0])

    def weights_of(l, h):
        if l == 0:
            return as_weights(gathered0), started1[0:1, 0:1]
        landed = gather_forward(gather_wait(flying1, send1, recv1, h))
        return as_weights(landed), jnp.zeros((1, 1), F32)

    reduced = [None] * DEPTH
    flying = {}

    def finish_reduce(l, after):
        partials, parts = scatter_wait(*flying[l], after)
        reduced[l] = sibling_share_layer([sum_chips(a, p, where) for a, p in zip(partials, parts)])

    def grads_done(l, grads):
        if l == 0:
            finish_reduce(1, grads[0])
        full = [g.reshape(N_CHIPS, 2, g.size // (2 * N_CHIPS * g.shape[-1]), g.shape[-1]) for g in grads]
        others = sibling_swap_halves(full)
        send, recv, thru, lands, started = scatter_start([add_own_half(g, o, c_arr) for g, o in zip(full, others)])
        flying[l] = (thru, lands, send, recv)
        return started[0:1, 0:1]

    d_x, sp = local_step(x, loss_target, meta_full, rel_bias, norm_pre, conv_full, attn_sinks, norm_post,
                         weights_of, grads_done)
    finish_reduce(0, sp)

    meta_rows, rest = sum_small(exchange_small(sp, reduced[0][0]))
    o = 0
    g_meta_full = meta_rows.reshape(N_META, D_MODEL)
    g_norm_pre = rest[o:o + SP_NORM].reshape(DEPTH, D_MODEL); o += SP_NORM
    g_norm_post = rest[o:o + SP_NORM].reshape(DEPTH, D_MODEL); o += SP_NORM
    g_rel_bias = rest[o:o + N_BUCKETS, :ATT_HEADS]; o += SP_RB
    g_sinks = rest[o:o + SP_SINK, 0].reshape(DEPTH, ATT_HEADS); o += SP_SINK
    g_conv_full = rest[o:o + SP_CONV].reshape(DEPTH, 3, BRANCH_WIDTH); o += SP_CONV
    loss = rest[o, 0]
    g_meta = lax.dynamic_slice_in_dim(g_meta_full, chip * SHARD_D, SHARD_D, axis=1)
    g_conv = lax.dynamic_slice_in_dim(g_conv_full, chip * LANES, LANES, axis=2)

    m3 = [tr_(m_w_in), m_w_branch.reshape(w3[1].shape), m_w_out]
    v3 = [tr_(v_w_in), v_w_branch.reshape(w3[1].shape), v_w_out]
    big = []
    for t in range(3):
        acc = None
        for l in reversed(range(DEPTH)):
            acc = adamw_layer(w3[t], reduced[l][t].reshape(w3[t].shape[1:]), m3[t], v3[t], l, acc)
        big.append(acc)
    g_in, *u_in = [tr_(a) for a in big[0]]
    g_br, *u_br = [a.reshape(w_branch.shape) for a in big[1]]
    g_out, *u_out = big[2]
    to2 = lambda a: a.reshape(-1, a.shape[-1])
    smalls = [(meta_tokens, g_meta, m_meta_tokens, v_meta_tokens),
              (rel_bias, g_rel_bias, m_rel_bias, v_rel_bias),
              (norm_pre, g_norm_pre, m_norm_pre, v_norm_pre),
              (to2(conv_w), to2(g_conv), to2(m_conv_w), to2(v_conv_w)),
              (attn_sinks, g_sinks, m_attn_sinks, v_attn_sinks),
              (norm_post, g_norm_post, m_norm_post, v_norm_post)]
    u_meta, u_rb, u_npre, u_conv, u_sink, u_npost = adamw_small(smalls)
    u_conv = tuple(a.reshape(conv_w.shape) for a in u_conv)

    grads = [g_meta, g_rel_bias, g_norm_pre, g_in, g_conv, g_sinks, g_br, g_out, g_norm_post]
    upd = [u_meta, u_rb, u_npre, u_in, u_conv, u_sink, u_br, u_out, u_npost]
    return (loss, d_x, *grads, *[u[0] for u in upd], *[u[1] for u in upd], *[u[2] for u in upd])
```

```python
import functools
import math

import numpy as np
import jax
import jax.numpy as jnp
from jax import lax
from jax.experimental import pallas as pl
from jax.experimental.pallas import tpu as pltpu

F32 = jnp.float32
BF16 = jnp.bfloat16
MESH = pl.DeviceIdType.MESH

D_MODEL = 1024
DEPTH = 2
N_META = 16
BLOCK = 128
PAD_FRONT = BLOCK - N_META
ATT_HEADS = 8
ATT_HEAD_DIM = 64
N_BUCKETS = 32
MAX_EXACT = 16
MAX_DISTANCE = 128
RET_HEADS = 4
ROT_BASE = 10000.0
N_BRANCH = 3
BRANCH_WIDTH = 512
PROJ_WIDTH = 8448
ABC_WIDTH = 5376
MERGE_WIDTH = N_BRANCH * D_MODEL
RMS_EPS = 1e-6
GN_EPS = 1e-6
NEG_INF = -1e30
ATT_SCALE = ATT_HEAD_DIM ** -0.5
RET_SCALE = BLOCK ** -0.5
LOG_GAMMA = tuple(math.log1p(-(2.0 ** (-5.0 - h))) for h in range(RET_HEADS))

C_AQ, C_AK, C_AV, C_AG = 0, 512, 640, 768
C_RQ, C_RK, C_RV, C_RG = 1280, 1792, 2304, 2816
C_CB, C_CC, C_CX, C_CG = 3328, 3840, 4352, 4864

ADAM_LR = 0.001
ADAM_B1 = 0.9
ADAM_B2 = 0.999
ADAM_EPS = 1e-08
ADAM_WD = 0.01
ADAM_STEP = 10

N_CHIPS = 4
SHARD_IN = PROJ_WIDTH // N_CHIPS
SHARD_D = D_MODEL // N_CHIPS
LANES = 128
PACK_IN = D_MODEL * SHARD_IN
PACK_BR = N_BRANCH * BRANCH_WIDTH * SHARD_D
PACK_OUT = SHARD_D * D_MODEL
PACK_ROWS = (PACK_IN + PACK_BR + PACK_OUT) // LANES

VMEM_LIMIT = 56 * 1024 * 1024
COL_TILE = 768
ROW_TILE = 1088


def _cparams(*sem):
    return pltpu.CompilerParams(dimension_semantics=sem, vmem_limit_bytes=VMEM_LIMIT)


def _nt(a, b):
    return lax.dot_general(a, b, (((1,), (1,)), ((), ())), preferred_element_type=F32)


def _tn(a, b):
    return lax.dot_general(a, b, (((0,), (0,)), ((), ())), preferred_element_type=F32)


def _nn(a, b):
    return jnp.dot(a, b, preferred_element_type=F32)


def _sigmoid(x):
    return 0.5 * jnp.tanh(0.5 * x) + 0.5


def _silu(x):
    return x * _sigmoid(x)


def _dsilu(x):
    s = _sigmoid(x)
    return s * (1.0 + x * (1.0 - s))


def _bucket_table():
    r = np.arange(BLOCK)[:, None]
    c = np.arange(2 * BLOCK)[None, :]
    n = np.maximum(BLOCK + r - c, 0)
    nf = np.maximum(n, 1).astype(np.float32)
    large = MAX_EXACT + (np.log(nf / MAX_EXACT) / math.log(MAX_DISTANCE / MAX_EXACT)
                         * (N_BUCKETS - MAX_EXACT)).astype(np.int32)
    large = np.minimum(large, N_BUCKETS - 1)
    return np.where(n < MAX_EXACT, n, large).astype(np.int32)


def _rot_tables(lp):
    half = BLOCK // 2
    pos = (jnp.arange(lp) - PAD_FRONT).astype(F32)
    theta = 1.0 / (ROT_BASE ** jnp.linspace(0.0, 1.0, half, dtype=F32))
    ang = pos[:, None] * theta[None, :]
    cos, sin = jnp.cos(ang), jnp.sin(ang)
    return jnp.concatenate([cos, cos], axis=1), jnp.concatenate([-sin, sin], axis=1)


def norm_matmul(x2d, g, w, col0_blocks, n_col_blocks):
    t = x2d.shape[0]
    tm = ROW_TILE if t % ROW_TILE == 0 else BLOCK

    def body(x_ref, g_ref, w_ref, hb_ref, o_ref):
        @pl.when(pl.program_id(1) == 0)
        def _():
            x = x_ref[...]
            r = lax.rsqrt(jnp.mean(x * x, axis=-1, keepdims=True) + RMS_EPS)
            hb_ref[...] = (x * r * g_ref[...]).astype(BF16)

        o_ref[...] = _nt(hb_ref[...], w_ref[...])

    return pl.pallas_call(
        body, name="norm_matmul",
        grid=(t // tm, n_col_blocks),
        in_specs=[pl.BlockSpec((tm, D_MODEL), lambda i, j: (i, 0)),
                  pl.BlockSpec((1, D_MODEL), lambda i, j: (0, 0)),
                  pl.BlockSpec((COL_TILE, D_MODEL), lambda i, j: (j + col0_blocks, 0))],
        out_specs=[pl.BlockSpec((tm, D_MODEL), lambda i, j: (i, 0)),
                   pl.BlockSpec((tm, COL_TILE), lambda i, j: (i, j))],
        out_shape=[jax.ShapeDtypeStruct((t, D_MODEL), BF16),
                   jax.ShapeDtypeStruct((t, n_col_blocks * COL_TILE), F32)],
        compiler_params=_cparams("parallel", "arbitrary"),
    )(x2d, g, w)


def matmul_cols(a, w, col0_blocks, n_col_blocks):
    t, k = a.shape
    tm = ROW_TILE if t % ROW_TILE == 0 else BLOCK

    def body(a_ref, w_ref, o_ref):
        o_ref[...] = _nt(a_ref[...], w_ref[...])

    return pl.pallas_call(
        body, name="matmul_cols",
        grid=(t // tm, n_col_blocks),
        in_specs=[pl.BlockSpec((tm, k), lambda i, j: (i, 0)),
                  pl.BlockSpec((COL_TILE, k), lambda i, j: (j + col0_blocks, 0))],
        out_specs=pl.BlockSpec((tm, COL_TILE), lambda i, j: (i, j)),
        out_shape=jax.ShapeDtypeStruct((t, n_col_blocks * COL_TILE), F32),
        compiler_params=_cparams("parallel", "arbitrary"),
    )(a, w)


def _build_bias(bkt_ref, rb_ref, bias_s):
    bkt = bkt_ref[...]
    for h in range(ATT_HEADS):
        acc = jnp.zeros((BLOCK, 2 * BLOCK), F32)
        for b in range(N_BUCKETS):
            acc = jnp.where(bkt == b, rb_ref[b, h], acc)
        bias_s[h] = acc


def _band_mask(n):
    r = lax.broadcasted_iota(jnp.int32, (BLOCK, 2 * BLOCK), 0)
    c = lax.broadcasted_iota(jnp.int32, (BLOCK, 2 * BLOCK), 1)
    key_pos = (n - 1) * BLOCK + c
    return (c > r) & (c <= r + BLOCK) & (key_pos >= PAD_FRONT)


def _split_heads(kv, kh):
    lane = lax.broadcasted_iota(jnp.int32, kv.shape, 1)
    if kh == 0:
        lo = jnp.where(lane < ATT_HEAD_DIM, kv, 0.0)
        hi = pltpu.roll(lo, ATT_HEAD_DIM, 1)
    else:
        hi = jnp.where(lane >= ATT_HEAD_DIM, kv, 0.0)
        lo = pltpu.roll(hi, ATT_HEAD_DIM, 1)
    return lo, hi


def _merge_heads(acc_lo, acc_hi, kh):
    lane = lax.broadcasted_iota(jnp.int32, acc_lo.shape, 1)
    if kh == 0:
        return jnp.where(lane < ATT_HEAD_DIM, acc_lo + pltpu.roll(acc_hi, ATT_HEAD_DIM, 1), 0.0)
    return jnp.where(lane >= ATT_HEAD_DIM, acc_hi + pltpu.roll(acc_lo, ATT_HEAD_DIM, 1), 0.0)


def _softmax_sink(q2b, kxb, bias_h, mask, sink_h):
    return _softmax_of(_nt(q2b, kxb), bias_h, mask, sink_h)


def _softmax_of(qk, bias_h, mask, sink_h):
    s = qk * ATT_SCALE + bias_h
    s = jnp.where(mask, s, NEG_INF)
    m = jnp.maximum(jnp.max(s, axis=-1, keepdims=True), sink_h)
    p = jnp.exp(s - m)
    es = jnp.exp(sink_h - m)
    inv = 1.0 / (jnp.sum(p, axis=-1, keepdims=True) + es)
    return p * inv, es * inv


def _rot(t, cosf, sinf):
    return t * cosf + pltpu.roll(t, BLOCK // 2, 1) * sinf


def _rot_t(d, cosf, sinf):
    return d * cosf + pltpu.roll(d * sinf, BLOCK // 2, 1)


def _decay_tables(h):
    lg = LOG_GAMMA[h]
    i = lax.broadcasted_iota(jnp.int32, (BLOCK, BLOCK), 0)
    j = lax.broadcasted_iota(jnp.int32, (BLOCK, BLOCK), 1)
    diff = (i - j).astype(F32)
    dm = jnp.where(diff >= 0, jnp.exp(diff * lg), 0.0)
    row = lax.broadcasted_iota(jnp.int32, (BLOCK, 1), 0).astype(F32)
    zeta = jnp.exp((BLOCK - 1 - row) * lg)
    xi = jnp.exp((row + 1.0) * lg)
    return dm, zeta, xi, math.exp(BLOCK * lg)


def _valid_col(n):
    row = lax.broadcasted_iota(jnp.int32, (BLOCK, 1), 0)
    return ((n * BLOCK + row) >= PAD_FRONT).astype(F32)


def _shift_down(cur, prev, k):
    row = lax.broadcasted_iota(jnp.int32, cur.shape, 0)
    return jnp.where(row >= k, pltpu.roll(cur, k, 0), pltpu.roll(prev, k, 0))


def _shift_up(cur, nxt, k):
    row = lax.broadcasted_iota(jnp.int32, cur.shape, 0)
    return jnp.where(row < BLOCK - k, pltpu.roll(cur, BLOCK - k, 0), pltpu.roll(nxt, BLOCK - k, 0))


def mixers_fwd(proj, cosf, sinf, bkt, rel_bias, sinks, conv_w, nb, nc):
    def body(p_ref, cos_ref, sin_ref, bkt_ref, rb_ref, sk_ref, cw_ref, br_ref, st_ref,
             bias_s, kv_s, state_s, u_s):
        n = pl.program_id(0)

        @pl.when(n == 0)
        def _():
            _build_bias(bkt_ref, rb_ref, bias_s)
            kv_s[:, 0:BLOCK, :] = jnp.zeros((nb, BLOCK, 2 * BLOCK), F32)
            state_s[...] = jnp.zeros_like(state_s)
            u_s[...] = jnp.zeros_like(u_s)

        valid = _valid_col(n)
        mask = _band_mask(n)
        ex = range(nb)

        for b in ex:
            kv_s[b, BLOCK:2 * BLOCK, :] = p_ref[b, :, C_AK:C_AK + 2 * BLOCK]
        for kh in range(2):
            ks = [[t.astype(BF16) for t in _split_heads(kv_s[b, :, 0:BLOCK], kh)] for b in ex]
            vs = [[t.astype(BF16) for t in _split_heads(kv_s[b, :, BLOCK:2 * BLOCK], kh)] for b in ex]
            pairs = [(b, 2 * kh + jj) for jj in range(2) for b in ex]
            subs = [(b, j, x) for (b, j) in pairs for x in range(2)]
            qb_ = {(b, j): p_ref[b, :, C_AQ + BLOCK * j:C_AQ + BLOCK * (j + 1)].astype(BF16) for (b, j) in pairs}
            qk_ = {(b, j, x): _nt(qb_[(b, j)], ks[b][x]) for (b, j, x) in subs}
            pb_ = {}
            for u in subs:
                h = 2 * u[1] + u[2]
                pb_[u] = _softmax_of(qk_[u], bias_s[h], mask, sk_ref[0, h])[0].astype(BF16)
            o_ = {u: _nn(pb_[u], vs[u[0]][u[2]]) for u in subs}
            for (b, j) in pairs:
                gate = p_ref[b, :, C_AG + BLOCK * j:C_AG + BLOCK * (j + 1)]
                br_ref[b, :, BLOCK * j:BLOCK * (j + 1)] = ((o_[(b, j, 0)] + o_[(b, j, 1)]) * _silu(gate)).astype(BF16)
        for b in ex:
            kv_s[b, 0:BLOCK, :] = kv_s[b, BLOCK:2 * BLOCK, :]

        cosv = cos_ref[...]
        sinv = sin_ref[...]
        tabs = [_decay_tables(h) for h in range(RET_HEADS)]
        units = [(b, h) for h in range(RET_HEADS) for b in ex]
        sl = lambda c0, h: slice(c0 + BLOCK * h, c0 + BLOCK * (h + 1))
        q_, k_, v_, sp_ = {}, {}, {}, {}
        for u in units:
            b, h = u
            q_[u] = _rot(p_ref[b, :, sl(C_RQ, h)], cosv, sinv).astype(BF16)
            k_[u] = (_rot(p_ref[b, :, sl(C_RK, h)], cosv, sinv) * RET_SCALE * valid).astype(BF16)
            v_[u] = p_ref[b, :, sl(C_RV, h)]
            sp_[u] = state_s[b, h]
            st_ref[b, 0, h] = sp_[u]
        qk_ = {u: _nt(q_[u], k_[u]) for u in units}
        qs_ = {u: _nn(q_[u], sp_[u].astype(BF16)) for u in units}
        kv_ = {u: _tn(k_[u], (v_[u] * tabs[u[1]][1]).astype(BF16)) for u in units}
        a_ = {u: (qk_[u] * tabs[u[1]][0]).astype(BF16) for u in units}
        av_ = {u: _nn(a_[u], v_[u].astype(BF16)) for u in units}
        for u in units:
            b, h = u
            o = av_[u] + tabs[h][2] * qs_[u]
            mu = jnp.mean(o, axis=-1, keepdims=True)
            var = jnp.mean(jnp.square(o - mu), axis=-1, keepdims=True)
            oh = (o - mu) * lax.rsqrt(var + GN_EPS)
            gate = p_ref[b, :, sl(C_RG, h)]
            br_ref[b, :, BRANCH_WIDTH + BLOCK * h:BRANCH_WIDTH + BLOCK * (h + 1)] = (oh * _silu(gate)).astype(BF16)
            state_s[b, h] = tabs[h][3] * sp_[u] + kv_[u]

        for b in ex:
            u = p_ref[b, :, C_CC:C_CC + BRANCH_WIDTH] * p_ref[b, :, C_CX:C_CX + BRANCH_WIDTH] * valid
            u_prev = u_s[b]
            y = (cw_ref[0:1, :] * _shift_down(u, u_prev, 2) + cw_ref[1:2, :] * _shift_down(u, u_prev, 1)
                 + cw_ref[2:3, :] * u)
            yc = p_ref[b, :, C_CB:C_CB + BRANCH_WIDTH] * y * _silu(p_ref[b, :, C_CG:C_CG + BRANCH_WIDTH])
            br_ref[b, :, 2 * BRANCH_WIDTH:3 * BRANCH_WIDTH] = yc.astype(BF16)
            u_s[b] = u

    lp = nc * BLOCK
    smem = pl.BlockSpec(memory_space=pltpu.SMEM)
    br, states = pl.pallas_call(
        body, name="mixers_fwd",
        grid=(nc,),
        in_specs=[pl.BlockSpec((nb, BLOCK, ABC_WIDTH), lambda n: (0, n, 0)),
                  pl.BlockSpec((BLOCK, BLOCK), lambda n: (n, 0)),
                  pl.BlockSpec((BLOCK, BLOCK), lambda n: (n, 0)),
                  pl.BlockSpec((BLOCK, 2 * BLOCK), lambda n: (0, 0)),
                  smem, smem,
                  pl.BlockSpec((3, BRANCH_WIDTH), lambda n: (0, 0))],
        out_specs=[pl.BlockSpec((nb, BLOCK, N_BRANCH * BRANCH_WIDTH), lambda n: (0, n, 0)),
                   pl.BlockSpec((nb, 1, RET_HEADS, BLOCK, BLOCK), lambda n: (0, n, 0, 0, 0))],
        out_shape=[jax.ShapeDtypeStruct((nb, lp, N_BRANCH * BRANCH_WIDTH), BF16),
                   jax.ShapeDtypeStruct((nb, nc, RET_HEADS, BLOCK, BLOCK), F32)],
        scratch_shapes=[pltpu.VMEM((ATT_HEADS, BLOCK, 2 * BLOCK), F32),
                        pltpu.VMEM((nb, 2 * BLOCK, 2 * BLOCK), F32),
                        pltpu.VMEM((nb, RET_HEADS, BLOCK, BLOCK), F32),
                        pltpu.VMEM((nb, BLOCK, BRANCH_WIDTH), F32)],
        compiler_params=_cparams("arbitrary"),
    )(proj.reshape(nb, lp, ABC_WIDTH), cosf, sinf, bkt, rel_bias, sinks, conv_w)
    return br.reshape(nb * lp, N_BRANCH * BRANCH_WIDTH), states


def mixers_bwd(proj, d_br, states, cosf, sinf, bkt, rel_bias, sinks, conv_w, nb, nc):
    def body(p_ref, kvp_ref, cp_ref, dbr_ref, st_ref, cos_ref, sin_ref, bkt_ref, rb_ref, sk_ref, cw_ref,
             dp_ref, drb_ref, dsk_ref, dcw_ref,
             bias_s, dbias_s, dkv_s, g_s, dy_s):
        step = pl.program_id(0)
        n = nc - 1 - step
        ex = range(nb)

        @pl.when(step == 0)
        def _():
            _build_bias(bkt_ref, rb_ref, bias_s)
            dbias_s[...] = jnp.zeros_like(dbias_s)
            dsk_ref[...] = jnp.zeros_like(dsk_ref)
            dcw_ref[...] = jnp.zeros_like(dcw_ref)
            drb_ref[...] = jnp.zeros_like(drb_ref)
            dkv_s[...] = jnp.zeros_like(dkv_s)
            g_s[...] = jnp.zeros_like(g_s)
            dy_s[...] = jnp.zeros_like(dy_s)

        valid = _valid_col(n)
        mask = _band_mask(n)
        has_prev = (n > 0).astype(F32)

        k_all, v_all = [], []
        for b in ex:
            kv_prev = kvp_ref[b] * has_prev
            kv_cur = p_ref[b, :, C_AK:C_AK + 2 * BLOCK]
            k_all.append(jnp.concatenate([kv_prev[:, 0:BLOCK], kv_cur[:, 0:BLOCK]], axis=0))
            v_all.append(jnp.concatenate([kv_prev[:, BLOCK:], kv_cur[:, BLOCK:]], axis=0))
        zero2 = jnp.zeros((2 * BLOCK, BLOCK), F32)
        dk_tot = [zero2 for _ in ex]
        dv_tot = [zero2 for _ in ex]
        for kh in range(2):
            ks = [[t.astype(BF16) for t in _split_heads(k_all[b], kh)] for b in ex]
            vs = [[t.astype(BF16) for t in _split_heads(v_all[b], kh)] for b in ex]
            pairs = [(b, 2 * kh + jj) for jj in range(2) for b in ex]
            subs = [(b, j, x) for (b, j) in pairs for x in range(2)]
            qb_, gate_, dya_, do2_ = {}, {}, {}, {}
            for w in pairs:
                b, j = w
                qb_[w] = p_ref[b, :, C_AQ + BLOCK * j:C_AQ + BLOCK * (j + 1)].astype(BF16)
                gate_[w] = p_ref[b, :, C_AG + BLOCK * j:C_AG + BLOCK * (j + 1)]
                dya_[w] = dbr_ref[b, :, BLOCK * j:BLOCK * (j + 1)]
                do2_[w] = (dya_[w] * _silu(gate_[w])).astype(BF16)
            qk_ = {(b, j, x): _nt(qb_[(b, j)], ks[b][x]) for (b, j, x) in subs}
            dpm_ = {(b, j, x): _nt(do2_[(b, j)], vs[b][x]) for (b, j, x) in subs}
            pb_, dsb_ = {}, {}
            for u in subs:
                b, j, x = u
                h = 2 * j + x
                p, p_sink = _softmax_of(qk_[u], bias_s[h], mask, sk_ref[0, h])
                pb_[u] = p.astype(BF16)
                delta = jnp.sum(p * dpm_[u], axis=-1, keepdims=True)
                ds = p * (dpm_[u] - delta)
                dbias_s[h] += ds
                dsk_ref[h:h + 1, :] += jnp.broadcast_to(
                    jnp.sum(-p_sink * delta, axis=0, keepdims=True), (1, BLOCK))
                dsb_[u] = ds.astype(BF16)
            o_ = {u: _nn(pb_[u], vs[u[0]][u[2]]) for u in subs}
            dq_ = {u: _nn(dsb_[u], ks[u[0]][u[2]]) for u in subs}
            dkm_ = {u: _tn(dsb_[u], qb_[(u[0], u[1])]) for u in subs}
            dvm_ = {u: _tn(pb_[u], do2_[(u[0], u[1])]) for u in subs}
            for w in pairs:
                b, j = w
                o2 = o_[(b, j, 0)] + o_[(b, j, 1)]
                dq2 = (dq_[(b, j, 0)] + dq_[(b, j, 1)]) * ATT_SCALE
                dp_ref[b, :, C_AQ + BLOCK * j:C_AQ + BLOCK * (j + 1)] = dq2.astype(BF16)
                dp_ref[b, :, C_AG + BLOCK * j:C_AG + BLOCK * (j + 1)] = (
                    dya_[w] * o2 * _dsilu(gate_[w])).astype(BF16)
            for b in ex:
                j0, j1 = 2 * kh, 2 * kh + 1
                dk_lo = (dkm_[(b, j0, 0)] + dkm_[(b, j1, 0)]) * ATT_SCALE
                dk_hi = (dkm_[(b, j0, 1)] + dkm_[(b, j1, 1)]) * ATT_SCALE
                dk_tot[b] = dk_tot[b] + _merge_heads(dk_lo, dk_hi, kh)
                dv_tot[b] = dv_tot[b] + _merge_heads(dvm_[(b, j0, 0)] + dvm_[(b, j1, 0)],
                                                     dvm_[(b, j0, 1)] + dvm_[(b, j1, 1)], kh)
        for b in ex:
            dp_ref[b, :, C_AK:C_AK + BLOCK] = (dk_tot[b][BLOCK:, :] + dkv_s[b, :, 0:BLOCK]).astype(BF16)
            dp_ref[b, :, C_AV:C_AV + BLOCK] = (dv_tot[b][BLOCK:, :] + dkv_s[b, :, BLOCK:]).astype(BF16)
            dkv_s[b, :, 0:BLOCK] = dk_tot[b][0:BLOCK, :]
            dkv_s[b, :, BLOCK:] = dv_tot[b][0:BLOCK, :]

        cosv = cos_ref[...]
        sinv = sin_ref[...]
        tabs = [_decay_tables(h) for h in range(RET_HEADS)]
        units = [(b, h) for h in range(RET_HEADS) for b in ex]
        sl = lambda c0, h: slice(c0 + BLOCK * h, c0 + BLOCK * (h + 1))
        q_, k_, v_, vb_, sp_ = {}, {}, {}, {}, {}
        for u in units:
            b, h = u
            q_[u] = _rot(p_ref[b, :, sl(C_RQ, h)], cosv, sinv).astype(BF16)
            k_[u] = (_rot(p_ref[b, :, sl(C_RK, h)], cosv, sinv) * RET_SCALE * valid).astype(BF16)
            v_[u] = p_ref[b, :, sl(C_RV, h)]
            vb_[u] = v_[u].astype(BF16)
            sp_[u] = st_ref[b, 0, h].astype(BF16)
        qk_ = {u: _nt(q_[u], k_[u]) for u in units}
        qs_ = {u: _nn(q_[u], sp_[u]) for u in units}
        a_ = {u: (qk_[u] * tabs[u[1]][0]).astype(BF16) for u in units}
        av_ = {u: _nn(a_[u], vb_[u]) for u in units}
        dob_, dxo_ = {}, {}
        for u in units:
            b, h = u
            xi = tabs[h][2]
            o = av_[u] + xi * qs_[u]
            mu = jnp.mean(o, axis=-1, keepdims=True)
            var = jnp.mean(jnp.square(o - mu), axis=-1, keepdims=True)
            rstd = lax.rsqrt(var + GN_EPS)
            oh = (o - mu) * rstd
            gate = p_ref[b, :, sl(C_RG, h)]
            d_yr = dbr_ref[b, :, BRANCH_WIDTH + BLOCK * h:BRANCH_WIDTH + BLOCK * (h + 1)]
            dp_ref[b, :, sl(C_RG, h)] = (d_yr * oh * _dsilu(gate)).astype(BF16)
            doh = d_yr * _silu(gate)
            do = rstd * (doh - jnp.mean(doh, axis=-1, keepdims=True)
                         - oh * jnp.mean(doh * oh, axis=-1, keepdims=True))
            dob_[u] = do.astype(BF16)
            dxo_[u] = (do * xi).astype(BF16)
        dov_ = {u: _nt(dob_[u], vb_[u]) for u in units}
        dv1_ = {u: _tn(a_[u], dob_[u]) for u in units}
        dq1_ = {u: _nt(dxo_[u], sp_[u]) for u in units}
        gq_ = {u: _tn(q_[u], dxo_[u]) for u in units}
        da_, gb_, zv_ = {}, {}, {}
        for u in units:
            b, h = u
            da_[u] = (dov_[u] * tabs[h][0]).astype(BF16)
            g_next = g_s[b, h]
            gb_[u] = g_next.astype(BF16)
            zv_[u] = (v_[u] * tabs[h][1]).astype(BF16)
            g_s[b, h] = tabs[h][3] * g_next + gq_[u]
        dq2_ = {u: _nn(da_[u], k_[u]) for u in units}
        dk1_ = {u: _tn(da_[u], q_[u]) for u in units}
        dk2_ = {u: _nt(zv_[u], gb_[u]) for u in units}
        dv2_ = {u: _nn(k_[u], gb_[u]) for u in units}
        for u in units:
            b, h = u
            dp_ref[b, :, sl(C_RQ, h)] = _rot_t(dq2_[u] + dq1_[u], cosv, sinv).astype(BF16)
            dp_ref[b, :, sl(C_RK, h)] = _rot_t((dk1_[u] + dk2_[u]) * (RET_SCALE * valid), cosv, sinv).astype(BF16)
            dp_ref[b, :, sl(C_RV, h)] = (dv1_[u] + tabs[h][1] * dv2_[u]).astype(BF16)

        w0, w1, w2 = cw_ref[0:1, :], cw_ref[1:2, :], cw_ref[2:3, :]
        for b in ex:
            cb = p_ref[b, :, C_CB:C_CB + BRANCH_WIDTH]
            cc = p_ref[b, :, C_CC:C_CC + BRANCH_WIDTH]
            cx = p_ref[b, :, C_CX:C_CX + BRANCH_WIDTH]
            cg = p_ref[b, :, C_CG:C_CG + BRANCH_WIDTH]
            u = cc * cx * valid
            u_prev = (cp_ref[b, :, 0:BRANCH_WIDTH] * cp_ref[b, :, BRANCH_WIDTH:2 * BRANCH_WIDTH]
                      * (_valid_col(n - 1) * has_prev))
            u1 = _shift_down(u, u_prev, 1)
            u2 = _shift_down(u, u_prev, 2)
            y = w0 * u2 + w1 * u1 + w2 * u
            d_yc = dbr_ref[b, :, 2 * BRANCH_WIDTH:3 * BRANCH_WIDTH]
            sg = _silu(cg)
            dp_ref[b, :, C_CB:C_CB + BRANCH_WIDTH] = (d_yc * y * sg).astype(BF16)
            dp_ref[b, :, C_CG:C_CG + BRANCH_WIDTH] = (d_yc * cb * y * _dsilu(cg)).astype(BF16)
            dy = d_yc * cb * sg
            dy_next = dy_s[b]
            du = (w2 * dy + w1 * _shift_up(dy, dy_next, 1) + w0 * _shift_up(dy, dy_next, 2)) * valid
            dp_ref[b, :, C_CC:C_CC + BRANCH_WIDTH] = (du * cx).astype(BF16)
            dp_ref[b, :, C_CX:C_CX + BRANCH_WIDTH] = (du * cc).astype(BF16)
            dcw_ref[0:1, :] += jnp.sum(dy * u2, axis=0, keepdims=True)
            dcw_ref[1:2, :] += jnp.sum(dy * u1, axis=0, keepdims=True)
            dcw_ref[2:3, :] += jnp.sum(dy * u, axis=0, keepdims=True)
            dy_s[b] = dy

        @pl.when(step == nc - 1)
        def _():
            bkt = bkt_ref[...]
            row = lax.broadcasted_iota(jnp.int32, (N_BUCKETS, BLOCK), 0)
            lane = lax.broadcasted_iota(jnp.int32, (N_BUCKETS, BLOCK), 1)

            def one_bucket(bk, acc):
                sel = bkt == bk
                for h in range(ATT_HEADS):
                    t = jnp.where(sel, dbias_s[h], 0.0)
                    s = jnp.sum(jnp.sum(t, axis=1, keepdims=True), axis=0, keepdims=True)
                    acc = acc + jnp.where((row == bk) & (lane == h), jnp.broadcast_to(s, acc.shape), 0.0)
                return acc

            drb_ref[...] = lax.fori_loop(0, N_BUCKETS, one_bucket, jnp.zeros((N_BUCKETS, BLOCK), F32))

    lp = nc * BLOCK
    smem = pl.BlockSpec(memory_space=pltpu.SMEM)
    blk = lambda s: nc - 1 - s
    prev = lambda s: jnp.maximum(nc - 2 - s, 0)
    proj3 = proj.reshape(nb, lp, ABC_WIDTH)
    res = pl.pallas_call(
        body, name="mixers_bwd",
        grid=(nc,),
        in_specs=[pl.BlockSpec((nb, BLOCK, ABC_WIDTH), lambda s: (0, blk(s), 0)),
                  pl.BlockSpec((nb, BLOCK, 2 * BLOCK), lambda s: (0, prev(s), C_AK // (2 * BLOCK))),
                  pl.BlockSpec((nb, BLOCK, 1280), lambda s: (0, prev(s), C_CC // 1280)),
                  pl.BlockSpec((nb, BLOCK, N_BRANCH * BRANCH_WIDTH), lambda s: (0, blk(s), 0)),
                  pl.BlockSpec((nb, 1, RET_HEADS, BLOCK, BLOCK), lambda s: (0, blk(s), 0, 0, 0)),
                  pl.BlockSpec((BLOCK, BLOCK), lambda s: (blk(s), 0)),
                  pl.BlockSpec((BLOCK, BLOCK), lambda s: (blk(s), 0)),
                  pl.BlockSpec((BLOCK, 2 * BLOCK), lambda s: (0, 0)),
                  smem, smem,
                  pl.BlockSpec((3, BRANCH_WIDTH), lambda s: (0, 0))],
        out_specs=[pl.BlockSpec((nb, BLOCK, ABC_WIDTH), lambda s: (0, blk(s), 0)),
                   pl.BlockSpec((N_BUCKETS, BLOCK), lambda s: (0, 0)),
                   pl.BlockSpec((ATT_HEADS, BLOCK), lambda s: (0, 0)),
                   pl.BlockSpec((8, BRANCH_WIDTH), lambda s: (0, 0))],
        out_shape=[jax.ShapeDtypeStruct((nb, lp, ABC_WIDTH), BF16),
                   jax.ShapeDtypeStruct((N_BUCKETS, BLOCK), F32),
                   jax.ShapeDtypeStruct((ATT_HEADS, BLOCK), F32),
                   jax.ShapeDtypeStruct((8, BRANCH_WIDTH), F32)],
        scratch_shapes=[pltpu.VMEM((ATT_HEADS, BLOCK, 2 * BLOCK), F32),
                        pltpu.VMEM((ATT_HEADS, BLOCK, 2 * BLOCK), F32),
                        pltpu.VMEM((nb, BLOCK, 2 * BLOCK), F32),
                        pltpu.VMEM((nb, RET_HEADS, BLOCK, BLOCK), F32),
                        pltpu.VMEM((nb, BLOCK, BRANCH_WIDTH), F32)],
        compiler_params=_cparams("arbitrary"),
    )(proj3, proj3, proj3, d_br.reshape(nb, lp, N_BRANCH * BRANCH_WIDTH), states, cosf, sinf, bkt, rel_bias, sinks,
      conv_w)
    return (res[0].reshape(nb * lp, ABC_WIDTH),) + tuple(res[1:])


MERGE_TILE = 256


def _merge_forward(br_ref, m_ref, wb_ref, wo_ref):
    bo, gates = [], []
    mixed_pre = None
    for g in range(N_BRANCH):
        br_g = br_ref[:, BRANCH_WIDTH * g:BRANCH_WIDTH * (g + 1)]
        bo_g = jnp.concatenate([_nn(br_g, wb_ref[p, g]) for p in range(N_CHIPS)], axis=1)
        gate_g = _sigmoid(m_ref[:, D_MODEL * g:D_MODEL * (g + 1)])
        bo.append(bo_g)
        gates.append(gate_g)
        mixed_pre = gate_g * bo_g if mixed_pre is None else mixed_pre + gate_g * bo_g
    mixed = _nn(mixed_pre.astype(BF16), wo_ref[...])
    r = lax.rsqrt(jnp.mean(mixed * mixed, axis=-1, keepdims=True) + RMS_EPS)
    return bo, gates, mixed_pre, mixed, r


def merge_fwd(x2d, br, pm, wb, wo, g_post):
    t = x2d.shape[0]
    tm = MERGE_TILE if t % MERGE_TILE == 0 else BLOCK

    def body(x_ref, br_ref, m_ref, wb_ref, wo_ref, g_ref, o_ref):
        _, _, _, mixed, r = _merge_forward(br_ref, m_ref, wb_ref, wo_ref)
        o_ref[...] = x_ref[...] + mixed * r * g_ref[...]

    return pl.pallas_call(
        body, name="merge_fwd",
        grid=(t // tm,),
        in_specs=[pl.BlockSpec((tm, D_MODEL), lambda i: (i, 0)),
                  pl.BlockSpec((tm, N_BRANCH * BRANCH_WIDTH), lambda i: (i, 0)),
                  pl.BlockSpec((tm, MERGE_WIDTH), lambda i: (i, 0)),
                  pl.BlockSpec((N_CHIPS, N_BRANCH, BRANCH_WIDTH, SHARD_D), lambda i: (0, 0, 0, 0)),
                  pl.BlockSpec((D_MODEL, D_MODEL), lambda i: (0, 0)),
                  pl.BlockSpec((1, D_MODEL), lambda i: (0, 0))],
        out_specs=pl.BlockSpec((tm, D_MODEL), lambda i: (i, 0)),
        out_shape=jax.ShapeDtypeStruct((t, D_MODEL), F32),
        compiler_params=_cparams("parallel"),
    )(x2d, br, pm, wb, wo, g_post)


def merge_bwd(d_out, br, pm, wb, wo, g_post):
    t = d_out.shape[0]
    tm = MERGE_TILE if t % MERGE_TILE == 0 else BLOCK

    def body(do_ref, br_ref, m_ref, wb_ref, wo_ref, g_ref, dbr_ref, dm_ref, dg_ref, dwb_ref, dwo_ref):

        @pl.when(pl.program_id(0) == 0)
        def _():
            dwb_ref[...] = jnp.zeros_like(dwb_ref)
            dwo_ref[...] = jnp.zeros_like(dwo_ref)
            dg_ref[...] = jnp.zeros_like(dg_ref)

        bo, gates, mixed_pre, mixed, r = _merge_forward(br_ref, m_ref, wb_ref, wo_ref)
        d_o = do_ref[...]
        nh = mixed * r
        dg_ref[0:1, :] += jnp.sum(d_o * nh, axis=0, keepdims=True)
        dn = d_o * g_ref[...]
        d_mixed = (r * (dn - nh * jnp.mean(dn * nh, axis=-1, keepdims=True))).astype(BF16)
        dwo_ref[...] += _tn(mixed_pre.astype(BF16), d_mixed)
        d_pre = _nt(d_mixed, wo_ref[...])
        for g in range(N_BRANCH):
            br_g = br_ref[:, BRANCH_WIDTH * g:BRANCH_WIDTH * (g + 1)]
            d_bo = (d_pre * gates[g]).astype(BF16)
            dm_ref[:, D_MODEL * g:D_MODEL * (g + 1)] = (
                d_pre * bo[g] * gates[g] * (1.0 - gates[g])).astype(BF16)
            d_br_g = None
            for p in range(N_CHIPS):
                d_bo_p = d_bo[:, SHARD_D * p:SHARD_D * (p + 1)]
                part = _nt(d_bo_p, wb_ref[p, g])
                d_br_g = part if d_br_g is None else d_br_g + part
                dwb_ref[p, g] += _tn(br_g, d_bo_p)
            dbr_ref[:, BRANCH_WIDTH * g:BRANCH_WIDTH * (g + 1)] = d_br_g

    return pl.pallas_call(
        body, name="merge_bwd",
        grid=(t // tm,),
        in_specs=[pl.BlockSpec((tm, D_MODEL), lambda i: (i, 0)),
                  pl.BlockSpec((tm, N_BRANCH * BRANCH_WIDTH), lambda i: (i, 0)),
                  pl.BlockSpec((tm, MERGE_WIDTH), lambda i: (i, 0)),
                  pl.BlockSpec((N_CHIPS, N_BRANCH, BRANCH_WIDTH, SHARD_D), lambda i: (0, 0, 0, 0)),
                  pl.BlockSpec((D_MODEL, D_MODEL), lambda i: (0, 0)),
                  pl.BlockSpec((1, D_MODEL), lambda i: (0, 0))],
        out_specs=[pl.BlockSpec((tm, N_BRANCH * BRANCH_WIDTH), lambda i: (i, 0)),
                   pl.BlockSpec((tm, MERGE_WIDTH), lambda i: (i, 0)),
                   pl.BlockSpec((8, D_MODEL), lambda i: (0, 0)),
                   pl.BlockSpec((N_CHIPS, N_BRANCH, BRANCH_WIDTH, SHARD_D), lambda i: (0, 0, 0, 0)),
                   pl.BlockSpec((D_MODEL, D_MODEL), lambda i: (0, 0))],
        out_shape=[jax.ShapeDtypeStruct((t, N_BRANCH * BRANCH_WIDTH), F32),
                   jax.ShapeDtypeStruct((t, MERGE_WIDTH), BF16),
                   jax.ShapeDtypeStruct((8, D_MODEL), F32),
                   jax.ShapeDtypeStruct((N_CHIPS, N_BRANCH, BRANCH_WIDTH, SHARD_D), F32),
                   jax.ShapeDtypeStruct((D_MODEL, D_MODEL), F32)],
        compiler_params=_cparams("arbitrary"),
    )(d_out, br, pm, wb, wo, g_post)


def loss_head(xf, target2d, nb, nc):
    def body(x_ref, t_ref, l_ref, dx_ref):
        b = pl.program_id(0)
        n = pl.program_id(1)

        @pl.when((b == 0) & (n == 0))
        def _():
            l_ref[...] = jnp.zeros_like(l_ref)

        @pl.when(n == 0)
        def _():
            dx_ref[...] = jnp.zeros_like(dx_ref)

        @pl.when(n > 0)
        def _():
            e = x_ref[...] - t_ref[...]
            dx_ref[...] = e * (1.0 / D_MODEL)
            s = jnp.sum(jnp.sum(e * e, axis=1, keepdims=True), axis=0, keepdims=True)
            l_ref[...] += jnp.broadcast_to(s * (0.5 / D_MODEL), l_ref.shape)

    return pl.pallas_call(
        body, name="loss_head",
        grid=(nb, nc),
        in_specs=[pl.BlockSpec((BLOCK, D_MODEL), lambda b, n: (b * nc + n, 0)),
                  pl.BlockSpec((BLOCK, D_MODEL), lambda b, n: (b * (nc - 1) + jnp.maximum(n - 1, 0), 0))],
        out_specs=[pl.BlockSpec((8, BLOCK), lambda b, n: (0, 0)),
                   pl.BlockSpec((BLOCK, D_MODEL), lambda b, n: (b * nc + n, 0))],
        out_shape=[jax.ShapeDtypeStruct((8, BLOCK), F32),
                   jax.ShapeDtypeStruct(xf.shape, F32)],
        compiler_params=_cparams("arbitrary", "arbitrary"),
    )(xf, target2d)


N_ABC_TILES = ABC_WIDTH // COL_TILE
N_M_TILES = MERGE_WIDTH // COL_TILE


def proj_dgrad(d_abc, d_m, w, x2d, g, d_out):
    t = x2d.shape[0]
    tm = ROW_TILE if t % ROW_TILE == 0 else BLOCK
    nk = N_ABC_TILES + N_M_TILES

    def body(da_ref, dm_ref, w_ref, x_ref, g_ref, do_ref, dx_ref, dg_ref, acc):
        i = pl.program_id(0)
        k = pl.program_id(1)

        @pl.when((i == 0) & (k == 0))
        def _():
            dg_ref[...] = jnp.zeros_like(dg_ref)

        @pl.when(k == 0)
        def _():
            acc[...] = jnp.zeros_like(acc)

        @pl.when(k < N_ABC_TILES)
        def _():
            acc[...] += _nn(da_ref[...], w_ref[...])

        @pl.when(k >= N_ABC_TILES)
        def _():
            acc[...] += _nn(dm_ref[...], w_ref[...])

        @pl.when(k == nk - 1)
        def _():
            x = x_ref[...]
            r = lax.rsqrt(jnp.mean(x * x, axis=-1, keepdims=True) + RMS_EPS)
            nh = x * r
            dh = acc[...]
            dg_ref[0:1, :] += jnp.sum(dh * nh, axis=0, keepdims=True)
            dn = dh * g_ref[...]
            dx_ref[...] = do_ref[...] + r * (dn - nh * jnp.mean(dn * nh, axis=-1, keepdims=True))

    return pl.pallas_call(
        body, name="proj_dgrad",
        grid=(t // tm, nk),
        in_specs=[pl.BlockSpec((tm, COL_TILE), lambda i, k: (i, jnp.minimum(k, N_ABC_TILES - 1))),
                  pl.BlockSpec((tm, COL_TILE), lambda i, k: (i, jnp.maximum(k - N_ABC_TILES, 0))),
                  pl.BlockSpec((COL_TILE, D_MODEL), lambda i, k: (k, 0)),
                  pl.BlockSpec((tm, D_MODEL), lambda i, k: (i, 0)),
                  pl.BlockSpec((1, D_MODEL), lambda i, k: (0, 0)),
                  pl.BlockSpec((tm, D_MODEL), lambda i, k: (i, 0))],
        out_specs=[pl.BlockSpec((tm, D_MODEL), lambda i, k: (i, 0)),
                   pl.BlockSpec((8, D_MODEL), lambda i, k: (0, 0))],
        out_shape=[jax.ShapeDtypeStruct((t, D_MODEL), F32),
                   jax.ShapeDtypeStruct((8, D_MODEL), F32)],
        scratch_shapes=[pltpu.VMEM((tm, D_MODEL), F32)],
        compiler_params=_cparams("arbitrary", "arbitrary"),
    )(d_abc, d_m, w, x2d, g, d_out)


def proj_wgrad(hb, d_abc, d_m):
    t = hb.shape[0]
    nj = N_ABC_TILES + N_M_TILES

    def body(h_ref, da_ref, dm_ref, o_ref):
        j = pl.program_id(0)

        @pl.when(j < N_ABC_TILES)
        def _():
            o_ref[...] = _tn(da_ref[...], h_ref[...])

        @pl.when(j >= N_ABC_TILES)
        def _():
            o_ref[...] = _tn(dm_ref[...], h_ref[...])

    return pl.pallas_call(
        body, name="proj_wgrad",
        grid=(nj,),
        in_specs=[pl.BlockSpec((t, D_MODEL), lambda j: (0, 0)),
                  pl.BlockSpec((t, COL_TILE), lambda j: (0, jnp.minimum(j, N_ABC_TILES - 1))),
                  pl.BlockSpec((t, COL_TILE), lambda j: (0, jnp.maximum(j - N_ABC_TILES, 0)))],
        out_specs=pl.BlockSpec((COL_TILE, D_MODEL), lambda j: (j, 0)),
        out_shape=jax.ShapeDtypeStruct((PROJ_WIDTH, D_MODEL), F32),
        compiler_params=_cparams("arbitrary"),
    )(hb, d_abc, d_m)


def _adamw_math(w, g, m, v):
    m = ADAM_B1 * m + (1.0 - ADAM_B1) * g
    v = ADAM_B2 * v + (1.0 - ADAM_B2) * jnp.square(g)
    m_hat = m / (1.0 - ADAM_B1 ** ADAM_STEP)
    v_hat = v / (1.0 - ADAM_B2 ** ADAM_STEP)
    delta = -ADAM_LR * (m_hat / (jnp.sqrt(v_hat) + ADAM_EPS) + ADAM_WD * w)
    return delta, m, v


def adamw_layer(w, g, m, v, layer, acc):
    _, r, c = w.shape
    tr = _row_tile(r)

    def body(*refs):
        w_ref, g_ref, m_ref, v_ref = refs[:4]
        go_ref, d_ref, mo_ref, vo_ref = refs[-4:]
        g_val = g_ref[...]
        d, m_new, v_new = _adamw_math(w_ref[...], g_val, m_ref[...], v_ref[...])
        go_ref[...] = g_val
        d_ref[...] = d
        mo_ref[...] = m_new
        vo_ref[...] = v_new

    slab = pl.BlockSpec((None, tr, c), lambda i: (layer, i, 0))
    ins = [w, g, m, v]
    in_specs = [slab, pl.BlockSpec((tr, c), lambda i: (i, 0)), slab, slab]
    aliases = {}
    if acc is not None:
        ins += list(acc)
        in_specs += [ANY] * 4
        aliases = {4 + i: i for i in range(4)}
    return pl.pallas_call(
        body, name="adamw_layer",
        grid=(r // tr,),
        in_specs=in_specs, out_specs=[slab] * 4,
        out_shape=[jax.ShapeDtypeStruct(w.shape, F32)] * 4,
        input_output_aliases=aliases,
        compiler_params=_cparams("parallel"),
    )(*ins)


def adamw_small(params):
    k = len(params)

    def body(*refs):
        ins, outs = refs[:4 * k], refs[4 * k:]
        for i in range(k):
            d, m_new, v_new = _adamw_math(*[r[...] for r in ins[4 * i:4 * i + 4]])
            outs[3 * i][...] = d
            outs[3 * i + 1][...] = m_new
            outs[3 * i + 2][...] = v_new

    flat = [a for p in params for a in p]
    vm = pl.BlockSpec(memory_space=pltpu.VMEM)
    out_shape = [jax.ShapeDtypeStruct(p[0].shape, F32) for p in params for _ in range(3)]
    res = pl.pallas_call(
        body, name="adamw_small",
        in_specs=[vm] * len(flat), out_specs=[vm] * len(out_shape), out_shape=out_shape,
    )(*flat)
    return [tuple(res[3 * i:3 * i + 3]) for i in range(k)]


ANY = pl.BlockSpec(memory_space=pl.ANY)


def _place():
    return lax.axis_index("x"), lax.axis_index("y"), lax.axis_index("c")


HBM = pl.BlockSpec(memory_space=pltpu.HBM)
SEM = pl.BlockSpec(memory_space=pltpu.SEMAPHORE)
EFFECT = pltpu.SideEffectType.DATAFLOW_SIDE_EFFECTING


def _other_chips(x, y):
    return [(1 - x, y), (x, 1 - y), (1 - x, 1 - y)]


def _own_slot(shard, chip):
    buf = lax.empty((N_CHIPS,) + shard.shape, shard.dtype)
    return lax.dynamic_update_slice(buf, shard[None], (chip, 0, 0, 0))


def gather_weight_shards(bufs, after):
    n = len(bufs)

    def body(*refs):
        g_refs = refs[n + 1:2 * n + 1]
        send_sems, recv_sems = refs[2 * n + 1:]
        x, y, c = _place()
        me_p = 2 * x + y
        sibling = (x, y, 1 - c)
        chips = _other_chips(x, y)

        def copy(k, slab, to):
            return pltpu.make_async_remote_copy(src_ref=slab, dst_ref=slab, send_sem=send_sems.at[k],
                                                recv_sem=recv_sems.at[k], device_id=to, device_id_type=MESH)

        first, passed = [], []
        for t in range(n):
            for k, (qx, qy) in enumerate(chips):
                cp = copy(6 * t + k, g_refs[t].at[me_p, c], (qx, qy, c))
                cp.start()
                first.append(cp)
        for t in range(n):
            for k, (qx, qy) in enumerate(chips):
                slab = g_refs[t].at[2 * qx + qy, c]
                copy(6 * t + k, slab, (qx, qy, c)).wait_recv()
                fwd = copy(6 * t + 3 + k, slab, sibling)
                fwd.start()
                passed.append(fwd)
        for t in range(n):
            for k, (qx, qy) in enumerate(chips):
                copy(6 * t + 3 + k, g_refs[t].at[2 * qx + qy, 1 - c], sibling).wait_recv()
        for cp in first + passed:
            cp.wait_send()

    return pl.pallas_call(
        body, name="gather_weight_shards",
        in_specs=[ANY] * (n + 1), out_specs=[ANY] * n,
        out_shape=[jax.ShapeDtypeStruct(b.shape, b.dtype) for b in bufs],
        input_output_aliases={t: t for t in range(n)},
        scratch_shapes=[pltpu.SemaphoreType.DMA((6 * n,)), pltpu.SemaphoreType.DMA((6 * n,))],
    )(*bufs, after)


def _hbm(a):
    return pltpu.with_memory_space_constraint(a, pltpu.HBM)


def gather_start(bufs, after):
    n = len(bufs)

    def body(*refs):
        g_refs = refs[:n]
        send_sems, recv_sems = refs[n + 1], refs[n + 2]
        token = refs[-1]
        x, y, c = _place()
        me_p = 2 * x + y
        for t in range(n):
            for k, (qx, qy) in enumerate(_other_chips(x, y)):
                slab = g_refs[t].at[me_p, c]
                pltpu.make_async_remote_copy(src_ref=slab, dst_ref=slab, send_sem=send_sems.at[3 * t + k],
                                             recv_sem=recv_sems.at[3 * t + k], device_id=(qx, qy, c),
                                             device_id_type=MESH).start()
        token[...] = jnp.zeros_like(token)

    res = pl.pallas_call(
        body, name="gather_start",
        in_specs=[HBM] * n + [ANY],
        out_specs=[SEM, SEM] + [HBM] * n + [pl.BlockSpec(memory_space=pltpu.VMEM)],
        out_shape=[pltpu.SemaphoreType.DMA((3 * n,)), pltpu.SemaphoreType.DMA((3 * n,))]
        + [pltpu.HBM(b.shape, b.dtype) for b in bufs] + [jax.ShapeDtypeStruct((8, LANES), F32)],
        input_output_aliases={t: 2 + t for t in range(n)},
        compiler_params=pltpu.CompilerParams(has_side_effects=EFFECT),
    )(*[_hbm(b) for b in bufs], after)
    return res[0], res[1], list(res[2:2 + n]), res[-1]


def gather_wait(bufs, send_sems, recv_sems, after):
    n = len(bufs)

    def body(*refs):
        g_refs = refs[:n]
        send_sems, recv_sems = refs[n], refs[n + 1]
        x, y, c = _place()
        me_p = 2 * x + y
        for t in range(n):
            for k, (qx, qy) in enumerate(_other_chips(x, y)):
                cp = pltpu.make_async_remote_copy(src_ref=g_refs[t].at[me_p, c], dst_ref=g_refs[t].at[2 * qx + qy, c],
                                                  send_sem=send_sems.at[3 * t + k], recv_sem=recv_sems.at[3 * t + k],
                                                  device_id=(qx, qy, c), device_id_type=MESH)
                cp.wait_send()
                cp.wait_recv()

    return pl.pallas_call(
        body, name="gather_wait",
        in_specs=[HBM] * n + [SEM, SEM, ANY],
        out_specs=[HBM] * n,
        out_shape=[pltpu.HBM(b.shape, b.dtype) for b in bufs],
        input_output_aliases={t: t for t in range(n)},
        compiler_params=pltpu.CompilerParams(has_side_effects=EFFECT),
    )(*bufs, send_sems, recv_sems, after)


def gather_forward(bufs):
    n = len(bufs)

    def body(*refs):
        g_refs = refs[n:2 * n]
        send_sems, recv_sems = refs[2 * n:]
        x, y, c = _place()
        sibling = (x, y, 1 - c)
        chips = _other_chips(x, y)
        passed = []
        for t in range(n):
            for k, (qx, qy) in enumerate(chips):
                slab = g_refs[t].at[2 * qx + qy, c]
                fwd = pltpu.make_async_remote_copy(src_ref=slab, dst_ref=slab, send_sem=send_sems.at[3 * t + k],
                                                   recv_sem=recv_sems.at[3 * t + k], device_id=sibling,
                                                   device_id_type=MESH)
                fwd.start()
                passed.append(fwd)
        for t in range(n):
            for k, (qx, qy) in enumerate(chips):
                slab = g_refs[t].at[2 * qx + qy, 1 - c]
                pltpu.make_async_remote_copy(src_ref=slab, dst_ref=slab, send_sem=send_sems.at[3 * t + k],
                                             recv_sem=recv_sems.at[3 * t + k], device_id=sibling,
                                             device_id_type=MESH).wait_recv()
        for cp in passed:
            cp.wait_send()

    return pl.pallas_call(
        body, name="gather_forward",
        in_specs=[ANY] * n, out_specs=[ANY] * n,
        out_shape=[jax.ShapeDtypeStruct(b.shape, b.dtype) for b in bufs],
        input_output_aliases={t: t for t in range(n)},
        scratch_shapes=[pltpu.SemaphoreType.DMA((3 * n,)), pltpu.SemaphoreType.DMA((3 * n,))],
    )(*bufs)


def exchange_small(pack, after):
    def body(p_ref, after_ref, o_ref, send_sems, recv_sems, local_sem):
        x, y, c = _place()
        me = 4 * x + 2 * y + c
        mine = pltpu.make_async_copy(p_ref, o_ref.at[me], local_sem)
        mine.start()
        sends = []
        for k in range(1, 8):
            fx, fy, fc = (k >> 2) & 1, (k >> 1) & 1, k & 1
            peer = (x ^ fx, y ^ fy, c ^ fc)
            cp = pltpu.make_async_remote_copy(src_ref=p_ref, dst_ref=o_ref.at[me], send_sem=send_sems.at[k - 1],
                                              recv_sem=recv_sems.at[k - 1], device_id=peer, device_id_type=MESH)
            cp.start()
            sends.append(cp)
        for k in range(1, 8):
            fx, fy, fc = (k >> 2) & 1, (k >> 1) & 1, k & 1
            peer = (x ^ fx, y ^ fy, c ^ fc)
            slot = o_ref.at[4 * peer[0] + 2 * peer[1] + peer[2]]
            pltpu.make_async_remote_copy(src_ref=slot, dst_ref=slot, send_sem=send_sems.at[k - 1],
                                         recv_sem=recv_sems.at[k - 1], device_id=peer, device_id_type=MESH).wait_recv()
        for cp in sends:
            cp.wait_send()
        mine.wait()

    return pl.pallas_call(
        body, name="exchange_small",
        in_specs=[ANY, ANY], out_specs=ANY,
        out_shape=jax.ShapeDtypeStruct((8,) + pack.shape, pack.dtype),
        scratch_shapes=[pltpu.SemaphoreType.DMA((7,)), pltpu.SemaphoreType.DMA((7,)), pltpu.SemaphoreType.DMA],
    )(pack, after)


def sibling_swap_halves(grads):
    n = len(grads)

    def body(*refs):
        g_refs, o_refs = refs[:n], refs[n:2 * n]
        send_sems, recv_sems = refs[2 * n:]
        x, y, c = _place()
        cps = []
        for t in range(n):
            for p in range(N_CHIPS):
                cp = pltpu.make_async_remote_copy(src_ref=g_refs[t].at[p, 1 - c], dst_ref=o_refs[t].at[p],
                                                  send_sem=send_sems.at[N_CHIPS * t + p],
                                                  recv_sem=recv_sems.at[N_CHIPS * t + p],
                                                  device_id=(x, y, 1 - c), device_id_type=MESH)
                cp.start()
                cps.append(cp)
        for cp in cps:
            cp.wait()

    return pl.pallas_call(
        body, name="sibling_swap_halves",
        in_specs=[ANY] * n, out_specs=[ANY] * n,
        out_shape=[jax.ShapeDtypeStruct((N_CHIPS,) + g.shape[2:], g.dtype) for g in grads],
        scratch_shapes=[pltpu.SemaphoreType.DMA((N_CHIPS * n,)), pltpu.SemaphoreType.DMA((N_CHIPS * n,))],
    )(*grads)


def _row_tile(r):
    return max(t for t in range(16, 513, 16) if r % t == 0)


def add_own_half(g, other, c_arr):
    _, _, r, cols = g.shape
    tr = _row_tile(r)

    def body(c_ref, a_ref, b_ref, o_ref):
        o_ref[...] = (a_ref[...] + b_ref[...]).astype(BF16)

    return pl.pallas_call(
        body, name="add_own_half",
        grid_spec=pltpu.PrefetchScalarGridSpec(
            num_scalar_prefetch=1, grid=(N_CHIPS, r // tr),
            in_specs=[pl.BlockSpec((None, None, tr, cols), lambda p, i, c_ref: (p, c_ref[0], i, 0)),
                      pl.BlockSpec((None, tr, cols), lambda p, i, c_ref: (p, i, 0))],
            out_specs=pl.BlockSpec((None, tr, cols), lambda p, i, c_ref: (p, i, 0))),
        out_shape=jax.ShapeDtypeStruct((N_CHIPS, r, cols), BF16),
        compiler_params=_cparams("parallel", "parallel"),
    )(c_arr, g, other)


def scatter_start(partials):
    n = len(partials)

    def body(*refs):
        s_refs, l_refs = refs[:n], refs[n:2 * n]
        send_sems, recv_sems = refs[2 * n], refs[2 * n + 1]
        token = refs[-1]
        x, y, c = _place()
        for t in range(n):
            for k, (qx, qy) in enumerate(_other_chips(x, y)):
                pltpu.make_async_remote_copy(src_ref=s_refs[t].at[2 * qx + qy], dst_ref=l_refs[t].at[k],
                                             send_sem=send_sems.at[3 * t + k], recv_sem=recv_sems.at[3 * t + k],
                                             device_id=(qx, qy, c), device_id_type=MESH).start()
        token[...] = jnp.zeros_like(token)

    lands = [lax.empty((3,) + s.shape[1:], s.dtype) for s in partials]
    res = pl.pallas_call(
        body, name="scatter_start",
        in_specs=[HBM] * (2 * n),
        out_specs=[SEM, SEM] + [HBM] * (2 * n) + [pl.BlockSpec(memory_space=pltpu.VMEM)],
        out_shape=[pltpu.SemaphoreType.DMA((3 * n,)), pltpu.SemaphoreType.DMA((3 * n,))]
        + [pltpu.HBM(a.shape, a.dtype) for a in partials + lands] + [jax.ShapeDtypeStruct((8, LANES), F32)],
        input_output_aliases={t: 2 + t for t in range(2 * n)},
        compiler_params=pltpu.CompilerParams(has_side_effects=EFFECT),
    )(*[_hbm(a) for a in partials + lands])
    return res[0], res[1], list(res[2:2 + n]), list(res[2 + n:2 + 2 * n]), res[-1]


def scatter_wait(partials, lands, send_sems, recv_sems, after):
    n = len(partials)

    def body(*refs):
        s_refs, l_refs = refs[:n], refs[n:2 * n]
        send_sems, recv_sems = refs[2 * n], refs[2 * n + 1]
        x, y, c = _place()
        for t in range(n):
            for k, (qx, qy) in enumerate(_other_chips(x, y)):
                cp = pltpu.make_async_remote_copy(src_ref=s_refs[t].at[2 * qx + qy], dst_ref=l_refs[t].at[k],
                                                  send_sem=send_sems.at[3 * t + k], recv_sem=recv_sems.at[3 * t + k],
                                                  device_id=(qx, qy, c), device_id_type=MESH)
                cp.wait_send()
                cp.wait_recv()

    res = pl.pallas_call(
        body, name="scatter_wait",
        in_specs=[HBM] * (2 * n) + [SEM, SEM, ANY],
        out_specs=[HBM] * (2 * n),
        out_shape=[pltpu.HBM(a.shape, a.dtype) for a in partials + lands],
        input_output_aliases={t: t for t in range(2 * n)},
        compiler_params=pltpu.CompilerParams(has_side_effects=EFFECT),
    )(*partials, *lands, send_sems, recv_sems, after)
    return list(res[:n]), list(res[n:])


def sum_chips(own, parts, where):
    _, r, cols = own.shape
    tr = _row_tile(r)

    def body(w_ref, a_ref, p_ref, o_ref):
        acc = a_ref[...].astype(F32)
        for k in range(3):
            acc = acc + p_ref[k].astype(F32)
        o_ref[...] = acc

    return pl.pallas_call(
        body, name="sum_chips",
        grid_spec=pltpu.PrefetchScalarGridSpec(
            num_scalar_prefetch=1, grid=(r // tr,),
            in_specs=[pl.BlockSpec((None, tr, cols), lambda i, w_ref: (w_ref[0], i, 0)),
                      pl.BlockSpec((3, tr, cols), lambda i, w_ref: (0, i, 0))],
            out_specs=pl.BlockSpec((None, tr, cols), lambda i, w_ref: (w_ref[1], i, 0))),
        out_shape=jax.ShapeDtypeStruct((DEPTH, r, cols), F32),
        compiler_params=_cparams("parallel"),
    )(where, own, parts)


def sibling_share_layer(bufs):
    n = len(bufs)

    def body(*refs):
        o_refs = refs[n:2 * n]
        send_sems, recv_sems = refs[2 * n:]
        x, y, c = _place()
        cps = []
        for t in range(n):
            cp = pltpu.make_async_remote_copy(src_ref=o_refs[t].at[c], dst_ref=o_refs[t].at[c], send_sem=send_sems.at[t],
                                              recv_sem=recv_sems.at[t], device_id=(x, y, 1 - c), device_id_type=MESH)
            cp.start()
            cps.append(cp)
        for t in range(n):
            slot = o_refs[t].at[1 - c]
            pltpu.make_async_remote_copy(src_ref=slot, dst_ref=slot, send_sem=send_sems.at[t], recv_sem=recv_sems.at[t],
                                         device_id=(x, y, 1 - c), device_id_type=MESH).wait_recv()
        for cp in cps:
            cp.wait_send()

    return pl.pallas_call(
        body, name="sibling_share_layer",
        in_specs=[ANY] * n, out_specs=[ANY] * n,
        out_shape=[jax.ShapeDtypeStruct(b.shape, b.dtype) for b in bufs],
        input_output_aliases={t: t for t in range(n)},
        scratch_shapes=[pltpu.SemaphoreType.DMA((n,)), pltpu.SemaphoreType.DMA((n,))],
    )(*bufs)


SP_META = 2 * (N_META * D_MODEL // LANES)
SP_NORM = DEPTH * D_MODEL // LANES
SP_RB = DEPTH * N_BUCKETS
SP_SINK = DEPTH * ATT_HEADS
SP_CONV = DEPTH * 3 * BRANCH_WIDTH // LANES
SP_LOSS = 8
SP_ROWS = SP_META + 2 * SP_NORM + SP_RB + SP_SINK + SP_CONV + SP_LOSS


def sum_small(slots):
    half = SP_META // 2
    rb0 = SP_META + 2 * SP_NORM
    rest_rows = SP_ROWS - SP_META

    def body(s_ref, meta_ref, rest_ref):
        acc = s_ref[0]
        for d in range(1, 8):
            acc = acc + s_ref[d]
        meta_ref[...] = acc[0:half] + acc[half:SP_META]
        rest_ref[...] = acc[SP_META:]
        rest_ref[rb0 - SP_META:rb0 - SP_META + N_BUCKETS, :] = (
            acc[rb0:rb0 + N_BUCKETS] + acc[rb0 + N_BUCKETS:rb0 + 2 * N_BUCKETS])

    vm = pl.BlockSpec(memory_space=pltpu.VMEM)
    return pl.pallas_call(
        body, name="sum_small",
        in_specs=[vm], out_specs=[vm, vm],
        out_shape=[jax.ShapeDtypeStruct((half, LANES), F32), jax.ShapeDtypeStruct((rest_rows, LANES), F32)],
    )(slots)


def local_step(x, loss_target, meta_full, rel_bias, norm_pre, conv_w_full, attn_sinks, norm_post, weights_of, grads_done):
    nb, seq, _ = x.shape
    nc = seq // BLOCK + 1
    lp = nc * BLOCK
    rows = nb * lp
    pad = jnp.zeros((nb, PAD_FRONT, D_MODEL), F32)
    meta = jnp.broadcast_to(meta_full[None], (nb, N_META, D_MODEL))
    h0 = jnp.concatenate([pad, meta, x], axis=1).reshape(rows, D_MODEL)
    cosf, sinf = _rot_tables(lp)
    bkt = jnp.asarray(_bucket_table())

    acts = []
    h = h0
    for l in range(DEPTH):
        (w_in, w_br, w_out), zero = weights_of(l, h)
        hb, p_abc = norm_matmul(h, norm_pre[l][None] + zero, w_in, 0, N_ABC_TILES)
        p_m = matmul_cols(hb, w_in, N_ABC_TILES, N_M_TILES)
        br, states = mixers_fwd(p_abc, cosf, sinf, bkt, rel_bias, attn_sinks[l][None], conv_w_full[l], nb, nc)
        h_next = merge_fwd(h, br, p_m, w_br, w_out, norm_post[l][None])
        acts.append((h, hb, p_abc, p_m, br, states, w_in, w_br, w_out))
        h = h_next

    loss_part, d_h = loss_head(h, loss_target.reshape(nb * seq, D_MODEL), nb, nc)

    small = [None] * DEPTH
    for l in reversed(range(DEPTH)):
        h_in, hb, p_abc, p_m, br, states, w_in, w_br, w_out = acts[l]
        d_br, d_m, d_gpost, g_wbr, g_wout = merge_bwd(d_h, br, p_m, w_br, w_out, norm_post[l][None])
        d_abc, d_rb, d_sk, d_cw = mixers_bwd(p_abc, d_br, states, cosf, sinf, bkt, rel_bias,
                                             attn_sinks[l][None], conv_w_full[l], nb, nc)
        g_win = proj_wgrad(hb, d_abc, d_m)
        zero = grads_done(l, [g_win, g_wbr, g_wout])
        d_h, d_gpre = proj_dgrad(d_abc, d_m, w_in, h_in, norm_pre[l][None] + zero, d_h)
        small[l] = (d_gpre[0], d_gpost[0], d_rb, d_sk, d_cw[0:3])

    d_h3 = d_h.reshape(nb, lp, D_MODEL)
    d_x = d_h3[:, BLOCK:]
    d_meta = d_h3[:, PAD_FRONT:BLOCK]
    sp = jnp.concatenate([
        d_meta.reshape(-1, LANES),
        jnp.stack([small[l][0] for l in range(DEPTH)]).reshape(-1, LANES),
        jnp.stack([small[l][1] for l in range(DEPTH)]).reshape(-1, LANES),
        jnp.concatenate([small[l][2] for l in range(DEPTH)], axis=0),
        jnp.concatenate([small[l][3] for l in range(DEPTH)], axis=0),
        jnp.stack([small[l][4] for l in range(DEPTH)]).reshape(-1, LANES),
        loss_part], axis=0)
    return d_x, sp


def kernel(x, meta_tokens, rel_bias, norm_pre, w_in, conv_w, attn_sinks, w_branch, w_out, norm_post, loss_target, m_meta_tokens, m_rel_bias, m_norm_pre, m_w_in, m_conv_w, m_attn_sinks, m_w_branch, m_w_out, m_norm_post, v_meta_tokens, v_rel_bias, v_norm_pre, v_w_in, v_conv_w, v_attn_sinks, v_w_branch, v_w_out, v_norm_post):
    assert x.shape[0] == 2 and SP_META == 2 * N_META * D_MODEL // LANES
    px, py, pc = _place()
    chip = 2 * px + py

    c_arr = jnp.reshape(pc, (1,)).astype(jnp.int32)
    where = jnp.stack([chip, pc]).astype(jnp.int32)
    tr_ = lambda a: jnp.swapaxes(a, 1, 2)
    w3 = [tr_(w_in), w_branch.reshape(DEPTH, N_BRANCH * BRANCH_WIDTH, SHARD_D), w_out]
    halves = lambda a: a.reshape(2, a.shape[0] // 2, a.shape[1])

    def as_weights(bufs):
        a_in, a_br, a_out = bufs
        return (a_in.reshape(PROJ_WIDTH, D_MODEL), a_br.reshape(N_CHIPS, N_BRANCH, BRANCH_WIDTH, SHARD_D),
                a_out.reshape(D_MODEL, D_MODEL))

    side = jnp.concatenate([meta_tokens.reshape(-1), conv_w.reshape(-1)]).reshape(-1, LANES)
    side = jnp.concatenate([side, jnp.zeros((40 - side.shape[0], LANES), F32)], axis=0)
    side_all = exchange_small(side, side)
    side_chips = side_all[0::2]
    n_meta_rows = N_META * SHARD_D // LANES
    meta_full = jnp.moveaxis(side_chips[:, :n_meta_rows].reshape(N_CHIPS, N_META, SHARD_D), 0, 1).reshape(N_META, D_MODEL)
    conv_full = jnp.moveaxis(side_chips[:, n_meta_rows:n_meta_rows + 6].reshape(N_CHIPS, DEPTH, 3, LANES), 0, 2).reshape(DEPTH, 3, BRANCH_WIDTH)

    slots = [[_own_slot(halves(w[l].astype(BF16)), chip) for w in w3] for l in range(DEPTH)]
    gathered0 = gather_weight_shards(slots[0], side_all)
    send1, recv1, flying1, started1 = gather_start(slots[1], gathered0[0])

    def weights_of(l, h):
        if l == 0:
            return as_weights(gathered0), started1[0:1, 0:1]
        landed = gather_forward(gather_wait(flying1, send1, recv1, h))
        return as_weights(landed), jnp.zeros((1, 1), F32)

    reduced = [None] * DEPTH
    flying = {}

    def finish_reduce(l, after):
        partials, parts = scatter_wait(*flying[l], after)
        reduced[l] = sibling_share_layer([sum_chips(a, p, where) for a, p in zip(partials, parts)])

    def grads_done(l, grads):
        if l == 0:
            finish_reduce(1, grads[0])
        full = [g.reshape(N_CHIPS, 2, g.size // (2 * N_CHIPS * g.shape[-1]), g.shape[-1]) for g in grads]
        others = sibling_swap_halves(full)
        send, recv, thru, lands, started = scatter_start([add_own_half(g, o, c_arr) for g, o in zip(full, others)])
        flying[l] = (thru, lands, send, recv)
        return started[0:1, 0:1]

    d_x, sp = local_step(x, loss_target, meta_full, rel_bias, norm_pre, conv_full, attn_sinks, norm_post,
                         weights_of, grads_done)
    finish_reduce(0, sp)

    meta_rows, rest = sum_small(exchange_small(sp, reduced[0][0]))
    o = 0
    g_meta_full = meta_rows.reshape(N_META, D_MODEL)
    g_norm_pre = rest[o:o + SP_NORM].reshape(DEPTH, D_MODEL); o += SP_NORM
    g_norm_post = rest[o:o + SP_NORM].reshape(DEPTH, D_MODEL); o += SP_NORM
    g_rel_bias = rest[o:o + N_BUCKETS, :ATT_HEADS]; o += SP_RB
    g_sinks = rest[o:o + SP_SINK, 0].reshape(DEPTH, ATT_HEADS); o += SP_SINK
    g_conv_full = rest[o:o + SP_CONV].reshape(DEPTH, 3, BRANCH_WIDTH); o += SP_CONV
    loss = rest[o, 0]
    g_meta = lax.dynamic_slice_in_dim(g_meta_full, chip * SHARD_D, SHARD_D, axis=1)
    g_conv = lax.dynamic_slice_in_dim(g_conv_full, chip * LANES, LANES, axis=2)

    m3 = [tr_(m_w_in), m_w_branch.reshape(w3[1].shape), m_w_out]
    v3 = [tr_(v_w_in), v_w_branch.reshape(w3[1].shape), v_w_out]
    big = []
    for t in range(3):
        acc = None
        for l in reversed(range(DEPTH)):
            acc = adamw_layer(w3[t], reduced[l][t].reshape(w3[t].shape[1:]), m3[t], v3[t], l, acc)
        big.append(acc)
    g_in, *u_in = [tr_(a) for a in big[0]]
    g_br, *u_br = [a.reshape(w_branch.shape) for a in big[1]]
    g_out, *u_out = big[2]
    to2 = lambda a: a.reshape(-1, a.shape[-1])
    smalls = [(meta_tokens, g_meta, m_meta_tokens, v_meta_tokens),
              (rel_bias, g_rel_bias, m_rel_bias, v_rel_bias),
              (norm_pre, g_norm_pre, m_norm_pre, v_norm_pre),
              (to2(conv_w), to2(g_conv), to2(m_conv_w), to2(v_conv_w)),
              (attn_sinks, g_sinks, m_attn_sinks, v_attn_sinks),
              (norm_post, g_norm_post, m_norm_post, v_norm_post)]
    u_meta, u_rb, u_npre, u_conv, u_sink, u_npost = adamw_small(smalls)
    u_conv = tuple(a.reshape(conv_w.shape) for a in u_conv)

    grads = [g_meta, g_rel_bias, g_norm_pre, g_in, g_conv, g_sinks, g_br, g_out, g_norm_post]
    upd = [u_meta, u_rb, u_npre, u_in, u_conv, u_sink, u_br, u_out, u_npost]
    return (loss, d_x, *grads, *[u[0] for u in upd], *[u[1] for u in upd], *[u[2] for u in upd])
```

```python
import functools
import math

import numpy as np
import jax
import jax.numpy as jnp
from jax import lax
from jax.experimental import pallas as pl
from jax.experimental.pallas import tpu as pltpu

F32 = jnp.float32
BF16 = jnp.bfloat16
MESH = pl.DeviceIdType.MESH

D_MODEL = 1024
DEPTH = 2
N_META = 16
BLOCK = 128
PAD_FRONT = BLOCK - N_META
ATT_HEADS = 8
ATT_HEAD_DIM = 64
N_BUCKETS = 32
MAX_EXACT = 16
MAX_DISTANCE = 128
RET_HEADS = 4
ROT_BASE = 10000.0
N_BRANCH = 3
BRANCH_WIDTH = 512
PROJ_WIDTH = 8448
ABC_WIDTH = 5376
MERGE_WIDTH = N_BRANCH * D_MODEL
RMS_EPS = 1e-6
GN_EPS = 1e-6
NEG_INF = -1e30
ATT_SCALE = ATT_HEAD_DIM ** -0.5
RET_SCALE = BLOCK ** -0.5
LOG_GAMMA = tuple(math.log1p(-(2.0 ** (-5.0 - h))) for h in range(RET_HEADS))

C_AQ, C_AK, C_AV, C_AG = 0, 512, 640, 768
C_RQ, C_RK, C_RV, C_RG = 1280, 1792, 2304, 2816
C_CB, C_CC, C_CX, C_CG = 3328, 3840, 4352, 4864

ADAM_LR = 0.001
ADAM_B1 = 0.9
ADAM_B2 = 0.999
ADAM_EPS = 1e-08
ADAM_WD = 0.01
ADAM_STEP = 10

N_CHIPS = 4
SHARD_IN = PROJ_WIDTH // N_CHIPS
SHARD_D = D_MODEL // N_CHIPS
LANES = 128
PACK_IN = D_MODEL * SHARD_IN
PACK_BR = N_BRANCH * BRANCH_WIDTH * SHARD_D
PACK_OUT = SHARD_D * D_MODEL
PACK_ROWS = (PACK_IN + PACK_BR + PACK_OUT) // LANES

VMEM_LIMIT = 56 * 1024 * 1024
COL_TILE = 768
ROW_TILE = 1088


def _cparams(*sem):
    return pltpu.CompilerParams(dimension_semantics=sem, vmem_limit_bytes=VMEM_LIMIT)


def _nt(a, b):
    return lax.dot_general(a, b, (((1,), (1,)), ((), ())), preferred_element_type=F32)


def _tn(a, b):
    return lax.dot_general(a, b, (((0,), (0,)), ((), ())), preferred_element_type=F32)


def _nn(a, b):
    return jnp.dot(a, b, preferred_element_type=F32)


def _sigmoid(x):
    return 0.5 * jnp.tanh(0.5 * x) + 0.5


def _silu(x):
    return x * _sigmoid(x)


def _dsilu(x):
    s = _sigmoid(x)
    return s * (1.0 + x * (1.0 - s))


def _bucket_table():
    r = np.arange(BLOCK)[:, None]
    c = np.arange(2 * BLOCK)[None, :]
    n = np.maximum(BLOCK + r - c, 0)
    nf = np.maximum(n, 1).astype(np.float32)
    large = MAX_EXACT + (np.log(nf / MAX_EXACT) / math.log(MAX_DISTANCE / MAX_EXACT)
                         * (N_BUCKETS - MAX_EXACT)).astype(np.int32)
    large = np.minimum(large, N_BUCKETS - 1)
    return np.where(n < MAX_EXACT, n, large).astype(np.int32)


def _rot_tables(lp):
    half = BLOCK // 2
    pos = (jnp.arange(lp) - PAD_FRONT).astype(F32)
    theta = 1.0 / (ROT_BASE ** jnp.linspace(0.0, 1.0, half, dtype=F32))
    ang = pos[:, None] * theta[None, :]
    cos, sin = jnp.cos(ang), jnp.sin(ang)
    return jnp.concatenate([cos, cos], axis=1), jnp.concatenate([-sin, sin], axis=1)


def norm_matmul(x2d, g, w, col0_blocks, n_col_blocks):
    t = x2d.shape[0]
    tm = ROW_TILE if t % ROW_TILE == 0 else BLOCK

    def body(x_ref, g_ref, w_ref, hb_ref, o_ref):
        @pl.when(pl.program_id(1) == 0)
        def _():
            x = x_ref[...]
            r = lax.rsqrt(jnp.mean(x * x, axis=-1, keepdims=True) + RMS_EPS)
            hb_ref[...] = (x * r * g_ref[...]).astype(BF16)

        o_ref[...] = _nt(hb_ref[...], w_ref[...]).astype(BF16)

    return pl.pallas_call(
        body, name="norm_matmul",
        grid=(t // tm, n_col_blocks),
        in_specs=[pl.BlockSpec((tm, D_MODEL), lambda i, j: (i, 0)),
                  pl.BlockSpec((1, D_MODEL), lambda i, j: (0, 0)),
                  pl.BlockSpec((COL_TILE, D_MODEL), lambda i, j: (j + col0_blocks, 0))],
        out_specs=[pl.BlockSpec((tm, D_MODEL), lambda i, j: (i, 0)),
                   pl.BlockSpec((tm, COL_TILE), lambda i, j: (i, j))],
        out_shape=[jax.ShapeDtypeStruct((t, D_MODEL), BF16),
                   jax.ShapeDtypeStruct((t, n_col_blocks * COL_TILE), BF16)],
        compiler_params=_cparams("parallel", "arbitrary"),
    )(x2d, g, w)


def matmul_cols(a, w, col0_blocks, n_col_blocks):
    t, k = a.shape
    tm = ROW_TILE if t % ROW_TILE == 0 else BLOCK

    def body(a_ref, w_ref, o_ref):
        o_ref[...] = _nt(a_ref[...], w_ref[...]).astype(BF16)

    return pl.pallas_call(
        body, name="matmul_cols",
        grid=(t // tm, n_col_blocks),
        in_specs=[pl.BlockSpec((tm, k), lambda i, j: (i, 0)),
                  pl.BlockSpec((COL_TILE, k), lambda i, j: (j + col0_blocks, 0))],
        out_specs=pl.BlockSpec((tm, COL_TILE), lambda i, j: (i, j)),
        out_shape=jax.ShapeDtypeStruct((t, n_col_blocks * COL_TILE), BF16),
        compiler_params=_cparams("parallel", "arbitrary"),
    )(a, w)


class _Widened:
    def __init__(self, ref):
        self.ref = ref

    def __getitem__(self, idx):
        return self.ref[idx].astype(F32)


def _build_bias(bkt_ref, rb_ref, bias_s):
    bkt = bkt_ref[...]
    for h in range(ATT_HEADS):
        acc = jnp.zeros((BLOCK, 2 * BLOCK), F32)
        for b in range(N_BUCKETS):
            acc = jnp.where(bkt == b, rb_ref[b, h], acc)
        bias_s[h] = acc


def _band_mask(n):
    r = lax.broadcasted_iota(jnp.int32, (BLOCK, 2 * BLOCK), 0)
    c = lax.broadcasted_iota(jnp.int32, (BLOCK, 2 * BLOCK), 1)
    key_pos = (n - 1) * BLOCK + c
    return (c > r) & (c <= r + BLOCK) & (key_pos >= PAD_FRONT)


def _split_heads(kv, kh):
    lane = lax.broadcasted_iota(jnp.int32, kv.shape, 1)
    if kh == 0:
        lo = jnp.where(lane < ATT_HEAD_DIM, kv, 0.0)
        hi = pltpu.roll(lo, ATT_HEAD_DIM, 1)
    else:
        hi = jnp.where(lane >= ATT_HEAD_DIM, kv, 0.0)
        lo = pltpu.roll(hi, ATT_HEAD_DIM, 1)
    return lo, hi


def _merge_heads(acc_lo, acc_hi, kh):
    lane = lax.broadcasted_iota(jnp.int32, acc_lo.shape, 1)
    if kh == 0:
        return jnp.where(lane < ATT_HEAD_DIM, acc_lo + pltpu.roll(acc_hi, ATT_HEAD_DIM, 1), 0.0)
    return jnp.where(lane >= ATT_HEAD_DIM, acc_hi + pltpu.roll(acc_lo, ATT_HEAD_DIM, 1), 0.0)


def _softmax_sink(q2b, kxb, bias_h, mask, sink_h):
    return _softmax_of(_nt(q2b, kxb), bias_h, mask, sink_h)


def _softmax_of(qk, bias_h, mask, sink_h):
    s = qk * ATT_SCALE + bias_h
    s = jnp.where(mask, s, NEG_INF)
    m = jnp.maximum(jnp.max(s, axis=-1, keepdims=True), sink_h)
    p = jnp.exp(s - m)
    es = jnp.exp(sink_h - m)
    inv = 1.0 / (jnp.sum(p, axis=-1, keepdims=True) + es)
    return p * inv, es * inv


def _rot(t, cosf, sinf):
    return t * cosf + pltpu.roll(t, BLOCK // 2, 1) * sinf


def _rot_t(d, cosf, sinf):
    return d * cosf + pltpu.roll(d * sinf, BLOCK // 2, 1)


def _decay_tables(h):
    lg = LOG_GAMMA[h]
    i = lax.broadcasted_iota(jnp.int32, (BLOCK, BLOCK), 0)
    j = lax.broadcasted_iota(jnp.int32, (BLOCK, BLOCK), 1)
    diff = (i - j).astype(F32)
    dm = jnp.where(diff >= 0, jnp.exp(diff * lg), 0.0)
    row = lax.broadcasted_iota(jnp.int32, (BLOCK, 1), 0).astype(F32)
    zeta = jnp.exp((BLOCK - 1 - row) * lg)
    xi = jnp.exp((row + 1.0) * lg)
    return dm, zeta, xi, math.exp(BLOCK * lg)


def _valid_col(n):
    row = lax.broadcasted_iota(jnp.int32, (BLOCK, 1), 0)
    return ((n * BLOCK + row) >= PAD_FRONT).astype(F32)


def _shift_down(cur, prev, k):
    row = lax.broadcasted_iota(jnp.int32, cur.shape, 0)
    return jnp.where(row >= k, pltpu.roll(cur, k, 0), pltpu.roll(prev, k, 0))


def _shift_up(cur, nxt, k):
    row = lax.broadcasted_iota(jnp.int32, cur.shape, 0)
    return jnp.where(row < BLOCK - k, pltpu.roll(cur, BLOCK - k, 0), pltpu.roll(nxt, BLOCK - k, 0))


def mixers_fwd(proj, cosf, sinf, bkt, rel_bias, sinks, conv_w, nb, nc):
    def body(p_ref, cos_ref, sin_ref, bkt_ref, rb_ref, sk_ref, cw_ref, br_ref, st_ref,
             bias_s, kv_s, state_s, u_s):
        p_ref = _Widened(p_ref)
        n = pl.program_id(0)

        @pl.when(n == 0)
        def _():
            _build_bias(bkt_ref, rb_ref, bias_s)
            kv_s[:, 0:BLOCK, :] = jnp.zeros((nb, BLOCK, 2 * BLOCK), F32)
            state_s[...] = jnp.zeros_like(state_s)
            u_s[...] = jnp.zeros_like(u_s)

        valid = _valid_col(n)
        mask = _band_mask(n)
        ex = range(nb)

        for b in ex:
            kv_s[b, BLOCK:2 * BLOCK, :] = p_ref[b, :, C_AK:C_AK + 2 * BLOCK]
        for kh in range(2):
            ks = [[t.astype(BF16) for t in _split_heads(kv_s[b, :, 0:BLOCK], kh)] for b in ex]
            vs = [[t.astype(BF16) for t in _split_heads(kv_s[b, :, BLOCK:2 * BLOCK], kh)] for b in ex]
            pairs = [(b, 2 * kh + jj) for jj in range(2) for b in ex]
            subs = [(b, j, x) for (b, j) in pairs for x in range(2)]
            qb_ = {(b, j): p_ref[b, :, C_AQ + BLOCK * j:C_AQ + BLOCK * (j + 1)].astype(BF16) for (b, j) in pairs}
            qk_ = {(b, j, x): _nt(qb_[(b, j)], ks[b][x]) for (b, j, x) in subs}
            pb_ = {}
            for u in subs:
                h = 2 * u[1] + u[2]
                pb_[u] = _softmax_of(qk_[u], bias_s[h], mask, sk_ref[0, h])[0].astype(BF16)
            o_ = {u: _nn(pb_[u], vs[u[0]][u[2]]) for u in subs}
            for (b, j) in pairs:
                gate = p_ref[b, :, C_AG + BLOCK * j:C_AG + BLOCK * (j + 1)]
                br_ref[b, :, BLOCK * j:BLOCK * (j + 1)] = ((o_[(b, j, 0)] + o_[(b, j, 1)]) * _silu(gate)).astype(BF16)
        for b in ex:
            kv_s[b, 0:BLOCK, :] = kv_s[b, BLOCK:2 * BLOCK, :]

        cosv = cos_ref[...]
        sinv = sin_ref[...]
        tabs = [_decay_tables(h) for h in range(RET_HEADS)]
        units = [(b, h) for h in range(RET_HEADS) for b in ex]
        sl = lambda c0, h: slice(c0 + BLOCK * h, c0 + BLOCK * (h + 1))
        q_, k_, v_, sp_ = {}, {}, {}, {}
        for u in units:
            b, h = u
            q_[u] = _rot(p_ref[b, :, sl(C_RQ, h)], cosv, sinv).astype(BF16)
            k_[u] = (_rot(p_ref[b, :, sl(C_RK, h)], cosv, sinv) * RET_SCALE * valid).astype(BF16)
            v_[u] = p_ref[b, :, sl(C_RV, h)]
            sp_[u] = state_s[b, h]
            st_ref[b, 0, h] = sp_[u]
        qk_ = {u: _nt(q_[u], k_[u]) for u in units}
        qs_ = {u: _nn(q_[u], sp_[u].astype(BF16)) for u in units}
        kv_ = {u: _tn(k_[u], (v_[u] * tabs[u[1]][1]).astype(BF16)) for u in units}
        a_ = {u: (qk_[u] * tabs[u[1]][0]).astype(BF16) for u in units}
        av_ = {u: _nn(a_[u], v_[u].astype(BF16)) for u in units}
        for u in units:
            b, h = u
            o = av_[u] + tabs[h][2] * qs_[u]
            mu = jnp.mean(o, axis=-1, keepdims=True)
            var = jnp.mean(jnp.square(o - mu), axis=-1, keepdims=True)
            oh = (o - mu) * lax.rsqrt(var + GN_EPS)
            gate = p_ref[b, :, sl(C_RG, h)]
            br_ref[b, :, BRANCH_WIDTH + BLOCK * h:BRANCH_WIDTH + BLOCK * (h + 1)] = (oh * _silu(gate)).astype(BF16)
            state_s[b, h] = tabs[h][3] * sp_[u] + kv_[u]

        for b in ex:
            u = p_ref[b, :, C_CC:C_CC + BRANCH_WIDTH] * p_ref[b, :, C_CX:C_CX + BRANCH_WIDTH] * valid
            u_prev = u_s[b]
            y = (cw_ref[0:1, :] * _shift_down(u, u_prev, 2) + cw_ref[1:2, :] * _shift_down(u, u_prev, 1)
                 + cw_ref[2:3, :] * u)
            yc = p_ref[b, :, C_CB:C_CB + BRANCH_WIDTH] * y * _silu(p_ref[b, :, C_CG:C_CG + BRANCH_WIDTH])
            br_ref[b, :, 2 * BRANCH_WIDTH:3 * BRANCH_WIDTH] = yc.astype(BF16)
            u_s[b] = u

    lp = nc * BLOCK
    smem = pl.BlockSpec(memory_space=pltpu.SMEM)
    br, states = pl.pallas_call(
        body, name="mixers_fwd",
        grid=(nc,),
        in_specs=[pl.BlockSpec((nb, BLOCK, ABC_WIDTH), lambda n: (0, n, 0)),
                  pl.BlockSpec((BLOCK, BLOCK), lambda n: (n, 0)),
                  pl.BlockSpec((BLOCK, BLOCK), lambda n: (n, 0)),
                  pl.BlockSpec((BLOCK, 2 * BLOCK), lambda n: (0, 0)),
                  smem, smem,
                  pl.BlockSpec((3, BRANCH_WIDTH), lambda n: (0, 0))],
        out_specs=[pl.BlockSpec((nb, BLOCK, N_BRANCH * BRANCH_WIDTH), lambda n: (0, n, 0)),
                   pl.BlockSpec((nb, 1, RET_HEADS, BLOCK, BLOCK), lambda n: (0, n, 0, 0, 0))],
        out_shape=[jax.ShapeDtypeStruct((nb, lp, N_BRANCH * BRANCH_WIDTH), BF16),
                   jax.ShapeDtypeStruct((nb, nc, RET_HEADS, BLOCK, BLOCK), F32)],
        scratch_shapes=[pltpu.VMEM((ATT_HEADS, BLOCK, 2 * BLOCK), F32),
                        pltpu.VMEM((nb, 2 * BLOCK, 2 * BLOCK), F32),
                        pltpu.VMEM((nb, RET_HEADS, BLOCK, BLOCK), F32),
                        pltpu.VMEM((nb, BLOCK, BRANCH_WIDTH), F32)],
        compiler_params=_cparams("arbitrary"),
    )(proj.reshape(nb, lp, ABC_WIDTH), cosf, sinf, bkt, rel_bias, sinks, conv_w)
    return br.reshape(nb * lp, N_BRANCH * BRANCH_WIDTH), states


def mixers_bwd(proj, d_br, states, cosf, sinf, bkt, rel_bias, sinks, conv_w, nb, nc):
    def body(p_ref, kvp_ref, cp_ref, dbr_ref, st_ref, cos_ref, sin_ref, bkt_ref, rb_ref, sk_ref, cw_ref,
             dp_ref, drb_ref, dsk_ref, dcw_ref,
             bias_s, dbias_s, dkv_s, g_s, dy_s):
        p_ref, kvp_ref, cp_ref, dbr_ref = [_Widened(r) for r in (p_ref, kvp_ref, cp_ref, dbr_ref)]
        step = pl.program_id(0)
        n = nc - 1 - step
        ex = range(nb)

        @pl.when(step == 0)
        def _():
            _build_bias(bkt_ref, rb_ref, bias_s)
            dbias_s[...] = jnp.zeros_like(dbias_s)
            dsk_ref[...] = jnp.zeros_like(dsk_ref)
            dcw_ref[...] = jnp.zeros_like(dcw_ref)
            drb_ref[...] = jnp.zeros_like(drb_ref)
            dkv_s[...] = jnp.zeros_like(dkv_s)
            g_s[...] = jnp.zeros_like(g_s)
            dy_s[...] = jnp.zeros_like(dy_s)

        valid = _valid_col(n)
        mask = _band_mask(n)
        has_prev = (n > 0).astype(F32)

        k_all, v_all = [], []
        for b in ex:
            kv_prev = kvp_ref[b] * has_prev
            kv_cur = p_ref[b, :, C_AK:C_AK + 2 * BLOCK]
            k_all.append(jnp.concatenate([kv_prev[:, 0:BLOCK], kv_cur[:, 0:BLOCK]], axis=0))
            v_all.append(jnp.concatenate([kv_prev[:, BLOCK:], kv_cur[:, BLOCK:]], axis=0))
        zero2 = jnp.zeros((2 * BLOCK, BLOCK), F32)
        dk_tot = [zero2 for _ in ex]
        dv_tot = [zero2 for _ in ex]
        for kh in range(2):
            ks = [[t.astype(BF16) for t in _split_heads(k_all[b], kh)] for b in ex]
            vs = [[t.astype(BF16) for t in _split_heads(v_all[b], kh)] for b in ex]
            pairs = [(b, 2 * kh + jj) for jj in range(2) for b in ex]
            subs = [(b, j, x) for (b, j) in pairs for x in range(2)]
            qb_, gate_, dya_, do2_ = {}, {}, {}, {}
            for w in pairs:
                b, j = w
                qb_[w] = p_ref[b, :, C_AQ + BLOCK * j:C_AQ + BLOCK * (j + 1)].astype(BF16)
                gate_[w] = p_ref[b, :, C_AG + BLOCK * j:C_AG + BLOCK * (j + 1)]
                dya_[w] = dbr_ref[b, :, BLOCK * j:BLOCK * (j + 1)]
                do2_[w] = (dya_[w] * _silu(gate_[w])).astype(BF16)
            qk_ = {(b, j, x): _nt(qb_[(b, j)], ks[b][x]) for (b, j, x) in subs}
            dpm_ = {(b, j, x): _nt(do2_[(b, j)], vs[b][x]) for (b, j, x) in subs}
            pb_, dsb_ = {}, {}
            for u in subs:
                b, j, x = u
                h = 2 * j + x
                p, p_sink = _softmax_of(qk_[u], bias_s[h], mask, sk_ref[0, h])
                pb_[u] = p.astype(BF16)
                delta = jnp.sum(p * dpm_[u], axis=-1, keepdims=True)
                ds = p * (dpm_[u] - delta)
                dbias_s[h] += ds
                dsk_ref[h:h + 1, :] += jnp.broadcast_to(
                    jnp.sum(-p_sink * delta, axis=0, keepdims=True), (1, BLOCK))
                dsb_[u] = ds.astype(BF16)
            o_ = {u: _nn(pb_[u], vs[u[0]][u[2]]) for u in subs}
            dq_ = {u: _nn(dsb_[u], ks[u[0]][u[2]]) for u in subs}
            dkm_ = {u: _tn(dsb_[u], qb_[(u[0], u[1])]) for u in subs}
            dvm_ = {u: _tn(pb_[u], do2_[(u[0], u[1])]) for u in subs}
            for w in pairs:
                b, j = w
                o2 = o_[(b, j, 0)] + o_[(b, j, 1)]
                dq2 = (dq_[(b, j, 0)] + dq_[(b, j, 1)]) * ATT_SCALE
                dp_ref[b, :, C_AQ + BLOCK * j:C_AQ + BLOCK * (j + 1)] = dq2.astype(BF16)
                dp_ref[b, :, C_AG + BLOCK * j:C_AG + BLOCK * (j + 1)] = (
                    dya_[w] * o2 * _dsilu(gate_[w])).astype(BF16)
            for b in ex:
                j0, j1 = 2 * kh, 2 * kh + 1
                dk_lo = (dkm_[(b, j0, 0)] + dkm_[(b, j1, 0)]) * ATT_SCALE
                dk_hi = (dkm_[(b, j0, 1)] + dkm_[(b, j1, 1)]) * ATT_SCALE
                dk_tot[b] = dk_tot[b] + _merge_heads(dk_lo, dk_hi, kh)
                dv_tot[b] = dv_tot[b] + _merge_heads(dvm_[(b, j0, 0)] + dvm_[(b, j1, 0)],
                                                     dvm_[(b, j0, 1)] + dvm_[(b, j1, 1)], kh)
        for b in ex:
            dp_ref[b, :, C_AK:C_AK + BLOCK] = (dk_tot[b][BLOCK:, :] + dkv_s[b, :, 0:BLOCK]).astype(BF16)
            dp_ref[b, :, C_AV:C_AV + BLOCK] = (dv_tot[b][BLOCK:, :] + dkv_s[b, :, BLOCK:]).astype(BF16)
            dkv_s[b, :, 0:BLOCK] = dk_tot[b][0:BLOCK, :]
            dkv_s[b, :, BLOCK:] = dv_tot[b][0:BLOCK, :]

        cosv = cos_ref[...]
        sinv = sin_ref[...]
        tabs = [_decay_tables(h) for h in range(RET_HEADS)]
        units = [(b, h) for h in range(RET_HEADS) for b in ex]
        sl = lambda c0, h: slice(c0 + BLOCK * h, c0 + BLOCK * (h + 1))
        q_, k_, v_, vb_, sp_ = {}, {}, {}, {}, {}
        for u in units:
            b, h = u
            q_[u] = _rot(p_ref[b, :, sl(C_RQ, h)], cosv, sinv).astype(BF16)
            k_[u] = (_rot(p_ref[b, :, sl(C_RK, h)], cosv, sinv) * RET_SCALE * valid).astype(BF16)
            v_[u] = p_ref[b, :, sl(C_RV, h)]
            vb_[u] = v_[u].astype(BF16)
            sp_[u] = st_ref[b, 0, h].astype(BF16)
        qk_ = {u: _nt(q_[u], k_[u]) for u in units}
        qs_ = {u: _nn(q_[u], sp_[u]) for u in units}
        a_ = {u: (qk_[u] * tabs[u[1]][0]).astype(BF16) for u in units}
        av_ = {u: _nn(a_[u], vb_[u]) for u in units}
        dob_, dxo_ = {}, {}
        for u in units:
            b, h = u
            xi = tabs[h][2]
            o = av_[u] + xi * qs_[u]
            mu = jnp.mean(o, axis=-1, keepdims=True)
            var = jnp.mean(jnp.square(o - mu), axis=-1, keepdims=True)
            rstd = lax.rsqrt(var + GN_EPS)
            oh = (o - mu) * rstd
            gate = p_ref[b, :, sl(C_RG, h)]
            d_yr = dbr_ref[b, :, BRANCH_WIDTH + BLOCK * h:BRANCH_WIDTH + BLOCK * (h + 1)]
            dp_ref[b, :, sl(C_RG, h)] = (d_yr * oh * _dsilu(gate)).astype(BF16)
            doh = d_yr * _silu(gate)
            do = rstd * (doh - jnp.mean(doh, axis=-1, keepdims=True)
                         - oh * jnp.mean(doh * oh, axis=-1, keepdims=True))
            dob_[u] = do.astype(BF16)
            dxo_[u] = (do * xi).astype(BF16)
        dov_ = {u: _nt(dob_[u], vb_[u]) for u in units}
        dv1_ = {u: _tn(a_[u], dob_[u]) for u in units}
        dq1_ = {u: _nt(dxo_[u], sp_[u]) for u in units}
        gq_ = {u: _tn(q_[u], dxo_[u]) for u in units}
        da_, gb_, zv_ = {}, {}, {}
        for u in units:
            b, h = u
            da_[u] = (dov_[u] * tabs[h][0]).astype(BF16)
            g_next = g_s[b, h]
            gb_[u] = g_next.astype(BF16)
            zv_[u] = (v_[u] * tabs[h][1]).astype(BF16)
            g_s[b, h] = tabs[h][3] * g_next + gq_[u]
        dq2_ = {u: _nn(da_[u], k_[u]) for u in units}
        dk1_ = {u: _tn(da_[u], q_[u]) for u in units}
        dk2_ = {u: _nt(zv_[u], gb_[u]) for u in units}
        dv2_ = {u: _nn(k_[u], gb_[u]) for u in units}
        for u in units:
            b, h = u
            dp_ref[b, :, sl(C_RQ, h)] = _rot_t(dq2_[u] + dq1_[u], cosv, sinv).astype(BF16)
            dp_ref[b, :, sl(C_RK, h)] = _rot_t((dk1_[u] + dk2_[u]) * (RET_SCALE * valid), cosv, sinv).astype(BF16)
            dp_ref[b, :, sl(C_RV, h)] = (dv1_[u] + tabs[h][1] * dv2_[u]).astype(BF16)

        w0, w1, w2 = cw_ref[0:1, :], cw_ref[1:2, :], cw_ref[2:3, :]
        for b in ex:
            cb = p_ref[b, :, C_CB:C_CB + BRANCH_WIDTH]
            cc = p_ref[b, :, C_CC:C_CC + BRANCH_WIDTH]
            cx = p_ref[b, :, C_CX:C_CX + BRANCH_WIDTH]
            cg = p_ref[b, :, C_CG:C_CG + BRANCH_WIDTH]
            u = cc * cx * valid
            u_prev = (cp_ref[b, :, 0:BRANCH_WIDTH] * cp_ref[b, :, BRANCH_WIDTH:2 * BRANCH_WIDTH]
                      * (_valid_col(n - 1) * has_prev))
            u1 = _shift_down(u, u_prev, 1)
            u2 = _shift_down(u, u_prev, 2)
            y = w0 * u2 + w1 * u1 + w2 * u
            d_yc = dbr_ref[b, :, 2 * BRANCH_WIDTH:3 * BRANCH_WIDTH]
            sg = _silu(cg)
            dp_ref[b, :, C_CB:C_CB + BRANCH_WIDTH] = (d_yc * y * sg).astype(BF16)
            dp_ref[b, :, C_CG:C_CG + BRANCH_WIDTH] = (d_yc * cb * y * _dsilu(cg)).astype(BF16)
            dy = d_yc * cb * sg
            dy_next = dy_s[b]
            du = (w2 * dy + w1 * _shift_up(dy, dy_next, 1) + w0 * _shift_up(dy, dy_next, 2)) * valid
            dp_ref[b, :, C_CC:C_CC + BRANCH_WIDTH] = (du * cx).astype(BF16)
            dp_ref[b, :, C_CX:C_CX + BRANCH_WIDTH] = (du * cc).astype(BF16)
            dcw_ref[0:1, :] += jnp.sum(dy * u2, axis=0, keepdims=True)
            dcw_ref[1:2, :] += jnp.sum(dy * u1, axis=0, keepdims=True)
            dcw_ref[2:3, :] += jnp.sum(dy * u, axis=0, keepdims=True)
            dy_s[b] = dy

        @pl.when(step == nc - 1)
        def _():
            bkt = bkt_ref[...]
            row = lax.broadcasted_iota(jnp.int32, (N_BUCKETS, BLOCK), 0)
            lane = lax.broadcasted_iota(jnp.int32, (N_BUCKETS, BLOCK), 1)

            def one_bucket(bk, acc):
                sel = bkt == bk
                for h in range(ATT_HEADS):
                    t = jnp.where(sel, dbias_s[h], 0.0)
                    s = jnp.sum(jnp.sum(t, axis=1, keepdims=True), axis=0, keepdims=True)
                    acc = acc + jnp.where((row == bk) & (lane == h), jnp.broadcast_to(s, acc.shape), 0.0)
                return acc

            drb_ref[...] = lax.fori_loop(0, N_BUCKETS, one_bucket, jnp.zeros((N_BUCKETS, BLOCK), F32))

    lp = nc * BLOCK
    smem = pl.BlockSpec(memory_space=pltpu.SMEM)
    blk = lambda s: nc - 1 - s
    prev = lambda s: jnp.maximum(nc - 2 - s, 0)
    proj3 = proj.reshape(nb, lp, ABC_WIDTH)
    res = pl.pallas_call(
        body, name="mixers_bwd",
        grid=(nc,),
        in_specs=[pl.BlockSpec((nb, BLOCK, ABC_WIDTH), lambda s: (0, blk(s), 0)),
                  pl.BlockSpec((nb, BLOCK, 2 * BLOCK), lambda s: (0, prev(s), C_AK // (2 * BLOCK))),
                  pl.BlockSpec((nb, BLOCK, 1280), lambda s: (0, prev(s), C_CC // 1280)),
                  pl.BlockSpec((nb, BLOCK, N_BRANCH * BRANCH_WIDTH), lambda s: (0, blk(s), 0)),
                  pl.BlockSpec((nb, 1, RET_HEADS, BLOCK, BLOCK), lambda s: (0, blk(s), 0, 0, 0)),
                  pl.BlockSpec((BLOCK, BLOCK), lambda s: (blk(s), 0)),
                  pl.BlockSpec((BLOCK, BLOCK), lambda s: (blk(s), 0)),
                  pl.BlockSpec((BLOCK, 2 * BLOCK), lambda s: (0, 0)),
                  smem, smem,
                  pl.BlockSpec((3, BRANCH_WIDTH), lambda s: (0, 0))],
        out_specs=[pl.BlockSpec((nb, BLOCK, ABC_WIDTH), lambda s: (0, blk(s), 0)),
                   pl.BlockSpec((N_BUCKETS, BLOCK), lambda s: (0, 0)),
                   pl.BlockSpec((ATT_HEADS, BLOCK), lambda s: (0, 0)),
                   pl.BlockSpec((8, BRANCH_WIDTH), lambda s: (0, 0))],
        out_shape=[jax.ShapeDtypeStruct((nb, lp, ABC_WIDTH), BF16),
                   jax.ShapeDtypeStruct((N_BUCKETS, BLOCK), F32),
                   jax.ShapeDtypeStruct((ATT_HEADS, BLOCK), F32),
                   jax.ShapeDtypeStruct((8, BRANCH_WIDTH), F32)],
        scratch_shapes=[pltpu.VMEM((ATT_HEADS, BLOCK, 2 * BLOCK), F32),
                        pltpu.VMEM((ATT_HEADS, BLOCK, 2 * BLOCK), F32),
                        pltpu.VMEM((nb, BLOCK, 2 * BLOCK), F32),
                        pltpu.VMEM((nb, RET_HEADS, BLOCK, BLOCK), F32),
                        pltpu.VMEM((nb, BLOCK, BRANCH_WIDTH), F32)],
        compiler_params=_cparams("arbitrary"),
    )(proj3, proj3, proj3, d_br.reshape(nb, lp, N_BRANCH * BRANCH_WIDTH), states, cosf, sinf, bkt, rel_bias, sinks,
      conv_w)
    return (res[0].reshape(nb * lp, ABC_WIDTH),) + tuple(res[1:])


MERGE_TILE = 256


def _merge_forward(br_ref, m_ref, wb_ref, wo_ref):
    bo, gates = [], []
    mixed_pre = None
    for g in range(N_BRANCH):
        br_g = br_ref[:, BRANCH_WIDTH * g:BRANCH_WIDTH * (g + 1)]
        bo_g = jnp.concatenate([_nn(br_g, wb_ref[p, g]) for p in range(N_CHIPS)], axis=1)
        gate_g = _sigmoid(m_ref[:, D_MODEL * g:D_MODEL * (g + 1)].astype(F32))
        bo.append(bo_g)
        gates.append(gate_g)
        mixed_pre = gate_g * bo_g if mixed_pre is None else mixed_pre + gate_g * bo_g
    mixed = _nn(mixed_pre.astype(BF16), wo_ref[...])
    r = lax.rsqrt(jnp.mean(mixed * mixed, axis=-1, keepdims=True) + RMS_EPS)
    return bo, gates, mixed_pre, mixed, r


def merge_fwd(x2d, br, pm, wb, wo, g_post):
    t = x2d.shape[0]
    tm = MERGE_TILE if t % MERGE_TILE == 0 else BLOCK

    def body(x_ref, br_ref, m_ref, wb_ref, wo_ref, g_ref, o_ref):
        _, _, _, mixed, r = _merge_forward(br_ref, m_ref, wb_ref, wo_ref)
        o_ref[...] = x_ref[...] + mixed * r * g_ref[...]

    return pl.pallas_call(
        body, name="merge_fwd",
        grid=(t // tm,),
        in_specs=[pl.BlockSpec((tm, D_MODEL), lambda i: (i, 0)),
                  pl.BlockSpec((tm, N_BRANCH * BRANCH_WIDTH), lambda i: (i, 0)),
                  pl.BlockSpec((tm, MERGE_WIDTH), lambda i: (i, 0)),
                  pl.BlockSpec((N_CHIPS, N_BRANCH, BRANCH_WIDTH, SHARD_D), lambda i: (0, 0, 0, 0)),
                  pl.BlockSpec((D_MODEL, D_MODEL), lambda i: (0, 0)),
                  pl.BlockSpec((1, D_MODEL), lambda i: (0, 0))],
        out_specs=pl.BlockSpec((tm, D_MODEL), lambda i: (i, 0)),
        out_shape=jax.ShapeDtypeStruct((t, D_MODEL), F32),
        compiler_params=_cparams("parallel"),
    )(x2d, br, pm, wb, wo, g_post)


def merge_bwd(d_out, br, pm, wb, wo, g_post):
    t = d_out.shape[0]
    tm = MERGE_TILE if t % MERGE_TILE == 0 else BLOCK

    def body(do_ref, br_ref, m_ref, wb_ref, wo_ref, g_ref, dbr_ref, dm_ref, dg_ref, dwb_ref, dwo_ref):

        @pl.when(pl.program_id(0) == 0)
        def _():
            dwb_ref[...] = jnp.zeros_like(dwb_ref)
            dwo_ref[...] = jnp.zeros_like(dwo_ref)
            dg_ref[...] = jnp.zeros_like(dg_ref)

        bo, gates, mixed_pre, mixed, r = _merge_forward(br_ref, m_ref, wb_ref, wo_ref)
        d_o = do_ref[...]
        nh = mixed * r
        dg_ref[0:1, :] += jnp.sum(d_o * nh, axis=0, keepdims=True)
        dn = d_o * g_ref[...]
        d_mixed = (r * (dn - nh * jnp.mean(dn * nh, axis=-1, keepdims=True))).astype(BF16)
        dwo_ref[...] += _tn(mixed_pre.astype(BF16), d_mixed)
        d_pre = _nt(d_mixed, wo_ref[...])
        for g in range(N_BRANCH):
            br_g = br_ref[:, BRANCH_WIDTH * g:BRANCH_WIDTH * (g + 1)]
            d_bo = (d_pre * gates[g]).astype(BF16)
            dm_ref[:, D_MODEL * g:D_MODEL * (g + 1)] = (
                d_pre * bo[g] * gates[g] * (1.0 - gates[g])).astype(BF16)
            d_br_g = None
            for p in range(N_CHIPS):
                d_bo_p = d_bo[:, SHARD_D * p:SHARD_D * (p + 1)]
                part = _nt(d_bo_p, wb_ref[p, g])
                d_br_g = part if d_br_g is None else d_br_g + part
                dwb_ref[p, g] += _tn(br_g, d_bo_p)
            dbr_ref[:, BRANCH_WIDTH * g:BRANCH_WIDTH * (g + 1)] = d_br_g.astype(BF16)

    return pl.pallas_call(
        body, name="merge_bwd",
        grid=(t // tm,),
        in_specs=[pl.BlockSpec((tm, D_MODEL), lambda i: (i, 0)),
                  pl.BlockSpec((tm, N_BRANCH * BRANCH_WIDTH), lambda i: (i, 0)),
                  pl.BlockSpec((tm, MERGE_WIDTH), lambda i: (i, 0)),
                  pl.BlockSpec((N_CHIPS, N_BRANCH, BRANCH_WIDTH, SHARD_D), lambda i: (0, 0, 0, 0)),
                  pl.BlockSpec((D_MODEL, D_MODEL), lambda i: (0, 0)),
                  pl.BlockSpec((1, D_MODEL), lambda i: (0, 0))],
        out_specs=[pl.BlockSpec((tm, N_BRANCH * BRANCH_WIDTH), lambda i: (i, 0)),
                   pl.BlockSpec((tm, MERGE_WIDTH), lambda i: (i, 0)),
                   pl.BlockSpec((8, D_MODEL), lambda i: (0, 0)),
                   pl.BlockSpec((N_CHIPS, N_BRANCH, BRANCH_WIDTH, SHARD_D), lambda i: (0, 0, 0, 0)),
                   pl.BlockSpec((D_MODEL, D_MODEL), lambda i: (0, 0))],
        out_shape=[jax.ShapeDtypeStruct((t, N_BRANCH * BRANCH_WIDTH), BF16),
                   jax.ShapeDtypeStruct((t, MERGE_WIDTH), BF16),
                   jax.ShapeDtypeStruct((8, D_MODEL), F32),
                   jax.ShapeDtypeStruct((N_CHIPS, N_BRANCH, BRANCH_WIDTH, SHARD_D), F32),
                   jax.ShapeDtypeStruct((D_MODEL, D_MODEL), F32)],
        compiler_params=_cparams("arbitrary"),
    )(d_out, br, pm, wb, wo, g_post)


def loss_head(xf, target2d, nb, nc):
    def body(x_ref, t_ref, l_ref, dx_ref):
        b = pl.program_id(0)
        n = pl.program_id(1)

        @pl.when((b == 0) & (n == 0))
        def _():
            l_ref[...] = jnp.zeros_like(l_ref)

        @pl.when(n == 0)
        def _():
            dx_ref[...] = jnp.zeros_like(dx_ref)

        @pl.when(n > 0)
        def _():
            e = x_ref[...] - t_ref[...]
            dx_ref[...] = e * (1.0 / D_MODEL)
            s = jnp.sum(jnp.sum(e * e, axis=1, keepdims=True), axis=0, keepdims=True)
            l_ref[...] += jnp.broadcast_to(s * (0.5 / D_MODEL), l_ref.shape)

    return pl.pallas_call(
        body, name="loss_head",
        grid=(nb, nc),
        in_specs=[pl.BlockSpec((BLOCK, D_MODEL), lambda b, n: (b * nc + n, 0)),
                  pl.BlockSpec((BLOCK, D_MODEL), lambda b, n: (b * (nc - 1) + jnp.maximum(n - 1, 0), 0))],
        out_specs=[pl.BlockSpec((8, BLOCK), lambda b, n: (0, 0)),
                   pl.BlockSpec((BLOCK, D_MODEL), lambda b, n: (b * nc + n, 0))],
        out_shape=[jax.ShapeDtypeStruct((8, BLOCK), F32),
                   jax.ShapeDtypeStruct(xf.shape, F32)],
        compiler_params=_cparams("arbitrary", "arbitrary"),
    )(xf, target2d)


N_ABC_TILES = ABC_WIDTH // COL_TILE
N_M_TILES = MERGE_WIDTH // COL_TILE


def proj_dgrad(d_abc, d_m, w, x2d, g, d_out):
    t = x2d.shape[0]
    tm = ROW_TILE if t % ROW_TILE == 0 else BLOCK
    nk = N_ABC_TILES + N_M_TILES

    def body(da_ref, dm_ref, w_ref, x_ref, g_ref, do_ref, dx_ref, dg_ref, acc):
        i = pl.program_id(0)
        k = pl.program_id(1)

        @pl.when((i == 0) & (k == 0))
        def _():
            dg_ref[...] = jnp.zeros_like(dg_ref)

        @pl.when(k == 0)
        def _():
            acc[...] = jnp.zeros_like(acc)

        @pl.when(k < N_ABC_TILES)
        def _():
            acc[...] += _nn(da_ref[...], w_ref[...])

        @pl.when(k >= N_ABC_TILES)
        def _():
            acc[...] += _nn(dm_ref[...], w_ref[...])

        @pl.when(k == nk - 1)
        def _():
            x = x_ref[...]
            r = lax.rsqrt(jnp.mean(x * x, axis=-1, keepdims=True) + RMS_EPS)
            nh = x * r
            dh = acc[...]
            dg_ref[0:1, :] += jnp.sum(dh * nh, axis=0, keepdims=True)
            dn = dh * g_ref[...]
            dx_ref[...] = do_ref[...] + r * (dn - nh * jnp.mean(dn * nh, axis=-1, keepdims=True))

    return pl.pallas_call(
        body, name="proj_dgrad",
        grid=(t // tm, nk),
        in_specs=[pl.BlockSpec((tm, COL_TILE), lambda i, k: (i, jnp.minimum(k, N_ABC_TILES - 1))),
                  pl.BlockSpec((tm, COL_TILE), lambda i, k: (i, jnp.maximum(k - N_ABC_TILES, 0))),
                  pl.BlockSpec((COL_TILE, D_MODEL), lambda i, k: (k, 0)),
                  pl.BlockSpec((tm, D_MODEL), lambda i, k: (i, 0)),
                  pl.BlockSpec((1, D_MODEL), lambda i, k: (0, 0)),
                  pl.BlockSpec((tm, D_MODEL), lambda i, k: (i, 0))],
        out_specs=[pl.BlockSpec((tm, D_MODEL), lambda i, k: (i, 0)),
                   pl.BlockSpec((8, D_MODEL), lambda i, k: (0, 0))],
        out_shape=[jax.ShapeDtypeStruct((t, D_MODEL), F32),
                   jax.ShapeDtypeStruct((8, D_MODEL), F32)],
        scratch_shapes=[pltpu.VMEM((tm, D_MODEL), F32)],
        compiler_params=_cparams("arbitrary", "arbitrary"),
    )(d_abc, d_m, w, x2d, g, d_out)


def proj_wgrad(hb, d_abc, d_m):
    t = hb.shape[0]
    nj = N_ABC_TILES + N_M_TILES

    def body(h_ref, da_ref, dm_ref, o_ref):
        j = pl.program_id(0)

        @pl.when(j < N_ABC_TILES)
        def _():
            o_ref[...] = _tn(da_ref[...], h_ref[...])

        @pl.when(j >= N_ABC_TILES)
        def _():
            o_ref[...] = _tn(dm_ref[...], h_ref[...])

    return pl.pallas_call(
        body, name="proj_wgrad",
        grid=(nj,),
        in_specs=[pl.BlockSpec((t, D_MODEL), lambda j: (0, 0)),
                  pl.BlockSpec((t, COL_TILE), lambda j: (0, jnp.minimum(j, N_ABC_TILES - 1))),
                  pl.BlockSpec((t, COL_TILE), lambda j: (0, jnp.maximum(j - N_ABC_TILES, 0)))],
        out_specs=pl.BlockSpec((COL_TILE, D_MODEL), lambda j: (j, 0)),
        out_shape=jax.ShapeDtypeStruct((PROJ_WIDTH, D_MODEL), F32),
        compiler_params=_cparams("arbitrary"),
    )(hb, d_abc, d_m)


def _adamw_math(w, g, m, v):
    m = ADAM_B1 * m + (1.0 - ADAM_B1) * g
    v = ADAM_B2 * v + (1.0 - ADAM_B2) * jnp.square(g)
    m_hat = m / (1.0 - ADAM_B1 ** ADAM_STEP)
    v_hat = v / (1.0 - ADAM_B2 ** ADAM_STEP)
    delta = -ADAM_LR * (m_hat / (jnp.sqrt(v_hat) + ADAM_EPS) + ADAM_WD * w)
    return delta, m, v


def adamw_layer(w, g, m, v, layer, acc):
    _, r, c = w.shape
    tr = _row_tile(r)

    def body(*refs):
        w_ref, g_ref, m_ref, v_ref = refs[:4]
        go_ref, d_ref, mo_ref, vo_ref = refs[-4:]
        g_val = g_ref[...]
        d, m_new, v_new = _adamw_math(w_ref[...], g_val, m_ref[...], v_ref[...])
        go_ref[...] = g_val
        d_ref[...] = d
        mo_ref[...] = m_new
        vo_ref[...] = v_new

    slab = pl.BlockSpec((None, tr, c), lambda i: (layer, i, 0))
    ins = [w, g, m, v]
    in_specs = [slab, pl.BlockSpec((tr, c), lambda i: (i, 0)), slab, slab]
    aliases = {}
    if acc is not None:
        ins += list(acc)
        in_specs += [ANY] * 4
        aliases = {4 + i: i for i in range(4)}
    return pl.pallas_call(
        body, name="adamw_layer",
        grid=(r // tr,),
        in_specs=in_specs, out_specs=[slab] * 4,
        out_shape=[jax.ShapeDtypeStruct(w.shape, F32)] * 4,
        input_output_aliases=aliases,
        compiler_params=_cparams("parallel"),
    )(*ins)


def adamw_small(params):
    k = len(params)

    def body(*refs):
        ins, outs = refs[:4 * k], refs[4 * k:]
        for i in range(k):
            d, m_new, v_new = _adamw_math(*[r[...] for r in ins[4 * i:4 * i + 4]])
            outs[3 * i][...] = d
            outs[3 * i + 1][...] = m_new
            outs[3 * i + 2][...] = v_new

    flat = [a for p in params for a in p]
    vm = pl.BlockSpec(memory_space=pltpu.VMEM)
    out_shape = [jax.ShapeDtypeStruct(p[0].shape, F32) for p in params for _ in range(3)]
    res = pl.pallas_call(
        body, name="adamw_small",
        in_specs=[vm] * len(flat), out_specs=[vm] * len(out_shape), out_shape=out_shape,
    )(*flat)
    return [tuple(res[3 * i:3 * i + 3]) for i in range(k)]


ANY = pl.BlockSpec(memory_space=pl.ANY)


def _place():
    return lax.axis_index("x"), lax.axis_index("y"), lax.axis_index("c")


HBM = pl.BlockSpec(memory_space=pltpu.HBM)
SEM = pl.BlockSpec(memory_space=pltpu.SEMAPHORE)
EFFECT = pltpu.SideEffectType.DATAFLOW_SIDE_EFFECTING


def _other_chips(x, y):
    return [(1 - x, y), (x, 1 - y), (1 - x, 1 - y)]


def _own_slot(shard, chip):
    buf = lax.empty((N_CHIPS,) + shard.shape, shard.dtype)
    return lax.dynamic_update_slice(buf, shard[None], (chip, 0, 0, 0))


def gather_weight_shards(bufs, after):
    n = len(bufs)

    def body(*refs):
        g_refs = refs[n + 1:2 * n + 1]
        send_sems, recv_sems = refs[2 * n + 1:]
        x, y, c = _place()
        me_p = 2 * x + y
        sibling = (x, y, 1 - c)
        chips = _other_chips(x, y)

        def copy(k, slab, to):
            return pltpu.make_async_remote_copy(src_ref=slab, dst_ref=slab, send_sem=send_sems.at[k],
                                                recv_sem=recv_sems.at[k], device_id=to, device_id_type=MESH)

        first, passed = [], []
        for t in range(n):
            for k, (qx, qy) in enumerate(chips):
                cp = copy(6 * t + k, g_refs[t].at[me_p, c], (qx, qy, c))
                cp.start()
                first.append(cp)
        for t in range(n):
            for k, (qx, qy) in enumerate(chips):
                slab = g_refs[t].at[2 * qx + qy, c]
                copy(6 * t + k, slab, (qx, qy, c)).wait_recv()
                fwd = copy(6 * t + 3 + k, slab, sibling)
                fwd.start()
                passed.append(fwd)
        for t in range(n):
            for k, (qx, qy) in enumerate(chips):
                copy(6 * t + 3 + k, g_refs[t].at[2 * qx + qy, 1 - c], sibling).wait_recv()
        for cp in first + passed:
            cp.wait_send()

    return pl.pallas_call(
        body, name="gather_weight_shards",
        in_specs=[ANY] * (n + 1), out_specs=[ANY] * n,
        out_shape=[jax.ShapeDtypeStruct(b.shape, b.dtype) for b in bufs],
        input_output_aliases={t: t for t in range(n)},
        scratch_shapes=[pltpu.SemaphoreType.DMA((6 * n,)), pltpu.SemaphoreType.DMA((6 * n,))],
    )(*bufs, after)


def _hbm(a):
    return pltpu.with_memory_space_constraint(a, pltpu.HBM)


def gather_start(bufs, after):
    n = len(bufs)

    def body(*refs):
        g_refs = refs[:n]
        send_sems, recv_sems = refs[n + 1], refs[n + 2]
        token = refs[-1]
        x, y, c = _place()
        me_p = 2 * x + y
        for t in range(n):
            for k, (qx, qy) in enumerate(_other_chips(x, y)):
                slab = g_refs[t].at[me_p, c]
                pltpu.make_async_remote_copy(src_ref=slab, dst_ref=slab, send_sem=send_sems.at[3 * t + k],
                                             recv_sem=recv_sems.at[3 * t + k], device_id=(qx, qy, c),
                                             device_id_type=MESH).start()
        token[...] = jnp.zeros_like(token)

    res = pl.pallas_call(
        body, name="gather_start",
        in_specs=[HBM] * n + [ANY],
        out_specs=[SEM, SEM] + [HBM] * n + [pl.BlockSpec(memory_space=pltpu.VMEM)],
        out_shape=[pltpu.SemaphoreType.DMA((3 * n,)), pltpu.SemaphoreType.DMA((3 * n,))]
        + [pltpu.HBM(b.shape, b.dtype) for b in bufs] + [jax.ShapeDtypeStruct((8, LANES), F32)],
        input_output_aliases={t: 2 + t for t in range(n)},
        compiler_params=pltpu.CompilerParams(has_side_effects=EFFECT),
    )(*[_hbm(b) for b in bufs], after)
    return res[0], res[1], list(res[2:2 + n]), res[-1]


def gather_wait(bufs, send_sems, recv_sems, after):
    n = len(bufs)

    def body(*refs):
        g_refs = refs[:n]
        send_sems, recv_sems = refs[n], refs[n + 1]
        x, y, c = _place()
        me_p = 2 * x + y
        for t in range(n):
            for k, (qx, qy) in enumerate(_other_chips(x, y)):
                cp = pltpu.make_async_remote_copy(src_ref=g_refs[t].at[me_p, c], dst_ref=g_refs[t].at[2 * qx + qy, c],
                                                  send_sem=send_sems.at[3 * t + k], recv_sem=recv_sems.at[3 * t + k],
                                                  device_id=(qx, qy, c), device_id_type=MESH)
                cp.wait_send()
                cp.wait_recv()

    return pl.pallas_call(
        body, name="gather_wait",
        in_specs=[HBM] * n + [SEM, SEM, ANY],
        out_specs=[HBM] * n,
        out_shape=[pltpu.HBM(b.shape, b.dtype) for b in bufs],
        input_output_aliases={t: t for t in range(n)},
        compiler_params=pltpu.CompilerParams(has_side_effects=EFFECT),
    )(*bufs, send_sems, recv_sems, after)


def gather_forward(bufs):
    n = len(bufs)

    def body(*refs):
        g_refs = refs[n:2 * n]
        send_sems, recv_sems = refs[2 * n:]
        x, y, c = _place()
        sibling = (x, y, 1 - c)
        chips = _other_chips(x, y)
        passed = []
        for t in range(n):
            for k, (qx, qy) in enumerate(chips):
                slab = g_refs[t].at[2 * qx + qy, c]
                fwd = pltpu.make_async_remote_copy(src_ref=slab, dst_ref=slab, send_sem=send_sems.at[3 * t + k],
                                                   recv_sem=recv_sems.at[3 * t + k], device_id=sibling,
                                                   device_id_type=MESH)
                fwd.start()
                passed.append(fwd)
        for t in range(n):
            for k, (qx, qy) in enumerate(chips):
                slab = g_refs[t].at[2 * qx + qy, 1 - c]
                pltpu.make_async_remote_copy(src_ref=slab, dst_ref=slab, send_sem=send_sems.at[3 * t + k],
                                             recv_sem=recv_sems.at[3 * t + k], device_id=sibling,
                                             device_id_type=MESH).wait_recv()
        for cp in passed:
            cp.wait_send()

    return pl.pallas_call(
        body, name="gather_forward",
        in_specs=[ANY] * n, out_specs=[ANY] * n,
        out_shape=[jax.ShapeDtypeStruct(b.shape, b.dtype) for b in bufs],
        input_output_aliases={t: t for t in range(n)},
        scratch_shapes=[pltpu.SemaphoreType.DMA((3 * n,)), pltpu.SemaphoreType.DMA((3 * n,))],
    )(*bufs)


def exchange_small(pack, after):
    def body(p_ref, after_ref, o_ref, send_sems, recv_sems, local_sem):
        x, y, c = _place()
        me = 4 * x + 2 * y + c
        mine = pltpu.make_async_copy(p_ref, o_ref.at[me], local_sem)
        mine.start()
        sends = []
        for k in range(1, 8):
            fx, fy, fc = (k >> 2) & 1, (k >> 1) & 1, k & 1
            peer = (x ^ fx, y ^ fy, c ^ fc)
            cp = pltpu.make_async_remote_copy(src_ref=p_ref, dst_ref=o_ref.at[me], send_sem=send_sems.at[k - 1],
                                              recv_sem=recv_sems.at[k - 1], device_id=peer, device_id_type=MESH)
            cp.start()
            sends.append(cp)
        for k in range(1, 8):
            fx, fy, fc = (k >> 2) & 1, (k >> 1) & 1, k & 1
            peer = (x ^ fx, y ^ fy, c ^ fc)
            slot = o_ref.at[4 * peer[0] + 2 * peer[1] + peer[2]]
            pltpu.make_async_remote_copy(src_ref=slot, dst_ref=slot, send_sem=send_sems.at[k - 1],
                                         recv_sem=recv_sems.at[k - 1], device_id=peer, device_id_type=MESH).wait_recv()
        for cp in sends:
            cp.wait_send()
        mine.wait()

    return pl.pallas_call(
        body, name="exchange_small",
        in_specs=[ANY, ANY], out_specs=ANY,
        out_shape=jax.ShapeDtypeStruct((8,) + pack.shape, pack.dtype),
        scratch_shapes=[pltpu.SemaphoreType.DMA((7,)), pltpu.SemaphoreType.DMA((7,)), pltpu.SemaphoreType.DMA],
    )(pack, after)


def sibling_swap_halves(grads):
    n = len(grads)

    def body(*refs):
        g_refs, o_refs = refs[:n], refs[n:2 * n]
        send_sems, recv_sems = refs[2 * n:]
        x, y, c = _place()
        cps = []
        for t in range(n):
            for p in range(N_CHIPS):
                cp = pltpu.make_async_remote_copy(src_ref=g_refs[t].at[p, 1 - c], dst_ref=o_refs[t].at[p],
                                                  send_sem=send_sems.at[N_CHIPS * t + p],
                                                  recv_sem=recv_sems.at[N_CHIPS * t + p],
                                                  device_id=(x, y, 1 - c), device_id_type=MESH)
                cp.start()
                cps.append(cp)
        for cp in cps:
            cp.wait()

    return pl.pallas_call(
        body, name="sibling_swap_halves",
        in_specs=[ANY] * n, out_specs=[ANY] * n,
        out_shape=[jax.ShapeDtypeStruct((N_CHIPS,) + g.shape[2:], g.dtype) for g in grads],
        scratch_shapes=[pltpu.SemaphoreType.DMA((N_CHIPS * n,)), pltpu.SemaphoreType.DMA((N_CHIPS * n,))],
    )(*grads)


def _row_tile(r):
    return max(t for t in range(16, 513, 16) if r % t == 0)


def add_own_half(g, other, c_arr):
    _, _, r, cols = g.shape
    tr = _row_tile(r)

    def body(c_ref, a_ref, b_ref, o_ref):
        o_ref[...] = (a_ref[...] + b_ref[...]).astype(BF16)

    return pl.pallas_call(
        body, name="add_own_half",
        grid_spec=pltpu.PrefetchScalarGridSpec(
            num_scalar_prefetch=1, grid=(N_CHIPS, r // tr),
            in_specs=[pl.BlockSpec((None, None, tr, cols), lambda p, i, c_ref: (p, c_ref[0], i, 0)),
                      pl.BlockSpec((None, tr, cols), lambda p, i, c_ref: (p, i, 0))],
            out_specs=pl.BlockSpec((None, tr, cols), lambda p, i, c_ref: (p, i, 0))),
        out_shape=jax.ShapeDtypeStruct((N_CHIPS, r, cols), BF16),
        compiler_params=_cparams("parallel", "parallel"),
    )(c_arr, g, other)


def scatter_start(partials):
    n = len(partials)

    def body(*refs):
        s_refs, l_refs = refs[:n], refs[n:2 * n]
        send_sems, recv_sems = refs[2 * n], refs[2 * n + 1]
        token = refs[-1]
        x, y, c = _place()
        for t in range(n):
            for k, (qx, qy) in enumerate(_other_chips(x, y)):
                pltpu.make_async_remote_copy(src_ref=s_refs[t].at[2 * qx + qy], dst_ref=l_refs[t].at[k],
                                             send_sem=send_sems.at[3 * t + k], recv_sem=recv_sems.at[3 * t + k],
                                             device_id=(qx, qy, c), device_id_type=MESH).start()
        token[...] = jnp.zeros_like(token)

    lands = [lax.empty((3,) + s.shape[1:], s.dtype) for s in partials]
    res = pl.pallas_call(
        body, name="scatter_start",
        in_specs=[HBM] * (2 * n),
        out_specs=[SEM, SEM] + [HBM] * (2 * n) + [pl.BlockSpec(memory_space=pltpu.VMEM)],
        out_shape=[pltpu.SemaphoreType.DMA((3 * n,)), pltpu.SemaphoreType.DMA((3 * n,))]
        + [pltpu.HBM(a.shape, a.dtype) for a in partials + lands] + [jax.ShapeDtypeStruct((8, LANES), F32)],
        input_output_aliases={t: 2 + t for t in range(2 * n)},
        compiler_params=pltpu.CompilerParams(has_side_effects=EFFECT),
    )(*[_hbm(a) for a in partials + lands])
    return res[0], res[1], list(res[2:2 + n]), list(res[2 + n:2 + 2 * n]), res[-1]


def scatter_wait(partials, lands, send_sems, recv_sems, after):
    n = len(partials)

    def body(*refs):
        s_refs, l_refs = refs[:n], refs[n:2 * n]
        send_sems, recv_sems = refs[2 * n], refs[2 * n + 1]
        x, y, c = _place()
        for t in range(n):
            for k, (qx, qy) in enumerate(_other_chips(x, y)):
                cp = pltpu.make_async_remote_copy(src_ref=s_refs[t].at[2 * qx + qy], dst_ref=l_refs[t].at[k],
                                                  send_sem=send_sems.at[3 * t + k], recv_sem=recv_sems.at[3 * t + k],
                                                  device_id=(qx, qy, c), device_id_type=MESH)
                cp.wait_send()
                cp.wait_recv()

    res = pl.pallas_call(
        body, name="scatter_wait",
        in_specs=[HBM] * (2 * n) + [SEM, SEM, ANY],
        out_specs=[HBM] * (2 * n),
        out_shape=[pltpu.HBM(a.shape, a.dtype) for a in partials + lands],
        input_output_aliases={t: t for t in range(2 * n)},
        compiler_params=pltpu.CompilerParams(has_side_effects=EFFECT),
    )(*partials, *lands, send_sems, recv_sems, after)
    return list(res[:n]), list(res[n:])


def sum_chips(own, parts, where):
    _, r, cols = own.shape
    tr = _row_tile(r)

    def body(w_ref, a_ref, p_ref, o_ref):
        acc = a_ref[...].astype(F32)
        for k in range(3):
            acc = acc + p_ref[k].astype(F32)
        o_ref[...] = acc

    return pl.pallas_call(
        body, name="sum_chips",
        grid_spec=pltpu.PrefetchScalarGridSpec(
            num_scalar_prefetch=1, grid=(r // tr,),
            in_specs=[pl.BlockSpec((None, tr, cols), lambda i, w_ref: (w_ref[0], i, 0)),
                      pl.BlockSpec((3, tr, cols), lambda i, w_ref: (0, i, 0))],
            out_specs=pl.BlockSpec((None, tr, cols), lambda i, w_ref: (w_ref[1], i, 0))),
        out_shape=jax.ShapeDtypeStruct((DEPTH, r, cols), F32),
        compiler_params=_cparams("parallel"),
    )(where, own, parts)


def sibling_share_layer(bufs):
    n = len(bufs)

    def body(*refs):
        o_refs = refs[n:2 * n]
        send_sems, recv_sems = refs[2 * n:]
        x, y, c = _place()
        cps = []
        for t in range(n):
            cp = pltpu.make_async_remote_copy(src_ref=o_refs[t].at[c], dst_ref=o_refs[t].at[c], send_sem=send_sems.at[t],
                                              recv_sem=recv_sems.at[t], device_id=(x, y, 1 - c), device_id_type=MESH)
            cp.start()
            cps.append(cp)
        for t in range(n):
            slot = o_refs[t].at[1 - c]
            pltpu.make_async_remote_copy(src_ref=slot, dst_ref=slot, send_sem=send_sems.at[t], recv_sem=recv_sems.at[t],
                                         device_id=(x, y, 1 - c), device_id_type=MESH).wait_recv()
        for cp in cps:
            cp.wait_send()

    return pl.pallas_call(
        body, name="sibling_share_layer",
        in_specs=[ANY] * n, out_specs=[ANY] * n,
        out_shape=[jax.ShapeDtypeStruct(b.shape, b.dtype) for b in bufs],
        input_output_aliases={t: t for t in range(n)},
        scratch_shapes=[pltpu.SemaphoreType.DMA((n,)), pltpu.SemaphoreType.DMA((n,))],
    )(*bufs)


SP_META = 2 * (N_META * D_MODEL // LANES)
SP_NORM = DEPTH * D_MODEL // LANES
SP_RB = DEPTH * N_BUCKETS
SP_SINK = DEPTH * ATT_HEADS
SP_CONV = DEPTH * 3 * BRANCH_WIDTH // LANES
SP_LOSS = 8
SP_ROWS = SP_META + 2 * SP_NORM + SP_RB + SP_SINK + SP_CONV + SP_LOSS


def sum_small(slots):
    half = SP_META // 2
    rb0 = SP_META + 2 * SP_NORM
    rest_rows = SP_ROWS - SP_META

    def body(s_ref, meta_ref, rest_ref):
        acc = s_ref[0]
        for d in range(1, 8):
            acc = acc + s_ref[d]
        meta_ref[...] = acc[0:half] + acc[half:SP_META]
        rest_ref[...] = acc[SP_META:]
        rest_ref[rb0 - SP_META:rb0 - SP_META + N_BUCKETS, :] = (
            acc[rb0:rb0 + N_BUCKETS] + acc[rb0 + N_BUCKETS:rb0 + 2 * N_BUCKETS])

    vm = pl.BlockSpec(memory_space=pltpu.VMEM)
    return pl.pallas_call(
        body, name="sum_small",
        in_specs=[vm], out_specs=[vm, vm],
        out_shape=[jax.ShapeDtypeStruct((half, LANES), F32), jax.ShapeDtypeStruct((rest_rows, LANES), F32)],
    )(slots)


def local_step(x, loss_target, meta_full, rel_bias, norm_pre, conv_w_full, attn_sinks, norm_post, weights_of, grads_done):
    nb, seq, _ = x.shape
    nc = seq // BLOCK + 1
    lp = nc * BLOCK
    rows = nb * lp
    pad = jnp.zeros((nb, PAD_FRONT, D_MODEL), F32)
    meta = jnp.broadcast_to(meta_full[None], (nb, N_META, D_MODEL))
    h0 = jnp.concatenate([pad, meta, x], axis=1).reshape(rows, D_MODEL)
    cosf, sinf = _rot_tables(lp)
    bkt = jnp.asarray(_bucket_table())

    acts = []
    h = h0
    for l in range(DEPTH):
        (w_in, w_br, w_out), zero = weights_of(l, h)
        hb, p_abc = norm_matmul(h, norm_pre[l][None] + zero, w_in, 0, N_ABC_TILES)
        p_m = matmul_cols(hb, w_in, N_ABC_TILES, N_M_TILES)
        br, states = mixers_fwd(p_abc, cosf, sinf, bkt, rel_bias, attn_sinks[l][None], conv_w_full[l], nb, nc)
        h_next = merge_fwd(h, br, p_m, w_br, w_out, norm_post[l][None])
        acts.append((h, hb, p_abc, p_m, br, states, w_in, w_br, w_out))
        h = h_next

    loss_part, d_h = loss_head(h, loss_target.reshape(nb * seq, D_MODEL), nb, nc)

    small = [None] * DEPTH
    for l in reversed(range(DEPTH)):
        h_in, hb, p_abc, p_m, br, states, w_in, w_br, w_out = acts[l]
        d_br, d_m, d_gpost, g_wbr, g_wout = merge_bwd(d_h, br, p_m, w_br, w_out, norm_post[l][None])
        d_abc, d_rb, d_sk, d_cw = mixers_bwd(p_abc, d_br, states, cosf, sinf, bkt, rel_bias,
                                             attn_sinks[l][None], conv_w_full[l], nb, nc)
        g_win = proj_wgrad(hb, d_abc, d_m)
        zero = grads_done(l, [g_win, g_wbr, g_wout])
        d_h, d_gpre = proj_dgrad(d_abc, d_m, w_in, h_in, norm_pre[l][None] + zero, d_h)
        small[l] = (d_gpre[0], d_gpost[0], d_rb, d_sk, d_cw[0:3])

    d_h3 = d_h.reshape(nb, lp, D_MODEL)
    d_x = d_h3[:, BLOCK:]
    d_meta = d_h3[:, PAD_FRONT:BLOCK]
    sp = jnp.concatenate([
        d_meta.reshape(-1, LANES),
        jnp.stack([small[l][0] for l in range(DEPTH)]).reshape(-1, LANES),
        jnp.stack([small[l][1] for l in range(DEPTH)]).reshape(-1, LANES),
        jnp.concatenate([small[l][2] for l in range(DEPTH)], axis=0),
        jnp.concatenate([small[l][3] for l in range(DEPTH)], axis=0),
        jnp.stack([small[l][4] for l in range(DEPTH)]).reshape(-1, LANES),
        loss_part], axis=0)
    return d_x, sp


def kernel(x, meta_tokens, rel_bias, norm_pre, w_in, conv_w, attn_sinks, w_branch, w_out, norm_post, loss_target, m_meta_tokens, m_rel_bias, m_norm_pre, m_w_in, m_conv_w, m_attn_sinks, m_w_branch, m_w_out, m_norm_post, v_meta_tokens, v_rel_bias, v_norm_pre, v_w_in, v_conv_w, v_attn_sinks, v_w_branch, v_w_out, v_norm_post):
    assert x.shape[0] == 2 and SP_META == 2 * N_META * D_MODEL // LANES
    px, py, pc = _place()
    chip = 2 * px + py

    c_arr = jnp.reshape(pc, (1,)).astype(jnp.int32)
    where = jnp.stack([chip, pc]).astype(jnp.int32)
    tr_ = lambda a: jnp.swapaxes(a, 1, 2)
    w3 = [tr_(w_in), w_branch.reshape(DEPTH, N_BRANCH * BRANCH_WIDTH, SHARD_D), w_out]
    halves = lambda a: a.reshape(2, a.shape[0] // 2, a.shape[1])

    def as_weights(bufs):
        a_in, a_br, a_out = bufs
        return (a_in.reshape(PROJ_WIDTH, D_MODEL), a_br.reshape(N_CHIPS, N_BRANCH, BRANCH_WIDTH, SHARD_D),
                a_out.reshape(D_MODEL, D_MODEL))

    side = jnp.concatenate([meta_tokens.reshape(-1), conv_w.reshape(-1)]).reshape(-1, LANES)
    side = jnp.concatenate([side, jnp.zeros((40 - side.shape[0], LANES), F32)], axis=0)
    side_all = exchange_small(side, side)
    side_chips = side_all[0::2]
    n_meta_rows = N_META * SHARD_D // LANES
    meta_full = jnp.moveaxis(side_chips[:, :n_meta_rows].reshape(N_CHIPS, N_META, SHARD_D), 0, 1).reshape(N_META, D_MODEL)
    conv_full = jnp.moveaxis(side_chips[:, n_meta_rows:n_meta_rows + 6].reshape(N_CHIPS, DEPTH, 3, LANES), 0, 2).reshape(DEPTH, 3, BRANCH_WIDTH)

    slots = [[_own_slot(halves(w[l].astype(BF16)), chip) for w in w3] for l in range(DEPTH)]
    gathered0 = gather_weight_shards(slots[0], side_all)
    send1, recv1, flying1, started1 = gather_start(slots[1], gathered0[0])

    def weights_of(l, h):
        if l == 0:
            return as_weights(gathered0), started1[0:1, 0:1]
        landed = gather_forward(gather_wait(flying1, send1, recv1, h))
        return as_weights(landed), jnp.zeros((1, 1), F32)

    reduced = [None] * DEPTH
    flying = {}

    def finish_reduce(l, after):
        partials, parts = scatter_wait(*flying[l], after)
        reduced[l] = sibling_share_layer([sum_chips(a, p, where) for a, p in zip(partials, parts)])

    def grads_done(l, grads):
        if l == 0:
            finish_reduce(1, grads[0])
        full = [g.reshape(N_CHIPS, 2, g.size // (2 * N_CHIPS * g.shape[-1]), g.shape[-1]) for g in grads]
        others = sibling_swap_halves(full)
        send, recv, thru, lands, started = scatter_start([add_own_half(g, o, c_arr) for g, o in zip(full, others)])
        flying[l] = (thru, lands, send, recv)
        return started[0:1, 0:1]

    d_x, sp = local_step(x, loss_target, meta_full, rel_bias, norm_pre, conv_full, attn_sinks, norm_post,
                         weights_of, grads_done)
    finish_reduce(0, sp)

    meta_rows, rest = sum_small(exchange_small(sp, reduced[0][0]))
    o = 0
    g_meta_full = meta_rows.reshape(N_META, D_MODEL)
    g_norm_pre = rest[o:o + SP_NORM].reshape(DEPTH, D_MODEL); o += SP_NORM
    g_norm_post = rest[o:o + SP_NORM].reshape(DEPTH, D_MODEL); o += SP_NORM
    g_rel_bias = rest[o:o + N_BUCKETS, :ATT_HEADS]; o += SP_RB
    g_sinks = rest[o:o + SP_SINK, 0].reshape(DEPTH, ATT_HEADS); o += SP_SINK
    g_conv_full = rest[o:o + SP_CONV].reshape(DEPTH, 3, BRANCH_WIDTH); o += SP_CONV
    loss = rest[o, 0]
    g_meta = lax.dynamic_slice_in_dim(g_meta_full, chip * SHARD_D, SHARD_D, axis=1)
    g_conv = lax.dynamic_slice_in_dim(g_conv_full, chip * LANES, LANES, axis=2)

    m3 = [tr_(m_w_in), m_w_branch.reshape(w3[1].shape), m_w_out]
    v3 = [tr_(v_w_in), v_w_branch.reshape(w3[1].shape), v_w_out]
    big = []
    for t in range(3):
        acc = None
        for l in reversed(range(DEPTH)):
            acc = adamw_layer(w3[t], reduced[l][t].reshape(w3[t].shape[1:]), m3[t], v3[t], l, acc)
        big.append(acc)
    g_in, *u_in = [tr_(a) for a in big[0]]
    g_br, *u_br = [a.reshape(w_branch.shape) for a in big[1]]
    g_out, *u_out = big[2]
    to2 = lambda a: a.reshape(-1, a.shape[-1])
    smalls = [(meta_tokens, g_meta, m_meta_tokens, v_meta_tokens),
              (rel_bias, g_rel_bias, m_rel_bias, v_rel_bias),
              (norm_pre, g_norm_pre, m_norm_pre, v_norm_pre),
              (to2(conv_w), to2(g_conv), to2(m_conv_w), to2(v_conv_w)),
              (attn_sinks, g_sinks, m_attn_sinks, v_attn_sinks),
              (norm_post, g_norm_post, m_norm_post, v_norm_post)]
    u_meta, u_rb, u_npre, u_conv, u_sink, u_npost = adamw_small(smalls)
    u_conv = tuple(a.reshape(conv_w.shape) for a in u_conv)

    grads = [g_meta, g_rel_bias, g_norm_pre, g_in, g_conv, g_sinks, g_br, g_out, g_norm_post]
    upd = [u_meta, u_rb, u_npre, u_in, u_conv, u_sink, u_br, u_out, u_npost]
    return (loss, d_x, *grads, *[u[0] for u in upd], *[u[1] for u in upd], *[u[2] for u in upd])
```

```python
import functools
import math

import numpy as np
import jax
import jax.numpy as jnp
from jax import lax
from jax.experimental import pallas as pl
from jax.experimental.pallas import tpu as pltpu

F32 = jnp.float32
BF16 = jnp.bfloat16
MESH = pl.DeviceIdType.MESH

D_MODEL = 1024
DEPTH = 2
N_META = 16
BLOCK = 128
PAD_FRONT = BLOCK - N_META
ATT_HEADS = 8
ATT_HEAD_DIM = 64
N_BUCKETS = 32
MAX_EXACT = 16
MAX_DISTANCE = 128
RET_HEADS = 4
ROT_BASE = 10000.0
N_BRANCH = 3
BRANCH_WIDTH = 512
PROJ_WIDTH = 8448
ABC_WIDTH = 5376
MERGE_WIDTH = N_BRANCH * D_MODEL
RMS_EPS = 1e-6
GN_EPS = 1e-6
NEG_INF = -1e30
ATT_SCALE = ATT_HEAD_DIM ** -0.5
RET_SCALE = BLOCK ** -0.5
LOG_GAMMA = tuple(math.log1p(-(2.0 ** (-5.0 - h))) for h in range(RET_HEADS))

C_AQ, C_AK, C_AV, C_AG = 0, 512, 640, 768
C_RQ, C_RK, C_RV, C_RG = 1280, 1792, 2304, 2816
C_CB, C_CC, C_CX, C_CG = 3328, 3840, 4352, 4864

ADAM_LR = 0.001
ADAM_B1 = 0.9
ADAM_B2 = 0.999
ADAM_EPS = 1e-08
ADAM_WD = 0.01
ADAM_STEP = 10

N_CHIPS = 4
SHARD_IN = PROJ_WIDTH // N_CHIPS
SHARD_D = D_MODEL // N_CHIPS
LANES = 128
PACK_IN = D_MODEL * SHARD_IN
PACK_BR = N_BRANCH * BRANCH_WIDTH * SHARD_D
PACK_OUT = SHARD_D * D_MODEL
PACK_ROWS = (PACK_IN + PACK_BR + PACK_OUT) // LANES

VMEM_LIMIT = 56 * 1024 * 1024
COL_TILE = 768
ROW_TILE = 1088


def _cparams(*sem):
    return pltpu.CompilerParams(dimension_semantics=sem, vmem_limit_bytes=VMEM_LIMIT)


def _nt(a, b):
    return lax.dot_general(a, b, (((1,), (1,)), ((), ())), preferred_element_type=F32)


def _tn(a, b):
    return lax.dot_general(a, b, (((0,), (0,)), ((), ())), preferred_element_type=F32)


def _nn(a, b):
    return jnp.dot(a, b, preferred_element_type=F32)


def _sigmoid(x):
    return 0.5 * jnp.tanh(0.5 * x) + 0.5


def _silu(x):
    return x * _sigmoid(x)


def _dsilu(x):
    s = _sigmoid(x)
    return s * (1.0 + x * (1.0 - s))


def _bucket_table():
    r = np.arange(BLOCK)[:, None]
    c = np.arange(2 * BLOCK)[None, :]
    n = np.maximum(BLOCK + r - c, 0)
    nf = np.maximum(n, 1).astype(np.float32)
    large = MAX_EXACT + (np.log(nf / MAX_EXACT) / math.log(MAX_DISTANCE / MAX_EXACT)
                         * (N_BUCKETS - MAX_EXACT)).astype(np.int32)
    large = np.minimum(large, N_BUCKETS - 1)
    return np.where(n < MAX_EXACT, n, large).astype(np.int32)


def _rot_tables(lp):
    half = BLOCK // 2
    pos = (jnp.arange(lp) - PAD_FRONT).astype(F32)
    theta = 1.0 / (ROT_BASE ** jnp.linspace(0.0, 1.0, half, dtype=F32))
    ang = pos[:, None] * theta[None, :]
    cos, sin = jnp.cos(ang), jnp.sin(ang)
    return jnp.concatenate([cos, cos], axis=1), jnp.concatenate([-sin, sin], axis=1)


def norm_matmul(x2d, g, w, col0_blocks, n_col_blocks):
    t = x2d.shape[0]
    tm = ROW_TILE if t % ROW_TILE == 0 else BLOCK

    def body(x_ref, g_ref, w_ref, hb_ref, o_ref):
        @pl.when(pl.program_id(1) == 0)
        def _():
            x = x_ref[...]
            r = lax.rsqrt(jnp.mean(x * x, axis=-1, keepdims=True) + RMS_EPS)
            hb_ref[...] = (x * r * g_ref[...]).astype(BF16)

        o_ref[...] = _nt(hb_ref[...], w_ref[...]).astype(BF16)

    return pl.pallas_call(
        body, name="norm_matmul",
        grid=(t // tm, n_col_blocks),
        in_specs=[pl.BlockSpec((tm, D_MODEL), lambda i, j: (i, 0)),
                  pl.BlockSpec((1, D_MODEL), lambda i, j: (0, 0)),
                  pl.BlockSpec((COL_TILE, D_MODEL), lambda i, j: (j + col0_blocks, 0))],
        out_specs=[pl.BlockSpec((tm, D_MODEL), lambda i, j: (i, 0)),
                   pl.BlockSpec((tm, COL_TILE), lambda i, j: (i, j))],
        out_shape=[jax.ShapeDtypeStruct((t, D_MODEL), BF16),
                   jax.ShapeDtypeStruct((t, n_col_blocks * COL_TILE), BF16)],
        compiler_params=_cparams("parallel", "arbitrary"),
    )(x2d, g, w)


def matmul_cols(a, w, col0_blocks, n_col_blocks):
    t, k = a.shape
    tm = ROW_TILE if t % ROW_TILE == 0 else BLOCK

    def body(a_ref, w_ref, o_ref):
        o_ref[...] = _nt(a_ref[...], w_ref[...]).astype(BF16)

    return pl.pallas_call(
        body, name="matmul_cols",
        grid=(t // tm, n_col_blocks),
        in_specs=[pl.BlockSpec((tm, k), lambda i, j: (i, 0)),
                  pl.BlockSpec((COL_TILE, k), lambda i, j: (j + col0_blocks, 0))],
        out_specs=pl.BlockSpec((tm, COL_TILE), lambda i, j: (i, j)),
        out_shape=jax.ShapeDtypeStruct((t, n_col_blocks * COL_TILE), BF16),
        compiler_params=_cparams("parallel", "arbitrary"),
    )(a, w)


class _Widened:
    def __init__(self, ref):
        self.ref = ref

    def __getitem__(self, idx):
        return self.ref[idx].astype(F32)


def _build_bias(bkt_ref, rb_ref, bias_s):
    bkt = bkt_ref[...]
    for h in range(ATT_HEADS):
        acc = jnp.zeros((BLOCK, 2 * BLOCK), F32)
        for b in range(N_BUCKETS):
            acc = jnp.where(bkt == b, rb_ref[b, h], acc)
        bias_s[h] = acc


def _band_mask(n):
    r = lax.broadcasted_iota(jnp.int32, (BLOCK, 2 * BLOCK), 0)
    c = lax.broadcasted_iota(jnp.int32, (BLOCK, 2 * BLOCK), 1)
    key_pos = (n - 1) * BLOCK + c
    return (c > r) & (c <= r + BLOCK) & (key_pos >= PAD_FRONT)


def _split_heads(kv, kh):
    lane = lax.broadcasted_iota(jnp.int32, kv.shape, 1)
    if kh == 0:
        lo = jnp.where(lane < ATT_HEAD_DIM, kv, 0.0)
        hi = pltpu.roll(lo, ATT_HEAD_DIM, 1)
    else:
        hi = jnp.where(lane >= ATT_HEAD_DIM, kv, 0.0)
        lo = pltpu.roll(hi, ATT_HEAD_DIM, 1)
    return lo, hi


def _merge_heads(acc_lo, acc_hi, kh):
    lane = lax.broadcasted_iota(jnp.int32, acc_lo.shape, 1)
    if kh == 0:
        return jnp.where(lane < ATT_HEAD_DIM, acc_lo + pltpu.roll(acc_hi, ATT_HEAD_DIM, 1), 0.0)
    return jnp.where(lane >= ATT_HEAD_DIM, acc_hi + pltpu.roll(acc_lo, ATT_HEAD_DIM, 1), 0.0)


def _softmax_sink(q2b, kxb, bias_h, mask, sink_h):
    return _softmax_of(_nt(q2b, kxb), bias_h, mask, sink_h)


def _softmax_of(qk, bias_h, mask, sink_h):
    s = qk * ATT_SCALE + bias_h
    s = jnp.where(mask, s, NEG_INF)
    m = jnp.maximum(jnp.max(s, axis=-1, keepdims=True), sink_h)
    p = jnp.exp(s - m)
    es = jnp.exp(sink_h - m)
    inv = 1.0 / (jnp.sum(p, axis=-1, keepdims=True) + es)
    return p * inv, es * inv


def _rot(t, cosf, sinf):
    return t * cosf + pltpu.roll(t, BLOCK // 2, 1) * sinf


def _rot_t(d, cosf, sinf):
    return d * cosf + pltpu.roll(d * sinf, BLOCK // 2, 1)


def _decay_tables(h):
    lg = LOG_GAMMA[h]
    i = lax.broadcasted_iota(jnp.int32, (BLOCK, BLOCK), 0)
    j = lax.broadcasted_iota(jnp.int32, (BLOCK, BLOCK), 1)
    diff = (i - j).astype(F32)
    dm = jnp.where(diff >= 0, jnp.exp(diff * lg), 0.0)
    row = lax.broadcasted_iota(jnp.int32, (BLOCK, 1), 0).astype(F32)
    zeta = jnp.exp((BLOCK - 1 - row) * lg)
    xi = jnp.exp((row + 1.0) * lg)
    return dm, zeta, xi, math.exp(BLOCK * lg)


def _valid_col(n):
    row = lax.broadcasted_iota(jnp.int32, (BLOCK, 1), 0)
    return ((n * BLOCK + row) >= PAD_FRONT).astype(F32)


def _shift_down(cur, prev, k):
    row = lax.broadcasted_iota(jnp.int32, cur.shape, 0)
    return jnp.where(row >= k, pltpu.roll(cur, k, 0), pltpu.roll(prev, k, 0))


def _shift_up(cur, nxt, k):
    row = lax.broadcasted_iota(jnp.int32, cur.shape, 0)
    return jnp.where(row < BLOCK - k, pltpu.roll(cur, BLOCK - k, 0), pltpu.roll(nxt, BLOCK - k, 0))


def mixers_fwd(proj, cosf, sinf, bkt, rel_bias, sinks, conv_w, nb, nc):
    def body(p_ref, cos_ref, sin_ref, bkt_ref, rb_ref, sk_ref, cw_ref, br_ref, st_ref,
             bias_s, kv_s, state_s, u_s):
        p_ref = _Widened(p_ref)
        n = pl.program_id(0)

        @pl.when(n == 0)
        def _():
            _build_bias(bkt_ref, rb_ref, bias_s)
            kv_s[:, 0:BLOCK, :] = jnp.zeros((nb, BLOCK, 2 * BLOCK), F32)
            state_s[...] = jnp.zeros_like(state_s)
            u_s[...] = jnp.zeros_like(u_s)

        valid = _valid_col(n)
        mask = _band_mask(n)
        ex = range(nb)

        for b in ex:
            kv_s[b, BLOCK:2 * BLOCK, :] = p_ref[b, :, C_AK:C_AK + 2 * BLOCK]
        for kh in range(2):
            ks = [[t.astype(BF16) for t in _split_heads(kv_s[b, :, 0:BLOCK], kh)] for b in ex]
            vs = [[t.astype(BF16) for t in _split_heads(kv_s[b, :, BLOCK:2 * BLOCK], kh)] for b in ex]
            pairs = [(b, 2 * kh + jj) for jj in range(2) for b in ex]
            subs = [(b, j, x) for (b, j) in pairs for x in range(2)]
            qb_ = {(b, j): p_ref[b, :, C_AQ + BLOCK * j:C_AQ + BLOCK * (j + 1)].astype(BF16) for (b, j) in pairs}
            qk_ = {(b, j, x): _nt(qb_[(b, j)], ks[b][x]) for (b, j, x) in subs}
            pb_ = {}
            for u in subs:
                h = 2 * u[1] + u[2]
                pb_[u] = _softmax_of(qk_[u], bias_s[h], mask, sk_ref[0, h])[0].astype(BF16)
            o_ = {u: _nn(pb_[u], vs[u[0]][u[2]]) for u in subs}
            for (b, j) in pairs:
                gate = p_ref[b, :, C_AG + BLOCK * j:C_AG + BLOCK * (j + 1)]
                br_ref[b, :, BLOCK * j:BLOCK * (j + 1)] = ((o_[(b, j, 0)] + o_[(b, j, 1)]) * _silu(gate)).astype(BF16)
        for b in ex:
            kv_s[b, 0:BLOCK, :] = kv_s[b, BLOCK:2 * BLOCK, :]

        cosv = cos_ref[...]
        sinv = sin_ref[...]
        tabs = [_decay_tables(h) for h in range(RET_HEADS)]
        units = [(b, h) for h in range(RET_HEADS) for b in ex]
        sl = lambda c0, h: slice(c0 + BLOCK * h, c0 + BLOCK * (h + 1))
        q_, k_, v_, sp_ = {}, {}, {}, {}
        for u in units:
            b, h = u
            q_[u] = _rot(p_ref[b, :, sl(C_RQ, h)], cosv, sinv).astype(BF16)
            k_[u] = (_rot(p_ref[b, :, sl(C_RK, h)], cosv, sinv) * RET_SCALE * valid).astype(BF16)
            v_[u] = p_ref[b, :, sl(C_RV, h)]
            sp_[u] = state_s[b, h]
            st_ref[b, 0, h] = sp_[u]
        qk_ = {u: _nt(q_[u], k_[u]) for u in units}
        qs_ = {u: _nn(q_[u], sp_[u].astype(BF16)) for u in units}
        kv_ = {u: _tn(k_[u], (v_[u] * tabs[u[1]][1]).astype(BF16)) for u in units}
        a_ = {u: (qk_[u] * tabs[u[1]][0]).astype(BF16) for u in units}
        av_ = {u: _nn(a_[u], v_[u].astype(BF16)) for u in units}
        for u in units:
            b, h = u
            o = av_[u] + tabs[h][2] * qs_[u]
            mu = jnp.mean(o, axis=-1, keepdims=True)
            var = jnp.mean(jnp.square(o - mu), axis=-1, keepdims=True)
            oh = (o - mu) * lax.rsqrt(var + GN_EPS)
            gate = p_ref[b, :, sl(C_RG, h)]
            br_ref[b, :, BRANCH_WIDTH + BLOCK * h:BRANCH_WIDTH + BLOCK * (h + 1)] = (oh * _silu(gate)).astype(BF16)
            state_s[b, h] = tabs[h][3] * sp_[u] + kv_[u]

        for b in ex:
            u = p_ref[b, :, C_CC:C_CC + BRANCH_WIDTH] * p_ref[b, :, C_CX:C_CX + BRANCH_WIDTH] * valid
            u_prev = u_s[b]
            y = (cw_ref[0:1, :] * _shift_down(u, u_prev, 2) + cw_ref[1:2, :] * _shift_down(u, u_prev, 1)
                 + cw_ref[2:3, :] * u)
            yc = p_ref[b, :, C_CB:C_CB + BRANCH_WIDTH] * y * _silu(p_ref[b, :, C_CG:C_CG + BRANCH_WIDTH])
            br_ref[b, :, 2 * BRANCH_WIDTH:3 * BRANCH_WIDTH] = yc.astype(BF16)
            u_s[b] = u

    lp = nc * BLOCK
    smem = pl.BlockSpec(memory_space=pltpu.SMEM)
    br, states = pl.pallas_call(
        body, name="mixers_fwd",
        grid=(nc,),
        in_specs=[pl.BlockSpec((nb, BLOCK, ABC_WIDTH), lambda n: (0, n, 0)),
                  pl.BlockSpec((BLOCK, BLOCK), lambda n: (n, 0)),
                  pl.BlockSpec((BLOCK, BLOCK), lambda n: (n, 0)),
                  pl.BlockSpec((BLOCK, 2 * BLOCK), lambda n: (0, 0)),
                  smem, smem,
                  pl.BlockSpec((3, BRANCH_WIDTH), lambda n: (0, 0))],
        out_specs=[pl.BlockSpec((nb, BLOCK, N_BRANCH * BRANCH_WIDTH), lambda n: (0, n, 0)),
                   pl.BlockSpec((nb, 1, RET_HEADS, BLOCK, BLOCK), lambda n: (0, n, 0, 0, 0))],
        out_shape=[jax.ShapeDtypeStruct((nb, lp, N_BRANCH * BRANCH_WIDTH), BF16),
                   jax.ShapeDtypeStruct((nb, nc, RET_HEADS, BLOCK, BLOCK), F32)],
        scratch_shapes=[pltpu.VMEM((ATT_HEADS, BLOCK, 2 * BLOCK), F32),
                        pltpu.VMEM((nb, 2 * BLOCK, 2 * BLOCK), F32),
                        pltpu.VMEM((nb, RET_HEADS, BLOCK, BLOCK), F32),
                        pltpu.VMEM((nb, BLOCK, BRANCH_WIDTH), F32)],
        compiler_params=_cparams("arbitrary"),
    )(proj.reshape(nb, lp, ABC_WIDTH), cosf, sinf, bkt, rel_bias, sinks, conv_w)
    return br.reshape(nb * lp, N_BRANCH * BRANCH_WIDTH), states


def mixers_bwd(proj, d_br, states, cosf, sinf, bkt, rel_bias, sinks, conv_w, nb, nc):
    def body(p_ref, kvp_ref, cp_ref, dbr_ref, st_ref, cos_ref, sin_ref, bkt_ref, rb_ref, sk_ref, cw_ref,
             dp_ref, drb_ref, dsk_ref, dcw_ref,
             bias_s, dbias_s, dkv_s, g_s, dy_s):
        p_ref, kvp_ref, cp_ref, dbr_ref = [_Widened(r) for r in (p_ref, kvp_ref, cp_ref, dbr_ref)]
        step = pl.program_id(0)
        n = nc - 1 - step
        ex = range(nb)

        @pl.when(step == 0)
        def _():
            _build_bias(bkt_ref, rb_ref, bias_s)
            dbias_s[...] = jnp.zeros_like(dbias_s)
            dsk_ref[...] = jnp.zeros_like(dsk_ref)
            dcw_ref[...] = jnp.zeros_like(dcw_ref)
            drb_ref[...] = jnp.zeros_like(drb_ref)
            dkv_s[...] = jnp.zeros_like(dkv_s)
            g_s[...] = jnp.zeros_like(g_s)
            dy_s[...] = jnp.zeros_like(dy_s)

        valid = _valid_col(n)
        mask = _band_mask(n)
        has_prev = (n > 0).astype(F32)

        k_all, v_all = [], []
        for b in ex:
            kv_prev = kvp_ref[b] * has_prev
            kv_cur = p_ref[b, :, C_AK:C_AK + 2 * BLOCK]
            k_all.append(jnp.concatenate([kv_prev[:, 0:BLOCK], kv_cur[:, 0:BLOCK]], axis=0))
            v_all.append(jnp.concatenate([kv_prev[:, BLOCK:], kv_cur[:, BLOCK:]], axis=0))
        zero2 = jnp.zeros((2 * BLOCK, BLOCK), F32)
        dk_tot = [zero2 for _ in ex]
        dv_tot = [zero2 for _ in ex]
        for kh in range(2):
            ks = [[t.astype(BF16) for t in _split_heads(k_all[b], kh)] for b in ex]
            vs = [[t.astype(BF16) for t in _split_heads(v_all[b], kh)] for b in ex]
            pairs = [(b, 2 * kh + jj) for jj in range(2) for b in ex]
            subs = [(b, j, x) for (b, j) in pairs for x in range(2)]
            qb_, gate_, dya_, do2_ = {}, {}, {}, {}
            for w in pairs:
                b, j = w
                qb_[w] = p_ref[b, :, C_AQ + BLOCK * j:C_AQ + BLOCK * (j + 1)].astype(BF16)
                gate_[w] = p_ref[b, :, C_AG + BLOCK * j:C_AG + BLOCK * (j + 1)]
                dya_[w] = dbr_ref[b, :, BLOCK * j:BLOCK * (j + 1)]
                do2_[w] = (dya_[w] * _silu(gate_[w])).astype(BF16)
            qk_ = {(b, j, x): _nt(qb_[(b, j)], ks[b][x]) for (b, j, x) in subs}
            dpm_ = {(b, j, x): _nt(do2_[(b, j)], vs[b][x]) for (b, j, x) in subs}
            pb_, dsb_ = {}, {}
            for u in subs:
                b, j, x = u
                h = 2 * j + x
                p, p_sink = _softmax_of(qk_[u], bias_s[h], mask, sk_ref[0, h])
                pb_[u] = p.astype(BF16)
                delta = jnp.sum(p * dpm_[u], axis=-1, keepdims=True)
                ds = p * (dpm_[u] - delta)
                dbias_s[h] += ds
                dsk_ref[h:h + 1, :] += jnp.broadcast_to(
                    jnp.sum(-p_sink * delta, axis=0, keepdims=True), (1, BLOCK))
                dsb_[u] = ds.astype(BF16)
            o_ = {u: _nn(pb_[u], vs[u[0]][u[2]]) for u in subs}
            dq_ = {u: _nn(dsb_[u], ks[u[0]][u[2]]) for u in subs}
            dkm_ = {u: _tn(dsb_[u], qb_[(u[0], u[1])]) for u in subs}
            dvm_ = {u: _tn(pb_[u], do2_[(u[0], u[1])]) for u in subs}
            for w in pairs:
                b, j = w
                o2 = o_[(b, j, 0)] + o_[(b, j, 1)]
                dq2 = (dq_[(b, j, 0)] + dq_[(b, j, 1)]) * ATT_SCALE
                dp_ref[b, :, C_AQ + BLOCK * j:C_AQ + BLOCK * (j + 1)] = dq2.astype(BF16)
                dp_ref[b, :, C_AG + BLOCK * j:C_AG + BLOCK * (j + 1)] = (
                    dya_[w] * o2 * _dsilu(gate_[w])).astype(BF16)
            for b in ex:
                j0, j1 = 2 * kh, 2 * kh + 1
                dk_lo = (dkm_[(b, j0, 0)] + dkm_[(b, j1, 0)]) * ATT_SCALE
                dk_hi = (dkm_[(b, j0, 1)] + dkm_[(b, j1, 1)]) * ATT_SCALE
                dk_tot[b] = dk_tot[b] + _merge_heads(dk_lo, dk_hi, kh)
                dv_tot[b] = dv_tot[b] + _merge_heads(dvm_[(b, j0, 0)] + dvm_[(b, j1, 0)],
                                                     dvm_[(b, j0, 1)] + dvm_[(b, j1, 1)], kh)
        for b in ex:
            dp_ref[b, :, C_AK:C_AK + BLOCK] = (dk_tot[b][BLOCK:, :] + dkv_s[b, :, 0:BLOCK]).astype(BF16)
            dp_ref[b, :, C_AV:C_AV + BLOCK] = (dv_tot[b][BLOCK:, :] + dkv_s[b, :, BLOCK:]).astype(BF16)
            dkv_s[b, :, 0:BLOCK] = dk_tot[b][0:BLOCK, :]
            dkv_s[b, :, BLOCK:] = dv_tot[b][0:BLOCK, :]

        cosv = cos_ref[...]
        sinv = sin_ref[...]
        tabs = [_decay_tables(h) for h in range(RET_HEADS)]
        units = [(b, h) for h in range(RET_HEADS) for b in ex]
        sl = lambda c0, h: slice(c0 + BLOCK * h, c0 + BLOCK * (h + 1))
        q_, k_, v_, vb_, sp_ = {}, {}, {}, {}, {}
        for u in units:
            b, h = u
            q_[u] = _rot(p_ref[b, :, sl(C_RQ, h)], cosv, sinv).astype(BF16)
            k_[u] = (_rot(p_ref[b, :, sl(C_RK, h)], cosv, sinv) * RET_SCALE * valid).astype(BF16)
            v_[u] = p_ref[b, :, sl(C_RV, h)]
            vb_[u] = v_[u].astype(BF16)
            sp_[u] = st_ref[b, 0, h].astype(BF16)
        qk_ = {u: _nt(q_[u], k_[u]) for u in units}
        qs_ = {u: _nn(q_[u], sp_[u]) for u in units}
        a_ = {u: (qk_[u] * tabs[u[1]][0]).astype(BF16) for u in units}
        av_ = {u: _nn(a_[u], vb_[u]) for u in units}
        dob_, dxo_ = {}, {}
        for u in units:
            b, h = u
            xi = tabs[h][2]
            o = av_[u] + xi * qs_[u]
            mu = jnp.mean(o, axis=-1, keepdims=True)
            var = jnp.mean(jnp.square(o - mu), axis=-1, keepdims=True)
            rstd = lax.rsqrt(var + GN_EPS)
            oh = (o - mu) * rstd
            gate = p_ref[b, :, sl(C_RG, h)]
            d_yr = dbr_ref[b, :, BRANCH_WIDTH + BLOCK * h:BRANCH_WIDTH + BLOCK * (h + 1)]
            dp_ref[b, :, sl(C_RG, h)] = (d_yr * oh * _dsilu(gate)).astype(BF16)
            doh = d_yr * _silu(gate)
            do = rstd * (doh - jnp.mean(doh, axis=-1, keepdims=True)
                         - oh * jnp.mean(doh * oh, axis=-1, keepdims=True))
            dob_[u] = do.astype(BF16)
            dxo_[u] = (do * xi).astype(BF16)
        dov_ = {u: _nt(dob_[u], vb_[u]) for u in units}
        dv1_ = {u: _tn(a_[u], dob_[u]) for u in units}
        dq1_ = {u: _nt(dxo_[u], sp_[u]) for u in units}
        gq_ = {u: _tn(q_[u], dxo_[u]) for u in units}
        da_, gb_, zv_ = {}, {}, {}
        for u in units:
            b, h = u
            da_[u] = (dov_[u] * tabs[h][0]).astype(BF16)
            g_next = g_s[b, h]
            gb_[u] = g_next.astype(BF16)
            zv_[u] = (v_[u] * tabs[h][1]).astype(BF16)
            g_s[b, h] = tabs[h][3] * g_next + gq_[u]
        dq2_ = {u: _nn(da_[u], k_[u]) for u in units}
        dk1_ = {u: _tn(da_[u], q_[u]) for u in units}
        dk2_ = {u: _nt(zv_[u], gb_[u]) for u in units}
        dv2_ = {u: _nn(k_[u], gb_[u]) for u in units}
        for u in units:
            b, h = u
            dp_ref[b, :, sl(C_RQ, h)] = _rot_t(dq2_[u] + dq1_[u], cosv, sinv).astype(BF16)
            dp_ref[b, :, sl(C_RK, h)] = _rot_t((dk1_[u] + dk2_[u]) * (RET_SCALE * valid), cosv, sinv).astype(BF16)
            dp_ref[b, :, sl(C_RV, h)] = (dv1_[u] + tabs[h][1] * dv2_[u]).astype(BF16)

        w0, w1, w2 = cw_ref[0:1, :], cw_ref[1:2, :], cw_ref[2:3, :]
        for b in ex:
            cb = p_ref[b, :, C_CB:C_CB + BRANCH_WIDTH]
            cc = p_ref[b, :, C_CC:C_CC + BRANCH_WIDTH]
            cx = p_ref[b, :, C_CX:C_CX + BRANCH_WIDTH]
            cg = p_ref[b, :, C_CG:C_CG + BRANCH_WIDTH]
            u = cc * cx * valid
            u_prev = (cp_ref[b, :, 0:BRANCH_WIDTH] * cp_ref[b, :, BRANCH_WIDTH:2 * BRANCH_WIDTH]
                      * (_valid_col(n - 1) * has_prev))
            u1 = _shift_down(u, u_prev, 1)
            u2 = _shift_down(u, u_prev, 2)
            y = w0 * u2 + w1 * u1 + w2 * u
            d_yc = dbr_ref[b, :, 2 * BRANCH_WIDTH:3 * BRANCH_WIDTH]
            sg = _silu(cg)
            dp_ref[b, :, C_CB:C_CB + BRANCH_WIDTH] = (d_yc * y * sg).astype(BF16)
            dp_ref[b, :, C_CG:C_CG + BRANCH_WIDTH] = (d_yc * cb * y * _dsilu(cg)).astype(BF16)
            dy = d_yc * cb * sg
            dy_next = dy_s[b]
            du = (w2 * dy + w1 * _shift_up(dy, dy_next, 1) + w0 * _shift_up(dy, dy_next, 2)) * valid
            dp_ref[b, :, C_CC:C_CC + BRANCH_WIDTH] = (du * cx).astype(BF16)
            dp_ref[b, :, C_CX:C_CX + BRANCH_WIDTH] = (du * cc).astype(BF16)
            dcw_ref[0:1, :] += jnp.sum(dy * u2, axis=0, keepdims=True)
            dcw_ref[1:2, :] += jnp.sum(dy * u1, axis=0, keepdims=True)
            dcw_ref[2:3, :] += jnp.sum(dy * u, axis=0, keepdims=True)
            dy_s[b] = dy

        @pl.when(step == nc - 1)
        def _():
            bkt = bkt_ref[...]
            row = lax.broadcasted_iota(jnp.int32, (N_BUCKETS, BLOCK), 0)
            lane = lax.broadcasted_iota(jnp.int32, (N_BUCKETS, BLOCK), 1)

            def one_bucket(bk, acc):
                sel = bkt == bk
                for h in range(ATT_HEADS):
                    t = jnp.where(sel, dbias_s[h], 0.0)
                    s = jnp.sum(jnp.sum(t, axis=1, keepdims=True), axis=0, keepdims=True)
                    acc = acc + jnp.where((row == bk) & (lane == h), jnp.broadcast_to(s, acc.shape), 0.0)
                return acc

            drb_ref[...] = lax.fori_loop(0, N_BUCKETS, one_bucket, jnp.zeros((N_BUCKETS, BLOCK), F32))

    lp = nc * BLOCK
    smem = pl.BlockSpec(memory_space=pltpu.SMEM)
    blk = lambda s: nc - 1 - s
    prev = lambda s: jnp.maximum(nc - 2 - s, 0)
    proj3 = proj.reshape(nb, lp, ABC_WIDTH)
    res = pl.pallas_call(
        body, name="mixers_bwd",
        grid=(nc,),
        in_specs=[pl.BlockSpec((nb, BLOCK, ABC_WIDTH), lambda s: (0, blk(s), 0)),
                  pl.BlockSpec((nb, BLOCK, 2 * BLOCK), lambda s: (0, prev(s), C_AK // (2 * BLOCK))),
                  pl.BlockSpec((nb, BLOCK, 1280), lambda s: (0, prev(s), C_CC // 1280)),
                  pl.BlockSpec((nb, BLOCK, N_BRANCH * BRANCH_WIDTH), lambda s: (0, blk(s), 0)),
                  pl.BlockSpec((nb, 1, RET_HEADS, BLOCK, BLOCK), lambda s: (0, blk(s), 0, 0, 0)),
                  pl.BlockSpec((BLOCK, BLOCK), lambda s: (blk(s), 0)),
                  pl.BlockSpec((BLOCK, BLOCK), lambda s: (blk(s), 0)),
                  pl.BlockSpec((BLOCK, 2 * BLOCK), lambda s: (0, 0)),
                  smem, smem,
                  pl.BlockSpec((3, BRANCH_WIDTH), lambda s: (0, 0))],
        out_specs=[pl.BlockSpec((nb, BLOCK, ABC_WIDTH), lambda s: (0, blk(s), 0)),
                   pl.BlockSpec((N_BUCKETS, BLOCK), lambda s: (0, 0)),
                   pl.BlockSpec((ATT_HEADS, BLOCK), lambda s: (0, 0)),
                   pl.BlockSpec((8, BRANCH_WIDTH), lambda s: (0, 0))],
        out_shape=[jax.ShapeDtypeStruct((nb, lp, ABC_WIDTH), BF16),
                   jax.ShapeDtypeStruct((N_BUCKETS, BLOCK), F32),
                   jax.ShapeDtypeStruct((ATT_HEADS, BLOCK), F32),
                   jax.ShapeDtypeStruct((8, BRANCH_WIDTH), F32)],
        scratch_shapes=[pltpu.VMEM((ATT_HEADS, BLOCK, 2 * BLOCK), F32),
                        pltpu.VMEM((ATT_HEADS, BLOCK, 2 * BLOCK), F32),
                        pltpu.VMEM((nb, BLOCK, 2 * BLOCK), F32),
                        pltpu.VMEM((nb, RET_HEADS, BLOCK, BLOCK), F32),
                        pltpu.VMEM((nb, BLOCK, BRANCH_WIDTH), F32)],
        compiler_params=_cparams("arbitrary"),
    )(proj3, proj3, proj3, d_br.reshape(nb, lp, N_BRANCH * BRANCH_WIDTH), states, cosf, sinf, bkt, rel_bias, sinks,
      conv_w)
    return (res[0].reshape(nb * lp, ABC_WIDTH),) + tuple(res[1:])


MERGE_TILE = 256


def _merge_forward(br_ref, m_ref, wb_ref, wo_ref):
    bo, gates = [], []
    mixed_pre = None
    for g in range(N_BRANCH):
        br_g = br_ref[:, BRANCH_WIDTH * g:BRANCH_WIDTH * (g + 1)]
        bo_g = jnp.concatenate([_nn(br_g, wb_ref[p, g]) for p in range(N_CHIPS)], axis=1)
        gate_g = _sigmoid(m_ref[:, D_MODEL * g:D_MODEL * (g + 1)].astype(F32))
        bo.append(bo_g)
        gates.append(gate_g)
        mixed_pre = gate_g * bo_g if mixed_pre is None else mixed_pre + gate_g * bo_g
    mixed = _nn(mixed_pre.astype(BF16), wo_ref[...])
    r = lax.rsqrt(jnp.mean(mixed * mixed, axis=-1, keepdims=True) + RMS_EPS)
    return bo, gates, mixed_pre, mixed, r


def merge_fwd(x2d, br, pm, wb, wo, g_post):
    t = x2d.shape[0]
    tm = MERGE_TILE if t % MERGE_TILE == 0 else BLOCK

    def body(x_ref, br_ref, m_ref, wb_ref, wo_ref, g_ref, o_ref):
        _, _, _, mixed, r = _merge_forward(br_ref, m_ref, wb_ref, wo_ref)
        o_ref[...] = x_ref[...] + mixed * r * g_ref[...]

    return pl.pallas_call(
        body, name="merge_fwd",
        grid=(t // tm,),
        in_specs=[pl.BlockSpec((tm, D_MODEL), lambda i: (i, 0)),
                  pl.BlockSpec((tm, N_BRANCH * BRANCH_WIDTH), lambda i: (i, 0)),
                  pl.BlockSpec((tm, MERGE_WIDTH), lambda i: (i, 0)),
                  pl.BlockSpec((N_CHIPS, N_BRANCH, BRANCH_WIDTH, SHARD_D), lambda i: (0, 0, 0, 0)),
                  pl.BlockSpec((D_MODEL, D_MODEL), lambda i: (0, 0)),
                  pl.BlockSpec((1, D_MODEL), lambda i: (0, 0))],
        out_specs=pl.BlockSpec((tm, D_MODEL), lambda i: (i, 0)),
        out_shape=jax.ShapeDtypeStruct((t, D_MODEL), F32),
        compiler_params=_cparams("parallel"),
    )(x2d, br, pm, wb, wo, g_post)


def merge_bwd(d_out, br, pm, wb, wo, g_post):
    t = d_out.shape[0]
    tm = MERGE_TILE if t % MERGE_TILE == 0 else BLOCK

    def body(do_ref, br_ref, m_ref, wb_ref, wo_ref, g_ref, dbr_ref, dm_ref, dg_ref, dwb_ref, dwo_ref):

        @pl.when(pl.program_id(0) == 0)
        def _():
            dwb_ref[...] = jnp.zeros_like(dwb_ref)
            dwo_ref[...] = jnp.zeros_like(dwo_ref)
            dg_ref[...] = jnp.zeros_like(dg_ref)

        bo, gates, mixed_pre, mixed, r = _merge_forward(br_ref, m_ref, wb_ref, wo_ref)
        d_o = do_ref[...]
        nh = mixed * r
        dg_ref[0:1, :] += jnp.sum(d_o * nh, axis=0, keepdims=True)
        dn = d_o * g_ref[...]
        d_mixed = (r * (dn - nh * jnp.mean(dn * nh, axis=-1, keepdims=True))).astype(BF16)
        dwo_ref[...] += _tn(mixed_pre.astype(BF16), d_mixed)
        d_pre = _nt(d_mixed, wo_ref[...])
        for g in range(N_BRANCH):
            br_g = br_ref[:, BRANCH_WIDTH * g:BRANCH_WIDTH * (g + 1)]
            d_bo = (d_pre * gates[g]).astype(BF16)
            dm_ref[:, D_MODEL * g:D_MODEL * (g + 1)] = (
                d_pre * bo[g] * gates[g] * (1.0 - gates[g])).astype(BF16)
            d_br_g = None
            for p in range(N_CHIPS):
                d_bo_p = d_bo[:, SHARD_D * p:SHARD_D * (p + 1)]
                part = _nt(d_bo_p, wb_ref[p, g])
                d_br_g = part if d_br_g is None else d_br_g + part
                dwb_ref[p, g] += _tn(br_g, d_bo_p)
            dbr_ref[:, BRANCH_WIDTH * g:BRANCH_WIDTH * (g + 1)] = d_br_g.astype(BF16)

    return pl.pallas_call(
        body, name="merge_bwd",
        grid=(t // tm,),
        in_specs=[pl.BlockSpec((tm, D_MODEL), lambda i: (i, 0)),
                  pl.BlockSpec((tm, N_BRANCH * BRANCH_WIDTH), lambda i: (i, 0)),
                  pl.BlockSpec((tm, MERGE_WIDTH), lambda i: (i, 0)),
                  pl.BlockSpec((N_CHIPS, N_BRANCH, BRANCH_WIDTH, SHARD_D), lambda i: (0, 0, 0, 0)),
                  pl.BlockSpec((D_MODEL, D_MODEL), lambda i: (0, 0)),
                  pl.BlockSpec((1, D_MODEL), lambda i: (0, 0))],
        out_specs=[pl.BlockSpec((tm, N_BRANCH * BRANCH_WIDTH), lambda i: (i, 0)),
                   pl.BlockSpec((tm, MERGE_WIDTH), lambda i: (i, 0)),
                   pl.BlockSpec((8, D_MODEL), lambda i: (0, 0)),
                   pl.BlockSpec((N_CHIPS, N_BRANCH, BRANCH_WIDTH, SHARD_D), lambda i: (0, 0, 0, 0)),
                   pl.BlockSpec((D_MODEL, D_MODEL), lambda i: (0, 0))],
        out_shape=[jax.ShapeDtypeStruct((t, N_BRANCH * BRANCH_WIDTH), BF16),
                   jax.ShapeDtypeStruct((t, MERGE_WIDTH), BF16),
                   jax.ShapeDtypeStruct((8, D_MODEL), F32),
                   jax.ShapeDtypeStruct((N_CHIPS, N_BRANCH, BRANCH_WIDTH, SHARD_D), F32),
                   jax.ShapeDtypeStruct((D_MODEL, D_MODEL), F32)],
        compiler_params=_cparams("arbitrary"),
    )(d_out, br, pm, wb, wo, g_post)


def loss_head(xf, target2d, nb, nc):
    def body(x_ref, t_ref, l_ref, dx_ref):
        b = pl.program_id(0)
        n = pl.program_id(1)

        @pl.when((b == 0) & (n == 0))
        def _():
            l_ref[...] = jnp.zeros_like(l_ref)

        @pl.when(n == 0)
        def _():
            dx_ref[...] = jnp.zeros_like(dx_ref)

        @pl.when(n > 0)
        def _():
            e = x_ref[...] - t_ref[...]
            dx_ref[...] = e * (1.0 / D_MODEL)
            s = jnp.sum(jnp.sum(e * e, axis=1, keepdims=True), axis=0, keepdims=True)
            l_ref[...] += jnp.broadcast_to(s * (0.5 / D_MODEL), l_ref.shape)

    return pl.pallas_call(
        body, name="loss_head",
        grid=(nb, nc),
        in_specs=[pl.BlockSpec((BLOCK, D_MODEL), lambda b, n: (b * nc + n, 0)),
                  pl.BlockSpec((BLOCK, D_MODEL), lambda b, n: (b * (nc - 1) + jnp.maximum(n - 1, 0), 0))],
        out_specs=[pl.BlockSpec((8, BLOCK), lambda b, n: (0, 0)),
                   pl.BlockSpec((BLOCK, D_MODEL), lambda b, n: (b * nc + n, 0))],
        out_shape=[jax.ShapeDtypeStruct((8, BLOCK), F32),
                   jax.ShapeDtypeStruct(xf.shape, F32)],
        compiler_params=_cparams("arbitrary", "arbitrary"),
    )(xf, target2d)


N_ABC_TILES = ABC_WIDTH // COL_TILE
N_M_TILES = MERGE_WIDTH // COL_TILE


def proj_dgrad(d_abc, d_m, w, x2d, g, d_out):
    t = x2d.shape[0]
    tm = ROW_TILE if t % ROW_TILE == 0 else BLOCK
    nk = N_ABC_TILES + N_M_TILES

    def body(da_ref, dm_ref, w_ref, x_ref, g_ref, do_ref, dx_ref, dg_ref, acc):
        i = pl.program_id(0)
        k = pl.program_id(1)

        @pl.when((i == 0) & (k == 0))
        def _():
            dg_ref[...] = jnp.zeros_like(dg_ref)

        @pl.when(k == 0)
        def _():
            acc[...] = jnp.zeros_like(acc)

        @pl.when(k < N_ABC_TILES)
        def _():
            acc[...] += _nn(da_ref[...], w_ref[...])

        @pl.when(k >= N_ABC_TILES)
        def _():
            acc[...] += _nn(dm_ref[...], w_ref[...])

        @pl.when(k == nk - 1)
        def _():
            x = x_ref[...]
            r = lax.rsqrt(jnp.mean(x * x, axis=-1, keepdims=True) + RMS_EPS)
            nh = x * r
            dh = acc[...]
            dg_ref[0:1, :] += jnp.sum(dh * nh, axis=0, keepdims=True)
            dn = dh * g_ref[...]
            dx_ref[...] = do_ref[...] + r * (dn - nh * jnp.mean(dn * nh, axis=-1, keepdims=True))

    return pl.pallas_call(
        body, name="proj_dgrad",
        grid=(t // tm, nk),
        in_specs=[pl.BlockSpec((tm, COL_TILE), lambda i, k: (i, jnp.minimum(k, N_ABC_TILES - 1))),
                  pl.BlockSpec((tm, COL_TILE), lambda i, k: (i, jnp.maximum(k - N_ABC_TILES, 0))),
                  pl.BlockSpec((COL_TILE, D_MODEL), lambda i, k: (k, 0)),
                  pl.BlockSpec((tm, D_MODEL), lambda i, k: (i, 0)),
                  pl.BlockSpec((1, D_MODEL), lambda i, k: (0, 0)),
                  pl.BlockSpec((tm, D_MODEL), lambda i, k: (i, 0))],
        out_specs=[pl.BlockSpec((tm, D_MODEL), lambda i, k: (i, 0)),
                   pl.BlockSpec((8, D_MODEL), lambda i, k: (0, 0))],
        out_shape=[jax.ShapeDtypeStruct((t, D_MODEL), F32),
                   jax.ShapeDtypeStruct((8, D_MODEL), F32)],
        scratch_shapes=[pltpu.VMEM((tm, D_MODEL), F32)],
        compiler_params=_cparams("arbitrary", "arbitrary"),
    )(d_abc, d_m, w, x2d, g, d_out)


def proj_wgrad(hb, d_abc, d_m):
    t = hb.shape[0]
    nj = N_ABC_TILES + N_M_TILES

    def body(h_ref, da_ref, dm_ref, o_ref):
        j = pl.program_id(0)

        @pl.when(j < N_ABC_TILES)
        def _():
            o_ref[...] = _tn(da_ref[...], h_ref[...])

        @pl.when(j >= N_ABC_TILES)
        def _():
            o_ref[...] = _tn(dm_ref[...], h_ref[...])

    return pl.pallas_call(
        body, name="proj_wgrad",
        grid=(nj,),
        in_specs=[pl.BlockSpec((t, D_MODEL), lambda j: (0, 0)),
                  pl.BlockSpec((t, COL_TILE), lambda j: (0, jnp.minimum(j, N_ABC_TILES - 1))),
                  pl.BlockSpec((t, COL_TILE), lambda j: (0, jnp.maximum(j - N_ABC_TILES, 0)))],
        out_specs=pl.BlockSpec((COL_TILE, D_MODEL), lambda j: (j, 0)),
        out_shape=jax.ShapeDtypeStruct((PROJ_WIDTH, D_MODEL), F32),
        compiler_params=_cparams("arbitrary"),
    )(hb, d_abc, d_m)


def _adamw_math(w, g, m, v):
    m = ADAM_B1 * m + (1.0 - ADAM_B1) * g
    v = ADAM_B2 * v + (1.0 - ADAM_B2) * jnp.square(g)
    m_hat = m / (1.0 - ADAM_B1 ** ADAM_STEP)
    v_hat = v / (1.0 - ADAM_B2 ** ADAM_STEP)
    delta = -ADAM_LR * (m_hat / (jnp.sqrt(v_hat) + ADAM_EPS) + ADAM_WD * w)
    return delta, m, v


def adamw_layer(w, g, m, v, layer, acc, after):
    _, r, c = w.shape
    tr = _row_tile(r)

    def body(*refs):
        w_ref, g_ref, m_ref, v_ref = refs[:4]
        go_ref, d_ref, mo_ref, vo_ref = refs[-4:]
        g_val = g_ref[...]
        d, m_new, v_new = _adamw_math(w_ref[...], g_val, m_ref[...], v_ref[...])
        go_ref[...] = g_val
        d_ref[...] = d
        mo_ref[...] = m_new
        vo_ref[...] = v_new

    slab = pl.BlockSpec((None, tr, c), lambda i: (layer, i, 0))
    ins = [w, g, m, v, after]
    in_specs = [slab, pl.BlockSpec((tr, c), lambda i: (i, 0)), slab, slab, ANY]
    aliases = {}
    if acc is not None:
        ins += list(acc)
        in_specs += [ANY] * 4
        aliases = {5 + i: i for i in range(4)}
    return pl.pallas_call(
        body, name="adamw_layer",
        grid=(r // tr,),
        in_specs=in_specs, out_specs=[slab] * 4,
        out_shape=[jax.ShapeDtypeStruct(w.shape, F32)] * 4,
        input_output_aliases=aliases,
        compiler_params=_cparams("parallel"),
    )(*ins)


def adamw_small(params):
    k = len(params)

    def body(*refs):
        ins, outs = refs[:4 * k], refs[4 * k:]
        for i in range(k):
            d, m_new, v_new = _adamw_math(*[r[...] for r in ins[4 * i:4 * i + 4]])
            outs[3 * i][...] = d
            outs[3 * i + 1][...] = m_new
            outs[3 * i + 2][...] = v_new

    flat = [a for p in params for a in p]
    vm = pl.BlockSpec(memory_space=pltpu.VMEM)
    out_shape = [jax.ShapeDtypeStruct(p[0].shape, F32) for p in params for _ in range(3)]
    res = pl.pallas_call(
        body, name="adamw_small",
        in_specs=[vm] * len(flat), out_specs=[vm] * len(out_shape), out_shape=out_shape,
    )(*flat)
    return [tuple(res[3 * i:3 * i + 3]) for i in range(k)]


ANY = pl.BlockSpec(memory_space=pl.ANY)


def _place():
    return lax.axis_index("x"), lax.axis_index("y"), lax.axis_index("c")


HBM = pl.BlockSpec(memory_space=pltpu.HBM)
SEM = pl.BlockSpec(memory_space=pltpu.SEMAPHORE)
EFFECT = pltpu.SideEffectType.DATAFLOW_SIDE_EFFECTING


def _other_chips(x, y):
    return [(1 - x, y), (x, 1 - y), (1 - x, 1 - y)]


def _own_slot(shard, chip):
    buf = lax.empty((N_CHIPS,) + shard.shape, shard.dtype)
    return lax.dynamic_update_slice(buf, shard[None], (chip, 0, 0, 0))


def gather_weight_shards(bufs, after):
    n = len(bufs)

    def body(*refs):
        g_refs = refs[n + 1:2 * n + 1]
        send_sems, recv_sems = refs[2 * n + 1:]
        x, y, c = _place()
        me_p = 2 * x + y
        sibling = (x, y, 1 - c)
        chips = _other_chips(x, y)

        def copy(k, slab, to):
            return pltpu.make_async_remote_copy(src_ref=slab, dst_ref=slab, send_sem=send_sems.at[k],
                                                recv_sem=recv_sems.at[k], device_id=to, device_id_type=MESH)

        first, passed = [], []
        for t in range(n):
            for k, (qx, qy) in enumerate(chips):
                cp = copy(6 * t + k, g_refs[t].at[me_p, c], (qx, qy, c))
                cp.start()
                first.append(cp)
        for t in range(n):
            for k, (qx, qy) in enumerate(chips):
                slab = g_refs[t].at[2 * qx + qy, c]
                copy(6 * t + k, slab, (qx, qy, c)).wait_recv()
                fwd = copy(6 * t + 3 + k, slab, sibling)
                fwd.start()
                passed.append(fwd)
        for t in range(n):
            for k, (qx, qy) in enumerate(chips):
                copy(6 * t + 3 + k, g_refs[t].at[2 * qx + qy, 1 - c], sibling).wait_recv()
        for cp in first + passed:
            cp.wait_send()

    return pl.pallas_call(
        body, name="gather_weight_shards",
        in_specs=[ANY] * (n + 1), out_specs=[ANY] * n,
        out_shape=[jax.ShapeDtypeStruct(b.shape, b.dtype) for b in bufs],
        input_output_aliases={t: t for t in range(n)},
        scratch_shapes=[pltpu.SemaphoreType.DMA((6 * n,)), pltpu.SemaphoreType.DMA((6 * n,))],
    )(*bufs, after)


def _hbm(a):
    return pltpu.with_memory_space_constraint(a, pltpu.HBM)


def gather_start(bufs, after):
    n = len(bufs)

    def body(*refs):
        g_refs = refs[:n]
        send_sems, recv_sems = refs[n + 1], refs[n + 2]
        token = refs[-1]
        x, y, c = _place()
        me_p = 2 * x + y
        for t in range(n):
            for k, (qx, qy) in enumerate(_other_chips(x, y)):
                slab = g_refs[t].at[me_p, c]
                pltpu.make_async_remote_copy(src_ref=slab, dst_ref=slab, send_sem=send_sems.at[3 * t + k],
                                             recv_sem=recv_sems.at[3 * t + k], device_id=(qx, qy, c),
                                             device_id_type=MESH).start()
        token[...] = jnp.zeros_like(token)

    res = pl.pallas_call(
        body, name="gather_start",
        in_specs=[HBM] * n + [ANY],
        out_specs=[SEM, SEM] + [HBM] * n + [pl.BlockSpec(memory_space=pltpu.VMEM)],
        out_shape=[pltpu.SemaphoreType.DMA((3 * n,)), pltpu.SemaphoreType.DMA((3 * n,))]
        + [pltpu.HBM(b.shape, b.dtype) for b in bufs] + [jax.ShapeDtypeStruct((8, LANES), F32)],
        input_output_aliases={t: 2 + t for t in range(n)},
        compiler_params=pltpu.CompilerParams(has_side_effects=EFFECT),
    )(*[_hbm(b) for b in bufs], after)
    return res[0], res[1], list(res[2:2 + n]), res[-1]


def gather_wait(bufs, send_sems, recv_sems, after):
    n = len(bufs)

    def body(*refs):
        g_refs = refs[:n]
        send_sems, recv_sems = refs[n], refs[n + 1]
        x, y, c = _place()
        me_p = 2 * x + y
        for t in range(n):
            for k, (qx, qy) in enumerate(_other_chips(x, y)):
                cp = pltpu.make_async_remote_copy(src_ref=g_refs[t].at[me_p, c], dst_ref=g_refs[t].at[2 * qx + qy, c],
                                                  send_sem=send_sems.at[3 * t + k], recv_sem=recv_sems.at[3 * t + k],
                                                  device_id=(qx, qy, c), device_id_type=MESH)
                cp.wait_send()
                cp.wait_recv()

    return pl.pallas_call(
        body, name="gather_wait",
        in_specs=[HBM] * n + [SEM, SEM, ANY],
        out_specs=[HBM] * n,
        out_shape=[pltpu.HBM(b.shape, b.dtype) for b in bufs],
        input_output_aliases={t: t for t in range(n)},
        compiler_params=pltpu.CompilerParams(has_side_effects=EFFECT),
    )(*bufs, send_sems, recv_sems, after)


def gather_forward(bufs):
    n = len(bufs)

    def body(*refs):
        g_refs = refs[n:2 * n]
        send_sems, recv_sems = refs[2 * n:]
        x, y, c = _place()
        sibling = (x, y, 1 - c)
        chips = _other_chips(x, y)
        passed = []
        for t in range(n):
            for k, (qx, qy) in enumerate(chips):
                slab = g_refs[t].at[2 * qx + qy, c]
                fwd = pltpu.make_async_remote_copy(src_ref=slab, dst_ref=slab, send_sem=send_sems.at[3 * t + k],
                                                   recv_sem=recv_sems.at[3 * t + k], device_id=sibling,
                                                   device_id_type=MESH)
                fwd.start()
                passed.append(fwd)
        for t in range(n):
            for k, (qx, qy) in enumerate(chips):
                slab = g_refs[t].at[2 * qx + qy, 1 - c]
                pltpu.make_async_remote_copy(src_ref=slab, dst_ref=slab, send_sem=send_sems.at[3 * t + k],
                                             recv_sem=recv_sems.at[3 * t + k], device_id=sibling,
                                             device_id_type=MESH).wait_recv()
        for cp in passed:
            cp.wait_send()

    return pl.pallas_call(
        body, name="gather_forward",
        in_specs=[ANY] * n, out_specs=[ANY] * n,
        out_shape=[jax.ShapeDtypeStruct(b.shape, b.dtype) for b in bufs],
        input_output_aliases={t: t for t in range(n)},
        scratch_shapes=[pltpu.SemaphoreType.DMA((3 * n,)), pltpu.SemaphoreType.DMA((3 * n,))],
    )(*bufs)


def exchange_small(pack, after):
    def body(p_ref, after_ref, o_ref, send_sems, recv_sems, local_sem):
        x, y, c = _place()
        me = 4 * x + 2 * y + c
        mine = pltpu.make_async_copy(p_ref, o_ref.at[me], local_sem)
        mine.start()
        sends = []
        for k in range(1, 8):
            fx, fy, fc = (k >> 2) & 1, (k >> 1) & 1, k & 1
            peer = (x ^ fx, y ^ fy, c ^ fc)
            cp = pltpu.make_async_remote_copy(src_ref=p_ref, dst_ref=o_ref.at[me], send_sem=send_sems.at[k - 1],
                                              recv_sem=recv_sems.at[k - 1], device_id=peer, device_id_type=MESH)
            cp.start()
            sends.append(cp)
        for k in range(1, 8):
            fx, fy, fc = (k >> 2) & 1, (k >> 1) & 1, k & 1
            peer = (x ^ fx, y ^ fy, c ^ fc)
            slot = o_ref.at[4 * peer[0] + 2 * peer[1] + peer[2]]
            pltpu.make_async_remote_copy(src_ref=slot, dst_ref=slot, send_sem=send_sems.at[k - 1],
                                         recv_sem=recv_sems.at[k - 1], device_id=peer, device_id_type=MESH).wait_recv()
        for cp in sends:
            cp.wait_send()
        mine.wait()

    return pl.pallas_call(
        body, name="exchange_small",
        in_specs=[ANY, ANY], out_specs=ANY,
        out_shape=jax.ShapeDtypeStruct((8,) + pack.shape, pack.dtype),
        scratch_shapes=[pltpu.SemaphoreType.DMA((7,)), pltpu.SemaphoreType.DMA((7,)), pltpu.SemaphoreType.DMA],
    )(pack, after)


def small_start(pack, me, after):
    buf = lax.dynamic_update_slice(lax.empty((8,) + pack.shape, pack.dtype), pack[None], (me, 0, 0))

    def body(b_ref, after_ref, send_sems, recv_sems, thru, token):
        x, y, c = _place()
        slot = b_ref.at[4 * x + 2 * y + c]
        for k in range(1, 8):
            peer = (x ^ ((k >> 2) & 1), y ^ ((k >> 1) & 1), c ^ (k & 1))
            pltpu.make_async_remote_copy(src_ref=slot, dst_ref=slot, send_sem=send_sems.at[k - 1],
                                         recv_sem=recv_sems.at[k - 1], device_id=peer, device_id_type=MESH).start()
        token[...] = jnp.zeros_like(token)

    return pl.pallas_call(
        body, name="small_start",
        in_specs=[HBM, ANY],
        out_specs=[SEM, SEM, HBM, pl.BlockSpec(memory_space=pltpu.VMEM)],
        out_shape=[pltpu.SemaphoreType.DMA((7,)), pltpu.SemaphoreType.DMA((7,)), pltpu.HBM(buf.shape, buf.dtype),
                   jax.ShapeDtypeStruct((8, LANES), F32)],
        input_output_aliases={0: 2},
        compiler_params=pltpu.CompilerParams(has_side_effects=EFFECT),
    )(_hbm(buf), after)


def small_wait(buf, send_sems, recv_sems, after):
    def body(b_ref, send_sems, recv_sems, after_ref, thru):
        x, y, c = _place()
        mine = b_ref.at[4 * x + 2 * y + c]
        for k in range(1, 8):
            peer = (x ^ ((k >> 2) & 1), y ^ ((k >> 1) & 1), c ^ (k & 1))
            cp = pltpu.make_async_remote_copy(src_ref=mine, dst_ref=b_ref.at[4 * peer[0] + 2 * peer[1] + peer[2]],
                                              send_sem=send_sems.at[k - 1], recv_sem=recv_sems.at[k - 1],
                                              device_id=peer, device_id_type=MESH)
            cp.wait_send()
            cp.wait_recv()

    return pl.pallas_call(
        body, name="small_wait",
        in_specs=[HBM, SEM, SEM, ANY], out_specs=HBM,
        out_shape=pltpu.HBM(buf.shape, buf.dtype),
        input_output_aliases={0: 0},
        compiler_params=pltpu.CompilerParams(has_side_effects=EFFECT),
    )(buf, send_sems, recv_sems, after)


def sibling_swap_halves(grads):
    n = len(grads)

    def body(*refs):
        g_refs, o_refs = refs[:n], refs[n:2 * n]
        send_sems, recv_sems = refs[2 * n:]
        x, y, c = _place()
        cps = []
        for t in range(n):
            for p in range(N_CHIPS):
                cp = pltpu.make_async_remote_copy(src_ref=g_refs[t].at[p, 1 - c], dst_ref=o_refs[t].at[p],
                                                  send_sem=send_sems.at[N_CHIPS * t + p],
                                                  recv_sem=recv_sems.at[N_CHIPS * t + p],
                                                  device_id=(x, y, 1 - c), device_id_type=MESH)
                cp.start()
                cps.append(cp)
        for cp in cps:
            cp.wait()

    return pl.pallas_call(
        body, name="sibling_swap_halves",
        in_specs=[ANY] * n, out_specs=[ANY] * n,
        out_shape=[jax.ShapeDtypeStruct((N_CHIPS,) + g.shape[2:], g.dtype) for g in grads],
        scratch_shapes=[pltpu.SemaphoreType.DMA((N_CHIPS * n,)), pltpu.SemaphoreType.DMA((N_CHIPS * n,))],
    )(*grads)


def swap_start(grads):
    n = len(grads)

    def body(*refs):
        g_refs, l_refs = refs[:n], refs[n:2 * n]
        send_sems, recv_sems = refs[2 * n], refs[2 * n + 1]
        token = refs[-1]
        x, y, c = _place()
        for t in range(n):
            for p in range(N_CHIPS):
                pltpu.make_async_remote_copy(src_ref=g_refs[t].at[p, 1 - c], dst_ref=l_refs[t].at[p],
                                             send_sem=send_sems.at[N_CHIPS * t + p],
                                             recv_sem=recv_sems.at[N_CHIPS * t + p],
                                             device_id=(x, y, 1 - c), device_id_type=MESH).start()
        token[...] = jnp.zeros_like(token)

    lands = [lax.empty((N_CHIPS,) + g.shape[2:], g.dtype) for g in grads]
    res = pl.pallas_call(
        body, name="swap_start",
        in_specs=[HBM] * (2 * n),
        out_specs=[SEM, SEM] + [HBM] * (2 * n) + [pl.BlockSpec(memory_space=pltpu.VMEM)],
        out_shape=[pltpu.SemaphoreType.DMA((N_CHIPS * n,)), pltpu.SemaphoreType.DMA((N_CHIPS * n,))]
        + [pltpu.HBM(a.shape, a.dtype) for a in grads + lands] + [jax.ShapeDtypeStruct((8, LANES), F32)],
        input_output_aliases={t: 2 + t for t in range(2 * n)},
        compiler_params=pltpu.CompilerParams(has_side_effects=EFFECT),
    )(*[_hbm(a) for a in grads + lands])
    return res[0], res[1], list(res[2:2 + n]), list(res[2 + n:2 + 2 * n]), res[-1]


def swap_wait(grads, lands, send_sems, recv_sems, after):
    n = len(grads)

    def body(*refs):
        g_refs, l_refs = refs[:n], refs[n:2 * n]
        send_sems, recv_sems = refs[2 * n], refs[2 * n + 1]
        x, y, c = _place()
        for t in range(n):
            for p in range(N_CHIPS):
                cp = pltpu.make_async_remote_copy(src_ref=g_refs[t].at[p, 1 - c], dst_ref=l_refs[t].at[p],
                                                  send_sem=send_sems.at[N_CHIPS * t + p],
                                                  recv_sem=recv_sems.at[N_CHIPS * t + p],
                                                  device_id=(x, y, 1 - c), device_id_type=MESH)
                cp.wait_send()
                cp.wait_recv()

    res = pl.pallas_call(
        body, name="swap_wait",
        in_specs=[HBM] * (2 * n) + [SEM, SEM, ANY],
        out_specs=[HBM] * (2 * n),
        out_shape=[pltpu.HBM(a.shape, a.dtype) for a in grads + lands],
        input_output_aliases={t: t for t in range(2 * n)},
        compiler_params=pltpu.CompilerParams(has_side_effects=EFFECT),
    )(*grads, *lands, send_sems, recv_sems, after)
    return list(res[:n]), list(res[n:])


def _row_tile(r):
    return max(t for t in range(16, 513, 16) if r % t == 0)


def add_own_half(g, other, c_arr):
    _, _, r, cols = g.shape
    tr = _row_tile(r)

    def body(c_ref, a_ref, b_ref, o_ref):
        o_ref[...] = (a_ref[...] + b_ref[...]).astype(BF16)

    return pl.pallas_call(
        body, name="add_own_half",
        grid_spec=pltpu.PrefetchScalarGridSpec(
            num_scalar_prefetch=1, grid=(N_CHIPS, r // tr),
            in_specs=[pl.BlockSpec((None, None, tr, cols), lambda p, i, c_ref: (p, c_ref[0], i, 0)),
                      pl.BlockSpec((None, tr, cols), lambda p, i, c_ref: (p, i, 0))],
            out_specs=pl.BlockSpec((None, tr, cols), lambda p, i, c_ref: (p, i, 0))),
        out_shape=jax.ShapeDtypeStruct((N_CHIPS, r, cols), BF16),
        compiler_params=_cparams("parallel", "parallel"),
    )(c_arr, g, other)


def scatter_start(partials):
    n = len(partials)

    def body(*refs):
        s_refs, l_refs = refs[:n], refs[n:2 * n]
        send_sems, recv_sems = refs[2 * n], refs[2 * n + 1]
        token = refs[-1]
        x, y, c = _place()
        for t in range(n):
            for k, (qx, qy) in enumerate(_other_chips(x, y)):
                pltpu.make_async_remote_copy(src_ref=s_refs[t].at[2 * qx + qy], dst_ref=l_refs[t].at[k],
                                             send_sem=send_sems.at[3 * t + k], recv_sem=recv_sems.at[3 * t + k],
                                             device_id=(qx, qy, c), device_id_type=MESH).start()
        token[...] = jnp.zeros_like(token)

    lands = [lax.empty((3,) + s.shape[1:], s.dtype) for s in partials]
    res = pl.pallas_call(
        body, name="scatter_start",
        in_specs=[HBM] * (2 * n),
        out_specs=[SEM, SEM] + [HBM] * (2 * n) + [pl.BlockSpec(memory_space=pltpu.VMEM)],
        out_shape=[pltpu.SemaphoreType.DMA((3 * n,)), pltpu.SemaphoreType.DMA((3 * n,))]
        + [pltpu.HBM(a.shape, a.dtype) for a in partials + lands] + [jax.ShapeDtypeStruct((8, LANES), F32)],
        input_output_aliases={t: 2 + t for t in range(2 * n)},
        compiler_params=pltpu.CompilerParams(has_side_effects=EFFECT),
    )(*[_hbm(a) for a in partials + lands])
    return res[0], res[1], list(res[2:2 + n]), list(res[2 + n:2 + 2 * n]), res[-1]


def scatter_wait(partials, lands, send_sems, recv_sems, after):
    n = len(partials)

    def body(*refs):
        s_refs, l_refs = refs[:n], refs[n:2 * n]
        send_sems, recv_sems = refs[2 * n], refs[2 * n + 1]
        x, y, c = _place()
        for t in range(n):
            for k, (qx, qy) in enumerate(_other_chips(x, y)):
                cp = pltpu.make_async_remote_copy(src_ref=s_refs[t].at[2 * qx + qy], dst_ref=l_refs[t].at[k],
                                                  send_sem=send_sems.at[3 * t + k], recv_sem=recv_sems.at[3 * t + k],
                                                  device_id=(qx, qy, c), device_id_type=MESH)
                cp.wait_send()
                cp.wait_recv()

    res = pl.pallas_call(
        body, name="scatter_wait",
        in_specs=[HBM] * (2 * n) + [SEM, SEM, ANY],
        out_specs=[HBM] * (2 * n),
        out_shape=[pltpu.HBM(a.shape, a.dtype) for a in partials + lands],
        input_output_aliases={t: t for t in range(2 * n)},
        compiler_params=pltpu.CompilerParams(has_side_effects=EFFECT),
    )(*partials, *lands, send_sems, recv_sems, after)
    return list(res[:n]), list(res[n:])


def sum_chips(own, parts, where):
    _, r, cols = own.shape
    tr = _row_tile(r)

    def body(w_ref, a_ref, p_ref, o_ref):
        acc = a_ref[...].astype(F32)
        for k in range(3):
            acc = acc + p_ref[k].astype(F32)
        o_ref[...] = acc

    return pl.pallas_call(
        body, name="sum_chips",
        grid_spec=pltpu.PrefetchScalarGridSpec(
            num_scalar_prefetch=1, grid=(r // tr,),
            in_specs=[pl.BlockSpec((None, tr, cols), lambda i, w_ref: (w_ref[0], i, 0)),
                      pl.BlockSpec((3, tr, cols), lambda i, w_ref: (0, i, 0))],
            out_specs=pl.BlockSpec((None, tr, cols), lambda i, w_ref: (w_ref[1], i, 0))),
        out_shape=jax.ShapeDtypeStruct((DEPTH, r, cols), F32),
        compiler_params=_cparams("parallel"),
    )(where, own, parts)


def sibling_share_layer(bufs):
    n = len(bufs)

    def body(*refs):
        o_refs = refs[n:2 * n]
        send_sems, recv_sems = refs[2 * n:]
        x, y, c = _place()
        cps = []
        for t in range(n):
            cp = pltpu.make_async_remote_copy(src_ref=o_refs[t].at[c], dst_ref=o_refs[t].at[c], send_sem=send_sems.at[t],
                                              recv_sem=recv_sems.at[t], device_id=(x, y, 1 - c), device_id_type=MESH)
            cp.start()
            cps.append(cp)
        for t in range(n):
            slot = o_refs[t].at[1 - c]
            pltpu.make_async_remote_copy(src_ref=slot, dst_ref=slot, send_sem=send_sems.at[t], recv_sem=recv_sems.at[t],
                                         device_id=(x, y, 1 - c), device_id_type=MESH).wait_recv()
        for cp in cps:
            cp.wait_send()

    return pl.pallas_call(
        body, name="sibling_share_layer",
        in_specs=[ANY] * n, out_specs=[ANY] * n,
        out_shape=[jax.ShapeDtypeStruct(b.shape, b.dtype) for b in bufs],
        input_output_aliases={t: t for t in range(n)},
        scratch_shapes=[pltpu.SemaphoreType.DMA((n,)), pltpu.SemaphoreType.DMA((n,))],
    )(*bufs)


SP_META = 2 * (N_META * D_MODEL // LANES)
SP_NORM = DEPTH * D_MODEL // LANES
SP_RB = DEPTH * N_BUCKETS
SP_SINK = DEPTH * ATT_HEADS
SP_CONV = DEPTH * 3 * BRANCH_WIDTH // LANES
SP_LOSS = 8
SP_ROWS = SP_META + 2 * SP_NORM + SP_RB + SP_SINK + SP_CONV + SP_LOSS


def sum_small(slots):
    half = SP_META // 2
    rb0 = SP_META + 2 * SP_NORM
    rest_rows = SP_ROWS - SP_META

    def body(s_ref, meta_ref, rest_ref):
        acc = s_ref[0]
        for d in range(1, 8):
            acc = acc + s_ref[d]
        meta_ref[...] = acc[0:half] + acc[half:SP_META]
        rest_ref[...] = acc[SP_META:]
        rest_ref[rb0 - SP_META:rb0 - SP_META + N_BUCKETS, :] = (
            acc[rb0:rb0 + N_BUCKETS] + acc[rb0 + N_BUCKETS:rb0 + 2 * N_BUCKETS])

    vm = pl.BlockSpec(memory_space=pltpu.VMEM)
    return pl.pallas_call(
        body, name="sum_small",
        in_specs=[vm], out_specs=[vm, vm],
        out_shape=[jax.ShapeDtypeStruct((half, LANES), F32), jax.ShapeDtypeStruct((rest_rows, LANES), F32)],
    )(slots)


def local_step(x, loss_target, meta_full, rel_bias, norm_pre, conv_w_full, attn_sinks, norm_post, weights_of, grads_done,
               bwd_done):
    nb, seq, _ = x.shape
    nc = seq // BLOCK + 1
    lp = nc * BLOCK
    rows = nb * lp
    pad = jnp.zeros((nb, PAD_FRONT, D_MODEL), F32)
    meta = jnp.broadcast_to(meta_full[None], (nb, N_META, D_MODEL))
    h0 = jnp.concatenate([pad, meta, x], axis=1).reshape(rows, D_MODEL)
    cosf, sinf = _rot_tables(lp)
    bkt = jnp.asarray(_bucket_table())

    acts = []
    h = h0
    for l in range(DEPTH):
        (w_in, w_br, w_out), zero = weights_of(l, h)
        hb, p_abc = norm_matmul(h, norm_pre[l][None] + zero, w_in, 0, N_ABC_TILES)
        p_m = matmul_cols(hb, w_in, N_ABC_TILES, N_M_TILES)
        br, states = mixers_fwd(p_abc, cosf, sinf, bkt, rel_bias, attn_sinks[l][None], conv_w_full[l], nb, nc)
        h_next = merge_fwd(h, br, p_m, w_br, w_out, norm_post[l][None])
        acts.append((h, hb, p_abc, p_m, br, states, w_in, w_br, w_out))
        h = h_next

    loss_part, d_h = loss_head(h, loss_target.reshape(nb * seq, D_MODEL), nb, nc)

    small = [None] * DEPTH
    zero_m = jnp.zeros((1, 1), F32)
    for l in reversed(range(DEPTH)):
        h_in, hb, p_abc, p_m, br, states, w_in, w_br, w_out = acts[l]
        d_br, d_m, d_gpost, g_wbr, g_wout = merge_bwd(d_h, br, p_m, w_br, w_out, norm_post[l][None] + zero_m)
        d_abc, d_rb, d_sk, d_cw = mixers_bwd(p_abc, d_br, states, cosf, sinf, bkt, rel_bias,
                                             attn_sinks[l][None], conv_w_full[l], nb, nc)
        g_win = proj_wgrad(hb, d_abc, d_m)
        zero = grads_done(l, [g_win, g_wbr, g_wout])
        d_h, d_gpre = proj_dgrad(d_abc, d_m, w_in, h_in, norm_pre[l][None] + zero, d_h)
        zero_m = bwd_done(l, d_h)
        small[l] = (d_gpre[0], d_gpost[0], d_rb, d_sk, d_cw[0:3])

    d_h3 = d_h.reshape(nb, lp, D_MODEL)
    d_x = d_h3[:, BLOCK:]
    d_meta = d_h3[:, PAD_FRONT:BLOCK]
    sp = jnp.concatenate([
        d_meta.reshape(-1, LANES),
        jnp.stack([small[l][0] for l in range(DEPTH)]).reshape(-1, LANES),
        jnp.stack([small[l][1] for l in range(DEPTH)]).reshape(-1, LANES),
        jnp.concatenate([small[l][2] for l in range(DEPTH)], axis=0),
        jnp.concatenate([small[l][3] for l in range(DEPTH)], axis=0),
        jnp.stack([small[l][4] for l in range(DEPTH)]).reshape(-1, LANES),
        loss_part], axis=0)
    return d_x, sp


def kernel(x, meta_tokens, rel_bias, norm_pre, w_in, conv_w, attn_sinks, w_branch, w_out, norm_post, loss_target, m_meta_tokens, m_rel_bias, m_norm_pre, m_w_in, m_conv_w, m_attn_sinks, m_w_branch, m_w_out, m_norm_post, v_meta_tokens, v_rel_bias, v_norm_pre, v_w_in, v_conv_w, v_attn_sinks, v_w_branch, v_w_out, v_norm_post):
    assert x.shape[0] == 2 and SP_META == 2 * N_META * D_MODEL // LANES
    px, py, pc = _place()
    chip = 2 * px + py

    c_arr = jnp.reshape(pc, (1,)).astype(jnp.int32)
    where = jnp.stack([chip, pc]).astype(jnp.int32)
    tr_ = lambda a: jnp.swapaxes(a, 1, 2)
    w3 = [tr_(w_in), w_branch.reshape(DEPTH, N_BRANCH * BRANCH_WIDTH, SHARD_D), w_out]
    halves = lambda a: a.reshape(2, a.shape[0] // 2, a.shape[1])

    def as_weights(bufs):
        a_in, a_br, a_out = bufs
        return (a_in.reshape(PROJ_WIDTH, D_MODEL), a_br.reshape(N_CHIPS, N_BRANCH, BRANCH_WIDTH, SHARD_D),
                a_out.reshape(D_MODEL, D_MODEL))

    side = jnp.concatenate([meta_tokens.reshape(-1), conv_w.reshape(-1)]).reshape(-1, LANES)
    side = jnp.concatenate([side, jnp.zeros((40 - side.shape[0], LANES), F32)], axis=0)
    side_all = exchange_small(side, side)
    side_chips = side_all[0::2]
    n_meta_rows = N_META * SHARD_D // LANES
    meta_full = jnp.moveaxis(side_chips[:, :n_meta_rows].reshape(N_CHIPS, N_META, SHARD_D), 0, 1).reshape(N_META, D_MODEL)
    conv_full = jnp.moveaxis(side_chips[:, n_meta_rows:n_meta_rows + 6].reshape(N_CHIPS, DEPTH, 3, LANES), 0, 2).reshape(DEPTH, 3, BRANCH_WIDTH)

    slots = [[_own_slot(halves(w[l].astype(BF16)), chip) for w in w3] for l in range(DEPTH)]
    gathered0 = gather_weight_shards(slots[0], side_all)
    send1, recv1, flying1, started1 = gather_start(slots[1], gathered0[0])

    def weights_of(l, h):
        if l == 0:
            return as_weights(gathered0), started1[0:1, 0:1]
        landed = gather_forward(gather_wait(flying1, send1, recv1, h))
        return as_weights(landed), jnp.zeros((1, 1), F32)

    reduced = [None] * DEPTH
    flying = {}

    def finish_reduce(l, after):
        partials, parts = scatter_wait(*flying[l], after)
        reduced[l] = sibling_share_layer([sum_chips(a, p, where) for a, p in zip(partials, parts)])

    def start_scatter(l, full, others):
        send, recv, thru, lands, started = scatter_start([add_own_half(g, o, c_arr) for g, o in zip(full, others)])
        flying[l] = (thru, lands, send, recv)
        return started[0:1, 0:1]

    def grads_done(l, grads):
        full = [g.reshape(N_CHIPS, 2, g.size // (2 * N_CHIPS * g.shape[-1]), g.shape[-1]) for g in grads]
        if l == 1:
            send, recv, thru, lands, started = swap_start(full)
            flying["swap"] = (thru, lands, send, recv)
            return started[0:1, 0:1]
        finish_reduce(1, grads[0])
        return start_scatter(0, full, sibling_swap_halves(full))

    def bwd_done(l, d_h):
        if l == 1:
            return start_scatter(1, *swap_wait(*flying["swap"], d_h))
        return jnp.zeros((1, 1), F32)

    d_x, sp = local_step(x, loss_target, meta_full, rel_bias, norm_pre, conv_full, attn_sinks, norm_post,
                         weights_of, grads_done, bwd_done)
    finish_reduce(0, sp)

    s_send, s_recv, s_buf, s_started = small_start(sp, 4 * px + 2 * py + pc, reduced[0][0])

    m3 = [tr_(m_w_in), m_w_branch.reshape(w3[1].shape), m_w_out]
    v3 = [tr_(v_w_in), v_w_branch.reshape(w3[1].shape), v_w_out]
    big = []
    for t in range(3):
        acc = None
        for l in reversed(range(DEPTH)):
            acc = adamw_layer(w3[t], reduced[l][t].reshape(w3[t].shape[1:]), m3[t], v3[t], l, acc, s_started)
        big.append(acc)
    g_in, *u_in = [tr_(a) for a in big[0]]
    g_br, *u_br = [a.reshape(w_branch.shape) for a in big[1]]
    g_out, *u_out = big[2]

    meta_rows, rest = sum_small(small_wait(s_buf, s_send, s_recv, big[2][1]))
    o = 0
    g_meta_full = meta_rows.reshape(N_META, D_MODEL)
    g_norm_pre = rest[o:o + SP_NORM].reshape(DEPTH, D_MODEL); o += SP_NORM
    g_norm_post = rest[o:o + SP_NORM].reshape(DEPTH, D_MODEL); o += SP_NORM
    g_rel_bias = rest[o:o + N_BUCKETS, :ATT_HEADS]; o += SP_RB
    g_sinks = rest[o:o + SP_SINK, 0].reshape(DEPTH, ATT_HEADS); o += SP_SINK
    g_conv_full = rest[o:o + SP_CONV].reshape(DEPTH, 3, BRANCH_WIDTH); o += SP_CONV
    loss = rest[o, 0]
    g_meta = lax.dynamic_slice_in_dim(g_meta_full, chip * SHARD_D, SHARD_D, axis=1)
    g_conv = lax.dynamic_slice_in_dim(g_conv_full, chip * LANES, LANES, axis=2)

    to2 = lambda a: a.reshape(-1, a.shape[-1])
    smalls = [(meta_tokens, g_meta, m_meta_tokens, v_meta_tokens),
              (rel_bias, g_rel_bias, m_rel_bias, v_rel_bias),
              (norm_pre, g_norm_pre, m_norm_pre, v_norm_pre),
              (to2(conv_w), to2(g_conv), to2(m_conv_w), to2(v_conv_w)),
              (attn_sinks, g_sinks, m_attn_sinks, v_attn_sinks),
              (norm_post, g_norm_post, m_norm_post, v_norm_post)]
    u_meta, u_rb, u_npre, u_conv, u_sink, u_npost = adamw_small(smalls)
    u_conv = tuple(a.reshape(conv_w.shape) for a in u_conv)

    grads = [g_meta, g_rel_bias, g_norm_pre, g_in, g_conv, g_sinks, g_br, g_out, g_norm_post]
    upd = [u_meta, u_rb, u_npre, u_in, u_conv, u_sink, u_br, u_out, u_npost]
    return (loss, d_x, *grads, *[u[0] for u in upd], *[u[1] for u in upd], *[u[2] for u in upd])
```

```python
import functools
import math

import numpy as np
import jax
import jax.numpy as jnp
from jax import lax
from jax.experimental import pallas as pl
from jax.experimental.pallas import tpu as pltpu

F32 = jnp.float32
BF16 = jnp.bfloat16
MESH = pl.DeviceIdType.MESH

D_MODEL = 1024
DEPTH = 2
N_META = 16
BLOCK = 128
PAD_FRONT = BLOCK - N_META
ATT_HEADS = 8
ATT_HEAD_DIM = 64
N_BUCKETS = 32
MAX_EXACT = 16
MAX_DISTANCE = 128
RET_HEADS = 4
ROT_BASE = 10000.0
N_BRANCH = 3
BRANCH_WIDTH = 512
PROJ_WIDTH = 8448
ABC_WIDTH = 5376
MERGE_WIDTH = N_BRANCH * D_MODEL
RMS_EPS = 1e-6
GN_EPS = 1e-6
NEG_INF = -1e30
ATT_SCALE = ATT_HEAD_DIM ** -0.5
RET_SCALE = BLOCK ** -0.5
LOG_GAMMA = tuple(math.log1p(-(2.0 ** (-5.0 - h))) for h in range(RET_HEADS))

C_AQ, C_AK, C_AV, C_AG = 0, 512, 640, 768
C_RQ, C_RK, C_RV, C_RG = 1280, 1792, 2304, 2816
C_CB, C_CC, C_CX, C_CG = 3328, 3840, 4352, 4864

ADAM_LR = 0.001
ADAM_B1 = 0.9
ADAM_B2 = 0.999
ADAM_EPS = 1e-08
ADAM_WD = 0.01
ADAM_STEP = 10

N_CHIPS = 4
SHARD_IN = PROJ_WIDTH // N_CHIPS
SHARD_D = D_MODEL // N_CHIPS
LANES = 128
PACK_IN = D_MODEL * SHARD_IN
PACK_BR = N_BRANCH * BRANCH_WIDTH * SHARD_D
PACK_OUT = SHARD_D * D_MODEL
PACK_ROWS = (PACK_IN + PACK_BR + PACK_OUT) // LANES

VMEM_LIMIT = 56 * 1024 * 1024
COL_TILE = 768
ROW_TILE = 1088


def _cparams(*sem):
    return pltpu.CompilerParams(dimension_semantics=sem, vmem_limit_bytes=VMEM_LIMIT)


def _nt(a, b):
    return lax.dot_general(a, b, (((1,), (1,)), ((), ())), preferred_element_type=F32)


def _tn(a, b):
    return lax.dot_general(a, b, (((0,), (0,)), ((), ())), preferred_element_type=F32)


def _nn(a, b):
    return jnp.dot(a, b, preferred_element_type=F32)


def _sigmoid(x):
    return 0.5 * jnp.tanh(0.5 * x) + 0.5


def _silu(x):
    return x * _sigmoid(x)


def _dsilu(x):
    s = _sigmoid(x)
    return s * (1.0 + x * (1.0 - s))


def _bucket_table():
    r = np.arange(BLOCK)[:, None]
    c = np.arange(2 * BLOCK)[None, :]
    n = np.maximum(BLOCK + r - c, 0)
    nf = np.maximum(n, 1).astype(np.float32)
    large = MAX_EXACT + (np.log(nf / MAX_EXACT) / math.log(MAX_DISTANCE / MAX_EXACT)
                         * (N_BUCKETS - MAX_EXACT)).astype(np.int32)
    large = np.minimum(large, N_BUCKETS - 1)
    return np.where(n < MAX_EXACT, n, large).astype(np.int32)


def _rot_tables(lp):
    half = BLOCK // 2
    pos = (jnp.arange(lp) - PAD_FRONT).astype(F32)
    theta = 1.0 / (ROT_BASE ** jnp.linspace(0.0, 1.0, half, dtype=F32))
    ang = pos[:, None] * theta[None, :]
    cos, sin = jnp.cos(ang), jnp.sin(ang)
    return jnp.concatenate([cos, cos], axis=1), jnp.concatenate([-sin, sin], axis=1)


def norm_matmul(x2d, g, w, col0_blocks, n_col_blocks):
    t = x2d.shape[0]
    tm = ROW_TILE if t % ROW_TILE == 0 else BLOCK

    def body(x_ref, g_ref, w_ref, hb_ref, o_ref):
        @pl.when(pl.program_id(1) == 0)
        def _():
            x = x_ref[...]
            r = lax.rsqrt(jnp.mean(x * x, axis=-1, keepdims=True) + RMS_EPS)
            hb_ref[...] = (x * r * g_ref[...]).astype(BF16)

        o_ref[...] = _nt(hb_ref[...], w_ref[...]).astype(BF16)

    return pl.pallas_call(
        body, name="norm_matmul",
        grid=(t // tm, n_col_blocks),
        in_specs=[pl.BlockSpec((tm, D_MODEL), lambda i, j: (i, 0)),
                  pl.BlockSpec((1, D_MODEL), lambda i, j: (0, 0)),
                  pl.BlockSpec((COL_TILE, D_MODEL), lambda i, j: (j + col0_blocks, 0))],
        out_specs=[pl.BlockSpec((tm, D_MODEL), lambda i, j: (i, 0)),
                   pl.BlockSpec((tm, COL_TILE), lambda i, j: (i, j))],
        out_shape=[jax.ShapeDtypeStruct((t, D_MODEL), BF16),
                   jax.ShapeDtypeStruct((t, n_col_blocks * COL_TILE), BF16)],
        compiler_params=_cparams("parallel", "arbitrary"),
    )(x2d, g, w)


def matmul_cols(a, w, col0_blocks, n_col_blocks):
    t, k = a.shape
    tm = ROW_TILE if t % ROW_TILE == 0 else BLOCK

    def body(a_ref, w_ref, o_ref):
        o_ref[...] = _nt(a_ref[...], w_ref[...]).astype(BF16)

    return pl.pallas_call(
        body, name="matmul_cols",
        grid=(t // tm, n_col_blocks),
        in_specs=[pl.BlockSpec((tm, k), lambda i, j: (i, 0)),
                  pl.BlockSpec((COL_TILE, k), lambda i, j: (j + col0_blocks, 0))],
        out_specs=pl.BlockSpec((tm, COL_TILE), lambda i, j: (i, j)),
        out_shape=jax.ShapeDtypeStruct((t, n_col_blocks * COL_TILE), BF16),
        compiler_params=_cparams("parallel", "arbitrary"),
    )(a, w)


class _Widened:
    def __init__(self, ref):
        self.ref = ref

    def __getitem__(self, idx):
        return self.ref[idx].astype(F32)


def _build_bias(bkt_ref, rb_ref, bias_s):
    bkt = bkt_ref[...]
    for h in range(ATT_HEADS):
        acc = jnp.zeros((BLOCK, 2 * BLOCK), F32)
        for b in range(N_BUCKETS):
            acc = jnp.where(bkt == b, rb_ref[b, h], acc)
        bias_s[h] = acc


def _band_mask(n):
    r = lax.broadcasted_iota(jnp.int32, (BLOCK, 2 * BLOCK), 0)
    c = lax.broadcasted_iota(jnp.int32, (BLOCK, 2 * BLOCK), 1)
    key_pos = (n - 1) * BLOCK + c
    return (c > r) & (c <= r + BLOCK) & (key_pos >= PAD_FRONT)


def _split_heads(kv, kh):
    lane = lax.broadcasted_iota(jnp.int32, kv.shape, 1)
    if kh == 0:
        lo = jnp.where(lane < ATT_HEAD_DIM, kv, 0.0)
        hi = pltpu.roll(lo, ATT_HEAD_DIM, 1)
    else:
        hi = jnp.where(lane >= ATT_HEAD_DIM, kv, 0.0)
        lo = pltpu.roll(hi, ATT_HEAD_DIM, 1)
    return lo, hi


def _merge_heads(acc_lo, acc_hi, kh):
    lane = lax.broadcasted_iota(jnp.int32, acc_lo.shape, 1)
    if kh == 0:
        return jnp.where(lane < ATT_HEAD_DIM, acc_lo + pltpu.roll(acc_hi, ATT_HEAD_DIM, 1), 0.0)
    return jnp.where(lane >= ATT_HEAD_DIM, acc_hi + pltpu.roll(acc_lo, ATT_HEAD_DIM, 1), 0.0)


def _softmax_sink(q2b, kxb, bias_h, mask, sink_h):
    return _softmax_of(_nt(q2b, kxb), bias_h, mask, sink_h)


def _softmax_of(qk, bias_h, mask, sink_h):
    s = qk * ATT_SCALE + bias_h
    s = jnp.where(mask, s, NEG_INF)
    m = jnp.maximum(jnp.max(s, axis=-1, keepdims=True), sink_h)
    p = jnp.exp(s - m)
    es = jnp.exp(sink_h - m)
    inv = 1.0 / (jnp.sum(p, axis=-1, keepdims=True) + es)
    return p * inv, es * inv


def _rot(t, cosf, sinf):
    return t * cosf + pltpu.roll(t, BLOCK // 2, 1) * sinf


def _rot_t(d, cosf, sinf):
    return d * cosf + pltpu.roll(d * sinf, BLOCK // 2, 1)


def _decay_tables(h):
    lg = LOG_GAMMA[h]
    i = lax.broadcasted_iota(jnp.int32, (BLOCK, BLOCK), 0)
    j = lax.broadcasted_iota(jnp.int32, (BLOCK, BLOCK), 1)
    diff = (i - j).astype(F32)
    dm = jnp.where(diff >= 0, jnp.exp(diff * lg), 0.0)
    row = lax.broadcasted_iota(jnp.int32, (BLOCK, 1), 0).astype(F32)
    zeta = jnp.exp((BLOCK - 1 - row) * lg)
    xi = jnp.exp((row + 1.0) * lg)
    return dm, zeta, xi, math.exp(BLOCK * lg)


def _valid_col(n):
    row = lax.broadcasted_iota(jnp.int32, (BLOCK, 1), 0)
    return ((n * BLOCK + row) >= PAD_FRONT).astype(F32)


def _shift_down(cur, prev, k):
    row = lax.broadcasted_iota(jnp.int32, cur.shape, 0)
    return jnp.where(row >= k, pltpu.roll(cur, k, 0), pltpu.roll(prev, k, 0))


def _shift_up(cur, nxt, k):
    row = lax.broadcasted_iota(jnp.int32, cur.shape, 0)
    return jnp.where(row < BLOCK - k, pltpu.roll(cur, BLOCK - k, 0), pltpu.roll(nxt, BLOCK - k, 0))


def mixers_fwd(proj, cosf, sinf, bkt, rel_bias, sinks, conv_w, nb, nc):
    def body(p_ref, cos_ref, sin_ref, bkt_ref, rb_ref, sk_ref, cw_ref, br_ref, st_ref,
             bias_s, kv_s, state_s, u_s):
        p_ref = _Widened(p_ref)
        n = pl.program_id(0)

        @pl.when(n == 0)
        def _():
            _build_bias(bkt_ref, rb_ref, bias_s)
            kv_s[:, 0:BLOCK, :] = jnp.zeros((nb, BLOCK, 2 * BLOCK), F32)
            state_s[...] = jnp.zeros_like(state_s)
            u_s[...] = jnp.zeros_like(u_s)

        valid = _valid_col(n)
        mask = _band_mask(n)
        ex = range(nb)

        for b in ex:
            kv_s[b, BLOCK:2 * BLOCK, :] = p_ref[b, :, C_AK:C_AK + 2 * BLOCK]
        for kh in range(2):
            ks = [[t.astype(BF16) for t in _split_heads(kv_s[b, :, 0:BLOCK], kh)] for b in ex]
            vs = [[t.astype(BF16) for t in _split_heads(kv_s[b, :, BLOCK:2 * BLOCK], kh)] for b in ex]
            pairs = [(b, 2 * kh + jj) for jj in range(2) for b in ex]
            subs = [(b, j, x) for (b, j) in pairs for x in range(2)]
            qb_ = {(b, j): p_ref[b, :, C_AQ + BLOCK * j:C_AQ + BLOCK * (j + 1)].astype(BF16) for (b, j) in pairs}
            qk_ = {(b, j, x): _nt(qb_[(b, j)], ks[b][x]) for (b, j, x) in subs}
            pb_ = {}
            for u in subs:
                h = 2 * u[1] + u[2]
                pb_[u] = _softmax_of(qk_[u], bias_s[h], mask, sk_ref[0, h])[0].astype(BF16)
            o_ = {u: _nn(pb_[u], vs[u[0]][u[2]]) for u in subs}
            for (b, j) in pairs:
                gate = p_ref[b, :, C_AG + BLOCK * j:C_AG + BLOCK * (j + 1)]
                br_ref[b, :, BLOCK * j:BLOCK * (j + 1)] = ((o_[(b, j, 0)] + o_[(b, j, 1)]) * _silu(gate)).astype(BF16)
        for b in ex:
            kv_s[b, 0:BLOCK, :] = kv_s[b, BLOCK:2 * BLOCK, :]

        cosv = cos_ref[...]
        sinv = sin_ref[...]
        tabs = [_decay_tables(h) for h in range(RET_HEADS)]
        units = [(b, h) for h in range(RET_HEADS) for b in ex]
        sl = lambda c0, h: slice(c0 + BLOCK * h, c0 + BLOCK * (h + 1))
        q_, k_, v_, sp_ = {}, {}, {}, {}
        for u in units:
            b, h = u
            q_[u] = _rot(p_ref[b, :, sl(C_RQ, h)], cosv, sinv).astype(BF16)
            k_[u] = (_rot(p_ref[b, :, sl(C_RK, h)], cosv, sinv) * RET_SCALE * valid).astype(BF16)
            v_[u] = p_ref[b, :, sl(C_RV, h)]
            sp_[u] = state_s[b, h]
            st_ref[b, 0, h] = sp_[u]
        qk_ = {u: _nt(q_[u], k_[u]) for u in units}
        qs_ = {u: _nn(q_[u], sp_[u].astype(BF16)) for u in units}
        kv_ = {u: _tn(k_[u], (v_[u] * tabs[u[1]][1]).astype(BF16)) for u in units}
        a_ = {u: (qk_[u] * tabs[u[1]][0]).astype(BF16) for u in units}
        av_ = {u: _nn(a_[u], v_[u].astype(BF16)) for u in units}
        for u in units:
            b, h = u
            o = av_[u] + tabs[h][2] * qs_[u]
            mu = jnp.mean(o, axis=-1, keepdims=True)
            var = jnp.mean(jnp.square(o - mu), axis=-1, keepdims=True)
            oh = (o - mu) * lax.rsqrt(var + GN_EPS)
            gate = p_ref[b, :, sl(C_RG, h)]
            br_ref[b, :, BRANCH_WIDTH + BLOCK * h:BRANCH_WIDTH + BLOCK * (h + 1)] = (oh * _silu(gate)).astype(BF16)
            state_s[b, h] = tabs[h][3] * sp_[u] + kv_[u]

        for b in ex:
            u = p_ref[b, :, C_CC:C_CC + BRANCH_WIDTH] * p_ref[b, :, C_CX:C_CX + BRANCH_WIDTH] * valid
            u_prev = u_s[b]
            y = (cw_ref[0:1, :] * _shift_down(u, u_prev, 2) + cw_ref[1:2, :] * _shift_down(u, u_prev, 1)
                 + cw_ref[2:3, :] * u)
            yc = p_ref[b, :, C_CB:C_CB + BRANCH_WIDTH] * y * _silu(p_ref[b, :, C_CG:C_CG + BRANCH_WIDTH])
            br_ref[b, :, 2 * BRANCH_WIDTH:3 * BRANCH_WIDTH] = yc.astype(BF16)
            u_s[b] = u

    lp = nc * BLOCK
    smem = pl.BlockSpec(memory_space=pltpu.SMEM)
    br, states = pl.pallas_call(
        body, name="mixers_fwd",
        grid=(nc,),
        in_specs=[pl.BlockSpec((nb, BLOCK, ABC_WIDTH), lambda n: (0, n, 0)),
                  pl.BlockSpec((BLOCK, BLOCK), lambda n: (n, 0)),
                  pl.BlockSpec((BLOCK, BLOCK), lambda n: (n, 0)),
                  pl.BlockSpec((BLOCK, 2 * BLOCK), lambda n: (0, 0)),
                  smem, smem,
                  pl.BlockSpec((3, BRANCH_WIDTH), lambda n: (0, 0))],
        out_specs=[pl.BlockSpec((nb, BLOCK, N_BRANCH * BRANCH_WIDTH), lambda n: (0, n, 0)),
                   pl.BlockSpec((nb, 1, RET_HEADS, BLOCK, BLOCK), lambda n: (0, n, 0, 0, 0))],
        out_shape=[jax.ShapeDtypeStruct((nb, lp, N_BRANCH * BRANCH_WIDTH), BF16),
                   jax.ShapeDtypeStruct((nb, nc, RET_HEADS, BLOCK, BLOCK), F32)],
        scratch_shapes=[pltpu.VMEM((ATT_HEADS, BLOCK, 2 * BLOCK), F32),
                        pltpu.VMEM((nb, 2 * BLOCK, 2 * BLOCK), F32),
                        pltpu.VMEM((nb, RET_HEADS, BLOCK, BLOCK), F32),
                        pltpu.VMEM((nb, BLOCK, BRANCH_WIDTH), F32)],
        compiler_params=_cparams("arbitrary"),
    )(proj.reshape(nb, lp, ABC_WIDTH), cosf, sinf, bkt, rel_bias, sinks, conv_w)
    return br.reshape(nb * lp, N_BRANCH * BRANCH_WIDTH), states


def mixers_bwd(proj, d_br, states, cosf, sinf, bkt, rel_bias, sinks, conv_w, nb, nc):
    def body(p_ref, kvp_ref, cp_ref, dbr_ref, st_ref, cos_ref, sin_ref, bkt_ref, rb_ref, sk_ref, cw_ref,
             dp_ref, drb_ref, dsk_ref, dcw_ref,
             bias_s, dbias_s, dkv_s, g_s, dy_s):
        p_ref, kvp_ref, cp_ref, dbr_ref = [_Widened(r) for r in (p_ref, kvp_ref, cp_ref, dbr_ref)]
        step = pl.program_id(0)
        n = nc - 1 - step
        ex = range(nb)

        @pl.when(step == 0)
        def _():
            _build_bias(bkt_ref, rb_ref, bias_s)
            dbias_s[...] = jnp.zeros_like(dbias_s)
            dsk_ref[...] = jnp.zeros_like(dsk_ref)
            dcw_ref[...] = jnp.zeros_like(dcw_ref)
            drb_ref[...] = jnp.zeros_like(drb_ref)
            dkv_s[...] = jnp.zeros_like(dkv_s)
            g_s[...] = jnp.zeros_like(g_s)
            dy_s[...] = jnp.zeros_like(dy_s)

        valid = _valid_col(n)
        mask = _band_mask(n)
        has_prev = (n > 0).astype(F32)

        k_all, v_all = [], []
        for b in ex:
            kv_prev = kvp_ref[b] * has_prev
            kv_cur = p_ref[b, :, C_AK:C_AK + 2 * BLOCK]
            k_all.append(jnp.concatenate([kv_prev[:, 0:BLOCK], kv_cur[:, 0:BLOCK]], axis=0))
            v_all.append(jnp.concatenate([kv_prev[:, BLOCK:], kv_cur[:, BLOCK:]], axis=0))
        zero2 = jnp.zeros((2 * BLOCK, BLOCK), F32)
        dk_tot = [zero2 for _ in ex]
        dv_tot = [zero2 for _ in ex]
        for kh in range(2):
            ks = [[t.astype(BF16) for t in _split_heads(k_all[b], kh)] for b in ex]
            vs = [[t.astype(BF16) for t in _split_heads(v_all[b], kh)] for b in ex]
            pairs = [(b, 2 * kh + jj) for jj in range(2) for b in ex]
            subs = [(b, j, x) for (b, j) in pairs for x in range(2)]
            qb_, gate_, dya_, do2_ = {}, {}, {}, {}
            for w in pairs:
                b, j = w
                qb_[w] = p_ref[b, :, C_AQ + BLOCK * j:C_AQ + BLOCK * (j + 1)].astype(BF16)
                gate_[w] = p_ref[b, :, C_AG + BLOCK * j:C_AG + BLOCK * (j + 1)]
                dya_[w] = dbr_ref[b, :, BLOCK * j:BLOCK * (j + 1)]
                do2_[w] = (dya_[w] * _silu(gate_[w])).astype(BF16)
            qk_ = {(b, j, x): _nt(qb_[(b, j)], ks[b][x]) for (b, j, x) in subs}
            dpm_ = {(b, j, x): _nt(do2_[(b, j)], vs[b][x]) for (b, j, x) in subs}
            pb_, dsb_ = {}, {}
            for u in subs:
                b, j, x = u
                h = 2 * j + x
                p, p_sink = _softmax_of(qk_[u], bias_s[h], mask, sk_ref[0, h])
                pb_[u] = p.astype(BF16)
                delta = jnp.sum(p * dpm_[u], axis=-1, keepdims=True)
                ds = p * (dpm_[u] - delta)
                dbias_s[h] += ds
                dsk_ref[h:h + 1, :] += jnp.broadcast_to(
                    jnp.sum(-p_sink * delta, axis=0, keepdims=True), (1, BLOCK))
                dsb_[u] = ds.astype(BF16)
            o_ = {u: _nn(pb_[u], vs[u[0]][u[2]]) for u in subs}
            dq_ = {u: _nn(dsb_[u], ks[u[0]][u[2]]) for u in subs}
            dkm_ = {u: _tn(dsb_[u], qb_[(u[0], u[1])]) for u in subs}
            dvm_ = {u: _tn(pb_[u], do2_[(u[0], u[1])]) for u in subs}
            for w in pairs:
                b, j = w
                o2 = o_[(b, j, 0)] + o_[(b, j, 1)]
                dq2 = (dq_[(b, j, 0)] + dq_[(b, j, 1)]) * ATT_SCALE
                dp_ref[b, :, C_AQ + BLOCK * j:C_AQ + BLOCK * (j + 1)] = dq2.astype(BF16)
                dp_ref[b, :, C_AG + BLOCK * j:C_AG + BLOCK * (j + 1)] = (
                    dya_[w] * o2 * _dsilu(gate_[w])).astype(BF16)
            for b in ex:
                j0, j1 = 2 * kh, 2 * kh + 1
                dk_lo = (dkm_[(b, j0, 0)] + dkm_[(b, j1, 0)]) * ATT_SCALE
                dk_hi = (dkm_[(b, j0, 1)] + dkm_[(b, j1, 1)]) * ATT_SCALE
                dk_tot[b] = dk_tot[b] + _merge_heads(dk_lo, dk_hi, kh)
                dv_tot[b] = dv_tot[b] + _merge_heads(dvm_[(b, j0, 0)] + dvm_[(b, j1, 0)],
                                                     dvm_[(b, j0, 1)] + dvm_[(b, j1, 1)], kh)
        for b in ex:
            dp_ref[b, :, C_AK:C_AK + BLOCK] = (dk_tot[b][BLOCK:, :] + dkv_s[b, :, 0:BLOCK]).astype(BF16)
            dp_ref[b, :, C_AV:C_AV + BLOCK] = (dv_tot[b][BLOCK:, :] + dkv_s[b, :, BLOCK:]).astype(BF16)
            dkv_s[b, :, 0:BLOCK] = dk_tot[b][0:BLOCK, :]
            dkv_s[b, :, BLOCK:] = dv_tot[b][0:BLOCK, :]

        cosv = cos_ref[...]
        sinv = sin_ref[...]
        tabs = [_decay_tables(h) for h in range(RET_HEADS)]
        units = [(b, h) for h in range(RET_HEADS) for b in ex]
        sl = lambda c0, h: slice(c0 + BLOCK * h, c0 + BLOCK * (h + 1))
        q_, k_, v_, vb_, sp_ = {}, {}, {}, {}, {}
        for u in units:
            b, h = u
            q_[u] = _rot(p_ref[b, :, sl(C_RQ, h)], cosv, sinv).astype(BF16)
            k_[u] = (_rot(p_ref[b, :, sl(C_RK, h)], cosv, sinv) * RET_SCALE * valid).astype(BF16)
            v_[u] = p_ref[b, :, sl(C_RV, h)]
            vb_[u] = v_[u].astype(BF16)
            sp_[u] = st_ref[b, 0, h].astype(BF16)
        qk_ = {u: _nt(q_[u], k_[u]) for u in units}
        qs_ = {u: _nn(q_[u], sp_[u]) for u in units}
        a_ = {u: (qk_[u] * tabs[u[1]][0]).astype(BF16) for u in units}
        av_ = {u: _nn(a_[u], vb_[u]) for u in units}
        dob_, dxo_ = {}, {}
        for u in units:
            b, h = u
            xi = tabs[h][2]
            o = av_[u] + xi * qs_[u]
            mu = jnp.mean(o, axis=-1, keepdims=True)
            var = jnp.mean(jnp.square(o - mu), axis=-1, keepdims=True)
            rstd = lax.rsqrt(var + GN_EPS)
            oh = (o - mu) * rstd
            gate = p_ref[b, :, sl(C_RG, h)]
            d_yr = dbr_ref[b, :, BRANCH_WIDTH + BLOCK * h:BRANCH_WIDTH + BLOCK * (h + 1)]
            dp_ref[b, :, sl(C_RG, h)] = (d_yr * oh * _dsilu(gate)).astype(BF16)
            doh = d_yr * _silu(gate)
            do = rstd * (doh - jnp.mean(doh, axis=-1, keepdims=True)
                         - oh * jnp.mean(doh * oh, axis=-1, keepdims=True))
            dob_[u] = do.astype(BF16)
            dxo_[u] = (do * xi).astype(BF16)
        dov_ = {u: _nt(dob_[u], vb_[u]) for u in units}
        dv1_ = {u: _tn(a_[u], dob_[u]) for u in units}
        dq1_ = {u: _nt(dxo_[u], sp_[u]) for u in units}
        gq_ = {u: _tn(q_[u], dxo_[u]) for u in units}
        da_, gb_, zv_ = {}, {}, {}
        for u in units:
            b, h = u
            da_[u] = (dov_[u] * tabs[h][0]).astype(BF16)
            g_next = g_s[b, h]
            gb_[u] = g_next.astype(BF16)
            zv_[u] = (v_[u] * tabs[h][1]).astype(BF16)
            g_s[b, h] = tabs[h][3] * g_next + gq_[u]
        dq2_ = {u: _nn(da_[u], k_[u]) for u in units}
        dk1_ = {u: _tn(da_[u], q_[u]) for u in units}
        dk2_ = {u: _nt(zv_[u], gb_[u]) for u in units}
        dv2_ = {u: _nn(k_[u], gb_[u]) for u in units}
        for u in units:
            b, h = u
            dp_ref[b, :, sl(C_RQ, h)] = _rot_t(dq2_[u] + dq1_[u], cosv, sinv).astype(BF16)
            dp_ref[b, :, sl(C_RK, h)] = _rot_t((dk1_[u] + dk2_[u]) * (RET_SCALE * valid), cosv, sinv).astype(BF16)
            dp_ref[b, :, sl(C_RV, h)] = (dv1_[u] + tabs[h][1] * dv2_[u]).astype(BF16)

        w0, w1, w2 = cw_ref[0:1, :], cw_ref[1:2, :], cw_ref[2:3, :]
        for b in ex:
            cb = p_ref[b, :, C_CB:C_CB + BRANCH_WIDTH]
            cc = p_ref[b, :, C_CC:C_CC + BRANCH_WIDTH]
            cx = p_ref[b, :, C_CX:C_CX + BRANCH_WIDTH]
            cg = p_ref[b, :, C_CG:C_CG + BRANCH_WIDTH]
            u = cc * cx * valid
            u_prev = (cp_ref[b, :, 0:BRANCH_WIDTH] * cp_ref[b, :, BRANCH_WIDTH:2 * BRANCH_WIDTH]
                      * (_valid_col(n - 1) * has_prev))
            u1 = _shift_down(u, u_prev, 1)
            u2 = _shift_down(u, u_prev, 2)
            y = w0 * u2 + w1 * u1 + w2 * u
            d_yc = dbr_ref[b, :, 2 * BRANCH_WIDTH:3 * BRANCH_WIDTH]
            sg = _silu(cg)
            dp_ref[b, :, C_CB:C_CB + BRANCH_WIDTH] = (d_yc * y * sg).astype(BF16)
            dp_ref[b, :, C_CG:C_CG + BRANCH_WIDTH] = (d_yc * cb * y * _dsilu(cg)).astype(BF16)
            dy = d_yc * cb * sg
            dy_next = dy_s[b]
            du = (w2 * dy + w1 * _shift_up(dy, dy_next, 1) + w0 * _shift_up(dy, dy_next, 2)) * valid
            dp_ref[b, :, C_CC:C_CC + BRANCH_WIDTH] = (du * cx).astype(BF16)
            dp_ref[b, :, C_CX:C_CX + BRANCH_WIDTH] = (du * cc).astype(BF16)
            dcw_ref[0:1, :] += jnp.sum(dy * u2, axis=0, keepdims=True)
            dcw_ref[1:2, :] += jnp.sum(dy * u1, axis=0, keepdims=True)
            dcw_ref[2:3, :] += jnp.sum(dy * u, axis=0, keepdims=True)
            dy_s[b] = dy

        @pl.when(step == nc - 1)
        def _():
            bkt = bkt_ref[...]
            row = lax.broadcasted_iota(jnp.int32, (N_BUCKETS, BLOCK), 0)
            lane = lax.broadcasted_iota(jnp.int32, (N_BUCKETS, BLOCK), 1)

            def one_bucket(bk, acc):
                sel = bkt == bk
                for h in range(ATT_HEADS):
                    t = jnp.where(sel, dbias_s[h], 0.0)
                    s = jnp.sum(jnp.sum(t, axis=1, keepdims=True), axis=0, keepdims=True)
                    acc = acc + jnp.where((row == bk) & (lane == h), jnp.broadcast_to(s, acc.shape), 0.0)
                return acc

            drb_ref[...] = lax.fori_loop(0, N_BUCKETS, one_bucket, jnp.zeros((N_BUCKETS, BLOCK), F32))

    lp = nc * BLOCK
    smem = pl.BlockSpec(memory_space=pltpu.SMEM)
    blk = lambda s: nc - 1 - s
    prev = lambda s: jnp.maximum(nc - 2 - s, 0)
    proj3 = proj.reshape(nb, lp, ABC_WIDTH)
    res = pl.pallas_call(
        body, name="mixers_bwd",
        grid=(nc,),
        in_specs=[pl.BlockSpec((nb, BLOCK, ABC_WIDTH), lambda s: (0, blk(s), 0)),
                  pl.BlockSpec((nb, BLOCK, 2 * BLOCK), lambda s: (0, prev(s), C_AK // (2 * BLOCK))),
                  pl.BlockSpec((nb, BLOCK, 1280), lambda s: (0, prev(s), C_CC // 1280)),
                  pl.BlockSpec((nb, BLOCK, N_BRANCH * BRANCH_WIDTH), lambda s: (0, blk(s), 0)),
                  pl.BlockSpec((nb, 1, RET_HEADS, BLOCK, BLOCK), lambda s: (0, blk(s), 0, 0, 0)),
                  pl.BlockSpec((BLOCK, BLOCK), lambda s: (blk(s), 0)),
                  pl.BlockSpec((BLOCK, BLOCK), lambda s: (blk(s), 0)),
                  pl.BlockSpec((BLOCK, 2 * BLOCK), lambda s: (0, 0)),
                  smem, smem,
                  pl.BlockSpec((3, BRANCH_WIDTH), lambda s: (0, 0))],
        out_specs=[pl.BlockSpec((nb, BLOCK, ABC_WIDTH), lambda s: (0, blk(s), 0)),
                   pl.BlockSpec((N_BUCKETS, BLOCK), lambda s: (0, 0)),
                   pl.BlockSpec((ATT_HEADS, BLOCK), lambda s: (0, 0)),
                   pl.BlockSpec((8, BRANCH_WIDTH), lambda s: (0, 0))],
        out_shape=[jax.ShapeDtypeStruct((nb, lp, ABC_WIDTH), BF16),
                   jax.ShapeDtypeStruct((N_BUCKETS, BLOCK), F32),
                   jax.ShapeDtypeStruct((ATT_HEADS, BLOCK), F32),
                   jax.ShapeDtypeStruct((8, BRANCH_WIDTH), F32)],
        scratch_shapes=[pltpu.VMEM((ATT_HEADS, BLOCK, 2 * BLOCK), F32),
                        pltpu.VMEM((ATT_HEADS, BLOCK, 2 * BLOCK), F32),
                        pltpu.VMEM((nb, BLOCK, 2 * BLOCK), F32),
                        pltpu.VMEM((nb, RET_HEADS, BLOCK, BLOCK), F32),
                        pltpu.VMEM((nb, BLOCK, BRANCH_WIDTH), F32)],
        compiler_params=_cparams("arbitrary"),
    )(proj3, proj3, proj3, d_br.reshape(nb, lp, N_BRANCH * BRANCH_WIDTH), states, cosf, sinf, bkt, rel_bias, sinks,
      conv_w)
    return (res[0].reshape(nb * lp, ABC_WIDTH),) + tuple(res[1:])


MERGE_TILE = 256


def _merge_forward(br_ref, m_ref, wb_ref, wo_ref):
    bo, gates = [], []
    mixed_pre = None
    for g in range(N_BRANCH):
        br_g = br_ref[:, BRANCH_WIDTH * g:BRANCH_WIDTH * (g + 1)]
        bo_g = jnp.concatenate([_nn(br_g, wb_ref[p, g]) for p in range(N_CHIPS)], axis=1)
        gate_g = _sigmoid(m_ref[:, D_MODEL * g:D_MODEL * (g + 1)].astype(F32))
        bo.append(bo_g)
        gates.append(gate_g)
        mixed_pre = gate_g * bo_g if mixed_pre is None else mixed_pre + gate_g * bo_g
    mixed = _nn(mixed_pre.astype(BF16), wo_ref[...])
    r = lax.rsqrt(jnp.mean(mixed * mixed, axis=-1, keepdims=True) + RMS_EPS)
    return bo, gates, mixed_pre, mixed, r


def merge_fwd(x2d, br, pm, wb, wo, g_post):
    t = x2d.shape[0]
    tm = MERGE_TILE if t % MERGE_TILE == 0 else BLOCK

    def body(x_ref, br_ref, m_ref, wb_ref, wo_ref, g_ref, o_ref):
        _, _, _, mixed, r = _merge_forward(br_ref, m_ref, wb_ref, wo_ref)
        o_ref[...] = x_ref[...] + mixed * r * g_ref[...]

    return pl.pallas_call(
        body, name="merge_fwd",
        grid=(t // tm,),
        in_specs=[pl.BlockSpec((tm, D_MODEL), lambda i: (i, 0)),
                  pl.BlockSpec((tm, N_BRANCH * BRANCH_WIDTH), lambda i: (i, 0)),
                  pl.BlockSpec((tm, MERGE_WIDTH), lambda i: (i, 0)),
                  pl.BlockSpec((N_CHIPS, N_BRANCH, BRANCH_WIDTH, SHARD_D), lambda i: (0, 0, 0, 0)),
                  pl.BlockSpec((D_MODEL, D_MODEL), lambda i: (0, 0)),
                  pl.BlockSpec((1, D_MODEL), lambda i: (0, 0))],
        out_specs=pl.BlockSpec((tm, D_MODEL), lambda i: (i, 0)),
        out_shape=jax.ShapeDtypeStruct((t, D_MODEL), F32),
        compiler_params=_cparams("parallel"),
    )(x2d, br, pm, wb, wo, g_post)


def merge_bwd(d_out, br, pm, wb, wo, g_post):
    t = d_out.shape[0]
    tm = MERGE_TILE if t % MERGE_TILE == 0 else BLOCK

    def body(do_ref, br_ref, m_ref, wb_ref, wo_ref, g_ref, dbr_ref, dm_ref, dg_ref, dwb_ref, dwo_ref):

        @pl.when(pl.program_id(0) == 0)
        def _():
            dwb_ref[...] = jnp.zeros_like(dwb_ref)
            dwo_ref[...] = jnp.zeros_like(dwo_ref)
            dg_ref[...] = jnp.zeros_like(dg_ref)

        bo, gates, mixed_pre, mixed, r = _merge_forward(br_ref, m_ref, wb_ref, wo_ref)
        d_o = do_ref[...]
        nh = mixed * r
        dg_ref[0:1, :] += jnp.sum(d_o * nh, axis=0, keepdims=True)
        dn = d_o * g_ref[...]
        d_mixed = (r * (dn - nh * jnp.mean(dn * nh, axis=-1, keepdims=True))).astype(BF16)
        dwo_ref[...] += _tn(mixed_pre.astype(BF16), d_mixed)
        d_pre = _nt(d_mixed, wo_ref[...])
        for g in range(N_BRANCH):
            br_g = br_ref[:, BRANCH_WIDTH * g:BRANCH_WIDTH * (g + 1)]
            d_bo = (d_pre * gates[g]).astype(BF16)
            dm_ref[:, D_MODEL * g:D_MODEL * (g + 1)] = (
                d_pre * bo[g] * gates[g] * (1.0 - gates[g])).astype(BF16)
            d_br_g = None
            for p in range(N_CHIPS):
                d_bo_p = d_bo[:, SHARD_D * p:SHARD_D * (p + 1)]
                part = _nt(d_bo_p, wb_ref[p, g])
                d_br_g = part if d_br_g is None else d_br_g + part
                dwb_ref[p, g] += _tn(br_g, d_bo_p)
            dbr_ref[:, BRANCH_WIDTH * g:BRANCH_WIDTH * (g + 1)] = d_br_g.astype(BF16)

    return pl.pallas_call(
        body, name="merge_bwd",
        grid=(t // tm,),
        in_specs=[pl.BlockSpec((tm, D_MODEL), lambda i: (i, 0)),
                  pl.BlockSpec((tm, N_BRANCH * BRANCH_WIDTH), lambda i: (i, 0)),
                  pl.BlockSpec((tm, MERGE_WIDTH), lambda i: (i, 0)),
                  pl.BlockSpec((N_CHIPS, N_BRANCH, BRANCH_WIDTH, SHARD_D), lambda i: (0, 0, 0, 0)),
                  pl.BlockSpec((D_MODEL, D_MODEL), lambda i: (0, 0)),
                  pl.BlockSpec((1, D_MODEL), lambda i: (0, 0))],
        out_specs=[pl.BlockSpec((tm, N_BRANCH * BRANCH_WIDTH), lambda i: (i, 0)),
                   pl.BlockSpec((tm, MERGE_WIDTH), lambda i: (i, 0)),
                   pl.BlockSpec((8, D_MODEL), lambda i: (0, 0)),
                   pl.BlockSpec((N_CHIPS, N_BRANCH, BRANCH_WIDTH, SHARD_D), lambda i: (0, 0, 0, 0)),
                   pl.BlockSpec((D_MODEL, D_MODEL), lambda i: (0, 0))],
        out_shape=[jax.ShapeDtypeStruct((t, N_BRANCH * BRANCH_WIDTH), BF16),
                   jax.ShapeDtypeStruct((t, MERGE_WIDTH), BF16),
                   jax.ShapeDtypeStruct((8, D_MODEL), F32),
                   jax.ShapeDtypeStruct((N_CHIPS, N_BRANCH, BRANCH_WIDTH, SHARD_D), F32),
                   jax.ShapeDtypeStruct((D_MODEL, D_MODEL), F32)],
        compiler_params=_cparams("arbitrary"),
    )(d_out, br, pm, wb, wo, g_post)


def loss_head(xf, target2d, nb, nc):
    def body(x_ref, t_ref, l_ref, dx_ref):
        b = pl.program_id(0)
        n = pl.program_id(1)

        @pl.when((b == 0) & (n == 0))
        def _():
            l_ref[...] = jnp.zeros_like(l_ref)

        @pl.when(n == 0)
        def _():
            dx_ref[...] = jnp.zeros_like(dx_ref)

        @pl.when(n > 0)
        def _():
            e = x_ref[...] - t_ref[...]
            dx_ref[...] = e * (1.0 / D_MODEL)
            s = jnp.sum(jnp.sum(e * e, axis=1, keepdims=True), axis=0, keepdims=True)
            l_ref[...] += jnp.broadcast_to(s * (0.5 / D_MODEL), l_ref.shape)

    return pl.pallas_call(
        body, name="loss_head",
        grid=(nb, nc),
        in_specs=[pl.BlockSpec((BLOCK, D_MODEL), lambda b, n: (b * nc + n, 0)),
                  pl.BlockSpec((BLOCK, D_MODEL), lambda b, n: (b * (nc - 1) + jnp.maximum(n - 1, 0), 0))],
        out_specs=[pl.BlockSpec((8, BLOCK), lambda b, n: (0, 0)),
                   pl.BlockSpec((BLOCK, D_MODEL), lambda b, n: (b * nc + n, 0))],
        out_shape=[jax.ShapeDtypeStruct((8, BLOCK), F32),
                   jax.ShapeDtypeStruct(xf.shape, F32)],
        compiler_params=_cparams("arbitrary", "arbitrary"),
    )(xf, target2d)


N_ABC_TILES = ABC_WIDTH // COL_TILE
N_M_TILES = MERGE_WIDTH // COL_TILE


def proj_dgrad(d_abc, d_m, w, x2d, g, d_out):
    t = x2d.shape[0]
    tm = ROW_TILE if t % ROW_TILE == 0 else BLOCK
    nk = N_ABC_TILES + N_M_TILES

    def body(da_ref, dm_ref, w_ref, x_ref, g_ref, do_ref, dx_ref, dg_ref, acc):
        i = pl.program_id(0)
        k = pl.program_id(1)

        @pl.when((i == 0) & (k == 0))
        def _():
            dg_ref[...] = jnp.zeros_like(dg_ref)

        @pl.when(k == 0)
        def _():
            acc[...] = jnp.zeros_like(acc)

        @pl.when(k < N_ABC_TILES)
        def _():
            acc[...] += _nn(da_ref[...], w_ref[...])

        @pl.when(k >= N_ABC_TILES)
        def _():
            acc[...] += _nn(dm_ref[...], w_ref[...])

        @pl.when(k == nk - 1)
        def _():
            x = x_ref[...]
            r = lax.rsqrt(jnp.mean(x * x, axis=-1, keepdims=True) + RMS_EPS)
            nh = x * r
            dh = acc[...]
            dg_ref[0:1, :] += jnp.sum(dh * nh, axis=0, keepdims=True)
            dn = dh * g_ref[...]
            dx_ref[...] = do_ref[...] + r * (dn - nh * jnp.mean(dn * nh, axis=-1, keepdims=True))

    return pl.pallas_call(
        body, name="proj_dgrad",
        grid=(t // tm, nk),
        in_specs=[pl.BlockSpec((tm, COL_TILE), lambda i, k: (i, jnp.minimum(k, N_ABC_TILES - 1))),
                  pl.BlockSpec((tm, COL_TILE), lambda i, k: (i, jnp.maximum(k - N_ABC_TILES, 0))),
                  pl.BlockSpec((COL_TILE, D_MODEL), lambda i, k: (k, 0)),
                  pl.BlockSpec((tm, D_MODEL), lambda i, k: (i, 0)),
                  pl.BlockSpec((1, D_MODEL), lambda i, k: (0, 0)),
                  pl.BlockSpec((tm, D_MODEL), lambda i, k: (i, 0))],
        out_specs=[pl.BlockSpec((tm, D_MODEL), lambda i, k: (i, 0)),
                   pl.BlockSpec((8, D_MODEL), lambda i, k: (0, 0))],
        out_shape=[jax.ShapeDtypeStruct((t, D_MODEL), F32),
                   jax.ShapeDtypeStruct((8, D_MODEL), F32)],
        scratch_shapes=[pltpu.VMEM((tm, D_MODEL), F32)],
        compiler_params=_cparams("arbitrary", "arbitrary"),
    )(d_abc, d_m, w, x2d, g, d_out)


def proj_wgrad(hb, d_abc, d_m):
    t = hb.shape[0]
    nj = N_ABC_TILES + N_M_TILES

    def body(h_ref, da_ref, dm_ref, o_ref):
        j = pl.program_id(0)

        @pl.when(j < N_ABC_TILES)
        def _():
            o_ref[...] = _tn(da_ref[...], h_ref[...])

        @pl.when(j >= N_ABC_TILES)
        def _():
            o_ref[...] = _tn(dm_ref[...], h_ref[...])

    return pl.pallas_call(
        body, name="proj_wgrad",
        grid=(nj,),
        in_specs=[pl.BlockSpec((t, D_MODEL), lambda j: (0, 0)),
                  pl.BlockSpec((t, COL_TILE), lambda j: (0, jnp.minimum(j, N_ABC_TILES - 1))),
                  pl.BlockSpec((t, COL_TILE), lambda j: (0, jnp.maximum(j - N_ABC_TILES, 0)))],
        out_specs=pl.BlockSpec((COL_TILE, D_MODEL), lambda j: (j, 0)),
        out_shape=jax.ShapeDtypeStruct((PROJ_WIDTH, D_MODEL), F32),
        compiler_params=_cparams("arbitrary"),
    )(hb, d_abc, d_m)


def _adamw_math(w, g, m, v):
    m = ADAM_B1 * m + (1.0 - ADAM_B1) * g
    v = ADAM_B2 * v + (1.0 - ADAM_B2) * jnp.square(g)
    m_hat = m / (1.0 - ADAM_B1 ** ADAM_STEP)
    v_hat = v / (1.0 - ADAM_B2 ** ADAM_STEP)
    delta = -ADAM_LR * (m_hat / (jnp.sqrt(v_hat) + ADAM_EPS) + ADAM_WD * w)
    return delta, m, v


def adamw_layer(w, g, m, v, layer, acc, after):
    _, r, c = w.shape
    tr = _row_tile(r)

    def body(*refs):
        w_ref, g_ref, m_ref, v_ref = refs[:4]
        go_ref, d_ref, mo_ref, vo_ref = refs[-4:]
        g_val = g_ref[...]
        d, m_new, v_new = _adamw_math(w_ref[...], g_val, m_ref[...], v_ref[...])
        go_ref[...] = g_val
        d_ref[...] = d
        mo_ref[...] = m_new
        vo_ref[...] = v_new

    slab = pl.BlockSpec((None, tr, c), lambda i: (layer, i, 0))
    ins = [w, g, m, v, after]
    in_specs = [slab, pl.BlockSpec((tr, c), lambda i: (i, 0)), slab, slab, ANY]
    aliases = {}
    if acc is not None:
        ins += list(acc)
        in_specs += [ANY] * 4
        aliases = {5 + i: i for i in range(4)}
    return pl.pallas_call(
        body, name="adamw_layer",
        grid=(r // tr,),
        in_specs=in_specs, out_specs=[slab] * 4,
        out_shape=[jax.ShapeDtypeStruct(w.shape, F32)] * 4,
        input_output_aliases=aliases,
        compiler_params=_cparams("parallel"),
    )(*ins)


def adamw_small(params):
    k = len(params)

    def body(*refs):
        ins, outs = refs[:4 * k], refs[4 * k:]
        for i in range(k):
            d, m_new, v_new = _adamw_math(*[r[...] for r in ins[4 * i:4 * i + 4]])
            outs[3 * i][...] = d
            outs[3 * i + 1][...] = m_new
            outs[3 * i + 2][...] = v_new

    flat = [a for p in params for a in p]
    vm = pl.BlockSpec(memory_space=pltpu.VMEM)
    out_shape = [jax.ShapeDtypeStruct(p[0].shape, F32) for p in params for _ in range(3)]
    res = pl.pallas_call(
        body, name="adamw_small",
        in_specs=[vm] * len(flat), out_specs=[vm] * len(out_shape), out_shape=out_shape,
    )(*flat)
    return [tuple(res[3 * i:3 * i + 3]) for i in range(k)]


ANY = pl.BlockSpec(memory_space=pl.ANY)


def _place():
    return lax.axis_index("x"), lax.axis_index("y"), lax.axis_index("c")


HBM = pl.BlockSpec(memory_space=pltpu.HBM)
SEM = pl.BlockSpec(memory_space=pltpu.SEMAPHORE)
EFFECT = pltpu.SideEffectType.DATAFLOW_SIDE_EFFECTING


def _other_chips(x, y):
    return [(1 - x, y), (x, 1 - y), (1 - x, 1 - y)]


def _own_slot(shard, chip):
    buf = lax.empty((N_CHIPS,) + shard.shape, shard.dtype)
    return lax.dynamic_update_slice(buf, shard[None], (chip, 0, 0, 0))


def gather_weight_shards(bufs, after):
    n = len(bufs)

    def body(*refs):
        g_refs = refs[n + 1:2 * n + 1]
        send_sems, recv_sems = refs[2 * n + 1:]
        x, y, c = _place()
        me_p = 2 * x + y
        sibling = (x, y, 1 - c)
        chips = _other_chips(x, y)

        def copy(k, slab, to):
            return pltpu.make_async_remote_copy(src_ref=slab, dst_ref=slab, send_sem=send_sems.at[k],
                                                recv_sem=recv_sems.at[k], device_id=to, device_id_type=MESH)

        first, passed = [], []
        for t in range(n):
            for k, (qx, qy) in enumerate(chips):
                cp = copy(6 * t + k, g_refs[t].at[me_p, c], (qx, qy, c))
                cp.start()
                first.append(cp)
        for t in range(n):
            for k, (qx, qy) in enumerate(chips):
                slab = g_refs[t].at[2 * qx + qy, c]
                copy(6 * t + k, slab, (qx, qy, c)).wait_recv()
                fwd = copy(6 * t + 3 + k, slab, sibling)
                fwd.start()
                passed.append(fwd)
        for t in range(n):
            for k, (qx, qy) in enumerate(chips):
                copy(6 * t + 3 + k, g_refs[t].at[2 * qx + qy, 1 - c], sibling).wait_recv()
        for cp in first + passed:
            cp.wait_send()

    return pl.pallas_call(
        body, name="gather_weight_shards",
        in_specs=[ANY] * (n + 1), out_specs=[ANY] * n,
        out_shape=[jax.ShapeDtypeStruct(b.shape, b.dtype) for b in bufs],
        input_output_aliases={t: t for t in range(n)},
        scratch_shapes=[pltpu.SemaphoreType.DMA((6 * n,)), pltpu.SemaphoreType.DMA((6 * n,))],
    )(*bufs, after)


def _hbm(a):
    return pltpu.with_memory_space_constraint(a, pltpu.HBM)


def gather_start(bufs, after):
    n = len(bufs)

    def body(*refs):
        g_refs = refs[:n]
        send_sems, recv_sems = refs[n + 1], refs[n + 2]
        token = refs[-1]
        x, y, c = _place()
        me_p = 2 * x + y
        for t in range(n):
            for k, (qx, qy) in enumerate(_other_chips(x, y)):
                slab = g_refs[t].at[me_p, c]
                pltpu.make_async_remote_copy(src_ref=slab, dst_ref=slab, send_sem=send_sems.at[3 * t + k],
                                             recv_sem=recv_sems.at[3 * t + k], device_id=(qx, qy, c),
                                             device_id_type=MESH).start()
        token[...] = jnp.zeros_like(token)

    res = pl.pallas_call(
        body, name="gather_start",
        in_specs=[HBM] * n + [ANY],
        out_specs=[SEM, SEM] + [HBM] * n + [pl.BlockSpec(memory_space=pltpu.VMEM)],
        out_shape=[pltpu.SemaphoreType.DMA((3 * n,)), pltpu.SemaphoreType.DMA((3 * n,))]
        + [pltpu.HBM(b.shape, b.dtype) for b in bufs] + [jax.ShapeDtypeStruct((8, LANES), F32)],
        input_output_aliases={t: 2 + t for t in range(n)},
        compiler_params=pltpu.CompilerParams(has_side_effects=EFFECT),
    )(*[_hbm(b) for b in bufs], after)
    return res[0], res[1], list(res[2:2 + n]), res[-1]


def gather_wait(bufs, send_sems, recv_sems, after):
    n = len(bufs)

    def body(*refs):
        g_refs = refs[:n]
        send_sems, recv_sems = refs[n], refs[n + 1]
        x, y, c = _place()
        me_p = 2 * x + y
        for t in range(n):
            for k, (qx, qy) in enumerate(_other_chips(x, y)):
                cp = pltpu.make_async_remote_copy(src_ref=g_refs[t].at[me_p, c], dst_ref=g_refs[t].at[2 * qx + qy, c],
                                                  send_sem=send_sems.at[3 * t + k], recv_sem=recv_sems.at[3 * t + k],
                                                  device_id=(qx, qy, c), device_id_type=MESH)
                cp.wait_send()
                cp.wait_recv()

    return pl.pallas_call(
        body, name="gather_wait",
        in_specs=[HBM] * n + [SEM, SEM, ANY],
        out_specs=[HBM] * n,
        out_shape=[pltpu.HBM(b.shape, b.dtype) for b in bufs],
        input_output_aliases={t: t for t in range(n)},
        compiler_params=pltpu.CompilerParams(has_side_effects=EFFECT),
    )(*bufs, send_sems, recv_sems, after)


def gather_forward(bufs):
    n = len(bufs)

    def body(*refs):
        g_refs = refs[n:2 * n]
        send_sems, recv_sems = refs[2 * n:]
        x, y, c = _place()
        sibling = (x, y, 1 - c)
        chips = _other_chips(x, y)
        passed = []
        for t in range(n):
            for k, (qx, qy) in enumerate(chips):
                slab = g_refs[t].at[2 * qx + qy, c]
                fwd = pltpu.make_async_remote_copy(src_ref=slab, dst_ref=slab, send_sem=send_sems.at[3 * t + k],
                                                   recv_sem=recv_sems.at[3 * t + k], device_id=sibling,
                                                   device_id_type=MESH)
                fwd.start()
                passed.append(fwd)
        for t in range(n):
            for k, (qx, qy) in enumerate(chips):
                slab = g_refs[t].at[2 * qx + qy, 1 - c]
                pltpu.make_async_remote_copy(src_ref=slab, dst_ref=slab, send_sem=send_sems.at[3 * t + k],
                                             recv_sem=recv_sems.at[3 * t + k], device_id=sibling,
                                             device_id_type=MESH).wait_recv()
        for cp in passed:
            cp.wait_send()

    return pl.pallas_call(
        body, name="gather_forward",
        in_specs=[ANY] * n, out_specs=[ANY] * n,
        out_shape=[jax.ShapeDtypeStruct(b.shape, b.dtype) for b in bufs],
        input_output_aliases={t: t for t in range(n)},
        scratch_shapes=[pltpu.SemaphoreType.DMA((3 * n,)), pltpu.SemaphoreType.DMA((3 * n,))],
    )(*bufs)


def exchange_small(pack, after):
    def body(p_ref, after_ref, o_ref, send_sems, recv_sems, local_sem):
        x, y, c = _place()
        me = 4 * x + 2 * y + c
        mine = pltpu.make_async_copy(p_ref, o_ref.at[me], local_sem)
        mine.start()
        sends = []
        for k in range(1, 8):
            fx, fy, fc = (k >> 2) & 1, (k >> 1) & 1, k & 1
            peer = (x ^ fx, y ^ fy, c ^ fc)
            cp = pltpu.make_async_remote_copy(src_ref=p_ref, dst_ref=o_ref.at[me], send_sem=send_sems.at[k - 1],
                                              recv_sem=recv_sems.at[k - 1], device_id=peer, device_id_type=MESH)
            cp.start()
            sends.append(cp)
        for k in range(1, 8):
            fx, fy, fc = (k >> 2) & 1, (k >> 1) & 1, k & 1
            peer = (x ^ fx, y ^ fy, c ^ fc)
            slot = o_ref.at[4 * peer[0] + 2 * peer[1] + peer[2]]
            pltpu.make_async_remote_copy(src_ref=slot, dst_ref=slot, send_sem=send_sems.at[k - 1],
                                         recv_sem=recv_sems.at[k - 1], device_id=peer, device_id_type=MESH).wait_recv()
        for cp in sends:
            cp.wait_send()
        mine.wait()

    return pl.pallas_call(
        body, name="exchange_small",
        in_specs=[ANY, ANY], out_specs=ANY,
        out_shape=jax.ShapeDtypeStruct((8,) + pack.shape, pack.dtype),
        scratch_shapes=[pltpu.SemaphoreType.DMA((7,)), pltpu.SemaphoreType.DMA((7,)), pltpu.SemaphoreType.DMA],
    )(pack, after)


def small_start(pack, me, after):
    buf = lax.dynamic_update_slice(lax.empty((8,) + pack.shape, pack.dtype), pack[None], (me, 0, 0))

    def body(b_ref, after_ref, send_sems, recv_sems, thru, token):
        x, y, c = _place()
        slot = b_ref.at[4 * x + 2 * y + c]
        for k in range(1, 8):
            peer = (x ^ ((k >> 2) & 1), y ^ ((k >> 1) & 1), c ^ (k & 1))
            pltpu.make_async_remote_copy(src_ref=slot, dst_ref=slot, send_sem=send_sems.at[k - 1],
                                         recv_sem=recv_sems.at[k - 1], device_id=peer, device_id_type=MESH).start()
        token[...] = jnp.zeros_like(token)

    return pl.pallas_call(
        body, name="small_start",
        in_specs=[HBM, ANY],
        out_specs=[SEM, SEM, HBM, pl.BlockSpec(memory_space=pltpu.VMEM)],
        out_shape=[pltpu.SemaphoreType.DMA((7,)), pltpu.SemaphoreType.DMA((7,)), pltpu.HBM(buf.shape, buf.dtype),
                   jax.ShapeDtypeStruct((8, LANES), F32)],
        input_output_aliases={0: 2},
        compiler_params=pltpu.CompilerParams(has_side_effects=EFFECT),
    )(_hbm(buf), after)


def small_wait(buf, send_sems, recv_sems, after):
    def body(b_ref, send_sems, recv_sems, after_ref, thru):
        x, y, c = _place()
        mine = b_ref.at[4 * x + 2 * y + c]
        for k in range(1, 8):
            peer = (x ^ ((k >> 2) & 1), y ^ ((k >> 1) & 1), c ^ (k & 1))
            cp = pltpu.make_async_remote_copy(src_ref=mine, dst_ref=b_ref.at[4 * peer[0] + 2 * peer[1] + peer[2]],
                                              send_sem=send_sems.at[k - 1], recv_sem=recv_sems.at[k - 1],
                                              device_id=peer, device_id_type=MESH)
            cp.wait_send()
            cp.wait_recv()

    return pl.pallas_call(
        body, name="small_wait",
        in_specs=[HBM, SEM, SEM, ANY], out_specs=HBM,
        out_shape=pltpu.HBM(buf.shape, buf.dtype),
        input_output_aliases={0: 0},
        compiler_params=pltpu.CompilerParams(has_side_effects=EFFECT),
    )(buf, send_sems, recv_sems, after)


def swap_start(grads):
    n = len(grads)

    def body(*refs):
        g_refs, l_refs = refs[:n], refs[n:2 * n]
        send_sems, recv_sems = refs[2 * n], refs[2 * n + 1]
        token = refs[-1]
        x, y, c = _place()
        for t in range(n):
            for p in range(N_CHIPS):
                pltpu.make_async_remote_copy(src_ref=g_refs[t].at[p, 1 - c], dst_ref=l_refs[t].at[p],
                                             send_sem=send_sems.at[N_CHIPS * t + p],
                                             recv_sem=recv_sems.at[N_CHIPS * t + p],
                                             device_id=(x, y, 1 - c), device_id_type=MESH).start()
        token[...] = jnp.zeros_like(token)

    lands = [lax.empty((N_CHIPS,) + g.shape[2:], g.dtype) for g in grads]
    res = pl.pallas_call(
        body, name="swap_start",
        in_specs=[HBM] * (2 * n),
        out_specs=[SEM, SEM] + [HBM] * (2 * n) + [pl.BlockSpec(memory_space=pltpu.VMEM)],
        out_shape=[pltpu.SemaphoreType.DMA((N_CHIPS * n,)), pltpu.SemaphoreType.DMA((N_CHIPS * n,))]
        + [pltpu.HBM(a.shape, a.dtype) for a in grads + lands] + [jax.ShapeDtypeStruct((8, LANES), F32)],
        input_output_aliases={t: 2 + t for t in range(2 * n)},
        compiler_params=pltpu.CompilerParams(has_side_effects=EFFECT),
    )(*[_hbm(a) for a in grads + lands])
    return res[0], res[1], list(res[2:2 + n]), list(res[2 + n:2 + 2 * n]), res[-1]


def swap_wait(grads, lands, send_sems, recv_sems, after):
    n = len(grads)

    def body(*refs):
        g_refs, l_refs = refs[:n], refs[n:2 * n]
        send_sems, recv_sems = refs[2 * n], refs[2 * n + 1]
        x, y, c = _place()
        for t in range(n):
            for p in range(N_CHIPS):
                cp = pltpu.make_async_remote_copy(src_ref=g_refs[t].at[p, 1 - c], dst_ref=l_refs[t].at[p],
                                                  send_sem=send_sems.at[N_CHIPS * t + p],
                                                  recv_sem=recv_sems.at[N_CHIPS * t + p],
                                                  device_id=(x, y, 1 - c), device_id_type=MESH)
                cp.wait_send()
                cp.wait_recv()

    res = pl.pallas_call(
        body, name="swap_wait",
        in_specs=[HBM] * (2 * n) + [SEM, SEM, ANY],
        out_specs=[HBM] * (2 * n),
        out_shape=[pltpu.HBM(a.shape, a.dtype) for a in grads + lands],
        input_output_aliases={t: t for t in range(2 * n)},
        compiler_params=pltpu.CompilerParams(has_side_effects=EFFECT),
    )(*grads, *lands, send_sems, recv_sems, after)
    return list(res[:n]), list(res[n:])


def _row_tile(r):
    return max(t for t in range(16, 513, 16) if r % t == 0)


def add_own_half(g, other, c_arr):
    _, _, r, cols = g.shape
    tr = _row_tile(r)

    def body(c_ref, a_ref, b_ref, o_ref):
        o_ref[...] = (a_ref[...] + b_ref[...]).astype(BF16)

    return pl.pallas_call(
        body, name="add_own_half",
        grid_spec=pltpu.PrefetchScalarGridSpec(
            num_scalar_prefetch=1, grid=(N_CHIPS, r // tr),
            in_specs=[pl.BlockSpec((None, None, tr, cols), lambda p, i, c_ref: (p, c_ref[0], i, 0)),
                      pl.BlockSpec((None, tr, cols), lambda p, i, c_ref: (p, i, 0))],
            out_specs=pl.BlockSpec((None, tr, cols), lambda p, i, c_ref: (p, i, 0))),
        out_shape=jax.ShapeDtypeStruct((N_CHIPS, r, cols), BF16),
        compiler_params=_cparams("parallel", "parallel"),
    )(c_arr, g, other)


def scatter_start(partials):
    n = len(partials)

    def body(*refs):
        s_refs, l_refs = refs[:n], refs[n:2 * n]
        send_sems, recv_sems = refs[2 * n], refs[2 * n + 1]
        token = refs[-1]
        x, y, c = _place()
        for t in range(n):
            for k, (qx, qy) in enumerate(_other_chips(x, y)):
                pltpu.make_async_remote_copy(src_ref=s_refs[t].at[2 * qx + qy], dst_ref=l_refs[t].at[k],
                                             send_sem=send_sems.at[3 * t + k], recv_sem=recv_sems.at[3 * t + k],
                                             device_id=(qx, qy, c), device_id_type=MESH).start()
        token[...] = jnp.zeros_like(token)

    lands = [lax.empty((3,) + s.shape[1:], s.dtype) for s in partials]
    res = pl.pallas_call(
        body, name="scatter_start",
        in_specs=[HBM] * (2 * n),
        out_specs=[SEM, SEM] + [HBM] * (2 * n) + [pl.BlockSpec(memory_space=pltpu.VMEM)],
        out_shape=[pltpu.SemaphoreType.DMA((3 * n,)), pltpu.SemaphoreType.DMA((3 * n,))]
        + [pltpu.HBM(a.shape, a.dtype) for a in partials + lands] + [jax.ShapeDtypeStruct((8, LANES), F32)],
        input_output_aliases={t: 2 + t for t in range(2 * n)},
        compiler_params=pltpu.CompilerParams(has_side_effects=EFFECT),
    )(*[_hbm(a) for a in partials + lands])
    return res[0], res[1], list(res[2:2 + n]), list(res[2 + n:2 + 2 * n]), res[-1]


def scatter_wait(partials, lands, send_sems, recv_sems, after):
    n = len(partials)

    def body(*refs):
        s_refs, l_refs = refs[:n], refs[n:2 * n]
        send_sems, recv_sems = refs[2 * n], refs[2 * n + 1]
        x, y, c = _place()
        for t in range(n):
            for k, (qx, qy) in enumerate(_other_chips(x, y)):
                cp = pltpu.make_async_remote_copy(src_ref=s_refs[t].at[2 * qx + qy], dst_ref=l_refs[t].at[k],
                                                  send_sem=send_sems.at[3 * t + k], recv_sem=recv_sems.at[3 * t + k],
                                                  device_id=(qx, qy, c), device_id_type=MESH)
                cp.wait_send()
                cp.wait_recv()

    res = pl.pallas_call(
        body, name="scatter_wait",
        in_specs=[HBM] * (2 * n) + [SEM, SEM, ANY],
        out_specs=[HBM] * (2 * n),
        out_shape=[pltpu.HBM(a.shape, a.dtype) for a in partials + lands],
        input_output_aliases={t: t for t in range(2 * n)},
        compiler_params=pltpu.CompilerParams(has_side_effects=EFFECT),
    )(*partials, *lands, send_sems, recv_sems, after)
    return list(res[:n]), list(res[n:])


def sum_chips(own, parts, where):
    _, r, cols = own.shape
    tr = _row_tile(r)

    def body(w_ref, a_ref, p_ref, o_ref):
        acc = a_ref[...].astype(F32)
        for k in range(3):
            acc = acc + p_ref[k].astype(F32)
        o_ref[...] = acc

    return pl.pallas_call(
        body, name="sum_chips",
        grid_spec=pltpu.PrefetchScalarGridSpec(
            num_scalar_prefetch=1, grid=(r // tr,),
            in_specs=[pl.BlockSpec((None, tr, cols), lambda i, w_ref: (w_ref[0], i, 0)),
                      pl.BlockSpec((3, tr, cols), lambda i, w_ref: (0, i, 0))],
            out_specs=pl.BlockSpec((None, tr, cols), lambda i, w_ref: (w_ref[1], i, 0))),
        out_shape=jax.ShapeDtypeStruct((DEPTH, r, cols), F32),
        compiler_params=_cparams("parallel"),
    )(where, own, parts)


def sibling_share_layer(bufs):
    n = len(bufs)

    def body(*refs):
        o_refs = refs[n:2 * n]
        send_sems, recv_sems = refs[2 * n:]
        x, y, c = _place()
        cps = []
        for t in range(n):
            cp = pltpu.make_async_remote_copy(src_ref=o_refs[t].at[c], dst_ref=o_refs[t].at[c], send_sem=send_sems.at[t],
                                              recv_sem=recv_sems.at[t], device_id=(x, y, 1 - c), device_id_type=MESH)
            cp.start()
            cps.append(cp)
        for t in range(n):
            slot = o_refs[t].at[1 - c]
            pltpu.make_async_remote_copy(src_ref=slot, dst_ref=slot, send_sem=send_sems.at[t], recv_sem=recv_sems.at[t],
                                         device_id=(x, y, 1 - c), device_id_type=MESH).wait_recv()
        for cp in cps:
            cp.wait_send()

    return pl.pallas_call(
        body, name="sibling_share_layer",
        in_specs=[ANY] * n, out_specs=[ANY] * n,
        out_shape=[jax.ShapeDtypeStruct(b.shape, b.dtype) for b in bufs],
        input_output_aliases={t: t for t in range(n)},
        scratch_shapes=[pltpu.SemaphoreType.DMA((n,)), pltpu.SemaphoreType.DMA((n,))],
    )(*bufs)


SP_META = 2 * (N_META * D_MODEL // LANES)
SP_NORM = DEPTH * D_MODEL // LANES
SP_RB = DEPTH * N_BUCKETS
SP_SINK = DEPTH * ATT_HEADS
SP_CONV = DEPTH * 3 * BRANCH_WIDTH // LANES
SP_LOSS = 8
SP_ROWS = SP_META + 2 * SP_NORM + SP_RB + SP_SINK + SP_CONV + SP_LOSS


def sum_small(slots):
    half = SP_META // 2
    rb0 = SP_META + 2 * SP_NORM
    rest_rows = SP_ROWS - SP_META

    def body(s_ref, meta_ref, rest_ref):
        acc = s_ref[0]
        for d in range(1, 8):
            acc = acc + s_ref[d]
        meta_ref[...] = acc[0:half] + acc[half:SP_META]
        rest_ref[...] = acc[SP_META:]
        rest_ref[rb0 - SP_META:rb0 - SP_META + N_BUCKETS, :] = (
            acc[rb0:rb0 + N_BUCKETS] + acc[rb0 + N_BUCKETS:rb0 + 2 * N_BUCKETS])

    vm = pl.BlockSpec(memory_space=pltpu.VMEM)
    return pl.pallas_call(
        body, name="sum_small",
        in_specs=[vm], out_specs=[vm, vm],
        out_shape=[jax.ShapeDtypeStruct((half, LANES), F32), jax.ShapeDtypeStruct((rest_rows, LANES), F32)],
    )(slots)


def local_step(x, loss_target, meta_full, rel_bias, norm_pre, conv_w_full, attn_sinks, norm_post, weights_of, grads_done,
               bwd_done):
    nb, seq, _ = x.shape
    nc = seq // BLOCK + 1
    lp = nc * BLOCK
    rows = nb * lp
    pad = jnp.zeros((nb, PAD_FRONT, D_MODEL), F32)
    meta = jnp.broadcast_to(meta_full[None], (nb, N_META, D_MODEL))
    h0 = jnp.concatenate([pad, meta, x], axis=1).reshape(rows, D_MODEL)
    cosf, sinf = _rot_tables(lp)
    bkt = jnp.asarray(_bucket_table())

    acts = []
    h = h0
    for l in range(DEPTH):
        (w_in, w_br, w_out), zero = weights_of(l, h)
        hb, p_abc = norm_matmul(h, norm_pre[l][None] + zero, w_in, 0, N_ABC_TILES)
        p_m = matmul_cols(hb, w_in, N_ABC_TILES, N_M_TILES)
        br, states = mixers_fwd(p_abc, cosf, sinf, bkt, rel_bias, attn_sinks[l][None], conv_w_full[l], nb, nc)
        h_next = merge_fwd(h, br, p_m, w_br, w_out, norm_post[l][None])
        acts.append((h, hb, p_abc, p_m, br, states, w_in, w_br, w_out))
        h = h_next

    loss_part, d_h = loss_head(h, loss_target.reshape(nb * seq, D_MODEL), nb, nc)

    small = [None] * DEPTH
    zero_m = jnp.zeros((1, 1), F32)
    for l in reversed(range(DEPTH)):
        h_in, hb, p_abc, p_m, br, states, w_in, w_br, w_out = acts[l]
        d_br, d_m, d_gpost, g_wbr, g_wout = merge_bwd(d_h, br, p_m, w_br, w_out, norm_post[l][None] + zero_m)
        d_abc, d_rb, d_sk, d_cw = mixers_bwd(p_abc, d_br, states, cosf, sinf, bkt, rel_bias,
                                             attn_sinks[l][None], conv_w_full[l], nb, nc)
        g_win = proj_wgrad(hb, d_abc, d_m)
        zero = grads_done(l, [g_win, g_wbr, g_wout])
        d_h, d_gpre = proj_dgrad(d_abc, d_m, w_in, h_in, norm_pre[l][None] + zero, d_h)
        zero_m = bwd_done(l, d_h)
        small[l] = (d_gpre[0], d_gpost[0], d_rb, d_sk, d_cw[0:3])

    d_h3 = d_h.reshape(nb, lp, D_MODEL)
    d_x = d_h3[:, BLOCK:]
    d_meta = d_h3[:, PAD_FRONT:BLOCK]
    sp = jnp.concatenate([
        d_meta.reshape(-1, LANES),
        jnp.stack([small[l][0] for l in range(DEPTH)]).reshape(-1, LANES),
        jnp.stack([small[l][1] for l in range(DEPTH)]).reshape(-1, LANES),
        jnp.concatenate([small[l][2] for l in range(DEPTH)], axis=0),
        jnp.concatenate([small[l][3] for l in range(DEPTH)], axis=0),
        jnp.stack([small[l][4] for l in range(DEPTH)]).reshape(-1, LANES),
        loss_part], axis=0)
    return d_x, sp


def kernel(x, meta_tokens, rel_bias, norm_pre, w_in, conv_w, attn_sinks, w_branch, w_out, norm_post, loss_target, m_meta_tokens, m_rel_bias, m_norm_pre, m_w_in, m_conv_w, m_attn_sinks, m_w_branch, m_w_out, m_norm_post, v_meta_tokens, v_rel_bias, v_norm_pre, v_w_in, v_conv_w, v_attn_sinks, v_w_branch, v_w_out, v_norm_post):
    assert x.shape[0] == 2 and SP_META == 2 * N_META * D_MODEL // LANES
    px, py, pc = _place()
    chip = 2 * px + py

    c_arr = jnp.reshape(pc, (1,)).astype(jnp.int32)
    where = jnp.stack([chip, pc]).astype(jnp.int32)
    tr_ = lambda a: jnp.swapaxes(a, 1, 2)
    w3 = [tr_(w_in), w_branch.reshape(DEPTH, N_BRANCH * BRANCH_WIDTH, SHARD_D), w_out]
    halves = lambda a: a.reshape(2, a.shape[0] // 2, a.shape[1])

    def as_weights(bufs):
        a_in, a_br, a_out = bufs
        return (a_in.reshape(PROJ_WIDTH, D_MODEL), a_br.reshape(N_CHIPS, N_BRANCH, BRANCH_WIDTH, SHARD_D),
                a_out.reshape(D_MODEL, D_MODEL))

    side = jnp.concatenate([meta_tokens.reshape(-1), conv_w.reshape(-1)]).reshape(-1, LANES)
    side = jnp.concatenate([side, jnp.zeros((40 - side.shape[0], LANES), F32)], axis=0)
    side_all = exchange_small(side, side)
    side_chips = side_all[0::2]
    n_meta_rows = N_META * SHARD_D // LANES
    meta_full = jnp.moveaxis(side_chips[:, :n_meta_rows].reshape(N_CHIPS, N_META, SHARD_D), 0, 1).reshape(N_META, D_MODEL)
    conv_full = jnp.moveaxis(side_chips[:, n_meta_rows:n_meta_rows + 6].reshape(N_CHIPS, DEPTH, 3, LANES), 0, 2).reshape(DEPTH, 3, BRANCH_WIDTH)

    slots = [[_own_slot(halves(w[l].astype(BF16)), chip) for w in w3] for l in range(DEPTH)]
    gathered0 = gather_weight_shards(slots[0], side_all)
    send1, recv1, flying1, started1 = gather_start(slots[1], gathered0[0])

    def weights_of(l, h):
        if l == 0:
            return as_weights(gathered0), started1[0:1, 0:1]
        landed = gather_forward(gather_wait(flying1, send1, recv1, h))
        return as_weights(landed), jnp.zeros((1, 1), F32)

    reduced = [None] * DEPTH
    flying = {}

    def finish_reduce(l, after):
        partials, parts = scatter_wait(*flying[l], after)
        reduced[l] = sibling_share_layer([sum_chips(a, p, where) for a, p in zip(partials, parts)])

    def start_scatter(l, full, others):
        send, recv, thru, lands, started = scatter_start([add_own_half(g, o, c_arr) for g, o in zip(full, others)])
        flying[l] = (thru, lands, send, recv)
        return started[0:1, 0:1]

    m3 = [tr_(m_w_in), m_w_branch.reshape(w3[1].shape), m_w_out]
    v3 = [tr_(v_w_in), v_w_branch.reshape(w3[1].shape), v_w_out]
    big = [None] * 3

    def adamw_of(l, after):
        for t in range(3):
            big[t] = adamw_layer(w3[t], reduced[l][t].reshape(w3[t].shape[1:]), m3[t], v3[t], l, big[t], after)

    def grads_done(l, grads):
        full = [g.reshape(N_CHIPS, 2, g.size // (2 * N_CHIPS * g.shape[-1]), g.shape[-1]) for g in grads]
        if l == 0:
            finish_reduce(1, grads[0])
        send, recv, thru, lands, started = swap_start(full)
        if l == 1:
            flying["swap"] = (thru, lands, send, recv)
            return started[0:1, 0:1]
        adamw_of(1, started)
        return start_scatter(0, *swap_wait(thru, lands, send, recv, big[2][1]))

    def bwd_done(l, d_h):
        if l == 1:
            return start_scatter(1, *swap_wait(*flying["swap"], d_h))
        return jnp.zeros((1, 1), F32)

    d_x, sp = local_step(x, loss_target, meta_full, rel_bias, norm_pre, conv_full, attn_sinks, norm_post,
                         weights_of, grads_done, bwd_done)
    finish_reduce(0, sp)

    s_send, s_recv, s_buf, s_started = small_start(sp, 4 * px + 2 * py + pc, reduced[0][0])

    adamw_of(0, s_started)
    g_in, *u_in = [tr_(a) for a in big[0]]
    g_br, *u_br = [a.reshape(w_branch.shape) for a in big[1]]
    g_out, *u_out = big[2]

    meta_rows, rest = sum_small(small_wait(s_buf, s_send, s_recv, big[2][1]))
    o = 0
    g_meta_full = meta_rows.reshape(N_META, D_MODEL)
    g_norm_pre = rest[o:o + SP_NORM].reshape(DEPTH, D_MODEL); o += SP_NORM
    g_norm_post = rest[o:o + SP_NORM].reshape(DEPTH, D_MODEL); o += SP_NORM
    g_rel_bias = rest[o:o + N_BUCKETS, :ATT_HEADS]; o += SP_RB
    g_sinks = rest[o:o + SP_SINK, 0].reshape(DEPTH, ATT_HEADS); o += SP_SINK
    g_conv_full = rest[o:o + SP_CONV].reshape(DEPTH, 3, BRANCH_WIDTH); o += SP_CONV
    loss = rest[o, 0]
    g_meta = lax.dynamic_slice_in_dim(g_meta_full, chip * SHARD_D, SHARD_D, axis=1)
    g_conv = lax.dynamic_slice_in_dim(g_conv_full, chip * LANES, LANES, axis=2)

    to2 = lambda a: a.reshape(-1, a.shape[-1])
    smalls = [(meta_tokens, g_meta, m_meta_tokens, v_meta_tokens),
              (rel_bias, g_rel_bias, m_rel_bias, v_rel_bias),
              (norm_pre, g_norm_pre, m_norm_pre, v_norm_pre),
              (to2(conv_w), to2(g_conv), to2(m_conv_w), to2(v_conv_w)),
              (attn_sinks, g_sinks, m_attn_sinks, v_attn_sinks),
              (norm_post, g_norm_post, m_norm_post, v_norm_post)]
    u_meta, u_rb, u_npre, u_conv, u_sink, u_npost = adamw_small(smalls)
    u_conv = tuple(a.reshape(conv_w.shape) for a in u_conv)

    grads = [g_meta, g_rel_bias, g_norm_pre, g_in, g_conv, g_sinks, g_br, g_out, g_norm_post]
    upd = [u_meta, u_rb, u_npre, u_in, u_conv, u_sink, u_br, u_out, u_npost]
    return (loss, d_x, *grads, *[u[0] for u in upd], *[u[1] for u in upd], *[u[2] for u in upd])
```

```python
import functools
import math

import numpy as np
import jax
import jax.numpy as jnp
from jax import lax
from jax.experimental import pallas as pl
from jax.experimental.pallas import tpu as pltpu

F32 = jnp.float32
BF16 = jnp.bfloat16
MESH = pl.DeviceIdType.MESH

D_MODEL = 1024
DEPTH = 2
N_META = 16
BLOCK = 128
PAD_FRONT = BLOCK - N_META
ATT_HEADS = 8
ATT_HEAD_DIM = 64
N_BUCKETS = 32
MAX_EXACT = 16
MAX_DISTANCE = 128
RET_HEADS = 4
ROT_BASE = 10000.0
N_BRANCH = 3
BRANCH_WIDTH = 512
PROJ_WIDTH = 8448
ABC_WIDTH = 5376
MERGE_WIDTH = N_BRANCH * D_MODEL
RMS_EPS = 1e-6
GN_EPS = 1e-6
NEG_INF = -1e30
ATT_SCALE = ATT_HEAD_DIM ** -0.5
RET_SCALE = BLOCK ** -0.5
LOG_GAMMA = tuple(math.log1p(-(2.0 ** (-5.0 - h))) for h in range(RET_HEADS))

C_AQ, C_AK, C_AV, C_AG = 0, 512, 640, 768
C_RQ, C_RK, C_RV, C_RG = 1280, 1792, 2304, 2816
C_CB, C_CC, C_CX, C_CG = 3328, 3840, 4352, 4864

ADAM_LR = 0.001
ADAM_B1 = 0.9
ADAM_B2 = 0.999
ADAM_EPS = 1e-08
ADAM_WD = 0.01
ADAM_STEP = 10

N_CHIPS = 4
SHARD_IN = PROJ_WIDTH // N_CHIPS
SHARD_D = D_MODEL // N_CHIPS
LANES = 128
PACK_IN = D_MODEL * SHARD_IN
PACK_BR = N_BRANCH * BRANCH_WIDTH * SHARD_D
PACK_OUT = SHARD_D * D_MODEL
PACK_ROWS = (PACK_IN + PACK_BR + PACK_OUT) // LANES

VMEM_LIMIT = 56 * 1024 * 1024
COL_TILE = 768
ROW_TILE = 1088


def _cparams(*sem):
    return pltpu.CompilerParams(dimension_semantics=sem, vmem_limit_bytes=VMEM_LIMIT)


def _nt(a, b):
    return lax.dot_general(a, b, (((1,), (1,)), ((), ())), preferred_element_type=F32)


def _tn(a, b):
    return lax.dot_general(a, b, (((0,), (0,)), ((), ())), preferred_element_type=F32)


def _nn(a, b):
    return jnp.dot(a, b, preferred_element_type=F32)


def _sigmoid(x):
    return 0.5 * jnp.tanh(0.5 * x) + 0.5


def _silu(x):
    return x * _sigmoid(x)


def _dsilu(x):
    s = _sigmoid(x)
    return s * (1.0 + x * (1.0 - s))


def _bucket_table():
    r = np.arange(BLOCK)[:, None]
    c = np.arange(2 * BLOCK)[None, :]
    n = np.maximum(BLOCK + r - c, 0)
    nf = np.maximum(n, 1).astype(np.float32)
    large = MAX_EXACT + (np.log(nf / MAX_EXACT) / math.log(MAX_DISTANCE / MAX_EXACT)
                         * (N_BUCKETS - MAX_EXACT)).astype(np.int32)
    large = np.minimum(large, N_BUCKETS - 1)
    return np.where(n < MAX_EXACT, n, large).astype(np.int32)


def _rot_tables(lp):
    half = BLOCK // 2
    pos = (jnp.arange(lp) - PAD_FRONT).astype(F32)
    theta = 1.0 / (ROT_BASE ** jnp.linspace(0.0, 1.0, half, dtype=F32))
    ang = pos[:, None] * theta[None, :]
    cos, sin = jnp.cos(ang), jnp.sin(ang)
    return jnp.concatenate([cos, cos], axis=1), jnp.concatenate([-sin, sin], axis=1)


def norm_matmul(x2d, g, w, col0_blocks, n_col_blocks):
    t = x2d.shape[0]
    tm = ROW_TILE if t % ROW_TILE == 0 else BLOCK

    def body(x_ref, g_ref, w_ref, hb_ref, o_ref):
        @pl.when(pl.program_id(1) == 0)
        def _():
            x = x_ref[...]
            r = lax.rsqrt(jnp.mean(x * x, axis=-1, keepdims=True) + RMS_EPS)
            hb_ref[...] = (x * r * g_ref[...]).astype(BF16)

        o_ref[...] = _nt(hb_ref[...], w_ref[...]).astype(BF16)

    return pl.pallas_call(
        body, name="norm_matmul",
        grid=(t // tm, n_col_blocks),
        in_specs=[pl.BlockSpec((tm, D_MODEL), lambda i, j: (i, 0)),
                  pl.BlockSpec((1, D_MODEL), lambda i, j: (0, 0)),
                  pl.BlockSpec((COL_TILE, D_MODEL), lambda i, j: (j + col0_blocks, 0))],
        out_specs=[pl.BlockSpec((tm, D_MODEL), lambda i, j: (i, 0)),
                   pl.BlockSpec((tm, COL_TILE), lambda i, j: (i, j))],
        out_shape=[jax.ShapeDtypeStruct((t, D_MODEL), BF16),
                   jax.ShapeDtypeStruct((t, n_col_blocks * COL_TILE), BF16)],
        compiler_params=_cparams("parallel", "arbitrary"),
    )(x2d, g, w)


def matmul_cols(a, w, col0_blocks, n_col_blocks):
    t, k = a.shape
    tm = ROW_TILE if t % ROW_TILE == 0 else BLOCK

    def body(a_ref, w_ref, o_ref):
        o_ref[...] = _nt(a_ref[...], w_ref[...]).astype(BF16)

    return pl.pallas_call(
        body, name="matmul_cols",
        grid=(t // tm, n_col_blocks),
        in_specs=[pl.BlockSpec((tm, k), lambda i, j: (i, 0)),
                  pl.BlockSpec((COL_TILE, k), lambda i, j: (j + col0_blocks, 0))],
        out_specs=pl.BlockSpec((tm, COL_TILE), lambda i, j: (i, j)),
        out_shape=jax.ShapeDtypeStruct((t, n_col_blocks * COL_TILE), BF16),
        compiler_params=_cparams("parallel", "arbitrary"),
    )(a, w)


class _Widened:
    def __init__(self, ref):
        self.ref = ref

    def __getitem__(self, idx):
        return self.ref[idx].astype(F32)


def _build_bias(bkt_ref, rb_ref, bias_s):
    bkt = bkt_ref[...]
    for h in range(ATT_HEADS):
        acc = jnp.zeros((BLOCK, 2 * BLOCK), F32)
        for b in range(N_BUCKETS):
            acc = jnp.where(bkt == b, rb_ref[b, h], acc)
        bias_s[h] = acc


def _band_mask(n):
    r = lax.broadcasted_iota(jnp.int32, (BLOCK, 2 * BLOCK), 0)
    c = lax.broadcasted_iota(jnp.int32, (BLOCK, 2 * BLOCK), 1)
    key_pos = (n - 1) * BLOCK + c
    return (c > r) & (c <= r + BLOCK) & (key_pos >= PAD_FRONT)


def _split_heads(kv, kh):
    lane = lax.broadcasted_iota(jnp.int32, kv.shape, 1)
    if kh == 0:
        lo = jnp.where(lane < ATT_HEAD_DIM, kv, 0.0)
        hi = pltpu.roll(lo, ATT_HEAD_DIM, 1)
    else:
        hi = jnp.where(lane >= ATT_HEAD_DIM, kv, 0.0)
        lo = pltpu.roll(hi, ATT_HEAD_DIM, 1)
    return lo, hi


def _merge_heads(acc_lo, acc_hi, kh):
    lane = lax.broadcasted_iota(jnp.int32, acc_lo.shape, 1)
    if kh == 0:
        return jnp.where(lane < ATT_HEAD_DIM, acc_lo + pltpu.roll(acc_hi, ATT_HEAD_DIM, 1), 0.0)
    return jnp.where(lane >= ATT_HEAD_DIM, acc_hi + pltpu.roll(acc_lo, ATT_HEAD_DIM, 1), 0.0)


def _softmax_sink(q2b, kxb, bias_h, mask, sink_h):
    return _softmax_of(_nt(q2b, kxb), bias_h, mask, sink_h)


def _softmax_of(qk, bias_h, mask, sink_h):
    s = qk * ATT_SCALE + bias_h
    s = jnp.where(mask, s, NEG_INF)
    m = jnp.maximum(jnp.max(s, axis=-1, keepdims=True), sink_h)
    p = jnp.exp(s - m)
    es = jnp.exp(sink_h - m)
    inv = 1.0 / (jnp.sum(p, axis=-1, keepdims=True) + es)
    return p * inv, es * inv


def _rot(t, cosf, sinf):
    return t * cosf + pltpu.roll(t, BLOCK // 2, 1) * sinf


def _rot_t(d, cosf, sinf):
    return d * cosf + pltpu.roll(d * sinf, BLOCK // 2, 1)


def _decay_tables(h):
    lg = LOG_GAMMA[h]
    i = lax.broadcasted_iota(jnp.int32, (BLOCK, BLOCK), 0)
    j = lax.broadcasted_iota(jnp.int32, (BLOCK, BLOCK), 1)
    diff = (i - j).astype(F32)
    dm = jnp.where(diff >= 0, jnp.exp(diff * lg), 0.0)
    row = lax.broadcasted_iota(jnp.int32, (BLOCK, 1), 0).astype(F32)
    zeta = jnp.exp((BLOCK - 1 - row) * lg)
    xi = jnp.exp((row + 1.0) * lg)
    return dm, zeta, xi, math.exp(BLOCK * lg)


def _valid_col(n):
    row = lax.broadcasted_iota(jnp.int32, (BLOCK, 1), 0)
    return ((n * BLOCK + row) >= PAD_FRONT).astype(F32)


def _shift_down(cur, prev, k):
    row = lax.broadcasted_iota(jnp.int32, cur.shape, 0)
    return jnp.where(row >= k, pltpu.roll(cur, k, 0), pltpu.roll(prev, k, 0))


def _shift_up(cur, nxt, k):
    row = lax.broadcasted_iota(jnp.int32, cur.shape, 0)
    return jnp.where(row < BLOCK - k, pltpu.roll(cur, BLOCK - k, 0), pltpu.roll(nxt, BLOCK - k, 0))


def mixers_fwd(proj, cosf, sinf, bkt, rel_bias, sinks, conv_w, nb, nc):
    def body(p_ref, cos_ref, sin_ref, bkt_ref, rb_ref, sk_ref, cw_ref, br_ref, st_ref,
             bias_s, kv_s, state_s, u_s):
        p_ref = _Widened(p_ref)
        n = pl.program_id(0)

        @pl.when(n == 0)
        def _():
            _build_bias(bkt_ref, rb_ref, bias_s)
            kv_s[:, 0:BLOCK, :] = jnp.zeros((nb, BLOCK, 2 * BLOCK), F32)
            state_s[...] = jnp.zeros_like(state_s)
            u_s[...] = jnp.zeros_like(u_s)

        valid = _valid_col(n)
        mask = _band_mask(n)
        ex = range(nb)

        for b in ex:
            kv_s[b, BLOCK:2 * BLOCK, :] = p_ref[b, :, C_AK:C_AK + 2 * BLOCK]
        for kh in range(2):
            ks = [[t.astype(BF16) for t in _split_heads(kv_s[b, :, 0:BLOCK], kh)] for b in ex]
            vs = [[t.astype(BF16) for t in _split_heads(kv_s[b, :, BLOCK:2 * BLOCK], kh)] for b in ex]
            pairs = [(b, 2 * kh + jj) for jj in range(2) for b in ex]
            subs = [(b, j, x) for (b, j) in pairs for x in range(2)]
            qb_ = {(b, j): p_ref[b, :, C_AQ + BLOCK * j:C_AQ + BLOCK * (j + 1)].astype(BF16) for (b, j) in pairs}
            qk_ = {(b, j, x): _nt(qb_[(b, j)], ks[b][x]) for (b, j, x) in subs}
            pb_ = {}
            for u in subs:
                h = 2 * u[1] + u[2]
                pb_[u] = _softmax_of(qk_[u], bias_s[h], mask, sk_ref[0, h])[0].astype(BF16)
            o_ = {u: _nn(pb_[u], vs[u[0]][u[2]]) for u in subs}
            for (b, j) in pairs:
                gate = p_ref[b, :, C_AG + BLOCK * j:C_AG + BLOCK * (j + 1)]
                br_ref[b, :, BLOCK * j:BLOCK * (j + 1)] = ((o_[(b, j, 0)] + o_[(b, j, 1)]) * _silu(gate)).astype(BF16)
        for b in ex:
            kv_s[b, 0:BLOCK, :] = kv_s[b, BLOCK:2 * BLOCK, :]

        cosv = cos_ref[...]
        sinv = sin_ref[...]
        tabs = [_decay_tables(h) for h in range(RET_HEADS)]
        units = [(b, h) for h in range(RET_HEADS) for b in ex]
        sl = lambda c0, h: slice(c0 + BLOCK * h, c0 + BLOCK * (h + 1))
        q_, k_, v_, sp_ = {}, {}, {}, {}
        for u in units:
            b, h = u
            q_[u] = _rot(p_ref[b, :, sl(C_RQ, h)], cosv, sinv).astype(BF16)
            k_[u] = (_rot(p_ref[b, :, sl(C_RK, h)], cosv, sinv) * RET_SCALE * valid).astype(BF16)
            v_[u] = p_ref[b, :, sl(C_RV, h)]
            sp_[u] = state_s[b, h]
            st_ref[b, 0, h] = sp_[u]
        qk_ = {u: _nt(q_[u], k_[u]) for u in units}
        qs_ = {u: _nn(q_[u], sp_[u].astype(BF16)) for u in units}
        kv_ = {u: _tn(k_[u], (v_[u] * tabs[u[1]][1]).astype(BF16)) for u in units}
        a_ = {u: (qk_[u] * tabs[u[1]][0]).astype(BF16) for u in units}
        av_ = {u: _nn(a_[u], v_[u].astype(BF16)) for u in units}
        for u in units:
            b, h = u
            o = av_[u] + tabs[h][2] * qs_[u]
            mu = jnp.mean(o, axis=-1, keepdims=True)
            var = jnp.mean(jnp.square(o - mu), axis=-1, keepdims=True)
            oh = (o - mu) * lax.rsqrt(var + GN_EPS)
            gate = p_ref[b, :, sl(C_RG, h)]
            br_ref[b, :, BRANCH_WIDTH + BLOCK * h:BRANCH_WIDTH + BLOCK * (h + 1)] = (oh * _silu(gate)).astype(BF16)
            state_s[b, h] = tabs[h][3] * sp_[u] + kv_[u]

        for b in ex:
            u = p_ref[b, :, C_CC:C_CC + BRANCH_WIDTH] * p_ref[b, :, C_CX:C_CX + BRANCH_WIDTH] * valid
            u_prev = u_s[b]
            y = (cw_ref[0:1, :] * _shift_down(u, u_prev, 2) + cw_ref[1:2, :] * _shift_down(u, u_prev, 1)
                 + cw_ref[2:3, :] * u)
            yc = p_ref[b, :, C_CB:C_CB + BRANCH_WIDTH] * y * _silu(p_ref[b, :, C_CG:C_CG + BRANCH_WIDTH])
            br_ref[b, :, 2 * BRANCH_WIDTH:3 * BRANCH_WIDTH] = yc.astype(BF16)
            u_s[b] = u

    lp = nc * BLOCK
    smem = pl.BlockSpec(memory_space=pltpu.SMEM)
    br, states = pl.pallas_call(
        body, name="mixers_fwd",
        grid=(nc,),
        in_specs=[pl.BlockSpec((nb, BLOCK, ABC_WIDTH), lambda n: (0, n, 0)),
                  pl.BlockSpec((BLOCK, BLOCK), lambda n: (n, 0)),
                  pl.BlockSpec((BLOCK, BLOCK), lambda n: (n, 0)),
                  pl.BlockSpec((BLOCK, 2 * BLOCK), lambda n: (0, 0)),
                  smem, smem,
                  pl.BlockSpec((3, BRANCH_WIDTH), lambda n: (0, 0))],
        out_specs=[pl.BlockSpec((nb, BLOCK, N_BRANCH * BRANCH_WIDTH), lambda n: (0, n, 0)),
                   pl.BlockSpec((nb, 1, RET_HEADS, BLOCK, BLOCK), lambda n: (0, n, 0, 0, 0))],
        out_shape=[jax.ShapeDtypeStruct((nb, lp, N_BRANCH * BRANCH_WIDTH), BF16),
                   jax.ShapeDtypeStruct((nb, nc, RET_HEADS, BLOCK, BLOCK), F32)],
        scratch_shapes=[pltpu.VMEM((ATT_HEADS, BLOCK, 2 * BLOCK), F32),
                        pltpu.VMEM((nb, 2 * BLOCK, 2 * BLOCK), F32),
                        pltpu.VMEM((nb, RET_HEADS, BLOCK, BLOCK), F32),
                        pltpu.VMEM((nb, BLOCK, BRANCH_WIDTH), F32)],
        compiler_params=_cparams("arbitrary"),
    )(proj.reshape(nb, lp, ABC_WIDTH), cosf, sinf, bkt, rel_bias, sinks, conv_w)
    return br.reshape(nb * lp, N_BRANCH * BRANCH_WIDTH), states


def mixers_bwd(proj, d_br, states, cosf, sinf, bkt, rel_bias, sinks, conv_w, nb, nc):
    def body(p_ref, kvp_ref, cp_ref, dbr_ref, st_ref, cos_ref, sin_ref, bkt_ref, rb_ref, sk_ref, cw_ref,
             dp_ref, drb_ref, dsk_ref, dcw_ref,
             bias_s, dbias_s, dkv_s, g_s, dy_s):
        p_ref, kvp_ref, cp_ref, dbr_ref = [_Widened(r) for r in (p_ref, kvp_ref, cp_ref, dbr_ref)]
        step = pl.program_id(0)
        n = nc - 1 - step
        ex = range(nb)

        @pl.when(step == 0)
        def _():
            _build_bias(bkt_ref, rb_ref, bias_s)
            dbias_s[...] = jnp.zeros_like(dbias_s)
            dsk_ref[...] = jnp.zeros_like(dsk_ref)
            dcw_ref[...] = jnp.zeros_like(dcw_ref)
            drb_ref[...] = jnp.zeros_like(drb_ref)
            dkv_s[...] = jnp.zeros_like(dkv_s)
            g_s[...] = jnp.zeros_like(g_s)
            dy_s[...] = jnp.zeros_like(dy_s)

        valid = _valid_col(n)
        mask = _band_mask(n)
        has_prev = (n > 0).astype(F32)

        k_all, v_all = [], []
        for b in ex:
            kv_prev = kvp_ref[b] * has_prev
            kv_cur = p_ref[b, :, C_AK:C_AK + 2 * BLOCK]
            k_all.append(jnp.concatenate([kv_prev[:, 0:BLOCK], kv_cur[:, 0:BLOCK]], axis=0))
            v_all.append(jnp.concatenate([kv_prev[:, BLOCK:], kv_cur[:, BLOCK:]], axis=0))
        zero2 = jnp.zeros((2 * BLOCK, BLOCK), F32)
        dk_tot = [zero2 for _ in ex]
        dv_tot = [zero2 for _ in ex]
        for kh in range(2):
            ks = [[t.astype(BF16) for t in _split_heads(k_all[b], kh)] for b in ex]
            vs = [[t.astype(BF16) for t in _split_heads(v_all[b], kh)] for b in ex]
            pairs = [(b, 2 * kh + jj) for jj in range(2) for b in ex]
            subs = [(b, j, x) for (b, j) in pairs for x in range(2)]
            qb_, gate_, dya_, do2_ = {}, {}, {}, {}
            for w in pairs:
                b, j = w
                qb_[w] = p_ref[b, :, C_AQ + BLOCK * j:C_AQ + BLOCK * (j + 1)].astype(BF16)
                gate_[w] = p_ref[b, :, C_AG + BLOCK * j:C_AG + BLOCK * (j + 1)]
                dya_[w] = dbr_ref[b, :, BLOCK * j:BLOCK * (j + 1)]
                do2_[w] = (dya_[w] * _silu(gate_[w])).astype(BF16)
            qk_ = {(b, j, x): _nt(qb_[(b, j)], ks[b][x]) for (b, j, x) in subs}
            dpm_ = {(b, j, x): _nt(do2_[(b, j)], vs[b][x]) for (b, j, x) in subs}
            pb_, dsb_ = {}, {}
            for u in subs:
                b, j, x = u
                h = 2 * j + x
                p, p_sink = _softmax_of(qk_[u], bias_s[h], mask, sk_ref[0, h])
                pb_[u] = p.astype(BF16)
                delta = jnp.sum(p * dpm_[u], axis=-1, keepdims=True)
                ds = p * (dpm_[u] - delta)
                dbias_s[h] += ds
                dsk_ref[h:h + 1, :] += jnp.broadcast_to(
                    jnp.sum(-p_sink * delta, axis=0, keepdims=True), (1, BLOCK))
                dsb_[u] = ds.astype(BF16)
            o_ = {u: _nn(pb_[u], vs[u[0]][u[2]]) for u in subs}
            dq_ = {u: _nn(dsb_[u], ks[u[0]][u[2]]) for u in subs}
            dkm_ = {u: _tn(dsb_[u], qb_[(u[0], u[1])]) for u in subs}
            dvm_ = {u: _tn(pb_[u], do2_[(u[0], u[1])]) for u in subs}
            for w in pairs:
                b, j = w
                o2 = o_[(b, j, 0)] + o_[(b, j, 1)]
                dq2 = (dq_[(b, j, 0)] + dq_[(b, j, 1)]) * ATT_SCALE
                dp_ref[b, :, C_AQ + BLOCK * j:C_AQ + BLOCK * (j + 1)] = dq2.astype(BF16)
                dp_ref[b, :, C_AG + BLOCK * j:C_AG + BLOCK * (j + 1)] = (
                    dya_[w] * o2 * _dsilu(gate_[w])).astype(BF16)
            for b in ex:
                j0, j1 = 2 * kh, 2 * kh + 1
                dk_lo = (dkm_[(b, j0, 0)] + dkm_[(b, j1, 0)]) * ATT_SCALE
                dk_hi = (dkm_[(b, j0, 1)] + dkm_[(b, j1, 1)]) * ATT_SCALE
                dk_tot[b] = dk_tot[b] + _merge_heads(dk_lo, dk_hi, kh)
                dv_tot[b] = dv_tot[b] + _merge_heads(dvm_[(b, j0, 0)] + dvm_[(b, j1, 0)],
                                                     dvm_[(b, j0, 1)] + dvm_[(b, j1, 1)], kh)
        for b in ex:
            dp_ref[b, :, C_AK:C_AK + BLOCK] = (dk_tot[b][BLOCK:, :] + dkv_s[b, :, 0:BLOCK]).astype(BF16)
            dp_ref[b, :, C_AV:C_AV + BLOCK] = (dv_tot[b][BLOCK:, :] + dkv_s[b, :, BLOCK:]).astype(BF16)
            dkv_s[b, :, 0:BLOCK] = dk_tot[b][0:BLOCK, :]
            dkv_s[b, :, BLOCK:] = dv_tot[b][0:BLOCK, :]

        cosv = cos_ref[...]
        sinv = sin_ref[...]
        tabs = [_decay_tables(h) for h in range(RET_HEADS)]
        units = [(b, h) for h in range(RET_HEADS) for b in ex]
        sl = lambda c0, h: slice(c0 + BLOCK * h, c0 + BLOCK * (h + 1))
        q_, k_, v_, vb_, sp_ = {}, {}, {}, {}, {}
        for u in units:
            b, h = u
            q_[u] = _rot(p_ref[b, :, sl(C_RQ, h)], cosv, sinv).astype(BF16)
            k_[u] = (_rot(p_ref[b, :, sl(C_RK, h)], cosv, sinv) * RET_SCALE * valid).astype(BF16)
            v_[u] = p_ref[b, :, sl(C_RV, h)]
            vb_[u] = v_[u].astype(BF16)
            sp_[u] = st_ref[b, 0, h].astype(BF16)
        qk_ = {u: _nt(q_[u], k_[u]) for u in units}
        qs_ = {u: _nn(q_[u], sp_[u]) for u in units}
        a_ = {u: (qk_[u] * tabs[u[1]][0]).astype(BF16) for u in units}
        av_ = {u: _nn(a_[u], vb_[u]) for u in units}
        dob_, dxo_ = {}, {}
        for u in units:
            b, h = u
            xi = tabs[h][2]
            o = av_[u] + xi * qs_[u]
            mu = jnp.mean(o, axis=-1, keepdims=True)
            var = jnp.mean(jnp.square(o - mu), axis=-1, keepdims=True)
            rstd = lax.rsqrt(var + GN_EPS)
            oh = (o - mu) * rstd
            gate = p_ref[b, :, sl(C_RG, h)]
            d_yr = dbr_ref[b, :, BRANCH_WIDTH + BLOCK * h:BRANCH_WIDTH + BLOCK * (h + 1)]
            dp_ref[b, :, sl(C_RG, h)] = (d_yr * oh * _dsilu(gate)).astype(BF16)
            doh = d_yr * _silu(gate)
            do = rstd * (doh - jnp.mean(doh, axis=-1, keepdims=True)
                         - oh * jnp.mean(doh * oh, axis=-1, keepdims=True))
            dob_[u] = do.astype(BF16)
            dxo_[u] = (do * xi).astype(BF16)
        dov_ = {u: _nt(dob_[u], vb_[u]) for u in units}
        dv1_ = {u: _tn(a_[u], dob_[u]) for u in units}
        dq1_ = {u: _nt(dxo_[u], sp_[u]) for u in units}
        gq_ = {u: _tn(q_[u], dxo_[u]) for u in units}
        da_, gb_, zv_ = {}, {}, {}
        for u in units:
            b, h = u
            da_[u] = (dov_[u] * tabs[h][0]).astype(BF16)
            g_next = g_s[b, h]
            gb_[u] = g_next.astype(BF16)
            zv_[u] = (v_[u] * tabs[h][1]).astype(BF16)
            g_s[b, h] = tabs[h][3] * g_next + gq_[u]
        dq2_ = {u: _nn(da_[u], k_[u]) for u in units}
        dk1_ = {u: _tn(da_[u], q_[u]) for u in units}
        dk2_ = {u: _nt(zv_[u], gb_[u]) for u in units}
        dv2_ = {u: _nn(k_[u], gb_[u]) for u in units}
        for u in units:
            b, h = u
            dp_ref[b, :, sl(C_RQ, h)] = _rot_t(dq2_[u] + dq1_[u], cosv, sinv).astype(BF16)
            dp_ref[b, :, sl(C_RK, h)] = _rot_t((dk1_[u] + dk2_[u]) * (RET_SCALE * valid), cosv, sinv).astype(BF16)
            dp_ref[b, :, sl(C_RV, h)] = (dv1_[u] + tabs[h][1] * dv2_[u]).astype(BF16)

        w0, w1, w2 = cw_ref[0:1, :], cw_ref[1:2, :], cw_ref[2:3, :]
        for b in ex:
            cb = p_ref[b, :, C_CB:C_CB + BRANCH_WIDTH]
            cc = p_ref[b, :, C_CC:C_CC + BRANCH_WIDTH]
            cx = p_ref[b, :, C_CX:C_CX + BRANCH_WIDTH]
            cg = p_ref[b, :, C_CG:C_CG + BRANCH_WIDTH]
            u = cc * cx * valid
            u_prev = (cp_ref[b, :, 0:BRANCH_WIDTH] * cp_ref[b, :, BRANCH_WIDTH:2 * BRANCH_WIDTH]
                      * (_valid_col(n - 1) * has_prev))
            u1 = _shift_down(u, u_prev, 1)
            u2 = _shift_down(u, u_prev, 2)
            y = w0 * u2 + w1 * u1 + w2 * u
            d_yc = dbr_ref[b, :, 2 * BRANCH_WIDTH:3 * BRANCH_WIDTH]
            sg = _silu(cg)
            dp_ref[b, :, C_CB:C_CB + BRANCH_WIDTH] = (d_yc * y * sg).astype(BF16)
            dp_ref[b, :, C_CG:C_CG + BRANCH_WIDTH] = (d_yc * cb * y * _dsilu(cg)).astype(BF16)
            dy = d_yc * cb * sg
            dy_next = dy_s[b]
            du = (w2 * dy + w1 * _shift_up(dy, dy_next, 1) + w0 * _shift_up(dy, dy_next, 2)) * valid
            dp_ref[b, :, C_CC:C_CC + BRANCH_WIDTH] = (du * cx).astype(BF16)
            dp_ref[b, :, C_CX:C_CX + BRANCH_WIDTH] = (du * cc).astype(BF16)
            dcw_ref[0:1, :] += jnp.sum(dy * u2, axis=0, keepdims=True)
            dcw_ref[1:2, :] += jnp.sum(dy * u1, axis=0, keepdims=True)
            dcw_ref[2:3, :] += jnp.sum(dy * u, axis=0, keepdims=True)
            dy_s[b] = dy

        @pl.when(step == nc - 1)
        def _():
            bkt = bkt_ref[...]
            row = lax.broadcasted_iota(jnp.int32, (N_BUCKETS, BLOCK), 0)
            lane = lax.broadcasted_iota(jnp.int32, (N_BUCKETS, BLOCK), 1)

            def one_bucket(bk, acc):
                sel = bkt == bk
                for h in range(ATT_HEADS):
                    t = jnp.where(sel, dbias_s[h], 0.0)
                    s = jnp.sum(jnp.sum(t, axis=1, keepdims=True), axis=0, keepdims=True)
                    acc = acc + jnp.where((row == bk) & (lane == h), jnp.broadcast_to(s, acc.shape), 0.0)
                return acc

            drb_ref[...] = lax.fori_loop(0, N_BUCKETS, one_bucket, jnp.zeros((N_BUCKETS, BLOCK), F32))

    lp = nc * BLOCK
    smem = pl.BlockSpec(memory_space=pltpu.SMEM)
    blk = lambda s: nc - 1 - s
    prev = lambda s: jnp.maximum(nc - 2 - s, 0)
    proj3 = proj.reshape(nb, lp, ABC_WIDTH)
    res = pl.pallas_call(
        body, name="mixers_bwd",
        grid=(nc,),
        in_specs=[pl.BlockSpec((nb, BLOCK, ABC_WIDTH), lambda s: (0, blk(s), 0)),
                  pl.BlockSpec((nb, BLOCK, 2 * BLOCK), lambda s: (0, prev(s), C_AK // (2 * BLOCK))),
                  pl.BlockSpec((nb, BLOCK, 1280), lambda s: (0, prev(s), C_CC // 1280)),
                  pl.BlockSpec((nb, BLOCK, N_BRANCH * BRANCH_WIDTH), lambda s: (0, blk(s), 0)),
                  pl.BlockSpec((nb, 1, RET_HEADS, BLOCK, BLOCK), lambda s: (0, blk(s), 0, 0, 0)),
                  pl.BlockSpec((BLOCK, BLOCK), lambda s: (blk(s), 0)),
                  pl.BlockSpec((BLOCK, BLOCK), lambda s: (blk(s), 0)),
                  pl.BlockSpec((BLOCK, 2 * BLOCK), lambda s: (0, 0)),
                  smem, smem,
                  pl.BlockSpec((3, BRANCH_WIDTH), lambda s: (0, 0))],
        out_specs=[pl.BlockSpec((nb, BLOCK, ABC_WIDTH), lambda s: (0, blk(s), 0)),
                   pl.BlockSpec((N_BUCKETS, BLOCK), lambda s: (0, 0)),
                   pl.BlockSpec((ATT_HEADS, BLOCK), lambda s: (0, 0)),
                   pl.BlockSpec((8, BRANCH_WIDTH), lambda s: (0, 0))],
        out_shape=[jax.ShapeDtypeStruct((nb, lp, ABC_WIDTH), BF16),
                   jax.ShapeDtypeStruct((N_BUCKETS, BLOCK), F32),
                   jax.ShapeDtypeStruct((ATT_HEADS, BLOCK), F32),
                   jax.ShapeDtypeStruct((8, BRANCH_WIDTH), F32)],
        scratch_shapes=[pltpu.VMEM((ATT_HEADS, BLOCK, 2 * BLOCK), F32),
                        pltpu.VMEM((ATT_HEADS, BLOCK, 2 * BLOCK), F32),
                        pltpu.VMEM((nb, BLOCK, 2 * BLOCK), F32),
                        pltpu.VMEM((nb, RET_HEADS, BLOCK, BLOCK), F32),
                        pltpu.VMEM((nb, BLOCK, BRANCH_WIDTH), F32)],
        compiler_params=_cparams("arbitrary"),
    )(proj3, proj3, proj3, d_br.reshape(nb, lp, N_BRANCH * BRANCH_WIDTH), states, cosf, sinf, bkt, rel_bias, sinks,
      conv_w)
    return (res[0].reshape(nb * lp, ABC_WIDTH),) + tuple(res[1:])


MERGE_TILE = 256


def _merge_forward(br_ref, m_ref, wb_ref, wo_ref):
    bo, gates = [], []
    mixed_pre = None
    for g in range(N_BRANCH):
        br_g = br_ref[:, BRANCH_WIDTH * g:BRANCH_WIDTH * (g + 1)]
        bo_g = jnp.concatenate([_nn(br_g, wb_ref[p, g]) for p in range(N_CHIPS)], axis=1)
        gate_g = _sigmoid(m_ref[:, D_MODEL * g:D_MODEL * (g + 1)].astype(F32))
        bo.append(bo_g)
        gates.append(gate_g)
        mixed_pre = gate_g * bo_g if mixed_pre is None else mixed_pre + gate_g * bo_g
    mixed = _nn(mixed_pre.astype(BF16), wo_ref[...])
    r = lax.rsqrt(jnp.mean(mixed * mixed, axis=-1, keepdims=True) + RMS_EPS)
    return bo, gates, mixed_pre, mixed, r


def merge_fwd(x2d, br, pm, wb, wo, g_post):
    t = x2d.shape[0]
    tm = MERGE_TILE if t % MERGE_TILE == 0 else BLOCK

    def body(x_ref, br_ref, m_ref, wb_ref, wo_ref, g_ref, o_ref):
        _, _, _, mixed, r = _merge_forward(br_ref, m_ref, wb_ref, wo_ref)
        o_ref[...] = x_ref[...] + mixed * r * g_ref[...]

    return pl.pallas_call(
        body, name="merge_fwd",
        grid=(t // tm,),
        in_specs=[pl.BlockSpec((tm, D_MODEL), lambda i: (i, 0)),
                  pl.BlockSpec((tm, N_BRANCH * BRANCH_WIDTH), lambda i: (i, 0)),
                  pl.BlockSpec((tm, MERGE_WIDTH), lambda i: (i, 0)),
                  pl.BlockSpec((N_CHIPS, N_BRANCH, BRANCH_WIDTH, SHARD_D), lambda i: (0, 0, 0, 0)),
                  pl.BlockSpec((D_MODEL, D_MODEL), lambda i: (0, 0)),
                  pl.BlockSpec((1, D_MODEL), lambda i: (0, 0))],
        out_specs=pl.BlockSpec((tm, D_MODEL), lambda i: (i, 0)),
        out_shape=jax.ShapeDtypeStruct((t, D_MODEL), F32),
        compiler_params=_cparams("parallel"),
    )(x2d, br, pm, wb, wo, g_post)


def merge_bwd(d_out, br, pm, wb, wo, g_post):
    t = d_out.shape[0]
    tm = MERGE_TILE if t % MERGE_TILE == 0 else BLOCK

    def body(do_ref, br_ref, m_ref, wb_ref, wo_ref, g_ref, dbr_ref, dm_ref, dg_ref, dwb_ref, dwo_ref):

        @pl.when(pl.program_id(0) == 0)
        def _():
            dwb_ref[...] = jnp.zeros_like(dwb_ref)
            dwo_ref[...] = jnp.zeros_like(dwo_ref)
            dg_ref[...] = jnp.zeros_like(dg_ref)

        bo, gates, mixed_pre, mixed, r = _merge_forward(br_ref, m_ref, wb_ref, wo_ref)
        d_o = do_ref[...]
        nh = mixed * r
        dg_ref[0:1, :] += jnp.sum(d_o * nh, axis=0, keepdims=True)
        dn = d_o * g_ref[...]
        d_mixed = (r * (dn - nh * jnp.mean(dn * nh, axis=-1, keepdims=True))).astype(BF16)
        dwo_ref[...] += _tn(mixed_pre.astype(BF16), d_mixed)
        d_pre = _nt(d_mixed, wo_ref[...])
        for g in range(N_BRANCH):
            br_g = br_ref[:, BRANCH_WIDTH * g:BRANCH_WIDTH * (g + 1)]
            d_bo = (d_pre * gates[g]).astype(BF16)
            dm_ref[:, D_MODEL * g:D_MODEL * (g + 1)] = (
                d_pre * bo[g] * gates[g] * (1.0 - gates[g])).astype(BF16)
            d_br_g = None
            for p in range(N_CHIPS):
                d_bo_p = d_bo[:, SHARD_D * p:SHARD_D * (p + 1)]
                part = _nt(d_bo_p, wb_ref[p, g])
                d_br_g = part if d_br_g is None else d_br_g + part
                dwb_ref[p, g] += _tn(br_g, d_bo_p)
            dbr_ref[:, BRANCH_WIDTH * g:BRANCH_WIDTH * (g + 1)] = d_br_g.astype(BF16)

    return pl.pallas_call(
        body, name="merge_bwd",
        grid=(t // tm,),
        in_specs=[pl.BlockSpec((tm, D_MODEL), lambda i: (i, 0)),
                  pl.BlockSpec((tm, N_BRANCH * BRANCH_WIDTH), lambda i: (i, 0)),
                  pl.BlockSpec((tm, MERGE_WIDTH), lambda i: (i, 0)),
                  pl.BlockSpec((N_CHIPS, N_BRANCH, BRANCH_WIDTH, SHARD_D), lambda i: (0, 0, 0, 0)),
                  pl.BlockSpec((D_MODEL, D_MODEL), lambda i: (0, 0)),
                  pl.BlockSpec((1, D_MODEL), lambda i: (0, 0))],
        out_specs=[pl.BlockSpec((tm, N_BRANCH * BRANCH_WIDTH), lambda i: (i, 0)),
                   pl.BlockSpec((tm, MERGE_WIDTH), lambda i: (i, 0)),
                   pl.BlockSpec((8, D_MODEL), lambda i: (0, 0)),
                   pl.BlockSpec((N_CHIPS, N_BRANCH, BRANCH_WIDTH, SHARD_D), lambda i: (0, 0, 0, 0)),
                   pl.BlockSpec((D_MODEL, D_MODEL), lambda i: (0, 0))],
        out_shape=[jax.ShapeDtypeStruct((t, N_BRANCH * BRANCH_WIDTH), BF16),
                   jax.ShapeDtypeStruct((t, MERGE_WIDTH), BF16),
                   jax.ShapeDtypeStruct((8, D_MODEL), F32),
                   jax.ShapeDtypeStruct((N_CHIPS, N_BRANCH, BRANCH_WIDTH, SHARD_D), F32),
                   jax.ShapeDtypeStruct((D_MODEL, D_MODEL), F32)],
        compiler_params=_cparams("arbitrary"),
    )(d_out, br, pm, wb, wo, g_post)


def loss_head(xf, target2d, nb, nc):
    def body(x_ref, t_ref, l_ref, dx_ref):
        b = pl.program_id(0)
        n = pl.program_id(1)

        @pl.when((b == 0) & (n == 0))
        def _():
            l_ref[...] = jnp.zeros_like(l_ref)

        @pl.when(n == 0)
        def _():
            dx_ref[...] = jnp.zeros_like(dx_ref)

        @pl.when(n > 0)
        def _():
            e = x_ref[...] - t_ref[...]
            dx_ref[...] = e * (1.0 / D_MODEL)
            s = jnp.sum(jnp.sum(e * e, axis=1, keepdims=True), axis=0, keepdims=True)
            l_ref[...] += jnp.broadcast_to(s * (0.5 / D_MODEL), l_ref.shape)

    return pl.pallas_call(
        body, name="loss_head",
        grid=(nb, nc),
        in_specs=[pl.BlockSpec((BLOCK, D_MODEL), lambda b, n: (b * nc + n, 0)),
                  pl.BlockSpec((BLOCK, D_MODEL), lambda b, n: (b * (nc - 1) + jnp.maximum(n - 1, 0), 0))],
        out_specs=[pl.BlockSpec((8, BLOCK), lambda b, n: (0, 0)),
                   pl.BlockSpec((BLOCK, D_MODEL), lambda b, n: (b * nc + n, 0))],
        out_shape=[jax.ShapeDtypeStruct((8, BLOCK), F32),
                   jax.ShapeDtypeStruct(xf.shape, F32)],
        compiler_params=_cparams("arbitrary", "arbitrary"),
    )(xf, target2d)


N_ABC_TILES = ABC_WIDTH // COL_TILE
N_M_TILES = MERGE_WIDTH // COL_TILE


def proj_dgrad(d_abc, d_m, w, x2d, g, d_out):
    t = x2d.shape[0]
    tm = ROW_TILE if t % ROW_TILE == 0 else BLOCK
    nk = N_ABC_TILES + N_M_TILES

    def body(da_ref, dm_ref, w_ref, x_ref, g_ref, do_ref, dx_ref, dg_ref, acc):
        i = pl.program_id(0)
        k = pl.program_id(1)

        @pl.when((i == 0) & (k == 0))
        def _():
            dg_ref[...] = jnp.zeros_like(dg_ref)

        @pl.when(k == 0)
        def _():
            acc[...] = jnp.zeros_like(acc)

        @pl.when(k < N_ABC_TILES)
        def _():
            acc[...] += _nn(da_ref[...], w_ref[...])

        @pl.when(k >= N_ABC_TILES)
        def _():
            acc[...] += _nn(dm_ref[...], w_ref[...])

        @pl.when(k == nk - 1)
        def _():
            x = x_ref[...]
            r = lax.rsqrt(jnp.mean(x * x, axis=-1, keepdims=True) + RMS_EPS)
            nh = x * r
            dh = acc[...]
            dg_ref[0:1, :] += jnp.sum(dh * nh, axis=0, keepdims=True)
            dn = dh * g_ref[...]
            dx_ref[...] = do_ref[...] + r * (dn - nh * jnp.mean(dn * nh, axis=-1, keepdims=True))

    return pl.pallas_call(
        body, name="proj_dgrad",
        grid=(t // tm, nk),
        in_specs=[pl.BlockSpec((tm, COL_TILE), lambda i, k: (i, jnp.minimum(k, N_ABC_TILES - 1))),
                  pl.BlockSpec((tm, COL_TILE), lambda i, k: (i, jnp.maximum(k - N_ABC_TILES, 0))),
                  pl.BlockSpec((COL_TILE, D_MODEL), lambda i, k: (k, 0)),
                  pl.BlockSpec((tm, D_MODEL), lambda i, k: (i, 0)),
                  pl.BlockSpec((1, D_MODEL), lambda i, k: (0, 0)),
                  pl.BlockSpec((tm, D_MODEL), lambda i, k: (i, 0))],
        out_specs=[pl.BlockSpec((tm, D_MODEL), lambda i, k: (i, 0)),
                   pl.BlockSpec((8, D_MODEL), lambda i, k: (0, 0))],
        out_shape=[jax.ShapeDtypeStruct((t, D_MODEL), F32),
                   jax.ShapeDtypeStruct((8, D_MODEL), F32)],
        scratch_shapes=[pltpu.VMEM((tm, D_MODEL), F32)],
        compiler_params=_cparams("arbitrary", "arbitrary"),
    )(d_abc, d_m, w, x2d, g, d_out)


def proj_wgrad(hb, d_abc, d_m):
    t = hb.shape[0]
    nj = N_ABC_TILES + N_M_TILES

    def body(h_ref, da_ref, dm_ref, o_ref):
        j = pl.program_id(0)

        @pl.when(j < N_ABC_TILES)
        def _():
            o_ref[...] = _tn(da_ref[...], h_ref[...])

        @pl.when(j >= N_ABC_TILES)
        def _():
            o_ref[...] = _tn(dm_ref[...], h_ref[...])

    return pl.pallas_call(
        body, name="proj_wgrad",
        grid=(nj,),
        in_specs=[pl.BlockSpec((t, D_MODEL), lambda j: (0, 0)),
                  pl.BlockSpec((t, COL_TILE), lambda j: (0, jnp.minimum(j, N_ABC_TILES - 1))),
                  pl.BlockSpec((t, COL_TILE), lambda j: (0, jnp.maximum(j - N_ABC_TILES, 0)))],
        out_specs=pl.BlockSpec((COL_TILE, D_MODEL), lambda j: (j, 0)),
        out_shape=jax.ShapeDtypeStruct((PROJ_WIDTH, D_MODEL), F32),
        compiler_params=_cparams("arbitrary"),
    )(hb, d_abc, d_m)


def _adamw_math(w, g, m, v):
    m = ADAM_B1 * m + (1.0 - ADAM_B1) * g
    v = ADAM_B2 * v + (1.0 - ADAM_B2) * jnp.square(g)
    m_hat = m / (1.0 - ADAM_B1 ** ADAM_STEP)
    v_hat = v / (1.0 - ADAM_B2 ** ADAM_STEP)
    delta = -ADAM_LR * (m_hat / (jnp.sqrt(v_hat) + ADAM_EPS) + ADAM_WD * w)
    return delta, m, v


def adamw_layer(w, g, m, v, layer, acc, after):
    _, r, c = w.shape
    tr = _row_tile(r)

    def body(*refs):
        w_ref, g_ref, m_ref, v_ref = refs[:4]
        go_ref, d_ref, mo_ref, vo_ref = refs[-4:]
        g_val = g_ref[...]
        d, m_new, v_new = _adamw_math(w_ref[...], g_val, m_ref[...], v_ref[...])
        go_ref[...] = g_val
        d_ref[...] = d
        mo_ref[...] = m_new
        vo_ref[...] = v_new

    slab = pl.BlockSpec((None, tr, c), lambda i: (layer, i, 0))
    ins = [w, g, m, v, after]
    in_specs = [slab, pl.BlockSpec((tr, c), lambda i: (i, 0)), slab, slab, ANY]
    aliases = {}
    if acc is not None:
        ins += list(acc)
        in_specs += [ANY] * 4
        aliases = {5 + i: i for i in range(4)}
    return pl.pallas_call(
        body, name="adamw_layer",
        grid=(r // tr,),
        in_specs=in_specs, out_specs=[slab] * 4,
        out_shape=[jax.ShapeDtypeStruct(w.shape, F32)] * 4,
        input_output_aliases=aliases,
        compiler_params=_cparams("parallel"),
    )(*ins)


def adamw_small(params):
    k = len(params)

    def body(*refs):
        ins, outs = refs[:4 * k], refs[4 * k:]
        for i in range(k):
            d, m_new, v_new = _adamw_math(*[r[...] for r in ins[4 * i:4 * i + 4]])
            outs[3 * i][...] = d
            outs[3 * i + 1][...] = m_new
            outs[3 * i + 2][...] = v_new

    flat = [a for p in params for a in p]
    vm = pl.BlockSpec(memory_space=pltpu.VMEM)
    out_shape = [jax.ShapeDtypeStruct(p[0].shape, F32) for p in params for _ in range(3)]
    res = pl.pallas_call(
        body, name="adamw_small",
        in_specs=[vm] * len(flat), out_specs=[vm] * len(out_shape), out_shape=out_shape,
    )(*flat)
    return [tuple(res[3 * i:3 * i + 3]) for i in range(k)]


ANY = pl.BlockSpec(memory_space=pl.ANY)


def _place():
    return lax.axis_index("x"), lax.axis_index("y"), lax.axis_index("c")


HBM = pl.BlockSpec(memory_space=pltpu.HBM)
SEM = pl.BlockSpec(memory_space=pltpu.SEMAPHORE)
EFFECT = pltpu.SideEffectType.DATAFLOW_SIDE_EFFECTING


def _other_chips(x, y):
    return [(1 - x, y), (x, 1 - y), (1 - x, 1 - y)]


def _own_slot(shard, chip):
    buf = lax.empty((N_CHIPS,) + shard.shape, shard.dtype)
    return lax.dynamic_update_slice(buf, shard[None], (chip, 0, 0, 0))


def _hbm(a):
    return pltpu.with_memory_space_constraint(a, pltpu.HBM)


def gather_start(bufs, after):
    n = len(bufs)

    def body(*refs):
        g_refs = refs[:n]
        send_sems, recv_sems = refs[n + 1], refs[n + 2]
        token = refs[-1]
        x, y, c = _place()
        me_p = 2 * x + y
        for t in range(n):
            for k, (qx, qy) in enumerate(_other_chips(x, y)):
                slab = g_refs[t].at[me_p, c]
                pltpu.make_async_remote_copy(src_ref=slab, dst_ref=slab, send_sem=send_sems.at[3 * t + k],
                                             recv_sem=recv_sems.at[3 * t + k], device_id=(qx, qy, c),
                                             device_id_type=MESH).start()
        token[...] = jnp.zeros_like(token)

    res = pl.pallas_call(
        body, name="gather_start",
        in_specs=[HBM] * n + [ANY],
        out_specs=[SEM, SEM] + [HBM] * n + [pl.BlockSpec(memory_space=pltpu.VMEM)],
        out_shape=[pltpu.SemaphoreType.DMA((3 * n,)), pltpu.SemaphoreType.DMA((3 * n,))]
        + [pltpu.HBM(b.shape, b.dtype) for b in bufs] + [jax.ShapeDtypeStruct((8, LANES), F32)],
        input_output_aliases={t: 2 + t for t in range(n)},
        compiler_params=pltpu.CompilerParams(has_side_effects=EFFECT),
    )(*[_hbm(b) for b in bufs], after)
    return res[0], res[1], list(res[2:2 + n]), res[-1]


def gather_wait(bufs, send_sems, recv_sems, after):
    n = len(bufs)

    def body(*refs):
        g_refs = refs[:n]
        send_sems, recv_sems = refs[n], refs[n + 1]
        x, y, c = _place()
        me_p = 2 * x + y
        for t in range(n):
            for k, (qx, qy) in enumerate(_other_chips(x, y)):
                cp = pltpu.make_async_remote_copy(src_ref=g_refs[t].at[me_p, c], dst_ref=g_refs[t].at[2 * qx + qy, c],
                                                  send_sem=send_sems.at[3 * t + k], recv_sem=recv_sems.at[3 * t + k],
                                                  device_id=(qx, qy, c), device_id_type=MESH)
                cp.wait_send()
                cp.wait_recv()

    return pl.pallas_call(
        body, name="gather_wait",
        in_specs=[HBM] * n + [SEM, SEM, ANY],
        out_specs=[HBM] * n,
        out_shape=[pltpu.HBM(b.shape, b.dtype) for b in bufs],
        input_output_aliases={t: t for t in range(n)},
        compiler_params=pltpu.CompilerParams(has_side_effects=EFFECT),
    )(*bufs, send_sems, recv_sems, after)


def gather_forward(bufs):
    n = len(bufs)

    def body(*refs):
        g_refs = refs[n:2 * n]
        send_sems, recv_sems = refs[2 * n:]
        x, y, c = _place()
        sibling = (x, y, 1 - c)
        chips = _other_chips(x, y)
        passed = []
        for t in range(n):
            for k, (qx, qy) in enumerate(chips):
                slab = g_refs[t].at[2 * qx + qy, c]
                fwd = pltpu.make_async_remote_copy(src_ref=slab, dst_ref=slab, send_sem=send_sems.at[3 * t + k],
                                                   recv_sem=recv_sems.at[3 * t + k], device_id=sibling,
                                                   device_id_type=MESH)
                fwd.start()
                passed.append(fwd)
        for t in range(n):
            for k, (qx, qy) in enumerate(chips):
                slab = g_refs[t].at[2 * qx + qy, 1 - c]
                pltpu.make_async_remote_copy(src_ref=slab, dst_ref=slab, send_sem=send_sems.at[3 * t + k],
                                             recv_sem=recv_sems.at[3 * t + k], device_id=sibling,
                                             device_id_type=MESH).wait_recv()
        for cp in passed:
            cp.wait_send()

    return pl.pallas_call(
        body, name="gather_forward",
        in_specs=[ANY] * n, out_specs=[ANY] * n,
        out_shape=[jax.ShapeDtypeStruct(b.shape, b.dtype) for b in bufs],
        input_output_aliases={t: t for t in range(n)},
        scratch_shapes=[pltpu.SemaphoreType.DMA((3 * n,)), pltpu.SemaphoreType.DMA((3 * n,))],
    )(*bufs)


def forward_start(bufs):
    n = len(bufs)

    def body(*refs):
        g_refs = refs[:n]
        send_sems, recv_sems = refs[n], refs[n + 1]
        token = refs[-1]
        x, y, c = _place()
        for t in range(n):
            for k, (qx, qy) in enumerate(_other_chips(x, y)):
                slab = g_refs[t].at[2 * qx + qy, c]
                pltpu.make_async_remote_copy(src_ref=slab, dst_ref=slab, send_sem=send_sems.at[3 * t + k],
                                             recv_sem=recv_sems.at[3 * t + k], device_id=(x, y, 1 - c),
                                             device_id_type=MESH).start()
        token[...] = jnp.zeros_like(token)

    res = pl.pallas_call(
        body, name="forward_start",
        in_specs=[HBM] * n,
        out_specs=[SEM, SEM] + [HBM] * n + [pl.BlockSpec(memory_space=pltpu.VMEM)],
        out_shape=[pltpu.SemaphoreType.DMA((3 * n,)), pltpu.SemaphoreType.DMA((3 * n,))]
        + [pltpu.HBM(b.shape, b.dtype) for b in bufs] + [jax.ShapeDtypeStruct((8, LANES), F32)],
        input_output_aliases={t: 2 + t for t in range(n)},
        compiler_params=pltpu.CompilerParams(has_side_effects=EFFECT),
    )(*[_hbm(b) for b in bufs])
    return res[0], res[1], list(res[2:2 + n]), res[-1]


def forward_wait(bufs, send_sems, recv_sems, after):
    n = len(bufs)

    def body(*refs):
        g_refs = refs[:n]
        send_sems, recv_sems = refs[n], refs[n + 1]
        x, y, c = _place()
        for t in range(n):
            for k, (qx, qy) in enumerate(_other_chips(x, y)):
                cp = pltpu.make_async_remote_copy(src_ref=g_refs[t].at[2 * qx + qy, c],
                                                  dst_ref=g_refs[t].at[2 * qx + qy, 1 - c],
                                                  send_sem=send_sems.at[3 * t + k], recv_sem=recv_sems.at[3 * t + k],
                                                  device_id=(x, y, 1 - c), device_id_type=MESH)
                cp.wait_send()
                cp.wait_recv()

    return pl.pallas_call(
        body, name="forward_wait",
        in_specs=[HBM] * n + [SEM, SEM, ANY],
        out_specs=[HBM] * n,
        out_shape=[pltpu.HBM(b.shape, b.dtype) for b in bufs],
        input_output_aliases={t: t for t in range(n)},
        compiler_params=pltpu.CompilerParams(has_side_effects=EFFECT),
    )(*bufs, send_sems, recv_sems, after)


def exchange_small(pack, after):
    def body(p_ref, after_ref, o_ref, send_sems, recv_sems, local_sem):
        x, y, c = _place()
        me = 4 * x + 2 * y + c
        mine = pltpu.make_async_copy(p_ref, o_ref.at[me], local_sem)
        mine.start()
        sends = []
        for k in range(1, 8):
            fx, fy, fc = (k >> 2) & 1, (k >> 1) & 1, k & 1
            peer = (x ^ fx, y ^ fy, c ^ fc)
            cp = pltpu.make_async_remote_copy(src_ref=p_ref, dst_ref=o_ref.at[me], send_sem=send_sems.at[k - 1],
                                              recv_sem=recv_sems.at[k - 1], device_id=peer, device_id_type=MESH)
            cp.start()
            sends.append(cp)
        for k in range(1, 8):
            fx, fy, fc = (k >> 2) & 1, (k >> 1) & 1, k & 1
            peer = (x ^ fx, y ^ fy, c ^ fc)
            slot = o_ref.at[4 * peer[0] + 2 * peer[1] + peer[2]]
            pltpu.make_async_remote_copy(src_ref=slot, dst_ref=slot, send_sem=send_sems.at[k - 1],
                                         recv_sem=recv_sems.at[k - 1], device_id=peer, device_id_type=MESH).wait_recv()
        for cp in sends:
            cp.wait_send()
        mine.wait()

    return pl.pallas_call(
        body, name="exchange_small",
        in_specs=[ANY, ANY], out_specs=ANY,
        out_shape=jax.ShapeDtypeStruct((8,) + pack.shape, pack.dtype),
        scratch_shapes=[pltpu.SemaphoreType.DMA((7,)), pltpu.SemaphoreType.DMA((7,)), pltpu.SemaphoreType.DMA],
    )(pack, after)


def small_start(pack, me, after):
    buf = lax.dynamic_update_slice(lax.empty((8,) + pack.shape, pack.dtype), pack[None], (me, 0, 0))

    def body(b_ref, after_ref, send_sems, recv_sems, thru, token):
        x, y, c = _place()
        slot = b_ref.at[4 * x + 2 * y + c]
        for k in range(1, 8):
            peer = (x ^ ((k >> 2) & 1), y ^ ((k >> 1) & 1), c ^ (k & 1))
            pltpu.make_async_remote_copy(src_ref=slot, dst_ref=slot, send_sem=send_sems.at[k - 1],
                                         recv_sem=recv_sems.at[k - 1], device_id=peer, device_id_type=MESH).start()
        token[...] = jnp.zeros_like(token)

    return pl.pallas_call(
        body, name="small_start",
        in_specs=[HBM, ANY],
        out_specs=[SEM, SEM, HBM, pl.BlockSpec(memory_space=pltpu.VMEM)],
        out_shape=[pltpu.SemaphoreType.DMA((7,)), pltpu.SemaphoreType.DMA((7,)), pltpu.HBM(buf.shape, buf.dtype),
                   jax.ShapeDtypeStruct((8, LANES), F32)],
        input_output_aliases={0: 2},
        compiler_params=pltpu.CompilerParams(has_side_effects=EFFECT),
    )(_hbm(buf), after)


def small_wait(buf, send_sems, recv_sems, after):
    def body(b_ref, send_sems, recv_sems, after_ref, thru):
        x, y, c = _place()
        mine = b_ref.at[4 * x + 2 * y + c]
        for k in range(1, 8):
            peer = (x ^ ((k >> 2) & 1), y ^ ((k >> 1) & 1), c ^ (k & 1))
            cp = pltpu.make_async_remote_copy(src_ref=mine, dst_ref=b_ref.at[4 * peer[0] + 2 * peer[1] + peer[2]],
                                              send_sem=send_sems.at[k - 1], recv_sem=recv_sems.at[k - 1],
                                              device_id=peer, device_id_type=MESH)
            cp.wait_send()
            cp.wait_recv()

    return pl.pallas_call(
        body, name="small_wait",
        in_specs=[HBM, SEM, SEM, ANY], out_specs=HBM,
        out_shape=pltpu.HBM(buf.shape, buf.dtype),
        input_output_aliases={0: 0},
        compiler_params=pltpu.CompilerParams(has_side_effects=EFFECT),
    )(buf, send_sems, recv_sems, after)


def swap_start(grads):
    n = len(grads)

    def body(*refs):
        g_refs, l_refs = refs[:n], refs[n:2 * n]
        send_sems, recv_sems = refs[2 * n], refs[2 * n + 1]
        token = refs[-1]
        x, y, c = _place()
        for t in range(n):
            for p in range(N_CHIPS):
                pltpu.make_async_remote_copy(src_ref=g_refs[t].at[p, 1 - c], dst_ref=l_refs[t].at[p],
                                             send_sem=send_sems.at[N_CHIPS * t + p],
                                             recv_sem=recv_sems.at[N_CHIPS * t + p],
                                             device_id=(x, y, 1 - c), device_id_type=MESH).start()
        token[...] = jnp.zeros_like(token)

    lands = [lax.empty((N_CHIPS,) + g.shape[2:], g.dtype) for g in grads]
    res = pl.pallas_call(
        body, name="swap_start",
        in_specs=[HBM] * (2 * n),
        out_specs=[SEM, SEM] + [HBM] * (2 * n) + [pl.BlockSpec(memory_space=pltpu.VMEM)],
        out_shape=[pltpu.SemaphoreType.DMA((N_CHIPS * n,)), pltpu.SemaphoreType.DMA((N_CHIPS * n,))]
        + [pltpu.HBM(a.shape, a.dtype) for a in grads + lands] + [jax.ShapeDtypeStruct((8, LANES), F32)],
        input_output_aliases={t: 2 + t for t in range(2 * n)},
        compiler_params=pltpu.CompilerParams(has_side_effects=EFFECT),
    )(*[_hbm(a) for a in grads + lands])
    return res[0], res[1], list(res[2:2 + n]), list(res[2 + n:2 + 2 * n]), res[-1]


def swap_wait(grads, lands, send_sems, recv_sems, after):
    n = len(grads)

    def body(*refs):
        g_refs, l_refs = refs[:n], refs[n:2 * n]
        send_sems, recv_sems = refs[2 * n], refs[2 * n + 1]
        x, y, c = _place()
        for t in range(n):
            for p in range(N_CHIPS):
                cp = pltpu.make_async_remote_copy(src_ref=g_refs[t].at[p, 1 - c], dst_ref=l_refs[t].at[p],
                                                  send_sem=send_sems.at[N_CHIPS * t + p],
                                                  recv_sem=recv_sems.at[N_CHIPS * t + p],
                                                  device_id=(x, y, 1 - c), device_id_type=MESH)
                cp.wait_send()
                cp.wait_recv()

    res = pl.pallas_call(
        body, name="swap_wait",
        in_specs=[HBM] * (2 * n) + [SEM, SEM, ANY],
        out_specs=[HBM] * (2 * n),
        out_shape=[pltpu.HBM(a.shape, a.dtype) for a in grads + lands],
        input_output_aliases={t: t for t in range(2 * n)},
        compiler_params=pltpu.CompilerParams(has_side_effects=EFFECT),
    )(*grads, *lands, send_sems, recv_sems, after)
    return list(res[:n]), list(res[n:])


def _row_tile(r):
    return max(t for t in range(16, 513, 16) if r % t == 0)


def add_own_half(g, other, c_arr):
    _, _, r, cols = g.shape
    tr = _row_tile(r)

    def body(c_ref, a_ref, b_ref, o_ref):
        o_ref[...] = (a_ref[...] + b_ref[...]).astype(BF16)

    return pl.pallas_call(
        body, name="add_own_half",
        grid_spec=pltpu.PrefetchScalarGridSpec(
            num_scalar_prefetch=1, grid=(N_CHIPS, r // tr),
            in_specs=[pl.BlockSpec((None, None, tr, cols), lambda p, i, c_ref: (p, c_ref[0], i, 0)),
                      pl.BlockSpec((None, tr, cols), lambda p, i, c_ref: (p, i, 0))],
            out_specs=pl.BlockSpec((None, tr, cols), lambda p, i, c_ref: (p, i, 0))),
        out_shape=jax.ShapeDtypeStruct((N_CHIPS, r, cols), BF16),
        compiler_params=_cparams("parallel", "parallel"),
    )(c_arr, g, other)


def scatter_start(partials):
    n = len(partials)

    def body(*refs):
        s_refs, l_refs = refs[:n], refs[n:2 * n]
        send_sems, recv_sems = refs[2 * n], refs[2 * n + 1]
        token = refs[-1]
        x, y, c = _place()
        for t in range(n):
            for k, (qx, qy) in enumerate(_other_chips(x, y)):
                pltpu.make_async_remote_copy(src_ref=s_refs[t].at[2 * qx + qy], dst_ref=l_refs[t].at[k],
                                             send_sem=send_sems.at[3 * t + k], recv_sem=recv_sems.at[3 * t + k],
                                             device_id=(qx, qy, c), device_id_type=MESH).start()
        token[...] = jnp.zeros_like(token)

    lands = [lax.empty((3,) + s.shape[1:], s.dtype) for s in partials]
    res = pl.pallas_call(
        body, name="scatter_start",
        in_specs=[HBM] * (2 * n),
        out_specs=[SEM, SEM] + [HBM] * (2 * n) + [pl.BlockSpec(memory_space=pltpu.VMEM)],
        out_shape=[pltpu.SemaphoreType.DMA((3 * n,)), pltpu.SemaphoreType.DMA((3 * n,))]
        + [pltpu.HBM(a.shape, a.dtype) for a in partials + lands] + [jax.ShapeDtypeStruct((8, LANES), F32)],
        input_output_aliases={t: 2 + t for t in range(2 * n)},
        compiler_params=pltpu.CompilerParams(has_side_effects=EFFECT),
    )(*[_hbm(a) for a in partials + lands])
    return res[0], res[1], list(res[2:2 + n]), list(res[2 + n:2 + 2 * n]), res[-1]


def scatter_wait(partials, lands, send_sems, recv_sems, after):
    n = len(partials)

    def body(*refs):
        s_refs, l_refs = refs[:n], refs[n:2 * n]
        send_sems, recv_sems = refs[2 * n], refs[2 * n + 1]
        x, y, c = _place()
        for t in range(n):
            for k, (qx, qy) in enumerate(_other_chips(x, y)):
                cp = pltpu.make_async_remote_copy(src_ref=s_refs[t].at[2 * qx + qy], dst_ref=l_refs[t].at[k],
                                                  send_sem=send_sems.at[3 * t + k], recv_sem=recv_sems.at[3 * t + k],
                                                  device_id=(qx, qy, c), device_id_type=MESH)
                cp.wait_send()
                cp.wait_recv()

    res = pl.pallas_call(
        body, name="scatter_wait",
        in_specs=[HBM] * (2 * n) + [SEM, SEM, ANY],
        out_specs=[HBM] * (2 * n),
        out_shape=[pltpu.HBM(a.shape, a.dtype) for a in partials + lands],
        input_output_aliases={t: t for t in range(2 * n)},
        compiler_params=pltpu.CompilerParams(has_side_effects=EFFECT),
    )(*partials, *lands, send_sems, recv_sems, after)
    return list(res[:n]), list(res[n:])


def sum_chips(own, parts, where):
    _, r, cols = own.shape
    tr = _row_tile(r)

    def body(w_ref, a_ref, p_ref, o_ref):
        acc = a_ref[...].astype(F32)
        for k in range(3):
            acc = acc + p_ref[k].astype(F32)
        o_ref[...] = acc

    return pl.pallas_call(
        body, name="sum_chips",
        grid_spec=pltpu.PrefetchScalarGridSpec(
            num_scalar_prefetch=1, grid=(r // tr,),
            in_specs=[pl.BlockSpec((None, tr, cols), lambda i, w_ref: (w_ref[0], i, 0)),
                      pl.BlockSpec((3, tr, cols), lambda i, w_ref: (0, i, 0))],
            out_specs=pl.BlockSpec((None, tr, cols), lambda i, w_ref: (w_ref[1], i, 0))),
        out_shape=jax.ShapeDtypeStruct((DEPTH, r, cols), F32),
        compiler_params=_cparams("parallel"),
    )(where, own, parts)


def sibling_share_layer(bufs):
    n = len(bufs)

    def body(*refs):
        o_refs = refs[n:2 * n]
        send_sems, recv_sems = refs[2 * n:]
        x, y, c = _place()
        cps = []
        for t in range(n):
            cp = pltpu.make_async_remote_copy(src_ref=o_refs[t].at[c], dst_ref=o_refs[t].at[c], send_sem=send_sems.at[t],
                                              recv_sem=recv_sems.at[t], device_id=(x, y, 1 - c), device_id_type=MESH)
            cp.start()
            cps.append(cp)
        for t in range(n):
            slot = o_refs[t].at[1 - c]
            pltpu.make_async_remote_copy(src_ref=slot, dst_ref=slot, send_sem=send_sems.at[t], recv_sem=recv_sems.at[t],
                                         device_id=(x, y, 1 - c), device_id_type=MESH).wait_recv()
        for cp in cps:
            cp.wait_send()

    return pl.pallas_call(
        body, name="sibling_share_layer",
        in_specs=[ANY] * n, out_specs=[ANY] * n,
        out_shape=[jax.ShapeDtypeStruct(b.shape, b.dtype) for b in bufs],
        input_output_aliases={t: t for t in range(n)},
        scratch_shapes=[pltpu.SemaphoreType.DMA((n,)), pltpu.SemaphoreType.DMA((n,))],
    )(*bufs)


SP_META = 2 * (N_META * D_MODEL // LANES)
SP_NORM = DEPTH * D_MODEL // LANES
SP_RB = DEPTH * N_BUCKETS
SP_SINK = DEPTH * ATT_HEADS
SP_CONV = DEPTH * 3 * BRANCH_WIDTH // LANES
SP_LOSS = 8
SP_ROWS = SP_META + 2 * SP_NORM + SP_RB + SP_SINK + SP_CONV + SP_LOSS


def sum_small(slots):
    half = SP_META // 2
    rb0 = SP_META + 2 * SP_NORM
    rest_rows = SP_ROWS - SP_META

    def body(s_ref, meta_ref, rest_ref):
        acc = s_ref[0]
        for d in range(1, 8):
            acc = acc + s_ref[d]
        meta_ref[...] = acc[0:half] + acc[half:SP_META]
        rest_ref[...] = acc[SP_META:]
        rest_ref[rb0 - SP_META:rb0 - SP_META + N_BUCKETS, :] = (
            acc[rb0:rb0 + N_BUCKETS] + acc[rb0 + N_BUCKETS:rb0 + 2 * N_BUCKETS])

    vm = pl.BlockSpec(memory_space=pltpu.VMEM)
    return pl.pallas_call(
        body, name="sum_small",
        in_specs=[vm], out_specs=[vm, vm],
        out_shape=[jax.ShapeDtypeStruct((half, LANES), F32), jax.ShapeDtypeStruct((rest_rows, LANES), F32)],
    )(slots)


def local_step(x, loss_target, meta_full, rel_bias, norm_pre, conv_w_full, attn_sinks, norm_post, weights_of, mid_fwd,
               grads_done, bwd_done):
    nb, seq, _ = x.shape
    nc = seq // BLOCK + 1
    lp = nc * BLOCK
    rows = nb * lp
    pad = jnp.zeros((nb, PAD_FRONT, D_MODEL), F32)
    meta = jnp.broadcast_to(meta_full[None], (nb, N_META, D_MODEL))
    h0 = jnp.concatenate([pad, meta, x], axis=1).reshape(rows, D_MODEL)
    cosf, sinf = _rot_tables(lp)
    bkt = jnp.asarray(_bucket_table())

    acts = []
    h = h0
    for l in range(DEPTH):
        (w_in, w_br, w_out), zero = weights_of(l, h)
        hb, p_abc = norm_matmul(h, norm_pre[l][None] + zero, w_in, 0, N_ABC_TILES)
        p_m = matmul_cols(hb, w_in, N_ABC_TILES, N_M_TILES)
        br, states = mixers_fwd(p_abc, cosf, sinf, bkt, rel_bias, attn_sinks[l][None], conv_w_full[l], nb, nc)
        h_next = merge_fwd(h, br, p_m, w_br, w_out, norm_post[l][None] + mid_fwd(l, br))
        acts.append((h, hb, p_abc, p_m, br, states, w_in, w_br, w_out))
        h = h_next

    loss_part, d_h = loss_head(h, loss_target.reshape(nb * seq, D_MODEL), nb, nc)

    small = [None] * DEPTH
    zero_m = jnp.zeros((1, 1), F32)
    for l in reversed(range(DEPTH)):
        h_in, hb, p_abc, p_m, br, states, w_in, w_br, w_out = acts[l]
        d_br, d_m, d_gpost, g_wbr, g_wout = merge_bwd(d_h, br, p_m, w_br, w_out, norm_post[l][None] + zero_m)
        d_abc, d_rb, d_sk, d_cw = mixers_bwd(p_abc, d_br, states, cosf, sinf, bkt, rel_bias,
                                             attn_sinks[l][None], conv_w_full[l], nb, nc)
        g_win = proj_wgrad(hb, d_abc, d_m)
        zero = grads_done(l, [g_win, g_wbr, g_wout])
        d_h, d_gpre = proj_dgrad(d_abc, d_m, w_in, h_in, norm_pre[l][None] + zero, d_h)
        zero_m = bwd_done(l, d_h)
        small[l] = (d_gpre[0], d_gpost[0], d_rb, d_sk, d_cw[0:3])

    d_h3 = d_h.reshape(nb, lp, D_MODEL)
    d_x = d_h3[:, BLOCK:]
    d_meta = d_h3[:, PAD_FRONT:BLOCK]
    sp = jnp.concatenate([
        d_meta.reshape(-1, LANES),
        jnp.stack([small[l][0] for l in range(DEPTH)]).reshape(-1, LANES),
        jnp.stack([small[l][1] for l in range(DEPTH)]).reshape(-1, LANES),
        jnp.concatenate([small[l][2] for l in range(DEPTH)], axis=0),
        jnp.concatenate([small[l][3] for l in range(DEPTH)], axis=0),
        jnp.stack([small[l][4] for l in range(DEPTH)]).reshape(-1, LANES),
        loss_part], axis=0)
    return d_x, sp


def kernel(x, meta_tokens, rel_bias, norm_pre, w_in, conv_w, attn_sinks, w_branch, w_out, norm_post, loss_target, m_meta_tokens, m_rel_bias, m_norm_pre, m_w_in, m_conv_w, m_attn_sinks, m_w_branch, m_w_out, m_norm_post, v_meta_tokens, v_rel_bias, v_norm_pre, v_w_in, v_conv_w, v_attn_sinks, v_w_branch, v_w_out, v_norm_post):
    assert x.shape[0] == 2 and SP_META == 2 * N_META * D_MODEL // LANES
    px, py, pc = _place()
    chip = 2 * px + py

    c_arr = jnp.reshape(pc, (1,)).astype(jnp.int32)
    where = jnp.stack([chip, pc]).astype(jnp.int32)
    tr_ = lambda a: jnp.swapaxes(a, 1, 2)
    w3 = [tr_(w_in), w_branch.reshape(DEPTH, N_BRANCH * BRANCH_WIDTH, SHARD_D), w_out]
    halves = lambda a: a.reshape(2, a.shape[0] // 2, a.shape[1])

    def as_weights(bufs):
        a_in, a_br, a_out = bufs
        return (a_in.reshape(PROJ_WIDTH, D_MODEL), a_br.reshape(N_CHIPS, N_BRANCH, BRANCH_WIDTH, SHARD_D),
                a_out.reshape(D_MODEL, D_MODEL))

    side = jnp.concatenate([meta_tokens.reshape(-1), conv_w.reshape(-1)]).reshape(-1, LANES)
    side = jnp.concatenate([side, jnp.zeros((40 - side.shape[0], LANES), F32)], axis=0)
    side_all = exchange_small(side, side)
    side_chips = side_all[0::2]
    n_meta_rows = N_META * SHARD_D // LANES
    meta_full = jnp.moveaxis(side_chips[:, :n_meta_rows].reshape(N_CHIPS, N_META, SHARD_D), 0, 1).reshape(N_META, D_MODEL)
    conv_full = jnp.moveaxis(side_chips[:, n_meta_rows:n_meta_rows + 6].reshape(N_CHIPS, DEPTH, 3, LANES), 0, 2).reshape(DEPTH, 3, BRANCH_WIDTH)

    slots = [[_own_slot(halves(w[l].astype(BF16)), chip) for w in w3] for l in range(DEPTH)]
    send0, recv0, flying0, started0 = gather_start(slots[0], side_all)
    meta_full = meta_full + started0[0:1, 0:1]
    inbound = {}

    def weights_of(l, h):
        if l == 0:
            gathered0 = gather_forward(gather_wait(flying0, send0, recv0, h))
            inbound[1] = gather_start(slots[1], gathered0[0])
            return as_weights(gathered0), inbound[1][3][0:1, 0:1]
        send, recv, thru = inbound[1]
        return as_weights(forward_wait(thru, send, recv, h)), jnp.zeros((1, 1), F32)

    def mid_fwd(l, br):
        if l == 0:
            send, recv, flying1, _ = inbound[1]
            send, recv, thru, started = forward_start(gather_wait(flying1, send, recv, br))
            inbound[1] = (send, recv, thru)
            return started[0:1, 0:1]
        return jnp.zeros((1, 1), F32)

    reduced = [None] * DEPTH
    flying = {}

    def finish_reduce(l, after):
        partials, parts = scatter_wait(*flying[l], after)
        reduced[l] = sibling_share_layer([sum_chips(a, p, where) for a, p in zip(partials, parts)])

    def start_scatter(l, full, others):
        send, recv, thru, lands, started = scatter_start([add_own_half(g, o, c_arr) for g, o in zip(full, others)])
        flying[l] = (thru, lands, send, recv)
        return started[0:1, 0:1]

    m3 = [tr_(m_w_in), m_w_branch.reshape(w3[1].shape), m_w_out]
    v3 = [tr_(v_w_in), v_w_branch.reshape(w3[1].shape), v_w_out]
    big = [None] * 3

    def adamw_of(l, after):
        for t in range(3):
            big[t] = adamw_layer(w3[t], reduced[l][t].reshape(w3[t].shape[1:]), m3[t], v3[t], l, big[t], after)

    def grads_done(l, grads):
        full = [g.reshape(N_CHIPS, 2, g.size // (2 * N_CHIPS * g.shape[-1]), g.shape[-1]) for g in grads]
        if l == 0:
            finish_reduce(1, grads[0])
        send, recv, thru, lands, started = swap_start(full)
        if l == 1:
            flying["swap"] = (thru, lands, send, recv)
            return started[0:1, 0:1]
        adamw_of(1, started)
        return start_scatter(0, *swap_wait(thru, lands, send, recv, big[0][1]))

    def bwd_done(l, d_h):
        if l == 1:
            return start_scatter(1, *swap_wait(*flying["swap"], d_h))
        return jnp.zeros((1, 1), F32)

    d_x, sp = local_step(x, loss_target, meta_full, rel_bias, norm_pre, conv_full, attn_sinks, norm_post,
                         weights_of, mid_fwd, grads_done, bwd_done)
    finish_reduce(0, sp)

    s_send, s_recv, s_buf, s_started = small_start(sp, 4 * px + 2 * py + pc, reduced[0][0])

    adamw_of(0, s_started)
    g_in, *u_in = [tr_(a) for a in big[0]]
    g_br, *u_br = [a.reshape(w_branch.shape) for a in big[1]]
    g_out, *u_out = big[2]

    meta_rows, rest = sum_small(small_wait(s_buf, s_send, s_recv, big[0][1]))
    o = 0
    g_meta_full = meta_rows.reshape(N_META, D_MODEL)
    g_norm_pre = rest[o:o + SP_NORM].reshape(DEPTH, D_MODEL); o += SP_NORM
    g_norm_post = rest[o:o + SP_NORM].reshape(DEPTH, D_MODEL); o += SP_NORM
    g_rel_bias = rest[o:o + N_BUCKETS, :ATT_HEADS]; o += SP_RB
    g_sinks = rest[o:o + SP_SINK, 0].reshape(DEPTH, ATT_HEADS); o += SP_SINK
    g_conv_full = rest[o:o + SP_CONV].reshape(DEPTH, 3, BRANCH_WIDTH); o += SP_CONV
    loss = rest[o, 0]
    g_meta = lax.dynamic_slice_in_dim(g_meta_full, chip * SHARD_D, SHARD_D, axis=1)
    g_conv = lax.dynamic_slice_in_dim(g_conv_full, chip * LANES, LANES, axis=2)

    to2 = lambda a: a.reshape(-1, a.shape[-1])
    smalls = [(meta_tokens, g_meta, m_meta_tokens, v_meta_tokens),
              (rel_bias, g_rel_bias, m_rel_bias, v_rel_bias),
              (norm_pre, g_norm_pre, m_norm_pre, v_norm_pre),
              (to2(conv_w), to2(g_conv), to2(m_conv_w), to2(v_conv_w)),
              (attn_sinks, g_sinks, m_attn_sinks, v_attn_sinks),
              (norm_post, g_norm_post, m_norm_post, v_norm_post)]
    u_meta, u_rb, u_npre, u_conv, u_sink, u_npost = adamw_small(smalls)
    u_conv = tuple(a.reshape(conv_w.shape) for a in u_conv)

    grads = [g_meta, g_rel_bias, g_norm_pre, g_in, g_conv, g_sinks, g_br, g_out, g_norm_post]
    upd = [u_meta, u_rb, u_npre, u_in, u_conv, u_sink, u_br, u_out, u_npost]
    return (loss, d_x, *grads, *[u[0] for u in upd], *[u[1] for u in upd], *[u[2] for u in upd])
```

```python
import functools
import math

import numpy as np
import jax
import jax.numpy as jnp
from jax import lax
from jax.experimental import pallas as pl
from jax.experimental.pallas import tpu as pltpu

F32 = jnp.float32
BF16 = jnp.bfloat16
MESH = pl.DeviceIdType.MESH

D_MODEL = 1024
DEPTH = 2
N_META = 16
BLOCK = 128
PAD_FRONT = BLOCK - N_META
ATT_HEADS = 8
ATT_HEAD_DIM = 64
N_BUCKETS = 32
MAX_EXACT = 16
MAX_DISTANCE = 128
RET_HEADS = 4
ROT_BASE = 10000.0
N_BRANCH = 3
BRANCH_WIDTH = 512
PROJ_WIDTH = 8448
ABC_WIDTH = 5376
MERGE_WIDTH = N_BRANCH * D_MODEL
RMS_EPS = 1e-6
GN_EPS = 1e-6
NEG_INF = -1e30
ATT_SCALE = ATT_HEAD_DIM ** -0.5
RET_SCALE = BLOCK ** -0.5
LOG_GAMMA = tuple(math.log1p(-(2.0 ** (-5.0 - h))) for h in range(RET_HEADS))

C_AQ, C_AK, C_AV, C_AG = 0, 512, 640, 768
C_RQ, C_RK, C_RV, C_RG = 1280, 1792, 2304, 2816
C_CB, C_CC, C_CX, C_CG = 3328, 3840, 4352, 4864

ADAM_LR = 0.001
ADAM_B1 = 0.9
ADAM_B2 = 0.999
ADAM_EPS = 1e-08
ADAM_WD = 0.01
ADAM_STEP = 10

N_CHIPS = 4
SHARD_IN = PROJ_WIDTH // N_CHIPS
SHARD_D = D_MODEL // N_CHIPS
LANES = 128
PACK_IN = D_MODEL * SHARD_IN
PACK_BR = N_BRANCH * BRANCH_WIDTH * SHARD_D
PACK_OUT = SHARD_D * D_MODEL
PACK_ROWS = (PACK_IN + PACK_BR + PACK_OUT) // LANES

VMEM_LIMIT = 56 * 1024 * 1024
COL_TILE = 768
ROW_TILE = 1088
PROJ_ROW_TILE = 2176


def _cparams(*sem):
    return pltpu.CompilerParams(dimension_semantics=sem, vmem_limit_bytes=VMEM_LIMIT)


def _nt(a, b):
    return lax.dot_general(a, b, (((1,), (1,)), ((), ())), preferred_element_type=F32)


def _tn(a, b):
    return lax.dot_general(a, b, (((0,), (0,)), ((), ())), preferred_element_type=F32)


def _nn(a, b):
    return jnp.dot(a, b, preferred_element_type=F32)


def _sigmoid(x):
    return 0.5 * jnp.tanh(0.5 * x) + 0.5


def _silu(x):
    return x * _sigmoid(x)


def _dsilu(x):
    s = _sigmoid(x)
    return s * (1.0 + x * (1.0 - s))


def _bucket_table():
    r = np.arange(BLOCK)[:, None]
    c = np.arange(2 * BLOCK)[None, :]
    n = np.maximum(BLOCK + r - c, 0)
    nf = np.maximum(n, 1).astype(np.float32)
    large = MAX_EXACT + (np.log(nf / MAX_EXACT) / math.log(MAX_DISTANCE / MAX_EXACT)
                         * (N_BUCKETS - MAX_EXACT)).astype(np.int32)
    large = np.minimum(large, N_BUCKETS - 1)
    return np.where(n < MAX_EXACT, n, large).astype(np.int32)


def _rot_tables(lp):
    half = BLOCK // 2
    pos = (jnp.arange(lp) - PAD_FRONT).astype(F32)
    theta = 1.0 / (ROT_BASE ** jnp.linspace(0.0, 1.0, half, dtype=F32))
    ang = pos[:, None] * theta[None, :]
    cos, sin = jnp.cos(ang), jnp.sin(ang)
    return jnp.concatenate([cos, cos], axis=1), jnp.concatenate([-sin, sin], axis=1)


def norm_matmul(x2d, g, w, col0_blocks, n_col_blocks):
    t = x2d.shape[0]
    tm = PROJ_ROW_TILE if t % PROJ_ROW_TILE == 0 else BLOCK

    def body(x_ref, g_ref, w_ref, hb_ref, o_ref):
        @pl.when(pl.program_id(1) == 0)
        def _():
            x = x_ref[...]
            r = lax.rsqrt(jnp.mean(x * x, axis=-1, keepdims=True) + RMS_EPS)
            hb_ref[...] = (x * r * g_ref[...]).astype(BF16)

        o_ref[...] = _nt(hb_ref[...], w_ref[...]).astype(BF16)

    return pl.pallas_call(
        body, name="norm_matmul",
        grid=(t // tm, n_col_blocks),
        in_specs=[pl.BlockSpec((tm, D_MODEL), lambda i, j: (i, 0)),
                  pl.BlockSpec((1, D_MODEL), lambda i, j: (0, 0)),
                  pl.BlockSpec((COL_TILE, D_MODEL), lambda i, j: (j + col0_blocks, 0))],
        out_specs=[pl.BlockSpec((tm, D_MODEL), lambda i, j: (i, 0)),
                   pl.BlockSpec((tm, COL_TILE), lambda i, j: (i, j))],
        out_shape=[jax.ShapeDtypeStruct((t, D_MODEL), BF16),
                   jax.ShapeDtypeStruct((t, n_col_blocks * COL_TILE), BF16)],
        compiler_params=_cparams("parallel", "arbitrary"),
    )(x2d, g, w)


def matmul_cols(a, w, col0_blocks, n_col_blocks):
    t, k = a.shape
    tm = PROJ_ROW_TILE if t % PROJ_ROW_TILE == 0 else BLOCK

    def body(a_ref, w_ref, o_ref):
        o_ref[...] = _nt(a_ref[...], w_ref[...]).astype(BF16)

    return pl.pallas_call(
        body, name="matmul_cols",
        grid=(t // tm, n_col_blocks),
        in_specs=[pl.BlockSpec((tm, k), lambda i, j: (i, 0)),
                  pl.BlockSpec((COL_TILE, k), lambda i, j: (j + col0_blocks, 0))],
        out_specs=pl.BlockSpec((tm, COL_TILE), lambda i, j: (i, j)),
        out_shape=jax.ShapeDtypeStruct((t, n_col_blocks * COL_TILE), BF16),
        compiler_params=_cparams("parallel", "arbitrary"),
    )(a, w)


class _Widened:
    def __init__(self, ref):
        self.ref = ref

    def __getitem__(self, idx):
        return self.ref[idx].astype(F32)


def _build_bias(bkt_ref, rb_ref, bias_s):
    bkt = bkt_ref[...]
    for h in range(ATT_HEADS):
        acc = jnp.zeros((BLOCK, 2 * BLOCK), F32)
        for b in range(N_BUCKETS):
            acc = jnp.where(bkt == b, rb_ref[b, h], acc)
        bias_s[h] = acc


def _band_mask(n):
    r = lax.broadcasted_iota(jnp.int32, (BLOCK, 2 * BLOCK), 0)
    c = lax.broadcasted_iota(jnp.int32, (BLOCK, 2 * BLOCK), 1)
    key_pos = (n - 1) * BLOCK + c
    return (c > r) & (c <= r + BLOCK) & (key_pos >= PAD_FRONT)


def _split_heads(kv, kh):
    lane = lax.broadcasted_iota(jnp.int32, kv.shape, 1)
    if kh == 0:
        lo = jnp.where(lane < ATT_HEAD_DIM, kv, 0.0)
        hi = pltpu.roll(lo, ATT_HEAD_DIM, 1)
    else:
        hi = jnp.where(lane >= ATT_HEAD_DIM, kv, 0.0)
        lo = pltpu.roll(hi, ATT_HEAD_DIM, 1)
    return lo, hi


def _merge_heads(acc_lo, acc_hi, kh):
    lane = lax.broadcasted_iota(jnp.int32, acc_lo.shape, 1)
    if kh == 0:
        return jnp.where(lane < ATT_HEAD_DIM, acc_lo + pltpu.roll(acc_hi, ATT_HEAD_DIM, 1), 0.0)
    return jnp.where(lane >= ATT_HEAD_DIM, acc_hi + pltpu.roll(acc_lo, ATT_HEAD_DIM, 1), 0.0)


def _softmax_sink(q2b, kxb, bias_h, mask, sink_h):
    return _softmax_of(_nt(q2b, kxb), bias_h, mask, sink_h)


def _softmax_of(qk, bias_h, mask, sink_h):
    s = qk * ATT_SCALE + bias_h
    s = jnp.where(mask, s, NEG_INF)
    m = jnp.maximum(jnp.max(s, axis=-1, keepdims=True), sink_h)
    p = jnp.exp(s - m)
    es = jnp.exp(sink_h - m)
    inv = 1.0 / (jnp.sum(p, axis=-1, keepdims=True) + es)
    return p * inv, es * inv


def _rot(t, cosf, sinf):
    return t * cosf + pltpu.roll(t, BLOCK // 2, 1) * sinf


def _rot_t(d, cosf, sinf):
    return d * cosf + pltpu.roll(d * sinf, BLOCK // 2, 1)


def _decay_tables(h):
    lg = LOG_GAMMA[h]
    i = lax.broadcasted_iota(jnp.int32, (BLOCK, BLOCK), 0)
    j = lax.broadcasted_iota(jnp.int32, (BLOCK, BLOCK), 1)
    diff = (i - j).astype(F32)
    dm = jnp.where(diff >= 0, jnp.exp(diff * lg), 0.0)
    row = lax.broadcasted_iota(jnp.int32, (BLOCK, 1), 0).astype(F32)
    zeta = jnp.exp((BLOCK - 1 - row) * lg)
    xi = jnp.exp((row + 1.0) * lg)
    return dm, zeta, xi, math.exp(BLOCK * lg)


def _valid_col(n):
    row = lax.broadcasted_iota(jnp.int32, (BLOCK, 1), 0)
    return ((n * BLOCK + row) >= PAD_FRONT).astype(F32)


def _shift_down(cur, prev, k):
    row = lax.broadcasted_iota(jnp.int32, cur.shape, 0)
    return jnp.where(row >= k, pltpu.roll(cur, k, 0), pltpu.roll(prev, k, 0))


def _shift_up(cur, nxt, k):
    row = lax.broadcasted_iota(jnp.int32, cur.shape, 0)
    return jnp.where(row < BLOCK - k, pltpu.roll(cur, BLOCK - k, 0), pltpu.roll(nxt, BLOCK - k, 0))


def mixers_fwd(proj, cosf, sinf, bkt, rel_bias, sinks, conv_w, nb, nc):
    def body(p_ref, cos_ref, sin_ref, bkt_ref, rb_ref, sk_ref, cw_ref, br_ref, st_ref,
             bias_s, kv_s, state_s, u_s):
        p_ref = _Widened(p_ref)
        n = pl.program_id(0)

        @pl.when(n == 0)
        def _():
            _build_bias(bkt_ref, rb_ref, bias_s)
            kv_s[:, 0:BLOCK, :] = jnp.zeros((nb, BLOCK, 2 * BLOCK), F32)
            state_s[...] = jnp.zeros_like(state_s)
            u_s[...] = jnp.zeros_like(u_s)

        valid = _valid_col(n)
        mask = _band_mask(n)
        ex = range(nb)

        for b in ex:
            kv_s[b, BLOCK:2 * BLOCK, :] = p_ref[b, :, C_AK:C_AK + 2 * BLOCK]
        for kh in range(2):
            ks = [[t.astype(BF16) for t in _split_heads(kv_s[b, :, 0:BLOCK], kh)] for b in ex]
            vs = [[t.astype(BF16) for t in _split_heads(kv_s[b, :, BLOCK:2 * BLOCK], kh)] for b in ex]
            pairs = [(b, 2 * kh + jj) for jj in range(2) for b in ex]
            subs = [(b, j, x) for (b, j) in pairs for x in range(2)]
            qb_ = {(b, j): p_ref[b, :, C_AQ + BLOCK * j:C_AQ + BLOCK * (j + 1)].astype(BF16) for (b, j) in pairs}
            qk_ = {(b, j, x): _nt(qb_[(b, j)], ks[b][x]) for (b, j, x) in subs}
            pb_ = {}
            for u in subs:
                h = 2 * u[1] + u[2]
                pb_[u] = _softmax_of(qk_[u], bias_s[h], mask, sk_ref[0, h])[0].astype(BF16)
            o_ = {u: _nn(pb_[u], vs[u[0]][u[2]]) for u in subs}
            for (b, j) in pairs:
                gate = p_ref[b, :, C_AG + BLOCK * j:C_AG + BLOCK * (j + 1)]
                br_ref[b, :, BLOCK * j:BLOCK * (j + 1)] = ((o_[(b, j, 0)] + o_[(b, j, 1)]) * _silu(gate)).astype(BF16)
        for b in ex:
            kv_s[b, 0:BLOCK, :] = kv_s[b, BLOCK:2 * BLOCK, :]

        cosv = cos_ref[...]
        sinv = sin_ref[...]
        tabs = [_decay_tables(h) for h in range(RET_HEADS)]
        units = [(b, h) for h in range(RET_HEADS) for b in ex]
        sl = lambda c0, h: slice(c0 + BLOCK * h, c0 + BLOCK * (h + 1))
        q_, k_, v_, sp_ = {}, {}, {}, {}
        for u in units:
            b, h = u
            q_[u] = _rot(p_ref[b, :, sl(C_RQ, h)], cosv, sinv).astype(BF16)
            k_[u] = (_rot(p_ref[b, :, sl(C_RK, h)], cosv, sinv) * RET_SCALE * valid).astype(BF16)
            v_[u] = p_ref[b, :, sl(C_RV, h)]
            sp_[u] = state_s[b, h]
            st_ref[b, 0, h] = sp_[u]
        qk_ = {u: _nt(q_[u], k_[u]) for u in units}
        qs_ = {u: _nn(q_[u], sp_[u].astype(BF16)) for u in units}
        kv_ = {u: _tn(k_[u], (v_[u] * tabs[u[1]][1]).astype(BF16)) for u in units}
        a_ = {u: (qk_[u] * tabs[u[1]][0]).astype(BF16) for u in units}
        av_ = {u: _nn(a_[u], v_[u].astype(BF16)) for u in units}
        for u in units:
            b, h = u
            o = av_[u] + tabs[h][2] * qs_[u]
            mu = jnp.mean(o, axis=-1, keepdims=True)
            var = jnp.mean(jnp.square(o - mu), axis=-1, keepdims=True)
            oh = (o - mu) * lax.rsqrt(var + GN_EPS)
            gate = p_ref[b, :, sl(C_RG, h)]
            br_ref[b, :, BRANCH_WIDTH + BLOCK * h:BRANCH_WIDTH + BLOCK * (h + 1)] = (oh * _silu(gate)).astype(BF16)
            state_s[b, h] = tabs[h][3] * sp_[u] + kv_[u]

        for b in ex:
            u = p_ref[b, :, C_CC:C_CC + BRANCH_WIDTH] * p_ref[b, :, C_CX:C_CX + BRANCH_WIDTH] * valid
            u_prev = u_s[b]
            y = (cw_ref[0:1, :] * _shift_down(u, u_prev, 2) + cw_ref[1:2, :] * _shift_down(u, u_prev, 1)
                 + cw_ref[2:3, :] * u)
            yc = p_ref[b, :, C_CB:C_CB + BRANCH_WIDTH] * y * _silu(p_ref[b, :, C_CG:C_CG + BRANCH_WIDTH])
            br_ref[b, :, 2 * BRANCH_WIDTH:3 * BRANCH_WIDTH] = yc.astype(BF16)
            u_s[b] = u

    lp = nc * BLOCK
    smem = pl.BlockSpec(memory_space=pltpu.SMEM)
    br, states = pl.pallas_call(
        body, name="mixers_fwd",
        grid=(nc,),
        in_specs=[pl.BlockSpec((nb, BLOCK, ABC_WIDTH), lambda n: (0, n, 0)),
                  pl.BlockSpec((BLOCK, BLOCK), lambda n: (n, 0)),
                  pl.BlockSpec((BLOCK, BLOCK), lambda n: (n, 0)),
                  pl.BlockSpec((BLOCK, 2 * BLOCK), lambda n: (0, 0)),
                  smem, smem,
                  pl.BlockSpec((3, BRANCH_WIDTH), lambda n: (0, 0))],
        out_specs=[pl.BlockSpec((nb, BLOCK, N_BRANCH * BRANCH_WIDTH), lambda n: (0, n, 0)),
                   pl.BlockSpec((nb, 1, RET_HEADS, BLOCK, BLOCK), lambda n: (0, n, 0, 0, 0))],
        out_shape=[jax.ShapeDtypeStruct((nb, lp, N_BRANCH * BRANCH_WIDTH), BF16),
                   jax.ShapeDtypeStruct((nb, nc, RET_HEADS, BLOCK, BLOCK), F32)],
        scratch_shapes=[pltpu.VMEM((ATT_HEADS, BLOCK, 2 * BLOCK), F32),
                        pltpu.VMEM((nb, 2 * BLOCK, 2 * BLOCK), F32),
                        pltpu.VMEM((nb, RET_HEADS, BLOCK, BLOCK), F32),
                        pltpu.VMEM((nb, BLOCK, BRANCH_WIDTH), F32)],
        compiler_params=_cparams("arbitrary"),
    )(proj.reshape(nb, lp, ABC_WIDTH), cosf, sinf, bkt, rel_bias, sinks, conv_w)
    return br.reshape(nb * lp, N_BRANCH * BRANCH_WIDTH), states


def mixers_bwd(proj, d_br, states, cosf, sinf, bkt, rel_bias, sinks, conv_w, nb, nc):
    def body(p_ref, kvp_ref, cp_ref, dbr_ref, st_ref, cos_ref, sin_ref, bkt_ref, rb_ref, sk_ref, cw_ref,
             dp_ref, drb_ref, dsk_ref, dcw_ref,
             bias_s, dbias_s, dkv_s, g_s, dy_s):
        p_ref, kvp_ref, cp_ref, dbr_ref = [_Widened(r) for r in (p_ref, kvp_ref, cp_ref, dbr_ref)]
        step = pl.program_id(0)
        n = nc - 1 - step
        ex = range(nb)

        @pl.when(step == 0)
        def _():
            _build_bias(bkt_ref, rb_ref, bias_s)
            dbias_s[...] = jnp.zeros_like(dbias_s)
            dsk_ref[...] = jnp.zeros_like(dsk_ref)
            dcw_ref[...] = jnp.zeros_like(dcw_ref)
            drb_ref[...] = jnp.zeros_like(drb_ref)
            dkv_s[...] = jnp.zeros_like(dkv_s)
            g_s[...] = jnp.zeros_like(g_s)
            dy_s[...] = jnp.zeros_like(dy_s)

        valid = _valid_col(n)
        mask = _band_mask(n)
        has_prev = (n > 0).astype(F32)

        k_all, v_all = [], []
        for b in ex:
            kv_prev = kvp_ref[b] * has_prev
            kv_cur = p_ref[b, :, C_AK:C_AK + 2 * BLOCK]
            k_all.append(jnp.concatenate([kv_prev[:, 0:BLOCK], kv_cur[:, 0:BLOCK]], axis=0))
            v_all.append(jnp.concatenate([kv_prev[:, BLOCK:], kv_cur[:, BLOCK:]], axis=0))
        zero2 = jnp.zeros((2 * BLOCK, BLOCK), F32)
        dk_tot = [zero2 for _ in ex]
        dv_tot = [zero2 for _ in ex]
        for kh in range(2):
            ks = [[t.astype(BF16) for t in _split_heads(k_all[b], kh)] for b in ex]
            vs = [[t.astype(BF16) for t in _split_heads(v_all[b], kh)] for b in ex]
            pairs = [(b, 2 * kh + jj) for jj in range(2) for b in ex]
            subs = [(b, j, x) for (b, j) in pairs for x in range(2)]
            qb_, gate_, dya_, do2_ = {}, {}, {}, {}
            for w in pairs:
                b, j = w
                qb_[w] = p_ref[b, :, C_AQ + BLOCK * j:C_AQ + BLOCK * (j + 1)].astype(BF16)
                gate_[w] = p_ref[b, :, C_AG + BLOCK * j:C_AG + BLOCK * (j + 1)]
                dya_[w] = dbr_ref[b, :, BLOCK * j:BLOCK * (j + 1)]
                do2_[w] = (dya_[w] * _silu(gate_[w])).astype(BF16)
            qk_ = {(b, j, x): _nt(qb_[(b, j)], ks[b][x]) for (b, j, x) in subs}
            dpm_ = {(b, j, x): _nt(do2_[(b, j)], vs[b][x]) for (b, j, x) in subs}
            pb_, dsb_ = {}, {}
            for u in subs:
                b, j, x = u
                h = 2 * j + x
                p, p_sink = _softmax_of(qk_[u], bias_s[h], mask, sk_ref[0, h])
                pb_[u] = p.astype(BF16)
                delta = jnp.sum(p * dpm_[u], axis=-1, keepdims=True)
                ds = p * (dpm_[u] - delta)
                dbias_s[h] += ds
                dsk_ref[h:h + 1, :] += jnp.broadcast_to(
                    jnp.sum(-p_sink * delta, axis=0, keepdims=True), (1, BLOCK))
                dsb_[u] = ds.astype(BF16)
            o_ = {u: _nn(pb_[u], vs[u[0]][u[2]]) for u in subs}
            dq_ = {u: _nn(dsb_[u], ks[u[0]][u[2]]) for u in subs}
            dkm_ = {u: _tn(dsb_[u], qb_[(u[0], u[1])]) for u in subs}
            dvm_ = {u: _tn(pb_[u], do2_[(u[0], u[1])]) for u in subs}
            for w in pairs:
                b, j = w
                o2 = o_[(b, j, 0)] + o_[(b, j, 1)]
                dq2 = (dq_[(b, j, 0)] + dq_[(b, j, 1)]) * ATT_SCALE
                dp_ref[b, :, C_AQ + BLOCK * j:C_AQ + BLOCK * (j + 1)] = dq2.astype(BF16)
                dp_ref[b, :, C_AG + BLOCK * j:C_AG + BLOCK * (j + 1)] = (
                    dya_[w] * o2 * _dsilu(gate_[w])).astype(BF16)
            for b in ex:
                j0, j1 = 2 * kh, 2 * kh + 1
                dk_lo = (dkm_[(b, j0, 0)] + dkm_[(b, j1, 0)]) * ATT_SCALE
                dk_hi = (dkm_[(b, j0, 1)] + dkm_[(b, j1, 1)]) * ATT_SCALE
                dk_tot[b] = dk_tot[b] + _merge_heads(dk_lo, dk_hi, kh)
                dv_tot[b] = dv_tot[b] + _merge_heads(dvm_[(b, j0, 0)] + dvm_[(b, j1, 0)],
                                                     dvm_[(b, j0, 1)] + dvm_[(b, j1, 1)], kh)
        for b in ex:
            dp_ref[b, :, C_AK:C_AK + BLOCK] = (dk_tot[b][BLOCK:, :] + dkv_s[b, :, 0:BLOCK]).astype(BF16)
            dp_ref[b, :, C_AV:C_AV + BLOCK] = (dv_tot[b][BLOCK:, :] + dkv_s[b, :, BLOCK:]).astype(BF16)
            dkv_s[b, :, 0:BLOCK] = dk_tot[b][0:BLOCK, :]
            dkv_s[b, :, BLOCK:] = dv_tot[b][0:BLOCK, :]

        cosv = cos_ref[...]
        sinv = sin_ref[...]
        tabs = [_decay_tables(h) for h in range(RET_HEADS)]
        units = [(b, h) for h in range(RET_HEADS) for b in ex]
        sl = lambda c0, h: slice(c0 + BLOCK * h, c0 + BLOCK * (h + 1))
        q_, k_, v_, vb_, sp_ = {}, {}, {}, {}, {}
        for u in units:
            b, h = u
            q_[u] = _rot(p_ref[b, :, sl(C_RQ, h)], cosv, sinv).astype(BF16)
            k_[u] = (_rot(p_ref[b, :, sl(C_RK, h)], cosv, sinv) * RET_SCALE * valid).astype(BF16)
            v_[u] = p_ref[b, :, sl(C_RV, h)]
            vb_[u] = v_[u].astype(BF16)
            sp_[u] = st_ref[b, 0, h].astype(BF16)
        qk_ = {u: _nt(q_[u], k_[u]) for u in units}
        qs_ = {u: _nn(q_[u], sp_[u]) for u in units}
        a_ = {u: (qk_[u] * tabs[u[1]][0]).astype(BF16) for u in units}
        av_ = {u: _nn(a_[u], vb_[u]) for u in units}
        dob_, dxo_ = {}, {}
        for u in units:
            b, h = u
            xi = tabs[h][2]
            o = av_[u] + xi * qs_[u]
            mu = jnp.mean(o, axis=-1, keepdims=True)
            var = jnp.mean(jnp.square(o - mu), axis=-1, keepdims=True)
            rstd = lax.rsqrt(var + GN_EPS)
            oh = (o - mu) * rstd
            gate = p_ref[b, :, sl(C_RG, h)]
            d_yr = dbr_ref[b, :, BRANCH_WIDTH + BLOCK * h:BRANCH_WIDTH + BLOCK * (h + 1)]
            dp_ref[b, :, sl(C_RG, h)] = (d_yr * oh * _dsilu(gate)).astype(BF16)
            doh = d_yr * _silu(gate)
            do = rstd * (doh - jnp.mean(doh, axis=-1, keepdims=True)
                         - oh * jnp.mean(doh * oh, axis=-1, keepdims=True))
            dob_[u] = do.astype(BF16)
            dxo_[u] = (do * xi).astype(BF16)
        dov_ = {u: _nt(dob_[u], vb_[u]) for u in units}
        dv1_ = {u: _tn(a_[u], dob_[u]) for u in units}
        dq1_ = {u: _nt(dxo_[u], sp_[u]) for u in units}
        gq_ = {u: _tn(q_[u], dxo_[u]) for u in units}
        da_, gb_, zv_ = {}, {}, {}
        for u in units:
            b, h = u
            da_[u] = (dov_[u] * tabs[h][0]).astype(BF16)
            g_next = g_s[b, h]
            gb_[u] = g_next.astype(BF16)
            zv_[u] = (v_[u] * tabs[h][1]).astype(BF16)
            g_s[b, h] = tabs[h][3] * g_next + gq_[u]
        dq2_ = {u: _nn(da_[u], k_[u]) for u in units}
        dk1_ = {u: _tn(da_[u], q_[u]) for u in units}
        dk2_ = {u: _nt(zv_[u], gb_[u]) for u in units}
        dv2_ = {u: _nn(k_[u], gb_[u]) for u in units}
        for u in units:
            b, h = u
            dp_ref[b, :, sl(C_RQ, h)] = _rot_t(dq2_[u] + dq1_[u], cosv, sinv).astype(BF16)
            dp_ref[b, :, sl(C_RK, h)] = _rot_t((dk1_[u] + dk2_[u]) * (RET_SCALE * valid), cosv, sinv).astype(BF16)
            dp_ref[b, :, sl(C_RV, h)] = (dv1_[u] + tabs[h][1] * dv2_[u]).astype(BF16)

        w0, w1, w2 = cw_ref[0:1, :], cw_ref[1:2, :], cw_ref[2:3, :]
        for b in ex:
            cb = p_ref[b, :, C_CB:C_CB + BRANCH_WIDTH]
            cc = p_ref[b, :, C_CC:C_CC + BRANCH_WIDTH]
            cx = p_ref[b, :, C_CX:C_CX + BRANCH_WIDTH]
            cg = p_ref[b, :, C_CG:C_CG + BRANCH_WIDTH]
            u = cc * cx * valid
            u_prev = (cp_ref[b, :, 0:BRANCH_WIDTH] * cp_ref[b, :, BRANCH_WIDTH:2 * BRANCH_WIDTH]
                      * (_valid_col(n - 1) * has_prev))
            u1 = _shift_down(u, u_prev, 1)
            u2 = _shift_down(u, u_prev, 2)
            y = w0 * u2 + w1 * u1 + w2 * u
            d_yc = dbr_ref[b, :, 2 * BRANCH_WIDTH:3 * BRANCH_WIDTH]
            sg = _silu(cg)
            dp_ref[b, :, C_CB:C_CB + BRANCH_WIDTH] = (d_yc * y * sg).astype(BF16)
            dp_ref[b, :, C_CG:C_CG + BRANCH_WIDTH] = (d_yc * cb * y * _dsilu(cg)).astype(BF16)
            dy = d_yc * cb * sg
            dy_next = dy_s[b]
            du = (w2 * dy + w1 * _shift_up(dy, dy_next, 1) + w0 * _shift_up(dy, dy_next, 2)) * valid
            dp_ref[b, :, C_CC:C_CC + BRANCH_WIDTH] = (du * cx).astype(BF16)
            dp_ref[b, :, C_CX:C_CX + BRANCH_WIDTH] = (du * cc).astype(BF16)
            dcw_ref[0:1, :] += jnp.sum(dy * u2, axis=0, keepdims=True)
            dcw_ref[1:2, :] += jnp.sum(dy * u1, axis=0, keepdims=True)
            dcw_ref[2:3, :] += jnp.sum(dy * u, axis=0, keepdims=True)
            dy_s[b] = dy

        @pl.when(step == nc - 1)
        def _():
            bkt = bkt_ref[...]
            row = lax.broadcasted_iota(jnp.int32, (N_BUCKETS, BLOCK), 0)
            lane = lax.broadcasted_iota(jnp.int32, (N_BUCKETS, BLOCK), 1)

            def one_bucket(bk, acc):
                sel = bkt == bk
                for h in range(ATT_HEADS):
                    t = jnp.where(sel, dbias_s[h], 0.0)
                    s = jnp.sum(jnp.sum(t, axis=1, keepdims=True), axis=0, keepdims=True)
                    acc = acc + jnp.where((row == bk) & (lane == h), jnp.broadcast_to(s, acc.shape), 0.0)
                return acc

            drb_ref[...] = lax.fori_loop(0, N_BUCKETS, one_bucket, jnp.zeros((N_BUCKETS, BLOCK), F32))

    lp = nc * BLOCK
    smem = pl.BlockSpec(memory_space=pltpu.SMEM)
    blk = lambda s: nc - 1 - s
    prev = lambda s: jnp.maximum(nc - 2 - s, 0)
    proj3 = proj.reshape(nb, lp, ABC_WIDTH)
    res = pl.pallas_call(
        body, name="mixers_bwd",
        grid=(nc,),
        in_specs=[pl.BlockSpec((nb, BLOCK, ABC_WIDTH), lambda s: (0, blk(s), 0)),
                  pl.BlockSpec((nb, BLOCK, 2 * BLOCK), lambda s: (0, prev(s), C_AK // (2 * BLOCK))),
                  pl.BlockSpec((nb, BLOCK, 1280), lambda s: (0, prev(s), C_CC // 1280)),
                  pl.BlockSpec((nb, BLOCK, N_BRANCH * BRANCH_WIDTH), lambda s: (0, blk(s), 0)),
                  pl.BlockSpec((nb, 1, RET_HEADS, BLOCK, BLOCK), lambda s: (0, blk(s), 0, 0, 0)),
                  pl.BlockSpec((BLOCK, BLOCK), lambda s: (blk(s), 0)),
                  pl.BlockSpec((BLOCK, BLOCK), lambda s: (blk(s), 0)),
                  pl.BlockSpec((BLOCK, 2 * BLOCK), lambda s: (0, 0)),
                  smem, smem,
                  pl.BlockSpec((3, BRANCH_WIDTH), lambda s: (0, 0))],
        out_specs=[pl.BlockSpec((nb, BLOCK, ABC_WIDTH), lambda s: (0, blk(s), 0)),
                   pl.BlockSpec((N_BUCKETS, BLOCK), lambda s: (0, 0)),
                   pl.BlockSpec((ATT_HEADS, BLOCK), lambda s: (0, 0)),
                   pl.BlockSpec((8, BRANCH_WIDTH), lambda s: (0, 0))],
        out_shape=[jax.ShapeDtypeStruct((nb, lp, ABC_WIDTH), BF16),
                   jax.ShapeDtypeStruct((N_BUCKETS, BLOCK), F32),
                   jax.ShapeDtypeStruct((ATT_HEADS, BLOCK), F32),
                   jax.ShapeDtypeStruct((8, BRANCH_WIDTH), F32)],
        scratch_shapes=[pltpu.VMEM((ATT_HEADS, BLOCK, 2 * BLOCK), F32),
                        pltpu.VMEM((ATT_HEADS, BLOCK, 2 * BLOCK), F32),
                        pltpu.VMEM((nb, BLOCK, 2 * BLOCK), F32),
                        pltpu.VMEM((nb, RET_HEADS, BLOCK, BLOCK), F32),
                        pltpu.VMEM((nb, BLOCK, BRANCH_WIDTH), F32)],
        compiler_params=_cparams("arbitrary"),
    )(proj3, proj3, proj3, d_br.reshape(nb, lp, N_BRANCH * BRANCH_WIDTH), states, cosf, sinf, bkt, rel_bias, sinks,
      conv_w)
    return (res[0].reshape(nb * lp, ABC_WIDTH),) + tuple(res[1:])


MERGE_TILE = 256
MERGE_FWD_TILE = 544


def _merge_forward(br_ref, m_ref, wb_ref, wo_ref):
    bo, gates = [], []
    mixed_pre = None
    for g in range(N_BRANCH):
        br_g = br_ref[:, BRANCH_WIDTH * g:BRANCH_WIDTH * (g + 1)]
        bo_g = jnp.concatenate([_nn(br_g, wb_ref[p, g]) for p in range(N_CHIPS)], axis=1)
        gate_g = _sigmoid(m_ref[:, D_MODEL * g:D_MODEL * (g + 1)].astype(F32))
        bo.append(bo_g)
        gates.append(gate_g)
        mixed_pre = gate_g * bo_g if mixed_pre is None else mixed_pre + gate_g * bo_g
    mixed = _nn(mixed_pre.astype(BF16), wo_ref[...])
    r = lax.rsqrt(jnp.mean(mixed * mixed, axis=-1, keepdims=True) + RMS_EPS)
    return bo, gates, mixed_pre, mixed, r


def merge_fwd(x2d, br, pm, wb, wo, g_post):
    t = x2d.shape[0]
    tm = MERGE_FWD_TILE if t % MERGE_FWD_TILE == 0 else BLOCK

    def body(x_ref, br_ref, m_ref, wb_ref, wo_ref, g_ref, o_ref):
        _, _, _, mixed, r = _merge_forward(br_ref, m_ref, wb_ref, wo_ref)
        o_ref[...] = x_ref[...] + mixed * r * g_ref[...]

    return pl.pallas_call(
        body, name="merge_fwd",
        grid=(t // tm,),
        in_specs=[pl.BlockSpec((tm, D_MODEL), lambda i: (i, 0)),
                  pl.BlockSpec((tm, N_BRANCH * BRANCH_WIDTH), lambda i: (i, 0)),
                  pl.BlockSpec((tm, MERGE_WIDTH), lambda i: (i, 0)),
                  pl.BlockSpec((N_CHIPS, N_BRANCH, BRANCH_WIDTH, SHARD_D), lambda i: (0, 0, 0, 0)),
                  pl.BlockSpec((D_MODEL, D_MODEL), lambda i: (0, 0)),
                  pl.BlockSpec((1, D_MODEL), lambda i: (0, 0))],
        out_specs=pl.BlockSpec((tm, D_MODEL), lambda i: (i, 0)),
        out_shape=jax.ShapeDtypeStruct((t, D_MODEL), F32),
        compiler_params=_cparams("parallel"),
    )(x2d, br, pm, wb, wo, g_post)


def merge_bwd(d_out, br, pm, wb, wo, g_post):
    t = d_out.shape[0]
    tm = MERGE_TILE if t % MERGE_TILE == 0 else BLOCK

    def body(do_ref, br_ref, m_ref, wb_ref, wo_ref, g_ref, dbr_ref, dm_ref, dg_ref, dwb_ref, dwo_ref):

        @pl.when(pl.program_id(0) == 0)
        def _():
            dwb_ref[...] = jnp.zeros_like(dwb_ref)
            dwo_ref[...] = jnp.zeros_like(dwo_ref)
            dg_ref[...] = jnp.zeros_like(dg_ref)

        bo, gates, mixed_pre, mixed, r = _merge_forward(br_ref, m_ref, wb_ref, wo_ref)
        d_o = do_ref[...]
        nh = mixed * r
        dg_ref[0:1, :] += jnp.sum(d_o * nh, axis=0, keepdims=True)
        dn = d_o * g_ref[...]
        d_mixed = (r * (dn - nh * jnp.mean(dn * nh, axis=-1, keepdims=True))).astype(BF16)
        dwo_ref[...] += _tn(mixed_pre.astype(BF16), d_mixed)
        d_pre = _nt(d_mixed, wo_ref[...])
        for g in range(N_BRANCH):
            br_g = br_ref[:, BRANCH_WIDTH * g:BRANCH_WIDTH * (g + 1)]
            d_bo = (d_pre * gates[g]).astype(BF16)
            dm_ref[:, D_MODEL * g:D_MODEL * (g + 1)] = (
                d_pre * bo[g] * gates[g] * (1.0 - gates[g])).astype(BF16)
            d_br_g = None
            for p in range(N_CHIPS):
                d_bo_p = d_bo[:, SHARD_D * p:SHARD_D * (p + 1)]
                part = _nt(d_bo_p, wb_ref[p, g])
                d_br_g = part if d_br_g is None else d_br_g + part
                dwb_ref[p, g] += _tn(br_g, d_bo_p)
            dbr_ref[:, BRANCH_WIDTH * g:BRANCH_WIDTH * (g + 1)] = d_br_g.astype(BF16)

    return pl.pallas_call(
        body, name="merge_bwd",
        grid=(t // tm,),
        in_specs=[pl.BlockSpec((tm, D_MODEL), lambda i: (i, 0)),
                  pl.BlockSpec((tm, N_BRANCH * BRANCH_WIDTH), lambda i: (i, 0)),
                  pl.BlockSpec((tm, MERGE_WIDTH), lambda i: (i, 0)),
                  pl.BlockSpec((N_CHIPS, N_BRANCH, BRANCH_WIDTH, SHARD_D), lambda i: (0, 0, 0, 0)),
                  pl.BlockSpec((D_MODEL, D_MODEL), lambda i: (0, 0)),
                  pl.BlockSpec((1, D_MODEL), lambda i: (0, 0))],
        out_specs=[pl.BlockSpec((tm, N_BRANCH * BRANCH_WIDTH), lambda i: (i, 0)),
                   pl.BlockSpec((tm, MERGE_WIDTH), lambda i: (i, 0)),
                   pl.BlockSpec((8, D_MODEL), lambda i: (0, 0)),
                   pl.BlockSpec((N_CHIPS, N_BRANCH, BRANCH_WIDTH, SHARD_D), lambda i: (0, 0, 0, 0)),
                   pl.BlockSpec((D_MODEL, D_MODEL), lambda i: (0, 0))],
        out_shape=[jax.ShapeDtypeStruct((t, N_BRANCH * BRANCH_WIDTH), BF16),
                   jax.ShapeDtypeStruct((t, MERGE_WIDTH), BF16),
                   jax.ShapeDtypeStruct((8, D_MODEL), F32),
                   jax.ShapeDtypeStruct((N_CHIPS, N_BRANCH, BRANCH_WIDTH, SHARD_D), F32),
                   jax.ShapeDtypeStruct((D_MODEL, D_MODEL), F32)],
        compiler_params=_cparams("arbitrary"),
    )(d_out, br, pm, wb, wo, g_post)


def loss_head(xf, target2d, nb, nc):
    def body(x_ref, t_ref, l_ref, dx_ref):
        b = pl.program_id(0)
        n = pl.program_id(1)

        @pl.when((b == 0) & (n == 0))
        def _():
            l_ref[...] = jnp.zeros_like(l_ref)

        @pl.when(n == 0)
        def _():
            dx_ref[...] = jnp.zeros_like(dx_ref)

        @pl.when(n > 0)
        def _():
            e = x_ref[...] - t_ref[...]
            dx_ref[...] = e * (1.0 / D_MODEL)
            s = jnp.sum(jnp.sum(e * e, axis=1, keepdims=True), axis=0, keepdims=True)
            l_ref[...] += jnp.broadcast_to(s * (0.5 / D_MODEL), l_ref.shape)

    return pl.pallas_call(
        body, name="loss_head",
        grid=(nb, nc),
        in_specs=[pl.BlockSpec((BLOCK, D_MODEL), lambda b, n: (b * nc + n, 0)),
                  pl.BlockSpec((BLOCK, D_MODEL), lambda b, n: (b * (nc - 1) + jnp.maximum(n - 1, 0), 0))],
        out_specs=[pl.BlockSpec((8, BLOCK), lambda b, n: (0, 0)),
                   pl.BlockSpec((BLOCK, D_MODEL), lambda b, n: (b * nc + n, 0))],
        out_shape=[jax.ShapeDtypeStruct((8, BLOCK), F32),
                   jax.ShapeDtypeStruct(xf.shape, F32)],
        compiler_params=_cparams("arbitrary", "arbitrary"),
    )(xf, target2d)


N_ABC_TILES = ABC_WIDTH // COL_TILE
N_M_TILES = MERGE_WIDTH // COL_TILE


def proj_dgrad(d_abc, d_m, w, x2d, g, d_out):
    t = x2d.shape[0]
    tm = ROW_TILE if t % ROW_TILE == 0 else BLOCK
    nk = N_ABC_TILES + N_M_TILES

    def body(da_ref, dm_ref, w_ref, x_ref, g_ref, do_ref, dx_ref, dg_ref, acc):
        i = pl.program_id(0)
        k = pl.program_id(1)

        @pl.when((i == 0) & (k == 0))
        def _():
            dg_ref[...] = jnp.zeros_like(dg_ref)

        @pl.when(k == 0)
        def _():
            acc[...] = jnp.zeros_like(acc)

        @pl.when(k < N_ABC_TILES)
        def _():
            acc[...] += _nn(da_ref[...], w_ref[...])

        @pl.when(k >= N_ABC_TILES)
        def _():
            acc[...] += _nn(dm_ref[...], w_ref[...])

        @pl.when(k == nk - 1)
        def _():
            x = x_ref[...]
            r = lax.rsqrt(jnp.mean(x * x, axis=-1, keepdims=True) + RMS_EPS)
            nh = x * r
            dh = acc[...]
            dg_ref[0:1, :] += jnp.sum(dh * nh, axis=0, keepdims=True)
            dn = dh * g_ref[...]
            dx_ref[...] = do_ref[...] + r * (dn - nh * jnp.mean(dn * nh, axis=-1, keepdims=True))

    return pl.pallas_call(
        body, name="proj_dgrad",
        grid=(t // tm, nk),
        in_specs=[pl.BlockSpec((tm, COL_TILE), lambda i, k: (i, jnp.minimum(k, N_ABC_TILES - 1))),
                  pl.BlockSpec((tm, COL_TILE), lambda i, k: (i, jnp.maximum(k - N_ABC_TILES, 0))),
                  pl.BlockSpec((COL_TILE, D_MODEL), lambda i, k: (k, 0)),
                  pl.BlockSpec((tm, D_MODEL), lambda i, k: (i, 0)),
                  pl.BlockSpec((1, D_MODEL), lambda i, k: (0, 0)),
                  pl.BlockSpec((tm, D_MODEL), lambda i, k: (i, 0))],
        out_specs=[pl.BlockSpec((tm, D_MODEL), lambda i, k: (i, 0)),
                   pl.BlockSpec((8, D_MODEL), lambda i, k: (0, 0))],
        out_shape=[jax.ShapeDtypeStruct((t, D_MODEL), F32),
                   jax.ShapeDtypeStruct((8, D_MODEL), F32)],
        scratch_shapes=[pltpu.VMEM((tm, D_MODEL), F32)],
        compiler_params=_cparams("arbitrary", "arbitrary"),
    )(d_abc, d_m, w, x2d, g, d_out)


def proj_wgrad(hb, d_abc, d_m):
    t = hb.shape[0]
    nj = N_ABC_TILES + N_M_TILES

    def body(h_ref, da_ref, dm_ref, o_ref):
        j = pl.program_id(0)

        @pl.when(j < N_ABC_TILES)
        def _():
            o_ref[...] = _tn(da_ref[...], h_ref[...])

        @pl.when(j >= N_ABC_TILES)
        def _():
            o_ref[...] = _tn(dm_ref[...], h_ref[...])

    return pl.pallas_call(
        body, name="proj_wgrad",
        grid=(nj,),
        in_specs=[pl.BlockSpec((t, D_MODEL), lambda j: (0, 0)),
                  pl.BlockSpec((t, COL_TILE), lambda j: (0, jnp.minimum(j, N_ABC_TILES - 1))),
                  pl.BlockSpec((t, COL_TILE), lambda j: (0, jnp.maximum(j - N_ABC_TILES, 0)))],
        out_specs=pl.BlockSpec((COL_TILE, D_MODEL), lambda j: (j, 0)),
        out_shape=jax.ShapeDtypeStruct((PROJ_WIDTH, D_MODEL), F32),
        compiler_params=_cparams("arbitrary"),
    )(hb, d_abc, d_m)


def _adamw_math(w, g, m, v):
    m = ADAM_B1 * m + (1.0 - ADAM_B1) * g
    v = ADAM_B2 * v + (1.0 - ADAM_B2) * jnp.square(g)
    m_hat = m / (1.0 - ADAM_B1 ** ADAM_STEP)
    v_hat = v / (1.0 - ADAM_B2 ** ADAM_STEP)
    delta = -ADAM_LR * (m_hat / (jnp.sqrt(v_hat) + ADAM_EPS) + ADAM_WD * w)
    return delta, m, v


def adamw_layer(w, g, m, v, layer, acc, after):
    _, r, c = w.shape
    tr = _row_tile(r)

    def body(*refs):
        w_ref, g_ref, m_ref, v_ref = refs[:4]
        go_ref, d_ref, mo_ref, vo_ref = refs[-4:]
        g_val = g_ref[...]
        d, m_new, v_new = _adamw_math(w_ref[...], g_val, m_ref[...], v_ref[...])
        go_ref[...] = g_val
        d_ref[...] = d
        mo_ref[...] = m_new
        vo_ref[...] = v_new

    slab = pl.BlockSpec((None, tr, c), lambda i: (layer, i, 0))
    ins = [w, g, m, v, after]
    in_specs = [slab, pl.BlockSpec((tr, c), lambda i: (i, 0)), slab, slab, ANY]
    aliases = {}
    if acc is not None:
        ins += list(acc)
        in_specs += [ANY] * 4
        aliases = {5 + i: i for i in range(4)}
    return pl.pallas_call(
        body, name="adamw_layer",
        grid=(r // tr,),
        in_specs=in_specs, out_specs=[slab] * 4,
        out_shape=[jax.ShapeDtypeStruct(w.shape, F32)] * 4,
        input_output_aliases=aliases,
        compiler_params=_cparams("parallel"),
    )(*ins)


def adamw_small(params):
    k = len(params)

    def body(*refs):
        ins, outs = refs[:4 * k], refs[4 * k:]
        for i in range(k):
            d, m_new, v_new = _adamw_math(*[r[...] for r in ins[4 * i:4 * i + 4]])
            outs[3 * i][...] = d
            outs[3 * i + 1][...] = m_new
            outs[3 * i + 2][...] = v_new

    flat = [a for p in params for a in p]
    vm = pl.BlockSpec(memory_space=pltpu.VMEM)
    out_shape = [jax.ShapeDtypeStruct(p[0].shape, F32) for p in params for _ in range(3)]
    res = pl.pallas_call(
        body, name="adamw_small",
        in_specs=[vm] * len(flat), out_specs=[vm] * len(out_shape), out_shape=out_shape,
    )(*flat)
    return [tuple(res[3 * i:3 * i + 3]) for i in range(k)]


ANY = pl.BlockSpec(memory_space=pl.ANY)


def _place():
    return lax.axis_index("x"), lax.axis_index("y"), lax.axis_index("c")


HBM = pl.BlockSpec(memory_space=pltpu.HBM)
SEM = pl.BlockSpec(memory_space=pltpu.SEMAPHORE)
EFFECT = pltpu.SideEffectType.DATAFLOW_SIDE_EFFECTING


def _other_chips(x, y):
    return [(1 - x, y), (x, 1 - y), (1 - x, 1 - y)]


def _own_slot(shard, chip):
    buf = lax.empty((N_CHIPS,) + shard.shape, shard.dtype)
    return lax.dynamic_update_slice(buf, shard[None], (chip, 0, 0, 0))


def _hbm(a):
    return pltpu.with_memory_space_constraint(a, pltpu.HBM)


def gather_start(bufs, after):
    n = len(bufs)

    def body(*refs):
        g_refs = refs[:n]
        send_sems, recv_sems = refs[n + 1], refs[n + 2]
        token = refs[-1]
        x, y, c = _place()
        me_p = 2 * x + y
        for t in range(n):
            for k, (qx, qy) in enumerate(_other_chips(x, y)):
                slab = g_refs[t].at[me_p, c]
                pltpu.make_async_remote_copy(src_ref=slab, dst_ref=slab, send_sem=send_sems.at[3 * t + k],
                                             recv_sem=recv_sems.at[3 * t + k], device_id=(qx, qy, c),
                                             device_id_type=MESH).start()
        token[...] = jnp.zeros_like(token)

    res = pl.pallas_call(
        body, name="gather_start",
        in_specs=[HBM] * n + [ANY],
        out_specs=[SEM, SEM] + [HBM] * n + [pl.BlockSpec(memory_space=pltpu.VMEM)],
        out_shape=[pltpu.SemaphoreType.DMA((3 * n,)), pltpu.SemaphoreType.DMA((3 * n,))]
        + [pltpu.HBM(b.shape, b.dtype) for b in bufs] + [jax.ShapeDtypeStruct((8, LANES), F32)],
        input_output_aliases={t: 2 + t for t in range(n)},
        compiler_params=pltpu.CompilerParams(has_side_effects=EFFECT),
    )(*[_hbm(b) for b in bufs], after)
    return res[0], res[1], list(res[2:2 + n]), res[-1]


def gather_wait(bufs, send_sems, recv_sems, after):
    n = len(bufs)

    def body(*refs):
        g_refs = refs[:n]
        send_sems, recv_sems = refs[n], refs[n + 1]
        x, y, c = _place()
        me_p = 2 * x + y
        for t in range(n):
            for k, (qx, qy) in enumerate(_other_chips(x, y)):
                cp = pltpu.make_async_remote_copy(src_ref=g_refs[t].at[me_p, c], dst_ref=g_refs[t].at[2 * qx + qy, c],
                                                  send_sem=send_sems.at[3 * t + k], recv_sem=recv_sems.at[3 * t + k],
                                                  device_id=(qx, qy, c), device_id_type=MESH)
                cp.wait_send()
                cp.wait_recv()

    return pl.pallas_call(
        body, name="gather_wait",
        in_specs=[HBM] * n + [SEM, SEM, ANY],
        out_specs=[HBM] * n,
        out_shape=[pltpu.HBM(b.shape, b.dtype) for b in bufs],
        input_output_aliases={t: t for t in range(n)},
        compiler_params=pltpu.CompilerParams(has_side_effects=EFFECT),
    )(*bufs, send_sems, recv_sems, after)


def gather_forward(bufs):
    n = len(bufs)

    def body(*refs):
        g_refs = refs[n:2 * n]
        send_sems, recv_sems = refs[2 * n:]
        x, y, c = _place()
        sibling = (x, y, 1 - c)
        chips = _other_chips(x, y)
        passed = []
        for t in range(n):
            for k, (qx, qy) in enumerate(chips):
                slab = g_refs[t].at[2 * qx + qy, c]
                fwd = pltpu.make_async_remote_copy(src_ref=slab, dst_ref=slab, send_sem=send_sems.at[3 * t + k],
                                                   recv_sem=recv_sems.at[3 * t + k], device_id=sibling,
                                                   device_id_type=MESH)
                fwd.start()
                passed.append(fwd)
        for t in range(n):
            for k, (qx, qy) in enumerate(chips):
                slab = g_refs[t].at[2 * qx + qy, 1 - c]
                pltpu.make_async_remote_copy(src_ref=slab, dst_ref=slab, send_sem=send_sems.at[3 * t + k],
                                             recv_sem=recv_sems.at[3 * t + k], device_id=sibling,
                                             device_id_type=MESH).wait_recv()
        for cp in passed:
            cp.wait_send()

    return pl.pallas_call(
        body, name="gather_forward",
        in_specs=[ANY] * n, out_specs=[ANY] * n,
        out_shape=[jax.ShapeDtypeStruct(b.shape, b.dtype) for b in bufs],
        input_output_aliases={t: t for t in range(n)},
        scratch_shapes=[pltpu.SemaphoreType.DMA((3 * n,)), pltpu.SemaphoreType.DMA((3 * n,))],
    )(*bufs)


def forward_start(bufs):
    n = len(bufs)

    def body(*refs):
        g_refs = refs[:n]
        send_sems, recv_sems = refs[n], refs[n + 1]
        token = refs[-1]
        x, y, c = _place()
        for t in range(n):
            for k, (qx, qy) in enumerate(_other_chips(x, y)):
                slab = g_refs[t].at[2 * qx + qy, c]
                pltpu.make_async_remote_copy(src_ref=slab, dst_ref=slab, send_sem=send_sems.at[3 * t + k],
                                             recv_sem=recv_sems.at[3 * t + k], device_id=(x, y, 1 - c),
                                             device_id_type=MESH).start()
        token[...] = jnp.zeros_like(token)

    res = pl.pallas_call(
        body, name="forward_start",
        in_specs=[HBM] * n,
        out_specs=[SEM, SEM] + [HBM] * n + [pl.BlockSpec(memory_space=pltpu.VMEM)],
        out_shape=[pltpu.SemaphoreType.DMA((3 * n,)), pltpu.SemaphoreType.DMA((3 * n,))]
        + [pltpu.HBM(b.shape, b.dtype) for b in bufs] + [jax.ShapeDtypeStruct((8, LANES), F32)],
        input_output_aliases={t: 2 + t for t in range(n)},
        compiler_params=pltpu.CompilerParams(has_side_effects=EFFECT),
    )(*[_hbm(b) for b in bufs])
    return res[0], res[1], list(res[2:2 + n]), res[-1]


def forward_wait(bufs, send_sems, recv_sems, after):
    n = len(bufs)

    def body(*refs):
        g_refs = refs[:n]
        send_sems, recv_sems = refs[n], refs[n + 1]
        x, y, c = _place()
        for t in range(n):
            for k, (qx, qy) in enumerate(_other_chips(x, y)):
                cp = pltpu.make_async_remote_copy(src_ref=g_refs[t].at[2 * qx + qy, c],
                                                  dst_ref=g_refs[t].at[2 * qx + qy, 1 - c],
                                                  send_sem=send_sems.at[3 * t + k], recv_sem=recv_sems.at[3 * t + k],
                                                  device_id=(x, y, 1 - c), device_id_type=MESH)
                cp.wait_send()
                cp.wait_recv()

    return pl.pallas_call(
        body, name="forward_wait",
        in_specs=[HBM] * n + [SEM, SEM, ANY],
        out_specs=[HBM] * n,
        out_shape=[pltpu.HBM(b.shape, b.dtype) for b in bufs],
        input_output_aliases={t: t for t in range(n)},
        compiler_params=pltpu.CompilerParams(has_side_effects=EFFECT),
    )(*bufs, send_sems, recv_sems, after)


def exchange_small(pack, after):
    def body(p_ref, after_ref, o_ref, send_sems, recv_sems, local_sem):
        x, y, c = _place()
        me = 4 * x + 2 * y + c
        mine = pltpu.make_async_copy(p_ref, o_ref.at[me], local_sem)
        mine.start()
        sends = []
        for k in range(1, 8):
            fx, fy, fc = (k >> 2) & 1, (k >> 1) & 1, k & 1
            peer = (x ^ fx, y ^ fy, c ^ fc)
            cp = pltpu.make_async_remote_copy(src_ref=p_ref, dst_ref=o_ref.at[me], send_sem=send_sems.at[k - 1],
                                              recv_sem=recv_sems.at[k - 1], device_id=peer, device_id_type=MESH)
            cp.start()
            sends.append(cp)
        for k in range(1, 8):
            fx, fy, fc = (k >> 2) & 1, (k >> 1) & 1, k & 1
            peer = (x ^ fx, y ^ fy, c ^ fc)
            slot = o_ref.at[4 * peer[0] + 2 * peer[1] + peer[2]]
            pltpu.make_async_remote_copy(src_ref=slot, dst_ref=slot, send_sem=send_sems.at[k - 1],
                                         recv_sem=recv_sems.at[k - 1], device_id=peer, device_id_type=MESH).wait_recv()
        for cp in sends:
            cp.wait_send()
        mine.wait()

    return pl.pallas_call(
        body, name="exchange_small",
        in_specs=[ANY, ANY], out_specs=ANY,
        out_shape=jax.ShapeDtypeStruct((8,) + pack.shape, pack.dtype),
        scratch_shapes=[pltpu.SemaphoreType.DMA((7,)), pltpu.SemaphoreType.DMA((7,)), pltpu.SemaphoreType.DMA],
    )(pack, after)


def small_start(pack, me, after):
    buf = lax.dynamic_update_slice(lax.empty((8,) + pack.shape, pack.dtype), pack[None], (me, 0, 0))

    def body(b_ref, after_ref, send_sems, recv_sems, thru, token):
        x, y, c = _place()
        slot = b_ref.at[4 * x + 2 * y + c]
        for k in range(1, 8):
            peer = (x ^ ((k >> 2) & 1), y ^ ((k >> 1) & 1), c ^ (k & 1))
            pltpu.make_async_remote_copy(src_ref=slot, dst_ref=slot, send_sem=send_sems.at[k - 1],
                                         recv_sem=recv_sems.at[k - 1], device_id=peer, device_id_type=MESH).start()
        token[...] = jnp.zeros_like(token)

    return pl.pallas_call(
        body, name="small_start",
        in_specs=[HBM, ANY],
        out_specs=[SEM, SEM, HBM, pl.BlockSpec(memory_space=pltpu.VMEM)],
        out_shape=[pltpu.SemaphoreType.DMA((7,)), pltpu.SemaphoreType.DMA((7,)), pltpu.HBM(buf.shape, buf.dtype),
                   jax.ShapeDtypeStruct((8, LANES), F32)],
        input_output_aliases={0: 2},
        compiler_params=pltpu.CompilerParams(has_side_effects=EFFECT),
    )(_hbm(buf), after)


def small_wait(buf, send_sems, recv_sems, after):
    def body(b_ref, send_sems, recv_sems, after_ref, thru):
        x, y, c = _place()
        mine = b_ref.at[4 * x + 2 * y + c]
        for k in range(1, 8):
            peer = (x ^ ((k >> 2) & 1), y ^ ((k >> 1) & 1), c ^ (k & 1))
            cp = pltpu.make_async_remote_copy(src_ref=mine, dst_ref=b_ref.at[4 * peer[0] + 2 * peer[1] + peer[2]],
                                              send_sem=send_sems.at[k - 1], recv_sem=recv_sems.at[k - 1],
                                              device_id=peer, device_id_type=MESH)
            cp.wait_send()
            cp.wait_recv()

    return pl.pallas_call(
        body, name="small_wait",
        in_specs=[HBM, SEM, SEM, ANY], out_specs=HBM,
        out_shape=pltpu.HBM(buf.shape, buf.dtype),
        input_output_aliases={0: 0},
        compiler_params=pltpu.CompilerParams(has_side_effects=EFFECT),
    )(buf, send_sems, recv_sems, after)


def swap_start(grads, after):
    n = len(grads)

    def body(*refs):
        g_refs, l_refs = refs[:n], refs[n:2 * n]
        send_sems, recv_sems = refs[2 * n + 1], refs[2 * n + 2]
        token = refs[-1]
        x, y, c = _place()
        for t in range(n):
            for p in range(N_CHIPS):
                pltpu.make_async_remote_copy(src_ref=g_refs[t].at[p, 1 - c], dst_ref=l_refs[t].at[p],
                                             send_sem=send_sems.at[N_CHIPS * t + p],
                                             recv_sem=recv_sems.at[N_CHIPS * t + p],
                                             device_id=(x, y, 1 - c), device_id_type=MESH).start()
        token[...] = jnp.zeros_like(token)

    lands = [lax.empty((N_CHIPS,) + g.shape[2:], g.dtype) for g in grads]
    res = pl.pallas_call(
        body, name="swap_start",
        in_specs=[HBM] * (2 * n) + [ANY],
        out_specs=[SEM, SEM] + [HBM] * (2 * n) + [pl.BlockSpec(memory_space=pltpu.VMEM)],
        out_shape=[pltpu.SemaphoreType.DMA((N_CHIPS * n,)), pltpu.SemaphoreType.DMA((N_CHIPS * n,))]
        + [pltpu.HBM(a.shape, a.dtype) for a in grads + lands] + [jax.ShapeDtypeStruct((8, LANES), F32)],
        input_output_aliases={t: 2 + t for t in range(2 * n)},
        compiler_params=pltpu.CompilerParams(has_side_effects=EFFECT),
    )(*[_hbm(a) for a in grads + lands], after)
    return res[0], res[1], list(res[2:2 + n]), list(res[2 + n:2 + 2 * n]), res[-1]


def swap_wait(grads, lands, send_sems, recv_sems, after):
    n = len(grads)

    def body(*refs):
        g_refs, l_refs = refs[:n], refs[n:2 * n]
        send_sems, recv_sems = refs[2 * n], refs[2 * n + 1]
        x, y, c = _place()
        for t in range(n):
            for p in range(N_CHIPS):
                cp = pltpu.make_async_remote_copy(src_ref=g_refs[t].at[p, 1 - c], dst_ref=l_refs[t].at[p],
                                                  send_sem=send_sems.at[N_CHIPS * t + p],
                                                  recv_sem=recv_sems.at[N_CHIPS * t + p],
                                                  device_id=(x, y, 1 - c), device_id_type=MESH)
                cp.wait_send()
                cp.wait_recv()

    res = pl.pallas_call(
        body, name="swap_wait",
        in_specs=[HBM] * (2 * n) + [SEM, SEM, ANY],
        out_specs=[HBM] * (2 * n),
        out_shape=[pltpu.HBM(a.shape, a.dtype) for a in grads + lands],
        input_output_aliases={t: t for t in range(2 * n)},
        compiler_params=pltpu.CompilerParams(has_side_effects=EFFECT),
    )(*grads, *lands, send_sems, recv_sems, after)
    return list(res[:n]), list(res[n:])


def _row_tile(r):
    return max(t for t in range(16, 513, 16) if r % t == 0)


def add_own_half(g, other, c_arr):
    _, _, r, cols = g.shape
    tr = _row_tile(r)

    def body(c_ref, a_ref, b_ref, o_ref):
        o_ref[...] = (a_ref[...] + b_ref[...]).astype(BF16)

    return pl.pallas_call(
        body, name="add_own_half",
        grid_spec=pltpu.PrefetchScalarGridSpec(
            num_scalar_prefetch=1, grid=(N_CHIPS, r // tr),
            in_specs=[pl.BlockSpec((None, None, tr, cols), lambda p, i, c_ref: (p, c_ref[0], i, 0)),
                      pl.BlockSpec((None, tr, cols), lambda p, i, c_ref: (p, i, 0))],
            out_specs=pl.BlockSpec((None, tr, cols), lambda p, i, c_ref: (p, i, 0))),
        out_shape=jax.ShapeDtypeStruct((N_CHIPS, r, cols), BF16),
        compiler_params=_cparams("parallel", "parallel"),
    )(c_arr, g, other)


def scatter_start(partials):
    n = len(partials)

    def body(*refs):
        s_refs, l_refs = refs[:n], refs[n:2 * n]
        send_sems, recv_sems = refs[2 * n], refs[2 * n + 1]
        token = refs[-1]
        x, y, c = _place()
        for t in range(n):
            for k, (qx, qy) in enumerate(_other_chips(x, y)):
                pltpu.make_async_remote_copy(src_ref=s_refs[t].at[2 * qx + qy], dst_ref=l_refs[t].at[k],
                                             send_sem=send_sems.at[3 * t + k], recv_sem=recv_sems.at[3 * t + k],
                                             device_id=(qx, qy, c), device_id_type=MESH).start()
        token[...] = jnp.zeros_like(token)

    lands = [lax.empty((3,) + s.shape[1:], s.dtype) for s in partials]
    res = pl.pallas_call(
        body, name="scatter_start",
        in_specs=[HBM] * (2 * n),
        out_specs=[SEM, SEM] + [HBM] * (2 * n) + [pl.BlockSpec(memory_space=pltpu.VMEM)],
        out_shape=[pltpu.SemaphoreType.DMA((3 * n,)), pltpu.SemaphoreType.DMA((3 * n,))]
        + [pltpu.HBM(a.shape, a.dtype) for a in partials + lands] + [jax.ShapeDtypeStruct((8, LANES), F32)],
        input_output_aliases={t: 2 + t for t in range(2 * n)},
        compiler_params=pltpu.CompilerParams(has_side_effects=EFFECT),
    )(*[_hbm(a) for a in partials + lands])
    return res[0], res[1], list(res[2:2 + n]), list(res[2 + n:2 + 2 * n]), res[-1]


def scatter_wait(partials, lands, send_sems, recv_sems, after):
    n = len(partials)

    def body(*refs):
        s_refs, l_refs = refs[:n], refs[n:2 * n]
        send_sems, recv_sems = refs[2 * n], refs[2 * n + 1]
        x, y, c = _place()
        for t in range(n):
            for k, (qx, qy) in enumerate(_other_chips(x, y)):
                cp = pltpu.make_async_remote_copy(src_ref=s_refs[t].at[2 * qx + qy], dst_ref=l_refs[t].at[k],
                                                  send_sem=send_sems.at[3 * t + k], recv_sem=recv_sems.at[3 * t + k],
                                                  device_id=(qx, qy, c), device_id_type=MESH)
                cp.wait_send()
                cp.wait_recv()

    res = pl.pallas_call(
        body, name="scatter_wait",
        in_specs=[HBM] * (2 * n) + [SEM, SEM, ANY],
        out_specs=[HBM] * (2 * n),
        out_shape=[pltpu.HBM(a.shape, a.dtype) for a in partials + lands],
        input_output_aliases={t: t for t in range(2 * n)},
        compiler_params=pltpu.CompilerParams(has_side_effects=EFFECT),
    )(*partials, *lands, send_sems, recv_sems, after)
    return list(res[:n]), list(res[n:])


def sum_chips(own, parts, where):
    _, r, cols = own.shape
    tr = _row_tile(r)

    def body(w_ref, a_ref, p_ref, o_ref):
        acc = a_ref[...].astype(F32)
        for k in range(3):
            acc = acc + p_ref[k].astype(F32)
        o_ref[...] = acc

    return pl.pallas_call(
        body, name="sum_chips",
        grid_spec=pltpu.PrefetchScalarGridSpec(
            num_scalar_prefetch=1, grid=(r // tr,),
            in_specs=[pl.BlockSpec((None, tr, cols), lambda i, w_ref: (w_ref[0], i, 0)),
                      pl.BlockSpec((3, tr, cols), lambda i, w_ref: (0, i, 0))],
            out_specs=pl.BlockSpec((None, tr, cols), lambda i, w_ref: (w_ref[1], i, 0))),
        out_shape=jax.ShapeDtypeStruct((DEPTH, r, cols), F32),
        compiler_params=_cparams("parallel"),
    )(where, own, parts)


def sibling_share_layer(bufs):
    n = len(bufs)

    def body(*refs):
        o_refs = refs[n:2 * n]
        send_sems, recv_sems = refs[2 * n:]
        x, y, c = _place()
        cps = []
        for t in range(n):
            cp = pltpu.make_async_remote_copy(src_ref=o_refs[t].at[c], dst_ref=o_refs[t].at[c], send_sem=send_sems.at[t],
                                              recv_sem=recv_sems.at[t], device_id=(x, y, 1 - c), device_id_type=MESH)
            cp.start()
            cps.append(cp)
        for t in range(n):
            slot = o_refs[t].at[1 - c]
            pltpu.make_async_remote_copy(src_ref=slot, dst_ref=slot, send_sem=send_sems.at[t], recv_sem=recv_sems.at[t],
                                         device_id=(x, y, 1 - c), device_id_type=MESH).wait_recv()
        for cp in cps:
            cp.wait_send()

    return pl.pallas_call(
        body, name="sibling_share_layer",
        in_specs=[ANY] * n, out_specs=[ANY] * n,
        out_shape=[jax.ShapeDtypeStruct(b.shape, b.dtype) for b in bufs],
        input_output_aliases={t: t for t in range(n)},
        scratch_shapes=[pltpu.SemaphoreType.DMA((n,)), pltpu.SemaphoreType.DMA((n,))],
    )(*bufs)


SP_META = 2 * (N_META * D_MODEL // LANES)
SP_NORM = DEPTH * D_MODEL // LANES
SP_RB = DEPTH * N_BUCKETS
SP_SINK = DEPTH * ATT_HEADS
SP_CONV = DEPTH * 3 * BRANCH_WIDTH // LANES
SP_LOSS = 8
SP_ROWS = SP_META + 2 * SP_NORM + SP_RB + SP_SINK + SP_CONV + SP_LOSS


def sum_small(slots):
    half = SP_META // 2
    rb0 = SP_META + 2 * SP_NORM
    rest_rows = SP_ROWS - SP_META

    def body(s_ref, meta_ref, rest_ref):
        acc = s_ref[0]
        for d in range(1, 8):
            acc = acc + s_ref[d]
        meta_ref[...] = acc[0:half] + acc[half:SP_META]
        rest_ref[...] = acc[SP_META:]
        rest_ref[rb0 - SP_META:rb0 - SP_META + N_BUCKETS, :] = (
            acc[rb0:rb0 + N_BUCKETS] + acc[rb0 + N_BUCKETS:rb0 + 2 * N_BUCKETS])

    vm = pl.BlockSpec(memory_space=pltpu.VMEM)
    return pl.pallas_call(
        body, name="sum_small",
        in_specs=[vm], out_specs=[vm, vm],
        out_shape=[jax.ShapeDtypeStruct((half, LANES), F32), jax.ShapeDtypeStruct((rest_rows, LANES), F32)],
    )(slots)


def local_step(x, loss_target, meta_full, rel_bias, norm_pre, conv_w_full, attn_sinks, norm_post, weights_of, mid_fwd,
               grads_done, bwd_done):
    nb, seq, _ = x.shape
    nc = seq // BLOCK + 1
    lp = nc * BLOCK
    rows = nb * lp
    pad = jnp.zeros((nb, PAD_FRONT, D_MODEL), F32)
    meta = jnp.broadcast_to(meta_full[None], (nb, N_META, D_MODEL))
    h0 = jnp.concatenate([pad, meta, x], axis=1).reshape(rows, D_MODEL)
    cosf, sinf = _rot_tables(lp)
    bkt = jnp.asarray(_bucket_table())

    acts = []
    h = h0
    for l in range(DEPTH):
        (w_in, w_br, w_out), zero = weights_of(l, h)
        hb, p_abc = norm_matmul(h, norm_pre[l][None] + zero, w_in, 0, N_ABC_TILES)
        p_m = matmul_cols(hb, w_in, N_ABC_TILES, N_M_TILES)
        br, states = mixers_fwd(p_abc, cosf, sinf, bkt, rel_bias, attn_sinks[l][None], conv_w_full[l], nb, nc)
        h_next = merge_fwd(h, br, p_m, w_br, w_out, norm_post[l][None] + mid_fwd(l, br))
        acts.append((h, hb, p_abc, p_m, br, states, w_in, w_br, w_out))
        h = h_next

    loss_part, d_h = loss_head(h, loss_target.reshape(nb * seq, D_MODEL), nb, nc)

    small = [None] * DEPTH
    zero_m = jnp.zeros((1, 1), F32)
    for l in reversed(range(DEPTH)):
        h_in, hb, p_abc, p_m, br, states, w_in, w_br, w_out = acts[l]
        d_br, d_m, d_gpost, g_wbr, g_wout = merge_bwd(d_h, br, p_m, w_br, w_out, norm_post[l][None] + zero_m)
        d_abc, d_rb, d_sk, d_cw = mixers_bwd(p_abc, d_br, states, cosf, sinf, bkt, rel_bias,
                                             attn_sinks[l][None], conv_w_full[l], nb, nc)
        g_win = proj_wgrad(hb, d_abc, d_m)
        zero = grads_done(l, [g_win, g_wbr, g_wout])
        d_h, d_gpre = proj_dgrad(d_abc, d_m, w_in, h_in, norm_pre[l][None] + zero, d_h)
        zero_m = bwd_done(l, d_h)
        small[l] = (d_gpre[0], d_gpost[0], d_rb, d_sk, d_cw[0:3])

    d_h3 = d_h.reshape(nb, lp, D_MODEL)
    d_x = d_h3[:, BLOCK:]
    d_meta = d_h3[:, PAD_FRONT:BLOCK]
    sp = jnp.concatenate([
        d_meta.reshape(-1, LANES),
        jnp.stack([small[l][0] for l in range(DEPTH)]).reshape(-1, LANES),
        jnp.stack([small[l][1] for l in range(DEPTH)]).reshape(-1, LANES),
        jnp.concatenate([small[l][2] for l in range(DEPTH)], axis=0),
        jnp.concatenate([small[l][3] for l in range(DEPTH)], axis=0),
        jnp.stack([small[l][4] for l in range(DEPTH)]).reshape(-1, LANES),
        loss_part], axis=0)
    return d_x, sp


def kernel(x, meta_tokens, rel_bias, norm_pre, w_in, conv_w, attn_sinks, w_branch, w_out, norm_post, loss_target, m_meta_tokens, m_rel_bias, m_norm_pre, m_w_in, m_conv_w, m_attn_sinks, m_w_branch, m_w_out, m_norm_post, v_meta_tokens, v_rel_bias, v_norm_pre, v_w_in, v_conv_w, v_attn_sinks, v_w_branch, v_w_out, v_norm_post):
    assert x.shape[0] == 2 and SP_META == 2 * N_META * D_MODEL // LANES
    px, py, pc = _place()
    chip = 2 * px + py

    c_arr = jnp.reshape(pc, (1,)).astype(jnp.int32)
    where = jnp.stack([chip, pc]).astype(jnp.int32)
    tr_ = lambda a: jnp.swapaxes(a, 1, 2)
    w3 = [tr_(w_in), w_branch.reshape(DEPTH, N_BRANCH * BRANCH_WIDTH, SHARD_D), w_out]
    halves = lambda a: a.reshape(2, a.shape[0] // 2, a.shape[1])

    def as_weights(bufs):
        a_in, a_br, a_out = bufs
        return (a_in.reshape(PROJ_WIDTH, D_MODEL), a_br.reshape(N_CHIPS, N_BRANCH, BRANCH_WIDTH, SHARD_D),
                a_out.reshape(D_MODEL, D_MODEL))

    side = jnp.concatenate([meta_tokens.reshape(-1), conv_w.reshape(-1)]).reshape(-1, LANES)
    side = jnp.concatenate([side, jnp.zeros((40 - side.shape[0], LANES), F32)], axis=0)
    side_all = exchange_small(side, side)
    side_chips = side_all[0::2]
    n_meta_rows = N_META * SHARD_D // LANES
    meta_full = jnp.moveaxis(side_chips[:, :n_meta_rows].reshape(N_CHIPS, N_META, SHARD_D), 0, 1).reshape(N_META, D_MODEL)
    conv_full = jnp.moveaxis(side_chips[:, n_meta_rows:n_meta_rows + 6].reshape(N_CHIPS, DEPTH, 3, LANES), 0, 2).reshape(DEPTH, 3, BRANCH_WIDTH)

    slots = [[_own_slot(halves(w[l].astype(BF16)), chip) for w in w3] for l in range(DEPTH)]
    send0, recv0, flying0, started0 = gather_start(slots[0], side_all)
    meta_full = meta_full + started0[0:1, 0:1]
    inbound = {}

    def weights_of(l, h):
        if l == 0:
            gathered0 = gather_forward(gather_wait(flying0, send0, recv0, h))
            inbound[1] = gather_start(slots[1], gathered0[0])
            return as_weights(gathered0), inbound[1][3][0:1, 0:1]
        send, recv, thru = inbound[1]
        return as_weights(forward_wait(thru, send, recv, h)), jnp.zeros((1, 1), F32)

    def mid_fwd(l, br):
        if l == 0:
            send, recv, flying1, _ = inbound[1]
            send, recv, thru, started = forward_start(gather_wait(flying1, send, recv, br))
            inbound[1] = (send, recv, thru)
            return started[0:1, 0:1]
        return jnp.zeros((1, 1), F32)

    reduced = [None] * DEPTH
    flying = {}

    def finish_reduce(l, after):
        partials, parts = scatter_wait(*flying[l], after)
        reduced[l] = sibling_share_layer([sum_chips(a, p, where) for a, p in zip(partials, parts)])

    def start_scatter(l, full, others):
        send, recv, thru, lands, started = scatter_start([add_own_half(g, o, c_arr) for g, o in zip(full, others)])
        flying[l] = (thru, lands, send, recv)
        return started[0:1, 0:1]

    m3 = [tr_(m_w_in), m_w_branch.reshape(w3[1].shape), m_w_out]
    v3 = [tr_(v_w_in), v_w_branch.reshape(w3[1].shape), v_w_out]
    big = [None] * 3

    def adamw_of(l, after):
        for t in range(3):
            big[t] = adamw_layer(w3[t], reduced[l][t].reshape(w3[t].shape[1:]), m3[t], v3[t], l, big[t], after)

    def grads_done(l, grads):
        full = [g.reshape(N_CHIPS, 2, g.size // (2 * N_CHIPS * g.shape[-1]), g.shape[-1]) for g in grads]
        if l == 0:
            finish_reduce(1, grads[0])
        send, recv, thru, lands, started = swap_start(full, grads[0] if l == 1 else reduced[1][0])
        if l == 1:
            flying["swap"] = (thru, lands, send, recv)
            return started[0:1, 0:1]
        adamw_of(1, started)
        return start_scatter(0, *swap_wait(thru, lands, send, recv, big[0][1]))

    def bwd_done(l, d_h):
        if l == 1:
            return start_scatter(1, *swap_wait(*flying["swap"], d_h))
        return jnp.zeros((1, 1), F32)

    d_x, sp = local_step(x, loss_target, meta_full, rel_bias, norm_pre, conv_full, attn_sinks, norm_post,
                         weights_of, mid_fwd, grads_done, bwd_done)
    finish_reduce(0, sp)

    s_send, s_recv, s_buf, s_started = small_start(sp, 4 * px + 2 * py + pc, reduced[0][0])

    adamw_of(0, s_started)
    g_in, *u_in = [tr_(a) for a in big[0]]
    g_br, *u_br = [a.reshape(w_branch.shape) for a in big[1]]
    g_out, *u_out = big[2]

    meta_rows, rest = sum_small(small_wait(s_buf, s_send, s_recv, big[0][1]))
    o = 0
    g_meta_full = meta_rows.reshape(N_META, D_MODEL)
    g_norm_pre = rest[o:o + SP_NORM].reshape(DEPTH, D_MODEL); o += SP_NORM
    g_norm_post = rest[o:o + SP_NORM].reshape(DEPTH, D_MODEL); o += SP_NORM
    g_rel_bias = rest[o:o + N_BUCKETS, :ATT_HEADS]; o += SP_RB
    g_sinks = rest[o:o + SP_SINK, 0].reshape(DEPTH, ATT_HEADS); o += SP_SINK
    g_conv_full = rest[o:o + SP_CONV].reshape(DEPTH, 3, BRANCH_WIDTH); o += SP_CONV
    loss = rest[o, 0]
    g_meta = lax.dynamic_slice_in_dim(g_meta_full, chip * SHARD_D, SHARD_D, axis=1)
    g_conv = lax.dynamic_slice_in_dim(g_conv_full, chip * LANES, LANES, axis=2)

    to2 = lambda a: a.reshape(-1, a.shape[-1])
    smalls = [(meta_tokens, g_meta, m_meta_tokens, v_meta_tokens),
              (rel_bias, g_rel_bias, m_rel_bias, v_rel_bias),
              (norm_pre, g_norm_pre, m_norm_pre, v_norm_pre),
              (to2(conv_w), to2(g_conv), to2(m_conv_w), to2(v_conv_w)),
              (attn_sinks, g_sinks, m_attn_sinks, v_attn_sinks),
              (norm_post, g_norm_post, m_norm_post, v_norm_post)]
    u_meta, u_rb, u_npre, u_conv, u_sink, u_npost = adamw_small(smalls)
    u_conv = tuple(a.reshape(conv_w.shape) for a in u_conv)

    grads = [g_meta, g_rel_bias, g_norm_pre, g_in, g_conv, g_sinks, g_br, g_out, g_norm_post]
    upd = [u_meta, u_rb, u_npre, u_in, u_conv, u_sink, u_br, u_out, u_npost]
    return (loss, d_x, *grads, *[u[0] for u in upd], *[u[1] for u in upd], *[u[2] for u in upd])
```

```python
import functools
import math

import numpy as np
import jax
import jax.numpy as jnp
from jax import lax
from jax.experimental import pallas as pl
from jax.experimental.pallas import tpu as pltpu

F32 = jnp.float32
BF16 = jnp.bfloat16
MESH = pl.DeviceIdType.MESH

D_MODEL = 1024
DEPTH = 2
N_META = 16
BLOCK = 128
PAD_FRONT = BLOCK - N_META
ATT_HEADS = 8
ATT_HEAD_DIM = 64
N_BUCKETS = 32
MAX_EXACT = 16
MAX_DISTANCE = 128
RET_HEADS = 4
ROT_BASE = 10000.0
N_BRANCH = 3
BRANCH_WIDTH = 512
PROJ_WIDTH = 8448
ABC_WIDTH = 5376
MERGE_WIDTH = N_BRANCH * D_MODEL
RMS_EPS = 1e-6
GN_EPS = 1e-6
NEG_INF = -1e30
ATT_SCALE = ATT_HEAD_DIM ** -0.5
RET_SCALE = BLOCK ** -0.5
LOG_GAMMA = tuple(math.log1p(-(2.0 ** (-5.0 - h))) for h in range(RET_HEADS))

C_AQ, C_AK, C_AV, C_AG = 0, 512, 640, 768
C_RQ, C_RK, C_RV, C_RG = 1280, 1792, 2304, 2816
C_CB, C_CC, C_CX, C_CG = 3328, 3840, 4352, 4864

ADAM_LR = 0.001
ADAM_B1 = 0.9
ADAM_B2 = 0.999
ADAM_EPS = 1e-08
ADAM_WD = 0.01
ADAM_STEP = 10

N_CHIPS = 4
SHARD_IN = PROJ_WIDTH // N_CHIPS
SHARD_D = D_MODEL // N_CHIPS
LANES = 128
PACK_IN = D_MODEL * SHARD_IN
PACK_BR = N_BRANCH * BRANCH_WIDTH * SHARD_D
PACK_OUT = SHARD_D * D_MODEL
PACK_ROWS = (PACK_IN + PACK_BR + PACK_OUT) // LANES

VMEM_LIMIT = 56 * 1024 * 1024
COL_TILE = 768
ROW_TILE = 1088
PROJ_ROW_TILE = 2176


def _cparams(*sem):
    return pltpu.CompilerParams(dimension_semantics=sem, vmem_limit_bytes=VMEM_LIMIT)


def _nt(a, b):
    return lax.dot_general(a, b, (((1,), (1,)), ((), ())), preferred_element_type=F32)


def _tn(a, b):
    return lax.dot_general(a, b, (((0,), (0,)), ((), ())), preferred_element_type=F32)


def _nn(a, b):
    return jnp.dot(a, b, preferred_element_type=F32)


def _sigmoid(x):
    return 0.5 * jnp.tanh(0.5 * x) + 0.5


def _silu(x):
    return x * _sigmoid(x)


def _dsilu(x):
    s = _sigmoid(x)
    return s * (1.0 + x * (1.0 - s))


def _bucket_table():
    r = np.arange(BLOCK)[:, None]
    c = np.arange(2 * BLOCK)[None, :]
    n = np.maximum(BLOCK + r - c, 0)
    nf = np.maximum(n, 1).astype(np.float32)
    large = MAX_EXACT + (np.log(nf / MAX_EXACT) / math.log(MAX_DISTANCE / MAX_EXACT)
                         * (N_BUCKETS - MAX_EXACT)).astype(np.int32)
    large = np.minimum(large, N_BUCKETS - 1)
    return np.where(n < MAX_EXACT, n, large).astype(np.int32)


def _rot_tables(lp):
    half = BLOCK // 2
    pos = (jnp.arange(lp) - PAD_FRONT).astype(F32)
    theta = 1.0 / (ROT_BASE ** jnp.linspace(0.0, 1.0, half, dtype=F32))
    ang = pos[:, None] * theta[None, :]
    cos, sin = jnp.cos(ang), jnp.sin(ang)
    return jnp.concatenate([cos, cos], axis=1), jnp.concatenate([-sin, sin], axis=1)


def norm_matmul(x2d, g, w, col0_blocks, n_col_blocks):
    t = x2d.shape[0]
    tm = PROJ_ROW_TILE if t % PROJ_ROW_TILE == 0 else BLOCK

    def body(x_ref, g_ref, w_ref, hb_ref, o_ref):
        @pl.when(pl.program_id(1) == 0)
        def _():
            x = x_ref[...]
            r = lax.rsqrt(jnp.mean(x * x, axis=-1, keepdims=True) + RMS_EPS)
            hb_ref[...] = (x * r * g_ref[...]).astype(BF16)

        o_ref[...] = _nt(hb_ref[...], w_ref[...]).astype(BF16)

    return pl.pallas_call(
        body, name="norm_matmul",
        grid=(t // tm, n_col_blocks),
        in_specs=[pl.BlockSpec((tm, D_MODEL), lambda i, j: (i, 0)),
                  pl.BlockSpec((1, D_MODEL), lambda i, j: (0, 0)),
                  pl.BlockSpec((COL_TILE, D_MODEL), lambda i, j: (j + col0_blocks, 0))],
        out_specs=[pl.BlockSpec((tm, D_MODEL), lambda i, j: (i, 0)),
                   pl.BlockSpec((tm, COL_TILE), lambda i, j: (i, j))],
        out_shape=[jax.ShapeDtypeStruct((t, D_MODEL), BF16),
                   jax.ShapeDtypeStruct((t, n_col_blocks * COL_TILE), BF16)],
        compiler_params=_cparams("parallel", "arbitrary"),
    )(x2d, g, w)


def matmul_cols(a, w, col0_blocks, n_col_blocks):
    t, k = a.shape
    tm = PROJ_ROW_TILE if t % PROJ_ROW_TILE == 0 else BLOCK

    def body(a_ref, w_ref, o_ref):
        o_ref[...] = _nt(a_ref[...], w_ref[...]).astype(BF16)

    return pl.pallas_call(
        body, name="matmul_cols",
        grid=(t // tm, n_col_blocks),
        in_specs=[pl.BlockSpec((tm, k), lambda i, j: (i, 0)),
                  pl.BlockSpec((COL_TILE, k), lambda i, j: (j + col0_blocks, 0))],
        out_specs=pl.BlockSpec((tm, COL_TILE), lambda i, j: (i, j)),
        out_shape=jax.ShapeDtypeStruct((t, n_col_blocks * COL_TILE), BF16),
        compiler_params=_cparams("parallel", "arbitrary"),
    )(a, w)


class _Widened:
    def __init__(self, ref):
        self.ref = ref

    def __getitem__(self, idx):
        return self.ref[idx].astype(F32)


def _build_bias(bkt_ref, rb_ref, bias_s):
    bkt = bkt_ref[...]
    for h in range(ATT_HEADS):
        acc = jnp.zeros((BLOCK, 2 * BLOCK), F32)
        for b in range(N_BUCKETS):
            acc = jnp.where(bkt == b, rb_ref[b, h], acc)
        bias_s[h] = acc


def _band_mask(n):
    r = lax.broadcasted_iota(jnp.int32, (BLOCK, 2 * BLOCK), 0)
    c = lax.broadcasted_iota(jnp.int32, (BLOCK, 2 * BLOCK), 1)
    key_pos = (n - 1) * BLOCK + c
    return (c > r) & (c <= r + BLOCK) & (key_pos >= PAD_FRONT)


def _split_heads(kv, kh):
    lane = lax.broadcasted_iota(jnp.int32, kv.shape, 1)
    if kh == 0:
        lo = jnp.where(lane < ATT_HEAD_DIM, kv, 0.0)
        hi = pltpu.roll(lo, ATT_HEAD_DIM, 1)
    else:
        hi = jnp.where(lane >= ATT_HEAD_DIM, kv, 0.0)
        lo = pltpu.roll(hi, ATT_HEAD_DIM, 1)
    return lo, hi


def _merge_heads(acc_lo, acc_hi, kh):
    lane = lax.broadcasted_iota(jnp.int32, acc_lo.shape, 1)
    if kh == 0:
        return jnp.where(lane < ATT_HEAD_DIM, acc_lo + pltpu.roll(acc_hi, ATT_HEAD_DIM, 1), 0.0)
    return jnp.where(lane >= ATT_HEAD_DIM, acc_hi + pltpu.roll(acc_lo, ATT_HEAD_DIM, 1), 0.0)


def _softmax_sink(q2b, kxb, bias_h, mask, sink_h):
    return _softmax_of(_nt(q2b, kxb), bias_h, mask, sink_h)


def _softmax_of(qk, bias_h, mask, sink_h):
    s = qk * ATT_SCALE + bias_h
    s = jnp.where(mask, s, NEG_INF)
    m = jnp.maximum(jnp.max(s, axis=-1, keepdims=True), sink_h)
    p = jnp.exp(s - m)
    es = jnp.exp(sink_h - m)
    inv = 1.0 / (jnp.sum(p, axis=-1, keepdims=True) + es)
    return p * inv, es * inv


def _rot(t, cosf, sinf):
    return t * cosf + pltpu.roll(t, BLOCK // 2, 1) * sinf


def _rot_t(d, cosf, sinf):
    return d * cosf + pltpu.roll(d * sinf, BLOCK // 2, 1)


def _decay_tables(h):
    lg = LOG_GAMMA[h]
    i = lax.broadcasted_iota(jnp.int32, (BLOCK, BLOCK), 0)
    j = lax.broadcasted_iota(jnp.int32, (BLOCK, BLOCK), 1)
    diff = (i - j).astype(F32)
    dm = jnp.where(diff >= 0, jnp.exp(diff * lg), 0.0)
    row = lax.broadcasted_iota(jnp.int32, (BLOCK, 1), 0).astype(F32)
    zeta = jnp.exp((BLOCK - 1 - row) * lg)
    xi = jnp.exp((row + 1.0) * lg)
    return dm, zeta, xi, math.exp(BLOCK * lg)


def _valid_col(n):
    row = lax.broadcasted_iota(jnp.int32, (BLOCK, 1), 0)
    return ((n * BLOCK + row) >= PAD_FRONT).astype(F32)


def _shift_down(cur, prev, k):
    row = lax.broadcasted_iota(jnp.int32, cur.shape, 0)
    return jnp.where(row >= k, pltpu.roll(cur, k, 0), pltpu.roll(prev, k, 0))


def _shift_up(cur, nxt, k):
    row = lax.broadcasted_iota(jnp.int32, cur.shape, 0)
    return jnp.where(row < BLOCK - k, pltpu.roll(cur, BLOCK - k, 0), pltpu.roll(nxt, BLOCK - k, 0))


def mixers_fwd(proj, cosf, sinf, bkt, rel_bias, sinks, conv_w, nb, nc):
    def body(p_ref, cos_ref, sin_ref, bkt_ref, rb_ref, sk_ref, cw_ref, br_ref, st_ref,
             bias_s, kv_s, state_s, u_s):
        p_ref = _Widened(p_ref)
        n = pl.program_id(0)

        @pl.when(n == 0)
        def _():
            _build_bias(bkt_ref, rb_ref, bias_s)
            kv_s[:, 0:BLOCK, :] = jnp.zeros((nb, BLOCK, 2 * BLOCK), F32)
            state_s[...] = jnp.zeros_like(state_s)
            u_s[...] = jnp.zeros_like(u_s)

        valid = _valid_col(n)
        mask = _band_mask(n)
        ex = range(nb)

        for b in ex:
            kv_s[b, BLOCK:2 * BLOCK, :] = p_ref[b, :, C_AK:C_AK + 2 * BLOCK]
        for kh in range(2):
            ks = [[t.astype(BF16) for t in _split_heads(kv_s[b, :, 0:BLOCK], kh)] for b in ex]
            vs = [[t.astype(BF16) for t in _split_heads(kv_s[b, :, BLOCK:2 * BLOCK], kh)] for b in ex]
            pairs = [(b, 2 * kh + jj) for jj in range(2) for b in ex]
            subs = [(b, j, x) for (b, j) in pairs for x in range(2)]
            qb_ = {(b, j): p_ref[b, :, C_AQ + BLOCK * j:C_AQ + BLOCK * (j + 1)].astype(BF16) for (b, j) in pairs}
            qk_ = {(b, j, x): _nt(qb_[(b, j)], ks[b][x]) for (b, j, x) in subs}
            pb_ = {}
            for u in subs:
                h = 2 * u[1] + u[2]
                pb_[u] = _softmax_of(qk_[u], bias_s[h], mask, sk_ref[0, h])[0].astype(BF16)
            o_ = {u: _nn(pb_[u], vs[u[0]][u[2]]) for u in subs}
            for (b, j) in pairs:
                gate = p_ref[b, :, C_AG + BLOCK * j:C_AG + BLOCK * (j + 1)]
                br_ref[b, :, BLOCK * j:BLOCK * (j + 1)] = ((o_[(b, j, 0)] + o_[(b, j, 1)]) * _silu(gate)).astype(BF16)
        for b in ex:
            kv_s[b, 0:BLOCK, :] = kv_s[b, BLOCK:2 * BLOCK, :]

        cosv = cos_ref[...]
        sinv = sin_ref[...]
        tabs = [_decay_tables(h) for h in range(RET_HEADS)]
        units = [(b, h) for h in range(RET_HEADS) for b in ex]
        sl = lambda c0, h: slice(c0 + BLOCK * h, c0 + BLOCK * (h + 1))
        q_, k_, v_, sp_ = {}, {}, {}, {}
        for u in units:
            b, h = u
            q_[u] = _rot(p_ref[b, :, sl(C_RQ, h)], cosv, sinv).astype(BF16)
            k_[u] = (_rot(p_ref[b, :, sl(C_RK, h)], cosv, sinv) * RET_SCALE * valid).astype(BF16)
            v_[u] = p_ref[b, :, sl(C_RV, h)]
            sp_[u] = state_s[b, h]
            st_ref[b, 0, h] = sp_[u]
        qk_ = {u: _nt(q_[u], k_[u]) for u in units}
        qs_ = {u: _nn(q_[u], sp_[u].astype(BF16)) for u in units}
        kv_ = {u: _tn(k_[u], (v_[u] * tabs[u[1]][1]).astype(BF16)) for u in units}
        a_ = {u: (qk_[u] * tabs[u[1]][0]).astype(BF16) for u in units}
        av_ = {u: _nn(a_[u], v_[u].astype(BF16)) for u in units}
        for u in units:
            b, h = u
            o = av_[u] + tabs[h][2] * qs_[u]
            mu = jnp.mean(o, axis=-1, keepdims=True)
            var = jnp.mean(jnp.square(o - mu), axis=-1, keepdims=True)
            oh = (o - mu) * lax.rsqrt(var + GN_EPS)
            gate = p_ref[b, :, sl(C_RG, h)]
            br_ref[b, :, BRANCH_WIDTH + BLOCK * h:BRANCH_WIDTH + BLOCK * (h + 1)] = (oh * _silu(gate)).astype(BF16)
            state_s[b, h] = tabs[h][3] * sp_[u] + kv_[u]

        for b in ex:
            u = p_ref[b, :, C_CC:C_CC + BRANCH_WIDTH] * p_ref[b, :, C_CX:C_CX + BRANCH_WIDTH] * valid
            u_prev = u_s[b]
            y = (cw_ref[0:1, :] * _shift_down(u, u_prev, 2) + cw_ref[1:2, :] * _shift_down(u, u_prev, 1)
                 + cw_ref[2:3, :] * u)
            yc = p_ref[b, :, C_CB:C_CB + BRANCH_WIDTH] * y * _silu(p_ref[b, :, C_CG:C_CG + BRANCH_WIDTH])
            br_ref[b, :, 2 * BRANCH_WIDTH:3 * BRANCH_WIDTH] = yc.astype(BF16)
            u_s[b] = u

    lp = nc * BLOCK
    smem = pl.BlockSpec(memory_space=pltpu.SMEM)
    br, states = pl.pallas_call(
        body, name="mixers_fwd",
        grid=(nc,),
        in_specs=[pl.BlockSpec((nb, BLOCK, ABC_WIDTH), lambda n: (0, n, 0)),
                  pl.BlockSpec((BLOCK, BLOCK), lambda n: (n, 0)),
                  pl.BlockSpec((BLOCK, BLOCK), lambda n: (n, 0)),
                  pl.BlockSpec((BLOCK, 2 * BLOCK), lambda n: (0, 0)),
                  smem, smem,
                  pl.BlockSpec((3, BRANCH_WIDTH), lambda n: (0, 0))],
        out_specs=[pl.BlockSpec((nb, BLOCK, N_BRANCH * BRANCH_WIDTH), lambda n: (0, n, 0)),
                   pl.BlockSpec((nb, 1, RET_HEADS, BLOCK, BLOCK), lambda n: (0, n, 0, 0, 0))],
        out_shape=[jax.ShapeDtypeStruct((nb, lp, N_BRANCH * BRANCH_WIDTH), BF16),
                   jax.ShapeDtypeStruct((nb, nc, RET_HEADS, BLOCK, BLOCK), F32)],
        scratch_shapes=[pltpu.VMEM((ATT_HEADS, BLOCK, 2 * BLOCK), F32),
                        pltpu.VMEM((nb, 2 * BLOCK, 2 * BLOCK), F32),
                        pltpu.VMEM((nb, RET_HEADS, BLOCK, BLOCK), F32),
                        pltpu.VMEM((nb, BLOCK, BRANCH_WIDTH), F32)],
        compiler_params=_cparams("arbitrary"),
    )(proj.reshape(nb, lp, ABC_WIDTH), cosf, sinf, bkt, rel_bias, sinks, conv_w)
    return br.reshape(nb * lp, N_BRANCH * BRANCH_WIDTH), states


def mixers_bwd(proj, d_br, states, cosf, sinf, bkt, rel_bias, sinks, conv_w, nb, nc):
    def body(p_ref, kvp_ref, cp_ref, dbr_ref, st_ref, cos_ref, sin_ref, bkt_ref, rb_ref, sk_ref, cw_ref,
             dp_ref, drb_ref, dsk_ref, dcw_ref,
             bias_s, dbias_s, dkv_s, g_s, dy_s):
        p_ref, kvp_ref, cp_ref, dbr_ref = [_Widened(r) for r in (p_ref, kvp_ref, cp_ref, dbr_ref)]
        step = pl.program_id(0)
        n = nc - 1 - step
        ex = range(nb)

        @pl.when(step == 0)
        def _():
            _build_bias(bkt_ref, rb_ref, bias_s)
            dbias_s[...] = jnp.zeros_like(dbias_s)
            dsk_ref[...] = jnp.zeros_like(dsk_ref)
            dcw_ref[...] = jnp.zeros_like(dcw_ref)
            drb_ref[...] = jnp.zeros_like(drb_ref)
            dkv_s[...] = jnp.zeros_like(dkv_s)
            g_s[...] = jnp.zeros_like(g_s)
            dy_s[...] = jnp.zeros_like(dy_s)

        valid = _valid_col(n)
        mask = _band_mask(n)
        has_prev = (n > 0).astype(F32)

        k_all, v_all = [], []
        for b in ex:
            kv_prev = kvp_ref[b] * has_prev
            kv_cur = p_ref[b, :, C_AK:C_AK + 2 * BLOCK]
            k_all.append(jnp.concatenate([kv_prev[:, 0:BLOCK], kv_cur[:, 0:BLOCK]], axis=0))
            v_all.append(jnp.concatenate([kv_prev[:, BLOCK:], kv_cur[:, BLOCK:]], axis=0))
        zero2 = jnp.zeros((2 * BLOCK, BLOCK), F32)
        dk_tot = [zero2 for _ in ex]
        dv_tot = [zero2 for _ in ex]
        for kh in range(2):
            ks = [[t.astype(BF16) for t in _split_heads(k_all[b], kh)] for b in ex]
            vs = [[t.astype(BF16) for t in _split_heads(v_all[b], kh)] for b in ex]
            pairs = [(b, 2 * kh + jj) for jj in range(2) for b in ex]
            subs = [(b, j, x) for (b, j) in pairs for x in range(2)]
            qb_, gate_, dya_, do2_ = {}, {}, {}, {}
            for w in pairs:
                b, j = w
                qb_[w] = p_ref[b, :, C_AQ + BLOCK * j:C_AQ + BLOCK * (j + 1)].astype(BF16)
                gate_[w] = p_ref[b, :, C_AG + BLOCK * j:C_AG + BLOCK * (j + 1)]
                dya_[w] = dbr_ref[b, :, BLOCK * j:BLOCK * (j + 1)]
                do2_[w] = (dya_[w] * _silu(gate_[w])).astype(BF16)
            qk_ = {(b, j, x): _nt(qb_[(b, j)], ks[b][x]) for (b, j, x) in subs}
            dpm_ = {(b, j, x): _nt(do2_[(b, j)], vs[b][x]) for (b, j, x) in subs}
            pb_, dsb_ = {}, {}
            for u in subs:
                b, j, x = u
                h = 2 * j + x
                p, p_sink = _softmax_of(qk_[u], bias_s[h], mask, sk_ref[0, h])
                pb_[u] = p.astype(BF16)
                delta = jnp.sum(p * dpm_[u], axis=-1, keepdims=True)
                ds = p * (dpm_[u] - delta)
                dbias_s[h] += ds
                dsk_ref[h:h + 1, :] += jnp.broadcast_to(
                    jnp.sum(-p_sink * delta, axis=0, keepdims=True), (1, BLOCK))
                dsb_[u] = ds.astype(BF16)
            o_ = {u: _nn(pb_[u], vs[u[0]][u[2]]) for u in subs}
            dq_ = {u: _nn(dsb_[u], ks[u[0]][u[2]]) for u in subs}
            dkm_ = {u: _tn(dsb_[u], qb_[(u[0], u[1])]) for u in subs}
            dvm_ = {u: _tn(pb_[u], do2_[(u[0], u[1])]) for u in subs}
            for w in pairs:
                b, j = w
                o2 = o_[(b, j, 0)] + o_[(b, j, 1)]
                dq2 = (dq_[(b, j, 0)] + dq_[(b, j, 1)]) * ATT_SCALE
                dp_ref[b, :, C_AQ + BLOCK * j:C_AQ + BLOCK * (j + 1)] = dq2.astype(BF16)
                dp_ref[b, :, C_AG + BLOCK * j:C_AG + BLOCK * (j + 1)] = (
                    dya_[w] * o2 * _dsilu(gate_[w])).astype(BF16)
            for b in ex:
                j0, j1 = 2 * kh, 2 * kh + 1
                dk_lo = (dkm_[(b, j0, 0)] + dkm_[(b, j1, 0)]) * ATT_SCALE
                dk_hi = (dkm_[(b, j0, 1)] + dkm_[(b, j1, 1)]) * ATT_SCALE
                dk_tot[b] = dk_tot[b] + _merge_heads(dk_lo, dk_hi, kh)
                dv_tot[b] = dv_tot[b] + _merge_heads(dvm_[(b, j0, 0)] + dvm_[(b, j1, 0)],
                                                     dvm_[(b, j0, 1)] + dvm_[(b, j1, 1)], kh)
        for b in ex:
            dp_ref[b, :, C_AK:C_AK + BLOCK] = (dk_tot[b][BLOCK:, :] + dkv_s[b, :, 0:BLOCK]).astype(BF16)
            dp_ref[b, :, C_AV:C_AV + BLOCK] = (dv_tot[b][BLOCK:, :] + dkv_s[b, :, BLOCK:]).astype(BF16)
            dkv_s[b, :, 0:BLOCK] = dk_tot[b][0:BLOCK, :]
            dkv_s[b, :, BLOCK:] = dv_tot[b][0:BLOCK, :]

        cosv = cos_ref[...]
        sinv = sin_ref[...]
        tabs = [_decay_tables(h) for h in range(RET_HEADS)]
        units = [(b, h) for h in range(RET_HEADS) for b in ex]
        sl = lambda c0, h: slice(c0 + BLOCK * h, c0 + BLOCK * (h + 1))
        q_, k_, v_, vb_, sp_ = {}, {}, {}, {}, {}
        for u in units:
            b, h = u
            q_[u] = _rot(p_ref[b, :, sl(C_RQ, h)], cosv, sinv).astype(BF16)
            k_[u] = (_rot(p_ref[b, :, sl(C_RK, h)], cosv, sinv) * RET_SCALE * valid).astype(BF16)
            v_[u] = p_ref[b, :, sl(C_RV, h)]
            vb_[u] = v_[u].astype(BF16)
            sp_[u] = st_ref[b, 0, h].astype(BF16)
        qk_ = {u: _nt(q_[u], k_[u]) for u in units}
        qs_ = {u: _nn(q_[u], sp_[u]) for u in units}
        a_ = {u: (qk_[u] * tabs[u[1]][0]).astype(BF16) for u in units}
        av_ = {u: _nn(a_[u], vb_[u]) for u in units}
        dob_, dxo_ = {}, {}
        for u in units:
            b, h = u
            xi = tabs[h][2]
            o = av_[u] + xi * qs_[u]
            mu = jnp.mean(o, axis=-1, keepdims=True)
            var = jnp.mean(jnp.square(o - mu), axis=-1, keepdims=True)
            rstd = lax.rsqrt(var + GN_EPS)
            oh = (o - mu) * rstd
            gate = p_ref[b, :, sl(C_RG, h)]
            d_yr = dbr_ref[b, :, BRANCH_WIDTH + BLOCK * h:BRANCH_WIDTH + BLOCK * (h + 1)]
            dp_ref[b, :, sl(C_RG, h)] = (d_yr * oh * _dsilu(gate)).astype(BF16)
            doh = d_yr * _silu(gate)
            do = rstd * (doh - jnp.mean(doh, axis=-1, keepdims=True)
                         - oh * jnp.mean(doh * oh, axis=-1, keepdims=True))
            dob_[u] = do.astype(BF16)
            dxo_[u] = (do * xi).astype(BF16)
        dov_ = {u: _nt(dob_[u], vb_[u]) for u in units}
        dv1_ = {u: _tn(a_[u], dob_[u]) for u in units}
        dq1_ = {u: _nt(dxo_[u], sp_[u]) for u in units}
        gq_ = {u: _tn(q_[u], dxo_[u]) for u in units}
        da_, gb_, zv_ = {}, {}, {}
        for u in units:
            b, h = u
            da_[u] = (dov_[u] * tabs[h][0]).astype(BF16)
            g_next = g_s[b, h]
            gb_[u] = g_next.astype(BF16)
            zv_[u] = (v_[u] * tabs[h][1]).astype(BF16)
            g_s[b, h] = tabs[h][3] * g_next + gq_[u]
        dq2_ = {u: _nn(da_[u], k_[u]) for u in units}
        dk1_ = {u: _tn(da_[u], q_[u]) for u in units}
        dk2_ = {u: _nt(zv_[u], gb_[u]) for u in units}
        dv2_ = {u: _nn(k_[u], gb_[u]) for u in units}
        for u in units:
            b, h = u
            dp_ref[b, :, sl(C_RQ, h)] = _rot_t(dq2_[u] + dq1_[u], cosv, sinv).astype(BF16)
            dp_ref[b, :, sl(C_RK, h)] = _rot_t((dk1_[u] + dk2_[u]) * (RET_SCALE * valid), cosv, sinv).astype(BF16)
            dp_ref[b, :, sl(C_RV, h)] = (dv1_[u] + tabs[h][1] * dv2_[u]).astype(BF16)

        w0, w1, w2 = cw_ref[0:1, :], cw_ref[1:2, :], cw_ref[2:3, :]
        for b in ex:
            cb = p_ref[b, :, C_CB:C_CB + BRANCH_WIDTH]
            cc = p_ref[b, :, C_CC:C_CC + BRANCH_WIDTH]
            cx = p_ref[b, :, C_CX:C_CX + BRANCH_WIDTH]
            cg = p_ref[b, :, C_CG:C_CG + BRANCH_WIDTH]
            u = cc * cx * valid
            u_prev = (cp_ref[b, :, 0:BRANCH_WIDTH] * cp_ref[b, :, BRANCH_WIDTH:2 * BRANCH_WIDTH]
                      * (_valid_col(n - 1) * has_prev))
            u1 = _shift_down(u, u_prev, 1)
            u2 = _shift_down(u, u_prev, 2)
            y = w0 * u2 + w1 * u1 + w2 * u
            d_yc = dbr_ref[b, :, 2 * BRANCH_WIDTH:3 * BRANCH_WIDTH]
            sg = _silu(cg)
            dp_ref[b, :, C_CB:C_CB + BRANCH_WIDTH] = (d_yc * y * sg).astype(BF16)
            dp_ref[b, :, C_CG:C_CG + BRANCH_WIDTH] = (d_yc * cb * y * _dsilu(cg)).astype(BF16)
            dy = d_yc * cb * sg
            dy_next = dy_s[b]
            du = (w2 * dy + w1 * _shift_up(dy, dy_next, 1) + w0 * _shift_up(dy, dy_next, 2)) * valid
            dp_ref[b, :, C_CC:C_CC + BRANCH_WIDTH] = (du * cx).astype(BF16)
            dp_ref[b, :, C_CX:C_CX + BRANCH_WIDTH] = (du * cc).astype(BF16)
            dcw_ref[0:1, :] += jnp.sum(dy * u2, axis=0, keepdims=True)
            dcw_ref[1:2, :] += jnp.sum(dy * u1, axis=0, keepdims=True)
            dcw_ref[2:3, :] += jnp.sum(dy * u, axis=0, keepdims=True)
            dy_s[b] = dy

        @pl.when(step == nc - 1)
        def _():
            bkt = bkt_ref[...]
            row = lax.broadcasted_iota(jnp.int32, (N_BUCKETS, BLOCK), 0)
            lane = lax.broadcasted_iota(jnp.int32, (N_BUCKETS, BLOCK), 1)

            def one_bucket(bk, acc):
                sel = bkt == bk
                for h in range(ATT_HEADS):
                    t = jnp.where(sel, dbias_s[h], 0.0)
                    s = jnp.sum(jnp.sum(t, axis=1, keepdims=True), axis=0, keepdims=True)
                    acc = acc + jnp.where((row == bk) & (lane == h), jnp.broadcast_to(s, acc.shape), 0.0)
                return acc

            drb_ref[...] = lax.fori_loop(0, N_BUCKETS, one_bucket, jnp.zeros((N_BUCKETS, BLOCK), F32))

    lp = nc * BLOCK
    smem = pl.BlockSpec(memory_space=pltpu.SMEM)
    blk = lambda s: nc - 1 - s
    prev = lambda s: jnp.maximum(nc - 2 - s, 0)
    proj3 = proj.reshape(nb, lp, ABC_WIDTH)
    res = pl.pallas_call(
        body, name="mixers_bwd",
        grid=(nc,),
        in_specs=[pl.BlockSpec((nb, BLOCK, ABC_WIDTH), lambda s: (0, blk(s), 0)),
                  pl.BlockSpec((nb, BLOCK, 2 * BLOCK), lambda s: (0, prev(s), C_AK // (2 * BLOCK))),
                  pl.BlockSpec((nb, BLOCK, 1280), lambda s: (0, prev(s), C_CC // 1280)),
                  pl.BlockSpec((nb, BLOCK, N_BRANCH * BRANCH_WIDTH), lambda s: (0, blk(s), 0)),
                  pl.BlockSpec((nb, 1, RET_HEADS, BLOCK, BLOCK), lambda s: (0, blk(s), 0, 0, 0)),
                  pl.BlockSpec((BLOCK, BLOCK), lambda s: (blk(s), 0)),
                  pl.BlockSpec((BLOCK, BLOCK), lambda s: (blk(s), 0)),
                  pl.BlockSpec((BLOCK, 2 * BLOCK), lambda s: (0, 0)),
                  smem, smem,
                  pl.BlockSpec((3, BRANCH_WIDTH), lambda s: (0, 0))],
        out_specs=[pl.BlockSpec((nb, BLOCK, ABC_WIDTH), lambda s: (0, blk(s), 0)),
                   pl.BlockSpec((N_BUCKETS, BLOCK), lambda s: (0, 0)),
                   pl.BlockSpec((ATT_HEADS, BLOCK), lambda s: (0, 0)),
                   pl.BlockSpec((8, BRANCH_WIDTH), lambda s: (0, 0))],
        out_shape=[jax.ShapeDtypeStruct((nb, lp, ABC_WIDTH), BF16),
                   jax.ShapeDtypeStruct((N_BUCKETS, BLOCK), F32),
                   jax.ShapeDtypeStruct((ATT_HEADS, BLOCK), F32),
                   jax.ShapeDtypeStruct((8, BRANCH_WIDTH), F32)],
        scratch_shapes=[pltpu.VMEM((ATT_HEADS, BLOCK, 2 * BLOCK), F32),
                        pltpu.VMEM((ATT_HEADS, BLOCK, 2 * BLOCK), F32),
                        pltpu.VMEM((nb, BLOCK, 2 * BLOCK), F32),
                        pltpu.VMEM((nb, RET_HEADS, BLOCK, BLOCK), F32),
                        pltpu.VMEM((nb, BLOCK, BRANCH_WIDTH), F32)],
        compiler_params=_cparams("arbitrary"),
    )(proj3, proj3, proj3, d_br.reshape(nb, lp, N_BRANCH * BRANCH_WIDTH), states, cosf, sinf, bkt, rel_bias, sinks,
      conv_w)
    return (res[0].reshape(nb * lp, ABC_WIDTH),) + tuple(res[1:])


MERGE_TILE = 256
MERGE_FWD_TILE = 544


def _merge_forward(br_ref, m_ref, wb_ref, wo_ref):
    bo, gates = [], []
    mixed_pre = None
    for g in range(N_BRANCH):
        br_g = br_ref[:, BRANCH_WIDTH * g:BRANCH_WIDTH * (g + 1)]
        bo_g = jnp.concatenate([_nn(br_g, wb_ref[p, g]) for p in range(N_CHIPS)], axis=1)
        gate_g = _sigmoid(m_ref[:, D_MODEL * g:D_MODEL * (g + 1)].astype(F32))
        bo.append(bo_g)
        gates.append(gate_g)
        mixed_pre = gate_g * bo_g if mixed_pre is None else mixed_pre + gate_g * bo_g
    mixed = _nn(mixed_pre.astype(BF16), wo_ref[...])
    r = lax.rsqrt(jnp.mean(mixed * mixed, axis=-1, keepdims=True) + RMS_EPS)
    return bo, gates, mixed_pre, mixed, r


def merge_fwd(x2d, br, pm, wb, wo, g_post):
    t = x2d.shape[0]
    tm = MERGE_FWD_TILE if t % MERGE_FWD_TILE == 0 else BLOCK

    def body(x_ref, br_ref, m_ref, wb_ref, wo_ref, g_ref, o_ref):
        _, _, _, mixed, r = _merge_forward(br_ref, m_ref, wb_ref, wo_ref)
        o_ref[...] = x_ref[...] + mixed * r * g_ref[...]

    return pl.pallas_call(
        body, name="merge_fwd",
        grid=(t // tm,),
        in_specs=[pl.BlockSpec((tm, D_MODEL), lambda i: (i, 0)),
                  pl.BlockSpec((tm, N_BRANCH * BRANCH_WIDTH), lambda i: (i, 0)),
                  pl.BlockSpec((tm, MERGE_WIDTH), lambda i: (i, 0)),
                  pl.BlockSpec((N_CHIPS, N_BRANCH, BRANCH_WIDTH, SHARD_D), lambda i: (0, 0, 0, 0)),
                  pl.BlockSpec((D_MODEL, D_MODEL), lambda i: (0, 0)),
                  pl.BlockSpec((1, D_MODEL), lambda i: (0, 0))],
        out_specs=pl.BlockSpec((tm, D_MODEL), lambda i: (i, 0)),
        out_shape=jax.ShapeDtypeStruct((t, D_MODEL), F32),
        compiler_params=_cparams("parallel"),
    )(x2d, br, pm, wb, wo, g_post)


def merge_bwd(d_out, br, pm, wb, wo, g_post):
    t = d_out.shape[0]
    tm = MERGE_TILE if t % MERGE_TILE == 0 else BLOCK

    def body(do_ref, br_ref, m_ref, wb_ref, wo_ref, g_ref, dbr_ref, dm_ref, dg_ref, dwb_ref, dwo_ref):

        @pl.when(pl.program_id(0) == 0)
        def _():
            dwb_ref[...] = jnp.zeros_like(dwb_ref)
            dwo_ref[...] = jnp.zeros_like(dwo_ref)
            dg_ref[...] = jnp.zeros_like(dg_ref)

        bo, gates, mixed_pre, mixed, r = _merge_forward(br_ref, m_ref, wb_ref, wo_ref)
        d_o = do_ref[...]
        nh = mixed * r
        dg_ref[0:1, :] += jnp.sum(d_o * nh, axis=0, keepdims=True)
        dn = d_o * g_ref[...]
        d_mixed = (r * (dn - nh * jnp.mean(dn * nh, axis=-1, keepdims=True))).astype(BF16)
        dwo_ref[...] += _tn(mixed_pre.astype(BF16), d_mixed)
        d_pre = _nt(d_mixed, wo_ref[...])
        for g in range(N_BRANCH):
            br_g = br_ref[:, BRANCH_WIDTH * g:BRANCH_WIDTH * (g + 1)]
            d_bo = (d_pre * gates[g]).astype(BF16)
            dm_ref[:, D_MODEL * g:D_MODEL * (g + 1)] = (
                d_pre * bo[g] * gates[g] * (1.0 - gates[g])).astype(BF16)
            d_br_g = None
            for p in range(N_CHIPS):
                d_bo_p = d_bo[:, SHARD_D * p:SHARD_D * (p + 1)]
                part = _nt(d_bo_p, wb_ref[p, g])
                d_br_g = part if d_br_g is None else d_br_g + part
                dwb_ref[p, g] += _tn(br_g, d_bo_p)
            dbr_ref[:, BRANCH_WIDTH * g:BRANCH_WIDTH * (g + 1)] = d_br_g.astype(BF16)

    return pl.pallas_call(
        body, name="merge_bwd",
        grid=(t // tm,),
        in_specs=[pl.BlockSpec((tm, D_MODEL), lambda i: (i, 0)),
                  pl.BlockSpec((tm, N_BRANCH * BRANCH_WIDTH), lambda i: (i, 0)),
                  pl.BlockSpec((tm, MERGE_WIDTH), lambda i: (i, 0)),
                  pl.BlockSpec((N_CHIPS, N_BRANCH, BRANCH_WIDTH, SHARD_D), lambda i: (0, 0, 0, 0)),
                  pl.BlockSpec((D_MODEL, D_MODEL), lambda i: (0, 0)),
                  pl.BlockSpec((1, D_MODEL), lambda i: (0, 0))],
        out_specs=[pl.BlockSpec((tm, N_BRANCH * BRANCH_WIDTH), lambda i: (i, 0)),
                   pl.BlockSpec((tm, MERGE_WIDTH), lambda i: (i, 0)),
                   pl.BlockSpec((8, D_MODEL), lambda i: (0, 0)),
                   pl.BlockSpec((N_CHIPS, N_BRANCH, BRANCH_WIDTH, SHARD_D), lambda i: (0, 0, 0, 0)),
                   pl.BlockSpec((D_MODEL, D_MODEL), lambda i: (0, 0))],
        out_shape=[jax.ShapeDtypeStruct((t, N_BRANCH * BRANCH_WIDTH), BF16),
                   jax.ShapeDtypeStruct((t, MERGE_WIDTH), BF16),
                   jax.ShapeDtypeStruct((8, D_MODEL), F32),
                   jax.ShapeDtypeStruct((N_CHIPS, N_BRANCH, BRANCH_WIDTH, SHARD_D), F32),
                   jax.ShapeDtypeStruct((D_MODEL, D_MODEL), F32)],
        compiler_params=_cparams("arbitrary"),
    )(d_out, br, pm, wb, wo, g_post)


def loss_head(xf, target2d, nb, nc):
    def body(x_ref, t_ref, l_ref, dx_ref):
        b = pl.program_id(0)
        n = pl.program_id(1)

        @pl.when((b == 0) & (n == 0))
        def _():
            l_ref[...] = jnp.zeros_like(l_ref)

        @pl.when(n == 0)
        def _():
            dx_ref[...] = jnp.zeros_like(dx_ref)

        @pl.when(n > 0)
        def _():
            e = x_ref[...] - t_ref[...]
            dx_ref[...] = e * (1.0 / D_MODEL)
            s = jnp.sum(jnp.sum(e * e, axis=1, keepdims=True), axis=0, keepdims=True)
            l_ref[...] += jnp.broadcast_to(s * (0.5 / D_MODEL), l_ref.shape)

    return pl.pallas_call(
        body, name="loss_head",
        grid=(nb, nc),
        in_specs=[pl.BlockSpec((BLOCK, D_MODEL), lambda b, n: (b * nc + n, 0)),
                  pl.BlockSpec((BLOCK, D_MODEL), lambda b, n: (b * (nc - 1) + jnp.maximum(n - 1, 0), 0))],
        out_specs=[pl.BlockSpec((8, BLOCK), lambda b, n: (0, 0)),
                   pl.BlockSpec((BLOCK, D_MODEL), lambda b, n: (b * nc + n, 0))],
        out_shape=[jax.ShapeDtypeStruct((8, BLOCK), F32),
                   jax.ShapeDtypeStruct(xf.shape, F32)],
        compiler_params=_cparams("arbitrary", "arbitrary"),
    )(xf, target2d)


N_ABC_TILES = ABC_WIDTH // COL_TILE
N_M_TILES = MERGE_WIDTH // COL_TILE


def proj_dgrad(d_abc, d_m, w, x2d, g, d_out):
    t = x2d.shape[0]
    tm = ROW_TILE if t % ROW_TILE == 0 else BLOCK
    nk = N_ABC_TILES + N_M_TILES

    def body(da_ref, dm_ref, w_ref, x_ref, g_ref, do_ref, dx_ref, dg_ref, acc):
        i = pl.program_id(0)
        k = pl.program_id(1)

        @pl.when((i == 0) & (k == 0))
        def _():
            dg_ref[...] = jnp.zeros_like(dg_ref)

        @pl.when(k == 0)
        def _():
            acc[...] = jnp.zeros_like(acc)

        @pl.when(k < N_ABC_TILES)
        def _():
            acc[...] += _nn(da_ref[...], w_ref[...])

        @pl.when(k >= N_ABC_TILES)
        def _():
            acc[...] += _nn(dm_ref[...], w_ref[...])

        @pl.when(k == nk - 1)
        def _():
            x = x_ref[...]
            r = lax.rsqrt(jnp.mean(x * x, axis=-1, keepdims=True) + RMS_EPS)
            nh = x * r
            dh = acc[...]
            dg_ref[0:1, :] += jnp.sum(dh * nh, axis=0, keepdims=True)
            dn = dh * g_ref[...]
            dx_ref[...] = do_ref[...] + r * (dn - nh * jnp.mean(dn * nh, axis=-1, keepdims=True))

    return pl.pallas_call(
        body, name="proj_dgrad",
        grid=(t // tm, nk),
        in_specs=[pl.BlockSpec((tm, COL_TILE), lambda i, k: (i, jnp.minimum(k, N_ABC_TILES - 1))),
                  pl.BlockSpec((tm, COL_TILE), lambda i, k: (i, jnp.maximum(k - N_ABC_TILES, 0))),
                  pl.BlockSpec((COL_TILE, D_MODEL), lambda i, k: (k, 0)),
                  pl.BlockSpec((tm, D_MODEL), lambda i, k: (i, 0)),
                  pl.BlockSpec((1, D_MODEL), lambda i, k: (0, 0)),
                  pl.BlockSpec((tm, D_MODEL), lambda i, k: (i, 0))],
        out_specs=[pl.BlockSpec((tm, D_MODEL), lambda i, k: (i, 0)),
                   pl.BlockSpec((8, D_MODEL), lambda i, k: (0, 0))],
        out_shape=[jax.ShapeDtypeStruct((t, D_MODEL), F32),
                   jax.ShapeDtypeStruct((8, D_MODEL), F32)],
        scratch_shapes=[pltpu.VMEM((tm, D_MODEL), F32)],
        compiler_params=_cparams("arbitrary", "arbitrary"),
    )(d_abc, d_m, w, x2d, g, d_out)


def proj_wgrad(hb, d_abc, d_m):
    t = hb.shape[0]
    nj = N_ABC_TILES + N_M_TILES

    def body(h_ref, da_ref, dm_ref, o_ref):
        j = pl.program_id(0)

        @pl.when(j < N_ABC_TILES)
        def _():
            o_ref[...] = _tn(da_ref[...], h_ref[...])

        @pl.when(j >= N_ABC_TILES)
        def _():
            o_ref[...] = _tn(dm_ref[...], h_ref[...])

    return pl.pallas_call(
        body, name="proj_wgrad",
        grid=(nj,),
        in_specs=[pl.BlockSpec((t, D_MODEL), lambda j: (0, 0)),
                  pl.BlockSpec((t, COL_TILE), lambda j: (0, jnp.minimum(j, N_ABC_TILES - 1))),
                  pl.BlockSpec((t, COL_TILE), lambda j: (0, jnp.maximum(j - N_ABC_TILES, 0)))],
        out_specs=pl.BlockSpec((COL_TILE, D_MODEL), lambda j: (j, 0)),
        out_shape=jax.ShapeDtypeStruct((PROJ_WIDTH, D_MODEL), F32),
        compiler_params=_cparams("arbitrary"),
    )(hb, d_abc, d_m)


def _adamw_math(w, g, m, v):
    m = ADAM_B1 * m + (1.0 - ADAM_B1) * g
    v = ADAM_B2 * v + (1.0 - ADAM_B2) * jnp.square(g)
    m_hat = m / (1.0 - ADAM_B1 ** ADAM_STEP)
    v_hat = v / (1.0 - ADAM_B2 ** ADAM_STEP)
    delta = -ADAM_LR * (m_hat / (jnp.sqrt(v_hat) + ADAM_EPS) + ADAM_WD * w)
    return delta, m, v


def adamw_layer(w, g, m, v, layer, acc, after):
    _, r, c = w.shape
    tr = _row_tile(r)

    def body(*refs):
        w_ref, g_ref, m_ref, v_ref = refs[:4]
        go_ref, d_ref, mo_ref, vo_ref = refs[-4:]
        g_val = g_ref[...]
        d, m_new, v_new = _adamw_math(w_ref[...], g_val, m_ref[...], v_ref[...])
        go_ref[...] = g_val
        d_ref[...] = d
        mo_ref[...] = m_new
        vo_ref[...] = v_new

    slab = pl.BlockSpec((None, tr, c), lambda i: (layer, i, 0))
    ins = [w, g, m, v, after]
    in_specs = [slab, pl.BlockSpec((tr, c), lambda i: (i, 0)), slab, slab, ANY]
    aliases = {}
    if acc is not None:
        ins += list(acc)
        in_specs += [ANY] * 4
        aliases = {5 + i: i for i in range(4)}
    return pl.pallas_call(
        body, name="adamw_layer",
        grid=(r // tr,),
        in_specs=in_specs, out_specs=[slab] * 4,
        out_shape=[jax.ShapeDtypeStruct(w.shape, F32)] * 4,
        input_output_aliases=aliases,
        compiler_params=_cparams("parallel"),
    )(*ins)


def adamw_small(params):
    k = len(params)

    def body(*refs):
        ins, outs = refs[:4 * k], refs[4 * k:]
        for i in range(k):
            d, m_new, v_new = _adamw_math(*[r[...] for r in ins[4 * i:4 * i + 4]])
            outs[3 * i][...] = d
            outs[3 * i + 1][...] = m_new
            outs[3 * i + 2][...] = v_new

    flat = [a for p in params for a in p]
    vm = pl.BlockSpec(memory_space=pltpu.VMEM)
    out_shape = [jax.ShapeDtypeStruct(p[0].shape, F32) for p in params for _ in range(3)]
    res = pl.pallas_call(
        body, name="adamw_small",
        in_specs=[vm] * len(flat), out_specs=[vm] * len(out_shape), out_shape=out_shape,
    )(*flat)
    return [tuple(res[3 * i:3 * i + 3]) for i in range(k)]


ANY = pl.BlockSpec(memory_space=pl.ANY)


def _place():
    return lax.axis_index("x"), lax.axis_index("y"), lax.axis_index("c")


HBM = pl.BlockSpec(memory_space=pltpu.HBM)
SEM = pl.BlockSpec(memory_space=pltpu.SEMAPHORE)
EFFECT = pltpu.SideEffectType.DATAFLOW_SIDE_EFFECTING


def _other_chips(x, y):
    return [(1 - x, y), (x, 1 - y), (1 - x, 1 - y)]


def _own_slot(shard, chip):
    buf = lax.empty((N_CHIPS,) + shard.shape, shard.dtype)
    return lax.dynamic_update_slice(buf, shard[None], (chip, 0, 0, 0))


def _hbm(a):
    return pltpu.with_memory_space_constraint(a, pltpu.HBM)


def gather_start(bufs, after):
    n = len(bufs)

    def body(*refs):
        g_refs = refs[:n]
        send_sems, recv_sems = refs[n + 1], refs[n + 2]
        token = refs[-1]
        x, y, c = _place()
        me_p = 2 * x + y
        for t in range(n):
            for k, (qx, qy) in enumerate(_other_chips(x, y)):
                slab = g_refs[t].at[me_p, c]
                pltpu.make_async_remote_copy(src_ref=slab, dst_ref=slab, send_sem=send_sems.at[3 * t + k],
                                             recv_sem=recv_sems.at[3 * t + k], device_id=(qx, qy, c),
                                             device_id_type=MESH).start()
        token[...] = jnp.zeros_like(token)

    res = pl.pallas_call(
        body, name="gather_start",
        in_specs=[HBM] * n + [ANY],
        out_specs=[SEM, SEM] + [HBM] * n + [pl.BlockSpec(memory_space=pltpu.VMEM)],
        out_shape=[pltpu.SemaphoreType.DMA((3 * n,)), pltpu.SemaphoreType.DMA((3 * n,))]
        + [pltpu.HBM(b.shape, b.dtype) for b in bufs] + [jax.ShapeDtypeStruct((8, LANES), F32)],
        input_output_aliases={t: 2 + t for t in range(n)},
        compiler_params=pltpu.CompilerParams(has_side_effects=EFFECT),
    )(*[_hbm(b) for b in bufs], after)
    return res[0], res[1], list(res[2:2 + n]), res[-1]


def gather_wait(bufs, send_sems, recv_sems, after):
    n = len(bufs)

    def body(*refs):
        g_refs = refs[:n]
        send_sems, recv_sems = refs[n], refs[n + 1]
        x, y, c = _place()
        me_p = 2 * x + y
        for t in range(n):
            for k, (qx, qy) in enumerate(_other_chips(x, y)):
                cp = pltpu.make_async_remote_copy(src_ref=g_refs[t].at[me_p, c], dst_ref=g_refs[t].at[2 * qx + qy, c],
                                                  send_sem=send_sems.at[3 * t + k], recv_sem=recv_sems.at[3 * t + k],
                                                  device_id=(qx, qy, c), device_id_type=MESH)
                cp.wait_send()
                cp.wait_recv()

    return pl.pallas_call(
        body, name="gather_wait",
        in_specs=[HBM] * n + [SEM, SEM, ANY],
        out_specs=[HBM] * n,
        out_shape=[pltpu.HBM(b.shape, b.dtype) for b in bufs],
        input_output_aliases={t: t for t in range(n)},
        compiler_params=pltpu.CompilerParams(has_side_effects=EFFECT),
    )(*bufs, send_sems, recv_sems, after)


def gather_forward(bufs):
    n = len(bufs)

    def body(*refs):
        g_refs = refs[n:2 * n]
        send_sems, recv_sems = refs[2 * n:]
        x, y, c = _place()
        sibling = (x, y, 1 - c)
        chips = _other_chips(x, y)
        passed = []
        for t in range(n):
            for k, (qx, qy) in enumerate(chips):
                slab = g_refs[t].at[2 * qx + qy, c]
                fwd = pltpu.make_async_remote_copy(src_ref=slab, dst_ref=slab, send_sem=send_sems.at[3 * t + k],
                                                   recv_sem=recv_sems.at[3 * t + k], device_id=sibling,
                                                   device_id_type=MESH)
                fwd.start()
                passed.append(fwd)
        for t in range(n):
            for k, (qx, qy) in enumerate(chips):
                slab = g_refs[t].at[2 * qx + qy, 1 - c]
                pltpu.make_async_remote_copy(src_ref=slab, dst_ref=slab, send_sem=send_sems.at[3 * t + k],
                                             recv_sem=recv_sems.at[3 * t + k], device_id=sibling,
                                             device_id_type=MESH).wait_recv()
        for cp in passed:
            cp.wait_send()

    return pl.pallas_call(
        body, name="gather_forward",
        in_specs=[ANY] * n, out_specs=[ANY] * n,
        out_shape=[jax.ShapeDtypeStruct(b.shape, b.dtype) for b in bufs],
        input_output_aliases={t: t for t in range(n)},
        scratch_shapes=[pltpu.SemaphoreType.DMA((3 * n,)), pltpu.SemaphoreType.DMA((3 * n,))],
    )(*bufs)


def forward_start(bufs):
    n = len(bufs)

    def body(*refs):
        g_refs = refs[:n]
        send_sems, recv_sems = refs[n], refs[n + 1]
        token = refs[-1]
        x, y, c = _place()
        for t in range(n):
            for k, (qx, qy) in enumerate(_other_chips(x, y)):
                slab = g_refs[t].at[2 * qx + qy, c]
                pltpu.make_async_remote_copy(src_ref=slab, dst_ref=slab, send_sem=send_sems.at[3 * t + k],
                                             recv_sem=recv_sems.at[3 * t + k], device_id=(x, y, 1 - c),
                                             device_id_type=MESH).start()
        token[...] = jnp.zeros_like(token)

    res = pl.pallas_call(
        body, name="forward_start",
        in_specs=[HBM] * n,
        out_specs=[SEM, SEM] + [HBM] * n + [pl.BlockSpec(memory_space=pltpu.VMEM)],
        out_shape=[pltpu.SemaphoreType.DMA((3 * n,)), pltpu.SemaphoreType.DMA((3 * n,))]
        + [pltpu.HBM(b.shape, b.dtype) for b in bufs] + [jax.ShapeDtypeStruct((8, LANES), F32)],
        input_output_aliases={t: 2 + t for t in range(n)},
        compiler_params=pltpu.CompilerParams(has_side_effects=EFFECT),
    )(*[_hbm(b) for b in bufs])
    return res[0], res[1], list(res[2:2 + n]), res[-1]


def forward_wait(bufs, send_sems, recv_sems, after):
    n = len(bufs)

    def body(*refs):
        g_refs = refs[:n]
        send_sems, recv_sems = refs[n], refs[n + 1]
        x, y, c = _place()
        for t in range(n):
            for k, (qx, qy) in enumerate(_other_chips(x, y)):
                cp = pltpu.make_async_remote_copy(src_ref=g_refs[t].at[2 * qx + qy, c],
                                                  dst_ref=g_refs[t].at[2 * qx + qy, 1 - c],
                                                  send_sem=send_sems.at[3 * t + k], recv_sem=recv_sems.at[3 * t + k],
                                                  device_id=(x, y, 1 - c), device_id_type=MESH)
                cp.wait_send()
                cp.wait_recv()

    return pl.pallas_call(
        body, name="forward_wait",
        in_specs=[HBM] * n + [SEM, SEM, ANY],
        out_specs=[HBM] * n,
        out_shape=[pltpu.HBM(b.shape, b.dtype) for b in bufs],
        input_output_aliases={t: t for t in range(n)},
        compiler_params=pltpu.CompilerParams(has_side_effects=EFFECT),
    )(*bufs, send_sems, recv_sems, after)


def exchange_small(pack, after):
    def body(p_ref, after_ref, o_ref, send_sems, recv_sems, local_sem):
        x, y, c = _place()
        me = 4 * x + 2 * y + c
        mine = pltpu.make_async_copy(p_ref, o_ref.at[me], local_sem)
        mine.start()
        sends = []
        for k in range(1, 8):
            fx, fy, fc = (k >> 2) & 1, (k >> 1) & 1, k & 1
            peer = (x ^ fx, y ^ fy, c ^ fc)
            cp = pltpu.make_async_remote_copy(src_ref=p_ref, dst_ref=o_ref.at[me], send_sem=send_sems.at[k - 1],
                                              recv_sem=recv_sems.at[k - 1], device_id=peer, device_id_type=MESH)
            cp.start()
            sends.append(cp)
        for k in range(1, 8):
            fx, fy, fc = (k >> 2) & 1, (k >> 1) & 1, k & 1
            peer = (x ^ fx, y ^ fy, c ^ fc)
            slot = o_ref.at[4 * peer[0] + 2 * peer[1] + peer[2]]
            pltpu.make_async_remote_copy(src_ref=slot, dst_ref=slot, send_sem=send_sems.at[k - 1],
                                         recv_sem=recv_sems.at[k - 1], device_id=peer, device_id_type=MESH).wait_recv()
        for cp in sends:
            cp.wait_send()
        mine.wait()

    return pl.pallas_call(
        body, name="exchange_small",
        in_specs=[ANY, ANY], out_specs=ANY,
        out_shape=jax.ShapeDtypeStruct((8,) + pack.shape, pack.dtype),
        scratch_shapes=[pltpu.SemaphoreType.DMA((7,)), pltpu.SemaphoreType.DMA((7,)), pltpu.SemaphoreType.DMA],
    )(pack, after)


def small_start(pack, me, after):
    buf = lax.dynamic_update_slice(lax.empty((8,) + pack.shape, pack.dtype), pack[None], (me, 0, 0))

    def body(b_ref, after_ref, send_sems, recv_sems, thru, token):
        x, y, c = _place()
        slot = b_ref.at[4 * x + 2 * y + c]
        for k in range(1, 8):
            peer = (x ^ ((k >> 2) & 1), y ^ ((k >> 1) & 1), c ^ (k & 1))
            pltpu.make_async_remote_copy(src_ref=slot, dst_ref=slot, send_sem=send_sems.at[k - 1],
                                         recv_sem=recv_sems.at[k - 1], device_id=peer, device_id_type=MESH).start()
        token[...] = jnp.zeros_like(token)

    return pl.pallas_call(
        body, name="small_start",
        in_specs=[HBM, ANY],
        out_specs=[SEM, SEM, HBM, pl.BlockSpec(memory_space=pltpu.VMEM)],
        out_shape=[pltpu.SemaphoreType.DMA((7,)), pltpu.SemaphoreType.DMA((7,)), pltpu.HBM(buf.shape, buf.dtype),
                   jax.ShapeDtypeStruct((8, LANES), F32)],
        input_output_aliases={0: 2},
        compiler_params=pltpu.CompilerParams(has_side_effects=EFFECT),
    )(_hbm(buf), after)


def small_wait(buf, send_sems, recv_sems, after):
    def body(b_ref, send_sems, recv_sems, after_ref, thru):
        x, y, c = _place()
        mine = b_ref.at[4 * x + 2 * y + c]
        for k in range(1, 8):
            peer = (x ^ ((k >> 2) & 1), y ^ ((k >> 1) & 1), c ^ (k & 1))
            cp = pltpu.make_async_remote_copy(src_ref=mine, dst_ref=b_ref.at[4 * peer[0] + 2 * peer[1] + peer[2]],
                                              send_sem=send_sems.at[k - 1], recv_sem=recv_sems.at[k - 1],
                                              device_id=peer, device_id_type=MESH)
            cp.wait_send()
            cp.wait_recv()

    return pl.pallas_call(
        body, name="small_wait",
        in_specs=[HBM, SEM, SEM, ANY], out_specs=HBM,
        out_shape=pltpu.HBM(buf.shape, buf.dtype),
        input_output_aliases={0: 0},
        compiler_params=pltpu.CompilerParams(has_side_effects=EFFECT),
    )(buf, send_sems, recv_sems, after)


def swap_start(grads, after):
    n = len(grads)

    def body(*refs):
        g_refs, l_refs = refs[:n], refs[n:2 * n]
        send_sems, recv_sems = refs[2 * n + 1], refs[2 * n + 2]
        token = refs[-1]
        x, y, c = _place()
        for t in range(n):
            for p in range(N_CHIPS):
                pltpu.make_async_remote_copy(src_ref=g_refs[t].at[p, 1 - c], dst_ref=l_refs[t].at[p],
                                             send_sem=send_sems.at[N_CHIPS * t + p],
                                             recv_sem=recv_sems.at[N_CHIPS * t + p],
                                             device_id=(x, y, 1 - c), device_id_type=MESH).start()
        token[...] = jnp.zeros_like(token)

    lands = [lax.empty((N_CHIPS,) + g.shape[2:], g.dtype) for g in grads]
    res = pl.pallas_call(
        body, name="swap_start",
        in_specs=[HBM] * (2 * n) + [ANY],
        out_specs=[SEM, SEM] + [HBM] * (2 * n) + [pl.BlockSpec(memory_space=pltpu.VMEM)],
        out_shape=[pltpu.SemaphoreType.DMA((N_CHIPS * n,)), pltpu.SemaphoreType.DMA((N_CHIPS * n,))]
        + [pltpu.HBM(a.shape, a.dtype) for a in grads + lands] + [jax.ShapeDtypeStruct((8, LANES), F32)],
        input_output_aliases={t: 2 + t for t in range(2 * n)},
        compiler_params=pltpu.CompilerParams(has_side_effects=EFFECT),
    )(*[_hbm(a) for a in grads + lands], after)
    return res[0], res[1], list(res[2:2 + n]), list(res[2 + n:2 + 2 * n]), res[-1]


def swap_wait(grads, lands, send_sems, recv_sems, after):
    n = len(grads)

    def body(*refs):
        g_refs, l_refs = refs[:n], refs[n:2 * n]
        send_sems, recv_sems = refs[2 * n], refs[2 * n + 1]
        x, y, c = _place()
        for t in range(n):
            for p in range(N_CHIPS):
                cp = pltpu.make_async_remote_copy(src_ref=g_refs[t].at[p, 1 - c], dst_ref=l_refs[t].at[p],
                                                  send_sem=send_sems.at[N_CHIPS * t + p],
                                                  recv_sem=recv_sems.at[N_CHIPS * t + p],
                                                  device_id=(x, y, 1 - c), device_id_type=MESH)
                cp.wait_send()
                cp.wait_recv()

    res = pl.pallas_call(
        body, name="swap_wait",
        in_specs=[HBM] * (2 * n) + [SEM, SEM, ANY],
        out_specs=[HBM] * (2 * n),
        out_shape=[pltpu.HBM(a.shape, a.dtype) for a in grads + lands],
        input_output_aliases={t: t for t in range(2 * n)},
        compiler_params=pltpu.CompilerParams(has_side_effects=EFFECT),
    )(*grads, *lands, send_sems, recv_sems, after)
    return list(res[:n]), list(res[n:])


def _row_tile(r):
    return max(t for t in range(16, 513, 16) if r % t == 0)


def add_own_half(g, other, c_arr):
    _, _, r, cols = g.shape
    tr = _row_tile(r)

    def body(c_ref, a_ref, b_ref, o_ref):
        o_ref[...] = (a_ref[...] + b_ref[...]).astype(BF16)

    return pl.pallas_call(
        body, name="add_own_half",
        grid_spec=pltpu.PrefetchScalarGridSpec(
            num_scalar_prefetch=1, grid=(N_CHIPS, r // tr),
            in_specs=[pl.BlockSpec((None, None, tr, cols), lambda p, i, c_ref: (p, c_ref[0], i, 0)),
                      pl.BlockSpec((None, tr, cols), lambda p, i, c_ref: (p, i, 0))],
            out_specs=pl.BlockSpec((None, tr, cols), lambda p, i, c_ref: (p, i, 0))),
        out_shape=jax.ShapeDtypeStruct((N_CHIPS, r, cols), BF16),
        compiler_params=_cparams("parallel", "parallel"),
    )(c_arr, g, other)


def scatter_start(partials):
    n = len(partials)

    def body(*refs):
        s_refs, l_refs = refs[:n], refs[n:2 * n]
        send_sems, recv_sems = refs[2 * n], refs[2 * n + 1]
        token = refs[-1]
        x, y, c = _place()
        for t in range(n):
            for k, (qx, qy) in enumerate(_other_chips(x, y)):
                pltpu.make_async_remote_copy(src_ref=s_refs[t].at[2 * qx + qy], dst_ref=l_refs[t].at[k],
                                             send_sem=send_sems.at[3 * t + k], recv_sem=recv_sems.at[3 * t + k],
                                             device_id=(qx, qy, c), device_id_type=MESH).start()
        token[...] = jnp.zeros_like(token)

    lands = [lax.empty((3,) + s.shape[1:], s.dtype) for s in partials]
    res = pl.pallas_call(
        body, name="scatter_start",
        in_specs=[HBM] * (2 * n),
        out_specs=[SEM, SEM] + [HBM] * (2 * n) + [pl.BlockSpec(memory_space=pltpu.VMEM)],
        out_shape=[pltpu.SemaphoreType.DMA((3 * n,)), pltpu.SemaphoreType.DMA((3 * n,))]
        + [pltpu.HBM(a.shape, a.dtype) for a in partials + lands] + [jax.ShapeDtypeStruct((8, LANES), F32)],
        input_output_aliases={t: 2 + t for t in range(2 * n)},
        compiler_params=pltpu.CompilerParams(has_side_effects=EFFECT),
    )(*[_hbm(a) for a in partials + lands])
    return res[0], res[1], list(res[2:2 + n]), list(res[2 + n:2 + 2 * n]), res[-1]


def scatter_wait(partials, lands, send_sems, recv_sems, after):
    n = len(partials)

    def body(*refs):
        s_refs, l_refs = refs[:n], refs[n:2 * n]
        send_sems, recv_sems = refs[2 * n], refs[2 * n + 1]
        x, y, c = _place()
        for t in range(n):
            for k, (qx, qy) in enumerate(_other_chips(x, y)):
                cp = pltpu.make_async_remote_copy(src_ref=s_refs[t].at[2 * qx + qy], dst_ref=l_refs[t].at[k],
                                                  send_sem=send_sems.at[3 * t + k], recv_sem=recv_sems.at[3 * t + k],
                                                  device_id=(qx, qy, c), device_id_type=MESH)
                cp.wait_send()
                cp.wait_recv()

    res = pl.pallas_call(
        body, name="scatter_wait",
        in_specs=[HBM] * (2 * n) + [SEM, SEM, ANY],
        out_specs=[HBM] * (2 * n),
        out_shape=[pltpu.HBM(a.shape, a.dtype) for a in partials + lands],
        input_output_aliases={t: t for t in range(2 * n)},
        compiler_params=pltpu.CompilerParams(has_side_effects=EFFECT),
    )(*partials, *lands, send_sems, recv_sems, after)
    return list(res[:n]), list(res[n:])


def sum_chips(own, parts, where):
    _, r, cols = own.shape
    tr = _row_tile(r)

    def body(w_ref, a_ref, p_ref, o_ref):
        acc = a_ref[...].astype(F32)
        for k in range(3):
            acc = acc + p_ref[k].astype(F32)
        o_ref[...] = acc

    return pl.pallas_call(
        body, name="sum_chips",
        grid_spec=pltpu.PrefetchScalarGridSpec(
            num_scalar_prefetch=1, grid=(r // tr,),
            in_specs=[pl.BlockSpec((None, tr, cols), lambda i, w_ref: (w_ref[0], i, 0)),
                      pl.BlockSpec((3, tr, cols), lambda i, w_ref: (0, i, 0))],
            out_specs=pl.BlockSpec((None, tr, cols), lambda i, w_ref: (w_ref[1], i, 0))),
        out_shape=jax.ShapeDtypeStruct((DEPTH, r, cols), F32),
        compiler_params=_cparams("parallel"),
    )(where, own, parts)


def sibling_share_layer(bufs):
    n = len(bufs)

    def body(*refs):
        o_refs = refs[n:2 * n]
        send_sems, recv_sems = refs[2 * n:]
        x, y, c = _place()
        cps = []
        for t in range(n):
            cp = pltpu.make_async_remote_copy(src_ref=o_refs[t].at[c], dst_ref=o_refs[t].at[c], send_sem=send_sems.at[t],
                                              recv_sem=recv_sems.at[t], device_id=(x, y, 1 - c), device_id_type=MESH)
            cp.start()
            cps.append(cp)
        for t in range(n):
            slot = o_refs[t].at[1 - c]
            pltpu.make_async_remote_copy(src_ref=slot, dst_ref=slot, send_sem=send_sems.at[t], recv_sem=recv_sems.at[t],
                                         device_id=(x, y, 1 - c), device_id_type=MESH).wait_recv()
        for cp in cps:
            cp.wait_send()

    return pl.pallas_call(
        body, name="sibling_share_layer",
        in_specs=[ANY] * n, out_specs=[ANY] * n,
        out_shape=[jax.ShapeDtypeStruct(b.shape, b.dtype) for b in bufs],
        input_output_aliases={t: t for t in range(n)},
        scratch_shapes=[pltpu.SemaphoreType.DMA((n,)), pltpu.SemaphoreType.DMA((n,))],
    )(*bufs)


SP_META = 2 * (N_META * D_MODEL // LANES)
SP_NORM = DEPTH * D_MODEL // LANES
SP_RB = DEPTH * N_BUCKETS
SP_SINK = DEPTH * ATT_HEADS
SP_CONV = DEPTH * 3 * BRANCH_WIDTH // LANES
SP_LOSS = 8
SP_ROWS = SP_META + 2 * SP_NORM + SP_RB + SP_SINK + SP_CONV + SP_LOSS


def sum_small(slots):
    half = SP_META // 2
    rb0 = SP_META + 2 * SP_NORM
    rest_rows = SP_ROWS - SP_META

    def body(s_ref, meta_ref, rest_ref):
        acc = s_ref[0]
        for d in range(1, 8):
            acc = acc + s_ref[d]
        meta_ref[...] = acc[0:half] + acc[half:SP_META]
        rest_ref[...] = acc[SP_META:]
        rest_ref[rb0 - SP_META:rb0 - SP_META + N_BUCKETS, :] = (
            acc[rb0:rb0 + N_BUCKETS] + acc[rb0 + N_BUCKETS:rb0 + 2 * N_BUCKETS])

    vm = pl.BlockSpec(memory_space=pltpu.VMEM)
    return pl.pallas_call(
        body, name="sum_small",
        in_specs=[vm], out_specs=[vm, vm],
        out_shape=[jax.ShapeDtypeStruct((half, LANES), F32), jax.ShapeDtypeStruct((rest_rows, LANES), F32)],
    )(slots)


def local_step(x, loss_target, meta_full, rel_bias, norm_pre, conv_w_full, attn_sinks, norm_post, weights_of, mid_fwd,
               grads_done, bwd_done):
    nb, seq, _ = x.shape
    nc = seq // BLOCK + 1
    lp = nc * BLOCK
    rows = nb * lp
    pad = jnp.zeros((nb, PAD_FRONT, D_MODEL), F32)
    meta = jnp.broadcast_to(meta_full[None], (nb, N_META, D_MODEL))
    h0 = jnp.concatenate([pad, meta, x], axis=1).reshape(rows, D_MODEL)
    cosf, sinf = _rot_tables(lp)
    bkt = jnp.asarray(_bucket_table())

    acts = []
    h = h0
    for l in range(DEPTH):
        (w_in, w_br, w_out), zero = weights_of(l, h)
        hb, p_abc = norm_matmul(h, norm_pre[l][None] + zero, w_in, 0, N_ABC_TILES)
        p_m = matmul_cols(hb, w_in, N_ABC_TILES, N_M_TILES)
        br, states = mixers_fwd(p_abc, cosf, sinf, bkt, rel_bias, attn_sinks[l][None], conv_w_full[l], nb, nc)
        h_next = merge_fwd(h, br, p_m, w_br, w_out, norm_post[l][None] + mid_fwd(l, br))
        acts.append((h, hb, p_abc, p_m, br, states, w_in, w_br, w_out))
        h = h_next

    loss_part, d_h = loss_head(h, loss_target.reshape(nb * seq, D_MODEL), nb, nc)

    small = [None] * DEPTH
    zero_m = jnp.zeros((1, 1), F32)
    for l in reversed(range(DEPTH)):
        h_in, hb, p_abc, p_m, br, states, w_in, w_br, w_out = acts[l]
        d_br, d_m, d_gpost, g_wbr, g_wout = merge_bwd(d_h, br, p_m, w_br, w_out, norm_post[l][None] + zero_m)
        d_abc, d_rb, d_sk, d_cw = mixers_bwd(p_abc, d_br, states, cosf, sinf, bkt, rel_bias,
                                             attn_sinks[l][None], conv_w_full[l], nb, nc)
        g_win = proj_wgrad(hb, d_abc, d_m)
        zero = grads_done(l, [g_win, g_wbr, g_wout])
        d_h, d_gpre = proj_dgrad(d_abc, d_m, w_in, h_in, norm_pre[l][None] + zero, d_h)
        zero_m = bwd_done(l, d_h)
        small[l] = (d_gpre[0], d_gpost[0], d_rb, d_sk, d_cw[0:3])

    d_h3 = d_h.reshape(nb, lp, D_MODEL)
    d_x = d_h3[:, BLOCK:]
    d_meta = d_h3[:, PAD_FRONT:BLOCK]
    sp = jnp.concatenate([
        d_meta.reshape(-1, LANES),
        jnp.stack([small[l][0] for l in range(DEPTH)]).reshape(-1, LANES),
        jnp.stack([small[l][1] for l in range(DEPTH)]).reshape(-1, LANES),
        jnp.concatenate([small[l][2] for l in range(DEPTH)], axis=0),
        jnp.concatenate([small[l][3] for l in range(DEPTH)], axis=0),
        jnp.stack([small[l][4] for l in range(DEPTH)]).reshape(-1, LANES),
        loss_part], axis=0)
    return d_x, sp


def kernel(x, meta_tokens, rel_bias, norm_pre, w_in, conv_w, attn_sinks, w_branch, w_out, norm_post, loss_target, m_meta_tokens, m_rel_bias, m_norm_pre, m_w_in, m_conv_w, m_attn_sinks, m_w_branch, m_w_out, m_norm_post, v_meta_tokens, v_rel_bias, v_norm_pre, v_w_in, v_conv_w, v_attn_sinks, v_w_branch, v_w_out, v_norm_post):
    assert x.shape[0] == 2 and SP_META == 2 * N_META * D_MODEL // LANES
    px, py, pc = _place()
    chip = 2 * px + py

    c_arr = jnp.reshape(pc, (1,)).astype(jnp.int32)
    where = jnp.stack([chip, pc]).astype(jnp.int32)
    tr_ = lambda a: jnp.swapaxes(a, 1, 2)
    w3 = [tr_(w_in), w_branch.reshape(DEPTH, N_BRANCH * BRANCH_WIDTH, SHARD_D), w_out]
    halves = lambda a: a.reshape(2, a.shape[0] // 2, a.shape[1])

    def as_weights(bufs):
        a_in, a_br, a_out = bufs
        return (a_in.reshape(PROJ_WIDTH, D_MODEL), a_br.reshape(N_CHIPS, N_BRANCH, BRANCH_WIDTH, SHARD_D),
                a_out.reshape(D_MODEL, D_MODEL))

    side = jnp.concatenate([meta_tokens.reshape(-1), conv_w.reshape(-1)]).reshape(-1, LANES)
    side = jnp.concatenate([side, jnp.zeros((40 - side.shape[0], LANES), F32)], axis=0)
    side_all = exchange_small(side, side)
    side_chips = side_all[0::2]
    n_meta_rows = N_META * SHARD_D // LANES
    meta_full = jnp.moveaxis(side_chips[:, :n_meta_rows].reshape(N_CHIPS, N_META, SHARD_D), 0, 1).reshape(N_META, D_MODEL)
    conv_full = jnp.moveaxis(side_chips[:, n_meta_rows:n_meta_rows + 6].reshape(N_CHIPS, DEPTH, 3, LANES), 0, 2).reshape(DEPTH, 3, BRANCH_WIDTH)

    slots = [[_own_slot(halves(w[l].astype(BF16)), chip) for w in w3] for l in range(DEPTH)]
    send0, recv0, flying0, started0 = gather_start(slots[0], side_all)
    meta_full = meta_full + started0[0:1, 0:1]
    inbound = {}

    def weights_of(l, h):
        if l == 0:
            gathered0 = gather_forward(gather_wait(flying0, send0, recv0, h))
            inbound[1] = gather_start(slots[1], gathered0[0])
            return as_weights(gathered0), inbound[1][3][0:1, 0:1]
        send, recv, thru = inbound[1]
        return as_weights(forward_wait(thru, send, recv, h)), jnp.zeros((1, 1), F32)

    def mid_fwd(l, br):
        if l == 0:
            send, recv, flying1, _ = inbound[1]
            send, recv, thru, started = forward_start(gather_wait(flying1, send, recv, br))
            inbound[1] = (send, recv, thru)
            return started[0:1, 0:1]
        return jnp.zeros((1, 1), F32)

    reduced = [None] * DEPTH
    flying = {}

    def finish_reduce(l, after):
        partials, parts = scatter_wait(*flying[l], after)
        reduced[l] = sibling_share_layer([sum_chips(a, p, where) for a, p in zip(partials, parts)])

    def start_scatter(l, full, others):
        send, recv, thru, lands, started = scatter_start([add_own_half(g, o, c_arr) for g, o in zip(full, others)])
        flying[l] = (thru, lands, send, recv)
        return started[0:1, 0:1]

    m3 = [tr_(m_w_in), m_w_branch.reshape(w3[1].shape), m_w_out]
    v3 = [tr_(v_w_in), v_w_branch.reshape(w3[1].shape), v_w_out]
    big = [None] * 3

    def adamw_of(l, after):
        for t in range(3):
            big[t] = adamw_layer(w3[t], reduced[l][t].reshape(w3[t].shape[1:]), m3[t], v3[t], l, big[t], after)

    def grads_done(l, grads):
        full = [g.reshape(N_CHIPS, 2, g.size // (2 * N_CHIPS * g.shape[-1]), g.shape[-1]) for g in grads]
        if l == 0:
            finish_reduce(1, grads[0])
        send, recv, thru, lands, started = swap_start(full, where if l == 1 else reduced[1][0])
        if l == 1:
            flying["swap"] = (thru, lands, send, recv)
            return started[0:1, 0:1]
        adamw_of(1, started)
        return start_scatter(0, *swap_wait(thru, lands, send, recv, big[0][1]))

    def bwd_done(l, d_h):
        if l == 1:
            return start_scatter(1, *swap_wait(*flying["swap"], d_h))
        return jnp.zeros((1, 1), F32)

    d_x, sp = local_step(x, loss_target, meta_full, rel_bias, norm_pre, conv_full, attn_sinks, norm_post,
                         weights_of, mid_fwd, grads_done, bwd_done)
    finish_reduce(0, sp)

    s_send, s_recv, s_buf, s_started = small_start(sp, 4 * px + 2 * py + pc, reduced[0][0])

    adamw_of(0, s_started)
    g_in, *u_in = [tr_(a) for a in big[0]]
    g_br, *u_br = [a.reshape(w_branch.shape) for a in big[1]]
    g_out, *u_out = big[2]

    meta_rows, rest = sum_small(small_wait(s_buf, s_send, s_recv, big[0][1]))
    o = 0
    g_meta_full = meta_rows.reshape(N_META, D_MODEL)
    g_norm_pre = rest[o:o + SP_NORM].reshape(DEPTH, D_MODEL); o += SP_NORM
    g_norm_post = rest[o:o + SP_NORM].reshape(DEPTH, D_MODEL); o += SP_NORM
    g_rel_bias = rest[o:o + N_BUCKETS, :ATT_HEADS]; o += SP_RB
    g_sinks = rest[o:o + SP_SINK, 0].reshape(DEPTH, ATT_HEADS); o += SP_SINK
    g_conv_full = rest[o:o + SP_CONV].reshape(DEPTH, 3, BRANCH_WIDTH); o += SP_CONV
    loss = rest[o, 0]
    g_meta = lax.dynamic_slice_in_dim(g_meta_full, chip * SHARD_D, SHARD_D, axis=1)
    g_conv = lax.dynamic_slice_in_dim(g_conv_full, chip * LANES, LANES, axis=2)

    to2 = lambda a: a.reshape(-1, a.shape[-1])
    smalls = [(meta_tokens, g_meta, m_meta_tokens, v_meta_tokens),
              (rel_bias, g_rel_bias, m_rel_bias, v_rel_bias),
              (norm_pre, g_norm_pre, m_norm_pre, v_norm_pre),
              (to2(conv_w), to2(g_conv), to2(m_conv_w), to2(v_conv_w)),
              (attn_sinks, g_sinks, m_attn_sinks, v_attn_sinks),
              (norm_post, g_norm_post, m_norm_post, v_norm_post)]
    u_meta, u_rb, u_npre, u_conv, u_sink, u_npost = adamw_small(smalls)
    u_conv = tuple(a.reshape(conv_w.shape) for a in u_conv)

    grads = [g_meta, g_rel_bias, g_norm_pre, g_in, g_conv, g_sinks, g_br, g_out, g_norm_post]
    upd = [u_meta, u_rb, u_npre, u_in, u_conv, u_sink, u_br, u_out, u_npost]
    return (loss, d_x, *grads, *[u[0] for u in upd], *[u[1] for u in upd], *[u[2] for u in upd])
```

```python
import functools
import math

import numpy as np
import jax
import jax.numpy as jnp
from jax import lax
from jax.experimental import pallas as pl
from jax.experimental.pallas import tpu as pltpu

F32 = jnp.float32
BF16 = jnp.bfloat16
MESH = pl.DeviceIdType.MESH

D_MODEL = 1024
DEPTH = 2
N_META = 16
BLOCK = 128
PAD_FRONT = BLOCK - N_META
ATT_HEADS = 8
ATT_HEAD_DIM = 64
N_BUCKETS = 32
MAX_EXACT = 16
MAX_DISTANCE = 128
RET_HEADS = 4
ROT_BASE = 10000.0
N_BRANCH = 3
BRANCH_WIDTH = 512
PROJ_WIDTH = 8448
ABC_WIDTH = 5376
MERGE_WIDTH = N_BRANCH * D_MODEL
RMS_EPS = 1e-6
GN_EPS = 1e-6
NEG_INF = -1e30
ATT_SCALE = ATT_HEAD_DIM ** -0.5
RET_SCALE = BLOCK ** -0.5
LOG_GAMMA = tuple(math.log1p(-(2.0 ** (-5.0 - h))) for h in range(RET_HEADS))

C_AQ, C_AK, C_AV, C_AG = 0, 512, 640, 768
C_RQ, C_RK, C_RV, C_RG = 1280, 1792, 2304, 2816
C_CB, C_CC, C_CX, C_CG = 3328, 3840, 4352, 4864

ADAM_LR = 0.001
ADAM_B1 = 0.9
ADAM_B2 = 0.999
ADAM_EPS = 1e-08
ADAM_WD = 0.01
ADAM_STEP = 10

N_CHIPS = 4
SHARD_IN = PROJ_WIDTH // N_CHIPS
SHARD_D = D_MODEL // N_CHIPS
LANES = 128
PACK_IN = D_MODEL * SHARD_IN
PACK_BR = N_BRANCH * BRANCH_WIDTH * SHARD_D
PACK_OUT = SHARD_D * D_MODEL
PACK_ROWS = (PACK_IN + PACK_BR + PACK_OUT) // LANES

VMEM_LIMIT = 56 * 1024 * 1024
COL_TILE = 768
ROW_TILE = 1088
PROJ_ROW_TILE = 2176


def _cparams(*sem):
    return pltpu.CompilerParams(dimension_semantics=sem, vmem_limit_bytes=VMEM_LIMIT)


def _nt(a, b):
    return lax.dot_general(a, b, (((1,), (1,)), ((), ())), preferred_element_type=F32)


def _tn(a, b):
    return lax.dot_general(a, b, (((0,), (0,)), ((), ())), preferred_element_type=F32)


def _nn(a, b):
    return jnp.dot(a, b, preferred_element_type=F32)


def _sigmoid(x):
    return 0.5 * jnp.tanh(0.5 * x) + 0.5


def _silu(x):
    return x * _sigmoid(x)


def _dsilu(x):
    s = _sigmoid(x)
    return s * (1.0 + x * (1.0 - s))


def _bucket_table():
    r = np.arange(BLOCK)[:, None]
    c = np.arange(2 * BLOCK)[None, :]
    n = np.maximum(BLOCK + r - c, 0)
    nf = np.maximum(n, 1).astype(np.float32)
    large = MAX_EXACT + (np.log(nf / MAX_EXACT) / math.log(MAX_DISTANCE / MAX_EXACT)
                         * (N_BUCKETS - MAX_EXACT)).astype(np.int32)
    large = np.minimum(large, N_BUCKETS - 1)
    return np.where(n < MAX_EXACT, n, large).astype(np.int32)


def _rot_tables(lp):
    half = BLOCK // 2
    pos = (jnp.arange(lp) - PAD_FRONT).astype(F32)
    theta = 1.0 / (ROT_BASE ** jnp.linspace(0.0, 1.0, half, dtype=F32))
    ang = pos[:, None] * theta[None, :]
    cos, sin = jnp.cos(ang), jnp.sin(ang)
    return jnp.concatenate([cos, cos], axis=1), jnp.concatenate([-sin, sin], axis=1)


def norm_matmul(x2d, g, layer, w, col0_blocks, n_col_blocks, after):
    t = x2d.shape[0]
    tm = PROJ_ROW_TILE if t % PROJ_ROW_TILE == 0 else BLOCK

    def body(x_ref, g_ref, w_ref, after_ref, hb_ref, o_ref):
        @pl.when(pl.program_id(1) == 0)
        def _():
            x = x_ref[...]
            r = lax.rsqrt(jnp.mean(x * x, axis=-1, keepdims=True) + RMS_EPS)
            hb_ref[...] = (x * r * g_ref[...]).astype(BF16)

        o_ref[...] = _nt(hb_ref[...], w_ref[...]).astype(BF16)

    return pl.pallas_call(
        body, name="norm_matmul",
        grid=(t // tm, n_col_blocks),
        in_specs=[pl.BlockSpec((tm, D_MODEL), lambda i, j: (i, 0)),
                  pl.BlockSpec((None, 1, D_MODEL), lambda i, j: (layer, 0, 0)),
                  pl.BlockSpec((COL_TILE, D_MODEL), lambda i, j: (j + col0_blocks, 0)),
                  ANY],
        out_specs=[pl.BlockSpec((tm, D_MODEL), lambda i, j: (i, 0)),
                   pl.BlockSpec((tm, COL_TILE), lambda i, j: (i, j))],
        out_shape=[jax.ShapeDtypeStruct((t, D_MODEL), BF16),
                   jax.ShapeDtypeStruct((t, n_col_blocks * COL_TILE), BF16)],
        compiler_params=_cparams("parallel", "arbitrary"),
    )(x2d, g, w, after)


def matmul_cols(a, w, col0_blocks, n_col_blocks):
    t, k = a.shape
    tm = PROJ_ROW_TILE if t % PROJ_ROW_TILE == 0 else BLOCK

    def body(a_ref, w_ref, o_ref):
        o_ref[...] = _nt(a_ref[...], w_ref[...]).astype(BF16)

    return pl.pallas_call(
        body, name="matmul_cols",
        grid=(t // tm, n_col_blocks),
        in_specs=[pl.BlockSpec((tm, k), lambda i, j: (i, 0)),
                  pl.BlockSpec((COL_TILE, k), lambda i, j: (j + col0_blocks, 0))],
        out_specs=pl.BlockSpec((tm, COL_TILE), lambda i, j: (i, j)),
        out_shape=jax.ShapeDtypeStruct((t, n_col_blocks * COL_TILE), BF16),
        compiler_params=_cparams("parallel", "arbitrary"),
    )(a, w)


class _Widened:
    def __init__(self, ref):
        self.ref = ref

    def __getitem__(self, idx):
        return self.ref[idx].astype(F32)


def _build_bias(bkt_ref, rb_ref, bias_s):
    bkt = bkt_ref[...]
    for h in range(ATT_HEADS):
        acc = jnp.zeros((BLOCK, 2 * BLOCK), F32)
        for b in range(N_BUCKETS):
            acc = jnp.where(bkt == b, rb_ref[b, h], acc)
        bias_s[h] = acc


def _band_mask(n):
    r = lax.broadcasted_iota(jnp.int32, (BLOCK, 2 * BLOCK), 0)
    c = lax.broadcasted_iota(jnp.int32, (BLOCK, 2 * BLOCK), 1)
    key_pos = (n - 1) * BLOCK + c
    return (c > r) & (c <= r + BLOCK) & (key_pos >= PAD_FRONT)


def _split_heads(kv, kh):
    lane = lax.broadcasted_iota(jnp.int32, kv.shape, 1)
    if kh == 0:
        lo = jnp.where(lane < ATT_HEAD_DIM, kv, 0.0)
        hi = pltpu.roll(lo, ATT_HEAD_DIM, 1)
    else:
        hi = jnp.where(lane >= ATT_HEAD_DIM, kv, 0.0)
        lo = pltpu.roll(hi, ATT_HEAD_DIM, 1)
    return lo, hi


def _merge_heads(acc_lo, acc_hi, kh):
    lane = lax.broadcasted_iota(jnp.int32, acc_lo.shape, 1)
    if kh == 0:
        return jnp.where(lane < ATT_HEAD_DIM, acc_lo + pltpu.roll(acc_hi, ATT_HEAD_DIM, 1), 0.0)
    return jnp.where(lane >= ATT_HEAD_DIM, acc_hi + pltpu.roll(acc_lo, ATT_HEAD_DIM, 1), 0.0)


def _softmax_sink(q2b, kxb, bias_h, mask, sink_h):
    return _softmax_of(_nt(q2b, kxb), bias_h, mask, sink_h)


def _softmax_of(qk, bias_h, mask, sink_h):
    s = qk * ATT_SCALE + bias_h
    s = jnp.where(mask, s, NEG_INF)
    m = jnp.maximum(jnp.max(s, axis=-1, keepdims=True), sink_h)
    p = jnp.exp(s - m)
    es = jnp.exp(sink_h - m)
    inv = 1.0 / (jnp.sum(p, axis=-1, keepdims=True) + es)
    return p * inv, es * inv


def _rot(t, cosf, sinf):
    return t * cosf + pltpu.roll(t, BLOCK // 2, 1) * sinf


def _rot_t(d, cosf, sinf):
    return d * cosf + pltpu.roll(d * sinf, BLOCK // 2, 1)


def _decay_tables(h):
    lg = LOG_GAMMA[h]
    i = lax.broadcasted_iota(jnp.int32, (BLOCK, BLOCK), 0)
    j = lax.broadcasted_iota(jnp.int32, (BLOCK, BLOCK), 1)
    diff = (i - j).astype(F32)
    dm = jnp.where(diff >= 0, jnp.exp(diff * lg), 0.0)
    row = lax.broadcasted_iota(jnp.int32, (BLOCK, 1), 0).astype(F32)
    zeta = jnp.exp((BLOCK - 1 - row) * lg)
    xi = jnp.exp((row + 1.0) * lg)
    return dm, zeta, xi, math.exp(BLOCK * lg)


def _valid_col(n):
    row = lax.broadcasted_iota(jnp.int32, (BLOCK, 1), 0)
    return ((n * BLOCK + row) >= PAD_FRONT).astype(F32)


def _shift_down(cur, prev, k):
    row = lax.broadcasted_iota(jnp.int32, cur.shape, 0)
    return jnp.where(row >= k, pltpu.roll(cur, k, 0), pltpu.roll(prev, k, 0))


def _shift_up(cur, nxt, k):
    row = lax.broadcasted_iota(jnp.int32, cur.shape, 0)
    return jnp.where(row < BLOCK - k, pltpu.roll(cur, BLOCK - k, 0), pltpu.roll(nxt, BLOCK - k, 0))


def mixers_fwd(proj, cosf, sinf, bkt, rel_bias, sinks, conv_w, layer, nb, nc):
    def body(p_ref, cos_ref, sin_ref, bkt_ref, rb_ref, sk_ref, cw_ref, br_ref, st_ref,
             bias_s, kv_s, state_s, u_s):
        p_ref = _Widened(p_ref)
        n = pl.program_id(0)

        @pl.when(n == 0)
        def _():
            _build_bias(bkt_ref, rb_ref, bias_s)
            kv_s[:, 0:BLOCK, :] = jnp.zeros((nb, BLOCK, 2 * BLOCK), F32)
            state_s[...] = jnp.zeros_like(state_s)
            u_s[...] = jnp.zeros_like(u_s)

        valid = _valid_col(n)
        mask = _band_mask(n)
        ex = range(nb)

        for b in ex:
            kv_s[b, BLOCK:2 * BLOCK, :] = p_ref[b, :, C_AK:C_AK + 2 * BLOCK]
        for kh in range(2):
            ks = [[t.astype(BF16) for t in _split_heads(kv_s[b, :, 0:BLOCK], kh)] for b in ex]
            vs = [[t.astype(BF16) for t in _split_heads(kv_s[b, :, BLOCK:2 * BLOCK], kh)] for b in ex]
            pairs = [(b, 2 * kh + jj) for jj in range(2) for b in ex]
            subs = [(b, j, x) for (b, j) in pairs for x in range(2)]
            qb_ = {(b, j): p_ref[b, :, C_AQ + BLOCK * j:C_AQ + BLOCK * (j + 1)].astype(BF16) for (b, j) in pairs}
            qk_ = {(b, j, x): _nt(qb_[(b, j)], ks[b][x]) for (b, j, x) in subs}
            pb_ = {}
            for u in subs:
                h = 2 * u[1] + u[2]
                pb_[u] = _softmax_of(qk_[u], bias_s[h], mask, sk_ref[layer, h])[0].astype(BF16)
            o_ = {u: _nn(pb_[u], vs[u[0]][u[2]]) for u in subs}
            for (b, j) in pairs:
                gate = p_ref[b, :, C_AG + BLOCK * j:C_AG + BLOCK * (j + 1)]
                br_ref[b, :, BLOCK * j:BLOCK * (j + 1)] = ((o_[(b, j, 0)] + o_[(b, j, 1)]) * _silu(gate)).astype(BF16)
        for b in ex:
            kv_s[b, 0:BLOCK, :] = kv_s[b, BLOCK:2 * BLOCK, :]

        cosv = cos_ref[...]
        sinv = sin_ref[...]
        tabs = [_decay_tables(h) for h in range(RET_HEADS)]
        units = [(b, h) for h in range(RET_HEADS) for b in ex]
        sl = lambda c0, h: slice(c0 + BLOCK * h, c0 + BLOCK * (h + 1))
        q_, k_, v_, sp_ = {}, {}, {}, {}
        for u in units:
            b, h = u
            q_[u] = _rot(p_ref[b, :, sl(C_RQ, h)], cosv, sinv).astype(BF16)
            k_[u] = (_rot(p_ref[b, :, sl(C_RK, h)], cosv, sinv) * RET_SCALE * valid).astype(BF16)
            v_[u] = p_ref[b, :, sl(C_RV, h)]
            sp_[u] = state_s[b, h]
            st_ref[b, 0, h] = sp_[u]
        qk_ = {u: _nt(q_[u], k_[u]) for u in units}
        qs_ = {u: _nn(q_[u], sp_[u].astype(BF16)) for u in units}
        kv_ = {u: _tn(k_[u], (v_[u] * tabs[u[1]][1]).astype(BF16)) for u in units}
        a_ = {u: (qk_[u] * tabs[u[1]][0]).astype(BF16) for u in units}
        av_ = {u: _nn(a_[u], v_[u].astype(BF16)) for u in units}
        for u in units:
            b, h = u
            o = av_[u] + tabs[h][2] * qs_[u]
            mu = jnp.mean(o, axis=-1, keepdims=True)
            var = jnp.mean(jnp.square(o - mu), axis=-1, keepdims=True)
            oh = (o - mu) * lax.rsqrt(var + GN_EPS)
            gate = p_ref[b, :, sl(C_RG, h)]
            br_ref[b, :, BRANCH_WIDTH + BLOCK * h:BRANCH_WIDTH + BLOCK * (h + 1)] = (oh * _silu(gate)).astype(BF16)
            state_s[b, h] = tabs[h][3] * sp_[u] + kv_[u]

        for b in ex:
            u = p_ref[b, :, C_CC:C_CC + BRANCH_WIDTH] * p_ref[b, :, C_CX:C_CX + BRANCH_WIDTH] * valid
            u_prev = u_s[b]
            y = (cw_ref[0:1, :] * _shift_down(u, u_prev, 2) + cw_ref[1:2, :] * _shift_down(u, u_prev, 1)
                 + cw_ref[2:3, :] * u)
            yc = p_ref[b, :, C_CB:C_CB + BRANCH_WIDTH] * y * _silu(p_ref[b, :, C_CG:C_CG + BRANCH_WIDTH])
            br_ref[b, :, 2 * BRANCH_WIDTH:3 * BRANCH_WIDTH] = yc.astype(BF16)
            u_s[b] = u

    lp = nc * BLOCK
    smem = pl.BlockSpec(memory_space=pltpu.SMEM)
    br, states = pl.pallas_call(
        body, name="mixers_fwd",
        grid=(nc,),
        in_specs=[pl.BlockSpec((nb, BLOCK, ABC_WIDTH), lambda n: (0, n, 0)),
                  pl.BlockSpec((BLOCK, BLOCK), lambda n: (n, 0)),
                  pl.BlockSpec((BLOCK, BLOCK), lambda n: (n, 0)),
                  pl.BlockSpec((BLOCK, 2 * BLOCK), lambda n: (0, 0)),
                  smem, smem,
                  pl.BlockSpec((None, 3, BRANCH_WIDTH), lambda n: (layer, 0, 0))],
        out_specs=[pl.BlockSpec((nb, BLOCK, N_BRANCH * BRANCH_WIDTH), lambda n: (0, n, 0)),
                   pl.BlockSpec((nb, 1, RET_HEADS, BLOCK, BLOCK), lambda n: (0, n, 0, 0, 0))],
        out_shape=[jax.ShapeDtypeStruct((nb, lp, N_BRANCH * BRANCH_WIDTH), BF16),
                   jax.ShapeDtypeStruct((nb, nc, RET_HEADS, BLOCK, BLOCK), F32)],
        scratch_shapes=[pltpu.VMEM((ATT_HEADS, BLOCK, 2 * BLOCK), F32),
                        pltpu.VMEM((nb, 2 * BLOCK, 2 * BLOCK), F32),
                        pltpu.VMEM((nb, RET_HEADS, BLOCK, BLOCK), F32),
                        pltpu.VMEM((nb, BLOCK, BRANCH_WIDTH), F32)],
        compiler_params=_cparams("arbitrary"),
    )(proj.reshape(nb, lp, ABC_WIDTH), cosf, sinf, bkt, rel_bias, sinks, conv_w)
    return br.reshape(nb * lp, N_BRANCH * BRANCH_WIDTH), states


def mixers_bwd(proj, d_br, states, cosf, sinf, bkt, rel_bias, sinks, conv_w, layer, nb, nc):
    def body(p_ref, kvp_ref, cp_ref, dbr_ref, st_ref, cos_ref, sin_ref, bkt_ref, rb_ref, sk_ref, cw_ref,
             dp_ref, drb_ref, dsk_ref, dcw_ref,
             bias_s, dbias_s, dkv_s, g_s, dy_s):
        p_ref, kvp_ref, cp_ref, dbr_ref = [_Widened(r) for r in (p_ref, kvp_ref, cp_ref, dbr_ref)]
        step = pl.program_id(0)
        n = nc - 1 - step
        ex = range(nb)

        @pl.when(step == 0)
        def _():
            _build_bias(bkt_ref, rb_ref, bias_s)
            dbias_s[...] = jnp.zeros_like(dbias_s)
            dsk_ref[...] = jnp.zeros_like(dsk_ref)
            dcw_ref[...] = jnp.zeros_like(dcw_ref)
            drb_ref[...] = jnp.zeros_like(drb_ref)
            dkv_s[...] = jnp.zeros_like(dkv_s)
            g_s[...] = jnp.zeros_like(g_s)
            dy_s[...] = jnp.zeros_like(dy_s)

        valid = _valid_col(n)
        mask = _band_mask(n)
        has_prev = (n > 0).astype(F32)

        k_all, v_all = [], []
        for b in ex:
            kv_prev = kvp_ref[b] * has_prev
            kv_cur = p_ref[b, :, C_AK:C_AK + 2 * BLOCK]
            k_all.append(jnp.concatenate([kv_prev[:, 0:BLOCK], kv_cur[:, 0:BLOCK]], axis=0))
            v_all.append(jnp.concatenate([kv_prev[:, BLOCK:], kv_cur[:, BLOCK:]], axis=0))
        zero2 = jnp.zeros((2 * BLOCK, BLOCK), F32)
        dk_tot = [zero2 for _ in ex]
        dv_tot = [zero2 for _ in ex]
        for kh in range(2):
            ks = [[t.astype(BF16) for t in _split_heads(k_all[b], kh)] for b in ex]
            vs = [[t.astype(BF16) for t in _split_heads(v_all[b], kh)] for b in ex]
            pairs = [(b, 2 * kh + jj) for jj in range(2) for b in ex]
            subs = [(b, j, x) for (b, j) in pairs for x in range(2)]
            qb_, gate_, dya_, do2_ = {}, {}, {}, {}
            for w in pairs:
                b, j = w
                qb_[w] = p_ref[b, :, C_AQ + BLOCK * j:C_AQ + BLOCK * (j + 1)].astype(BF16)
                gate_[w] = p_ref[b, :, C_AG + BLOCK * j:C_AG + BLOCK * (j + 1)]
                dya_[w] = dbr_ref[b, :, BLOCK * j:BLOCK * (j + 1)]
                do2_[w] = (dya_[w] * _silu(gate_[w])).astype(BF16)
            qk_ = {(b, j, x): _nt(qb_[(b, j)], ks[b][x]) for (b, j, x) in subs}
            dpm_ = {(b, j, x): _nt(do2_[(b, j)], vs[b][x]) for (b, j, x) in subs}
            pb_, dsb_ = {}, {}
            for u in subs:
                b, j, x = u
                h = 2 * j + x
                p, p_sink = _softmax_of(qk_[u], bias_s[h], mask, sk_ref[layer, h])
                pb_[u] = p.astype(BF16)
                delta = jnp.sum(p * dpm_[u], axis=-1, keepdims=True)
                ds = p * (dpm_[u] - delta)
                dbias_s[h] += ds
                dsk_ref[h:h + 1, :] += jnp.broadcast_to(
                    jnp.sum(-p_sink * delta, axis=0, keepdims=True), (1, BLOCK))
                dsb_[u] = ds.astype(BF16)
            o_ = {u: _nn(pb_[u], vs[u[0]][u[2]]) for u in subs}
            dq_ = {u: _nn(dsb_[u], ks[u[0]][u[2]]) for u in subs}
            dkm_ = {u: _tn(dsb_[u], qb_[(u[0], u[1])]) for u in subs}
            dvm_ = {u: _tn(pb_[u], do2_[(u[0], u[1])]) for u in subs}
            for w in pairs:
                b, j = w
                o2 = o_[(b, j, 0)] + o_[(b, j, 1)]
                dq2 = (dq_[(b, j, 0)] + dq_[(b, j, 1)]) * ATT_SCALE
                dp_ref[b, :, C_AQ + BLOCK * j:C_AQ + BLOCK * (j + 1)] = dq2.astype(BF16)
                dp_ref[b, :, C_AG + BLOCK * j:C_AG + BLOCK * (j + 1)] = (
                    dya_[w] * o2 * _dsilu(gate_[w])).astype(BF16)
            for b in ex:
                j0, j1 = 2 * kh, 2 * kh + 1
                dk_lo = (dkm_[(b, j0, 0)] + dkm_[(b, j1, 0)]) * ATT_SCALE
                dk_hi = (dkm_[(b, j0, 1)] + dkm_[(b, j1, 1)]) * ATT_SCALE
                dk_tot[b] = dk_tot[b] + _merge_heads(dk_lo, dk_hi, kh)
                dv_tot[b] = dv_tot[b] + _merge_heads(dvm_[(b, j0, 0)] + dvm_[(b, j1, 0)],
                                                     dvm_[(b, j0, 1)] + dvm_[(b, j1, 1)], kh)
        for b in ex:
            dp_ref[b, :, C_AK:C_AK + BLOCK] = (dk_tot[b][BLOCK:, :] + dkv_s[b, :, 0:BLOCK]).astype(BF16)
            dp_ref[b, :, C_AV:C_AV + BLOCK] = (dv_tot[b][BLOCK:, :] + dkv_s[b, :, BLOCK:]).astype(BF16)
            dkv_s[b, :, 0:BLOCK] = dk_tot[b][0:BLOCK, :]
            dkv_s[b, :, BLOCK:] = dv_tot[b][0:BLOCK, :]

        cosv = cos_ref[...]
        sinv = sin_ref[...]
        tabs = [_decay_tables(h) for h in range(RET_HEADS)]
        units = [(b, h) for h in range(RET_HEADS) for b in ex]
        sl = lambda c0, h: slice(c0 + BLOCK * h, c0 + BLOCK * (h + 1))
        q_, k_, v_, vb_, sp_ = {}, {}, {}, {}, {}
        for u in units:
            b, h = u
            q_[u] = _rot(p_ref[b, :, sl(C_RQ, h)], cosv, sinv).astype(BF16)
            k_[u] = (_rot(p_ref[b, :, sl(C_RK, h)], cosv, sinv) * RET_SCALE * valid).astype(BF16)
            v_[u] = p_ref[b, :, sl(C_RV, h)]
            vb_[u] = v_[u].astype(BF16)
            sp_[u] = st_ref[b, 0, h].astype(BF16)
        qk_ = {u: _nt(q_[u], k_[u]) for u in units}
        qs_ = {u: _nn(q_[u], sp_[u]) for u in units}
        a_ = {u: (qk_[u] * tabs[u[1]][0]).astype(BF16) for u in units}
        av_ = {u: _nn(a_[u], vb_[u]) for u in units}
        dob_, dxo_ = {}, {}
        for u in units:
            b, h = u
            xi = tabs[h][2]
            o = av_[u] + xi * qs_[u]
            mu = jnp.mean(o, axis=-1, keepdims=True)
            var = jnp.mean(jnp.square(o - mu), axis=-1, keepdims=True)
            rstd = lax.rsqrt(var + GN_EPS)
            oh = (o - mu) * rstd
            gate = p_ref[b, :, sl(C_RG, h)]
            d_yr = dbr_ref[b, :, BRANCH_WIDTH + BLOCK * h:BRANCH_WIDTH + BLOCK * (h + 1)]
            dp_ref[b, :, sl(C_RG, h)] = (d_yr * oh * _dsilu(gate)).astype(BF16)
            doh = d_yr * _silu(gate)
            do = rstd * (doh - jnp.mean(doh, axis=-1, keepdims=True)
                         - oh * jnp.mean(doh * oh, axis=-1, keepdims=True))
            dob_[u] = do.astype(BF16)
            dxo_[u] = (do * xi).astype(BF16)
        dov_ = {u: _nt(dob_[u], vb_[u]) for u in units}
        dv1_ = {u: _tn(a_[u], dob_[u]) for u in units}
        dq1_ = {u: _nt(dxo_[u], sp_[u]) for u in units}
        gq_ = {u: _tn(q_[u], dxo_[u]) for u in units}
        da_, gb_, zv_ = {}, {}, {}
        for u in units:
            b, h = u
            da_[u] = (dov_[u] * tabs[h][0]).astype(BF16)
            g_next = g_s[b, h]
            gb_[u] = g_next.astype(BF16)
            zv_[u] = (v_[u] * tabs[h][1]).astype(BF16)
            g_s[b, h] = tabs[h][3] * g_next + gq_[u]
        dq2_ = {u: _nn(da_[u], k_[u]) for u in units}
        dk1_ = {u: _tn(da_[u], q_[u]) for u in units}
        dk2_ = {u: _nt(zv_[u], gb_[u]) for u in units}
        dv2_ = {u: _nn(k_[u], gb_[u]) for u in units}
        for u in units:
            b, h = u
            dp_ref[b, :, sl(C_RQ, h)] = _rot_t(dq2_[u] + dq1_[u], cosv, sinv).astype(BF16)
            dp_ref[b, :, sl(C_RK, h)] = _rot_t((dk1_[u] + dk2_[u]) * (RET_SCALE * valid), cosv, sinv).astype(BF16)
            dp_ref[b, :, sl(C_RV, h)] = (dv1_[u] + tabs[h][1] * dv2_[u]).astype(BF16)

        w0, w1, w2 = cw_ref[0:1, :], cw_ref[1:2, :], cw_ref[2:3, :]
        for b in ex:
            cb = p_ref[b, :, C_CB:C_CB + BRANCH_WIDTH]
            cc = p_ref[b, :, C_CC:C_CC + BRANCH_WIDTH]
            cx = p_ref[b, :, C_CX:C_CX + BRANCH_WIDTH]
            cg = p_ref[b, :, C_CG:C_CG + BRANCH_WIDTH]
            u = cc * cx * valid
            u_prev = (cp_ref[b, :, 0:BRANCH_WIDTH] * cp_ref[b, :, BRANCH_WIDTH:2 * BRANCH_WIDTH]
                      * (_valid_col(n - 1) * has_prev))
            u1 = _shift_down(u, u_prev, 1)
            u2 = _shift_down(u, u_prev, 2)
            y = w0 * u2 + w1 * u1 + w2 * u
            d_yc = dbr_ref[b, :, 2 * BRANCH_WIDTH:3 * BRANCH_WIDTH]
            sg = _silu(cg)
            dp_ref[b, :, C_CB:C_CB + BRANCH_WIDTH] = (d_yc * y * sg).astype(BF16)
            dp_ref[b, :, C_CG:C_CG + BRANCH_WIDTH] = (d_yc * cb * y * _dsilu(cg)).astype(BF16)
            dy = d_yc * cb * sg
            dy_next = dy_s[b]
            du = (w2 * dy + w1 * _shift_up(dy, dy_next, 1) + w0 * _shift_up(dy, dy_next, 2)) * valid
            dp_ref[b, :, C_CC:C_CC + BRANCH_WIDTH] = (du * cx).astype(BF16)
            dp_ref[b, :, C_CX:C_CX + BRANCH_WIDTH] = (du * cc).astype(BF16)
            dcw_ref[0:1, :] += jnp.sum(dy * u2, axis=0, keepdims=True)
            dcw_ref[1:2, :] += jnp.sum(dy * u1, axis=0, keepdims=True)
            dcw_ref[2:3, :] += jnp.sum(dy * u, axis=0, keepdims=True)
            dy_s[b] = dy

        @pl.when(step == nc - 1)
        def _():
            bkt = bkt_ref[...]
            row = lax.broadcasted_iota(jnp.int32, (N_BUCKETS, BLOCK), 0)
            lane = lax.broadcasted_iota(jnp.int32, (N_BUCKETS, BLOCK), 1)

            def one_bucket(bk, acc):
                sel = bkt == bk
                for h in range(ATT_HEADS):
                    t = jnp.where(sel, dbias_s[h], 0.0)
                    s = jnp.sum(jnp.sum(t, axis=1, keepdims=True), axis=0, keepdims=True)
                    acc = acc + jnp.where((row == bk) & (lane == h), jnp.broadcast_to(s, acc.shape), 0.0)
                return acc

            drb_ref[...] = lax.fori_loop(0, N_BUCKETS, one_bucket, jnp.zeros((N_BUCKETS, BLOCK), F32))

    lp = nc * BLOCK
    smem = pl.BlockSpec(memory_space=pltpu.SMEM)
    blk = lambda s: nc - 1 - s
    prev = lambda s: jnp.maximum(nc - 2 - s, 0)
    proj3 = proj.reshape(nb, lp, ABC_WIDTH)
    res = pl.pallas_call(
        body, name="mixers_bwd",
        grid=(nc,),
        in_specs=[pl.BlockSpec((nb, BLOCK, ABC_WIDTH), lambda s: (0, blk(s), 0)),
                  pl.BlockSpec((nb, BLOCK, 2 * BLOCK), lambda s: (0, prev(s), C_AK // (2 * BLOCK))),
                  pl.BlockSpec((nb, BLOCK, 1280), lambda s: (0, prev(s), C_CC // 1280)),
                  pl.BlockSpec((nb, BLOCK, N_BRANCH * BRANCH_WIDTH), lambda s: (0, blk(s), 0)),
                  pl.BlockSpec((nb, 1, RET_HEADS, BLOCK, BLOCK), lambda s: (0, blk(s), 0, 0, 0)),
                  pl.BlockSpec((BLOCK, BLOCK), lambda s: (blk(s), 0)),
                  pl.BlockSpec((BLOCK, BLOCK), lambda s: (blk(s), 0)),
                  pl.BlockSpec((BLOCK, 2 * BLOCK), lambda s: (0, 0)),
                  smem, smem,
                  pl.BlockSpec((None, 3, BRANCH_WIDTH), lambda s: (layer, 0, 0))],
        out_specs=[pl.BlockSpec((nb, BLOCK, ABC_WIDTH), lambda s: (0, blk(s), 0)),
                   pl.BlockSpec((N_BUCKETS, BLOCK), lambda s: (0, 0)),
                   pl.BlockSpec((ATT_HEADS, BLOCK), lambda s: (0, 0)),
                   pl.BlockSpec((8, BRANCH_WIDTH), lambda s: (0, 0))],
        out_shape=[jax.ShapeDtypeStruct((nb, lp, ABC_WIDTH), BF16),
                   jax.ShapeDtypeStruct((N_BUCKETS, BLOCK), F32),
                   jax.ShapeDtypeStruct((ATT_HEADS, BLOCK), F32),
                   jax.ShapeDtypeStruct((8, BRANCH_WIDTH), F32)],
        scratch_shapes=[pltpu.VMEM((ATT_HEADS, BLOCK, 2 * BLOCK), F32),
                        pltpu.VMEM((ATT_HEADS, BLOCK, 2 * BLOCK), F32),
                        pltpu.VMEM((nb, BLOCK, 2 * BLOCK), F32),
                        pltpu.VMEM((nb, RET_HEADS, BLOCK, BLOCK), F32),
                        pltpu.VMEM((nb, BLOCK, BRANCH_WIDTH), F32)],
        compiler_params=_cparams("arbitrary"),
    )(proj3, proj3, proj3, d_br.reshape(nb, lp, N_BRANCH * BRANCH_WIDTH), states, cosf, sinf, bkt, rel_bias, sinks,
      conv_w)
    return (res[0].reshape(nb * lp, ABC_WIDTH),) + tuple(res[1:])


MERGE_TILE = 256
MERGE_FWD_TILE = 544


def _merge_forward(br_ref, m_ref, wb_ref, wo_ref):
    bo, gates = [], []
    mixed_pre = None
    for g in range(N_BRANCH):
        br_g = br_ref[:, BRANCH_WIDTH * g:BRANCH_WIDTH * (g + 1)]
        bo_g = jnp.concatenate([_nn(br_g, wb_ref[p, g]) for p in range(N_CHIPS)], axis=1)
        gate_g = _sigmoid(m_ref[:, D_MODEL * g:D_MODEL * (g + 1)].astype(F32))
        bo.append(bo_g)
        gates.append(gate_g)
        mixed_pre = gate_g * bo_g if mixed_pre is None else mixed_pre + gate_g * bo_g
    mixed = _nn(mixed_pre.astype(BF16), wo_ref[...])
    r = lax.rsqrt(jnp.mean(mixed * mixed, axis=-1, keepdims=True) + RMS_EPS)
    return bo, gates, mixed_pre, mixed, r


def merge_fwd(x2d, br, pm, wb, wo, g_post, layer, after):
    t = x2d.shape[0]
    tm = MERGE_FWD_TILE if t % MERGE_FWD_TILE == 0 else BLOCK

    def body(x_ref, br_ref, m_ref, wb_ref, wo_ref, g_ref, after_ref, o_ref):
        _, _, _, mixed, r = _merge_forward(br_ref, m_ref, wb_ref, wo_ref)
        o_ref[...] = x_ref[...] + mixed * r * g_ref[...]

    return pl.pallas_call(
        body, name="merge_fwd",
        grid=(t // tm,),
        in_specs=[pl.BlockSpec((tm, D_MODEL), lambda i: (i, 0)),
                  pl.BlockSpec((tm, N_BRANCH * BRANCH_WIDTH), lambda i: (i, 0)),
                  pl.BlockSpec((tm, MERGE_WIDTH), lambda i: (i, 0)),
                  pl.BlockSpec((N_CHIPS, N_BRANCH, BRANCH_WIDTH, SHARD_D), lambda i: (0, 0, 0, 0)),
                  pl.BlockSpec((D_MODEL, D_MODEL), lambda i: (0, 0)),
                  pl.BlockSpec((None, 1, D_MODEL), lambda i: (layer, 0, 0)),
                  ANY],
        out_specs=pl.BlockSpec((tm, D_MODEL), lambda i: (i, 0)),
        out_shape=jax.ShapeDtypeStruct((t, D_MODEL), F32),
        compiler_params=_cparams("parallel"),
    )(x2d, br, pm, wb, wo, g_post, after)


def merge_bwd(d_out, br, pm, wb, wo, g_post, layer, after):
    t = d_out.shape[0]
    tm = MERGE_TILE if t % MERGE_TILE == 0 else BLOCK

    def body(do_ref, br_ref, m_ref, wb_ref, wo_ref, g_ref, after_ref, dbr_ref, dm_ref, dg_ref, dwb_ref, dwo_ref):

        @pl.when(pl.program_id(0) == 0)
        def _():
            dwb_ref[...] = jnp.zeros_like(dwb_ref)
            dwo_ref[...] = jnp.zeros_like(dwo_ref)
            dg_ref[...] = jnp.zeros_like(dg_ref)

        bo, gates, mixed_pre, mixed, r = _merge_forward(br_ref, m_ref, wb_ref, wo_ref)
        d_o = do_ref[...]
        nh = mixed * r
        dg_ref[0:1, :] += jnp.sum(d_o * nh, axis=0, keepdims=True)
        dn = d_o * g_ref[...]
        d_mixed = (r * (dn - nh * jnp.mean(dn * nh, axis=-1, keepdims=True))).astype(BF16)
        dwo_ref[...] += _tn(mixed_pre.astype(BF16), d_mixed)
        d_pre = _nt(d_mixed, wo_ref[...])
        for g in range(N_BRANCH):
            br_g = br_ref[:, BRANCH_WIDTH * g:BRANCH_WIDTH * (g + 1)]
            d_bo = (d_pre * gates[g]).astype(BF16)
            dm_ref[:, D_MODEL * g:D_MODEL * (g + 1)] = (
                d_pre * bo[g] * gates[g] * (1.0 - gates[g])).astype(BF16)
            d_br_g = None
            for p in range(N_CHIPS):
                d_bo_p = d_bo[:, SHARD_D * p:SHARD_D * (p + 1)]
                part = _nt(d_bo_p, wb_ref[p, g])
                d_br_g = part if d_br_g is None else d_br_g + part
                dwb_ref[p, g] += _tn(br_g, d_bo_p)
            dbr_ref[:, BRANCH_WIDTH * g:BRANCH_WIDTH * (g + 1)] = d_br_g.astype(BF16)

    return pl.pallas_call(
        body, name="merge_bwd",
        grid=(t // tm,),
        in_specs=[pl.BlockSpec((tm, D_MODEL), lambda i: (i, 0)),
                  pl.BlockSpec((tm, N_BRANCH * BRANCH_WIDTH), lambda i: (i, 0)),
                  pl.BlockSpec((tm, MERGE_WIDTH), lambda i: (i, 0)),
                  pl.BlockSpec((N_CHIPS, N_BRANCH, BRANCH_WIDTH, SHARD_D), lambda i: (0, 0, 0, 0)),
                  pl.BlockSpec((D_MODEL, D_MODEL), lambda i: (0, 0)),
                  pl.BlockSpec((None, 1, D_MODEL), lambda i: (layer, 0, 0)),
                  ANY],
        out_specs=[pl.BlockSpec((tm, N_BRANCH * BRANCH_WIDTH), lambda i: (i, 0)),
                   pl.BlockSpec((tm, MERGE_WIDTH), lambda i: (i, 0)),
                   pl.BlockSpec((8, D_MODEL), lambda i: (0, 0)),
                   pl.BlockSpec((N_CHIPS, N_BRANCH, BRANCH_WIDTH, SHARD_D), lambda i: (0, 0, 0, 0)),
                   pl.BlockSpec((D_MODEL, D_MODEL), lambda i: (0, 0))],
        out_shape=[jax.ShapeDtypeStruct((t, N_BRANCH * BRANCH_WIDTH), BF16),
                   jax.ShapeDtypeStruct((t, MERGE_WIDTH), BF16),
                   jax.ShapeDtypeStruct((8, D_MODEL), F32),
                   jax.ShapeDtypeStruct((N_CHIPS, N_BRANCH, BRANCH_WIDTH, SHARD_D), F32),
                   jax.ShapeDtypeStruct((D_MODEL, D_MODEL), F32)],
        compiler_params=_cparams("arbitrary"),
    )(d_out, br, pm, wb, wo, g_post, after)


def loss_head(xf, target2d, nb, nc):
    def body(x_ref, t_ref, l_ref, dx_ref):
        b = pl.program_id(0)
        n = pl.program_id(1)

        @pl.when((b == 0) & (n == 0))
        def _():
            l_ref[...] = jnp.zeros_like(l_ref)

        @pl.when(n == 0)
        def _():
            dx_ref[...] = jnp.zeros_like(dx_ref)

        @pl.when(n > 0)
        def _():
            e = x_ref[...] - t_ref[...]
            dx_ref[...] = e * (1.0 / D_MODEL)
            s = jnp.sum(jnp.sum(e * e, axis=1, keepdims=True), axis=0, keepdims=True)
            l_ref[...] += jnp.broadcast_to(s * (0.5 / D_MODEL), l_ref.shape)

    return pl.pallas_call(
        body, name="loss_head",
        grid=(nb, nc),
        in_specs=[pl.BlockSpec((BLOCK, D_MODEL), lambda b, n: (b * nc + n, 0)),
                  pl.BlockSpec((BLOCK, D_MODEL), lambda b, n: (b * (nc - 1) + jnp.maximum(n - 1, 0), 0))],
        out_specs=[pl.BlockSpec((8, BLOCK), lambda b, n: (0, 0)),
                   pl.BlockSpec((BLOCK, D_MODEL), lambda b, n: (b * nc + n, 0))],
        out_shape=[jax.ShapeDtypeStruct((8, BLOCK), F32),
                   jax.ShapeDtypeStruct(xf.shape, F32)],
        compiler_params=_cparams("arbitrary", "arbitrary"),
    )(xf, target2d)


N_ABC_TILES = ABC_WIDTH // COL_TILE
N_M_TILES = MERGE_WIDTH // COL_TILE


def proj_dgrad(d_abc, d_m, w, x2d, g, layer, d_out, after):
    t = x2d.shape[0]
    tm = ROW_TILE if t % ROW_TILE == 0 else BLOCK
    nk = N_ABC_TILES + N_M_TILES

    def body(da_ref, dm_ref, w_ref, x_ref, g_ref, do_ref, after_ref, dx_ref, dg_ref, acc):
        i = pl.program_id(0)
        k = pl.program_id(1)

        @pl.when((i == 0) & (k == 0))
        def _():
            dg_ref[...] = jnp.zeros_like(dg_ref)

        @pl.when(k == 0)
        def _():
            acc[...] = jnp.zeros_like(acc)

        @pl.when(k < N_ABC_TILES)
        def _():
            acc[...] += _nn(da_ref[...], w_ref[...])

        @pl.when(k >= N_ABC_TILES)
        def _():
            acc[...] += _nn(dm_ref[...], w_ref[...])

        @pl.when(k == nk - 1)
        def _():
            x = x_ref[...]
            r = lax.rsqrt(jnp.mean(x * x, axis=-1, keepdims=True) + RMS_EPS)
            nh = x * r
            dh = acc[...]
            dg_ref[0:1, :] += jnp.sum(dh * nh, axis=0, keepdims=True)
            dn = dh * g_ref[...]
            dx_ref[...] = do_ref[...] + r * (dn - nh * jnp.mean(dn * nh, axis=-1, keepdims=True))

    return pl.pallas_call(
        body, name="proj_dgrad",
        grid=(t // tm, nk),
        in_specs=[pl.BlockSpec((tm, COL_TILE), lambda i, k: (i, jnp.minimum(k, N_ABC_TILES - 1))),
                  pl.BlockSpec((tm, COL_TILE), lambda i, k: (i, jnp.maximum(k - N_ABC_TILES, 0))),
                  pl.BlockSpec((COL_TILE, D_MODEL), lambda i, k: (k, 0)),
                  pl.BlockSpec((tm, D_MODEL), lambda i, k: (i, 0)),
                  pl.BlockSpec((None, 1, D_MODEL), lambda i, k: (layer, 0, 0)),
                  pl.BlockSpec((tm, D_MODEL), lambda i, k: (i, 0)),
                  ANY],
        out_specs=[pl.BlockSpec((tm, D_MODEL), lambda i, k: (i, 0)),
                   pl.BlockSpec((8, D_MODEL), lambda i, k: (0, 0))],
        out_shape=[jax.ShapeDtypeStruct((t, D_MODEL), F32),
                   jax.ShapeDtypeStruct((8, D_MODEL), F32)],
        scratch_shapes=[pltpu.VMEM((tm, D_MODEL), F32)],
        compiler_params=_cparams("arbitrary", "arbitrary"),
    )(d_abc, d_m, w, x2d, g, d_out, after)


def proj_wgrad(hb, d_abc, d_m):
    t = hb.shape[0]
    nj = N_ABC_TILES + N_M_TILES

    def body(h_ref, da_ref, dm_ref, o_ref):
        j = pl.program_id(0)

        @pl.when(j < N_ABC_TILES)
        def _():
            o_ref[...] = _tn(da_ref[...], h_ref[...])

        @pl.when(j >= N_ABC_TILES)
        def _():
            o_ref[...] = _tn(dm_ref[...], h_ref[...])

    return pl.pallas_call(
        body, name="proj_wgrad",
        grid=(nj,),
        in_specs=[pl.BlockSpec((t, D_MODEL), lambda j: (0, 0)),
                  pl.BlockSpec((t, COL_TILE), lambda j: (0, jnp.minimum(j, N_ABC_TILES - 1))),
                  pl.BlockSpec((t, COL_TILE), lambda j: (0, jnp.maximum(j - N_ABC_TILES, 0)))],
        out_specs=pl.BlockSpec((COL_TILE, D_MODEL), lambda j: (j, 0)),
        out_shape=jax.ShapeDtypeStruct((PROJ_WIDTH, D_MODEL), F32),
        compiler_params=_cparams("arbitrary"),
    )(hb, d_abc, d_m)


def _adamw_math(w, g, m, v):
    m = ADAM_B1 * m + (1.0 - ADAM_B1) * g
    v = ADAM_B2 * v + (1.0 - ADAM_B2) * jnp.square(g)
    m_hat = m / (1.0 - ADAM_B1 ** ADAM_STEP)
    v_hat = v / (1.0 - ADAM_B2 ** ADAM_STEP)
    delta = -ADAM_LR * (m_hat / (jnp.sqrt(v_hat) + ADAM_EPS) + ADAM_WD * w)
    return delta, m, v


def adamw_layer(w, g, m, v, layer, acc, after):
    _, r, c = w.shape
    tr = _row_tile(r)

    def body(*refs):
        w_ref, g_ref, m_ref, v_ref = refs[:4]
        go_ref, d_ref, mo_ref, vo_ref = refs[-4:]
        g_val = g_ref[...]
        d, m_new, v_new = _adamw_math(w_ref[...], g_val, m_ref[...], v_ref[...])
        go_ref[...] = g_val
        d_ref[...] = d
        mo_ref[...] = m_new
        vo_ref[...] = v_new

    slab = pl.BlockSpec((None, tr, c), lambda i: (layer, i, 0))
    ins = [w, g, m, v, after]
    in_specs = [slab, pl.BlockSpec((tr, c), lambda i: (i, 0)), slab, slab, ANY]
    aliases = {}
    if acc is not None:
        ins += list(acc)
        in_specs += [ANY] * 4
        aliases = {5 + i: i for i in range(4)}
    return pl.pallas_call(
        body, name="adamw_layer",
        grid=(r // tr,),
        in_specs=in_specs, out_specs=[slab] * 4,
        out_shape=[jax.ShapeDtypeStruct(w.shape, F32)] * 4,
        input_output_aliases=aliases,
        compiler_params=_cparams("parallel"),
    )(*ins)


def adamw_small(params):
    k = len(params)

    def body(*refs):
        ins, outs = refs[:4 * k], refs[4 * k:]
        for i in range(k):
            d, m_new, v_new = _adamw_math(*[r[...] for r in ins[4 * i:4 * i + 4]])
            outs[3 * i][...] = d
            outs[3 * i + 1][...] = m_new
            outs[3 * i + 2][...] = v_new

    flat = [a for p in params for a in p]
    vm = pl.BlockSpec(memory_space=pltpu.VMEM)
    out_shape = [jax.ShapeDtypeStruct(p[0].shape, F32) for p in params for _ in range(3)]
    res = pl.pallas_call(
        body, name="adamw_small",
        in_specs=[vm] * len(flat), out_specs=[vm] * len(out_shape), out_shape=out_shape,
    )(*flat)
    return [tuple(res[3 * i:3 * i + 3]) for i in range(k)]


ANY = pl.BlockSpec(memory_space=pl.ANY)


def _place():
    return lax.axis_index("x"), lax.axis_index("y"), lax.axis_index("c")


HBM = pl.BlockSpec(memory_space=pltpu.HBM)
SEM = pl.BlockSpec(memory_space=pltpu.SEMAPHORE)
EFFECT = pltpu.SideEffectType.DATAFLOW_SIDE_EFFECTING


def _other_chips(x, y):
    return [(1 - x, y), (x, 1 - y), (1 - x, 1 - y)]


def _own_slot(shard, chip):
    buf = lax.empty((N_CHIPS,) + shard.shape, shard.dtype)
    return lax.dynamic_update_slice(buf, shard[None], (chip, 0, 0, 0))


def _hbm(a):
    return pltpu.with_memory_space_constraint(a, pltpu.HBM)


def gather_start(bufs, after):
    n = len(bufs)

    def body(*refs):
        g_refs = refs[:n]
        send_sems, recv_sems = refs[n + 1], refs[n + 2]
        token = refs[-1]
        x, y, c = _place()
        me_p = 2 * x + y
        for t in range(n):
            for k, (qx, qy) in enumerate(_other_chips(x, y)):
                slab = g_refs[t].at[me_p, c]
                pltpu.make_async_remote_copy(src_ref=slab, dst_ref=slab, send_sem=send_sems.at[3 * t + k],
                                             recv_sem=recv_sems.at[3 * t + k], device_id=(qx, qy, c),
                                             device_id_type=MESH).start()
        token[...] = jnp.zeros_like(token)

    res = pl.pallas_call(
        body, name="gather_start",
        in_specs=[HBM] * n + [ANY],
        out_specs=[SEM, SEM] + [HBM] * n + [pl.BlockSpec(memory_space=pltpu.VMEM)],
        out_shape=[pltpu.SemaphoreType.DMA((3 * n,)), pltpu.SemaphoreType.DMA((3 * n,))]
        + [pltpu.HBM(b.shape, b.dtype) for b in bufs] + [jax.ShapeDtypeStruct((8, LANES), F32)],
        input_output_aliases={t: 2 + t for t in range(n)},
        compiler_params=pltpu.CompilerParams(has_side_effects=EFFECT),
    )(*[_hbm(b) for b in bufs], after)
    return res[0], res[1], list(res[2:2 + n]), res[-1]


def gather_wait(bufs, send_sems, recv_sems, after, first=0):
    n = len(bufs)

    def body(*refs):
        g_refs = refs[:n]
        send_sems, recv_sems = refs[n], refs[n + 1]
        x, y, c = _place()
        me_p = 2 * x + y
        for t in range(n):
            for k, (qx, qy) in enumerate(_other_chips(x, y)):
                s = 3 * (first + t) + k
                cp = pltpu.make_async_remote_copy(src_ref=g_refs[t].at[me_p, c], dst_ref=g_refs[t].at[2 * qx + qy, c],
                                                  send_sem=send_sems.at[s], recv_sem=recv_sems.at[s],
                                                  device_id=(qx, qy, c), device_id_type=MESH)
                cp.wait_send()
                cp.wait_recv()

    return pl.pallas_call(
        body, name="gather_wait",
        in_specs=[HBM] * n + [SEM, SEM, ANY],
        out_specs=[HBM] * n,
        out_shape=[pltpu.HBM(b.shape, b.dtype) for b in bufs],
        input_output_aliases={t: t for t in range(n)},
        compiler_params=pltpu.CompilerParams(has_side_effects=EFFECT),
    )(*bufs, send_sems, recv_sems, after)


def gather_forward(bufs):
    n = len(bufs)

    def body(*refs):
        g_refs = refs[n:2 * n]
        send_sems, recv_sems = refs[2 * n:]
        x, y, c = _place()
        sibling = (x, y, 1 - c)
        chips = _other_chips(x, y)
        passed = []
        for t in range(n):
            for k, (qx, qy) in enumerate(chips):
                slab = g_refs[t].at[2 * qx + qy, c]
                fwd = pltpu.make_async_remote_copy(src_ref=slab, dst_ref=slab, send_sem=send_sems.at[3 * t + k],
                                                   recv_sem=recv_sems.at[3 * t + k], device_id=sibling,
                                                   device_id_type=MESH)
                fwd.start()
                passed.append(fwd)
        for t in range(n):
            for k, (qx, qy) in enumerate(chips):
                slab = g_refs[t].at[2 * qx + qy, 1 - c]
                pltpu.make_async_remote_copy(src_ref=slab, dst_ref=slab, send_sem=send_sems.at[3 * t + k],
                                             recv_sem=recv_sems.at[3 * t + k], device_id=sibling,
                                             device_id_type=MESH).wait_recv()
        for cp in passed:
            cp.wait_send()

    return pl.pallas_call(
        body, name="gather_forward",
        in_specs=[ANY] * n, out_specs=[ANY] * n,
        out_shape=[jax.ShapeDtypeStruct(b.shape, b.dtype) for b in bufs],
        input_output_aliases={t: t for t in range(n)},
        scratch_shapes=[pltpu.SemaphoreType.DMA((3 * n,)), pltpu.SemaphoreType.DMA((3 * n,))],
    )(*bufs)


def forward_start(bufs):
    n = len(bufs)

    def body(*refs):
        g_refs = refs[:n]
        send_sems, recv_sems = refs[n], refs[n + 1]
        token = refs[-1]
        x, y, c = _place()
        for t in range(n):
            for k, (qx, qy) in enumerate(_other_chips(x, y)):
                slab = g_refs[t].at[2 * qx + qy, c]
                pltpu.make_async_remote_copy(src_ref=slab, dst_ref=slab, send_sem=send_sems.at[3 * t + k],
                                             recv_sem=recv_sems.at[3 * t + k], device_id=(x, y, 1 - c),
                                             device_id_type=MESH).start()
        token[...] = jnp.zeros_like(token)

    res = pl.pallas_call(
        body, name="forward_start",
        in_specs=[HBM] * n,
        out_specs=[SEM, SEM] + [HBM] * n + [pl.BlockSpec(memory_space=pltpu.VMEM)],
        out_shape=[pltpu.SemaphoreType.DMA((3 * n,)), pltpu.SemaphoreType.DMA((3 * n,))]
        + [pltpu.HBM(b.shape, b.dtype) for b in bufs] + [jax.ShapeDtypeStruct((8, LANES), F32)],
        input_output_aliases={t: 2 + t for t in range(n)},
        compiler_params=pltpu.CompilerParams(has_side_effects=EFFECT),
    )(*[_hbm(b) for b in bufs])
    return res[0], res[1], list(res[2:2 + n]), res[-1]


def forward_wait(bufs, send_sems, recv_sems, after):
    n = len(bufs)

    def body(*refs):
        g_refs = refs[:n]
        send_sems, recv_sems = refs[n], refs[n + 1]
        x, y, c = _place()
        for t in range(n):
            for k, (qx, qy) in enumerate(_other_chips(x, y)):
                cp = pltpu.make_async_remote_copy(src_ref=g_refs[t].at[2 * qx + qy, c],
                                                  dst_ref=g_refs[t].at[2 * qx + qy, 1 - c],
                                                  send_sem=send_sems.at[3 * t + k], recv_sem=recv_sems.at[3 * t + k],
                                                  device_id=(x, y, 1 - c), device_id_type=MESH)
                cp.wait_send()
                cp.wait_recv()

    return pl.pallas_call(
        body, name="forward_wait",
        in_specs=[HBM] * n + [SEM, SEM, ANY],
        out_specs=[HBM] * n,
        out_shape=[pltpu.HBM(b.shape, b.dtype) for b in bufs],
        input_output_aliases={t: t for t in range(n)},
        compiler_params=pltpu.CompilerParams(has_side_effects=EFFECT),
    )(*bufs, send_sems, recv_sems, after)


def exchange_small(pack, after):
    def body(p_ref, after_ref, o_ref, send_sems, recv_sems, local_sem):
        x, y, c = _place()
        me = 4 * x + 2 * y + c
        mine = pltpu.make_async_copy(p_ref, o_ref.at[me], local_sem)
        mine.start()
        sends = []
        for k in range(1, 8):
            fx, fy, fc = (k >> 2) & 1, (k >> 1) & 1, k & 1
            peer = (x ^ fx, y ^ fy, c ^ fc)
            cp = pltpu.make_async_remote_copy(src_ref=p_ref, dst_ref=o_ref.at[me], send_sem=send_sems.at[k - 1],
                                              recv_sem=recv_sems.at[k - 1], device_id=peer, device_id_type=MESH)
            cp.start()
            sends.append(cp)
        for k in range(1, 8):
            fx, fy, fc = (k >> 2) & 1, (k >> 1) & 1, k & 1
            peer = (x ^ fx, y ^ fy, c ^ fc)
            slot = o_ref.at[4 * peer[0] + 2 * peer[1] + peer[2]]
            pltpu.make_async_remote_copy(src_ref=slot, dst_ref=slot, send_sem=send_sems.at[k - 1],
                                         recv_sem=recv_sems.at[k - 1], device_id=peer, device_id_type=MESH).wait_recv()
        for cp in sends:
            cp.wait_send()
        mine.wait()

    return pl.pallas_call(
        body, name="exchange_small",
        in_specs=[ANY, ANY], out_specs=ANY,
        out_shape=jax.ShapeDtypeStruct((8,) + pack.shape, pack.dtype),
        scratch_shapes=[pltpu.SemaphoreType.DMA((7,)), pltpu.SemaphoreType.DMA((7,)), pltpu.SemaphoreType.DMA],
    )(pack, after)


def small_start(pack, me, after):
    buf = lax.dynamic_update_slice(lax.empty((8,) + pack.shape, pack.dtype), pack[None], (me, 0, 0))

    def body(b_ref, after_ref, send_sems, recv_sems, thru, token):
        x, y, c = _place()
        slot = b_ref.at[4 * x + 2 * y + c]
        for k in range(1, 8):
            peer = (x ^ ((k >> 2) & 1), y ^ ((k >> 1) & 1), c ^ (k & 1))
            pltpu.make_async_remote_copy(src_ref=slot, dst_ref=slot, send_sem=send_sems.at[k - 1],
                                         recv_sem=recv_sems.at[k - 1], device_id=peer, device_id_type=MESH).start()
        token[...] = jnp.zeros_like(token)

    return pl.pallas_call(
        body, name="small_start",
        in_specs=[HBM, ANY],
        out_specs=[SEM, SEM, HBM, pl.BlockSpec(memory_space=pltpu.VMEM)],
        out_shape=[pltpu.SemaphoreType.DMA((7,)), pltpu.SemaphoreType.DMA((7,)), pltpu.HBM(buf.shape, buf.dtype),
                   jax.ShapeDtypeStruct((8, LANES), F32)],
        input_output_aliases={0: 2},
        compiler_params=pltpu.CompilerParams(has_side_effects=EFFECT),
    )(_hbm(buf), after)


def small_wait(buf, send_sems, recv_sems, after):
    def body(b_ref, send_sems, recv_sems, after_ref, thru):
        x, y, c = _place()
        mine = b_ref.at[4 * x + 2 * y + c]
        for k in range(1, 8):
            peer = (x ^ ((k >> 2) & 1), y ^ ((k >> 1) & 1), c ^ (k & 1))
            cp = pltpu.make_async_remote_copy(src_ref=mine, dst_ref=b_ref.at[4 * peer[0] + 2 * peer[1] + peer[2]],
                                              send_sem=send_sems.at[k - 1], recv_sem=recv_sems.at[k - 1],
                                              device_id=peer, device_id_type=MESH)
            cp.wait_send()
            cp.wait_recv()

    return pl.pallas_call(
        body, name="small_wait",
        in_specs=[HBM, SEM, SEM, ANY], out_specs=HBM,
        out_shape=pltpu.HBM(buf.shape, buf.dtype),
        input_output_aliases={0: 0},
        compiler_params=pltpu.CompilerParams(has_side_effects=EFFECT),
    )(buf, send_sems, recv_sems, after)


def swap_start(grads, after):
    n = len(grads)

    def body(*refs):
        g_refs, l_refs = refs[:n], refs[n:2 * n]
        send_sems, recv_sems = refs[2 * n + 1], refs[2 * n + 2]
        token = refs[-1]
        x, y, c = _place()
        for t in range(n):
            for p in range(N_CHIPS):
                pltpu.make_async_remote_copy(src_ref=g_refs[t].at[p, 1 - c], dst_ref=l_refs[t].at[p],
                                             send_sem=send_sems.at[N_CHIPS * t + p],
                                             recv_sem=recv_sems.at[N_CHIPS * t + p],
                                             device_id=(x, y, 1 - c), device_id_type=MESH).start()
        token[...] = jnp.zeros_like(token)

    lands = [lax.empty((N_CHIPS,) + g.shape[2:], g.dtype) for g in grads]
    res = pl.pallas_call(
        body, name="swap_start",
        in_specs=[HBM] * (2 * n) + [ANY],
        out_specs=[SEM, SEM] + [HBM] * (2 * n) + [pl.BlockSpec(memory_space=pltpu.VMEM)],
        out_shape=[pltpu.SemaphoreType.DMA((N_CHIPS * n,)), pltpu.SemaphoreType.DMA((N_CHIPS * n,))]
        + [pltpu.HBM(a.shape, a.dtype) for a in grads + lands] + [jax.ShapeDtypeStruct((8, LANES), F32)],
        input_output_aliases={t: 2 + t for t in range(2 * n)},
        compiler_params=pltpu.CompilerParams(has_side_effects=EFFECT),
    )(*[_hbm(a) for a in grads + lands], after)
    return res[0], res[1], list(res[2:2 + n]), list(res[2 + n:2 + 2 * n]), res[-1]


def swap_wait(grads, lands, send_sems, recv_sems, after):
    n = len(grads)

    def body(*refs):
        g_refs, l_refs = refs[:n], refs[n:2 * n]
        send_sems, recv_sems = refs[2 * n], refs[2 * n + 1]
        x, y, c = _place()
        for t in range(n):
            for p in range(N_CHIPS):
                cp = pltpu.make_async_remote_copy(src_ref=g_refs[t].at[p, 1 - c], dst_ref=l_refs[t].at[p],
                                                  send_sem=send_sems.at[N_CHIPS * t + p],
                                                  recv_sem=recv_sems.at[N_CHIPS * t + p],
                                                  device_id=(x, y, 1 - c), device_id_type=MESH)
                cp.wait_send()
                cp.wait_recv()

    res = pl.pallas_call(
        body, name="swap_wait",
        in_specs=[HBM] * (2 * n) + [SEM, SEM, ANY],
        out_specs=[HBM] * (2 * n),
        out_shape=[pltpu.HBM(a.shape, a.dtype) for a in grads + lands],
        input_output_aliases={t: t for t in range(2 * n)},
        compiler_params=pltpu.CompilerParams(has_side_effects=EFFECT),
    )(*grads, *lands, send_sems, recv_sems, after)
    return list(res[:n]), list(res[n:])


def _row_tile(r):
    return max(t for t in range(16, 513, 16) if r % t == 0)


def add_own_half(g, other, c_arr):
    _, _, r, cols = g.shape
    tr = _row_tile(r)

    def body(c_ref, a_ref, b_ref, o_ref):
        o_ref[...] = (a_ref[...] + b_ref[...]).astype(BF16)

    return pl.pallas_call(
        body, name="add_own_half",
        grid_spec=pltpu.PrefetchScalarGridSpec(
            num_scalar_prefetch=1, grid=(N_CHIPS, r // tr),
            in_specs=[pl.BlockSpec((None, None, tr, cols), lambda p, i, c_ref: (p, c_ref[0], i, 0)),
                      pl.BlockSpec((None, tr, cols), lambda p, i, c_ref: (p, i, 0))],
            out_specs=pl.BlockSpec((None, tr, cols), lambda p, i, c_ref: (p, i, 0))),
        out_shape=jax.ShapeDtypeStruct((N_CHIPS, r, cols), BF16),
        compiler_params=_cparams("parallel", "parallel"),
    )(c_arr, g, other)


def scatter_start(partials):
    n = len(partials)

    def body(*refs):
        s_refs, l_refs = refs[:n], refs[n:2 * n]
        send_sems, recv_sems = refs[2 * n], refs[2 * n + 1]
        token = refs[-1]
        x, y, c = _place()
        for t in range(n):
            for k, (qx, qy) in enumerate(_other_chips(x, y)):
                pltpu.make_async_remote_copy(src_ref=s_refs[t].at[2 * qx + qy], dst_ref=l_refs[t].at[k],
                                             send_sem=send_sems.at[3 * t + k], recv_sem=recv_sems.at[3 * t + k],
                                             device_id=(qx, qy, c), device_id_type=MESH).start()
        token[...] = jnp.zeros_like(token)

    lands = [lax.empty((3,) + s.shape[1:], s.dtype) for s in partials]
    res = pl.pallas_call(
        body, name="scatter_start",
        in_specs=[HBM] * (2 * n),
        out_specs=[SEM, SEM] + [HBM] * (2 * n) + [pl.BlockSpec(memory_space=pltpu.VMEM)],
        out_shape=[pltpu.SemaphoreType.DMA((3 * n,)), pltpu.SemaphoreType.DMA((3 * n,))]
        + [pltpu.HBM(a.shape, a.dtype) for a in partials + lands] + [jax.ShapeDtypeStruct((8, LANES), F32)],
        input_output_aliases={t: 2 + t for t in range(2 * n)},
        compiler_params=pltpu.CompilerParams(has_side_effects=EFFECT),
    )(*[_hbm(a) for a in partials + lands])
    return res[0], res[1], list(res[2:2 + n]), list(res[2 + n:2 + 2 * n]), res[-1]


def scatter_wait(partials, lands, send_sems, recv_sems, after):
    n = len(partials)

    def body(*refs):
        s_refs, l_refs = refs[:n], refs[n:2 * n]
        send_sems, recv_sems = refs[2 * n], refs[2 * n + 1]
        x, y, c = _place()
        for t in range(n):
            for k, (qx, qy) in enumerate(_other_chips(x, y)):
                cp = pltpu.make_async_remote_copy(src_ref=s_refs[t].at[2 * qx + qy], dst_ref=l_refs[t].at[k],
                                                  send_sem=send_sems.at[3 * t + k], recv_sem=recv_sems.at[3 * t + k],
                                                  device_id=(qx, qy, c), device_id_type=MESH)
                cp.wait_send()
                cp.wait_recv()

    res = pl.pallas_call(
        body, name="scatter_wait",
        in_specs=[HBM] * (2 * n) + [SEM, SEM, ANY],
        out_specs=[HBM] * (2 * n),
        out_shape=[pltpu.HBM(a.shape, a.dtype) for a in partials + lands],
        input_output_aliases={t: t for t in range(2 * n)},
        compiler_params=pltpu.CompilerParams(has_side_effects=EFFECT),
    )(*partials, *lands, send_sems, recv_sems, after)
    return list(res[:n]), list(res[n:])


def sum_chips(own, parts, where):
    _, r, cols = own.shape
    tr = _row_tile(r)

    def body(w_ref, a_ref, p_ref, o_ref):
        acc = a_ref[...].astype(F32)
        for k in range(3):
            acc = acc + p_ref[k].astype(F32)
        o_ref[...] = acc

    return pl.pallas_call(
        body, name="sum_chips",
        grid_spec=pltpu.PrefetchScalarGridSpec(
            num_scalar_prefetch=1, grid=(r // tr,),
            in_specs=[pl.BlockSpec((None, tr, cols), lambda i, w_ref: (w_ref[0], i, 0)),
                      pl.BlockSpec((3, tr, cols), lambda i, w_ref: (0, i, 0))],
            out_specs=pl.BlockSpec((None, tr, cols), lambda i, w_ref: (w_ref[1], i, 0))),
        out_shape=jax.ShapeDtypeStruct((DEPTH, r, cols), F32),
        compiler_params=_cparams("parallel"),
    )(where, own, parts)


def sibling_share_layer(bufs):
    n = len(bufs)

    def body(*refs):
        o_refs = refs[n:2 * n]
        send_sems, recv_sems = refs[2 * n:]
        x, y, c = _place()
        cps = []
        for t in range(n):
            cp = pltpu.make_async_remote_copy(src_ref=o_refs[t].at[c], dst_ref=o_refs[t].at[c], send_sem=send_sems.at[t],
                                              recv_sem=recv_sems.at[t], device_id=(x, y, 1 - c), device_id_type=MESH)
            cp.start()
            cps.append(cp)
        for t in range(n):
            slot = o_refs[t].at[1 - c]
            pltpu.make_async_remote_copy(src_ref=slot, dst_ref=slot, send_sem=send_sems.at[t], recv_sem=recv_sems.at[t],
                                         device_id=(x, y, 1 - c), device_id_type=MESH).wait_recv()
        for cp in cps:
            cp.wait_send()

    return pl.pallas_call(
        body, name="sibling_share_layer",
        in_specs=[ANY] * n, out_specs=[ANY] * n,
        out_shape=[jax.ShapeDtypeStruct(b.shape, b.dtype) for b in bufs],
        input_output_aliases={t: t for t in range(n)},
        scratch_shapes=[pltpu.SemaphoreType.DMA((n,)), pltpu.SemaphoreType.DMA((n,))],
    )(*bufs)


SP_META = 2 * (N_META * D_MODEL // LANES)
SP_NORM = DEPTH * D_MODEL // LANES
SP_RB = DEPTH * N_BUCKETS
SP_SINK = DEPTH * ATT_HEADS
SP_CONV = DEPTH * 3 * BRANCH_WIDTH // LANES
SP_LOSS = 8
SP_ROWS = SP_META + 2 * SP_NORM + SP_RB + SP_SINK + SP_CONV + SP_LOSS


def sum_small(slots):
    half = SP_META // 2
    rb0 = SP_META + 2 * SP_NORM
    rest_rows = SP_ROWS - SP_META

    def body(s_ref, meta_ref, rest_ref):
        acc = s_ref[0]
        for d in range(1, 8):
            acc = acc + s_ref[d]
        meta_ref[...] = acc[0:half] + acc[half:SP_META]
        rest_ref[...] = acc[SP_META:]
        rest_ref[rb0 - SP_META:rb0 - SP_META + N_BUCKETS, :] = (
            acc[rb0:rb0 + N_BUCKETS] + acc[rb0 + N_BUCKETS:rb0 + 2 * N_BUCKETS])

    vm = pl.BlockSpec(memory_space=pltpu.VMEM)
    return pl.pallas_call(
        body, name="sum_small",
        in_specs=[vm], out_specs=[vm, vm],
        out_shape=[jax.ShapeDtypeStruct((half, LANES), F32), jax.ShapeDtypeStruct((rest_rows, LANES), F32)],
    )(slots)


def local_step(x, loss_target, meta_full, rel_bias, norm_pre, conv_w_full, attn_sinks, norm_post, weights_of, mid_fwd,
               grads_done, bwd_done):
    nb, seq, _ = x.shape
    nc = seq // BLOCK + 1
    lp = nc * BLOCK
    rows = nb * lp
    pad = jnp.zeros((nb, PAD_FRONT, D_MODEL), F32)
    meta = jnp.broadcast_to(meta_full[None], (nb, N_META, D_MODEL))
    h0 = jnp.concatenate([pad, meta, x], axis=1).reshape(rows, D_MODEL)
    cosf, sinf = _rot_tables(lp)
    bkt = jnp.asarray(_bucket_table())

    g_pre = norm_pre.reshape(DEPTH, 1, D_MODEL)
    g_post = norm_post.reshape(DEPTH, 1, D_MODEL)
    order = lambda token: bkt if token is None else token

    acts = []
    h = h0
    for l in range(DEPTH):
        w_in, token = weights_of(l, h)
        hb, p_abc = norm_matmul(h, g_pre, l, w_in, 0, N_ABC_TILES, order(token))
        p_m = matmul_cols(hb, w_in, N_ABC_TILES, N_M_TILES)
        br, states = mixers_fwd(p_abc, cosf, sinf, bkt, rel_bias, attn_sinks, conv_w_full, l, nb, nc)
        (w_br, w_out), token = mid_fwd(l, br)
        h_next = merge_fwd(h, br, p_m, w_br, w_out, g_post, l, order(token))
        acts.append((h, hb, p_abc, p_m, br, states, w_in, w_br, w_out))
        h = h_next

    loss_part, d_h = loss_head(h, loss_target.reshape(nb * seq, D_MODEL), nb, nc)

    small = [None] * DEPTH
    token = None
    for l in reversed(range(DEPTH)):
        h_in, hb, p_abc, p_m, br, states, w_in, w_br, w_out = acts[l]
        d_br, d_m, d_gpost, g_wbr, g_wout = merge_bwd(d_h, br, p_m, w_br, w_out, g_post, l, order(token))
        d_abc, d_rb, d_sk, d_cw = mixers_bwd(p_abc, d_br, states, cosf, sinf, bkt, rel_bias,
                                             attn_sinks, conv_w_full, l, nb, nc)
        g_win = proj_wgrad(hb, d_abc, d_m)
        token = grads_done(l, [g_win, g_wbr, g_wout])
        d_h, d_gpre = proj_dgrad(d_abc, d_m, w_in, h_in, g_pre, l, d_h, order(token))
        token = bwd_done(l, d_h)
        small[l] = (d_gpre[0], d_gpost[0], d_rb, d_sk, d_cw[0:3])

    d_h3 = d_h.reshape(nb, lp, D_MODEL)
    d_x = d_h3[:, BLOCK:]
    d_meta = d_h3[:, PAD_FRONT:BLOCK]
    sp = jnp.concatenate([
        d_meta.reshape(-1, LANES),
        jnp.stack([small[l][0] for l in range(DEPTH)]).reshape(-1, LANES),
        jnp.stack([small[l][1] for l in range(DEPTH)]).reshape(-1, LANES),
        jnp.concatenate([small[l][2] for l in range(DEPTH)], axis=0),
        jnp.concatenate([small[l][3] for l in range(DEPTH)], axis=0),
        jnp.stack([small[l][4] for l in range(DEPTH)]).reshape(-1, LANES),
        loss_part], axis=0)
    return d_x, sp


def kernel(x, meta_tokens, rel_bias, norm_pre, w_in, conv_w, attn_sinks, w_branch, w_out, norm_post, loss_target, m_meta_tokens, m_rel_bias, m_norm_pre, m_w_in, m_conv_w, m_attn_sinks, m_w_branch, m_w_out, m_norm_post, v_meta_tokens, v_rel_bias, v_norm_pre, v_w_in, v_conv_w, v_attn_sinks, v_w_branch, v_w_out, v_norm_post):
    assert x.shape[0] == 2 and SP_META == 2 * N_META * D_MODEL // LANES
    px, py, pc = _place()
    chip = 2 * px + py

    c_arr = jnp.reshape(pc, (1,)).astype(jnp.int32)
    where = jnp.stack([chip, pc]).astype(jnp.int32)
    tr_ = lambda a: jnp.swapaxes(a, 1, 2)
    w3 = [tr_(w_in), w_branch.reshape(DEPTH, N_BRANCH * BRANCH_WIDTH, SHARD_D), w_out]
    halves = lambda a: a.reshape(2, a.shape[0] // 2, a.shape[1])

    def as_weights(bufs):
        a_in, a_br, a_out = bufs
        return (a_in.reshape(PROJ_WIDTH, D_MODEL), a_br.reshape(N_CHIPS, N_BRANCH, BRANCH_WIDTH, SHARD_D),
                a_out.reshape(D_MODEL, D_MODEL))

    side = jnp.concatenate([meta_tokens.reshape(-1), conv_w.reshape(-1)]).reshape(-1, LANES)
    side = jnp.concatenate([side, jnp.zeros((40 - side.shape[0], LANES), F32)], axis=0)
    side_all = exchange_small(side, side)
    side_chips = side_all[0::2]
    n_meta_rows = N_META * SHARD_D // LANES
    meta_full = jnp.moveaxis(side_chips[:, :n_meta_rows].reshape(N_CHIPS, N_META, SHARD_D), 0, 1).reshape(N_META, D_MODEL)
    conv_full = jnp.moveaxis(side_chips[:, n_meta_rows:n_meta_rows + 6].reshape(N_CHIPS, DEPTH, 3, LANES), 0, 2).reshape(DEPTH, 3, BRANCH_WIDTH)

    slots = [[_own_slot(halves(w[l].astype(BF16)), chip) for w in w3] for l in range(DEPTH)]
    send0, recv0, flying0, started0 = gather_start(slots[0], side_all)
    meta_full = meta_full + started0[0:1, 0:1]
    inbound = {}

    def weights_of(l, h):
        if l == 0:
            inbound[0] = gather_forward(gather_wait(flying0[:1], send0, recv0, h))
            inbound[1] = gather_start(slots[1], inbound[0][0])
            return inbound[0][0].reshape(PROJ_WIDTH, D_MODEL), inbound[1][3]
        send, recv, thru = inbound[1]
        inbound[1] = as_weights(forward_wait(thru, send, recv, h))
        return inbound[1][0], None

    def mid_fwd(l, br):
        if l == 0:
            rest = gather_forward(gather_wait(flying0[1:], send0, recv0, br, first=1))
            send, recv, flying1, _ = inbound[1]
            send, recv, thru, started = forward_start(gather_wait(flying1, send, recv, rest[0]))
            inbound[1] = (send, recv, thru)
            return as_weights(inbound[0] + rest)[1:], started
        return inbound[1][1:], None

    reduced = [None] * DEPTH
    flying = {}

    def finish_reduce(l, after):
        partials, parts = scatter_wait(*flying[l], after)
        reduced[l] = sibling_share_layer([sum_chips(a, p, where) for a, p in zip(partials, parts)])

    def start_scatter(l, full, others):
        send, recv, thru, lands, started = scatter_start([add_own_half(g, o, c_arr) for g, o in zip(full, others)])
        flying[l] = (thru, lands, send, recv)
        return started

    m3 = [tr_(m_w_in), m_w_branch.reshape(w3[1].shape), m_w_out]
    v3 = [tr_(v_w_in), v_w_branch.reshape(w3[1].shape), v_w_out]
    big = [None] * 3

    def adamw_of(l, after):
        for t in range(3):
            big[t] = adamw_layer(w3[t], reduced[l][t].reshape(w3[t].shape[1:]), m3[t], v3[t], l, big[t], after)

    def grads_done(l, grads):
        full = [g.reshape(N_CHIPS, 2, g.size // (2 * N_CHIPS * g.shape[-1]), g.shape[-1]) for g in grads]
        if l == 0:
            finish_reduce(1, grads[0])
        send, recv, thru, lands, started = swap_start(full, where if l == 1 else reduced[1][0])
        if l == 1:
            flying["swap"] = (thru, lands, send, recv)
            return started
        adamw_of(1, started)
        return start_scatter(0, *swap_wait(thru, lands, send, recv, big[0][1]))

    def bwd_done(l, d_h):
        if l == 1:
            return start_scatter(1, *swap_wait(*flying["swap"], d_h))
        return None

    d_x, sp = local_step(x, loss_target, meta_full, rel_bias, norm_pre, conv_full, attn_sinks, norm_post,
                         weights_of, mid_fwd, grads_done, bwd_done)
    finish_reduce(0, sp)

    s_send, s_recv, s_buf, s_started = small_start(sp, 4 * px + 2 * py + pc, reduced[0][0])

    adamw_of(0, s_started)
    g_in, *u_in = [tr_(a) for a in big[0]]
    g_br, *u_br = [a.reshape(w_branch.shape) for a in big[1]]
    g_out, *u_out = big[2]

    meta_rows, rest = sum_small(small_wait(s_buf, s_send, s_recv, big[0][1]))
    o = 0
    g_meta_full = meta_rows.reshape(N_META, D_MODEL)
    g_norm_pre = rest[o:o + SP_NORM].reshape(DEPTH, D_MODEL); o += SP_NORM
    g_norm_post = rest[o:o + SP_NORM].reshape(DEPTH, D_MODEL); o += SP_NORM
    g_rel_bias = rest[o:o + N_BUCKETS, :ATT_HEADS]; o += SP_RB
    g_sinks = rest[o:o + SP_SINK, 0].reshape(DEPTH, ATT_HEADS); o += SP_SINK
    g_conv_full = rest[o:o + SP_CONV].reshape(DEPTH, 3, BRANCH_WIDTH); o += SP_CONV
    loss = rest[o, 0]
    g_meta = lax.dynamic_slice_in_dim(g_meta_full, chip * SHARD_D, SHARD_D, axis=1)
    g_conv = lax.dynamic_slice_in_dim(g_conv_full, chip * LANES, LANES, axis=2)

    to2 = lambda a: a.reshape(-1, a.shape[-1])
    smalls = [(meta_tokens, g_meta, m_meta_tokens, v_meta_tokens),
              (rel_bias, g_rel_bias, m_rel_bias, v_rel_bias),
              (norm_pre, g_norm_pre, m_norm_pre, v_norm_pre),
              (to2(conv_w), to2(g_conv), to2(m_conv_w), to2(v_conv_w)),
              (attn_sinks, g_sinks, m_attn_sinks, v_attn_sinks),
              (norm_post, g_norm_post, m_norm_post, v_norm_post)]
    u_meta, u_rb, u_npre, u_conv, u_sink, u_npost = adamw_small(smalls)
    u_conv = tuple(a.reshape(conv_w.shape) for a in u_conv)

    grads = [g_meta, g_rel_bias, g_norm_pre, g_in, g_conv, g_sinks, g_br, g_out, g_norm_post]
    upd = [u_meta, u_rb, u_npre, u_in, u_conv, u_sink, u_br, u_out, u_npost]
    return (loss, d_x, *grads, *[u[0] for u in upd], *[u[1] for u in upd], *[u[2] for u in upd])
```

```python
import functools
import math

import numpy as np
import jax
import jax.numpy as jnp
from jax import lax
from jax.experimental import pallas as pl
from jax.experimental.pallas import tpu as pltpu

F32 = jnp.float32
BF16 = jnp.bfloat16
MESH = pl.DeviceIdType.MESH

D_MODEL = 1024
DEPTH = 2
N_META = 16
BLOCK = 128
PAD_FRONT = BLOCK - N_META
ATT_HEADS = 8
ATT_HEAD_DIM = 64
N_BUCKETS = 32
MAX_EXACT = 16
MAX_DISTANCE = 128
RET_HEADS = 4
ROT_BASE = 10000.0
N_BRANCH = 3
BRANCH_WIDTH = 512
PROJ_WIDTH = 8448
ABC_WIDTH = 5376
MERGE_WIDTH = N_BRANCH * D_MODEL
RMS_EPS = 1e-6
GN_EPS = 1e-6
NEG_INF = -1e30
ATT_SCALE = ATT_HEAD_DIM ** -0.5
RET_SCALE = BLOCK ** -0.5
LOG_GAMMA = tuple(math.log1p(-(2.0 ** (-5.0 - h))) for h in range(RET_HEADS))

C_AQ, C_AK, C_AV, C_AG = 0, 512, 640, 768
C_RQ, C_RK, C_RV, C_RG = 1280, 1792, 2304, 2816
C_CB, C_CC, C_CX, C_CG = 3328, 3840, 4352, 4864

ADAM_LR = 0.001
ADAM_B1 = 0.9
ADAM_B2 = 0.999
ADAM_EPS = 1e-08
ADAM_WD = 0.01
ADAM_STEP = 10

N_CHIPS = 4
SHARD_IN = PROJ_WIDTH // N_CHIPS
SHARD_D = D_MODEL // N_CHIPS
LANES = 128
PACK_IN = D_MODEL * SHARD_IN
PACK_BR = N_BRANCH * BRANCH_WIDTH * SHARD_D
PACK_OUT = SHARD_D * D_MODEL
PACK_ROWS = (PACK_IN + PACK_BR + PACK_OUT) // LANES

VMEM_LIMIT = 56 * 1024 * 1024
COL_TILE = 768
ROW_TILE = 1088
PROJ_ROW_TILE = 2176


def _cparams(*sem):
    return pltpu.CompilerParams(dimension_semantics=sem, vmem_limit_bytes=VMEM_LIMIT)


def _nt(a, b):
    return lax.dot_general(a, b, (((1,), (1,)), ((), ())), preferred_element_type=F32)


def _tn(a, b):
    return lax.dot_general(a, b, (((0,), (0,)), ((), ())), preferred_element_type=F32)


def _nn(a, b):
    return jnp.dot(a, b, preferred_element_type=F32)


def _sigmoid(x):
    return 0.5 * jnp.tanh(0.5 * x) + 0.5


def _silu(x):
    return x * _sigmoid(x)


def _dsilu(x):
    s = _sigmoid(x)
    return s * (1.0 + x * (1.0 - s))


def _bucket_table():
    r = np.arange(BLOCK)[:, None]
    c = np.arange(2 * BLOCK)[None, :]
    n = np.maximum(BLOCK + r - c, 0)
    nf = np.maximum(n, 1).astype(np.float32)
    large = MAX_EXACT + (np.log(nf / MAX_EXACT) / math.log(MAX_DISTANCE / MAX_EXACT)
                         * (N_BUCKETS - MAX_EXACT)).astype(np.int32)
    large = np.minimum(large, N_BUCKETS - 1)
    return np.where(n < MAX_EXACT, n, large).astype(np.int32)


def _rot_tables(lp):
    half = BLOCK // 2
    pos = (jnp.arange(lp) - PAD_FRONT).astype(F32)
    theta = 1.0 / (ROT_BASE ** jnp.linspace(0.0, 1.0, half, dtype=F32))
    ang = pos[:, None] * theta[None, :]
    cos, sin = jnp.cos(ang), jnp.sin(ang)
    return jnp.concatenate([cos, cos], axis=1), jnp.concatenate([-sin, sin], axis=1)


def norm_matmul(x2d, g, layer, w, col0_blocks, n_col_blocks, after):
    t = x2d.shape[0]
    tm = PROJ_ROW_TILE if t % PROJ_ROW_TILE == 0 else BLOCK

    def body(x_ref, g_ref, w_ref, after_ref, hb_ref, o_ref):
        @pl.when(pl.program_id(1) == 0)
        def _():
            x = x_ref[...]
            r = lax.rsqrt(jnp.mean(x * x, axis=-1, keepdims=True) + RMS_EPS)
            hb_ref[...] = (x * r * g_ref[...]).astype(BF16)

        o_ref[...] = _nt(hb_ref[...], w_ref[...]).astype(BF16)

    return pl.pallas_call(
        body, name="norm_matmul",
        grid=(t // tm, n_col_blocks),
        in_specs=[pl.BlockSpec((tm, D_MODEL), lambda i, j: (i, 0)),
                  pl.BlockSpec((None, 1, D_MODEL), lambda i, j: (layer, 0, 0)),
                  pl.BlockSpec((COL_TILE, D_MODEL), lambda i, j: (j + col0_blocks, 0)),
                  ANY],
        out_specs=[pl.BlockSpec((tm, D_MODEL), lambda i, j: (i, 0)),
                   pl.BlockSpec((tm, COL_TILE), lambda i, j: (i, j))],
        out_shape=[jax.ShapeDtypeStruct((t, D_MODEL), BF16),
                   jax.ShapeDtypeStruct((t, n_col_blocks * COL_TILE), BF16)],
        compiler_params=_cparams("parallel", "arbitrary"),
    )(x2d, g, w, after)


def matmul_cols(a, w, col0_blocks, n_col_blocks):
    t, k = a.shape
    tm = PROJ_ROW_TILE if t % PROJ_ROW_TILE == 0 else BLOCK

    def body(a_ref, w_ref, o_ref):
        o_ref[...] = _nt(a_ref[...], w_ref[...]).astype(BF16)

    return pl.pallas_call(
        body, name="matmul_cols",
        grid=(t // tm, n_col_blocks),
        in_specs=[pl.BlockSpec((tm, k), lambda i, j: (i, 0)),
                  pl.BlockSpec((COL_TILE, k), lambda i, j: (j + col0_blocks, 0))],
        out_specs=pl.BlockSpec((tm, COL_TILE), lambda i, j: (i, j)),
        out_shape=jax.ShapeDtypeStruct((t, n_col_blocks * COL_TILE), BF16),
        compiler_params=_cparams("parallel", "arbitrary"),
    )(a, w)


class _Widened:
    def __init__(self, ref):
        self.ref = ref

    def __getitem__(self, idx):
        return self.ref[idx].astype(F32)


def _build_bias(bkt_ref, rb_ref, bias_s):
    bkt = bkt_ref[...]
    for h in range(ATT_HEADS):
        acc = jnp.zeros((BLOCK, 2 * BLOCK), F32)
        for b in range(N_BUCKETS):
            acc = jnp.where(bkt == b, rb_ref[b, h], acc)
        bias_s[h] = acc


def _band_mask(n):
    r = lax.broadcasted_iota(jnp.int32, (BLOCK, 2 * BLOCK), 0)
    c = lax.broadcasted_iota(jnp.int32, (BLOCK, 2 * BLOCK), 1)
    key_pos = (n - 1) * BLOCK + c
    return (c > r) & (c <= r + BLOCK) & (key_pos >= PAD_FRONT)


def _split_heads(kv, kh):
    lane = lax.broadcasted_iota(jnp.int32, kv.shape, 1)
    if kh == 0:
        lo = jnp.where(lane < ATT_HEAD_DIM, kv, 0.0)
        hi = pltpu.roll(lo, ATT_HEAD_DIM, 1)
    else:
        hi = jnp.where(lane >= ATT_HEAD_DIM, kv, 0.0)
        lo = pltpu.roll(hi, ATT_HEAD_DIM, 1)
    return lo, hi


def _merge_heads(acc_lo, acc_hi, kh):
    lane = lax.broadcasted_iota(jnp.int32, acc_lo.shape, 1)
    if kh == 0:
        return jnp.where(lane < ATT_HEAD_DIM, acc_lo + pltpu.roll(acc_hi, ATT_HEAD_DIM, 1), 0.0)
    return jnp.where(lane >= ATT_HEAD_DIM, acc_hi + pltpu.roll(acc_lo, ATT_HEAD_DIM, 1), 0.0)


def _softmax_sink(q2b, kxb, bias_h, mask, sink_h):
    return _softmax_of(_nt(q2b, kxb), bias_h, mask, sink_h)


def _softmax_of(qk, bias_h, mask, sink_h):
    s = qk * ATT_SCALE + bias_h
    s = jnp.where(mask, s, NEG_INF)
    m = jnp.maximum(jnp.max(s, axis=-1, keepdims=True), sink_h)
    p = jnp.exp(s - m)
    es = jnp.exp(sink_h - m)
    inv = 1.0 / (jnp.sum(p, axis=-1, keepdims=True) + es)
    return p * inv, es * inv


def _rot(t, cosf, sinf):
    return t * cosf + pltpu.roll(t, BLOCK // 2, 1) * sinf


def _rot_t(d, cosf, sinf):
    return d * cosf + pltpu.roll(d * sinf, BLOCK // 2, 1)


def _decay_tables(h):
    lg = LOG_GAMMA[h]
    i = lax.broadcasted_iota(jnp.int32, (BLOCK, BLOCK), 0)
    j = lax.broadcasted_iota(jnp.int32, (BLOCK, BLOCK), 1)
    diff = (i - j).astype(F32)
    dm = jnp.where(diff >= 0, jnp.exp(diff * lg), 0.0)
    row = lax.broadcasted_iota(jnp.int32, (BLOCK, 1), 0).astype(F32)
    zeta = jnp.exp((BLOCK - 1 - row) * lg)
    xi = jnp.exp((row + 1.0) * lg)
    return dm, zeta, xi, math.exp(BLOCK * lg)


def _valid_col(n):
    row = lax.broadcasted_iota(jnp.int32, (BLOCK, 1), 0)
    return ((n * BLOCK + row) >= PAD_FRONT).astype(F32)


def _shift_down(cur, prev, k):
    row = lax.broadcasted_iota(jnp.int32, cur.shape, 0)
    return jnp.where(row >= k, pltpu.roll(cur, k, 0), pltpu.roll(prev, k, 0))


def _shift_up(cur, nxt, k):
    row = lax.broadcasted_iota(jnp.int32, cur.shape, 0)
    return jnp.where(row < BLOCK - k, pltpu.roll(cur, BLOCK - k, 0), pltpu.roll(nxt, BLOCK - k, 0))


def mixers_fwd(proj, cosf, sinf, bkt, rel_bias, sinks, conv_w, layer, nb, nc):
    def body(p_ref, cos_ref, sin_ref, bkt_ref, rb_ref, sk_ref, cw_ref, br_ref, st_ref,
             bias_s, kv_s, state_s, u_s):
        p_ref = _Widened(p_ref)
        n = pl.program_id(0)

        @pl.when(n == 0)
        def _():
            _build_bias(bkt_ref, rb_ref, bias_s)
            kv_s[:, 0:BLOCK, :] = jnp.zeros((nb, BLOCK, 2 * BLOCK), F32)
            state_s[...] = jnp.zeros_like(state_s)
            u_s[...] = jnp.zeros_like(u_s)

        valid = _valid_col(n)
        mask = _band_mask(n)
        ex = range(nb)

        for b in ex:
            kv_s[b, BLOCK:2 * BLOCK, :] = p_ref[b, :, C_AK:C_AK + 2 * BLOCK]
        for kh in range(2):
            ks = [[t.astype(BF16) for t in _split_heads(kv_s[b, :, 0:BLOCK], kh)] for b in ex]
            vs = [[t.astype(BF16) for t in _split_heads(kv_s[b, :, BLOCK:2 * BLOCK], kh)] for b in ex]
            pairs = [(b, 2 * kh + jj) for jj in range(2) for b in ex]
            subs = [(b, j, x) for (b, j) in pairs for x in range(2)]
            qb_ = {(b, j): p_ref[b, :, C_AQ + BLOCK * j:C_AQ + BLOCK * (j + 1)].astype(BF16) for (b, j) in pairs}
            qk_ = {(b, j, x): _nt(qb_[(b, j)], ks[b][x]) for (b, j, x) in subs}
            pb_ = {}
            for u in subs:
                h = 2 * u[1] + u[2]
                pb_[u] = _softmax_of(qk_[u], bias_s[h], mask, sk_ref[layer, h])[0].astype(BF16)
            o_ = {u: _nn(pb_[u], vs[u[0]][u[2]]) for u in subs}
            for (b, j) in pairs:
                gate = p_ref[b, :, C_AG + BLOCK * j:C_AG + BLOCK * (j + 1)]
                br_ref[b, :, BLOCK * j:BLOCK * (j + 1)] = ((o_[(b, j, 0)] + o_[(b, j, 1)]) * _silu(gate)).astype(BF16)
        for b in ex:
            kv_s[b, 0:BLOCK, :] = kv_s[b, BLOCK:2 * BLOCK, :]

        cosv = cos_ref[...]
        sinv = sin_ref[...]
        tabs = [_decay_tables(h) for h in range(RET_HEADS)]
        units = [(b, h) for h in range(RET_HEADS) for b in ex]
        sl = lambda c0, h: slice(c0 + BLOCK * h, c0 + BLOCK * (h + 1))
        q_, k_, v_, sp_ = {}, {}, {}, {}
        for u in units:
            b, h = u
            q_[u] = _rot(p_ref[b, :, sl(C_RQ, h)], cosv, sinv).astype(BF16)
            k_[u] = (_rot(p_ref[b, :, sl(C_RK, h)], cosv, sinv) * RET_SCALE * valid).astype(BF16)
            v_[u] = p_ref[b, :, sl(C_RV, h)]
            sp_[u] = state_s[b, h]
            st_ref[b, 0, h] = sp_[u]
        qk_ = {u: _nt(q_[u], k_[u]) for u in units}
        qs_ = {u: _nn(q_[u], sp_[u].astype(BF16)) for u in units}
        kv_ = {u: _tn(k_[u], (v_[u] * tabs[u[1]][1]).astype(BF16)) for u in units}
        a_ = {u: (qk_[u] * tabs[u[1]][0]).astype(BF16) for u in units}
        av_ = {u: _nn(a_[u], v_[u].astype(BF16)) for u in units}
        for u in units:
            b, h = u
            o = av_[u] + tabs[h][2] * qs_[u]
            mu = jnp.mean(o, axis=-1, keepdims=True)
            var = jnp.mean(jnp.square(o - mu), axis=-1, keepdims=True)
            oh = (o - mu) * lax.rsqrt(var + GN_EPS)
            gate = p_ref[b, :, sl(C_RG, h)]
            br_ref[b, :, BRANCH_WIDTH + BLOCK * h:BRANCH_WIDTH + BLOCK * (h + 1)] = (oh * _silu(gate)).astype(BF16)
            state_s[b, h] = tabs[h][3] * sp_[u] + kv_[u]

        for b in ex:
            u = p_ref[b, :, C_CC:C_CC + BRANCH_WIDTH] * p_ref[b, :, C_CX:C_CX + BRANCH_WIDTH] * valid
            u_prev = u_s[b]
            y = (cw_ref[0:1, :] * _shift_down(u, u_prev, 2) + cw_ref[1:2, :] * _shift_down(u, u_prev, 1)
                 + cw_ref[2:3, :] * u)
            yc = p_ref[b, :, C_CB:C_CB + BRANCH_WIDTH] * y * _silu(p_ref[b, :, C_CG:C_CG + BRANCH_WIDTH])
            br_ref[b, :, 2 * BRANCH_WIDTH:3 * BRANCH_WIDTH] = yc.astype(BF16)
            u_s[b] = u

    lp = nc * BLOCK
    smem = pl.BlockSpec(memory_space=pltpu.SMEM)
    br, states = pl.pallas_call(
        body, name="mixers_fwd",
        grid=(nc,),
        in_specs=[pl.BlockSpec((nb, BLOCK, ABC_WIDTH), lambda n: (0, n, 0)),
                  pl.BlockSpec((BLOCK, BLOCK), lambda n: (n, 0)),
                  pl.BlockSpec((BLOCK, BLOCK), lambda n: (n, 0)),
                  pl.BlockSpec((BLOCK, 2 * BLOCK), lambda n: (0, 0)),
                  smem, smem,
                  pl.BlockSpec((None, 3, BRANCH_WIDTH), lambda n: (layer, 0, 0))],
        out_specs=[pl.BlockSpec((nb, BLOCK, N_BRANCH * BRANCH_WIDTH), lambda n: (0, n, 0)),
                   pl.BlockSpec((nb, 1, RET_HEADS, BLOCK, BLOCK), lambda n: (0, n, 0, 0, 0))],
        out_shape=[jax.ShapeDtypeStruct((nb, lp, N_BRANCH * BRANCH_WIDTH), BF16),
                   jax.ShapeDtypeStruct((nb, nc, RET_HEADS, BLOCK, BLOCK), F32)],
        scratch_shapes=[pltpu.VMEM((ATT_HEADS, BLOCK, 2 * BLOCK), F32),
                        pltpu.VMEM((nb, 2 * BLOCK, 2 * BLOCK), F32),
                        pltpu.VMEM((nb, RET_HEADS, BLOCK, BLOCK), F32),
                        pltpu.VMEM((nb, BLOCK, BRANCH_WIDTH), F32)],
        compiler_params=_cparams("arbitrary"),
    )(proj.reshape(nb, lp, ABC_WIDTH), cosf, sinf, bkt, rel_bias, sinks, conv_w)
    return br.reshape(nb * lp, N_BRANCH * BRANCH_WIDTH), states


def mixers_bwd(proj, d_br, states, cosf, sinf, bkt, rel_bias, sinks, conv_w, layer, nb, nc):
    def body(p_ref, kvp_ref, cp_ref, dbr_ref, st_ref, cos_ref, sin_ref, bkt_ref, rb_ref, sk_ref, cw_ref,
             dp_ref, drb_ref, dsk_ref, dcw_ref,
             bias_s, dbias_s, dkv_s, g_s, dy_s):
        p_ref, kvp_ref, cp_ref, dbr_ref = [_Widened(r) for r in (p_ref, kvp_ref, cp_ref, dbr_ref)]
        step = pl.program_id(0)
        n = nc - 1 - step
        ex = range(nb)

        @pl.when(step == 0)
        def _():
            _build_bias(bkt_ref, rb_ref, bias_s)
            dbias_s[...] = jnp.zeros_like(dbias_s)
            dsk_ref[...] = jnp.zeros_like(dsk_ref)
            dcw_ref[...] = jnp.zeros_like(dcw_ref)
            drb_ref[...] = jnp.zeros_like(drb_ref)
            dkv_s[...] = jnp.zeros_like(dkv_s)
            g_s[...] = jnp.zeros_like(g_s)
            dy_s[...] = jnp.zeros_like(dy_s)

        valid = _valid_col(n)
        mask = _band_mask(n)
        has_prev = (n > 0).astype(F32)

        k_all, v_all = [], []
        for b in ex:
            kv_prev = kvp_ref[b] * has_prev
            kv_cur = p_ref[b, :, C_AK:C_AK + 2 * BLOCK]
            k_all.append(jnp.concatenate([kv_prev[:, 0:BLOCK], kv_cur[:, 0:BLOCK]], axis=0))
            v_all.append(jnp.concatenate([kv_prev[:, BLOCK:], kv_cur[:, BLOCK:]], axis=0))
        zero2 = jnp.zeros((2 * BLOCK, BLOCK), F32)
        dk_tot = [zero2 for _ in ex]
        dv_tot = [zero2 for _ in ex]
        for kh in range(2):
            ks = [[t.astype(BF16) for t in _split_heads(k_all[b], kh)] for b in ex]
            vs = [[t.astype(BF16) for t in _split_heads(v_all[b], kh)] for b in ex]
            pairs = [(b, 2 * kh + jj) for jj in range(2) for b in ex]
            subs = [(b, j, x) for (b, j) in pairs for x in range(2)]
            qb_, gate_, dya_, do2_ = {}, {}, {}, {}
            for w in pairs:
                b, j = w
                qb_[w] = p_ref[b, :, C_AQ + BLOCK * j:C_AQ + BLOCK * (j + 1)].astype(BF16)
                gate_[w] = p_ref[b, :, C_AG + BLOCK * j:C_AG + BLOCK * (j + 1)]
                dya_[w] = dbr_ref[b, :, BLOCK * j:BLOCK * (j + 1)]
                do2_[w] = (dya_[w] * _silu(gate_[w])).astype(BF16)
            qk_ = {(b, j, x): _nt(qb_[(b, j)], ks[b][x]) for (b, j, x) in subs}
            dpm_ = {(b, j, x): _nt(do2_[(b, j)], vs[b][x]) for (b, j, x) in subs}
            pb_, dsb_ = {}, {}
            for u in subs:
                b, j, x = u
                h = 2 * j + x
                p, p_sink = _softmax_of(qk_[u], bias_s[h], mask, sk_ref[layer, h])
                pb_[u] = p.astype(BF16)
                delta = jnp.sum(p * dpm_[u], axis=-1, keepdims=True)
                ds = p * (dpm_[u] - delta)
                dbias_s[h] += ds
                dsk_ref[h:h + 1, :] += jnp.broadcast_to(
                    jnp.sum(-p_sink * delta, axis=0, keepdims=True), (1, BLOCK))
                dsb_[u] = ds.astype(BF16)
            o_ = {u: _nn(pb_[u], vs[u[0]][u[2]]) for u in subs}
            dq_ = {u: _nn(dsb_[u], ks[u[0]][u[2]]) for u in subs}
            dkm_ = {u: _tn(dsb_[u], qb_[(u[0], u[1])]) for u in subs}
            dvm_ = {u: _tn(pb_[u], do2_[(u[0], u[1])]) for u in subs}
            for w in pairs:
                b, j = w
                o2 = o_[(b, j, 0)] + o_[(b, j, 1)]
                dq2 = (dq_[(b, j, 0)] + dq_[(b, j, 1)]) * ATT_SCALE
                dp_ref[b, :, C_AQ + BLOCK * j:C_AQ + BLOCK * (j + 1)] = dq2.astype(BF16)
                dp_ref[b, :, C_AG + BLOCK * j:C_AG + BLOCK * (j + 1)] = (
                    dya_[w] * o2 * _dsilu(gate_[w])).astype(BF16)
            for b in ex:
                j0, j1 = 2 * kh, 2 * kh + 1
                dk_lo = (dkm_[(b, j0, 0)] + dkm_[(b, j1, 0)]) * ATT_SCALE
                dk_hi = (dkm_[(b, j0, 1)] + dkm_[(b, j1, 1)]) * ATT_SCALE
                dk_tot[b] = dk_tot[b] + _merge_heads(dk_lo, dk_hi, kh)
                dv_tot[b] = dv_tot[b] + _merge_heads(dvm_[(b, j0, 0)] + dvm_[(b, j1, 0)],
                                                     dvm_[(b, j0, 1)] + dvm_[(b, j1, 1)], kh)
        for b in ex:
            dp_ref[b, :, C_AK:C_AK + BLOCK] = (dk_tot[b][BLOCK:, :] + dkv_s[b, :, 0:BLOCK]).astype(BF16)
            dp_ref[b, :, C_AV:C_AV + BLOCK] = (dv_tot[b][BLOCK:, :] + dkv_s[b, :, BLOCK:]).astype(BF16)
            dkv_s[b, :, 0:BLOCK] = dk_tot[b][0:BLOCK, :]
            dkv_s[b, :, BLOCK:] = dv_tot[b][0:BLOCK, :]

        cosv = cos_ref[...]
        sinv = sin_ref[...]
        tabs = [_decay_tables(h) for h in range(RET_HEADS)]
        units = [(b, h) for h in range(RET_HEADS) for b in ex]
        sl = lambda c0, h: slice(c0 + BLOCK * h, c0 + BLOCK * (h + 1))
        q_, k_, v_, vb_, sp_ = {}, {}, {}, {}, {}
        for u in units:
            b, h = u
            q_[u] = _rot(p_ref[b, :, sl(C_RQ, h)], cosv, sinv).astype(BF16)
            k_[u] = (_rot(p_ref[b, :, sl(C_RK, h)], cosv, sinv) * RET_SCALE * valid).astype(BF16)
            v_[u] = p_ref[b, :, sl(C_RV, h)]
            vb_[u] = v_[u].astype(BF16)
            sp_[u] = st_ref[b, 0, h].astype(BF16)
        qk_ = {u: _nt(q_[u], k_[u]) for u in units}
        qs_ = {u: _nn(q_[u], sp_[u]) for u in units}
        a_ = {u: (qk_[u] * tabs[u[1]][0]).astype(BF16) for u in units}
        av_ = {u: _nn(a_[u], vb_[u]) for u in units}
        dob_, dxo_ = {}, {}
        for u in units:
            b, h = u
            xi = tabs[h][2]
            o = av_[u] + xi * qs_[u]
            mu = jnp.mean(o, axis=-1, keepdims=True)
            var = jnp.mean(jnp.square(o - mu), axis=-1, keepdims=True)
            rstd = lax.rsqrt(var + GN_EPS)
            oh = (o - mu) * rstd
            gate = p_ref[b, :, sl(C_RG, h)]
            d_yr = dbr_ref[b, :, BRANCH_WIDTH + BLOCK * h:BRANCH_WIDTH + BLOCK * (h + 1)]
            dp_ref[b, :, sl(C_RG, h)] = (d_yr * oh * _dsilu(gate)).astype(BF16)
            doh = d_yr * _silu(gate)
            do = rstd * (doh - jnp.mean(doh, axis=-1, keepdims=True)
                         - oh * jnp.mean(doh * oh, axis=-1, keepdims=True))
            dob_[u] = do.astype(BF16)
            dxo_[u] = (do * xi).astype(BF16)
        dov_ = {u: _nt(dob_[u], vb_[u]) for u in units}
        dv1_ = {u: _tn(a_[u], dob_[u]) for u in units}
        dq1_ = {u: _nt(dxo_[u], sp_[u]) for u in units}
        gq_ = {u: _tn(q_[u], dxo_[u]) for u in units}
        da_, gb_, zv_ = {}, {}, {}
        for u in units:
            b, h = u
            da_[u] = (dov_[u] * tabs[h][0]).astype(BF16)
            g_next = g_s[b, h]
            gb_[u] = g_next.astype(BF16)
            zv_[u] = (v_[u] * tabs[h][1]).astype(BF16)
            g_s[b, h] = tabs[h][3] * g_next + gq_[u]
        dq2_ = {u: _nn(da_[u], k_[u]) for u in units}
        dk1_ = {u: _tn(da_[u], q_[u]) for u in units}
        dk2_ = {u: _nt(zv_[u], gb_[u]) for u in units}
        dv2_ = {u: _nn(k_[u], gb_[u]) for u in units}
        for u in units:
            b, h = u
            dp_ref[b, :, sl(C_RQ, h)] = _rot_t(dq2_[u] + dq1_[u], cosv, sinv).astype(BF16)
            dp_ref[b, :, sl(C_RK, h)] = _rot_t((dk1_[u] + dk2_[u]) * (RET_SCALE * valid), cosv, sinv).astype(BF16)
            dp_ref[b, :, sl(C_RV, h)] = (dv1_[u] + tabs[h][1] * dv2_[u]).astype(BF16)

        w0, w1, w2 = cw_ref[0:1, :], cw_ref[1:2, :], cw_ref[2:3, :]
        for b in ex:
            cb = p_ref[b, :, C_CB:C_CB + BRANCH_WIDTH]
            cc = p_ref[b, :, C_CC:C_CC + BRANCH_WIDTH]
            cx = p_ref[b, :, C_CX:C_CX + BRANCH_WIDTH]
            cg = p_ref[b, :, C_CG:C_CG + BRANCH_WIDTH]
            u = cc * cx * valid
            u_prev = (cp_ref[b, :, 0:BRANCH_WIDTH] * cp_ref[b, :, BRANCH_WIDTH:2 * BRANCH_WIDTH]
                      * (_valid_col(n - 1) * has_prev))
            u1 = _shift_down(u, u_prev, 1)
            u2 = _shift_down(u, u_prev, 2)
            y = w0 * u2 + w1 * u1 + w2 * u
            d_yc = dbr_ref[b, :, 2 * BRANCH_WIDTH:3 * BRANCH_WIDTH]
            sg = _silu(cg)
            dp_ref[b, :, C_CB:C_CB + BRANCH_WIDTH] = (d_yc * y * sg).astype(BF16)
            dp_ref[b, :, C_CG:C_CG + BRANCH_WIDTH] = (d_yc * cb * y * _dsilu(cg)).astype(BF16)
            dy = d_yc * cb * sg
            dy_next = dy_s[b]
            du = (w2 * dy + w1 * _shift_up(dy, dy_next, 1) + w0 * _shift_up(dy, dy_next, 2)) * valid
            dp_ref[b, :, C_CC:C_CC + BRANCH_WIDTH] = (du * cx).astype(BF16)
            dp_ref[b, :, C_CX:C_CX + BRANCH_WIDTH] = (du * cc).astype(BF16)
            dcw_ref[0:1, :] += jnp.sum(dy * u2, axis=0, keepdims=True)
            dcw_ref[1:2, :] += jnp.sum(dy * u1, axis=0, keepdims=True)
            dcw_ref[2:3, :] += jnp.sum(dy * u, axis=0, keepdims=True)
            dy_s[b] = dy

        @pl.when(step == nc - 1)
        def _():
            bkt = bkt_ref[...]
            row = lax.broadcasted_iota(jnp.int32, (N_BUCKETS, BLOCK), 0)
            lane = lax.broadcasted_iota(jnp.int32, (N_BUCKETS, BLOCK), 1)

            def one_bucket(bk, acc):
                sel = bkt == bk
                for h in range(ATT_HEADS):
                    t = jnp.where(sel, dbias_s[h], 0.0)
                    s = jnp.sum(jnp.sum(t, axis=1, keepdims=True), axis=0, keepdims=True)
                    acc = acc + jnp.where((row == bk) & (lane == h), jnp.broadcast_to(s, acc.shape), 0.0)
                return acc

            drb_ref[...] = lax.fori_loop(0, N_BUCKETS, one_bucket, jnp.zeros((N_BUCKETS, BLOCK), F32))

    lp = nc * BLOCK
    smem = pl.BlockSpec(memory_space=pltpu.SMEM)
    blk = lambda s: nc - 1 - s
    prev = lambda s: jnp.maximum(nc - 2 - s, 0)
    proj3 = proj.reshape(nb, lp, ABC_WIDTH)
    res = pl.pallas_call(
        body, name="mixers_bwd",
        grid=(nc,),
        in_specs=[pl.BlockSpec((nb, BLOCK, ABC_WIDTH), lambda s: (0, blk(s), 0)),
                  pl.BlockSpec((nb, BLOCK, 2 * BLOCK), lambda s: (0, prev(s), C_AK // (2 * BLOCK))),
                  pl.BlockSpec((nb, BLOCK, 1280), lambda s: (0, prev(s), C_CC // 1280)),
                  pl.BlockSpec((nb, BLOCK, N_BRANCH * BRANCH_WIDTH), lambda s: (0, blk(s), 0)),
                  pl.BlockSpec((nb, 1, RET_HEADS, BLOCK, BLOCK), lambda s: (0, blk(s), 0, 0, 0)),
                  pl.BlockSpec((BLOCK, BLOCK), lambda s: (blk(s), 0)),
                  pl.BlockSpec((BLOCK, BLOCK), lambda s: (blk(s), 0)),
                  pl.BlockSpec((BLOCK, 2 * BLOCK), lambda s: (0, 0)),
                  smem, smem,
                  pl.BlockSpec((None, 3, BRANCH_WIDTH), lambda s: (layer, 0, 0))],
        out_specs=[pl.BlockSpec((nb, BLOCK, ABC_WIDTH), lambda s: (0, blk(s), 0)),
                   pl.BlockSpec((N_BUCKETS, BLOCK), lambda s: (0, 0)),
                   pl.BlockSpec((ATT_HEADS, BLOCK), lambda s: (0, 0)),
                   pl.BlockSpec((8, BRANCH_WIDTH), lambda s: (0, 0))],
        out_shape=[jax.ShapeDtypeStruct((nb, lp, ABC_WIDTH), BF16),
                   jax.ShapeDtypeStruct((N_BUCKETS, BLOCK), F32),
                   jax.ShapeDtypeStruct((ATT_HEADS, BLOCK), F32),
                   jax.ShapeDtypeStruct((8, BRANCH_WIDTH), F32)],
        scratch_shapes=[pltpu.VMEM((ATT_HEADS, BLOCK, 2 * BLOCK), F32),
                        pltpu.VMEM((ATT_HEADS, BLOCK, 2 * BLOCK), F32),
                        pltpu.VMEM((nb, BLOCK, 2 * BLOCK), F32),
                        pltpu.VMEM((nb, RET_HEADS, BLOCK, BLOCK), F32),
                        pltpu.VMEM((nb, BLOCK, BRANCH_WIDTH), F32)],
        compiler_params=_cparams("arbitrary"),
    )(proj3, proj3, proj3, d_br.reshape(nb, lp, N_BRANCH * BRANCH_WIDTH), states, cosf, sinf, bkt, rel_bias, sinks,
      conv_w)
    return (res[0].reshape(nb * lp, ABC_WIDTH),) + tuple(res[1:])


MERGE_TILE = 256
MERGE_FWD_TILE = 544


def _merge_forward(br_ref, m_ref, wb_ref, wo_ref):
    bo, gates = [], []
    mixed_pre = None
    for g in range(N_BRANCH):
        br_g = br_ref[:, BRANCH_WIDTH * g:BRANCH_WIDTH * (g + 1)]
        bo_g = jnp.concatenate([_nn(br_g, wb_ref[p, g]) for p in range(N_CHIPS)], axis=1)
        gate_g = _sigmoid(m_ref[:, D_MODEL * g:D_MODEL * (g + 1)].astype(F32))
        bo.append(bo_g)
        gates.append(gate_g)
        mixed_pre = gate_g * bo_g if mixed_pre is None else mixed_pre + gate_g * bo_g
    mixed = _nn(mixed_pre.astype(BF16), wo_ref[...])
    r = lax.rsqrt(jnp.mean(mixed * mixed, axis=-1, keepdims=True) + RMS_EPS)
    return bo, gates, mixed_pre, mixed, r


def merge_fwd(x2d, br, pm, wb, wo, g_post, layer, after):
    t = x2d.shape[0]
    tm = MERGE_FWD_TILE if t % MERGE_FWD_TILE == 0 else BLOCK

    def body(x_ref, br_ref, m_ref, wb_ref, wo_ref, g_ref, after_ref, o_ref):
        _, _, _, mixed, r = _merge_forward(br_ref, m_ref, wb_ref, wo_ref)
        o_ref[...] = x_ref[...] + mixed * r * g_ref[...]

    return pl.pallas_call(
        body, name="merge_fwd",
        grid=(t // tm,),
        in_specs=[pl.BlockSpec((tm, D_MODEL), lambda i: (i, 0)),
                  pl.BlockSpec((tm, N_BRANCH * BRANCH_WIDTH), lambda i: (i, 0)),
                  pl.BlockSpec((tm, MERGE_WIDTH), lambda i: (i, 0)),
                  pl.BlockSpec((N_CHIPS, N_BRANCH, BRANCH_WIDTH, SHARD_D), lambda i: (0, 0, 0, 0)),
                  pl.BlockSpec((D_MODEL, D_MODEL), lambda i: (0, 0)),
                  pl.BlockSpec((None, 1, D_MODEL), lambda i: (layer, 0, 0)),
                  ANY],
        out_specs=pl.BlockSpec((tm, D_MODEL), lambda i: (i, 0)),
        out_shape=jax.ShapeDtypeStruct((t, D_MODEL), F32),
        compiler_params=_cparams("parallel"),
    )(x2d, br, pm, wb, wo, g_post, after)


def merge_bwd(d_out, br, pm, wb, wo, g_post, layer, after):
    t = d_out.shape[0]
    tm = MERGE_TILE if t % MERGE_TILE == 0 else BLOCK

    def body(do_ref, br_ref, m_ref, wb_ref, wo_ref, g_ref, after_ref, dbr_ref, dm_ref, dg_ref, dwb_ref, dwo_ref):

        @pl.when(pl.program_id(0) == 0)
        def _():
            dwb_ref[...] = jnp.zeros_like(dwb_ref)
            dwo_ref[...] = jnp.zeros_like(dwo_ref)
            dg_ref[...] = jnp.zeros_like(dg_ref)

        bo, gates, mixed_pre, mixed, r = _merge_forward(br_ref, m_ref, wb_ref, wo_ref)
        d_o = do_ref[...]
        nh = mixed * r
        dg_ref[0:1, :] += jnp.sum(d_o * nh, axis=0, keepdims=True)
        dn = d_o * g_ref[...]
        d_mixed = (r * (dn - nh * jnp.mean(dn * nh, axis=-1, keepdims=True))).astype(BF16)
        dwo_ref[...] += _tn(mixed_pre.astype(BF16), d_mixed)
        d_pre = _nt(d_mixed, wo_ref[...])
        for g in range(N_BRANCH):
            br_g = br_ref[:, BRANCH_WIDTH * g:BRANCH_WIDTH * (g + 1)]
            d_bo = (d_pre * gates[g]).astype(BF16)
            dm_ref[:, D_MODEL * g:D_MODEL * (g + 1)] = (
                d_pre * bo[g] * gates[g] * (1.0 - gates[g])).astype(BF16)
            d_br_g = None
            for p in range(N_CHIPS):
                d_bo_p = d_bo[:, SHARD_D * p:SHARD_D * (p + 1)]
                part = _nt(d_bo_p, wb_ref[p, g])
                d_br_g = part if d_br_g is None else d_br_g + part
                dwb_ref[p, g] += _tn(br_g, d_bo_p)
            dbr_ref[:, BRANCH_WIDTH * g:BRANCH_WIDTH * (g + 1)] = d_br_g.astype(BF16)

    return pl.pallas_call(
        body, name="merge_bwd",
        grid=(t // tm,),
        in_specs=[pl.BlockSpec((tm, D_MODEL), lambda i: (i, 0)),
                  pl.BlockSpec((tm, N_BRANCH * BRANCH_WIDTH), lambda i: (i, 0)),
                  pl.BlockSpec((tm, MERGE_WIDTH), lambda i: (i, 0)),
                  pl.BlockSpec((N_CHIPS, N_BRANCH, BRANCH_WIDTH, SHARD_D), lambda i: (0, 0, 0, 0)),
                  pl.BlockSpec((D_MODEL, D_MODEL), lambda i: (0, 0)),
                  pl.BlockSpec((None, 1, D_MODEL), lambda i: (layer, 0, 0)),
                  ANY],
        out_specs=[pl.BlockSpec((tm, N_BRANCH * BRANCH_WIDTH), lambda i: (i, 0)),
                   pl.BlockSpec((tm, MERGE_WIDTH), lambda i: (i, 0)),
                   pl.BlockSpec((8, D_MODEL), lambda i: (0, 0)),
                   pl.BlockSpec((N_CHIPS, N_BRANCH, BRANCH_WIDTH, SHARD_D), lambda i: (0, 0, 0, 0)),
                   pl.BlockSpec((D_MODEL, D_MODEL), lambda i: (0, 0))],
        out_shape=[jax.ShapeDtypeStruct((t, N_BRANCH * BRANCH_WIDTH), BF16),
                   jax.ShapeDtypeStruct((t, MERGE_WIDTH), BF16),
                   jax.ShapeDtypeStruct((8, D_MODEL), F32),
                   jax.ShapeDtypeStruct((N_CHIPS, N_BRANCH, BRANCH_WIDTH, SHARD_D), F32),
                   jax.ShapeDtypeStruct((D_MODEL, D_MODEL), F32)],
        compiler_params=_cparams("arbitrary"),
    )(d_out, br, pm, wb, wo, g_post, after)


def loss_head(xf, target2d, nb, nc):
    def body(x_ref, t_ref, l_ref, dx_ref):
        b = pl.program_id(0)
        n = pl.program_id(1)

        @pl.when((b == 0) & (n == 0))
        def _():
            l_ref[...] = jnp.zeros_like(l_ref)

        @pl.when(n == 0)
        def _():
            dx_ref[...] = jnp.zeros_like(dx_ref)

        @pl.when(n > 0)
        def _():
            e = x_ref[...] - t_ref[...]
            dx_ref[...] = e * (1.0 / D_MODEL)
            s = jnp.sum(jnp.sum(e * e, axis=1, keepdims=True), axis=0, keepdims=True)
            l_ref[...] += jnp.broadcast_to(s * (0.5 / D_MODEL), l_ref.shape)

    return pl.pallas_call(
        body, name="loss_head",
        grid=(nb, nc),
        in_specs=[pl.BlockSpec((BLOCK, D_MODEL), lambda b, n: (b * nc + n, 0)),
                  pl.BlockSpec((BLOCK, D_MODEL), lambda b, n: (b * (nc - 1) + jnp.maximum(n - 1, 0), 0))],
        out_specs=[pl.BlockSpec((8, BLOCK), lambda b, n: (0, 0)),
                   pl.BlockSpec((BLOCK, D_MODEL), lambda b, n: (b * nc + n, 0))],
        out_shape=[jax.ShapeDtypeStruct((8, BLOCK), F32),
                   jax.ShapeDtypeStruct(xf.shape, F32)],
        compiler_params=_cparams("arbitrary", "arbitrary"),
    )(xf, target2d)


N_ABC_TILES = ABC_WIDTH // COL_TILE
N_M_TILES = MERGE_WIDTH // COL_TILE


def proj_dgrad(d_abc, d_m, w, x2d, g, layer, d_out, after):
    t = x2d.shape[0]
    tm = ROW_TILE if t % ROW_TILE == 0 else BLOCK
    nk = N_ABC_TILES + N_M_TILES

    def body(da_ref, dm_ref, w_ref, x_ref, g_ref, do_ref, after_ref, dx_ref, dg_ref, acc):
        i = pl.program_id(0)
        k = pl.program_id(1)

        @pl.when((i == 0) & (k == 0))
        def _():
            dg_ref[...] = jnp.zeros_like(dg_ref)

        @pl.when(k == 0)
        def _():
            acc[...] = jnp.zeros_like(acc)

        @pl.when(k < N_ABC_TILES)
        def _():
            acc[...] += _nn(da_ref[...], w_ref[...])

        @pl.when(k >= N_ABC_TILES)
        def _():
            acc[...] += _nn(dm_ref[...], w_ref[...])

        @pl.when(k == nk - 1)
        def _():
            x = x_ref[...]
            r = lax.rsqrt(jnp.mean(x * x, axis=-1, keepdims=True) + RMS_EPS)
            nh = x * r
            dh = acc[...]
            dg_ref[0:1, :] += jnp.sum(dh * nh, axis=0, keepdims=True)
            dn = dh * g_ref[...]
            dx_ref[...] = do_ref[...] + r * (dn - nh * jnp.mean(dn * nh, axis=-1, keepdims=True))

    return pl.pallas_call(
        body, name="proj_dgrad",
        grid=(t // tm, nk),
        in_specs=[pl.BlockSpec((tm, COL_TILE), lambda i, k: (i, jnp.minimum(k, N_ABC_TILES - 1))),
                  pl.BlockSpec((tm, COL_TILE), lambda i, k: (i, jnp.maximum(k - N_ABC_TILES, 0))),
                  pl.BlockSpec((COL_TILE, D_MODEL), lambda i, k: (k, 0)),
                  pl.BlockSpec((tm, D_MODEL), lambda i, k: (i, 0)),
                  pl.BlockSpec((None, 1, D_MODEL), lambda i, k: (layer, 0, 0)),
                  pl.BlockSpec((tm, D_MODEL), lambda i, k: (i, 0)),
                  ANY],
        out_specs=[pl.BlockSpec((tm, D_MODEL), lambda i, k: (i, 0)),
                   pl.BlockSpec((8, D_MODEL), lambda i, k: (0, 0))],
        out_shape=[jax.ShapeDtypeStruct((t, D_MODEL), F32),
                   jax.ShapeDtypeStruct((8, D_MODEL), F32)],
        scratch_shapes=[pltpu.VMEM((tm, D_MODEL), F32)],
        compiler_params=_cparams("arbitrary", "arbitrary"),
    )(d_abc, d_m, w, x2d, g, d_out, after)


def proj_wgrad(hb, d_abc, d_m):
    t = hb.shape[0]
    nj = N_ABC_TILES + N_M_TILES

    def body(h_ref, da_ref, dm_ref, o_ref):
        j = pl.program_id(0)

        @pl.when(j < N_ABC_TILES)
        def _():
            o_ref[...] = _tn(da_ref[...], h_ref[...])

        @pl.when(j >= N_ABC_TILES)
        def _():
            o_ref[...] = _tn(dm_ref[...], h_ref[...])

    return pl.pallas_call(
        body, name="proj_wgrad",
        grid=(nj,),
        in_specs=[pl.BlockSpec((t, D_MODEL), lambda j: (0, 0)),
                  pl.BlockSpec((t, COL_TILE), lambda j: (0, jnp.minimum(j, N_ABC_TILES - 1))),
                  pl.BlockSpec((t, COL_TILE), lambda j: (0, jnp.maximum(j - N_ABC_TILES, 0)))],
        out_specs=pl.BlockSpec((COL_TILE, D_MODEL), lambda j: (j, 0)),
        out_shape=jax.ShapeDtypeStruct((PROJ_WIDTH, D_MODEL), F32),
        compiler_params=_cparams("arbitrary"),
    )(hb, d_abc, d_m)


def _adamw_math(w, g, m, v):
    m = ADAM_B1 * m + (1.0 - ADAM_B1) * g
    v = ADAM_B2 * v + (1.0 - ADAM_B2) * jnp.square(g)
    m_hat = m / (1.0 - ADAM_B1 ** ADAM_STEP)
    v_hat = v / (1.0 - ADAM_B2 ** ADAM_STEP)
    delta = -ADAM_LR * (m_hat / (jnp.sqrt(v_hat) + ADAM_EPS) + ADAM_WD * w)
    return delta, m, v


def adamw_layer(w, g, m, v, layer, acc, after):
    _, r, c = w.shape
    tr = _row_tile(r)

    def body(*refs):
        w_ref, g_ref, m_ref, v_ref = refs[:4]
        go_ref, d_ref, mo_ref, vo_ref = refs[-4:]
        g_val = g_ref[...]
        d, m_new, v_new = _adamw_math(w_ref[...], g_val, m_ref[...], v_ref[...])
        go_ref[...] = g_val
        d_ref[...] = d
        mo_ref[...] = m_new
        vo_ref[...] = v_new

    slab = pl.BlockSpec((None, tr, c), lambda i: (layer, i, 0))
    ins = [w, g, m, v, after]
    in_specs = [slab, pl.BlockSpec((tr, c), lambda i: (i, 0)), slab, slab, ANY]
    aliases = {}
    if acc is not None:
        ins += list(acc)
        in_specs += [ANY] * 4
        aliases = {5 + i: i for i in range(4)}
    return pl.pallas_call(
        body, name="adamw_layer",
        grid=(r // tr,),
        in_specs=in_specs, out_specs=[slab] * 4,
        out_shape=[jax.ShapeDtypeStruct(w.shape, F32)] * 4,
        input_output_aliases=aliases,
        compiler_params=_cparams("parallel"),
    )(*ins)


def adamw_small(params):
    k = len(params)

    def body(*refs):
        ins, outs = refs[:4 * k], refs[4 * k:]
        for i in range(k):
            d, m_new, v_new = _adamw_math(*[r[...] for r in ins[4 * i:4 * i + 4]])
            outs[3 * i][...] = d
            outs[3 * i + 1][...] = m_new
            outs[3 * i + 2][...] = v_new

    flat = [a for p in params for a in p]
    vm = pl.BlockSpec(memory_space=pltpu.VMEM)
    out_shape = [jax.ShapeDtypeStruct(p[0].shape, F32) for p in params for _ in range(3)]
    res = pl.pallas_call(
        body, name="adamw_small",
        in_specs=[vm] * len(flat), out_specs=[vm] * len(out_shape), out_shape=out_shape,
    )(*flat)
    return [tuple(res[3 * i:3 * i + 3]) for i in range(k)]


ANY = pl.BlockSpec(memory_space=pl.ANY)


def _place():
    return lax.axis_index("x"), lax.axis_index("y"), lax.axis_index("c")


HBM = pl.BlockSpec(memory_space=pltpu.HBM)
SEM = pl.BlockSpec(memory_space=pltpu.SEMAPHORE)
EFFECT = pltpu.SideEffectType.DATAFLOW_SIDE_EFFECTING


def _other_chips(x, y):
    return [(1 - x, y), (x, 1 - y), (1 - x, 1 - y)]


def _own_slot(shard, chip):
    buf = lax.empty((N_CHIPS,) + shard.shape, shard.dtype)
    return lax.dynamic_update_slice(buf, shard[None], (chip, 0, 0, 0))


def _hbm(a):
    return pltpu.with_memory_space_constraint(a, pltpu.HBM)


def gather_start(bufs, after):
    n = len(bufs)

    def body(*refs):
        g_refs = refs[:n]
        send_sems, recv_sems = refs[n + 1], refs[n + 2]
        token = refs[-1]
        x, y, c = _place()
        me_p = 2 * x + y
        for t in range(n):
            for k, (qx, qy) in enumerate(_other_chips(x, y)):
                slab = g_refs[t].at[me_p, c]
                pltpu.make_async_remote_copy(src_ref=slab, dst_ref=slab, send_sem=send_sems.at[3 * t + k],
                                             recv_sem=recv_sems.at[3 * t + k], device_id=(qx, qy, c),
                                             device_id_type=MESH).start()
        token[...] = jnp.zeros_like(token)

    res = pl.pallas_call(
        body, name="gather_start",
        in_specs=[HBM] * n + [ANY],
        out_specs=[SEM, SEM] + [HBM] * n + [pl.BlockSpec(memory_space=pltpu.VMEM)],
        out_shape=[pltpu.SemaphoreType.DMA((3 * n,)), pltpu.SemaphoreType.DMA((3 * n,))]
        + [pltpu.HBM(b.shape, b.dtype) for b in bufs] + [jax.ShapeDtypeStruct((8, LANES), F32)],
        input_output_aliases={t: 2 + t for t in range(n)},
        compiler_params=pltpu.CompilerParams(has_side_effects=EFFECT),
    )(*[_hbm(b) for b in bufs], after)
    return res[0], res[1], list(res[2:2 + n]), res[-1]


def gather_wait(bufs, send_sems, recv_sems, after, first=0):
    n = len(bufs)

    def body(*refs):
        g_refs = refs[:n]
        send_sems, recv_sems = refs[n], refs[n + 1]
        x, y, c = _place()
        me_p = 2 * x + y
        for t in range(n):
            for k, (qx, qy) in enumerate(_other_chips(x, y)):
                s = 3 * (first + t) + k
                cp = pltpu.make_async_remote_copy(src_ref=g_refs[t].at[me_p, c], dst_ref=g_refs[t].at[2 * qx + qy, c],
                                                  send_sem=send_sems.at[s], recv_sem=recv_sems.at[s],
                                                  device_id=(qx, qy, c), device_id_type=MESH)
                cp.wait_send()
                cp.wait_recv()

    return pl.pallas_call(
        body, name="gather_wait",
        in_specs=[HBM] * n + [SEM, SEM, ANY],
        out_specs=[HBM] * n,
        out_shape=[pltpu.HBM(b.shape, b.dtype) for b in bufs],
        input_output_aliases={t: t for t in range(n)},
        compiler_params=pltpu.CompilerParams(has_side_effects=EFFECT),
    )(*bufs, send_sems, recv_sems, after)


def gather_forward(bufs):
    n = len(bufs)

    def body(*refs):
        g_refs = refs[n:2 * n]
        send_sems, recv_sems = refs[2 * n:]
        x, y, c = _place()
        sibling = (x, y, 1 - c)
        chips = _other_chips(x, y)
        passed = []
        for t in range(n):
            for k, (qx, qy) in enumerate(chips):
                slab = g_refs[t].at[2 * qx + qy, c]
                fwd = pltpu.make_async_remote_copy(src_ref=slab, dst_ref=slab, send_sem=send_sems.at[3 * t + k],
                                                   recv_sem=recv_sems.at[3 * t + k], device_id=sibling,
                                                   device_id_type=MESH)
                fwd.start()
                passed.append(fwd)
        for t in range(n):
            for k, (qx, qy) in enumerate(chips):
                slab = g_refs[t].at[2 * qx + qy, 1 - c]
                pltpu.make_async_remote_copy(src_ref=slab, dst_ref=slab, send_sem=send_sems.at[3 * t + k],
                                             recv_sem=recv_sems.at[3 * t + k], device_id=sibling,
                                             device_id_type=MESH).wait_recv()
        for cp in passed:
            cp.wait_send()

    return pl.pallas_call(
        body, name="gather_forward",
        in_specs=[ANY] * n, out_specs=[ANY] * n,
        out_shape=[jax.ShapeDtypeStruct(b.shape, b.dtype) for b in bufs],
        input_output_aliases={t: t for t in range(n)},
        scratch_shapes=[pltpu.SemaphoreType.DMA((3 * n,)), pltpu.SemaphoreType.DMA((3 * n,))],
    )(*bufs)


def forward_start(bufs):
    n = len(bufs)

    def body(*refs):
        g_refs = refs[:n]
        send_sems, recv_sems = refs[n], refs[n + 1]
        token = refs[-1]
        x, y, c = _place()
        for t in range(n):
            for k, (qx, qy) in enumerate(_other_chips(x, y)):
                slab = g_refs[t].at[2 * qx + qy, c]
                pltpu.make_async_remote_copy(src_ref=slab, dst_ref=slab, send_sem=send_sems.at[3 * t + k],
                                             recv_sem=recv_sems.at[3 * t + k], device_id=(x, y, 1 - c),
                                             device_id_type=MESH).start()
        token[...] = jnp.zeros_like(token)

    res = pl.pallas_call(
        body, name="forward_start",
        in_specs=[HBM] * n,
        out_specs=[SEM, SEM] + [HBM] * n + [pl.BlockSpec(memory_space=pltpu.VMEM)],
        out_shape=[pltpu.SemaphoreType.DMA((3 * n,)), pltpu.SemaphoreType.DMA((3 * n,))]
        + [pltpu.HBM(b.shape, b.dtype) for b in bufs] + [jax.ShapeDtypeStruct((8, LANES), F32)],
        input_output_aliases={t: 2 + t for t in range(n)},
        compiler_params=pltpu.CompilerParams(has_side_effects=EFFECT),
    )(*[_hbm(b) for b in bufs])
    return res[0], res[1], list(res[2:2 + n]), res[-1]


def forward_wait(bufs, send_sems, recv_sems, after):
    n = len(bufs)

    def body(*refs):
        g_refs = refs[:n]
        send_sems, recv_sems = refs[n], refs[n + 1]
        x, y, c = _place()
        for t in range(n):
            for k, (qx, qy) in enumerate(_other_chips(x, y)):
                cp = pltpu.make_async_remote_copy(src_ref=g_refs[t].at[2 * qx + qy, c],
                                                  dst_ref=g_refs[t].at[2 * qx + qy, 1 - c],
                                                  send_sem=send_sems.at[3 * t + k], recv_sem=recv_sems.at[3 * t + k],
                                                  device_id=(x, y, 1 - c), device_id_type=MESH)
                cp.wait_send()
                cp.wait_recv()

    return pl.pallas_call(
        body, name="forward_wait",
        in_specs=[HBM] * n + [SEM, SEM, ANY],
        out_specs=[HBM] * n,
        out_shape=[pltpu.HBM(b.shape, b.dtype) for b in bufs],
        input_output_aliases={t: t for t in range(n)},
        compiler_params=pltpu.CompilerParams(has_side_effects=EFFECT),
    )(*bufs, send_sems, recv_sems, after)


def small_start(pack, me, after):
    buf = lax.dynamic_update_slice(lax.empty((8,) + pack.shape, pack.dtype), pack[None], (me, 0, 0))

    def body(b_ref, after_ref, send_sems, recv_sems, thru, token):
        x, y, c = _place()
        slot = b_ref.at[4 * x + 2 * y + c]
        for k in range(1, 8):
            peer = (x ^ ((k >> 2) & 1), y ^ ((k >> 1) & 1), c ^ (k & 1))
            pltpu.make_async_remote_copy(src_ref=slot, dst_ref=slot, send_sem=send_sems.at[k - 1],
                                         recv_sem=recv_sems.at[k - 1], device_id=peer, device_id_type=MESH).start()
        token[...] = jnp.zeros_like(token)

    return pl.pallas_call(
        body, name="small_start",
        in_specs=[HBM, ANY],
        out_specs=[SEM, SEM, HBM, pl.BlockSpec(memory_space=pltpu.VMEM)],
        out_shape=[pltpu.SemaphoreType.DMA((7,)), pltpu.SemaphoreType.DMA((7,)), pltpu.HBM(buf.shape, buf.dtype),
                   jax.ShapeDtypeStruct((8, LANES), F32)],
        input_output_aliases={0: 2},
        compiler_params=pltpu.CompilerParams(has_side_effects=EFFECT),
    )(_hbm(buf), after)


def small_wait(buf, send_sems, recv_sems, after):
    def body(b_ref, send_sems, recv_sems, after_ref, thru):
        x, y, c = _place()
        mine = b_ref.at[4 * x + 2 * y + c]
        for k in range(1, 8):
            peer = (x ^ ((k >> 2) & 1), y ^ ((k >> 1) & 1), c ^ (k & 1))
            cp = pltpu.make_async_remote_copy(src_ref=mine, dst_ref=b_ref.at[4 * peer[0] + 2 * peer[1] + peer[2]],
                                              send_sem=send_sems.at[k - 1], recv_sem=recv_sems.at[k - 1],
                                              device_id=peer, device_id_type=MESH)
            cp.wait_send()
            cp.wait_recv()

    return pl.pallas_call(
        body, name="small_wait",
        in_specs=[HBM, SEM, SEM, ANY], out_specs=HBM,
        out_shape=pltpu.HBM(buf.shape, buf.dtype),
        input_output_aliases={0: 0},
        compiler_params=pltpu.CompilerParams(has_side_effects=EFFECT),
    )(buf, send_sems, recv_sems, after)


def swap_start(grads, after):
    n = len(grads)

    def body(*refs):
        g_refs, l_refs = refs[:n], refs[n:2 * n]
        send_sems, recv_sems = refs[2 * n + 1], refs[2 * n + 2]
        token = refs[-1]
        x, y, c = _place()
        for t in range(n):
            for p in range(N_CHIPS):
                pltpu.make_async_remote_copy(src_ref=g_refs[t].at[p, 1 - c], dst_ref=l_refs[t].at[p],
                                             send_sem=send_sems.at[N_CHIPS * t + p],
                                             recv_sem=recv_sems.at[N_CHIPS * t + p],
                                             device_id=(x, y, 1 - c), device_id_type=MESH).start()
        token[...] = jnp.zeros_like(token)

    lands = [lax.empty((N_CHIPS,) + g.shape[2:], g.dtype) for g in grads]
    res = pl.pallas_call(
        body, name="swap_start",
        in_specs=[HBM] * (2 * n) + [ANY],
        out_specs=[SEM, SEM] + [HBM] * (2 * n) + [pl.BlockSpec(memory_space=pltpu.VMEM)],
        out_shape=[pltpu.SemaphoreType.DMA((N_CHIPS * n,)), pltpu.SemaphoreType.DMA((N_CHIPS * n,))]
        + [pltpu.HBM(a.shape, a.dtype) for a in grads + lands] + [jax.ShapeDtypeStruct((8, LANES), F32)],
        input_output_aliases={t: 2 + t for t in range(2 * n)},
        compiler_params=pltpu.CompilerParams(has_side_effects=EFFECT),
    )(*[_hbm(a) for a in grads + lands], after)
    return res[0], res[1], list(res[2:2 + n]), list(res[2 + n:2 + 2 * n]), res[-1]


def swap_wait(grads, lands, send_sems, recv_sems, after):
    n = len(grads)

    def body(*refs):
        g_refs, l_refs = refs[:n], refs[n:2 * n]
        send_sems, recv_sems = refs[2 * n], refs[2 * n + 1]
        x, y, c = _place()
        for t in range(n):
            for p in range(N_CHIPS):
                cp = pltpu.make_async_remote_copy(src_ref=g_refs[t].at[p, 1 - c], dst_ref=l_refs[t].at[p],
                                                  send_sem=send_sems.at[N_CHIPS * t + p],
                                                  recv_sem=recv_sems.at[N_CHIPS * t + p],
                                                  device_id=(x, y, 1 - c), device_id_type=MESH)
                cp.wait_send()
                cp.wait_recv()

    res = pl.pallas_call(
        body, name="swap_wait",
        in_specs=[HBM] * (2 * n) + [SEM, SEM, ANY],
        out_specs=[HBM] * (2 * n),
        out_shape=[pltpu.HBM(a.shape, a.dtype) for a in grads + lands],
        input_output_aliases={t: t for t in range(2 * n)},
        compiler_params=pltpu.CompilerParams(has_side_effects=EFFECT),
    )(*grads, *lands, send_sems, recv_sems, after)
    return list(res[:n]), list(res[n:])


def _row_tile(r):
    return max(t for t in range(16, 513, 16) if r % t == 0)


def add_own_half(g, other, c_arr):
    _, _, r, cols = g.shape
    tr = _row_tile(r)

    def body(c_ref, a_ref, b_ref, o_ref):
        o_ref[...] = (a_ref[...] + b_ref[...]).astype(BF16)

    return pl.pallas_call(
        body, name="add_own_half",
        grid_spec=pltpu.PrefetchScalarGridSpec(
            num_scalar_prefetch=1, grid=(N_CHIPS, r // tr),
            in_specs=[pl.BlockSpec((None, None, tr, cols), lambda p, i, c_ref: (p, c_ref[0], i, 0)),
                      pl.BlockSpec((None, tr, cols), lambda p, i, c_ref: (p, i, 0))],
            out_specs=pl.BlockSpec((None, tr, cols), lambda p, i, c_ref: (p, i, 0))),
        out_shape=jax.ShapeDtypeStruct((N_CHIPS, r, cols), BF16),
        compiler_params=_cparams("parallel", "parallel"),
    )(c_arr, g, other)


def scatter_start(partials):
    n = len(partials)

    def body(*refs):
        s_refs, l_refs = refs[:n], refs[n:2 * n]
        send_sems, recv_sems = refs[2 * n], refs[2 * n + 1]
        token = refs[-1]
        x, y, c = _place()
        for t in range(n):
            for k, (qx, qy) in enumerate(_other_chips(x, y)):
                pltpu.make_async_remote_copy(src_ref=s_refs[t].at[2 * qx + qy], dst_ref=l_refs[t].at[k],
                                             send_sem=send_sems.at[3 * t + k], recv_sem=recv_sems.at[3 * t + k],
                                             device_id=(qx, qy, c), device_id_type=MESH).start()
        token[...] = jnp.zeros_like(token)

    lands = [lax.empty((3,) + s.shape[1:], s.dtype) for s in partials]
    res = pl.pallas_call(
        body, name="scatter_start",
        in_specs=[HBM] * (2 * n),
        out_specs=[SEM, SEM] + [HBM] * (2 * n) + [pl.BlockSpec(memory_space=pltpu.VMEM)],
        out_shape=[pltpu.SemaphoreType.DMA((3 * n,)), pltpu.SemaphoreType.DMA((3 * n,))]
        + [pltpu.HBM(a.shape, a.dtype) for a in partials + lands] + [jax.ShapeDtypeStruct((8, LANES), F32)],
        input_output_aliases={t: 2 + t for t in range(2 * n)},
        compiler_params=pltpu.CompilerParams(has_side_effects=EFFECT),
    )(*[_hbm(a) for a in partials + lands])
    return res[0], res[1], list(res[2:2 + n]), list(res[2 + n:2 + 2 * n]), res[-1]


def scatter_wait(partials, lands, send_sems, recv_sems, after):
    n = len(partials)

    def body(*refs):
        s_refs, l_refs = refs[:n], refs[n:2 * n]
        send_sems, recv_sems = refs[2 * n], refs[2 * n + 1]
        x, y, c = _place()
        for t in range(n):
            for k, (qx, qy) in enumerate(_other_chips(x, y)):
                cp = pltpu.make_async_remote_copy(src_ref=s_refs[t].at[2 * qx + qy], dst_ref=l_refs[t].at[k],
                                                  send_sem=send_sems.at[3 * t + k], recv_sem=recv_sems.at[3 * t + k],
                                                  device_id=(qx, qy, c), device_id_type=MESH)
                cp.wait_send()
                cp.wait_recv()

    res = pl.pallas_call(
        body, name="scatter_wait",
        in_specs=[HBM] * (2 * n) + [SEM, SEM, ANY],
        out_specs=[HBM] * (2 * n),
        out_shape=[pltpu.HBM(a.shape, a.dtype) for a in partials + lands],
        input_output_aliases={t: t for t in range(2 * n)},
        compiler_params=pltpu.CompilerParams(has_side_effects=EFFECT),
    )(*partials, *lands, send_sems, recv_sems, after)
    return list(res[:n]), list(res[n:])


def sum_chips(own, parts, where):
    _, r, cols = own.shape
    tr = _row_tile(r)

    def body(w_ref, a_ref, p_ref, o_ref):
        acc = a_ref[...].astype(F32)
        for k in range(3):
            acc = acc + p_ref[k].astype(F32)
        o_ref[...] = acc

    return pl.pallas_call(
        body, name="sum_chips",
        grid_spec=pltpu.PrefetchScalarGridSpec(
            num_scalar_prefetch=1, grid=(r // tr,),
            in_specs=[pl.BlockSpec((None, tr, cols), lambda i, w_ref: (w_ref[0], i, 0)),
                      pl.BlockSpec((3, tr, cols), lambda i, w_ref: (0, i, 0))],
            out_specs=pl.BlockSpec((None, tr, cols), lambda i, w_ref: (w_ref[1], i, 0))),
        out_shape=jax.ShapeDtypeStruct((DEPTH, r, cols), F32),
        compiler_params=_cparams("parallel"),
    )(where, own, parts)


def sibling_share_layer(bufs):
    n = len(bufs)

    def body(*refs):
        o_refs = refs[n:2 * n]
        send_sems, recv_sems = refs[2 * n:]
        x, y, c = _place()
        cps = []
        for t in range(n):
            cp = pltpu.make_async_remote_copy(src_ref=o_refs[t].at[c], dst_ref=o_refs[t].at[c], send_sem=send_sems.at[t],
                                              recv_sem=recv_sems.at[t], device_id=(x, y, 1 - c), device_id_type=MESH)
            cp.start()
            cps.append(cp)
        for t in range(n):
            slot = o_refs[t].at[1 - c]
            pltpu.make_async_remote_copy(src_ref=slot, dst_ref=slot, send_sem=send_sems.at[t], recv_sem=recv_sems.at[t],
                                         device_id=(x, y, 1 - c), device_id_type=MESH).wait_recv()
        for cp in cps:
            cp.wait_send()

    return pl.pallas_call(
        body, name="sibling_share_layer",
        in_specs=[ANY] * n, out_specs=[ANY] * n,
        out_shape=[jax.ShapeDtypeStruct(b.shape, b.dtype) for b in bufs],
        input_output_aliases={t: t for t in range(n)},
        scratch_shapes=[pltpu.SemaphoreType.DMA((n,)), pltpu.SemaphoreType.DMA((n,))],
    )(*bufs)


SP_META = 2 * (N_META * D_MODEL // LANES)
SP_NORM = DEPTH * D_MODEL // LANES
SP_RB = DEPTH * N_BUCKETS
SP_SINK = DEPTH * ATT_HEADS
SP_CONV = DEPTH * 3 * BRANCH_WIDTH // LANES
SP_LOSS = 8
SIDE_ROWS = 48
SP_ROWS = SP_META + 2 * SP_NORM + SP_RB + SP_SINK + SP_CONV + SP_LOSS


def sum_small(slots):
    half = SP_META // 2
    rb0 = SP_META + 2 * SP_NORM
    rest_rows = SP_ROWS - SP_META

    def body(s_ref, meta_ref, rest_ref):
        acc = s_ref[0]
        for d in range(1, 8):
            acc = acc + s_ref[d]
        meta_ref[...] = acc[0:half] + acc[half:SP_META]
        rest_ref[...] = acc[SP_META:]
        rest_ref[rb0 - SP_META:rb0 - SP_META + N_BUCKETS, :] = (
            acc[rb0:rb0 + N_BUCKETS] + acc[rb0 + N_BUCKETS:rb0 + 2 * N_BUCKETS])

    vm = pl.BlockSpec(memory_space=pltpu.VMEM)
    return pl.pallas_call(
        body, name="sum_small",
        in_specs=[vm], out_specs=[vm, vm],
        out_shape=[jax.ShapeDtypeStruct((half, LANES), F32), jax.ShapeDtypeStruct((rest_rows, LANES), F32)],
    )(slots)


def local_step(x, loss_target, meta_full, rel_bias, norm_pre, conv_w_full, attn_sinks, norm_post, weights_of, mid_fwd,
               grads_done, bwd_done):
    nb, seq, _ = x.shape
    nc = seq // BLOCK + 1
    lp = nc * BLOCK
    rows = nb * lp
    pad = jnp.zeros((nb, PAD_FRONT, D_MODEL), F32)
    meta = jnp.broadcast_to(meta_full[None], (nb, N_META, D_MODEL))
    h0 = jnp.concatenate([pad, meta, x], axis=1).reshape(rows, D_MODEL)
    cosf, sinf = _rot_tables(lp)
    bkt = jnp.asarray(_bucket_table())

    g_pre = norm_pre.reshape(DEPTH, 1, D_MODEL)
    g_post = norm_post.reshape(DEPTH, 1, D_MODEL)
    order = lambda token: bkt if token is None else token

    acts = []
    h = h0
    for l in range(DEPTH):
        w_in, token = weights_of(l, h)
        hb, p_abc = norm_matmul(h, g_pre, l, w_in, 0, N_ABC_TILES, order(token))
        p_m = matmul_cols(hb, w_in, N_ABC_TILES, N_M_TILES)
        br, states = mixers_fwd(p_abc, cosf, sinf, bkt, rel_bias, attn_sinks, conv_w_full, l, nb, nc)
        (w_br, w_out), token = mid_fwd(l, br)
        h_next = merge_fwd(h, br, p_m, w_br, w_out, g_post, l, order(token))
        acts.append((h, hb, p_abc, p_m, br, states, w_in, w_br, w_out))
        h = h_next

    loss_part, d_h = loss_head(h, loss_target.reshape(nb * seq, D_MODEL), nb, nc)

    small = [None] * DEPTH
    token = None
    for l in reversed(range(DEPTH)):
        h_in, hb, p_abc, p_m, br, states, w_in, w_br, w_out = acts[l]
        d_br, d_m, d_gpost, g_wbr, g_wout = merge_bwd(d_h, br, p_m, w_br, w_out, g_post, l, order(token))
        d_abc, d_rb, d_sk, d_cw = mixers_bwd(p_abc, d_br, states, cosf, sinf, bkt, rel_bias,
                                             attn_sinks, conv_w_full, l, nb, nc)
        g_win = proj_wgrad(hb, d_abc, d_m)
        token = grads_done(l, [g_win, g_wbr, g_wout])
        d_h, d_gpre = proj_dgrad(d_abc, d_m, w_in, h_in, g_pre, l, d_h, order(token))
        token = bwd_done(l, d_h)
        small[l] = (d_gpre[0], d_gpost[0], d_rb, d_sk, d_cw[0:3])

    d_h3 = d_h.reshape(nb, lp, D_MODEL)
    d_x = d_h3[:, BLOCK:]
    d_meta = d_h3[:, PAD_FRONT:BLOCK]
    sp = jnp.concatenate([
        d_meta.reshape(-1, LANES),
        jnp.stack([small[l][0] for l in range(DEPTH)]).reshape(-1, LANES),
        jnp.stack([small[l][1] for l in range(DEPTH)]).reshape(-1, LANES),
        jnp.concatenate([small[l][2] for l in range(DEPTH)], axis=0),
        jnp.concatenate([small[l][3] for l in range(DEPTH)], axis=0),
        jnp.stack([small[l][4] for l in range(DEPTH)]).reshape(-1, LANES),
        loss_part], axis=0)
    return d_x, sp


def kernel(x, meta_tokens, rel_bias, norm_pre, w_in, conv_w, attn_sinks, w_branch, w_out, norm_post, loss_target, m_meta_tokens, m_rel_bias, m_norm_pre, m_w_in, m_conv_w, m_attn_sinks, m_w_branch, m_w_out, m_norm_post, v_meta_tokens, v_rel_bias, v_norm_pre, v_w_in, v_conv_w, v_attn_sinks, v_w_branch, v_w_out, v_norm_post):
    assert x.shape[0] == 2 and SP_META == 2 * N_META * D_MODEL // LANES
    px, py, pc = _place()
    chip = 2 * px + py

    c_arr = jnp.reshape(pc, (1,)).astype(jnp.int32)
    where = jnp.stack([chip, pc]).astype(jnp.int32)
    tr_ = lambda a: jnp.swapaxes(a, 1, 2)
    w3 = [tr_(w_in), w_branch.reshape(DEPTH, N_BRANCH * BRANCH_WIDTH, SHARD_D), w_out]
    halves = lambda a: a.reshape(2, a.shape[0] // 2, a.shape[1])

    def as_weights(bufs):
        a_in, a_br, a_out = bufs
        return (a_in.reshape(PROJ_WIDTH, D_MODEL), a_br.reshape(N_CHIPS, N_BRANCH, BRANCH_WIDTH, SHARD_D),
                a_out.reshape(D_MODEL, D_MODEL))

    n_meta_rows = N_META * SHARD_D // LANES
    side = jnp.concatenate([meta_tokens.reshape(-1), conv_w.reshape(-1)]).reshape(-1, LANES)
    side = jnp.concatenate([side, jnp.zeros((SIDE_ROWS - side.shape[0], LANES), F32)], axis=0)
    slots = [[_own_slot(halves(w[l].astype(BF16)), chip) for w in w3] for l in range(DEPTH)]
    send0, recv0, flying0, _ = gather_start([_own_slot(halves(side), chip)] + slots[0], where)
    side_chips = gather_forward(gather_wait(flying0[:1], send0, recv0, where))[0].reshape(N_CHIPS, SIDE_ROWS, LANES)
    meta_full = jnp.moveaxis(side_chips[:, :n_meta_rows].reshape(N_CHIPS, N_META, SHARD_D), 0, 1).reshape(N_META, D_MODEL)
    conv_full = jnp.moveaxis(side_chips[:, n_meta_rows:n_meta_rows + 6].reshape(N_CHIPS, DEPTH, 3, LANES), 0, 2).reshape(DEPTH, 3, BRANCH_WIDTH)
    inbound = {}

    def weights_of(l, h):
        if l == 0:
            inbound[0] = gather_forward(gather_wait(flying0[1:2], send0, recv0, h, first=1))
            inbound[1] = gather_start(slots[1], inbound[0][0])
            return inbound[0][0].reshape(PROJ_WIDTH, D_MODEL), inbound[1][3]
        send, recv, thru = inbound[1]
        inbound[1] = as_weights(forward_wait(thru, send, recv, h))
        return inbound[1][0], None

    def mid_fwd(l, br):
        if l == 0:
            rest = gather_forward(gather_wait(flying0[2:], send0, recv0, br, first=2))
            send, recv, flying1, _ = inbound[1]
            send, recv, thru, started = forward_start(gather_wait(flying1, send, recv, rest[0]))
            inbound[1] = (send, recv, thru)
            return as_weights(inbound[0] + rest)[1:], started
        return inbound[1][1:], None

    reduced = [None] * DEPTH
    flying = {}

    def finish_reduce(l, after):
        partials, parts = scatter_wait(*flying[l], after)
        reduced[l] = sibling_share_layer([sum_chips(a, p, where) for a, p in zip(partials, parts)])

    def start_scatter(l, full, others):
        send, recv, thru, lands, started = scatter_start([add_own_half(g, o, c_arr) for g, o in zip(full, others)])
        flying[l] = (thru, lands, send, recv)
        return started

    m3 = [tr_(m_w_in), m_w_branch.reshape(w3[1].shape), m_w_out]
    v3 = [tr_(v_w_in), v_w_branch.reshape(w3[1].shape), v_w_out]
    big = [None] * 3

    def adamw_of(l, after):
        for t in range(3):
            big[t] = adamw_layer(w3[t], reduced[l][t].reshape(w3[t].shape[1:]), m3[t], v3[t], l, big[t], after)

    def grads_done(l, grads):
        full = [g.reshape(N_CHIPS, 2, g.size // (2 * N_CHIPS * g.shape[-1]), g.shape[-1]) for g in grads]
        if l == 0:
            finish_reduce(1, grads[0])
        send, recv, thru, lands, started = swap_start(full, where if l == 1 else reduced[1][0])
        if l == 1:
            flying["swap"] = (thru, lands, send, recv)
            return started
        adamw_of(1, started)
        return start_scatter(0, *swap_wait(thru, lands, send, recv, big[0][1]))

    def bwd_done(l, d_h):
        if l == 1:
            return start_scatter(1, *swap_wait(*flying["swap"], d_h))
        return None

    d_x, sp = local_step(x, loss_target, meta_full, rel_bias, norm_pre, conv_full, attn_sinks, norm_post,
                         weights_of, mid_fwd, grads_done, bwd_done)
    finish_reduce(0, sp)

    s_send, s_recv, s_buf, s_started = small_start(sp, 4 * px + 2 * py + pc, reduced[0][0])

    adamw_of(0, s_started)
    g_in, *u_in = [tr_(a) for a in big[0]]
    g_br, *u_br = [a.reshape(w_branch.shape) for a in big[1]]
    g_out, *u_out = big[2]

    meta_rows, rest = sum_small(small_wait(s_buf, s_send, s_recv, big[0][1]))
    o = 0
    g_meta_full = meta_rows.reshape(N_META, D_MODEL)
    g_norm_pre = rest[o:o + SP_NORM].reshape(DEPTH, D_MODEL); o += SP_NORM
    g_norm_post = rest[o:o + SP_NORM].reshape(DEPTH, D_MODEL); o += SP_NORM
    g_rel_bias = rest[o:o + N_BUCKETS, :ATT_HEADS]; o += SP_RB
    g_sinks = rest[o:o + SP_SINK, 0].reshape(DEPTH, ATT_HEADS); o += SP_SINK
    g_conv_full = rest[o:o + SP_CONV].reshape(DEPTH, 3, BRANCH_WIDTH); o += SP_CONV
    loss = rest[o, 0]
    g_meta = lax.dynamic_slice_in_dim(g_meta_full, chip * SHARD_D, SHARD_D, axis=1)
    g_conv = lax.dynamic_slice_in_dim(g_conv_full, chip * LANES, LANES, axis=2)

    to2 = lambda a: a.reshape(-1, a.shape[-1])
    smalls = [(meta_tokens, g_meta, m_meta_tokens, v_meta_tokens),
              (rel_bias, g_rel_bias, m_rel_bias, v_rel_bias),
              (norm_pre, g_norm_pre, m_norm_pre, v_norm_pre),
              (to2(conv_w), to2(g_conv), to2(m_conv_w), to2(v_conv_w)),
              (attn_sinks, g_sinks, m_attn_sinks, v_attn_sinks),
              (norm_post, g_norm_post, m_norm_post, v_norm_post)]
    u_meta, u_rb, u_npre, u_conv, u_sink, u_npost = adamw_small(smalls)
    u_conv = tuple(a.reshape(conv_w.shape) for a in u_conv)

    grads = [g_meta, g_rel_bias, g_norm_pre, g_in, g_conv, g_sinks, g_br, g_out, g_norm_post]
    upd = [u_meta, u_rb, u_npre, u_in, u_conv, u_sink, u_br, u_out, u_npost]
    return (loss, d_x, *grads, *[u[0] for u in upd], *[u[1] for u in upd], *[u[2] for u in upd])
```

```python
import math

import numpy as np
import jax
import jax.numpy as jnp
from jax import lax
from jax.experimental import pallas as pl
from jax.experimental.pallas import tpu as pltpu

F32 = jnp.float32
BF16 = jnp.bfloat16
MESH = pl.DeviceIdType.MESH

D_MODEL = 1024
DEPTH = 2
N_META = 16
BLOCK = 128
PAD_FRONT = BLOCK - N_META
ATT_HEADS = 8
ATT_HEAD_DIM = 64
N_BUCKETS = 32
MAX_EXACT = 16
MAX_DISTANCE = 128
RET_HEADS = 4
ROT_BASE = 10000.0
N_BRANCH = 3
BRANCH_WIDTH = 512
PROJ_WIDTH = 8448
ABC_WIDTH = 5376
MERGE_WIDTH = N_BRANCH * D_MODEL
RMS_EPS = 1e-6
GN_EPS = 1e-6
NEG_INF = -1e30
ATT_SCALE = ATT_HEAD_DIM ** -0.5
RET_SCALE = BLOCK ** -0.5
LOG_GAMMA = tuple(math.log1p(-(2.0 ** (-5.0 - h))) for h in range(RET_HEADS))

C_AQ, C_AK, C_AV, C_AG = 0, 512, 640, 768
C_RQ, C_RK, C_RV, C_RG = 1280, 1792, 2304, 2816
C_CB, C_CC, C_CX, C_CG = 3328, 3840, 4352, 4864

ADAM_LR = 0.001
ADAM_B1 = 0.9
ADAM_B2 = 0.999
ADAM_EPS = 1e-08
ADAM_WD = 0.01
ADAM_STEP = 10

N_CHIPS = 4
SHARD_D = D_MODEL // N_CHIPS
LANES = 128

VMEM_LIMIT = 56 * 1024 * 1024
COL_TILE = 768
ROW_TILE = 1088
PROJ_ROW_TILE = 2176


def _cparams(*sem):
    return pltpu.CompilerParams(dimension_semantics=sem, vmem_limit_bytes=VMEM_LIMIT)


def _nt(a, b):
    return lax.dot_general(a, b, (((1,), (1,)), ((), ())), preferred_element_type=F32)


def _tn(a, b):
    return lax.dot_general(a, b, (((0,), (0,)), ((), ())), preferred_element_type=F32)


def _nn(a, b):
    return jnp.dot(a, b, preferred_element_type=F32)


def _sigmoid(x):
    return 0.5 * jnp.tanh(0.5 * x) + 0.5


def _silu(x):
    return x * _sigmoid(x)


def _dsilu(x):
    s = _sigmoid(x)
    return s * (1.0 + x * (1.0 - s))


def _bucket_table():
    r = np.arange(BLOCK)[:, None]
    c = np.arange(2 * BLOCK)[None, :]
    n = np.maximum(BLOCK + r - c, 0)
    nf = np.maximum(n, 1).astype(np.float32)
    large = MAX_EXACT + (np.log(nf / MAX_EXACT) / math.log(MAX_DISTANCE / MAX_EXACT)
                         * (N_BUCKETS - MAX_EXACT)).astype(np.int32)
    large = np.minimum(large, N_BUCKETS - 1)
    return np.where(n < MAX_EXACT, n, large).astype(np.int32)


def _rot_tables(lp):
    half = BLOCK // 2
    pos = (jnp.arange(lp) - PAD_FRONT).astype(F32)
    theta = 1.0 / (ROT_BASE ** jnp.linspace(0.0, 1.0, half, dtype=F32))
    ang = pos[:, None] * theta[None, :]
    cos, sin = jnp.cos(ang), jnp.sin(ang)
    return jnp.concatenate([cos, cos], axis=1), jnp.concatenate([-sin, sin], axis=1)


def norm_matmul(x2d, g, layer, w, col0_blocks, n_col_blocks, after):
    t = x2d.shape[0]
    tm = PROJ_ROW_TILE if t % PROJ_ROW_TILE == 0 else BLOCK

    def body(x_ref, g_ref, w_ref, after_ref, hb_ref, o_ref):
        @pl.when(pl.program_id(1) == 0)
        def _():
            x = x_ref[...]
            r = lax.rsqrt(jnp.mean(x * x, axis=-1, keepdims=True) + RMS_EPS)
            hb_ref[...] = (x * r * g_ref[...]).astype(BF16)

        o_ref[...] = _nt(hb_ref[...], w_ref[...]).astype(BF16)

    return pl.pallas_call(
        body, name="norm_matmul",
        grid=(t // tm, n_col_blocks),
        in_specs=[pl.BlockSpec((tm, D_MODEL), lambda i, j: (i, 0)),
                  pl.BlockSpec((None, 1, D_MODEL), lambda i, j: (layer, 0, 0)),
                  pl.BlockSpec((COL_TILE, D_MODEL), lambda i, j: (j + col0_blocks, 0)),
                  ANY],
        out_specs=[pl.BlockSpec((tm, D_MODEL), lambda i, j: (i, 0)),
                   pl.BlockSpec((tm, COL_TILE), lambda i, j: (i, j))],
        out_shape=[jax.ShapeDtypeStruct((t, D_MODEL), BF16),
                   jax.ShapeDtypeStruct((t, n_col_blocks * COL_TILE), BF16)],
        compiler_params=_cparams("parallel", "arbitrary"),
    )(x2d, g, w, after)


def matmul_cols(a, w, col0_blocks, n_col_blocks):
    t, k = a.shape
    tm = PROJ_ROW_TILE if t % PROJ_ROW_TILE == 0 else BLOCK

    def body(a_ref, w_ref, o_ref):
        o_ref[...] = _nt(a_ref[...], w_ref[...]).astype(BF16)

    return pl.pallas_call(
        body, name="matmul_cols",
        grid=(t // tm, n_col_blocks),
        in_specs=[pl.BlockSpec((tm, k), lambda i, j: (i, 0)),
                  pl.BlockSpec((COL_TILE, k), lambda i, j: (j + col0_blocks, 0))],
        out_specs=pl.BlockSpec((tm, COL_TILE), lambda i, j: (i, j)),
        out_shape=jax.ShapeDtypeStruct((t, n_col_blocks * COL_TILE), BF16),
        compiler_params=_cparams("parallel", "arbitrary"),
    )(a, w)


class _Widened:
    def __init__(self, ref):
        self.ref = ref

    def __getitem__(self, idx):
        return self.ref[idx].astype(F32)


def _build_bias(bkt_ref, rb_ref, bias_s):
    bkt = bkt_ref[...]
    for h in range(ATT_HEADS):
        acc = jnp.zeros((BLOCK, 2 * BLOCK), F32)
        for b in range(N_BUCKETS):
            acc = jnp.where(bkt == b, rb_ref[b, h], acc)
        bias_s[h] = acc


def _band_mask(n):
    r = lax.broadcasted_iota(jnp.int32, (BLOCK, 2 * BLOCK), 0)
    c = lax.broadcasted_iota(jnp.int32, (BLOCK, 2 * BLOCK), 1)
    key_pos = (n - 1) * BLOCK + c
    return (c > r) & (c <= r + BLOCK) & (key_pos >= PAD_FRONT)


def _split_heads(kv, kh):
    lane = lax.broadcasted_iota(jnp.int32, kv.shape, 1)
    if kh == 0:
        lo = jnp.where(lane < ATT_HEAD_DIM, kv, 0.0)
        hi = pltpu.roll(lo, ATT_HEAD_DIM, 1)
    else:
        hi = jnp.where(lane >= ATT_HEAD_DIM, kv, 0.0)
        lo = pltpu.roll(hi, ATT_HEAD_DIM, 1)
    return lo, hi


def _merge_heads(acc_lo, acc_hi, kh):
    lane = lax.broadcasted_iota(jnp.int32, acc_lo.shape, 1)
    if kh == 0:
        return jnp.where(lane < ATT_HEAD_DIM, acc_lo + pltpu.roll(acc_hi, ATT_HEAD_DIM, 1), 0.0)
    return jnp.where(lane >= ATT_HEAD_DIM, acc_hi + pltpu.roll(acc_lo, ATT_HEAD_DIM, 1), 0.0)


def _softmax_of(qk, bias_h, mask, sink_h):
    s = qk * ATT_SCALE + bias_h
    s = jnp.where(mask, s, NEG_INF)
    m = jnp.maximum(jnp.max(s, axis=-1, keepdims=True), sink_h)
    p = jnp.exp(s - m)
    es = jnp.exp(sink_h - m)
    inv = 1.0 / (jnp.sum(p, axis=-1, keepdims=True) + es)
    return p * inv, es * inv


def _rot(t, cosf, sinf):
    return t * cosf + pltpu.roll(t, BLOCK // 2, 1) * sinf


def _rot_t(d, cosf, sinf):
    return d * cosf + pltpu.roll(d * sinf, BLOCK // 2, 1)


def _decay_tables(h):
    lg = LOG_GAMMA[h]
    i = lax.broadcasted_iota(jnp.int32, (BLOCK, BLOCK), 0)
    j = lax.broadcasted_iota(jnp.int32, (BLOCK, BLOCK), 1)
    diff = (i - j).astype(F32)
    dm = jnp.where(diff >= 0, jnp.exp(diff * lg), 0.0)
    row = lax.broadcasted_iota(jnp.int32, (BLOCK, 1), 0).astype(F32)
    zeta = jnp.exp((BLOCK - 1 - row) * lg)
    xi = jnp.exp((row + 1.0) * lg)
    return dm, zeta, xi, math.exp(BLOCK * lg)


def _valid_col(n):
    row = lax.broadcasted_iota(jnp.int32, (BLOCK, 1), 0)
    return ((n * BLOCK + row) >= PAD_FRONT).astype(F32)


def _shift_down(cur, prev, k):
    row = lax.broadcasted_iota(jnp.int32, cur.shape, 0)
    return jnp.where(row >= k, pltpu.roll(cur, k, 0), pltpu.roll(prev, k, 0))


def _shift_up(cur, nxt, k):
    row = lax.broadcasted_iota(jnp.int32, cur.shape, 0)
    return jnp.where(row < BLOCK - k, pltpu.roll(cur, BLOCK - k, 0), pltpu.roll(nxt, BLOCK - k, 0))


def mixers_fwd(proj, cosf, sinf, bkt, rel_bias, sinks, conv_w, layer, nb, nc):
    def body(p_ref, cos_ref, sin_ref, bkt_ref, rb_ref, sk_ref, cw_ref, br_ref, st_ref,
             bias_s, kv_s, state_s, u_s):
        p_ref = _Widened(p_ref)
        n = pl.program_id(0)

        @pl.when(n == 0)
        def _():
            _build_bias(bkt_ref, rb_ref, bias_s)
            kv_s[:, 0:BLOCK, :] = jnp.zeros((nb, BLOCK, 2 * BLOCK), F32)
            state_s[...] = jnp.zeros_like(state_s)
            u_s[...] = jnp.zeros_like(u_s)

        valid = _valid_col(n)
        mask = _band_mask(n)
        ex = range(nb)

        for b in ex:
            kv_s[b, BLOCK:2 * BLOCK, :] = p_ref[b, :, C_AK:C_AK + 2 * BLOCK]
        for kh in range(2):
            ks = [[t.astype(BF16) for t in _split_heads(kv_s[b, :, 0:BLOCK], kh)] for b in ex]
            vs = [[t.astype(BF16) for t in _split_heads(kv_s[b, :, BLOCK:2 * BLOCK], kh)] for b in ex]
            pairs = [(b, 2 * kh + jj) for jj in range(2) for b in ex]
            subs = [(b, j, x) for (b, j) in pairs for x in range(2)]
            qb_ = {(b, j): p_ref[b, :, C_AQ + BLOCK * j:C_AQ + BLOCK * (j + 1)].astype(BF16) for (b, j) in pairs}
            qk_ = {(b, j, x): _nt(qb_[(b, j)], ks[b][x]) for (b, j, x) in subs}
            pb_ = {}
            for u in subs:
                h = 2 * u[1] + u[2]
                pb_[u] = _softmax_of(qk_[u], bias_s[h], mask, sk_ref[layer, h])[0].astype(BF16)
            o_ = {u: _nn(pb_[u], vs[u[0]][u[2]]) for u in subs}
            for (b, j) in pairs:
                gate = p_ref[b, :, C_AG + BLOCK * j:C_AG + BLOCK * (j + 1)]
                br_ref[b, :, BLOCK * j:BLOCK * (j + 1)] = ((o_[(b, j, 0)] + o_[(b, j, 1)]) * _silu(gate)).astype(BF16)
        for b in ex:
            kv_s[b, 0:BLOCK, :] = kv_s[b, BLOCK:2 * BLOCK, :]

        cosv = cos_ref[...]
        sinv = sin_ref[...]
        tabs = [_decay_tables(h) for h in range(RET_HEADS)]
        units = [(b, h) for h in range(RET_HEADS) for b in ex]
        sl = lambda c0, h: slice(c0 + BLOCK * h, c0 + BLOCK * (h + 1))
        q_, k_, v_, sp_ = {}, {}, {}, {}
        for u in units:
            b, h = u
            q_[u] = _rot(p_ref[b, :, sl(C_RQ, h)], cosv, sinv).astype(BF16)
            k_[u] = (_rot(p_ref[b, :, sl(C_RK, h)], cosv, sinv) * RET_SCALE * valid).astype(BF16)
            v_[u] = p_ref[b, :, sl(C_RV, h)]
            sp_[u] = state_s[b, h]
            st_ref[b, 0, h] = sp_[u]
        qk_ = {u: _nt(q_[u], k_[u]) for u in units}
        qs_ = {u: _nn(q_[u], sp_[u].astype(BF16)) for u in units}
        kv_ = {u: _tn(k_[u], (v_[u] * tabs[u[1]][1]).astype(BF16)) for u in units}
        a_ = {u: (qk_[u] * tabs[u[1]][0]).astype(BF16) for u in units}
        av_ = {u: _nn(a_[u], v_[u].astype(BF16)) for u in units}
        for u in units:
            b, h = u
            o = av_[u] + tabs[h][2] * qs_[u]
            mu = jnp.mean(o, axis=-1, keepdims=True)
            var = jnp.mean(jnp.square(o - mu), axis=-1, keepdims=True)
            oh = (o - mu) * lax.rsqrt(var + GN_EPS)
            gate = p_ref[b, :, sl(C_RG, h)]
            br_ref[b, :, BRANCH_WIDTH + BLOCK * h:BRANCH_WIDTH + BLOCK * (h + 1)] = (oh * _silu(gate)).astype(BF16)
            state_s[b, h] = tabs[h][3] * sp_[u] + kv_[u]

        for b in ex:
            u = p_ref[b, :, C_CC:C_CC + BRANCH_WIDTH] * p_ref[b, :, C_CX:C_CX + BRANCH_WIDTH] * valid
            u_prev = u_s[b]
            y = (cw_ref[0:1, :] * _shift_down(u, u_prev, 2) + cw_ref[1:2, :] * _shift_down(u, u_prev, 1)
                 + cw_ref[2:3, :] * u)
            yc = p_ref[b, :, C_CB:C_CB + BRANCH_WIDTH] * y * _silu(p_ref[b, :, C_CG:C_CG + BRANCH_WIDTH])
            br_ref[b, :, 2 * BRANCH_WIDTH:3 * BRANCH_WIDTH] = yc.astype(BF16)
            u_s[b] = u

    lp = nc * BLOCK
    smem = pl.BlockSpec(memory_space=pltpu.SMEM)
    br, states = pl.pallas_call(
        body, name="mixers_fwd",
        grid=(nc,),
        in_specs=[pl.BlockSpec((nb, BLOCK, ABC_WIDTH), lambda n: (0, n, 0)),
                  pl.BlockSpec((BLOCK, BLOCK), lambda n: (n, 0)),
                  pl.BlockSpec((BLOCK, BLOCK), lambda n: (n, 0)),
                  pl.BlockSpec((BLOCK, 2 * BLOCK), lambda n: (0, 0)),
                  smem, smem,
                  pl.BlockSpec((None, 3, BRANCH_WIDTH), lambda n: (layer, 0, 0))],
        out_specs=[pl.BlockSpec((nb, BLOCK, N_BRANCH * BRANCH_WIDTH), lambda n: (0, n, 0)),
                   pl.BlockSpec((nb, 1, RET_HEADS, BLOCK, BLOCK), lambda n: (0, n, 0, 0, 0))],
        out_shape=[jax.ShapeDtypeStruct((nb, lp, N_BRANCH * BRANCH_WIDTH), BF16),
                   jax.ShapeDtypeStruct((nb, nc, RET_HEADS, BLOCK, BLOCK), F32)],
        scratch_shapes=[pltpu.VMEM((ATT_HEADS, BLOCK, 2 * BLOCK), F32),
                        pltpu.VMEM((nb, 2 * BLOCK, 2 * BLOCK), F32),
                        pltpu.VMEM((nb, RET_HEADS, BLOCK, BLOCK), F32),
                        pltpu.VMEM((nb, BLOCK, BRANCH_WIDTH), F32)],
        compiler_params=_cparams("arbitrary"),
    )(proj.reshape(nb, lp, ABC_WIDTH), cosf, sinf, bkt, rel_bias, sinks, conv_w)
    return br.reshape(nb * lp, N_BRANCH * BRANCH_WIDTH), states


def mixers_bwd(proj, d_br, states, cosf, sinf, bkt, rel_bias, sinks, conv_w, layer, nb, nc):
    def body(p_ref, kvp_ref, cp_ref, dbr_ref, st_ref, cos_ref, sin_ref, bkt_ref, rb_ref, sk_ref, cw_ref,
             dp_ref, drb_ref, dsk_ref, dcw_ref,
             bias_s, dbias_s, dkv_s, g_s, dy_s):
        p_ref, kvp_ref, cp_ref, dbr_ref = [_Widened(r) for r in (p_ref, kvp_ref, cp_ref, dbr_ref)]
        step = pl.program_id(0)
        n = nc - 1 - step
        ex = range(nb)

        @pl.when(step == 0)
        def _():
            _build_bias(bkt_ref, rb_ref, bias_s)
            dbias_s[...] = jnp.zeros_like(dbias_s)
            dsk_ref[...] = jnp.zeros_like(dsk_ref)
            dcw_ref[...] = jnp.zeros_like(dcw_ref)
            drb_ref[...] = jnp.zeros_like(drb_ref)
            dkv_s[...] = jnp.zeros_like(dkv_s)
            g_s[...] = jnp.zeros_like(g_s)
            dy_s[...] = jnp.zeros_like(dy_s)

        valid = _valid_col(n)
        mask = _band_mask(n)
        has_prev = (n > 0).astype(F32)

        k_all, v_all = [], []
        for b in ex:
            kv_prev = kvp_ref[b] * has_prev
            kv_cur = p_ref[b, :, C_AK:C_AK + 2 * BLOCK]
            k_all.append(jnp.concatenate([kv_prev[:, 0:BLOCK], kv_cur[:, 0:BLOCK]], axis=0))
            v_all.append(jnp.concatenate([kv_prev[:, BLOCK:], kv_cur[:, BLOCK:]], axis=0))
        zero2 = jnp.zeros((2 * BLOCK, BLOCK), F32)
        dk_tot = [zero2 for _ in ex]
        dv_tot = [zero2 for _ in ex]
        for kh in range(2):
            ks = [[t.astype(BF16) for t in _split_heads(k_all[b], kh)] for b in ex]
            vs = [[t.astype(BF16) for t in _split_heads(v_all[b], kh)] for b in ex]
            pairs = [(b, 2 * kh + jj) for jj in range(2) for b in ex]
            subs = [(b, j, x) for (b, j) in pairs for x in range(2)]
            qb_, gate_, dya_, do2_ = {}, {}, {}, {}
            for w in pairs:
                b, j = w
                qb_[w] = p_ref[b, :, C_AQ + BLOCK * j:C_AQ + BLOCK * (j + 1)].astype(BF16)
                gate_[w] = p_ref[b, :, C_AG + BLOCK * j:C_AG + BLOCK * (j + 1)]
                dya_[w] = dbr_ref[b, :, BLOCK * j:BLOCK * (j + 1)]
                do2_[w] = (dya_[w] * _silu(gate_[w])).astype(BF16)
            qk_ = {(b, j, x): _nt(qb_[(b, j)], ks[b][x]) for (b, j, x) in subs}
            dpm_ = {(b, j, x): _nt(do2_[(b, j)], vs[b][x]) for (b, j, x) in subs}
            pb_, dsb_ = {}, {}
            for u in subs:
                b, j, x = u
                h = 2 * j + x
                p, p_sink = _softmax_of(qk_[u], bias_s[h], mask, sk_ref[layer, h])
                pb_[u] = p.astype(BF16)
                delta = jnp.sum(p * dpm_[u], axis=-1, keepdims=True)
                ds = p * (dpm_[u] - delta)
                dbias_s[h] += ds
                dsk_ref[h:h + 1, :] += jnp.broadcast_to(
                    jnp.sum(-p_sink * delta, axis=0, keepdims=True), (1, BLOCK))
                dsb_[u] = ds.astype(BF16)
            o_ = {u: _nn(pb_[u], vs[u[0]][u[2]]) for u in subs}
            dq_ = {u: _nn(dsb_[u], ks[u[0]][u[2]]) for u in subs}
            dkm_ = {u: _tn(dsb_[u], qb_[(u[0], u[1])]) for u in subs}
            dvm_ = {u: _tn(pb_[u], do2_[(u[0], u[1])]) for u in subs}
            for w in pairs:
                b, j = w
                o2 = o_[(b, j, 0)] + o_[(b, j, 1)]
                dq2 = (dq_[(b, j, 0)] + dq_[(b, j, 1)]) * ATT_SCALE
                dp_ref[b, :, C_AQ + BLOCK * j:C_AQ + BLOCK * (j + 1)] = dq2.astype(BF16)
                dp_ref[b, :, C_AG + BLOCK * j:C_AG + BLOCK * (j + 1)] = (
                    dya_[w] * o2 * _dsilu(gate_[w])).astype(BF16)
            for b in ex:
                j0, j1 = 2 * kh, 2 * kh + 1
                dk_lo = (dkm_[(b, j0, 0)] + dkm_[(b, j1, 0)]) * ATT_SCALE
                dk_hi = (dkm_[(b, j0, 1)] + dkm_[(b, j1, 1)]) * ATT_SCALE
                dk_tot[b] = dk_tot[b] + _merge_heads(dk_lo, dk_hi, kh)
                dv_tot[b] = dv_tot[b] + _merge_heads(dvm_[(b, j0, 0)] + dvm_[(b, j1, 0)],
                                                     dvm_[(b, j0, 1)] + dvm_[(b, j1, 1)], kh)
        for b in ex:
            dp_ref[b, :, C_AK:C_AK + BLOCK] = (dk_tot[b][BLOCK:, :] + dkv_s[b, :, 0:BLOCK]).astype(BF16)
            dp_ref[b, :, C_AV:C_AV + BLOCK] = (dv_tot[b][BLOCK:, :] + dkv_s[b, :, BLOCK:]).astype(BF16)
            dkv_s[b, :, 0:BLOCK] = dk_tot[b][0:BLOCK, :]
            dkv_s[b, :, BLOCK:] = dv_tot[b][0:BLOCK, :]

        cosv = cos_ref[...]
        sinv = sin_ref[...]
        tabs = [_decay_tables(h) for h in range(RET_HEADS)]
        units = [(b, h) for h in range(RET_HEADS) for b in ex]
        sl = lambda c0, h: slice(c0 + BLOCK * h, c0 + BLOCK * (h + 1))
        q_, k_, v_, vb_, sp_ = {}, {}, {}, {}, {}
        for u in units:
            b, h = u
            q_[u] = _rot(p_ref[b, :, sl(C_RQ, h)], cosv, sinv).astype(BF16)
            k_[u] = (_rot(p_ref[b, :, sl(C_RK, h)], cosv, sinv) * RET_SCALE * valid).astype(BF16)
            v_[u] = p_ref[b, :, sl(C_RV, h)]
            vb_[u] = v_[u].astype(BF16)
            sp_[u] = st_ref[b, 0, h].astype(BF16)
        qk_ = {u: _nt(q_[u], k_[u]) for u in units}
        qs_ = {u: _nn(q_[u], sp_[u]) for u in units}
        a_ = {u: (qk_[u] * tabs[u[1]][0]).astype(BF16) for u in units}
        av_ = {u: _nn(a_[u], vb_[u]) for u in units}
        dob_, dxo_ = {}, {}
        for u in units:
            b, h = u
            xi = tabs[h][2]
            o = av_[u] + xi * qs_[u]
            mu = jnp.mean(o, axis=-1, keepdims=True)
            var = jnp.mean(jnp.square(o - mu), axis=-1, keepdims=True)
            rstd = lax.rsqrt(var + GN_EPS)
            oh = (o - mu) * rstd
            gate = p_ref[b, :, sl(C_RG, h)]
            d_yr = dbr_ref[b, :, BRANCH_WIDTH + BLOCK * h:BRANCH_WIDTH + BLOCK * (h + 1)]
            dp_ref[b, :, sl(C_RG, h)] = (d_yr * oh * _dsilu(gate)).astype(BF16)
            doh = d_yr * _silu(gate)
            do = rstd * (doh - jnp.mean(doh, axis=-1, keepdims=True)
                         - oh * jnp.mean(doh * oh, axis=-1, keepdims=True))
            dob_[u] = do.astype(BF16)
            dxo_[u] = (do * xi).astype(BF16)
        dov_ = {u: _nt(dob_[u], vb_[u]) for u in units}
        dv1_ = {u: _tn(a_[u], dob_[u]) for u in units}
        dq1_ = {u: _nt(dxo_[u], sp_[u]) for u in units}
        gq_ = {u: _tn(q_[u], dxo_[u]) for u in units}
        da_, gb_, zv_ = {}, {}, {}
        for u in units:
            b, h = u
            da_[u] = (dov_[u] * tabs[h][0]).astype(BF16)
            g_next = g_s[b, h]
            gb_[u] = g_next.astype(BF16)
            zv_[u] = (v_[u] * tabs[h][1]).astype(BF16)
            g_s[b, h] = tabs[h][3] * g_next + gq_[u]
        dq2_ = {u: _nn(da_[u], k_[u]) for u in units}
        dk1_ = {u: _tn(da_[u], q_[u]) for u in units}
        dk2_ = {u: _nt(zv_[u], gb_[u]) for u in units}
        dv2_ = {u: _nn(k_[u], gb_[u]) for u in units}
        for u in units:
            b, h = u
            dp_ref[b, :, sl(C_RQ, h)] = _rot_t(dq2_[u] + dq1_[u], cosv, sinv).astype(BF16)
            dp_ref[b, :, sl(C_RK, h)] = _rot_t((dk1_[u] + dk2_[u]) * (RET_SCALE * valid), cosv, sinv).astype(BF16)
            dp_ref[b, :, sl(C_RV, h)] = (dv1_[u] + tabs[h][1] * dv2_[u]).astype(BF16)

        w0, w1, w2 = cw_ref[0:1, :], cw_ref[1:2, :], cw_ref[2:3, :]
        for b in ex:
            cb = p_ref[b, :, C_CB:C_CB + BRANCH_WIDTH]
            cc = p_ref[b, :, C_CC:C_CC + BRANCH_WIDTH]
            cx = p_ref[b, :, C_CX:C_CX + BRANCH_WIDTH]
            cg = p_ref[b, :, C_CG:C_CG + BRANCH_WIDTH]
            u = cc * cx * valid
            u_prev = (cp_ref[b, :, 0:BRANCH_WIDTH] * cp_ref[b, :, BRANCH_WIDTH:2 * BRANCH_WIDTH]
                      * (_valid_col(n - 1) * has_prev))
            u1 = _shift_down(u, u_prev, 1)
            u2 = _shift_down(u, u_prev, 2)
            y = w0 * u2 + w1 * u1 + w2 * u
            d_yc = dbr_ref[b, :, 2 * BRANCH_WIDTH:3 * BRANCH_WIDTH]
            sg = _silu(cg)
            dp_ref[b, :, C_CB:C_CB + BRANCH_WIDTH] = (d_yc * y * sg).astype(BF16)
            dp_ref[b, :, C_CG:C_CG + BRANCH_WIDTH] = (d_yc * cb * y * _dsilu(cg)).astype(BF16)
            dy = d_yc * cb * sg
            dy_next = dy_s[b]
            du = (w2 * dy + w1 * _shift_up(dy, dy_next, 1) + w0 * _shift_up(dy, dy_next, 2)) * valid
            dp_ref[b, :, C_CC:C_CC + BRANCH_WIDTH] = (du * cx).astype(BF16)
            dp_ref[b, :, C_CX:C_CX + BRANCH_WIDTH] = (du * cc).astype(BF16)
            dcw_ref[0:1, :] += jnp.sum(dy * u2, axis=0, keepdims=True)
            dcw_ref[1:2, :] += jnp.sum(dy * u1, axis=0, keepdims=True)
            dcw_ref[2:3, :] += jnp.sum(dy * u, axis=0, keepdims=True)
            dy_s[b] = dy

        @pl.when(step == nc - 1)
        def _():
            bkt = bkt_ref[...]
            row = lax.broadcasted_iota(jnp.int32, (N_BUCKETS, BLOCK), 0)
            lane = lax.broadcasted_iota(jnp.int32, (N_BUCKETS, BLOCK), 1)

            def one_bucket(bk, acc):
                sel = bkt == bk
                for h in range(ATT_HEADS):
                    t = jnp.where(sel, dbias_s[h], 0.0)
                    s = jnp.sum(jnp.sum(t, axis=1, keepdims=True), axis=0, keepdims=True)
                    acc = acc + jnp.where((row == bk) & (lane == h), jnp.broadcast_to(s, acc.shape), 0.0)
                return acc

            drb_ref[...] = lax.fori_loop(0, N_BUCKETS, one_bucket, jnp.zeros((N_BUCKETS, BLOCK), F32))

    lp = nc * BLOCK
    smem = pl.BlockSpec(memory_space=pltpu.SMEM)
    blk = lambda s: nc - 1 - s
    prev = lambda s: jnp.maximum(nc - 2 - s, 0)
    proj3 = proj.reshape(nb, lp, ABC_WIDTH)
    res = pl.pallas_call(
        body, name="mixers_bwd",
        grid=(nc,),
        in_specs=[pl.BlockSpec((nb, BLOCK, ABC_WIDTH), lambda s: (0, blk(s), 0)),
                  pl.BlockSpec((nb, BLOCK, 2 * BLOCK), lambda s: (0, prev(s), C_AK // (2 * BLOCK))),
                  pl.BlockSpec((nb, BLOCK, 1280), lambda s: (0, prev(s), C_CC // 1280)),
                  pl.BlockSpec((nb, BLOCK, N_BRANCH * BRANCH_WIDTH), lambda s: (0, blk(s), 0)),
                  pl.BlockSpec((nb, 1, RET_HEADS, BLOCK, BLOCK), lambda s: (0, blk(s), 0, 0, 0)),
                  pl.BlockSpec((BLOCK, BLOCK), lambda s: (blk(s), 0)),
                  pl.BlockSpec((BLOCK, BLOCK), lambda s: (blk(s), 0)),
                  pl.BlockSpec((BLOCK, 2 * BLOCK), lambda s: (0, 0)),
                  smem, smem,
                  pl.BlockSpec((None, 3, BRANCH_WIDTH), lambda s: (layer, 0, 0))],
        out_specs=[pl.BlockSpec((nb, BLOCK, ABC_WIDTH), lambda s: (0, blk(s), 0)),
                   pl.BlockSpec((N_BUCKETS, BLOCK), lambda s: (0, 0)),
                   pl.BlockSpec((ATT_HEADS, BLOCK), lambda s: (0, 0)),
                   pl.BlockSpec((8, BRANCH_WIDTH), lambda s: (0, 0))],
        out_shape=[jax.ShapeDtypeStruct((nb, lp, ABC_WIDTH), BF16),
                   jax.ShapeDtypeStruct((N_BUCKETS, BLOCK), F32),
                   jax.ShapeDtypeStruct((ATT_HEADS, BLOCK), F32),
                   jax.ShapeDtypeStruct((8, BRANCH_WIDTH), F32)],
        scratch_shapes=[pltpu.VMEM((ATT_HEADS, BLOCK, 2 * BLOCK), F32),
                        pltpu.VMEM((ATT_HEADS, BLOCK, 2 * BLOCK), F32),
                        pltpu.VMEM((nb, BLOCK, 2 * BLOCK), F32),
                        pltpu.VMEM((nb, RET_HEADS, BLOCK, BLOCK), F32),
                        pltpu.VMEM((nb, BLOCK, BRANCH_WIDTH), F32)],
        compiler_params=_cparams("arbitrary"),
    )(proj3, proj3, proj3, d_br.reshape(nb, lp, N_BRANCH * BRANCH_WIDTH), states, cosf, sinf, bkt, rel_bias, sinks,
      conv_w)
    return (res[0].reshape(nb * lp, ABC_WIDTH),) + tuple(res[1:])


MERGE_TILE = 256
MERGE_FWD_TILE = 544


def _merge_forward(br_ref, m_ref, wb_ref, wo_ref):
    bo, gates = [], []
    mixed_pre = None
    for g in range(N_BRANCH):
        br_g = br_ref[:, BRANCH_WIDTH * g:BRANCH_WIDTH * (g + 1)]
        bo_g = jnp.concatenate([_nn(br_g, wb_ref[p, g]) for p in range(N_CHIPS)], axis=1)
        gate_g = _sigmoid(m_ref[:, D_MODEL * g:D_MODEL * (g + 1)].astype(F32))
        bo.append(bo_g)
        gates.append(gate_g)
        mixed_pre = gate_g * bo_g if mixed_pre is None else mixed_pre + gate_g * bo_g
    mixed = _nn(mixed_pre.astype(BF16), wo_ref[...])
    r = lax.rsqrt(jnp.mean(mixed * mixed, axis=-1, keepdims=True) + RMS_EPS)
    return bo, gates, mixed_pre, mixed, r


def merge_fwd(x2d, br, pm, wb, wo, g_post, layer, after):
    t = x2d.shape[0]
    tm = MERGE_FWD_TILE if t % MERGE_FWD_TILE == 0 else BLOCK

    def body(x_ref, br_ref, m_ref, wb_ref, wo_ref, g_ref, after_ref, o_ref):
        _, _, _, mixed, r = _merge_forward(br_ref, m_ref, wb_ref, wo_ref)
        o_ref[...] = x_ref[...] + mixed * r * g_ref[...]

    return pl.pallas_call(
        body, name="merge_fwd",
        grid=(t // tm,),
        in_specs=[pl.BlockSpec((tm, D_MODEL), lambda i: (i, 0)),
                  pl.BlockSpec((tm, N_BRANCH * BRANCH_WIDTH), lambda i: (i, 0)),
                  pl.BlockSpec((tm, MERGE_WIDTH), lambda i: (i, 0)),
                  pl.BlockSpec((N_CHIPS, N_BRANCH, BRANCH_WIDTH, SHARD_D), lambda i: (0, 0, 0, 0)),
                  pl.BlockSpec((D_MODEL, D_MODEL), lambda i: (0, 0)),
                  pl.BlockSpec((None, 1, D_MODEL), lambda i: (layer, 0, 0)),
                  ANY],
        out_specs=pl.BlockSpec((tm, D_MODEL), lambda i: (i, 0)),
        out_shape=jax.ShapeDtypeStruct((t, D_MODEL), F32),
        compiler_params=_cparams("parallel"),
    )(x2d, br, pm, wb, wo, g_post, after)


def merge_fwd_loss(x2d, br, pm, wb, wo, g_post, layer, target, lp):
    t = x2d.shape[0]
    tm = MERGE_FWD_TILE if t % MERGE_FWD_TILE == 0 else BLOCK

    def body(x_ref, br_ref, m_ref, wb_ref, wo_ref, g_ref, t_ref, l_ref, d_ref):
        i = pl.program_id(0)

        @pl.when(i == 0)
        def _():
            l_ref[...] = jnp.zeros_like(l_ref)

        _, _, _, mixed, r = _merge_forward(br_ref, m_ref, wb_ref, wo_ref)
        y = x_ref[...] + mixed * r * g_ref[...]
        row = i * tm + lax.broadcasted_iota(jnp.int32, (tm, 1), 0)
        e = jnp.where(row % lp >= BLOCK, y - t_ref[...], 0.0)
        d_ref[...] = e * (1.0 / D_MODEL)
        s = jnp.sum(jnp.sum(e * e, axis=1, keepdims=True), axis=0, keepdims=True)
        l_ref[...] += jnp.broadcast_to(s * (0.5 / D_MODEL), l_ref.shape)

    return pl.pallas_call(
        body, name="merge_fwd_loss",
        grid=(t // tm,),
        in_specs=[pl.BlockSpec((tm, D_MODEL), lambda i: (i, 0)),
                  pl.BlockSpec((tm, N_BRANCH * BRANCH_WIDTH), lambda i: (i, 0)),
                  pl.BlockSpec((tm, MERGE_WIDTH), lambda i: (i, 0)),
                  pl.BlockSpec((N_CHIPS, N_BRANCH, BRANCH_WIDTH, SHARD_D), lambda i: (0, 0, 0, 0)),
                  pl.BlockSpec((D_MODEL, D_MODEL), lambda i: (0, 0)),
                  pl.BlockSpec((None, 1, D_MODEL), lambda i: (layer, 0, 0)),
                  pl.BlockSpec((tm, D_MODEL), lambda i: (i, 0))],
        out_specs=[pl.BlockSpec((8, BLOCK), lambda i: (0, 0)),
                   pl.BlockSpec((tm, D_MODEL), lambda i: (i, 0))],
        out_shape=[jax.ShapeDtypeStruct((8, BLOCK), F32),
                   jax.ShapeDtypeStruct((t, D_MODEL), F32)],
        compiler_params=_cparams("arbitrary"),
    )(x2d, br, pm, wb, wo, g_post, target)


def merge_bwd(d_out, br, pm, wb, wo, g_post, layer, after):
    t = d_out.shape[0]
    tm = MERGE_TILE if t % MERGE_TILE == 0 else BLOCK

    def body(do_ref, br_ref, m_ref, wb_ref, wo_ref, g_ref, after_ref, dbr_ref, dm_ref, dg_ref, dwb_ref, dwo_ref):

        @pl.when(pl.program_id(0) == 0)
        def _():
            dwb_ref[...] = jnp.zeros_like(dwb_ref)
            dwo_ref[...] = jnp.zeros_like(dwo_ref)
            dg_ref[...] = jnp.zeros_like(dg_ref)

        bo, gates, mixed_pre, mixed, r = _merge_forward(br_ref, m_ref, wb_ref, wo_ref)
        d_o = do_ref[...]
        nh = mixed * r
        dg_ref[0:1, :] += jnp.sum(d_o * nh, axis=0, keepdims=True)
        dn = d_o * g_ref[...]
        d_mixed = (r * (dn - nh * jnp.mean(dn * nh, axis=-1, keepdims=True))).astype(BF16)
        dwo_ref[...] += _tn(mixed_pre.astype(BF16), d_mixed)
        d_pre = _nt(d_mixed, wo_ref[...])
        for g in range(N_BRANCH):
            br_g = br_ref[:, BRANCH_WIDTH * g:BRANCH_WIDTH * (g + 1)]
            d_bo = (d_pre * gates[g]).astype(BF16)
            dm_ref[:, D_MODEL * g:D_MODEL * (g + 1)] = (
                d_pre * bo[g] * gates[g] * (1.0 - gates[g])).astype(BF16)
            d_br_g = None
            for p in range(N_CHIPS):
                d_bo_p = d_bo[:, SHARD_D * p:SHARD_D * (p + 1)]
                part = _nt(d_bo_p, wb_ref[p, g])
                d_br_g = part if d_br_g is None else d_br_g + part
                dwb_ref[p, g] += _tn(br_g, d_bo_p)
            dbr_ref[:, BRANCH_WIDTH * g:BRANCH_WIDTH * (g + 1)] = d_br_g.astype(BF16)

    return pl.pallas_call(
        body, name="merge_bwd",
        grid=(t // tm,),
        in_specs=[pl.BlockSpec((tm, D_MODEL), lambda i: (i, 0)),
                  pl.BlockSpec((tm, N_BRANCH * BRANCH_WIDTH), lambda i: (i, 0)),
                  pl.BlockSpec((tm, MERGE_WIDTH), lambda i: (i, 0)),
                  pl.BlockSpec((N_CHIPS, N_BRANCH, BRANCH_WIDTH, SHARD_D), lambda i: (0, 0, 0, 0)),
                  pl.BlockSpec((D_MODEL, D_MODEL), lambda i: (0, 0)),
                  pl.BlockSpec((None, 1, D_MODEL), lambda i: (layer, 0, 0)),
                  ANY],
        out_specs=[pl.BlockSpec((tm, N_BRANCH * BRANCH_WIDTH), lambda i: (i, 0)),
                   pl.BlockSpec((tm, MERGE_WIDTH), lambda i: (i, 0)),
                   pl.BlockSpec((8, D_MODEL), lambda i: (0, 0)),
                   pl.BlockSpec((N_CHIPS, N_BRANCH, BRANCH_WIDTH, SHARD_D), lambda i: (0, 0, 0, 0)),
                   pl.BlockSpec((D_MODEL, D_MODEL), lambda i: (0, 0))],
        out_shape=[jax.ShapeDtypeStruct((t, N_BRANCH * BRANCH_WIDTH), BF16),
                   jax.ShapeDtypeStruct((t, MERGE_WIDTH), BF16),
                   jax.ShapeDtypeStruct((8, D_MODEL), F32),
                   jax.ShapeDtypeStruct((N_CHIPS, N_BRANCH, BRANCH_WIDTH, SHARD_D), F32),
                   jax.ShapeDtypeStruct((D_MODEL, D_MODEL), F32)],
        compiler_params=_cparams("arbitrary"),
    )(d_out, br, pm, wb, wo, g_post, after)


N_ABC_TILES = ABC_WIDTH // COL_TILE
N_M_TILES = MERGE_WIDTH // COL_TILE


def proj_dgrad(d_abc, d_m, w, x2d, g, layer, d_out, after):
    t = x2d.shape[0]
    tm = ROW_TILE if t % ROW_TILE == 0 else BLOCK
    nk = N_ABC_TILES + N_M_TILES

    def body(da_ref, dm_ref, w_ref, x_ref, g_ref, do_ref, after_ref, dx_ref, dg_ref, acc):
        i = pl.program_id(0)
        k = pl.program_id(1)

        @pl.when((i == 0) & (k == 0))
        def _():
            dg_ref[...] = jnp.zeros_like(dg_ref)

        @pl.when(k == 0)
        def _():
            acc[...] = jnp.zeros_like(acc)

        @pl.when(k < N_ABC_TILES)
        def _():
            acc[...] += _nn(da_ref[...], w_ref[...])

        @pl.when(k >= N_ABC_TILES)
        def _():
            acc[...] += _nn(dm_ref[...], w_ref[...])

        @pl.when(k == nk - 1)
        def _():
            x = x_ref[...]
            r = lax.rsqrt(jnp.mean(x * x, axis=-1, keepdims=True) + RMS_EPS)
            nh = x * r
            dh = acc[...]
            dg_ref[0:1, :] += jnp.sum(dh * nh, axis=0, keepdims=True)
            dn = dh * g_ref[...]
            dx_ref[...] = do_ref[...] + r * (dn - nh * jnp.mean(dn * nh, axis=-1, keepdims=True))

    return pl.pallas_call(
        body, name="proj_dgrad",
        grid=(t // tm, nk),
        in_specs=[pl.BlockSpec((tm, COL_TILE), lambda i, k: (i, jnp.minimum(k, N_ABC_TILES - 1))),
                  pl.BlockSpec((tm, COL_TILE), lambda i, k: (i, jnp.maximum(k - N_ABC_TILES, 0))),
                  pl.BlockSpec((COL_TILE, D_MODEL), lambda i, k: (k, 0)),
                  pl.BlockSpec((tm, D_MODEL), lambda i, k: (i, 0)),
                  pl.BlockSpec((None, 1, D_MODEL), lambda i, k: (layer, 0, 0)),
                  pl.BlockSpec((tm, D_MODEL), lambda i, k: (i, 0)),
                  ANY],
        out_specs=[pl.BlockSpec((tm, D_MODEL), lambda i, k: (i, 0)),
                   pl.BlockSpec((8, D_MODEL), lambda i, k: (0, 0))],
        out_shape=[jax.ShapeDtypeStruct((t, D_MODEL), F32),
                   jax.ShapeDtypeStruct((8, D_MODEL), F32)],
        scratch_shapes=[pltpu.VMEM((tm, D_MODEL), F32)],
        compiler_params=_cparams("arbitrary", "arbitrary"),
    )(d_abc, d_m, w, x2d, g, d_out, after)


def proj_wgrad(hb, d_abc, d_m):
    t = hb.shape[0]
    nj = N_ABC_TILES + N_M_TILES

    def body(h_ref, da_ref, dm_ref, o_ref):
        j = pl.program_id(0)

        @pl.when(j < N_ABC_TILES)
        def _():
            o_ref[...] = _tn(da_ref[...], h_ref[...])

        @pl.when(j >= N_ABC_TILES)
        def _():
            o_ref[...] = _tn(dm_ref[...], h_ref[...])

    return pl.pallas_call(
        body, name="proj_wgrad",
        grid=(nj,),
        in_specs=[pl.BlockSpec((t, D_MODEL), lambda j: (0, 0)),
                  pl.BlockSpec((t, COL_TILE), lambda j: (0, jnp.minimum(j, N_ABC_TILES - 1))),
                  pl.BlockSpec((t, COL_TILE), lambda j: (0, jnp.maximum(j - N_ABC_TILES, 0)))],
        out_specs=pl.BlockSpec((COL_TILE, D_MODEL), lambda j: (j, 0)),
        out_shape=jax.ShapeDtypeStruct((PROJ_WIDTH, D_MODEL), F32),
        compiler_params=_cparams("arbitrary"),
    )(hb, d_abc, d_m)


def _adamw_math(w, g, m, v):
    m = ADAM_B1 * m + (1.0 - ADAM_B1) * g
    v = ADAM_B2 * v + (1.0 - ADAM_B2) * jnp.square(g)
    m_hat = m / (1.0 - ADAM_B1 ** ADAM_STEP)
    v_hat = v / (1.0 - ADAM_B2 ** ADAM_STEP)
    delta = -ADAM_LR * (m_hat / (jnp.sqrt(v_hat) + ADAM_EPS) + ADAM_WD * w)
    return delta, m, v


def adamw_layer(w, g, m, v, layer, acc, after):
    _, r, c = w.shape
    tr = _row_tile(r)

    def body(*refs):
        w_ref, g_ref, m_ref, v_ref = refs[:4]
        go_ref, d_ref, mo_ref, vo_ref = refs[-4:]
        g_val = g_ref[...]
        d, m_new, v_new = _adamw_math(w_ref[...], g_val, m_ref[...], v_ref[...])
        go_ref[...] = g_val
        d_ref[...] = d
        mo_ref[...] = m_new
        vo_ref[...] = v_new

    slab = pl.BlockSpec((None, tr, c), lambda i: (layer, i, 0))
    ins = [w, g, m, v, after]
    in_specs = [slab, pl.BlockSpec((tr, c), lambda i: (i, 0)), slab, slab, ANY]
    aliases = {}
    if acc is not None:
        ins += list(acc)
        in_specs += [ANY] * 4
        aliases = {5 + i: i for i in range(4)}
    return pl.pallas_call(
        body, name="adamw_layer",
        grid=(r // tr,),
        in_specs=in_specs, out_specs=[slab] * 4,
        out_shape=[jax.ShapeDtypeStruct(w.shape, F32)] * 4,
        input_output_aliases=aliases,
        compiler_params=_cparams("parallel"),
    )(*ins)


def adamw_small(params):
    k = len(params)

    def body(*refs):
        ins, outs = refs[:4 * k], refs[4 * k:]
        for i in range(k):
            d, m_new, v_new = _adamw_math(*[r[...] for r in ins[4 * i:4 * i + 4]])
            outs[3 * i][...] = d
            outs[3 * i + 1][...] = m_new
            outs[3 * i + 2][...] = v_new

    flat = [a for p in params for a in p]
    vm = pl.BlockSpec(memory_space=pltpu.VMEM)
    out_shape = [jax.ShapeDtypeStruct(p[0].shape, F32) for p in params for _ in range(3)]
    res = pl.pallas_call(
        body, name="adamw_small",
        in_specs=[vm] * len(flat), out_specs=[vm] * len(out_shape), out_shape=out_shape,
    )(*flat)
    return [tuple(res[3 * i:3 * i + 3]) for i in range(k)]


ANY = pl.BlockSpec(memory_space=pl.ANY)


def _place():
    return lax.axis_index("x"), lax.axis_index("y"), lax.axis_index("c")


HBM = pl.BlockSpec(memory_space=pltpu.HBM)
SEM = pl.BlockSpec(memory_space=pltpu.SEMAPHORE)
EFFECT = pltpu.SideEffectType.DATAFLOW_SIDE_EFFECTING


def _other_chips(x, y):
    return [(1 - x, y), (x, 1 - y), (1 - x, 1 - y)]


def _own_slot(shard, chip):
    buf = lax.empty((N_CHIPS,) + shard.shape, shard.dtype)
    return lax.dynamic_update_slice(buf, shard[None], (chip, 0, 0, 0))


def _hbm(a):
    return pltpu.with_memory_space_constraint(a, pltpu.HBM)


def gather_start(bufs, after):
    n = len(bufs)

    def body(*refs):
        g_refs = refs[:n]
        send_sems, recv_sems = refs[n + 1], refs[n + 2]
        token = refs[-1]
        x, y, c = _place()
        me_p = 2 * x + y
        for t in range(n):
            for k, (qx, qy) in enumerate(_other_chips(x, y)):
                slab = g_refs[t].at[me_p, c]
                pltpu.make_async_remote_copy(src_ref=slab, dst_ref=slab, send_sem=send_sems.at[3 * t + k],
                                             recv_sem=recv_sems.at[3 * t + k], device_id=(qx, qy, c),
                                             device_id_type=MESH).start()
        token[...] = jnp.zeros_like(token)

    res = pl.pallas_call(
        body, name="gather_start",
        in_specs=[HBM] * n + [ANY],
        out_specs=[SEM, SEM] + [HBM] * n + [pl.BlockSpec(memory_space=pltpu.VMEM)],
        out_shape=[pltpu.SemaphoreType.DMA((3 * n,)), pltpu.SemaphoreType.DMA((3 * n,))]
        + [pltpu.HBM(b.shape, b.dtype) for b in bufs] + [jax.ShapeDtypeStruct((8, LANES), F32)],
        input_output_aliases={t: 2 + t for t in range(n)},
        compiler_params=pltpu.CompilerParams(has_side_effects=EFFECT),
    )(*[_hbm(b) for b in bufs], after)
    return res[0], res[1], list(res[2:2 + n]), res[-1]


def gather_wait(bufs, send_sems, recv_sems, after, first=0):
    n = len(bufs)

    def body(*refs):
        g_refs = refs[:n]
        send_sems, recv_sems = refs[n], refs[n + 1]
        x, y, c = _place()
        me_p = 2 * x + y
        for t in range(n):
            for k, (qx, qy) in enumerate(_other_chips(x, y)):
                s = 3 * (first + t) + k
                cp = pltpu.make_async_remote_copy(src_ref=g_refs[t].at[me_p, c], dst_ref=g_refs[t].at[2 * qx + qy, c],
                                                  send_sem=send_sems.at[s], recv_sem=recv_sems.at[s],
                                                  device_id=(qx, qy, c), device_id_type=MESH)
                cp.wait_send()
                cp.wait_recv()

    return pl.pallas_call(
        body, name="gather_wait",
        in_specs=[HBM] * n + [SEM, SEM, ANY],
        out_specs=[HBM] * n,
        out_shape=[pltpu.HBM(b.shape, b.dtype) for b in bufs],
        input_output_aliases={t: t for t in range(n)},
        compiler_params=pltpu.CompilerParams(has_side_effects=EFFECT),
    )(*bufs, send_sems, recv_sems, after)


def gather_forward(bufs):
    n = len(bufs)

    def body(*refs):
        g_refs = refs[n:2 * n]
        send_sems, recv_sems = refs[2 * n:]
        x, y, c = _place()
        sibling = (x, y, 1 - c)
        chips = _other_chips(x, y)
        passed = []
        for t in range(n):
            for k, (qx, qy) in enumerate(chips):
                slab = g_refs[t].at[2 * qx + qy, c]
                fwd = pltpu.make_async_remote_copy(src_ref=slab, dst_ref=slab, send_sem=send_sems.at[3 * t + k],
                                                   recv_sem=recv_sems.at[3 * t + k], device_id=sibling,
                                                   device_id_type=MESH)
                fwd.start()
                passed.append(fwd)
        for t in range(n):
            for k, (qx, qy) in enumerate(chips):
                slab = g_refs[t].at[2 * qx + qy, 1 - c]
                pltpu.make_async_remote_copy(src_ref=slab, dst_ref=slab, send_sem=send_sems.at[3 * t + k],
                                             recv_sem=recv_sems.at[3 * t + k], device_id=sibling,
                                             device_id_type=MESH).wait_recv()
        for cp in passed:
            cp.wait_send()

    return pl.pallas_call(
        body, name="gather_forward",
        in_specs=[ANY] * n, out_specs=[ANY] * n,
        out_shape=[jax.ShapeDtypeStruct(b.shape, b.dtype) for b in bufs],
        input_output_aliases={t: t for t in range(n)},
        scratch_shapes=[pltpu.SemaphoreType.DMA((3 * n,)), pltpu.SemaphoreType.DMA((3 * n,))],
    )(*bufs)


def forward_start(bufs):
    n = len(bufs)

    def body(*refs):
        g_refs = refs[:n]
        send_sems, recv_sems = refs[n], refs[n + 1]
        token = refs[-1]
        x, y, c = _place()
        for t in range(n):
            for k, (qx, qy) in enumerate(_other_chips(x, y)):
                slab = g_refs[t].at[2 * qx + qy, c]
                pltpu.make_async_remote_copy(src_ref=slab, dst_ref=slab, send_sem=send_sems.at[3 * t + k],
                                             recv_sem=recv_sems.at[3 * t + k], device_id=(x, y, 1 - c),
                                             device_id_type=MESH).start()
        token[...] = jnp.zeros_like(token)

    res = pl.pallas_call(
        body, name="forward_start",
        in_specs=[HBM] * n,
        out_specs=[SEM, SEM] + [HBM] * n + [pl.BlockSpec(memory_space=pltpu.VMEM)],
        out_shape=[pltpu.SemaphoreType.DMA((3 * n,)), pltpu.SemaphoreType.DMA((3 * n,))]
        + [pltpu.HBM(b.shape, b.dtype) for b in bufs] + [jax.ShapeDtypeStruct((8, LANES), F32)],
        input_output_aliases={t: 2 + t for t in range(n)},
        compiler_params=pltpu.CompilerParams(has_side_effects=EFFECT),
    )(*[_hbm(b) for b in bufs])
    return res[0], res[1], list(res[2:2 + n]), res[-1]


def forward_wait(bufs, send_sems, recv_sems, after):
    n = len(bufs)

    def body(*refs):
        g_refs = refs[:n]
        send_sems, recv_sems = refs[n], refs[n + 1]
        x, y, c = _place()
        for t in range(n):
            for k, (qx, qy) in enumerate(_other_chips(x, y)):
                cp = pltpu.make_async_remote_copy(src_ref=g_refs[t].at[2 * qx + qy, c],
                                                  dst_ref=g_refs[t].at[2 * qx + qy, 1 - c],
                                                  send_sem=send_sems.at[3 * t + k], recv_sem=recv_sems.at[3 * t + k],
                                                  device_id=(x, y, 1 - c), device_id_type=MESH)
                cp.wait_send()
                cp.wait_recv()

    return pl.pallas_call(
        body, name="forward_wait",
        in_specs=[HBM] * n + [SEM, SEM, ANY],
        out_specs=[HBM] * n,
        out_shape=[pltpu.HBM(b.shape, b.dtype) for b in bufs],
        input_output_aliases={t: t for t in range(n)},
        compiler_params=pltpu.CompilerParams(has_side_effects=EFFECT),
    )(*bufs, send_sems, recv_sems, after)


def small_start(pack, me, after):
    buf = lax.dynamic_update_slice(lax.empty((8,) + pack.shape, pack.dtype), pack[None], (me, 0, 0))

    def body(b_ref, after_ref, send_sems, recv_sems, thru, token):
        x, y, c = _place()
        slot = b_ref.at[4 * x + 2 * y + c]
        for k in range(1, 8):
            peer = (x ^ ((k >> 2) & 1), y ^ ((k >> 1) & 1), c ^ (k & 1))
            pltpu.make_async_remote_copy(src_ref=slot, dst_ref=slot, send_sem=send_sems.at[k - 1],
                                         recv_sem=recv_sems.at[k - 1], device_id=peer, device_id_type=MESH).start()
        token[...] = jnp.zeros_like(token)

    return pl.pallas_call(
        body, name="small_start",
        in_specs=[HBM, ANY],
        out_specs=[SEM, SEM, HBM, pl.BlockSpec(memory_space=pltpu.VMEM)],
        out_shape=[pltpu.SemaphoreType.DMA((7,)), pltpu.SemaphoreType.DMA((7,)), pltpu.HBM(buf.shape, buf.dtype),
                   jax.ShapeDtypeStruct((8, LANES), F32)],
        input_output_aliases={0: 2},
        compiler_params=pltpu.CompilerParams(has_side_effects=EFFECT),
    )(_hbm(buf), after)


def small_wait(buf, send_sems, recv_sems, after):
    def body(b_ref, send_sems, recv_sems, after_ref, thru):
        x, y, c = _place()
        mine = b_ref.at[4 * x + 2 * y + c]
        for k in range(1, 8):
            peer = (x ^ ((k >> 2) & 1), y ^ ((k >> 1) & 1), c ^ (k & 1))
            cp = pltpu.make_async_remote_copy(src_ref=mine, dst_ref=b_ref.at[4 * peer[0] + 2 * peer[1] + peer[2]],
                                              send_sem=send_sems.at[k - 1], recv_sem=recv_sems.at[k - 1],
                                              device_id=peer, device_id_type=MESH)
            cp.wait_send()
            cp.wait_recv()

    return pl.pallas_call(
        body, name="small_wait",
        in_specs=[HBM, SEM, SEM, ANY], out_specs=HBM,
        out_shape=pltpu.HBM(buf.shape, buf.dtype),
        input_output_aliases={0: 0},
        compiler_params=pltpu.CompilerParams(has_side_effects=EFFECT),
    )(buf, send_sems, recv_sems, after)


def swap_start(grads, after):
    n = len(grads)

    def body(*refs):
        g_refs, l_refs = refs[:n], refs[n:2 * n]
        send_sems, recv_sems = refs[2 * n + 1], refs[2 * n + 2]
        token = refs[-1]
        x, y, c = _place()
        for t in range(n):
            for p in range(N_CHIPS):
                pltpu.make_async_remote_copy(src_ref=g_refs[t].at[p, 1 - c], dst_ref=l_refs[t].at[p],
                                             send_sem=send_sems.at[N_CHIPS * t + p],
                                             recv_sem=recv_sems.at[N_CHIPS * t + p],
                                             device_id=(x, y, 1 - c), device_id_type=MESH).start()
        token[...] = jnp.zeros_like(token)

    lands = [lax.empty((N_CHIPS,) + g.shape[2:], g.dtype) for g in grads]
    res = pl.pallas_call(
        body, name="swap_start",
        in_specs=[HBM] * (2 * n) + [ANY],
        out_specs=[SEM, SEM] + [HBM] * (2 * n) + [pl.BlockSpec(memory_space=pltpu.VMEM)],
        out_shape=[pltpu.SemaphoreType.DMA((N_CHIPS * n,)), pltpu.SemaphoreType.DMA((N_CHIPS * n,))]
        + [pltpu.HBM(a.shape, a.dtype) for a in grads + lands] + [jax.ShapeDtypeStruct((8, LANES), F32)],
        input_output_aliases={t: 2 + t for t in range(2 * n)},
        compiler_params=pltpu.CompilerParams(has_side_effects=EFFECT),
    )(*[_hbm(a) for a in grads + lands], after)
    return res[0], res[1], list(res[2:2 + n]), list(res[2 + n:2 + 2 * n]), res[-1]


def swap_wait(grads, lands, send_sems, recv_sems, after):
    n = len(grads)

    def body(*refs):
        g_refs, l_refs = refs[:n], refs[n:2 * n]
        send_sems, recv_sems = refs[2 * n], refs[2 * n + 1]
        x, y, c = _place()
        for t in range(n):
            for p in range(N_CHIPS):
                cp = pltpu.make_async_remote_copy(src_ref=g_refs[t].at[p, 1 - c], dst_ref=l_refs[t].at[p],
                                                  send_sem=send_sems.at[N_CHIPS * t + p],
                                                  recv_sem=recv_sems.at[N_CHIPS * t + p],
                                                  device_id=(x, y, 1 - c), device_id_type=MESH)
                cp.wait_send()
                cp.wait_recv()

    res = pl.pallas_call(
        body, name="swap_wait",
        in_specs=[HBM] * (2 * n) + [SEM, SEM, ANY],
        out_specs=[HBM] * (2 * n),
        out_shape=[pltpu.HBM(a.shape, a.dtype) for a in grads + lands],
        input_output_aliases={t: t for t in range(2 * n)},
        compiler_params=pltpu.CompilerParams(has_side_effects=EFFECT),
    )(*grads, *lands, send_sems, recv_sems, after)
    return list(res[:n]), list(res[n:])


def _row_tile(r):
    return max(t for t in range(16, 513, 16) if r % t == 0)


def add_own_half(g, other, c_arr):
    _, _, r, cols = g.shape
    tr = _row_tile(r)

    def body(c_ref, a_ref, b_ref, o_ref):
        o_ref[...] = (a_ref[...] + b_ref[...]).astype(BF16)

    return pl.pallas_call(
        body, name="add_own_half",
        grid_spec=pltpu.PrefetchScalarGridSpec(
            num_scalar_prefetch=1, grid=(N_CHIPS, r // tr),
            in_specs=[pl.BlockSpec((None, None, tr, cols), lambda p, i, c_ref: (p, c_ref[0], i, 0)),
                      pl.BlockSpec((None, tr, cols), lambda p, i, c_ref: (p, i, 0))],
            out_specs=pl.BlockSpec((None, tr, cols), lambda p, i, c_ref: (p, i, 0))),
        out_shape=jax.ShapeDtypeStruct((N_CHIPS, r, cols), BF16),
        compiler_params=_cparams("parallel", "parallel"),
    )(c_arr, g, other)


def scatter_start(partials):
    n = len(partials)

    def body(*refs):
        s_refs, l_refs = refs[:n], refs[n:2 * n]
        send_sems, recv_sems = refs[2 * n], refs[2 * n + 1]
        token = refs[-1]
        x, y, c = _place()
        for t in range(n):
            for k, (qx, qy) in enumerate(_other_chips(x, y)):
                pltpu.make_async_remote_copy(src_ref=s_refs[t].at[2 * qx + qy], dst_ref=l_refs[t].at[k],
                                             send_sem=send_sems.at[3 * t + k], recv_sem=recv_sems.at[3 * t + k],
                                             device_id=(qx, qy, c), device_id_type=MESH).start()
        token[...] = jnp.zeros_like(token)

    lands = [lax.empty((3,) + s.shape[1:], s.dtype) for s in partials]
    res = pl.pallas_call(
        body, name="scatter_start",
        in_specs=[HBM] * (2 * n),
        out_specs=[SEM, SEM] + [HBM] * (2 * n) + [pl.BlockSpec(memory_space=pltpu.VMEM)],
        out_shape=[pltpu.SemaphoreType.DMA((3 * n,)), pltpu.SemaphoreType.DMA((3 * n,))]
        + [pltpu.HBM(a.shape, a.dtype) for a in partials + lands] + [jax.ShapeDtypeStruct((8, LANES), F32)],
        input_output_aliases={t: 2 + t for t in range(2 * n)},
        compiler_params=pltpu.CompilerParams(has_side_effects=EFFECT),
    )(*[_hbm(a) for a in partials + lands])
    return res[0], res[1], list(res[2:2 + n]), list(res[2 + n:2 + 2 * n]), res[-1]


def scatter_wait(partials, lands, send_sems, recv_sems, after):
    n = len(partials)

    def body(*refs):
        s_refs, l_refs = refs[:n], refs[n:2 * n]
        send_sems, recv_sems = refs[2 * n], refs[2 * n + 1]
        x, y, c = _place()
        for t in range(n):
            for k, (qx, qy) in enumerate(_other_chips(x, y)):
                cp = pltpu.make_async_remote_copy(src_ref=s_refs[t].at[2 * qx + qy], dst_ref=l_refs[t].at[k],
                                                  send_sem=send_sems.at[3 * t + k], recv_sem=recv_sems.at[3 * t + k],
                                                  device_id=(qx, qy, c), device_id_type=MESH)
                cp.wait_send()
                cp.wait_recv()

    res = pl.pallas_call(
        body, name="scatter_wait",
        in_specs=[HBM] * (2 * n) + [SEM, SEM, ANY],
        out_specs=[HBM] * (2 * n),
        out_shape=[pltpu.HBM(a.shape, a.dtype) for a in partials + lands],
        input_output_aliases={t: t for t in range(2 * n)},
        compiler_params=pltpu.CompilerParams(has_side_effects=EFFECT),
    )(*partials, *lands, send_sems, recv_sems, after)
    return list(res[:n]), list(res[n:])


def sum_chips(own, parts, where):
    _, r, cols = own.shape
    tr = _row_tile(r)

    def body(w_ref, a_ref, p_ref, o_ref):
        acc = a_ref[...].astype(F32)
        for k in range(3):
            acc = acc + p_ref[k].astype(F32)
        o_ref[...] = acc

    return pl.pallas_call(
        body, name="sum_chips",
        grid_spec=pltpu.PrefetchScalarGridSpec(
            num_scalar_prefetch=1, grid=(r // tr,),
            in_specs=[pl.BlockSpec((None, tr, cols), lambda i, w_ref: (w_ref[0], i, 0)),
                      pl.BlockSpec((3, tr, cols), lambda i, w_ref: (0, i, 0))],
            out_specs=pl.BlockSpec((None, tr, cols), lambda i, w_ref: (w_ref[1], i, 0))),
        out_shape=jax.ShapeDtypeStruct((DEPTH, r, cols), F32),
        compiler_params=_cparams("parallel"),
    )(where, own, parts)


def sibling_share_layer(bufs):
    n = len(bufs)

    def body(*refs):
        o_refs = refs[n:2 * n]
        send_sems, recv_sems = refs[2 * n:]
        x, y, c = _place()
        cps = []
        for t in range(n):
            cp = pltpu.make_async_remote_copy(src_ref=o_refs[t].at[c], dst_ref=o_refs[t].at[c], send_sem=send_sems.at[t],
                                              recv_sem=recv_sems.at[t], device_id=(x, y, 1 - c), device_id_type=MESH)
            cp.start()
            cps.append(cp)
        for t in range(n):
            slot = o_refs[t].at[1 - c]
            pltpu.make_async_remote_copy(src_ref=slot, dst_ref=slot, send_sem=send_sems.at[t], recv_sem=recv_sems.at[t],
                                         device_id=(x, y, 1 - c), device_id_type=MESH).wait_recv()
        for cp in cps:
            cp.wait_send()

    return pl.pallas_call(
        body, name="sibling_share_layer",
        in_specs=[ANY] * n, out_specs=[ANY] * n,
        out_shape=[jax.ShapeDtypeStruct(b.shape, b.dtype) for b in bufs],
        input_output_aliases={t: t for t in range(n)},
        scratch_shapes=[pltpu.SemaphoreType.DMA((n,)), pltpu.SemaphoreType.DMA((n,))],
    )(*bufs)


SP_META = 2 * (N_META * D_MODEL // LANES)
SP_NORM = DEPTH * D_MODEL // LANES
SP_RB = DEPTH * N_BUCKETS
SP_SINK = DEPTH * ATT_HEADS
SP_CONV = DEPTH * 3 * BRANCH_WIDTH // LANES
SP_LOSS = 8
SIDE_ROWS = 48
SP_ROWS = SP_META + 2 * SP_NORM + SP_RB + SP_SINK + SP_CONV + SP_LOSS


def sum_small(slots):
    half = SP_META // 2
    rb0 = SP_META + 2 * SP_NORM
    rest_rows = SP_ROWS - SP_META

    def body(s_ref, meta_ref, rest_ref):
        acc = s_ref[0]
        for d in range(1, 8):
            acc = acc + s_ref[d]
        meta_ref[...] = acc[0:half] + acc[half:SP_META]
        rest_ref[...] = acc[SP_META:]
        rest_ref[rb0 - SP_META:rb0 - SP_META + N_BUCKETS, :] = (
            acc[rb0:rb0 + N_BUCKETS] + acc[rb0 + N_BUCKETS:rb0 + 2 * N_BUCKETS])

    vm = pl.BlockSpec(memory_space=pltpu.VMEM)
    return pl.pallas_call(
        body, name="sum_small",
        in_specs=[vm], out_specs=[vm, vm],
        out_shape=[jax.ShapeDtypeStruct((half, LANES), F32), jax.ShapeDtypeStruct((rest_rows, LANES), F32)],
    )(slots)


def local_step(x, loss_target, meta_full, rel_bias, norm_pre, conv_w_full, attn_sinks, norm_post, weights_of, mid_fwd,
               grads_done, bwd_done):
    nb, seq, _ = x.shape
    nc = seq // BLOCK + 1
    lp = nc * BLOCK
    rows = nb * lp
    pad = jnp.zeros((nb, PAD_FRONT, D_MODEL), F32)
    meta = jnp.broadcast_to(meta_full[None], (nb, N_META, D_MODEL))
    h0 = jnp.concatenate([pad, meta, x], axis=1).reshape(rows, D_MODEL)
    target = jnp.pad(loss_target, ((0, 0), (BLOCK, 0), (0, 0))).reshape(rows, D_MODEL)
    cosf, sinf = _rot_tables(lp)
    bkt = jnp.asarray(_bucket_table())

    g_pre = norm_pre.reshape(DEPTH, 1, D_MODEL)
    g_post = norm_post.reshape(DEPTH, 1, D_MODEL)
    order = lambda token: bkt if token is None else token

    acts = []
    h = h0
    for l in range(DEPTH):
        w_in, token = weights_of(l, h)
        hb, p_abc = norm_matmul(h, g_pre, l, w_in, 0, N_ABC_TILES, order(token))
        p_m = matmul_cols(hb, w_in, N_ABC_TILES, N_M_TILES)
        br, states = mixers_fwd(p_abc, cosf, sinf, bkt, rel_bias, attn_sinks, conv_w_full, l, nb, nc)
        (w_br, w_out), token = mid_fwd(l, br)
        acts.append((h, hb, p_abc, p_m, br, states, w_in, w_br, w_out))
        if l < DEPTH - 1:
            h = merge_fwd(h, br, p_m, w_br, w_out, g_post, l, order(token))
        else:
            assert token is None
            loss_part, d_h = merge_fwd_loss(h, br, p_m, w_br, w_out, g_post, l, target, lp)

    small = [None] * DEPTH
    token = None
    for l in reversed(range(DEPTH)):
        h_in, hb, p_abc, p_m, br, states, w_in, w_br, w_out = acts[l]
        d_br, d_m, d_gpost, g_wbr, g_wout = merge_bwd(d_h, br, p_m, w_br, w_out, g_post, l, order(token))
        d_abc, d_rb, d_sk, d_cw = mixers_bwd(p_abc, d_br, states, cosf, sinf, bkt, rel_bias,
                                             attn_sinks, conv_w_full, l, nb, nc)
        g_win = proj_wgrad(hb, d_abc, d_m)
        token = grads_done(l, [g_win, g_wbr, g_wout])
        d_h, d_gpre = proj_dgrad(d_abc, d_m, w_in, h_in, g_pre, l, d_h, order(token))
        token = bwd_done(l, d_h)
        small[l] = (d_gpre[0], d_gpost[0], d_rb, d_sk, d_cw[0:3])

    d_h3 = d_h.reshape(nb, lp, D_MODEL)
    d_x = d_h3[:, BLOCK:]
    d_meta = d_h3[:, PAD_FRONT:BLOCK]
    sp = jnp.concatenate([
        d_meta.reshape(-1, LANES),
        jnp.stack([small[l][0] for l in range(DEPTH)]).reshape(-1, LANES),
        jnp.stack([small[l][1] for l in range(DEPTH)]).reshape(-1, LANES),
        jnp.concatenate([small[l][2] for l in range(DEPTH)], axis=0),
        jnp.concatenate([small[l][3] for l in range(DEPTH)], axis=0),
        jnp.stack([small[l][4] for l in range(DEPTH)]).reshape(-1, LANES),
        loss_part], axis=0)
    return d_x, sp


def kernel(x, meta_tokens, rel_bias, norm_pre, w_in, conv_w, attn_sinks, w_branch, w_out, norm_post, loss_target, m_meta_tokens, m_rel_bias, m_norm_pre, m_w_in, m_conv_w, m_attn_sinks, m_w_branch, m_w_out, m_norm_post, v_meta_tokens, v_rel_bias, v_norm_pre, v_w_in, v_conv_w, v_attn_sinks, v_w_branch, v_w_out, v_norm_post):
    assert x.shape[0] == 2 and SP_META == 2 * N_META * D_MODEL // LANES
    px, py, pc = _place()
    chip = 2 * px + py

    c_arr = jnp.reshape(pc, (1,)).astype(jnp.int32)
    where = jnp.stack([chip, pc]).astype(jnp.int32)
    tr_ = lambda a: jnp.swapaxes(a, 1, 2)
    w3 = [tr_(w_in), w_branch.reshape(DEPTH, N_BRANCH * BRANCH_WIDTH, SHARD_D), w_out]
    halves = lambda a: a.reshape(2, a.shape[0] // 2, a.shape[1])

    def as_weights(bufs):
        a_in, a_br, a_out = bufs
        return (a_in.reshape(PROJ_WIDTH, D_MODEL), a_br.reshape(N_CHIPS, N_BRANCH, BRANCH_WIDTH, SHARD_D),
                a_out.reshape(D_MODEL, D_MODEL))

    n_meta_rows = N_META * SHARD_D // LANES
    side = jnp.concatenate([meta_tokens.reshape(-1), conv_w.reshape(-1)]).reshape(-1, LANES)
    side = jnp.concatenate([side, jnp.zeros((SIDE_ROWS - side.shape[0], LANES), F32)], axis=0)
    slots = [[_own_slot(halves(w[l].astype(BF16)), chip) for w in w3] for l in range(DEPTH)]
    send0, recv0, flying0, _ = gather_start([_own_slot(halves(side), chip)] + slots[0], where)
    side_chips = gather_forward(gather_wait(flying0[:1], send0, recv0, where))[0].reshape(N_CHIPS, SIDE_ROWS, LANES)
    meta_full = jnp.moveaxis(side_chips[:, :n_meta_rows].reshape(N_CHIPS, N_META, SHARD_D), 0, 1).reshape(N_META, D_MODEL)
    conv_full = jnp.moveaxis(side_chips[:, n_meta_rows:n_meta_rows + 6].reshape(N_CHIPS, DEPTH, 3, LANES), 0, 2).reshape(DEPTH, 3, BRANCH_WIDTH)
    inbound = {}

    def weights_of(l, h):
        if l == 0:
            inbound[0] = gather_forward(gather_wait(flying0[1:2], send0, recv0, h, first=1))
            inbound[1] = gather_start(slots[1], inbound[0][0])
            return inbound[0][0].reshape(PROJ_WIDTH, D_MODEL), inbound[1][3]
        send, recv, thru = inbound[1]
        inbound[1] = as_weights(forward_wait(thru, send, recv, h))
        return inbound[1][0], None

    def mid_fwd(l, br):
        if l == 0:
            rest = gather_forward(gather_wait(flying0[2:], send0, recv0, br, first=2))
            send, recv, flying1, _ = inbound[1]
            send, recv, thru, started = forward_start(gather_wait(flying1, send, recv, rest[0]))
            inbound[1] = (send, recv, thru)
            return as_weights(inbound[0] + rest)[1:], started
        return inbound[1][1:], None

    reduced = [None] * DEPTH
    flying = {}

    def finish_reduce(l, after):
        partials, parts = scatter_wait(*flying[l], after)
        reduced[l] = sibling_share_layer([sum_chips(a, p, where) for a, p in zip(partials, parts)])

    def start_scatter(l, full, others):
        send, recv, thru, lands, started = scatter_start([add_own_half(g, o, c_arr) for g, o in zip(full, others)])
        flying[l] = (thru, lands, send, recv)
        return started

    m3 = [tr_(m_w_in), m_w_branch.reshape(w3[1].shape), m_w_out]
    v3 = [tr_(v_w_in), v_w_branch.reshape(w3[1].shape), v_w_out]
    big = [None] * 3

    def adamw_of(l, after):
        for t in range(3):
            big[t] = adamw_layer(w3[t], reduced[l][t].reshape(w3[t].shape[1:]), m3[t], v3[t], l, big[t], after)

    def grads_done(l, grads):
        full = [g.reshape(N_CHIPS, 2, g.size // (2 * N_CHIPS * g.shape[-1]), g.shape[-1]) for g in grads]
        if l == 0:
            finish_reduce(1, grads[0])
        send, recv, thru, lands, started = swap_start(full, where if l == 1 else reduced[1][0])
        if l == 1:
            flying["swap"] = (thru, lands, send, recv)
            return started
        adamw_of(1, started)
        return start_scatter(0, *swap_wait(thru, lands, send, recv, big[0][1]))

    def bwd_done(l, d_h):
        if l == 1:
            return start_scatter(1, *swap_wait(*flying["swap"], d_h))
        return None

    d_x, sp = local_step(x, loss_target, meta_full, rel_bias, norm_pre, conv_full, attn_sinks, norm_post,
                         weights_of, mid_fwd, grads_done, bwd_done)
    finish_reduce(0, sp)

    s_send, s_recv, s_buf, s_started = small_start(sp, 4 * px + 2 * py + pc, reduced[0][0])

    adamw_of(0, s_started)
    g_in, *u_in = [tr_(a) for a in big[0]]
    g_br, *u_br = [a.reshape(w_branch.shape) for a in big[1]]
    g_out, *u_out = big[2]

    meta_rows, rest = sum_small(small_wait(s_buf, s_send, s_recv, big[0][1]))
    o = 0
    g_meta_full = meta_rows.reshape(N_META, D_MODEL)
    g_norm_pre = rest[o:o + SP_NORM].reshape(DEPTH, D_MODEL); o += SP_NORM
    g_norm_post = rest[o:o + SP_NORM].reshape(DEPTH, D_MODEL); o += SP_NORM
    g_rel_bias = rest[o:o + N_BUCKETS, :ATT_HEADS]; o += SP_RB
    g_sinks = rest[o:o + SP_SINK, 0].reshape(DEPTH, ATT_HEADS); o += SP_SINK
    g_conv_full = rest[o:o + SP_CONV].reshape(DEPTH, 3, BRANCH_WIDTH); o += SP_CONV
    loss = rest[o, 0]
    g_meta = lax.dynamic_slice_in_dim(g_meta_full, chip * SHARD_D, SHARD_D, axis=1)
    g_conv = lax.dynamic_slice_in_dim(g_conv_full, chip * LANES, LANES, axis=2)

    to2 = lambda a: a.reshape(-1, a.shape[-1])
    smalls = [(meta_tokens, g_meta, m_meta_tokens, v_meta_tokens),
              (rel_bias, g_rel_bias, m_rel_bias, v_rel_bias),
              (norm_pre, g_norm_pre, m_norm_pre, v_norm_pre),
              (to2(conv_w), to2(g_conv), to2(m_conv_w), to2(v_conv_w)),
              (attn_sinks, g_sinks, m_attn_sinks, v_attn_sinks),
              (norm_post, g_norm_post, m_norm_post, v_norm_post)]
    u_meta, u_rb, u_npre, u_conv, u_sink, u_npost = adamw_small(smalls)
    u_conv = tuple(a.reshape(conv_w.shape) for a in u_conv)

    grads = [g_meta, g_rel_bias, g_norm_pre, g_in, g_conv, g_sinks, g_br, g_out, g_norm_post]
    upd = [u_meta, u_rb, u_npre, u_in, u_conv, u_sink, u_br, u_out, u_npost]
    return (loss, d_x, *grads, *[u[0] for u in upd], *[u[1] for u in upd], *[u[2] for u in upd])
```

```python
import math

import numpy as np
import jax
import jax.numpy as jnp
from jax import lax
from jax.experimental import pallas as pl
from jax.experimental.pallas import tpu as pltpu

F32 = jnp.float32
BF16 = jnp.bfloat16
MESH = pl.DeviceIdType.MESH

D_MODEL = 1024
DEPTH = 2
N_META = 16
BLOCK = 128
PAD_FRONT = BLOCK - N_META
ATT_HEADS = 8
ATT_HEAD_DIM = 64
N_BUCKETS = 32
MAX_EXACT = 16
MAX_DISTANCE = 128
RET_HEADS = 4
ROT_BASE = 10000.0
N_BRANCH = 3
BRANCH_WIDTH = 512
PROJ_WIDTH = 8448
ABC_WIDTH = 5376
MERGE_WIDTH = N_BRANCH * D_MODEL
RMS_EPS = 1e-6
GN_EPS = 1e-6
NEG_INF = -1e30
ATT_SCALE = ATT_HEAD_DIM ** -0.5
RET_SCALE = BLOCK ** -0.5
LOG_GAMMA = tuple(math.log1p(-(2.0 ** (-5.0 - h))) for h in range(RET_HEADS))

C_AQ, C_AK, C_AV, C_AG = 0, 512, 640, 768
C_RQ, C_RK, C_RV, C_RG = 1280, 1792, 2304, 2816
C_CB, C_CC, C_CX, C_CG = 3328, 3840, 4352, 4864

ADAM_LR = 0.001
ADAM_B1 = 0.9
ADAM_B2 = 0.999
ADAM_EPS = 1e-08
ADAM_WD = 0.01
ADAM_STEP = 10

N_CHIPS = 4
SHARD_D = D_MODEL // N_CHIPS
LANES = 128

VMEM_LIMIT = 56 * 1024 * 1024
COL_TILE = 768
ROW_TILE = 1088
PROJ_ROW_TILE = 2176


def _cparams(*sem):
    return pltpu.CompilerParams(dimension_semantics=sem, vmem_limit_bytes=VMEM_LIMIT)


def _nt(a, b):
    return lax.dot_general(a, b, (((1,), (1,)), ((), ())), preferred_element_type=F32)


def _tn(a, b):
    return lax.dot_general(a, b, (((0,), (0,)), ((), ())), preferred_element_type=F32)


def _nn(a, b):
    return jnp.dot(a, b, preferred_element_type=F32)


def _sigmoid(x):
    return 0.5 * jnp.tanh(0.5 * x) + 0.5


def _silu(x):
    return x * _sigmoid(x)


def _dsilu(x):
    s = _sigmoid(x)
    return s * (1.0 + x * (1.0 - s))


def _bucket_table():
    r = np.arange(BLOCK)[:, None]
    c = np.arange(2 * BLOCK)[None, :]
    n = np.maximum(BLOCK + r - c, 0)
    nf = np.maximum(n, 1).astype(np.float32)
    large = MAX_EXACT + (np.log(nf / MAX_EXACT) / math.log(MAX_DISTANCE / MAX_EXACT)
                         * (N_BUCKETS - MAX_EXACT)).astype(np.int32)
    large = np.minimum(large, N_BUCKETS - 1)
    return np.where(n < MAX_EXACT, n, large).astype(np.int32)


def _rot_tables(lp):
    half = BLOCK // 2
    pos = (jnp.arange(lp) - PAD_FRONT).astype(F32)
    theta = 1.0 / (ROT_BASE ** jnp.linspace(0.0, 1.0, half, dtype=F32))
    ang = pos[:, None] * theta[None, :]
    cos, sin = jnp.cos(ang), jnp.sin(ang)
    return jnp.concatenate([cos, cos], axis=1), jnp.concatenate([-sin, sin], axis=1)


def norm_matmul(x2d, g, layer, w, col0_blocks, n_col_blocks, after):
    t = x2d.shape[0]
    tm = PROJ_ROW_TILE if t % PROJ_ROW_TILE == 0 else BLOCK

    def body(x_ref, g_ref, w_ref, after_ref, hb_ref, o_ref):
        @pl.when(pl.program_id(1) == 0)
        def _():
            x = x_ref[...]
            r = lax.rsqrt(jnp.mean(x * x, axis=-1, keepdims=True) + RMS_EPS)
            hb_ref[...] = (x * r * g_ref[...]).astype(BF16)

        o_ref[...] = _nt(hb_ref[...], w_ref[...]).astype(BF16)

    return pl.pallas_call(
        body, name="norm_matmul",
        grid=(t // tm, n_col_blocks),
        in_specs=[pl.BlockSpec((tm, D_MODEL), lambda i, j: (i, 0)),
                  pl.BlockSpec((None, 1, D_MODEL), lambda i, j: (layer, 0, 0)),
                  pl.BlockSpec((COL_TILE, D_MODEL), lambda i, j: (j + col0_blocks, 0)),
                  ANY],
        out_specs=[pl.BlockSpec((tm, D_MODEL), lambda i, j: (i, 0)),
                   pl.BlockSpec((tm, COL_TILE), lambda i, j: (i, j))],
        out_shape=[jax.ShapeDtypeStruct((t, D_MODEL), BF16),
                   jax.ShapeDtypeStruct((t, n_col_blocks * COL_TILE), BF16)],
        compiler_params=_cparams("parallel", "arbitrary"),
    )(x2d, g, w, after)


def matmul_cols(a, w, col0_blocks, n_col_blocks):
    t, k = a.shape
    tm = PROJ_ROW_TILE if t % PROJ_ROW_TILE == 0 else BLOCK

    def body(a_ref, w_ref, o_ref):
        o_ref[...] = _nt(a_ref[...], w_ref[...]).astype(BF16)

    return pl.pallas_call(
        body, name="matmul_cols",
        grid=(t // tm, n_col_blocks),
        in_specs=[pl.BlockSpec((tm, k), lambda i, j: (i, 0)),
                  pl.BlockSpec((COL_TILE, k), lambda i, j: (j + col0_blocks, 0))],
        out_specs=pl.BlockSpec((tm, COL_TILE), lambda i, j: (i, j)),
        out_shape=jax.ShapeDtypeStruct((t, n_col_blocks * COL_TILE), BF16),
        compiler_params=_cparams("parallel", "arbitrary"),
    )(a, w)


class _Widened:
    def __init__(self, ref):
        self.ref = ref

    def __getitem__(self, idx):
        return self.ref[idx].astype(F32)


def _build_bias(bkt_ref, rb_ref, bias_s):
    bkt = bkt_ref[...]
    for h in range(ATT_HEADS):
        acc = jnp.zeros((BLOCK, 2 * BLOCK), F32)
        for b in range(N_BUCKETS):
            acc = jnp.where(bkt == b, rb_ref[b, h], acc)
        bias_s[h] = acc


def _band_mask(n):
    r = lax.broadcasted_iota(jnp.int32, (BLOCK, 2 * BLOCK), 0)
    c = lax.broadcasted_iota(jnp.int32, (BLOCK, 2 * BLOCK), 1)
    key_pos = (n - 1) * BLOCK + c
    return (c > r) & (c <= r + BLOCK) & (key_pos >= PAD_FRONT)


def _split_heads(kv, kh):
    lane = lax.broadcasted_iota(jnp.int32, kv.shape, 1)
    if kh == 0:
        lo = jnp.where(lane < ATT_HEAD_DIM, kv, 0.0)
        hi = pltpu.roll(lo, ATT_HEAD_DIM, 1)
    else:
        hi = jnp.where(lane >= ATT_HEAD_DIM, kv, 0.0)
        lo = pltpu.roll(hi, ATT_HEAD_DIM, 1)
    return lo, hi


def _merge_heads(acc_lo, acc_hi, kh):
    lane = lax.broadcasted_iota(jnp.int32, acc_lo.shape, 1)
    if kh == 0:
        return jnp.where(lane < ATT_HEAD_DIM, acc_lo + pltpu.roll(acc_hi, ATT_HEAD_DIM, 1), 0.0)
    return jnp.where(lane >= ATT_HEAD_DIM, acc_hi + pltpu.roll(acc_lo, ATT_HEAD_DIM, 1), 0.0)


def _softmax_of(qk, bias_h, mask, sink_h):
    s = qk + bias_h
    s = jnp.where(mask, s, NEG_INF)
    m = jnp.maximum(jnp.max(s, axis=-1, keepdims=True), sink_h)
    p = jnp.exp(s - m)
    es = jnp.exp(sink_h - m)
    inv = 1.0 / (jnp.sum(p, axis=-1, keepdims=True) + es)
    return p * inv, es * inv


def _rot(t, cosf, sinf):
    return t * cosf + pltpu.roll(t, BLOCK // 2, 1) * sinf


def _rot_t(d, cosf, sinf):
    return d * cosf + pltpu.roll(d * sinf, BLOCK // 2, 1)


def _decay_tables(h):
    lg = LOG_GAMMA[h]
    i = lax.broadcasted_iota(jnp.int32, (BLOCK, BLOCK), 0)
    j = lax.broadcasted_iota(jnp.int32, (BLOCK, BLOCK), 1)
    diff = (i - j).astype(F32)
    dm = jnp.where(diff >= 0, jnp.exp(diff * lg), 0.0)
    row = lax.broadcasted_iota(jnp.int32, (BLOCK, 1), 0).astype(F32)
    zeta = jnp.exp((BLOCK - 1 - row) * lg)
    xi = jnp.exp((row + 1.0) * lg)
    return dm, zeta, xi, math.exp(BLOCK * lg)


def _valid_col(n):
    row = lax.broadcasted_iota(jnp.int32, (BLOCK, 1), 0)
    return ((n * BLOCK + row) >= PAD_FRONT).astype(F32)


def _shift_down(cur, prev, k):
    row = lax.broadcasted_iota(jnp.int32, cur.shape, 0)
    return jnp.where(row >= k, pltpu.roll(cur, k, 0), pltpu.roll(prev, k, 0))


def _shift_up(cur, nxt, k):
    row = lax.broadcasted_iota(jnp.int32, cur.shape, 0)
    return jnp.where(row < BLOCK - k, pltpu.roll(cur, BLOCK - k, 0), pltpu.roll(nxt, BLOCK - k, 0))


def mixers_fwd(proj, cosf, sinf, bkt, rel_bias, sinks, conv_w, layer, nb, nc):
    def body(p_ref, cos_ref, sin_ref, bkt_ref, rb_ref, sk_ref, cw_ref, br_ref, st_ref,
             bias_s, kv_s, state_s, u_s):
        p_ref = _Widened(p_ref)
        n = pl.program_id(0)

        @pl.when(n == 0)
        def _():
            _build_bias(bkt_ref, rb_ref, bias_s)
            kv_s[:, 0:BLOCK, :] = jnp.zeros((nb, BLOCK, 2 * BLOCK), F32)
            state_s[...] = jnp.zeros_like(state_s)
            u_s[...] = jnp.zeros_like(u_s)

        valid = _valid_col(n)
        mask = _band_mask(n)
        ex = range(nb)

        for b in ex:
            kv_s[b, BLOCK:2 * BLOCK, :] = p_ref[b, :, C_AK:C_AK + 2 * BLOCK]
        for kh in range(2):
            ks = [[t.astype(BF16) for t in _split_heads(kv_s[b, :, 0:BLOCK], kh)] for b in ex]
            vs = [[t.astype(BF16) for t in _split_heads(kv_s[b, :, BLOCK:2 * BLOCK], kh)] for b in ex]
            pairs = [(b, 2 * kh + jj) for jj in range(2) for b in ex]
            subs = [(b, j, x) for (b, j) in pairs for x in range(2)]
            qb_ = {(b, j): (p_ref[b, :, C_AQ + BLOCK * j:C_AQ + BLOCK * (j + 1)] * ATT_SCALE).astype(BF16)
                   for (b, j) in pairs}
            qk_ = {(b, j, x): _nt(qb_[(b, j)], ks[b][x]) for (b, j, x) in subs}
            pb_ = {}
            for u in subs:
                h = 2 * u[1] + u[2]
                pb_[u] = _softmax_of(qk_[u], bias_s[h], mask, sk_ref[layer, h])[0].astype(BF16)
            o_ = {u: _nn(pb_[u], vs[u[0]][u[2]]) for u in subs}
            for (b, j) in pairs:
                gate = p_ref[b, :, C_AG + BLOCK * j:C_AG + BLOCK * (j + 1)]
                br_ref[b, :, BLOCK * j:BLOCK * (j + 1)] = ((o_[(b, j, 0)] + o_[(b, j, 1)]) * _silu(gate)).astype(BF16)
        for b in ex:
            kv_s[b, 0:BLOCK, :] = kv_s[b, BLOCK:2 * BLOCK, :]

        cosv = cos_ref[...]
        sinv = sin_ref[...]
        tabs = [_decay_tables(h) for h in range(RET_HEADS)]
        units = [(b, h) for h in range(RET_HEADS) for b in ex]
        sl = lambda c0, h: slice(c0 + BLOCK * h, c0 + BLOCK * (h + 1))
        q_, k_, v_, sp_ = {}, {}, {}, {}
        for u in units:
            b, h = u
            q_[u] = _rot(p_ref[b, :, sl(C_RQ, h)], cosv, sinv).astype(BF16)
            k_[u] = (_rot(p_ref[b, :, sl(C_RK, h)], cosv, sinv) * RET_SCALE * valid).astype(BF16)
            v_[u] = p_ref[b, :, sl(C_RV, h)]
            sp_[u] = state_s[b, h]
            st_ref[b, 0, h] = sp_[u]
        qk_ = {u: _nt(q_[u], k_[u]) for u in units}
        qs_ = {u: _nn(q_[u], sp_[u].astype(BF16)) for u in units}
        kv_ = {u: _tn(k_[u], (v_[u] * tabs[u[1]][1]).astype(BF16)) for u in units}
        a_ = {u: (qk_[u] * tabs[u[1]][0]).astype(BF16) for u in units}
        av_ = {u: _nn(a_[u], v_[u].astype(BF16)) for u in units}
        for u in units:
            b, h = u
            o = av_[u] + tabs[h][2] * qs_[u]
            mu = jnp.mean(o, axis=-1, keepdims=True)
            var = jnp.mean(jnp.square(o - mu), axis=-1, keepdims=True)
            oh = (o - mu) * lax.rsqrt(var + GN_EPS)
            gate = p_ref[b, :, sl(C_RG, h)]
            br_ref[b, :, BRANCH_WIDTH + BLOCK * h:BRANCH_WIDTH + BLOCK * (h + 1)] = (oh * _silu(gate)).astype(BF16)
            state_s[b, h] = tabs[h][3] * sp_[u] + kv_[u]

        for b in ex:
            u = p_ref[b, :, C_CC:C_CC + BRANCH_WIDTH] * p_ref[b, :, C_CX:C_CX + BRANCH_WIDTH] * valid
            u_prev = u_s[b]
            y = (cw_ref[0:1, :] * _shift_down(u, u_prev, 2) + cw_ref[1:2, :] * _shift_down(u, u_prev, 1)
                 + cw_ref[2:3, :] * u)
            yc = p_ref[b, :, C_CB:C_CB + BRANCH_WIDTH] * y * _silu(p_ref[b, :, C_CG:C_CG + BRANCH_WIDTH])
            br_ref[b, :, 2 * BRANCH_WIDTH:3 * BRANCH_WIDTH] = yc.astype(BF16)
            u_s[b] = u

    lp = nc * BLOCK
    smem = pl.BlockSpec(memory_space=pltpu.SMEM)
    br, states = pl.pallas_call(
        body, name="mixers_fwd",
        grid=(nc,),
        in_specs=[pl.BlockSpec((nb, BLOCK, ABC_WIDTH), lambda n: (0, n, 0)),
                  pl.BlockSpec((BLOCK, BLOCK), lambda n: (n, 0)),
                  pl.BlockSpec((BLOCK, BLOCK), lambda n: (n, 0)),
                  pl.BlockSpec((BLOCK, 2 * BLOCK), lambda n: (0, 0)),
                  smem, smem,
                  pl.BlockSpec((None, 3, BRANCH_WIDTH), lambda n: (layer, 0, 0))],
        out_specs=[pl.BlockSpec((nb, BLOCK, N_BRANCH * BRANCH_WIDTH), lambda n: (0, n, 0)),
                   pl.BlockSpec((nb, 1, RET_HEADS, BLOCK, BLOCK), lambda n: (0, n, 0, 0, 0))],
        out_shape=[jax.ShapeDtypeStruct((nb, lp, N_BRANCH * BRANCH_WIDTH), BF16),
                   jax.ShapeDtypeStruct((nb, nc, RET_HEADS, BLOCK, BLOCK), F32)],
        scratch_shapes=[pltpu.VMEM((ATT_HEADS, BLOCK, 2 * BLOCK), F32),
                        pltpu.VMEM((nb, 2 * BLOCK, 2 * BLOCK), F32),
                        pltpu.VMEM((nb, RET_HEADS, BLOCK, BLOCK), F32),
                        pltpu.VMEM((nb, BLOCK, BRANCH_WIDTH), F32)],
        compiler_params=_cparams("arbitrary"),
    )(proj.reshape(nb, lp, ABC_WIDTH), cosf, sinf, bkt, rel_bias, sinks, conv_w)
    return br.reshape(nb * lp, N_BRANCH * BRANCH_WIDTH), states


def mixers_bwd(proj, d_br, states, cosf, sinf, bkt, rel_bias, sinks, conv_w, layer, nb, nc):
    def body(p_ref, kvp_ref, cp_ref, dbr_ref, st_ref, cos_ref, sin_ref, bkt_ref, rb_ref, sk_ref, cw_ref,
             dp_ref, drb_ref, dsk_ref, dcw_ref,
             bias_s, dbias_s, dkv_s, g_s, dy_s):
        p_ref, kvp_ref, cp_ref, dbr_ref = [_Widened(r) for r in (p_ref, kvp_ref, cp_ref, dbr_ref)]
        step = pl.program_id(0)
        n = nc - 1 - step
        ex = range(nb)

        @pl.when(step == 0)
        def _():
            _build_bias(bkt_ref, rb_ref, bias_s)
            dbias_s[...] = jnp.zeros_like(dbias_s)
            dsk_ref[...] = jnp.zeros_like(dsk_ref)
            dcw_ref[...] = jnp.zeros_like(dcw_ref)
            drb_ref[...] = jnp.zeros_like(drb_ref)
            dkv_s[...] = jnp.zeros_like(dkv_s)
            g_s[...] = jnp.zeros_like(g_s)
            dy_s[...] = jnp.zeros_like(dy_s)

        valid = _valid_col(n)
        mask = _band_mask(n)
        has_prev = (n > 0).astype(F32)

        k_all, v_all = [], []
        for b in ex:
            kv_prev = kvp_ref[b] * has_prev
            kv_cur = p_ref[b, :, C_AK:C_AK + 2 * BLOCK]
            k_all.append(jnp.concatenate([kv_prev[:, 0:BLOCK], kv_cur[:, 0:BLOCK]], axis=0))
            v_all.append(jnp.concatenate([kv_prev[:, BLOCK:], kv_cur[:, BLOCK:]], axis=0))
        zero2 = jnp.zeros((2 * BLOCK, BLOCK), F32)
        dk_tot = [zero2 for _ in ex]
        dv_tot = [zero2 for _ in ex]
        for kh in range(2):
            ks = [[t.astype(BF16) for t in _split_heads(k_all[b], kh)] for b in ex]
            vs = [[t.astype(BF16) for t in _split_heads(v_all[b], kh)] for b in ex]
            pairs = [(b, 2 * kh + jj) for jj in range(2) for b in ex]
            subs = [(b, j, x) for (b, j) in pairs for x in range(2)]
            qb_, gate_, dya_, do2_ = {}, {}, {}, {}
            for w in pairs:
                b, j = w
                qb_[w] = (p_ref[b, :, C_AQ + BLOCK * j:C_AQ + BLOCK * (j + 1)] * ATT_SCALE).astype(BF16)
                gate_[w] = p_ref[b, :, C_AG + BLOCK * j:C_AG + BLOCK * (j + 1)]
                dya_[w] = dbr_ref[b, :, BLOCK * j:BLOCK * (j + 1)]
                do2_[w] = (dya_[w] * _silu(gate_[w])).astype(BF16)
            qk_ = {(b, j, x): _nt(qb_[(b, j)], ks[b][x]) for (b, j, x) in subs}
            dpm_ = {(b, j, x): _nt(do2_[(b, j)], vs[b][x]) for (b, j, x) in subs}
            pb_, dsb_ = {}, {}
            for u in subs:
                b, j, x = u
                h = 2 * j + x
                p, p_sink = _softmax_of(qk_[u], bias_s[h], mask, sk_ref[layer, h])
                pb_[u] = p.astype(BF16)
                delta = jnp.sum(p * dpm_[u], axis=-1, keepdims=True)
                ds = p * (dpm_[u] - delta)
                dbias_s[h] += ds
                dsk_ref[h:h + 1, :] += jnp.broadcast_to(
                    jnp.sum(-p_sink * delta, axis=0, keepdims=True), (1, BLOCK))
                dsb_[u] = ds.astype(BF16)
            o_ = {u: _nn(pb_[u], vs[u[0]][u[2]]) for u in subs}
            dq_ = {u: _nn(dsb_[u], ks[u[0]][u[2]]) for u in subs}
            dkm_ = {u: _tn(dsb_[u], qb_[(u[0], u[1])]) for u in subs}
            dvm_ = {u: _tn(pb_[u], do2_[(u[0], u[1])]) for u in subs}
            for w in pairs:
                b, j = w
                o2 = o_[(b, j, 0)] + o_[(b, j, 1)]
                dq2 = (dq_[(b, j, 0)] + dq_[(b, j, 1)]) * ATT_SCALE
                dp_ref[b, :, C_AQ + BLOCK * j:C_AQ + BLOCK * (j + 1)] = dq2.astype(BF16)
                dp_ref[b, :, C_AG + BLOCK * j:C_AG + BLOCK * (j + 1)] = (
                    dya_[w] * o2 * _dsilu(gate_[w])).astype(BF16)
            for b in ex:
                j0, j1 = 2 * kh, 2 * kh + 1
                dk_tot[b] = dk_tot[b] + _merge_heads(dkm_[(b, j0, 0)] + dkm_[(b, j1, 0)],
                                                     dkm_[(b, j0, 1)] + dkm_[(b, j1, 1)], kh)
                dv_tot[b] = dv_tot[b] + _merge_heads(dvm_[(b, j0, 0)] + dvm_[(b, j1, 0)],
                                                     dvm_[(b, j0, 1)] + dvm_[(b, j1, 1)], kh)
        for b in ex:
            dp_ref[b, :, C_AK:C_AK + BLOCK] = (dk_tot[b][BLOCK:, :] + dkv_s[b, :, 0:BLOCK]).astype(BF16)
            dp_ref[b, :, C_AV:C_AV + BLOCK] = (dv_tot[b][BLOCK:, :] + dkv_s[b, :, BLOCK:]).astype(BF16)
            dkv_s[b, :, 0:BLOCK] = dk_tot[b][0:BLOCK, :]
            dkv_s[b, :, BLOCK:] = dv_tot[b][0:BLOCK, :]

        cosv = cos_ref[...]
        sinv = sin_ref[...]
        tabs = [_decay_tables(h) for h in range(RET_HEADS)]
        units = [(b, h) for h in range(RET_HEADS) for b in ex]
        sl = lambda c0, h: slice(c0 + BLOCK * h, c0 + BLOCK * (h + 1))
        q_, k_, v_, vb_, sp_ = {}, {}, {}, {}, {}
        for u in units:
            b, h = u
            q_[u] = _rot(p_ref[b, :, sl(C_RQ, h)], cosv, sinv).astype(BF16)
            k_[u] = (_rot(p_ref[b, :, sl(C_RK, h)], cosv, sinv) * RET_SCALE * valid).astype(BF16)
            v_[u] = p_ref[b, :, sl(C_RV, h)]
            vb_[u] = v_[u].astype(BF16)
            sp_[u] = st_ref[b, 0, h].astype(BF16)
        qk_ = {u: _nt(q_[u], k_[u]) for u in units}
        qs_ = {u: _nn(q_[u], sp_[u]) for u in units}
        a_ = {u: (qk_[u] * tabs[u[1]][0]).astype(BF16) for u in units}
        av_ = {u: _nn(a_[u], vb_[u]) for u in units}
        dob_, dxo_ = {}, {}
        for u in units:
            b, h = u
            xi = tabs[h][2]
            o = av_[u] + xi * qs_[u]
            mu = jnp.mean(o, axis=-1, keepdims=True)
            var = jnp.mean(jnp.square(o - mu), axis=-1, keepdims=True)
            rstd = lax.rsqrt(var + GN_EPS)
            oh = (o - mu) * rstd
            gate = p_ref[b, :, sl(C_RG, h)]
            d_yr = dbr_ref[b, :, BRANCH_WIDTH + BLOCK * h:BRANCH_WIDTH + BLOCK * (h + 1)]
            dp_ref[b, :, sl(C_RG, h)] = (d_yr * oh * _dsilu(gate)).astype(BF16)
            doh = d_yr * _silu(gate)
            do = rstd * (doh - jnp.mean(doh, axis=-1, keepdims=True)
                         - oh * jnp.mean(doh * oh, axis=-1, keepdims=True))
            dob_[u] = do.astype(BF16)
            dxo_[u] = (do * xi).astype(BF16)
        dov_ = {u: _nt(dob_[u], vb_[u]) for u in units}
        dv1_ = {u: _tn(a_[u], dob_[u]) for u in units}
        dq1_ = {u: _nt(dxo_[u], sp_[u]) for u in units}
        gq_ = {u: _tn(q_[u], dxo_[u]) for u in units}
        da_, gb_, zv_ = {}, {}, {}
        for u in units:
            b, h = u
            da_[u] = (dov_[u] * tabs[h][0]).astype(BF16)
            g_next = g_s[b, h]
            gb_[u] = g_next.astype(BF16)
            zv_[u] = (v_[u] * tabs[h][1]).astype(BF16)
            g_s[b, h] = tabs[h][3] * g_next + gq_[u]
        dq2_ = {u: _nn(da_[u], k_[u]) for u in units}
        dk1_ = {u: _tn(da_[u], q_[u]) for u in units}
        dk2_ = {u: _nt(zv_[u], gb_[u]) for u in units}
        dv2_ = {u: _nn(k_[u], gb_[u]) for u in units}
        for u in units:
            b, h = u
            dp_ref[b, :, sl(C_RQ, h)] = _rot_t(dq2_[u] + dq1_[u], cosv, sinv).astype(BF16)
            dp_ref[b, :, sl(C_RK, h)] = _rot_t((dk1_[u] + dk2_[u]) * (RET_SCALE * valid), cosv, sinv).astype(BF16)
            dp_ref[b, :, sl(C_RV, h)] = (dv1_[u] + tabs[h][1] * dv2_[u]).astype(BF16)

        w0, w1, w2 = cw_ref[0:1, :], cw_ref[1:2, :], cw_ref[2:3, :]
        for b in ex:
            cb = p_ref[b, :, C_CB:C_CB + BRANCH_WIDTH]
            cc = p_ref[b, :, C_CC:C_CC + BRANCH_WIDTH]
            cx = p_ref[b, :, C_CX:C_CX + BRANCH_WIDTH]
            cg = p_ref[b, :, C_CG:C_CG + BRANCH_WIDTH]
            u = cc * cx * valid
            u_prev = (cp_ref[b, :, 0:BRANCH_WIDTH] * cp_ref[b, :, BRANCH_WIDTH:2 * BRANCH_WIDTH]
                      * (_valid_col(n - 1) * has_prev))
            u1 = _shift_down(u, u_prev, 1)
            u2 = _shift_down(u, u_prev, 2)
            y = w0 * u2 + w1 * u1 + w2 * u
            d_yc = dbr_ref[b, :, 2 * BRANCH_WIDTH:3 * BRANCH_WIDTH]
            sg = _silu(cg)
            dp_ref[b, :, C_CB:C_CB + BRANCH_WIDTH] = (d_yc * y * sg).astype(BF16)
            dp_ref[b, :, C_CG:C_CG + BRANCH_WIDTH] = (d_yc * cb * y * _dsilu(cg)).astype(BF16)
            dy = d_yc * cb * sg
            dy_next = dy_s[b]
            du = (w2 * dy + w1 * _shift_up(dy, dy_next, 1) + w0 * _shift_up(dy, dy_next, 2)) * valid
            dp_ref[b, :, C_CC:C_CC + BRANCH_WIDTH] = (du * cx).astype(BF16)
            dp_ref[b, :, C_CX:C_CX + BRANCH_WIDTH] = (du * cc).astype(BF16)
            dcw_ref[0:1, :] += jnp.sum(dy * u2, axis=0, keepdims=True)
            dcw_ref[1:2, :] += jnp.sum(dy * u1, axis=0, keepdims=True)
            dcw_ref[2:3, :] += jnp.sum(dy * u, axis=0, keepdims=True)
            dy_s[b] = dy

        @pl.when(step == nc - 1)
        def _():
            bkt = bkt_ref[...]
            row = lax.broadcasted_iota(jnp.int32, (N_BUCKETS, BLOCK), 0)
            lane = lax.broadcasted_iota(jnp.int32, (N_BUCKETS, BLOCK), 1)

            def one_bucket(bk, acc):
                sel = bkt == bk
                for h in range(ATT_HEADS):
                    t = jnp.where(sel, dbias_s[h], 0.0)
                    s = jnp.sum(jnp.sum(t, axis=1, keepdims=True), axis=0, keepdims=True)
                    acc = acc + jnp.where((row == bk) & (lane == h), jnp.broadcast_to(s, acc.shape), 0.0)
                return acc

            drb_ref[...] = lax.fori_loop(0, N_BUCKETS, one_bucket, jnp.zeros((N_BUCKETS, BLOCK), F32))

    lp = nc * BLOCK
    smem = pl.BlockSpec(memory_space=pltpu.SMEM)
    blk = lambda s: nc - 1 - s
    prev = lambda s: jnp.maximum(nc - 2 - s, 0)
    proj3 = proj.reshape(nb, lp, ABC_WIDTH)
    res = pl.pallas_call(
        body, name="mixers_bwd",
        grid=(nc,),
        in_specs=[pl.BlockSpec((nb, BLOCK, ABC_WIDTH), lambda s: (0, blk(s), 0)),
                  pl.BlockSpec((nb, BLOCK, 2 * BLOCK), lambda s: (0, prev(s), C_AK // (2 * BLOCK))),
                  pl.BlockSpec((nb, BLOCK, 1280), lambda s: (0, prev(s), C_CC // 1280)),
                  pl.BlockSpec((nb, BLOCK, N_BRANCH * BRANCH_WIDTH), lambda s: (0, blk(s), 0)),
                  pl.BlockSpec((nb, 1, RET_HEADS, BLOCK, BLOCK), lambda s: (0, blk(s), 0, 0, 0)),
                  pl.BlockSpec((BLOCK, BLOCK), lambda s: (blk(s), 0)),
                  pl.BlockSpec((BLOCK, BLOCK), lambda s: (blk(s), 0)),
                  pl.BlockSpec((BLOCK, 2 * BLOCK), lambda s: (0, 0)),
                  smem, smem,
                  pl.BlockSpec((None, 3, BRANCH_WIDTH), lambda s: (layer, 0, 0))],
        out_specs=[pl.BlockSpec((nb, BLOCK, ABC_WIDTH), lambda s: (0, blk(s), 0)),
                   pl.BlockSpec((N_BUCKETS, BLOCK), lambda s: (0, 0)),
                   pl.BlockSpec((ATT_HEADS, BLOCK), lambda s: (0, 0)),
                   pl.BlockSpec((8, BRANCH_WIDTH), lambda s: (0, 0))],
        out_shape=[jax.ShapeDtypeStruct((nb, lp, ABC_WIDTH), BF16),
                   jax.ShapeDtypeStruct((N_BUCKETS, BLOCK), F32),
                   jax.ShapeDtypeStruct((ATT_HEADS, BLOCK), F32),
                   jax.ShapeDtypeStruct((8, BRANCH_WIDTH), F32)],
        scratch_shapes=[pltpu.VMEM((ATT_HEADS, BLOCK, 2 * BLOCK), F32),
                        pltpu.VMEM((ATT_HEADS, BLOCK, 2 * BLOCK), F32),
                        pltpu.VMEM((nb, BLOCK, 2 * BLOCK), F32),
                        pltpu.VMEM((nb, RET_HEADS, BLOCK, BLOCK), F32),
                        pltpu.VMEM((nb, BLOCK, BRANCH_WIDTH), F32)],
        compiler_params=_cparams("arbitrary"),
    )(proj3, proj3, proj3, d_br.reshape(nb, lp, N_BRANCH * BRANCH_WIDTH), states, cosf, sinf, bkt, rel_bias, sinks,
      conv_w)
    return (res[0].reshape(nb * lp, ABC_WIDTH),) + tuple(res[1:])


MERGE_TILE = 256
MERGE_FWD_TILE = 544


def _merge_forward(br_ref, m_ref, wb_ref, wo_ref):
    bo, gates = [], []
    mixed_pre = None
    for g in range(N_BRANCH):
        br_g = br_ref[:, BRANCH_WIDTH * g:BRANCH_WIDTH * (g + 1)]
        bo_g = jnp.concatenate([_nn(br_g, wb_ref[p, g]) for p in range(N_CHIPS)], axis=1)
        gate_g = _sigmoid(m_ref[:, D_MODEL * g:D_MODEL * (g + 1)].astype(F32))
        bo.append(bo_g)
        gates.append(gate_g)
        mixed_pre = gate_g * bo_g if mixed_pre is None else mixed_pre + gate_g * bo_g
    mixed = _nn(mixed_pre.astype(BF16), wo_ref[...])
    r = lax.rsqrt(jnp.mean(mixed * mixed, axis=-1, keepdims=True) + RMS_EPS)
    return bo, gates, mixed_pre, mixed, r


def merge_fwd(x2d, br, pm, wb, wo, g_post, layer, after):
    t = x2d.shape[0]
    tm = MERGE_FWD_TILE if t % MERGE_FWD_TILE == 0 else BLOCK

    def body(x_ref, br_ref, m_ref, wb_ref, wo_ref, g_ref, after_ref, o_ref):
        _, _, _, mixed, r = _merge_forward(br_ref, m_ref, wb_ref, wo_ref)
        o_ref[...] = x_ref[...] + mixed * r * g_ref[...]

    return pl.pallas_call(
        body, name="merge_fwd",
        grid=(t // tm,),
        in_specs=[pl.BlockSpec((tm, D_MODEL), lambda i: (i, 0)),
                  pl.BlockSpec((tm, N_BRANCH * BRANCH_WIDTH), lambda i: (i, 0)),
                  pl.BlockSpec((tm, MERGE_WIDTH), lambda i: (i, 0)),
                  pl.BlockSpec((N_CHIPS, N_BRANCH, BRANCH_WIDTH, SHARD_D), lambda i: (0, 0, 0, 0)),
                  pl.BlockSpec((D_MODEL, D_MODEL), lambda i: (0, 0)),
                  pl.BlockSpec((None, 1, D_MODEL), lambda i: (layer, 0, 0)),
                  ANY],
        out_specs=pl.BlockSpec((tm, D_MODEL), lambda i: (i, 0)),
        out_shape=jax.ShapeDtypeStruct((t, D_MODEL), F32),
        compiler_params=_cparams("parallel"),
    )(x2d, br, pm, wb, wo, g_post, after)


def merge_fwd_loss(x2d, br, pm, wb, wo, g_post, layer, target, lp):
    t = x2d.shape[0]
    tm = MERGE_FWD_TILE if t % MERGE_FWD_TILE == 0 else BLOCK

    def body(x_ref, br_ref, m_ref, wb_ref, wo_ref, g_ref, t_ref, l_ref, d_ref):
        i = pl.program_id(0)

        @pl.when(i == 0)
        def _():
            l_ref[...] = jnp.zeros_like(l_ref)

        _, _, _, mixed, r = _merge_forward(br_ref, m_ref, wb_ref, wo_ref)
        y = x_ref[...] + mixed * r * g_ref[...]
        row = i * tm + lax.broadcasted_iota(jnp.int32, (tm, 1), 0)
        e = jnp.where(row % lp >= BLOCK, y - t_ref[...], 0.0)
        d_ref[...] = e * (1.0 / D_MODEL)
        s = jnp.sum(jnp.sum(e * e, axis=1, keepdims=True), axis=0, keepdims=True)
        l_ref[...] += jnp.broadcast_to(s * (0.5 / D_MODEL), l_ref.shape)

    return pl.pallas_call(
        body, name="merge_fwd_loss",
        grid=(t // tm,),
        in_specs=[pl.BlockSpec((tm, D_MODEL), lambda i: (i, 0)),
                  pl.BlockSpec((tm, N_BRANCH * BRANCH_WIDTH), lambda i: (i, 0)),
                  pl.BlockSpec((tm, MERGE_WIDTH), lambda i: (i, 0)),
                  pl.BlockSpec((N_CHIPS, N_BRANCH, BRANCH_WIDTH, SHARD_D), lambda i: (0, 0, 0, 0)),
                  pl.BlockSpec((D_MODEL, D_MODEL), lambda i: (0, 0)),
                  pl.BlockSpec((None, 1, D_MODEL), lambda i: (layer, 0, 0)),
                  pl.BlockSpec((tm, D_MODEL), lambda i: (i, 0))],
        out_specs=[pl.BlockSpec((8, BLOCK), lambda i: (0, 0)),
                   pl.BlockSpec((tm, D_MODEL), lambda i: (i, 0))],
        out_shape=[jax.ShapeDtypeStruct((8, BLOCK), F32),
                   jax.ShapeDtypeStruct((t, D_MODEL), F32)],
        compiler_params=_cparams("arbitrary"),
    )(x2d, br, pm, wb, wo, g_post, target)


def merge_bwd(d_out, br, pm, wb, wo, g_post, layer, after):
    t = d_out.shape[0]
    tm = MERGE_TILE if t % MERGE_TILE == 0 else BLOCK

    def body(do_ref, br_ref, m_ref, wb_ref, wo_ref, g_ref, after_ref, dbr_ref, dm_ref, dg_ref, dwb_ref, dwo_ref):

        @pl.when(pl.program_id(0) == 0)
        def _():
            dwb_ref[...] = jnp.zeros_like(dwb_ref)
            dwo_ref[...] = jnp.zeros_like(dwo_ref)
            dg_ref[...] = jnp.zeros_like(dg_ref)

        bo, gates, mixed_pre, mixed, r = _merge_forward(br_ref, m_ref, wb_ref, wo_ref)
        d_o = do_ref[...]
        nh = mixed * r
        dg_ref[0:1, :] += jnp.sum(d_o * nh, axis=0, keepdims=True)
        dn = d_o * g_ref[...]
        d_mixed = (r * (dn - nh * jnp.mean(dn * nh, axis=-1, keepdims=True))).astype(BF16)
        dwo_ref[...] += _tn(mixed_pre.astype(BF16), d_mixed)
        d_pre = _nt(d_mixed, wo_ref[...])
        for g in range(N_BRANCH):
            br_g = br_ref[:, BRANCH_WIDTH * g:BRANCH_WIDTH * (g + 1)]
            d_bo = (d_pre * gates[g]).astype(BF16)
            dm_ref[:, D_MODEL * g:D_MODEL * (g + 1)] = (
                d_pre * bo[g] * gates[g] * (1.0 - gates[g])).astype(BF16)
            d_br_g = None
            for p in range(N_CHIPS):
                d_bo_p = d_bo[:, SHARD_D * p:SHARD_D * (p + 1)]
                part = _nt(d_bo_p, wb_ref[p, g])
                d_br_g = part if d_br_g is None else d_br_g + part
                dwb_ref[p, g] += _tn(br_g, d_bo_p)
            dbr_ref[:, BRANCH_WIDTH * g:BRANCH_WIDTH * (g + 1)] = d_br_g.astype(BF16)

    return pl.pallas_call(
        body, name="merge_bwd",
        grid=(t // tm,),
        in_specs=[pl.BlockSpec((tm, D_MODEL), lambda i: (i, 0)),
                  pl.BlockSpec((tm, N_BRANCH * BRANCH_WIDTH), lambda i: (i, 0)),
                  pl.BlockSpec((tm, MERGE_WIDTH), lambda i: (i, 0)),
                  pl.BlockSpec((N_CHIPS, N_BRANCH, BRANCH_WIDTH, SHARD_D), lambda i: (0, 0, 0, 0)),
                  pl.BlockSpec((D_MODEL, D_MODEL), lambda i: (0, 0)),
                  pl.BlockSpec((None, 1, D_MODEL), lambda i: (layer, 0, 0)),
                  ANY],
        out_specs=[pl.BlockSpec((tm, N_BRANCH * BRANCH_WIDTH), lambda i: (i, 0)),
                   pl.BlockSpec((tm, MERGE_WIDTH), lambda i: (i, 0)),
                   pl.BlockSpec((8, D_MODEL), lambda i: (0, 0)),
                   pl.BlockSpec((N_CHIPS, N_BRANCH, BRANCH_WIDTH, SHARD_D), lambda i: (0, 0, 0, 0)),
                   pl.BlockSpec((D_MODEL, D_MODEL), lambda i: (0, 0))],
        out_shape=[jax.ShapeDtypeStruct((t, N_BRANCH * BRANCH_WIDTH), BF16),
                   jax.ShapeDtypeStruct((t, MERGE_WIDTH), BF16),
                   jax.ShapeDtypeStruct((8, D_MODEL), F32),
                   jax.ShapeDtypeStruct((N_CHIPS, N_BRANCH, BRANCH_WIDTH, SHARD_D), F32),
                   jax.ShapeDtypeStruct((D_MODEL, D_MODEL), F32)],
        compiler_params=_cparams("arbitrary"),
    )(d_out, br, pm, wb, wo, g_post, after)


N_ABC_TILES = ABC_WIDTH // COL_TILE
N_M_TILES = MERGE_WIDTH // COL_TILE


def proj_dgrad(d_abc, d_m, w, x2d, g, layer, d_out, after):
    t = x2d.shape[0]
    tm = ROW_TILE if t % ROW_TILE == 0 else BLOCK
    nk = N_ABC_TILES + N_M_TILES

    def body(da_ref, dm_ref, w_ref, x_ref, g_ref, do_ref, after_ref, dx_ref, dg_ref, acc):
        i = pl.program_id(0)
        k = pl.program_id(1)

        @pl.when((i == 0) & (k == 0))
        def _():
            dg_ref[...] = jnp.zeros_like(dg_ref)

        @pl.when(k == 0)
        def _():
            acc[...] = jnp.zeros_like(acc)

        @pl.when(k < N_ABC_TILES)
        def _():
            acc[...] += _nn(da_ref[...], w_ref[...])

        @pl.when(k >= N_ABC_TILES)
        def _():
            acc[...] += _nn(dm_ref[...], w_ref[...])

        @pl.when(k == nk - 1)
        def _():
            x = x_ref[...]
            r = lax.rsqrt(jnp.mean(x * x, axis=-1, keepdims=True) + RMS_EPS)
            nh = x * r
            dh = acc[...]
            dg_ref[0:1, :] += jnp.sum(dh * nh, axis=0, keepdims=True)
            dn = dh * g_ref[...]
            dx_ref[...] = do_ref[...] + r * (dn - nh * jnp.mean(dn * nh, axis=-1, keepdims=True))

    return pl.pallas_call(
        body, name="proj_dgrad",
        grid=(t // tm, nk),
        in_specs=[pl.BlockSpec((tm, COL_TILE), lambda i, k: (i, jnp.minimum(k, N_ABC_TILES - 1))),
                  pl.BlockSpec((tm, COL_TILE), lambda i, k: (i, jnp.maximum(k - N_ABC_TILES, 0))),
                  pl.BlockSpec((COL_TILE, D_MODEL), lambda i, k: (k, 0)),
                  pl.BlockSpec((tm, D_MODEL), lambda i, k: (i, 0)),
                  pl.BlockSpec((None, 1, D_MODEL), lambda i, k: (layer, 0, 0)),
                  pl.BlockSpec((tm, D_MODEL), lambda i, k: (i, 0)),
                  ANY],
        out_specs=[pl.BlockSpec((tm, D_MODEL), lambda i, k: (i, 0)),
                   pl.BlockSpec((8, D_MODEL), lambda i, k: (0, 0))],
        out_shape=[jax.ShapeDtypeStruct((t, D_MODEL), F32),
                   jax.ShapeDtypeStruct((8, D_MODEL), F32)],
        scratch_shapes=[pltpu.VMEM((tm, D_MODEL), F32)],
        compiler_params=_cparams("arbitrary", "arbitrary"),
    )(d_abc, d_m, w, x2d, g, d_out, after)


def proj_wgrad(hb, d_abc, d_m):
    t = hb.shape[0]
    nj = N_ABC_TILES + N_M_TILES

    def body(h_ref, da_ref, dm_ref, o_ref):
        j = pl.program_id(0)

        @pl.when(j < N_ABC_TILES)
        def _():
            o_ref[...] = _tn(da_ref[...], h_ref[...])

        @pl.when(j >= N_ABC_TILES)
        def _():
            o_ref[...] = _tn(dm_ref[...], h_ref[...])

    return pl.pallas_call(
        body, name="proj_wgrad",
        grid=(nj,),
        in_specs=[pl.BlockSpec((t, D_MODEL), lambda j: (0, 0)),
                  pl.BlockSpec((t, COL_TILE), lambda j: (0, jnp.minimum(j, N_ABC_TILES - 1))),
                  pl.BlockSpec((t, COL_TILE), lambda j: (0, jnp.maximum(j - N_ABC_TILES, 0)))],
        out_specs=pl.BlockSpec((COL_TILE, D_MODEL), lambda j: (j, 0)),
        out_shape=jax.ShapeDtypeStruct((PROJ_WIDTH, D_MODEL), F32),
        compiler_params=_cparams("arbitrary"),
    )(hb, d_abc, d_m)


def _adamw_math(w, g, m, v):
    m = ADAM_B1 * m + (1.0 - ADAM_B1) * g
    v = ADAM_B2 * v + (1.0 - ADAM_B2) * jnp.square(g)
    m_hat = m / (1.0 - ADAM_B1 ** ADAM_STEP)
    v_hat = v / (1.0 - ADAM_B2 ** ADAM_STEP)
    delta = -ADAM_LR * (m_hat / (jnp.sqrt(v_hat) + ADAM_EPS) + ADAM_WD * w)
    return delta, m, v


def adamw_layer(w, g, m, v, layer, acc, after):
    _, r, c = w.shape
    tr = _row_tile(r)

    def body(*refs):
        w_ref, g_ref, m_ref, v_ref = refs[:4]
        go_ref, d_ref, mo_ref, vo_ref = refs[-4:]
        g_val = g_ref[...]
        d, m_new, v_new = _adamw_math(w_ref[...], g_val, m_ref[...], v_ref[...])
        go_ref[...] = g_val
        d_ref[...] = d
        mo_ref[...] = m_new
        vo_ref[...] = v_new

    slab = pl.BlockSpec((None, tr, c), lambda i: (layer, i, 0))
    ins = [w, g, m, v, after]
    in_specs = [slab, pl.BlockSpec((tr, c), lambda i: (i, 0)), slab, slab, ANY]
    aliases = {}
    if acc is not None:
        ins += list(acc)
        in_specs += [ANY] * 4
        aliases = {5 + i: i for i in range(4)}
    return pl.pallas_call(
        body, name="adamw_layer",
        grid=(r // tr,),
        in_specs=in_specs, out_specs=[slab] * 4,
        out_shape=[jax.ShapeDtypeStruct(w.shape, F32)] * 4,
        input_output_aliases=aliases,
        compiler_params=_cparams("parallel"),
    )(*ins)


def adamw_small(params):
    k = len(params)

    def body(*refs):
        ins, outs = refs[:4 * k], refs[4 * k:]
        for i in range(k):
            d, m_new, v_new = _adamw_math(*[r[...] for r in ins[4 * i:4 * i + 4]])
            outs[3 * i][...] = d
            outs[3 * i + 1][...] = m_new
            outs[3 * i + 2][...] = v_new

    flat = [a for p in params for a in p]
    vm = pl.BlockSpec(memory_space=pltpu.VMEM)
    out_shape = [jax.ShapeDtypeStruct(p[0].shape, F32) for p in params for _ in range(3)]
    res = pl.pallas_call(
        body, name="adamw_small",
        in_specs=[vm] * len(flat), out_specs=[vm] * len(out_shape), out_shape=out_shape,
    )(*flat)
    return [tuple(res[3 * i:3 * i + 3]) for i in range(k)]


ANY = pl.BlockSpec(memory_space=pl.ANY)


def _place():
    return lax.axis_index("x"), lax.axis_index("y"), lax.axis_index("c")


HBM = pl.BlockSpec(memory_space=pltpu.HBM)
SEM = pl.BlockSpec(memory_space=pltpu.SEMAPHORE)
EFFECT = pltpu.SideEffectType.DATAFLOW_SIDE_EFFECTING


def _other_chips(x, y):
    return [(1 - x, y), (x, 1 - y), (1 - x, 1 - y)]


def _own_slot(shard, chip):
    buf = lax.empty((N_CHIPS,) + shard.shape, shard.dtype)
    return lax.dynamic_update_slice(buf, shard[None], (chip, 0, 0, 0))


def _hbm(a):
    return pltpu.with_memory_space_constraint(a, pltpu.HBM)


def gather_start(bufs, after):
    n = len(bufs)

    def body(*refs):
        g_refs = refs[:n]
        send_sems, recv_sems = refs[n + 1], refs[n + 2]
        token = refs[-1]
        x, y, c = _place()
        me_p = 2 * x + y
        for t in range(n):
            for k, (qx, qy) in enumerate(_other_chips(x, y)):
                slab = g_refs[t].at[me_p, c]
                pltpu.make_async_remote_copy(src_ref=slab, dst_ref=slab, send_sem=send_sems.at[3 * t + k],
                                             recv_sem=recv_sems.at[3 * t + k], device_id=(qx, qy, c),
                                             device_id_type=MESH).start()
        token[...] = jnp.zeros_like(token)

    res = pl.pallas_call(
        body, name="gather_start",
        in_specs=[HBM] * n + [ANY],
        out_specs=[SEM, SEM] + [HBM] * n + [pl.BlockSpec(memory_space=pltpu.VMEM)],
        out_shape=[pltpu.SemaphoreType.DMA((3 * n,)), pltpu.SemaphoreType.DMA((3 * n,))]
        + [pltpu.HBM(b.shape, b.dtype) for b in bufs] + [jax.ShapeDtypeStruct((8, LANES), F32)],
        input_output_aliases={t: 2 + t for t in range(n)},
        compiler_params=pltpu.CompilerParams(has_side_effects=EFFECT),
    )(*[_hbm(b) for b in bufs], after)
    return res[0], res[1], list(res[2:2 + n]), res[-1]


def gather_wait(bufs, send_sems, recv_sems, after, first=0):
    n = len(bufs)

    def body(*refs):
        g_refs = refs[:n]
        send_sems, recv_sems = refs[n], refs[n + 1]
        x, y, c = _place()
        me_p = 2 * x + y
        for t in range(n):
            for k, (qx, qy) in enumerate(_other_chips(x, y)):
                s = 3 * (first + t) + k
                cp = pltpu.make_async_remote_copy(src_ref=g_refs[t].at[me_p, c], dst_ref=g_refs[t].at[2 * qx + qy, c],
                                                  send_sem=send_sems.at[s], recv_sem=recv_sems.at[s],
                                                  device_id=(qx, qy, c), device_id_type=MESH)
                cp.wait_send()
                cp.wait_recv()

    return pl.pallas_call(
        body, name="gather_wait",
        in_specs=[HBM] * n + [SEM, SEM, ANY],
        out_specs=[HBM] * n,
        out_shape=[pltpu.HBM(b.shape, b.dtype) for b in bufs],
        input_output_aliases={t: t for t in range(n)},
        compiler_params=pltpu.CompilerParams(has_side_effects=EFFECT),
    )(*bufs, send_sems, recv_sems, after)


def gather_forward(bufs):
    n = len(bufs)

    def body(*refs):
        g_refs = refs[n:2 * n]
        send_sems, recv_sems = refs[2 * n:]
        x, y, c = _place()
        sibling = (x, y, 1 - c)
        chips = _other_chips(x, y)
        passed = []
        for t in range(n):
            for k, (qx, qy) in enumerate(chips):
                slab = g_refs[t].at[2 * qx + qy, c]
                fwd = pltpu.make_async_remote_copy(src_ref=slab, dst_ref=slab, send_sem=send_sems.at[3 * t + k],
                                                   recv_sem=recv_sems.at[3 * t + k], device_id=sibling,
                                                   device_id_type=MESH)
                fwd.start()
                passed.append(fwd)
        for t in range(n):
            for k, (qx, qy) in enumerate(chips):
                slab = g_refs[t].at[2 * qx + qy, 1 - c]
                pltpu.make_async_remote_copy(src_ref=slab, dst_ref=slab, send_sem=send_sems.at[3 * t + k],
                                             recv_sem=recv_sems.at[3 * t + k], device_id=sibling,
                                             device_id_type=MESH).wait_recv()
        for cp in passed:
            cp.wait_send()

    return pl.pallas_call(
        body, name="gather_forward",
        in_specs=[ANY] * n, out_specs=[ANY] * n,
        out_shape=[jax.ShapeDtypeStruct(b.shape, b.dtype) for b in bufs],
        input_output_aliases={t: t for t in range(n)},
        scratch_shapes=[pltpu.SemaphoreType.DMA((3 * n,)), pltpu.SemaphoreType.DMA((3 * n,))],
    )(*bufs)


def forward_start(bufs):
    n = len(bufs)

    def body(*refs):
        g_refs = refs[:n]
        send_sems, recv_sems = refs[n], refs[n + 1]
        token = refs[-1]
        x, y, c = _place()
        for t in range(n):
            for k, (qx, qy) in enumerate(_other_chips(x, y)):
                slab = g_refs[t].at[2 * qx + qy, c]
                pltpu.make_async_remote_copy(src_ref=slab, dst_ref=slab, send_sem=send_sems.at[3 * t + k],
                                             recv_sem=recv_sems.at[3 * t + k], device_id=(x, y, 1 - c),
                                             device_id_type=MESH).start()
        token[...] = jnp.zeros_like(token)

    res = pl.pallas_call(
        body, name="forward_start",
        in_specs=[HBM] * n,
        out_specs=[SEM, SEM] + [HBM] * n + [pl.BlockSpec(memory_space=pltpu.VMEM)],
        out_shape=[pltpu.SemaphoreType.DMA((3 * n,)), pltpu.SemaphoreType.DMA((3 * n,))]
        + [pltpu.HBM(b.shape, b.dtype) for b in bufs] + [jax.ShapeDtypeStruct((8, LANES), F32)],
        input_output_aliases={t: 2 + t for t in range(n)},
        compiler_params=pltpu.CompilerParams(has_side_effects=EFFECT),
    )(*[_hbm(b) for b in bufs])
    return res[0], res[1], list(res[2:2 + n]), res[-1]


def forward_wait(bufs, send_sems, recv_sems, after):
    n = len(bufs)

    def body(*refs):
        g_refs = refs[:n]
        send_sems, recv_sems = refs[n], refs[n + 1]
        x, y, c = _place()
        for t in range(n):
            for k, (qx, qy) in enumerate(_other_chips(x, y)):
                cp = pltpu.make_async_remote_copy(src_ref=g_refs[t].at[2 * qx + qy, c],
                                                  dst_ref=g_refs[t].at[2 * qx + qy, 1 - c],
                                                  send_sem=send_sems.at[3 * t + k], recv_sem=recv_sems.at[3 * t + k],
                                                  device_id=(x, y, 1 - c), device_id_type=MESH)
                cp.wait_send()
                cp.wait_recv()

    return pl.pallas_call(
        body, name="forward_wait",
        in_specs=[HBM] * n + [SEM, SEM, ANY],
        out_specs=[HBM] * n,
        out_shape=[pltpu.HBM(b.shape, b.dtype) for b in bufs],
        input_output_aliases={t: t for t in range(n)},
        compiler_params=pltpu.CompilerParams(has_side_effects=EFFECT),
    )(*bufs, send_sems, recv_sems, after)


def small_start(pack, me, after):
    buf = lax.dynamic_update_slice(lax.empty((8,) + pack.shape, pack.dtype), pack[None], (me, 0, 0))

    def body(b_ref, after_ref, send_sems, recv_sems, thru, token):
        x, y, c = _place()
        slot = b_ref.at[4 * x + 2 * y + c]
        for k in range(1, 8):
            peer = (x ^ ((k >> 2) & 1), y ^ ((k >> 1) & 1), c ^ (k & 1))
            pltpu.make_async_remote_copy(src_ref=slot, dst_ref=slot, send_sem=send_sems.at[k - 1],
                                         recv_sem=recv_sems.at[k - 1], device_id=peer, device_id_type=MESH).start()
        token[...] = jnp.zeros_like(token)

    return pl.pallas_call(
        body, name="small_start",
        in_specs=[HBM, ANY],
        out_specs=[SEM, SEM, HBM, pl.BlockSpec(memory_space=pltpu.VMEM)],
        out_shape=[pltpu.SemaphoreType.DMA((7,)), pltpu.SemaphoreType.DMA((7,)), pltpu.HBM(buf.shape, buf.dtype),
                   jax.ShapeDtypeStruct((8, LANES), F32)],
        input_output_aliases={0: 2},
        compiler_params=pltpu.CompilerParams(has_side_effects=EFFECT),
    )(_hbm(buf), after)


def small_wait(buf, send_sems, recv_sems, after):
    def body(b_ref, send_sems, recv_sems, after_ref, thru):
        x, y, c = _place()
        mine = b_ref.at[4 * x + 2 * y + c]
        for k in range(1, 8):
            peer = (x ^ ((k >> 2) & 1), y ^ ((k >> 1) & 1), c ^ (k & 1))
            cp = pltpu.make_async_remote_copy(src_ref=mine, dst_ref=b_ref.at[4 * peer[0] + 2 * peer[1] + peer[2]],
                                              send_sem=send_sems.at[k - 1], recv_sem=recv_sems.at[k - 1],
                                              device_id=peer, device_id_type=MESH)
            cp.wait_send()
            cp.wait_recv()

    return pl.pallas_call(
        body, name="small_wait",
        in_specs=[HBM, SEM, SEM, ANY], out_specs=HBM,
        out_shape=pltpu.HBM(buf.shape, buf.dtype),
        input_output_aliases={0: 0},
        compiler_params=pltpu.CompilerParams(has_side_effects=EFFECT),
    )(buf, send_sems, recv_sems, after)


def swap_start(grads, after):
    n = len(grads)

    def body(*refs):
        g_refs, l_refs = refs[:n], refs[n:2 * n]
        send_sems, recv_sems = refs[2 * n + 1], refs[2 * n + 2]
        token = refs[-1]
        x, y, c = _place()
        for t in range(n):
            for p in range(N_CHIPS):
                pltpu.make_async_remote_copy(src_ref=g_refs[t].at[p, 1 - c], dst_ref=l_refs[t].at[p],
                                             send_sem=send_sems.at[N_CHIPS * t + p],
                                             recv_sem=recv_sems.at[N_CHIPS * t + p],
                                             device_id=(x, y, 1 - c), device_id_type=MESH).start()
        token[...] = jnp.zeros_like(token)

    lands = [lax.empty((N_CHIPS,) + g.shape[2:], g.dtype) for g in grads]
    res = pl.pallas_call(
        body, name="swap_start",
        in_specs=[HBM] * (2 * n) + [ANY],
        out_specs=[SEM, SEM] + [HBM] * (2 * n) + [pl.BlockSpec(memory_space=pltpu.VMEM)],
        out_shape=[pltpu.SemaphoreType.DMA((N_CHIPS * n,)), pltpu.SemaphoreType.DMA((N_CHIPS * n,))]
        + [pltpu.HBM(a.shape, a.dtype) for a in grads + lands] + [jax.ShapeDtypeStruct((8, LANES), F32)],
        input_output_aliases={t: 2 + t for t in range(2 * n)},
        compiler_params=pltpu.CompilerParams(has_side_effects=EFFECT),
    )(*[_hbm(a) for a in grads + lands], after)
    return res[0], res[1], list(res[2:2 + n]), list(res[2 + n:2 + 2 * n]), res[-1]


def swap_wait(grads, lands, send_sems, recv_sems, after):
    n = len(grads)

    def body(*refs):
        g_refs, l_refs = refs[:n], refs[n:2 * n]
        send_sems, recv_sems = refs[2 * n], refs[2 * n + 1]
        x, y, c = _place()
        for t in range(n):
            for p in range(N_CHIPS):
                cp = pltpu.make_async_remote_copy(src_ref=g_refs[t].at[p, 1 - c], dst_ref=l_refs[t].at[p],
                                                  send_sem=send_sems.at[N_CHIPS * t + p],
                                                  recv_sem=recv_sems.at[N_CHIPS * t + p],
                                                  device_id=(x, y, 1 - c), device_id_type=MESH)
                cp.wait_send()
                cp.wait_recv()

    res = pl.pallas_call(
        body, name="swap_wait",
        in_specs=[HBM] * (2 * n) + [SEM, SEM, ANY],
        out_specs=[HBM] * (2 * n),
        out_shape=[pltpu.HBM(a.shape, a.dtype) for a in grads + lands],
        input_output_aliases={t: t for t in range(2 * n)},
        compiler_params=pltpu.CompilerParams(has_side_effects=EFFECT),
    )(*grads, *lands, send_sems, recv_sems, after)
    return list(res[:n]), list(res[n:])


def _row_tile(r):
    return max(t for t in range(16, 513, 16) if r % t == 0)


def add_own_half(g, other, c_arr):
    _, _, r, cols = g.shape
    tr = _row_tile(r)

    def body(c_ref, a_ref, b_ref, o_ref):
        o_ref[...] = (a_ref[...] + b_ref[...]).astype(BF16)

    return pl.pallas_call(
        body, name="add_own_half",
        grid_spec=pltpu.PrefetchScalarGridSpec(
            num_scalar_prefetch=1, grid=(N_CHIPS, r // tr),
            in_specs=[pl.BlockSpec((None, None, tr, cols), lambda p, i, c_ref: (p, c_ref[0], i, 0)),
                      pl.BlockSpec((None, tr, cols), lambda p, i, c_ref: (p, i, 0))],
            out_specs=pl.BlockSpec((None, tr, cols), lambda p, i, c_ref: (p, i, 0))),
        out_shape=jax.ShapeDtypeStruct((N_CHIPS, r, cols), BF16),
        compiler_params=_cparams("parallel", "parallel"),
    )(c_arr, g, other)


def scatter_start(partials):
    n = len(partials)

    def body(*refs):
        s_refs, l_refs = refs[:n], refs[n:2 * n]
        send_sems, recv_sems = refs[2 * n], refs[2 * n + 1]
        token = refs[-1]
        x, y, c = _place()
        for t in range(n):
            for k, (qx, qy) in enumerate(_other_chips(x, y)):
                pltpu.make_async_remote_copy(src_ref=s_refs[t].at[2 * qx + qy], dst_ref=l_refs[t].at[k],
                                             send_sem=send_sems.at[3 * t + k], recv_sem=recv_sems.at[3 * t + k],
                                             device_id=(qx, qy, c), device_id_type=MESH).start()
        token[...] = jnp.zeros_like(token)

    lands = [lax.empty((3,) + s.shape[1:], s.dtype) for s in partials]
    res = pl.pallas_call(
        body, name="scatter_start",
        in_specs=[HBM] * (2 * n),
        out_specs=[SEM, SEM] + [HBM] * (2 * n) + [pl.BlockSpec(memory_space=pltpu.VMEM)],
        out_shape=[pltpu.SemaphoreType.DMA((3 * n,)), pltpu.SemaphoreType.DMA((3 * n,))]
        + [pltpu.HBM(a.shape, a.dtype) for a in partials + lands] + [jax.ShapeDtypeStruct((8, LANES), F32)],
        input_output_aliases={t: 2 + t for t in range(2 * n)},
        compiler_params=pltpu.CompilerParams(has_side_effects=EFFECT),
    )(*[_hbm(a) for a in partials + lands])
    return res[0], res[1], list(res[2:2 + n]), list(res[2 + n:2 + 2 * n]), res[-1]


def scatter_wait(partials, lands, send_sems, recv_sems, after):
    n = len(partials)

    def body(*refs):
        s_refs, l_refs = refs[:n], refs[n:2 * n]
        send_sems, recv_sems = refs[2 * n], refs[2 * n + 1]
        x, y, c = _place()
        for t in range(n):
            for k, (qx, qy) in enumerate(_other_chips(x, y)):
                cp = pltpu.make_async_remote_copy(src_ref=s_refs[t].at[2 * qx + qy], dst_ref=l_refs[t].at[k],
                                                  send_sem=send_sems.at[3 * t + k], recv_sem=recv_sems.at[3 * t + k],
                                                  device_id=(qx, qy, c), device_id_type=MESH)
                cp.wait_send()
                cp.wait_recv()

    res = pl.pallas_call(
        body, name="scatter_wait",
        in_specs=[HBM] * (2 * n) + [SEM, SEM, ANY],
        out_specs=[HBM] * (2 * n),
        out_shape=[pltpu.HBM(a.shape, a.dtype) for a in partials + lands],
        input_output_aliases={t: t for t in range(2 * n)},
        compiler_params=pltpu.CompilerParams(has_side_effects=EFFECT),
    )(*partials, *lands, send_sems, recv_sems, after)
    return list(res[:n]), list(res[n:])


def sum_chips(own, parts, where):
    _, r, cols = own.shape
    tr = _row_tile(r)

    def body(w_ref, a_ref, p_ref, o_ref):
        acc = a_ref[...].astype(F32)
        for k in range(3):
            acc = acc + p_ref[k].astype(F32)
        o_ref[...] = acc

    return pl.pallas_call(
        body, name="sum_chips",
        grid_spec=pltpu.PrefetchScalarGridSpec(
            num_scalar_prefetch=1, grid=(r // tr,),
            in_specs=[pl.BlockSpec((None, tr, cols), lambda i, w_ref: (w_ref[0], i, 0)),
                      pl.BlockSpec((3, tr, cols), lambda i, w_ref: (0, i, 0))],
            out_specs=pl.BlockSpec((None, tr, cols), lambda i, w_ref: (w_ref[1], i, 0))),
        out_shape=jax.ShapeDtypeStruct((DEPTH, r, cols), F32),
        compiler_params=_cparams("parallel"),
    )(where, own, parts)


def sibling_share_layer(bufs):
    n = len(bufs)

    def body(*refs):
        o_refs = refs[n:2 * n]
        send_sems, recv_sems = refs[2 * n:]
        x, y, c = _place()
        cps = []
        for t in range(n):
            cp = pltpu.make_async_remote_copy(src_ref=o_refs[t].at[c], dst_ref=o_refs[t].at[c], send_sem=send_sems.at[t],
                                              recv_sem=recv_sems.at[t], device_id=(x, y, 1 - c), device_id_type=MESH)
            cp.start()
            cps.append(cp)
        for t in range(n):
            slot = o_refs[t].at[1 - c]
            pltpu.make_async_remote_copy(src_ref=slot, dst_ref=slot, send_sem=send_sems.at[t], recv_sem=recv_sems.at[t],
                                         device_id=(x, y, 1 - c), device_id_type=MESH).wait_recv()
        for cp in cps:
            cp.wait_send()

    return pl.pallas_call(
        body, name="sibling_share_layer",
        in_specs=[ANY] * n, out_specs=[ANY] * n,
        out_shape=[jax.ShapeDtypeStruct(b.shape, b.dtype) for b in bufs],
        input_output_aliases={t: t for t in range(n)},
        scratch_shapes=[pltpu.SemaphoreType.DMA((n,)), pltpu.SemaphoreType.DMA((n,))],
    )(*bufs)


SP_META = 2 * (N_META * D_MODEL // LANES)
SP_NORM = DEPTH * D_MODEL // LANES
SP_RB = DEPTH * N_BUCKETS
SP_SINK = DEPTH * ATT_HEADS
SP_CONV = DEPTH * 3 * BRANCH_WIDTH // LANES
SP_LOSS = 8
SIDE_ROWS = 48
SP_ROWS = SP_META + 2 * SP_NORM + SP_RB + SP_SINK + SP_CONV + SP_LOSS


def sum_small(slots):
    half = SP_META // 2
    rb0 = SP_META + 2 * SP_NORM
    rest_rows = SP_ROWS - SP_META

    def body(s_ref, meta_ref, rest_ref):
        acc = s_ref[0]
        for d in range(1, 8):
            acc = acc + s_ref[d]
        meta_ref[...] = acc[0:half] + acc[half:SP_META]
        rest_ref[...] = acc[SP_META:]
        rest_ref[rb0 - SP_META:rb0 - SP_META + N_BUCKETS, :] = (
            acc[rb0:rb0 + N_BUCKETS] + acc[rb0 + N_BUCKETS:rb0 + 2 * N_BUCKETS])

    vm = pl.BlockSpec(memory_space=pltpu.VMEM)
    return pl.pallas_call(
        body, name="sum_small",
        in_specs=[vm], out_specs=[vm, vm],
        out_shape=[jax.ShapeDtypeStruct((half, LANES), F32), jax.ShapeDtypeStruct((rest_rows, LANES), F32)],
    )(slots)


def local_step(x, loss_target, meta_full, rel_bias, norm_pre, conv_w_full, attn_sinks, norm_post, weights_of, mid_fwd,
               grads_done, bwd_done):
    nb, seq, _ = x.shape
    nc = seq // BLOCK + 1
    lp = nc * BLOCK
    rows = nb * lp
    pad = jnp.zeros((nb, PAD_FRONT, D_MODEL), F32)
    meta = jnp.broadcast_to(meta_full[None], (nb, N_META, D_MODEL))
    h0 = jnp.concatenate([pad, meta, x], axis=1).reshape(rows, D_MODEL)
    target = jnp.pad(loss_target, ((0, 0), (BLOCK, 0), (0, 0))).reshape(rows, D_MODEL)
    cosf, sinf = _rot_tables(lp)
    bkt = jnp.asarray(_bucket_table())

    g_pre = norm_pre.reshape(DEPTH, 1, D_MODEL)
    g_post = norm_post.reshape(DEPTH, 1, D_MODEL)
    order = lambda token: bkt if token is None else token

    acts = []
    h = h0
    for l in range(DEPTH):
        w_in, token = weights_of(l, h)
        hb, p_abc = norm_matmul(h, g_pre, l, w_in, 0, N_ABC_TILES, order(token))
        p_m = matmul_cols(hb, w_in, N_ABC_TILES, N_M_TILES)
        br, states = mixers_fwd(p_abc, cosf, sinf, bkt, rel_bias, attn_sinks, conv_w_full, l, nb, nc)
        (w_br, w_out), token = mid_fwd(l, br)
        acts.append((h, hb, p_abc, p_m, br, states, w_in, w_br, w_out))
        if l < DEPTH - 1:
            h = merge_fwd(h, br, p_m, w_br, w_out, g_post, l, order(token))
        else:
            assert token is None
            loss_part, d_h = merge_fwd_loss(h, br, p_m, w_br, w_out, g_post, l, target, lp)

    small = [None] * DEPTH
    token = None
    for l in reversed(range(DEPTH)):
        h_in, hb, p_abc, p_m, br, states, w_in, w_br, w_out = acts[l]
        d_br, d_m, d_gpost, g_wbr, g_wout = merge_bwd(d_h, br, p_m, w_br, w_out, g_post, l, order(token))
        d_abc, d_rb, d_sk, d_cw = mixers_bwd(p_abc, d_br, states, cosf, sinf, bkt, rel_bias,
                                             attn_sinks, conv_w_full, l, nb, nc)
        g_win = proj_wgrad(hb, d_abc, d_m)
        token = grads_done(l, [g_win, g_wbr, g_wout])
        d_h, d_gpre = proj_dgrad(d_abc, d_m, w_in, h_in, g_pre, l, d_h, order(token))
        token = bwd_done(l, d_h)
        small[l] = (d_gpre[0], d_gpost[0], d_rb, d_sk, d_cw[0:3])

    d_h3 = d_h.reshape(nb, lp, D_MODEL)
    d_x = d_h3[:, BLOCK:]
    d_meta = d_h3[:, PAD_FRONT:BLOCK]
    sp = jnp.concatenate([
        d_meta.reshape(-1, LANES),
        jnp.stack([small[l][0] for l in range(DEPTH)]).reshape(-1, LANES),
        jnp.stack([small[l][1] for l in range(DEPTH)]).reshape(-1, LANES),
        jnp.concatenate([small[l][2] for l in range(DEPTH)], axis=0),
        jnp.concatenate([small[l][3] for l in range(DEPTH)], axis=0),
        jnp.stack([small[l][4] for l in range(DEPTH)]).reshape(-1, LANES),
        loss_part], axis=0)
    return d_x, sp


def kernel(x, meta_tokens, rel_bias, norm_pre, w_in, conv_w, attn_sinks, w_branch, w_out, norm_post, loss_target, m_meta_tokens, m_rel_bias, m_norm_pre, m_w_in, m_conv_w, m_attn_sinks, m_w_branch, m_w_out, m_norm_post, v_meta_tokens, v_rel_bias, v_norm_pre, v_w_in, v_conv_w, v_attn_sinks, v_w_branch, v_w_out, v_norm_post):
    assert x.shape[0] == 2 and SP_META == 2 * N_META * D_MODEL // LANES
    px, py, pc = _place()
    chip = 2 * px + py

    c_arr = jnp.reshape(pc, (1,)).astype(jnp.int32)
    where = jnp.stack([chip, pc]).astype(jnp.int32)
    tr_ = lambda a: jnp.swapaxes(a, 1, 2)
    w3 = [tr_(w_in), w_branch.reshape(DEPTH, N_BRANCH * BRANCH_WIDTH, SHARD_D), w_out]
    halves = lambda a: a.reshape(2, a.shape[0] // 2, a.shape[1])

    def as_weights(bufs):
        a_in, a_br, a_out = bufs
        return (a_in.reshape(PROJ_WIDTH, D_MODEL), a_br.reshape(N_CHIPS, N_BRANCH, BRANCH_WIDTH, SHARD_D),
                a_out.reshape(D_MODEL, D_MODEL))

    n_meta_rows = N_META * SHARD_D // LANES
    side = jnp.concatenate([meta_tokens.reshape(-1), conv_w.reshape(-1)]).reshape(-1, LANES)
    side = jnp.concatenate([side, jnp.zeros((SIDE_ROWS - side.shape[0], LANES), F32)], axis=0)
    slots = [[_own_slot(halves(w[l].astype(BF16)), chip) for w in w3] for l in range(DEPTH)]
    send0, recv0, flying0, _ = gather_start([_own_slot(halves(side), chip)] + slots[0], where)
    side_chips = gather_forward(gather_wait(flying0[:1], send0, recv0, where))[0].reshape(N_CHIPS, SIDE_ROWS, LANES)
    meta_full = jnp.moveaxis(side_chips[:, :n_meta_rows].reshape(N_CHIPS, N_META, SHARD_D), 0, 1).reshape(N_META, D_MODEL)
    conv_full = jnp.moveaxis(side_chips[:, n_meta_rows:n_meta_rows + 6].reshape(N_CHIPS, DEPTH, 3, LANES), 0, 2).reshape(DEPTH, 3, BRANCH_WIDTH)
    inbound = {}

    def weights_of(l, h):
        if l == 0:
            inbound[0] = gather_forward(gather_wait(flying0[1:2], send0, recv0, h, first=1))
            inbound[1] = gather_start(slots[1], inbound[0][0])
            return inbound[0][0].reshape(PROJ_WIDTH, D_MODEL), inbound[1][3]
        send, recv, thru = inbound[1]
        inbound[1] = as_weights(forward_wait(thru, send, recv, h))
        return inbound[1][0], None

    def mid_fwd(l, br):
        if l == 0:
            rest = gather_forward(gather_wait(flying0[2:], send0, recv0, br, first=2))
            send, recv, flying1, _ = inbound[1]
            send, recv, thru, started = forward_start(gather_wait(flying1, send, recv, rest[0]))
            inbound[1] = (send, recv, thru)
            return as_weights(inbound[0] + rest)[1:], started
        return inbound[1][1:], None

    reduced = [None] * DEPTH
    flying = {}

    def finish_reduce(l, after):
        partials, parts = scatter_wait(*flying[l], after)
        reduced[l] = sibling_share_layer([sum_chips(a, p, where) for a, p in zip(partials, parts)])

    def start_scatter(l, full, others):
        send, recv, thru, lands, started = scatter_start([add_own_half(g, o, c_arr) for g, o in zip(full, others)])
        flying[l] = (thru, lands, send, recv)
        return started

    m3 = [tr_(m_w_in), m_w_branch.reshape(w3[1].shape), m_w_out]
    v3 = [tr_(v_w_in), v_w_branch.reshape(w3[1].shape), v_w_out]
    big = [None] * 3

    def adamw_of(l, after):
        for t in range(3):
            big[t] = adamw_layer(w3[t], reduced[l][t].reshape(w3[t].shape[1:]), m3[t], v3[t], l, big[t], after)
            after = big[t][1]

    def grads_done(l, grads):
        full = [g.reshape(N_CHIPS, 2, g.size // (2 * N_CHIPS * g.shape[-1]), g.shape[-1]) for g in grads]
        if l == 0:
            finish_reduce(1, grads[0])
        send, recv, thru, lands, started = swap_start(full, where if l == 1 else reduced[1][0])
        if l == 1:
            flying["swap"] = (thru, lands, send, recv)
            return started
        adamw_of(1, started)
        return start_scatter(0, *swap_wait(thru, lands, send, recv, big[2][1]))

    def bwd_done(l, d_h):
        if l == 1:
            return start_scatter(1, *swap_wait(*flying["swap"], d_h))
        return None

    d_x, sp = local_step(x, loss_target, meta_full, rel_bias, norm_pre, conv_full, attn_sinks, norm_post,
                         weights_of, mid_fwd, grads_done, bwd_done)
    finish_reduce(0, sp)

    s_send, s_recv, s_buf, s_started = small_start(sp, 4 * px + 2 * py + pc, reduced[0][0])

    adamw_of(0, s_started)
    g_in, *u_in = [tr_(a) for a in big[0]]
    g_br, *u_br = [a.reshape(w_branch.shape) for a in big[1]]
    g_out, *u_out = big[2]

    meta_rows, rest = sum_small(small_wait(s_buf, s_send, s_recv, big[2][1]))
    o = 0
    g_meta_full = meta_rows.reshape(N_META, D_MODEL)
    g_norm_pre = rest[o:o + SP_NORM].reshape(DEPTH, D_MODEL); o += SP_NORM
    g_norm_post = rest[o:o + SP_NORM].reshape(DEPTH, D_MODEL); o += SP_NORM
    g_rel_bias = rest[o:o + N_BUCKETS, :ATT_HEADS]; o += SP_RB
    g_sinks = rest[o:o + SP_SINK, 0].reshape(DEPTH, ATT_HEADS); o += SP_SINK
    g_conv_full = rest[o:o + SP_CONV].reshape(DEPTH, 3, BRANCH_WIDTH); o += SP_CONV
    loss = rest[o, 0]
    g_meta = lax.dynamic_slice_in_dim(g_meta_full, chip * SHARD_D, SHARD_D, axis=1)
    g_conv = lax.dynamic_slice_in_dim(g_conv_full, chip * LANES, LANES, axis=2)

    to2 = lambda a: a.reshape(-1, a.shape[-1])
    smalls = [(meta_tokens, g_meta, m_meta_tokens, v_meta_tokens),
              (rel_bias, g_rel_bias, m_rel_bias, v_rel_bias),
              (norm_pre, g_norm_pre, m_norm_pre, v_norm_pre),
              (to2(conv_w), to2(g_conv), to2(m_conv_w), to2(v_conv_w)),
              (attn_sinks, g_sinks, m_attn_sinks, v_attn_sinks),
              (norm_post, g_norm_post, m_norm_post, v_norm_post)]
    u_meta, u_rb, u_npre, u_conv, u_sink, u_npost = adamw_small(smalls)
    u_conv = tuple(a.reshape(conv_w.shape) for a in u_conv)

    grads = [g_meta, g_rel_bias, g_norm_pre, g_in, g_conv, g_sinks, g_br, g_out, g_norm_post]
    upd = [u_meta, u_rb, u_npre, u_in, u_conv, u_sink, u_br, u_out, u_npost]
    return (loss, d_x, *grads, *[u[0] for u in upd], *[u[1] for u in upd], *[u[2] for u in upd])
```

```python
import math

import numpy as np
import jax
import jax.numpy as jnp
from jax import lax
from jax.experimental import pallas as pl
from jax.experimental.pallas import tpu as pltpu

F32 = jnp.float32
BF16 = jnp.bfloat16
MESH = pl.DeviceIdType.MESH

D_MODEL = 1024
DEPTH = 2
N_META = 16
BLOCK = 128
PAD_FRONT = BLOCK - N_META
ATT_HEADS = 8
ATT_HEAD_DIM = 64
N_BUCKETS = 32
MAX_EXACT = 16
MAX_DISTANCE = 128
RET_HEADS = 4
ROT_BASE = 10000.0
N_BRANCH = 3
BRANCH_WIDTH = 512
PROJ_WIDTH = 8448
ABC_WIDTH = 5376
MERGE_WIDTH = N_BRANCH * D_MODEL
RMS_EPS = 1e-6
GN_EPS = 1e-6
NEG_INF = -1e30
ATT_SCALE = ATT_HEAD_DIM ** -0.5
RET_SCALE = BLOCK ** -0.5
LOG_GAMMA = tuple(math.log1p(-(2.0 ** (-5.0 - h))) for h in range(RET_HEADS))

C_AQ, C_AK, C_AV, C_AG = 0, 512, 640, 768
C_RQ, C_RK, C_RV, C_RG = 1280, 1792, 2304, 2816
C_CB, C_CC, C_CX, C_CG = 3328, 3840, 4352, 4864

ADAM_LR = 0.001
ADAM_B1 = 0.9
ADAM_B2 = 0.999
ADAM_EPS = 1e-08
ADAM_WD = 0.01
ADAM_STEP = 10

N_CHIPS = 4
SHARD_D = D_MODEL // N_CHIPS
LANES = 128

VMEM_LIMIT = 56 * 1024 * 1024
COL_TILE = 768
ROW_TILE = 1088
PROJ_ROW_TILE = 2176


def _cparams(*sem):
    return pltpu.CompilerParams(dimension_semantics=sem, vmem_limit_bytes=VMEM_LIMIT)


def _nt(a, b):
    return lax.dot_general(a, b, (((1,), (1,)), ((), ())), preferred_element_type=F32)


def _tn(a, b):
    return lax.dot_general(a, b, (((0,), (0,)), ((), ())), preferred_element_type=F32)


def _nn(a, b):
    return jnp.dot(a, b, preferred_element_type=F32)


def _sigmoid(x):
    return 0.5 * jnp.tanh(0.5 * x) + 0.5


def _silu(x):
    return x * _sigmoid(x)


def _dsilu(x):
    s = _sigmoid(x)
    return s * (1.0 + x * (1.0 - s))


def _bucket_table():
    r = np.arange(BLOCK)[:, None]
    c = np.arange(2 * BLOCK)[None, :]
    n = np.maximum(BLOCK + r - c, 0)
    nf = np.maximum(n, 1).astype(np.float32)
    large = MAX_EXACT + (np.log(nf / MAX_EXACT) / math.log(MAX_DISTANCE / MAX_EXACT)
                         * (N_BUCKETS - MAX_EXACT)).astype(np.int32)
    large = np.minimum(large, N_BUCKETS - 1)
    return np.where(n < MAX_EXACT, n, large).astype(np.int32)


def _rot_tables(lp):
    half = BLOCK // 2
    pos = (jnp.arange(lp) - PAD_FRONT).astype(F32)
    theta = 1.0 / (ROT_BASE ** jnp.linspace(0.0, 1.0, half, dtype=F32))
    ang = pos[:, None] * theta[None, :]
    cos, sin = jnp.cos(ang), jnp.sin(ang)
    return jnp.concatenate([cos, cos], axis=1), jnp.concatenate([-sin, sin], axis=1)


def norm_matmul(x2d, g, layer, w, col0_blocks, n_col_blocks, after):
    t = x2d.shape[0]
    tm = PROJ_ROW_TILE if t % PROJ_ROW_TILE == 0 else BLOCK

    def body(x_ref, g_ref, w_ref, after_ref, hb_ref, o_ref):
        @pl.when(pl.program_id(1) == 0)
        def _():
            x = x_ref[...]
            r = lax.rsqrt(jnp.mean(x * x, axis=-1, keepdims=True) + RMS_EPS)
            hb_ref[...] = (x * r * g_ref[...]).astype(BF16)

        o_ref[...] = _nt(hb_ref[...], w_ref[...]).astype(BF16)

    return pl.pallas_call(
        body, name="norm_matmul",
        grid=(t // tm, n_col_blocks),
        in_specs=[pl.BlockSpec((tm, D_MODEL), lambda i, j: (i, 0)),
                  pl.BlockSpec((None, 1, D_MODEL), lambda i, j: (layer, 0, 0)),
                  pl.BlockSpec((COL_TILE, D_MODEL), lambda i, j: (j + col0_blocks, 0)),
                  ANY],
        out_specs=[pl.BlockSpec((tm, D_MODEL), lambda i, j: (i, 0)),
                   pl.BlockSpec((tm, COL_TILE), lambda i, j: (i, j))],
        out_shape=[jax.ShapeDtypeStruct((t, D_MODEL), BF16),
                   jax.ShapeDtypeStruct((t, n_col_blocks * COL_TILE), BF16)],
        compiler_params=_cparams("parallel", "arbitrary"),
    )(x2d, g, w, after)


def matmul_cols(a, w, col0_blocks, n_col_blocks):
    t, k = a.shape
    tm = PROJ_ROW_TILE if t % PROJ_ROW_TILE == 0 else BLOCK

    def body(a_ref, w_ref, o_ref):
        o_ref[...] = _nt(a_ref[...], w_ref[...]).astype(BF16)

    return pl.pallas_call(
        body, name="matmul_cols",
        grid=(t // tm, n_col_blocks),
        in_specs=[pl.BlockSpec((tm, k), lambda i, j: (i, 0)),
                  pl.BlockSpec((COL_TILE, k), lambda i, j: (j + col0_blocks, 0))],
        out_specs=pl.BlockSpec((tm, COL_TILE), lambda i, j: (i, j)),
        out_shape=jax.ShapeDtypeStruct((t, n_col_blocks * COL_TILE), BF16),
        compiler_params=_cparams("parallel", "arbitrary"),
    )(a, w)


class _Widened:
    def __init__(self, ref):
        self.ref = ref

    def __getitem__(self, idx):
        return self.ref[idx].astype(F32)


def _build_bias(bkt_ref, rb_ref, bias_s):
    bkt = bkt_ref[...]
    for h in range(ATT_HEADS):
        acc = jnp.zeros((BLOCK, 2 * BLOCK), F32)
        for b in range(N_BUCKETS):
            acc = jnp.where(bkt == b, rb_ref[b, h], acc)
        bias_s[h] = acc


def _band_mask(n):
    r = lax.broadcasted_iota(jnp.int32, (BLOCK, 2 * BLOCK), 0)
    c = lax.broadcasted_iota(jnp.int32, (BLOCK, 2 * BLOCK), 1)
    key_pos = (n - 1) * BLOCK + c
    return (c > r) & (c <= r + BLOCK) & (key_pos >= PAD_FRONT)


def _split_heads(kv, kh):
    lane = lax.broadcasted_iota(jnp.int32, kv.shape, 1)
    if kh == 0:
        lo = jnp.where(lane < ATT_HEAD_DIM, kv, 0.0)
        hi = pltpu.roll(lo, ATT_HEAD_DIM, 1)
    else:
        hi = jnp.where(lane >= ATT_HEAD_DIM, kv, 0.0)
        lo = pltpu.roll(hi, ATT_HEAD_DIM, 1)
    return lo, hi


def _merge_heads(acc_lo, acc_hi, kh):
    lane = lax.broadcasted_iota(jnp.int32, acc_lo.shape, 1)
    if kh == 0:
        return jnp.where(lane < ATT_HEAD_DIM, acc_lo + pltpu.roll(acc_hi, ATT_HEAD_DIM, 1), 0.0)
    return jnp.where(lane >= ATT_HEAD_DIM, acc_hi + pltpu.roll(acc_lo, ATT_HEAD_DIM, 1), 0.0)


def _softmax_of(qk, bias_h, mask, sink_h):
    s = qk + bias_h
    s = jnp.where(mask, s, NEG_INF)
    m = jnp.maximum(jnp.max(s, axis=-1, keepdims=True), sink_h)
    p = jnp.exp(s - m)
    es = jnp.exp(sink_h - m)
    inv = 1.0 / (jnp.sum(p, axis=-1, keepdims=True) + es)
    return p * inv, es * inv


def _rot(t, cosf, sinf):
    return t * cosf + pltpu.roll(t, BLOCK // 2, 1) * sinf


def _rot_t(d, cosf, sinf):
    return d * cosf + pltpu.roll(d * sinf, BLOCK // 2, 1)


def _decay_tables(h):
    lg = LOG_GAMMA[h]
    i = lax.broadcasted_iota(jnp.int32, (BLOCK, BLOCK), 0)
    j = lax.broadcasted_iota(jnp.int32, (BLOCK, BLOCK), 1)
    diff = (i - j).astype(F32)
    dm = jnp.where(diff >= 0, jnp.exp(diff * lg), 0.0)
    row = lax.broadcasted_iota(jnp.int32, (BLOCK, 1), 0).astype(F32)
    zeta = jnp.exp((BLOCK - 1 - row) * lg)
    xi = jnp.exp((row + 1.0) * lg)
    return dm, zeta, xi, math.exp(BLOCK * lg)


def _valid_col(n):
    row = lax.broadcasted_iota(jnp.int32, (BLOCK, 1), 0)
    return ((n * BLOCK + row) >= PAD_FRONT).astype(F32)


def _shift_down(cur, prev, k):
    row = lax.broadcasted_iota(jnp.int32, cur.shape, 0)
    return jnp.where(row >= k, pltpu.roll(cur, k, 0), pltpu.roll(prev, k, 0))


def _shift_up(cur, nxt, k):
    row = lax.broadcasted_iota(jnp.int32, cur.shape, 0)
    return jnp.where(row < BLOCK - k, pltpu.roll(cur, BLOCK - k, 0), pltpu.roll(nxt, BLOCK - k, 0))


def mixers_fwd(proj, cosf, sinf, bkt, rel_bias, sinks, conv_w, layer, nb, nc):
    def body(p_ref, cos_ref, sin_ref, bkt_ref, rb_ref, sk_ref, cw_ref, br_ref, st_ref,
             bias_s, kv_s, state_s, u_s):
        p_ref = _Widened(p_ref)
        n = pl.program_id(0)

        @pl.when(n == 0)
        def _():
            _build_bias(bkt_ref, rb_ref, bias_s)
            kv_s[:, 0:BLOCK, :] = jnp.zeros((nb, BLOCK, 2 * BLOCK), F32)
            state_s[...] = jnp.zeros_like(state_s)
            u_s[...] = jnp.zeros_like(u_s)

        valid = _valid_col(n)
        mask = _band_mask(n)
        ex = range(nb)

        for b in ex:
            kv_s[b, BLOCK:2 * BLOCK, :] = p_ref[b, :, C_AK:C_AK + 2 * BLOCK]
        for kh in range(2):
            ks = [[t.astype(BF16) for t in _split_heads(kv_s[b, :, 0:BLOCK], kh)] for b in ex]
            vs = [[t.astype(BF16) for t in _split_heads(kv_s[b, :, BLOCK:2 * BLOCK], kh)] for b in ex]
            pairs = [(b, 2 * kh + jj) for jj in range(2) for b in ex]
            subs = [(b, j, x) for (b, j) in pairs for x in range(2)]
            qb_ = {(b, j): (p_ref[b, :, C_AQ + BLOCK * j:C_AQ + BLOCK * (j + 1)] * ATT_SCALE).astype(BF16)
                   for (b, j) in pairs}
            qk_ = {(b, j, x): _nt(qb_[(b, j)], ks[b][x]) for (b, j, x) in subs}
            pb_ = {}
            for u in subs:
                h = 2 * u[1] + u[2]
                pb_[u] = _softmax_of(qk_[u], bias_s[h], mask, sk_ref[layer, h])[0].astype(BF16)
            o_ = {u: _nn(pb_[u], vs[u[0]][u[2]]) for u in subs}
            for (b, j) in pairs:
                gate = p_ref[b, :, C_AG + BLOCK * j:C_AG + BLOCK * (j + 1)]
                br_ref[b, :, BLOCK * j:BLOCK * (j + 1)] = ((o_[(b, j, 0)] + o_[(b, j, 1)]) * _silu(gate)).astype(BF16)
        for b in ex:
            kv_s[b, 0:BLOCK, :] = kv_s[b, BLOCK:2 * BLOCK, :]

        cosv = cos_ref[...]
        sinv = sin_ref[...]
        tabs = [_decay_tables(h) for h in range(RET_HEADS)]
        units = [(b, h) for h in range(RET_HEADS) for b in ex]
        sl = lambda c0, h: slice(c0 + BLOCK * h, c0 + BLOCK * (h + 1))
        q_, k_, v_, sp_ = {}, {}, {}, {}
        for u in units:
            b, h = u
            q_[u] = _rot(p_ref[b, :, sl(C_RQ, h)], cosv, sinv).astype(BF16)
            k_[u] = (_rot(p_ref[b, :, sl(C_RK, h)], cosv, sinv) * RET_SCALE * valid).astype(BF16)
            v_[u] = p_ref[b, :, sl(C_RV, h)]
            sp_[u] = state_s[b, h]
            st_ref[b, 0, h] = sp_[u]
        qk_ = {u: _nt(q_[u], k_[u]) for u in units}
        qs_ = {u: _nn(q_[u], sp_[u].astype(BF16)) for u in units}
        kv_ = {u: _tn(k_[u], (v_[u] * tabs[u[1]][1]).astype(BF16)) for u in units}
        a_ = {u: (qk_[u] * tabs[u[1]][0]).astype(BF16) for u in units}
        av_ = {u: _nn(a_[u], v_[u].astype(BF16)) for u in units}
        for u in units:
            b, h = u
            o = av_[u] + tabs[h][2] * qs_[u]
            mu = jnp.mean(o, axis=-1, keepdims=True)
            var = jnp.mean(jnp.square(o - mu), axis=-1, keepdims=True)
            oh = (o - mu) * lax.rsqrt(var + GN_EPS)
            gate = p_ref[b, :, sl(C_RG, h)]
            br_ref[b, :, BRANCH_WIDTH + BLOCK * h:BRANCH_WIDTH + BLOCK * (h + 1)] = (oh * _silu(gate)).astype(BF16)
            state_s[b, h] = tabs[h][3] * sp_[u] + kv_[u]

        for b in ex:
            u = p_ref[b, :, C_CC:C_CC + BRANCH_WIDTH] * p_ref[b, :, C_CX:C_CX + BRANCH_WIDTH] * valid
            u_prev = u_s[b]
            y = (cw_ref[0:1, :] * _shift_down(u, u_prev, 2) + cw_ref[1:2, :] * _shift_down(u, u_prev, 1)
                 + cw_ref[2:3, :] * u)
            yc = p_ref[b, :, C_CB:C_CB + BRANCH_WIDTH] * y * _silu(p_ref[b, :, C_CG:C_CG + BRANCH_WIDTH])
            br_ref[b, :, 2 * BRANCH_WIDTH:3 * BRANCH_WIDTH] = yc.astype(BF16)
            u_s[b] = u

    lp = nc * BLOCK
    smem = pl.BlockSpec(memory_space=pltpu.SMEM)
    br, states = pl.pallas_call(
        body, name="mixers_fwd",
        grid=(nc,),
        in_specs=[pl.BlockSpec((nb, BLOCK, ABC_WIDTH), lambda n: (0, n, 0)),
                  pl.BlockSpec((BLOCK, BLOCK), lambda n: (n, 0)),
                  pl.BlockSpec((BLOCK, BLOCK), lambda n: (n, 0)),
                  pl.BlockSpec((BLOCK, 2 * BLOCK), lambda n: (0, 0)),
                  smem, smem,
                  pl.BlockSpec((None, 3, BRANCH_WIDTH), lambda n: (layer, 0, 0))],
        out_specs=[pl.BlockSpec((nb, BLOCK, N_BRANCH * BRANCH_WIDTH), lambda n: (0, n, 0)),
                   pl.BlockSpec((nb, 1, RET_HEADS, BLOCK, BLOCK), lambda n: (0, n, 0, 0, 0))],
        out_shape=[jax.ShapeDtypeStruct((nb, lp, N_BRANCH * BRANCH_WIDTH), BF16),
                   jax.ShapeDtypeStruct((nb, nc, RET_HEADS, BLOCK, BLOCK), F32)],
        scratch_shapes=[pltpu.VMEM((ATT_HEADS, BLOCK, 2 * BLOCK), F32),
                        pltpu.VMEM((nb, 2 * BLOCK, 2 * BLOCK), F32),
                        pltpu.VMEM((nb, RET_HEADS, BLOCK, BLOCK), F32),
                        pltpu.VMEM((nb, BLOCK, BRANCH_WIDTH), F32)],
        compiler_params=_cparams("arbitrary"),
    )(proj.reshape(nb, lp, ABC_WIDTH), cosf, sinf, bkt, rel_bias, sinks, conv_w)
    return br.reshape(nb * lp, N_BRANCH * BRANCH_WIDTH), states


def mixers_bwd(proj, d_br, states, cosf, sinf, bkt, rel_bias, sinks, conv_w, layer, nb, nc):
    def body(p_ref, kvp_ref, cp_ref, dbr_ref, st_ref, cos_ref, sin_ref, bkt_ref, rb_ref, sk_ref, cw_ref,
             dp_ref, drb_ref, dsk_ref, dcw_ref,
             bias_s, dbias_s, dkv_s, g_s, dy_s):
        p_ref, kvp_ref, cp_ref, dbr_ref = [_Widened(r) for r in (p_ref, kvp_ref, cp_ref, dbr_ref)]
        step = pl.program_id(0)
        n = nc - 1 - step
        ex = range(nb)

        @pl.when(step == 0)
        def _():
            _build_bias(bkt_ref, rb_ref, bias_s)
            dbias_s[...] = jnp.zeros_like(dbias_s)
            dsk_ref[...] = jnp.zeros_like(dsk_ref)
            dcw_ref[...] = jnp.zeros_like(dcw_ref)
            drb_ref[...] = jnp.zeros_like(drb_ref)
            dkv_s[...] = jnp.zeros_like(dkv_s)
            g_s[...] = jnp.zeros_like(g_s)
            dy_s[...] = jnp.zeros_like(dy_s)

        valid = _valid_col(n)
        mask = _band_mask(n)
        has_prev = (n > 0).astype(F32)

        k_all, v_all = [], []
        for b in ex:
            kv_prev = kvp_ref[b] * has_prev
            kv_cur = p_ref[b, :, C_AK:C_AK + 2 * BLOCK]
            k_all.append(jnp.concatenate([kv_prev[:, 0:BLOCK], kv_cur[:, 0:BLOCK]], axis=0))
            v_all.append(jnp.concatenate([kv_prev[:, BLOCK:], kv_cur[:, BLOCK:]], axis=0))
        zero2 = jnp.zeros((2 * BLOCK, BLOCK), F32)
        dk_tot = [zero2 for _ in ex]
        dv_tot = [zero2 for _ in ex]
        for kh in range(2):
            ks = [[t.astype(BF16) for t in _split_heads(k_all[b], kh)] for b in ex]
            vs = [[t.astype(BF16) for t in _split_heads(v_all[b], kh)] for b in ex]
            pairs = [(b, 2 * kh + jj) for jj in range(2) for b in ex]
            subs = [(b, j, x) for (b, j) in pairs for x in range(2)]
            qb_, gate_, dya_, do2_ = {}, {}, {}, {}
            for w in pairs:
                b, j = w
                qb_[w] = (p_ref[b, :, C_AQ + BLOCK * j:C_AQ + BLOCK * (j + 1)] * ATT_SCALE).astype(BF16)
                gate_[w] = p_ref[b, :, C_AG + BLOCK * j:C_AG + BLOCK * (j + 1)]
                dya_[w] = dbr_ref[b, :, BLOCK * j:BLOCK * (j + 1)]
                do2_[w] = (dya_[w] * _silu(gate_[w])).astype(BF16)
            qk_ = {(b, j, x): _nt(qb_[(b, j)], ks[b][x]) for (b, j, x) in subs}
            dpm_ = {(b, j, x): _nt(do2_[(b, j)], vs[b][x]) for (b, j, x) in subs}
            pb_, dsb_ = {}, {}
            for u in subs:
                b, j, x = u
                h = 2 * j + x
                p, p_sink = _softmax_of(qk_[u], bias_s[h], mask, sk_ref[layer, h])
                pb_[u] = p.astype(BF16)
                delta = jnp.sum(p * dpm_[u], axis=-1, keepdims=True)
                ds = p * (dpm_[u] - delta)
                dbias_s[h] += ds
                dsk_ref[h:h + 1, :] += jnp.broadcast_to(
                    jnp.sum(-p_sink * delta, axis=0, keepdims=True), (1, BLOCK))
                dsb_[u] = ds.astype(BF16)
            o_ = {u: _nn(pb_[u], vs[u[0]][u[2]]) for u in subs}
            dq_ = {u: _nn(dsb_[u], ks[u[0]][u[2]]) for u in subs}
            dkm_ = {u: _tn(dsb_[u], qb_[(u[0], u[1])]) for u in subs}
            dvm_ = {u: _tn(pb_[u], do2_[(u[0], u[1])]) for u in subs}
            for w in pairs:
                b, j = w
                o2 = o_[(b, j, 0)] + o_[(b, j, 1)]
                dq2 = (dq_[(b, j, 0)] + dq_[(b, j, 1)]) * ATT_SCALE
                dp_ref[b, :, C_AQ + BLOCK * j:C_AQ + BLOCK * (j + 1)] = dq2.astype(BF16)
                dp_ref[b, :, C_AG + BLOCK * j:C_AG + BLOCK * (j + 1)] = (
                    dya_[w] * o2 * _dsilu(gate_[w])).astype(BF16)
            for b in ex:
                j0, j1 = 2 * kh, 2 * kh + 1
                dk_tot[b] = dk_tot[b] + _merge_heads(dkm_[(b, j0, 0)] + dkm_[(b, j1, 0)],
                                                     dkm_[(b, j0, 1)] + dkm_[(b, j1, 1)], kh)
                dv_tot[b] = dv_tot[b] + _merge_heads(dvm_[(b, j0, 0)] + dvm_[(b, j1, 0)],
                                                     dvm_[(b, j0, 1)] + dvm_[(b, j1, 1)], kh)
        for b in ex:
            dp_ref[b, :, C_AK:C_AK + BLOCK] = (dk_tot[b][BLOCK:, :] + dkv_s[b, :, 0:BLOCK]).astype(BF16)
            dp_ref[b, :, C_AV:C_AV + BLOCK] = (dv_tot[b][BLOCK:, :] + dkv_s[b, :, BLOCK:]).astype(BF16)
            dkv_s[b, :, 0:BLOCK] = dk_tot[b][0:BLOCK, :]
            dkv_s[b, :, BLOCK:] = dv_tot[b][0:BLOCK, :]

        cosv = cos_ref[...]
        sinv = sin_ref[...]
        tabs = [_decay_tables(h) for h in range(RET_HEADS)]
        units = [(b, h) for h in range(RET_HEADS) for b in ex]
        sl = lambda c0, h: slice(c0 + BLOCK * h, c0 + BLOCK * (h + 1))
        q_, k_, v_, vb_, sp_ = {}, {}, {}, {}, {}
        for u in units:
            b, h = u
            q_[u] = _rot(p_ref[b, :, sl(C_RQ, h)], cosv, sinv).astype(BF16)
            k_[u] = (_rot(p_ref[b, :, sl(C_RK, h)], cosv, sinv) * RET_SCALE * valid).astype(BF16)
            v_[u] = p_ref[b, :, sl(C_RV, h)]
            vb_[u] = v_[u].astype(BF16)
            sp_[u] = st_ref[b, 0, h].astype(BF16)
        qk_ = {u: _nt(q_[u], k_[u]) for u in units}
        qs_ = {u: _nn(q_[u], sp_[u]) for u in units}
        a_ = {u: (qk_[u] * tabs[u[1]][0]).astype(BF16) for u in units}
        av_ = {u: _nn(a_[u], vb_[u]) for u in units}
        dob_, dxo_ = {}, {}
        for u in units:
            b, h = u
            xi = tabs[h][2]
            o = av_[u] + xi * qs_[u]
            mu = jnp.mean(o, axis=-1, keepdims=True)
            var = jnp.mean(jnp.square(o - mu), axis=-1, keepdims=True)
            rstd = lax.rsqrt(var + GN_EPS)
            oh = (o - mu) * rstd
            gate = p_ref[b, :, sl(C_RG, h)]
            d_yr = dbr_ref[b, :, BRANCH_WIDTH + BLOCK * h:BRANCH_WIDTH + BLOCK * (h + 1)]
            dp_ref[b, :, sl(C_RG, h)] = (d_yr * oh * _dsilu(gate)).astype(BF16)
            doh = d_yr * _silu(gate)
            do = rstd * (doh - jnp.mean(doh, axis=-1, keepdims=True)
                         - oh * jnp.mean(doh * oh, axis=-1, keepdims=True))
            dob_[u] = do.astype(BF16)
            dxo_[u] = (do * xi).astype(BF16)
        dov_ = {u: _nt(dob_[u], vb_[u]) for u in units}
        dv1_ = {u: _tn(a_[u], dob_[u]) for u in units}
        dq1_ = {u: _nt(dxo_[u], sp_[u]) for u in units}
        gq_ = {u: _tn(q_[u], dxo_[u]) for u in units}
        da_, gb_, zv_ = {}, {}, {}
        for u in units:
            b, h = u
            da_[u] = (dov_[u] * tabs[h][0]).astype(BF16)
            g_next = g_s[b, h]
            gb_[u] = g_next.astype(BF16)
            zv_[u] = (v_[u] * tabs[h][1]).astype(BF16)
            g_s[b, h] = tabs[h][3] * g_next + gq_[u]
        dq2_ = {u: _nn(da_[u], k_[u]) for u in units}
        dk1_ = {u: _tn(da_[u], q_[u]) for u in units}
        dk2_ = {u: _nt(zv_[u], gb_[u]) for u in units}
        dv2_ = {u: _nn(k_[u], gb_[u]) for u in units}
        for u in units:
            b, h = u
            dp_ref[b, :, sl(C_RQ, h)] = _rot_t(dq2_[u] + dq1_[u], cosv, sinv).astype(BF16)
            dp_ref[b, :, sl(C_RK, h)] = _rot_t((dk1_[u] + dk2_[u]) * (RET_SCALE * valid), cosv, sinv).astype(BF16)
            dp_ref[b, :, sl(C_RV, h)] = (dv1_[u] + tabs[h][1] * dv2_[u]).astype(BF16)

        w0, w1, w2 = cw_ref[0:1, :], cw_ref[1:2, :], cw_ref[2:3, :]
        for b in ex:
            cb = p_ref[b, :, C_CB:C_CB + BRANCH_WIDTH]
            cc = p_ref[b, :, C_CC:C_CC + BRANCH_WIDTH]
            cx = p_ref[b, :, C_CX:C_CX + BRANCH_WIDTH]
            cg = p_ref[b, :, C_CG:C_CG + BRANCH_WIDTH]
            u = cc * cx * valid
            u_prev = (cp_ref[b, :, 0:BRANCH_WIDTH] * cp_ref[b, :, BRANCH_WIDTH:2 * BRANCH_WIDTH]
                      * (_valid_col(n - 1) * has_prev))
            u1 = _shift_down(u, u_prev, 1)
            u2 = _shift_down(u, u_prev, 2)
            y = w0 * u2 + w1 * u1 + w2 * u
            d_yc = dbr_ref[b, :, 2 * BRANCH_WIDTH:3 * BRANCH_WIDTH]
            sg = _silu(cg)
            dp_ref[b, :, C_CB:C_CB + BRANCH_WIDTH] = (d_yc * y * sg).astype(BF16)
            dp_ref[b, :, C_CG:C_CG + BRANCH_WIDTH] = (d_yc * cb * y * _dsilu(cg)).astype(BF16)
            dy = d_yc * cb * sg
            dy_next = dy_s[b]
            du = (w2 * dy + w1 * _shift_up(dy, dy_next, 1) + w0 * _shift_up(dy, dy_next, 2)) * valid
            dp_ref[b, :, C_CC:C_CC + BRANCH_WIDTH] = (du * cx).astype(BF16)
            dp_ref[b, :, C_CX:C_CX + BRANCH_WIDTH] = (du * cc).astype(BF16)
            dcw_ref[0:1, :] += jnp.sum(dy * u2, axis=0, keepdims=True)
            dcw_ref[1:2, :] += jnp.sum(dy * u1, axis=0, keepdims=True)
            dcw_ref[2:3, :] += jnp.sum(dy * u, axis=0, keepdims=True)
            dy_s[b] = dy

        @pl.when(step == nc - 1)
        def _():
            bkt = bkt_ref[...]
            row = lax.broadcasted_iota(jnp.int32, (N_BUCKETS, BLOCK), 0)
            lane = lax.broadcasted_iota(jnp.int32, (N_BUCKETS, BLOCK), 1)

            def one_bucket(bk, acc):
                sel = bkt == bk
                for h in range(ATT_HEADS):
                    t = jnp.where(sel, dbias_s[h], 0.0)
                    s = jnp.sum(jnp.sum(t, axis=1, keepdims=True), axis=0, keepdims=True)
                    acc = acc + jnp.where((row == bk) & (lane == h), jnp.broadcast_to(s, acc.shape), 0.0)
                return acc

            drb_ref[...] = lax.fori_loop(0, N_BUCKETS, one_bucket, jnp.zeros((N_BUCKETS, BLOCK), F32))

    lp = nc * BLOCK
    smem = pl.BlockSpec(memory_space=pltpu.SMEM)
    blk = lambda s: nc - 1 - s
    prev = lambda s: jnp.maximum(nc - 2 - s, 0)
    proj3 = proj.reshape(nb, lp, ABC_WIDTH)
    res = pl.pallas_call(
        body, name="mixers_bwd",
        grid=(nc,),
        in_specs=[pl.BlockSpec((nb, BLOCK, ABC_WIDTH), lambda s: (0, blk(s), 0)),
                  pl.BlockSpec((nb, BLOCK, 2 * BLOCK), lambda s: (0, prev(s), C_AK // (2 * BLOCK))),
                  pl.BlockSpec((nb, BLOCK, 1280), lambda s: (0, prev(s), C_CC // 1280)),
                  pl.BlockSpec((nb, BLOCK, N_BRANCH * BRANCH_WIDTH), lambda s: (0, blk(s), 0)),
                  pl.BlockSpec((nb, 1, RET_HEADS, BLOCK, BLOCK), lambda s: (0, blk(s), 0, 0, 0)),
                  pl.BlockSpec((BLOCK, BLOCK), lambda s: (blk(s), 0)),
                  pl.BlockSpec((BLOCK, BLOCK), lambda s: (blk(s), 0)),
                  pl.BlockSpec((BLOCK, 2 * BLOCK), lambda s: (0, 0)),
                  smem, smem,
                  pl.BlockSpec((None, 3, BRANCH_WIDTH), lambda s: (layer, 0, 0))],
        out_specs=[pl.BlockSpec((nb, BLOCK, ABC_WIDTH), lambda s: (0, blk(s), 0)),
                   pl.BlockSpec((N_BUCKETS, BLOCK), lambda s: (0, 0)),
                   pl.BlockSpec((ATT_HEADS, BLOCK), lambda s: (0, 0)),
                   pl.BlockSpec((8, BRANCH_WIDTH), lambda s: (0, 0))],
        out_shape=[jax.ShapeDtypeStruct((nb, lp, ABC_WIDTH), BF16),
                   jax.ShapeDtypeStruct((N_BUCKETS, BLOCK), F32),
                   jax.ShapeDtypeStruct((ATT_HEADS, BLOCK), F32),
                   jax.ShapeDtypeStruct((8, BRANCH_WIDTH), F32)],
        scratch_shapes=[pltpu.VMEM((ATT_HEADS, BLOCK, 2 * BLOCK), F32),
                        pltpu.VMEM((ATT_HEADS, BLOCK, 2 * BLOCK), F32),
                        pltpu.VMEM((nb, BLOCK, 2 * BLOCK), F32),
                        pltpu.VMEM((nb, RET_HEADS, BLOCK, BLOCK), F32),
                        pltpu.VMEM((nb, BLOCK, BRANCH_WIDTH), F32)],
        compiler_params=_cparams("arbitrary"),
    )(proj3, proj3, proj3, d_br.reshape(nb, lp, N_BRANCH * BRANCH_WIDTH), states, cosf, sinf, bkt, rel_bias, sinks,
      conv_w)
    return (res[0].reshape(nb * lp, ABC_WIDTH),) + tuple(res[1:])


MERGE_TILE = 256
MERGE_FWD_TILE = 544


def _merge_forward(br_ref, m_ref, wb_ref, wo_ref):
    bo, gates = [], []
    mixed_pre = None
    for g in range(N_BRANCH):
        br_g = br_ref[:, BRANCH_WIDTH * g:BRANCH_WIDTH * (g + 1)]
        bo_g = jnp.concatenate([_nn(br_g, wb_ref[p, g]) for p in range(N_CHIPS)], axis=1)
        gate_g = _sigmoid(m_ref[:, D_MODEL * g:D_MODEL * (g + 1)].astype(F32))
        bo.append(bo_g)
        gates.append(gate_g)
        mixed_pre = gate_g * bo_g if mixed_pre is None else mixed_pre + gate_g * bo_g
    mixed = _nn(mixed_pre.astype(BF16), wo_ref[...])
    r = lax.rsqrt(jnp.mean(mixed * mixed, axis=-1, keepdims=True) + RMS_EPS)
    return bo, gates, mixed_pre, mixed, r


def merge_fwd(x2d, br, pm, wb, wo, g_post, layer, after):
    t = x2d.shape[0]
    tm = MERGE_FWD_TILE if t % MERGE_FWD_TILE == 0 else BLOCK

    def body(x_ref, br_ref, m_ref, wb_ref, wo_ref, g_ref, after_ref, o_ref):
        _, _, _, mixed, r = _merge_forward(br_ref, m_ref, wb_ref, wo_ref)
        o_ref[...] = x_ref[...] + mixed * r * g_ref[...]

    return pl.pallas_call(
        body, name="merge_fwd",
        grid=(t // tm,),
        in_specs=[pl.BlockSpec((tm, D_MODEL), lambda i: (i, 0)),
                  pl.BlockSpec((tm, N_BRANCH * BRANCH_WIDTH), lambda i: (i, 0)),
                  pl.BlockSpec((tm, MERGE_WIDTH), lambda i: (i, 0)),
                  pl.BlockSpec((N_CHIPS, N_BRANCH, BRANCH_WIDTH, SHARD_D), lambda i: (0, 0, 0, 0)),
                  pl.BlockSpec((D_MODEL, D_MODEL), lambda i: (0, 0)),
                  pl.BlockSpec((None, 1, D_MODEL), lambda i: (layer, 0, 0)),
                  ANY],
        out_specs=pl.BlockSpec((tm, D_MODEL), lambda i: (i, 0)),
        out_shape=jax.ShapeDtypeStruct((t, D_MODEL), F32),
        compiler_params=_cparams("parallel"),
    )(x2d, br, pm, wb, wo, g_post, after)


def merge_fwd_loss(x2d, br, pm, wb, wo, g_post, layer, target, lp):
    t = x2d.shape[0]
    tm = MERGE_FWD_TILE if t % MERGE_FWD_TILE == 0 else BLOCK

    def body(x_ref, br_ref, m_ref, wb_ref, wo_ref, g_ref, t_ref, l_ref, d_ref):
        i = pl.program_id(0)

        @pl.when(i == 0)
        def _():
            l_ref[...] = jnp.zeros_like(l_ref)

        _, _, _, mixed, r = _merge_forward(br_ref, m_ref, wb_ref, wo_ref)
        y = x_ref[...] + mixed * r * g_ref[...]
        row = i * tm + lax.broadcasted_iota(jnp.int32, (tm, 1), 0)
        e = jnp.where(row % lp >= BLOCK, y - t_ref[...], 0.0)
        d_ref[...] = e * (1.0 / D_MODEL)
        s = jnp.sum(jnp.sum(e * e, axis=1, keepdims=True), axis=0, keepdims=True)
        l_ref[...] += jnp.broadcast_to(s * (0.5 / D_MODEL), l_ref.shape)

    return pl.pallas_call(
        body, name="merge_fwd_loss",
        grid=(t // tm,),
        in_specs=[pl.BlockSpec((tm, D_MODEL), lambda i: (i, 0)),
                  pl.BlockSpec((tm, N_BRANCH * BRANCH_WIDTH), lambda i: (i, 0)),
                  pl.BlockSpec((tm, MERGE_WIDTH), lambda i: (i, 0)),
                  pl.BlockSpec((N_CHIPS, N_BRANCH, BRANCH_WIDTH, SHARD_D), lambda i: (0, 0, 0, 0)),
                  pl.BlockSpec((D_MODEL, D_MODEL), lambda i: (0, 0)),
                  pl.BlockSpec((None, 1, D_MODEL), lambda i: (layer, 0, 0)),
                  pl.BlockSpec((tm, D_MODEL), lambda i: (i, 0))],
        out_specs=[pl.BlockSpec((8, BLOCK), lambda i: (0, 0)),
                   pl.BlockSpec((tm, D_MODEL), lambda i: (i, 0))],
        out_shape=[jax.ShapeDtypeStruct((8, BLOCK), F32),
                   jax.ShapeDtypeStruct((t, D_MODEL), F32)],
        compiler_params=_cparams("arbitrary"),
    )(x2d, br, pm, wb, wo, g_post, target)


def merge_bwd(d_out, br, pm, wb, wo, g_post, layer, after):
    t = d_out.shape[0]
    tm = MERGE_TILE if t % MERGE_TILE == 0 else BLOCK

    def body(do_ref, br_ref, m_ref, wb_ref, wo_ref, g_ref, after_ref, dbr_ref, dm_ref, dg_ref, dwb_ref, dwo_ref):

        @pl.when(pl.program_id(0) == 0)
        def _():
            dwb_ref[...] = jnp.zeros_like(dwb_ref)
            dwo_ref[...] = jnp.zeros_like(dwo_ref)
            dg_ref[...] = jnp.zeros_like(dg_ref)

        bo, gates, mixed_pre, mixed, r = _merge_forward(br_ref, m_ref, wb_ref, wo_ref)
        d_o = do_ref[...]
        nh = mixed * r
        dg_ref[0:1, :] += jnp.sum(d_o * nh, axis=0, keepdims=True)
        dn = d_o * g_ref[...]
        d_mixed = (r * (dn - nh * jnp.mean(dn * nh, axis=-1, keepdims=True))).astype(BF16)
        dwo_ref[...] += _tn(mixed_pre.astype(BF16), d_mixed)
        d_pre = _nt(d_mixed, wo_ref[...])
        for g in range(N_BRANCH):
            br_g = br_ref[:, BRANCH_WIDTH * g:BRANCH_WIDTH * (g + 1)]
            d_bo = (d_pre * gates[g]).astype(BF16)
            dm_ref[:, D_MODEL * g:D_MODEL * (g + 1)] = (
                d_pre * bo[g] * gates[g] * (1.0 - gates[g])).astype(BF16)
            d_br_g = None
            for p in range(N_CHIPS):
                d_bo_p = d_bo[:, SHARD_D * p:SHARD_D * (p + 1)]
                part = _nt(d_bo_p, wb_ref[p, g])
                d_br_g = part if d_br_g is None else d_br_g + part
                dwb_ref[p, g] += _tn(br_g, d_bo_p)
            dbr_ref[:, BRANCH_WIDTH * g:BRANCH_WIDTH * (g + 1)] = d_br_g.astype(BF16)

    return pl.pallas_call(
        body, name="merge_bwd",
        grid=(t // tm,),
        in_specs=[pl.BlockSpec((tm, D_MODEL), lambda i: (i, 0)),
                  pl.BlockSpec((tm, N_BRANCH * BRANCH_WIDTH), lambda i: (i, 0)),
                  pl.BlockSpec((tm, MERGE_WIDTH), lambda i: (i, 0)),
                  pl.BlockSpec((N_CHIPS, N_BRANCH, BRANCH_WIDTH, SHARD_D), lambda i: (0, 0, 0, 0)),
                  pl.BlockSpec((D_MODEL, D_MODEL), lambda i: (0, 0)),
                  pl.BlockSpec((None, 1, D_MODEL), lambda i: (layer, 0, 0)),
                  ANY],
        out_specs=[pl.BlockSpec((tm, N_BRANCH * BRANCH_WIDTH), lambda i: (i, 0)),
                   pl.BlockSpec((tm, MERGE_WIDTH), lambda i: (i, 0)),
                   pl.BlockSpec((8, D_MODEL), lambda i: (0, 0)),
                   pl.BlockSpec((N_CHIPS, N_BRANCH, BRANCH_WIDTH, SHARD_D), lambda i: (0, 0, 0, 0)),
                   pl.BlockSpec((D_MODEL, D_MODEL), lambda i: (0, 0))],
        out_shape=[jax.ShapeDtypeStruct((t, N_BRANCH * BRANCH_WIDTH), BF16),
                   jax.ShapeDtypeStruct((t, MERGE_WIDTH), BF16),
                   jax.ShapeDtypeStruct((8, D_MODEL), F32),
                   jax.ShapeDtypeStruct((N_CHIPS, N_BRANCH, BRANCH_WIDTH, SHARD_D), F32),
                   jax.ShapeDtypeStruct((D_MODEL, D_MODEL), F32)],
        compiler_params=_cparams("arbitrary"),
    )(d_out, br, pm, wb, wo, g_post, after)


N_ABC_TILES = ABC_WIDTH // COL_TILE
N_M_TILES = MERGE_WIDTH // COL_TILE


def proj_dgrad(d_abc, d_m, w, x2d, g, layer, d_out, after):
    t = x2d.shape[0]
    tm = ROW_TILE if t % ROW_TILE == 0 else BLOCK
    nk = N_ABC_TILES + N_M_TILES

    def body(da_ref, dm_ref, w_ref, x_ref, g_ref, do_ref, after_ref, dx_ref, dg_ref, acc):
        i = pl.program_id(0)
        k = pl.program_id(1)

        @pl.when((i == 0) & (k == 0))
        def _():
            dg_ref[...] = jnp.zeros_like(dg_ref)

        @pl.when(k == 0)
        def _():
            acc[...] = jnp.zeros_like(acc)

        @pl.when(k < N_ABC_TILES)
        def _():
            acc[...] += _nn(da_ref[...], w_ref[...])

        @pl.when(k >= N_ABC_TILES)
        def _():
            acc[...] += _nn(dm_ref[...], w_ref[...])

        @pl.when(k == nk - 1)
        def _():
            x = x_ref[...]
            r = lax.rsqrt(jnp.mean(x * x, axis=-1, keepdims=True) + RMS_EPS)
            nh = x * r
            dh = acc[...]
            dg_ref[0:1, :] += jnp.sum(dh * nh, axis=0, keepdims=True)
            dn = dh * g_ref[...]
            dx_ref[...] = do_ref[...] + r * (dn - nh * jnp.mean(dn * nh, axis=-1, keepdims=True))

    return pl.pallas_call(
        body, name="proj_dgrad",
        grid=(t // tm, nk),
        in_specs=[pl.BlockSpec((tm, COL_TILE), lambda i, k: (i, jnp.minimum(k, N_ABC_TILES - 1))),
                  pl.BlockSpec((tm, COL_TILE), lambda i, k: (i, jnp.maximum(k - N_ABC_TILES, 0))),
                  pl.BlockSpec((COL_TILE, D_MODEL), lambda i, k: (k, 0)),
                  pl.BlockSpec((tm, D_MODEL), lambda i, k: (i, 0)),
                  pl.BlockSpec((None, 1, D_MODEL), lambda i, k: (layer, 0, 0)),
                  pl.BlockSpec((tm, D_MODEL), lambda i, k: (i, 0)),
                  ANY],
        out_specs=[pl.BlockSpec((tm, D_MODEL), lambda i, k: (i, 0)),
                   pl.BlockSpec((8, D_MODEL), lambda i, k: (0, 0))],
        out_shape=[jax.ShapeDtypeStruct((t, D_MODEL), F32),
                   jax.ShapeDtypeStruct((8, D_MODEL), F32)],
        scratch_shapes=[pltpu.VMEM((tm, D_MODEL), F32)],
        compiler_params=_cparams("arbitrary", "arbitrary"),
    )(d_abc, d_m, w, x2d, g, d_out, after)


def proj_wgrad(hb, d_abc, d_m):
    t = hb.shape[0]
    nj = N_ABC_TILES + N_M_TILES

    def body(h_ref, da_ref, dm_ref, o_ref):
        j = pl.program_id(0)

        @pl.when(j < N_ABC_TILES)
        def _():
            o_ref[...] = _tn(da_ref[...], h_ref[...])

        @pl.when(j >= N_ABC_TILES)
        def _():
            o_ref[...] = _tn(dm_ref[...], h_ref[...])

    return pl.pallas_call(
        body, name="proj_wgrad",
        grid=(nj,),
        in_specs=[pl.BlockSpec((t, D_MODEL), lambda j: (0, 0)),
                  pl.BlockSpec((t, COL_TILE), lambda j: (0, jnp.minimum(j, N_ABC_TILES - 1))),
                  pl.BlockSpec((t, COL_TILE), lambda j: (0, jnp.maximum(j - N_ABC_TILES, 0)))],
        out_specs=pl.BlockSpec((COL_TILE, D_MODEL), lambda j: (j, 0)),
        out_shape=jax.ShapeDtypeStruct((PROJ_WIDTH, D_MODEL), F32),
        compiler_params=_cparams("arbitrary"),
    )(hb, d_abc, d_m)


def _adamw_math(w, g, m, v):
    m = ADAM_B1 * m + (1.0 - ADAM_B1) * g
    v = ADAM_B2 * v + (1.0 - ADAM_B2) * jnp.square(g)
    m_hat = m / (1.0 - ADAM_B1 ** ADAM_STEP)
    v_hat = v / (1.0 - ADAM_B2 ** ADAM_STEP)
    delta = -ADAM_LR * (m_hat / (jnp.sqrt(v_hat) + ADAM_EPS) + ADAM_WD * w)
    return delta, m, v


def adamw_layer(w, g, m, v, layer, acc, after):
    _, r, c = w.shape
    tr = _row_tile(r)

    def body(*refs):
        w_ref, g_ref, m_ref, v_ref = refs[:4]
        go_ref, d_ref, mo_ref, vo_ref = refs[-4:]
        g_val = g_ref[...]
        d, m_new, v_new = _adamw_math(w_ref[...], g_val, m_ref[...], v_ref[...])
        go_ref[...] = g_val
        d_ref[...] = d
        mo_ref[...] = m_new
        vo_ref[...] = v_new

    slab = pl.BlockSpec((None, tr, c), lambda i: (layer, i, 0))
    ins = [w, g, m, v, after]
    in_specs = [slab, pl.BlockSpec((tr, c), lambda i: (i, 0)), slab, slab, ANY]
    aliases = {}
    if acc is not None:
        ins += list(acc)
        in_specs += [ANY] * 4
        aliases = {5 + i: i for i in range(4)}
    return pl.pallas_call(
        body, name="adamw_layer",
        grid=(r // tr,),
        in_specs=in_specs, out_specs=[slab] * 4,
        out_shape=[jax.ShapeDtypeStruct(w.shape, F32)] * 4,
        input_output_aliases=aliases,
        compiler_params=_cparams("parallel"),
    )(*ins)


def adamw_small(params):
    k = len(params)

    def body(*refs):
        ins, outs = refs[:4 * k], refs[4 * k:]
        for i in range(k):
            d, m_new, v_new = _adamw_math(*[r[...] for r in ins[4 * i:4 * i + 4]])
            outs[3 * i][...] = d
            outs[3 * i + 1][...] = m_new
            outs[3 * i + 2][...] = v_new

    flat = [a for p in params for a in p]
    vm = pl.BlockSpec(memory_space=pltpu.VMEM)
    out_shape = [jax.ShapeDtypeStruct(p[0].shape, F32) for p in params for _ in range(3)]
    res = pl.pallas_call(
        body, name="adamw_small",
        in_specs=[vm] * len(flat), out_specs=[vm] * len(out_shape), out_shape=out_shape,
    )(*flat)
    return [tuple(res[3 * i:3 * i + 3]) for i in range(k)]


ANY = pl.BlockSpec(memory_space=pl.ANY)


def _place():
    return lax.axis_index("x"), lax.axis_index("y"), lax.axis_index("c")


HBM = pl.BlockSpec(memory_space=pltpu.HBM)
SEM = pl.BlockSpec(memory_space=pltpu.SEMAPHORE)
EFFECT = pltpu.SideEffectType.DATAFLOW_SIDE_EFFECTING


def _other_chips(x, y):
    return [(1 - x, y), (x, 1 - y), (1 - x, 1 - y)]


def _own_slot(shard, chip):
    buf = lax.empty((N_CHIPS,) + shard.shape, shard.dtype)
    return lax.dynamic_update_slice(buf, shard[None], (chip, 0, 0, 0))


def _hbm(a):
    return pltpu.with_memory_space_constraint(a, pltpu.HBM)


def gather_start(bufs, after):
    n = len(bufs)

    def body(*refs):
        g_refs = refs[:n]
        send_sems, recv_sems = refs[n + 1], refs[n + 2]
        token = refs[-1]
        x, y, c = _place()
        me_p = 2 * x + y
        for t in range(n):
            for k, (qx, qy) in enumerate(_other_chips(x, y)):
                slab = g_refs[t].at[me_p, c]
                pltpu.make_async_remote_copy(src_ref=slab, dst_ref=slab, send_sem=send_sems.at[3 * t + k],
                                             recv_sem=recv_sems.at[3 * t + k], device_id=(qx, qy, c),
                                             device_id_type=MESH).start()
        token[...] = jnp.zeros_like(token)

    res = pl.pallas_call(
        body, name="gather_start",
        in_specs=[HBM] * n + [ANY],
        out_specs=[SEM, SEM] + [HBM] * n + [pl.BlockSpec(memory_space=pltpu.VMEM)],
        out_shape=[pltpu.SemaphoreType.DMA((3 * n,)), pltpu.SemaphoreType.DMA((3 * n,))]
        + [pltpu.HBM(b.shape, b.dtype) for b in bufs] + [jax.ShapeDtypeStruct((8, LANES), F32)],
        input_output_aliases={t: 2 + t for t in range(n)},
        compiler_params=pltpu.CompilerParams(has_side_effects=EFFECT),
    )(*[_hbm(b) for b in bufs], after)
    return res[0], res[1], list(res[2:2 + n]), res[-1]


def gather_wait(bufs, send_sems, recv_sems, after, first=0):
    n = len(bufs)

    def body(*refs):
        g_refs = refs[:n]
        send_sems, recv_sems = refs[n], refs[n + 1]
        x, y, c = _place()
        me_p = 2 * x + y
        for t in range(n):
            for k, (qx, qy) in enumerate(_other_chips(x, y)):
                s = 3 * (first + t) + k
                cp = pltpu.make_async_remote_copy(src_ref=g_refs[t].at[me_p, c], dst_ref=g_refs[t].at[2 * qx + qy, c],
                                                  send_sem=send_sems.at[s], recv_sem=recv_sems.at[s],
                                                  device_id=(qx, qy, c), device_id_type=MESH)
                cp.wait_send()
                cp.wait_recv()

    return pl.pallas_call(
        body, name="gather_wait",
        in_specs=[HBM] * n + [SEM, SEM, ANY],
        out_specs=[HBM] * n,
        out_shape=[pltpu.HBM(b.shape, b.dtype) for b in bufs],
        input_output_aliases={t: t for t in range(n)},
        compiler_params=pltpu.CompilerParams(has_side_effects=EFFECT),
    )(*bufs, send_sems, recv_sems, after)


def gather_forward(bufs):
    n = len(bufs)

    def body(*refs):
        g_refs = refs[n:2 * n]
        send_sems, recv_sems = refs[2 * n:]
        x, y, c = _place()
        sibling = (x, y, 1 - c)
        chips = _other_chips(x, y)
        passed = []
        for t in range(n):
            for k, (qx, qy) in enumerate(chips):
                slab = g_refs[t].at[2 * qx + qy, c]
                fwd = pltpu.make_async_remote_copy(src_ref=slab, dst_ref=slab, send_sem=send_sems.at[3 * t + k],
                                                   recv_sem=recv_sems.at[3 * t + k], device_id=sibling,
                                                   device_id_type=MESH)
                fwd.start()
                passed.append(fwd)
        for t in range(n):
            for k, (qx, qy) in enumerate(chips):
                slab = g_refs[t].at[2 * qx + qy, 1 - c]
                pltpu.make_async_remote_copy(src_ref=slab, dst_ref=slab, send_sem=send_sems.at[3 * t + k],
                                             recv_sem=recv_sems.at[3 * t + k], device_id=sibling,
                                             device_id_type=MESH).wait_recv()
        for cp in passed:
            cp.wait_send()

    return pl.pallas_call(
        body, name="gather_forward",
        in_specs=[ANY] * n, out_specs=[ANY] * n,
        out_shape=[jax.ShapeDtypeStruct(b.shape, b.dtype) for b in bufs],
        input_output_aliases={t: t for t in range(n)},
        scratch_shapes=[pltpu.SemaphoreType.DMA((3 * n,)), pltpu.SemaphoreType.DMA((3 * n,))],
    )(*bufs)


def forward_start(bufs):
    n = len(bufs)

    def body(*refs):
        g_refs = refs[:n]
        send_sems, recv_sems = refs[n], refs[n + 1]
        token = refs[-1]
        x, y, c = _place()
        for t in range(n):
            for k, (qx, qy) in enumerate(_other_chips(x, y)):
                slab = g_refs[t].at[2 * qx + qy, c]
                pltpu.make_async_remote_copy(src_ref=slab, dst_ref=slab, send_sem=send_sems.at[3 * t + k],
                                             recv_sem=recv_sems.at[3 * t + k], device_id=(x, y, 1 - c),
                                             device_id_type=MESH).start()
        token[...] = jnp.zeros_like(token)

    res = pl.pallas_call(
        body, name="forward_start",
        in_specs=[HBM] * n,
        out_specs=[SEM, SEM] + [HBM] * n + [pl.BlockSpec(memory_space=pltpu.VMEM)],
        out_shape=[pltpu.SemaphoreType.DMA((3 * n,)), pltpu.SemaphoreType.DMA((3 * n,))]
        + [pltpu.HBM(b.shape, b.dtype) for b in bufs] + [jax.ShapeDtypeStruct((8, LANES), F32)],
        input_output_aliases={t: 2 + t for t in range(n)},
        compiler_params=pltpu.CompilerParams(has_side_effects=EFFECT),
    )(*[_hbm(b) for b in bufs])
    return res[0], res[1], list(res[2:2 + n]), res[-1]


def forward_wait(bufs, send_sems, recv_sems, after):
    n = len(bufs)

    def body(*refs):
        g_refs = refs[:n]
        send_sems, recv_sems = refs[n], refs[n + 1]
        x, y, c = _place()
        for t in range(n):
            for k, (qx, qy) in enumerate(_other_chips(x, y)):
                cp = pltpu.make_async_remote_copy(src_ref=g_refs[t].at[2 * qx + qy, c],
                                                  dst_ref=g_refs[t].at[2 * qx + qy, 1 - c],
                                                  send_sem=send_sems.at[3 * t + k], recv_sem=recv_sems.at[3 * t + k],
                                                  device_id=(x, y, 1 - c), device_id_type=MESH)
                cp.wait_send()
                cp.wait_recv()

    return pl.pallas_call(
        body, name="forward_wait",
        in_specs=[HBM] * n + [SEM, SEM, ANY],
        out_specs=[HBM] * n,
        out_shape=[pltpu.HBM(b.shape, b.dtype) for b in bufs],
        input_output_aliases={t: t for t in range(n)},
        compiler_params=pltpu.CompilerParams(has_side_effects=EFFECT),
    )(*bufs, send_sems, recv_sems, after)


def small_start(pack, me, after):
    buf = lax.dynamic_update_slice(lax.empty((8,) + pack.shape, pack.dtype), pack[None], (me, 0, 0))

    def body(b_ref, after_ref, send_sems, recv_sems, thru, token):
        x, y, c = _place()
        slot = b_ref.at[4 * x + 2 * y + c]
        for k in range(1, 8):
            peer = (x ^ ((k >> 2) & 1), y ^ ((k >> 1) & 1), c ^ (k & 1))
            pltpu.make_async_remote_copy(src_ref=slot, dst_ref=slot, send_sem=send_sems.at[k - 1],
                                         recv_sem=recv_sems.at[k - 1], device_id=peer, device_id_type=MESH).start()
        token[...] = jnp.zeros_like(token)

    return pl.pallas_call(
        body, name="small_start",
        in_specs=[HBM, ANY],
        out_specs=[SEM, SEM, HBM, pl.BlockSpec(memory_space=pltpu.VMEM)],
        out_shape=[pltpu.SemaphoreType.DMA((7,)), pltpu.SemaphoreType.DMA((7,)), pltpu.HBM(buf.shape, buf.dtype),
                   jax.ShapeDtypeStruct((8, LANES), F32)],
        input_output_aliases={0: 2},
        compiler_params=pltpu.CompilerParams(has_side_effects=EFFECT),
    )(_hbm(buf), after)


def small_wait(buf, send_sems, recv_sems, after):
    def body(b_ref, send_sems, recv_sems, after_ref, thru):
        x, y, c = _place()
        mine = b_ref.at[4 * x + 2 * y + c]
        for k in range(1, 8):
            peer = (x ^ ((k >> 2) & 1), y ^ ((k >> 1) & 1), c ^ (k & 1))
            cp = pltpu.make_async_remote_copy(src_ref=mine, dst_ref=b_ref.at[4 * peer[0] + 2 * peer[1] + peer[2]],
                                              send_sem=send_sems.at[k - 1], recv_sem=recv_sems.at[k - 1],
                                              device_id=peer, device_id_type=MESH)
            cp.wait_send()
            cp.wait_recv()

    return pl.pallas_call(
        body, name="small_wait",
        in_specs=[HBM, SEM, SEM, ANY], out_specs=HBM,
        out_shape=pltpu.HBM(buf.shape, buf.dtype),
        input_output_aliases={0: 0},
        compiler_params=pltpu.CompilerParams(has_side_effects=EFFECT),
    )(buf, send_sems, recv_sems, after)


def swap_start(grads, after):
    n = len(grads)

    def body(*refs):
        g_refs, l_refs = refs[:n], refs[n:2 * n]
        send_sems, recv_sems = refs[2 * n + 1], refs[2 * n + 2]
        token = refs[-1]
        x, y, c = _place()
        for t in range(n):
            for p in range(N_CHIPS):
                pltpu.make_async_remote_copy(src_ref=g_refs[t].at[p, 1 - c], dst_ref=l_refs[t].at[p],
                                             send_sem=send_sems.at[N_CHIPS * t + p],
                                             recv_sem=recv_sems.at[N_CHIPS * t + p],
                                             device_id=(x, y, 1 - c), device_id_type=MESH).start()
        token[...] = jnp.zeros_like(token)

    lands = [lax.empty((N_CHIPS,) + g.shape[2:], g.dtype) for g in grads]
    res = pl.pallas_call(
        body, name="swap_start",
        in_specs=[HBM] * (2 * n) + [ANY],
        out_specs=[SEM, SEM] + [HBM] * (2 * n) + [pl.BlockSpec(memory_space=pltpu.VMEM)],
        out_shape=[pltpu.SemaphoreType.DMA((N_CHIPS * n,)), pltpu.SemaphoreType.DMA((N_CHIPS * n,))]
        + [pltpu.HBM(a.shape, a.dtype) for a in grads + lands] + [jax.ShapeDtypeStruct((8, LANES), F32)],
        input_output_aliases={t: 2 + t for t in range(2 * n)},
        compiler_params=pltpu.CompilerParams(has_side_effects=EFFECT),
    )(*[_hbm(a) for a in grads + lands], after)
    return res[0], res[1], list(res[2:2 + n]), list(res[2 + n:2 + 2 * n]), res[-1]


def swap_wait(grads, lands, send_sems, recv_sems, after):
    n = len(grads)

    def body(*refs):
        g_refs, l_refs = refs[:n], refs[n:2 * n]
        send_sems, recv_sems = refs[2 * n], refs[2 * n + 1]
        x, y, c = _place()
        for t in range(n):
            for p in range(N_CHIPS):
                cp = pltpu.make_async_remote_copy(src_ref=g_refs[t].at[p, 1 - c], dst_ref=l_refs[t].at[p],
                                                  send_sem=send_sems.at[N_CHIPS * t + p],
                                                  recv_sem=recv_sems.at[N_CHIPS * t + p],
                                                  device_id=(x, y, 1 - c), device_id_type=MESH)
                cp.wait_send()
                cp.wait_recv()

    res = pl.pallas_call(
        body, name="swap_wait",
        in_specs=[HBM] * (2 * n) + [SEM, SEM, ANY],
        out_specs=[HBM] * (2 * n),
        out_shape=[pltpu.HBM(a.shape, a.dtype) for a in grads + lands],
        input_output_aliases={t: t for t in range(2 * n)},
        compiler_params=pltpu.CompilerParams(has_side_effects=EFFECT),
    )(*grads, *lands, send_sems, recv_sems, after)
    return list(res[:n]), list(res[n:])


def _row_tile(r):
    return max(t for t in range(16, 513, 16) if r % t == 0)


def add_own_half(g, other, c_arr):
    _, _, r, cols = g.shape
    tr = _row_tile(r)

    def body(c_ref, a_ref, b_ref, o_ref):
        o_ref[...] = (a_ref[...] + b_ref[...]).astype(BF16)

    return pl.pallas_call(
        body, name="add_own_half",
        grid_spec=pltpu.PrefetchScalarGridSpec(
            num_scalar_prefetch=1, grid=(N_CHIPS, r // tr),
            in_specs=[pl.BlockSpec((None, None, tr, cols), lambda p, i, c_ref: (p, c_ref[0], i, 0)),
                      pl.BlockSpec((None, tr, cols), lambda p, i, c_ref: (p, i, 0))],
            out_specs=pl.BlockSpec((None, tr, cols), lambda p, i, c_ref: (p, i, 0))),
        out_shape=jax.ShapeDtypeStruct((N_CHIPS, r, cols), BF16),
        compiler_params=_cparams("parallel", "parallel"),
    )(c_arr, g, other)


def scatter_start(partials):
    n = len(partials)

    def body(*refs):
        s_refs, l_refs = refs[:n], refs[n:2 * n]
        send_sems, recv_sems = refs[2 * n], refs[2 * n + 1]
        token = refs[-1]
        x, y, c = _place()
        for t in range(n):
            for k, (qx, qy) in enumerate(_other_chips(x, y)):
                pltpu.make_async_remote_copy(src_ref=s_refs[t].at[2 * qx + qy], dst_ref=l_refs[t].at[k],
                                             send_sem=send_sems.at[3 * t + k], recv_sem=recv_sems.at[3 * t + k],
                                             device_id=(qx, qy, c), device_id_type=MESH).start()
        token[...] = jnp.zeros_like(token)

    lands = [lax.empty((3,) + s.shape[1:], s.dtype) for s in partials]
    res = pl.pallas_call(
        body, name="scatter_start",
        in_specs=[HBM] * (2 * n),
        out_specs=[SEM, SEM] + [HBM] * (2 * n) + [pl.BlockSpec(memory_space=pltpu.VMEM)],
        out_shape=[pltpu.SemaphoreType.DMA((3 * n,)), pltpu.SemaphoreType.DMA((3 * n,))]
        + [pltpu.HBM(a.shape, a.dtype) for a in partials + lands] + [jax.ShapeDtypeStruct((8, LANES), F32)],
        input_output_aliases={t: 2 + t for t in range(2 * n)},
        compiler_params=pltpu.CompilerParams(has_side_effects=EFFECT),
    )(*[_hbm(a) for a in partials + lands])
    return res[0], res[1], list(res[2:2 + n]), list(res[2 + n:2 + 2 * n]), res[-1]


def scatter_wait(partials, lands, send_sems, recv_sems, after):
    n = len(partials)

    def body(*refs):
        s_refs, l_refs = refs[:n], refs[n:2 * n]
        send_sems, recv_sems = refs[2 * n], refs[2 * n + 1]
        x, y, c = _place()
        for t in range(n):
            for k, (qx, qy) in enumerate(_other_chips(x, y)):
                cp = pltpu.make_async_remote_copy(src_ref=s_refs[t].at[2 * qx + qy], dst_ref=l_refs[t].at[k],
                                                  send_sem=send_sems.at[3 * t + k], recv_sem=recv_sems.at[3 * t + k],
                                                  device_id=(qx, qy, c), device_id_type=MESH)
                cp.wait_send()
                cp.wait_recv()

    res = pl.pallas_call(
        body, name="scatter_wait",
        in_specs=[HBM] * (2 * n) + [SEM, SEM, ANY],
        out_specs=[HBM] * (2 * n),
        out_shape=[pltpu.HBM(a.shape, a.dtype) for a in partials + lands],
        input_output_aliases={t: t for t in range(2 * n)},
        compiler_params=pltpu.CompilerParams(has_side_effects=EFFECT),
    )(*partials, *lands, send_sems, recv_sems, after)
    return list(res[:n]), list(res[n:])


def sum_chips(own, parts, where):
    _, r, cols = own.shape
    tr = _row_tile(r)

    def body(w_ref, a_ref, p_ref, o_ref):
        acc = a_ref[...].astype(F32)
        for k in range(3):
            acc = acc + p_ref[k].astype(F32)
        o_ref[...] = acc

    return pl.pallas_call(
        body, name="sum_chips",
        grid_spec=pltpu.PrefetchScalarGridSpec(
            num_scalar_prefetch=1, grid=(r // tr,),
            in_specs=[pl.BlockSpec((None, tr, cols), lambda i, w_ref: (w_ref[0], i, 0)),
                      pl.BlockSpec((3, tr, cols), lambda i, w_ref: (0, i, 0))],
            out_specs=pl.BlockSpec((None, tr, cols), lambda i, w_ref: (w_ref[1], i, 0))),
        out_shape=jax.ShapeDtypeStruct((DEPTH, r, cols), F32),
        compiler_params=_cparams("parallel"),
    )(where, own, parts)


def sibling_share_layer(bufs):
    n = len(bufs)

    def body(*refs):
        o_refs = refs[n:2 * n]
        send_sems, recv_sems = refs[2 * n:]
        x, y, c = _place()
        cps = []
        for t in range(n):
            cp = pltpu.make_async_remote_copy(src_ref=o_refs[t].at[c], dst_ref=o_refs[t].at[c], send_sem=send_sems.at[t],
                                              recv_sem=recv_sems.at[t], device_id=(x, y, 1 - c), device_id_type=MESH)
            cp.start()
            cps.append(cp)
        for t in range(n):
            slot = o_refs[t].at[1 - c]
            pltpu.make_async_remote_copy(src_ref=slot, dst_ref=slot, send_sem=send_sems.at[t], recv_sem=recv_sems.at[t],
                                         device_id=(x, y, 1 - c), device_id_type=MESH).wait_recv()
        for cp in cps:
            cp.wait_send()

    return pl.pallas_call(
        body, name="sibling_share_layer",
        in_specs=[ANY] * n, out_specs=[ANY] * n,
        out_shape=[jax.ShapeDtypeStruct(b.shape, b.dtype) for b in bufs],
        input_output_aliases={t: t for t in range(n)},
        scratch_shapes=[pltpu.SemaphoreType.DMA((n,)), pltpu.SemaphoreType.DMA((n,))],
    )(*bufs)


def share_start(bufs):
    n = len(bufs)

    def body(*refs):
        o_refs = refs[:n]
        send_sems, recv_sems = refs[n], refs[n + 1]
        token = refs[-1]
        x, y, c = _place()
        for t in range(n):
            pltpu.make_async_remote_copy(src_ref=o_refs[t].at[c], dst_ref=o_refs[t].at[c], send_sem=send_sems.at[t],
                                         recv_sem=recv_sems.at[t], device_id=(x, y, 1 - c), device_id_type=MESH).start()
        token[...] = jnp.zeros_like(token)

    res = pl.pallas_call(
        body, name="share_start",
        in_specs=[HBM] * n,
        out_specs=[SEM, SEM] + [HBM] * n + [pl.BlockSpec(memory_space=pltpu.VMEM)],
        out_shape=[pltpu.SemaphoreType.DMA((n,)), pltpu.SemaphoreType.DMA((n,))]
        + [pltpu.HBM(b.shape, b.dtype) for b in bufs] + [jax.ShapeDtypeStruct((8, LANES), F32)],
        input_output_aliases={t: 2 + t for t in range(n)},
        compiler_params=pltpu.CompilerParams(has_side_effects=EFFECT),
    )(*[_hbm(b) for b in bufs])
    return res[0], res[1], list(res[2:2 + n]), res[-1]


def share_wait(bufs, send_sems, recv_sems, after):
    n = len(bufs)

    def body(*refs):
        o_refs = refs[:n]
        send_sems, recv_sems = refs[n], refs[n + 1]
        x, y, c = _place()
        for t in range(n):
            cp = pltpu.make_async_remote_copy(src_ref=o_refs[t].at[c], dst_ref=o_refs[t].at[1 - c],
                                              send_sem=send_sems.at[t], recv_sem=recv_sems.at[t],
                                              device_id=(x, y, 1 - c), device_id_type=MESH)
            cp.wait_send()
            cp.wait_recv()

    return pl.pallas_call(
        body, name="share_wait",
        in_specs=[HBM] * n + [SEM, SEM, ANY],
        out_specs=[HBM] * n,
        out_shape=[pltpu.HBM(b.shape, b.dtype) for b in bufs],
        input_output_aliases={t: t for t in range(n)},
        compiler_params=pltpu.CompilerParams(has_side_effects=EFFECT),
    )(*bufs, send_sems, recv_sems, after)


SP_META = 2 * (N_META * D_MODEL // LANES)
SP_NORM = DEPTH * D_MODEL // LANES
SP_RB = DEPTH * N_BUCKETS
SP_SINK = DEPTH * ATT_HEADS
SP_CONV = DEPTH * 3 * BRANCH_WIDTH // LANES
SP_LOSS = 8
SIDE_ROWS = 48
SP_ROWS = SP_META + 2 * SP_NORM + SP_RB + SP_SINK + SP_CONV + SP_LOSS


def sum_small(slots):
    half = SP_META // 2
    rb0 = SP_META + 2 * SP_NORM
    rest_rows = SP_ROWS - SP_META

    def body(s_ref, meta_ref, rest_ref):
        acc = s_ref[0]
        for d in range(1, 8):
            acc = acc + s_ref[d]
        meta_ref[...] = acc[0:half] + acc[half:SP_META]
        rest_ref[...] = acc[SP_META:]
        rest_ref[rb0 - SP_META:rb0 - SP_META + N_BUCKETS, :] = (
            acc[rb0:rb0 + N_BUCKETS] + acc[rb0 + N_BUCKETS:rb0 + 2 * N_BUCKETS])

    vm = pl.BlockSpec(memory_space=pltpu.VMEM)
    return pl.pallas_call(
        body, name="sum_small",
        in_specs=[vm], out_specs=[vm, vm],
        out_shape=[jax.ShapeDtypeStruct((half, LANES), F32), jax.ShapeDtypeStruct((rest_rows, LANES), F32)],
    )(slots)


def local_step(x, loss_target, meta_full, rel_bias, norm_pre, conv_w_full, attn_sinks, norm_post, weights_of, mid_fwd,
               grads_done, bwd_done):
    nb, seq, _ = x.shape
    nc = seq // BLOCK + 1
    lp = nc * BLOCK
    rows = nb * lp
    pad = jnp.zeros((nb, PAD_FRONT, D_MODEL), F32)
    meta = jnp.broadcast_to(meta_full[None], (nb, N_META, D_MODEL))
    h0 = jnp.concatenate([pad, meta, x], axis=1).reshape(rows, D_MODEL)
    target = jnp.pad(loss_target, ((0, 0), (BLOCK, 0), (0, 0))).reshape(rows, D_MODEL)
    cosf, sinf = _rot_tables(lp)
    bkt = jnp.asarray(_bucket_table())

    g_pre = norm_pre.reshape(DEPTH, 1, D_MODEL)
    g_post = norm_post.reshape(DEPTH, 1, D_MODEL)
    order = lambda token: bkt if token is None else token

    acts = []
    h = h0
    for l in range(DEPTH):
        w_in, token = weights_of(l, h)
        hb, p_abc = norm_matmul(h, g_pre, l, w_in, 0, N_ABC_TILES, order(token))
        p_m = matmul_cols(hb, w_in, N_ABC_TILES, N_M_TILES)
        br, states = mixers_fwd(p_abc, cosf, sinf, bkt, rel_bias, attn_sinks, conv_w_full, l, nb, nc)
        (w_br, w_out), token = mid_fwd(l, br)
        acts.append((h, hb, p_abc, p_m, br, states, w_in, w_br, w_out))
        if l < DEPTH - 1:
            h = merge_fwd(h, br, p_m, w_br, w_out, g_post, l, order(token))
        else:
            assert token is None
            loss_part, d_h = merge_fwd_loss(h, br, p_m, w_br, w_out, g_post, l, target, lp)

    small = [None] * DEPTH
    token = None
    for l in reversed(range(DEPTH)):
        h_in, hb, p_abc, p_m, br, states, w_in, w_br, w_out = acts[l]
        d_br, d_m, d_gpost, g_wbr, g_wout = merge_bwd(d_h, br, p_m, w_br, w_out, g_post, l, order(token))
        d_abc, d_rb, d_sk, d_cw = mixers_bwd(p_abc, d_br, states, cosf, sinf, bkt, rel_bias,
                                             attn_sinks, conv_w_full, l, nb, nc)
        g_win = proj_wgrad(hb, d_abc, d_m)
        token = grads_done(l, [g_win, g_wbr, g_wout])
        d_h, d_gpre = proj_dgrad(d_abc, d_m, w_in, h_in, g_pre, l, d_h, order(token))
        token = bwd_done(l, d_h)
        small[l] = (d_gpre[0], d_gpost[0], d_rb, d_sk, d_cw[0:3])

    d_h3 = d_h.reshape(nb, lp, D_MODEL)
    d_meta = d_h3[:, PAD_FRONT:BLOCK]
    sp = jnp.concatenate([
        d_meta.reshape(-1, LANES),
        jnp.stack([small[l][0] for l in range(DEPTH)]).reshape(-1, LANES),
        jnp.stack([small[l][1] for l in range(DEPTH)]).reshape(-1, LANES),
        jnp.concatenate([small[l][2] for l in range(DEPTH)], axis=0),
        jnp.concatenate([small[l][3] for l in range(DEPTH)], axis=0),
        jnp.stack([small[l][4] for l in range(DEPTH)]).reshape(-1, LANES),
        loss_part], axis=0)
    return d_h3, sp


def kernel(x, meta_tokens, rel_bias, norm_pre, w_in, conv_w, attn_sinks, w_branch, w_out, norm_post, loss_target, m_meta_tokens, m_rel_bias, m_norm_pre, m_w_in, m_conv_w, m_attn_sinks, m_w_branch, m_w_out, m_norm_post, v_meta_tokens, v_rel_bias, v_norm_pre, v_w_in, v_conv_w, v_attn_sinks, v_w_branch, v_w_out, v_norm_post):
    assert x.shape[0] == 2 and SP_META == 2 * N_META * D_MODEL // LANES
    px, py, pc = _place()
    chip = 2 * px + py

    c_arr = jnp.reshape(pc, (1,)).astype(jnp.int32)
    where = jnp.stack([chip, pc]).astype(jnp.int32)
    tr_ = lambda a: jnp.swapaxes(a, 1, 2)
    w3 = [tr_(w_in), w_branch.reshape(DEPTH, N_BRANCH * BRANCH_WIDTH, SHARD_D), w_out]
    halves = lambda a: a.reshape(2, a.shape[0] // 2, a.shape[1])

    def as_weights(bufs):
        a_in, a_br, a_out = bufs
        return (a_in.reshape(PROJ_WIDTH, D_MODEL), a_br.reshape(N_CHIPS, N_BRANCH, BRANCH_WIDTH, SHARD_D),
                a_out.reshape(D_MODEL, D_MODEL))

    n_meta_rows = N_META * SHARD_D // LANES
    side = jnp.concatenate([meta_tokens.reshape(-1), conv_w.reshape(-1)]).reshape(-1, LANES)
    side = jnp.concatenate([side, jnp.zeros((SIDE_ROWS - side.shape[0], LANES), F32)], axis=0)
    slots = [[_own_slot(halves(w[l].astype(BF16)), chip) for w in w3] for l in range(DEPTH)]
    send0, recv0, flying0, _ = gather_start([_own_slot(halves(side), chip)] + slots[0], where)
    side_chips = gather_forward(gather_wait(flying0[:1], send0, recv0, where))[0].reshape(N_CHIPS, SIDE_ROWS, LANES)
    meta_full = jnp.moveaxis(side_chips[:, :n_meta_rows].reshape(N_CHIPS, N_META, SHARD_D), 0, 1).reshape(N_META, D_MODEL)
    conv_full = jnp.moveaxis(side_chips[:, n_meta_rows:n_meta_rows + 6].reshape(N_CHIPS, DEPTH, 3, LANES), 0, 2).reshape(DEPTH, 3, BRANCH_WIDTH)
    inbound = {}

    def weights_of(l, h):
        if l == 0:
            inbound[0] = gather_forward(gather_wait(flying0[1:2], send0, recv0, h, first=1))
            inbound[1] = gather_start(slots[1], inbound[0][0])
            return inbound[0][0].reshape(PROJ_WIDTH, D_MODEL), inbound[1][3]
        send, recv, thru = inbound[1]
        inbound[1] = as_weights(forward_wait(thru, send, recv, h))
        return inbound[1][0], None

    def mid_fwd(l, br):
        if l == 0:
            rest = gather_forward(gather_wait(flying0[2:], send0, recv0, br, first=2))
            send, recv, flying1, _ = inbound[1]
            send, recv, thru, started = forward_start(gather_wait(flying1, send, recv, rest[0]))
            inbound[1] = (send, recv, thru)
            return as_weights(inbound[0] + rest)[1:], started
        return inbound[1][1:], None

    reduced = [None] * DEPTH
    flying = {}

    def finish_reduce(l, after, meanwhile=None):
        partials, parts = scatter_wait(*flying[l], after)
        sums = [sum_chips(a, p, where) for a, p in zip(partials, parts)]
        if meanwhile is None:
            reduced[l] = sibling_share_layer(sums)
            return None
        send, recv, thru, started = share_start(sums)
        out = meanwhile(started)
        reduced[l] = share_wait(thru, send, recv, out)
        return out

    def start_scatter(l, full, others):
        send, recv, thru, lands, started = scatter_start([add_own_half(g, o, c_arr) for g, o in zip(full, others)])
        flying[l] = (thru, lands, send, recv)
        return started

    m3 = [tr_(m_w_in), m_w_branch.reshape(w3[1].shape), m_w_out]
    v3 = [tr_(v_w_in), v_w_branch.reshape(w3[1].shape), v_w_out]
    big = [None] * 3

    def adamw_of(l, after):
        for t in range(3):
            big[t] = adamw_layer(w3[t], reduced[l][t].reshape(w3[t].shape[1:]), m3[t], v3[t], l, big[t], after)
            after = big[t][1]

    def grads_done(l, grads):
        full = [g.reshape(N_CHIPS, 2, g.size // (2 * N_CHIPS * g.shape[-1]), g.shape[-1]) for g in grads]
        if l == 0:
            finish_reduce(1, grads[0])
        send, recv, thru, lands, started = swap_start(full, where if l == 1 else reduced[1][0])
        if l == 1:
            flying["swap"] = (thru, lands, send, recv)
            return started
        adamw_of(1, started)
        return start_scatter(0, *swap_wait(thru, lands, send, recv, big[2][1]))

    def bwd_done(l, d_h):
        if l == 1:
            return start_scatter(1, *swap_wait(*flying["swap"], d_h))
        return None

    d_h, sp = local_step(x, loss_target, meta_full, rel_bias, norm_pre, conv_full, attn_sinks, norm_post,
                         weights_of, mid_fwd, grads_done, bwd_done)
    d_x = finish_reduce(0, sp, lambda started: d_h[:, BLOCK:] + started[0, 0])

    s_send, s_recv, s_buf, s_started = small_start(sp, 4 * px + 2 * py + pc, reduced[0][0])

    adamw_of(0, s_started)
    g_in, *u_in = [tr_(a) for a in big[0]]
    g_br, *u_br = [a.reshape(w_branch.shape) for a in big[1]]
    g_out, *u_out = big[2]

    meta_rows, rest = sum_small(small_wait(s_buf, s_send, s_recv, big[2][1]))
    o = 0
    g_meta_full = meta_rows.reshape(N_META, D_MODEL)
    g_norm_pre = rest[o:o + SP_NORM].reshape(DEPTH, D_MODEL); o += SP_NORM
    g_norm_post = rest[o:o + SP_NORM].reshape(DEPTH, D_MODEL); o += SP_NORM
    g_rel_bias = rest[o:o + N_BUCKETS, :ATT_HEADS]; o += SP_RB
    g_sinks = rest[o:o + SP_SINK, 0].reshape(DEPTH, ATT_HEADS); o += SP_SINK
    g_conv_full = rest[o:o + SP_CONV].reshape(DEPTH, 3, BRANCH_WIDTH); o += SP_CONV
    loss = rest[o, 0]
    g_meta = lax.dynamic_slice_in_dim(g_meta_full, chip * SHARD_D, SHARD_D, axis=1)
    g_conv = lax.dynamic_slice_in_dim(g_conv_full, chip * LANES, LANES, axis=2)

    to2 = lambda a: a.reshape(-1, a.shape[-1])
    smalls = [(meta_tokens, g_meta, m_meta_tokens, v_meta_tokens),
              (rel_bias, g_rel_bias, m_rel_bias, v_rel_bias),
              (norm_pre, g_norm_pre, m_norm_pre, v_norm_pre),
              (to2(conv_w), to2(g_conv), to2(m_conv_w), to2(v_conv_w)),
              (attn_sinks, g_sinks, m_attn_sinks, v_attn_sinks),
              (norm_post, g_norm_post, m_norm_post, v_norm_post)]
    u_meta, u_rb, u_npre, u_conv, u_sink, u_npost = adamw_small(smalls)
    u_conv = tuple(a.reshape(conv_w.shape) for a in u_conv)

    grads = [g_meta, g_rel_bias, g_norm_pre, g_in, g_conv, g_sinks, g_br, g_out, g_norm_post]
    upd = [u_meta, u_rb, u_npre, u_in, u_conv, u_sink, u_br, u_out, u_npost]
    return (loss, d_x, *grads, *[u[0] for u in upd], *[u[1] for u in upd], *[u[2] for u in upd])
```

```python
import math

import numpy as np
import jax
import jax.numpy as jnp
from jax import lax
from jax.experimental import pallas as pl
from jax.experimental.pallas import tpu as pltpu

F32 = jnp.float32
BF16 = jnp.bfloat16
MESH = pl.DeviceIdType.MESH

D_MODEL = 1024
DEPTH = 2
N_META = 16
BLOCK = 128
PAD_FRONT = BLOCK - N_META
ATT_HEADS = 8
ATT_HEAD_DIM = 64
N_BUCKETS = 32
MAX_EXACT = 16
MAX_DISTANCE = 128
RET_HEADS = 4
ROT_BASE = 10000.0
N_BRANCH = 3
BRANCH_WIDTH = 512
PROJ_WIDTH = 8448
ABC_WIDTH = 5376
MERGE_WIDTH = N_BRANCH * D_MODEL
RMS_EPS = 1e-6
GN_EPS = 1e-6
NEG_INF = -1e30
ATT_SCALE = ATT_HEAD_DIM ** -0.5
RET_SCALE = BLOCK ** -0.5
LOG_GAMMA = tuple(math.log1p(-(2.0 ** (-5.0 - h))) for h in range(RET_HEADS))

C_AQ, C_AK, C_AV, C_AG = 0, 512, 640, 768
C_RQ, C_RK, C_RV, C_RG = 1280, 1792, 2304, 2816
C_CB, C_CC, C_CX, C_CG = 3328, 3840, 4352, 4864

ADAM_LR = 0.001
ADAM_B1 = 0.9
ADAM_B2 = 0.999
ADAM_EPS = 1e-08
ADAM_WD = 0.01
ADAM_STEP = 10

N_CHIPS = 4
SHARD_D = D_MODEL // N_CHIPS
LANES = 128

VMEM_LIMIT = 56 * 1024 * 1024
COL_TILE = 768
ROW_TILE = 1088
PROJ_ROW_TILE = 2176


def _cparams(*sem):
    return pltpu.CompilerParams(dimension_semantics=sem, vmem_limit_bytes=VMEM_LIMIT)


def _nt(a, b):
    return lax.dot_general(a, b, (((1,), (1,)), ((), ())), preferred_element_type=F32)


def _tn(a, b):
    return lax.dot_general(a, b, (((0,), (0,)), ((), ())), preferred_element_type=F32)


def _nn(a, b):
    return jnp.dot(a, b, preferred_element_type=F32)


def _sigmoid(x):
    return 0.5 * jnp.tanh(0.5 * x) + 0.5


def _silu(x):
    return x * _sigmoid(x)


def _dsilu(x):
    s = _sigmoid(x)
    return s * (1.0 + x * (1.0 - s))


def _bucket_table():
    r = np.arange(BLOCK)[:, None]
    c = np.arange(2 * BLOCK)[None, :]
    n = np.maximum(BLOCK + r - c, 0)
    nf = np.maximum(n, 1).astype(np.float32)
    large = MAX_EXACT + (np.log(nf / MAX_EXACT) / math.log(MAX_DISTANCE / MAX_EXACT)
                         * (N_BUCKETS - MAX_EXACT)).astype(np.int32)
    large = np.minimum(large, N_BUCKETS - 1)
    return np.where(n < MAX_EXACT, n, large).astype(np.int32)


def _rot_tables(lp):
    half = BLOCK // 2
    pos = (jnp.arange(lp) - PAD_FRONT).astype(F32)
    theta = 1.0 / (ROT_BASE ** jnp.linspace(0.0, 1.0, half, dtype=F32))
    ang = pos[:, None] * theta[None, :]
    cos, sin = jnp.cos(ang), jnp.sin(ang)
    return jnp.concatenate([cos, cos], axis=1), jnp.concatenate([-sin, sin], axis=1)


def norm_matmul(x2d, g, layer, w, col0_blocks, n_col_blocks, after):
    t = x2d.shape[0]
    tm = PROJ_ROW_TILE if t % PROJ_ROW_TILE == 0 else BLOCK

    def body(x_ref, g_ref, w_ref, after_ref, hb_ref, o_ref):
        @pl.when(pl.program_id(1) == 0)
        def _():
            x = x_ref[...]
            r = lax.rsqrt(jnp.mean(x * x, axis=-1, keepdims=True) + RMS_EPS)
            hb_ref[...] = (x * r * g_ref[...]).astype(BF16)

        o_ref[...] = _nt(hb_ref[...], w_ref[...])

    return pl.pallas_call(
        body, name="norm_matmul",
        grid=(t // tm, n_col_blocks),
        in_specs=[pl.BlockSpec((tm, D_MODEL), lambda i, j: (i, 0)),
                  pl.BlockSpec((None, 1, D_MODEL), lambda i, j: (layer, 0, 0)),
                  pl.BlockSpec((COL_TILE, D_MODEL), lambda i, j: (j + col0_blocks, 0)),
                  ANY],
        out_specs=[pl.BlockSpec((tm, D_MODEL), lambda i, j: (i, 0)),
                   pl.BlockSpec((tm, COL_TILE), lambda i, j: (i, j))],
        out_shape=[jax.ShapeDtypeStruct((t, D_MODEL), BF16),
                   jax.ShapeDtypeStruct((t, n_col_blocks * COL_TILE), F32)],
        compiler_params=_cparams("parallel", "arbitrary"),
    )(x2d, g, w, after)


def matmul_cols(a, w, col0_blocks, n_col_blocks):
    t, k = a.shape
    tm = PROJ_ROW_TILE if t % PROJ_ROW_TILE == 0 else BLOCK

    def body(a_ref, w_ref, o_ref):
        o_ref[...] = _nt(a_ref[...], w_ref[...]).astype(BF16)

    return pl.pallas_call(
        body, name="matmul_cols",
        grid=(t // tm, n_col_blocks),
        in_specs=[pl.BlockSpec((tm, k), lambda i, j: (i, 0)),
                  pl.BlockSpec((COL_TILE, k), lambda i, j: (j + col0_blocks, 0))],
        out_specs=pl.BlockSpec((tm, COL_TILE), lambda i, j: (i, j)),
        out_shape=jax.ShapeDtypeStruct((t, n_col_blocks * COL_TILE), BF16),
        compiler_params=_cparams("parallel", "arbitrary"),
    )(a, w)


class _Widened:
    def __init__(self, ref):
        self.ref = ref

    def __getitem__(self, idx):
        return self.ref[idx].astype(F32)


def _build_bias(bkt_ref, rb_ref, bias_s):
    bkt = bkt_ref[...]
    for h in range(ATT_HEADS):
        acc = jnp.zeros((BLOCK, 2 * BLOCK), F32)
        for b in range(N_BUCKETS):
            acc = jnp.where(bkt == b, rb_ref[b, h], acc)
        bias_s[h] = acc


def _band_mask(n):
    r = lax.broadcasted_iota(jnp.int32, (BLOCK, 2 * BLOCK), 0)
    c = lax.broadcasted_iota(jnp.int32, (BLOCK, 2 * BLOCK), 1)
    key_pos = (n - 1) * BLOCK + c
    return (c > r) & (c <= r + BLOCK) & (key_pos >= PAD_FRONT)


def _split_heads(kv, kh):
    lane = lax.broadcasted_iota(jnp.int32, kv.shape, 1)
    if kh == 0:
        lo = jnp.where(lane < ATT_HEAD_DIM, kv, 0.0)
        hi = pltpu.roll(lo, ATT_HEAD_DIM, 1)
    else:
        hi = jnp.where(lane >= ATT_HEAD_DIM, kv, 0.0)
        lo = pltpu.roll(hi, ATT_HEAD_DIM, 1)
    return lo, hi


def _merge_heads(acc_lo, acc_hi, kh):
    lane = lax.broadcasted_iota(jnp.int32, acc_lo.shape, 1)
    if kh == 0:
        return jnp.where(lane < ATT_HEAD_DIM, acc_lo + pltpu.roll(acc_hi, ATT_HEAD_DIM, 1), 0.0)
    return jnp.where(lane >= ATT_HEAD_DIM, acc_hi + pltpu.roll(acc_lo, ATT_HEAD_DIM, 1), 0.0)


def _softmax_of(qk, bias_h, mask, sink_h):
    s = qk + bias_h
    s = jnp.where(mask, s, NEG_INF)
    m = jnp.maximum(jnp.max(s, axis=-1, keepdims=True), sink_h)
    p = jnp.exp(s - m)
    es = jnp.exp(sink_h - m)
    inv = 1.0 / (jnp.sum(p, axis=-1, keepdims=True) + es)
    return p * inv, es * inv


def _rot(t, cosf, sinf):
    return t * cosf + pltpu.roll(t, BLOCK // 2, 1) * sinf


def _rot_t(d, cosf, sinf):
    return d * cosf + pltpu.roll(d * sinf, BLOCK // 2, 1)


def _decay_tables(h):
    lg = LOG_GAMMA[h]
    i = lax.broadcasted_iota(jnp.int32, (BLOCK, BLOCK), 0)
    j = lax.broadcasted_iota(jnp.int32, (BLOCK, BLOCK), 1)
    diff = (i - j).astype(F32)
    dm = jnp.where(diff >= 0, jnp.exp(diff * lg), 0.0)
    row = lax.broadcasted_iota(jnp.int32, (BLOCK, 1), 0).astype(F32)
    zeta = jnp.exp((BLOCK - 1 - row) * lg)
    xi = jnp.exp((row + 1.0) * lg)
    return dm, zeta, xi, math.exp(BLOCK * lg)


def _valid_col(n):
    row = lax.broadcasted_iota(jnp.int32, (BLOCK, 1), 0)
    return ((n * BLOCK + row) >= PAD_FRONT).astype(F32)


def _shift_down(cur, prev, k):
    row = lax.broadcasted_iota(jnp.int32, cur.shape, 0)
    return jnp.where(row >= k, pltpu.roll(cur, k, 0), pltpu.roll(prev, k, 0))


def _shift_up(cur, nxt, k):
    row = lax.broadcasted_iota(jnp.int32, cur.shape, 0)
    return jnp.where(row < BLOCK - k, pltpu.roll(cur, BLOCK - k, 0), pltpu.roll(nxt, BLOCK - k, 0))


def mixers_fwd(proj, cosf, sinf, bkt, rel_bias, sinks, conv_w, layer, nb, nc):
    def body(p_ref, cos_ref, sin_ref, bkt_ref, rb_ref, sk_ref, cw_ref, br_ref, st_ref,
             bias_s, kv_s, state_s, u_s):
        p_ref = _Widened(p_ref)
        n = pl.program_id(0)

        @pl.when(n == 0)
        def _():
            _build_bias(bkt_ref, rb_ref, bias_s)
            kv_s[:, 0:BLOCK, :] = jnp.zeros((nb, BLOCK, 2 * BLOCK), F32)
            state_s[...] = jnp.zeros_like(state_s)
            u_s[...] = jnp.zeros_like(u_s)

        valid = _valid_col(n)
        mask = _band_mask(n)
        ex = range(nb)

        for b in ex:
            kv_s[b, BLOCK:2 * BLOCK, :] = p_ref[b, :, C_AK:C_AK + 2 * BLOCK]
        for kh in range(2):
            ks = [[t.astype(BF16) for t in _split_heads(kv_s[b, :, 0:BLOCK], kh)] for b in ex]
            vs = [[t.astype(BF16) for t in _split_heads(kv_s[b, :, BLOCK:2 * BLOCK], kh)] for b in ex]
            pairs = [(b, 2 * kh + jj) for jj in range(2) for b in ex]
            subs = [(b, j, x) for (b, j) in pairs for x in range(2)]
            qb_ = {(b, j): (p_ref[b, :, C_AQ + BLOCK * j:C_AQ + BLOCK * (j + 1)] * ATT_SCALE).astype(BF16)
                   for (b, j) in pairs}
            qk_ = {(b, j, x): _nt(qb_[(b, j)], ks[b][x]) for (b, j, x) in subs}
            pb_ = {}
            for u in subs:
                h = 2 * u[1] + u[2]
                pb_[u] = _softmax_of(qk_[u], bias_s[h], mask, sk_ref[layer, h])[0].astype(BF16)
            o_ = {u: _nn(pb_[u], vs[u[0]][u[2]]) for u in subs}
            for (b, j) in pairs:
                gate = p_ref[b, :, C_AG + BLOCK * j:C_AG + BLOCK * (j + 1)]
                br_ref[b, :, BLOCK * j:BLOCK * (j + 1)] = ((o_[(b, j, 0)] + o_[(b, j, 1)]) * _silu(gate)).astype(BF16)
        for b in ex:
            kv_s[b, 0:BLOCK, :] = kv_s[b, BLOCK:2 * BLOCK, :]

        cosv = cos_ref[...]
        sinv = sin_ref[...]
        tabs = [_decay_tables(h) for h in range(RET_HEADS)]
        units = [(b, h) for h in range(RET_HEADS) for b in ex]
        sl = lambda c0, h: slice(c0 + BLOCK * h, c0 + BLOCK * (h + 1))
        q_, k_, v_, sp_ = {}, {}, {}, {}
        for u in units:
            b, h = u
            q_[u] = _rot(p_ref[b, :, sl(C_RQ, h)], cosv, sinv).astype(BF16)
            k_[u] = (_rot(p_ref[b, :, sl(C_RK, h)], cosv, sinv) * RET_SCALE * valid).astype(BF16)
            v_[u] = p_ref[b, :, sl(C_RV, h)]
            sp_[u] = state_s[b, h]
            st_ref[b, 0, h] = sp_[u]
        qk_ = {u: _nt(q_[u], k_[u]) for u in units}
        qs_ = {u: _nn(q_[u], sp_[u].astype(BF16)) for u in units}
        kv_ = {u: _tn(k_[u], (v_[u] * tabs[u[1]][1]).astype(BF16)) for u in units}
        a_ = {u: (qk_[u] * tabs[u[1]][0]).astype(BF16) for u in units}
        av_ = {u: _nn(a_[u], v_[u].astype(BF16)) for u in units}
        for u in units:
            b, h = u
            o = av_[u] + tabs[h][2] * qs_[u]
            mu = jnp.mean(o, axis=-1, keepdims=True)
            var = jnp.mean(jnp.square(o - mu), axis=-1, keepdims=True)
            oh = (o - mu) * lax.rsqrt(var + GN_EPS)
            gate = p_ref[b, :, sl(C_RG, h)]
            br_ref[b, :, BRANCH_WIDTH + BLOCK * h:BRANCH_WIDTH + BLOCK * (h + 1)] = (oh * _silu(gate)).astype(BF16)
            state_s[b, h] = tabs[h][3] * sp_[u] + kv_[u]

        for b in ex:
            u = p_ref[b, :, C_CC:C_CC + BRANCH_WIDTH] * p_ref[b, :, C_CX:C_CX + BRANCH_WIDTH] * valid
            u_prev = u_s[b]
            y = (cw_ref[0:1, :] * _shift_down(u, u_prev, 2) + cw_ref[1:2, :] * _shift_down(u, u_prev, 1)
                 + cw_ref[2:3, :] * u)
            yc = p_ref[b, :, C_CB:C_CB + BRANCH_WIDTH] * y * _silu(p_ref[b, :, C_CG:C_CG + BRANCH_WIDTH])
            br_ref[b, :, 2 * BRANCH_WIDTH:3 * BRANCH_WIDTH] = yc.astype(BF16)
            u_s[b] = u

    lp = nc * BLOCK
    smem = pl.BlockSpec(memory_space=pltpu.SMEM)
    br, states = pl.pallas_call(
        body, name="mixers_fwd",
        grid=(nc,),
        in_specs=[pl.BlockSpec((nb, BLOCK, ABC_WIDTH), lambda n: (0, n, 0)),
                  pl.BlockSpec((BLOCK, BLOCK), lambda n: (n, 0)),
                  pl.BlockSpec((BLOCK, BLOCK), lambda n: (n, 0)),
                  pl.BlockSpec((BLOCK, 2 * BLOCK), lambda n: (0, 0)),
                  smem, smem,
                  pl.BlockSpec((None, 3, BRANCH_WIDTH), lambda n: (layer, 0, 0))],
        out_specs=[pl.BlockSpec((nb, BLOCK, N_BRANCH * BRANCH_WIDTH), lambda n: (0, n, 0)),
                   pl.BlockSpec((nb, 1, RET_HEADS, BLOCK, BLOCK), lambda n: (0, n, 0, 0, 0))],
        out_shape=[jax.ShapeDtypeStruct((nb, lp, N_BRANCH * BRANCH_WIDTH), BF16),
                   jax.ShapeDtypeStruct((nb, nc, RET_HEADS, BLOCK, BLOCK), F32)],
        scratch_shapes=[pltpu.VMEM((ATT_HEADS, BLOCK, 2 * BLOCK), F32),
                        pltpu.VMEM((nb, 2 * BLOCK, 2 * BLOCK), F32),
                        pltpu.VMEM((nb, RET_HEADS, BLOCK, BLOCK), F32),
                        pltpu.VMEM((nb, BLOCK, BRANCH_WIDTH), F32)],
        compiler_params=_cparams("arbitrary"),
    )(proj.reshape(nb, lp, ABC_WIDTH), cosf, sinf, bkt, rel_bias, sinks, conv_w)
    return br.reshape(nb * lp, N_BRANCH * BRANCH_WIDTH), states


def mixers_bwd(proj, d_br, states, cosf, sinf, bkt, rel_bias, sinks, conv_w, layer, nb, nc):
    def body(p_ref, kvp_ref, cp_ref, dbr_ref, st_ref, cos_ref, sin_ref, bkt_ref, rb_ref, sk_ref, cw_ref,
             dp_ref, drb_ref, dsk_ref, dcw_ref,
             bias_s, dbias_s, dkv_s, g_s, dy_s):
        p_ref, kvp_ref, cp_ref, dbr_ref = [_Widened(r) for r in (p_ref, kvp_ref, cp_ref, dbr_ref)]
        step = pl.program_id(0)
        n = nc - 1 - step
        ex = range(nb)

        @pl.when(step == 0)
        def _():
            _build_bias(bkt_ref, rb_ref, bias_s)
            dbias_s[...] = jnp.zeros_like(dbias_s)
            dsk_ref[...] = jnp.zeros_like(dsk_ref)
            dcw_ref[...] = jnp.zeros_like(dcw_ref)
            drb_ref[...] = jnp.zeros_like(drb_ref)
            dkv_s[...] = jnp.zeros_like(dkv_s)
            g_s[...] = jnp.zeros_like(g_s)
            dy_s[...] = jnp.zeros_like(dy_s)

        valid = _valid_col(n)
        mask = _band_mask(n)
        has_prev = (n > 0).astype(F32)

        k_all, v_all = [], []
        for b in ex:
            kv_prev = kvp_ref[b] * has_prev
            kv_cur = p_ref[b, :, C_AK:C_AK + 2 * BLOCK]
            k_all.append(jnp.concatenate([kv_prev[:, 0:BLOCK], kv_cur[:, 0:BLOCK]], axis=0))
            v_all.append(jnp.concatenate([kv_prev[:, BLOCK:], kv_cur[:, BLOCK:]], axis=0))
        zero2 = jnp.zeros((2 * BLOCK, BLOCK), F32)
        dk_tot = [zero2 for _ in ex]
        dv_tot = [zero2 for _ in ex]
        for kh in range(2):
            ks = [[t.astype(BF16) for t in _split_heads(k_all[b], kh)] for b in ex]
            vs = [[t.astype(BF16) for t in _split_heads(v_all[b], kh)] for b in ex]
            pairs = [(b, 2 * kh + jj) for jj in range(2) for b in ex]
            subs = [(b, j, x) for (b, j) in pairs for x in range(2)]
            qb_, gate_, dya_, do2_ = {}, {}, {}, {}
            for w in pairs:
                b, j = w
                qb_[w] = (p_ref[b, :, C_AQ + BLOCK * j:C_AQ + BLOCK * (j + 1)] * ATT_SCALE).astype(BF16)
                gate_[w] = p_ref[b, :, C_AG + BLOCK * j:C_AG + BLOCK * (j + 1)]
                dya_[w] = dbr_ref[b, :, BLOCK * j:BLOCK * (j + 1)]
                do2_[w] = (dya_[w] * _silu(gate_[w])).astype(BF16)
            qk_ = {(b, j, x): _nt(qb_[(b, j)], ks[b][x]) for (b, j, x) in subs}
            dpm_ = {(b, j, x): _nt(do2_[(b, j)], vs[b][x]) for (b, j, x) in subs}
            pb_, dsb_ = {}, {}
            for u in subs:
                b, j, x = u
                h = 2 * j + x
                p, p_sink = _softmax_of(qk_[u], bias_s[h], mask, sk_ref[layer, h])
                pb_[u] = p.astype(BF16)
                delta = jnp.sum(p * dpm_[u], axis=-1, keepdims=True)
                ds = p * (dpm_[u] - delta)
                dbias_s[h] += ds
                dsk_ref[h:h + 1, :] += jnp.broadcast_to(
                    jnp.sum(-p_sink * delta, axis=0, keepdims=True), (1, BLOCK))
                dsb_[u] = ds.astype(BF16)
            o_ = {u: _nn(pb_[u], vs[u[0]][u[2]]) for u in subs}
            dq_ = {u: _nn(dsb_[u], ks[u[0]][u[2]]) for u in subs}
            dkm_ = {u: _tn(dsb_[u], qb_[(u[0], u[1])]) for u in subs}
            dvm_ = {u: _tn(pb_[u], do2_[(u[0], u[1])]) for u in subs}
            for w in pairs:
                b, j = w
                o2 = o_[(b, j, 0)] + o_[(b, j, 1)]
                dq2 = (dq_[(b, j, 0)] + dq_[(b, j, 1)]) * ATT_SCALE
                dp_ref[b, :, C_AQ + BLOCK * j:C_AQ + BLOCK * (j + 1)] = dq2.astype(BF16)
                dp_ref[b, :, C_AG + BLOCK * j:C_AG + BLOCK * (j + 1)] = (
                    dya_[w] * o2 * _dsilu(gate_[w])).astype(BF16)
            for b in ex:
                j0, j1 = 2 * kh, 2 * kh + 1
                dk_tot[b] = dk_tot[b] + _merge_heads(dkm_[(b, j0, 0)] + dkm_[(b, j1, 0)],
                                                     dkm_[(b, j0, 1)] + dkm_[(b, j1, 1)], kh)
                dv_tot[b] = dv_tot[b] + _merge_heads(dvm_[(b, j0, 0)] + dvm_[(b, j1, 0)],
                                                     dvm_[(b, j0, 1)] + dvm_[(b, j1, 1)], kh)
        for b in ex:
            dp_ref[b, :, C_AK:C_AK + BLOCK] = (dk_tot[b][BLOCK:, :] + dkv_s[b, :, 0:BLOCK]).astype(BF16)
            dp_ref[b, :, C_AV:C_AV + BLOCK] = (dv_tot[b][BLOCK:, :] + dkv_s[b, :, BLOCK:]).astype(BF16)
            dkv_s[b, :, 0:BLOCK] = dk_tot[b][0:BLOCK, :]
            dkv_s[b, :, BLOCK:] = dv_tot[b][0:BLOCK, :]

        cosv = cos_ref[...]
        sinv = sin_ref[...]
        tabs = [_decay_tables(h) for h in range(RET_HEADS)]
        units = [(b, h) for h in range(RET_HEADS) for b in ex]
        sl = lambda c0, h: slice(c0 + BLOCK * h, c0 + BLOCK * (h + 1))
        q_, k_, v_, vb_, sp_ = {}, {}, {}, {}, {}
        for u in units:
            b, h = u
            q_[u] = _rot(p_ref[b, :, sl(C_RQ, h)], cosv, sinv).astype(BF16)
            k_[u] = (_rot(p_ref[b, :, sl(C_RK, h)], cosv, sinv) * RET_SCALE * valid).astype(BF16)
            v_[u] = p_ref[b, :, sl(C_RV, h)]
            vb_[u] = v_[u].astype(BF16)
            sp_[u] = st_ref[b, 0, h].astype(BF16)
        qk_ = {u: _nt(q_[u], k_[u]) for u in units}
        qs_ = {u: _nn(q_[u], sp_[u]) for u in units}
        a_ = {u: (qk_[u] * tabs[u[1]][0]).astype(BF16) for u in units}
        av_ = {u: _nn(a_[u], vb_[u]) for u in units}
        dob_, dxo_ = {}, {}
        for u in units:
            b, h = u
            xi = tabs[h][2]
            o = av_[u] + xi * qs_[u]
            mu = jnp.mean(o, axis=-1, keepdims=True)
            var = jnp.mean(jnp.square(o - mu), axis=-1, keepdims=True)
            rstd = lax.rsqrt(var + GN_EPS)
            oh = (o - mu) * rstd
            gate = p_ref[b, :, sl(C_RG, h)]
            d_yr = dbr_ref[b, :, BRANCH_WIDTH + BLOCK * h:BRANCH_WIDTH + BLOCK * (h + 1)]
            dp_ref[b, :, sl(C_RG, h)] = (d_yr * oh * _dsilu(gate)).astype(BF16)
            doh = d_yr * _silu(gate)
            do = rstd * (doh - jnp.mean(doh, axis=-1, keepdims=True)
                         - oh * jnp.mean(doh * oh, axis=-1, keepdims=True))
            dob_[u] = do.astype(BF16)
            dxo_[u] = (do * xi).astype(BF16)
        dov_ = {u: _nt(dob_[u], vb_[u]) for u in units}
        dv1_ = {u: _tn(a_[u], dob_[u]) for u in units}
        dq1_ = {u: _nt(dxo_[u], sp_[u]) for u in units}
        gq_ = {u: _tn(q_[u], dxo_[u]) for u in units}
        da_, gb_, zv_ = {}, {}, {}
        for u in units:
            b, h = u
            da_[u] = (dov_[u] * tabs[h][0]).astype(BF16)
            g_next = g_s[b, h]
            gb_[u] = g_next.astype(BF16)
            zv_[u] = (v_[u] * tabs[h][1]).astype(BF16)
            g_s[b, h] = tabs[h][3] * g_next + gq_[u]
        dq2_ = {u: _nn(da_[u], k_[u]) for u in units}
        dk1_ = {u: _tn(da_[u], q_[u]) for u in units}
        dk2_ = {u: _nt(zv_[u], gb_[u]) for u in units}
        dv2_ = {u: _nn(k_[u], gb_[u]) for u in units}
        for u in units:
            b, h = u
            dp_ref[b, :, sl(C_RQ, h)] = _rot_t(dq2_[u] + dq1_[u], cosv, sinv).astype(BF16)
            dp_ref[b, :, sl(C_RK, h)] = _rot_t((dk1_[u] + dk2_[u]) * (RET_SCALE * valid), cosv, sinv).astype(BF16)
            dp_ref[b, :, sl(C_RV, h)] = (dv1_[u] + tabs[h][1] * dv2_[u]).astype(BF16)

        w0, w1, w2 = cw_ref[0:1, :], cw_ref[1:2, :], cw_ref[2:3, :]
        for b in ex:
            cb = p_ref[b, :, C_CB:C_CB + BRANCH_WIDTH]
            cc = p_ref[b, :, C_CC:C_CC + BRANCH_WIDTH]
            cx = p_ref[b, :, C_CX:C_CX + BRANCH_WIDTH]
            cg = p_ref[b, :, C_CG:C_CG + BRANCH_WIDTH]
            u = cc * cx * valid
            u_prev = (cp_ref[b, :, 0:BRANCH_WIDTH] * cp_ref[b, :, BRANCH_WIDTH:2 * BRANCH_WIDTH]
                      * (_valid_col(n - 1) * has_prev))
            u1 = _shift_down(u, u_prev, 1)
            u2 = _shift_down(u, u_prev, 2)
            y = w0 * u2 + w1 * u1 + w2 * u
            d_yc = dbr_ref[b, :, 2 * BRANCH_WIDTH:3 * BRANCH_WIDTH]
            sg = _silu(cg)
            dp_ref[b, :, C_CB:C_CB + BRANCH_WIDTH] = (d_yc * y * sg).astype(BF16)
            dp_ref[b, :, C_CG:C_CG + BRANCH_WIDTH] = (d_yc * cb * y * _dsilu(cg)).astype(BF16)
            dy = d_yc * cb * sg
            dy_next = dy_s[b]
            du = (w2 * dy + w1 * _shift_up(dy, dy_next, 1) + w0 * _shift_up(dy, dy_next, 2)) * valid
            dp_ref[b, :, C_CC:C_CC + BRANCH_WIDTH] = (du * cx).astype(BF16)
            dp_ref[b, :, C_CX:C_CX + BRANCH_WIDTH] = (du * cc).astype(BF16)
            dcw_ref[0:1, :] += jnp.sum(dy * u2, axis=0, keepdims=True)
            dcw_ref[1:2, :] += jnp.sum(dy * u1, axis=0, keepdims=True)
            dcw_ref[2:3, :] += jnp.sum(dy * u, axis=0, keepdims=True)
            dy_s[b] = dy

        @pl.when(step == nc - 1)
        def _():
            bkt = bkt_ref[...]
            row = lax.broadcasted_iota(jnp.int32, (N_BUCKETS, BLOCK), 0)
            lane = lax.broadcasted_iota(jnp.int32, (N_BUCKETS, BLOCK), 1)

            def one_bucket(bk, acc):
                sel = bkt == bk
                for h in range(ATT_HEADS):
                    t = jnp.where(sel, dbias_s[h], 0.0)
                    s = jnp.sum(jnp.sum(t, axis=1, keepdims=True), axis=0, keepdims=True)
                    acc = acc + jnp.where((row == bk) & (lane == h), jnp.broadcast_to(s, acc.shape), 0.0)
                return acc

            drb_ref[...] = lax.fori_loop(0, N_BUCKETS, one_bucket, jnp.zeros((N_BUCKETS, BLOCK), F32))

    lp = nc * BLOCK
    smem = pl.BlockSpec(memory_space=pltpu.SMEM)
    blk = lambda s: nc - 1 - s
    prev = lambda s: jnp.maximum(nc - 2 - s, 0)
    proj3 = proj.reshape(nb, lp, ABC_WIDTH)
    res = pl.pallas_call(
        body, name="mixers_bwd",
        grid=(nc,),
        in_specs=[pl.BlockSpec((nb, BLOCK, ABC_WIDTH), lambda s: (0, blk(s), 0)),
                  pl.BlockSpec((nb, BLOCK, 2 * BLOCK), lambda s: (0, prev(s), C_AK // (2 * BLOCK))),
                  pl.BlockSpec((nb, BLOCK, 1280), lambda s: (0, prev(s), C_CC // 1280)),
                  pl.BlockSpec((nb, BLOCK, N_BRANCH * BRANCH_WIDTH), lambda s: (0, blk(s), 0)),
                  pl.BlockSpec((nb, 1, RET_HEADS, BLOCK, BLOCK), lambda s: (0, blk(s), 0, 0, 0)),
                  pl.BlockSpec((BLOCK, BLOCK), lambda s: (blk(s), 0)),
                  pl.BlockSpec((BLOCK, BLOCK), lambda s: (blk(s), 0)),
                  pl.BlockSpec((BLOCK, 2 * BLOCK), lambda s: (0, 0)),
                  smem, smem,
                  pl.BlockSpec((None, 3, BRANCH_WIDTH), lambda s: (layer, 0, 0))],
        out_specs=[pl.BlockSpec((nb, BLOCK, ABC_WIDTH), lambda s: (0, blk(s), 0)),
                   pl.BlockSpec((N_BUCKETS, BLOCK), lambda s: (0, 0)),
                   pl.BlockSpec((ATT_HEADS, BLOCK), lambda s: (0, 0)),
                   pl.BlockSpec((8, BRANCH_WIDTH), lambda s: (0, 0))],
        out_shape=[jax.ShapeDtypeStruct((nb, lp, ABC_WIDTH), BF16),
                   jax.ShapeDtypeStruct((N_BUCKETS, BLOCK), F32),
                   jax.ShapeDtypeStruct((ATT_HEADS, BLOCK), F32),
                   jax.ShapeDtypeStruct((8, BRANCH_WIDTH), F32)],
        scratch_shapes=[pltpu.VMEM((ATT_HEADS, BLOCK, 2 * BLOCK), F32),
                        pltpu.VMEM((ATT_HEADS, BLOCK, 2 * BLOCK), F32),
                        pltpu.VMEM((nb, BLOCK, 2 * BLOCK), F32),
                        pltpu.VMEM((nb, RET_HEADS, BLOCK, BLOCK), F32),
                        pltpu.VMEM((nb, BLOCK, BRANCH_WIDTH), F32)],
        compiler_params=_cparams("arbitrary"),
    )(proj3, proj3, proj3, d_br.reshape(nb, lp, N_BRANCH * BRANCH_WIDTH), states, cosf, sinf, bkt, rel_bias, sinks,
      conv_w)
    return (res[0].reshape(nb * lp, ABC_WIDTH),) + tuple(res[1:])


MERGE_TILE = 256
MERGE_FWD_TILE = 544


def _merge_forward(br_ref, m_ref, wb_ref, wo_ref):
    bo, gates = [], []
    mixed_pre = None
    for g in range(N_BRANCH):
        br_g = br_ref[:, BRANCH_WIDTH * g:BRANCH_WIDTH * (g + 1)]
        bo_g = jnp.concatenate([_nn(br_g, wb_ref[p, g]) for p in range(N_CHIPS)], axis=1)
        gate_g = _sigmoid(m_ref[:, D_MODEL * g:D_MODEL * (g + 1)].astype(F32))
        bo.append(bo_g)
        gates.append(gate_g)
        mixed_pre = gate_g * bo_g if mixed_pre is None else mixed_pre + gate_g * bo_g
    mixed = _nn(mixed_pre.astype(BF16), wo_ref[...])
    r = lax.rsqrt(jnp.mean(mixed * mixed, axis=-1, keepdims=True) + RMS_EPS)
    return bo, gates, mixed_pre, mixed, r


def merge_fwd(x2d, br, pm, wb, wo, g_post, layer, after):
    t = x2d.shape[0]
    tm = MERGE_FWD_TILE if t % MERGE_FWD_TILE == 0 else BLOCK

    def body(x_ref, br_ref, m_ref, wb_ref, wo_ref, g_ref, after_ref, o_ref):
        _, _, _, mixed, r = _merge_forward(br_ref, m_ref, wb_ref, wo_ref)
        o_ref[...] = x_ref[...] + mixed * r * g_ref[...]

    return pl.pallas_call(
        body, name="merge_fwd",
        grid=(t // tm,),
        in_specs=[pl.BlockSpec((tm, D_MODEL), lambda i: (i, 0)),
                  pl.BlockSpec((tm, N_BRANCH * BRANCH_WIDTH), lambda i: (i, 0)),
                  pl.BlockSpec((tm, MERGE_WIDTH), lambda i: (i, 0)),
                  pl.BlockSpec((N_CHIPS, N_BRANCH, BRANCH_WIDTH, SHARD_D), lambda i: (0, 0, 0, 0)),
                  pl.BlockSpec((D_MODEL, D_MODEL), lambda i: (0, 0)),
                  pl.BlockSpec((None, 1, D_MODEL), lambda i: (layer, 0, 0)),
                  ANY],
        out_specs=pl.BlockSpec((tm, D_MODEL), lambda i: (i, 0)),
        out_shape=jax.ShapeDtypeStruct((t, D_MODEL), F32),
        compiler_params=_cparams("parallel"),
    )(x2d, br, pm, wb, wo, g_post, after)


def merge_fwd_loss(x2d, br, pm, wb, wo, g_post, layer, target, lp):
    t = x2d.shape[0]
    tm = MERGE_FWD_TILE if t % MERGE_FWD_TILE == 0 else BLOCK

    def body(x_ref, br_ref, m_ref, wb_ref, wo_ref, g_ref, t_ref, l_ref, d_ref):
        i = pl.program_id(0)

        @pl.when(i == 0)
        def _():
            l_ref[...] = jnp.zeros_like(l_ref)

        _, _, _, mixed, r = _merge_forward(br_ref, m_ref, wb_ref, wo_ref)
        y = x_ref[...] + mixed * r * g_ref[...]
        row = i * tm + lax.broadcasted_iota(jnp.int32, (tm, 1), 0)
        e = jnp.where(row % lp >= BLOCK, y - t_ref[...], 0.0)
        d_ref[...] = e * (1.0 / D_MODEL)
        s = jnp.sum(jnp.sum(e * e, axis=1, keepdims=True), axis=0, keepdims=True)
        l_ref[...] += jnp.broadcast_to(s * (0.5 / D_MODEL), l_ref.shape)

    return pl.pallas_call(
        body, name="merge_fwd_loss",
        grid=(t // tm,),
        in_specs=[pl.BlockSpec((tm, D_MODEL), lambda i: (i, 0)),
                  pl.BlockSpec((tm, N_BRANCH * BRANCH_WIDTH), lambda i: (i, 0)),
                  pl.BlockSpec((tm, MERGE_WIDTH), lambda i: (i, 0)),
                  pl.BlockSpec((N_CHIPS, N_BRANCH, BRANCH_WIDTH, SHARD_D), lambda i: (0, 0, 0, 0)),
                  pl.BlockSpec((D_MODEL, D_MODEL), lambda i: (0, 0)),
                  pl.BlockSpec((None, 1, D_MODEL), lambda i: (layer, 0, 0)),
                  pl.BlockSpec((tm, D_MODEL), lambda i: (i, 0))],
        out_specs=[pl.BlockSpec((8, BLOCK), lambda i: (0, 0)),
                   pl.BlockSpec((tm, D_MODEL), lambda i: (i, 0))],
        out_shape=[jax.ShapeDtypeStruct((8, BLOCK), F32),
                   jax.ShapeDtypeStruct((t, D_MODEL), F32)],
        compiler_params=_cparams("arbitrary"),
    )(x2d, br, pm, wb, wo, g_post, target)


def merge_bwd(d_out, br, pm, wb, wo, g_post, layer, after):
    t = d_out.shape[0]
    tm = MERGE_TILE if t % MERGE_TILE == 0 else BLOCK

    def body(do_ref, br_ref, m_ref, wb_ref, wo_ref, g_ref, after_ref, dbr_ref, dm_ref, dg_ref, dwb_ref, dwo_ref):

        @pl.when(pl.program_id(0) == 0)
        def _():
            dwb_ref[...] = jnp.zeros_like(dwb_ref)
            dwo_ref[...] = jnp.zeros_like(dwo_ref)
            dg_ref[...] = jnp.zeros_like(dg_ref)

        bo, gates, mixed_pre, mixed, r = _merge_forward(br_ref, m_ref, wb_ref, wo_ref)
        d_o = do_ref[...]
        nh = mixed * r
        dg_ref[0:1, :] += jnp.sum(d_o * nh, axis=0, keepdims=True)
        dn = d_o * g_ref[...]
        d_mixed = (r * (dn - nh * jnp.mean(dn * nh, axis=-1, keepdims=True))).astype(BF16)
        dwo_ref[...] += _tn(mixed_pre.astype(BF16), d_mixed)
        d_pre = _nt(d_mixed, wo_ref[...])
        for g in range(N_BRANCH):
            br_g = br_ref[:, BRANCH_WIDTH * g:BRANCH_WIDTH * (g + 1)]
            d_bo = (d_pre * gates[g]).astype(BF16)
            dm_ref[:, D_MODEL * g:D_MODEL * (g + 1)] = (
                d_pre * bo[g] * gates[g] * (1.0 - gates[g])).astype(BF16)
            d_br_g = None
            for p in range(N_CHIPS):
                d_bo_p = d_bo[:, SHARD_D * p:SHARD_D * (p + 1)]
                part = _nt(d_bo_p, wb_ref[p, g])
                d_br_g = part if d_br_g is None else d_br_g + part
                dwb_ref[p, g] += _tn(br_g, d_bo_p)
            dbr_ref[:, BRANCH_WIDTH * g:BRANCH_WIDTH * (g + 1)] = d_br_g.astype(BF16)

    return pl.pallas_call(
        body, name="merge_bwd",
        grid=(t // tm,),
        in_specs=[pl.BlockSpec((tm, D_MODEL), lambda i: (i, 0)),
                  pl.BlockSpec((tm, N_BRANCH * BRANCH_WIDTH), lambda i: (i, 0)),
                  pl.BlockSpec((tm, MERGE_WIDTH), lambda i: (i, 0)),
                  pl.BlockSpec((N_CHIPS, N_BRANCH, BRANCH_WIDTH, SHARD_D), lambda i: (0, 0, 0, 0)),
                  pl.BlockSpec((D_MODEL, D_MODEL), lambda i: (0, 0)),
                  pl.BlockSpec((None, 1, D_MODEL), lambda i: (layer, 0, 0)),
                  ANY],
        out_specs=[pl.BlockSpec((tm, N_BRANCH * BRANCH_WIDTH), lambda i: (i, 0)),
                   pl.BlockSpec((tm, MERGE_WIDTH), lambda i: (i, 0)),
                   pl.BlockSpec((8, D_MODEL), lambda i: (0, 0)),
                   pl.BlockSpec((N_CHIPS, N_BRANCH, BRANCH_WIDTH, SHARD_D), lambda i: (0, 0, 0, 0)),
                   pl.BlockSpec((D_MODEL, D_MODEL), lambda i: (0, 0))],
        out_shape=[jax.ShapeDtypeStruct((t, N_BRANCH * BRANCH_WIDTH), BF16),
                   jax.ShapeDtypeStruct((t, MERGE_WIDTH), BF16),
                   jax.ShapeDtypeStruct((8, D_MODEL), F32),
                   jax.ShapeDtypeStruct((N_CHIPS, N_BRANCH, BRANCH_WIDTH, SHARD_D), F32),
                   jax.ShapeDtypeStruct((D_MODEL, D_MODEL), F32)],
        compiler_params=_cparams("arbitrary"),
    )(d_out, br, pm, wb, wo, g_post, after)


N_ABC_TILES = ABC_WIDTH // COL_TILE
N_M_TILES = MERGE_WIDTH // COL_TILE


def proj_dgrad(d_abc, d_m, w, x2d, g, layer, d_out, after):
    t = x2d.shape[0]
    tm = ROW_TILE if t % ROW_TILE == 0 else BLOCK
    nk = N_ABC_TILES + N_M_TILES

    def body(da_ref, dm_ref, w_ref, x_ref, g_ref, do_ref, after_ref, dx_ref, dg_ref, acc):
        i = pl.program_id(0)
        k = pl.program_id(1)

        @pl.when((i == 0) & (k == 0))
        def _():
            dg_ref[...] = jnp.zeros_like(dg_ref)

        @pl.when(k == 0)
        def _():
            acc[...] = jnp.zeros_like(acc)

        @pl.when(k < N_ABC_TILES)
        def _():
            acc[...] += _nn(da_ref[...], w_ref[...])

        @pl.when(k >= N_ABC_TILES)
        def _():
            acc[...] += _nn(dm_ref[...], w_ref[...])

        @pl.when(k == nk - 1)
        def _():
            x = x_ref[...]
            r = lax.rsqrt(jnp.mean(x * x, axis=-1, keepdims=True) + RMS_EPS)
            nh = x * r
            dh = acc[...]
            dg_ref[0:1, :] += jnp.sum(dh * nh, axis=0, keepdims=True)
            dn = dh * g_ref[...]
            dx_ref[...] = do_ref[...] + r * (dn - nh * jnp.mean(dn * nh, axis=-1, keepdims=True))

    return pl.pallas_call(
        body, name="proj_dgrad",
        grid=(t // tm, nk),
        in_specs=[pl.BlockSpec((tm, COL_TILE), lambda i, k: (i, jnp.minimum(k, N_ABC_TILES - 1))),
                  pl.BlockSpec((tm, COL_TILE), lambda i, k: (i, jnp.maximum(k - N_ABC_TILES, 0))),
                  pl.BlockSpec((COL_TILE, D_MODEL), lambda i, k: (k, 0)),
                  pl.BlockSpec((tm, D_MODEL), lambda i, k: (i, 0)),
                  pl.BlockSpec((None, 1, D_MODEL), lambda i, k: (layer, 0, 0)),
                  pl.BlockSpec((tm, D_MODEL), lambda i, k: (i, 0)),
                  ANY],
        out_specs=[pl.BlockSpec((tm, D_MODEL), lambda i, k: (i, 0)),
                   pl.BlockSpec((8, D_MODEL), lambda i, k: (0, 0))],
        out_shape=[jax.ShapeDtypeStruct((t, D_MODEL), F32),
                   jax.ShapeDtypeStruct((8, D_MODEL), F32)],
        scratch_shapes=[pltpu.VMEM((tm, D_MODEL), F32)],
        compiler_params=_cparams("arbitrary", "arbitrary"),
    )(d_abc, d_m, w, x2d, g, d_out, after)


def proj_wgrad(hb, d_abc, d_m):
    t = hb.shape[0]
    nj = N_ABC_TILES + N_M_TILES

    def body(h_ref, da_ref, dm_ref, o_ref):
        j = pl.program_id(0)

        @pl.when(j < N_ABC_TILES)
        def _():
            o_ref[...] = _tn(da_ref[...], h_ref[...])

        @pl.when(j >= N_ABC_TILES)
        def _():
            o_ref[...] = _tn(dm_ref[...], h_ref[...])

    return pl.pallas_call(
        body, name="proj_wgrad",
        grid=(nj,),
        in_specs=[pl.BlockSpec((t, D_MODEL), lambda j: (0, 0)),
                  pl.BlockSpec((t, COL_TILE), lambda j: (0, jnp.minimum(j, N_ABC_TILES - 1))),
                  pl.BlockSpec((t, COL_TILE), lambda j: (0, jnp.maximum(j - N_ABC_TILES, 0)))],
        out_specs=pl.BlockSpec((COL_TILE, D_MODEL), lambda j: (j, 0)),
        out_shape=jax.ShapeDtypeStruct((PROJ_WIDTH, D_MODEL), F32),
        compiler_params=_cparams("arbitrary"),
    )(hb, d_abc, d_m)


def _adamw_math(w, g, m, v):
    m = ADAM_B1 * m + (1.0 - ADAM_B1) * g
    v = ADAM_B2 * v + (1.0 - ADAM_B2) * jnp.square(g)
    m_hat = m / (1.0 - ADAM_B1 ** ADAM_STEP)
    v_hat = v / (1.0 - ADAM_B2 ** ADAM_STEP)
    delta = -ADAM_LR * (m_hat / (jnp.sqrt(v_hat) + ADAM_EPS) + ADAM_WD * w)
    return delta, m, v


def adamw_layer(w, g, m, v, layer, acc, after):
    _, r, c = w.shape
    tr = _row_tile(r)

    def body(*refs):
        w_ref, g_ref, m_ref, v_ref = refs[:4]
        go_ref, d_ref, mo_ref, vo_ref = refs[-4:]
        g_val = g_ref[...]
        d, m_new, v_new = _adamw_math(w_ref[...], g_val, m_ref[...], v_ref[...])
        go_ref[...] = g_val
        d_ref[...] = d
        mo_ref[...] = m_new
        vo_ref[...] = v_new

    slab = pl.BlockSpec((None, tr, c), lambda i: (layer, i, 0))
    ins = [w, g, m, v, after]
    in_specs = [slab, pl.BlockSpec((tr, c), lambda i: (i, 0)), slab, slab, ANY]
    aliases = {}
    if acc is not None:
        ins += list(acc)
        in_specs += [ANY] * 4
        aliases = {5 + i: i for i in range(4)}
    return pl.pallas_call(
        body, name="adamw_layer",
        grid=(r // tr,),
        in_specs=in_specs, out_specs=[slab] * 4,
        out_shape=[jax.ShapeDtypeStruct(w.shape, F32)] * 4,
        input_output_aliases=aliases,
        compiler_params=_cparams("parallel"),
    )(*ins)


def adamw_small(params):
    k = len(params)

    def body(*refs):
        ins, outs = refs[:4 * k], refs[4 * k:]
        for i in range(k):
            d, m_new, v_new = _adamw_math(*[r[...] for r in ins[4 * i:4 * i + 4]])
            outs[3 * i][...] = d
            outs[3 * i + 1][...] = m_new
            outs[3 * i + 2][...] = v_new

    flat = [a for p in params for a in p]
    vm = pl.BlockSpec(memory_space=pltpu.VMEM)
    out_shape = [jax.ShapeDtypeStruct(p[0].shape, F32) for p in params for _ in range(3)]
    res = pl.pallas_call(
        body, name="adamw_small",
        in_specs=[vm] * len(flat), out_specs=[vm] * len(out_shape), out_shape=out_shape,
    )(*flat)
    return [tuple(res[3 * i:3 * i + 3]) for i in range(k)]


ANY = pl.BlockSpec(memory_space=pl.ANY)


def _place():
    return lax.axis_index("x"), lax.axis_index("y"), lax.axis_index("c")


HBM = pl.BlockSpec(memory_space=pltpu.HBM)
SEM = pl.BlockSpec(memory_space=pltpu.SEMAPHORE)
EFFECT = pltpu.SideEffectType.DATAFLOW_SIDE_EFFECTING


def _other_chips(x, y):
    return [(1 - x, y), (x, 1 - y), (1 - x, 1 - y)]


def _own_slot(shard, chip):
    buf = lax.empty((N_CHIPS,) + shard.shape, shard.dtype)
    return lax.dynamic_update_slice(buf, shard[None], (chip, 0, 0, 0))


def _hbm(a):
    return pltpu.with_memory_space_constraint(a, pltpu.HBM)


def gather_start(bufs, after):
    n = len(bufs)

    def body(*refs):
        g_refs = refs[:n]
        send_sems, recv_sems = refs[n + 1], refs[n + 2]
        token = refs[-1]
        x, y, c = _place()
        me_p = 2 * x + y
        for t in range(n):
            for k, (qx, qy) in enumerate(_other_chips(x, y)):
                slab = g_refs[t].at[me_p, c]
                pltpu.make_async_remote_copy(src_ref=slab, dst_ref=slab, send_sem=send_sems.at[3 * t + k],
                                             recv_sem=recv_sems.at[3 * t + k], device_id=(qx, qy, c),
                                             device_id_type=MESH).start()
        token[...] = jnp.zeros_like(token)

    res = pl.pallas_call(
        body, name="gather_start",
        in_specs=[HBM] * n + [ANY],
        out_specs=[SEM, SEM] + [HBM] * n + [pl.BlockSpec(memory_space=pltpu.VMEM)],
        out_shape=[pltpu.SemaphoreType.DMA((3 * n,)), pltpu.SemaphoreType.DMA((3 * n,))]
        + [pltpu.HBM(b.shape, b.dtype) for b in bufs] + [jax.ShapeDtypeStruct((8, LANES), F32)],
        input_output_aliases={t: 2 + t for t in range(n)},
        compiler_params=pltpu.CompilerParams(has_side_effects=EFFECT),
    )(*[_hbm(b) for b in bufs], after)
    return res[0], res[1], list(res[2:2 + n]), res[-1]


def gather_wait(bufs, send_sems, recv_sems, after, first=0):
    n = len(bufs)

    def body(*refs):
        g_refs = refs[:n]
        send_sems, recv_sems = refs[n], refs[n + 1]
        x, y, c = _place()
        me_p = 2 * x + y
        for t in range(n):
            for k, (qx, qy) in enumerate(_other_chips(x, y)):
                s = 3 * (first + t) + k
                cp = pltpu.make_async_remote_copy(src_ref=g_refs[t].at[me_p, c], dst_ref=g_refs[t].at[2 * qx + qy, c],
                                                  send_sem=send_sems.at[s], recv_sem=recv_sems.at[s],
                                                  device_id=(qx, qy, c), device_id_type=MESH)
                cp.wait_send()
                cp.wait_recv()

    return pl.pallas_call(
        body, name="gather_wait",
        in_specs=[HBM] * n + [SEM, SEM, ANY],
        out_specs=[HBM] * n,
        out_shape=[pltpu.HBM(b.shape, b.dtype) for b in bufs],
        input_output_aliases={t: t for t in range(n)},
        compiler_params=pltpu.CompilerParams(has_side_effects=EFFECT),
    )(*bufs, send_sems, recv_sems, after)


def gather_forward(bufs):
    n = len(bufs)

    def body(*refs):
        g_refs = refs[n:2 * n]
        send_sems, recv_sems = refs[2 * n:]
        x, y, c = _place()
        sibling = (x, y, 1 - c)
        chips = _other_chips(x, y)
        passed = []
        for t in range(n):
            for k, (qx, qy) in enumerate(chips):
                slab = g_refs[t].at[2 * qx + qy, c]
                fwd = pltpu.make_async_remote_copy(src_ref=slab, dst_ref=slab, send_sem=send_sems.at[3 * t + k],
                                                   recv_sem=recv_sems.at[3 * t + k], device_id=sibling,
                                                   device_id_type=MESH)
                fwd.start()
                passed.append(fwd)
        for t in range(n):
            for k, (qx, qy) in enumerate(chips):
                slab = g_refs[t].at[2 * qx + qy, 1 - c]
                pltpu.make_async_remote_copy(src_ref=slab, dst_ref=slab, send_sem=send_sems.at[3 * t + k],
                                             recv_sem=recv_sems.at[3 * t + k], device_id=sibling,
                                             device_id_type=MESH).wait_recv()
        for cp in passed:
            cp.wait_send()

    return pl.pallas_call(
        body, name="gather_forward",
        in_specs=[ANY] * n, out_specs=[ANY] * n,
        out_shape=[jax.ShapeDtypeStruct(b.shape, b.dtype) for b in bufs],
        input_output_aliases={t: t for t in range(n)},
        scratch_shapes=[pltpu.SemaphoreType.DMA((3 * n,)), pltpu.SemaphoreType.DMA((3 * n,))],
    )(*bufs)


def forward_start(bufs):
    n = len(bufs)

    def body(*refs):
        g_refs = refs[:n]
        send_sems, recv_sems = refs[n], refs[n + 1]
        token = refs[-1]
        x, y, c = _place()
        for t in range(n):
            for k, (qx, qy) in enumerate(_other_chips(x, y)):
                slab = g_refs[t].at[2 * qx + qy, c]
                pltpu.make_async_remote_copy(src_ref=slab, dst_ref=slab, send_sem=send_sems.at[3 * t + k],
                                             recv_sem=recv_sems.at[3 * t + k], device_id=(x, y, 1 - c),
                                             device_id_type=MESH).start()
        token[...] = jnp.zeros_like(token)

    res = pl.pallas_call(
        body, name="forward_start",
        in_specs=[HBM] * n,
        out_specs=[SEM, SEM] + [HBM] * n + [pl.BlockSpec(memory_space=pltpu.VMEM)],
        out_shape=[pltpu.SemaphoreType.DMA((3 * n,)), pltpu.SemaphoreType.DMA((3 * n,))]
        + [pltpu.HBM(b.shape, b.dtype) for b in bufs] + [jax.ShapeDtypeStruct((8, LANES), F32)],
        input_output_aliases={t: 2 + t for t in range(n)},
        compiler_params=pltpu.CompilerParams(has_side_effects=EFFECT),
    )(*[_hbm(b) for b in bufs])
    return res[0], res[1], list(res[2:2 + n]), res[-1]


def forward_wait(bufs, send_sems, recv_sems, after):
    n = len(bufs)

    def body(*refs):
        g_refs = refs[:n]
        send_sems, recv_sems = refs[n], refs[n + 1]
        x, y, c = _place()
        for t in range(n):
            for k, (qx, qy) in enumerate(_other_chips(x, y)):
                cp = pltpu.make_async_remote_copy(src_ref=g_refs[t].at[2 * qx + qy, c],
                                                  dst_ref=g_refs[t].at[2 * qx + qy, 1 - c],
                                                  send_sem=send_sems.at[3 * t + k], recv_sem=recv_sems.at[3 * t + k],
                                                  device_id=(x, y, 1 - c), device_id_type=MESH)
                cp.wait_send()
                cp.wait_recv()

    return pl.pallas_call(
        body, name="forward_wait",
        in_specs=[HBM] * n + [SEM, SEM, ANY],
        out_specs=[HBM] * n,
        out_shape=[pltpu.HBM(b.shape, b.dtype) for b in bufs],
        input_output_aliases={t: t for t in range(n)},
        compiler_params=pltpu.CompilerParams(has_side_effects=EFFECT),
    )(*bufs, send_sems, recv_sems, after)


def small_start(pack, me, after):
    buf = lax.dynamic_update_slice(lax.empty((8,) + pack.shape, pack.dtype), pack[None], (me, 0, 0))

    def body(b_ref, after_ref, send_sems, recv_sems, thru, token):
        x, y, c = _place()
        slot = b_ref.at[4 * x + 2 * y + c]
        for k in range(1, 8):
            peer = (x ^ ((k >> 2) & 1), y ^ ((k >> 1) & 1), c ^ (k & 1))
            pltpu.make_async_remote_copy(src_ref=slot, dst_ref=slot, send_sem=send_sems.at[k - 1],
                                         recv_sem=recv_sems.at[k - 1], device_id=peer, device_id_type=MESH).start()
        token[...] = jnp.zeros_like(token)

    return pl.pallas_call(
        body, name="small_start",
        in_specs=[HBM, ANY],
        out_specs=[SEM, SEM, HBM, pl.BlockSpec(memory_space=pltpu.VMEM)],
        out_shape=[pltpu.SemaphoreType.DMA((7,)), pltpu.SemaphoreType.DMA((7,)), pltpu.HBM(buf.shape, buf.dtype),
                   jax.ShapeDtypeStruct((8, LANES), F32)],
        input_output_aliases={0: 2},
        compiler_params=pltpu.CompilerParams(has_side_effects=EFFECT),
    )(_hbm(buf), after)


def small_wait(buf, send_sems, recv_sems, after):
    def body(b_ref, send_sems, recv_sems, after_ref, thru):
        x, y, c = _place()
        mine = b_ref.at[4 * x + 2 * y + c]
        for k in range(1, 8):
            peer = (x ^ ((k >> 2) & 1), y ^ ((k >> 1) & 1), c ^ (k & 1))
            cp = pltpu.make_async_remote_copy(src_ref=mine, dst_ref=b_ref.at[4 * peer[0] + 2 * peer[1] + peer[2]],
                                              send_sem=send_sems.at[k - 1], recv_sem=recv_sems.at[k - 1],
                                              device_id=peer, device_id_type=MESH)
            cp.wait_send()
            cp.wait_recv()

    return pl.pallas_call(
        body, name="small_wait",
        in_specs=[HBM, SEM, SEM, ANY], out_specs=HBM,
        out_shape=pltpu.HBM(buf.shape, buf.dtype),
        input_output_aliases={0: 0},
        compiler_params=pltpu.CompilerParams(has_side_effects=EFFECT),
    )(buf, send_sems, recv_sems, after)


def swap_start(grads, after):
    n = len(grads)

    def body(*refs):
        g_refs, l_refs = refs[:n], refs[n:2 * n]
        send_sems, recv_sems = refs[2 * n + 1], refs[2 * n + 2]
        token = refs[-1]
        x, y, c = _place()
        for t in range(n):
            for p in range(N_CHIPS):
                pltpu.make_async_remote_copy(src_ref=g_refs[t].at[p, 1 - c], dst_ref=l_refs[t].at[p],
                                             send_sem=send_sems.at[N_CHIPS * t + p],
                                             recv_sem=recv_sems.at[N_CHIPS * t + p],
                                             device_id=(x, y, 1 - c), device_id_type=MESH).start()
        token[...] = jnp.zeros_like(token)

    lands = [lax.empty((N_CHIPS,) + g.shape[2:], g.dtype) for g in grads]
    res = pl.pallas_call(
        body, name="swap_start",
        in_specs=[HBM] * (2 * n) + [ANY],
        out_specs=[SEM, SEM] + [HBM] * (2 * n) + [pl.BlockSpec(memory_space=pltpu.VMEM)],
        out_shape=[pltpu.SemaphoreType.DMA((N_CHIPS * n,)), pltpu.SemaphoreType.DMA((N_CHIPS * n,))]
        + [pltpu.HBM(a.shape, a.dtype) for a in grads + lands] + [jax.ShapeDtypeStruct((8, LANES), F32)],
        input_output_aliases={t: 2 + t for t in range(2 * n)},
        compiler_params=pltpu.CompilerParams(has_side_effects=EFFECT),
    )(*[_hbm(a) for a in grads + lands], after)
    return res[0], res[1], list(res[2:2 + n]), list(res[2 + n:2 + 2 * n]), res[-1]


def swap_wait(grads, lands, send_sems, recv_sems, after):
    n = len(grads)

    def body(*refs):
        g_refs, l_refs = refs[:n], refs[n:2 * n]
        send_sems, recv_sems = refs[2 * n], refs[2 * n + 1]
        x, y, c = _place()
        for t in range(n):
            for p in range(N_CHIPS):
                cp = pltpu.make_async_remote_copy(src_ref=g_refs[t].at[p, 1 - c], dst_ref=l_refs[t].at[p],
                                                  send_sem=send_sems.at[N_CHIPS * t + p],
                                                  recv_sem=recv_sems.at[N_CHIPS * t + p],
                                                  device_id=(x, y, 1 - c), device_id_type=MESH)
                cp.wait_send()
                cp.wait_recv()

    res = pl.pallas_call(
        body, name="swap_wait",
        in_specs=[HBM] * (2 * n) + [SEM, SEM, ANY],
        out_specs=[HBM] * (2 * n),
        out_shape=[pltpu.HBM(a.shape, a.dtype) for a in grads + lands],
        input_output_aliases={t: t for t in range(2 * n)},
        compiler_params=pltpu.CompilerParams(has_side_effects=EFFECT),
    )(*grads, *lands, send_sems, recv_sems, after)
    return list(res[:n]), list(res[n:])


def _row_tile(r):
    return max(t for t in range(16, 513, 16) if r % t == 0)


def add_own_half(g, other, c_arr):
    _, _, r, cols = g.shape
    tr = _row_tile(r)

    def body(c_ref, a_ref, b_ref, o_ref):
        o_ref[...] = (a_ref[...] + b_ref[...]).astype(BF16)

    return pl.pallas_call(
        body, name="add_own_half",
        grid_spec=pltpu.PrefetchScalarGridSpec(
            num_scalar_prefetch=1, grid=(N_CHIPS, r // tr),
            in_specs=[pl.BlockSpec((None, None, tr, cols), lambda p, i, c_ref: (p, c_ref[0], i, 0)),
                      pl.BlockSpec((None, tr, cols), lambda p, i, c_ref: (p, i, 0))],
            out_specs=pl.BlockSpec((None, tr, cols), lambda p, i, c_ref: (p, i, 0))),
        out_shape=jax.ShapeDtypeStruct((N_CHIPS, r, cols), BF16),
        compiler_params=_cparams("parallel", "parallel"),
    )(c_arr, g, other)


def scatter_start(partials):
    n = len(partials)

    def body(*refs):
        s_refs, l_refs = refs[:n], refs[n:2 * n]
        send_sems, recv_sems = refs[2 * n], refs[2 * n + 1]
        token = refs[-1]
        x, y, c = _place()
        for t in range(n):
            for k, (qx, qy) in enumerate(_other_chips(x, y)):
                pltpu.make_async_remote_copy(src_ref=s_refs[t].at[2 * qx + qy], dst_ref=l_refs[t].at[k],
                                             send_sem=send_sems.at[3 * t + k], recv_sem=recv_sems.at[3 * t + k],
                                             device_id=(qx, qy, c), device_id_type=MESH).start()
        token[...] = jnp.zeros_like(token)

    lands = [lax.empty((3,) + s.shape[1:], s.dtype) for s in partials]
    res = pl.pallas_call(
        body, name="scatter_start",
        in_specs=[HBM] * (2 * n),
        out_specs=[SEM, SEM] + [HBM] * (2 * n) + [pl.BlockSpec(memory_space=pltpu.VMEM)],
        out_shape=[pltpu.SemaphoreType.DMA((3 * n,)), pltpu.SemaphoreType.DMA((3 * n,))]
        + [pltpu.HBM(a.shape, a.dtype) for a in partials + lands] + [jax.ShapeDtypeStruct((8, LANES), F32)],
        input_output_aliases={t: 2 + t for t in range(2 * n)},
        compiler_params=pltpu.CompilerParams(has_side_effects=EFFECT),
    )(*[_hbm(a) for a in partials + lands])
    return res[0], res[1], list(res[2:2 + n]), list(res[2 + n:2 + 2 * n]), res[-1]


def scatter_wait(partials, lands, send_sems, recv_sems, after):
    n = len(partials)

    def body(*refs):
        s_refs, l_refs = refs[:n], refs[n:2 * n]
        send_sems, recv_sems = refs[2 * n], refs[2 * n + 1]
        x, y, c = _place()
        for t in range(n):
            for k, (qx, qy) in enumerate(_other_chips(x, y)):
                cp = pltpu.make_async_remote_copy(src_ref=s_refs[t].at[2 * qx + qy], dst_ref=l_refs[t].at[k],
                                                  send_sem=send_sems.at[3 * t + k], recv_sem=recv_sems.at[3 * t + k],
                                                  device_id=(qx, qy, c), device_id_type=MESH)
                cp.wait_send()
                cp.wait_recv()

    res = pl.pallas_call(
        body, name="scatter_wait",
        in_specs=[HBM] * (2 * n) + [SEM, SEM, ANY],
        out_specs=[HBM] * (2 * n),
        out_shape=[pltpu.HBM(a.shape, a.dtype) for a in partials + lands],
        input_output_aliases={t: t for t in range(2 * n)},
        compiler_params=pltpu.CompilerParams(has_side_effects=EFFECT),
    )(*partials, *lands, send_sems, recv_sems, after)
    return list(res[:n]), list(res[n:])


def sum_chips(own, parts, where):
    _, r, cols = own.shape
    tr = _row_tile(r)

    def body(w_ref, a_ref, p_ref, o_ref):
        acc = a_ref[...].astype(F32)
        for k in range(3):
            acc = acc + p_ref[k].astype(F32)
        o_ref[...] = acc

    return pl.pallas_call(
        body, name="sum_chips",
        grid_spec=pltpu.PrefetchScalarGridSpec(
            num_scalar_prefetch=1, grid=(r // tr,),
            in_specs=[pl.BlockSpec((None, tr, cols), lambda i, w_ref: (w_ref[0], i, 0)),
                      pl.BlockSpec((3, tr, cols), lambda i, w_ref: (0, i, 0))],
            out_specs=pl.BlockSpec((None, tr, cols), lambda i, w_ref: (w_ref[1], i, 0))),
        out_shape=jax.ShapeDtypeStruct((DEPTH, r, cols), F32),
        compiler_params=_cparams("parallel"),
    )(where, own, parts)


def sibling_share_layer(bufs):
    n = len(bufs)

    def body(*refs):
        o_refs = refs[n:2 * n]
        send_sems, recv_sems = refs[2 * n:]
        x, y, c = _place()
        cps = []
        for t in range(n):
            cp = pltpu.make_async_remote_copy(src_ref=o_refs[t].at[c], dst_ref=o_refs[t].at[c], send_sem=send_sems.at[t],
                                              recv_sem=recv_sems.at[t], device_id=(x, y, 1 - c), device_id_type=MESH)
            cp.start()
            cps.append(cp)
        for t in range(n):
            slot = o_refs[t].at[1 - c]
            pltpu.make_async_remote_copy(src_ref=slot, dst_ref=slot, send_sem=send_sems.at[t], recv_sem=recv_sems.at[t],
                                         device_id=(x, y, 1 - c), device_id_type=MESH).wait_recv()
        for cp in cps:
            cp.wait_send()

    return pl.pallas_call(
        body, name="sibling_share_layer",
        in_specs=[ANY] * n, out_specs=[ANY] * n,
        out_shape=[jax.ShapeDtypeStruct(b.shape, b.dtype) for b in bufs],
        input_output_aliases={t: t for t in range(n)},
        scratch_shapes=[pltpu.SemaphoreType.DMA((n,)), pltpu.SemaphoreType.DMA((n,))],
    )(*bufs)


SP_META = 2 * (N_META * D_MODEL // LANES)
SP_NORM = DEPTH * D_MODEL // LANES
SP_RB = DEPTH * N_BUCKETS
SP_SINK = DEPTH * ATT_HEADS
SP_CONV = DEPTH * 3 * BRANCH_WIDTH // LANES
SP_LOSS = 8
SIDE_ROWS = 48
SP_ROWS = SP_META + 2 * SP_NORM + SP_RB + SP_SINK + SP_CONV + SP_LOSS


def sum_small(slots):
    half = SP_META // 2
    rb0 = SP_META + 2 * SP_NORM
    rest_rows = SP_ROWS - SP_META

    def body(s_ref, meta_ref, rest_ref):
        acc = s_ref[0]
        for d in range(1, 8):
            acc = acc + s_ref[d]
        meta_ref[...] = acc[0:half] + acc[half:SP_META]
        rest_ref[...] = acc[SP_META:]
        rest_ref[rb0 - SP_META:rb0 - SP_META + N_BUCKETS, :] = (
            acc[rb0:rb0 + N_BUCKETS] + acc[rb0 + N_BUCKETS:rb0 + 2 * N_BUCKETS])

    vm = pl.BlockSpec(memory_space=pltpu.VMEM)
    return pl.pallas_call(
        body, name="sum_small",
        in_specs=[vm], out_specs=[vm, vm],
        out_shape=[jax.ShapeDtypeStruct((half, LANES), F32), jax.ShapeDtypeStruct((rest_rows, LANES), F32)],
    )(slots)


def local_step(x, loss_target, meta_full, rel_bias, norm_pre, conv_w_full, attn_sinks, norm_post, weights_of, mid_fwd,
               grads_done, bwd_done):
    nb, seq, _ = x.shape
    nc = seq // BLOCK + 1
    lp = nc * BLOCK
    rows = nb * lp
    pad = jnp.zeros((nb, PAD_FRONT, D_MODEL), F32)
    meta = jnp.broadcast_to(meta_full[None], (nb, N_META, D_MODEL))
    h0 = jnp.concatenate([pad, meta, x], axis=1).reshape(rows, D_MODEL)
    target = jnp.pad(loss_target, ((0, 0), (BLOCK, 0), (0, 0))).reshape(rows, D_MODEL)
    cosf, sinf = _rot_tables(lp)
    bkt = jnp.asarray(_bucket_table())

    g_pre = norm_pre.reshape(DEPTH, 1, D_MODEL)
    g_post = norm_post.reshape(DEPTH, 1, D_MODEL)
    order = lambda token: bkt if token is None else token

    acts = []
    h = h0
    for l in range(DEPTH):
        w_in, token = weights_of(l, h)
        hb, p_abc = norm_matmul(h, g_pre, l, w_in, 0, N_ABC_TILES, order(token))
        p_m = matmul_cols(hb, w_in, N_ABC_TILES, N_M_TILES)
        br, states = mixers_fwd(p_abc, cosf, sinf, bkt, rel_bias, attn_sinks, conv_w_full, l, nb, nc)
        (w_br, w_out), token = mid_fwd(l, br)
        acts.append((h, hb, p_abc, p_m, br, states, w_in, w_br, w_out))
        if l < DEPTH - 1:
            h = merge_fwd(h, br, p_m, w_br, w_out, g_post, l, order(token))
        else:
            assert token is None
            loss_part, d_h = merge_fwd_loss(h, br, p_m, w_br, w_out, g_post, l, target, lp)

    small = [None] * DEPTH
    token = None
    for l in reversed(range(DEPTH)):
        h_in, hb, p_abc, p_m, br, states, w_in, w_br, w_out = acts[l]
        d_br, d_m, d_gpost, g_wbr, g_wout = merge_bwd(d_h, br, p_m, w_br, w_out, g_post, l, order(token))
        d_abc, d_rb, d_sk, d_cw = mixers_bwd(p_abc, d_br, states, cosf, sinf, bkt, rel_bias,
                                             attn_sinks, conv_w_full, l, nb, nc)
        g_win = proj_wgrad(hb, d_abc, d_m)
        token = grads_done(l, [g_win, g_wbr, g_wout])
        d_h, d_gpre = proj_dgrad(d_abc, d_m, w_in, h_in, g_pre, l, d_h, order(token))
        token = bwd_done(l, d_h)
        small[l] = (d_gpre[0], d_gpost[0], d_rb, d_sk, d_cw[0:3])

    d_h3 = d_h.reshape(nb, lp, D_MODEL)
    d_x = d_h3[:, BLOCK:]
    d_meta = d_h3[:, PAD_FRONT:BLOCK]
    sp = jnp.concatenate([
        d_meta.reshape(-1, LANES),
        jnp.stack([small[l][0] for l in range(DEPTH)]).reshape(-1, LANES),
        jnp.stack([small[l][1] for l in range(DEPTH)]).reshape(-1, LANES),
        jnp.concatenate([small[l][2] for l in range(DEPTH)], axis=0),
        jnp.concatenate([small[l][3] for l in range(DEPTH)], axis=0),
        jnp.stack([small[l][4] for l in range(DEPTH)]).reshape(-1, LANES),
        loss_part], axis=0)
    return d_x, sp


def kernel(x, meta_tokens, rel_bias, norm_pre, w_in, conv_w, attn_sinks, w_branch, w_out, norm_post, loss_target, m_meta_tokens, m_rel_bias, m_norm_pre, m_w_in, m_conv_w, m_attn_sinks, m_w_branch, m_w_out, m_norm_post, v_meta_tokens, v_rel_bias, v_norm_pre, v_w_in, v_conv_w, v_attn_sinks, v_w_branch, v_w_out, v_norm_post):
    assert x.shape[0] == 2 and SP_META == 2 * N_META * D_MODEL // LANES
    px, py, pc = _place()
    chip = 2 * px + py

    c_arr = jnp.reshape(pc, (1,)).astype(jnp.int32)
    where = jnp.stack([chip, pc]).astype(jnp.int32)
    tr_ = lambda a: jnp.swapaxes(a, 1, 2)
    w3 = [tr_(w_in), w_branch.reshape(DEPTH, N_BRANCH * BRANCH_WIDTH, SHARD_D), w_out]
    halves = lambda a: a.reshape(2, a.shape[0] // 2, a.shape[1])

    def as_weights(bufs):
        a_in, a_br, a_out = bufs
        return (a_in.reshape(PROJ_WIDTH, D_MODEL), a_br.reshape(N_CHIPS, N_BRANCH, BRANCH_WIDTH, SHARD_D),
                a_out.reshape(D_MODEL, D_MODEL))

    n_meta_rows = N_META * SHARD_D // LANES
    side = jnp.concatenate([meta_tokens.reshape(-1), conv_w.reshape(-1)]).reshape(-1, LANES)
    side = jnp.concatenate([side, jnp.zeros((SIDE_ROWS - side.shape[0], LANES), F32)], axis=0)
    slots = [[_own_slot(halves(w[l].astype(BF16)), chip) for w in w3] for l in range(DEPTH)]
    send0, recv0, flying0, _ = gather_start([_own_slot(halves(side), chip)] + slots[0], where)
    side_chips = gather_forward(gather_wait(flying0[:1], send0, recv0, where))[0].reshape(N_CHIPS, SIDE_ROWS, LANES)
    meta_full = jnp.moveaxis(side_chips[:, :n_meta_rows].reshape(N_CHIPS, N_META, SHARD_D), 0, 1).reshape(N_META, D_MODEL)
    conv_full = jnp.moveaxis(side_chips[:, n_meta_rows:n_meta_rows + 6].reshape(N_CHIPS, DEPTH, 3, LANES), 0, 2).reshape(DEPTH, 3, BRANCH_WIDTH)
    inbound = {}

    def weights_of(l, h):
        if l == 0:
            inbound[0] = gather_forward(gather_wait(flying0[1:2], send0, recv0, h, first=1))
            inbound[1] = gather_start(slots[1], inbound[0][0])
            return inbound[0][0].reshape(PROJ_WIDTH, D_MODEL), inbound[1][3]
        send, recv, thru = inbound[1]
        inbound[1] = as_weights(forward_wait(thru, send, recv, h))
        return inbound[1][0], None

    def mid_fwd(l, br):
        if l == 0:
            rest = gather_forward(gather_wait(flying0[2:], send0, recv0, br, first=2))
            send, recv, flying1, _ = inbound[1]
            send, recv, thru, started = forward_start(gather_wait(flying1, send, recv, rest[0]))
            inbound[1] = (send, recv, thru)
            return as_weights(inbound[0] + rest)[1:], started
        return inbound[1][1:], None

    reduced = [None] * DEPTH
    flying = {}

    def finish_reduce(l, after):
        partials, parts = scatter_wait(*flying[l], after)
        reduced[l] = sibling_share_layer([sum_chips(a, p, where) for a, p in zip(partials, parts)])

    def start_scatter(l, full, others):
        send, recv, thru, lands, started = scatter_start([add_own_half(g, o, c_arr) for g, o in zip(full, others)])
        flying[l] = (thru, lands, send, recv)
        return started

    m3 = [tr_(m_w_in), m_w_branch.reshape(w3[1].shape), m_w_out]
    v3 = [tr_(v_w_in), v_w_branch.reshape(w3[1].shape), v_w_out]
    big = [None] * 3

    def adamw_of(l, after):
        for t in range(3):
            big[t] = adamw_layer(w3[t], reduced[l][t].reshape(w3[t].shape[1:]), m3[t], v3[t], l, big[t], after)
            after = big[t][1]

    def grads_done(l, grads):
        full = [g.reshape(N_CHIPS, 2, g.size // (2 * N_CHIPS * g.shape[-1]), g.shape[-1]) for g in grads]
        if l == 0:
            finish_reduce(1, grads[0])
        send, recv, thru, lands, started = swap_start(full, where if l == 1 else reduced[1][0])
        if l == 1:
            flying["swap"] = (thru, lands, send, recv)
            return started
        adamw_of(1, started)
        return start_scatter(0, *swap_wait(thru, lands, send, recv, big[2][1]))

    def bwd_done(l, d_h):
        if l == 1:
            return start_scatter(1, *swap_wait(*flying["swap"], d_h))
        return None

    d_x, sp = local_step(x, loss_target, meta_full, rel_bias, norm_pre, conv_full, attn_sinks, norm_post,
                         weights_of, mid_fwd, grads_done, bwd_done)
    finish_reduce(0, sp)

    s_send, s_recv, s_buf, s_started = small_start(sp, 4 * px + 2 * py + pc, reduced[0][0])

    adamw_of(0, s_started)
    g_in, *u_in = [tr_(a) for a in big[0]]
    g_br, *u_br = [a.reshape(w_branch.shape) for a in big[1]]
    g_out, *u_out = big[2]

    meta_rows, rest = sum_small(small_wait(s_buf, s_send, s_recv, big[2][1]))
    o = 0
    g_meta_full = meta_rows.reshape(N_META, D_MODEL)
    g_norm_pre = rest[o:o + SP_NORM].reshape(DEPTH, D_MODEL); o += SP_NORM
    g_norm_post = rest[o:o + SP_NORM].reshape(DEPTH, D_MODEL); o += SP_NORM
    g_rel_bias = rest[o:o + N_BUCKETS, :ATT_HEADS]; o += SP_RB
    g_sinks = rest[o:o + SP_SINK, 0].reshape(DEPTH, ATT_HEADS); o += SP_SINK
    g_conv_full = rest[o:o + SP_CONV].reshape(DEPTH, 3, BRANCH_WIDTH); o += SP_CONV
    loss = rest[o, 0]
    g_meta = lax.dynamic_slice_in_dim(g_meta_full, chip * SHARD_D, SHARD_D, axis=1)
    g_conv = lax.dynamic_slice_in_dim(g_conv_full, chip * LANES, LANES, axis=2)

    to2 = lambda a: a.reshape(-1, a.shape[-1])
    smalls = [(meta_tokens, g_meta, m_meta_tokens, v_meta_tokens),
              (rel_bias, g_rel_bias, m_rel_bias, v_rel_bias),
              (norm_pre, g_norm_pre, m_norm_pre, v_norm_pre),
              (to2(conv_w), to2(g_conv), to2(m_conv_w), to2(v_conv_w)),
              (attn_sinks, g_sinks, m_attn_sinks, v_attn_sinks),
              (norm_post, g_norm_post, m_norm_post, v_norm_post)]
    u_meta, u_rb, u_npre, u_conv, u_sink, u_npost = adamw_small(smalls)
    u_conv = tuple(a.reshape(conv_w.shape) for a in u_conv)

    grads = [g_meta, g_rel_bias, g_norm_pre, g_in, g_conv, g_sinks, g_br, g_out, g_norm_post]
    upd = [u_meta, u_rb, u_npre, u_in, u_conv, u_sink, u_br, u_out, u_npost]
    return (loss, d_x, *grads, *[u[0] for u in upd], *[u[1] for u in upd], *[u[2] for u in upd])
```

```python
import math

import numpy as np
import jax
import jax.numpy as jnp
from jax import lax
from jax.experimental import pallas as pl
from jax.experimental.pallas import tpu as pltpu

F32 = jnp.float32
BF16 = jnp.bfloat16
MESH = pl.DeviceIdType.MESH

D_MODEL = 1024
DEPTH = 2
N_META = 16
BLOCK = 128
PAD_FRONT = BLOCK - N_META
ATT_HEADS = 8
ATT_HEAD_DIM = 64
N_BUCKETS = 32
MAX_EXACT = 16
MAX_DISTANCE = 128
RET_HEADS = 4
ROT_BASE = 10000.0
N_BRANCH = 3
BRANCH_WIDTH = 512
PROJ_WIDTH = 8448
ABC_WIDTH = 5376
MERGE_WIDTH = N_BRANCH * D_MODEL
RMS_EPS = 1e-6
GN_EPS = 1e-6
NEG_INF = -1e30
ATT_SCALE = ATT_HEAD_DIM ** -0.5
RET_SCALE = BLOCK ** -0.5
LOG_GAMMA = tuple(math.log1p(-(2.0 ** (-5.0 - h))) for h in range(RET_HEADS))

C_AQ, C_AK, C_AV, C_AG = 0, 512, 640, 768
C_RQ, C_RK, C_RV, C_RG = 1280, 1792, 2304, 2816
C_CB, C_CC, C_CX, C_CG = 3328, 3840, 4352, 4864

ADAM_LR = 0.001
ADAM_B1 = 0.9
ADAM_B2 = 0.999
ADAM_EPS = 1e-08
ADAM_WD = 0.01
ADAM_STEP = 10

N_CHIPS = 4
SHARD_D = D_MODEL // N_CHIPS
LANES = 128

VMEM_LIMIT = 56 * 1024 * 1024
COL_TILE = 768
ROW_TILE = 1088
PROJ_ROW_TILE = 2176


def _cparams(*sem):
    return pltpu.CompilerParams(dimension_semantics=sem, vmem_limit_bytes=VMEM_LIMIT)


def _nt(a, b):
    return lax.dot_general(a, b, (((1,), (1,)), ((), ())), preferred_element_type=F32)


def _tn(a, b):
    return lax.dot_general(a, b, (((0,), (0,)), ((), ())), preferred_element_type=F32)


def _nn(a, b):
    return jnp.dot(a, b, preferred_element_type=F32)


def _sigmoid(x):
    return 0.5 * jnp.tanh(0.5 * x) + 0.5


def _silu(x):
    return x * _sigmoid(x)


def _dsilu(x):
    s = _sigmoid(x)
    return s * (1.0 + x * (1.0 - s))


def _bucket_table():
    r = np.arange(BLOCK)[:, None]
    c = np.arange(2 * BLOCK)[None, :]
    n = np.maximum(BLOCK + r - c, 0)
    nf = np.maximum(n, 1).astype(np.float32)
    large = MAX_EXACT + (np.log(nf / MAX_EXACT) / math.log(MAX_DISTANCE / MAX_EXACT)
                         * (N_BUCKETS - MAX_EXACT)).astype(np.int32)
    large = np.minimum(large, N_BUCKETS - 1)
    return np.where(n < MAX_EXACT, n, large).astype(np.int32)


def _rot_tables(lp):
    half = BLOCK // 2
    pos = (jnp.arange(lp) - PAD_FRONT).astype(F32)
    theta = 1.0 / (ROT_BASE ** jnp.linspace(0.0, 1.0, half, dtype=F32))
    ang = pos[:, None] * theta[None, :]
    cos, sin = jnp.cos(ang), jnp.sin(ang)
    return jnp.concatenate([cos, cos], axis=1), jnp.concatenate([-sin, sin], axis=1)


def norm_matmul(x2d, g, layer, w, col0_blocks, n_col_blocks, after):
    t = x2d.shape[0]
    tm = PROJ_ROW_TILE if t % PROJ_ROW_TILE == 0 else BLOCK

    def body(x_ref, g_ref, w_ref, after_ref, hb_ref, o_ref):
        @pl.when(pl.program_id(1) == 0)
        def _():
            x = x_ref[...]
            r = lax.rsqrt(jnp.mean(x * x, axis=-1, keepdims=True) + RMS_EPS)
            hb_ref[...] = (x * r * g_ref[...]).astype(BF16)

        o_ref[...] = _nt(hb_ref[...], w_ref[...])

    return pl.pallas_call(
        body, name="norm_matmul",
        grid=(t // tm, n_col_blocks),
        in_specs=[pl.BlockSpec((tm, D_MODEL), lambda i, j: (i, 0)),
                  pl.BlockSpec((None, 1, D_MODEL), lambda i, j: (layer, 0, 0)),
                  pl.BlockSpec((COL_TILE, D_MODEL), lambda i, j: (j + col0_blocks, 0)),
                  ANY],
        out_specs=[pl.BlockSpec((tm, D_MODEL), lambda i, j: (i, 0)),
                   pl.BlockSpec((tm, COL_TILE), lambda i, j: (i, j))],
        out_shape=[jax.ShapeDtypeStruct((t, D_MODEL), BF16),
                   jax.ShapeDtypeStruct((t, n_col_blocks * COL_TILE), F32)],
        compiler_params=_cparams("parallel", "arbitrary"),
    )(x2d, g, w, after)


def matmul_cols(a, w, col0_blocks, n_col_blocks):
    t, k = a.shape
    tm = PROJ_ROW_TILE if t % PROJ_ROW_TILE == 0 else BLOCK

    def body(a_ref, w_ref, o_ref):
        o_ref[...] = _nt(a_ref[...], w_ref[...]).astype(BF16)

    return pl.pallas_call(
        body, name="matmul_cols",
        grid=(t // tm, n_col_blocks),
        in_specs=[pl.BlockSpec((tm, k), lambda i, j: (i, 0)),
                  pl.BlockSpec((COL_TILE, k), lambda i, j: (j + col0_blocks, 0))],
        out_specs=pl.BlockSpec((tm, COL_TILE), lambda i, j: (i, j)),
        out_shape=jax.ShapeDtypeStruct((t, n_col_blocks * COL_TILE), BF16),
        compiler_params=_cparams("parallel", "arbitrary"),
    )(a, w)


class _Widened:
    def __init__(self, ref):
        self.ref = ref

    def __getitem__(self, idx):
        return self.ref[idx].astype(F32)


def _build_bias(bkt_ref, rb_ref, bias_s):
    bkt = bkt_ref[...]
    for h in range(ATT_HEADS):
        acc = jnp.zeros((BLOCK, 2 * BLOCK), F32)
        for b in range(N_BUCKETS):
            acc = jnp.where(bkt == b, rb_ref[b, h], acc)
        bias_s[h] = acc


def _band_mask(n):
    r = lax.broadcasted_iota(jnp.int32, (BLOCK, 2 * BLOCK), 0)
    c = lax.broadcasted_iota(jnp.int32, (BLOCK, 2 * BLOCK), 1)
    key_pos = (n - 1) * BLOCK + c
    return (c > r) & (c <= r + BLOCK) & (key_pos >= PAD_FRONT)


def _split_heads(kv, kh):
    lane = lax.broadcasted_iota(jnp.int32, kv.shape, 1)
    if kh == 0:
        lo = jnp.where(lane < ATT_HEAD_DIM, kv, 0.0)
        hi = pltpu.roll(lo, ATT_HEAD_DIM, 1)
    else:
        hi = jnp.where(lane >= ATT_HEAD_DIM, kv, 0.0)
        lo = pltpu.roll(hi, ATT_HEAD_DIM, 1)
    return lo, hi


def _merge_heads(acc_lo, acc_hi, kh):
    lane = lax.broadcasted_iota(jnp.int32, acc_lo.shape, 1)
    if kh == 0:
        return jnp.where(lane < ATT_HEAD_DIM, acc_lo + pltpu.roll(acc_hi, ATT_HEAD_DIM, 1), 0.0)
    return jnp.where(lane >= ATT_HEAD_DIM, acc_hi + pltpu.roll(acc_lo, ATT_HEAD_DIM, 1), 0.0)


def _softmax_of(qk, bias_h, mask, sink_h):
    s = qk + bias_h
    s = jnp.where(mask, s, NEG_INF)
    m = jnp.maximum(jnp.max(s, axis=-1, keepdims=True), sink_h)
    p = jnp.exp(s - m)
    es = jnp.exp(sink_h - m)
    inv = 1.0 / (jnp.sum(p, axis=-1, keepdims=True) + es)
    return p * inv, es * inv


def _rot(t, cosf, sinf):
    return t * cosf + pltpu.roll(t, BLOCK // 2, 1) * sinf


def _rot_t(d, cosf, sinf):
    return d * cosf + pltpu.roll(d * sinf, BLOCK // 2, 1)


def _decay_tables(h):
    lg = LOG_GAMMA[h]
    i = lax.broadcasted_iota(jnp.int32, (BLOCK, BLOCK), 0)
    j = lax.broadcasted_iota(jnp.int32, (BLOCK, BLOCK), 1)
    diff = (i - j).astype(F32)
    dm = jnp.where(diff >= 0, jnp.exp(diff * lg), 0.0)
    row = lax.broadcasted_iota(jnp.int32, (BLOCK, 1), 0).astype(F32)
    zeta = jnp.exp((BLOCK - 1 - row) * lg)
    xi = jnp.exp((row + 1.0) * lg)
    return dm, zeta, xi, math.exp(BLOCK * lg)


def _valid_col(n):
    row = lax.broadcasted_iota(jnp.int32, (BLOCK, 1), 0)
    return ((n * BLOCK + row) >= PAD_FRONT).astype(F32)


def _shift_down(cur, prev, k):
    row = lax.broadcasted_iota(jnp.int32, cur.shape, 0)
    return jnp.where(row >= k, pltpu.roll(cur, k, 0), pltpu.roll(prev, k, 0))


def _shift_up(cur, nxt, k):
    row = lax.broadcasted_iota(jnp.int32, cur.shape, 0)
    return jnp.where(row < BLOCK - k, pltpu.roll(cur, BLOCK - k, 0), pltpu.roll(nxt, BLOCK - k, 0))


def mixers_fwd(proj, cosf, sinf, bkt, rel_bias, sinks, conv_w, layer, nb, nc):
    def body(p_ref, cos_ref, sin_ref, bkt_ref, rb_ref, sk_ref, cw_ref, br_ref, st_ref,
             bias_s, kv_s, state_s, u_s):
        p_ref = _Widened(p_ref)
        n = pl.program_id(0)

        @pl.when(n == 0)
        def _():
            _build_bias(bkt_ref, rb_ref, bias_s)
            kv_s[:, 0:BLOCK, :] = jnp.zeros((nb, BLOCK, 2 * BLOCK), F32)
            state_s[...] = jnp.zeros_like(state_s)
            u_s[...] = jnp.zeros_like(u_s)

        valid = _valid_col(n)
        mask = _band_mask(n)
        ex = range(nb)

        for b in ex:
            kv_s[b, BLOCK:2 * BLOCK, :] = p_ref[b, :, C_AK:C_AK + 2 * BLOCK]
        for kh in range(2):
            ks = [[t.astype(BF16) for t in _split_heads(kv_s[b, :, 0:BLOCK], kh)] for b in ex]
            vs = [[t.astype(BF16) for t in _split_heads(kv_s[b, :, BLOCK:2 * BLOCK], kh)] for b in ex]
            pairs = [(b, 2 * kh + jj) for jj in range(2) for b in ex]
            subs = [(b, j, x) for (b, j) in pairs for x in range(2)]
            qb_ = {(b, j): (p_ref[b, :, C_AQ + BLOCK * j:C_AQ + BLOCK * (j + 1)] * ATT_SCALE).astype(BF16)
                   for (b, j) in pairs}
            qk_ = {(b, j, x): _nt(qb_[(b, j)], ks[b][x]) for (b, j, x) in subs}
            pb_ = {}
            for u in subs:
                h = 2 * u[1] + u[2]
                pb_[u] = _softmax_of(qk_[u], bias_s[h], mask, sk_ref[layer, h])[0].astype(BF16)
            o_ = {u: _nn(pb_[u], vs[u[0]][u[2]]) for u in subs}
            for (b, j) in pairs:
                gate = p_ref[b, :, C_AG + BLOCK * j:C_AG + BLOCK * (j + 1)]
                br_ref[b, :, BLOCK * j:BLOCK * (j + 1)] = ((o_[(b, j, 0)] + o_[(b, j, 1)]) * _silu(gate)).astype(BF16)
        for b in ex:
            kv_s[b, 0:BLOCK, :] = kv_s[b, BLOCK:2 * BLOCK, :]

        cosv = cos_ref[...]
        sinv = sin_ref[...]
        tabs = [_decay_tables(h) for h in range(RET_HEADS)]
        units = [(b, h) for h in range(RET_HEADS) for b in ex]
        sl = lambda c0, h: slice(c0 + BLOCK * h, c0 + BLOCK * (h + 1))
        q_, k_, v_, sp_ = {}, {}, {}, {}
        for u in units:
            b, h = u
            q_[u] = _rot(p_ref[b, :, sl(C_RQ, h)], cosv, sinv).astype(BF16)
            k_[u] = (_rot(p_ref[b, :, sl(C_RK, h)], cosv, sinv) * RET_SCALE * valid).astype(BF16)
            v_[u] = p_ref[b, :, sl(C_RV, h)]
            sp_[u] = state_s[b, h]
            st_ref[b, 0, h] = sp_[u]
        qk_ = {u: _nt(q_[u], k_[u]) for u in units}
        qs_ = {u: _nn(q_[u], sp_[u].astype(BF16)) for u in units}
        kv_ = {u: _tn(k_[u], (v_[u] * tabs[u[1]][1]).astype(BF16)) for u in units}
        a_ = {u: (qk_[u] * tabs[u[1]][0]).astype(BF16) for u in units}
        av_ = {u: _nn(a_[u], v_[u].astype(BF16)) for u in units}
        for u in units:
            b, h = u
            o = av_[u] + tabs[h][2] * qs_[u]
            mu = jnp.mean(o, axis=-1, keepdims=True)
            var = jnp.mean(jnp.square(o - mu), axis=-1, keepdims=True)
            oh = (o - mu) * lax.rsqrt(var + GN_EPS)
            gate = p_ref[b, :, sl(C_RG, h)]
            br_ref[b, :, BRANCH_WIDTH + BLOCK * h:BRANCH_WIDTH + BLOCK * (h + 1)] = (oh * _silu(gate)).astype(BF16)
            state_s[b, h] = tabs[h][3] * sp_[u] + kv_[u]

        for b in ex:
            u = p_ref[b, :, C_CC:C_CC + BRANCH_WIDTH] * p_ref[b, :, C_CX:C_CX + BRANCH_WIDTH] * valid
            u_prev = u_s[b]
            y = (cw_ref[0:1, :] * _shift_down(u, u_prev, 2) + cw_ref[1:2, :] * _shift_down(u, u_prev, 1)
                 + cw_ref[2:3, :] * u)
            yc = p_ref[b, :, C_CB:C_CB + BRANCH_WIDTH] * y * _silu(p_ref[b, :, C_CG:C_CG + BRANCH_WIDTH])
            br_ref[b, :, 2 * BRANCH_WIDTH:3 * BRANCH_WIDTH] = yc.astype(BF16)
            u_s[b] = u

    lp = nc * BLOCK
    smem = pl.BlockSpec(memory_space=pltpu.SMEM)
    br, states = pl.pallas_call(
        body, name="mixers_fwd",
        grid=(nc,),
        in_specs=[pl.BlockSpec((nb, BLOCK, ABC_WIDTH), lambda n: (0, n, 0)),
                  pl.BlockSpec((BLOCK, BLOCK), lambda n: (n, 0)),
                  pl.BlockSpec((BLOCK, BLOCK), lambda n: (n, 0)),
                  pl.BlockSpec((BLOCK, 2 * BLOCK), lambda n: (0, 0)),
                  smem, smem,
                  pl.BlockSpec((None, 3, BRANCH_WIDTH), lambda n: (layer, 0, 0))],
        out_specs=[pl.BlockSpec((nb, BLOCK, N_BRANCH * BRANCH_WIDTH), lambda n: (0, n, 0)),
                   pl.BlockSpec((nb, 1, RET_HEADS, BLOCK, BLOCK), lambda n: (0, n, 0, 0, 0))],
        out_shape=[jax.ShapeDtypeStruct((nb, lp, N_BRANCH * BRANCH_WIDTH), BF16),
                   jax.ShapeDtypeStruct((nb, nc, RET_HEADS, BLOCK, BLOCK), F32)],
        scratch_shapes=[pltpu.VMEM((ATT_HEADS, BLOCK, 2 * BLOCK), F32),
                        pltpu.VMEM((nb, 2 * BLOCK, 2 * BLOCK), F32),
                        pltpu.VMEM((nb, RET_HEADS, BLOCK, BLOCK), F32),
                        pltpu.VMEM((nb, BLOCK, BRANCH_WIDTH), F32)],
        compiler_params=_cparams("arbitrary"),
    )(proj.reshape(nb, lp, ABC_WIDTH), cosf, sinf, bkt, rel_bias, sinks, conv_w)
    return br.reshape(nb * lp, N_BRANCH * BRANCH_WIDTH), states


def mixers_bwd(proj, d_br, states, cosf, sinf, bkt, rel_bias, sinks, conv_w, layer, nb, nc):
    def body(p_ref, kvp_ref, cp_ref, dbr_ref, st_ref, cos_ref, sin_ref, bkt_ref, rb_ref, sk_ref, cw_ref,
             dp_ref, drb_ref, dsk_ref, dcw_ref,
             bias_s, dbias_s, dkv_s, g_s, dy_s):
        p_ref, kvp_ref, cp_ref, dbr_ref = [_Widened(r) for r in (p_ref, kvp_ref, cp_ref, dbr_ref)]
        step = pl.program_id(0)
        n = nc - 1 - step
        ex = range(nb)

        @pl.when(step == 0)
        def _():
            _build_bias(bkt_ref, rb_ref, bias_s)
            dbias_s[...] = jnp.zeros_like(dbias_s)
            dsk_ref[...] = jnp.zeros_like(dsk_ref)
            dcw_ref[...] = jnp.zeros_like(dcw_ref)
            drb_ref[...] = jnp.zeros_like(drb_ref)
            dkv_s[...] = jnp.zeros_like(dkv_s)
            g_s[...] = jnp.zeros_like(g_s)
            dy_s[...] = jnp.zeros_like(dy_s)

        valid = _valid_col(n)
        mask = _band_mask(n)
        has_prev = (n > 0).astype(F32)

        k_all, v_all = [], []
        for b in ex:
            kv_prev = kvp_ref[b] * has_prev
            kv_cur = p_ref[b, :, C_AK:C_AK + 2 * BLOCK]
            k_all.append(jnp.concatenate([kv_prev[:, 0:BLOCK], kv_cur[:, 0:BLOCK]], axis=0))
            v_all.append(jnp.concatenate([kv_prev[:, BLOCK:], kv_cur[:, BLOCK:]], axis=0))
        zero2 = jnp.zeros((2 * BLOCK, BLOCK), F32)
        dk_tot = [zero2 for _ in ex]
        dv_tot = [zero2 for _ in ex]
        for kh in range(2):
            ks = [[t.astype(BF16) for t in _split_heads(k_all[b], kh)] for b in ex]
            vs = [[t.astype(BF16) for t in _split_heads(v_all[b], kh)] for b in ex]
            pairs = [(b, 2 * kh + jj) for jj in range(2) for b in ex]
            subs = [(b, j, x) for (b, j) in pairs for x in range(2)]
            qb_, gate_, dya_, do2_ = {}, {}, {}, {}
            for w in pairs:
                b, j = w
                qb_[w] = (p_ref[b, :, C_AQ + BLOCK * j:C_AQ + BLOCK * (j + 1)] * ATT_SCALE).astype(BF16)
                gate_[w] = p_ref[b, :, C_AG + BLOCK * j:C_AG + BLOCK * (j + 1)]
                dya_[w] = dbr_ref[b, :, BLOCK * j:BLOCK * (j + 1)]
                do2_[w] = (dya_[w] * _silu(gate_[w])).astype(BF16)
            qk_ = {(b, j, x): _nt(qb_[(b, j)], ks[b][x]) for (b, j, x) in subs}
            dpm_ = {(b, j, x): _nt(do2_[(b, j)], vs[b][x]) for (b, j, x) in subs}
            pb_, dsb_ = {}, {}
            for u in subs:
                b, j, x = u
                h = 2 * j + x
                p, p_sink = _softmax_of(qk_[u], bias_s[h], mask, sk_ref[layer, h])
                pb_[u] = p.astype(BF16)
                delta = jnp.sum(p * dpm_[u], axis=-1, keepdims=True)
                ds = p * (dpm_[u] - delta)
                dbias_s[h] += ds
                dsk_ref[h:h + 1, :] += jnp.broadcast_to(
                    jnp.sum(-p_sink * delta, axis=0, keepdims=True), (1, BLOCK))
                dsb_[u] = ds.astype(BF16)
            o_ = {u: _nn(pb_[u], vs[u[0]][u[2]]) for u in subs}
            dq_ = {u: _nn(dsb_[u], ks[u[0]][u[2]]) for u in subs}
            dkm_ = {u: _tn(dsb_[u], qb_[(u[0], u[1])]) for u in subs}
            dvm_ = {u: _tn(pb_[u], do2_[(u[0], u[1])]) for u in subs}
            for w in pairs:
                b, j = w
                o2 = o_[(b, j, 0)] + o_[(b, j, 1)]
                dq2 = (dq_[(b, j, 0)] + dq_[(b, j, 1)]) * ATT_SCALE
                dp_ref[b, :, C_AQ + BLOCK * j:C_AQ + BLOCK * (j + 1)] = dq2.astype(BF16)
                dp_ref[b, :, C_AG + BLOCK * j:C_AG + BLOCK * (j + 1)] = (
                    dya_[w] * o2 * _dsilu(gate_[w])).astype(BF16)
            for b in ex:
                j0, j1 = 2 * kh, 2 * kh + 1
                dk_tot[b] = dk_tot[b] + _merge_heads(dkm_[(b, j0, 0)] + dkm_[(b, j1, 0)],
                                                     dkm_[(b, j0, 1)] + dkm_[(b, j1, 1)], kh)
                dv_tot[b] = dv_tot[b] + _merge_heads(dvm_[(b, j0, 0)] + dvm_[(b, j1, 0)],
                                                     dvm_[(b, j0, 1)] + dvm_[(b, j1, 1)], kh)
        for b in ex:
            dp_ref[b, :, C_AK:C_AK + BLOCK] = (dk_tot[b][BLOCK:, :] + dkv_s[b, :, 0:BLOCK]).astype(BF16)
            dp_ref[b, :, C_AV:C_AV + BLOCK] = (dv_tot[b][BLOCK:, :] + dkv_s[b, :, BLOCK:]).astype(BF16)
            dkv_s[b, :, 0:BLOCK] = dk_tot[b][0:BLOCK, :]
            dkv_s[b, :, BLOCK:] = dv_tot[b][0:BLOCK, :]

        cosv = cos_ref[...]
        sinv = sin_ref[...]
        tabs = [_decay_tables(h) for h in range(RET_HEADS)]
        units = [(b, h) for h in range(RET_HEADS) for b in ex]
        sl = lambda c0, h: slice(c0 + BLOCK * h, c0 + BLOCK * (h + 1))
        q_, k_, v_, vb_, sp_ = {}, {}, {}, {}, {}
        for u in units:
            b, h = u
            q_[u] = _rot(p_ref[b, :, sl(C_RQ, h)], cosv, sinv).astype(BF16)
            k_[u] = (_rot(p_ref[b, :, sl(C_RK, h)], cosv, sinv) * RET_SCALE * valid).astype(BF16)
            v_[u] = p_ref[b, :, sl(C_RV, h)]
            vb_[u] = v_[u].astype(BF16)
            sp_[u] = st_ref[b, 0, h].astype(BF16)
        qk_ = {u: _nt(q_[u], k_[u]) for u in units}
        qs_ = {u: _nn(q_[u], sp_[u]) for u in units}
        a_ = {u: (qk_[u] * tabs[u[1]][0]).astype(BF16) for u in units}
        av_ = {u: _nn(a_[u], vb_[u]) for u in units}
        dob_, dxo_ = {}, {}
        for u in units:
            b, h = u
            xi = tabs[h][2]
            o = av_[u] + xi * qs_[u]
            mu = jnp.mean(o, axis=-1, keepdims=True)
            var = jnp.mean(jnp.square(o - mu), axis=-1, keepdims=True)
            rstd = lax.rsqrt(var + GN_EPS)
            oh = (o - mu) * rstd
            gate = p_ref[b, :, sl(C_RG, h)]
            d_yr = dbr_ref[b, :, BRANCH_WIDTH + BLOCK * h:BRANCH_WIDTH + BLOCK * (h + 1)]
            dp_ref[b, :, sl(C_RG, h)] = (d_yr * oh * _dsilu(gate)).astype(BF16)
            doh = d_yr * _silu(gate)
            do = rstd * (doh - jnp.mean(doh, axis=-1, keepdims=True)
                         - oh * jnp.mean(doh * oh, axis=-1, keepdims=True))
            dob_[u] = do.astype(BF16)
            dxo_[u] = (do * xi).astype(BF16)
        dov_ = {u: _nt(dob_[u], vb_[u]) for u in units}
        dv1_ = {u: _tn(a_[u], dob_[u]) for u in units}
        dq1_ = {u: _nt(dxo_[u], sp_[u]) for u in units}
        gq_ = {u: _tn(q_[u], dxo_[u]) for u in units}
        da_, gb_, zv_ = {}, {}, {}
        for u in units:
            b, h = u
            da_[u] = (dov_[u] * tabs[h][0]).astype(BF16)
            g_next = g_s[b, h]
            gb_[u] = g_next.astype(BF16)
            zv_[u] = (v_[u] * tabs[h][1]).astype(BF16)
            g_s[b, h] = tabs[h][3] * g_next + gq_[u]
        dq2_ = {u: _nn(da_[u], k_[u]) for u in units}
        dk1_ = {u: _tn(da_[u], q_[u]) for u in units}
        dk2_ = {u: _nt(zv_[u], gb_[u]) for u in units}
        dv2_ = {u: _nn(k_[u], gb_[u]) for u in units}
        for u in units:
            b, h = u
            dp_ref[b, :, sl(C_RQ, h)] = _rot_t(dq2_[u] + dq1_[u], cosv, sinv).astype(BF16)
            dp_ref[b, :, sl(C_RK, h)] = _rot_t((dk1_[u] + dk2_[u]) * (RET_SCALE * valid), cosv, sinv).astype(BF16)
            dp_ref[b, :, sl(C_RV, h)] = (dv1_[u] + tabs[h][1] * dv2_[u]).astype(BF16)

        w0, w1, w2 = cw_ref[0:1, :], cw_ref[1:2, :], cw_ref[2:3, :]
        for b in ex:
            cb = p_ref[b, :, C_CB:C_CB + BRANCH_WIDTH]
            cc = p_ref[b, :, C_CC:C_CC + BRANCH_WIDTH]
            cx = p_ref[b, :, C_CX:C_CX + BRANCH_WIDTH]
            cg = p_ref[b, :, C_CG:C_CG + BRANCH_WIDTH]
            u = cc * cx * valid
            u_prev = (cp_ref[b, :, 0:BRANCH_WIDTH] * cp_ref[b, :, BRANCH_WIDTH:2 * BRANCH_WIDTH]
                      * (_valid_col(n - 1) * has_prev))
            u1 = _shift_down(u, u_prev, 1)
            u2 = _shift_down(u, u_prev, 2)
            y = w0 * u2 + w1 * u1 + w2 * u
            d_yc = dbr_ref[b, :, 2 * BRANCH_WIDTH:3 * BRANCH_WIDTH]
            sg = _silu(cg)
            dp_ref[b, :, C_CB:C_CB + BRANCH_WIDTH] = (d_yc * y * sg).astype(BF16)
            dp_ref[b, :, C_CG:C_CG + BRANCH_WIDTH] = (d_yc * cb * y * _dsilu(cg)).astype(BF16)
            dy = d_yc * cb * sg
            dy_next = dy_s[b]
            du = (w2 * dy + w1 * _shift_up(dy, dy_next, 1) + w0 * _shift_up(dy, dy_next, 2)) * valid
            dp_ref[b, :, C_CC:C_CC + BRANCH_WIDTH] = (du * cx).astype(BF16)
            dp_ref[b, :, C_CX:C_CX + BRANCH_WIDTH] = (du * cc).astype(BF16)
            dcw_ref[0:1, :] += jnp.sum(dy * u2, axis=0, keepdims=True)
            dcw_ref[1:2, :] += jnp.sum(dy * u1, axis=0, keepdims=True)
            dcw_ref[2:3, :] += jnp.sum(dy * u, axis=0, keepdims=True)
            dy_s[b] = dy

        @pl.when(step == nc - 1)
        def _():
            bkt = bkt_ref[...]
            row = lax.broadcasted_iota(jnp.int32, (N_BUCKETS, BLOCK), 0)
            lane = lax.broadcasted_iota(jnp.int32, (N_BUCKETS, BLOCK), 1)

            def one_bucket(bk, acc):
                sel = bkt == bk
                for h in range(ATT_HEADS):
                    t = jnp.where(sel, dbias_s[h], 0.0)
                    s = jnp.sum(jnp.sum(t, axis=1, keepdims=True), axis=0, keepdims=True)
                    acc = acc + jnp.where((row == bk) & (lane == h), jnp.broadcast_to(s, acc.shape), 0.0)
                return acc

            drb_ref[...] = lax.fori_loop(0, N_BUCKETS, one_bucket, jnp.zeros((N_BUCKETS, BLOCK), F32))

    lp = nc * BLOCK
    smem = pl.BlockSpec(memory_space=pltpu.SMEM)
    blk = lambda s: nc - 1 - s
    prev = lambda s: jnp.maximum(nc - 2 - s, 0)
    proj3 = proj.reshape(nb, lp, ABC_WIDTH)
    res = pl.pallas_call(
        body, name="mixers_bwd",
        grid=(nc,),
        in_specs=[pl.BlockSpec((nb, BLOCK, ABC_WIDTH), lambda s: (0, blk(s), 0)),
                  pl.BlockSpec((nb, BLOCK, 2 * BLOCK), lambda s: (0, prev(s), C_AK // (2 * BLOCK))),
                  pl.BlockSpec((nb, BLOCK, 1280), lambda s: (0, prev(s), C_CC // 1280)),
                  pl.BlockSpec((nb, BLOCK, N_BRANCH * BRANCH_WIDTH), lambda s: (0, blk(s), 0)),
                  pl.BlockSpec((nb, 1, RET_HEADS, BLOCK, BLOCK), lambda s: (0, blk(s), 0, 0, 0)),
                  pl.BlockSpec((BLOCK, BLOCK), lambda s: (blk(s), 0)),
                  pl.BlockSpec((BLOCK, BLOCK), lambda s: (blk(s), 0)),
                  pl.BlockSpec((BLOCK, 2 * BLOCK), lambda s: (0, 0)),
                  smem, smem,
                  pl.BlockSpec((None, 3, BRANCH_WIDTH), lambda s: (layer, 0, 0))],
        out_specs=[pl.BlockSpec((nb, BLOCK, ABC_WIDTH), lambda s: (0, blk(s), 0)),
                   pl.BlockSpec((N_BUCKETS, BLOCK), lambda s: (0, 0)),
                   pl.BlockSpec((ATT_HEADS, BLOCK), lambda s: (0, 0)),
                   pl.BlockSpec((8, BRANCH_WIDTH), lambda s: (0, 0))],
        out_shape=[jax.ShapeDtypeStruct((nb, lp, ABC_WIDTH), BF16),
                   jax.ShapeDtypeStruct((N_BUCKETS, BLOCK), F32),
                   jax.ShapeDtypeStruct((ATT_HEADS, BLOCK), F32),
                   jax.ShapeDtypeStruct((8, BRANCH_WIDTH), F32)],
        scratch_shapes=[pltpu.VMEM((ATT_HEADS, BLOCK, 2 * BLOCK), F32),
                        pltpu.VMEM((ATT_HEADS, BLOCK, 2 * BLOCK), F32),
                        pltpu.VMEM((nb, BLOCK, 2 * BLOCK), F32),
                        pltpu.VMEM((nb, RET_HEADS, BLOCK, BLOCK), F32),
                        pltpu.VMEM((nb, BLOCK, BRANCH_WIDTH), F32)],
        compiler_params=_cparams("arbitrary"),
    )(proj3, proj3, proj3, d_br.reshape(nb, lp, N_BRANCH * BRANCH_WIDTH), states, cosf, sinf, bkt, rel_bias, sinks,
      conv_w)
    return (res[0].reshape(nb * lp, ABC_WIDTH),) + tuple(res[1:])


MERGE_TILE = 256
MERGE_FWD_TILE = 544


def _merge_forward(br_ref, m_ref, wb_ref, wo_ref):
    bo, gates = [], []
    mixed_pre = None
    for g in range(N_BRANCH):
        br_g = br_ref[:, BRANCH_WIDTH * g:BRANCH_WIDTH * (g + 1)]
        bo_g = jnp.concatenate([_nn(br_g, wb_ref[p, g]) for p in range(N_CHIPS)], axis=1)
        gate_g = _sigmoid(m_ref[:, D_MODEL * g:D_MODEL * (g + 1)].astype(F32))
        bo.append(bo_g)
        gates.append(gate_g)
        mixed_pre = gate_g * bo_g if mixed_pre is None else mixed_pre + gate_g * bo_g
    mixed = _nn(mixed_pre.astype(BF16), wo_ref[...])
    r = lax.rsqrt(jnp.mean(mixed * mixed, axis=-1, keepdims=True) + RMS_EPS)
    return bo, gates, mixed_pre, mixed, r


def merge_fwd(x2d, br, pm, wb, wo, g_post, layer, after):
    t = x2d.shape[0]
    tm = MERGE_FWD_TILE if t % MERGE_FWD_TILE == 0 else BLOCK

    def body(x_ref, br_ref, m_ref, wb_ref, wo_ref, g_ref, after_ref, o_ref):
        _, _, _, mixed, r = _merge_forward(br_ref, m_ref, wb_ref, wo_ref)
        o_ref[...] = x_ref[...] + mixed * r * g_ref[...]

    return pl.pallas_call(
        body, name="merge_fwd",
        grid=(t // tm,),
        in_specs=[pl.BlockSpec((tm, D_MODEL), lambda i: (i, 0)),
                  pl.BlockSpec((tm, N_BRANCH * BRANCH_WIDTH), lambda i: (i, 0)),
                  pl.BlockSpec((tm, MERGE_WIDTH), lambda i: (i, 0)),
                  pl.BlockSpec((N_CHIPS, N_BRANCH, BRANCH_WIDTH, SHARD_D), lambda i: (0, 0, 0, 0)),
                  pl.BlockSpec((D_MODEL, D_MODEL), lambda i: (0, 0)),
                  pl.BlockSpec((None, 1, D_MODEL), lambda i: (layer, 0, 0)),
                  ANY],
        out_specs=pl.BlockSpec((tm, D_MODEL), lambda i: (i, 0)),
        out_shape=jax.ShapeDtypeStruct((t, D_MODEL), F32),
        compiler_params=_cparams("parallel"),
    )(x2d, br, pm, wb, wo, g_post, after)


def merge_fwd_loss(x2d, br, pm, wb, wo, g_post, layer, target, lp):
    t = x2d.shape[0]
    tm = MERGE_FWD_TILE if t % MERGE_FWD_TILE == 0 else BLOCK

    def body(x_ref, br_ref, m_ref, wb_ref, wo_ref, g_ref, t_ref, l_ref, d_ref):
        i = pl.program_id(0)

        @pl.when(i == 0)
        def _():
            l_ref[...] = jnp.zeros_like(l_ref)

        _, _, _, mixed, r = _merge_forward(br_ref, m_ref, wb_ref, wo_ref)
        y = x_ref[...] + mixed * r * g_ref[...]
        row = i * tm + lax.broadcasted_iota(jnp.int32, (tm, 1), 0)
        e = jnp.where(row % lp >= BLOCK, y - t_ref[...], 0.0)
        d_ref[...] = e * (1.0 / D_MODEL)
        s = jnp.sum(jnp.sum(e * e, axis=1, keepdims=True), axis=0, keepdims=True)
        l_ref[...] += jnp.broadcast_to(s * (0.5 / D_MODEL), l_ref.shape)

    return pl.pallas_call(
        body, name="merge_fwd_loss",
        grid=(t // tm,),
        in_specs=[pl.BlockSpec((tm, D_MODEL), lambda i: (i, 0)),
                  pl.BlockSpec((tm, N_BRANCH * BRANCH_WIDTH), lambda i: (i, 0)),
                  pl.BlockSpec((tm, MERGE_WIDTH), lambda i: (i, 0)),
                  pl.BlockSpec((N_CHIPS, N_BRANCH, BRANCH_WIDTH, SHARD_D), lambda i: (0, 0, 0, 0)),
                  pl.BlockSpec((D_MODEL, D_MODEL), lambda i: (0, 0)),
                  pl.BlockSpec((None, 1, D_MODEL), lambda i: (layer, 0, 0)),
                  pl.BlockSpec((tm, D_MODEL), lambda i: (i, 0))],
        out_specs=[pl.BlockSpec((8, BLOCK), lambda i: (0, 0)),
                   pl.BlockSpec((tm, D_MODEL), lambda i: (i, 0))],
        out_shape=[jax.ShapeDtypeStruct((8, BLOCK), F32),
                   jax.ShapeDtypeStruct((t, D_MODEL), F32)],
        compiler_params=_cparams("arbitrary"),
    )(x2d, br, pm, wb, wo, g_post, target)


def merge_bwd(d_out, br, pm, wb, wo, g_post, layer, after):
    t = d_out.shape[0]
    tm = MERGE_TILE if t % MERGE_TILE == 0 else BLOCK

    def body(do_ref, br_ref, m_ref, wb_ref, wo_ref, g_ref, after_ref, dbr_ref, dm_ref, dg_ref, dwb_ref, dwo_ref):

        @pl.when(pl.program_id(0) == 0)
        def _():
            dwb_ref[...] = jnp.zeros_like(dwb_ref)
            dwo_ref[...] = jnp.zeros_like(dwo_ref)
            dg_ref[...] = jnp.zeros_like(dg_ref)

        bo, gates, mixed_pre, mixed, r = _merge_forward(br_ref, m_ref, wb_ref, wo_ref)
        d_o = do_ref[...]
        nh = mixed * r
        dg_ref[0:1, :] += jnp.sum(d_o * nh, axis=0, keepdims=True)
        dn = d_o * g_ref[...]
        d_mixed = (r * (dn - nh * jnp.mean(dn * nh, axis=-1, keepdims=True))).astype(BF16)
        dwo_ref[...] += _tn(mixed_pre.astype(BF16), d_mixed)
        d_pre = _nt(d_mixed, wo_ref[...])
        for g in range(N_BRANCH):
            br_g = br_ref[:, BRANCH_WIDTH * g:BRANCH_WIDTH * (g + 1)]
            d_bo = (d_pre * gates[g]).astype(BF16)
            dm_ref[:, D_MODEL * g:D_MODEL * (g + 1)] = (
                d_pre * bo[g] * gates[g] * (1.0 - gates[g])).astype(BF16)
            d_br_g = None
            for p in range(N_CHIPS):
                d_bo_p = d_bo[:, SHARD_D * p:SHARD_D * (p + 1)]
                part = _nt(d_bo_p, wb_ref[p, g])
                d_br_g = part if d_br_g is None else d_br_g + part
                dwb_ref[p, g] += _tn(br_g, d_bo_p)
            dbr_ref[:, BRANCH_WIDTH * g:BRANCH_WIDTH * (g + 1)] = d_br_g.astype(BF16)

    return pl.pallas_call(
        body, name="merge_bwd",
        grid=(t // tm,),
        in_specs=[pl.BlockSpec((tm, D_MODEL), lambda i: (i, 0)),
                  pl.BlockSpec((tm, N_BRANCH * BRANCH_WIDTH), lambda i: (i, 0)),
                  pl.BlockSpec((tm, MERGE_WIDTH), lambda i: (i, 0)),
                  pl.BlockSpec((N_CHIPS, N_BRANCH, BRANCH_WIDTH, SHARD_D), lambda i: (0, 0, 0, 0)),
                  pl.BlockSpec((D_MODEL, D_MODEL), lambda i: (0, 0)),
                  pl.BlockSpec((None, 1, D_MODEL), lambda i: (layer, 0, 0)),
                  ANY],
        out_specs=[pl.BlockSpec((tm, N_BRANCH * BRANCH_WIDTH), lambda i: (i, 0)),
                   pl.BlockSpec((tm, MERGE_WIDTH), lambda i: (i, 0)),
                   pl.BlockSpec((8, D_MODEL), lambda i: (0, 0)),
                   pl.BlockSpec((N_CHIPS, N_BRANCH, BRANCH_WIDTH, SHARD_D), lambda i: (0, 0, 0, 0)),
                   pl.BlockSpec((D_MODEL, D_MODEL), lambda i: (0, 0))],
        out_shape=[jax.ShapeDtypeStruct((t, N_BRANCH * BRANCH_WIDTH), BF16),
                   jax.ShapeDtypeStruct((t, MERGE_WIDTH), BF16),
                   jax.ShapeDtypeStruct((8, D_MODEL), F32),
                   jax.ShapeDtypeStruct((N_CHIPS, N_BRANCH, BRANCH_WIDTH, SHARD_D), F32),
                   jax.ShapeDtypeStruct((D_MODEL, D_MODEL), F32)],
        compiler_params=_cparams("arbitrary"),
    )(d_out, br, pm, wb, wo, g_post, after)


N_ABC_TILES = ABC_WIDTH // COL_TILE
N_M_TILES = MERGE_WIDTH // COL_TILE


def proj_dgrad(d_abc, d_m, w, x2d, g, layer, d_out, after):
    t = x2d.shape[0]
    tm = ROW_TILE if t % ROW_TILE == 0 else BLOCK
    nk = N_ABC_TILES + N_M_TILES

    def body(da_ref, dm_ref, w_ref, x_ref, g_ref, do_ref, after_ref, dx_ref, dg_ref, acc):
        i = pl.program_id(0)
        k = pl.program_id(1)

        @pl.when((i == 0) & (k == 0))
        def _():
            dg_ref[...] = jnp.zeros_like(dg_ref)

        @pl.when(k == 0)
        def _():
            acc[...] = jnp.zeros_like(acc)

        @pl.when(k < N_ABC_TILES)
        def _():
            acc[...] += _nn(da_ref[...], w_ref[...])

        @pl.when(k >= N_ABC_TILES)
        def _():
            acc[...] += _nn(dm_ref[...], w_ref[...])

        @pl.when(k == nk - 1)
        def _():
            x = x_ref[...]
            r = lax.rsqrt(jnp.mean(x * x, axis=-1, keepdims=True) + RMS_EPS)
            nh = x * r
            dh = acc[...]
            dg_ref[0:1, :] += jnp.sum(dh * nh, axis=0, keepdims=True)
            dn = dh * g_ref[...]
            dx_ref[...] = do_ref[...] + r * (dn - nh * jnp.mean(dn * nh, axis=-1, keepdims=True))

    return pl.pallas_call(
        body, name="proj_dgrad",
        grid=(t // tm, nk),
        in_specs=[pl.BlockSpec((tm, COL_TILE), lambda i, k: (i, jnp.minimum(k, N_ABC_TILES - 1))),
                  pl.BlockSpec((tm, COL_TILE), lambda i, k: (i, jnp.maximum(k - N_ABC_TILES, 0))),
                  pl.BlockSpec((COL_TILE, D_MODEL), lambda i, k: (k, 0)),
                  pl.BlockSpec((tm, D_MODEL), lambda i, k: (i, 0)),
                  pl.BlockSpec((None, 1, D_MODEL), lambda i, k: (layer, 0, 0)),
                  pl.BlockSpec((tm, D_MODEL), lambda i, k: (i, 0)),
                  ANY],
        out_specs=[pl.BlockSpec((tm, D_MODEL), lambda i, k: (i, 0)),
                   pl.BlockSpec((8, D_MODEL), lambda i, k: (0, 0))],
        out_shape=[jax.ShapeDtypeStruct((t, D_MODEL), F32),
                   jax.ShapeDtypeStruct((8, D_MODEL), F32)],
        scratch_shapes=[pltpu.VMEM((tm, D_MODEL), F32)],
        compiler_params=_cparams("arbitrary", "arbitrary"),
    )(d_abc, d_m, w, x2d, g, d_out, after)


def proj_wgrad(hb, d_abc, d_m):
    t = hb.shape[0]
    nj = N_ABC_TILES + N_M_TILES

    def body(h_ref, da_ref, dm_ref, o_ref):
        j = pl.program_id(0)

        @pl.when(j < N_ABC_TILES)
        def _():
            o_ref[...] = _tn(da_ref[...], h_ref[...])

        @pl.when(j >= N_ABC_TILES)
        def _():
            o_ref[...] = _tn(dm_ref[...], h_ref[...])

    return pl.pallas_call(
        body, name="proj_wgrad",
        grid=(nj,),
        in_specs=[pl.BlockSpec((t, D_MODEL), lambda j: (0, 0)),
                  pl.BlockSpec((t, COL_TILE), lambda j: (0, jnp.minimum(j, N_ABC_TILES - 1))),
                  pl.BlockSpec((t, COL_TILE), lambda j: (0, jnp.maximum(j - N_ABC_TILES, 0)))],
        out_specs=pl.BlockSpec((COL_TILE, D_MODEL), lambda j: (j, 0)),
        out_shape=jax.ShapeDtypeStruct((PROJ_WIDTH, D_MODEL), F32),
        compiler_params=_cparams("arbitrary"),
    )(hb, d_abc, d_m)


def _adamw_math(w, g, m, v):
    m = ADAM_B1 * m + (1.0 - ADAM_B1) * g
    v = ADAM_B2 * v + (1.0 - ADAM_B2) * jnp.square(g)
    m_hat = m / (1.0 - ADAM_B1 ** ADAM_STEP)
    v_hat = v / (1.0 - ADAM_B2 ** ADAM_STEP)
    delta = -ADAM_LR * (m_hat / (jnp.sqrt(v_hat) + ADAM_EPS) + ADAM_WD * w)
    return delta, m, v


def adamw_layer(w, g, m, v, layer, acc, after):
    _, r, c = w.shape
    tr = _row_tile(r)

    def body(*refs):
        w_ref, g_ref, m_ref, v_ref = refs[:4]
        go_ref, d_ref, mo_ref, vo_ref = refs[-4:]
        g_val = g_ref[...]
        d, m_new, v_new = _adamw_math(w_ref[...], g_val, m_ref[...], v_ref[...])
        go_ref[...] = g_val
        d_ref[...] = d
        mo_ref[...] = m_new
        vo_ref[...] = v_new

    slab = pl.BlockSpec((None, tr, c), lambda i: (layer, i, 0))
    ins = [w, g, m, v, after]
    in_specs = [slab, pl.BlockSpec((tr, c), lambda i: (i, 0)), slab, slab, ANY]
    aliases = {}
    if acc is not None:
        ins += list(acc)
        in_specs += [ANY] * 4
        aliases = {5 + i: i for i in range(4)}
    return pl.pallas_call(
        body, name="adamw_layer",
        grid=(r // tr,),
        in_specs=in_specs, out_specs=[slab] * 4,
        out_shape=[jax.ShapeDtypeStruct(w.shape, F32)] * 4,
        input_output_aliases=aliases,
        compiler_params=_cparams("parallel"),
    )(*ins)


def adamw_small(params):
    k = len(params)

    def body(*refs):
        ins, outs = refs[:4 * k], refs[4 * k:]
        for i in range(k):
            d, m_new, v_new = _adamw_math(*[r[...] for r in ins[4 * i:4 * i + 4]])
            outs[3 * i][...] = d
            outs[3 * i + 1][...] = m_new
            outs[3 * i + 2][...] = v_new

    flat = [a for p in params for a in p]
    vm = pl.BlockSpec(memory_space=pltpu.VMEM)
    out_shape = [jax.ShapeDtypeStruct(p[0].shape, F32) for p in params for _ in range(3)]
    res = pl.pallas_call(
        body, name="adamw_small",
        in_specs=[vm] * len(flat), out_specs=[vm] * len(out_shape), out_shape=out_shape,
    )(*flat)
    return [tuple(res[3 * i:3 * i + 3]) for i in range(k)]


ANY = pl.BlockSpec(memory_space=pl.ANY)


def _place():
    return lax.axis_index("x"), lax.axis_index("y"), lax.axis_index("c")


HBM = pl.BlockSpec(memory_space=pltpu.HBM)
SEM = pl.BlockSpec(memory_space=pltpu.SEMAPHORE)
EFFECT = pltpu.SideEffectType.DATAFLOW_SIDE_EFFECTING


def _other_chips(x, y):
    return [(1 - x, y), (x, 1 - y), (1 - x, 1 - y)]


def _own_slot(shard, chip):
    buf = lax.empty((N_CHIPS,) + shard.shape, shard.dtype)
    return lax.dynamic_update_slice(buf, shard[None], (chip, 0, 0, 0))


def _hbm(a):
    return pltpu.with_memory_space_constraint(a, pltpu.HBM)


def gather_start(bufs, after):
    n = len(bufs)

    def body(*refs):
        g_refs = refs[:n]
        send_sems, recv_sems = refs[n + 1], refs[n + 2]
        token = refs[-1]
        x, y, c = _place()
        me_p = 2 * x + y
        for t in range(n):
            for k, (qx, qy) in enumerate(_other_chips(x, y)):
                slab = g_refs[t].at[me_p, c]
                pltpu.make_async_remote_copy(src_ref=slab, dst_ref=slab, send_sem=send_sems.at[3 * t + k],
                                             recv_sem=recv_sems.at[3 * t + k], device_id=(qx, qy, c),
                                             device_id_type=MESH).start()
        token[...] = jnp.zeros_like(token)

    res = pl.pallas_call(
        body, name="gather_start",
        in_specs=[HBM] * n + [ANY],
        out_specs=[SEM, SEM] + [HBM] * n + [pl.BlockSpec(memory_space=pltpu.VMEM)],
        out_shape=[pltpu.SemaphoreType.DMA((3 * n,)), pltpu.SemaphoreType.DMA((3 * n,))]
        + [pltpu.HBM(b.shape, b.dtype) for b in bufs] + [jax.ShapeDtypeStruct((8, LANES), F32)],
        input_output_aliases={t: 2 + t for t in range(n)},
        compiler_params=pltpu.CompilerParams(has_side_effects=EFFECT),
    )(*[_hbm(b) for b in bufs], after)
    return res[0], res[1], list(res[2:2 + n]), res[-1]


def gather_wait(bufs, send_sems, recv_sems, after, first=0):
    n = len(bufs)

    def body(*refs):
        g_refs = refs[:n]
        send_sems, recv_sems = refs[n], refs[n + 1]
        x, y, c = _place()
        me_p = 2 * x + y
        for t in range(n):
            for k, (qx, qy) in enumerate(_other_chips(x, y)):
                s = 3 * (first + t) + k
                cp = pltpu.make_async_remote_copy(src_ref=g_refs[t].at[me_p, c], dst_ref=g_refs[t].at[2 * qx + qy, c],
                                                  send_sem=send_sems.at[s], recv_sem=recv_sems.at[s],
                                                  device_id=(qx, qy, c), device_id_type=MESH)
                cp.wait_send()
                cp.wait_recv()

    return pl.pallas_call(
        body, name="gather_wait",
        in_specs=[HBM] * n + [SEM, SEM, ANY],
        out_specs=[HBM] * n,
        out_shape=[pltpu.HBM(b.shape, b.dtype) for b in bufs],
        input_output_aliases={t: t for t in range(n)},
        compiler_params=pltpu.CompilerParams(has_side_effects=EFFECT),
    )(*bufs, send_sems, recv_sems, after)


def gather_forward(bufs):
    n = len(bufs)

    def body(*refs):
        g_refs = refs[n:2 * n]
        send_sems, recv_sems = refs[2 * n:]
        x, y, c = _place()
        sibling = (x, y, 1 - c)
        chips = _other_chips(x, y)
        passed = []
        for t in range(n):
            for k, (qx, qy) in enumerate(chips):
                slab = g_refs[t].at[2 * qx + qy, c]
                fwd = pltpu.make_async_remote_copy(src_ref=slab, dst_ref=slab, send_sem=send_sems.at[3 * t + k],
                                                   recv_sem=recv_sems.at[3 * t + k], device_id=sibling,
                                                   device_id_type=MESH)
                fwd.start()
                passed.append(fwd)
        for t in range(n):
            for k, (qx, qy) in enumerate(chips):
                slab = g_refs[t].at[2 * qx + qy, 1 - c]
                pltpu.make_async_remote_copy(src_ref=slab, dst_ref=slab, send_sem=send_sems.at[3 * t + k],
                                             recv_sem=recv_sems.at[3 * t + k], device_id=sibling,
                                             device_id_type=MESH).wait_recv()
        for cp in passed:
            cp.wait_send()

    return pl.pallas_call(
        body, name="gather_forward",
        in_specs=[ANY] * n, out_specs=[ANY] * n,
        out_shape=[jax.ShapeDtypeStruct(b.shape, b.dtype) for b in bufs],
        input_output_aliases={t: t for t in range(n)},
        scratch_shapes=[pltpu.SemaphoreType.DMA((3 * n,)), pltpu.SemaphoreType.DMA((3 * n,))],
    )(*bufs)


def forward_start(bufs):
    n = len(bufs)

    def body(*refs):
        g_refs = refs[:n]
        send_sems, recv_sems = refs[n], refs[n + 1]
        token = refs[-1]
        x, y, c = _place()
        for t in range(n):
            for k, (qx, qy) in enumerate(_other_chips(x, y)):
                slab = g_refs[t].at[2 * qx + qy, c]
                pltpu.make_async_remote_copy(src_ref=slab, dst_ref=slab, send_sem=send_sems.at[3 * t + k],
                                             recv_sem=recv_sems.at[3 * t + k], device_id=(x, y, 1 - c),
                                             device_id_type=MESH).start()
        token[...] = jnp.zeros_like(token)

    res = pl.pallas_call(
        body, name="forward_start",
        in_specs=[HBM] * n,
        out_specs=[SEM, SEM] + [HBM] * n + [pl.BlockSpec(memory_space=pltpu.VMEM)],
        out_shape=[pltpu.SemaphoreType.DMA((3 * n,)), pltpu.SemaphoreType.DMA((3 * n,))]
        + [pltpu.HBM(b.shape, b.dtype) for b in bufs] + [jax.ShapeDtypeStruct((8, LANES), F32)],
        input_output_aliases={t: 2 + t for t in range(n)},
        compiler_params=pltpu.CompilerParams(has_side_effects=EFFECT),
    )(*[_hbm(b) for b in bufs])
    return res[0], res[1], list(res[2:2 + n]), res[-1]


def forward_wait(bufs, send_sems, recv_sems, after):
    n = len(bufs)

    def body(*refs):
        g_refs = refs[:n]
        send_sems, recv_sems = refs[n], refs[n + 1]
        x, y, c = _place()
        for t in range(n):
            for k, (qx, qy) in enumerate(_other_chips(x, y)):
                cp = pltpu.make_async_remote_copy(src_ref=g_refs[t].at[2 * qx + qy, c],
                                                  dst_ref=g_refs[t].at[2 * qx + qy, 1 - c],
                                                  send_sem=send_sems.at[3 * t + k], recv_sem=recv_sems.at[3 * t + k],
                                                  device_id=(x, y, 1 - c), device_id_type=MESH)
                cp.wait_send()
                cp.wait_recv()

    return pl.pallas_call(
        body, name="forward_wait",
        in_specs=[HBM] * n + [SEM, SEM, ANY],
        out_specs=[HBM] * n,
        out_shape=[pltpu.HBM(b.shape, b.dtype) for b in bufs],
        input_output_aliases={t: t for t in range(n)},
        compiler_params=pltpu.CompilerParams(has_side_effects=EFFECT),
    )(*bufs, send_sems, recv_sems, after)


def small_start(pack, me, after):
    buf = lax.dynamic_update_slice(lax.empty((8,) + pack.shape, pack.dtype), pack[None], (me, 0, 0))

    def body(b_ref, after_ref, send_sems, recv_sems, thru, token):
        x, y, c = _place()
        slot = b_ref.at[4 * x + 2 * y + c]
        for k in range(1, 8):
            peer = (x ^ ((k >> 2) & 1), y ^ ((k >> 1) & 1), c ^ (k & 1))
            pltpu.make_async_remote_copy(src_ref=slot, dst_ref=slot, send_sem=send_sems.at[k - 1],
                                         recv_sem=recv_sems.at[k - 1], device_id=peer, device_id_type=MESH).start()
        token[...] = jnp.zeros_like(token)

    return pl.pallas_call(
        body, name="small_start",
        in_specs=[HBM, ANY],
        out_specs=[SEM, SEM, HBM, pl.BlockSpec(memory_space=pltpu.VMEM)],
        out_shape=[pltpu.SemaphoreType.DMA((7,)), pltpu.SemaphoreType.DMA((7,)), pltpu.HBM(buf.shape, buf.dtype),
                   jax.ShapeDtypeStruct((8, LANES), F32)],
        input_output_aliases={0: 2},
        compiler_params=pltpu.CompilerParams(has_side_effects=EFFECT),
    )(_hbm(buf), after)


def small_wait(buf, send_sems, recv_sems, after):
    def body(b_ref, send_sems, recv_sems, after_ref, thru):
        x, y, c = _place()
        mine = b_ref.at[4 * x + 2 * y + c]
        for k in range(1, 8):
            peer = (x ^ ((k >> 2) & 1), y ^ ((k >> 1) & 1), c ^ (k & 1))
            cp = pltpu.make_async_remote_copy(src_ref=mine, dst_ref=b_ref.at[4 * peer[0] + 2 * peer[1] + peer[2]],
                                              send_sem=send_sems.at[k - 1], recv_sem=recv_sems.at[k - 1],
                                              device_id=peer, device_id_type=MESH)
            cp.wait_send()
            cp.wait_recv()

    return pl.pallas_call(
        body, name="small_wait",
        in_specs=[HBM, SEM, SEM, ANY], out_specs=HBM,
        out_shape=pltpu.HBM(buf.shape, buf.dtype),
        input_output_aliases={0: 0},
        compiler_params=pltpu.CompilerParams(has_side_effects=EFFECT),
    )(buf, send_sems, recv_sems, after)


def swap_start(grads, after):
    n = len(grads)

    def body(*refs):
        g_refs, l_refs = refs[:n], refs[n:2 * n]
        send_sems, recv_sems = refs[2 * n + 1], refs[2 * n + 2]
        token = refs[-1]
        x, y, c = _place()
        for t in range(n):
            for p in range(N_CHIPS):
                pltpu.make_async_remote_copy(src_ref=g_refs[t].at[p, 1 - c], dst_ref=l_refs[t].at[p],
                                             send_sem=send_sems.at[N_CHIPS * t + p],
                                             recv_sem=recv_sems.at[N_CHIPS * t + p],
                                             device_id=(x, y, 1 - c), device_id_type=MESH).start()
        token[...] = jnp.zeros_like(token)

    lands = [lax.empty((N_CHIPS,) + g.shape[2:], g.dtype) for g in grads]
    res = pl.pallas_call(
        body, name="swap_start",
        in_specs=[HBM] * (2 * n) + [ANY],
        out_specs=[SEM, SEM] + [HBM] * (2 * n) + [pl.BlockSpec(memory_space=pltpu.VMEM)],
        out_shape=[pltpu.SemaphoreType.DMA((N_CHIPS * n,)), pltpu.SemaphoreType.DMA((N_CHIPS * n,))]
        + [pltpu.HBM(a.shape, a.dtype) for a in grads + lands] + [jax.ShapeDtypeStruct((8, LANES), F32)],
        input_output_aliases={t: 2 + t for t in range(2 * n)},
        compiler_params=pltpu.CompilerParams(has_side_effects=EFFECT),
    )(*[_hbm(a) for a in grads + lands], after)
    return res[0], res[1], list(res[2:2 + n]), list(res[2 + n:2 + 2 * n]), res[-1]


def swap_wait(grads, lands, send_sems, recv_sems, after):
    n = len(grads)

    def body(*refs):
        g_refs, l_refs = refs[:n], refs[n:2 * n]
        send_sems, recv_sems = refs[2 * n], refs[2 * n + 1]
        x, y, c = _place()
        for t in range(n):
            for p in range(N_CHIPS):
                cp = pltpu.make_async_remote_copy(src_ref=g_refs[t].at[p, 1 - c], dst_ref=l_refs[t].at[p],
                                                  send_sem=send_sems.at[N_CHIPS * t + p],
                                                  recv_sem=recv_sems.at[N_CHIPS * t + p],
                                                  device_id=(x, y, 1 - c), device_id_type=MESH)
                cp.wait_send()
                cp.wait_recv()

    res = pl.pallas_call(
        body, name="swap_wait",
        in_specs=[HBM] * (2 * n) + [SEM, SEM, ANY],
        out_specs=[HBM] * (2 * n),
        out_shape=[pltpu.HBM(a.shape, a.dtype) for a in grads + lands],
        input_output_aliases={t: t for t in range(2 * n)},
        compiler_params=pltpu.CompilerParams(has_side_effects=EFFECT),
    )(*grads, *lands, send_sems, recv_sems, after)
    return list(res[:n]), list(res[n:])


def _row_tile(r):
    return max(t for t in range(16, 513, 16) if r % t == 0)


def add_own_half(g, other, c_arr):
    _, _, r, cols = g.shape
    tr = _row_tile(r)

    def body(c_ref, a_ref, b_ref, o_ref):
        o_ref[...] = (a_ref[...] + b_ref[...]).astype(BF16)

    return pl.pallas_call(
        body, name="add_own_half",
        grid_spec=pltpu.PrefetchScalarGridSpec(
            num_scalar_prefetch=1, grid=(N_CHIPS, r // tr),
            in_specs=[pl.BlockSpec((None, None, tr, cols), lambda p, i, c_ref: (p, c_ref[0], i, 0)),
                      pl.BlockSpec((None, tr, cols), lambda p, i, c_ref: (p, i, 0))],
            out_specs=pl.BlockSpec((None, tr, cols), lambda p, i, c_ref: (p, i, 0))),
        out_shape=jax.ShapeDtypeStruct((N_CHIPS, r, cols), BF16),
        compiler_params=_cparams("parallel", "parallel"),
    )(c_arr, g, other)


def scatter_start(partials):
    n = len(partials)

    def body(*refs):
        s_refs, l_refs = refs[:n], refs[n:2 * n]
        send_sems, recv_sems = refs[2 * n], refs[2 * n + 1]
        token = refs[-1]
        x, y, c = _place()
        for t in range(n):
            for k, (qx, qy) in enumerate(_other_chips(x, y)):
                pltpu.make_async_remote_copy(src_ref=s_refs[t].at[2 * qx + qy], dst_ref=l_refs[t].at[k],
                                             send_sem=send_sems.at[3 * t + k], recv_sem=recv_sems.at[3 * t + k],
                                             device_id=(qx, qy, c), device_id_type=MESH).start()
        token[...] = jnp.zeros_like(token)

    lands = [lax.empty((3,) + s.shape[1:], s.dtype) for s in partials]
    res = pl.pallas_call(
        body, name="scatter_start",
        in_specs=[HBM] * (2 * n),
        out_specs=[SEM, SEM] + [HBM] * (2 * n) + [pl.BlockSpec(memory_space=pltpu.VMEM)],
        out_shape=[pltpu.SemaphoreType.DMA((3 * n,)), pltpu.SemaphoreType.DMA((3 * n,))]
        + [pltpu.HBM(a.shape, a.dtype) for a in partials + lands] + [jax.ShapeDtypeStruct((8, LANES), F32)],
        input_output_aliases={t: 2 + t for t in range(2 * n)},
        compiler_params=pltpu.CompilerParams(has_side_effects=EFFECT),
    )(*[_hbm(a) for a in partials + lands])
    return res[0], res[1], list(res[2:2 + n]), list(res[2 + n:2 + 2 * n]), res[-1]


def scatter_wait(partials, lands, send_sems, recv_sems, after):
    n = len(partials)

    def body(*refs):
        s_refs, l_refs = refs[:n], refs[n:2 * n]
        send_sems, recv_sems = refs[2 * n], refs[2 * n + 1]
        x, y, c = _place()
        for t in range(n):
            for k, (qx, qy) in enumerate(_other_chips(x, y)):
                cp = pltpu.make_async_remote_copy(src_ref=s_refs[t].at[2 * qx + qy], dst_ref=l_refs[t].at[k],
                                                  send_sem=send_sems.at[3 * t + k], recv_sem=recv_sems.at[3 * t + k],
                                                  device_id=(qx, qy, c), device_id_type=MESH)
                cp.wait_send()
                cp.wait_recv()

    res = pl.pallas_call(
        body, name="scatter_wait",
        in_specs=[HBM] * (2 * n) + [SEM, SEM, ANY],
        out_specs=[HBM] * (2 * n),
        out_shape=[pltpu.HBM(a.shape, a.dtype) for a in partials + lands],
        input_output_aliases={t: t for t in range(2 * n)},
        compiler_params=pltpu.CompilerParams(has_side_effects=EFFECT),
    )(*partials, *lands, send_sems, recv_sems, after)
    return list(res[:n]), list(res[n:])


def sum_chips(own, parts, where):
    _, r, cols = own.shape
    tr = _row_tile(r)

    def body(w_ref, a_ref, p_ref, o_ref):
        acc = a_ref[...].astype(F32)
        for k in range(3):
            acc = acc + p_ref[k].astype(F32)
        o_ref[...] = acc

    return pl.pallas_call(
        body, name="sum_chips",
        grid_spec=pltpu.PrefetchScalarGridSpec(
            num_scalar_prefetch=1, grid=(r // tr,),
            in_specs=[pl.BlockSpec((None, tr, cols), lambda i, w_ref: (w_ref[0], i, 0)),
                      pl.BlockSpec((3, tr, cols), lambda i, w_ref: (0, i, 0))],
            out_specs=pl.BlockSpec((None, tr, cols), lambda i, w_ref: (w_ref[1], i, 0))),
        out_shape=jax.ShapeDtypeStruct((DEPTH, r, cols), F32),
        compiler_params=_cparams("parallel"),
    )(where, own, parts)


def sibling_share_layer(bufs):
    n = len(bufs)

    def body(*refs):
        o_refs = refs[n:2 * n]
        send_sems, recv_sems = refs[2 * n:]
        x, y, c = _place()
        cps = []
        for t in range(n):
            cp = pltpu.make_async_remote_copy(src_ref=o_refs[t].at[c], dst_ref=o_refs[t].at[c], send_sem=send_sems.at[t],
                                              recv_sem=recv_sems.at[t], device_id=(x, y, 1 - c), device_id_type=MESH)
            cp.start()
            cps.append(cp)
        for t in range(n):
            slot = o_refs[t].at[1 - c]
            pltpu.make_async_remote_copy(src_ref=slot, dst_ref=slot, send_sem=send_sems.at[t], recv_sem=recv_sems.at[t],
                                         device_id=(x, y, 1 - c), device_id_type=MESH).wait_recv()
        for cp in cps:
            cp.wait_send()

    return pl.pallas_call(
        body, name="sibling_share_layer",
        in_specs=[ANY] * n, out_specs=[ANY] * n,
        out_shape=[jax.ShapeDtypeStruct(b.shape, b.dtype) for b in bufs],
        input_output_aliases={t: t for t in range(n)},
        scratch_shapes=[pltpu.SemaphoreType.DMA((n,)), pltpu.SemaphoreType.DMA((n,))],
    )(*bufs)


SP_META = 2 * (N_META * D_MODEL // LANES)
SP_NORM = DEPTH * D_MODEL // LANES
SP_RB = DEPTH * N_BUCKETS
SP_SINK = DEPTH * ATT_HEADS
SP_CONV = DEPTH * 3 * BRANCH_WIDTH // LANES
SP_LOSS = 8
SIDE_ROWS = 48
SP_ROWS = SP_META + 2 * SP_NORM + SP_RB + SP_SINK + SP_CONV + SP_LOSS


def sum_small(slots):
    half = SP_META // 2
    rb0 = SP_META + 2 * SP_NORM
    rest_rows = SP_ROWS - SP_META

    def body(s_ref, meta_ref, rest_ref):
        acc = s_ref[0]
        for d in range(1, 8):
            acc = acc + s_ref[d]
        meta_ref[...] = acc[0:half] + acc[half:SP_META]
        rest_ref[...] = acc[SP_META:]
        rest_ref[rb0 - SP_META:rb0 - SP_META + N_BUCKETS, :] = (
            acc[rb0:rb0 + N_BUCKETS] + acc[rb0 + N_BUCKETS:rb0 + 2 * N_BUCKETS])

    vm = pl.BlockSpec(memory_space=pltpu.VMEM)
    return pl.pallas_call(
        body, name="sum_small",
        in_specs=[vm], out_specs=[vm, vm],
        out_shape=[jax.ShapeDtypeStruct((half, LANES), F32), jax.ShapeDtypeStruct((rest_rows, LANES), F32)],
    )(slots)


def local_step(x, loss_target, meta_full, rel_bias, norm_pre, conv_w_full, attn_sinks, norm_post, weights_of, mid_fwd,
               grads_done, bwd_done):
    nb, seq, _ = x.shape
    nc = seq // BLOCK + 1
    lp = nc * BLOCK
    rows = nb * lp
    pad = jnp.zeros((nb, PAD_FRONT, D_MODEL), F32)
    meta = jnp.broadcast_to(meta_full[None], (nb, N_META, D_MODEL))
    h0 = jnp.concatenate([pad, meta, x], axis=1).reshape(rows, D_MODEL)
    target = jnp.pad(loss_target, ((0, 0), (BLOCK, 0), (0, 0))).reshape(rows, D_MODEL)
    cosf, sinf = _rot_tables(lp)
    bkt = jnp.asarray(_bucket_table())

    g_pre = norm_pre.reshape(DEPTH, 1, D_MODEL)
    g_post = norm_post.reshape(DEPTH, 1, D_MODEL)
    order = lambda token: bkt if token is None else token

    acts = []
    h = h0
    for l in range(DEPTH):
        w_in, token = weights_of(l, h)
        hb, p_abc = norm_matmul(h, g_pre, l, w_in, 0, N_ABC_TILES, order(token))
        p_m = matmul_cols(hb, w_in, N_ABC_TILES, N_M_TILES)
        br, states = mixers_fwd(p_abc, cosf, sinf, bkt, rel_bias, attn_sinks, conv_w_full, l, nb, nc)
        (w_br, w_out), token = mid_fwd(l, br)
        acts.append((h, hb, p_abc, p_m, br, states, w_in, w_br, w_out))
        if l < DEPTH - 1:
            h = merge_fwd(h, br, p_m, w_br, w_out, g_post, l, order(token))
        else:
            assert token is None
            loss_part, d_h = merge_fwd_loss(h, br, p_m, w_br, w_out, g_post, l, target, lp)

    small = [None] * DEPTH
    token = None
    for l in reversed(range(DEPTH)):
        h_in, hb, p_abc, p_m, br, states, w_in, w_br, w_out = acts[l]
        d_br, d_m, d_gpost, g_wbr, g_wout = merge_bwd(d_h, br, p_m, w_br, w_out, g_post, l, order(token))
        d_abc, d_rb, d_sk, d_cw = mixers_bwd(p_abc, d_br, states, cosf, sinf, bkt, rel_bias,
                                             attn_sinks, conv_w_full, l, nb, nc)
        g_win = proj_wgrad(hb, d_abc, d_m)
        token = grads_done(l, [g_win, g_wbr, g_wout])
        d_h, d_gpre = proj_dgrad(d_abc, d_m, w_in, h_in, g_pre, l, d_h, order(token))
        token = bwd_done(l, d_h)
        small[l] = (d_gpre[0], d_gpost[0], d_rb, d_sk, d_cw[0:3])

    d_h3 = d_h.reshape(nb, lp, D_MODEL)
    d_x = d_h3[:, BLOCK:]
    d_meta = d_h3[:, PAD_FRONT:BLOCK]
    sp = jnp.concatenate([
        d_meta.reshape(-1, LANES),
        jnp.stack([small[l][0] for l in range(DEPTH)]).reshape(-1, LANES),
        jnp.stack([small[l][1] for l in range(DEPTH)]).reshape(-1, LANES),
        jnp.concatenate([small[l][2] for l in range(DEPTH)], axis=0),
        jnp.concatenate([small[l][3] for l in range(DEPTH)], axis=0),
        jnp.stack([small[l][4] for l in range(DEPTH)]).reshape(-1, LANES),
        loss_part], axis=0)
    return d_x, sp


def kernel(x, meta_tokens, rel_bias, norm_pre, w_in, conv_w, attn_sinks, w_branch, w_out, norm_post, loss_target, m_meta_tokens, m_rel_bias, m_norm_pre, m_w_in, m_conv_w, m_attn_sinks, m_w_branch, m_w_out, m_norm_post, v_meta_tokens, v_rel_bias, v_norm_pre, v_w_in, v_conv_w, v_attn_sinks, v_w_branch, v_w_out, v_norm_post):
    assert x.shape[0] == 2 and SP_META == 2 * N_META * D_MODEL // LANES
    px, py, pc = _place()
    chip = 2 * px + py

    c_arr = jnp.reshape(pc, (1,)).astype(jnp.int32)
    where = jnp.stack([chip, pc]).astype(jnp.int32)
    tr_ = lambda a: jnp.swapaxes(a, 1, 2)
    w3 = [tr_(w_in), w_branch.reshape(DEPTH, N_BRANCH * BRANCH_WIDTH, SHARD_D), w_out]
    halves = lambda a: a.reshape(2, a.shape[0] // 2, a.shape[1])

    def as_weights(bufs):
        a_in, a_br, a_out = bufs
        return (a_in.reshape(PROJ_WIDTH, D_MODEL), a_br.reshape(N_CHIPS, N_BRANCH, BRANCH_WIDTH, SHARD_D),
                a_out.reshape(D_MODEL, D_MODEL))

    n_meta_rows = N_META * SHARD_D // LANES
    side = jnp.concatenate([meta_tokens.reshape(-1), conv_w.reshape(-1)]).reshape(-1, LANES)
    side = jnp.concatenate([side, jnp.zeros((SIDE_ROWS - side.shape[0], LANES), F32)], axis=0)
    slots = [[_own_slot(halves(w[l].astype(BF16)), chip) for w in w3] for l in range(DEPTH)]
    send0, recv0, flying0, _ = gather_start([_own_slot(halves(side), chip)] + slots[0], where)
    side_chips = gather_forward(gather_wait(flying0[:1], send0, recv0, where))[0].reshape(N_CHIPS, SIDE_ROWS, LANES)
    meta_full = jnp.moveaxis(side_chips[:, :n_meta_rows].reshape(N_CHIPS, N_META, SHARD_D), 0, 1).reshape(N_META, D_MODEL)
    conv_full = jnp.moveaxis(side_chips[:, n_meta_rows:n_meta_rows + 6].reshape(N_CHIPS, DEPTH, 3, LANES), 0, 2).reshape(DEPTH, 3, BRANCH_WIDTH)
    inbound = {}

    def weights_of(l, h):
        if l == 0:
            inbound[0] = gather_forward(gather_wait(flying0[1:2], send0, recv0, h, first=1))
            inbound[1] = gather_start(slots[1], inbound[0][0])
            return inbound[0][0].reshape(PROJ_WIDTH, D_MODEL), inbound[1][3]
        send, recv, thru = inbound[1]
        inbound[1] = as_weights(forward_wait(thru, send, recv, h))
        return inbound[1][0], None

    def mid_fwd(l, br):
        if l == 0:
            rest = gather_forward(gather_wait(flying0[2:], send0, recv0, br, first=2))
            send, recv, flying1, _ = inbound[1]
            send, recv, thru, started = forward_start(gather_wait(flying1, send, recv, rest[0]))
            inbound[1] = (send, recv, thru)
            return as_weights(inbound[0] + rest)[1:], started
        return inbound[1][1:], None

    reduced = [None] * DEPTH
    flying = {}

    def finish_reduce(l, after):
        partials, parts = scatter_wait(*flying[l], after)
        reduced[l] = sibling_share_layer([sum_chips(a, p, where) for a, p in zip(partials, parts)])

    def start_scatter(l, full, others):
        send, recv, thru, lands, started = scatter_start([add_own_half(g, o, c_arr) for g, o in zip(full, others)])
        flying[l] = (thru, lands, send, recv)
        return started

    m3 = [tr_(m_w_in), m_w_branch.reshape(w3[1].shape), m_w_out]
    v3 = [tr_(v_w_in), v_w_branch.reshape(w3[1].shape), v_w_out]
    big = [None] * 3

    def adamw_of(l, after):
        for t in range(3):
            big[t] = adamw_layer(w3[t], reduced[l][t].reshape(w3[t].shape[1:]), m3[t], v3[t], l, big[t], after)
            after = big[t][1]

    def grads_done(l, grads):
        full = [g.reshape(N_CHIPS, 2, g.size // (2 * N_CHIPS * g.shape[-1]), g.shape[-1]) for g in grads]
        if l == 0:
            finish_reduce(1, grads[0])
        send, recv, thru, lands, started = swap_start(full, where if l == 1 else reduced[1][0])
        if l == 1:
            flying["swap"] = (thru, lands, send, recv)
            return started
        adamw_of(1, started)
        return start_scatter(0, *swap_wait(thru, lands, send, recv, big[2][1]))

    def bwd_done(l, d_h):
        if l == 1:
            return start_scatter(1, *swap_wait(*flying["swap"], d_h))
        return None

    d_x, sp = local_step(x, loss_target, meta_full, rel_bias, norm_pre, conv_full, attn_sinks, norm_post,
                         weights_of, mid_fwd, grads_done, bwd_done)
    finish_reduce(0, sp)

    s_send, s_recv, s_buf, s_started = small_start(sp, 4 * px + 2 * py + pc, reduced[0][0])

    adamw_of(0, s_started)
    g_in, *u_in = [tr_(a) for a in big[0]]
    g_br, *u_br = [a.reshape(w_branch.shape) for a in big[1]]
    g_out, *u_out = big[2]

    meta_rows, rest = sum_small(small_wait(s_buf, s_send, s_recv, big[2][1]))
    o = 0
    g_meta_full = meta_rows.reshape(N_META, D_MODEL)
    g_norm_pre = rest[o:o + SP_NORM].reshape(DEPTH, D_MODEL); o += SP_NORM
    g_norm_post = rest[o:o + SP_NORM].reshape(DEPTH, D_MODEL); o += SP_NORM
    g_rel_bias = rest[o:o + N_BUCKETS, :ATT_HEADS]; o += SP_RB
    g_sinks = rest[o:o + SP_SINK, 0].reshape(DEPTH, ATT_HEADS); o += SP_SINK
    g_conv_full = rest[o:o + SP_CONV].reshape(DEPTH, 3, BRANCH_WIDTH); o += SP_CONV
    loss = rest[o, 0]
    g_meta = lax.dynamic_slice_in_dim(g_meta_full, chip * SHARD_D, SHARD_D, axis=1)
    g_conv = lax.dynamic_slice_in_dim(g_conv_full, chip * LANES, LANES, axis=2)

    smalls = [(meta_tokens, g_meta, m_meta_tokens, v_meta_tokens),
              (rel_bias, g_rel_bias, m_rel_bias, v_rel_bias),
              (norm_pre, g_norm_pre, m_norm_pre, v_norm_pre),
              (conv_w, g_conv, m_conv_w, v_conv_w),
              (attn_sinks, g_sinks, m_attn_sinks, v_attn_sinks),
              (norm_post, g_norm_post, m_norm_post, v_norm_post)]
    u_meta, u_rb, u_npre, u_conv, u_sink, u_npost = adamw_small(smalls)

    grads = [g_meta, g_rel_bias, g_norm_pre, g_in, g_conv, g_sinks, g_br, g_out, g_norm_post]
    upd = [u_meta, u_rb, u_npre, u_in, u_conv, u_sink, u_br, u_out, u_npost]
    return (loss, d_x, *grads, *[u[0] for u in upd], *[u[1] for u in upd], *[u[2] for u in upd])
```

```python
import math

import numpy as np
import jax
import jax.numpy as jnp
from jax import lax
from jax.experimental import pallas as pl
from jax.experimental.pallas import tpu as pltpu

F32 = jnp.float32
BF16 = jnp.bfloat16
MESH = pl.DeviceIdType.MESH

D_MODEL = 1024
DEPTH = 2
N_META = 16
BLOCK = 128
PAD_FRONT = BLOCK - N_META
ATT_HEADS = 8
ATT_HEAD_DIM = 64
N_BUCKETS = 32
MAX_EXACT = 16
MAX_DISTANCE = 128
RET_HEADS = 4
ROT_BASE = 10000.0
N_BRANCH = 3
BRANCH_WIDTH = 512
PROJ_WIDTH = 8448
ABC_WIDTH = 5376
MERGE_WIDTH = N_BRANCH * D_MODEL
RMS_EPS = 1e-6
GN_EPS = 1e-6
NEG_INF = -1e30
ATT_SCALE = ATT_HEAD_DIM ** -0.5
RET_SCALE = BLOCK ** -0.5
LOG_GAMMA = tuple(math.log1p(-(2.0 ** (-5.0 - h))) for h in range(RET_HEADS))

C_AQ, C_AK, C_AV, C_AG = 0, 512, 640, 768
C_RQ, C_RK, C_RV, C_RG = 1280, 1792, 2304, 2816
C_CB, C_CC, C_CX, C_CG = 3328, 3840, 4352, 4864

ADAM_LR = 0.001
ADAM_B1 = 0.9
ADAM_B2 = 0.999
ADAM_EPS = 1e-08
ADAM_WD = 0.01
ADAM_STEP = 10

N_CHIPS = 4
SHARD_D = D_MODEL // N_CHIPS
LANES = 128

VMEM_LIMIT = 56 * 1024 * 1024
COL_TILE = 768
ROW_TILE = 1088
PROJ_ROW_TILE = 2176


def _cparams(*sem):
    return pltpu.CompilerParams(dimension_semantics=sem, vmem_limit_bytes=VMEM_LIMIT)


def _nt(a, b):
    return lax.dot_general(a, b, (((1,), (1,)), ((), ())), preferred_element_type=F32)


def _tn(a, b):
    return lax.dot_general(a, b, (((0,), (0,)), ((), ())), preferred_element_type=F32)


def _nn(a, b):
    return jnp.dot(a, b, preferred_element_type=F32)


def _sigmoid(x):
    return 0.5 * jnp.tanh(0.5 * x) + 0.5


def _silu(x):
    return x * _sigmoid(x)


def _dsilu(x):
    s = _sigmoid(x)
    return s * (1.0 + x * (1.0 - s))


def _bucket_table():
    r = np.arange(BLOCK)[:, None]
    c = np.arange(2 * BLOCK)[None, :]
    n = np.maximum(BLOCK + r - c, 0)
    nf = np.maximum(n, 1).astype(np.float32)
    large = MAX_EXACT + (np.log(nf / MAX_EXACT) / math.log(MAX_DISTANCE / MAX_EXACT)
                         * (N_BUCKETS - MAX_EXACT)).astype(np.int32)
    large = np.minimum(large, N_BUCKETS - 1)
    return np.where(n < MAX_EXACT, n, large).astype(np.int32)


def _rot_tables(lp):
    half = BLOCK // 2
    pos = (jnp.arange(lp) - PAD_FRONT).astype(F32)
    theta = 1.0 / (ROT_BASE ** jnp.linspace(0.0, 1.0, half, dtype=F32))
    ang = pos[:, None] * theta[None, :]
    cos, sin = jnp.cos(ang), jnp.sin(ang)
    return jnp.concatenate([cos, cos], axis=1), jnp.concatenate([-sin, sin], axis=1)


def norm_matmul(x2d, g, layer, w, col0_blocks, n_col_blocks, after):
    t = x2d.shape[0]
    tm = PROJ_ROW_TILE if t % PROJ_ROW_TILE == 0 else BLOCK

    def body(x_ref, g_ref, w_ref, after_ref, hb_ref, o_ref):
        @pl.when(pl.program_id(1) == 0)
        def _():
            x = x_ref[...]
            r = lax.rsqrt(jnp.mean(x * x, axis=-1, keepdims=True) + RMS_EPS)
            hb_ref[...] = (x * r * g_ref[...]).astype(BF16)

        o_ref[...] = _nt(hb_ref[...], w_ref[...])

    return pl.pallas_call(
        body, name="norm_matmul",
        grid=(t // tm, n_col_blocks),
        in_specs=[pl.BlockSpec((tm, D_MODEL), lambda i, j: (i, 0)),
                  pl.BlockSpec((None, 1, D_MODEL), lambda i, j: (layer, 0, 0)),
                  pl.BlockSpec((COL_TILE, D_MODEL), lambda i, j: (j + col0_blocks, 0)),
                  ANY],
        out_specs=[pl.BlockSpec((tm, D_MODEL), lambda i, j: (i, 0)),
                   pl.BlockSpec((tm, COL_TILE), lambda i, j: (i, j))],
        out_shape=[jax.ShapeDtypeStruct((t, D_MODEL), BF16),
                   jax.ShapeDtypeStruct((t, n_col_blocks * COL_TILE), F32)],
        compiler_params=_cparams("parallel", "arbitrary"),
    )(x2d, g, w, after)


def matmul_cols(a, w, col0_blocks, n_col_blocks):
    t, k = a.shape
    tm = PROJ_ROW_TILE if t % PROJ_ROW_TILE == 0 else BLOCK

    def body(a_ref, w_ref, o_ref):
        o_ref[...] = _nt(a_ref[...], w_ref[...]).astype(BF16)

    return pl.pallas_call(
        body, name="matmul_cols",
        grid=(t // tm, n_col_blocks),
        in_specs=[pl.BlockSpec((tm, k), lambda i, j: (i, 0)),
                  pl.BlockSpec((COL_TILE, k), lambda i, j: (j + col0_blocks, 0))],
        out_specs=pl.BlockSpec((tm, COL_TILE), lambda i, j: (i, j)),
        out_shape=jax.ShapeDtypeStruct((t, n_col_blocks * COL_TILE), BF16),
        compiler_params=_cparams("parallel", "arbitrary"),
    )(a, w)


class _Widened:
    def __init__(self, ref):
        self.ref = ref

    def __getitem__(self, idx):
        return self.ref[idx].astype(F32)


def _build_bias(bkt_ref, rb_ref, bias_s):
    bkt = bkt_ref[...]
    for h in range(ATT_HEADS):
        acc = jnp.zeros((BLOCK, 2 * BLOCK), F32)
        for b in range(N_BUCKETS):
            acc = jnp.where(bkt == b, rb_ref[b, h], acc)
        bias_s[h] = acc


def _band_mask(n):
    r = lax.broadcasted_iota(jnp.int32, (BLOCK, 2 * BLOCK), 0)
    c = lax.broadcasted_iota(jnp.int32, (BLOCK, 2 * BLOCK), 1)
    key_pos = (n - 1) * BLOCK + c
    return (c > r) & (c <= r + BLOCK) & (key_pos >= PAD_FRONT)


def _split_heads(kv, kh):
    lane = lax.broadcasted_iota(jnp.int32, kv.shape, 1)
    if kh == 0:
        lo = jnp.where(lane < ATT_HEAD_DIM, kv, 0.0)
        hi = pltpu.roll(lo, ATT_HEAD_DIM, 1)
    else:
        hi = jnp.where(lane >= ATT_HEAD_DIM, kv, 0.0)
        lo = pltpu.roll(hi, ATT_HEAD_DIM, 1)
    return lo, hi


def _merge_heads(acc_lo, acc_hi, kh):
    lane = lax.broadcasted_iota(jnp.int32, acc_lo.shape, 1)
    if kh == 0:
        return jnp.where(lane < ATT_HEAD_DIM, acc_lo + pltpu.roll(acc_hi, ATT_HEAD_DIM, 1), 0.0)
    return jnp.where(lane >= ATT_HEAD_DIM, acc_hi + pltpu.roll(acc_lo, ATT_HEAD_DIM, 1), 0.0)


def _softmax_of(qk, bias_h, mask, sink_h):
    s = qk + bias_h
    s = jnp.where(mask, s, NEG_INF)
    m = jnp.maximum(jnp.max(s, axis=-1, keepdims=True), sink_h)
    p = jnp.exp(s - m)
    es = jnp.exp(sink_h - m)
    inv = 1.0 / (jnp.sum(p, axis=-1, keepdims=True) + es)
    return p * inv, es * inv


def _rot(t, cosf, sinf):
    return t * cosf + pltpu.roll(t, BLOCK // 2, 1) * sinf


def _rot_t(d, cosf, sinf):
    return d * cosf + pltpu.roll(d * sinf, BLOCK // 2, 1)


def _decay_tables(h):
    lg = LOG_GAMMA[h]
    i = lax.broadcasted_iota(jnp.int32, (BLOCK, BLOCK), 0)
    j = lax.broadcasted_iota(jnp.int32, (BLOCK, BLOCK), 1)
    diff = (i - j).astype(F32)
    dm = jnp.where(diff >= 0, jnp.exp(diff * lg), 0.0)
    row = lax.broadcasted_iota(jnp.int32, (BLOCK, 1), 0).astype(F32)
    zeta = jnp.exp((BLOCK - 1 - row) * lg)
    xi = jnp.exp((row + 1.0) * lg)
    return dm, zeta, xi, math.exp(BLOCK * lg)


def _valid_col(n):
    row = lax.broadcasted_iota(jnp.int32, (BLOCK, 1), 0)
    return ((n * BLOCK + row) >= PAD_FRONT).astype(F32)


def _shift_down(cur, prev, k):
    row = lax.broadcasted_iota(jnp.int32, cur.shape, 0)
    return jnp.where(row >= k, pltpu.roll(cur, k, 0), pltpu.roll(prev, k, 0))


def _shift_up(cur, nxt, k):
    row = lax.broadcasted_iota(jnp.int32, cur.shape, 0)
    return jnp.where(row < BLOCK - k, pltpu.roll(cur, BLOCK - k, 0), pltpu.roll(nxt, BLOCK - k, 0))


def mixers_fwd(proj, cosf, sinf, bkt, rel_bias, sinks, conv_w, layer, nb, nc):
    def body(p_ref, cos_ref, sin_ref, bkt_ref, rb_ref, sk_ref, cw_ref, br_ref, st_ref,
             bias_s, kv_s, state_s, u_s):
        p_ref = _Widened(p_ref)
        n = pl.program_id(0)

        @pl.when(n == 0)
        def _():
            _build_bias(bkt_ref, rb_ref, bias_s)
            kv_s[:, 0:BLOCK, :] = jnp.zeros((nb, BLOCK, 2 * BLOCK), F32)
            state_s[...] = jnp.zeros_like(state_s)
            u_s[...] = jnp.zeros_like(u_s)

        valid = _valid_col(n)
        mask = _band_mask(n)
        ex = range(nb)

        for b in ex:
            kv_s[b, BLOCK:2 * BLOCK, :] = p_ref[b, :, C_AK:C_AK + 2 * BLOCK]
        for kh in range(2):
            ks = [[t.astype(BF16) for t in _split_heads(kv_s[b, :, 0:BLOCK], kh)] for b in ex]
            vs = [[t.astype(BF16) for t in _split_heads(kv_s[b, :, BLOCK:2 * BLOCK], kh)] for b in ex]
            pairs = [(b, 2 * kh + jj) for jj in range(2) for b in ex]
            subs = [(b, j, x) for (b, j) in pairs for x in range(2)]
            qb_ = {(b, j): (p_ref[b, :, C_AQ + BLOCK * j:C_AQ + BLOCK * (j + 1)] * ATT_SCALE).astype(BF16)
                   for (b, j) in pairs}
            qk_ = {(b, j, x): _nt(qb_[(b, j)], ks[b][x]) for (b, j, x) in subs}
            pb_ = {}
            for u in subs:
                h = 2 * u[1] + u[2]
                pb_[u] = _softmax_of(qk_[u], bias_s[h], mask, sk_ref[layer, h])[0].astype(BF16)
            o_ = {u: _nn(pb_[u], vs[u[0]][u[2]]) for u in subs}
            for (b, j) in pairs:
                gate = p_ref[b, :, C_AG + BLOCK * j:C_AG + BLOCK * (j + 1)]
                br_ref[b, :, BLOCK * j:BLOCK * (j + 1)] = ((o_[(b, j, 0)] + o_[(b, j, 1)]) * _silu(gate)).astype(BF16)
        for b in ex:
            kv_s[b, 0:BLOCK, :] = kv_s[b, BLOCK:2 * BLOCK, :]

        cosv = cos_ref[...]
        sinv = sin_ref[...]
        tabs = [_decay_tables(h) for h in range(RET_HEADS)]
        units = [(b, h) for h in range(RET_HEADS) for b in ex]
        sl = lambda c0, h: slice(c0 + BLOCK * h, c0 + BLOCK * (h + 1))
        q_, k_, v_, sp_ = {}, {}, {}, {}
        for u in units:
            b, h = u
            q_[u] = _rot(p_ref[b, :, sl(C_RQ, h)], cosv, sinv).astype(BF16)
            k_[u] = (_rot(p_ref[b, :, sl(C_RK, h)], cosv, sinv) * RET_SCALE * valid).astype(BF16)
            v_[u] = p_ref[b, :, sl(C_RV, h)]
            sp_[u] = state_s[b, h]
            st_ref[b, 0, h] = sp_[u]
        qk_ = {u: _nt(q_[u], k_[u]) for u in units}
        qs_ = {u: _nn(q_[u], sp_[u].astype(BF16)) for u in units}
        kv_ = {u: _tn(k_[u], (v_[u] * tabs[u[1]][1]).astype(BF16)) for u in units}
        a_ = {u: (qk_[u] * tabs[u[1]][0]).astype(BF16) for u in units}
        av_ = {u: _nn(a_[u], v_[u].astype(BF16)) for u in units}
        for u in units:
            b, h = u
            o = av_[u] + tabs[h][2] * qs_[u]
            mu = jnp.mean(o, axis=-1, keepdims=True)
            var = jnp.mean(jnp.square(o - mu), axis=-1, keepdims=True)
            oh = (o - mu) * lax.rsqrt(var + GN_EPS)
            gate = p_ref[b, :, sl(C_RG, h)]
            br_ref[b, :, BRANCH_WIDTH + BLOCK * h:BRANCH_WIDTH + BLOCK * (h + 1)] = (oh * _silu(gate)).astype(BF16)
            state_s[b, h] = tabs[h][3] * sp_[u] + kv_[u]

        for b in ex:
            u = p_ref[b, :, C_CC:C_CC + BRANCH_WIDTH] * p_ref[b, :, C_CX:C_CX + BRANCH_WIDTH] * valid
            u_prev = u_s[b]
            y = (cw_ref[0:1, :] * _shift_down(u, u_prev, 2) + cw_ref[1:2, :] * _shift_down(u, u_prev, 1)
                 + cw_ref[2:3, :] * u)
            yc = p_ref[b, :, C_CB:C_CB + BRANCH_WIDTH] * y * _silu(p_ref[b, :, C_CG:C_CG + BRANCH_WIDTH])
            br_ref[b, :, 2 * BRANCH_WIDTH:3 * BRANCH_WIDTH] = yc.astype(BF16)
            u_s[b] = u

    lp = nc * BLOCK
    smem = pl.BlockSpec(memory_space=pltpu.SMEM)
    br, states = pl.pallas_call(
        body, name="mixers_fwd",
        grid=(nc,),
        in_specs=[pl.BlockSpec((nb, BLOCK, ABC_WIDTH), lambda n: (0, n, 0)),
                  pl.BlockSpec((BLOCK, BLOCK), lambda n: (n, 0)),
                  pl.BlockSpec((BLOCK, BLOCK), lambda n: (n, 0)),
                  pl.BlockSpec((BLOCK, 2 * BLOCK), lambda n: (0, 0)),
                  smem, smem,
                  pl.BlockSpec((None, 3, BRANCH_WIDTH), lambda n: (layer, 0, 0))],
        out_specs=[pl.BlockSpec((nb, BLOCK, N_BRANCH * BRANCH_WIDTH), lambda n: (0, n, 0)),
                   pl.BlockSpec((nb, 1, RET_HEADS, BLOCK, BLOCK), lambda n: (0, n, 0, 0, 0))],
        out_shape=[jax.ShapeDtypeStruct((nb, lp, N_BRANCH * BRANCH_WIDTH), BF16),
                   jax.ShapeDtypeStruct((nb, nc, RET_HEADS, BLOCK, BLOCK), F32)],
        scratch_shapes=[pltpu.VMEM((ATT_HEADS, BLOCK, 2 * BLOCK), F32),
                        pltpu.VMEM((nb, 2 * BLOCK, 2 * BLOCK), F32),
                        pltpu.VMEM((nb, RET_HEADS, BLOCK, BLOCK), F32),
                        pltpu.VMEM((nb, BLOCK, BRANCH_WIDTH), F32)],
        compiler_params=_cparams("arbitrary"),
    )(proj.reshape(nb, lp, ABC_WIDTH), cosf, sinf, bkt, rel_bias, sinks, conv_w)
    return br.reshape(nb * lp, N_BRANCH * BRANCH_WIDTH), states


def mixers_bwd(proj, d_br, states, cosf, sinf, bkt, rel_bias, sinks, conv_w, layer, nb, nc):
    def body(p_ref, kvp_ref, cp_ref, dbr_ref, st_ref, cos_ref, sin_ref, bkt_ref, rb_ref, sk_ref, cw_ref,
             dp_ref, drb_ref, dsk_ref, dcw_ref,
             bias_s, dbias_s, dkv_s, g_s, dy_s):
        p_ref, kvp_ref, cp_ref, dbr_ref = [_Widened(r) for r in (p_ref, kvp_ref, cp_ref, dbr_ref)]
        step = pl.program_id(0)
        n = nc - 1 - step
        ex = range(nb)

        @pl.when(step == 0)
        def _():
            _build_bias(bkt_ref, rb_ref, bias_s)
            dbias_s[...] = jnp.zeros_like(dbias_s)
            dsk_ref[...] = jnp.zeros_like(dsk_ref)
            dcw_ref[...] = jnp.zeros_like(dcw_ref)
            drb_ref[...] = jnp.zeros_like(drb_ref)
            dkv_s[...] = jnp.zeros_like(dkv_s)
            g_s[...] = jnp.zeros_like(g_s)
            dy_s[...] = jnp.zeros_like(dy_s)

        valid = _valid_col(n)
        mask = _band_mask(n)
        has_prev = (n > 0).astype(F32)

        k_all, v_all = [], []
        for b in ex:
            kv_prev = kvp_ref[b] * has_prev
            kv_cur = p_ref[b, :, C_AK:C_AK + 2 * BLOCK]
            k_all.append(jnp.concatenate([kv_prev[:, 0:BLOCK], kv_cur[:, 0:BLOCK]], axis=0))
            v_all.append(jnp.concatenate([kv_prev[:, BLOCK:], kv_cur[:, BLOCK:]], axis=0))
        zero2 = jnp.zeros((2 * BLOCK, BLOCK), F32)
        dk_tot = [zero2 for _ in ex]
        dv_tot = [zero2 for _ in ex]
        for kh in range(2):
            ks = [[t.astype(BF16) for t in _split_heads(k_all[b], kh)] for b in ex]
            vs = [[t.astype(BF16) for t in _split_heads(v_all[b], kh)] for b in ex]
            pairs = [(b, 2 * kh + jj) for jj in range(2) for b in ex]
            subs = [(b, j, x) for (b, j) in pairs for x in range(2)]
            qb_, gate_, dya_, do2_ = {}, {}, {}, {}
            for w in pairs:
                b, j = w
                qb_[w] = (p_ref[b, :, C_AQ + BLOCK * j:C_AQ + BLOCK * (j + 1)] * ATT_SCALE).astype(BF16)
                gate_[w] = p_ref[b, :, C_AG + BLOCK * j:C_AG + BLOCK * (j + 1)]
                dya_[w] = dbr_ref[b, :, BLOCK * j:BLOCK * (j + 1)]
                do2_[w] = (dya_[w] * _silu(gate_[w])).astype(BF16)
            qk_ = {(b, j, x): _nt(qb_[(b, j)], ks[b][x]) for (b, j, x) in subs}
            dpm_ = {(b, j, x): _nt(do2_[(b, j)], vs[b][x]) for (b, j, x) in subs}
            pb_, dsb_ = {}, {}
            for u in subs:
                b, j, x = u
                h = 2 * j + x
                p, p_sink = _softmax_of(qk_[u], bias_s[h], mask, sk_ref[layer, h])
                pb_[u] = p.astype(BF16)
                delta = jnp.sum(p * dpm_[u], axis=-1, keepdims=True)
                ds = p * (dpm_[u] - delta)
                dbias_s[h] += ds
                dsk_ref[h:h + 1, :] += jnp.broadcast_to(
                    jnp.sum(-p_sink * delta, axis=0, keepdims=True), (1, BLOCK))
                dsb_[u] = ds.astype(BF16)
            o_ = {u: _nn(pb_[u], vs[u[0]][u[2]]) for u in subs}
            dq_ = {u: _nn(dsb_[u], ks[u[0]][u[2]]) for u in subs}
            dkm_ = {u: _tn(dsb_[u], qb_[(u[0], u[1])]) for u in subs}
            dvm_ = {u: _tn(pb_[u], do2_[(u[0], u[1])]) for u in subs}
            for w in pairs:
                b, j = w
                o2 = o_[(b, j, 0)] + o_[(b, j, 1)]
                dq2 = (dq_[(b, j, 0)] + dq_[(b, j, 1)]) * ATT_SCALE
                dp_ref[b, :, C_AQ + BLOCK * j:C_AQ + BLOCK * (j + 1)] = dq2.astype(BF16)
                dp_ref[b, :, C_AG + BLOCK * j:C_AG + BLOCK * (j + 1)] = (
                    dya_[w] * o2 * _dsilu(gate_[w])).astype(BF16)
            for b in ex:
                j0, j1 = 2 * kh, 2 * kh + 1
                dk_tot[b] = dk_tot[b] + _merge_heads(dkm_[(b, j0, 0)] + dkm_[(b, j1, 0)],
                                                     dkm_[(b, j0, 1)] + dkm_[(b, j1, 1)], kh)
                dv_tot[b] = dv_tot[b] + _merge_heads(dvm_[(b, j0, 0)] + dvm_[(b, j1, 0)],
                                                     dvm_[(b, j0, 1)] + dvm_[(b, j1, 1)], kh)
        for b in ex:
            dp_ref[b, :, C_AK:C_AK + BLOCK] = (dk_tot[b][BLOCK:, :] + dkv_s[b, :, 0:BLOCK]).astype(BF16)
            dp_ref[b, :, C_AV:C_AV + BLOCK] = (dv_tot[b][BLOCK:, :] + dkv_s[b, :, BLOCK:]).astype(BF16)
            dkv_s[b, :, 0:BLOCK] = dk_tot[b][0:BLOCK, :]
            dkv_s[b, :, BLOCK:] = dv_tot[b][0:BLOCK, :]

        cosv = cos_ref[...]
        sinv = sin_ref[...]
        tabs = [_decay_tables(h) for h in range(RET_HEADS)]
        units = [(b, h) for h in range(RET_HEADS) for b in ex]
        sl = lambda c0, h: slice(c0 + BLOCK * h, c0 + BLOCK * (h + 1))
        q_, k_, v_, vb_, sp_ = {}, {}, {}, {}, {}
        for u in units:
            b, h = u
            q_[u] = _rot(p_ref[b, :, sl(C_RQ, h)], cosv, sinv).astype(BF16)
            k_[u] = (_rot(p_ref[b, :, sl(C_RK, h)], cosv, sinv) * RET_SCALE * valid).astype(BF16)
            v_[u] = p_ref[b, :, sl(C_RV, h)]
            vb_[u] = v_[u].astype(BF16)
            sp_[u] = st_ref[b, 0, h].astype(BF16)
        qk_ = {u: _nt(q_[u], k_[u]) for u in units}
        qs_ = {u: _nn(q_[u], sp_[u]) for u in units}
        a_ = {u: (qk_[u] * tabs[u[1]][0]).astype(BF16) for u in units}
        av_ = {u: _nn(a_[u], vb_[u]) for u in units}
        dob_, dxo_ = {}, {}
        for u in units:
            b, h = u
            xi = tabs[h][2]
            o = av_[u] + xi * qs_[u]
            mu = jnp.mean(o, axis=-1, keepdims=True)
            var = jnp.mean(jnp.square(o - mu), axis=-1, keepdims=True)
            rstd = lax.rsqrt(var + GN_EPS)
            oh = (o - mu) * rstd
            gate = p_ref[b, :, sl(C_RG, h)]
            d_yr = dbr_ref[b, :, BRANCH_WIDTH + BLOCK * h:BRANCH_WIDTH + BLOCK * (h + 1)]
            dp_ref[b, :, sl(C_RG, h)] = (d_yr * oh * _dsilu(gate)).astype(BF16)
            doh = d_yr * _silu(gate)
            do = rstd * (doh - jnp.mean(doh, axis=-1, keepdims=True)
                         - oh * jnp.mean(doh * oh, axis=-1, keepdims=True))
            dob_[u] = do.astype(BF16)
            dxo_[u] = (do * xi).astype(BF16)
        dov_ = {u: _nt(dob_[u], vb_[u]) for u in units}
        dv1_ = {u: _tn(a_[u], dob_[u]) for u in units}
        dq1_ = {u: _nt(dxo_[u], sp_[u]) for u in units}
        gq_ = {u: _tn(q_[u], dxo_[u]) for u in units}
        da_, gb_, zv_ = {}, {}, {}
        for u in units:
            b, h = u
            da_[u] = (dov_[u] * tabs[h][0]).astype(BF16)
            g_next = g_s[b, h]
            gb_[u] = g_next.astype(BF16)
            zv_[u] = (v_[u] * tabs[h][1]).astype(BF16)
            g_s[b, h] = tabs[h][3] * g_next + gq_[u]
        dq2_ = {u: _nn(da_[u], k_[u]) for u in units}
        dk1_ = {u: _tn(da_[u], q_[u]) for u in units}
        dk2_ = {u: _nt(zv_[u], gb_[u]) for u in units}
        dv2_ = {u: _nn(k_[u], gb_[u]) for u in units}
        for u in units:
            b, h = u
            dp_ref[b, :, sl(C_RQ, h)] = _rot_t(dq2_[u] + dq1_[u], cosv, sinv).astype(BF16)
            dp_ref[b, :, sl(C_RK, h)] = _rot_t((dk1_[u] + dk2_[u]) * (RET_SCALE * valid), cosv, sinv).astype(BF16)
            dp_ref[b, :, sl(C_RV, h)] = (dv1_[u] + tabs[h][1] * dv2_[u]).astype(BF16)

        w0, w1, w2 = cw_ref[0:1, :], cw_ref[1:2, :], cw_ref[2:3, :]
        for b in ex:
            cb = p_ref[b, :, C_CB:C_CB + BRANCH_WIDTH]
            cc = p_ref[b, :, C_CC:C_CC + BRANCH_WIDTH]
            cx = p_ref[b, :, C_CX:C_CX + BRANCH_WIDTH]
            cg = p_ref[b, :, C_CG:C_CG + BRANCH_WIDTH]
            u = cc * cx * valid
            u_prev = (cp_ref[b, :, 0:BRANCH_WIDTH] * cp_ref[b, :, BRANCH_WIDTH:2 * BRANCH_WIDTH]
                      * (_valid_col(n - 1) * has_prev))
            u1 = _shift_down(u, u_prev, 1)
            u2 = _shift_down(u, u_prev, 2)
            y = w0 * u2 + w1 * u1 + w2 * u
            d_yc = dbr_ref[b, :, 2 * BRANCH_WIDTH:3 * BRANCH_WIDTH]
            sg = _silu(cg)
            dp_ref[b, :, C_CB:C_CB + BRANCH_WIDTH] = (d_yc * y * sg).astype(BF16)
            dp_ref[b, :, C_CG:C_CG + BRANCH_WIDTH] = (d_yc * cb * y * _dsilu(cg)).astype(BF16)
            dy = d_yc * cb * sg
            dy_next = dy_s[b]
            du = (w2 * dy + w1 * _shift_up(dy, dy_next, 1) + w0 * _shift_up(dy, dy_next, 2)) * valid
            dp_ref[b, :, C_CC:C_CC + BRANCH_WIDTH] = (du * cx).astype(BF16)
            dp_ref[b, :, C_CX:C_CX + BRANCH_WIDTH] = (du * cc).astype(BF16)
            dcw_ref[0:1, :] += jnp.sum(dy * u2, axis=0, keepdims=True)
            dcw_ref[1:2, :] += jnp.sum(dy * u1, axis=0, keepdims=True)
            dcw_ref[2:3, :] += jnp.sum(dy * u, axis=0, keepdims=True)
            dy_s[b] = dy

        @pl.when(step == nc - 1)
        def _():
            bkt = bkt_ref[...]
            row = lax.broadcasted_iota(jnp.int32, (N_BUCKETS, BLOCK), 0)
            lane = lax.broadcasted_iota(jnp.int32, (N_BUCKETS, BLOCK), 1)

            def one_bucket(bk, acc):
                sel = bkt == bk
                for h in range(ATT_HEADS):
                    t = jnp.where(sel, dbias_s[h], 0.0)
                    s = jnp.sum(jnp.sum(t, axis=1, keepdims=True), axis=0, keepdims=True)
                    acc = acc + jnp.where((row == bk) & (lane == h), jnp.broadcast_to(s, acc.shape), 0.0)
                return acc

            drb_ref[...] = lax.fori_loop(0, N_BUCKETS, one_bucket, jnp.zeros((N_BUCKETS, BLOCK), F32))

    lp = nc * BLOCK
    smem = pl.BlockSpec(memory_space=pltpu.SMEM)
    blk = lambda s: nc - 1 - s
    prev = lambda s: jnp.maximum(nc - 2 - s, 0)
    proj3 = proj.reshape(nb, lp, ABC_WIDTH)
    res = pl.pallas_call(
        body, name="mixers_bwd",
        grid=(nc,),
        in_specs=[pl.BlockSpec((nb, BLOCK, ABC_WIDTH), lambda s: (0, blk(s), 0)),
                  pl.BlockSpec((nb, BLOCK, 2 * BLOCK), lambda s: (0, prev(s), C_AK // (2 * BLOCK))),
                  pl.BlockSpec((nb, BLOCK, 1280), lambda s: (0, prev(s), C_CC // 1280)),
                  pl.BlockSpec((nb, BLOCK, N_BRANCH * BRANCH_WIDTH), lambda s: (0, blk(s), 0)),
                  pl.BlockSpec((nb, 1, RET_HEADS, BLOCK, BLOCK), lambda s: (0, blk(s), 0, 0, 0)),
                  pl.BlockSpec((BLOCK, BLOCK), lambda s: (blk(s), 0)),
                  pl.BlockSpec((BLOCK, BLOCK), lambda s: (blk(s), 0)),
                  pl.BlockSpec((BLOCK, 2 * BLOCK), lambda s: (0, 0)),
                  smem, smem,
                  pl.BlockSpec((None, 3, BRANCH_WIDTH), lambda s: (layer, 0, 0))],
        out_specs=[pl.BlockSpec((nb, BLOCK, ABC_WIDTH), lambda s: (0, blk(s), 0)),
                   pl.BlockSpec((N_BUCKETS, BLOCK), lambda s: (0, 0)),
                   pl.BlockSpec((ATT_HEADS, BLOCK), lambda s: (0, 0)),
                   pl.BlockSpec((8, BRANCH_WIDTH), lambda s: (0, 0))],
        out_shape=[jax.ShapeDtypeStruct((nb, lp, ABC_WIDTH), BF16),
                   jax.ShapeDtypeStruct((N_BUCKETS, BLOCK), F32),
                   jax.ShapeDtypeStruct((ATT_HEADS, BLOCK), F32),
                   jax.ShapeDtypeStruct((8, BRANCH_WIDTH), F32)],
        scratch_shapes=[pltpu.VMEM((ATT_HEADS, BLOCK, 2 * BLOCK), F32),
                        pltpu.VMEM((ATT_HEADS, BLOCK, 2 * BLOCK), F32),
                        pltpu.VMEM((nb, BLOCK, 2 * BLOCK), F32),
                        pltpu.VMEM((nb, RET_HEADS, BLOCK, BLOCK), F32),
                        pltpu.VMEM((nb, BLOCK, BRANCH_WIDTH), F32)],
        compiler_params=_cparams("arbitrary"),
    )(proj3, proj3, proj3, d_br.reshape(nb, lp, N_BRANCH * BRANCH_WIDTH), states, cosf, sinf, bkt, rel_bias, sinks,
      conv_w)
    return (res[0].reshape(nb * lp, ABC_WIDTH),) + tuple(res[1:])


MERGE_TILE = 256
MERGE_FWD_TILE = 544


def _merge_forward(br_ref, m_ref, wb_ref, wo_ref):
    bo, gates = [], []
    mixed_pre = None
    for g in range(N_BRANCH):
        br_g = br_ref[:, BRANCH_WIDTH * g:BRANCH_WIDTH * (g + 1)]
        bo_g = jnp.concatenate([_nn(br_g, wb_ref[p, g]) for p in range(N_CHIPS)], axis=1)
        gate_g = _sigmoid(m_ref[:, D_MODEL * g:D_MODEL * (g + 1)].astype(F32))
        bo.append(bo_g)
        gates.append(gate_g)
        mixed_pre = gate_g * bo_g if mixed_pre is None else mixed_pre + gate_g * bo_g
    mixed = _nn(mixed_pre.astype(BF16), wo_ref[...])
    r = lax.rsqrt(jnp.mean(mixed * mixed, axis=-1, keepdims=True) + RMS_EPS)
    return bo, gates, mixed_pre, mixed, r


def merge_fwd(x2d, br, pm, wb, wo, g_post, layer, after):
    t = x2d.shape[0]
    tm = MERGE_FWD_TILE if t % MERGE_FWD_TILE == 0 else BLOCK

    def body(x_ref, br_ref, m_ref, wb_ref, wo_ref, g_ref, after_ref, o_ref):
        _, _, _, mixed, r = _merge_forward(br_ref, m_ref, wb_ref, wo_ref)
        o_ref[...] = x_ref[...] + mixed * r * g_ref[...]

    return pl.pallas_call(
        body, name="merge_fwd",
        grid=(t // tm,),
        in_specs=[pl.BlockSpec((tm, D_MODEL), lambda i: (i, 0)),
                  pl.BlockSpec((tm, N_BRANCH * BRANCH_WIDTH), lambda i: (i, 0)),
                  pl.BlockSpec((tm, MERGE_WIDTH), lambda i: (i, 0)),
                  pl.BlockSpec((N_CHIPS, N_BRANCH, BRANCH_WIDTH, SHARD_D), lambda i: (0, 0, 0, 0)),
                  pl.BlockSpec((D_MODEL, D_MODEL), lambda i: (0, 0)),
                  pl.BlockSpec((None, 1, D_MODEL), lambda i: (layer, 0, 0)),
                  ANY],
        out_specs=pl.BlockSpec((tm, D_MODEL), lambda i: (i, 0)),
        out_shape=jax.ShapeDtypeStruct((t, D_MODEL), F32),
        compiler_params=_cparams("parallel"),
    )(x2d, br, pm, wb, wo, g_post, after)


def merge_fwd_loss(x2d, br, pm, wb, wo, g_post, layer, target, lp):
    t = x2d.shape[0]
    tm = MERGE_FWD_TILE if t % MERGE_FWD_TILE == 0 else BLOCK

    def body(x_ref, br_ref, m_ref, wb_ref, wo_ref, g_ref, t_ref, l_ref, d_ref):
        i = pl.program_id(0)

        @pl.when(i == 0)
        def _():
            l_ref[...] = jnp.zeros_like(l_ref)

        _, _, _, mixed, r = _merge_forward(br_ref, m_ref, wb_ref, wo_ref)
        y = x_ref[...] + mixed * r * g_ref[...]
        row = i * tm + lax.broadcasted_iota(jnp.int32, (tm, 1), 0)
        e = jnp.where(row % lp >= BLOCK, y - t_ref[...], 0.0)
        d_ref[...] = e * (1.0 / D_MODEL)
        s = jnp.sum(jnp.sum(e * e, axis=1, keepdims=True), axis=0, keepdims=True)
        l_ref[...] += jnp.broadcast_to(s * (0.5 / D_MODEL), l_ref.shape)

    return pl.pallas_call(
        body, name="merge_fwd_loss",
        grid=(t // tm,),
        in_specs=[pl.BlockSpec((tm, D_MODEL), lambda i: (i, 0)),
                  pl.BlockSpec((tm, N_BRANCH * BRANCH_WIDTH), lambda i: (i, 0)),
                  pl.BlockSpec((tm, MERGE_WIDTH), lambda i: (i, 0)),
                  pl.BlockSpec((N_CHIPS, N_BRANCH, BRANCH_WIDTH, SHARD_D), lambda i: (0, 0, 0, 0)),
                  pl.BlockSpec((D_MODEL, D_MODEL), lambda i: (0, 0)),
                  pl.BlockSpec((None, 1, D_MODEL), lambda i: (layer, 0, 0)),
                  pl.BlockSpec((tm, D_MODEL), lambda i: (i, 0))],
        out_specs=[pl.BlockSpec((8, BLOCK), lambda i: (0, 0)),
                   pl.BlockSpec((tm, D_MODEL), lambda i: (i, 0))],
        out_shape=[jax.ShapeDtypeStruct((8, BLOCK), F32),
                   jax.ShapeDtypeStruct((t, D_MODEL), F32)],
        compiler_params=_cparams("arbitrary"),
    )(x2d, br, pm, wb, wo, g_post, target)


def merge_bwd(d_out, br, pm, wb, wo, g_post, layer, after):
    t = d_out.shape[0]
    tm = MERGE_TILE if t % MERGE_TILE == 0 else BLOCK

    def body(do_ref, br_ref, m_ref, wb_ref, wo_ref, g_ref, after_ref, dbr_ref, dm_ref, dg_ref, dwb_ref, dwo_ref):

        @pl.when(pl.program_id(0) == 0)
        def _():
            dwb_ref[...] = jnp.zeros_like(dwb_ref)
            dwo_ref[...] = jnp.zeros_like(dwo_ref)
            dg_ref[...] = jnp.zeros_like(dg_ref)

        bo, gates, mixed_pre, mixed, r = _merge_forward(br_ref, m_ref, wb_ref, wo_ref)
        d_o = do_ref[...]
        nh = mixed * r
        dg_ref[0:1, :] += jnp.sum(d_o * nh, axis=0, keepdims=True)
        dn = d_o * g_ref[...]
        d_mixed = (r * (dn - nh * jnp.mean(dn * nh, axis=-1, keepdims=True))).astype(BF16)
        dwo_ref[...] += _tn(mixed_pre.astype(BF16), d_mixed)
        d_pre = _nt(d_mixed, wo_ref[...])
        for g in range(N_BRANCH):
            br_g = br_ref[:, BRANCH_WIDTH * g:BRANCH_WIDTH * (g + 1)]
            d_bo = (d_pre * gates[g]).astype(BF16)
            dm_ref[:, D_MODEL * g:D_MODEL * (g + 1)] = (
                d_pre * bo[g] * gates[g] * (1.0 - gates[g])).astype(BF16)
            d_br_g = None
            for p in range(N_CHIPS):
                d_bo_p = d_bo[:, SHARD_D * p:SHARD_D * (p + 1)]
                part = _nt(d_bo_p, wb_ref[p, g])
                d_br_g = part if d_br_g is None else d_br_g + part
                dwb_ref[p, g] += _tn(br_g, d_bo_p)
            dbr_ref[:, BRANCH_WIDTH * g:BRANCH_WIDTH * (g + 1)] = d_br_g.astype(BF16)

    return pl.pallas_call(
        body, name="merge_bwd",
        grid=(t // tm,),
        in_specs=[pl.BlockSpec((tm, D_MODEL), lambda i: (i, 0)),
                  pl.BlockSpec((tm, N_BRANCH * BRANCH_WIDTH), lambda i: (i, 0)),
                  pl.BlockSpec((tm, MERGE_WIDTH), lambda i: (i, 0)),
                  pl.BlockSpec((N_CHIPS, N_BRANCH, BRANCH_WIDTH, SHARD_D), lambda i: (0, 0, 0, 0)),
                  pl.BlockSpec((D_MODEL, D_MODEL), lambda i: (0, 0)),
                  pl.BlockSpec((None, 1, D_MODEL), lambda i: (layer, 0, 0)),
                  ANY],
        out_specs=[pl.BlockSpec((tm, N_BRANCH * BRANCH_WIDTH), lambda i: (i, 0)),
                   pl.BlockSpec((tm, MERGE_WIDTH), lambda i: (i, 0)),
                   pl.BlockSpec((8, D_MODEL), lambda i: (0, 0)),
                   pl.BlockSpec((N_CHIPS, N_BRANCH, BRANCH_WIDTH, SHARD_D), lambda i: (0, 0, 0, 0)),
                   pl.BlockSpec((D_MODEL, D_MODEL), lambda i: (0, 0))],
        out_shape=[jax.ShapeDtypeStruct((t, N_BRANCH * BRANCH_WIDTH), BF16),
                   jax.ShapeDtypeStruct((t, MERGE_WIDTH), BF16),
                   jax.ShapeDtypeStruct((8, D_MODEL), F32),
                   jax.ShapeDtypeStruct((N_CHIPS, N_BRANCH, BRANCH_WIDTH, SHARD_D), F32),
                   jax.ShapeDtypeStruct((D_MODEL, D_MODEL), F32)],
        compiler_params=_cparams("arbitrary"),
    )(d_out, br, pm, wb, wo, g_post, after)


N_ABC_TILES = ABC_WIDTH // COL_TILE
N_M_TILES = MERGE_WIDTH // COL_TILE


def proj_dgrad(d_abc, d_m, w, x2d, g, layer, d_out, after):
    t = x2d.shape[0]
    tm = ROW_TILE if t % ROW_TILE == 0 else BLOCK
    nk = N_ABC_TILES + N_M_TILES

    def body(da_ref, dm_ref, w_ref, x_ref, g_ref, do_ref, after_ref, dx_ref, dg_ref, acc):
        i = pl.program_id(0)
        k = pl.program_id(1)

        @pl.when((i == 0) & (k == 0))
        def _():
            dg_ref[...] = jnp.zeros_like(dg_ref)

        @pl.when(k == 0)
        def _():
            acc[...] = jnp.zeros_like(acc)

        @pl.when(k < N_ABC_TILES)
        def _():
            acc[...] += _nn(da_ref[...], w_ref[...])

        @pl.when(k >= N_ABC_TILES)
        def _():
            acc[...] += _nn(dm_ref[...], w_ref[...])

        @pl.when(k == nk - 1)
        def _():
            x = x_ref[...]
            r = lax.rsqrt(jnp.mean(x * x, axis=-1, keepdims=True) + RMS_EPS)
            nh = x * r
            dh = acc[...]
            dg_ref[0:1, :] += jnp.sum(dh * nh, axis=0, keepdims=True)
            dn = dh * g_ref[...]
            dx_ref[...] = do_ref[...] + r * (dn - nh * jnp.mean(dn * nh, axis=-1, keepdims=True))

    return pl.pallas_call(
        body, name="proj_dgrad",
        grid=(t // tm, nk),
        in_specs=[pl.BlockSpec((tm, COL_TILE), lambda i, k: (i, jnp.minimum(k, N_ABC_TILES - 1))),
                  pl.BlockSpec((tm, COL_TILE), lambda i, k: (i, jnp.maximum(k - N_ABC_TILES, 0))),
                  pl.BlockSpec((COL_TILE, D_MODEL), lambda i, k: (k, 0)),
                  pl.BlockSpec((tm, D_MODEL), lambda i, k: (i, 0)),
                  pl.BlockSpec((None, 1, D_MODEL), lambda i, k: (layer, 0, 0)),
                  pl.BlockSpec((tm, D_MODEL), lambda i, k: (i, 0)),
                  ANY],
        out_specs=[pl.BlockSpec((tm, D_MODEL), lambda i, k: (i, 0)),
                   pl.BlockSpec((8, D_MODEL), lambda i, k: (0, 0))],
        out_shape=[jax.ShapeDtypeStruct((t, D_MODEL), F32),
                   jax.ShapeDtypeStruct((8, D_MODEL), F32)],
        scratch_shapes=[pltpu.VMEM((tm, D_MODEL), F32)],
        compiler_params=_cparams("arbitrary", "arbitrary"),
    )(d_abc, d_m, w, x2d, g, d_out, after)


def proj_wgrad(hb, d_abc, d_m):
    t = hb.shape[0]
    nj = N_ABC_TILES + N_M_TILES

    def body(h_ref, da_ref, dm_ref, o_ref):
        j = pl.program_id(0)

        @pl.when(j < N_ABC_TILES)
        def _():
            o_ref[...] = _tn(da_ref[...], h_ref[...])

        @pl.when(j >= N_ABC_TILES)
        def _():
            o_ref[...] = _tn(dm_ref[...], h_ref[...])

    return pl.pallas_call(
        body, name="proj_wgrad",
        grid=(nj,),
        in_specs=[pl.BlockSpec((t, D_MODEL), lambda j: (0, 0)),
                  pl.BlockSpec((t, COL_TILE), lambda j: (0, jnp.minimum(j, N_ABC_TILES - 1))),
                  pl.BlockSpec((t, COL_TILE), lambda j: (0, jnp.maximum(j - N_ABC_TILES, 0)))],
        out_specs=pl.BlockSpec((COL_TILE, D_MODEL), lambda j: (j, 0)),
        out_shape=jax.ShapeDtypeStruct((PROJ_WIDTH, D_MODEL), F32),
        compiler_params=_cparams("arbitrary"),
    )(hb, d_abc, d_m)


def _adamw_math(w, g, m, v):
    m = ADAM_B1 * m + (1.0 - ADAM_B1) * g
    v = ADAM_B2 * v + (1.0 - ADAM_B2) * jnp.square(g)
    m_hat = m / (1.0 - ADAM_B1 ** ADAM_STEP)
    v_hat = v / (1.0 - ADAM_B2 ** ADAM_STEP)
    delta = -ADAM_LR * (m_hat / (jnp.sqrt(v_hat) + ADAM_EPS) + ADAM_WD * w)
    return delta, m, v


def adamw_layer(w, g, m, v, layer, acc, after):
    _, r, c = w.shape
    tr = _row_tile(r)

    def body(*refs):
        w_ref, g_ref, m_ref, v_ref = refs[:4]
        go_ref, d_ref, mo_ref, vo_ref = refs[-4:]
        g_val = g_ref[...]
        d, m_new, v_new = _adamw_math(w_ref[...], g_val, m_ref[...], v_ref[...])
        go_ref[...] = g_val
        d_ref[...] = d
        mo_ref[...] = m_new
        vo_ref[...] = v_new

    slab = pl.BlockSpec((None, tr, c), lambda i: (layer, i, 0))
    ins = [w, g, m, v, after]
    in_specs = [slab, pl.BlockSpec((tr, c), lambda i: (i, 0)), slab, slab, ANY]
    aliases = {}
    if acc is not None:
        ins += list(acc)
        in_specs += [ANY] * 4
        aliases = {5 + i: i for i in range(4)}
    return pl.pallas_call(
        body, name="adamw_layer",
        grid=(r // tr,),
        in_specs=in_specs, out_specs=[slab] * 4,
        out_shape=[jax.ShapeDtypeStruct(w.shape, F32)] * 4,
        input_output_aliases=aliases,
        compiler_params=_cparams("parallel"),
    )(*ins)


def adamw_small(params):
    k = len(params)

    def body(*refs):
        ins, outs = refs[:4 * k], refs[4 * k:]
        for i in range(k):
            d, m_new, v_new = _adamw_math(*[r[...] for r in ins[4 * i:4 * i + 4]])
            outs[3 * i][...] = d
            outs[3 * i + 1][...] = m_new
            outs[3 * i + 2][...] = v_new

    flat = [a for p in params for a in p]
    vm = pl.BlockSpec(memory_space=pltpu.VMEM)
    out_shape = [jax.ShapeDtypeStruct(p[0].shape, F32) for p in params for _ in range(3)]
    res = pl.pallas_call(
        body, name="adamw_small",
        in_specs=[vm] * len(flat), out_specs=[vm] * len(out_shape), out_shape=out_shape,
    )(*flat)
    return [tuple(res[3 * i:3 * i + 3]) for i in range(k)]


ANY = pl.BlockSpec(memory_space=pl.ANY)


def _place():
    return lax.axis_index("x"), lax.axis_index("y"), lax.axis_index("c")


HBM = pl.BlockSpec(memory_space=pltpu.HBM)
SEM = pl.BlockSpec(memory_space=pltpu.SEMAPHORE)
EFFECT = pltpu.SideEffectType.DATAFLOW_SIDE_EFFECTING


def _other_chips(x, y):
    return [(1 - x, y), (x, 1 - y), (1 - x, 1 - y)]


def _own_slot(shard, chip):
    buf = lax.empty((N_CHIPS,) + shard.shape, shard.dtype)
    return lax.dynamic_update_slice(buf, shard[None], (chip, 0, 0, 0))


def _hbm(a):
    return pltpu.with_memory_space_constraint(a, pltpu.HBM)


def gather_start(bufs, after):
    n = len(bufs)

    def body(*refs):
        g_refs = refs[:n]
        send_sems, recv_sems = refs[n + 1], refs[n + 2]
        token = refs[-1]
        x, y, c = _place()
        me_p = 2 * x + y
        for t in range(n):
            for k, (qx, qy) in enumerate(_other_chips(x, y)):
                slab = g_refs[t].at[me_p, c]
                pltpu.make_async_remote_copy(src_ref=slab, dst_ref=slab, send_sem=send_sems.at[3 * t + k],
                                             recv_sem=recv_sems.at[3 * t + k], device_id=(qx, qy, c),
                                             device_id_type=MESH).start()
        token[...] = jnp.zeros_like(token)

    res = pl.pallas_call(
        body, name="gather_start",
        in_specs=[HBM] * n + [ANY],
        out_specs=[SEM, SEM] + [HBM] * n + [pl.BlockSpec(memory_space=pltpu.VMEM)],
        out_shape=[pltpu.SemaphoreType.DMA((3 * n,)), pltpu.SemaphoreType.DMA((3 * n,))]
        + [pltpu.HBM(b.shape, b.dtype) for b in bufs] + [jax.ShapeDtypeStruct((8, LANES), F32)],
        input_output_aliases={t: 2 + t for t in range(n)},
        compiler_params=pltpu.CompilerParams(has_side_effects=EFFECT),
    )(*[_hbm(b) for b in bufs], after)
    return res[0], res[1], list(res[2:2 + n]), res[-1]


def gather_wait(bufs, send_sems, recv_sems, after, first=0):
    n = len(bufs)

    def body(*refs):
        g_refs = refs[:n]
        send_sems, recv_sems = refs[n], refs[n + 1]
        x, y, c = _place()
        me_p = 2 * x + y
        for t in range(n):
            for k, (qx, qy) in enumerate(_other_chips(x, y)):
                s = 3 * (first + t) + k
                cp = pltpu.make_async_remote_copy(src_ref=g_refs[t].at[me_p, c], dst_ref=g_refs[t].at[2 * qx + qy, c],
                                                  send_sem=send_sems.at[s], recv_sem=recv_sems.at[s],
                                                  device_id=(qx, qy, c), device_id_type=MESH)
                cp.wait_send()
                cp.wait_recv()

    return pl.pallas_call(
        body, name="gather_wait",
        in_specs=[HBM] * n + [SEM, SEM, ANY],
        out_specs=[HBM] * n,
        out_shape=[pltpu.HBM(b.shape, b.dtype) for b in bufs],
        input_output_aliases={t: t for t in range(n)},
        compiler_params=pltpu.CompilerParams(has_side_effects=EFFECT),
    )(*bufs, send_sems, recv_sems, after)


def gather_forward(bufs):
    n = len(bufs)

    def body(*refs):
        g_refs = refs[n:2 * n]
        send_sems, recv_sems = refs[2 * n:]
        x, y, c = _place()
        sibling = (x, y, 1 - c)
        chips = _other_chips(x, y)
        passed = []
        for t in range(n):
            for k, (qx, qy) in enumerate(chips):
                slab = g_refs[t].at[2 * qx + qy, c]
                fwd = pltpu.make_async_remote_copy(src_ref=slab, dst_ref=slab, send_sem=send_sems.at[3 * t + k],
                                                   recv_sem=recv_sems.at[3 * t + k], device_id=sibling,
                                                   device_id_type=MESH)
                fwd.start()
                passed.append(fwd)
        for t in range(n):
            for k, (qx, qy) in enumerate(chips):
                slab = g_refs[t].at[2 * qx + qy, 1 - c]
                pltpu.make_async_remote_copy(src_ref=slab, dst_ref=slab, send_sem=send_sems.at[3 * t + k],
                                             recv_sem=recv_sems.at[3 * t + k], device_id=sibling,
                                             device_id_type=MESH).wait_recv()
        for cp in passed:
            cp.wait_send()

    return pl.pallas_call(
        body, name="gather_forward",
        in_specs=[ANY] * n, out_specs=[ANY] * n,
        out_shape=[jax.ShapeDtypeStruct(b.shape, b.dtype) for b in bufs],
        input_output_aliases={t: t for t in range(n)},
        scratch_shapes=[pltpu.SemaphoreType.DMA((3 * n,)), pltpu.SemaphoreType.DMA((3 * n,))],
    )(*bufs)


def forward_start(bufs):
    n = len(bufs)

    def body(*refs):
        g_refs = refs[:n]
        send_sems, recv_sems = refs[n], refs[n + 1]
        token = refs[-1]
        x, y, c = _place()
        for t in range(n):
            for k, (qx, qy) in enumerate(_other_chips(x, y)):
                slab = g_refs[t].at[2 * qx + qy, c]
                pltpu.make_async_remote_copy(src_ref=slab, dst_ref=slab, send_sem=send_sems.at[3 * t + k],
                                             recv_sem=recv_sems.at[3 * t + k], device_id=(x, y, 1 - c),
                                             device_id_type=MESH).start()
        token[...] = jnp.zeros_like(token)

    res = pl.pallas_call(
        body, name="forward_start",
        in_specs=[HBM] * n,
        out_specs=[SEM, SEM] + [HBM] * n + [pl.BlockSpec(memory_space=pltpu.VMEM)],
        out_shape=[pltpu.SemaphoreType.DMA((3 * n,)), pltpu.SemaphoreType.DMA((3 * n,))]
        + [pltpu.HBM(b.shape, b.dtype) for b in bufs] + [jax.ShapeDtypeStruct((8, LANES), F32)],
        input_output_aliases={t: 2 + t for t in range(n)},
        compiler_params=pltpu.CompilerParams(has_side_effects=EFFECT),
    )(*[_hbm(b) for b in bufs])
    return res[0], res[1], list(res[2:2 + n]), res[-1]


def forward_wait(bufs, send_sems, recv_sems, after):
    n = len(bufs)

    def body(*refs):
        g_refs = refs[:n]
        send_sems, recv_sems = refs[n], refs[n + 1]
        x, y, c = _place()
        for t in range(n):
            for k, (qx, qy) in enumerate(_other_chips(x, y)):
                cp = pltpu.make_async_remote_copy(src_ref=g_refs[t].at[2 * qx + qy, c],
                                                  dst_ref=g_refs[t].at[2 * qx + qy, 1 - c],
                                                  send_sem=send_sems.at[3 * t + k], recv_sem=recv_sems.at[3 * t + k],
                                                  device_id=(x, y, 1 - c), device_id_type=MESH)
                cp.wait_send()
                cp.wait_recv()

    return pl.pallas_call(
        body, name="forward_wait",
        in_specs=[HBM] * n + [SEM, SEM, ANY],
        out_specs=[HBM] * n,
        out_shape=[pltpu.HBM(b.shape, b.dtype) for b in bufs],
        input_output_aliases={t: t for t in range(n)},
        compiler_params=pltpu.CompilerParams(has_side_effects=EFFECT),
    )(*bufs, send_sems, recv_sems, after)


def small_start(pack, me, after):
    buf = lax.dynamic_update_slice(lax.empty((8,) + pack.shape, pack.dtype), pack[None], (me, 0, 0))

    def body(b_ref, after_ref, send_sems, recv_sems, thru, token):
        x, y, c = _place()
        slot = b_ref.at[4 * x + 2 * y + c]
        for k in range(1, 8):
            peer = (x ^ ((k >> 2) & 1), y ^ ((k >> 1) & 1), c ^ (k & 1))
            pltpu.make_async_remote_copy(src_ref=slot, dst_ref=slot, send_sem=send_sems.at[k - 1],
                                         recv_sem=recv_sems.at[k - 1], device_id=peer, device_id_type=MESH).start()
        token[...] = jnp.zeros_like(token)

    return pl.pallas_call(
        body, name="small_start",
        in_specs=[HBM, ANY],
        out_specs=[SEM, SEM, HBM, pl.BlockSpec(memory_space=pltpu.VMEM)],
        out_shape=[pltpu.SemaphoreType.DMA((7,)), pltpu.SemaphoreType.DMA((7,)), pltpu.HBM(buf.shape, buf.dtype),
                   jax.ShapeDtypeStruct((8, LANES), F32)],
        input_output_aliases={0: 2},
        compiler_params=pltpu.CompilerParams(has_side_effects=EFFECT),
    )(_hbm(buf), after)


def small_wait(buf, send_sems, recv_sems, after):
    def body(b_ref, send_sems, recv_sems, after_ref, thru):
        x, y, c = _place()
        mine = b_ref.at[4 * x + 2 * y + c]
        for k in range(1, 8):
            peer = (x ^ ((k >> 2) & 1), y ^ ((k >> 1) & 1), c ^ (k & 1))
            cp = pltpu.make_async_remote_copy(src_ref=mine, dst_ref=b_ref.at[4 * peer[0] + 2 * peer[1] + peer[2]],
                                              send_sem=send_sems.at[k - 1], recv_sem=recv_sems.at[k - 1],
                                              device_id=peer, device_id_type=MESH)
            cp.wait_send()
            cp.wait_recv()

    return pl.pallas_call(
        body, name="small_wait",
        in_specs=[HBM, SEM, SEM, ANY], out_specs=HBM,
        out_shape=pltpu.HBM(buf.shape, buf.dtype),
        input_output_aliases={0: 0},
        compiler_params=pltpu.CompilerParams(has_side_effects=EFFECT),
    )(buf, send_sems, recv_sems, after)


def swap_start(grads, after):
    n = len(grads)

    def body(*refs):
        g_refs, l_refs = refs[:n], refs[n:2 * n]
        send_sems, recv_sems = refs[2 * n + 1], refs[2 * n + 2]
        token = refs[-1]
        x, y, c = _place()
        for t in range(n):
            for p in range(N_CHIPS):
                pltpu.make_async_remote_copy(src_ref=g_refs[t].at[p, 1 - c], dst_ref=l_refs[t].at[p],
                                             send_sem=send_sems.at[N_CHIPS * t + p],
                                             recv_sem=recv_sems.at[N_CHIPS * t + p],
                                             device_id=(x, y, 1 - c), device_id_type=MESH).start()
        token[...] = jnp.zeros_like(token)

    lands = [lax.empty((N_CHIPS,) + g.shape[2:], g.dtype) for g in grads]
    res = pl.pallas_call(
        body, name="swap_start",
        in_specs=[HBM] * (2 * n) + [ANY],
        out_specs=[SEM, SEM] + [HBM] * (2 * n) + [pl.BlockSpec(memory_space=pltpu.VMEM)],
        out_shape=[pltpu.SemaphoreType.DMA((N_CHIPS * n,)), pltpu.SemaphoreType.DMA((N_CHIPS * n,))]
        + [pltpu.HBM(a.shape, a.dtype) for a in grads + lands] + [jax.ShapeDtypeStruct((8, LANES), F32)],
        input_output_aliases={t: 2 + t for t in range(2 * n)},
        compiler_params=pltpu.CompilerParams(has_side_effects=EFFECT),
    )(*[_hbm(a) for a in grads + lands], after)
    return res[0], res[1], list(res[2:2 + n]), list(res[2 + n:2 + 2 * n]), res[-1]


def swap_wait(grads, lands, send_sems, recv_sems, after):
    n = len(grads)

    def body(*refs):
        g_refs, l_refs = refs[:n], refs[n:2 * n]
        send_sems, recv_sems = refs[2 * n], refs[2 * n + 1]
        x, y, c = _place()
        for t in range(n):
            for p in range(N_CHIPS):
                cp = pltpu.make_async_remote_copy(src_ref=g_refs[t].at[p, 1 - c], dst_ref=l_refs[t].at[p],
                                                  send_sem=send_sems.at[N_CHIPS * t + p],
                                                  recv_sem=recv_sems.at[N_CHIPS * t + p],
                                                  device_id=(x, y, 1 - c), device_id_type=MESH)
                cp.wait_send()
                cp.wait_recv()

    res = pl.pallas_call(
        body, name="swap_wait",
        in_specs=[HBM] * (2 * n) + [SEM, SEM, ANY],
        out_specs=[HBM] * (2 * n),
        out_shape=[pltpu.HBM(a.shape, a.dtype) for a in grads + lands],
        input_output_aliases={t: t for t in range(2 * n)},
        compiler_params=pltpu.CompilerParams(has_side_effects=EFFECT),
    )(*grads, *lands, send_sems, recv_sems, after)
    return list(res[:n]), list(res[n:])


def _row_tile(r):
    return max(t for t in range(16, 513, 16) if r % t == 0)


def add_own_half(g, other, c_arr):
    _, _, r, cols = g.shape
    tr = _row_tile(r)

    def body(c_ref, a_ref, b_ref, o_ref):
        o_ref[...] = (a_ref[...] + b_ref[...]).astype(BF16)

    return pl.pallas_call(
        body, name="add_own_half",
        grid_spec=pltpu.PrefetchScalarGridSpec(
            num_scalar_prefetch=1, grid=(N_CHIPS, r // tr),
            in_specs=[pl.BlockSpec((None, None, tr, cols), lambda p, i, c_ref: (p, c_ref[0], i, 0)),
                      pl.BlockSpec((None, tr, cols), lambda p, i, c_ref: (p, i, 0))],
            out_specs=pl.BlockSpec((None, tr, cols), lambda p, i, c_ref: (p, i, 0))),
        out_shape=jax.ShapeDtypeStruct((N_CHIPS, r, cols), BF16),
        compiler_params=_cparams("parallel", "parallel"),
    )(c_arr, g, other)


def scatter_start(partials):
    n = len(partials)

    def body(*refs):
        s_refs, l_refs = refs[:n], refs[n:2 * n]
        send_sems, recv_sems = refs[2 * n], refs[2 * n + 1]
        token = refs[-1]
        x, y, c = _place()
        for t in range(n):
            for k, (qx, qy) in enumerate(_other_chips(x, y)):
                pltpu.make_async_remote_copy(src_ref=s_refs[t].at[2 * qx + qy], dst_ref=l_refs[t].at[k],
                                             send_sem=send_sems.at[3 * t + k], recv_sem=recv_sems.at[3 * t + k],
                                             device_id=(qx, qy, c), device_id_type=MESH).start()
        token[...] = jnp.zeros_like(token)

    lands = [lax.empty((3,) + s.shape[1:], s.dtype) for s in partials]
    res = pl.pallas_call(
        body, name="scatter_start",
        in_specs=[HBM] * (2 * n),
        out_specs=[SEM, SEM] + [HBM] * (2 * n) + [pl.BlockSpec(memory_space=pltpu.VMEM)],
        out_shape=[pltpu.SemaphoreType.DMA((3 * n,)), pltpu.SemaphoreType.DMA((3 * n,))]
        + [pltpu.HBM(a.shape, a.dtype) for a in partials + lands] + [jax.ShapeDtypeStruct((8, LANES), F32)],
        input_output_aliases={t: 2 + t for t in range(2 * n)},
        compiler_params=pltpu.CompilerParams(has_side_effects=EFFECT),
    )(*[_hbm(a) for a in partials + lands])
    return res[0], res[1], list(res[2:2 + n]), list(res[2 + n:2 + 2 * n]), res[-1]


def scatter_wait(partials, lands, send_sems, recv_sems, after):
    n = len(partials)

    def body(*refs):
        s_refs, l_refs = refs[:n], refs[n:2 * n]
        send_sems, recv_sems = refs[2 * n], refs[2 * n + 1]
        x, y, c = _place()
        for t in range(n):
            for k, (qx, qy) in enumerate(_other_chips(x, y)):
                cp = pltpu.make_async_remote_copy(src_ref=s_refs[t].at[2 * qx + qy], dst_ref=l_refs[t].at[k],
                                                  send_sem=send_sems.at[3 * t + k], recv_sem=recv_sems.at[3 * t + k],
                                                  device_id=(qx, qy, c), device_id_type=MESH)
                cp.wait_send()
                cp.wait_recv()

    res = pl.pallas_call(
        body, name="scatter_wait",
        in_specs=[HBM] * (2 * n) + [SEM, SEM, ANY],
        out_specs=[HBM] * (2 * n),
        out_shape=[pltpu.HBM(a.shape, a.dtype) for a in partials + lands],
        input_output_aliases={t: t for t in range(2 * n)},
        compiler_params=pltpu.CompilerParams(has_side_effects=EFFECT),
    )(*partials, *lands, send_sems, recv_sems, after)
    return list(res[:n]), list(res[n:])


def sum_chips(own, parts, where):
    _, r, cols = own.shape
    tr = _row_tile(r)

    def body(w_ref, a_ref, p_ref, o_ref):
        acc = a_ref[...].astype(F32)
        for k in range(3):
            acc = acc + p_ref[k].astype(F32)
        o_ref[...] = acc

    return pl.pallas_call(
        body, name="sum_chips",
        grid_spec=pltpu.PrefetchScalarGridSpec(
            num_scalar_prefetch=1, grid=(r // tr,),
            in_specs=[pl.BlockSpec((None, tr, cols), lambda i, w_ref: (w_ref[0], i, 0)),
                      pl.BlockSpec((3, tr, cols), lambda i, w_ref: (0, i, 0))],
            out_specs=pl.BlockSpec((None, tr, cols), lambda i, w_ref: (w_ref[1], i, 0))),
        out_shape=jax.ShapeDtypeStruct((DEPTH, r, cols), F32),
        compiler_params=_cparams("parallel"),
    )(where, own, parts)


def sibling_share_layer(bufs):
    n = len(bufs)

    def body(*refs):
        o_refs = refs[n:2 * n]
        send_sems, recv_sems = refs[2 * n:]
        x, y, c = _place()
        cps = []
        for t in range(n):
            cp = pltpu.make_async_remote_copy(src_ref=o_refs[t].at[c], dst_ref=o_refs[t].at[c], send_sem=send_sems.at[t],
                                              recv_sem=recv_sems.at[t], device_id=(x, y, 1 - c), device_id_type=MESH)
            cp.start()
            cps.append(cp)
        for t in range(n):
            slot = o_refs[t].at[1 - c]
            pltpu.make_async_remote_copy(src_ref=slot, dst_ref=slot, send_sem=send_sems.at[t], recv_sem=recv_sems.at[t],
                                         device_id=(x, y, 1 - c), device_id_type=MESH).wait_recv()
        for cp in cps:
            cp.wait_send()

    return pl.pallas_call(
        body, name="sibling_share_layer",
        in_specs=[ANY] * n, out_specs=[ANY] * n,
        out_shape=[jax.ShapeDtypeStruct(b.shape, b.dtype) for b in bufs],
        input_output_aliases={t: t for t in range(n)},
        scratch_shapes=[pltpu.SemaphoreType.DMA((n,)), pltpu.SemaphoreType.DMA((n,))],
    )(*bufs)


SP_META = 2 * (N_META * D_MODEL // LANES)
SP_NORM = DEPTH * D_MODEL // LANES
SP_RB = DEPTH * N_BUCKETS
SP_SINK = DEPTH * ATT_HEADS
SP_CONV = DEPTH * 3 * BRANCH_WIDTH // LANES
SP_LOSS = 8
SIDE_ROWS = 48
SP_ROWS = SP_META + 2 * SP_NORM + SP_RB + SP_SINK + SP_CONV + SP_LOSS


def sum_small(slots):
    half = SP_META // 2
    rb0 = SP_META + 2 * SP_NORM
    rest_rows = SP_ROWS - SP_META

    def body(s_ref, meta_ref, rest_ref):
        acc = s_ref[0]
        for d in range(1, 8):
            acc = acc + s_ref[d]
        meta_ref[...] = acc[0:half] + acc[half:SP_META]
        rest_ref[...] = acc[SP_META:]
        rest_ref[rb0 - SP_META:rb0 - SP_META + N_BUCKETS, :] = (
            acc[rb0:rb0 + N_BUCKETS] + acc[rb0 + N_BUCKETS:rb0 + 2 * N_BUCKETS])

    vm = pl.BlockSpec(memory_space=pltpu.VMEM)
    return pl.pallas_call(
        body, name="sum_small",
        in_specs=[vm], out_specs=[vm, vm],
        out_shape=[jax.ShapeDtypeStruct((half, LANES), F32), jax.ShapeDtypeStruct((rest_rows, LANES), F32)],
    )(slots)


def local_step(x, loss_target, meta_full, rel_bias, norm_pre, conv_w_full, attn_sinks, norm_post, weights_of, mid_fwd,
               grads_done, bwd_done):
    nb, seq, _ = x.shape
    nc = seq // BLOCK + 1
    lp = nc * BLOCK
    rows = nb * lp
    pad = jnp.zeros((nb, PAD_FRONT, D_MODEL), F32)
    meta = jnp.broadcast_to(meta_full[None], (nb, N_META, D_MODEL))
    h0 = jnp.concatenate([pad, meta, x], axis=1).reshape(rows, D_MODEL)
    target = jnp.pad(loss_target, ((0, 0), (BLOCK, 0), (0, 0))).reshape(rows, D_MODEL)
    cosf, sinf = _rot_tables(lp)
    bkt = jnp.asarray(_bucket_table())

    g_pre = norm_pre.reshape(DEPTH, 1, D_MODEL)
    g_post = norm_post.reshape(DEPTH, 1, D_MODEL)
    order = lambda token: bkt if token is None else token

    acts = []
    h = h0
    for l in range(DEPTH):
        w_in, token = weights_of(l, h)
        hb, p_abc = norm_matmul(h, g_pre, l, w_in, 0, N_ABC_TILES, order(token))
        p_m = matmul_cols(hb, w_in, N_ABC_TILES, N_M_TILES)
        br, states = mixers_fwd(p_abc, cosf, sinf, bkt, rel_bias, attn_sinks, conv_w_full, l, nb, nc)
        (w_br, w_out), token = mid_fwd(l, br)
        acts.append((h, hb, p_abc, p_m, br, states, w_in, w_br, w_out))
        if l < DEPTH - 1:
            h = merge_fwd(h, br, p_m, w_br, w_out, g_post, l, order(token))
        else:
            assert token is None
            loss_part, d_h = merge_fwd_loss(h, br, p_m, w_br, w_out, g_post, l, target, lp)

    small = [None] * DEPTH
    token = None
    for l in reversed(range(DEPTH)):
        h_in, hb, p_abc, p_m, br, states, w_in, w_br, w_out = acts[l]
        d_br, d_m, d_gpost, g_wbr, g_wout = merge_bwd(d_h, br, p_m, w_br, w_out, g_post, l, order(token))
        d_abc, d_rb, d_sk, d_cw = mixers_bwd(p_abc, d_br, states, cosf, sinf, bkt, rel_bias,
                                             attn_sinks, conv_w_full, l, nb, nc)
        g_win = proj_wgrad(hb, d_abc, d_m)
        token = grads_done(l, [g_win, g_wbr, g_wout])
        d_h, d_gpre = proj_dgrad(d_abc, d_m, w_in, h_in, g_pre, l, d_h, order(token))
        token = bwd_done(l, d_h)
        small[l] = (d_gpre[0], d_gpost[0], d_rb, d_sk, d_cw[0:3])

    d_h3 = d_h.reshape(nb, lp, D_MODEL)
    d_x = d_h3[:, BLOCK:]
    d_meta = d_h3[:, PAD_FRONT:BLOCK]
    sp = jnp.concatenate([
        d_meta.reshape(-1, LANES),
        jnp.stack([small[l][0] for l in range(DEPTH)]).reshape(-1, LANES),
        jnp.stack([small[l][1] for l in range(DEPTH)]).reshape(-1, LANES),
        jnp.concatenate([small[l][2] for l in range(DEPTH)], axis=0),
        jnp.concatenate([small[l][3] for l in range(DEPTH)], axis=0),
        jnp.stack([small[l][4] for l in range(DEPTH)]).reshape(-1, LANES),
        loss_part], axis=0)
    return d_x, sp


def kernel(x, meta_tokens, rel_bias, norm_pre, w_in, conv_w, attn_sinks, w_branch, w_out, norm_post, loss_target, m_meta_tokens, m_rel_bias, m_norm_pre, m_w_in, m_conv_w, m_attn_sinks, m_w_branch, m_w_out, m_norm_post, v_meta_tokens, v_rel_bias, v_norm_pre, v_w_in, v_conv_w, v_attn_sinks, v_w_branch, v_w_out, v_norm_post):
    assert x.shape[0] == 2 and SP_META == 2 * N_META * D_MODEL // LANES
    px, py, pc = _place()
    chip = 2 * px + py

    c_arr = jnp.reshape(pc, (1,)).astype(jnp.int32)
    where = jnp.stack([chip, pc]).astype(jnp.int32)
    tr_ = lambda a: jnp.swapaxes(a, 1, 2)
    w3 = [tr_(w_in), w_branch.reshape(DEPTH, N_BRANCH * BRANCH_WIDTH, SHARD_D), w_out]
    halves = lambda a: a.reshape(2, a.shape[0] // 2, a.shape[1])

    def as_weights(bufs):
        a_in, a_br, a_out = bufs
        return (a_in.reshape(PROJ_WIDTH, D_MODEL), a_br.reshape(N_CHIPS, N_BRANCH, BRANCH_WIDTH, SHARD_D),
                a_out.reshape(D_MODEL, D_MODEL))

    n_meta_rows = N_META * SHARD_D // LANES
    side = jnp.concatenate([meta_tokens.reshape(-1), conv_w.reshape(-1)]).reshape(-1, LANES)
    side = jnp.concatenate([side, jnp.zeros((SIDE_ROWS - side.shape[0], LANES), F32)], axis=0)
    slots = [[_own_slot(halves(w[l].astype(BF16)), chip) for w in w3] for l in range(DEPTH)]
    send0, recv0, flying0, _ = gather_start([_own_slot(halves(side), chip)] + slots[0], where)
    side_chips = gather_forward(gather_wait(flying0[:1], send0, recv0, where))[0].reshape(N_CHIPS, SIDE_ROWS, LANES)
    meta_full = jnp.moveaxis(side_chips[:, :n_meta_rows].reshape(N_CHIPS, N_META, SHARD_D), 0, 1).reshape(N_META, D_MODEL)
    conv_full = jnp.moveaxis(side_chips[:, n_meta_rows:n_meta_rows + 6].reshape(N_CHIPS, DEPTH, 3, LANES), 0, 2).reshape(DEPTH, 3, BRANCH_WIDTH)
    inbound = {}

    def weights_of(l, h):
        if l == 0:
            inbound[0] = gather_forward(gather_wait(flying0[1:2], send0, recv0, h, first=1))
            inbound[1] = gather_start(slots[1], inbound[0][0])
            return inbound[0][0].reshape(PROJ_WIDTH, D_MODEL), inbound[1][3]
        send, recv, thru = inbound[1]
        inbound[1] = as_weights(forward_wait(thru, send, recv, h))
        return inbound[1][0], None

    def mid_fwd(l, br):
        if l == 0:
            rest = gather_forward(gather_wait(flying0[2:], send0, recv0, br, first=2))
            send, recv, flying1, _ = inbound[1]
            send, recv, thru, started = forward_start(gather_wait(flying1, send, recv, rest[0]))
            inbound[1] = (send, recv, thru)
            return as_weights(inbound[0] + rest)[1:], started
        return inbound[1][1:], None

    reduced = [None] * DEPTH
    flying = {}

    def finish_reduce(l, after):
        partials, parts = scatter_wait(*flying[l], after)
        reduced[l] = sibling_share_layer([sum_chips(a, p, where) for a, p in zip(partials, parts)])

    def start_scatter(l, full, others):
        send, recv, thru, lands, started = scatter_start([add_own_half(g, o, c_arr) for g, o in zip(full, others)])
        flying[l] = (thru, lands, send, recv)
        return started

    m3 = [tr_(m_w_in), m_w_branch.reshape(w3[1].shape), m_w_out]
    v3 = [tr_(v_w_in), v_w_branch.reshape(w3[1].shape), v_w_out]
    big = [None] * 3

    def adamw_of(l, after):
        for t in range(3):
            big[t] = adamw_layer(w3[t], reduced[l][t].reshape(w3[t].shape[1:]), m3[t], v3[t], l, big[t], after)
            after = big[t][1]

    def grads_done(l, grads):
        full = [g.reshape(N_CHIPS, 2, g.size // (2 * N_CHIPS * g.shape[-1]), g.shape[-1]) for g in grads]
        if l == 0:
            finish_reduce(1, grads[0])
        send, recv, thru, lands, started = swap_start(full, where if l == 1 else reduced[1][0])
        if l == 1:
            flying["swap"] = (thru, lands, send, recv)
            return started
        adamw_of(1, started)
        return start_scatter(0, *swap_wait(thru, lands, send, recv, big[2][1]))

    def bwd_done(l, d_h):
        if l == 1:
            return start_scatter(1, *swap_wait(*flying["swap"], d_h))
        return None

    d_x, sp = local_step(x, loss_target, meta_full, rel_bias, norm_pre, conv_full, attn_sinks, norm_post,
                         weights_of, mid_fwd, grads_done, bwd_done)
    finish_reduce(0, sp)

    s_send, s_recv, s_buf, s_started = small_start(sp, 4 * px + 2 * py + pc, reduced[0][0])

    adamw_of(0, s_started)
    g_in, *u_in = [tr_(a) for a in big[0]]
    g_br, *u_br = [a.reshape(w_branch.shape) for a in big[1]]
    g_out, *u_out = big[2]

    meta_rows, rest = sum_small(small_wait(s_buf, s_send, s_recv, big[2][1]))
    o = 0
    g_meta_full = meta_rows.reshape(N_META, D_MODEL)
    g_norm_pre = rest[o:o + SP_NORM].reshape(DEPTH, D_MODEL); o += SP_NORM
    g_norm_post = rest[o:o + SP_NORM].reshape(DEPTH, D_MODEL); o += SP_NORM
    g_rel_bias = rest[o:o + N_BUCKETS, :ATT_HEADS]; o += SP_RB
    g_sinks = rest[o:o + SP_SINK, 0].reshape(DEPTH, ATT_HEADS); o += SP_SINK
    g_conv_full = rest[o:o + SP_CONV].reshape(DEPTH, 3, BRANCH_WIDTH); o += SP_CONV
    loss = rest[o, 0]
    g_meta = lax.dynamic_slice_in_dim(g_meta_full, chip * SHARD_D, SHARD_D, axis=1)
    g_conv = lax.dynamic_slice_in_dim(g_conv_full, chip * LANES, LANES, axis=2)

    swap01 = lambda arrays: tuple(jnp.swapaxes(a, 0, 1) for a in arrays)
    smalls = [(meta_tokens, g_meta, m_meta_tokens, v_meta_tokens),
              swap01((rel_bias, g_rel_bias, m_rel_bias, v_rel_bias)),
              (norm_pre, g_norm_pre, m_norm_pre, v_norm_pre),
              swap01((conv_w, g_conv, m_conv_w, v_conv_w)),
              (attn_sinks, g_sinks, m_attn_sinks, v_attn_sinks),
              (norm_post, g_norm_post, m_norm_post, v_norm_post)]
    u_meta, u_rb, u_npre, u_conv, u_sink, u_npost = adamw_small(smalls)
    u_rb, u_conv = swap01(u_rb), swap01(u_conv)

    grads = [g_meta, g_rel_bias, g_norm_pre, g_in, g_conv, g_sinks, g_br, g_out, g_norm_post]
    upd = [u_meta, u_rb, u_npre, u_in, u_conv, u_sink, u_br, u_out, u_npost]
    return (loss, d_x, *grads, *[u[0] for u in upd], *[u[1] for u in upd], *[u[2] for u in upd])
```

```python
import math

import numpy as np
import jax
import jax.numpy as jnp
from jax import lax
from jax.experimental import pallas as pl
from jax.experimental.pallas import tpu as pltpu

F32 = jnp.float32
BF16 = jnp.bfloat16
MESH = pl.DeviceIdType.MESH

D_MODEL = 1024
DEPTH = 2
N_META = 16
BLOCK = 128
PAD_FRONT = BLOCK - N_META
ATT_HEADS = 8
ATT_HEAD_DIM = 64
N_BUCKETS = 32
MAX_EXACT = 16
MAX_DISTANCE = 128
RET_HEADS = 4
ROT_BASE = 10000.0
N_BRANCH = 3
BRANCH_WIDTH = 512
PROJ_WIDTH = 8448
ABC_WIDTH = 5376
MERGE_WIDTH = N_BRANCH * D_MODEL
RMS_EPS = 1e-6
GN_EPS = 1e-6
NEG_INF = -1e30
ATT_SCALE = ATT_HEAD_DIM ** -0.5
RET_SCALE = BLOCK ** -0.5
LOG_GAMMA = tuple(math.log1p(-(2.0 ** (-5.0 - h))) for h in range(RET_HEADS))

C_AQ, C_AK, C_AV, C_AG = 0, 512, 640, 768
C_RQ, C_RK, C_RV, C_RG = 1280, 1792, 2304, 2816
C_CB, C_CC, C_CX, C_CG = 3328, 3840, 4352, 4864

ADAM_LR = 0.001
ADAM_B1 = 0.9
ADAM_B2 = 0.999
ADAM_EPS = 1e-08
ADAM_WD = 0.01
ADAM_STEP = 10

N_CHIPS = 4
SHARD_D = D_MODEL // N_CHIPS
LANES = 128

VMEM_LIMIT = 56 * 1024 * 1024
COL_TILE = 768
ROW_TILE = 1088
PROJ_ROW_TILE = 2176


def _cparams(*sem):
    return pltpu.CompilerParams(dimension_semantics=sem, vmem_limit_bytes=VMEM_LIMIT)


def _nt(a, b):
    return lax.dot_general(a, b, (((1,), (1,)), ((), ())), preferred_element_type=F32)


def _tn(a, b):
    return lax.dot_general(a, b, (((0,), (0,)), ((), ())), preferred_element_type=F32)


def _nn(a, b):
    return jnp.dot(a, b, preferred_element_type=F32)


def _sigmoid(x):
    return 0.5 * jnp.tanh(0.5 * x) + 0.5


def _silu(x):
    return x * _sigmoid(x)


def _dsilu(x):
    s = _sigmoid(x)
    return s * (1.0 + x * (1.0 - s))


def _bucket_table():
    r = np.arange(BLOCK)[:, None]
    c = np.arange(2 * BLOCK)[None, :]
    n = np.maximum(BLOCK + r - c, 0)
    nf = np.maximum(n, 1).astype(np.float32)
    large = MAX_EXACT + (np.log(nf / MAX_EXACT) / math.log(MAX_DISTANCE / MAX_EXACT)
                         * (N_BUCKETS - MAX_EXACT)).astype(np.int32)
    large = np.minimum(large, N_BUCKETS - 1)
    return np.where(n < MAX_EXACT, n, large).astype(np.int32)


def _rot_tables(lp):
    half = BLOCK // 2
    pos = (jnp.arange(lp) - PAD_FRONT).astype(F32)
    theta = 1.0 / (ROT_BASE ** jnp.linspace(0.0, 1.0, half, dtype=F32))
    ang = pos[:, None] * theta[None, :]
    cos, sin = jnp.cos(ang), jnp.sin(ang)
    return jnp.concatenate([cos, cos], axis=1), jnp.concatenate([-sin, sin], axis=1)


def norm_matmul(x2d, g, layer, w, col0_blocks, n_col_blocks, after):
    t = x2d.shape[0]
    tm = PROJ_ROW_TILE if t % PROJ_ROW_TILE == 0 else BLOCK

    def body(x_ref, g_ref, w_ref, after_ref, hb_ref, o_ref):
        @pl.when(pl.program_id(1) == 0)
        def _():
            x = x_ref[...]
            r = lax.rsqrt(jnp.mean(x * x, axis=-1, keepdims=True) + RMS_EPS)
            hb_ref[...] = (x * r * g_ref[...]).astype(BF16)

        o_ref[...] = _nt(hb_ref[...], w_ref[...])

    return pl.pallas_call(
        body, name="norm_matmul",
        grid=(t // tm, n_col_blocks),
        in_specs=[pl.BlockSpec((tm, D_MODEL), lambda i, j: (i, 0)),
                  pl.BlockSpec((None, 1, D_MODEL), lambda i, j: (layer, 0, 0)),
                  pl.BlockSpec((COL_TILE, D_MODEL), lambda i, j: (j + col0_blocks, 0)),
                  ANY],
        out_specs=[pl.BlockSpec((tm, D_MODEL), lambda i, j: (i, 0)),
                   pl.BlockSpec((tm, COL_TILE), lambda i, j: (i, j))],
        out_shape=[jax.ShapeDtypeStruct((t, D_MODEL), BF16),
                   jax.ShapeDtypeStruct((t, n_col_blocks * COL_TILE), F32)],
        compiler_params=_cparams("parallel", "arbitrary"),
    )(x2d, g, w, after)


def matmul_cols(a, w, col0_blocks, n_col_blocks):
    t, k = a.shape
    tm = PROJ_ROW_TILE if t % PROJ_ROW_TILE == 0 else BLOCK

    def body(a_ref, w_ref, o_ref):
        o_ref[...] = _nt(a_ref[...], w_ref[...]).astype(BF16)

    return pl.pallas_call(
        body, name="matmul_cols",
        grid=(t // tm, n_col_blocks),
        in_specs=[pl.BlockSpec((tm, k), lambda i, j: (i, 0)),
                  pl.BlockSpec((COL_TILE, k), lambda i, j: (j + col0_blocks, 0))],
        out_specs=pl.BlockSpec((tm, COL_TILE), lambda i, j: (i, j)),
        out_shape=jax.ShapeDtypeStruct((t, n_col_blocks * COL_TILE), BF16),
        compiler_params=_cparams("parallel", "arbitrary"),
    )(a, w)


class _Widened:
    def __init__(self, ref):
        self.ref = ref

    def __getitem__(self, idx):
        return self.ref[idx].astype(F32)


def _build_bias(bkt_ref, rb_ref, bias_s):
    bkt = bkt_ref[...]
    for h in range(ATT_HEADS):
        acc = jnp.zeros((BLOCK, 2 * BLOCK), F32)
        for b in range(N_BUCKETS):
            acc = jnp.where(bkt == b, rb_ref[b, h], acc)
        bias_s[h] = acc


def _band_mask(n):
    r = lax.broadcasted_iota(jnp.int32, (BLOCK, 2 * BLOCK), 0)
    c = lax.broadcasted_iota(jnp.int32, (BLOCK, 2 * BLOCK), 1)
    key_pos = (n - 1) * BLOCK + c
    return (c > r) & (c <= r + BLOCK) & (key_pos >= PAD_FRONT)


def _split_heads(kv, kh):
    lane = lax.broadcasted_iota(jnp.int32, kv.shape, 1)
    if kh == 0:
        lo = jnp.where(lane < ATT_HEAD_DIM, kv, 0.0)
        hi = pltpu.roll(lo, ATT_HEAD_DIM, 1)
    else:
        hi = jnp.where(lane >= ATT_HEAD_DIM, kv, 0.0)
        lo = pltpu.roll(hi, ATT_HEAD_DIM, 1)
    return lo, hi


def _merge_heads(acc_lo, acc_hi, kh):
    lane = lax.broadcasted_iota(jnp.int32, acc_lo.shape, 1)
    if kh == 0:
        return jnp.where(lane < ATT_HEAD_DIM, acc_lo + pltpu.roll(acc_hi, ATT_HEAD_DIM, 1), 0.0)
    return jnp.where(lane >= ATT_HEAD_DIM, acc_hi + pltpu.roll(acc_lo, ATT_HEAD_DIM, 1), 0.0)


def _softmax_of(qk, bias_h, mask, sink_h):
    s = qk + bias_h
    s = jnp.where(mask, s, NEG_INF)
    m = jnp.maximum(jnp.max(s, axis=-1, keepdims=True), sink_h)
    p = jnp.exp(s - m)
    es = jnp.exp(sink_h - m)
    inv = 1.0 / (jnp.sum(p, axis=-1, keepdims=True) + es)
    return p * inv, es * inv


def _rot(t, cosf, sinf):
    return t * cosf + pltpu.roll(t, BLOCK // 2, 1) * sinf


def _rot_t(d, cosf, sinf):
    return d * cosf + pltpu.roll(d * sinf, BLOCK // 2, 1)


def _decay_tables(h):
    lg = LOG_GAMMA[h]
    i = lax.broadcasted_iota(jnp.int32, (BLOCK, BLOCK), 0)
    j = lax.broadcasted_iota(jnp.int32, (BLOCK, BLOCK), 1)
    diff = (i - j).astype(F32)
    dm = jnp.where(diff >= 0, jnp.exp(diff * lg), 0.0)
    row = lax.broadcasted_iota(jnp.int32, (BLOCK, 1), 0).astype(F32)
    zeta = jnp.exp((BLOCK - 1 - row) * lg)
    xi = jnp.exp((row + 1.0) * lg)
    return dm, zeta, xi, math.exp(BLOCK * lg)


def _valid_col(n):
    row = lax.broadcasted_iota(jnp.int32, (BLOCK, 1), 0)
    return ((n * BLOCK + row) >= PAD_FRONT).astype(F32)


def _shift_down(cur, prev, k):
    row = lax.broadcasted_iota(jnp.int32, cur.shape, 0)
    return jnp.where(row >= k, pltpu.roll(cur, k, 0), pltpu.roll(prev, k, 0))


def _shift_up(cur, nxt, k):
    row = lax.broadcasted_iota(jnp.int32, cur.shape, 0)
    return jnp.where(row < BLOCK - k, pltpu.roll(cur, BLOCK - k, 0), pltpu.roll(nxt, BLOCK - k, 0))


def mixers_fwd(proj, cosf, sinf, bkt, rel_bias, sinks, conv_w, layer, nb, nc):
    def body(p_ref, cos_ref, sin_ref, bkt_ref, rb_ref, sk_ref, cw_ref, br_ref, st_ref,
             bias_s, kv_s, state_s, u_s):
        p_ref = _Widened(p_ref)
        n = pl.program_id(0)

        @pl.when(n == 0)
        def _():
            _build_bias(bkt_ref, rb_ref, bias_s)
            kv_s[:, 0:BLOCK, :] = jnp.zeros((nb, BLOCK, 2 * BLOCK), F32)
            state_s[...] = jnp.zeros_like(state_s)
            u_s[...] = jnp.zeros_like(u_s)

        valid = _valid_col(n)
        mask = _band_mask(n)
        ex = range(nb)

        for b in ex:
            kv_s[b, BLOCK:2 * BLOCK, :] = p_ref[b, :, C_AK:C_AK + 2 * BLOCK]
        for kh in range(2):
            ks = [[t.astype(BF16) for t in _split_heads(kv_s[b, :, 0:BLOCK], kh)] for b in ex]
            vs = [[t.astype(BF16) for t in _split_heads(kv_s[b, :, BLOCK:2 * BLOCK], kh)] for b in ex]
            pairs = [(b, 2 * kh + jj) for jj in range(2) for b in ex]
            subs = [(b, j, x) for (b, j) in pairs for x in range(2)]
            qb_ = {(b, j): (p_ref[b, :, C_AQ + BLOCK * j:C_AQ + BLOCK * (j + 1)] * ATT_SCALE).astype(BF16)
                   for (b, j) in pairs}
            qk_ = {(b, j, x): _nt(qb_[(b, j)], ks[b][x]) for (b, j, x) in subs}
            pb_ = {}
            for u in subs:
                h = 2 * u[1] + u[2]
                pb_[u] = _softmax_of(qk_[u], bias_s[h], mask, sk_ref[layer, h])[0].astype(BF16)
            o_ = {u: _nn(pb_[u], vs[u[0]][u[2]]) for u in subs}
            for (b, j) in pairs:
                gate = p_ref[b, :, C_AG + BLOCK * j:C_AG + BLOCK * (j + 1)]
                br_ref[b, :, BLOCK * j:BLOCK * (j + 1)] = ((o_[(b, j, 0)] + o_[(b, j, 1)]) * _silu(gate)).astype(BF16)
        for b in ex:
            kv_s[b, 0:BLOCK, :] = kv_s[b, BLOCK:2 * BLOCK, :]

        cosv = cos_ref[...]
        sinv = sin_ref[...]
        tabs = [_decay_tables(h) for h in range(RET_HEADS)]
        units = [(b, h) for h in range(RET_HEADS) for b in ex]
        sl = lambda c0, h: slice(c0 + BLOCK * h, c0 + BLOCK * (h + 1))
        q_, k_, v_, sp_ = {}, {}, {}, {}
        for u in units:
            b, h = u
            q_[u] = _rot(p_ref[b, :, sl(C_RQ, h)], cosv, sinv).astype(BF16)
            k_[u] = (_rot(p_ref[b, :, sl(C_RK, h)], cosv, sinv) * RET_SCALE * valid).astype(BF16)
            v_[u] = p_ref[b, :, sl(C_RV, h)]
            sp_[u] = state_s[b, h]
            st_ref[b, 0, h] = sp_[u]
        qk_ = {u: _nt(q_[u], k_[u]) for u in units}
        qs_ = {u: _nn(q_[u], sp_[u].astype(BF16)) for u in units}
        kv_ = {u: _tn(k_[u], (v_[u] * tabs[u[1]][1]).astype(BF16)) for u in units}
        a_ = {u: (qk_[u] * tabs[u[1]][0]).astype(BF16) for u in units}
        av_ = {u: _nn(a_[u], v_[u].astype(BF16)) for u in units}
        for u in units:
            b, h = u
            o = av_[u] + tabs[h][2] * qs_[u]
            mu = jnp.mean(o, axis=-1, keepdims=True)
            var = jnp.mean(jnp.square(o - mu), axis=-1, keepdims=True)
            oh = (o - mu) * lax.rsqrt(var + GN_EPS)
            gate = p_ref[b, :, sl(C_RG, h)]
            br_ref[b, :, BRANCH_WIDTH + BLOCK * h:BRANCH_WIDTH + BLOCK * (h + 1)] = (oh * _silu(gate)).astype(BF16)
            state_s[b, h] = tabs[h][3] * sp_[u] + kv_[u]

        for b in ex:
            u = p_ref[b, :, C_CC:C_CC + BRANCH_WIDTH] * p_ref[b, :, C_CX:C_CX + BRANCH_WIDTH] * valid
            u_prev = u_s[b]
            y = (cw_ref[0:1, :] * _shift_down(u, u_prev, 2) + cw_ref[1:2, :] * _shift_down(u, u_prev, 1)
                 + cw_ref[2:3, :] * u)
            yc = p_ref[b, :, C_CB:C_CB + BRANCH_WIDTH] * y * _silu(p_ref[b, :, C_CG:C_CG + BRANCH_WIDTH])
            br_ref[b, :, 2 * BRANCH_WIDTH:3 * BRANCH_WIDTH] = yc.astype(BF16)
            u_s[b] = u

    lp = nc * BLOCK
    smem = pl.BlockSpec(memory_space=pltpu.SMEM)
    br, states = pl.pallas_call(
        body, name="mixers_fwd",
        grid=(nc,),
        in_specs=[pl.BlockSpec((nb, BLOCK, ABC_WIDTH), lambda n: (0, n, 0)),
                  pl.BlockSpec((BLOCK, BLOCK), lambda n: (n, 0)),
                  pl.BlockSpec((BLOCK, BLOCK), lambda n: (n, 0)),
                  pl.BlockSpec((BLOCK, 2 * BLOCK), lambda n: (0, 0)),
                  smem, smem,
                  pl.BlockSpec((None, 3, BRANCH_WIDTH), lambda n: (layer, 0, 0))],
        out_specs=[pl.BlockSpec((nb, BLOCK, N_BRANCH * BRANCH_WIDTH), lambda n: (0, n, 0)),
                   pl.BlockSpec((nb, 1, RET_HEADS, BLOCK, BLOCK), lambda n: (0, n, 0, 0, 0))],
        out_shape=[jax.ShapeDtypeStruct((nb, lp, N_BRANCH * BRANCH_WIDTH), BF16),
                   jax.ShapeDtypeStruct((nb, nc, RET_HEADS, BLOCK, BLOCK), F32)],
        scratch_shapes=[pltpu.VMEM((ATT_HEADS, BLOCK, 2 * BLOCK), F32),
                        pltpu.VMEM((nb, 2 * BLOCK, 2 * BLOCK), F32),
                        pltpu.VMEM((nb, RET_HEADS, BLOCK, BLOCK), F32),
                        pltpu.VMEM((nb, BLOCK, BRANCH_WIDTH), F32)],
        compiler_params=_cparams("arbitrary"),
    )(proj.reshape(nb, lp, ABC_WIDTH), cosf, sinf, bkt, rel_bias, sinks, conv_w)
    return br.reshape(nb * lp, N_BRANCH * BRANCH_WIDTH), states


def mixers_bwd(proj, d_br, states, cosf, sinf, bkt, rel_bias, sinks, conv_w, layer, nb, nc):
    def body(p_ref, kvp_ref, cp_ref, dbr_ref, st_ref, cos_ref, sin_ref, bkt_ref, rb_ref, sk_ref, cw_ref,
             dp_ref, drb_ref, dsk_ref, dcw_ref,
             bias_s, dbias_s, dkv_s, g_s, dy_s):
        p_ref, kvp_ref, cp_ref, dbr_ref = [_Widened(r) for r in (p_ref, kvp_ref, cp_ref, dbr_ref)]
        step = pl.program_id(0)
        n = nc - 1 - step
        ex = range(nb)

        @pl.when(step == 0)
        def _():
            _build_bias(bkt_ref, rb_ref, bias_s)
            dbias_s[...] = jnp.zeros_like(dbias_s)
            dsk_ref[...] = jnp.zeros_like(dsk_ref)
            dcw_ref[...] = jnp.zeros_like(dcw_ref)
            drb_ref[...] = jnp.zeros_like(drb_ref)
            dkv_s[...] = jnp.zeros_like(dkv_s)
            g_s[...] = jnp.zeros_like(g_s)
            dy_s[...] = jnp.zeros_like(dy_s)

        valid = _valid_col(n)
        mask = _band_mask(n)
        has_prev = (n > 0).astype(F32)

        k_all, v_all = [], []
        for b in ex:
            kv_prev = kvp_ref[b] * has_prev
            kv_cur = p_ref[b, :, C_AK:C_AK + 2 * BLOCK]
            k_all.append(jnp.concatenate([kv_prev[:, 0:BLOCK], kv_cur[:, 0:BLOCK]], axis=0))
            v_all.append(jnp.concatenate([kv_prev[:, BLOCK:], kv_cur[:, BLOCK:]], axis=0))
        zero2 = jnp.zeros((2 * BLOCK, BLOCK), F32)
        dk_tot = [zero2 for _ in ex]
        dv_tot = [zero2 for _ in ex]
        for kh in range(2):
            ks = [[t.astype(BF16) for t in _split_heads(k_all[b], kh)] for b in ex]
            vs = [[t.astype(BF16) for t in _split_heads(v_all[b], kh)] for b in ex]
            pairs = [(b, 2 * kh + jj) for jj in range(2) for b in ex]
            subs = [(b, j, x) for (b, j) in pairs for x in range(2)]
            qb_, gate_, dya_, do2_ = {}, {}, {}, {}
            for w in pairs:
                b, j = w
                qb_[w] = (p_ref[b, :, C_AQ + BLOCK * j:C_AQ + BLOCK * (j + 1)] * ATT_SCALE).astype(BF16)
                gate_[w] = p_ref[b, :, C_AG + BLOCK * j:C_AG + BLOCK * (j + 1)]
                dya_[w] = dbr_ref[b, :, BLOCK * j:BLOCK * (j + 1)]
                do2_[w] = (dya_[w] * _silu(gate_[w])).astype(BF16)
            qk_ = {(b, j, x): _nt(qb_[(b, j)], ks[b][x]) for (b, j, x) in subs}
            dpm_ = {(b, j, x): _nt(do2_[(b, j)], vs[b][x]) for (b, j, x) in subs}
            pb_, dsb_ = {}, {}
            for u in subs:
                b, j, x = u
                h = 2 * j + x
                p, p_sink = _softmax_of(qk_[u], bias_s[h], mask, sk_ref[layer, h])
                pb_[u] = p.astype(BF16)
                delta = jnp.sum(p * dpm_[u], axis=-1, keepdims=True)
                ds = p * (dpm_[u] - delta)
                dbias_s[h] += ds
                dsk_ref[h:h + 1, :] += jnp.broadcast_to(
                    jnp.sum(-p_sink * delta, axis=0, keepdims=True), (1, BLOCK))
                dsb_[u] = ds.astype(BF16)
            o_ = {u: _nn(pb_[u], vs[u[0]][u[2]]) for u in subs}
            dq_ = {u: _nn(dsb_[u], ks[u[0]][u[2]]) for u in subs}
            dkm_ = {u: _tn(dsb_[u], qb_[(u[0], u[1])]) for u in subs}
            dvm_ = {u: _tn(pb_[u], do2_[(u[0], u[1])]) for u in subs}
            for w in pairs:
                b, j = w
                o2 = o_[(b, j, 0)] + o_[(b, j, 1)]
                dq2 = (dq_[(b, j, 0)] + dq_[(b, j, 1)]) * ATT_SCALE
                dp_ref[b, :, C_AQ + BLOCK * j:C_AQ + BLOCK * (j + 1)] = dq2.astype(BF16)
                dp_ref[b, :, C_AG + BLOCK * j:C_AG + BLOCK * (j + 1)] = (
                    dya_[w] * o2 * _dsilu(gate_[w])).astype(BF16)
            for b in ex:
                j0, j1 = 2 * kh, 2 * kh + 1
                dk_tot[b] = dk_tot[b] + _merge_heads(dkm_[(b, j0, 0)] + dkm_[(b, j1, 0)],
                                                     dkm_[(b, j0, 1)] + dkm_[(b, j1, 1)], kh)
                dv_tot[b] = dv_tot[b] + _merge_heads(dvm_[(b, j0, 0)] + dvm_[(b, j1, 0)],
                                                     dvm_[(b, j0, 1)] + dvm_[(b, j1, 1)], kh)
        for b in ex:
            dp_ref[b, :, C_AK:C_AK + BLOCK] = (dk_tot[b][BLOCK:, :] + dkv_s[b, :, 0:BLOCK]).astype(BF16)
            dp_ref[b, :, C_AV:C_AV + BLOCK] = (dv_tot[b][BLOCK:, :] + dkv_s[b, :, BLOCK:]).astype(BF16)
            dkv_s[b, :, 0:BLOCK] = dk_tot[b][0:BLOCK, :]
            dkv_s[b, :, BLOCK:] = dv_tot[b][0:BLOCK, :]

        cosv = cos_ref[...]
        sinv = sin_ref[...]
        tabs = [_decay_tables(h) for h in range(RET_HEADS)]
        units = [(b, h) for h in range(RET_HEADS) for b in ex]
        sl = lambda c0, h: slice(c0 + BLOCK * h, c0 + BLOCK * (h + 1))
        q_, k_, v_, vb_, sp_ = {}, {}, {}, {}, {}
        for u in units:
            b, h = u
            q_[u] = _rot(p_ref[b, :, sl(C_RQ, h)], cosv, sinv).astype(BF16)
            k_[u] = (_rot(p_ref[b, :, sl(C_RK, h)], cosv, sinv) * RET_SCALE * valid).astype(BF16)
            v_[u] = p_ref[b, :, sl(C_RV, h)]
            vb_[u] = v_[u].astype(BF16)
            sp_[u] = st_ref[b, 0, h].astype(BF16)
        qk_ = {u: _nt(q_[u], k_[u]) for u in units}
        qs_ = {u: _nn(q_[u], sp_[u]) for u in units}
        a_ = {u: (qk_[u] * tabs[u[1]][0]).astype(BF16) for u in units}
        av_ = {u: _nn(a_[u], vb_[u]) for u in units}
        dob_, dxo_ = {}, {}
        for u in units:
            b, h = u
            xi = tabs[h][2]
            o = av_[u] + xi * qs_[u]
            mu = jnp.mean(o, axis=-1, keepdims=True)
            var = jnp.mean(jnp.square(o - mu), axis=-1, keepdims=True)
            rstd = lax.rsqrt(var + GN_EPS)
            oh = (o - mu) * rstd
            gate = p_ref[b, :, sl(C_RG, h)]
            d_yr = dbr_ref[b, :, BRANCH_WIDTH + BLOCK * h:BRANCH_WIDTH + BLOCK * (h + 1)]
            dp_ref[b, :, sl(C_RG, h)] = (d_yr * oh * _dsilu(gate)).astype(BF16)
            doh = d_yr * _silu(gate)
            do = rstd * (doh - jnp.mean(doh, axis=-1, keepdims=True)
                         - oh * jnp.mean(doh * oh, axis=-1, keepdims=True))
            dob_[u] = do.astype(BF16)
            dxo_[u] = (do * xi).astype(BF16)
        dov_ = {u: _nt(dob_[u], vb_[u]) for u in units}
        dv1_ = {u: _tn(a_[u], dob_[u]) for u in units}
        dq1_ = {u: _nt(dxo_[u], sp_[u]) for u in units}
        gq_ = {u: _tn(q_[u], dxo_[u]) for u in units}
        da_, gb_, zv_ = {}, {}, {}
        for u in units:
            b, h = u
            da_[u] = (dov_[u] * tabs[h][0]).astype(BF16)
            g_next = g_s[b, h]
            gb_[u] = g_next.astype(BF16)
            zv_[u] = (v_[u] * tabs[h][1]).astype(BF16)
            g_s[b, h] = tabs[h][3] * g_next + gq_[u]
        dq2_ = {u: _nn(da_[u], k_[u]) for u in units}
        dk1_ = {u: _tn(da_[u], q_[u]) for u in units}
        dk2_ = {u: _nt(zv_[u], gb_[u]) for u in units}
        dv2_ = {u: _nn(k_[u], gb_[u]) for u in units}
        for u in units:
            b, h = u
            dp_ref[b, :, sl(C_RQ, h)] = _rot_t(dq2_[u] + dq1_[u], cosv, sinv).astype(BF16)
            dp_ref[b, :, sl(C_RK, h)] = _rot_t((dk1_[u] + dk2_[u]) * (RET_SCALE * valid), cosv, sinv).astype(BF16)
            dp_ref[b, :, sl(C_RV, h)] = (dv1_[u] + tabs[h][1] * dv2_[u]).astype(BF16)

        w0, w1, w2 = cw_ref[0:1, :], cw_ref[1:2, :], cw_ref[2:3, :]
        for b in ex:
            cb = p_ref[b, :, C_CB:C_CB + BRANCH_WIDTH]
            cc = p_ref[b, :, C_CC:C_CC + BRANCH_WIDTH]
            cx = p_ref[b, :, C_CX:C_CX + BRANCH_WIDTH]
            cg = p_ref[b, :, C_CG:C_CG + BRANCH_WIDTH]
            u = cc * cx * valid
            u_prev = (cp_ref[b, :, 0:BRANCH_WIDTH] * cp_ref[b, :, BRANCH_WIDTH:2 * BRANCH_WIDTH]
                      * (_valid_col(n - 1) * has_prev))
            u1 = _shift_down(u, u_prev, 1)
            u2 = _shift_down(u, u_prev, 2)
            y = w0 * u2 + w1 * u1 + w2 * u
            d_yc = dbr_ref[b, :, 2 * BRANCH_WIDTH:3 * BRANCH_WIDTH]
            sg = _silu(cg)
            dp_ref[b, :, C_CB:C_CB + BRANCH_WIDTH] = (d_yc * y * sg).astype(BF16)
            dp_ref[b, :, C_CG:C_CG + BRANCH_WIDTH] = (d_yc * cb * y * _dsilu(cg)).astype(BF16)
            dy = d_yc * cb * sg
            dy_next = dy_s[b]
            du = (w2 * dy + w1 * _shift_up(dy, dy_next, 1) + w0 * _shift_up(dy, dy_next, 2)) * valid
            dp_ref[b, :, C_CC:C_CC + BRANCH_WIDTH] = (du * cx).astype(BF16)
            dp_ref[b, :, C_CX:C_CX + BRANCH_WIDTH] = (du * cc).astype(BF16)
            dcw_ref[0:1, :] += jnp.sum(dy * u2, axis=0, keepdims=True)
            dcw_ref[1:2, :] += jnp.sum(dy * u1, axis=0, keepdims=True)
            dcw_ref[2:3, :] += jnp.sum(dy * u, axis=0, keepdims=True)
            dy_s[b] = dy

        @pl.when(step == nc - 1)
        def _():
            bkt = bkt_ref[...]
            row = lax.broadcasted_iota(jnp.int32, (N_BUCKETS, BLOCK), 0)
            lane = lax.broadcasted_iota(jnp.int32, (N_BUCKETS, BLOCK), 1)

            def one_bucket(bk, acc):
                sel = bkt == bk
                for h in range(ATT_HEADS):
                    t = jnp.where(sel, dbias_s[h], 0.0)
                    s = jnp.sum(jnp.sum(t, axis=1, keepdims=True), axis=0, keepdims=True)
                    acc = acc + jnp.where((row == bk) & (lane == h), jnp.broadcast_to(s, acc.shape), 0.0)
                return acc

            drb_ref[...] = lax.fori_loop(0, N_BUCKETS, one_bucket, jnp.zeros((N_BUCKETS, BLOCK), F32))

    lp = nc * BLOCK
    smem = pl.BlockSpec(memory_space=pltpu.SMEM)
    blk = lambda s: nc - 1 - s
    prev = lambda s: jnp.maximum(nc - 2 - s, 0)
    proj3 = proj.reshape(nb, lp, ABC_WIDTH)
    res = pl.pallas_call(
        body, name="mixers_bwd",
        grid=(nc,),
        in_specs=[pl.BlockSpec((nb, BLOCK, ABC_WIDTH), lambda s: (0, blk(s), 0)),
                  pl.BlockSpec((nb, BLOCK, 2 * BLOCK), lambda s: (0, prev(s), C_AK // (2 * BLOCK))),
                  pl.BlockSpec((nb, BLOCK, 1280), lambda s: (0, prev(s), C_CC // 1280)),
                  pl.BlockSpec((nb, BLOCK, N_BRANCH * BRANCH_WIDTH), lambda s: (0, blk(s), 0)),
                  pl.BlockSpec((nb, 1, RET_HEADS, BLOCK, BLOCK), lambda s: (0, blk(s), 0, 0, 0)),
                  pl.BlockSpec((BLOCK, BLOCK), lambda s: (blk(s), 0)),
                  pl.BlockSpec((BLOCK, BLOCK), lambda s: (blk(s), 0)),
                  pl.BlockSpec((BLOCK, 2 * BLOCK), lambda s: (0, 0)),
                  smem, smem,
                  pl.BlockSpec((None, 3, BRANCH_WIDTH), lambda s: (layer, 0, 0))],
        out_specs=[pl.BlockSpec((nb, BLOCK, ABC_WIDTH), lambda s: (0, blk(s), 0)),
                   pl.BlockSpec((N_BUCKETS, BLOCK), lambda s: (0, 0)),
                   pl.BlockSpec((ATT_HEADS, BLOCK), lambda s: (0, 0)),
                   pl.BlockSpec((8, BRANCH_WIDTH), lambda s: (0, 0))],
        out_shape=[jax.ShapeDtypeStruct((nb, lp, ABC_WIDTH), BF16),
                   jax.ShapeDtypeStruct((N_BUCKETS, BLOCK), F32),
                   jax.ShapeDtypeStruct((ATT_HEADS, BLOCK), F32),
                   jax.ShapeDtypeStruct((8, BRANCH_WIDTH), F32)],
        scratch_shapes=[pltpu.VMEM((ATT_HEADS, BLOCK, 2 * BLOCK), F32),
                        pltpu.VMEM((ATT_HEADS, BLOCK, 2 * BLOCK), F32),
                        pltpu.VMEM((nb, BLOCK, 2 * BLOCK), F32),
                        pltpu.VMEM((nb, RET_HEADS, BLOCK, BLOCK), F32),
                        pltpu.VMEM((nb, BLOCK, BRANCH_WIDTH), F32)],
        compiler_params=_cparams("arbitrary"),
    )(proj3, proj3, proj3, d_br.reshape(nb, lp, N_BRANCH * BRANCH_WIDTH), states, cosf, sinf, bkt, rel_bias, sinks,
      conv_w)
    return (res[0].reshape(nb * lp, ABC_WIDTH),) + tuple(res[1:])


MERGE_TILE = 256
MERGE_FWD_TILE = 544


def _merge_forward(br_ref, m_ref, wb_ref, wo_ref):
    bo, gates = [], []
    mixed_pre = None
    for g in range(N_BRANCH):
        br_g = br_ref[:, BRANCH_WIDTH * g:BRANCH_WIDTH * (g + 1)]
        bo_g = jnp.concatenate([_nn(br_g, wb_ref[p, g]) for p in range(N_CHIPS)], axis=1)
        gate_g = _sigmoid(m_ref[:, D_MODEL * g:D_MODEL * (g + 1)].astype(F32))
        bo.append(bo_g)
        gates.append(gate_g)
        mixed_pre = gate_g * bo_g if mixed_pre is None else mixed_pre + gate_g * bo_g
    mixed = _nn(mixed_pre.astype(BF16), wo_ref[...])
    r = lax.rsqrt(jnp.mean(mixed * mixed, axis=-1, keepdims=True) + RMS_EPS)
    return bo, gates, mixed_pre, mixed, r


def merge_fwd(x2d, br, pm, wb, wo, g_post, layer, after):
    t = x2d.shape[0]
    tm = MERGE_FWD_TILE if t % MERGE_FWD_TILE == 0 else BLOCK

    def body(x_ref, br_ref, m_ref, wb_ref, wo_ref, g_ref, after_ref, o_ref):
        _, _, _, mixed, r = _merge_forward(br_ref, m_ref, wb_ref, wo_ref)
        o_ref[...] = x_ref[...] + mixed * r * g_ref[...]

    return pl.pallas_call(
        body, name="merge_fwd",
        grid=(t // tm,),
        in_specs=[pl.BlockSpec((tm, D_MODEL), lambda i: (i, 0)),
                  pl.BlockSpec((tm, N_BRANCH * BRANCH_WIDTH), lambda i: (i, 0)),
                  pl.BlockSpec((tm, MERGE_WIDTH), lambda i: (i, 0)),
                  pl.BlockSpec((N_CHIPS, N_BRANCH, BRANCH_WIDTH, SHARD_D), lambda i: (0, 0, 0, 0)),
                  pl.BlockSpec((D_MODEL, D_MODEL), lambda i: (0, 0)),
                  pl.BlockSpec((None, 1, D_MODEL), lambda i: (layer, 0, 0)),
                  ANY],
        out_specs=pl.BlockSpec((tm, D_MODEL), lambda i: (i, 0)),
        out_shape=jax.ShapeDtypeStruct((t, D_MODEL), F32),
        compiler_params=_cparams("parallel"),
    )(x2d, br, pm, wb, wo, g_post, after)


def merge_fwd_loss(x2d, br, pm, wb, wo, g_post, layer, target, lp):
    t = x2d.shape[0]
    tm = MERGE_FWD_TILE if t % MERGE_FWD_TILE == 0 else BLOCK

    def body(x_ref, br_ref, m_ref, wb_ref, wo_ref, g_ref, t_ref, l_ref, d_ref):
        i = pl.program_id(0)

        @pl.when(i == 0)
        def _():
            l_ref[...] = jnp.zeros_like(l_ref)

        _, _, _, mixed, r = _merge_forward(br_ref, m_ref, wb_ref, wo_ref)
        y = x_ref[...] + mixed * r * g_ref[...]
        row = i * tm + lax.broadcasted_iota(jnp.int32, (tm, 1), 0)
        e = jnp.where(row % lp >= BLOCK, y - t_ref[...], 0.0)
        d_ref[...] = e * (1.0 / D_MODEL)
        s = jnp.sum(jnp.sum(e * e, axis=1, keepdims=True), axis=0, keepdims=True)
        l_ref[...] += jnp.broadcast_to(s * (0.5 / D_MODEL), l_ref.shape)

    return pl.pallas_call(
        body, name="merge_fwd_loss",
        grid=(t // tm,),
        in_specs=[pl.BlockSpec((tm, D_MODEL), lambda i: (i, 0)),
                  pl.BlockSpec((tm, N_BRANCH * BRANCH_WIDTH), lambda i: (i, 0)),
                  pl.BlockSpec((tm, MERGE_WIDTH), lambda i: (i, 0)),
                  pl.BlockSpec((N_CHIPS, N_BRANCH, BRANCH_WIDTH, SHARD_D), lambda i: (0, 0, 0, 0)),
                  pl.BlockSpec((D_MODEL, D_MODEL), lambda i: (0, 0)),
                  pl.BlockSpec((None, 1, D_MODEL), lambda i: (layer, 0, 0)),
                  pl.BlockSpec((tm, D_MODEL), lambda i: (i, 0))],
        out_specs=[pl.BlockSpec((8, BLOCK), lambda i: (0, 0)),
                   pl.BlockSpec((tm, D_MODEL), lambda i: (i, 0))],
        out_shape=[jax.ShapeDtypeStruct((8, BLOCK), F32),
                   jax.ShapeDtypeStruct((t, D_MODEL), F32)],
        compiler_params=_cparams("arbitrary"),
    )(x2d, br, pm, wb, wo, g_post, target)


def merge_bwd(d_out, br, pm, wb, wo, g_post, layer, after):
    t = d_out.shape[0]
    tm = MERGE_TILE if t % MERGE_TILE == 0 else BLOCK

    def body(do_ref, br_ref, m_ref, wb_ref, wo_ref, g_ref, after_ref, dbr_ref, dm_ref, dg_ref, dwb_ref, dwo_ref):

        @pl.when(pl.program_id(0) == 0)
        def _():
            dwb_ref[...] = jnp.zeros_like(dwb_ref)
            dwo_ref[...] = jnp.zeros_like(dwo_ref)
            dg_ref[...] = jnp.zeros_like(dg_ref)

        bo, gates, mixed_pre, mixed, r = _merge_forward(br_ref, m_ref, wb_ref, wo_ref)
        d_o = do_ref[...]
        nh = mixed * r
        dg_ref[0:1, :] += jnp.sum(d_o * nh, axis=0, keepdims=True)
        dn = d_o * g_ref[...]
        d_mixed = (r * (dn - nh * jnp.mean(dn * nh, axis=-1, keepdims=True))).astype(BF16)
        dwo_ref[...] += _tn(mixed_pre.astype(BF16), d_mixed)
        d_pre = _nt(d_mixed, wo_ref[...])
        for g in range(N_BRANCH):
            br_g = br_ref[:, BRANCH_WIDTH * g:BRANCH_WIDTH * (g + 1)]
            d_bo = (d_pre * gates[g]).astype(BF16)
            dm_ref[:, D_MODEL * g:D_MODEL * (g + 1)] = (
                d_pre * bo[g] * gates[g] * (1.0 - gates[g])).astype(BF16)
            d_br_g = None
            for p in range(N_CHIPS):
                d_bo_p = d_bo[:, SHARD_D * p:SHARD_D * (p + 1)]
                part = _nt(d_bo_p, wb_ref[p, g])
                d_br_g = part if d_br_g is None else d_br_g + part
                dwb_ref[p, g] += _tn(br_g, d_bo_p)
            dbr_ref[:, BRANCH_WIDTH * g:BRANCH_WIDTH * (g + 1)] = d_br_g.astype(BF16)

    return pl.pallas_call(
        body, name="merge_bwd",
        grid=(t // tm,),
        in_specs=[pl.BlockSpec((tm, D_MODEL), lambda i: (i, 0)),
                  pl.BlockSpec((tm, N_BRANCH * BRANCH_WIDTH), lambda i: (i, 0)),
                  pl.BlockSpec((tm, MERGE_WIDTH), lambda i: (i, 0)),
                  pl.BlockSpec((N_CHIPS, N_BRANCH, BRANCH_WIDTH, SHARD_D), lambda i: (0, 0, 0, 0)),
                  pl.BlockSpec((D_MODEL, D_MODEL), lambda i: (0, 0)),
                  pl.BlockSpec((None, 1, D_MODEL), lambda i: (layer, 0, 0)),
                  ANY],
        out_specs=[pl.BlockSpec((tm, N_BRANCH * BRANCH_WIDTH), lambda i: (i, 0)),
                   pl.BlockSpec((tm, MERGE_WIDTH), lambda i: (i, 0)),
                   pl.BlockSpec((8, D_MODEL), lambda i: (0, 0)),
                   pl.BlockSpec((N_CHIPS, N_BRANCH, BRANCH_WIDTH, SHARD_D), lambda i: (0, 0, 0, 0)),
                   pl.BlockSpec((D_MODEL, D_MODEL), lambda i: (0, 0))],
        out_shape=[jax.ShapeDtypeStruct((t, N_BRANCH * BRANCH_WIDTH), BF16),
                   jax.ShapeDtypeStruct((t, MERGE_WIDTH), BF16),
                   jax.ShapeDtypeStruct((8, D_MODEL), F32),
                   jax.ShapeDtypeStruct((N_CHIPS, N_BRANCH, BRANCH_WIDTH, SHARD_D), F32),
                   jax.ShapeDtypeStruct((D_MODEL, D_MODEL), F32)],
        compiler_params=_cparams("arbitrary"),
    )(d_out, br, pm, wb, wo, g_post, after)


N_ABC_TILES = ABC_WIDTH // COL_TILE
N_M_TILES = MERGE_WIDTH // COL_TILE


def proj_dgrad(d_abc, d_m, w, x2d, g, layer, d_out, after, examples=None):
    t = x2d.shape[0]
    tm = ROW_TILE if t % ROW_TILE == 0 else BLOCK
    nk = N_ABC_TILES + N_M_TILES
    n_i = t // tm
    if examples is not None:
        lp = t // examples
        assert lp % tm == 0
        tiles_per_example = lp // tm

    def body(da_ref, dm_ref, w_ref, x_ref, g_ref, do_ref, after_ref, dx_ref, dg_ref, *rest):
        acc = rest[-3] if examples is not None else rest[-1]
        i = pl.program_id(0)
        k = pl.program_id(1)

        def for_copy_of_tile(tile, act):
            seq_ref, stage, sem = rest[0], rest[-2], rest[-1]
            e, j = tile // tiles_per_example, tile % tiles_per_example
            if tm > BLOCK:
                @pl.when(j == 0)
                def _():
                    act(pltpu.make_async_copy(stage.at[pl.ds(BLOCK, tm - BLOCK)],
                                              seq_ref.at[e, pl.ds(0, tm - BLOCK)], sem.at[0]))

            @pl.when(j > 0)
            def _():
                act(pltpu.make_async_copy(stage, seq_ref.at[e, pl.ds(pl.multiple_of(j * tm - BLOCK, 8), tm)],
                                          sem.at[0]))

        @pl.when((i == 0) & (k == 0))
        def _():
            dg_ref[...] = jnp.zeros_like(dg_ref)

        @pl.when(k == 0)
        def _():
            acc[...] = jnp.zeros_like(acc)

        @pl.when(k < N_ABC_TILES)
        def _():
            acc[...] += _nn(da_ref[...], w_ref[...])

        @pl.when(k >= N_ABC_TILES)
        def _():
            acc[...] += _nn(dm_ref[...], w_ref[...])

        @pl.when(k == nk - 1)
        def _():
            x = x_ref[...]
            r = lax.rsqrt(jnp.mean(x * x, axis=-1, keepdims=True) + RMS_EPS)
            nh = x * r
            dh = acc[...]
            dg_ref[0:1, :] += jnp.sum(dh * nh, axis=0, keepdims=True)
            dn = dh * g_ref[...]
            dx = do_ref[...] + r * (dn - nh * jnp.mean(dn * nh, axis=-1, keepdims=True))
            dx_ref[...] = dx
            if examples is not None:
                @pl.when(i > 0)
                def _():
                    for_copy_of_tile(i - 1, lambda copy: copy.wait())

                rest[-2][...] = dx
                for_copy_of_tile(i, lambda copy: copy.start())

                @pl.when(i == n_i - 1)
                def _():
                    for_copy_of_tile(i, lambda copy: copy.wait())

    out_specs = [pl.BlockSpec((tm, D_MODEL), lambda i, k: (i, 0)),
                 pl.BlockSpec((8, D_MODEL), lambda i, k: (0, 0))]
    out_shape = [jax.ShapeDtypeStruct((t, D_MODEL), F32),
                 jax.ShapeDtypeStruct((8, D_MODEL), F32)]
    scratch_shapes = [pltpu.VMEM((tm, D_MODEL), F32)]
    if examples is not None:
        out_specs.append(ANY)
        out_shape.append(jax.ShapeDtypeStruct((examples, lp - BLOCK, D_MODEL), F32))
        scratch_shapes += [pltpu.VMEM((tm, D_MODEL), F32), pltpu.SemaphoreType.DMA((1,))]

    return pl.pallas_call(
        body, name="proj_dgrad",
        grid=(n_i, nk),
        in_specs=[pl.BlockSpec((tm, COL_TILE), lambda i, k: (i, jnp.minimum(k, N_ABC_TILES - 1))),
                  pl.BlockSpec((tm, COL_TILE), lambda i, k: (i, jnp.maximum(k - N_ABC_TILES, 0))),
                  pl.BlockSpec((COL_TILE, D_MODEL), lambda i, k: (k, 0)),
                  pl.BlockSpec((tm, D_MODEL), lambda i, k: (i, 0)),
                  pl.BlockSpec((None, 1, D_MODEL), lambda i, k: (layer, 0, 0)),
                  pl.BlockSpec((tm, D_MODEL), lambda i, k: (i, 0)),
                  ANY],
        out_specs=out_specs, out_shape=out_shape, scratch_shapes=scratch_shapes,
        compiler_params=_cparams("arbitrary", "arbitrary"),
    )(d_abc, d_m, w, x2d, g, d_out, after)


def proj_wgrad(hb, d_abc, d_m):
    t = hb.shape[0]
    nj = N_ABC_TILES + N_M_TILES

    def body(h_ref, da_ref, dm_ref, o_ref):
        j = pl.program_id(0)

        @pl.when(j < N_ABC_TILES)
        def _():
            o_ref[...] = _tn(da_ref[...], h_ref[...])

        @pl.when(j >= N_ABC_TILES)
        def _():
            o_ref[...] = _tn(dm_ref[...], h_ref[...])

    return pl.pallas_call(
        body, name="proj_wgrad",
        grid=(nj,),
        in_specs=[pl.BlockSpec((t, D_MODEL), lambda j: (0, 0)),
                  pl.BlockSpec((t, COL_TILE), lambda j: (0, jnp.minimum(j, N_ABC_TILES - 1))),
                  pl.BlockSpec((t, COL_TILE), lambda j: (0, jnp.maximum(j - N_ABC_TILES, 0)))],
        out_specs=pl.BlockSpec((COL_TILE, D_MODEL), lambda j: (j, 0)),
        out_shape=jax.ShapeDtypeStruct((PROJ_WIDTH, D_MODEL), F32),
        compiler_params=_cparams("arbitrary"),
    )(hb, d_abc, d_m)


def _adamw_math(w, g, m, v):
    m = ADAM_B1 * m + (1.0 - ADAM_B1) * g
    v = ADAM_B2 * v + (1.0 - ADAM_B2) * jnp.square(g)
    m_hat = m / (1.0 - ADAM_B1 ** ADAM_STEP)
    v_hat = v / (1.0 - ADAM_B2 ** ADAM_STEP)
    delta = -ADAM_LR * (m_hat / (jnp.sqrt(v_hat) + ADAM_EPS) + ADAM_WD * w)
    return delta, m, v


def adamw_layer(w, g, m, v, layer, acc, after):
    _, r, c = w.shape
    tr = _row_tile(r)

    def body(*refs):
        w_ref, g_ref, m_ref, v_ref = refs[:4]
        go_ref, d_ref, mo_ref, vo_ref = refs[-4:]
        g_val = g_ref[...]
        d, m_new, v_new = _adamw_math(w_ref[...], g_val, m_ref[...], v_ref[...])
        go_ref[...] = g_val
        d_ref[...] = d
        mo_ref[...] = m_new
        vo_ref[...] = v_new

    slab = pl.BlockSpec((None, tr, c), lambda i: (layer, i, 0))
    ins = [w, g, m, v, after]
    in_specs = [slab, pl.BlockSpec((tr, c), lambda i: (i, 0)), slab, slab, ANY]
    aliases = {}
    if acc is not None:
        ins += list(acc)
        in_specs += [ANY] * 4
        aliases = {5 + i: i for i in range(4)}
    return pl.pallas_call(
        body, name="adamw_layer",
        grid=(r // tr,),
        in_specs=in_specs, out_specs=[slab] * 4,
        out_shape=[jax.ShapeDtypeStruct(w.shape, F32)] * 4,
        input_output_aliases=aliases,
        compiler_params=_cparams("parallel"),
    )(*ins)


def adamw_small(params):
    k = len(params)

    def body(*refs):
        ins, outs = refs[:4 * k], refs[4 * k:]
        for i in range(k):
            d, m_new, v_new = _adamw_math(*[r[...] for r in ins[4 * i:4 * i + 4]])
            outs[3 * i][...] = d
            outs[3 * i + 1][...] = m_new
            outs[3 * i + 2][...] = v_new

    flat = [a for p in params for a in p]
    vm = pl.BlockSpec(memory_space=pltpu.VMEM)
    out_shape = [jax.ShapeDtypeStruct(p[0].shape, F32) for p in params for _ in range(3)]
    res = pl.pallas_call(
        body, name="adamw_small",
        in_specs=[vm] * len(flat), out_specs=[vm] * len(out_shape), out_shape=out_shape,
    )(*flat)
    return [tuple(res[3 * i:3 * i + 3]) for i in range(k)]


ANY = pl.BlockSpec(memory_space=pl.ANY)


def _place():
    return lax.axis_index("x"), lax.axis_index("y"), lax.axis_index("c")


HBM = pl.BlockSpec(memory_space=pltpu.HBM)
SEM = pl.BlockSpec(memory_space=pltpu.SEMAPHORE)
EFFECT = pltpu.SideEffectType.DATAFLOW_SIDE_EFFECTING


def _other_chips(x, y):
    return [(1 - x, y), (x, 1 - y), (1 - x, 1 - y)]


def _own_slot(shard, chip):
    buf = lax.empty((N_CHIPS,) + shard.shape, shard.dtype)
    return lax.dynamic_update_slice(buf, shard[None], (chip, 0, 0, 0))


def _hbm(a):
    return pltpu.with_memory_space_constraint(a, pltpu.HBM)


def gather_start(bufs, after):
    n = len(bufs)

    def body(*refs):
        g_refs = refs[:n]
        send_sems, recv_sems = refs[n + 1], refs[n + 2]
        token = refs[-1]
        x, y, c = _place()
        me_p = 2 * x + y
        for t in range(n):
            for k, (qx, qy) in enumerate(_other_chips(x, y)):
                slab = g_refs[t].at[me_p, c]
                pltpu.make_async_remote_copy(src_ref=slab, dst_ref=slab, send_sem=send_sems.at[3 * t + k],
                                             recv_sem=recv_sems.at[3 * t + k], device_id=(qx, qy, c),
                                             device_id_type=MESH).start()
        token[...] = jnp.zeros_like(token)

    res = pl.pallas_call(
        body, name="gather_start",
        in_specs=[HBM] * n + [ANY],
        out_specs=[SEM, SEM] + [HBM] * n + [pl.BlockSpec(memory_space=pltpu.VMEM)],
        out_shape=[pltpu.SemaphoreType.DMA((3 * n,)), pltpu.SemaphoreType.DMA((3 * n,))]
        + [pltpu.HBM(b.shape, b.dtype) for b in bufs] + [jax.ShapeDtypeStruct((8, LANES), F32)],
        input_output_aliases={t: 2 + t for t in range(n)},
        compiler_params=pltpu.CompilerParams(has_side_effects=EFFECT),
    )(*[_hbm(b) for b in bufs], after)
    return res[0], res[1], list(res[2:2 + n]), res[-1]


def gather_wait(bufs, send_sems, recv_sems, after, first=0):
    n = len(bufs)

    def body(*refs):
        g_refs = refs[:n]
        send_sems, recv_sems = refs[n], refs[n + 1]
        x, y, c = _place()
        me_p = 2 * x + y
        for t in range(n):
            for k, (qx, qy) in enumerate(_other_chips(x, y)):
                s = 3 * (first + t) + k
                cp = pltpu.make_async_remote_copy(src_ref=g_refs[t].at[me_p, c], dst_ref=g_refs[t].at[2 * qx + qy, c],
                                                  send_sem=send_sems.at[s], recv_sem=recv_sems.at[s],
                                                  device_id=(qx, qy, c), device_id_type=MESH)
                cp.wait_send()
                cp.wait_recv()

    return pl.pallas_call(
        body, name="gather_wait",
        in_specs=[HBM] * n + [SEM, SEM, ANY],
        out_specs=[HBM] * n,
        out_shape=[pltpu.HBM(b.shape, b.dtype) for b in bufs],
        input_output_aliases={t: t for t in range(n)},
        compiler_params=pltpu.CompilerParams(has_side_effects=EFFECT),
    )(*bufs, send_sems, recv_sems, after)


def gather_forward(bufs):
    n = len(bufs)

    def body(*refs):
        g_refs = refs[n:2 * n]
        send_sems, recv_sems = refs[2 * n:]
        x, y, c = _place()
        sibling = (x, y, 1 - c)
        chips = _other_chips(x, y)
        passed = []
        for t in range(n):
            for k, (qx, qy) in enumerate(chips):
                slab = g_refs[t].at[2 * qx + qy, c]
                fwd = pltpu.make_async_remote_copy(src_ref=slab, dst_ref=slab, send_sem=send_sems.at[3 * t + k],
                                                   recv_sem=recv_sems.at[3 * t + k], device_id=sibling,
                                                   device_id_type=MESH)
                fwd.start()
                passed.append(fwd)
        for t in range(n):
            for k, (qx, qy) in enumerate(chips):
                slab = g_refs[t].at[2 * qx + qy, 1 - c]
                pltpu.make_async_remote_copy(src_ref=slab, dst_ref=slab, send_sem=send_sems.at[3 * t + k],
                                             recv_sem=recv_sems.at[3 * t + k], device_id=sibling,
                                             device_id_type=MESH).wait_recv()
        for cp in passed:
            cp.wait_send()

    return pl.pallas_call(
        body, name="gather_forward",
        in_specs=[ANY] * n, out_specs=[ANY] * n,
        out_shape=[jax.ShapeDtypeStruct(b.shape, b.dtype) for b in bufs],
        input_output_aliases={t: t for t in range(n)},
        scratch_shapes=[pltpu.SemaphoreType.DMA((3 * n,)), pltpu.SemaphoreType.DMA((3 * n,))],
    )(*bufs)


def forward_start(bufs):
    n = len(bufs)

    def body(*refs):
        g_refs = refs[:n]
        send_sems, recv_sems = refs[n], refs[n + 1]
        token = refs[-1]
        x, y, c = _place()
        for t in range(n):
            for k, (qx, qy) in enumerate(_other_chips(x, y)):
                slab = g_refs[t].at[2 * qx + qy, c]
                pltpu.make_async_remote_copy(src_ref=slab, dst_ref=slab, send_sem=send_sems.at[3 * t + k],
                                             recv_sem=recv_sems.at[3 * t + k], device_id=(x, y, 1 - c),
                                             device_id_type=MESH).start()
        token[...] = jnp.zeros_like(token)

    res = pl.pallas_call(
        body, name="forward_start",
        in_specs=[HBM] * n,
        out_specs=[SEM, SEM] + [HBM] * n + [pl.BlockSpec(memory_space=pltpu.VMEM)],
        out_shape=[pltpu.SemaphoreType.DMA((3 * n,)), pltpu.SemaphoreType.DMA((3 * n,))]
        + [pltpu.HBM(b.shape, b.dtype) for b in bufs] + [jax.ShapeDtypeStruct((8, LANES), F32)],
        input_output_aliases={t: 2 + t for t in range(n)},
        compiler_params=pltpu.CompilerParams(has_side_effects=EFFECT),
    )(*[_hbm(b) for b in bufs])
    return res[0], res[1], list(res[2:2 + n]), res[-1]


def forward_wait(bufs, send_sems, recv_sems, after):
    n = len(bufs)

    def body(*refs):
        g_refs = refs[:n]
        send_sems, recv_sems = refs[n], refs[n + 1]
        x, y, c = _place()
        for t in range(n):
            for k, (qx, qy) in enumerate(_other_chips(x, y)):
                cp = pltpu.make_async_remote_copy(src_ref=g_refs[t].at[2 * qx + qy, c],
                                                  dst_ref=g_refs[t].at[2 * qx + qy, 1 - c],
                                                  send_sem=send_sems.at[3 * t + k], recv_sem=recv_sems.at[3 * t + k],
                                                  device_id=(x, y, 1 - c), device_id_type=MESH)
                cp.wait_send()
                cp.wait_recv()

    return pl.pallas_call(
        body, name="forward_wait",
        in_specs=[HBM] * n + [SEM, SEM, ANY],
        out_specs=[HBM] * n,
        out_shape=[pltpu.HBM(b.shape, b.dtype) for b in bufs],
        input_output_aliases={t: t for t in range(n)},
        compiler_params=pltpu.CompilerParams(has_side_effects=EFFECT),
    )(*bufs, send_sems, recv_sems, after)


def small_start(pack, me, after):
    buf = lax.dynamic_update_slice(lax.empty((8,) + pack.shape, pack.dtype), pack[None], (me, 0, 0))

    def body(b_ref, after_ref, send_sems, recv_sems, thru, token):
        x, y, c = _place()
        slot = b_ref.at[4 * x + 2 * y + c]
        for k in range(1, 8):
            peer = (x ^ ((k >> 2) & 1), y ^ ((k >> 1) & 1), c ^ (k & 1))
            pltpu.make_async_remote_copy(src_ref=slot, dst_ref=slot, send_sem=send_sems.at[k - 1],
                                         recv_sem=recv_sems.at[k - 1], device_id=peer, device_id_type=MESH).start()
        token[...] = jnp.zeros_like(token)

    return pl.pallas_call(
        body, name="small_start",
        in_specs=[HBM, ANY],
        out_specs=[SEM, SEM, HBM, pl.BlockSpec(memory_space=pltpu.VMEM)],
        out_shape=[pltpu.SemaphoreType.DMA((7,)), pltpu.SemaphoreType.DMA((7,)), pltpu.HBM(buf.shape, buf.dtype),
                   jax.ShapeDtypeStruct((8, LANES), F32)],
        input_output_aliases={0: 2},
        compiler_params=pltpu.CompilerParams(has_side_effects=EFFECT),
    )(_hbm(buf), after)


def small_wait(buf, send_sems, recv_sems, after):
    def body(b_ref, send_sems, recv_sems, after_ref, thru):
        x, y, c = _place()
        mine = b_ref.at[4 * x + 2 * y + c]
        for k in range(1, 8):
            peer = (x ^ ((k >> 2) & 1), y ^ ((k >> 1) & 1), c ^ (k & 1))
            cp = pltpu.make_async_remote_copy(src_ref=mine, dst_ref=b_ref.at[4 * peer[0] + 2 * peer[1] + peer[2]],
                                              send_sem=send_sems.at[k - 1], recv_sem=recv_sems.at[k - 1],
                                              device_id=peer, device_id_type=MESH)
            cp.wait_send()
            cp.wait_recv()

    return pl.pallas_call(
        body, name="small_wait",
        in_specs=[HBM, SEM, SEM, ANY], out_specs=HBM,
        out_shape=pltpu.HBM(buf.shape, buf.dtype),
        input_output_aliases={0: 0},
        compiler_params=pltpu.CompilerParams(has_side_effects=EFFECT),
    )(buf, send_sems, recv_sems, after)


def swap_start(grads, after):
    n = len(grads)

    def body(*refs):
        g_refs, l_refs = refs[:n], refs[n:2 * n]
        send_sems, recv_sems = refs[2 * n + 1], refs[2 * n + 2]
        token = refs[-1]
        x, y, c = _place()
        for t in range(n):
            for p in range(N_CHIPS):
                pltpu.make_async_remote_copy(src_ref=g_refs[t].at[p, 1 - c], dst_ref=l_refs[t].at[p],
                                             send_sem=send_sems.at[N_CHIPS * t + p],
                                             recv_sem=recv_sems.at[N_CHIPS * t + p],
                                             device_id=(x, y, 1 - c), device_id_type=MESH).start()
        token[...] = jnp.zeros_like(token)

    lands = [lax.empty((N_CHIPS,) + g.shape[2:], g.dtype) for g in grads]
    res = pl.pallas_call(
        body, name="swap_start",
        in_specs=[HBM] * (2 * n) + [ANY],
        out_specs=[SEM, SEM] + [HBM] * (2 * n) + [pl.BlockSpec(memory_space=pltpu.VMEM)],
        out_shape=[pltpu.SemaphoreType.DMA((N_CHIPS * n,)), pltpu.SemaphoreType.DMA((N_CHIPS * n,))]
        + [pltpu.HBM(a.shape, a.dtype) for a in grads + lands] + [jax.ShapeDtypeStruct((8, LANES), F32)],
        input_output_aliases={t: 2 + t for t in range(2 * n)},
        compiler_params=pltpu.CompilerParams(has_side_effects=EFFECT),
    )(*[_hbm(a) for a in grads + lands], after)
    return res[0], res[1], list(res[2:2 + n]), list(res[2 + n:2 + 2 * n]), res[-1]


def swap_wait(grads, lands, send_sems, recv_sems, after):
    n = len(grads)

    def body(*refs):
        g_refs, l_refs = refs[:n], refs[n:2 * n]
        send_sems, recv_sems = refs[2 * n], refs[2 * n + 1]
        x, y, c = _place()
        for t in range(n):
            for p in range(N_CHIPS):
                cp = pltpu.make_async_remote_copy(src_ref=g_refs[t].at[p, 1 - c], dst_ref=l_refs[t].at[p],
                                                  send_sem=send_sems.at[N_CHIPS * t + p],
                                                  recv_sem=recv_sems.at[N_CHIPS * t + p],
                                                  device_id=(x, y, 1 - c), device_id_type=MESH)
                cp.wait_send()
                cp.wait_recv()

    res = pl.pallas_call(
        body, name="swap_wait",
        in_specs=[HBM] * (2 * n) + [SEM, SEM, ANY],
        out_specs=[HBM] * (2 * n),
        out_shape=[pltpu.HBM(a.shape, a.dtype) for a in grads + lands],
        input_output_aliases={t: t for t in range(2 * n)},
        compiler_params=pltpu.CompilerParams(has_side_effects=EFFECT),
    )(*grads, *lands, send_sems, recv_sems, after)
    return list(res[:n]), list(res[n:])


def _row_tile(r):
    return max(t for t in range(16, 513, 16) if r % t == 0)


def add_own_half(g, other, c_arr):
    _, _, r, cols = g.shape
    tr = _row_tile(r)

    def body(c_ref, a_ref, b_ref, o_ref):
        o_ref[...] = (a_ref[...] + b_ref[...]).astype(BF16)

    return pl.pallas_call(
        body, name="add_own_half",
        grid_spec=pltpu.PrefetchScalarGridSpec(
            num_scalar_prefetch=1, grid=(N_CHIPS, r // tr),
            in_specs=[pl.BlockSpec((None, None, tr, cols), lambda p, i, c_ref: (p, c_ref[0], i, 0)),
                      pl.BlockSpec((None, tr, cols), lambda p, i, c_ref: (p, i, 0))],
            out_specs=pl.BlockSpec((None, tr, cols), lambda p, i, c_ref: (p, i, 0))),
        out_shape=jax.ShapeDtypeStruct((N_CHIPS, r, cols), BF16),
        compiler_params=_cparams("parallel", "parallel"),
    )(c_arr, g, other)


def scatter_start(partials):
    n = len(partials)

    def body(*refs):
        s_refs, l_refs = refs[:n], refs[n:2 * n]
        send_sems, recv_sems = refs[2 * n], refs[2 * n + 1]
        token = refs[-1]
        x, y, c = _place()
        for t in range(n):
            for k, (qx, qy) in enumerate(_other_chips(x, y)):
                pltpu.make_async_remote_copy(src_ref=s_refs[t].at[2 * qx + qy], dst_ref=l_refs[t].at[k],
                                             send_sem=send_sems.at[3 * t + k], recv_sem=recv_sems.at[3 * t + k],
                                             device_id=(qx, qy, c), device_id_type=MESH).start()
        token[...] = jnp.zeros_like(token)

    lands = [lax.empty((3,) + s.shape[1:], s.dtype) for s in partials]
    res = pl.pallas_call(
        body, name="scatter_start",
        in_specs=[HBM] * (2 * n),
        out_specs=[SEM, SEM] + [HBM] * (2 * n) + [pl.BlockSpec(memory_space=pltpu.VMEM)],
        out_shape=[pltpu.SemaphoreType.DMA((3 * n,)), pltpu.SemaphoreType.DMA((3 * n,))]
        + [pltpu.HBM(a.shape, a.dtype) for a in partials + lands] + [jax.ShapeDtypeStruct((8, LANES), F32)],
        input_output_aliases={t: 2 + t for t in range(2 * n)},
        compiler_params=pltpu.CompilerParams(has_side_effects=EFFECT),
    )(*[_hbm(a) for a in partials + lands])
    return res[0], res[1], list(res[2:2 + n]), list(res[2 + n:2 + 2 * n]), res[-1]


def scatter_wait(partials, lands, send_sems, recv_sems, after):
    n = len(partials)

    def body(*refs):
        s_refs, l_refs = refs[:n], refs[n:2 * n]
        send_sems, recv_sems = refs[2 * n], refs[2 * n + 1]
        x, y, c = _place()
        for t in range(n):
            for k, (qx, qy) in enumerate(_other_chips(x, y)):
                cp = pltpu.make_async_remote_copy(src_ref=s_refs[t].at[2 * qx + qy], dst_ref=l_refs[t].at[k],
                                                  send_sem=send_sems.at[3 * t + k], recv_sem=recv_sems.at[3 * t + k],
                                                  device_id=(qx, qy, c), device_id_type=MESH)
                cp.wait_send()
                cp.wait_recv()

    res = pl.pallas_call(
        body, name="scatter_wait",
        in_specs=[HBM] * (2 * n) + [SEM, SEM, ANY],
        out_specs=[HBM] * (2 * n),
        out_shape=[pltpu.HBM(a.shape, a.dtype) for a in partials + lands],
        input_output_aliases={t: t for t in range(2 * n)},
        compiler_params=pltpu.CompilerParams(has_side_effects=EFFECT),
    )(*partials, *lands, send_sems, recv_sems, after)
    return list(res[:n]), list(res[n:])


def sum_chips(own, parts, where):
    _, r, cols = own.shape
    tr = _row_tile(r)

    def body(w_ref, a_ref, p_ref, o_ref):
        acc = a_ref[...].astype(F32)
        for k in range(3):
            acc = acc + p_ref[k].astype(F32)
        o_ref[...] = acc

    return pl.pallas_call(
        body, name="sum_chips",
        grid_spec=pltpu.PrefetchScalarGridSpec(
            num_scalar_prefetch=1, grid=(r // tr,),
            in_specs=[pl.BlockSpec((None, tr, cols), lambda i, w_ref: (w_ref[0], i, 0)),
                      pl.BlockSpec((3, tr, cols), lambda i, w_ref: (0, i, 0))],
            out_specs=pl.BlockSpec((None, tr, cols), lambda i, w_ref: (w_ref[1], i, 0))),
        out_shape=jax.ShapeDtypeStruct((DEPTH, r, cols), F32),
        compiler_params=_cparams("parallel"),
    )(where, own, parts)


def sibling_share_layer(bufs):
    n = len(bufs)

    def body(*refs):
        o_refs = refs[n:2 * n]
        send_sems, recv_sems = refs[2 * n:]
        x, y, c = _place()
        cps = []
        for t in range(n):
            cp = pltpu.make_async_remote_copy(src_ref=o_refs[t].at[c], dst_ref=o_refs[t].at[c], send_sem=send_sems.at[t],
                                              recv_sem=recv_sems.at[t], device_id=(x, y, 1 - c), device_id_type=MESH)
            cp.start()
            cps.append(cp)
        for t in range(n):
            slot = o_refs[t].at[1 - c]
            pltpu.make_async_remote_copy(src_ref=slot, dst_ref=slot, send_sem=send_sems.at[t], recv_sem=recv_sems.at[t],
                                         device_id=(x, y, 1 - c), device_id_type=MESH).wait_recv()
        for cp in cps:
            cp.wait_send()

    return pl.pallas_call(
        body, name="sibling_share_layer",
        in_specs=[ANY] * n, out_specs=[ANY] * n,
        out_shape=[jax.ShapeDtypeStruct(b.shape, b.dtype) for b in bufs],
        input_output_aliases={t: t for t in range(n)},
        scratch_shapes=[pltpu.SemaphoreType.DMA((n,)), pltpu.SemaphoreType.DMA((n,))],
    )(*bufs)


SP_META = 2 * (N_META * D_MODEL // LANES)
SP_NORM = DEPTH * D_MODEL // LANES
SP_RB = DEPTH * N_BUCKETS
SP_SINK = DEPTH * ATT_HEADS
SP_CONV = DEPTH * 3 * BRANCH_WIDTH // LANES
SP_LOSS = 8
SIDE_ROWS = 48
SP_ROWS = SP_META + 2 * SP_NORM + SP_RB + SP_SINK + SP_CONV + SP_LOSS


def sum_small(slots):
    half = SP_META // 2
    rb0 = SP_META + 2 * SP_NORM
    rest_rows = SP_ROWS - SP_META

    def body(s_ref, meta_ref, rest_ref):
        acc = s_ref[0]
        for d in range(1, 8):
            acc = acc + s_ref[d]
        meta_ref[...] = acc[0:half] + acc[half:SP_META]
        rest_ref[...] = acc[SP_META:]
        rest_ref[rb0 - SP_META:rb0 - SP_META + N_BUCKETS, :] = (
            acc[rb0:rb0 + N_BUCKETS] + acc[rb0 + N_BUCKETS:rb0 + 2 * N_BUCKETS])

    vm = pl.BlockSpec(memory_space=pltpu.VMEM)
    return pl.pallas_call(
        body, name="sum_small",
        in_specs=[vm], out_specs=[vm, vm],
        out_shape=[jax.ShapeDtypeStruct((half, LANES), F32), jax.ShapeDtypeStruct((rest_rows, LANES), F32)],
    )(slots)


def local_step(x, loss_target, meta_full, rel_bias, norm_pre, conv_w_full, attn_sinks, norm_post, weights_of, mid_fwd,
               grads_done, bwd_done):
    nb, seq, _ = x.shape
    nc = seq // BLOCK + 1
    lp = nc * BLOCK
    rows = nb * lp
    pad = jnp.zeros((nb, PAD_FRONT, D_MODEL), F32)
    meta = jnp.broadcast_to(meta_full[None], (nb, N_META, D_MODEL))
    h0 = jnp.concatenate([pad, meta, x], axis=1).reshape(rows, D_MODEL)
    target = jnp.pad(loss_target, ((0, 0), (BLOCK, 0), (0, 0))).reshape(rows, D_MODEL)
    cosf, sinf = _rot_tables(lp)
    bkt = jnp.asarray(_bucket_table())

    g_pre = norm_pre.reshape(DEPTH, 1, D_MODEL)
    g_post = norm_post.reshape(DEPTH, 1, D_MODEL)
    order = lambda token: bkt if token is None else token

    acts = []
    h = h0
    for l in range(DEPTH):
        w_in, token = weights_of(l, h)
        hb, p_abc = norm_matmul(h, g_pre, l, w_in, 0, N_ABC_TILES, order(token))
        p_m = matmul_cols(hb, w_in, N_ABC_TILES, N_M_TILES)
        br, states = mixers_fwd(p_abc, cosf, sinf, bkt, rel_bias, attn_sinks, conv_w_full, l, nb, nc)
        (w_br, w_out), token = mid_fwd(l, br)
        acts.append((h, hb, p_abc, p_m, br, states, w_in, w_br, w_out))
        if l < DEPTH - 1:
            h = merge_fwd(h, br, p_m, w_br, w_out, g_post, l, order(token))
        else:
            assert token is None
            loss_part, d_h = merge_fwd_loss(h, br, p_m, w_br, w_out, g_post, l, target, lp)

    small = [None] * DEPTH
    token = None
    for l in reversed(range(DEPTH)):
        h_in, hb, p_abc, p_m, br, states, w_in, w_br, w_out = acts[l]
        d_br, d_m, d_gpost, g_wbr, g_wout = merge_bwd(d_h, br, p_m, w_br, w_out, g_post, l, order(token))
        d_abc, d_rb, d_sk, d_cw = mixers_bwd(p_abc, d_br, states, cosf, sinf, bkt, rel_bias,
                                             attn_sinks, conv_w_full, l, nb, nc)
        g_win = proj_wgrad(hb, d_abc, d_m)
        token = grads_done(l, [g_win, g_wbr, g_wout])
        d_h, d_gpre, *d_x = proj_dgrad(d_abc, d_m, w_in, h_in, g_pre, l, d_h, order(token),
                                       examples=nb if l == 0 else None)
        token = bwd_done(l, d_h)
        small[l] = (d_gpre[0], d_gpost[0], d_rb, d_sk, d_cw[0:3])

    d_h3 = d_h.reshape(nb, lp, D_MODEL)
    d_x, = d_x
    d_meta = d_h3[:, PAD_FRONT:BLOCK]
    sp = jnp.concatenate([
        d_meta.reshape(-1, LANES),
        jnp.stack([small[l][0] for l in range(DEPTH)]).reshape(-1, LANES),
        jnp.stack([small[l][1] for l in range(DEPTH)]).reshape(-1, LANES),
        jnp.concatenate([small[l][2] for l in range(DEPTH)], axis=0),
        jnp.concatenate([small[l][3] for l in range(DEPTH)], axis=0),
        jnp.stack([small[l][4] for l in range(DEPTH)]).reshape(-1, LANES),
        loss_part], axis=0)
    return d_x, sp


def kernel(x, meta_tokens, rel_bias, norm_pre, w_in, conv_w, attn_sinks, w_branch, w_out, norm_post, loss_target, m_meta_tokens, m_rel_bias, m_norm_pre, m_w_in, m_conv_w, m_attn_sinks, m_w_branch, m_w_out, m_norm_post, v_meta_tokens, v_rel_bias, v_norm_pre, v_w_in, v_conv_w, v_attn_sinks, v_w_branch, v_w_out, v_norm_post):
    assert x.shape[0] == 2 and SP_META == 2 * N_META * D_MODEL // LANES
    px, py, pc = _place()
    chip = 2 * px + py

    c_arr = jnp.reshape(pc, (1,)).astype(jnp.int32)
    where = jnp.stack([chip, pc]).astype(jnp.int32)
    tr_ = lambda a: jnp.swapaxes(a, 1, 2)
    w3 = [tr_(w_in), w_branch.reshape(DEPTH, N_BRANCH * BRANCH_WIDTH, SHARD_D), w_out]
    halves = lambda a: a.reshape(2, a.shape[0] // 2, a.shape[1])

    def as_weights(bufs):
        a_in, a_br, a_out = bufs
        return (a_in.reshape(PROJ_WIDTH, D_MODEL), a_br.reshape(N_CHIPS, N_BRANCH, BRANCH_WIDTH, SHARD_D),
                a_out.reshape(D_MODEL, D_MODEL))

    n_meta_rows = N_META * SHARD_D // LANES
    side = jnp.concatenate([meta_tokens.reshape(-1), conv_w.reshape(-1)]).reshape(-1, LANES)
    side = jnp.concatenate([side, jnp.zeros((SIDE_ROWS - side.shape[0], LANES), F32)], axis=0)
    slots = [[_own_slot(halves(w[l].astype(BF16)), chip) for w in w3] for l in range(DEPTH)]
    send0, recv0, flying0, _ = gather_start([_own_slot(halves(side), chip)] + slots[0], where)
    side_chips = gather_forward(gather_wait(flying0[:1], send0, recv0, where))[0].reshape(N_CHIPS, SIDE_ROWS, LANES)
    meta_full = jnp.moveaxis(side_chips[:, :n_meta_rows].reshape(N_CHIPS, N_META, SHARD_D), 0, 1).reshape(N_META, D_MODEL)
    conv_full = jnp.moveaxis(side_chips[:, n_meta_rows:n_meta_rows + 6].reshape(N_CHIPS, DEPTH, 3, LANES), 0, 2).reshape(DEPTH, 3, BRANCH_WIDTH)
    inbound = {}

    def weights_of(l, h):
        if l == 0:
            inbound[0] = gather_forward(gather_wait(flying0[1:2], send0, recv0, h, first=1))
            inbound[1] = gather_start(slots[1], inbound[0][0])
            return inbound[0][0].reshape(PROJ_WIDTH, D_MODEL), inbound[1][3]
        send, recv, thru = inbound[1]
        inbound[1] = as_weights(forward_wait(thru, send, recv, h))
        return inbound[1][0], None

    def mid_fwd(l, br):
        if l == 0:
            rest = gather_forward(gather_wait(flying0[2:], send0, recv0, br, first=2))
            send, recv, flying1, _ = inbound[1]
            send, recv, thru, started = forward_start(gather_wait(flying1, send, recv, rest[0]))
            inbound[1] = (send, recv, thru)
            return as_weights(inbound[0] + rest)[1:], started
        return inbound[1][1:], None

    reduced = [None] * DEPTH
    flying = {}

    def finish_reduce(l, after):
        partials, parts = scatter_wait(*flying[l], after)
        reduced[l] = sibling_share_layer([sum_chips(a, p, where) for a, p in zip(partials, parts)])

    def start_scatter(l, full, others):
        send, recv, thru, lands, started = scatter_start([add_own_half(g, o, c_arr) for g, o in zip(full, others)])
        flying[l] = (thru, lands, send, recv)
        return started

    m3 = [tr_(m_w_in), m_w_branch.reshape(w3[1].shape), m_w_out]
    v3 = [tr_(v_w_in), v_w_branch.reshape(w3[1].shape), v_w_out]
    big = [None] * 3

    def adamw_of(l, after):
        for t in range(3):
            big[t] = adamw_layer(w3[t], reduced[l][t].reshape(w3[t].shape[1:]), m3[t], v3[t], l, big[t], after)
            after = big[t][1]

    def grads_done(l, grads):
        full = [g.reshape(N_CHIPS, 2, g.size // (2 * N_CHIPS * g.shape[-1]), g.shape[-1]) for g in grads]
        if l == 0:
            finish_reduce(1, grads[0])
        send, recv, thru, lands, started = swap_start(full, where if l == 1 else reduced[1][0])
        if l == 1:
            flying["swap"] = (thru, lands, send, recv)
            return started
        adamw_of(1, started)
        return start_scatter(0, *swap_wait(thru, lands, send, recv, big[2][1]))

    def bwd_done(l, d_h):
        if l == 1:
            return start_scatter(1, *swap_wait(*flying["swap"], d_h))
        return None

    d_x, sp = local_step(x, loss_target, meta_full, rel_bias, norm_pre, conv_full, attn_sinks, norm_post,
                         weights_of, mid_fwd, grads_done, bwd_done)
    finish_reduce(0, sp)

    s_send, s_recv, s_buf, s_started = small_start(sp, 4 * px + 2 * py + pc, reduced[0][0])

    adamw_of(0, s_started)
    g_in, *u_in = [tr_(a) for a in big[0]]
    g_br, *u_br = [a.reshape(w_branch.shape) for a in big[1]]
    g_out, *u_out = big[2]

    meta_rows, rest = sum_small(small_wait(s_buf, s_send, s_recv, big[2][1]))
    o = 0
    g_meta_full = meta_rows.reshape(N_META, D_MODEL)
    g_norm_pre = rest[o:o + SP_NORM].reshape(DEPTH, D_MODEL); o += SP_NORM
    g_norm_post = rest[o:o + SP_NORM].reshape(DEPTH, D_MODEL); o += SP_NORM
    g_rel_bias = rest[o:o + N_BUCKETS, :ATT_HEADS]; o += SP_RB
    g_sinks = rest[o:o + SP_SINK, 0].reshape(DEPTH, ATT_HEADS); o += SP_SINK
    g_conv_full = rest[o:o + SP_CONV].reshape(DEPTH, 3, BRANCH_WIDTH); o += SP_CONV
    loss = rest[o, 0]
    g_meta = lax.dynamic_slice_in_dim(g_meta_full, chip * SHARD_D, SHARD_D, axis=1)
    g_conv = lax.dynamic_slice_in_dim(g_conv_full, chip * LANES, LANES, axis=2)

    swap01 = lambda arrays: tuple(jnp.swapaxes(a, 0, 1) for a in arrays)
    smalls = [(meta_tokens, g_meta, m_meta_tokens, v_meta_tokens),
              swap01((rel_bias, g_rel_bias, m_rel_bias, v_rel_bias)),
              (norm_pre, g_norm_pre, m_norm_pre, v_norm_pre),
              swap01((conv_w, g_conv, m_conv_w, v_conv_w)),
              (attn_sinks, g_sinks, m_attn_sinks, v_attn_sinks),
              (norm_post, g_norm_post, m_norm_post, v_norm_post)]
    u_meta, u_rb, u_npre, u_conv, u_sink, u_npost = adamw_small(smalls)
    u_rb, u_conv = swap01(u_rb), swap01(u_conv)

    grads = [g_meta, g_rel_bias, g_norm_pre, g_in, g_conv, g_sinks, g_br, g_out, g_norm_post]
    upd = [u_meta, u_rb, u_npre, u_in, u_conv, u_sink, u_br, u_out, u_npost]
    return (loss, d_x, *grads, *[u[0] for u in upd], *[u[1] for u in upd], *[u[2] for u in upd])
```

```python
import math

import numpy as np
import jax
import jax.numpy as jnp
from jax import lax
from jax.experimental import pallas as pl
from jax.experimental.pallas import tpu as pltpu

F32 = jnp.float32
BF16 = jnp.bfloat16
MESH = pl.DeviceIdType.MESH

D_MODEL = 1024
DEPTH = 2
N_META = 16
BLOCK = 128
PAD_FRONT = BLOCK - N_META
ATT_HEADS = 8
ATT_HEAD_DIM = 64
N_BUCKETS = 32
MAX_EXACT = 16
MAX_DISTANCE = 128
RET_HEADS = 4
ROT_BASE = 10000.0
N_BRANCH = 3
BRANCH_WIDTH = 512
PROJ_WIDTH = 8448
ABC_WIDTH = 5376
MERGE_WIDTH = N_BRANCH * D_MODEL
RMS_EPS = 1e-6
GN_EPS = 1e-6
NEG_INF = -1e30
ATT_SCALE = ATT_HEAD_DIM ** -0.5
RET_SCALE = BLOCK ** -0.5
LOG_GAMMA = tuple(math.log1p(-(2.0 ** (-5.0 - h))) for h in range(RET_HEADS))

C_AQ, C_AK, C_AV, C_AG = 0, 512, 640, 768
C_RQ, C_RK, C_RV, C_RG = 1280, 1792, 2304, 2816
C_CB, C_CC, C_CX, C_CG = 3328, 3840, 4352, 4864

ADAM_LR = 0.001
ADAM_B1 = 0.9
ADAM_B2 = 0.999
ADAM_EPS = 1e-08
ADAM_WD = 0.01
ADAM_STEP = 10

N_CHIPS = 4
SHARD_D = D_MODEL // N_CHIPS
LANES = 128

VMEM_LIMIT = 56 * 1024 * 1024
COL_TILE = 768
ROW_TILE = 1088
PROJ_ROW_TILE = 2176


def _cparams(*sem):
    return pltpu.CompilerParams(dimension_semantics=sem, vmem_limit_bytes=VMEM_LIMIT)


def _nt(a, b):
    return lax.dot_general(a, b, (((1,), (1,)), ((), ())), preferred_element_type=F32)


def _tn(a, b):
    return lax.dot_general(a, b, (((0,), (0,)), ((), ())), preferred_element_type=F32)


def _nn(a, b):
    return jnp.dot(a, b, preferred_element_type=F32)


def _sigmoid(x):
    return 0.5 * jnp.tanh(0.5 * x) + 0.5


def _silu(x):
    return x * _sigmoid(x)


def _dsilu(x):
    s = _sigmoid(x)
    return s * (1.0 + x * (1.0 - s))


def _bucket_table():
    r = np.arange(BLOCK)[:, None]
    c = np.arange(2 * BLOCK)[None, :]
    n = np.maximum(BLOCK + r - c, 0)
    nf = np.maximum(n, 1).astype(np.float32)
    large = MAX_EXACT + (np.log(nf / MAX_EXACT) / math.log(MAX_DISTANCE / MAX_EXACT)
                         * (N_BUCKETS - MAX_EXACT)).astype(np.int32)
    large = np.minimum(large, N_BUCKETS - 1)
    return np.where(n < MAX_EXACT, n, large).astype(np.int32)


def _rot_tables(lp):
    half = BLOCK // 2
    pos = (jnp.arange(lp) - PAD_FRONT).astype(F32)
    theta = 1.0 / (ROT_BASE ** jnp.linspace(0.0, 1.0, half, dtype=F32))
    ang = pos[:, None] * theta[None, :]
    cos, sin = jnp.cos(ang), jnp.sin(ang)
    return jnp.concatenate([cos, cos], axis=1), jnp.concatenate([-sin, sin], axis=1)


def norm_matmul(x2d, g, layer, w, col0_blocks, n_col_blocks, after):
    t = x2d.shape[0]
    tm = PROJ_ROW_TILE if t % PROJ_ROW_TILE == 0 else BLOCK

    def body(x_ref, g_ref, w_ref, after_ref, hb_ref, o_ref):
        @pl.when(pl.program_id(1) == 0)
        def _():
            x = x_ref[...]
            r = lax.rsqrt(jnp.mean(x * x, axis=-1, keepdims=True) + RMS_EPS)
            hb_ref[...] = (x * r * g_ref[...]).astype(BF16)

        o_ref[...] = _nt(hb_ref[...], w_ref[...])

    return pl.pallas_call(
        body, name="norm_matmul",
        grid=(t // tm, n_col_blocks),
        in_specs=[pl.BlockSpec((tm, D_MODEL), lambda i, j: (i, 0)),
                  pl.BlockSpec((None, 1, D_MODEL), lambda i, j: (layer, 0, 0)),
                  pl.BlockSpec((COL_TILE, D_MODEL), lambda i, j: (j + col0_blocks, 0)),
                  ANY],
        out_specs=[pl.BlockSpec((tm, D_MODEL), lambda i, j: (i, 0)),
                   pl.BlockSpec((tm, COL_TILE), lambda i, j: (i, j))],
        out_shape=[jax.ShapeDtypeStruct((t, D_MODEL), BF16),
                   jax.ShapeDtypeStruct((t, n_col_blocks * COL_TILE), F32)],
        compiler_params=_cparams("parallel", "arbitrary"),
    )(x2d, g, w, after)


def matmul_cols(a, w, col0_blocks, n_col_blocks):
    t, k = a.shape
    tm = PROJ_ROW_TILE if t % PROJ_ROW_TILE == 0 else BLOCK

    def body(a_ref, w_ref, o_ref):
        o_ref[...] = _nt(a_ref[...], w_ref[...]).astype(BF16)

    return pl.pallas_call(
        body, name="matmul_cols",
        grid=(t // tm, n_col_blocks),
        in_specs=[pl.BlockSpec((tm, k), lambda i, j: (i, 0)),
                  pl.BlockSpec((COL_TILE, k), lambda i, j: (j + col0_blocks, 0))],
        out_specs=pl.BlockSpec((tm, COL_TILE), lambda i, j: (i, j)),
        out_shape=jax.ShapeDtypeStruct((t, n_col_blocks * COL_TILE), BF16),
        compiler_params=_cparams("parallel", "arbitrary"),
    )(a, w)


class _Widened:
    def __init__(self, ref):
        self.ref = ref

    def __getitem__(self, idx):
        return self.ref[idx].astype(F32)


def _build_bias(bkt_ref, rb_ref, bias_s):
    bkt = bkt_ref[...]
    for h in range(ATT_HEADS):
        acc = jnp.zeros((BLOCK, 2 * BLOCK), F32)
        for b in range(N_BUCKETS):
            acc = jnp.where(bkt == b, rb_ref[b, h], acc)
        bias_s[h] = acc


def _band_mask(n):
    r = lax.broadcasted_iota(jnp.int32, (BLOCK, 2 * BLOCK), 0)
    c = lax.broadcasted_iota(jnp.int32, (BLOCK, 2 * BLOCK), 1)
    key_pos = (n - 1) * BLOCK + c
    return (c > r) & (c <= r + BLOCK) & (key_pos >= PAD_FRONT)


def _split_heads(kv, kh):
    lane = lax.broadcasted_iota(jnp.int32, kv.shape, 1)
    if kh == 0:
        lo = jnp.where(lane < ATT_HEAD_DIM, kv, 0.0)
        hi = pltpu.roll(lo, ATT_HEAD_DIM, 1)
    else:
        hi = jnp.where(lane >= ATT_HEAD_DIM, kv, 0.0)
        lo = pltpu.roll(hi, ATT_HEAD_DIM, 1)
    return lo, hi


def _merge_heads(acc_lo, acc_hi, kh):
    lane = lax.broadcasted_iota(jnp.int32, acc_lo.shape, 1)
    if kh == 0:
        return jnp.where(lane < ATT_HEAD_DIM, acc_lo + pltpu.roll(acc_hi, ATT_HEAD_DIM, 1), 0.0)
    return jnp.where(lane >= ATT_HEAD_DIM, acc_hi + pltpu.roll(acc_lo, ATT_HEAD_DIM, 1), 0.0)


def _softmax_of(qk, bias_h, mask, sink_h):
    s = qk + bias_h
    s = jnp.where(mask, s, NEG_INF)
    m = jnp.maximum(jnp.max(s, axis=-1, keepdims=True), sink_h)
    p = jnp.exp(s - m)
    es = jnp.exp(sink_h - m)
    inv = 1.0 / (jnp.sum(p, axis=-1, keepdims=True) + es)
    return p * inv, es * inv


def _rot(t, cosf, sinf):
    return t * cosf + pltpu.roll(t, BLOCK // 2, 1) * sinf


def _rot_t(d, cosf, sinf):
    return d * cosf + pltpu.roll(d * sinf, BLOCK // 2, 1)


def _decay_tables(h):
    lg = LOG_GAMMA[h]
    i = lax.broadcasted_iota(jnp.int32, (BLOCK, BLOCK), 0)
    j = lax.broadcasted_iota(jnp.int32, (BLOCK, BLOCK), 1)
    diff = (i - j).astype(F32)
    dm = jnp.where(diff >= 0, jnp.exp(diff * lg), 0.0)
    row = lax.broadcasted_iota(jnp.int32, (BLOCK, 1), 0).astype(F32)
    zeta = jnp.exp((BLOCK - 1 - row) * lg)
    xi = jnp.exp((row + 1.0) * lg)
    return dm, zeta, xi, math.exp(BLOCK * lg)


def _valid_col(n):
    row = lax.broadcasted_iota(jnp.int32, (BLOCK, 1), 0)
    return ((n * BLOCK + row) >= PAD_FRONT).astype(F32)


def _shift_down(cur, prev, k):
    row = lax.broadcasted_iota(jnp.int32, cur.shape, 0)
    return jnp.where(row >= k, pltpu.roll(cur, k, 0), pltpu.roll(prev, k, 0))


def _shift_up(cur, nxt, k):
    row = lax.broadcasted_iota(jnp.int32, cur.shape, 0)
    return jnp.where(row < BLOCK - k, pltpu.roll(cur, BLOCK - k, 0), pltpu.roll(nxt, BLOCK - k, 0))


def mixers_fwd(proj, cosf, sinf, bkt, rel_bias, sinks, conv_w, layer, nb, nc):
    def body(p_ref, cos_ref, sin_ref, bkt_ref, rb_ref, sk_ref, cw_ref, br_ref, st_ref,
             bias_s, kv_s, state_s, u_s):
        p_ref = _Widened(p_ref)
        n = pl.program_id(0)

        @pl.when(n == 0)
        def _():
            _build_bias(bkt_ref, rb_ref, bias_s)
            kv_s[:, 0:BLOCK, :] = jnp.zeros((nb, BLOCK, 2 * BLOCK), F32)
            state_s[...] = jnp.zeros_like(state_s)
            u_s[...] = jnp.zeros_like(u_s)

        valid = _valid_col(n)
        mask = _band_mask(n)
        ex = range(nb)

        for b in ex:
            kv_s[b, BLOCK:2 * BLOCK, :] = p_ref[b, :, C_AK:C_AK + 2 * BLOCK]
        for kh in range(2):
            ks = [[t.astype(BF16) for t in _split_heads(kv_s[b, :, 0:BLOCK], kh)] for b in ex]
            vs = [[t.astype(BF16) for t in _split_heads(kv_s[b, :, BLOCK:2 * BLOCK], kh)] for b in ex]
            pairs = [(b, 2 * kh + jj) for jj in range(2) for b in ex]
            subs = [(b, j, x) for (b, j) in pairs for x in range(2)]
            qb_ = {(b, j): (p_ref[b, :, C_AQ + BLOCK * j:C_AQ + BLOCK * (j + 1)] * ATT_SCALE).astype(BF16)
                   for (b, j) in pairs}
            qk_ = {(b, j, x): _nt(qb_[(b, j)], ks[b][x]) for (b, j, x) in subs}
            pb_ = {}
            for u in subs:
                h = 2 * u[1] + u[2]
                pb_[u] = _softmax_of(qk_[u], bias_s[h], mask, sk_ref[layer, h])[0].astype(BF16)
            o_ = {u: _nn(pb_[u], vs[u[0]][u[2]]) for u in subs}
            for (b, j) in pairs:
                gate = p_ref[b, :, C_AG + BLOCK * j:C_AG + BLOCK * (j + 1)]
                br_ref[b, :, BLOCK * j:BLOCK * (j + 1)] = ((o_[(b, j, 0)] + o_[(b, j, 1)]) * _silu(gate)).astype(BF16)
        for b in ex:
            kv_s[b, 0:BLOCK, :] = kv_s[b, BLOCK:2 * BLOCK, :]

        cosv = cos_ref[...]
        sinv = sin_ref[...]
        tabs = [_decay_tables(h) for h in range(RET_HEADS)]
        units = [(b, h) for h in range(RET_HEADS) for b in ex]
        sl = lambda c0, h: slice(c0 + BLOCK * h, c0 + BLOCK * (h + 1))
        q_, k_, v_, sp_ = {}, {}, {}, {}
        for u in units:
            b, h = u
            q_[u] = _rot(p_ref[b, :, sl(C_RQ, h)], cosv, sinv).astype(BF16)
            k_[u] = (_rot(p_ref[b, :, sl(C_RK, h)], cosv, sinv) * RET_SCALE * valid).astype(BF16)
            v_[u] = p_ref[b, :, sl(C_RV, h)]
            sp_[u] = state_s[b, h]
            st_ref[b, 0, h] = sp_[u]
        qk_ = {u: _nt(q_[u], k_[u]) for u in units}
        qs_ = {u: _nn(q_[u], sp_[u].astype(BF16)) for u in units}
        kv_ = {u: _tn(k_[u], (v_[u] * tabs[u[1]][1]).astype(BF16)) for u in units}
        a_ = {u: (qk_[u] * tabs[u[1]][0]).astype(BF16) for u in units}
        av_ = {u: _nn(a_[u], v_[u].astype(BF16)) for u in units}
        for u in units:
            b, h = u
            o = av_[u] + tabs[h][2] * qs_[u]
            mu = jnp.mean(o, axis=-1, keepdims=True)
            var = jnp.mean(jnp.square(o - mu), axis=-1, keepdims=True)
            oh = (o - mu) * lax.rsqrt(var + GN_EPS)
            gate = p_ref[b, :, sl(C_RG, h)]
            br_ref[b, :, BRANCH_WIDTH + BLOCK * h:BRANCH_WIDTH + BLOCK * (h + 1)] = (oh * _silu(gate)).astype(BF16)
            state_s[b, h] = tabs[h][3] * sp_[u] + kv_[u]

        for b in ex:
            u = p_ref[b, :, C_CC:C_CC + BRANCH_WIDTH] * p_ref[b, :, C_CX:C_CX + BRANCH_WIDTH] * valid
            u_prev = u_s[b]
            y = (cw_ref[0:1, :] * _shift_down(u, u_prev, 2) + cw_ref[1:2, :] * _shift_down(u, u_prev, 1)
                 + cw_ref[2:3, :] * u)
            yc = p_ref[b, :, C_CB:C_CB + BRANCH_WIDTH] * y * _silu(p_ref[b, :, C_CG:C_CG + BRANCH_WIDTH])
            br_ref[b, :, 2 * BRANCH_WIDTH:3 * BRANCH_WIDTH] = yc.astype(BF16)
            u_s[b] = u

    lp = nc * BLOCK
    smem = pl.BlockSpec(memory_space=pltpu.SMEM)
    br, states = pl.pallas_call(
        body, name="mixers_fwd",
        grid=(nc,),
        in_specs=[pl.BlockSpec((nb, BLOCK, ABC_WIDTH), lambda n: (0, n, 0)),
                  pl.BlockSpec((BLOCK, BLOCK), lambda n: (n, 0)),
                  pl.BlockSpec((BLOCK, BLOCK), lambda n: (n, 0)),
                  pl.BlockSpec((BLOCK, 2 * BLOCK), lambda n: (0, 0)),
                  smem, smem,
                  pl.BlockSpec((None, 3, BRANCH_WIDTH), lambda n: (layer, 0, 0))],
        out_specs=[pl.BlockSpec((nb, BLOCK, N_BRANCH * BRANCH_WIDTH), lambda n: (0, n, 0)),
                   pl.BlockSpec((nb, 1, RET_HEADS, BLOCK, BLOCK), lambda n: (0, n, 0, 0, 0))],
        out_shape=[jax.ShapeDtypeStruct((nb, lp, N_BRANCH * BRANCH_WIDTH), BF16),
                   jax.ShapeDtypeStruct((nb, nc, RET_HEADS, BLOCK, BLOCK), F32)],
        scratch_shapes=[pltpu.VMEM((ATT_HEADS, BLOCK, 2 * BLOCK), F32),
                        pltpu.VMEM((nb, 2 * BLOCK, 2 * BLOCK), F32),
                        pltpu.VMEM((nb, RET_HEADS, BLOCK, BLOCK), F32),
                        pltpu.VMEM((nb, BLOCK, BRANCH_WIDTH), F32)],
        compiler_params=_cparams("arbitrary"),
    )(proj.reshape(nb, lp, ABC_WIDTH), cosf, sinf, bkt, rel_bias, sinks, conv_w)
    return br.reshape(nb * lp, N_BRANCH * BRANCH_WIDTH), states


def mixers_bwd(proj, d_br, states, cosf, sinf, bkt, rel_bias, sinks, conv_w, layer, nb, nc):
    def body(p_ref, kvp_ref, cp_ref, dbr_ref, st_ref, cos_ref, sin_ref, bkt_ref, rb_ref, sk_ref, cw_ref,
             dp_ref, drb_ref, dsk_ref, dcw_ref,
             bias_s, dbias_s, dkv_s, g_s, dy_s):
        p_ref, kvp_ref, cp_ref, dbr_ref = [_Widened(r) for r in (p_ref, kvp_ref, cp_ref, dbr_ref)]
        step = pl.program_id(0)
        n = nc - 1 - step
        ex = range(nb)

        @pl.when(step == 0)
        def _():
            _build_bias(bkt_ref, rb_ref, bias_s)
            dbias_s[...] = jnp.zeros_like(dbias_s)
            dsk_ref[...] = jnp.zeros_like(dsk_ref)
            dcw_ref[...] = jnp.zeros_like(dcw_ref)
            drb_ref[...] = jnp.zeros_like(drb_ref)
            dkv_s[...] = jnp.zeros_like(dkv_s)
            g_s[...] = jnp.zeros_like(g_s)
            dy_s[...] = jnp.zeros_like(dy_s)

        valid = _valid_col(n)
        mask = _band_mask(n)
        has_prev = (n > 0).astype(F32)

        k_all, v_all = [], []
        for b in ex:
            kv_prev = kvp_ref[b] * has_prev
            kv_cur = p_ref[b, :, C_AK:C_AK + 2 * BLOCK]
            k_all.append(jnp.concatenate([kv_prev[:, 0:BLOCK], kv_cur[:, 0:BLOCK]], axis=0))
            v_all.append(jnp.concatenate([kv_prev[:, BLOCK:], kv_cur[:, BLOCK:]], axis=0))
        zero2 = jnp.zeros((2 * BLOCK, BLOCK), F32)
        dk_tot = [zero2 for _ in ex]
        dv_tot = [zero2 for _ in ex]
        for kh in range(2):
            ks = [[t.astype(BF16) for t in _split_heads(k_all[b], kh)] for b in ex]
            vs = [[t.astype(BF16) for t in _split_heads(v_all[b], kh)] for b in ex]
            pairs = [(b, 2 * kh + jj) for jj in range(2) for b in ex]
            subs = [(b, j, x) for (b, j) in pairs for x in range(2)]
            qb_, gate_, dya_, do2_ = {}, {}, {}, {}
            for w in pairs:
                b, j = w
                qb_[w] = (p_ref[b, :, C_AQ + BLOCK * j:C_AQ + BLOCK * (j + 1)] * ATT_SCALE).astype(BF16)
                gate_[w] = p_ref[b, :, C_AG + BLOCK * j:C_AG + BLOCK * (j + 1)]
                dya_[w] = dbr_ref[b, :, BLOCK * j:BLOCK * (j + 1)]
                do2_[w] = (dya_[w] * _silu(gate_[w])).astype(BF16)
            qk_ = {(b, j, x): _nt(qb_[(b, j)], ks[b][x]) for (b, j, x) in subs}
            dpm_ = {(b, j, x): _nt(do2_[(b, j)], vs[b][x]) for (b, j, x) in subs}
            pb_, dsb_ = {}, {}
            for u in subs:
                b, j, x = u
                h = 2 * j + x
                p, p_sink = _softmax_of(qk_[u], bias_s[h], mask, sk_ref[layer, h])
                pb_[u] = p.astype(BF16)
                delta = jnp.sum(p * dpm_[u], axis=-1, keepdims=True)
                ds = p * (dpm_[u] - delta)
                dbias_s[h] += ds
                dsk_ref[h:h + 1, :] += jnp.broadcast_to(
                    jnp.sum(-p_sink * delta, axis=0, keepdims=True), (1, BLOCK))
                dsb_[u] = ds.astype(BF16)
            o_ = {u: _nn(pb_[u], vs[u[0]][u[2]]) for u in subs}
            dq_ = {u: _nn(dsb_[u], ks[u[0]][u[2]]) for u in subs}
            dkm_ = {u: _tn(dsb_[u], qb_[(u[0], u[1])]) for u in subs}
            dvm_ = {u: _tn(pb_[u], do2_[(u[0], u[1])]) for u in subs}
            for w in pairs:
                b, j = w
                o2 = o_[(b, j, 0)] + o_[(b, j, 1)]
                dq2 = (dq_[(b, j, 0)] + dq_[(b, j, 1)]) * ATT_SCALE
                dp_ref[b, :, C_AQ + BLOCK * j:C_AQ + BLOCK * (j + 1)] = dq2.astype(BF16)
                dp_ref[b, :, C_AG + BLOCK * j:C_AG + BLOCK * (j + 1)] = (
                    dya_[w] * o2 * _dsilu(gate_[w])).astype(BF16)
            for b in ex:
                j0, j1 = 2 * kh, 2 * kh + 1
                dk_tot[b] = dk_tot[b] + _merge_heads(dkm_[(b, j0, 0)] + dkm_[(b, j1, 0)],
                                                     dkm_[(b, j0, 1)] + dkm_[(b, j1, 1)], kh)
                dv_tot[b] = dv_tot[b] + _merge_heads(dvm_[(b, j0, 0)] + dvm_[(b, j1, 0)],
                                                     dvm_[(b, j0, 1)] + dvm_[(b, j1, 1)], kh)
        for b in ex:
            dp_ref[b, :, C_AK:C_AK + BLOCK] = (dk_tot[b][BLOCK:, :] + dkv_s[b, :, 0:BLOCK]).astype(BF16)
            dp_ref[b, :, C_AV:C_AV + BLOCK] = (dv_tot[b][BLOCK:, :] + dkv_s[b, :, BLOCK:]).astype(BF16)
            dkv_s[b, :, 0:BLOCK] = dk_tot[b][0:BLOCK, :]
            dkv_s[b, :, BLOCK:] = dv_tot[b][0:BLOCK, :]

        cosv = cos_ref[...]
        sinv = sin_ref[...]
        tabs = [_decay_tables(h) for h in range(RET_HEADS)]
        units = [(b, h) for h in range(RET_HEADS) for b in ex]
        sl = lambda c0, h: slice(c0 + BLOCK * h, c0 + BLOCK * (h + 1))
        q_, k_, v_, vb_, sp_ = {}, {}, {}, {}, {}
        for u in units:
            b, h = u
            q_[u] = _rot(p_ref[b, :, sl(C_RQ, h)], cosv, sinv).astype(BF16)
            k_[u] = (_rot(p_ref[b, :, sl(C_RK, h)], cosv, sinv) * RET_SCALE * valid).astype(BF16)
            v_[u] = p_ref[b, :, sl(C_RV, h)]
            vb_[u] = v_[u].astype(BF16)
            sp_[u] = st_ref[b, 0, h].astype(BF16)
        qk_ = {u: _nt(q_[u], k_[u]) for u in units}
        qs_ = {u: _nn(q_[u], sp_[u]) for u in units}
        a_ = {u: (qk_[u] * tabs[u[1]][0]).astype(BF16) for u in units}
        av_ = {u: _nn(a_[u], vb_[u]) for u in units}
        dob_, dxo_ = {}, {}
        for u in units:
            b, h = u
            xi = tabs[h][2]
            o = av_[u] + xi * qs_[u]
            mu = jnp.mean(o, axis=-1, keepdims=True)
            var = jnp.mean(jnp.square(o - mu), axis=-1, keepdims=True)
            rstd = lax.rsqrt(var + GN_EPS)
            oh = (o - mu) * rstd
            gate = p_ref[b, :, sl(C_RG, h)]
            d_yr = dbr_ref[b, :, BRANCH_WIDTH + BLOCK * h:BRANCH_WIDTH + BLOCK * (h + 1)]
            dp_ref[b, :, sl(C_RG, h)] = (d_yr * oh * _dsilu(gate)).astype(BF16)
            doh = d_yr * _silu(gate)
            do = rstd * (doh - jnp.mean(doh, axis=-1, keepdims=True)
                         - oh * jnp.mean(doh * oh, axis=-1, keepdims=True))
            dob_[u] = do.astype(BF16)
            dxo_[u] = (do * xi).astype(BF16)
        dov_ = {u: _nt(dob_[u], vb_[u]) for u in units}
        dv1_ = {u: _tn(a_[u], dob_[u]) for u in units}
        dq1_ = {u: _nt(dxo_[u], sp_[u]) for u in units}
        gq_ = {u: _tn(q_[u], dxo_[u]) for u in units}
        da_, gb_, zv_ = {}, {}, {}
        for u in units:
            b, h = u
            da_[u] = (dov_[u] * tabs[h][0]).astype(BF16)
            g_next = g_s[b, h]
            gb_[u] = g_next.astype(BF16)
            zv_[u] = (v_[u] * tabs[h][1]).astype(BF16)
            g_s[b, h] = tabs[h][3] * g_next + gq_[u]
        dq2_ = {u: _nn(da_[u], k_[u]) for u in units}
        dk1_ = {u: _tn(da_[u], q_[u]) for u in units}
        dk2_ = {u: _nt(zv_[u], gb_[u]) for u in units}
        dv2_ = {u: _nn(k_[u], gb_[u]) for u in units}
        for u in units:
            b, h = u
            dp_ref[b, :, sl(C_RQ, h)] = _rot_t(dq2_[u] + dq1_[u], cosv, sinv).astype(BF16)
            dp_ref[b, :, sl(C_RK, h)] = _rot_t((dk1_[u] + dk2_[u]) * (RET_SCALE * valid), cosv, sinv).astype(BF16)
            dp_ref[b, :, sl(C_RV, h)] = (dv1_[u] + tabs[h][1] * dv2_[u]).astype(BF16)

        w0, w1, w2 = cw_ref[0:1, :], cw_ref[1:2, :], cw_ref[2:3, :]
        for b in ex:
            cb = p_ref[b, :, C_CB:C_CB + BRANCH_WIDTH]
            cc = p_ref[b, :, C_CC:C_CC + BRANCH_WIDTH]
            cx = p_ref[b, :, C_CX:C_CX + BRANCH_WIDTH]
            cg = p_ref[b, :, C_CG:C_CG + BRANCH_WIDTH]
            u = cc * cx * valid
            u_prev = (cp_ref[b, :, 0:BRANCH_WIDTH] * cp_ref[b, :, BRANCH_WIDTH:2 * BRANCH_WIDTH]
                      * (_valid_col(n - 1) * has_prev))
            u1 = _shift_down(u, u_prev, 1)
            u2 = _shift_down(u, u_prev, 2)
            y = w0 * u2 + w1 * u1 + w2 * u
            d_yc = dbr_ref[b, :, 2 * BRANCH_WIDTH:3 * BRANCH_WIDTH]
            sg = _silu(cg)
            dp_ref[b, :, C_CB:C_CB + BRANCH_WIDTH] = (d_yc * y * sg).astype(BF16)
            dp_ref[b, :, C_CG:C_CG + BRANCH_WIDTH] = (d_yc * cb * y * _dsilu(cg)).astype(BF16)
            dy = d_yc * cb * sg
            dy_next = dy_s[b]
            du = (w2 * dy + w1 * _shift_up(dy, dy_next, 1) + w0 * _shift_up(dy, dy_next, 2)) * valid
            dp_ref[b, :, C_CC:C_CC + BRANCH_WIDTH] = (du * cx).astype(BF16)
            dp_ref[b, :, C_CX:C_CX + BRANCH_WIDTH] = (du * cc).astype(BF16)
            dcw_ref[0:1, :] += jnp.sum(dy * u2, axis=0, keepdims=True)
            dcw_ref[1:2, :] += jnp.sum(dy * u1, axis=0, keepdims=True)
            dcw_ref[2:3, :] += jnp.sum(dy * u, axis=0, keepdims=True)
            dy_s[b] = dy

        @pl.when(step == nc - 1)
        def _():
            bkt = bkt_ref[...]
            row = lax.broadcasted_iota(jnp.int32, (N_BUCKETS, BLOCK), 0)
            lane = lax.broadcasted_iota(jnp.int32, (N_BUCKETS, BLOCK), 1)

            def one_bucket(bk, acc):
                sel = bkt == bk
                for h in range(ATT_HEADS):
                    t = jnp.where(sel, dbias_s[h], 0.0)
                    s = jnp.sum(jnp.sum(t, axis=1, keepdims=True), axis=0, keepdims=True)
                    acc = acc + jnp.where((row == bk) & (lane == h), jnp.broadcast_to(s, acc.shape), 0.0)
                return acc

            drb_ref[...] = lax.fori_loop(0, N_BUCKETS, one_bucket, jnp.zeros((N_BUCKETS, BLOCK), F32))

    lp = nc * BLOCK
    smem = pl.BlockSpec(memory_space=pltpu.SMEM)
    blk = lambda s: nc - 1 - s
    prev = lambda s: jnp.maximum(nc - 2 - s, 0)
    proj3 = proj.reshape(nb, lp, ABC_WIDTH)
    res = pl.pallas_call(
        body, name="mixers_bwd",
        grid=(nc,),
        in_specs=[pl.BlockSpec((nb, BLOCK, ABC_WIDTH), lambda s: (0, blk(s), 0)),
                  pl.BlockSpec((nb, BLOCK, 2 * BLOCK), lambda s: (0, prev(s), C_AK // (2 * BLOCK))),
                  pl.BlockSpec((nb, BLOCK, 1280), lambda s: (0, prev(s), C_CC // 1280)),
                  pl.BlockSpec((nb, BLOCK, N_BRANCH * BRANCH_WIDTH), lambda s: (0, blk(s), 0)),
                  pl.BlockSpec((nb, 1, RET_HEADS, BLOCK, BLOCK), lambda s: (0, blk(s), 0, 0, 0)),
                  pl.BlockSpec((BLOCK, BLOCK), lambda s: (blk(s), 0)),
                  pl.BlockSpec((BLOCK, BLOCK), lambda s: (blk(s), 0)),
                  pl.BlockSpec((BLOCK, 2 * BLOCK), lambda s: (0, 0)),
                  smem, smem,
                  pl.BlockSpec((None, 3, BRANCH_WIDTH), lambda s: (layer, 0, 0))],
        out_specs=[pl.BlockSpec((nb, BLOCK, ABC_WIDTH), lambda s: (0, blk(s), 0)),
                   pl.BlockSpec((N_BUCKETS, BLOCK), lambda s: (0, 0)),
                   pl.BlockSpec((ATT_HEADS, BLOCK), lambda s: (0, 0)),
                   pl.BlockSpec((8, BRANCH_WIDTH), lambda s: (0, 0))],
        out_shape=[jax.ShapeDtypeStruct((nb, lp, ABC_WIDTH), BF16),
                   jax.ShapeDtypeStruct((N_BUCKETS, BLOCK), F32),
                   jax.ShapeDtypeStruct((ATT_HEADS, BLOCK), F32),
                   jax.ShapeDtypeStruct((8, BRANCH_WIDTH), F32)],
        scratch_shapes=[pltpu.VMEM((ATT_HEADS, BLOCK, 2 * BLOCK), F32),
                        pltpu.VMEM((ATT_HEADS, BLOCK, 2 * BLOCK), F32),
                        pltpu.VMEM((nb, BLOCK, 2 * BLOCK), F32),
                        pltpu.VMEM((nb, RET_HEADS, BLOCK, BLOCK), F32),
                        pltpu.VMEM((nb, BLOCK, BRANCH_WIDTH), F32)],
        compiler_params=_cparams("arbitrary"),
    )(proj3, proj3, proj3, d_br.reshape(nb, lp, N_BRANCH * BRANCH_WIDTH), states, cosf, sinf, bkt, rel_bias, sinks,
      conv_w)
    return (res[0].reshape(nb * lp, ABC_WIDTH),) + tuple(res[1:])


MERGE_TILE = 256
MERGE_FWD_TILE = 544


def _merge_forward(br_ref, m_ref, wb_ref, wo_ref):
    bo, gates = [], []
    mixed_pre = None
    for g in range(N_BRANCH):
        br_g = br_ref[:, BRANCH_WIDTH * g:BRANCH_WIDTH * (g + 1)]
        bo_g = jnp.concatenate([_nn(br_g, wb_ref[p, g]) for p in range(N_CHIPS)], axis=1)
        gate_g = _sigmoid(m_ref[:, D_MODEL * g:D_MODEL * (g + 1)].astype(F32))
        bo.append(bo_g)
        gates.append(gate_g)
        mixed_pre = gate_g * bo_g if mixed_pre is None else mixed_pre + gate_g * bo_g
    mixed = _nn(mixed_pre.astype(BF16), wo_ref[...])
    r = lax.rsqrt(jnp.mean(mixed * mixed, axis=-1, keepdims=True) + RMS_EPS)
    return bo, gates, mixed_pre, mixed, r


def merge_fwd(x2d, br, pm, wb, wo, g_post, layer, after):
    t = x2d.shape[0]
    tm = MERGE_FWD_TILE if t % MERGE_FWD_TILE == 0 else BLOCK

    def body(x_ref, br_ref, m_ref, wb_ref, wo_ref, g_ref, after_ref, o_ref):
        _, _, _, mixed, r = _merge_forward(br_ref, m_ref, wb_ref, wo_ref)
        o_ref[...] = x_ref[...] + mixed * r * g_ref[...]

    return pl.pallas_call(
        body, name="merge_fwd",
        grid=(t // tm,),
        in_specs=[pl.BlockSpec((tm, D_MODEL), lambda i: (i, 0)),
                  pl.BlockSpec((tm, N_BRANCH * BRANCH_WIDTH), lambda i: (i, 0)),
                  pl.BlockSpec((tm, MERGE_WIDTH), lambda i: (i, 0)),
                  pl.BlockSpec((N_CHIPS, N_BRANCH, BRANCH_WIDTH, SHARD_D), lambda i: (0, 0, 0, 0)),
                  pl.BlockSpec((D_MODEL, D_MODEL), lambda i: (0, 0)),
                  pl.BlockSpec((None, 1, D_MODEL), lambda i: (layer, 0, 0)),
                  ANY],
        out_specs=pl.BlockSpec((tm, D_MODEL), lambda i: (i, 0)),
        out_shape=jax.ShapeDtypeStruct((t, D_MODEL), F32),
        compiler_params=_cparams("parallel"),
    )(x2d, br, pm, wb, wo, g_post, after)


def merge_fwd_loss(x2d, br, pm, wb, wo, g_post, layer, target, lp):
    t = x2d.shape[0]
    tm = MERGE_FWD_TILE if t % MERGE_FWD_TILE == 0 else BLOCK

    def body(x_ref, br_ref, m_ref, wb_ref, wo_ref, g_ref, t_ref, l_ref, d_ref):
        i = pl.program_id(0)

        @pl.when(i == 0)
        def _():
            l_ref[...] = jnp.zeros_like(l_ref)

        _, _, _, mixed, r = _merge_forward(br_ref, m_ref, wb_ref, wo_ref)
        y = x_ref[...] + mixed * r * g_ref[...]
        row = i * tm + lax.broadcasted_iota(jnp.int32, (tm, 1), 0)
        e = jnp.where(row % lp >= BLOCK, y - t_ref[...], 0.0)
        d_ref[...] = e * (1.0 / D_MODEL)
        s = jnp.sum(jnp.sum(e * e, axis=1, keepdims=True), axis=0, keepdims=True)
        l_ref[...] += jnp.broadcast_to(s * (0.5 / D_MODEL), l_ref.shape)

    return pl.pallas_call(
        body, name="merge_fwd_loss",
        grid=(t // tm,),
        in_specs=[pl.BlockSpec((tm, D_MODEL), lambda i: (i, 0)),
                  pl.BlockSpec((tm, N_BRANCH * BRANCH_WIDTH), lambda i: (i, 0)),
                  pl.BlockSpec((tm, MERGE_WIDTH), lambda i: (i, 0)),
                  pl.BlockSpec((N_CHIPS, N_BRANCH, BRANCH_WIDTH, SHARD_D), lambda i: (0, 0, 0, 0)),
                  pl.BlockSpec((D_MODEL, D_MODEL), lambda i: (0, 0)),
                  pl.BlockSpec((None, 1, D_MODEL), lambda i: (layer, 0, 0)),
                  pl.BlockSpec((tm, D_MODEL), lambda i: (i, 0))],
        out_specs=[pl.BlockSpec((8, BLOCK), lambda i: (0, 0)),
                   pl.BlockSpec((tm, D_MODEL), lambda i: (i, 0))],
        out_shape=[jax.ShapeDtypeStruct((8, BLOCK), F32),
                   jax.ShapeDtypeStruct((t, D_MODEL), F32)],
        compiler_params=_cparams("arbitrary"),
    )(x2d, br, pm, wb, wo, g_post, target)


def merge_bwd(d_out, br, pm, wb, wo, g_post, layer, after):
    t = d_out.shape[0]
    tm = MERGE_TILE if t % MERGE_TILE == 0 else BLOCK

    def body(do_ref, br_ref, m_ref, wb_ref, wo_ref, g_ref, after_ref, dbr_ref, dm_ref, dg_ref, dwb_ref, dwo_ref):

        @pl.when(pl.program_id(0) == 0)
        def _():
            dwb_ref[...] = jnp.zeros_like(dwb_ref)
            dwo_ref[...] = jnp.zeros_like(dwo_ref)
            dg_ref[...] = jnp.zeros_like(dg_ref)

        bo, gates, mixed_pre, mixed, r = _merge_forward(br_ref, m_ref, wb_ref, wo_ref)
        d_o = do_ref[...]
        nh = mixed * r
        dg_ref[0:1, :] += jnp.sum(d_o * nh, axis=0, keepdims=True)
        dn = d_o * g_ref[...]
        d_mixed = (r * (dn - nh * jnp.mean(dn * nh, axis=-1, keepdims=True))).astype(BF16)
        dwo_ref[...] += _tn(mixed_pre.astype(BF16), d_mixed)
        d_pre = _nt(d_mixed, wo_ref[...])
        for g in range(N_BRANCH):
            br_g = br_ref[:, BRANCH_WIDTH * g:BRANCH_WIDTH * (g + 1)]
            d_bo = (d_pre * gates[g]).astype(BF16)
            dm_ref[:, D_MODEL * g:D_MODEL * (g + 1)] = (
                d_pre * bo[g] * gates[g] * (1.0 - gates[g])).astype(BF16)
            d_br_g = None
            for p in range(N_CHIPS):
                d_bo_p = d_bo[:, SHARD_D * p:SHARD_D * (p + 1)]
                part = _nt(d_bo_p, wb_ref[p, g])
                d_br_g = part if d_br_g is None else d_br_g + part
                dwb_ref[p, g] += _tn(br_g, d_bo_p)
            dbr_ref[:, BRANCH_WIDTH * g:BRANCH_WIDTH * (g + 1)] = d_br_g.astype(BF16)

    return pl.pallas_call(
        body, name="merge_bwd",
        grid=(t // tm,),
        in_specs=[pl.BlockSpec((tm, D_MODEL), lambda i: (i, 0)),
                  pl.BlockSpec((tm, N_BRANCH * BRANCH_WIDTH), lambda i: (i, 0)),
                  pl.BlockSpec((tm, MERGE_WIDTH), lambda i: (i, 0)),
                  pl.BlockSpec((N_CHIPS, N_BRANCH, BRANCH_WIDTH, SHARD_D), lambda i: (0, 0, 0, 0)),
                  pl.BlockSpec((D_MODEL, D_MODEL), lambda i: (0, 0)),
                  pl.BlockSpec((None, 1, D_MODEL), lambda i: (layer, 0, 0)),
                  ANY],
        out_specs=[pl.BlockSpec((tm, N_BRANCH * BRANCH_WIDTH), lambda i: (i, 0)),
                   pl.BlockSpec((tm, MERGE_WIDTH), lambda i: (i, 0)),
                   pl.BlockSpec((8, D_MODEL), lambda i: (0, 0)),
                   pl.BlockSpec((N_CHIPS, N_BRANCH, BRANCH_WIDTH, SHARD_D), lambda i: (0, 0, 0, 0)),
                   pl.BlockSpec((D_MODEL, D_MODEL), lambda i: (0, 0))],
        out_shape=[jax.ShapeDtypeStruct((t, N_BRANCH * BRANCH_WIDTH), BF16),
                   jax.ShapeDtypeStruct((t, MERGE_WIDTH), BF16),
                   jax.ShapeDtypeStruct((8, D_MODEL), F32),
                   jax.ShapeDtypeStruct((N_CHIPS, N_BRANCH, BRANCH_WIDTH, SHARD_D), F32),
                   jax.ShapeDtypeStruct((D_MODEL, D_MODEL), F32)],
        compiler_params=_cparams("arbitrary"),
    )(d_out, br, pm, wb, wo, g_post, after)


N_ABC_TILES = ABC_WIDTH // COL_TILE
N_M_TILES = MERGE_WIDTH // COL_TILE


def proj_dgrad(d_abc, d_m, w, x2d, g, layer, d_out, after, examples=None):
    t = x2d.shape[0]
    tm = ROW_TILE if t % ROW_TILE == 0 else BLOCK
    nk = N_ABC_TILES + N_M_TILES
    n_i = t // tm
    if examples is not None:
        lp = t // examples
        assert lp % tm == 0
        tiles_per_example = lp // tm

    def body(da_ref, dm_ref, w_ref, x_ref, g_ref, do_ref, after_ref, dx_ref, dg_ref, *rest):
        acc = rest[-3] if examples is not None else rest[-1]
        i = pl.program_id(0)
        k = pl.program_id(1)

        def for_copy_of_tile(tile, act):
            seq_ref, stage, sem = rest[0], rest[-2], rest[-1]
            e, j = tile // tiles_per_example, tile % tiles_per_example
            if tm > BLOCK:
                @pl.when(j == 0)
                def _():
                    act(pltpu.make_async_copy(stage.at[pl.ds(BLOCK, tm - BLOCK)],
                                              seq_ref.at[e, pl.ds(0, tm - BLOCK)], sem.at[0]))

            @pl.when(j > 0)
            def _():
                act(pltpu.make_async_copy(stage, seq_ref.at[e, pl.ds(pl.multiple_of(j * tm - BLOCK, 8), tm)],
                                          sem.at[0]))

        @pl.when((i == 0) & (k == 0))
        def _():
            dg_ref[...] = jnp.zeros_like(dg_ref)

        @pl.when(k == 0)
        def _():
            acc[...] = jnp.zeros_like(acc)

        @pl.when(k < N_ABC_TILES)
        def _():
            acc[...] += _nn(da_ref[...], w_ref[...])

        @pl.when(k >= N_ABC_TILES)
        def _():
            acc[...] += _nn(dm_ref[...], w_ref[...])

        @pl.when(k == nk - 1)
        def _():
            x = x_ref[...]
            r = lax.rsqrt(jnp.mean(x * x, axis=-1, keepdims=True) + RMS_EPS)
            nh = x * r
            dh = acc[...]
            dg_ref[0:1, :] += jnp.sum(dh * nh, axis=0, keepdims=True)
            dn = dh * g_ref[...]
            dx = do_ref[...] + r * (dn - nh * jnp.mean(dn * nh, axis=-1, keepdims=True))
            dx_ref[...] = dx
            if examples is not None:
                @pl.when(i > 0)
                def _():
                    for_copy_of_tile(i - 1, lambda copy: copy.wait())

                rest[-2][...] = dx
                for_copy_of_tile(i, lambda copy: copy.start())

                @pl.when(i == n_i - 1)
                def _():
                    for_copy_of_tile(i, lambda copy: copy.wait())

    out_specs = [pl.BlockSpec((tm, D_MODEL), lambda i, k: (i, 0)),
                 pl.BlockSpec((8, D_MODEL), lambda i, k: (0, 0))]
    out_shape = [jax.ShapeDtypeStruct((t, D_MODEL), F32),
                 jax.ShapeDtypeStruct((8, D_MODEL), F32)]
    scratch_shapes = [pltpu.VMEM((tm, D_MODEL), F32)]
    if examples is not None:
        out_specs.append(ANY)
        out_shape.append(jax.ShapeDtypeStruct((examples, lp - BLOCK, D_MODEL), F32))
        scratch_shapes += [pltpu.VMEM((tm, D_MODEL), F32), pltpu.SemaphoreType.DMA((1,))]

    return pl.pallas_call(
        body, name="proj_dgrad",
        grid=(n_i, nk),
        in_specs=[pl.BlockSpec((tm, COL_TILE), lambda i, k: (i, jnp.minimum(k, N_ABC_TILES - 1))),
                  pl.BlockSpec((tm, COL_TILE), lambda i, k: (i, jnp.maximum(k - N_ABC_TILES, 0))),
                  pl.BlockSpec((COL_TILE, D_MODEL), lambda i, k: (k, 0)),
                  pl.BlockSpec((tm, D_MODEL), lambda i, k: (i, 0)),
                  pl.BlockSpec((None, 1, D_MODEL), lambda i, k: (layer, 0, 0)),
                  pl.BlockSpec((tm, D_MODEL), lambda i, k: (i, 0)),
                  ANY],
        out_specs=out_specs, out_shape=out_shape, scratch_shapes=scratch_shapes,
        compiler_params=_cparams("arbitrary", "arbitrary"),
    )(d_abc, d_m, w, x2d, g, d_out, after)


def proj_wgrad(hb, d_abc, d_m):
    t = hb.shape[0]
    nj = N_ABC_TILES + N_M_TILES

    def body(h_ref, da_ref, dm_ref, o_ref):
        j = pl.program_id(0)

        @pl.when(j < N_ABC_TILES)
        def _():
            o_ref[...] = _tn(da_ref[...], h_ref[...])

        @pl.when(j >= N_ABC_TILES)
        def _():
            o_ref[...] = _tn(dm_ref[...], h_ref[...])

    return pl.pallas_call(
        body, name="proj_wgrad",
        grid=(nj,),
        in_specs=[pl.BlockSpec((t, D_MODEL), lambda j: (0, 0)),
                  pl.BlockSpec((t, COL_TILE), lambda j: (0, jnp.minimum(j, N_ABC_TILES - 1))),
                  pl.BlockSpec((t, COL_TILE), lambda j: (0, jnp.maximum(j - N_ABC_TILES, 0)))],
        out_specs=pl.BlockSpec((COL_TILE, D_MODEL), lambda j: (j, 0)),
        out_shape=jax.ShapeDtypeStruct((PROJ_WIDTH, D_MODEL), F32),
        compiler_params=_cparams("arbitrary"),
    )(hb, d_abc, d_m)


def _adamw_math(w, g, m, v):
    m = ADAM_B1 * m + (1.0 - ADAM_B1) * g
    v = ADAM_B2 * v + (1.0 - ADAM_B2) * jnp.square(g)
    m_hat = m / (1.0 - ADAM_B1 ** ADAM_STEP)
    v_hat = v / (1.0 - ADAM_B2 ** ADAM_STEP)
    delta = -ADAM_LR * (m_hat / (jnp.sqrt(v_hat) + ADAM_EPS) + ADAM_WD * w)
    return delta, m, v


def adamw_layer(w, g, m, v, layer, acc, after):
    _, r, c = w.shape
    tr = _row_tile(r)

    def body(*refs):
        w_ref, g_ref, m_ref, v_ref = refs[:4]
        go_ref, d_ref, mo_ref, vo_ref = refs[-4:]
        g_val = g_ref[...]
        d, m_new, v_new = _adamw_math(w_ref[...], g_val, m_ref[...], v_ref[...])
        go_ref[...] = g_val
        d_ref[...] = d
        mo_ref[...] = m_new
        vo_ref[...] = v_new

    slab = pl.BlockSpec((None, tr, c), lambda i: (layer, i, 0))
    ins = [w, g, m, v, after]
    in_specs = [slab, pl.BlockSpec((tr, c), lambda i: (i, 0)), slab, slab, ANY]
    aliases = {}
    if acc is not None:
        ins += list(acc)
        in_specs += [ANY] * 4
        aliases = {5 + i: i for i in range(4)}
    return pl.pallas_call(
        body, name="adamw_layer",
        grid=(r // tr,),
        in_specs=in_specs, out_specs=[slab] * 4,
        out_shape=[jax.ShapeDtypeStruct(w.shape, F32)] * 4,
        input_output_aliases=aliases,
        compiler_params=_cparams("parallel"),
    )(*ins)


def adamw_small(params):
    k = len(params)

    def body(*refs):
        ins, outs = refs[:4 * k], refs[4 * k:]
        for i in range(k):
            d, m_new, v_new = _adamw_math(*[r[...] for r in ins[4 * i:4 * i + 4]])
            outs[3 * i][...] = d
            outs[3 * i + 1][...] = m_new
            outs[3 * i + 2][...] = v_new

    flat = [a for p in params for a in p]
    vm = pl.BlockSpec(memory_space=pltpu.VMEM)
    out_shape = [jax.ShapeDtypeStruct(p[0].shape, F32) for p in params for _ in range(3)]
    res = pl.pallas_call(
        body, name="adamw_small",
        in_specs=[vm] * len(flat), out_specs=[vm] * len(out_shape), out_shape=out_shape,
    )(*flat)
    return [tuple(res[3 * i:3 * i + 3]) for i in range(k)]


ANY = pl.BlockSpec(memory_space=pl.ANY)


def _place():
    return lax.axis_index("x"), lax.axis_index("y"), lax.axis_index("c")


HBM = pl.BlockSpec(memory_space=pltpu.HBM)
SEM = pl.BlockSpec(memory_space=pltpu.SEMAPHORE)
EFFECT = pltpu.SideEffectType.DATAFLOW_SIDE_EFFECTING


def _other_chips(x, y):
    return [(1 - x, y), (x, 1 - y), (1 - x, 1 - y)]


def _own_slot(shard, chip):
    buf = lax.empty((N_CHIPS,) + shard.shape, shard.dtype)
    return lax.dynamic_update_slice(buf, shard[None], (chip, 0, 0, 0))


def _hbm(a):
    return pltpu.with_memory_space_constraint(a, pltpu.HBM)


def gather_start(bufs, after):
    n = len(bufs)

    def body(*refs):
        g_refs = refs[:n]
        send_sems, recv_sems = refs[n + 1], refs[n + 2]
        token = refs[-1]
        x, y, c = _place()
        me_p = 2 * x + y
        for t in range(n):
            for k, (qx, qy) in enumerate(_other_chips(x, y)):
                slab = g_refs[t].at[me_p, c]
                pltpu.make_async_remote_copy(src_ref=slab, dst_ref=slab, send_sem=send_sems.at[3 * t + k],
                                             recv_sem=recv_sems.at[3 * t + k], device_id=(qx, qy, c),
                                             device_id_type=MESH).start()
        token[...] = jnp.zeros_like(token)

    res = pl.pallas_call(
        body, name="gather_start",
        in_specs=[HBM] * n + [ANY],
        out_specs=[SEM, SEM] + [HBM] * n + [pl.BlockSpec(memory_space=pltpu.VMEM)],
        out_shape=[pltpu.SemaphoreType.DMA((3 * n,)), pltpu.SemaphoreType.DMA((3 * n,))]
        + [pltpu.HBM(b.shape, b.dtype) for b in bufs] + [jax.ShapeDtypeStruct((8, LANES), F32)],
        input_output_aliases={t: 2 + t for t in range(n)},
        compiler_params=pltpu.CompilerParams(has_side_effects=EFFECT),
    )(*[_hbm(b) for b in bufs], after)
    return res[0], res[1], list(res[2:2 + n]), res[-1]


def gather_wait(bufs, send_sems, recv_sems, after, first=0):
    n = len(bufs)

    def body(*refs):
        g_refs = refs[:n]
        send_sems, recv_sems = refs[n], refs[n + 1]
        x, y, c = _place()
        me_p = 2 * x + y
        for t in range(n):
            for k, (qx, qy) in enumerate(_other_chips(x, y)):
                s = 3 * (first + t) + k
                cp = pltpu.make_async_remote_copy(src_ref=g_refs[t].at[me_p, c], dst_ref=g_refs[t].at[2 * qx + qy, c],
                                                  send_sem=send_sems.at[s], recv_sem=recv_sems.at[s],
                                                  device_id=(qx, qy, c), device_id_type=MESH)
                cp.wait_send()
                cp.wait_recv()

    return pl.pallas_call(
        body, name="gather_wait",
        in_specs=[HBM] * n + [SEM, SEM, ANY],
        out_specs=[HBM] * n,
        out_shape=[pltpu.HBM(b.shape, b.dtype) for b in bufs],
        input_output_aliases={t: t for t in range(n)},
        compiler_params=pltpu.CompilerParams(has_side_effects=EFFECT),
    )(*bufs, send_sems, recv_sems, after)


def gather_forward(bufs):
    n = len(bufs)

    def body(*refs):
        g_refs = refs[n:2 * n]
        send_sems, recv_sems = refs[2 * n:]
        x, y, c = _place()
        sibling = (x, y, 1 - c)
        chips = _other_chips(x, y)
        passed = []
        for t in range(n):
            for k, (qx, qy) in enumerate(chips):
                slab = g_refs[t].at[2 * qx + qy, c]
                fwd = pltpu.make_async_remote_copy(src_ref=slab, dst_ref=slab, send_sem=send_sems.at[3 * t + k],
                                                   recv_sem=recv_sems.at[3 * t + k], device_id=sibling,
                                                   device_id_type=MESH)
                fwd.start()
                passed.append(fwd)
        for t in range(n):
            for k, (qx, qy) in enumerate(chips):
                slab = g_refs[t].at[2 * qx + qy, 1 - c]
                pltpu.make_async_remote_copy(src_ref=slab, dst_ref=slab, send_sem=send_sems.at[3 * t + k],
                                             recv_sem=recv_sems.at[3 * t + k], device_id=sibling,
                                             device_id_type=MESH).wait_recv()
        for cp in passed:
            cp.wait_send()

    return pl.pallas_call(
        body, name="gather_forward",
        in_specs=[ANY] * n, out_specs=[ANY] * n,
        out_shape=[jax.ShapeDtypeStruct(b.shape, b.dtype) for b in bufs],
        input_output_aliases={t: t for t in range(n)},
        scratch_shapes=[pltpu.SemaphoreType.DMA((3 * n,)), pltpu.SemaphoreType.DMA((3 * n,))],
    )(*bufs)


def forward_start(bufs):
    n = len(bufs)

    def body(*refs):
        g_refs = refs[:n]
        send_sems, recv_sems = refs[n], refs[n + 1]
        token = refs[-1]
        x, y, c = _place()
        for t in range(n):
            for k, (qx, qy) in enumerate(_other_chips(x, y)):
                slab = g_refs[t].at[2 * qx + qy, c]
                pltpu.make_async_remote_copy(src_ref=slab, dst_ref=slab, send_sem=send_sems.at[3 * t + k],
                                             recv_sem=recv_sems.at[3 * t + k], device_id=(x, y, 1 - c),
                                             device_id_type=MESH).start()
        token[...] = jnp.zeros_like(token)

    res = pl.pallas_call(
        body, name="forward_start",
        in_specs=[HBM] * n,
        out_specs=[SEM, SEM] + [HBM] * n + [pl.BlockSpec(memory_space=pltpu.VMEM)],
        out_shape=[pltpu.SemaphoreType.DMA((3 * n,)), pltpu.SemaphoreType.DMA((3 * n,))]
        + [pltpu.HBM(b.shape, b.dtype) for b in bufs] + [jax.ShapeDtypeStruct((8, LANES), F32)],
        input_output_aliases={t: 2 + t for t in range(n)},
        compiler_params=pltpu.CompilerParams(has_side_effects=EFFECT),
    )(*[_hbm(b) for b in bufs])
    return res[0], res[1], list(res[2:2 + n]), res[-1]


def forward_wait(bufs, send_sems, recv_sems, after):
    n = len(bufs)

    def body(*refs):
        g_refs = refs[:n]
        send_sems, recv_sems = refs[n], refs[n + 1]
        x, y, c = _place()
        for t in range(n):
            for k, (qx, qy) in enumerate(_other_chips(x, y)):
                cp = pltpu.make_async_remote_copy(src_ref=g_refs[t].at[2 * qx + qy, c],
                                                  dst_ref=g_refs[t].at[2 * qx + qy, 1 - c],
                                                  send_sem=send_sems.at[3 * t + k], recv_sem=recv_sems.at[3 * t + k],
                                                  device_id=(x, y, 1 - c), device_id_type=MESH)
                cp.wait_send()
                cp.wait_recv()

    return pl.pallas_call(
        body, name="forward_wait",
        in_specs=[HBM] * n + [SEM, SEM, ANY],
        out_specs=[HBM] * n,
        out_shape=[pltpu.HBM(b.shape, b.dtype) for b in bufs],
        input_output_aliases={t: t for t in range(n)},
        compiler_params=pltpu.CompilerParams(has_side_effects=EFFECT),
    )(*bufs, send_sems, recv_sems, after)


def small_start(pack, me, after):
    buf = lax.dynamic_update_slice(lax.empty((8,) + pack.shape, pack.dtype), pack[None], (me, 0, 0))

    def body(b_ref, after_ref, send_sems, recv_sems, thru, token):
        x, y, c = _place()
        slot = b_ref.at[4 * x + 2 * y + c]
        for k in range(1, 8):
            peer = (x ^ ((k >> 2) & 1), y ^ ((k >> 1) & 1), c ^ (k & 1))
            pltpu.make_async_remote_copy(src_ref=slot, dst_ref=slot, send_sem=send_sems.at[k - 1],
                                         recv_sem=recv_sems.at[k - 1], device_id=peer, device_id_type=MESH).start()
        token[...] = jnp.zeros_like(token)

    return pl.pallas_call(
        body, name="small_start",
        in_specs=[HBM, ANY],
        out_specs=[SEM, SEM, HBM, pl.BlockSpec(memory_space=pltpu.VMEM)],
        out_shape=[pltpu.SemaphoreType.DMA((7,)), pltpu.SemaphoreType.DMA((7,)), pltpu.HBM(buf.shape, buf.dtype),
                   jax.ShapeDtypeStruct((8, LANES), F32)],
        input_output_aliases={0: 2},
        compiler_params=pltpu.CompilerParams(has_side_effects=EFFECT),
    )(_hbm(buf), after)


def small_wait(buf, send_sems, recv_sems, after):
    def body(b_ref, send_sems, recv_sems, after_ref, thru):
        x, y, c = _place()
        mine = b_ref.at[4 * x + 2 * y + c]
        for k in range(1, 8):
            peer = (x ^ ((k >> 2) & 1), y ^ ((k >> 1) & 1), c ^ (k & 1))
            cp = pltpu.make_async_remote_copy(src_ref=mine, dst_ref=b_ref.at[4 * peer[0] + 2 * peer[1] + peer[2]],
                                              send_sem=send_sems.at[k - 1], recv_sem=recv_sems.at[k - 1],
                                              device_id=peer, device_id_type=MESH)
            cp.wait_send()
            cp.wait_recv()

    return pl.pallas_call(
        body, name="small_wait",
        in_specs=[HBM, SEM, SEM, ANY], out_specs=HBM,
        out_shape=pltpu.HBM(buf.shape, buf.dtype),
        input_output_aliases={0: 0},
        compiler_params=pltpu.CompilerParams(has_side_effects=EFFECT),
    )(buf, send_sems, recv_sems, after)


def swap_start(grads, after):
    n = len(grads)

    def body(*refs):
        g_refs, l_refs = refs[:n], refs[n:2 * n]
        send_sems, recv_sems = refs[2 * n + 1], refs[2 * n + 2]
        token = refs[-1]
        x, y, c = _place()
        for t in range(n):
            for p in range(N_CHIPS):
                pltpu.make_async_remote_copy(src_ref=g_refs[t].at[p, 1 - c], dst_ref=l_refs[t].at[p],
                                             send_sem=send_sems.at[N_CHIPS * t + p],
                                             recv_sem=recv_sems.at[N_CHIPS * t + p],
                                             device_id=(x, y, 1 - c), device_id_type=MESH).start()
        token[...] = jnp.zeros_like(token)

    lands = [lax.empty((N_CHIPS,) + g.shape[2:], g.dtype) for g in grads]
    res = pl.pallas_call(
        body, name="swap_start",
        in_specs=[HBM] * (2 * n) + [ANY],
        out_specs=[SEM, SEM] + [HBM] * (2 * n) + [pl.BlockSpec(memory_space=pltpu.VMEM)],
        out_shape=[pltpu.SemaphoreType.DMA((N_CHIPS * n,)), pltpu.SemaphoreType.DMA((N_CHIPS * n,))]
        + [pltpu.HBM(a.shape, a.dtype) for a in grads + lands] + [jax.ShapeDtypeStruct((8, LANES), F32)],
        input_output_aliases={t: 2 + t for t in range(2 * n)},
        compiler_params=pltpu.CompilerParams(has_side_effects=EFFECT),
    )(*[_hbm(a) for a in grads + lands], after)
    return res[0], res[1], list(res[2:2 + n]), list(res[2 + n:2 + 2 * n]), res[-1]


def swap_wait(grads, lands, send_sems, recv_sems, after):
    n = len(grads)

    def body(*refs):
        g_refs, l_refs = refs[:n], refs[n:2 * n]
        send_sems, recv_sems = refs[2 * n], refs[2 * n + 1]
        x, y, c = _place()
        for t in range(n):
            for p in range(N_CHIPS):
                cp = pltpu.make_async_remote_copy(src_ref=g_refs[t].at[p, 1 - c], dst_ref=l_refs[t].at[p],
                                                  send_sem=send_sems.at[N_CHIPS * t + p],
                                                  recv_sem=recv_sems.at[N_CHIPS * t + p],
                                                  device_id=(x, y, 1 - c), device_id_type=MESH)
                cp.wait_send()
                cp.wait_recv()

    res = pl.pallas_call(
        body, name="swap_wait",
        in_specs=[HBM] * (2 * n) + [SEM, SEM, ANY],
        out_specs=[HBM] * (2 * n),
        out_shape=[pltpu.HBM(a.shape, a.dtype) for a in grads + lands],
        input_output_aliases={t: t for t in range(2 * n)},
        compiler_params=pltpu.CompilerParams(has_side_effects=EFFECT),
    )(*grads, *lands, send_sems, recv_sems, after)
    return list(res[:n]), list(res[n:])


def cast_into_slot(w, layer, where):
    _, r, c = w.shape
    rh = r // 2
    steps = 2 if rh % 32 == 0 else 1
    tr = rh // steps

    def body(where_ref, w_ref, o_ref):
        o_ref[...] = w_ref[...].astype(BF16)

    return pl.pallas_call(
        body, name="cast_into_slot",
        grid_spec=pltpu.PrefetchScalarGridSpec(
            num_scalar_prefetch=1, grid=(2, steps),
            in_specs=[pl.BlockSpec((None, tr, c), lambda h, i, where_ref: (layer, h * steps + i, 0))],
            out_specs=pl.BlockSpec((None, None, tr, c), lambda h, i, where_ref: (where_ref[0], h, i, 0))),
        out_shape=jax.ShapeDtypeStruct((N_CHIPS, 2, rh, c), BF16),
        compiler_params=_cparams("arbitrary", "arbitrary"),
    )(where, w)


def _row_tile(r):
    return max(t for t in range(16, 513, 16) if r % t == 0)


def add_own_half(g, other, c_arr):
    _, _, r, cols = g.shape
    tr = _row_tile(r)

    def body(c_ref, a_ref, b_ref, o_ref):
        o_ref[...] = (a_ref[...] + b_ref[...]).astype(BF16)

    return pl.pallas_call(
        body, name="add_own_half",
        grid_spec=pltpu.PrefetchScalarGridSpec(
            num_scalar_prefetch=1, grid=(N_CHIPS, r // tr),
            in_specs=[pl.BlockSpec((None, None, tr, cols), lambda p, i, c_ref: (p, c_ref[0], i, 0)),
                      pl.BlockSpec((None, tr, cols), lambda p, i, c_ref: (p, i, 0))],
            out_specs=pl.BlockSpec((None, tr, cols), lambda p, i, c_ref: (p, i, 0))),
        out_shape=jax.ShapeDtypeStruct((N_CHIPS, r, cols), BF16),
        compiler_params=_cparams("parallel", "parallel"),
    )(c_arr, g, other)


def scatter_start(partials):
    n = len(partials)

    def body(*refs):
        s_refs, l_refs = refs[:n], refs[n:2 * n]
        send_sems, recv_sems = refs[2 * n], refs[2 * n + 1]
        token = refs[-1]
        x, y, c = _place()
        for t in range(n):
            for k, (qx, qy) in enumerate(_other_chips(x, y)):
                pltpu.make_async_remote_copy(src_ref=s_refs[t].at[2 * qx + qy], dst_ref=l_refs[t].at[k],
                                             send_sem=send_sems.at[3 * t + k], recv_sem=recv_sems.at[3 * t + k],
                                             device_id=(qx, qy, c), device_id_type=MESH).start()
        token[...] = jnp.zeros_like(token)

    lands = [lax.empty((3,) + s.shape[1:], s.dtype) for s in partials]
    res = pl.pallas_call(
        body, name="scatter_start",
        in_specs=[HBM] * (2 * n),
        out_specs=[SEM, SEM] + [HBM] * (2 * n) + [pl.BlockSpec(memory_space=pltpu.VMEM)],
        out_shape=[pltpu.SemaphoreType.DMA((3 * n,)), pltpu.SemaphoreType.DMA((3 * n,))]
        + [pltpu.HBM(a.shape, a.dtype) for a in partials + lands] + [jax.ShapeDtypeStruct((8, LANES), F32)],
        input_output_aliases={t: 2 + t for t in range(2 * n)},
        compiler_params=pltpu.CompilerParams(has_side_effects=EFFECT),
    )(*[_hbm(a) for a in partials + lands])
    return res[0], res[1], list(res[2:2 + n]), list(res[2 + n:2 + 2 * n]), res[-1]


def scatter_wait(partials, lands, send_sems, recv_sems, after):
    n = len(partials)

    def body(*refs):
        s_refs, l_refs = refs[:n], refs[n:2 * n]
        send_sems, recv_sems = refs[2 * n], refs[2 * n + 1]
        x, y, c = _place()
        for t in range(n):
            for k, (qx, qy) in enumerate(_other_chips(x, y)):
                cp = pltpu.make_async_remote_copy(src_ref=s_refs[t].at[2 * qx + qy], dst_ref=l_refs[t].at[k],
                                                  send_sem=send_sems.at[3 * t + k], recv_sem=recv_sems.at[3 * t + k],
                                                  device_id=(qx, qy, c), device_id_type=MESH)
                cp.wait_send()
                cp.wait_recv()

    res = pl.pallas_call(
        body, name="scatter_wait",
        in_specs=[HBM] * (2 * n) + [SEM, SEM, ANY],
        out_specs=[HBM] * (2 * n),
        out_shape=[pltpu.HBM(a.shape, a.dtype) for a in partials + lands],
        input_output_aliases={t: t for t in range(2 * n)},
        compiler_params=pltpu.CompilerParams(has_side_effects=EFFECT),
    )(*partials, *lands, send_sems, recv_sems, after)
    return list(res[:n]), list(res[n:])


def sum_chips(own, parts, where):
    _, r, cols = own.shape
    tr = _row_tile(r)

    def body(w_ref, a_ref, p_ref, o_ref):
        acc = a_ref[...].astype(F32)
        for k in range(3):
            acc = acc + p_ref[k].astype(F32)
        o_ref[...] = acc

    return pl.pallas_call(
        body, name="sum_chips",
        grid_spec=pltpu.PrefetchScalarGridSpec(
            num_scalar_prefetch=1, grid=(r // tr,),
            in_specs=[pl.BlockSpec((None, tr, cols), lambda i, w_ref: (w_ref[0], i, 0)),
                      pl.BlockSpec((3, tr, cols), lambda i, w_ref: (0, i, 0))],
            out_specs=pl.BlockSpec((None, tr, cols), lambda i, w_ref: (w_ref[1], i, 0))),
        out_shape=jax.ShapeDtypeStruct((DEPTH, r, cols), F32),
        compiler_params=_cparams("parallel"),
    )(where, own, parts)


def sibling_share_layer(bufs):
    n = len(bufs)

    def body(*refs):
        o_refs = refs[n:2 * n]
        send_sems, recv_sems = refs[2 * n:]
        x, y, c = _place()
        cps = []
        for t in range(n):
            cp = pltpu.make_async_remote_copy(src_ref=o_refs[t].at[c], dst_ref=o_refs[t].at[c], send_sem=send_sems.at[t],
                                              recv_sem=recv_sems.at[t], device_id=(x, y, 1 - c), device_id_type=MESH)
            cp.start()
            cps.append(cp)
        for t in range(n):
            slot = o_refs[t].at[1 - c]
            pltpu.make_async_remote_copy(src_ref=slot, dst_ref=slot, send_sem=send_sems.at[t], recv_sem=recv_sems.at[t],
                                         device_id=(x, y, 1 - c), device_id_type=MESH).wait_recv()
        for cp in cps:
            cp.wait_send()

    return pl.pallas_call(
        body, name="sibling_share_layer",
        in_specs=[ANY] * n, out_specs=[ANY] * n,
        out_shape=[jax.ShapeDtypeStruct(b.shape, b.dtype) for b in bufs],
        input_output_aliases={t: t for t in range(n)},
        scratch_shapes=[pltpu.SemaphoreType.DMA((n,)), pltpu.SemaphoreType.DMA((n,))],
    )(*bufs)


SP_META = 2 * (N_META * D_MODEL // LANES)
SP_NORM = DEPTH * D_MODEL // LANES
SP_RB = DEPTH * N_BUCKETS
SP_SINK = DEPTH * ATT_HEADS
SP_CONV = DEPTH * 3 * BRANCH_WIDTH // LANES
SP_LOSS = 8
SIDE_ROWS = 48
SP_ROWS = SP_META + 2 * SP_NORM + SP_RB + SP_SINK + SP_CONV + SP_LOSS


def sum_small(slots):
    half = SP_META // 2
    rb0 = SP_META + 2 * SP_NORM
    rest_rows = SP_ROWS - SP_META

    def body(s_ref, meta_ref, rest_ref):
        acc = s_ref[0]
        for d in range(1, 8):
            acc = acc + s_ref[d]
        meta_ref[...] = acc[0:half] + acc[half:SP_META]
        rest_ref[...] = acc[SP_META:]
        rest_ref[rb0 - SP_META:rb0 - SP_META + N_BUCKETS, :] = (
            acc[rb0:rb0 + N_BUCKETS] + acc[rb0 + N_BUCKETS:rb0 + 2 * N_BUCKETS])

    vm = pl.BlockSpec(memory_space=pltpu.VMEM)
    return pl.pallas_call(
        body, name="sum_small",
        in_specs=[vm], out_specs=[vm, vm],
        out_shape=[jax.ShapeDtypeStruct((half, LANES), F32), jax.ShapeDtypeStruct((rest_rows, LANES), F32)],
    )(slots)


def local_step(x, loss_target, meta_full, rel_bias, norm_pre, conv_w_full, attn_sinks, norm_post, weights_of, mid_fwd,
               grads_done, bwd_done):
    nb, seq, _ = x.shape
    nc = seq // BLOCK + 1
    lp = nc * BLOCK
    rows = nb * lp
    pad = jnp.zeros((nb, PAD_FRONT, D_MODEL), F32)
    meta = jnp.broadcast_to(meta_full[None], (nb, N_META, D_MODEL))
    h0 = jnp.concatenate([pad, meta, x], axis=1).reshape(rows, D_MODEL)
    target = jnp.pad(loss_target, ((0, 0), (BLOCK, 0), (0, 0))).reshape(rows, D_MODEL)
    cosf, sinf = _rot_tables(lp)
    bkt = jnp.asarray(_bucket_table())

    g_pre = norm_pre.reshape(DEPTH, 1, D_MODEL)
    g_post = norm_post.reshape(DEPTH, 1, D_MODEL)
    order = lambda token: bkt if token is None else token

    acts = []
    h = h0
    for l in range(DEPTH):
        w_in, token = weights_of(l, h)
        hb, p_abc = norm_matmul(h, g_pre, l, w_in, 0, N_ABC_TILES, order(token))
        p_m = matmul_cols(hb, w_in, N_ABC_TILES, N_M_TILES)
        br, states = mixers_fwd(p_abc, cosf, sinf, bkt, rel_bias, attn_sinks, conv_w_full, l, nb, nc)
        (w_br, w_out), token = mid_fwd(l, br)
        acts.append((h, hb, p_abc, p_m, br, states, w_in, w_br, w_out))
        if l < DEPTH - 1:
            h = merge_fwd(h, br, p_m, w_br, w_out, g_post, l, order(token))
        else:
            assert token is None
            loss_part, d_h = merge_fwd_loss(h, br, p_m, w_br, w_out, g_post, l, target, lp)

    small = [None] * DEPTH
    token = None
    for l in reversed(range(DEPTH)):
        h_in, hb, p_abc, p_m, br, states, w_in, w_br, w_out = acts[l]
        d_br, d_m, d_gpost, g_wbr, g_wout = merge_bwd(d_h, br, p_m, w_br, w_out, g_post, l, order(token))
        d_abc, d_rb, d_sk, d_cw = mixers_bwd(p_abc, d_br, states, cosf, sinf, bkt, rel_bias,
                                             attn_sinks, conv_w_full, l, nb, nc)
        g_win = proj_wgrad(hb, d_abc, d_m)
        token = grads_done(l, [g_win, g_wbr, g_wout])
        d_h, d_gpre, *d_x = proj_dgrad(d_abc, d_m, w_in, h_in, g_pre, l, d_h, order(token),
                                       examples=nb if l == 0 else None)
        token = bwd_done(l, d_h)
        small[l] = (d_gpre[0], d_gpost[0], d_rb, d_sk, d_cw[0:3])

    d_h3 = d_h.reshape(nb, lp, D_MODEL)
    d_x, = d_x
    d_meta = d_h3[:, PAD_FRONT:BLOCK]
    sp = jnp.concatenate([
        d_meta.reshape(-1, LANES),
        jnp.stack([small[l][0] for l in range(DEPTH)]).reshape(-1, LANES),
        jnp.stack([small[l][1] for l in range(DEPTH)]).reshape(-1, LANES),
        jnp.concatenate([small[l][2] for l in range(DEPTH)], axis=0),
        jnp.concatenate([small[l][3] for l in range(DEPTH)], axis=0),
        jnp.stack([small[l][4] for l in range(DEPTH)]).reshape(-1, LANES),
        loss_part], axis=0)
    return d_x, sp


def kernel(x, meta_tokens, rel_bias, norm_pre, w_in, conv_w, attn_sinks, w_branch, w_out, norm_post, loss_target, m_meta_tokens, m_rel_bias, m_norm_pre, m_w_in, m_conv_w, m_attn_sinks, m_w_branch, m_w_out, m_norm_post, v_meta_tokens, v_rel_bias, v_norm_pre, v_w_in, v_conv_w, v_attn_sinks, v_w_branch, v_w_out, v_norm_post):
    assert x.shape[0] == 2 and SP_META == 2 * N_META * D_MODEL // LANES
    px, py, pc = _place()
    chip = 2 * px + py

    c_arr = jnp.reshape(pc, (1,)).astype(jnp.int32)
    where = jnp.stack([chip, pc]).astype(jnp.int32)
    tr_ = lambda a: jnp.swapaxes(a, 1, 2)
    w3 = [tr_(w_in), w_branch.reshape(DEPTH, N_BRANCH * BRANCH_WIDTH, SHARD_D), w_out]
    halves = lambda a: a.reshape(2, a.shape[0] // 2, a.shape[1])

    def as_weights(bufs):
        a_in, a_br, a_out = bufs
        return (a_in.reshape(PROJ_WIDTH, D_MODEL), a_br.reshape(N_CHIPS, N_BRANCH, BRANCH_WIDTH, SHARD_D),
                a_out.reshape(D_MODEL, D_MODEL))

    n_meta_rows = N_META * SHARD_D // LANES
    side = jnp.concatenate([meta_tokens.reshape(-1), conv_w.reshape(-1)]).reshape(-1, LANES)
    side = jnp.concatenate([side, jnp.zeros((SIDE_ROWS - side.shape[0], LANES), F32)], axis=0)
    slots = [[_own_slot(halves(w[l].astype(BF16)), chip) for w in w3] for l in range(DEPTH)]
    slots[0][0] = cast_into_slot(w3[0], 0, where)
    send0, recv0, flying0, _ = gather_start([_own_slot(halves(side), chip)] + slots[0], where)
    side_chips = gather_forward(gather_wait(flying0[:1], send0, recv0, where))[0].reshape(N_CHIPS, SIDE_ROWS, LANES)
    meta_full = jnp.moveaxis(side_chips[:, :n_meta_rows].reshape(N_CHIPS, N_META, SHARD_D), 0, 1).reshape(N_META, D_MODEL)
    conv_full = jnp.moveaxis(side_chips[:, n_meta_rows:n_meta_rows + 6].reshape(N_CHIPS, DEPTH, 3, LANES), 0, 2).reshape(DEPTH, 3, BRANCH_WIDTH)
    inbound = {}

    def weights_of(l, h):
        if l == 0:
            inbound[0] = gather_forward(gather_wait(flying0[1:2], send0, recv0, h, first=1))
            inbound[1] = gather_start(slots[1], inbound[0][0])
            return inbound[0][0].reshape(PROJ_WIDTH, D_MODEL), inbound[1][3]
        send, recv, thru = inbound[1]
        inbound[1] = as_weights(forward_wait(thru, send, recv, h))
        return inbound[1][0], None

    def mid_fwd(l, br):
        if l == 0:
            rest = gather_forward(gather_wait(flying0[2:], send0, recv0, br, first=2))
            send, recv, flying1, _ = inbound[1]
            send, recv, thru, started = forward_start(gather_wait(flying1, send, recv, rest[0]))
            inbound[1] = (send, recv, thru)
            return as_weights(inbound[0] + rest)[1:], started
        return inbound[1][1:], None

    reduced = [None] * DEPTH
    flying = {}

    def finish_reduce(l, after):
        partials, parts = scatter_wait(*flying[l], after)
        reduced[l] = sibling_share_layer([sum_chips(a, p, where) for a, p in zip(partials, parts)])

    def start_scatter(l, full, others):
        send, recv, thru, lands, started = scatter_start([add_own_half(g, o, c_arr) for g, o in zip(full, others)])
        flying[l] = (thru, lands, send, recv)
        return started

    m3 = [tr_(m_w_in), m_w_branch.reshape(w3[1].shape), m_w_out]
    v3 = [tr_(v_w_in), v_w_branch.reshape(w3[1].shape), v_w_out]
    big = [None] * 3

    def adamw_of(l, after):
        for t in range(3):
            big[t] = adamw_layer(w3[t], reduced[l][t].reshape(w3[t].shape[1:]), m3[t], v3[t], l, big[t], after)
            after = big[t][1]

    def grads_done(l, grads):
        full = [g.reshape(N_CHIPS, 2, g.size // (2 * N_CHIPS * g.shape[-1]), g.shape[-1]) for g in grads]
        if l == 0:
            finish_reduce(1, grads[0])
        send, recv, thru, lands, started = swap_start(full, where if l == 1 else reduced[1][0])
        if l == 1:
            flying["swap"] = (thru, lands, send, recv)
            return started
        adamw_of(1, started)
        return start_scatter(0, *swap_wait(thru, lands, send, recv, big[2][1]))

    def bwd_done(l, d_h):
        if l == 1:
            return start_scatter(1, *swap_wait(*flying["swap"], d_h))
        return None

    d_x, sp = local_step(x, loss_target, meta_full, rel_bias, norm_pre, conv_full, attn_sinks, norm_post,
                         weights_of, mid_fwd, grads_done, bwd_done)
    finish_reduce(0, sp)

    s_send, s_recv, s_buf, s_started = small_start(sp, 4 * px + 2 * py + pc, reduced[0][0])

    adamw_of(0, s_started)
    g_in, *u_in = [tr_(a) for a in big[0]]
    g_br, *u_br = [a.reshape(w_branch.shape) for a in big[1]]
    g_out, *u_out = big[2]

    meta_rows, rest = sum_small(small_wait(s_buf, s_send, s_recv, big[2][1]))
    o = 0
    g_meta_full = meta_rows.reshape(N_META, D_MODEL)
    g_norm_pre = rest[o:o + SP_NORM].reshape(DEPTH, D_MODEL); o += SP_NORM
    g_norm_post = rest[o:o + SP_NORM].reshape(DEPTH, D_MODEL); o += SP_NORM
    g_rel_bias = rest[o:o + N_BUCKETS, :ATT_HEADS]; o += SP_RB
    g_sinks = rest[o:o + SP_SINK, 0].reshape(DEPTH, ATT_HEADS); o += SP_SINK
    g_conv_full = rest[o:o + SP_CONV].reshape(DEPTH, 3, BRANCH_WIDTH); o += SP_CONV
    loss = rest[o, 0]
    g_meta = lax.dynamic_slice_in_dim(g_meta_full, chip * SHARD_D, SHARD_D, axis=1)
    g_conv = lax.dynamic_slice_in_dim(g_conv_full, chip * LANES, LANES, axis=2)

    swap01 = lambda arrays: tuple(jnp.swapaxes(a, 0, 1) for a in arrays)
    smalls = [(meta_tokens, g_meta, m_meta_tokens, v_meta_tokens),
              swap01((rel_bias, g_rel_bias, m_rel_bias, v_rel_bias)),
              (norm_pre, g_norm_pre, m_norm_pre, v_norm_pre),
              swap01((conv_w, g_conv, m_conv_w, v_conv_w)),
              (attn_sinks, g_sinks, m_attn_sinks, v_attn_sinks),
              (norm_post, g_norm_post, m_norm_post, v_norm_post)]
    u_meta, u_rb, u_npre, u_conv, u_sink, u_npost = adamw_small(smalls)
    u_rb, u_conv = swap01(u_rb), swap01(u_conv)

    grads = [g_meta, g_rel_bias, g_norm_pre, g_in, g_conv, g_sinks, g_br, g_out, g_norm_post]
    upd = [u_meta, u_rb, u_npre, u_in, u_conv, u_sink, u_br, u_out, u_npost]
    return (loss, d_x, *grads, *[u[0] for u in upd], *[u[1] for u in upd], *[u[2] for u in upd])
```

```python
import math

import numpy as np
import jax
import jax.numpy as jnp
from jax import lax
from jax.experimental import pallas as pl
from jax.experimental.pallas import tpu as pltpu

F32 = jnp.float32
BF16 = jnp.bfloat16
MESH = pl.DeviceIdType.MESH

D_MODEL = 1024
DEPTH = 2
N_META = 16
BLOCK = 128
PAD_FRONT = BLOCK - N_META
ATT_HEADS = 8
ATT_HEAD_DIM = 64
N_BUCKETS = 32
MAX_EXACT = 16
MAX_DISTANCE = 128
RET_HEADS = 4
ROT_BASE = 10000.0
N_BRANCH = 3
BRANCH_WIDTH = 512
PROJ_WIDTH = 8448
ABC_WIDTH = 5376
MERGE_WIDTH = N_BRANCH * D_MODEL
RMS_EPS = 1e-6
GN_EPS = 1e-6
NEG_INF = -1e30
ATT_SCALE = ATT_HEAD_DIM ** -0.5
RET_SCALE = BLOCK ** -0.5
LOG_GAMMA = tuple(math.log1p(-(2.0 ** (-5.0 - h))) for h in range(RET_HEADS))

C_AQ, C_AK, C_AV, C_AG = 0, 512, 640, 768
C_RQ, C_RK, C_RV, C_RG = 1280, 1792, 2304, 2816
C_CB, C_CC, C_CX, C_CG = 3328, 3840, 4352, 4864

ADAM_LR = 0.001
ADAM_B1 = 0.9
ADAM_B2 = 0.999
ADAM_EPS = 1e-08
ADAM_WD = 0.01
ADAM_STEP = 10

N_CHIPS = 4
SHARD_D = D_MODEL // N_CHIPS
LANES = 128

VMEM_LIMIT = 56 * 1024 * 1024
COL_TILE = 768
ROW_TILE = 1088
PROJ_ROW_TILE = 2176


def _cparams(*sem):
    return pltpu.CompilerParams(dimension_semantics=sem, vmem_limit_bytes=VMEM_LIMIT)


def _nt(a, b):
    return lax.dot_general(a, b, (((1,), (1,)), ((), ())), preferred_element_type=F32)


def _tn(a, b):
    return lax.dot_general(a, b, (((0,), (0,)), ((), ())), preferred_element_type=F32)


def _nn(a, b):
    return jnp.dot(a, b, preferred_element_type=F32)


def _sigmoid(x):
    return 0.5 * jnp.tanh(0.5 * x) + 0.5


def _silu(x):
    return x * _sigmoid(x)


def _dsilu(x):
    s = _sigmoid(x)
    return s * (1.0 + x * (1.0 - s))


def _bucket_table():
    r = np.arange(BLOCK)[:, None]
    c = np.arange(2 * BLOCK)[None, :]
    n = np.maximum(BLOCK + r - c, 0)
    nf = np.maximum(n, 1).astype(np.float32)
    large = MAX_EXACT + (np.log(nf / MAX_EXACT) / math.log(MAX_DISTANCE / MAX_EXACT)
                         * (N_BUCKETS - MAX_EXACT)).astype(np.int32)
    large = np.minimum(large, N_BUCKETS - 1)
    return np.where(n < MAX_EXACT, n, large).astype(np.int32)


def _rot_tables(lp):
    half = BLOCK // 2
    pos = (jnp.arange(lp) - PAD_FRONT).astype(F32)
    theta = 1.0 / (ROT_BASE ** jnp.linspace(0.0, 1.0, half, dtype=F32))
    ang = pos[:, None] * theta[None, :]
    cos, sin = jnp.cos(ang), jnp.sin(ang)
    return jnp.concatenate([cos, cos], axis=1), jnp.concatenate([-sin, sin], axis=1)


def norm_matmul(x2d, g, layer, w, col0_blocks, n_col_blocks, after):
    t = x2d.shape[0]
    tm = PROJ_ROW_TILE if t % PROJ_ROW_TILE == 0 else BLOCK

    def body(x_ref, g_ref, w_ref, after_ref, hb_ref, o_ref):
        @pl.when(pl.program_id(1) == 0)
        def _():
            x = x_ref[...]
            r = lax.rsqrt(jnp.mean(x * x, axis=-1, keepdims=True) + RMS_EPS)
            hb_ref[...] = (x * r * g_ref[...]).astype(BF16)

        o_ref[...] = _nt(hb_ref[...], w_ref[...])

    return pl.pallas_call(
        body, name="norm_matmul",
        grid=(t // tm, n_col_blocks),
        in_specs=[pl.BlockSpec((tm, D_MODEL), lambda i, j: (i, 0)),
                  pl.BlockSpec((None, 1, D_MODEL), lambda i, j: (layer, 0, 0)),
                  pl.BlockSpec((COL_TILE, D_MODEL), lambda i, j: (j + col0_blocks, 0)),
                  ANY],
        out_specs=[pl.BlockSpec((tm, D_MODEL), lambda i, j: (i, 0)),
                   pl.BlockSpec((tm, COL_TILE), lambda i, j: (i, j))],
        out_shape=[jax.ShapeDtypeStruct((t, D_MODEL), BF16),
                   jax.ShapeDtypeStruct((t, n_col_blocks * COL_TILE), F32)],
        compiler_params=_cparams("parallel", "arbitrary"),
    )(x2d, g, w, after)


def matmul_cols(a, w, col0_blocks, n_col_blocks):
    t, k = a.shape
    tm = PROJ_ROW_TILE if t % PROJ_ROW_TILE == 0 else BLOCK

    def body(a_ref, w_ref, o_ref):
        o_ref[...] = _nt(a_ref[...], w_ref[...]).astype(BF16)

    return pl.pallas_call(
        body, name="matmul_cols",
        grid=(t // tm, n_col_blocks),
        in_specs=[pl.BlockSpec((tm, k), lambda i, j: (i, 0)),
                  pl.BlockSpec((COL_TILE, k), lambda i, j: (j + col0_blocks, 0))],
        out_specs=pl.BlockSpec((tm, COL_TILE), lambda i, j: (i, j)),
        out_shape=jax.ShapeDtypeStruct((t, n_col_blocks * COL_TILE), BF16),
        compiler_params=_cparams("parallel", "arbitrary"),
    )(a, w)


class _Widened:
    def __init__(self, ref):
        self.ref = ref

    def __getitem__(self, idx):
        return self.ref[idx].astype(F32)


def _build_bias(bkt_ref, rb_ref, bias_s):
    bkt = bkt_ref[...]
    for h in range(ATT_HEADS):
        acc = jnp.zeros((BLOCK, 2 * BLOCK), F32)
        for b in range(N_BUCKETS):
            acc = jnp.where(bkt == b, rb_ref[b, h], acc)
        bias_s[h] = acc


def _band_mask(n):
    r = lax.broadcasted_iota(jnp.int32, (BLOCK, 2 * BLOCK), 0)
    c = lax.broadcasted_iota(jnp.int32, (BLOCK, 2 * BLOCK), 1)
    key_pos = (n - 1) * BLOCK + c
    return (c > r) & (c <= r + BLOCK) & (key_pos >= PAD_FRONT)


def _split_heads(kv, kh):
    lane = lax.broadcasted_iota(jnp.int32, kv.shape, 1)
    if kh == 0:
        lo = jnp.where(lane < ATT_HEAD_DIM, kv, 0.0)
        hi = pltpu.roll(lo, ATT_HEAD_DIM, 1)
    else:
        hi = jnp.where(lane >= ATT_HEAD_DIM, kv, 0.0)
        lo = pltpu.roll(hi, ATT_HEAD_DIM, 1)
    return lo, hi


def _merge_heads(acc_lo, acc_hi, kh):
    lane = lax.broadcasted_iota(jnp.int32, acc_lo.shape, 1)
    if kh == 0:
        return jnp.where(lane < ATT_HEAD_DIM, acc_lo + pltpu.roll(acc_hi, ATT_HEAD_DIM, 1), 0.0)
    return jnp.where(lane >= ATT_HEAD_DIM, acc_hi + pltpu.roll(acc_lo, ATT_HEAD_DIM, 1), 0.0)


def _softmax_of(qk, bias_h, mask, sink_h):
    s = qk + bias_h
    s = jnp.where(mask, s, NEG_INF)
    m = jnp.maximum(jnp.max(s, axis=-1, keepdims=True), sink_h)
    p = jnp.exp(s - m)
    es = jnp.exp(sink_h - m)
    inv = 1.0 / (jnp.sum(p, axis=-1, keepdims=True) + es)
    return p * inv, es * inv


def _rot(t, cosf, sinf):
    return t * cosf + pltpu.roll(t, BLOCK // 2, 1) * sinf


def _rot_t(d, cosf, sinf):
    return d * cosf + pltpu.roll(d * sinf, BLOCK // 2, 1)


def _decay_tables(h):
    lg = LOG_GAMMA[h]
    i = lax.broadcasted_iota(jnp.int32, (BLOCK, BLOCK), 0)
    j = lax.broadcasted_iota(jnp.int32, (BLOCK, BLOCK), 1)
    diff = (i - j).astype(F32)
    dm = jnp.where(diff >= 0, jnp.exp(diff * lg), 0.0)
    row = lax.broadcasted_iota(jnp.int32, (BLOCK, 1), 0).astype(F32)
    zeta = jnp.exp((BLOCK - 1 - row) * lg)
    xi = jnp.exp((row + 1.0) * lg)
    return dm, zeta, xi, math.exp(BLOCK * lg)


def _valid_col(n):
    row = lax.broadcasted_iota(jnp.int32, (BLOCK, 1), 0)
    return ((n * BLOCK + row) >= PAD_FRONT).astype(F32)


def _shift_down(cur, prev, k):
    row = lax.broadcasted_iota(jnp.int32, cur.shape, 0)
    return jnp.where(row >= k, pltpu.roll(cur, k, 0), pltpu.roll(prev, k, 0))


def _shift_up(cur, nxt, k):
    row = lax.broadcasted_iota(jnp.int32, cur.shape, 0)
    return jnp.where(row < BLOCK - k, pltpu.roll(cur, BLOCK - k, 0), pltpu.roll(nxt, BLOCK - k, 0))


def mixers_fwd(proj, cosf, sinf, bkt, rel_bias, sinks, conv_w, layer, nb, nc):
    def body(p_ref, cos_ref, sin_ref, bkt_ref, rb_ref, sk_ref, cw_ref, br_ref, st_ref,
             bias_s, kv_s, state_s, u_s):
        p_ref = _Widened(p_ref)
        n = pl.program_id(0)

        @pl.when(n == 0)
        def _():
            _build_bias(bkt_ref, rb_ref, bias_s)
            kv_s[:, 0:BLOCK, :] = jnp.zeros((nb, BLOCK, 2 * BLOCK), F32)
            state_s[...] = jnp.zeros_like(state_s)
            u_s[...] = jnp.zeros_like(u_s)

        valid = _valid_col(n)
        mask = _band_mask(n)
        ex = range(nb)

        for b in ex:
            kv_s[b, BLOCK:2 * BLOCK, :] = p_ref[b, :, C_AK:C_AK + 2 * BLOCK]
        for kh in range(2):
            ks = [[t.astype(BF16) for t in _split_heads(kv_s[b, :, 0:BLOCK], kh)] for b in ex]
            vs = [[t.astype(BF16) for t in _split_heads(kv_s[b, :, BLOCK:2 * BLOCK], kh)] for b in ex]
            pairs = [(b, 2 * kh + jj) for jj in range(2) for b in ex]
            subs = [(b, j, x) for (b, j) in pairs for x in range(2)]
            qb_ = {(b, j): (p_ref[b, :, C_AQ + BLOCK * j:C_AQ + BLOCK * (j + 1)] * ATT_SCALE).astype(BF16)
                   for (b, j) in pairs}
            qk_ = {(b, j, x): _nt(qb_[(b, j)], ks[b][x]) for (b, j, x) in subs}
            pb_ = {}
            for u in subs:
                h = 2 * u[1] + u[2]
                pb_[u] = _softmax_of(qk_[u], bias_s[h], mask, sk_ref[layer, h])[0].astype(BF16)
            o_ = {u: _nn(pb_[u], vs[u[0]][u[2]]) for u in subs}
            for (b, j) in pairs:
                gate = p_ref[b, :, C_AG + BLOCK * j:C_AG + BLOCK * (j + 1)]
                br_ref[b, :, BLOCK * j:BLOCK * (j + 1)] = ((o_[(b, j, 0)] + o_[(b, j, 1)]) * _silu(gate)).astype(BF16)
        for b in ex:
            kv_s[b, 0:BLOCK, :] = kv_s[b, BLOCK:2 * BLOCK, :]

        cosv = cos_ref[...]
        sinv = sin_ref[...]
        tabs = [_decay_tables(h) for h in range(RET_HEADS)]
        units = [(b, h) for h in range(RET_HEADS) for b in ex]
        sl = lambda c0, h: slice(c0 + BLOCK * h, c0 + BLOCK * (h + 1))
        q_, k_, v_, sp_ = {}, {}, {}, {}
        for u in units:
            b, h = u
            q_[u] = _rot(p_ref[b, :, sl(C_RQ, h)], cosv, sinv).astype(BF16)
            k_[u] = (_rot(p_ref[b, :, sl(C_RK, h)], cosv, sinv) * RET_SCALE * valid).astype(BF16)
            v_[u] = p_ref[b, :, sl(C_RV, h)]
            sp_[u] = state_s[b, h]
            st_ref[b, 0, h] = sp_[u]
        qk_ = {u: _nt(q_[u], k_[u]) for u in units}
        qs_ = {u: _nn(q_[u], sp_[u].astype(BF16)) for u in units}
        kv_ = {u: _tn(k_[u], (v_[u] * tabs[u[1]][1]).astype(BF16)) for u in units}
        a_ = {u: (qk_[u] * tabs[u[1]][0]).astype(BF16) for u in units}
        av_ = {u: _nn(a_[u], v_[u].astype(BF16)) for u in units}
        for u in units:
            b, h = u
            o = av_[u] + tabs[h][2] * qs_[u]
            mu = jnp.mean(o, axis=-1, keepdims=True)
            var = jnp.mean(jnp.square(o - mu), axis=-1, keepdims=True)
            oh = (o - mu) * lax.rsqrt(var + GN_EPS)
            gate = p_ref[b, :, sl(C_RG, h)]
            br_ref[b, :, BRANCH_WIDTH + BLOCK * h:BRANCH_WIDTH + BLOCK * (h + 1)] = (oh * _silu(gate)).astype(BF16)
            state_s[b, h] = tabs[h][3] * sp_[u] + kv_[u]

        for b in ex:
            u = p_ref[b, :, C_CC:C_CC + BRANCH_WIDTH] * p_ref[b, :, C_CX:C_CX + BRANCH_WIDTH] * valid
            u_prev = u_s[b]
            y = (cw_ref[0:1, :] * _shift_down(u, u_prev, 2) + cw_ref[1:2, :] * _shift_down(u, u_prev, 1)
                 + cw_ref[2:3, :] * u)
            yc = p_ref[b, :, C_CB:C_CB + BRANCH_WIDTH] * y * _silu(p_ref[b, :, C_CG:C_CG + BRANCH_WIDTH])
            br_ref[b, :, 2 * BRANCH_WIDTH:3 * BRANCH_WIDTH] = yc.astype(BF16)
            u_s[b] = u

    lp = nc * BLOCK
    smem = pl.BlockSpec(memory_space=pltpu.SMEM)
    br, states = pl.pallas_call(
        body, name="mixers_fwd",
        grid=(nc,),
        in_specs=[pl.BlockSpec((nb, BLOCK, ABC_WIDTH), lambda n: (0, n, 0)),
                  pl.BlockSpec((BLOCK, BLOCK), lambda n: (n, 0)),
                  pl.BlockSpec((BLOCK, BLOCK), lambda n: (n, 0)),
                  pl.BlockSpec((BLOCK, 2 * BLOCK), lambda n: (0, 0)),
                  smem, smem,
                  pl.BlockSpec((None, 3, BRANCH_WIDTH), lambda n: (layer, 0, 0))],
        out_specs=[pl.BlockSpec((nb, BLOCK, N_BRANCH * BRANCH_WIDTH), lambda n: (0, n, 0)),
                   pl.BlockSpec((nb, 1, RET_HEADS, BLOCK, BLOCK), lambda n: (0, n, 0, 0, 0))],
        out_shape=[jax.ShapeDtypeStruct((nb, lp, N_BRANCH * BRANCH_WIDTH), BF16),
                   jax.ShapeDtypeStruct((nb, nc, RET_HEADS, BLOCK, BLOCK), F32)],
        scratch_shapes=[pltpu.VMEM((ATT_HEADS, BLOCK, 2 * BLOCK), F32),
                        pltpu.VMEM((nb, 2 * BLOCK, 2 * BLOCK), F32),
                        pltpu.VMEM((nb, RET_HEADS, BLOCK, BLOCK), F32),
                        pltpu.VMEM((nb, BLOCK, BRANCH_WIDTH), F32)],
        compiler_params=_cparams("arbitrary"),
    )(proj.reshape(nb, lp, ABC_WIDTH), cosf, sinf, bkt, rel_bias, sinks, conv_w)
    return br.reshape(nb * lp, N_BRANCH * BRANCH_WIDTH), states


def mixers_bwd(proj, d_br, states, cosf, sinf, bkt, rel_bias, sinks, conv_w, layer, nb, nc):
    def body(p_ref, kvp_ref, cp_ref, dbr_ref, st_ref, cos_ref, sin_ref, bkt_ref, rb_ref, sk_ref, cw_ref,
             dp_ref, drb_ref, dsk_ref, dcw_ref,
             bias_s, dbias_s, dkv_s, g_s, dy_s):
        p_ref, kvp_ref, cp_ref, dbr_ref = [_Widened(r) for r in (p_ref, kvp_ref, cp_ref, dbr_ref)]
        step = pl.program_id(0)
        n = nc - 1 - step
        ex = range(nb)

        @pl.when(step == 0)
        def _():
            _build_bias(bkt_ref, rb_ref, bias_s)
            dbias_s[...] = jnp.zeros_like(dbias_s)
            dsk_ref[...] = jnp.zeros_like(dsk_ref)
            dcw_ref[...] = jnp.zeros_like(dcw_ref)
            drb_ref[...] = jnp.zeros_like(drb_ref)
            dkv_s[...] = jnp.zeros_like(dkv_s)
            g_s[...] = jnp.zeros_like(g_s)
            dy_s[...] = jnp.zeros_like(dy_s)

        valid = _valid_col(n)
        mask = _band_mask(n)
        has_prev = (n > 0).astype(F32)

        k_all, v_all = [], []
        for b in ex:
            kv_prev = kvp_ref[b] * has_prev
            kv_cur = p_ref[b, :, C_AK:C_AK + 2 * BLOCK]
            k_all.append(jnp.concatenate([kv_prev[:, 0:BLOCK], kv_cur[:, 0:BLOCK]], axis=0))
            v_all.append(jnp.concatenate([kv_prev[:, BLOCK:], kv_cur[:, BLOCK:]], axis=0))
        zero2 = jnp.zeros((2 * BLOCK, BLOCK), F32)
        dk_tot = [zero2 for _ in ex]
        dv_tot = [zero2 for _ in ex]
        for kh in range(2):
            ks = [[t.astype(BF16) for t in _split_heads(k_all[b], kh)] for b in ex]
            vs = [[t.astype(BF16) for t in _split_heads(v_all[b], kh)] for b in ex]
            pairs = [(b, 2 * kh + jj) for jj in range(2) for b in ex]
            subs = [(b, j, x) for (b, j) in pairs for x in range(2)]
            qb_, gate_, dya_, do2_ = {}, {}, {}, {}
            for w in pairs:
                b, j = w
                qb_[w] = (p_ref[b, :, C_AQ + BLOCK * j:C_AQ + BLOCK * (j + 1)] * ATT_SCALE).astype(BF16)
                gate_[w] = p_ref[b, :, C_AG + BLOCK * j:C_AG + BLOCK * (j + 1)]
                dya_[w] = dbr_ref[b, :, BLOCK * j:BLOCK * (j + 1)]
                do2_[w] = (dya_[w] * _silu(gate_[w])).astype(BF16)
            qk_ = {(b, j, x): _nt(qb_[(b, j)], ks[b][x]) for (b, j, x) in subs}
            dpm_ = {(b, j, x): _nt(do2_[(b, j)], vs[b][x]) for (b, j, x) in subs}
            pb_, dsb_ = {}, {}
            for u in subs:
                b, j, x = u
                h = 2 * j + x
                p, p_sink = _softmax_of(qk_[u], bias_s[h], mask, sk_ref[layer, h])
                pb_[u] = p.astype(BF16)
                delta = jnp.sum(p * dpm_[u], axis=-1, keepdims=True)
                ds = p * (dpm_[u] - delta)
                dbias_s[h] += ds
                dsk_ref[h:h + 1, :] += jnp.broadcast_to(
                    jnp.sum(-p_sink * delta, axis=0, keepdims=True), (1, BLOCK))
                dsb_[u] = ds.astype(BF16)
            o_ = {u: _nn(pb_[u], vs[u[0]][u[2]]) for u in subs}
            dq_ = {u: _nn(dsb_[u], ks[u[0]][u[2]]) for u in subs}
            dkm_ = {u: _tn(dsb_[u], qb_[(u[0], u[1])]) for u in subs}
            dvm_ = {u: _tn(pb_[u], do2_[(u[0], u[1])]) for u in subs}
            for w in pairs:
                b, j = w
                o2 = o_[(b, j, 0)] + o_[(b, j, 1)]
                dq2 = (dq_[(b, j, 0)] + dq_[(b, j, 1)]) * ATT_SCALE
                dp_ref[b, :, C_AQ + BLOCK * j:C_AQ + BLOCK * (j + 1)] = dq2.astype(BF16)
                dp_ref[b, :, C_AG + BLOCK * j:C_AG + BLOCK * (j + 1)] = (
                    dya_[w] * o2 * _dsilu(gate_[w])).astype(BF16)
            for b in ex:
                j0, j1 = 2 * kh, 2 * kh + 1
                dk_tot[b] = dk_tot[b] + _merge_heads(dkm_[(b, j0, 0)] + dkm_[(b, j1, 0)],
                                                     dkm_[(b, j0, 1)] + dkm_[(b, j1, 1)], kh)
                dv_tot[b] = dv_tot[b] + _merge_heads(dvm_[(b, j0, 0)] + dvm_[(b, j1, 0)],
                                                     dvm_[(b, j0, 1)] + dvm_[(b, j1, 1)], kh)
        for b in ex:
            dp_ref[b, :, C_AK:C_AK + BLOCK] = (dk_tot[b][BLOCK:, :] + dkv_s[b, :, 0:BLOCK]).astype(BF16)
            dp_ref[b, :, C_AV:C_AV + BLOCK] = (dv_tot[b][BLOCK:, :] + dkv_s[b, :, BLOCK:]).astype(BF16)
            dkv_s[b, :, 0:BLOCK] = dk_tot[b][0:BLOCK, :]
            dkv_s[b, :, BLOCK:] = dv_tot[b][0:BLOCK, :]

        cosv = cos_ref[...]
        sinv = sin_ref[...]
        tabs = [_decay_tables(h) for h in range(RET_HEADS)]
        units = [(b, h) for h in range(RET_HEADS) for b in ex]
        sl = lambda c0, h: slice(c0 + BLOCK * h, c0 + BLOCK * (h + 1))
        q_, k_, v_, vb_, sp_ = {}, {}, {}, {}, {}
        for u in units:
            b, h = u
            q_[u] = _rot(p_ref[b, :, sl(C_RQ, h)], cosv, sinv).astype(BF16)
            k_[u] = (_rot(p_ref[b, :, sl(C_RK, h)], cosv, sinv) * RET_SCALE * valid).astype(BF16)
            v_[u] = p_ref[b, :, sl(C_RV, h)]
            vb_[u] = v_[u].astype(BF16)
            sp_[u] = st_ref[b, 0, h].astype(BF16)
        qk_ = {u: _nt(q_[u], k_[u]) for u in units}
        qs_ = {u: _nn(q_[u], sp_[u]) for u in units}
        a_ = {u: (qk_[u] * tabs[u[1]][0]).astype(BF16) for u in units}
        av_ = {u: _nn(a_[u], vb_[u]) for u in units}
        dob_, dxo_ = {}, {}
        for u in units:
            b, h = u
            xi = tabs[h][2]
            o = av_[u] + xi * qs_[u]
            mu = jnp.mean(o, axis=-1, keepdims=True)
            var = jnp.mean(jnp.square(o - mu), axis=-1, keepdims=True)
            rstd = lax.rsqrt(var + GN_EPS)
            oh = (o - mu) * rstd
            gate = p_ref[b, :, sl(C_RG, h)]
            d_yr = dbr_ref[b, :, BRANCH_WIDTH + BLOCK * h:BRANCH_WIDTH + BLOCK * (h + 1)]
            dp_ref[b, :, sl(C_RG, h)] = (d_yr * oh * _dsilu(gate)).astype(BF16)
            doh = d_yr * _silu(gate)
            do = rstd * (doh - jnp.mean(doh, axis=-1, keepdims=True)
                         - oh * jnp.mean(doh * oh, axis=-1, keepdims=True))
            dob_[u] = do.astype(BF16)
            dxo_[u] = (do * xi).astype(BF16)
        dov_ = {u: _nt(dob_[u], vb_[u]) for u in units}
        dv1_ = {u: _tn(a_[u], dob_[u]) for u in units}
        dq1_ = {u: _nt(dxo_[u], sp_[u]) for u in units}
        gq_ = {u: _tn(q_[u], dxo_[u]) for u in units}
        da_, gb_, zv_ = {}, {}, {}
        for u in units:
            b, h = u
            da_[u] = (dov_[u] * tabs[h][0]).astype(BF16)
            g_next = g_s[b, h]
            gb_[u] = g_next.astype(BF16)
            zv_[u] = (v_[u] * tabs[h][1]).astype(BF16)
            g_s[b, h] = tabs[h][3] * g_next + gq_[u]
        dq2_ = {u: _nn(da_[u], k_[u]) for u in units}
        dk1_ = {u: _tn(da_[u], q_[u]) for u in units}
        dk2_ = {u: _nt(zv_[u], gb_[u]) for u in units}
        dv2_ = {u: _nn(k_[u], gb_[u]) for u in units}
        for u in units:
            b, h = u
            dp_ref[b, :, sl(C_RQ, h)] = _rot_t(dq2_[u] + dq1_[u], cosv, sinv).astype(BF16)
            dp_ref[b, :, sl(C_RK, h)] = _rot_t((dk1_[u] + dk2_[u]) * (RET_SCALE * valid), cosv, sinv).astype(BF16)
            dp_ref[b, :, sl(C_RV, h)] = (dv1_[u] + tabs[h][1] * dv2_[u]).astype(BF16)

        w0, w1, w2 = cw_ref[0:1, :], cw_ref[1:2, :], cw_ref[2:3, :]
        for b in ex:
            cb = p_ref[b, :, C_CB:C_CB + BRANCH_WIDTH]
            cc = p_ref[b, :, C_CC:C_CC + BRANCH_WIDTH]
            cx = p_ref[b, :, C_CX:C_CX + BRANCH_WIDTH]
            cg = p_ref[b, :, C_CG:C_CG + BRANCH_WIDTH]
            u = cc * cx * valid
            u_prev = (cp_ref[b, :, 0:BRANCH_WIDTH] * cp_ref[b, :, BRANCH_WIDTH:2 * BRANCH_WIDTH]
                      * (_valid_col(n - 1) * has_prev))
            u1 = _shift_down(u, u_prev, 1)
            u2 = _shift_down(u, u_prev, 2)
            y = w0 * u2 + w1 * u1 + w2 * u
            d_yc = dbr_ref[b, :, 2 * BRANCH_WIDTH:3 * BRANCH_WIDTH]
            sg = _silu(cg)
            dp_ref[b, :, C_CB:C_CB + BRANCH_WIDTH] = (d_yc * y * sg).astype(BF16)
            dp_ref[b, :, C_CG:C_CG + BRANCH_WIDTH] = (d_yc * cb * y * _dsilu(cg)).astype(BF16)
            dy = d_yc * cb * sg
            dy_next = dy_s[b]
            du = (w2 * dy + w1 * _shift_up(dy, dy_next, 1) + w0 * _shift_up(dy, dy_next, 2)) * valid
            dp_ref[b, :, C_CC:C_CC + BRANCH_WIDTH] = (du * cx).astype(BF16)
            dp_ref[b, :, C_CX:C_CX + BRANCH_WIDTH] = (du * cc).astype(BF16)
            dcw_ref[0:1, :] += jnp.sum(dy * u2, axis=0, keepdims=True)
            dcw_ref[1:2, :] += jnp.sum(dy * u1, axis=0, keepdims=True)
            dcw_ref[2:3, :] += jnp.sum(dy * u, axis=0, keepdims=True)
            dy_s[b] = dy

        @pl.when(step == nc - 1)
        def _():
            bkt = bkt_ref[...]
            row = lax.broadcasted_iota(jnp.int32, (N_BUCKETS, BLOCK), 0)
            lane = lax.broadcasted_iota(jnp.int32, (N_BUCKETS, BLOCK), 1)

            def one_bucket(bk, acc):
                sel = bkt == bk
                for h in range(ATT_HEADS):
                    t = jnp.where(sel, dbias_s[h], 0.0)
                    s = jnp.sum(jnp.sum(t, axis=1, keepdims=True), axis=0, keepdims=True)
                    acc = acc + jnp.where((row == bk) & (lane == h), jnp.broadcast_to(s, acc.shape), 0.0)
                return acc

            drb_ref[...] = lax.fori_loop(0, N_BUCKETS, one_bucket, jnp.zeros((N_BUCKETS, BLOCK), F32))

    lp = nc * BLOCK
    smem = pl.BlockSpec(memory_space=pltpu.SMEM)
    blk = lambda s: nc - 1 - s
    prev = lambda s: jnp.maximum(nc - 2 - s, 0)
    proj3 = proj.reshape(nb, lp, ABC_WIDTH)
    res = pl.pallas_call(
        body, name="mixers_bwd",
        grid=(nc,),
        in_specs=[pl.BlockSpec((nb, BLOCK, ABC_WIDTH), lambda s: (0, blk(s), 0)),
                  pl.BlockSpec((nb, BLOCK, 2 * BLOCK), lambda s: (0, prev(s), C_AK // (2 * BLOCK))),
                  pl.BlockSpec((nb, BLOCK, 1280), lambda s: (0, prev(s), C_CC // 1280)),
                  pl.BlockSpec((nb, BLOCK, N_BRANCH * BRANCH_WIDTH), lambda s: (0, blk(s), 0)),
                  pl.BlockSpec((nb, 1, RET_HEADS, BLOCK, BLOCK), lambda s: (0, blk(s), 0, 0, 0)),
                  pl.BlockSpec((BLOCK, BLOCK), lambda s: (blk(s), 0)),
                  pl.BlockSpec((BLOCK, BLOCK), lambda s: (blk(s), 0)),
                  pl.BlockSpec((BLOCK, 2 * BLOCK), lambda s: (0, 0)),
                  smem, smem,
                  pl.BlockSpec((None, 3, BRANCH_WIDTH), lambda s: (layer, 0, 0))],
        out_specs=[pl.BlockSpec((nb, BLOCK, ABC_WIDTH), lambda s: (0, blk(s), 0)),
                   pl.BlockSpec((N_BUCKETS, BLOCK), lambda s: (0, 0)),
                   pl.BlockSpec((ATT_HEADS, BLOCK), lambda s: (0, 0)),
                   pl.BlockSpec((8, BRANCH_WIDTH), lambda s: (0, 0))],
        out_shape=[jax.ShapeDtypeStruct((nb, lp, ABC_WIDTH), BF16),
                   jax.ShapeDtypeStruct((N_BUCKETS, BLOCK), F32),
                   jax.ShapeDtypeStruct((ATT_HEADS, BLOCK), F32),
                   jax.ShapeDtypeStruct((8, BRANCH_WIDTH), F32)],
        scratch_shapes=[pltpu.VMEM((ATT_HEADS, BLOCK, 2 * BLOCK), F32),
                        pltpu.VMEM((ATT_HEADS, BLOCK, 2 * BLOCK), F32),
                        pltpu.VMEM((nb, BLOCK, 2 * BLOCK), F32),
                        pltpu.VMEM((nb, RET_HEADS, BLOCK, BLOCK), F32),
                        pltpu.VMEM((nb, BLOCK, BRANCH_WIDTH), F32)],
        compiler_params=_cparams("arbitrary"),
    )(proj3, proj3, proj3, d_br.reshape(nb, lp, N_BRANCH * BRANCH_WIDTH), states, cosf, sinf, bkt, rel_bias, sinks,
      conv_w)
    return (res[0].reshape(nb * lp, ABC_WIDTH),) + tuple(res[1:])


MERGE_TILE = 256
MERGE_FWD_TILE = 544


def _merge_forward(br_ref, m_ref, wb_ref, wo_ref):
    bo, gates = [], []
    mixed_pre = None
    for g in range(N_BRANCH):
        br_g = br_ref[:, BRANCH_WIDTH * g:BRANCH_WIDTH * (g + 1)]
        bo_g = jnp.concatenate([_nn(br_g, wb_ref[p, g]) for p in range(N_CHIPS)], axis=1)
        gate_g = _sigmoid(m_ref[:, D_MODEL * g:D_MODEL * (g + 1)].astype(F32))
        bo.append(bo_g)
        gates.append(gate_g)
        mixed_pre = gate_g * bo_g if mixed_pre is None else mixed_pre + gate_g * bo_g
    mixed = _nn(mixed_pre.astype(BF16), wo_ref[...])
    r = lax.rsqrt(jnp.mean(mixed * mixed, axis=-1, keepdims=True) + RMS_EPS)
    return bo, gates, mixed_pre, mixed, r


def merge_fwd(x2d, br, pm, wb, wo, g_post, layer, after):
    t = x2d.shape[0]
    tm = MERGE_FWD_TILE if t % MERGE_FWD_TILE == 0 else BLOCK

    def body(x_ref, br_ref, m_ref, wb_ref, wo_ref, g_ref, after_ref, o_ref):
        _, _, _, mixed, r = _merge_forward(br_ref, m_ref, wb_ref, wo_ref)
        o_ref[...] = x_ref[...] + mixed * r * g_ref[...]

    return pl.pallas_call(
        body, name="merge_fwd",
        grid=(t // tm,),
        in_specs=[pl.BlockSpec((tm, D_MODEL), lambda i: (i, 0)),
                  pl.BlockSpec((tm, N_BRANCH * BRANCH_WIDTH), lambda i: (i, 0)),
                  pl.BlockSpec((tm, MERGE_WIDTH), lambda i: (i, 0)),
                  pl.BlockSpec((N_CHIPS, N_BRANCH, BRANCH_WIDTH, SHARD_D), lambda i: (0, 0, 0, 0)),
                  pl.BlockSpec((D_MODEL, D_MODEL), lambda i: (0, 0)),
                  pl.BlockSpec((None, 1, D_MODEL), lambda i: (layer, 0, 0)),
                  ANY],
        out_specs=pl.BlockSpec((tm, D_MODEL), lambda i: (i, 0)),
        out_shape=jax.ShapeDtypeStruct((t, D_MODEL), F32),
        compiler_params=_cparams("parallel"),
    )(x2d, br, pm, wb, wo, g_post, after)


def merge_fwd_loss(x2d, br, pm, wb, wo, g_post, layer, target, lp):
    t = x2d.shape[0]
    tm = MERGE_FWD_TILE if t % MERGE_FWD_TILE == 0 else BLOCK

    def body(x_ref, br_ref, m_ref, wb_ref, wo_ref, g_ref, t_ref, l_ref, d_ref):
        i = pl.program_id(0)

        @pl.when(i == 0)
        def _():
            l_ref[...] = jnp.zeros_like(l_ref)

        _, _, _, mixed, r = _merge_forward(br_ref, m_ref, wb_ref, wo_ref)
        y = x_ref[...] + mixed * r * g_ref[...]
        row = i * tm + lax.broadcasted_iota(jnp.int32, (tm, 1), 0)
        e = jnp.where(row % lp >= BLOCK, y - t_ref[...], 0.0)
        d_ref[...] = e * (1.0 / D_MODEL)
        s = jnp.sum(jnp.sum(e * e, axis=1, keepdims=True), axis=0, keepdims=True)
        l_ref[...] += jnp.broadcast_to(s * (0.5 / D_MODEL), l_ref.shape)

    return pl.pallas_call(
        body, name="merge_fwd_loss",
        grid=(t // tm,),
        in_specs=[pl.BlockSpec((tm, D_MODEL), lambda i: (i, 0)),
                  pl.BlockSpec((tm, N_BRANCH * BRANCH_WIDTH), lambda i: (i, 0)),
                  pl.BlockSpec((tm, MERGE_WIDTH), lambda i: (i, 0)),
                  pl.BlockSpec((N_CHIPS, N_BRANCH, BRANCH_WIDTH, SHARD_D), lambda i: (0, 0, 0, 0)),
                  pl.BlockSpec((D_MODEL, D_MODEL), lambda i: (0, 0)),
                  pl.BlockSpec((None, 1, D_MODEL), lambda i: (layer, 0, 0)),
                  pl.BlockSpec((tm, D_MODEL), lambda i: (i, 0))],
        out_specs=[pl.BlockSpec((8, BLOCK), lambda i: (0, 0)),
                   pl.BlockSpec((tm, D_MODEL), lambda i: (i, 0))],
        out_shape=[jax.ShapeDtypeStruct((8, BLOCK), F32),
                   jax.ShapeDtypeStruct((t, D_MODEL), F32)],
        compiler_params=_cparams("arbitrary"),
    )(x2d, br, pm, wb, wo, g_post, target)


def merge_bwd(d_out, br, pm, wb, wo, g_post, layer, after):
    t = d_out.shape[0]
    tm = MERGE_TILE if t % MERGE_TILE == 0 else BLOCK

    def body(do_ref, br_ref, m_ref, wb_ref, wo_ref, g_ref, after_ref, dbr_ref, dm_ref, dg_ref, dwb_ref, dwo_ref):

        @pl.when(pl.program_id(0) == 0)
        def _():
            dwb_ref[...] = jnp.zeros_like(dwb_ref)
            dwo_ref[...] = jnp.zeros_like(dwo_ref)
            dg_ref[...] = jnp.zeros_like(dg_ref)

        bo, gates, mixed_pre, mixed, r = _merge_forward(br_ref, m_ref, wb_ref, wo_ref)
        d_o = do_ref[...]
        nh = mixed * r
        dg_ref[0:1, :] += jnp.sum(d_o * nh, axis=0, keepdims=True)
        dn = d_o * g_ref[...]
        d_mixed = (r * (dn - nh * jnp.mean(dn * nh, axis=-1, keepdims=True))).astype(BF16)
        dwo_ref[...] += _tn(mixed_pre.astype(BF16), d_mixed)
        d_pre = _nt(d_mixed, wo_ref[...])
        for g in range(N_BRANCH):
            br_g = br_ref[:, BRANCH_WIDTH * g:BRANCH_WIDTH * (g + 1)]
            d_bo = (d_pre * gates[g]).astype(BF16)
            dm_ref[:, D_MODEL * g:D_MODEL * (g + 1)] = (
                d_pre * bo[g] * gates[g] * (1.0 - gates[g])).astype(BF16)
            d_br_g = None
            for p in range(N_CHIPS):
                d_bo_p = d_bo[:, SHARD_D * p:SHARD_D * (p + 1)]
                part = _nt(d_bo_p, wb_ref[p, g])
                d_br_g = part if d_br_g is None else d_br_g + part
                dwb_ref[p, g] += _tn(br_g, d_bo_p)
            dbr_ref[:, BRANCH_WIDTH * g:BRANCH_WIDTH * (g + 1)] = d_br_g.astype(BF16)

    return pl.pallas_call(
        body, name="merge_bwd",
        grid=(t // tm,),
        in_specs=[pl.BlockSpec((tm, D_MODEL), lambda i: (i, 0)),
                  pl.BlockSpec((tm, N_BRANCH * BRANCH_WIDTH), lambda i: (i, 0)),
                  pl.BlockSpec((tm, MERGE_WIDTH), lambda i: (i, 0)),
                  pl.BlockSpec((N_CHIPS, N_BRANCH, BRANCH_WIDTH, SHARD_D), lambda i: (0, 0, 0, 0)),
                  pl.BlockSpec((D_MODEL, D_MODEL), lambda i: (0, 0)),
                  pl.BlockSpec((None, 1, D_MODEL), lambda i: (layer, 0, 0)),
                  ANY],
        out_specs=[pl.BlockSpec((tm, N_BRANCH * BRANCH_WIDTH), lambda i: (i, 0)),
                   pl.BlockSpec((tm, MERGE_WIDTH), lambda i: (i, 0)),
                   pl.BlockSpec((8, D_MODEL), lambda i: (0, 0)),
                   pl.BlockSpec((N_CHIPS, N_BRANCH, BRANCH_WIDTH, SHARD_D), lambda i: (0, 0, 0, 0)),
                   pl.BlockSpec((D_MODEL, D_MODEL), lambda i: (0, 0))],
        out_shape=[jax.ShapeDtypeStruct((t, N_BRANCH * BRANCH_WIDTH), BF16),
                   jax.ShapeDtypeStruct((t, MERGE_WIDTH), BF16),
                   jax.ShapeDtypeStruct((8, D_MODEL), F32),
                   jax.ShapeDtypeStruct((N_CHIPS, N_BRANCH, BRANCH_WIDTH, SHARD_D), F32),
                   jax.ShapeDtypeStruct((D_MODEL, D_MODEL), F32)],
        compiler_params=_cparams("arbitrary"),
    )(d_out, br, pm, wb, wo, g_post, after)


N_ABC_TILES = ABC_WIDTH // COL_TILE
N_M_TILES = MERGE_WIDTH // COL_TILE


def proj_dgrad(d_abc, d_m, w, x2d, g, layer, d_out, after, examples=None):
    t = x2d.shape[0]
    tm = ROW_TILE if t % ROW_TILE == 0 else BLOCK
    nk = N_ABC_TILES + N_M_TILES
    n_i = t // tm
    if examples is not None:
        lp = t // examples
        assert lp % tm == 0
        tiles_per_example = lp // tm

    def body(da_ref, dm_ref, w_ref, x_ref, g_ref, do_ref, after_ref, dx_ref, dg_ref, *rest):
        acc = rest[-3] if examples is not None else rest[-1]
        i = pl.program_id(0)
        k = pl.program_id(1)

        def for_copy_of_tile(tile, act):
            seq_ref, stage, sem = rest[0], rest[-2], rest[-1]
            e, j = tile // tiles_per_example, tile % tiles_per_example
            if tm > BLOCK:
                @pl.when(j == 0)
                def _():
                    act(pltpu.make_async_copy(stage.at[pl.ds(BLOCK, tm - BLOCK)],
                                              seq_ref.at[e, pl.ds(0, tm - BLOCK)], sem.at[0]))

            @pl.when(j > 0)
            def _():
                act(pltpu.make_async_copy(stage, seq_ref.at[e, pl.ds(pl.multiple_of(j * tm - BLOCK, 8), tm)],
                                          sem.at[0]))

        @pl.when((i == 0) & (k == 0))
        def _():
            dg_ref[...] = jnp.zeros_like(dg_ref)

        @pl.when(k == 0)
        def _():
            acc[...] = jnp.zeros_like(acc)

        @pl.when(k < N_ABC_TILES)
        def _():
            acc[...] += _nn(da_ref[...], w_ref[...])

        @pl.when(k >= N_ABC_TILES)
        def _():
            acc[...] += _nn(dm_ref[...], w_ref[...])

        @pl.when(k == nk - 1)
        def _():
            x = x_ref[...]
            r = lax.rsqrt(jnp.mean(x * x, axis=-1, keepdims=True) + RMS_EPS)
            nh = x * r
            dh = acc[...]
            dg_ref[0:1, :] += jnp.sum(dh * nh, axis=0, keepdims=True)
            dn = dh * g_ref[...]
            dx = do_ref[...] + r * (dn - nh * jnp.mean(dn * nh, axis=-1, keepdims=True))
            dx_ref[...] = dx
            if examples is not None:
                @pl.when(i > 0)
                def _():
                    for_copy_of_tile(i - 1, lambda copy: copy.wait())

                rest[-2][...] = dx
                for_copy_of_tile(i, lambda copy: copy.start())

                @pl.when(i == n_i - 1)
                def _():
                    for_copy_of_tile(i, lambda copy: copy.wait())

    out_specs = [pl.BlockSpec((tm, D_MODEL), lambda i, k: (i, 0)),
                 pl.BlockSpec((8, D_MODEL), lambda i, k: (0, 0))]
    out_shape = [jax.ShapeDtypeStruct((t, D_MODEL), F32),
                 jax.ShapeDtypeStruct((8, D_MODEL), F32)]
    scratch_shapes = [pltpu.VMEM((tm, D_MODEL), F32)]
    if examples is not None:
        out_specs.append(ANY)
        out_shape.append(jax.ShapeDtypeStruct((examples, lp - BLOCK, D_MODEL), F32))
        scratch_shapes += [pltpu.VMEM((tm, D_MODEL), F32), pltpu.SemaphoreType.DMA((1,))]

    return pl.pallas_call(
        body, name="proj_dgrad",
        grid=(n_i, nk),
        in_specs=[pl.BlockSpec((tm, COL_TILE), lambda i, k: (i, jnp.minimum(k, N_ABC_TILES - 1))),
                  pl.BlockSpec((tm, COL_TILE), lambda i, k: (i, jnp.maximum(k - N_ABC_TILES, 0))),
                  pl.BlockSpec((COL_TILE, D_MODEL), lambda i, k: (k, 0)),
                  pl.BlockSpec((tm, D_MODEL), lambda i, k: (i, 0)),
                  pl.BlockSpec((None, 1, D_MODEL), lambda i, k: (layer, 0, 0)),
                  pl.BlockSpec((tm, D_MODEL), lambda i, k: (i, 0)),
                  ANY],
        out_specs=out_specs, out_shape=out_shape, scratch_shapes=scratch_shapes,
        compiler_params=_cparams("arbitrary", "arbitrary"),
    )(d_abc, d_m, w, x2d, g, d_out, after)


def proj_wgrad(hb, d_abc, d_m):
    t = hb.shape[0]
    nj = N_ABC_TILES + N_M_TILES

    def body(h_ref, da_ref, dm_ref, o_ref):
        j = pl.program_id(0)

        @pl.when(j < N_ABC_TILES)
        def _():
            o_ref[...] = _tn(da_ref[...], h_ref[...])

        @pl.when(j >= N_ABC_TILES)
        def _():
            o_ref[...] = _tn(dm_ref[...], h_ref[...])

    return pl.pallas_call(
        body, name="proj_wgrad",
        grid=(nj,),
        in_specs=[pl.BlockSpec((t, D_MODEL), lambda j: (0, 0)),
                  pl.BlockSpec((t, COL_TILE), lambda j: (0, jnp.minimum(j, N_ABC_TILES - 1))),
                  pl.BlockSpec((t, COL_TILE), lambda j: (0, jnp.maximum(j - N_ABC_TILES, 0)))],
        out_specs=pl.BlockSpec((COL_TILE, D_MODEL), lambda j: (j, 0)),
        out_shape=jax.ShapeDtypeStruct((PROJ_WIDTH, D_MODEL), F32),
        compiler_params=_cparams("arbitrary"),
    )(hb, d_abc, d_m)


def _adamw_math(w, g, m, v):
    m = ADAM_B1 * m + (1.0 - ADAM_B1) * g
    v = ADAM_B2 * v + (1.0 - ADAM_B2) * jnp.square(g)
    m_hat = m / (1.0 - ADAM_B1 ** ADAM_STEP)
    v_hat = v / (1.0 - ADAM_B2 ** ADAM_STEP)
    delta = -ADAM_LR * (m_hat / (jnp.sqrt(v_hat) + ADAM_EPS) + ADAM_WD * w)
    return delta, m, v


def adamw_layer(w, g, m, v, layer, acc, after):
    _, r, c = w.shape
    tr = _row_tile(r)

    def body(*refs):
        w_ref, g_ref, m_ref, v_ref = refs[:4]
        go_ref, d_ref, mo_ref, vo_ref = refs[-4:]
        g_val = g_ref[...]
        d, m_new, v_new = _adamw_math(w_ref[...], g_val, m_ref[...], v_ref[...])
        go_ref[...] = g_val
        d_ref[...] = d
        mo_ref[...] = m_new
        vo_ref[...] = v_new

    slab = pl.BlockSpec((None, tr, c), lambda i: (layer, i, 0))
    ins = [w, g, m, v, after]
    in_specs = [slab, pl.BlockSpec((tr, c), lambda i: (i, 0)), slab, slab, ANY]
    aliases = {}
    if acc is not None:
        ins += list(acc)
        in_specs += [ANY] * 4
        aliases = {5 + i: i for i in range(4)}
    return pl.pallas_call(
        body, name="adamw_layer",
        grid=(r // tr,),
        in_specs=in_specs, out_specs=[slab] * 4,
        out_shape=[jax.ShapeDtypeStruct(w.shape, F32)] * 4,
        input_output_aliases=aliases,
        compiler_params=_cparams("parallel"),
    )(*ins)


def adamw_small(params):
    k = len(params)

    def body(*refs):
        ins, outs = refs[:4 * k], refs[4 * k:]
        for i in range(k):
            d, m_new, v_new = _adamw_math(*[r[...] for r in ins[4 * i:4 * i + 4]])
            outs[3 * i][...] = d
            outs[3 * i + 1][...] = m_new
            outs[3 * i + 2][...] = v_new

    flat = [a for p in params for a in p]
    vm = pl.BlockSpec(memory_space=pltpu.VMEM)
    out_shape = [jax.ShapeDtypeStruct(p[0].shape, F32) for p in params for _ in range(3)]
    res = pl.pallas_call(
        body, name="adamw_small",
        in_specs=[vm] * len(flat), out_specs=[vm] * len(out_shape), out_shape=out_shape,
    )(*flat)
    return [tuple(res[3 * i:3 * i + 3]) for i in range(k)]


ANY = pl.BlockSpec(memory_space=pl.ANY)


def _place():
    return lax.axis_index("x"), lax.axis_index("y"), lax.axis_index("c")


HBM = pl.BlockSpec(memory_space=pltpu.HBM)
SEM = pl.BlockSpec(memory_space=pltpu.SEMAPHORE)
EFFECT = pltpu.SideEffectType.DATAFLOW_SIDE_EFFECTING


def _other_chips(x, y):
    return [(1 - x, y), (x, 1 - y), (1 - x, 1 - y)]


def _own_slot(shard, chip):
    buf = lax.empty((N_CHIPS,) + shard.shape, shard.dtype)
    return lax.dynamic_update_slice(buf, shard[None], (chip, 0, 0, 0))


def _hbm(a):
    return pltpu.with_memory_space_constraint(a, pltpu.HBM)


def gather_start(bufs, after):
    n = len(bufs)

    def body(*refs):
        g_refs = refs[:n]
        send_sems, recv_sems = refs[n + 1], refs[n + 2]
        token = refs[-1]
        x, y, c = _place()
        me_p = 2 * x + y
        for t in range(n):
            for k, (qx, qy) in enumerate(_other_chips(x, y)):
                slab = g_refs[t].at[me_p, c]
                pltpu.make_async_remote_copy(src_ref=slab, dst_ref=slab, send_sem=send_sems.at[3 * t + k],
                                             recv_sem=recv_sems.at[3 * t + k], device_id=(qx, qy, c),
                                             device_id_type=MESH).start()
        token[...] = jnp.zeros_like(token)

    res = pl.pallas_call(
        body, name="gather_start",
        in_specs=[HBM] * n + [ANY],
        out_specs=[SEM, SEM] + [HBM] * n + [pl.BlockSpec(memory_space=pltpu.VMEM)],
        out_shape=[pltpu.SemaphoreType.DMA((3 * n,)), pltpu.SemaphoreType.DMA((3 * n,))]
        + [pltpu.HBM(b.shape, b.dtype) for b in bufs] + [jax.ShapeDtypeStruct((8, LANES), F32)],
        input_output_aliases={t: 2 + t for t in range(n)},
        compiler_params=pltpu.CompilerParams(has_side_effects=EFFECT),
    )(*[_hbm(b) for b in bufs], after)
    return res[0], res[1], list(res[2:2 + n]), res[-1]


def gather_wait(bufs, send_sems, recv_sems, after, first=0):
    n = len(bufs)

    def body(*refs):
        g_refs = refs[:n]
        send_sems, recv_sems = refs[n], refs[n + 1]
        x, y, c = _place()
        me_p = 2 * x + y
        for t in range(n):
            for k, (qx, qy) in enumerate(_other_chips(x, y)):
                s = 3 * (first + t) + k
                cp = pltpu.make_async_remote_copy(src_ref=g_refs[t].at[me_p, c], dst_ref=g_refs[t].at[2 * qx + qy, c],
                                                  send_sem=send_sems.at[s], recv_sem=recv_sems.at[s],
                                                  device_id=(qx, qy, c), device_id_type=MESH)
                cp.wait_send()
                cp.wait_recv()

    return pl.pallas_call(
        body, name="gather_wait",
        in_specs=[HBM] * n + [SEM, SEM, ANY],
        out_specs=[HBM] * n,
        out_shape=[pltpu.HBM(b.shape, b.dtype) for b in bufs],
        input_output_aliases={t: t for t in range(n)},
        compiler_params=pltpu.CompilerParams(has_side_effects=EFFECT),
    )(*bufs, send_sems, recv_sems, after)


def gather_forward(bufs):
    n = len(bufs)

    def body(*refs):
        g_refs = refs[n:2 * n]
        send_sems, recv_sems = refs[2 * n:]
        x, y, c = _place()
        sibling = (x, y, 1 - c)
        chips = _other_chips(x, y)
        passed = []
        for t in range(n):
            for k, (qx, qy) in enumerate(chips):
                slab = g_refs[t].at[2 * qx + qy, c]
                fwd = pltpu.make_async_remote_copy(src_ref=slab, dst_ref=slab, send_sem=send_sems.at[3 * t + k],
                                                   recv_sem=recv_sems.at[3 * t + k], device_id=sibling,
                                                   device_id_type=MESH)
                fwd.start()
                passed.append(fwd)
        for t in range(n):
            for k, (qx, qy) in enumerate(chips):
                slab = g_refs[t].at[2 * qx + qy, 1 - c]
                pltpu.make_async_remote_copy(src_ref=slab, dst_ref=slab, send_sem=send_sems.at[3 * t + k],
                                             recv_sem=recv_sems.at[3 * t + k], device_id=sibling,
                                             device_id_type=MESH).wait_recv()
        for cp in passed:
            cp.wait_send()

    return pl.pallas_call(
        body, name="gather_forward",
        in_specs=[ANY] * n, out_specs=[ANY] * n,
        out_shape=[jax.ShapeDtypeStruct(b.shape, b.dtype) for b in bufs],
        input_output_aliases={t: t for t in range(n)},
        scratch_shapes=[pltpu.SemaphoreType.DMA((3 * n,)), pltpu.SemaphoreType.DMA((3 * n,))],
    )(*bufs)


def forward_start(bufs):
    n = len(bufs)

    def body(*refs):
        g_refs = refs[:n]
        send_sems, recv_sems = refs[n], refs[n + 1]
        token = refs[-1]
        x, y, c = _place()
        for t in range(n):
            for k, (qx, qy) in enumerate(_other_chips(x, y)):
                slab = g_refs[t].at[2 * qx + qy, c]
                pltpu.make_async_remote_copy(src_ref=slab, dst_ref=slab, send_sem=send_sems.at[3 * t + k],
                                             recv_sem=recv_sems.at[3 * t + k], device_id=(x, y, 1 - c),
                                             device_id_type=MESH).start()
        token[...] = jnp.zeros_like(token)

    res = pl.pallas_call(
        body, name="forward_start",
        in_specs=[HBM] * n,
        out_specs=[SEM, SEM] + [HBM] * n + [pl.BlockSpec(memory_space=pltpu.VMEM)],
        out_shape=[pltpu.SemaphoreType.DMA((3 * n,)), pltpu.SemaphoreType.DMA((3 * n,))]
        + [pltpu.HBM(b.shape, b.dtype) for b in bufs] + [jax.ShapeDtypeStruct((8, LANES), F32)],
        input_output_aliases={t: 2 + t for t in range(n)},
        compiler_params=pltpu.CompilerParams(has_side_effects=EFFECT),
    )(*[_hbm(b) for b in bufs])
    return res[0], res[1], list(res[2:2 + n]), res[-1]


def forward_wait(bufs, send_sems, recv_sems, after):
    n = len(bufs)

    def body(*refs):
        g_refs = refs[:n]
        send_sems, recv_sems = refs[n], refs[n + 1]
        x, y, c = _place()
        for t in range(n):
            for k, (qx, qy) in enumerate(_other_chips(x, y)):
                cp = pltpu.make_async_remote_copy(src_ref=g_refs[t].at[2 * qx + qy, c],
                                                  dst_ref=g_refs[t].at[2 * qx + qy, 1 - c],
                                                  send_sem=send_sems.at[3 * t + k], recv_sem=recv_sems.at[3 * t + k],
                                                  device_id=(x, y, 1 - c), device_id_type=MESH)
                cp.wait_send()
                cp.wait_recv()

    return pl.pallas_call(
        body, name="forward_wait",
        in_specs=[HBM] * n + [SEM, SEM, ANY],
        out_specs=[HBM] * n,
        out_shape=[pltpu.HBM(b.shape, b.dtype) for b in bufs],
        input_output_aliases={t: t for t in range(n)},
        compiler_params=pltpu.CompilerParams(has_side_effects=EFFECT),
    )(*bufs, send_sems, recv_sems, after)


def small_start(pack, me, after):
    buf = lax.dynamic_update_slice(lax.empty((8,) + pack.shape, pack.dtype), pack[None], (me, 0, 0))

    def body(b_ref, after_ref, send_sems, recv_sems, thru, token):
        x, y, c = _place()
        slot = b_ref.at[4 * x + 2 * y + c]
        for k in range(1, 8):
            peer = (x ^ ((k >> 2) & 1), y ^ ((k >> 1) & 1), c ^ (k & 1))
            pltpu.make_async_remote_copy(src_ref=slot, dst_ref=slot, send_sem=send_sems.at[k - 1],
                                         recv_sem=recv_sems.at[k - 1], device_id=peer, device_id_type=MESH).start()
        token[...] = jnp.zeros_like(token)

    return pl.pallas_call(
        body, name="small_start",
        in_specs=[HBM, ANY],
        out_specs=[SEM, SEM, HBM, pl.BlockSpec(memory_space=pltpu.VMEM)],
        out_shape=[pltpu.SemaphoreType.DMA((7,)), pltpu.SemaphoreType.DMA((7,)), pltpu.HBM(buf.shape, buf.dtype),
                   jax.ShapeDtypeStruct((8, LANES), F32)],
        input_output_aliases={0: 2},
        compiler_params=pltpu.CompilerParams(has_side_effects=EFFECT),
    )(_hbm(buf), after)


def small_wait(buf, send_sems, recv_sems, after):
    def body(b_ref, send_sems, recv_sems, after_ref, thru):
        x, y, c = _place()
        mine = b_ref.at[4 * x + 2 * y + c]
        for k in range(1, 8):
            peer = (x ^ ((k >> 2) & 1), y ^ ((k >> 1) & 1), c ^ (k & 1))
            cp = pltpu.make_async_remote_copy(src_ref=mine, dst_ref=b_ref.at[4 * peer[0] + 2 * peer[1] + peer[2]],
                                              send_sem=send_sems.at[k - 1], recv_sem=recv_sems.at[k - 1],
                                              device_id=peer, device_id_type=MESH)
            cp.wait_send()
            cp.wait_recv()

    return pl.pallas_call(
        body, name="small_wait",
        in_specs=[HBM, SEM, SEM, ANY], out_specs=HBM,
        out_shape=pltpu.HBM(buf.shape, buf.dtype),
        input_output_aliases={0: 0},
        compiler_params=pltpu.CompilerParams(has_side_effects=EFFECT),
    )(buf, send_sems, recv_sems, after)


def swap_start(grads, after):
    n = len(grads)

    def body(*refs):
        g_refs, l_refs = refs[:n], refs[n:2 * n]
        send_sems, recv_sems = refs[2 * n + 1], refs[2 * n + 2]
        token = refs[-1]
        x, y, c = _place()
        for t in range(n):
            for p in range(N_CHIPS):
                pltpu.make_async_remote_copy(src_ref=g_refs[t].at[p, 1 - c], dst_ref=l_refs[t].at[p],
                                             send_sem=send_sems.at[N_CHIPS * t + p],
                                             recv_sem=recv_sems.at[N_CHIPS * t + p],
                                             device_id=(x, y, 1 - c), device_id_type=MESH).start()
        token[...] = jnp.zeros_like(token)

    lands = [lax.empty((N_CHIPS,) + g.shape[2:], g.dtype) for g in grads]
    res = pl.pallas_call(
        body, name="swap_start",
        in_specs=[HBM] * (2 * n) + [ANY],
        out_specs=[SEM, SEM] + [HBM] * (2 * n) + [pl.BlockSpec(memory_space=pltpu.VMEM)],
        out_shape=[pltpu.SemaphoreType.DMA((N_CHIPS * n,)), pltpu.SemaphoreType.DMA((N_CHIPS * n,))]
        + [pltpu.HBM(a.shape, a.dtype) for a in grads + lands] + [jax.ShapeDtypeStruct((8, LANES), F32)],
        input_output_aliases={t: 2 + t for t in range(2 * n)},
        compiler_params=pltpu.CompilerParams(has_side_effects=EFFECT),
    )(*[_hbm(a) for a in grads + lands], after)
    return res[0], res[1], list(res[2:2 + n]), list(res[2 + n:2 + 2 * n]), res[-1]


def swap_wait(grads, lands, send_sems, recv_sems, after):
    n = len(grads)

    def body(*refs):
        g_refs, l_refs = refs[:n], refs[n:2 * n]
        send_sems, recv_sems = refs[2 * n], refs[2 * n + 1]
        x, y, c = _place()
        for t in range(n):
            for p in range(N_CHIPS):
                cp = pltpu.make_async_remote_copy(src_ref=g_refs[t].at[p, 1 - c], dst_ref=l_refs[t].at[p],
                                                  send_sem=send_sems.at[N_CHIPS * t + p],
                                                  recv_sem=recv_sems.at[N_CHIPS * t + p],
                                                  device_id=(x, y, 1 - c), device_id_type=MESH)
                cp.wait_send()
                cp.wait_recv()

    res = pl.pallas_call(
        body, name="swap_wait",
        in_specs=[HBM] * (2 * n) + [SEM, SEM, ANY],
        out_specs=[HBM] * (2 * n),
        out_shape=[pltpu.HBM(a.shape, a.dtype) for a in grads + lands],
        input_output_aliases={t: t for t in range(2 * n)},
        compiler_params=pltpu.CompilerParams(has_side_effects=EFFECT),
    )(*grads, *lands, send_sems, recv_sems, after)
    return list(res[:n]), list(res[n:])


def cast_into_slot(w, layer, where):
    _, r, c = w.shape
    rh = r // 2
    steps = 2 if rh % 32 == 0 else 1
    tr = rh // steps

    def body(where_ref, w_ref, o_ref):
        o_ref[...] = w_ref[...].astype(BF16)

    return pl.pallas_call(
        body, name="cast_into_slot",
        grid_spec=pltpu.PrefetchScalarGridSpec(
            num_scalar_prefetch=1, grid=(2, steps),
            in_specs=[pl.BlockSpec((None, tr, c), lambda h, i, where_ref: (layer, h * steps + i, 0))],
            out_specs=pl.BlockSpec((None, None, tr, c), lambda h, i, where_ref: (where_ref[0], h, i, 0))),
        out_shape=jax.ShapeDtypeStruct((N_CHIPS, 2, rh, c), BF16),
        compiler_params=_cparams("arbitrary", "arbitrary"),
    )(where, w)


def _row_tile(r):
    return max(t for t in range(16, 513, 16) if r % t == 0)


def add_own_half(g, other, c_arr):
    _, _, r, cols = g.shape
    tr = _row_tile(r)

    def body(c_ref, a_ref, b_ref, o_ref):
        o_ref[...] = (a_ref[...] + b_ref[...]).astype(BF16)

    return pl.pallas_call(
        body, name="add_own_half",
        grid_spec=pltpu.PrefetchScalarGridSpec(
            num_scalar_prefetch=1, grid=(N_CHIPS, r // tr),
            in_specs=[pl.BlockSpec((None, None, tr, cols), lambda p, i, c_ref: (p, c_ref[0], i, 0)),
                      pl.BlockSpec((None, tr, cols), lambda p, i, c_ref: (p, i, 0))],
            out_specs=pl.BlockSpec((None, tr, cols), lambda p, i, c_ref: (p, i, 0))),
        out_shape=jax.ShapeDtypeStruct((N_CHIPS, r, cols), BF16),
        compiler_params=_cparams("parallel", "parallel"),
    )(c_arr, g, other)


def scatter_start(partials):
    n = len(partials)

    def body(*refs):
        s_refs, l_refs = refs[:n], refs[n:2 * n]
        send_sems, recv_sems = refs[2 * n], refs[2 * n + 1]
        token = refs[-1]
        x, y, c = _place()
        for t in range(n):
            for k, (qx, qy) in enumerate(_other_chips(x, y)):
                pltpu.make_async_remote_copy(src_ref=s_refs[t].at[2 * qx + qy], dst_ref=l_refs[t].at[k],
                                             send_sem=send_sems.at[3 * t + k], recv_sem=recv_sems.at[3 * t + k],
                                             device_id=(qx, qy, c), device_id_type=MESH).start()
        token[...] = jnp.zeros_like(token)

    lands = [lax.empty((3,) + s.shape[1:], s.dtype) for s in partials]
    res = pl.pallas_call(
        body, name="scatter_start",
        in_specs=[HBM] * (2 * n),
        out_specs=[SEM, SEM] + [HBM] * (2 * n) + [pl.BlockSpec(memory_space=pltpu.VMEM)],
        out_shape=[pltpu.SemaphoreType.DMA((3 * n,)), pltpu.SemaphoreType.DMA((3 * n,))]
        + [pltpu.HBM(a.shape, a.dtype) for a in partials + lands] + [jax.ShapeDtypeStruct((8, LANES), F32)],
        input_output_aliases={t: 2 + t for t in range(2 * n)},
        compiler_params=pltpu.CompilerParams(has_side_effects=EFFECT),
    )(*[_hbm(a) for a in partials + lands])
    return res[0], res[1], list(res[2:2 + n]), list(res[2 + n:2 + 2 * n]), res[-1]


def scatter_wait(partials, lands, send_sems, recv_sems, after):
    n = len(partials)

    def body(*refs):
        s_refs, l_refs = refs[:n], refs[n:2 * n]
        send_sems, recv_sems = refs[2 * n], refs[2 * n + 1]
        x, y, c = _place()
        for t in range(n):
            for k, (qx, qy) in enumerate(_other_chips(x, y)):
                cp = pltpu.make_async_remote_copy(src_ref=s_refs[t].at[2 * qx + qy], dst_ref=l_refs[t].at[k],
                                                  send_sem=send_sems.at[3 * t + k], recv_sem=recv_sems.at[3 * t + k],
                                                  device_id=(qx, qy, c), device_id_type=MESH)
                cp.wait_send()
                cp.wait_recv()

    res = pl.pallas_call(
        body, name="scatter_wait",
        in_specs=[HBM] * (2 * n) + [SEM, SEM, ANY],
        out_specs=[HBM] * (2 * n),
        out_shape=[pltpu.HBM(a.shape, a.dtype) for a in partials + lands],
        input_output_aliases={t: t for t in range(2 * n)},
        compiler_params=pltpu.CompilerParams(has_side_effects=EFFECT),
    )(*partials, *lands, send_sems, recv_sems, after)
    return list(res[:n]), list(res[n:])


def sum_chips(own, parts, where):
    _, r, cols = own.shape
    tr = _row_tile(r)

    def body(w_ref, a_ref, p_ref, o_ref):
        acc = a_ref[...].astype(F32)
        for k in range(3):
            acc = acc + p_ref[k].astype(F32)
        o_ref[...] = acc

    return pl.pallas_call(
        body, name="sum_chips",
        grid_spec=pltpu.PrefetchScalarGridSpec(
            num_scalar_prefetch=1, grid=(r // tr,),
            in_specs=[pl.BlockSpec((None, tr, cols), lambda i, w_ref: (w_ref[0], i, 0)),
                      pl.BlockSpec((3, tr, cols), lambda i, w_ref: (0, i, 0))],
            out_specs=pl.BlockSpec((None, tr, cols), lambda i, w_ref: (w_ref[1], i, 0))),
        out_shape=jax.ShapeDtypeStruct((DEPTH, r, cols), F32),
        compiler_params=_cparams("parallel"),
    )(where, own, parts)


def sibling_share_layer(bufs):
    n = len(bufs)

    def body(*refs):
        o_refs = refs[n:2 * n]
        send_sems, recv_sems = refs[2 * n:]
        x, y, c = _place()
        cps = []
        for t in range(n):
            cp = pltpu.make_async_remote_copy(src_ref=o_refs[t].at[c], dst_ref=o_refs[t].at[c], send_sem=send_sems.at[t],
                                              recv_sem=recv_sems.at[t], device_id=(x, y, 1 - c), device_id_type=MESH)
            cp.start()
            cps.append(cp)
        for t in range(n):
            slot = o_refs[t].at[1 - c]
            pltpu.make_async_remote_copy(src_ref=slot, dst_ref=slot, send_sem=send_sems.at[t], recv_sem=recv_sems.at[t],
                                         device_id=(x, y, 1 - c), device_id_type=MESH).wait_recv()
        for cp in cps:
            cp.wait_send()

    return pl.pallas_call(
        body, name="sibling_share_layer",
        in_specs=[ANY] * n, out_specs=[ANY] * n,
        out_shape=[jax.ShapeDtypeStruct(b.shape, b.dtype) for b in bufs],
        input_output_aliases={t: t for t in range(n)},
        scratch_shapes=[pltpu.SemaphoreType.DMA((n,)), pltpu.SemaphoreType.DMA((n,))],
    )(*bufs)


SP_META = 2 * (N_META * D_MODEL // LANES)
SP_NORM = DEPTH * D_MODEL // LANES
SP_RB = DEPTH * N_BUCKETS
SP_SINK = DEPTH * ATT_HEADS
SP_CONV = DEPTH * 3 * BRANCH_WIDTH // LANES
SP_LOSS = 8
SIDE_ROWS = 48
SP_ROWS = SP_META + 2 * SP_NORM + SP_RB + SP_SINK + SP_CONV + SP_LOSS


def sum_small(slots):
    half = SP_META // 2
    rb0 = SP_META + 2 * SP_NORM
    rest_rows = SP_ROWS - SP_META

    def body(s_ref, meta_ref, rest_ref):
        acc = s_ref[0]
        for d in range(1, 8):
            acc = acc + s_ref[d]
        meta_ref[...] = acc[0:half] + acc[half:SP_META]
        rest_ref[...] = acc[SP_META:]
        rest_ref[rb0 - SP_META:rb0 - SP_META + N_BUCKETS, :] = (
            acc[rb0:rb0 + N_BUCKETS] + acc[rb0 + N_BUCKETS:rb0 + 2 * N_BUCKETS])

    vm = pl.BlockSpec(memory_space=pltpu.VMEM)
    return pl.pallas_call(
        body, name="sum_small",
        in_specs=[vm], out_specs=[vm, vm],
        out_shape=[jax.ShapeDtypeStruct((half, LANES), F32), jax.ShapeDtypeStruct((rest_rows, LANES), F32)],
    )(slots)


def local_step(x, loss_target, meta_full, rel_bias, norm_pre, conv_w_full, attn_sinks, norm_post, weights_of, mid_fwd,
               grads_done, bwd_done):
    nb, seq, _ = x.shape
    nc = seq // BLOCK + 1
    lp = nc * BLOCK
    rows = nb * lp
    pad = jnp.zeros((nb, PAD_FRONT, D_MODEL), F32)
    meta = jnp.broadcast_to(meta_full[None], (nb, N_META, D_MODEL))
    h0 = jnp.concatenate([pad, meta, x], axis=1).reshape(rows, D_MODEL)
    target = jnp.pad(loss_target, ((0, 0), (BLOCK, 0), (0, 0))).reshape(rows, D_MODEL)
    cosf, sinf = _rot_tables(lp)
    bkt = jnp.asarray(_bucket_table())

    g_pre = norm_pre.reshape(DEPTH, 1, D_MODEL)
    g_post = norm_post.reshape(DEPTH, 1, D_MODEL)
    order = lambda token: bkt if token is None else token

    acts = []
    h = h0
    for l in range(DEPTH):
        w_in, token = weights_of(l, h)
        hb, p_abc = norm_matmul(h, g_pre, l, w_in, 0, N_ABC_TILES, order(token))
        p_m = matmul_cols(hb, w_in, N_ABC_TILES, N_M_TILES)
        br, states = mixers_fwd(p_abc, cosf, sinf, bkt, rel_bias, attn_sinks, conv_w_full, l, nb, nc)
        (w_br, w_out), token = mid_fwd(l, br)
        acts.append((h, hb, p_abc, p_m, br, states, w_in, w_br, w_out))
        if l < DEPTH - 1:
            h = merge_fwd(h, br, p_m, w_br, w_out, g_post, l, order(token))
        else:
            assert token is None
            loss_part, d_h = merge_fwd_loss(h, br, p_m, w_br, w_out, g_post, l, target, lp)

    small = [None] * DEPTH
    token = None
    for l in reversed(range(DEPTH)):
        h_in, hb, p_abc, p_m, br, states, w_in, w_br, w_out = acts[l]
        d_br, d_m, d_gpost, g_wbr, g_wout = merge_bwd(d_h, br, p_m, w_br, w_out, g_post, l, order(token))
        d_abc, d_rb, d_sk, d_cw = mixers_bwd(p_abc, d_br, states, cosf, sinf, bkt, rel_bias,
                                             attn_sinks, conv_w_full, l, nb, nc)
        g_win = proj_wgrad(hb, d_abc, d_m)
        token = grads_done(l, [g_win, g_wbr, g_wout])
        d_h, d_gpre, *d_x = proj_dgrad(d_abc, d_m, w_in, h_in, g_pre, l, d_h, order(token),
                                       examples=nb if l == 0 else None)
        token = bwd_done(l, d_h)
        small[l] = (d_gpre[0], d_gpost[0], d_rb, d_sk, d_cw[0:3])

    d_h3 = d_h.reshape(nb, lp, D_MODEL)
    d_x, = d_x
    d_meta = d_h3[:, PAD_FRONT:BLOCK]
    sp = jnp.concatenate([
        d_meta.reshape(-1, LANES),
        jnp.stack([small[l][0] for l in range(DEPTH)]).reshape(-1, LANES),
        jnp.stack([small[l][1] for l in range(DEPTH)]).reshape(-1, LANES),
        jnp.concatenate([small[l][2] for l in range(DEPTH)], axis=0),
        jnp.concatenate([small[l][3] for l in range(DEPTH)], axis=0),
        jnp.stack([small[l][4] for l in range(DEPTH)]).reshape(-1, LANES),
        loss_part], axis=0)
    return d_x, sp


def kernel(x, meta_tokens, rel_bias, norm_pre, w_in, conv_w, attn_sinks, w_branch, w_out, norm_post, loss_target, m_meta_tokens, m_rel_bias, m_norm_pre, m_w_in, m_conv_w, m_attn_sinks, m_w_branch, m_w_out, m_norm_post, v_meta_tokens, v_rel_bias, v_norm_pre, v_w_in, v_conv_w, v_attn_sinks, v_w_branch, v_w_out, v_norm_post):
    assert x.shape[0] == 2 and SP_META == 2 * N_META * D_MODEL // LANES
    px, py, pc = _place()
    chip = 2 * px + py

    c_arr = jnp.reshape(pc, (1,)).astype(jnp.int32)
    where = jnp.stack([chip, pc]).astype(jnp.int32)
    tr_ = lambda a: jnp.swapaxes(a, 1, 2)
    w3 = [tr_(w_in), w_branch.reshape(DEPTH, N_BRANCH * BRANCH_WIDTH, SHARD_D), w_out]
    halves = lambda a: a.reshape(2, a.shape[0] // 2, a.shape[1])

    def as_weights(bufs):
        a_in, a_br, a_out = bufs
        return (a_in.reshape(PROJ_WIDTH, D_MODEL), a_br.reshape(N_CHIPS, N_BRANCH, BRANCH_WIDTH, SHARD_D),
                a_out.reshape(D_MODEL, D_MODEL))

    n_meta_rows = N_META * SHARD_D // LANES
    side = jnp.concatenate([meta_tokens.reshape(-1), conv_w.reshape(-1)]).reshape(-1, LANES)
    side = jnp.concatenate([side, jnp.zeros((SIDE_ROWS - side.shape[0], LANES), F32)], axis=0)
    slots = [[cast_into_slot(w3[0], l, where)] + [_own_slot(halves(w[l].astype(BF16)), chip) for w in w3[1:]]
             for l in range(DEPTH)]
    send0, recv0, flying0, _ = gather_start([_own_slot(halves(side), chip)] + slots[0], where)
    side_chips = gather_forward(gather_wait(flying0[:1], send0, recv0, where))[0].reshape(N_CHIPS, SIDE_ROWS, LANES)
    meta_full = jnp.moveaxis(side_chips[:, :n_meta_rows].reshape(N_CHIPS, N_META, SHARD_D), 0, 1).reshape(N_META, D_MODEL)
    conv_full = jnp.moveaxis(side_chips[:, n_meta_rows:n_meta_rows + 6].reshape(N_CHIPS, DEPTH, 3, LANES), 0, 2).reshape(DEPTH, 3, BRANCH_WIDTH)
    inbound = {}

    def weights_of(l, h):
        if l == 0:
            inbound[0] = gather_forward(gather_wait(flying0[1:2], send0, recv0, h, first=1))
            inbound[1] = gather_start(slots[1], inbound[0][0])
            return inbound[0][0].reshape(PROJ_WIDTH, D_MODEL), inbound[1][3]
        send, recv, thru = inbound[1]
        inbound[1] = as_weights(forward_wait(thru, send, recv, h))
        return inbound[1][0], None

    def mid_fwd(l, br):
        if l == 0:
            rest = gather_forward(gather_wait(flying0[2:], send0, recv0, br, first=2))
            send, recv, flying1, _ = inbound[1]
            send, recv, thru, started = forward_start(gather_wait(flying1, send, recv, rest[0]))
            inbound[1] = (send, recv, thru)
            return as_weights(inbound[0] + rest)[1:], started
        return inbound[1][1:], None

    reduced = [None] * DEPTH
    flying = {}

    def finish_reduce(l, after):
        partials, parts = scatter_wait(*flying[l], after)
        reduced[l] = sibling_share_layer([sum_chips(a, p, where) for a, p in zip(partials, parts)])

    def start_scatter(l, full, others):
        send, recv, thru, lands, started = scatter_start([add_own_half(g, o, c_arr) for g, o in zip(full, others)])
        flying[l] = (thru, lands, send, recv)
        return started

    m3 = [tr_(m_w_in), m_w_branch.reshape(w3[1].shape), m_w_out]
    v3 = [tr_(v_w_in), v_w_branch.reshape(w3[1].shape), v_w_out]
    big = [None] * 3

    def adamw_of(l, after):
        for t in range(3):
            big[t] = adamw_layer(w3[t], reduced[l][t].reshape(w3[t].shape[1:]), m3[t], v3[t], l, big[t], after)
            after = big[t][1]

    def grads_done(l, grads):
        full = [g.reshape(N_CHIPS, 2, g.size // (2 * N_CHIPS * g.shape[-1]), g.shape[-1]) for g in grads]
        if l == 0:
            finish_reduce(1, grads[0])
        send, recv, thru, lands, started = swap_start(full, where if l == 1 else reduced[1][0])
        if l == 1:
            flying["swap"] = (thru, lands, send, recv)
            return started
        adamw_of(1, started)
        return start_scatter(0, *swap_wait(thru, lands, send, recv, big[2][1]))

    def bwd_done(l, d_h):
        if l == 1:
            return start_scatter(1, *swap_wait(*flying["swap"], d_h))
        return None

    d_x, sp = local_step(x, loss_target, meta_full, rel_bias, norm_pre, conv_full, attn_sinks, norm_post,
                         weights_of, mid_fwd, grads_done, bwd_done)
    finish_reduce(0, sp)

    s_send, s_recv, s_buf, s_started = small_start(sp, 4 * px + 2 * py + pc, reduced[0][0])

    adamw_of(0, s_started)
    g_in, *u_in = [tr_(a) for a in big[0]]
    g_br, *u_br = [a.reshape(w_branch.shape) for a in big[1]]
    g_out, *u_out = big[2]

    meta_rows, rest = sum_small(small_wait(s_buf, s_send, s_recv, big[2][1]))
    o = 0
    g_meta_full = meta_rows.reshape(N_META, D_MODEL)
    g_norm_pre = rest[o:o + SP_NORM].reshape(DEPTH, D_MODEL); o += SP_NORM
    g_norm_post = rest[o:o + SP_NORM].reshape(DEPTH, D_MODEL); o += SP_NORM
    g_rel_bias = rest[o:o + N_BUCKETS, :ATT_HEADS]; o += SP_RB
    g_sinks = rest[o:o + SP_SINK, 0].reshape(DEPTH, ATT_HEADS); o += SP_SINK
    g_conv_full = rest[o:o + SP_CONV].reshape(DEPTH, 3, BRANCH_WIDTH); o += SP_CONV
    loss = rest[o, 0]
    g_meta = lax.dynamic_slice_in_dim(g_meta_full, chip * SHARD_D, SHARD_D, axis=1)
    g_conv = lax.dynamic_slice_in_dim(g_conv_full, chip * LANES, LANES, axis=2)

    swap01 = lambda arrays: tuple(jnp.swapaxes(a, 0, 1) for a in arrays)
    smalls = [(meta_tokens, g_meta, m_meta_tokens, v_meta_tokens),
              swap01((rel_bias, g_rel_bias, m_rel_bias, v_rel_bias)),
              (norm_pre, g_norm_pre, m_norm_pre, v_norm_pre),
              swap01((conv_w, g_conv, m_conv_w, v_conv_w)),
              (attn_sinks, g_sinks, m_attn_sinks, v_attn_sinks),
              (norm_post, g_norm_post, m_norm_post, v_norm_post)]
    u_meta, u_rb, u_npre, u_conv, u_sink, u_npost = adamw_small(smalls)
    u_rb, u_conv = swap01(u_rb), swap01(u_conv)

    grads = [g_meta, g_rel_bias, g_norm_pre, g_in, g_conv, g_sinks, g_br, g_out, g_norm_post]
    upd = [u_meta, u_rb, u_npre, u_in, u_conv, u_sink, u_br, u_out, u_npost]
    return (loss, d_x, *grads, *[u[0] for u in upd], *[u[1] for u in upd], *[u[2] for u in upd])
```
